```python
import math
import jax, jax.numpy as jnp
from jax import lax
import numpy as np


D_MODEL = 1024
BATCH = 8
SEQ = 4096
DEPTH = 2

CHUNK = 64
D_MIX = D_MODEL
D_A = 3 * D_MIX // 8
A_BLOCK_DIM = 64
A_BLOCKS = D_A // A_BLOCK_DIM
CONV_WIDTH = 4
RG_C = 8.0
B_HEAD_DIM = 64
D_B = 3 * D_MIX // 8
B_HEADS = D_B // B_HEAD_DIM
Q_BLOCK = 128
D_C = D_MIX - D_A - D_B
C_GROUP = 16
C_GROUPS = D_C // C_GROUP
C_STATE = 64
N_IN = 2 * D_A + 3 * D_B + B_HEADS + D_C
D_FF = 2816
ALPHA = (2 * DEPTH) ** 0.25
BETA = (8 * DEPTH) ** -0.25
LN_EPS = 1e-5
RMS_EPS = 1e-6

kernel_name = "hybrid_rglru_fox_s5_macaron_deepnorm"


def layer_norm(x, g, b):
    xf = x.astype(jnp.float32)
    mu = jnp.mean(xf, axis=-1, keepdims=True)
    var = jnp.mean(jnp.square(xf - mu), axis=-1, keepdims=True)
    return ((xf - mu) * lax.rsqrt(var + LN_EPS) * g + b).astype(x.dtype)


def rms_norm(x, g):
    xf = x.astype(jnp.float32)
    return xf * lax.rsqrt(jnp.mean(xf * xf, axis=-1, keepdims=True) + RMS_EPS) * g


def swiglu(x, w_gate, w_up, w_down):
    return (jax.nn.silu(x @ w_gate) * (x @ w_up)) @ w_down


def causal_depthwise_conv(x, w, b):
    s = x.shape[1]
    xp = jnp.pad(x, ((0, 0), (CONV_WIDTH - 1, 0), (0, 0)))
    out = b
    for k in range(CONV_WIDTH):
        out = out + w[k] * xp[:, k:k + s]
    return out


def linear_scan(a, b):
    def combine(left, right):
        a_l, b_l = left
        a_r, b_r = right
        return a_l * a_r, a_r * b_l + b_r
    return lax.associative_scan(combine, (a, b), axis=1)[1]


def rg_lru(x, w_a, b_a, w_x, b_x, lam):
    bsz, s, _ = x.shape
    xb = x.reshape(bsz, s, A_BLOCKS, A_BLOCK_DIM)
    r = jax.nn.sigmoid(jnp.einsum('bshi,hij->bshj', xb, w_a).reshape(bsz, s, D_A) + b_a)
    i = jax.nn.sigmoid(jnp.einsum('bshi,hij->bshj', xb, w_x).reshape(bsz, s, D_A) + b_x)
    log_a = -RG_C * r * jax.nn.softplus(-lam)
    a = jnp.exp(log_a)
    gated = jnp.sqrt(-jnp.expm1(2.0 * log_a)) * (i * x)
    return linear_scan(a, gated)


def forgetting_attention(q, k, v, log_f):
    s = q.shape[2]
    scale = B_HEAD_DIM ** -0.5
    c = jnp.cumsum(log_f, axis=-1)
    outs = []
    for blk in range(s // Q_BLOCK):
        q0 = blk * Q_BLOCK
        q1 = q0 + Q_BLOCK
        qb = q[:, :, q0:q1]
        kb = k[:, :, :q1]
        vb = v[:, :, :q1]
        logits = jnp.einsum('bhqd,bhkd->bhqk', qb, kb) * scale
        logits = logits + (c[:, :, q0:q1, None] - c[:, :, None, :q1])
        mask = (q0 + jnp.arange(Q_BLOCK))[:, None] >= jnp.arange(q1)[None, :]
        logits = jnp.where(mask, logits, -jnp.inf)
        p = jax.nn.softmax(logits, axis=-1)
        outs.append(jnp.einsum('bhqk,bhkd->bhqd', p, vb))
    return jnp.concatenate(outs, axis=2)


def s5_ssm(u, a_re, a_im, log_dt, b_re, b_im, c_re, c_im, d, w_glu):
    f32 = jnp.float32
    bsz, s, _ = u.shape
    lam = lax.complex(a_re.astype(f32), a_im.astype(f32))
    dt = jnp.exp(log_dt.astype(f32))[:, None]
    a_bar = jnp.exp(lam * dt)
    b_bar = ((a_bar - 1.0) / lam)[:, :, None] * lax.complex(b_re.astype(f32), b_im.astype(f32))
    c_mat = lax.complex(c_re.astype(f32), c_im.astype(f32))
    ug = u.astype(f32).reshape(bsz, s, C_GROUPS, C_GROUP)
    bu = jnp.einsum('bsgc,gpc->bsgp', ug.astype(jnp.complex64), b_bar)
    a_seq = jnp.broadcast_to(a_bar, (1, s, C_GROUPS, C_STATE))
    h = linear_scan(a_seq, bu)
    y = jnp.einsum('bsgp,gcp->bsgc', h, c_mat).real + d.astype(f32).reshape(C_GROUPS, C_GROUP) * ug
    y = jax.nn.gelu(y.reshape(bsz, s, D_C))
    return y * jax.nn.sigmoid(y @ w_glu.astype(f32))


def hybrid_mixer(x, w_in, conv_w, conv_b, rg_w_a, rg_b_a, rg_w_x, rg_b_x, rg_lambda, fox_b_f,
                 s5_a_re, s5_a_im, s5_log_dt, s5_b_re, s5_b_im, s5_c_re, s5_c_im, s5_d, s5_w_glu,
                 mix_norm_g, w_out):
    f32 = jnp.float32
    bsz, s, _ = x.shape
    z = x @ w_in
    cuts = [int(v) for v in np.cumsum([D_A, D_A, D_B, D_B, D_B, B_HEADS])]
    a_x, a_gate, q, k, v, f_logit, c_u = jnp.split(z, cuts, axis=-1)

    xa = causal_depthwise_conv(a_x, conv_w, conv_b).astype(f32)
    out_a = jax.nn.gelu(a_gate.astype(f32)) * rg_lru(xa, rg_w_a, rg_b_a, rg_w_x, rg_b_x, rg_lambda)

    def heads(t):
        return t.astype(f32).reshape(bsz, s, B_HEADS, B_HEAD_DIM).transpose(0, 2, 1, 3)
    log_f = jax.nn.log_sigmoid(f_logit.astype(f32) + fox_b_f).transpose(0, 2, 1)
    out_b = forgetting_attention(heads(q), heads(k), heads(v), log_f)
    out_b = out_b.transpose(0, 2, 1, 3).reshape(bsz, s, D_B)

    out_c = s5_ssm(c_u, s5_a_re, s5_a_im, s5_log_dt, s5_b_re, s5_b_im, s5_c_re, s5_c_im, s5_d, s5_w_glu)

    g_a, g_b, g_c = jnp.split(mix_norm_g, [D_A, D_A + D_B])
    o = jnp.concatenate([rms_norm(out_a, g_a), rms_norm(out_b, g_b), rms_norm(out_c, g_c)], axis=-1)
    return o.astype(x.dtype) @ w_out


def _fwd_setup_inputs(seed: int = 0) -> dict:
    key = jax.random.key(seed)
    ks = jax.random.split(key, 40)
    L = DEPTH
    f32 = jnp.float32

    def nrm(k, shape, scale):
        return scale * jax.random.normal(k, shape, f32)

    u_a = jax.random.uniform(ks[11], (L, D_A), f32, minval=0.9, maxval=0.999)
    a0 = u_a ** (1.0 / RG_C)
    rg_lambda = jnp.log(a0) - jnp.log1p(-a0)
    s5_a_im = jnp.pi * jnp.arange(C_STATE, dtype=f32)[None, None, :] + nrm(ks[14], (L, C_GROUPS, C_STATE), 0.01)
    s5_log_dt = jax.random.uniform(ks[15], (L, C_GROUPS), f32, minval=math.log(1e-3), maxval=math.log(1e-1))
    return {
        'x': jax.random.normal(ks[0], (BATCH, SEQ, D_MODEL), f32),
        'ffn1_w_gate': nrm(ks[1], (L, D_MODEL, D_FF), D_MODEL ** -0.5),
        'ffn1_w_up': nrm(ks[2], (L, D_MODEL, D_FF), D_MODEL ** -0.5),
        'ffn1_w_down': nrm(ks[3], (L, D_FF, D_MODEL), BETA * D_FF ** -0.5),
        'ln1_g': 1.0 + nrm(ks[4], (L, D_MODEL), 0.02),
        'ln1_b': nrm(ks[5], (L, D_MODEL), 0.02),
        'w_in': nrm(ks[6], (L, D_MODEL, N_IN), D_MODEL ** -0.5),
        'conv_w': nrm(ks[7], (L, CONV_WIDTH, D_A), CONV_WIDTH ** -0.5),
        'conv_b': nrm(ks[8], (L, D_A), 0.02),
        'rg_w_a': nrm(ks[9], (L, A_BLOCKS, A_BLOCK_DIM, A_BLOCK_DIM), A_BLOCK_DIM ** -0.5),
        'rg_b_a': nrm(ks[10], (L, D_A), 0.02),
        'rg_w_x': nrm(ks[12], (L, A_BLOCKS, A_BLOCK_DIM, A_BLOCK_DIM), A_BLOCK_DIM ** -0.5),
        'rg_b_x': nrm(ks[16], (L, D_A), 0.02),
        'rg_lambda': rg_lambda,
        'fox_b_f': 2.0 + nrm(ks[17], (L, B_HEADS), 0.1),
        's5_a_re': -0.5 + nrm(ks[13], (L, C_GROUPS, C_STATE), 0.01),
        's5_a_im': s5_a_im,
        's5_log_dt': s5_log_dt,
        's5_b_re': nrm(ks[18], (L, C_GROUPS, C_STATE, C_GROUP), (2 * C_GROUP) ** -0.5),
        's5_b_im': nrm(ks[19], (L, C_GROUPS, C_STATE, C_GROUP), (2 * C_GROUP) ** -0.5),
        's5_c_re': nrm(ks[20], (L, C_GROUPS, C_GROUP, C_STATE), (2 * C_STATE) ** -0.5),
        's5_c_im': nrm(ks[21], (L, C_GROUPS, C_GROUP, C_STATE), (2 * C_STATE) ** -0.5),
        's5_d': nrm(ks[22], (L, D_C), 1.0),
        's5_w_glu': nrm(ks[23], (L, D_C, D_C), D_C ** -0.5),
        'mix_norm_g': 1.0 + nrm(ks[24], (L, D_MIX), 0.02),
        'w_out': nrm(ks[25], (L, D_MIX, D_MODEL), BETA * D_MIX ** -0.5),
        'ln2_g': 1.0 + nrm(ks[26], (L, D_MODEL), 0.02),
        'ln2_b': nrm(ks[27], (L, D_MODEL), 0.02),
        'ffn2_w_gate': nrm(ks[28], (L, D_MODEL, D_FF), D_MODEL ** -0.5),
        'ffn2_w_up': nrm(ks[29], (L, D_MODEL, D_FF), D_MODEL ** -0.5),
        'ffn2_w_down': nrm(ks[30], (L, D_FF, D_MODEL), BETA * D_FF ** -0.5),
        'ln3_g': 1.0 + nrm(ks[31], (L, D_MODEL), 0.02),
        'ln3_b': nrm(ks[32], (L, D_MODEL), 0.02),
    }


def _fwd_reference(x, ffn1_w_gate, ffn1_w_up, ffn1_w_down, ln1_g, ln1_b,
              w_in, conv_w, conv_b, rg_w_a, rg_b_a, rg_w_x, rg_b_x, rg_lambda, fox_b_f,
              s5_a_re, s5_a_im, s5_log_dt, s5_b_re, s5_b_im, s5_c_re, s5_c_im, s5_d, s5_w_glu,
              mix_norm_g, w_out, ln2_g, ln2_b,
              ffn2_w_gate, ffn2_w_up, ffn2_w_down, ln3_g, ln3_b):
    for l in range(DEPTH):
        x = layer_norm(ALPHA * x + 0.5 * swiglu(x, ffn1_w_gate[l], ffn1_w_up[l], ffn1_w_down[l]),
                       ln1_g[l], ln1_b[l])
        mix = hybrid_mixer(x, w_in[l], conv_w[l], conv_b[l], rg_w_a[l], rg_b_a[l], rg_w_x[l], rg_b_x[l],
                           rg_lambda[l], fox_b_f[l],
                           s5_a_re[l], s5_a_im[l], s5_log_dt[l], s5_b_re[l], s5_b_im[l],
                           s5_c_re[l], s5_c_im[l], s5_d[l], s5_w_glu[l],
                           mix_norm_g[l], w_out[l])
        x = layer_norm(ALPHA * x + mix, ln2_g[l], ln2_b[l])
        x = layer_norm(ALPHA * x + 0.5 * swiglu(x, ffn2_w_gate[l], ffn2_w_up[l], ffn2_w_down[l]),
                       ln3_g[l], ln3_b[l])
    return x


import jax as _jax
import jax.numpy as _jnp

TWIN_FORMAT = 'train_step'
FWD_PARAMS = ['x', 'ffn1_w_gate', 'ffn1_w_up', 'ffn1_w_down', 'ln1_g', 'ln1_b', 'w_in', 'conv_w', 'conv_b', 'rg_w_a', 'rg_b_a', 'rg_w_x', 'rg_b_x', 'rg_lambda', 'fox_b_f', 's5_a_re', 's5_a_im', 's5_log_dt', 's5_b_re', 's5_b_im', 's5_c_re', 's5_c_im', 's5_d', 's5_w_glu', 'mix_norm_g', 'w_out', 'ln2_g', 'ln2_b', 'ffn2_w_gate', 'ffn2_w_up', 'ffn2_w_down', 'ln3_g', 'ln3_b']
TWIN_WEIGHTS = ['ffn1_w_gate', 'ffn1_w_up', 'ffn1_w_down', 'ln1_g', 'ln1_b', 'w_in', 'conv_w', 'conv_b', 'rg_w_a', 'rg_b_a', 'rg_w_x', 'rg_b_x', 'rg_lambda', 'fox_b_f', 's5_a_re', 's5_a_im', 's5_log_dt', 's5_b_re', 's5_b_im', 's5_c_re', 's5_c_im', 's5_d', 's5_w_glu', 'mix_norm_g', 'w_out', 'ln2_g', 'ln2_b', 'ffn2_w_gate', 'ffn2_w_up', 'ffn2_w_down', 'ln3_g', 'ln3_b']
TWIN_DIFF_INPUT = 'x'
TWIN_INPUTS = ['x', 'ffn1_w_gate', 'ffn1_w_up', 'ffn1_w_down', 'ln1_g', 'ln1_b', 'w_in', 'conv_w', 'conv_b', 'rg_w_a', 'rg_b_a', 'rg_w_x', 'rg_b_x', 'rg_lambda', 'fox_b_f', 's5_a_re', 's5_a_im', 's5_log_dt', 's5_b_re', 's5_b_im', 's5_c_re', 's5_c_im', 's5_d', 's5_w_glu', 'mix_norm_g', 'w_out', 'ln2_g', 'ln2_b', 'ffn2_w_gate', 'ffn2_w_up', 'ffn2_w_down', 'ln3_g', 'ln3_b', 'loss_target', 'm_ffn1_w_gate', 'm_ffn1_w_up', 'm_ffn1_w_down', 'm_ln1_g', 'm_ln1_b', 'm_w_in', 'm_conv_w', 'm_conv_b', 'm_rg_w_a', 'm_rg_b_a', 'm_rg_w_x', 'm_rg_b_x', 'm_rg_lambda', 'm_fox_b_f', 'm_s5_a_re', 'm_s5_a_im', 'm_s5_log_dt', 'm_s5_b_re', 'm_s5_b_im', 'm_s5_c_re', 'm_s5_c_im', 'm_s5_d', 'm_s5_w_glu', 'm_mix_norm_g', 'm_w_out', 'm_ln2_g', 'm_ln2_b', 'm_ffn2_w_gate', 'm_ffn2_w_up', 'm_ffn2_w_down', 'm_ln3_g', 'm_ln3_b', 'v_ffn1_w_gate', 'v_ffn1_w_up', 'v_ffn1_w_down', 'v_ln1_g', 'v_ln1_b', 'v_w_in', 'v_conv_w', 'v_conv_b', 'v_rg_w_a', 'v_rg_b_a', 'v_rg_w_x', 'v_rg_b_x', 'v_rg_lambda', 'v_fox_b_f', 'v_s5_a_re', 'v_s5_a_im', 'v_s5_log_dt', 'v_s5_b_re', 'v_s5_b_im', 'v_s5_c_re', 'v_s5_c_im', 'v_s5_d', 'v_s5_w_glu', 'v_mix_norm_g', 'v_w_out', 'v_ln2_g', 'v_ln2_b', 'v_ffn2_w_gate', 'v_ffn2_w_up', 'v_ffn2_w_down', 'v_ln3_g', 'v_ln3_b']
TWIN_OUTPUTS = ['loss', 'grad_x', 'grad_ffn1_w_gate', 'grad_ffn1_w_up', 'grad_ffn1_w_down', 'grad_ln1_g', 'grad_ln1_b', 'grad_w_in', 'grad_conv_w', 'grad_conv_b', 'grad_rg_w_a', 'grad_rg_b_a', 'grad_rg_w_x', 'grad_rg_b_x', 'grad_rg_lambda', 'grad_fox_b_f', 'grad_s5_a_re', 'grad_s5_a_im', 'grad_s5_log_dt', 'grad_s5_b_re', 'grad_s5_b_im', 'grad_s5_c_re', 'grad_s5_c_im', 'grad_s5_d', 'grad_s5_w_glu', 'grad_mix_norm_g', 'grad_w_out', 'grad_ln2_g', 'grad_ln2_b', 'grad_ffn2_w_gate', 'grad_ffn2_w_up', 'grad_ffn2_w_down', 'grad_ln3_g', 'grad_ln3_b', 'delta_ffn1_w_gate', 'delta_ffn1_w_up', 'delta_ffn1_w_down', 'delta_ln1_g', 'delta_ln1_b', 'delta_w_in', 'delta_conv_w', 'delta_conv_b', 'delta_rg_w_a', 'delta_rg_b_a', 'delta_rg_w_x', 'delta_rg_b_x', 'delta_rg_lambda', 'delta_fox_b_f', 'delta_s5_a_re', 'delta_s5_a_im', 'delta_s5_log_dt', 'delta_s5_b_re', 'delta_s5_b_im', 'delta_s5_c_re', 'delta_s5_c_im', 'delta_s5_d', 'delta_s5_w_glu', 'delta_mix_norm_g', 'delta_w_out', 'delta_ln2_g', 'delta_ln2_b', 'delta_ffn2_w_gate', 'delta_ffn2_w_up', 'delta_ffn2_w_down', 'delta_ln3_g', 'delta_ln3_b', 'new_m_ffn1_w_gate', 'new_m_ffn1_w_up', 'new_m_ffn1_w_down', 'new_m_ln1_g', 'new_m_ln1_b', 'new_m_w_in', 'new_m_conv_w', 'new_m_conv_b', 'new_m_rg_w_a', 'new_m_rg_b_a', 'new_m_rg_w_x', 'new_m_rg_b_x', 'new_m_rg_lambda', 'new_m_fox_b_f', 'new_m_s5_a_re', 'new_m_s5_a_im', 'new_m_s5_log_dt', 'new_m_s5_b_re', 'new_m_s5_b_im', 'new_m_s5_c_re', 'new_m_s5_c_im', 'new_m_s5_d', 'new_m_s5_w_glu', 'new_m_mix_norm_g', 'new_m_w_out', 'new_m_ln2_g', 'new_m_ln2_b', 'new_m_ffn2_w_gate', 'new_m_ffn2_w_up', 'new_m_ffn2_w_down', 'new_m_ln3_g', 'new_m_ln3_b', 'new_v_ffn1_w_gate', 'new_v_ffn1_w_up', 'new_v_ffn1_w_down', 'new_v_ln1_g', 'new_v_ln1_b', 'new_v_w_in', 'new_v_conv_w', 'new_v_conv_b', 'new_v_rg_w_a', 'new_v_rg_b_a', 'new_v_rg_w_x', 'new_v_rg_b_x', 'new_v_rg_lambda', 'new_v_fox_b_f', 'new_v_s5_a_re', 'new_v_s5_a_im', 'new_v_s5_log_dt', 'new_v_s5_b_re', 'new_v_s5_b_im', 'new_v_s5_c_re', 'new_v_s5_c_im', 'new_v_s5_d', 'new_v_s5_w_glu', 'new_v_mix_norm_g', 'new_v_w_out', 'new_v_ln2_g', 'new_v_ln2_b', 'new_v_ffn2_w_gate', 'new_v_ffn2_w_up', 'new_v_ffn2_w_down', 'new_v_ln3_g', 'new_v_ln3_b']
TWIN_LEAF_KINDS = {'loss': 'loss', 'grad_x': 'grad_x', 'grad_ffn1_w_gate': 'grad_w', 'grad_ffn1_w_up': 'grad_w', 'grad_ffn1_w_down': 'grad_w', 'grad_ln1_g': 'grad_w', 'grad_ln1_b': 'grad_w', 'grad_w_in': 'grad_w', 'grad_conv_w': 'grad_w', 'grad_conv_b': 'grad_w', 'grad_rg_w_a': 'grad_w', 'grad_rg_b_a': 'grad_w', 'grad_rg_w_x': 'grad_w', 'grad_rg_b_x': 'grad_w', 'grad_rg_lambda': 'grad_w', 'grad_fox_b_f': 'grad_w', 'grad_s5_a_re': 'grad_w', 'grad_s5_a_im': 'grad_w', 'grad_s5_log_dt': 'grad_w', 'grad_s5_b_re': 'grad_w', 'grad_s5_b_im': 'grad_w', 'grad_s5_c_re': 'grad_w', 'grad_s5_c_im': 'grad_w', 'grad_s5_d': 'grad_w', 'grad_s5_w_glu': 'grad_w', 'grad_mix_norm_g': 'grad_w', 'grad_w_out': 'grad_w', 'grad_ln2_g': 'grad_w', 'grad_ln2_b': 'grad_w', 'grad_ffn2_w_gate': 'grad_w', 'grad_ffn2_w_up': 'grad_w', 'grad_ffn2_w_down': 'grad_w', 'grad_ln3_g': 'grad_w', 'grad_ln3_b': 'grad_w', 'delta_ffn1_w_gate': 'delta_w', 'delta_ffn1_w_up': 'delta_w', 'delta_ffn1_w_down': 'delta_w', 'delta_ln1_g': 'delta_w', 'delta_ln1_b': 'delta_w', 'delta_w_in': 'delta_w', 'delta_conv_w': 'delta_w', 'delta_conv_b': 'delta_w', 'delta_rg_w_a': 'delta_w', 'delta_rg_b_a': 'delta_w', 'delta_rg_w_x': 'delta_w', 'delta_rg_b_x': 'delta_w', 'delta_rg_lambda': 'delta_w', 'delta_fox_b_f': 'delta_w', 'delta_s5_a_re': 'delta_w', 'delta_s5_a_im': 'delta_w', 'delta_s5_log_dt': 'delta_w', 'delta_s5_b_re': 'delta_w', 'delta_s5_b_im': 'delta_w', 'delta_s5_c_re': 'delta_w', 'delta_s5_c_im': 'delta_w', 'delta_s5_d': 'delta_w', 'delta_s5_w_glu': 'delta_w', 'delta_mix_norm_g': 'delta_w', 'delta_w_out': 'delta_w', 'delta_ln2_g': 'delta_w', 'delta_ln2_b': 'delta_w', 'delta_ffn2_w_gate': 'delta_w', 'delta_ffn2_w_up': 'delta_w', 'delta_ffn2_w_down': 'delta_w', 'delta_ln3_g': 'delta_w', 'delta_ln3_b': 'delta_w', 'new_m_ffn1_w_gate': 'new_m', 'new_m_ffn1_w_up': 'new_m', 'new_m_ffn1_w_down': 'new_m', 'new_m_ln1_g': 'new_m', 'new_m_ln1_b': 'new_m', 'new_m_w_in': 'new_m', 'new_m_conv_w': 'new_m', 'new_m_conv_b': 'new_m', 'new_m_rg_w_a': 'new_m', 'new_m_rg_b_a': 'new_m', 'new_m_rg_w_x': 'new_m', 'new_m_rg_b_x': 'new_m', 'new_m_rg_lambda': 'new_m', 'new_m_fox_b_f': 'new_m', 'new_m_s5_a_re': 'new_m', 'new_m_s5_a_im': 'new_m', 'new_m_s5_log_dt': 'new_m', 'new_m_s5_b_re': 'new_m', 'new_m_s5_b_im': 'new_m', 'new_m_s5_c_re': 'new_m', 'new_m_s5_c_im': 'new_m', 'new_m_s5_d': 'new_m', 'new_m_s5_w_glu': 'new_m', 'new_m_mix_norm_g': 'new_m', 'new_m_w_out': 'new_m', 'new_m_ln2_g': 'new_m', 'new_m_ln2_b': 'new_m', 'new_m_ffn2_w_gate': 'new_m', 'new_m_ffn2_w_up': 'new_m', 'new_m_ffn2_w_down': 'new_m', 'new_m_ln3_g': 'new_m', 'new_m_ln3_b': 'new_m', 'new_v_ffn1_w_gate': 'new_v', 'new_v_ffn1_w_up': 'new_v', 'new_v_ffn1_w_down': 'new_v', 'new_v_ln1_g': 'new_v', 'new_v_ln1_b': 'new_v', 'new_v_w_in': 'new_v', 'new_v_conv_w': 'new_v', 'new_v_conv_b': 'new_v', 'new_v_rg_w_a': 'new_v', 'new_v_rg_b_a': 'new_v', 'new_v_rg_w_x': 'new_v', 'new_v_rg_b_x': 'new_v', 'new_v_rg_lambda': 'new_v', 'new_v_fox_b_f': 'new_v', 'new_v_s5_a_re': 'new_v', 'new_v_s5_a_im': 'new_v', 'new_v_s5_log_dt': 'new_v', 'new_v_s5_b_re': 'new_v', 'new_v_s5_b_im': 'new_v', 'new_v_s5_c_re': 'new_v', 'new_v_s5_c_im': 'new_v', 'new_v_s5_d': 'new_v', 'new_v_s5_w_glu': 'new_v', 'new_v_mix_norm_g': 'new_v', 'new_v_w_out': 'new_v', 'new_v_ln2_g': 'new_v', 'new_v_ln2_b': 'new_v', 'new_v_ffn2_w_gate': 'new_v', 'new_v_ffn2_w_up': 'new_v', 'new_v_ffn2_w_down': 'new_v', 'new_v_ln3_g': 'new_v', 'new_v_ln3_b': 'new_v'}


def _forward(args):
    return _fwd_reference(*[args[k] for k in FWD_PARAMS])


def _output_shape():
    out = _jax.eval_shape(lambda: _forward(_fwd_setup_inputs(0)))
    return out.shape, out.dtype

N_MICROBATCH = 1
ADAM_LR = 0.001
ADAM_B1 = 0.9
ADAM_B2 = 0.999
ADAM_EPS = 1e-08
ADAM_WD = 0.01
ADAM_STEP = 10
PER_EXAMPLE_BATCH_AXIS = {'x': 0, 'loss_target': 0}
SHARED_INPUTS = []
_WEIGHT_DTYPES = {'ffn1_w_gate': _jnp.float32, 'ffn1_w_up': _jnp.float32, 'ffn1_w_down': _jnp.float32, 'ln1_g': _jnp.float32, 'ln1_b': _jnp.float32, 'w_in': _jnp.float32, 'conv_w': _jnp.float32, 'conv_b': _jnp.float32, 'rg_w_a': _jnp.float32, 'rg_b_a': _jnp.float32, 'rg_w_x': _jnp.float32, 'rg_b_x': _jnp.float32, 'rg_lambda': _jnp.float32, 'fox_b_f': _jnp.float32, 's5_a_re': _jnp.float32, 's5_a_im': _jnp.float32, 's5_log_dt': _jnp.float32, 's5_b_re': _jnp.float32, 's5_b_im': _jnp.float32, 's5_c_re': _jnp.float32, 's5_c_im': _jnp.float32, 's5_d': _jnp.float32, 's5_w_glu': _jnp.float32, 'mix_norm_g': _jnp.float32, 'w_out': _jnp.float32, 'ln2_g': _jnp.float32, 'ln2_b': _jnp.float32, 'ffn2_w_gate': _jnp.float32, 'ffn2_w_up': _jnp.float32, 'ffn2_w_down': _jnp.float32, 'ln3_g': _jnp.float32, 'ln3_b': _jnp.float32}
MOMENT_SCALE = {'ffn1_w_gate': 1.244043e-02, 'ffn1_w_up': 1.204339e-02, 'ffn1_w_down': 3.992389e-02, 'ln1_g': 8.110916e-01, 'ln1_b': 6.861037e-01, 'w_in': 5.819628e-02, 'conv_w': 6.981387e-02, 'conv_b': 7.421974e-01, 'rg_w_a': 2.680205e-02, 'rg_b_a': 1.733409e-02, 'rg_w_x': 4.642320e-02, 'rg_b_x': 2.244424e-02, 'rg_lambda': 3.695581e-02, 'fox_b_f': 3.211571e-01, 's5_a_re': 4.314408e-03, 's5_a_im': 4.342900e-03, 's5_log_dt': 3.963368e+00, 's5_b_re': 2.463139e-03, 's5_b_im': 2.654700e-03, 's5_c_re': 4.875553e-03, 's5_c_im': 5.290316e-03, 's5_d': 1.006641e-01, 's5_w_glu': 1.926627e-02, 'mix_norm_g': 8.385998e-02, 'w_out': 1.509574e-01, 'ln2_g': 9.458933e-01, 'ln2_b': 5.320581e-01, 'ffn2_w_gate': 1.181916e-02, 'ffn2_w_up': 1.147013e-02, 'ffn2_w_down': 3.807694e-02, 'ln3_g': 2.267206e+01, 'ln3_b': 3.206993e+00}


def _to_microbatches(a, axis):
    t = _jnp.moveaxis(a, axis, 0)
    t = t.reshape((N_MICROBATCH, t.shape[0] // N_MICROBATCH) + t.shape[1:])
    return _jnp.moveaxis(t, 1, axis + 1)


def setup_inputs(seed: int = 0) -> dict:
    inp = _fwd_setup_inputs(seed)
    key = _jax.random.fold_in(_jax.random.key(seed), 7919)
    shape, _ = _output_shape()
    out = dict(inp)
    out["loss_target"] = _jax.random.normal(_jax.random.fold_in(key, 0), shape, _jnp.float32)
    for i, name in enumerate(TWIN_WEIGHTS):
        w = inp[name].astype(_jnp.float32)
        if MOMENT_SCALE is None:
            s = _jnp.sqrt(_jnp.mean(_jnp.square(w)) + 1e-30)
        else:
            s = MOMENT_SCALE[name]
        km, kv = _jax.random.split(_jax.random.fold_in(key, i + 1))
        out[name] = w
        out["m_" + name] = s * _jax.random.normal(km, w.shape, _jnp.float32)
        out["v_" + name] = (s * s) * _jax.random.uniform(kv, w.shape, _jnp.float32, 0.5, 1.5)
    if N_MICROBATCH > 1:
        for name, axis in PER_EXAMPLE_BATCH_AXIS.items():
            out[name] = _to_microbatches(out[name], axis)
    return {'x': out['x'], 'ffn1_w_gate': out['ffn1_w_gate'], 'ffn1_w_up': out['ffn1_w_up'], 'ffn1_w_down': out['ffn1_w_down'], 'ln1_g': out['ln1_g'], 'ln1_b': out['ln1_b'], 'w_in': out['w_in'], 'conv_w': out['conv_w'], 'conv_b': out['conv_b'], 'rg_w_a': out['rg_w_a'], 'rg_b_a': out['rg_b_a'], 'rg_w_x': out['rg_w_x'], 'rg_b_x': out['rg_b_x'], 'rg_lambda': out['rg_lambda'], 'fox_b_f': out['fox_b_f'], 's5_a_re': out['s5_a_re'], 's5_a_im': out['s5_a_im'], 's5_log_dt': out['s5_log_dt'], 's5_b_re': out['s5_b_re'], 's5_b_im': out['s5_b_im'], 's5_c_re': out['s5_c_re'], 's5_c_im': out['s5_c_im'], 's5_d': out['s5_d'], 's5_w_glu': out['s5_w_glu'], 'mix_norm_g': out['mix_norm_g'], 'w_out': out['w_out'], 'ln2_g': out['ln2_g'], 'ln2_b': out['ln2_b'], 'ffn2_w_gate': out['ffn2_w_gate'], 'ffn2_w_up': out['ffn2_w_up'], 'ffn2_w_down': out['ffn2_w_down'], 'ln3_g': out['ln3_g'], 'ln3_b': out['ln3_b'], 'loss_target': out['loss_target'], 'm_ffn1_w_gate': out['m_ffn1_w_gate'], 'm_ffn1_w_up': out['m_ffn1_w_up'], 'm_ffn1_w_down': out['m_ffn1_w_down'], 'm_ln1_g': out['m_ln1_g'], 'm_ln1_b': out['m_ln1_b'], 'm_w_in': out['m_w_in'], 'm_conv_w': out['m_conv_w'], 'm_conv_b': out['m_conv_b'], 'm_rg_w_a': out['m_rg_w_a'], 'm_rg_b_a': out['m_rg_b_a'], 'm_rg_w_x': out['m_rg_w_x'], 'm_rg_b_x': out['m_rg_b_x'], 'm_rg_lambda': out['m_rg_lambda'], 'm_fox_b_f': out['m_fox_b_f'], 'm_s5_a_re': out['m_s5_a_re'], 'm_s5_a_im': out['m_s5_a_im'], 'm_s5_log_dt': out['m_s5_log_dt'], 'm_s5_b_re': out['m_s5_b_re'], 'm_s5_b_im': out['m_s5_b_im'], 'm_s5_c_re': out['m_s5_c_re'], 'm_s5_c_im': out['m_s5_c_im'], 'm_s5_d': out['m_s5_d'], 'm_s5_w_glu': out['m_s5_w_glu'], 'm_mix_norm_g': out['m_mix_norm_g'], 'm_w_out': out['m_w_out'], 'm_ln2_g': out['m_ln2_g'], 'm_ln2_b': out['m_ln2_b'], 'm_ffn2_w_gate': out['m_ffn2_w_gate'], 'm_ffn2_w_up': out['m_ffn2_w_up'], 'm_ffn2_w_down': out['m_ffn2_w_down'], 'm_ln3_g': out['m_ln3_g'], 'm_ln3_b': out['m_ln3_b'], 'v_ffn1_w_gate': out['v_ffn1_w_gate'], 'v_ffn1_w_up': out['v_ffn1_w_up'], 'v_ffn1_w_down': out['v_ffn1_w_down'], 'v_ln1_g': out['v_ln1_g'], 'v_ln1_b': out['v_ln1_b'], 'v_w_in': out['v_w_in'], 'v_conv_w': out['v_conv_w'], 'v_conv_b': out['v_conv_b'], 'v_rg_w_a': out['v_rg_w_a'], 'v_rg_b_a': out['v_rg_b_a'], 'v_rg_w_x': out['v_rg_w_x'], 'v_rg_b_x': out['v_rg_b_x'], 'v_rg_lambda': out['v_rg_lambda'], 'v_fox_b_f': out['v_fox_b_f'], 'v_s5_a_re': out['v_s5_a_re'], 'v_s5_a_im': out['v_s5_a_im'], 'v_s5_log_dt': out['v_s5_log_dt'], 'v_s5_b_re': out['v_s5_b_re'], 'v_s5_b_im': out['v_s5_b_im'], 'v_s5_c_re': out['v_s5_c_re'], 'v_s5_c_im': out['v_s5_c_im'], 'v_s5_d': out['v_s5_d'], 'v_s5_w_glu': out['v_s5_w_glu'], 'v_mix_norm_g': out['v_mix_norm_g'], 'v_w_out': out['v_w_out'], 'v_ln2_g': out['v_ln2_g'], 'v_ln2_b': out['v_ln2_b'], 'v_ffn2_w_gate': out['v_ffn2_w_gate'], 'v_ffn2_w_up': out['v_ffn2_w_up'], 'v_ffn2_w_down': out['v_ffn2_w_down'], 'v_ln3_g': out['v_ln3_g'], 'v_ln3_b': out['v_ln3_b']}


def _loss(weights, diff, rest, loss_target):
    with _jax.named_scope("forward"):
        args = {**rest, TWIN_DIFF_INPUT: diff, **{k: w.astype(_WEIGHT_DTYPES[k]) for k, w in weights.items()}}
        y = _forward(args)
    with _jax.named_scope("loss_head"):
        err = _jnp.square(y.astype(_jnp.float32) - loss_target)
        return 0.5 * _jnp.sum(_jnp.mean(err, axis=-1)) if err.ndim else 0.5 * err


def _adamw(w, g, m, v):
    m = ADAM_B1 * m + (1.0 - ADAM_B1) * g
    v = ADAM_B2 * v + (1.0 - ADAM_B2) * _jnp.square(g)
    m_hat = m / (1.0 - ADAM_B1 ** ADAM_STEP)
    v_hat = v / (1.0 - ADAM_B2 ** ADAM_STEP)
    delta = -ADAM_LR * (m_hat / (_jnp.sqrt(v_hat) + ADAM_EPS) + ADAM_WD * w)
    return delta, m, v


def reference(x, ffn1_w_gate, ffn1_w_up, ffn1_w_down, ln1_g, ln1_b, w_in, conv_w, conv_b, rg_w_a, rg_b_a, rg_w_x, rg_b_x, rg_lambda, fox_b_f, s5_a_re, s5_a_im, s5_log_dt, s5_b_re, s5_b_im, s5_c_re, s5_c_im, s5_d, s5_w_glu, mix_norm_g, w_out, ln2_g, ln2_b, ffn2_w_gate, ffn2_w_up, ffn2_w_down, ln3_g, ln3_b, loss_target, m_ffn1_w_gate, m_ffn1_w_up, m_ffn1_w_down, m_ln1_g, m_ln1_b, m_w_in, m_conv_w, m_conv_b, m_rg_w_a, m_rg_b_a, m_rg_w_x, m_rg_b_x, m_rg_lambda, m_fox_b_f, m_s5_a_re, m_s5_a_im, m_s5_log_dt, m_s5_b_re, m_s5_b_im, m_s5_c_re, m_s5_c_im, m_s5_d, m_s5_w_glu, m_mix_norm_g, m_w_out, m_ln2_g, m_ln2_b, m_ffn2_w_gate, m_ffn2_w_up, m_ffn2_w_down, m_ln3_g, m_ln3_b, v_ffn1_w_gate, v_ffn1_w_up, v_ffn1_w_down, v_ln1_g, v_ln1_b, v_w_in, v_conv_w, v_conv_b, v_rg_w_a, v_rg_b_a, v_rg_w_x, v_rg_b_x, v_rg_lambda, v_fox_b_f, v_s5_a_re, v_s5_a_im, v_s5_log_dt, v_s5_b_re, v_s5_b_im, v_s5_c_re, v_s5_c_im, v_s5_d, v_s5_w_glu, v_mix_norm_g, v_w_out, v_ln2_g, v_ln2_b, v_ffn2_w_gate, v_ffn2_w_up, v_ffn2_w_down, v_ln3_g, v_ln3_b):
    given = dict(x=x, ffn1_w_gate=ffn1_w_gate, ffn1_w_up=ffn1_w_up, ffn1_w_down=ffn1_w_down, ln1_g=ln1_g, ln1_b=ln1_b, w_in=w_in, conv_w=conv_w, conv_b=conv_b, rg_w_a=rg_w_a, rg_b_a=rg_b_a, rg_w_x=rg_w_x, rg_b_x=rg_b_x, rg_lambda=rg_lambda, fox_b_f=fox_b_f, s5_a_re=s5_a_re, s5_a_im=s5_a_im, s5_log_dt=s5_log_dt, s5_b_re=s5_b_re, s5_b_im=s5_b_im, s5_c_re=s5_c_re, s5_c_im=s5_c_im, s5_d=s5_d, s5_w_glu=s5_w_glu, mix_norm_g=mix_norm_g, w_out=w_out, ln2_g=ln2_g, ln2_b=ln2_b, ffn2_w_gate=ffn2_w_gate, ffn2_w_up=ffn2_w_up, ffn2_w_down=ffn2_w_down, ln3_g=ln3_g, ln3_b=ln3_b, loss_target=loss_target, m_ffn1_w_gate=m_ffn1_w_gate, m_ffn1_w_up=m_ffn1_w_up, m_ffn1_w_down=m_ffn1_w_down, m_ln1_g=m_ln1_g, m_ln1_b=m_ln1_b, m_w_in=m_w_in, m_conv_w=m_conv_w, m_conv_b=m_conv_b, m_rg_w_a=m_rg_w_a, m_rg_b_a=m_rg_b_a, m_rg_w_x=m_rg_w_x, m_rg_b_x=m_rg_b_x, m_rg_lambda=m_rg_lambda, m_fox_b_f=m_fox_b_f, m_s5_a_re=m_s5_a_re, m_s5_a_im=m_s5_a_im, m_s5_log_dt=m_s5_log_dt, m_s5_b_re=m_s5_b_re, m_s5_b_im=m_s5_b_im, m_s5_c_re=m_s5_c_re, m_s5_c_im=m_s5_c_im, m_s5_d=m_s5_d, m_s5_w_glu=m_s5_w_glu, m_mix_norm_g=m_mix_norm_g, m_w_out=m_w_out, m_ln2_g=m_ln2_g, m_ln2_b=m_ln2_b, m_ffn2_w_gate=m_ffn2_w_gate, m_ffn2_w_up=m_ffn2_w_up, m_ffn2_w_down=m_ffn2_w_down, m_ln3_g=m_ln3_g, m_ln3_b=m_ln3_b, v_ffn1_w_gate=v_ffn1_w_gate, v_ffn1_w_up=v_ffn1_w_up, v_ffn1_w_down=v_ffn1_w_down, v_ln1_g=v_ln1_g, v_ln1_b=v_ln1_b, v_w_in=v_w_in, v_conv_w=v_conv_w, v_conv_b=v_conv_b, v_rg_w_a=v_rg_w_a, v_rg_b_a=v_rg_b_a, v_rg_w_x=v_rg_w_x, v_rg_b_x=v_rg_b_x, v_rg_lambda=v_rg_lambda, v_fox_b_f=v_fox_b_f, v_s5_a_re=v_s5_a_re, v_s5_a_im=v_s5_a_im, v_s5_log_dt=v_s5_log_dt, v_s5_b_re=v_s5_b_re, v_s5_b_im=v_s5_b_im, v_s5_c_re=v_s5_c_re, v_s5_c_im=v_s5_c_im, v_s5_d=v_s5_d, v_s5_w_glu=v_s5_w_glu, v_mix_norm_g=v_mix_norm_g, v_w_out=v_w_out, v_ln2_g=v_ln2_g, v_ln2_b=v_ln2_b, v_ffn2_w_gate=v_ffn2_w_gate, v_ffn2_w_up=v_ffn2_w_up, v_ffn2_w_down=v_ffn2_w_down, v_ln3_g=v_ln3_g, v_ln3_b=v_ln3_b)
    weights = {n: given[n] for n in TWIN_WEIGHTS}
    shared = {n: given[n] for n in SHARED_INPUTS}
    per_example = {n: given[n] for n in ['x']}
    grad_fn = _jax.value_and_grad(_loss, argnums=(0, 1))

    def one_microbatch(ex, loss_target):
        ex = dict(ex)
        diff = ex.pop(TWIN_DIFF_INPUT)
        return grad_fn(weights, diff, {**shared, **ex}, loss_target)

    if N_MICROBATCH == 1:
        loss, (grad_w, grad_x) = one_microbatch(per_example, given["loss_target"])
    else:
        def body(carry, xs):
            loss_sum, grad_sum = carry
            l_k, (gw_k, gx_k) = one_microbatch(xs[0], xs[1])
            with _jax.named_scope("update"):
                return (loss_sum + l_k, _jax.tree.map(_jnp.add, grad_sum, gw_k)), gx_k

        init = (_jnp.zeros((), _jnp.float32), _jax.tree.map(_jnp.zeros_like, weights))
        (loss, grad_w), grad_x = _jax.lax.scan(body, init, (per_example, given["loss_target"]))
    with _jax.named_scope("update"):
        delta_w, new_m, new_v = {}, {}, {}
        for n in TWIN_WEIGHTS:
            delta_w[n], new_m[n], new_v[n] = _adamw(weights[n], grad_w[n], given["m_" + n], given["v_" + n])
    return (loss, grad_x, *[grad_w[n] for n in TWIN_WEIGHTS], *[delta_w[n] for n in TWIN_WEIGHTS],
            *[new_m[n] for n in TWIN_WEIGHTS], *[new_v[n] for n in TWIN_WEIGHTS])
```

```python
import functools
import math

import jax
import jax.numpy as jnp
from jax import lax
from jax.experimental import pallas as pl
from jax.experimental.pallas import tpu as pltpu

f32 = jnp.float32
bf16 = jnp.bfloat16

D_MODEL = 1024
D_FF = 2816
D_A = 384
D_B = 384
D_C = 256
N_HEADS = 6
HEAD_DIM = 64
S5_GROUPS = 16
S5_GROUP = 16
S5_STATE = 64
S5_LANES = S5_GROUPS * S5_STATE
N_IN = 2 * D_A + 3 * D_B + N_HEADS + D_C
F_OFF = 5 * D_A
CU_OFF = F_OFF + 128
N_IN_P = CU_OFF + D_C
CONV_WIDTH = 4
DEPTH = 2
ALPHA = (2 * DEPTH) ** 0.25
LN_EPS = 1e-5
RMS_EPS = 1e-6
RG_C = 8.0
ATT_SCALE = HEAD_DIM ** -0.5
ADAM_LR, ADAM_B1, ADAM_B2, ADAM_EPS, ADAM_WD, ADAM_STEP = 0.001, 0.9, 0.999, 1e-08, 0.01, 10

SCAN_CHUNK = 64
ROW_TILE = 256
ATT_TILE = 256
N_CHIPS = 4
N_DEV = 8
MESH = pl.DeviceIdType.MESH

_DN = {
    "nn": (((1,), (0,)), ((), ())),
    "nt": (((1,), (1,)), ((), ())),
    "tn": (((0,), (0,)), ((), ())),
}


def _sds(shape, dtype=f32):
    return jax.ShapeDtypeStruct(shape, dtype)


def _tile(n, target):
    best = None
    for t in range(128, min(n, target) + 1, 128):
        if n % t == 0:
            best = t
    return best or n


def _params(*sem):
    return pltpu.CompilerParams(dimension_semantics=sem)


def _mm(name, mode, dims, tiles, a_list, b_list, pairs, n_acc, epilogue, outs, extras=(), vecs=()):
    m, n, k = dims
    tm, tn, tk = tiles
    nk = k // tk
    na, nb, ne, nv, no = len(a_list), len(b_list), len(extras), len(vecs), len(outs)

    def body(*refs):
        a_refs = refs[:na]
        b_refs = refs[na:na + nb]
        e_refs = refs[na + nb:na + nb + ne]
        v_refs = refs[na + nb + ne:na + nb + ne + nv]
        o_refs = refs[na + nb + ne + nv:na + nb + ne + nv + no]
        acc_refs = refs[na + nb + ne + nv + no:]
        kk = pl.program_id(2)

        @pl.when(kk == 0)
        def _():
            for acc in acc_refs:
                acc[...] = jnp.zeros_like(acc)

        a_vals = [r[...].astype(bf16) for r in a_refs]
        b_vals = [r[...].astype(bf16) for r in b_refs]
        for ai, bi, ci in pairs:
            acc_refs[ci][...] += lax.dot_general(a_vals[ai], b_vals[bi], _DN[mode], preferred_element_type=f32)

        @pl.when(kk == nk - 1)
        def _():
            res = epilogue([acc[...] for acc in acc_refs], [e[...] for e in e_refs], [v[...] for v in v_refs])
            for o, r in zip(o_refs, res):
                o[...] = r.astype(o.dtype)

    if mode == "tn":
        a_spec = pl.BlockSpec((tk, tm), lambda i, j, kk: (kk, i))
    else:
        a_spec = pl.BlockSpec((tm, tk), lambda i, j, kk: (i, kk))
    if mode == "nt":
        b_spec = pl.BlockSpec((tn, tk), lambda i, j, kk: (j, kk))
    else:
        b_spec = pl.BlockSpec((tk, tn), lambda i, j, kk: (kk, j))
    o_spec = pl.BlockSpec((tm, tn), lambda i, j, kk: (i, j))
    v_spec = pl.BlockSpec((1, tn), lambda i, j, kk: (0, j))
    res = pl.pallas_call(
        body,
        name=name,
        grid=(m // tm, n // tn, nk),
        in_specs=[a_spec] * na + [b_spec] * nb + [o_spec] * ne + [v_spec] * nv,
        out_specs=[o_spec] * no,
        out_shape=[_sds((m, n), dt) for dt in outs],
        scratch_shapes=[pltpu.VMEM((tm, tn), f32)] * n_acc,
        compiler_params=_params("parallel", "parallel", "arbitrary"),
    )(*a_list, *b_list, *extras, *vecs)
    return res


def _layer_norm_rows(r, gamma, beta):
    mu = jnp.mean(r, axis=-1, keepdims=True)
    xc = r - mu
    var = jnp.mean(xc * xc, axis=-1, keepdims=True)
    return xc * lax.rsqrt(var + LN_EPS) * gamma + beta


def _mm_plain(name, mode, a, b, dims, scale=1.0, out_dtype=f32, add=None, add_coef=1.0, tiles=None):
    m, n, k = dims
    tiles = tiles or (_tile(m, 512), _tile(n, 1024), _tile(k, 512))

    def epilogue(accs, extras, vecs):
        r = accs[0] if scale == 1.0 else accs[0] * scale
        if extras:
            r = r + add_coef * extras[0]
        return [r]

    return _mm(name, mode, dims, tiles, [a], [b], [(0, 0, 0)], 1, epilogue, [out_dtype],
               extras=[] if add is None else [add])[0]


def _ffn_up(name, h, wg, wu):
    s = h.shape[0]

    def epilogue(accs, extras, vecs):
        g, u = accs
        return [g, u, g * jax.nn.sigmoid(g) * u]

    return _mm(name, "nn", (s, D_FF, D_MODEL), (_tile(s, 512), _tile(D_FF, 1408), 512), [h], [wg, wu],
               [(0, 0, 0), (0, 1, 1)], 2, epilogue, [f32, f32, bf16])


def _mm_ln(name, a, w, resid, gamma, beta, scale):
    s, k = a.shape

    def epilogue(accs, extras, vecs):
        r = ALPHA * extras[0] + scale * accs[0]
        return [r, _layer_norm_rows(r, vecs[0], vecs[1])]

    return _mm(name, "nn", (s, D_MODEL, k), (_tile(s, 256), D_MODEL, _tile(k, 1408 if k == D_FF else 512)), [a], [w],
               [(0, 0, 0)], 1, epilogue, [f32, f32], extras=[resid], vecs=[gamma, beta])


def _ffn_dact(name, dr, wd, g, u):
    s = dr.shape[0]

    def epilogue(accs, extras, vecs):
        da = 0.5 * accs[0]
        gg, uu = extras
        sg = jax.nn.sigmoid(gg)
        return [da * uu * (sg * (1.0 + gg * (1.0 - sg))), da * (gg * sg)]

    return _mm(name, "nt", (s, D_FF, D_MODEL), (_tile(s, 512), _tile(D_FF, 1408), 512), [dr], [wd],
               [(0, 0, 0)], 1, epilogue, [bf16, bf16], extras=[g, u])


def _mm2(name, mode, dims, a0, b0, a1, b1, add=None, add_coef=1.0, separate=False, tiles=None):
    m, n, k = dims
    tiles = tiles or (_tile(m, 512), _tile(n, 1024), _tile(k, 512))

    def epilogue(accs, extras, vecs):
        if separate:
            return list(accs)
        r = accs[0]
        if extras:
            r = r + add_coef * extras[0]
        return [r]

    a_list = [a0] if a1 is None else [a0, a1]
    b_list = [b0] if b1 is None else [b0, b1]
    pairs = [(0, 0, 0), (len(a_list) - 1, len(b_list) - 1, 1 if separate else 0)]
    return _mm(name, mode, dims, tiles, a_list, b_list, pairs, 2 if separate else 1, epilogue,
               [f32, f32] if separate else [f32], extras=[] if add is None else [add])


def _row_call(name, body, s, ins, params, outs, accs):
    tm = ROW_TILE
    in_specs = [pl.BlockSpec((tm, a.shape[1]), lambda i: (i, 0)) for a in ins]
    in_specs += [pl.BlockSpec(p.shape, lambda i, nd=p.ndim: (0,) * nd) for p in params]
    out_specs = [pl.BlockSpec((tm, o.shape[1]), lambda i: (i, 0)) for o in outs]
    out_specs += [pl.BlockSpec(a.shape, lambda i, nd=len(a.shape): (0,) * nd) for a in accs]
    return pl.pallas_call(
        body,
        name=name,
        grid=(s // tm,),
        in_specs=in_specs,
        out_specs=out_specs,
        out_shape=list(outs) + list(accs),
        compiler_params=_params("arbitrary"),
    )(*ins, *params)


def _zero_at_first(refs):
    @pl.when(pl.program_id(0) == 0)
    def _():
        for r in refs:
            r[...] = jnp.zeros_like(r)


def _ln_bwd(name, r, dh, gamma):
    s = r.shape[0]

    def body(r_ref, dh_ref, g_ref, dr_ref, dg_ref, db_ref):
        _zero_at_first([dg_ref, db_ref])
        rr = r_ref[...]
        dy = dh_ref[...]
        mu = jnp.mean(rr, axis=-1, keepdims=True)
        xc = rr - mu
        rstd = lax.rsqrt(jnp.mean(xc * xc, axis=-1, keepdims=True) + LN_EPS)
        xhat = xc * rstd
        dxh = dy * g_ref[...]
        dr_ref[...] = rstd * (dxh - jnp.mean(dxh, axis=-1, keepdims=True)
                              - xhat * jnp.mean(dxh * xhat, axis=-1, keepdims=True))
        dg_ref[...] += jnp.sum(dy * xhat, axis=0, keepdims=True)
        db_ref[...] += jnp.sum(dy, axis=0, keepdims=True)

    return _row_call(name, body, s, [r, dh], [gamma], [_sds((s, D_MODEL))], [_sds((1, D_MODEL)), _sds((1, D_MODEL))])


def _loss_head(name, y, target):
    s = y.shape[0]

    def body(y_ref, t_ref, dy_ref, l_ref):
        _zero_at_first([l_ref])
        e = y_ref[...] - t_ref[...]
        dy_ref[...] = e / D_MODEL
        l_ref[...] += 0.5 * jnp.sum(jnp.mean(e * e, axis=-1, keepdims=True), axis=0, keepdims=True)

    return _row_call(name, body, s, [y, target], [], [_sds((s, D_MODEL))], [_sds((1, 128))])


def _expm1(x):
    series = x * (1.0 + x / 2.0 * (1.0 + x / 3.0 * (1.0 + x / 4.0 * (1.0 + x / 5.0 * (1.0 + x / 6.0 * (1.0 + x / 7.0))))))
    return jnp.where(jnp.abs(x) < 0.25, series, jnp.exp(x) - 1.0)


def _gates_fn(xa, wa, wx, ba, bx, lam, tap_a, tap_x):
    xb = xa.astype(bf16)
    r = jax.nn.sigmoid(jnp.dot(xb, wa, preferred_element_type=f32) + ba + tap_a)
    i = jax.nn.sigmoid(jnp.dot(xb, wx, preferred_element_type=f32) + bx + tap_x)
    log_a = -RG_C * r * jax.nn.softplus(-lam)
    a = jnp.exp(log_a)
    gated = jnp.sqrt(-_expm1(2.0 * log_a)) * (i * xa)
    return a, gated


def _rg_gates(name, xa, wa, wx, ba, bx, lam):
    s = xa.shape[0]

    def body(xa_ref, wa_ref, wx_ref, ba_ref, bx_ref, lam_ref, a_ref, g_ref):
        a, g = _gates_fn(xa_ref[...], wa_ref[...], wx_ref[...], ba_ref[...], bx_ref[...], lam_ref[...], 0.0, 0.0)
        a_ref[...] = a
        g_ref[...] = g

    return _row_call(name, body, s, [xa], [wa, wx, ba, bx, lam], [_sds((s, D_A)), _sds((s, D_A))], [])


def _rg_gates_bwd(name, xa, ga, h_prev, wa, wx, ba, bx, lam):
    s = xa.shape[0]

    def body(xa_ref, ga_ref, hp_ref, wa_ref, wx_ref, ba_ref, bx_ref, lam_ref,
             dxa_ref, dwa_ref, dwx_ref, dba_ref, dbx_ref, dlam_ref):
        _zero_at_first([dwa_ref, dwx_ref, dba_ref, dbx_ref, dlam_ref])
        xa_v = xa_ref[...]
        zero = jnp.zeros((xa_v.shape[0], D_A), f32)
        fn = lambda x, ba_, bx_, lam_, ta, tx: _gates_fn(x, wa_ref[...], wx_ref[...], ba_, bx_, lam_, ta, tx)
        _, vjp = jax.vjp(fn, xa_v, ba_ref[...], bx_ref[...], lam_ref[...], zero, zero)
        gav = ga_ref[...]
        dxa, dba, dbx, dlam, dta, dtx = vjp((gav * hp_ref[...], gav))
        dxa_ref[...] = dxa
        xb = xa_v.astype(bf16)
        dwa_ref[...] += lax.dot_general(xb, dta.astype(bf16), _DN["tn"], preferred_element_type=f32)
        dwx_ref[...] += lax.dot_general(xb, dtx.astype(bf16), _DN["tn"], preferred_element_type=f32)
        dba_ref[...] += dba
        dbx_ref[...] += dbx
        dlam_ref[...] += dlam

    return _row_call(name, body, s, [xa, ga, h_prev], [wa, wx, ba, bx, lam], [_sds((s, D_A))],
                     [_sds((D_A, D_A)), _sds((D_A, D_A)), _sds((1, D_A)), _sds((1, D_A)), _sds((1, D_A))])


def _rms(v, g):
    return v * lax.rsqrt(jnp.mean(v * v, axis=-1, keepdims=True) + RMS_EPS) * g


def _mix_out_fn(ag, ha, ob, hre, him, cu, d, gn, tap_y, tap_gl, wcr, wci, wglu):
    out_a = jax.nn.gelu(ag) * ha
    y = (jnp.dot(hre.astype(bf16), wcr, preferred_element_type=f32)
         + jnp.dot(him.astype(bf16), wci, preferred_element_type=f32) + d * cu + tap_y)
    y2 = jax.nn.gelu(y)
    gl = jnp.dot(y2.astype(bf16), wglu, preferred_element_type=f32) + tap_gl
    out_c = y2 * jax.nn.sigmoid(gl)
    o = jnp.concatenate([_rms(out_a, gn[:, :D_A]), _rms(ob, gn[:, D_A:D_A + D_B]), _rms(out_c, gn[:, D_A + D_B:])],
                        axis=-1)
    return o, y2


def _mix_out(name, ag, ha, ob, hre, him, cu, d, gn, wcr, wci, wglu):
    s = ag.shape[0]

    def body(ag_ref, ha_ref, ob_ref, hre_ref, him_ref, cu_ref, d_ref, gn_ref, wcr_ref, wci_ref, wglu_ref, o_ref):
        o, _ = _mix_out_fn(ag_ref[...], ha_ref[...], ob_ref[...], hre_ref[...], him_ref[...], cu_ref[...], d_ref[...],
                           gn_ref[...], 0.0, 0.0, wcr_ref[...], wci_ref[...], wglu_ref[...])
        o_ref[...] = o.astype(o_ref.dtype)

    return _row_call(name, body, s, [ag, ha, ob, hre, him, cu], [d, gn, wcr, wci, wglu], [_sds((s, D_MODEL), bf16)], [])[0]


def _mix_out_bwd(name, do, ag, ha, ob, hre, him, cu, d, gn, wcr, wci, wglu):
    s = ag.shape[0]

    def body(do_ref, ag_ref, ha_ref, ob_ref, hre_ref, him_ref, cu_ref, d_ref, gn_ref, wcr_ref, wci_ref, wglu_ref,
             dag_ref, dha_ref, dob_ref, dhre_ref, dhim_ref, dcu_ref, dwcr_ref, dwci_ref, dwglu_ref, dd_ref, dgn_ref):
        _zero_at_first([dwcr_ref, dwci_ref, dwglu_ref, dd_ref, dgn_ref])
        tm = ag_ref.shape[0]
        zero = jnp.zeros((tm, D_C), f32)
        hre_v, him_v = hre_ref[...], him_ref[...]
        fn = lambda *a: _mix_out_fn(*a, wcr_ref[...], wci_ref[...], wglu_ref[...])
        _, vjp, y2 = jax.vjp(fn, ag_ref[...], ha_ref[...], ob_ref[...], hre_v, him_v, cu_ref[...], d_ref[...],
                             gn_ref[...], zero, zero, has_aux=True)
        dag, dha, dob, dhre, dhim, dcu, dd, dgn, dy, dgl = vjp(do_ref[...])
        dag_ref[...] = dag
        dha_ref[...] = dha
        dob_ref[...] = dob
        dhre_ref[...] = dhre
        dhim_ref[...] = dhim
        dcu_ref[...] = dcu
        dyb = dy.astype(bf16)
        dwcr_ref[...] += lax.dot_general(hre_v.astype(bf16), dyb, _DN["tn"], preferred_element_type=f32)
        dwci_ref[...] += lax.dot_general(him_v.astype(bf16), dyb, _DN["tn"], preferred_element_type=f32)
        dwglu_ref[...] += lax.dot_general(y2.astype(bf16), dgl.astype(bf16), _DN["tn"], preferred_element_type=f32)
        dd_ref[...] += dd
        dgn_ref[...] += dgn

    outs = [_sds((s, D_A)), _sds((s, D_A)), _sds((s, D_B)), _sds((s, S5_LANES)), _sds((s, S5_LANES)), _sds((s, D_C))]
    accs = [_sds((S5_LANES, D_C)), _sds((S5_LANES, D_C)), _sds((D_C, D_C)), _sds((1, D_C)), _sds((1, D_MODEL))]
    return _row_call(name, body, s, [do, ag, ha, ob, hre, him, cu], [d, gn, wcr, wci, wglu], outs, accs)


def _log_f(name, f, bf):
    s = f.shape[0]

    def body(f_ref, b_ref, o_ref):
        o_ref[...] = jax.nn.log_sigmoid(f_ref[...] + b_ref[...])

    return _row_call(name, body, s, [f], [bf], [_sds((s, 128))], [])[0]


def _log_f_bwd(name, dlf, f, bf):
    s = f.shape[0]

    def body(dl_ref, f_ref, b_ref, df_ref, db_ref):
        _zero_at_first([db_ref])
        df = dl_ref[...] * jax.nn.sigmoid(-(f_ref[...] + b_ref[...]))
        df_ref[...] = df
        db_ref[...] += jnp.sum(df, axis=0, keepdims=True)

    return _row_call(name, body, s, [dlf, f], [bf], [_sds((s, 128))], [_sds((1, 128))])


def _s5_decay_grad(name, hp_re, hp_im, g_re, g_im):
    s = g_re.shape[0]

    def body(hr_ref, hi_ref, gr_ref, gi_ref, dr_ref, di_ref):
        _zero_at_first([dr_ref, di_ref])
        hr, hi, gr, gi = hr_ref[...], hi_ref[...], gr_ref[...], gi_ref[...]
        dr_ref[...] += jnp.sum(hr * gr + hi * gi, axis=0, keepdims=True)
        di_ref[...] += jnp.sum(hr * gi - hi * gr, axis=0, keepdims=True)

    return _row_call(name, body, s, [hp_re, hp_im, g_re, g_im], [], [], [_sds((1, S5_LANES)), _sds((1, S5_LANES))])


def _conv_fwd(name, ax, w, b):
    s = ax.shape[0]
    tm = ROW_TILE

    def body(x_ref, halo_ref, w_ref, b_ref, o_ref):
        i = pl.program_id(0)
        x = x_ref[...]
        halo = jnp.where(i == 0, 0.0, halo_ref[...])
        ext = jnp.concatenate([halo, x], axis=0)
        acc = b_ref[...] + w_ref[3:4, :] * x
        for k in range(CONV_WIDTH - 1):
            acc = acc + w_ref[k:k + 1, :] * pltpu.roll(ext, CONV_WIDTH - 1 - k, 0)[8:, :]
        o_ref[...] = acc

    return pl.pallas_call(
        body,
        name=name,
        grid=(s // tm,),
        in_specs=[pl.BlockSpec((tm, D_A), lambda i: (i, 0)),
                  pl.BlockSpec((8, D_A), lambda i: (jnp.maximum(i * (tm // 8) - 1, 0), 0)),
                  pl.BlockSpec((CONV_WIDTH, D_A), lambda i: (0, 0)),
                  pl.BlockSpec((1, D_A), lambda i: (0, 0))],
        out_specs=pl.BlockSpec((tm, D_A), lambda i: (i, 0)),
        out_shape=_sds((s, D_A)),
        compiler_params=_params("arbitrary"),
    )(ax, ax, w, b)


def _conv_bwd(name, dxa, ax, w):
    s = ax.shape[0]
    tm = ROW_TILE
    nblk = s // tm

    def body(dx_ref, dnext_ref, x_ref, halo_ref, w_ref, dax_ref, dw_ref):
        i = pl.program_id(0)
        _zero_at_first([dw_ref])
        dx = dx_ref[...]
        dnext = jnp.where(i == nblk - 1, 0.0, dnext_ref[...])
        dext = jnp.concatenate([dx, dnext], axis=0)
        x = x_ref[...]
        halo = jnp.where(i == 0, 0.0, halo_ref[...])
        ext = jnp.concatenate([halo, x], axis=0)
        acc = w_ref[3:4, :] * dx
        dw_ref[3:4, :] += jnp.sum(dx * x, axis=0, keepdims=True)
        for k in range(CONV_WIDTH - 1):
            sh = CONV_WIDTH - 1 - k
            acc = acc + w_ref[k:k + 1, :] * pltpu.roll(dext, tm + 8 - sh, 0)[:tm, :]
            dw_ref[k:k + 1, :] += jnp.sum(dx * pltpu.roll(ext, sh, 0)[8:, :], axis=0, keepdims=True)
        dw_ref[4:5, :] += jnp.sum(dx, axis=0, keepdims=True)
        dax_ref[...] = acc

    return pl.pallas_call(
        body,
        name=name,
        grid=(nblk,),
        in_specs=[pl.BlockSpec((tm, D_A), lambda i: (i, 0)),
                  pl.BlockSpec((8, D_A), lambda i: (jnp.minimum((i + 1) * (tm // 8), s // 8 - 1), 0)),
                  pl.BlockSpec((tm, D_A), lambda i: (i, 0)),
                  pl.BlockSpec((8, D_A), lambda i: (jnp.maximum(i * (tm // 8) - 1, 0), 0)),
                  pl.BlockSpec((CONV_WIDTH, D_A), lambda i: (0, 0))],
        out_specs=[pl.BlockSpec((tm, D_A), lambda i: (i, 0)), pl.BlockSpec((8, D_A), lambda i: (0, 0))],
        out_shape=[_sds((s, D_A)), _sds((8, D_A))],
        compiler_params=_params("arbitrary"),
    )(dxa, dxa, ax, ax, w)


def _lin_scan(name, a, b, reverse):
    s, c = a.shape
    L = SCAN_CHUNK
    n = s // L
    cb = 128

    def body(a_ref, b_ref, h_ref, p_ref, hl_ref, pl_ref, car_ref):
        def pos(j):
            return (L - 1 - j) if reverse else j

        def step(jj, carry):
            h, p = carry
            j = pos(jj)
            aj = a_ref[:, j, :]
            h = aj * h + b_ref[:, j, :]
            p = aj * p
            h_ref[:, j, :] = h
            p_ref[:, j, :] = p
            return h, p

        h_last, p_last = lax.fori_loop(0, L, step, (jnp.zeros((n, cb), f32), jnp.ones((n, cb), f32)))
        hl_ref[...] = h_last
        pl_ref[...] = p_last

        def chunk_step(cc, carry):
            ch = (n - 1 - cc) if reverse else cc
            car_ref[pl.ds(ch, 1), :] = carry
            return hl_ref[pl.ds(ch, 1), :] + pl_ref[pl.ds(ch, 1), :] * carry

        lax.fori_loop(0, n, chunk_step, jnp.zeros((1, cb), f32))
        car = car_ref[...]

        def fix(j, _):
            h_ref[:, j, :] = h_ref[:, j, :] + p_ref[:, j, :] * car
            return 0

        lax.fori_loop(0, L, fix, 0)

    spec = pl.BlockSpec((n, L, cb), lambda i: (0, 0, i))
    out = pl.pallas_call(
        body,
        name=name,
        grid=(c // cb,),
        in_specs=[spec, spec],
        out_specs=spec,
        out_shape=_sds((n, L, c)),
        scratch_shapes=[pltpu.VMEM((n, L, cb), f32), pltpu.VMEM((n, cb), f32), pltpu.VMEM((n, cb), f32),
                        pltpu.VMEM((n, cb), f32)],
        compiler_params=_params("arbitrary"),
    )(a.reshape(n, L, c), b.reshape(n, L, c))
    return out.reshape(s, c)


def _s5_scan(name, b_re, b_im, a_re, a_im, reverse):
    s, c = b_re.shape
    L = SCAN_CHUNK
    n = s // L
    cb = 128

    def body(br_ref, bi_ref, ar_ref, ai_ref, hr_ref, hi_ref, pr_ref, pi_ref, hlr_ref, hli_ref, cr_ref, ci_ref):
        ar = ar_ref[...]
        ai = ai_ref[...]

        def pos(j):
            return (L - 1 - j) if reverse else j

        def step(jj, carry):
            hr, hi, pr, pi = carry
            j = pos(jj)
            hr, hi = ar * hr - ai * hi + br_ref[:, j, :], ar * hi + ai * hr + bi_ref[:, j, :]
            pr, pi = ar * pr - ai * pi, ar * pi + ai * pr
            hr_ref[:, j, :] = hr
            hi_ref[:, j, :] = hi
            pr_ref[pl.ds(j, 1), :] = pr
            pi_ref[pl.ds(j, 1), :] = pi
            return hr, hi, pr, pi

        zero = jnp.zeros((n, cb), f32)
        hr_l, hi_l, pr_l, pi_l = lax.fori_loop(0, L, step, (zero, zero, jnp.ones((1, cb), f32), jnp.zeros((1, cb), f32)))
        hlr_ref[...] = hr_l
        hli_ref[...] = hi_l

        def chunk_step(cc, carry):
            car_r, car_i = carry
            ch = (n - 1 - cc) if reverse else cc
            cr_ref[pl.ds(ch, 1), :] = car_r
            ci_ref[pl.ds(ch, 1), :] = car_i
            return (hlr_ref[pl.ds(ch, 1), :] + pr_l * car_r - pi_l * car_i,
                    hli_ref[pl.ds(ch, 1), :] + pr_l * car_i + pi_l * car_r)

        lax.fori_loop(0, n, chunk_step, (jnp.zeros((1, cb), f32), jnp.zeros((1, cb), f32)))
        car_r = cr_ref[...]
        car_i = ci_ref[...]

        def fix(j, _):
            pr = pr_ref[pl.ds(j, 1), :]
            pi = pi_ref[pl.ds(j, 1), :]
            hr_ref[:, j, :] = hr_ref[:, j, :] + pr * car_r - pi * car_i
            hi_ref[:, j, :] = hi_ref[:, j, :] + pr * car_i + pi * car_r
            return 0

        lax.fori_loop(0, L, fix, 0)

    spec = pl.BlockSpec((n, L, cb), lambda i: (0, 0, i))
    vspec = pl.BlockSpec((1, cb), lambda i: (0, i))
    hr, hi = pl.pallas_call(
        body,
        name=name,
        grid=(c // cb,),
        in_specs=[spec, spec, vspec, vspec],
        out_specs=[spec, spec],
        out_shape=[_sds((n, L, c)), _sds((n, L, c))],
        scratch_shapes=[pltpu.VMEM((L, cb), f32), pltpu.VMEM((L, cb), f32), pltpu.VMEM((n, cb), f32),
                        pltpu.VMEM((n, cb), f32), pltpu.VMEM((n, cb), f32), pltpu.VMEM((n, cb), f32)],
        compiler_params=_params("arbitrary"),
    )(b_re.reshape(n, L, c), b_im.reshape(n, L, c), a_re, a_im)
    return hr.reshape(s, c), hi.reshape(s, c)


def _causal_mask(t):
    row = lax.broadcasted_iota(jnp.int32, (t, t), 0)
    col = lax.broadcasted_iota(jnp.int32, (t, t), 1)
    return row >= col


def _attn_fwd(name, q, k, v, cq, ck):
    h, s, dh = q.shape
    t = ATT_TILE
    nq = s // t

    def body(q_ref, k_ref, v_ref, cq_ref, ck_ref, o_ref, lse_ref):
        qi = pl.program_id(1)
        qb = q_ref[...].astype(bf16)
        cqv = cq_ref[...]

        def block(kb, carry, masked):
            m, l, acc = carry
            ks = pl.multiple_of(kb * t, t)
            kk = k_ref[pl.ds(ks, t), :].astype(bf16)
            vv = v_ref[pl.ds(ks, t), :].astype(bf16)
            sc = lax.dot_general(qb, kk, _DN["nt"], preferred_element_type=f32) * ATT_SCALE + (cqv - ck_ref[kb])
            if masked:
                sc = jnp.where(_causal_mask(t), sc, -jnp.inf)
            mn = jnp.maximum(m, jnp.max(sc, axis=1, keepdims=True))
            p = jnp.exp(sc - mn)
            al = jnp.exp(m - mn)
            l = al * l + jnp.sum(p, axis=1, keepdims=True)
            acc = al * acc + jnp.dot(p.astype(bf16), vv, preferred_element_type=f32)
            return mn, l, acc

        init = (jnp.full((t, 1), -jnp.inf, f32), jnp.zeros((t, 1), f32), jnp.zeros((t, dh), f32))
        carry = lax.fori_loop(0, qi, lambda kb, c: block(kb, c, False), init)
        m, l, acc = block(qi, carry, True)
        o_ref[...] = acc / l
        lse_ref[...] = m + jnp.log(l)

    return pl.pallas_call(
        body,
        name=name,
        grid=(h, nq),
        in_specs=[pl.BlockSpec((None, t, dh), lambda hh, i: (hh, i, 0)),
                  pl.BlockSpec((None, s, dh), lambda hh, i: (hh, 0, 0)),
                  pl.BlockSpec((None, s, dh), lambda hh, i: (hh, 0, 0)),
                  pl.BlockSpec((None, t, 1), lambda hh, i: (hh, i, 0)),
                  pl.BlockSpec((None, nq, 1, t), lambda hh, i: (hh, 0, 0, 0))],
        out_specs=[pl.BlockSpec((None, t, dh), lambda hh, i: (hh, i, 0)),
                   pl.BlockSpec((None, t, 1), lambda hh, i: (hh, i, 0))],
        out_shape=[_sds((h, s, dh)), _sds((h, s, 1))],
        compiler_params=_params("parallel", "arbitrary"),
    )(q, k, v, cq, ck)


def _attn_bwd_dq(name, q, k, v, cq, ck, o, do, lse):
    h, s, dh = q.shape
    t = ATT_TILE
    nq = s // t

    def body(q_ref, k_ref, v_ref, cq_ref, ck_ref, o_ref, do_ref, lse_ref, dq_ref, dl_ref):
        qi = pl.program_id(1)
        qb = q_ref[...].astype(bf16)
        cqv = cq_ref[...]
        dov = do_ref[...]
        dob = dov.astype(bf16)
        delta = jnp.sum(dov * o_ref[...], axis=1, keepdims=True)
        lse_v = lse_ref[...]

        def block(kb, carry, masked):
            dq, psum = carry
            ks = pl.multiple_of(kb * t, t)
            kk = k_ref[pl.ds(ks, t), :].astype(bf16)
            vv = v_ref[pl.ds(ks, t), :].astype(bf16)
            sc = lax.dot_general(qb, kk, _DN["nt"], preferred_element_type=f32) * ATT_SCALE + (cqv - ck_ref[kb])
            p = jnp.exp(sc - lse_v)
            if masked:
                p = jnp.where(_causal_mask(t), p, 0.0)
            dp = lax.dot_general(dob, vv, _DN["nt"], preferred_element_type=f32)
            ds = p * (dp - delta)
            return (dq + jnp.dot(ds.astype(bf16), kk, preferred_element_type=f32),
                    psum + jnp.sum(p * dp, axis=1, keepdims=True))

        carry = lax.fori_loop(0, qi, lambda kb, c: block(kb, c, False), (jnp.zeros((t, dh), f32), jnp.zeros((t, 1), f32)))
        dq, psum = block(qi, carry, True)
        dq_ref[...] = dq * ATT_SCALE
        dl_ref[...] = psum

    qspec = pl.BlockSpec((None, t, dh), lambda hh, i: (hh, i, 0))
    fspec = pl.BlockSpec((None, s, dh), lambda hh, i: (hh, 0, 0))
    cspec = pl.BlockSpec((None, t, 1), lambda hh, i: (hh, i, 0))
    return pl.pallas_call(
        body,
        name=name,
        grid=(h, nq),
        in_specs=[qspec, fspec, fspec, cspec, pl.BlockSpec((None, nq, 1, t), lambda hh, i: (hh, 0, 0, 0)),
                  qspec, qspec, cspec],
        out_specs=[qspec, cspec],
        out_shape=[_sds((h, s, dh)), _sds((h, s, 1))],
        compiler_params=_params("parallel", "arbitrary"),
    )(q, k, v, cq, ck, o, do, lse)


def _attn_bwd_dkv(name, q, k, v, cq, ck, do, lse, delta):
    h, s, dh = q.shape
    t = ATT_TILE
    nq = s // t

    def body(q_ref, k_ref, v_ref, cq_ref, ck_ref, do_ref, lse_ref, dl_ref, dk_ref, dv_ref, dck_ref):
        kj = pl.program_id(1)
        kk = k_ref[...].astype(bf16)
        vv = v_ref[...].astype(bf16)
        ckv = ck_ref[...]

        def block(qi, carry, masked):
            dk, dv, dcs = carry
            qs = pl.multiple_of(qi * t, t)
            qq = q_ref[pl.ds(qs, t), :].astype(bf16)
            dob = do_ref[pl.ds(qs, t), :].astype(bf16)
            sc = (lax.dot_general(qq, kk, _DN["nt"], preferred_element_type=f32) * ATT_SCALE
                  + (cq_ref[pl.ds(qs, t), :] - ckv))
            p = jnp.exp(sc - lse_ref[pl.ds(qs, t), :])
            if masked:
                p = jnp.where(_causal_mask(t), p, 0.0)
            dv = dv + lax.dot_general(p.astype(bf16), dob, _DN["tn"], preferred_element_type=f32)
            dp = lax.dot_general(dob, vv, _DN["nt"], preferred_element_type=f32)
            ds = p * (dp - dl_ref[pl.ds(qs, t), :])
            dk = dk + lax.dot_general(ds.astype(bf16), qq, _DN["tn"], preferred_element_type=f32)
            return dk, dv, dcs + jnp.sum(ds, axis=0, keepdims=True)

        init = (jnp.zeros((t, dh), f32), jnp.zeros((t, dh), f32), jnp.zeros((1, t), f32))
        carry = block(kj, init, True)
        dk, dv, dcs = lax.fori_loop(kj + 1, nq, lambda qi, c: block(qi, c, False), carry)
        dk_ref[...] = dk * ATT_SCALE
        dv_ref[...] = dv
        dck_ref[...] = -dcs

    kspec = pl.BlockSpec((None, t, dh), lambda hh, j: (hh, j, 0))
    fspec = pl.BlockSpec((None, s, dh), lambda hh, j: (hh, 0, 0))
    fcol = pl.BlockSpec((None, s, 1), lambda hh, j: (hh, 0, 0))
    crow = pl.BlockSpec((None, None, 1, t), lambda hh, j: (hh, j, 0, 0))
    return pl.pallas_call(
        body,
        name=name,
        grid=(h, nq),
        in_specs=[fspec, kspec, kspec, fcol, crow, fspec, fcol, fcol],
        out_specs=[kspec, kspec, crow],
        out_shape=[_sds((h, s, dh)), _sds((h, s, dh)), _sds((h, nq, 1, t))],
        compiler_params=_params("parallel", "arbitrary"),
    )(q, k, v, cq, ck, do, lse, delta)


def _s5_disc_fn(are, aim, ldt):
    dt = jnp.exp(ldt)
    er = jnp.exp(are * dt)
    br = er * jnp.cos(aim * dt)
    bi = er * jnp.sin(aim * dt)
    nr = br - 1.0
    den = are * are + aim * aim
    return br, bi, (nr * are + bi * aim) / den, (bi * are - nr * aim) / den


def _s5_disc(name, are, aim, ldt):
    def body(a_ref, b_ref, c_ref, o0, o1, o2, o3):
        r = _s5_disc_fn(a_ref[...], b_ref[...], c_ref[...])
        o0[...], o1[...], o2[...], o3[...] = r

    shp = _sds((S5_GROUPS, S5_STATE))
    return pl.pallas_call(body, name=name, out_shape=[shp] * 4)(are, aim, ldt)


def _s5_disc_bwd(name, are, aim, ldt, cts):
    def body(a_ref, b_ref, c_ref, d0, d1, d2, d3, o0, o1, o2):
        _, vjp = jax.vjp(_s5_disc_fn, a_ref[...], b_ref[...], c_ref[...])
        o0[...], o1[...], o2[...] = vjp((d0[...], d1[...], d2[...], d3[...]))

    shp = _sds((S5_GROUPS, S5_STATE))
    return pl.pallas_call(body, name=name, out_shape=[shp, shp, _sds((S5_GROUPS, 1))])(are, aim, ldt, *cts)


def _adamw_rows(w, g, m, v):
    m = ADAM_B1 * m + (1.0 - ADAM_B1) * g
    v = ADAM_B2 * v + (1.0 - ADAM_B2) * (g * g)
    m_hat = m / (1.0 - ADAM_B1 ** ADAM_STEP)
    v_hat = v / (1.0 - ADAM_B2 ** ADAM_STEP)
    return -ADAM_LR * (m_hat / (jnp.sqrt(v_hat) + ADAM_EPS) + ADAM_WD * w), m, v


def _adamw(name, w, ga, gb, m, v):
    rows, cols = w.shape
    tr = _tile(rows, 256) if rows % 128 == 0 else rows

    def body(w_ref, ga_ref, gb_ref, m_ref, v_ref, g_out, d_out, m_out, v_out):
        g = ga_ref[...] + gb_ref[...]
        d, mm, vv = _adamw_rows(w_ref[...], g, m_ref[...], v_ref[...])
        g_out[...] = g
        d_out[...] = d
        m_out[...] = mm
        v_out[...] = vv

    spec = pl.BlockSpec((tr, cols), lambda i: (i, 0))
    return pl.pallas_call(
        body, name=name, grid=(rows // tr,), in_specs=[spec] * 5, out_specs=[spec] * 4,
        out_shape=[_sds((rows, cols))] * 4, compiler_params=_params("parallel"),
    )(w, ga, gb, m, v)


def _sum_stack(name, st):
    n, rows, cols = st.shape
    tr = _tile(rows, 256) if rows % 128 == 0 else rows

    def body(s_ref, o_ref):
        acc = s_ref[0].astype(f32)
        for j in range(1, n):
            acc = acc + s_ref[j].astype(f32)
        o_ref[...] = acc

    return pl.pallas_call(
        body, name=name, grid=(rows // tr,), in_specs=[pl.BlockSpec((n, tr, cols), lambda i: (0, i, 0))],
        out_specs=pl.BlockSpec((tr, cols), lambda i: (i, 0)), out_shape=_sds((rows, cols)),
        compiler_params=_params("parallel"),
    )(st)


def _block_diag(w):
    h, n, m = w.shape
    return jnp.einsum("hij,hg->higj", w, jnp.eye(h, dtype=w.dtype)).reshape(h * n, h * m)


def _block_diag_part(dense, h):
    n, m = dense.shape[0] // h, dense.shape[1] // h
    return jnp.einsum("higj,hg->hij", dense.reshape(h, n, h, m), jnp.eye(h, dtype=dense.dtype))


def _s5_matrices(coef_re, coef_im, b_re, b_im, c_re, c_im):
    bb_re = coef_re[:, :, None] * b_re - coef_im[:, :, None] * b_im
    bb_im = coef_re[:, :, None] * b_im + coef_im[:, :, None] * b_re
    wb_re = _block_diag(jnp.swapaxes(bb_re, 1, 2))
    wb_im = _block_diag(jnp.swapaxes(bb_im, 1, 2))
    wc_re = _block_diag(jnp.swapaxes(c_re, 1, 2))
    wc_im = _block_diag(jnp.swapaxes(-c_im, 1, 2))
    return wb_re, wb_im, wc_re, wc_im


def _heads(t):
    s = t.shape[0]
    return t.reshape(s, N_HEADS, HEAD_DIM).transpose(1, 0, 2)


def _unheads(t):
    s = t.shape[1]
    return t.transpose(1, 0, 2).reshape(s, N_HEADS * HEAD_DIM)


def _shift_down(t):
    return jnp.concatenate([jnp.zeros((1, t.shape[1]), t.dtype), t[:-1]], axis=0)


def _shift_up(t):
    return jnp.concatenate([t[1:], jnp.zeros((1, t.shape[1]), t.dtype)], axis=0)


def _row(v):
    return v.reshape(1, -1)


def _ffn_fwd(tag, h, wg, wu, wd, gamma, beta):
    g, u, act = _ffn_up(tag + "_up", h, wg, wu)
    r, out = _mm_ln(tag + "_down", act, wd, h, gamma, beta, 0.5)
    return out, dict(h=h, g=g, u=u, act=act, r=r)


def _ffn_bwd(tag, dout, sv, wg, wu, wd, gamma):
    s = dout.shape[0]
    dr, dgam, dbet = _ln_bwd(tag + "_lnb", sv["r"], dout, gamma)
    dwd = _mm_plain(tag + "_dwd", "tn", sv["act"], dr, (D_FF, D_MODEL, s), scale=0.5,
                    tiles=(_tile(D_FF, 1408), 1024, 512))
    dg, du = _ffn_dact(tag + "_dact", dr, wd, sv["g"], sv["u"])
    dwg, dwu = _mm2(tag + "_dwgu", "tn", (D_MODEL, D_FF, s), sv["h"], dg, None, du, separate=True,
                    tiles=(512, _tile(D_FF, 1408), 512))
    dh = _mm2(tag + "_dh", "nt", (s, D_MODEL, D_FF), dg, wg, du, wu, add=dr, add_coef=ALPHA,
              tiles=(_tile(s, 512), 1024, _tile(D_FF, 1408)))[0]
    return dh, dwg, dwu, dwd, dgam, dbet


def _mixer_fwd(tag, h1, w):
    s = h1.shape[0]
    nt = s // ATT_TILE
    z = _mm_plain(tag + "_win", "nn", h1, w["w_in"], (s, N_IN_P, D_MODEL), tiles=(_tile(s, 512), 768, 512))
    ax, ag = z[:, :D_A], z[:, D_A:2 * D_A]
    q, k, v = z[:, 2 * D_A:3 * D_A], z[:, 3 * D_A:4 * D_A], z[:, 4 * D_A:5 * D_A]
    f, cu = z[:, F_OFF:F_OFF + 128], z[:, CU_OFF:]
    xa = _conv_fwd(tag + "_conv", ax, w["conv_w"], w["conv_b"])
    a, gated = _rg_gates(tag + "_gates", xa, w["rg_wa"], w["rg_wx"], w["rg_ba"], w["rg_bx"], w["rg_lam"])
    ha = _lin_scan(tag + "_rgscan", a, gated, False)
    ones = jnp.ones((s, 128), f32)
    c = _lin_scan(tag + "_cumf", ones, _log_f(tag + "_logf", f, w["fox_bf"]), False)
    c6 = c[:, :N_HEADS].T
    cq, ck = c6[:, :, None], c6.reshape(N_HEADS, nt, 1, ATT_TILE)
    qh, kh, vh = _heads(q), _heads(k), _heads(v)
    oh, lse = _attn_fwd(tag + "_attn", qh, kh, vh, cq, ck)
    ob = _unheads(oh)
    bu_re, bu_im = _mm2(tag + "_s5in", "nn", (s, S5_LANES, D_C), cu, w["wb_re"], None, w["wb_im"], separate=True,
                        tiles=(_tile(s, 512), 1024, D_C))
    hre, him = _s5_scan(tag + "_s5scan", bu_re, bu_im, w["abar_re"], w["abar_im"], False)
    o = _mix_out(tag + "_mixout", ag, ha, ob, hre, him, cu, w["s5_d"], w["mix_g"], w["wc_re"], w["wc_im"], w["w_glu"])
    sv = dict(h1=h1, ax=ax, ag=ag, f=f, cu=cu, xa=xa, a=a, ha=ha, cq=cq, ck=ck, qh=qh, kh=kh, vh=vh, oh=oh, lse=lse,
              ob=ob, hre=hre, him=him, o=o)
    return o, sv


def _mixer_bwd(tag, do, dr2, sv, w):
    s = do.shape[0]
    (dag, dha, dob, dhre, dhim, dcu1, dwcr, dwci, dwglu, dd, dgn) = _mix_out_bwd(
        tag + "_mixoutb", do, sv["ag"], sv["ha"], sv["ob"], sv["hre"], sv["him"], sv["cu"], w["s5_d"], w["mix_g"],
        w["wc_re"], w["wc_im"], w["w_glu"])
    gre, gim = _s5_scan(tag + "_s5scanb", dhre, dhim, w["abar_re"], -w["abar_im"], True)
    dab_re, dab_im = _s5_decay_grad(tag + "_s5dec", _shift_down(sv["hre"]), _shift_down(sv["him"]), gre, gim)
    dwb_re, dwb_im = _mm2(tag + "_s5dwb", "tn", (D_C, S5_LANES, s), sv["cu"], gre, None, gim, separate=True,
                          tiles=(D_C, 1024, 512))
    dcu = _mm2(tag + "_s5dcu", "nt", (s, D_C, S5_LANES), gre, w["wb_re"], gim, w["wb_im"], add=dcu1,
               tiles=(_tile(s, 512), D_C, 1024))[0]
    doh = _heads(dob)
    dqh, delta = _attn_bwd_dq(tag + "_attndq", sv["qh"], sv["kh"], sv["vh"], sv["cq"], sv["ck"], sv["oh"], doh,
                              sv["lse"])
    dkh, dvh, dck = _attn_bwd_dkv(tag + "_attndkv", sv["qh"], sv["kh"], sv["vh"], sv["cq"], sv["ck"], doh, sv["lse"],
                                  delta)
    dc = jnp.pad(dck.reshape(N_HEADS, s).T, ((0, 0), (0, 128 - N_HEADS)))
    dlf = _lin_scan(tag + "_cumfb", jnp.ones((s, 128), f32), dc, True)
    df, dbf = _log_f_bwd(tag + "_logfb", dlf, sv["f"], w["fox_bf"])
    ga = _lin_scan(tag + "_rgscanb", _shift_up(sv["a"]), dha, True)
    dxa, dwa, dwx, dba, dbx, dlam = _rg_gates_bwd(tag + "_gatesb", sv["xa"], ga, _shift_down(sv["ha"]), w["rg_wa"],
                                                  w["rg_wx"], w["rg_ba"], w["rg_bx"], w["rg_lam"])
    dax, dconv = _conv_bwd(tag + "_convb", dxa, sv["ax"], w["conv_w"])
    dz = jnp.concatenate([dax, dag, _unheads(dqh), _unheads(dkh), _unheads(dvh), df, dcu], axis=-1).astype(bf16)
    dwin = _mm_plain(tag + "_dwin", "tn", sv["h1"], dz, (D_MODEL, N_IN_P, s), tiles=(512, 768, 512))
    dh1 = _mm_plain(tag + "_dh1", "nt", dz, w["w_in"], (s, D_MODEL, N_IN_P), add=dr2, add_coef=ALPHA,
                    tiles=(_tile(s, 512), 1024, 768))
    grads = dict(dwin=dwin, dwglu=dwglu, dconv=dconv, dwa=dwa, dwx=dwx, dba=dba, dbx=dbx, dlam=dlam, dbf=dbf,
                 dab_re=dab_re, dab_im=dab_im, dwb_re=dwb_re, dwb_im=dwb_im, dwcr=dwcr, dwci=dwci, dd=dd, dgn=dgn)
    return dh1, grads


SMALL_NAMES = ["ln1_g", "ln1_b", "conv_w", "conv_b", "rg_w_a", "rg_b_a", "rg_w_x", "rg_b_x", "rg_lambda", "fox_b_f",
               "s5_a_re", "s5_a_im", "s5_log_dt", "s5_b_re", "s5_b_im", "s5_c_re", "s5_c_im", "s5_d", "mix_norm_g",
               "ln2_g", "ln2_b", "ln3_g", "ln3_b"]
BIG_NAMES = ["ffn1_w_gate", "ffn1_w_up", "ffn1_w_down", "w_in", "s5_w_glu", "w_out", "ffn2_w_gate", "ffn2_w_up",
             "ffn2_w_down"]


def _local_step(x, target, big, small):
    h = x
    saved = []
    for l in range(DEPTH):
        sm = {n: small[n][l] for n in SMALL_NAMES}
        abar_re, abar_im, coef_re, coef_im = _s5_disc(f"l{l}_s5disc", sm["s5_a_re"], sm["s5_a_im"],
                                                      sm["s5_log_dt"].reshape(S5_GROUPS, 1))
        mats, mats_vjp = jax.vjp(_s5_matrices, coef_re, coef_im, sm["s5_b_re"], sm["s5_b_im"], sm["s5_c_re"],
                                 sm["s5_c_im"])
        w = dict(
            w_in=big["w_in"][l], w_glu=big["s5_w_glu"][l], conv_w=sm["conv_w"], conv_b=_row(sm["conv_b"]),
            rg_wa=_block_diag(sm["rg_w_a"]).astype(bf16), rg_wx=_block_diag(sm["rg_w_x"]).astype(bf16),
            rg_ba=_row(sm["rg_b_a"]), rg_bx=_row(sm["rg_b_x"]), rg_lam=_row(sm["rg_lambda"]),
            fox_bf=jnp.pad(_row(sm["fox_b_f"]), ((0, 0), (0, 128 - N_HEADS))),
            abar_re=_row(abar_re), abar_im=_row(abar_im),
            wb_re=mats[0].astype(bf16), wb_im=mats[1].astype(bf16), wc_re=mats[2].astype(bf16),
            wc_im=mats[3].astype(bf16), s5_d=_row(sm["s5_d"]), mix_g=_row(sm["mix_norm_g"]))
        h1, sv1 = _ffn_fwd(f"l{l}_ffn1", h, big["ffn1_w_gate"][l], big["ffn1_w_up"][l], big["ffn1_w_down"][l],
                           _row(sm["ln1_g"]), _row(sm["ln1_b"]))
        o, svm = _mixer_fwd(f"l{l}_mix", h1, w)
        r2, h2 = _mm_ln(f"l{l}_wout", o, big["w_out"][l], h1, _row(sm["ln2_g"]), _row(sm["ln2_b"]), 1.0)
        h3, sv2 = _ffn_fwd(f"l{l}_ffn2", h2, big["ffn2_w_gate"][l], big["ffn2_w_up"][l], big["ffn2_w_down"][l],
                           _row(sm["ln3_g"]), _row(sm["ln3_b"]))
        saved.append(dict(sm=sm, w=w, sv1=sv1, svm=svm, r2=r2, sv2=sv2, mats_vjp=mats_vjp))
        h = h3

    dh, loss_row = _loss_head("loss_head", h, target)
    s = x.shape[0]
    gbig = {n: [None] * DEPTH for n in BIG_NAMES}
    gsmall = {n: [None] * DEPTH for n in SMALL_NAMES}
    for l in reversed(range(DEPTH)):
        sd = saved[l]
        sm, w = sd["sm"], sd["w"]
        dh2, dwg, dwu, dwd, dgam, dbet = _ffn_bwd(f"l{l}_ffn2", dh, sd["sv2"], big["ffn2_w_gate"][l],
                                                 big["ffn2_w_up"][l], big["ffn2_w_down"][l], _row(sm["ln3_g"]))
        gbig["ffn2_w_gate"][l], gbig["ffn2_w_up"][l], gbig["ffn2_w_down"][l] = dwg, dwu, dwd
        gsmall["ln3_g"][l], gsmall["ln3_b"][l] = dgam[0], dbet[0]
        dr2, dgam, dbet = _ln_bwd(f"l{l}_ln2b", sd["r2"], dh2, _row(sm["ln2_g"]))
        gsmall["ln2_g"][l], gsmall["ln2_b"][l] = dgam[0], dbet[0]
        gbig["w_out"][l] = _mm_plain(f"l{l}_dwout", "tn", sd["svm"]["o"], dr2, (D_MODEL, D_MODEL, s))
        do = _mm_plain(f"l{l}_do", "nt", dr2, big["w_out"][l], (s, D_MODEL, D_MODEL))
        dh1, g = _mixer_bwd(f"l{l}_mix", do, dr2, sd["svm"], w)
        gbig["w_in"][l], gbig["s5_w_glu"][l] = g["dwin"], g["dwglu"]
        gsmall["conv_w"][l], gsmall["conv_b"][l] = g["dconv"][:CONV_WIDTH], g["dconv"][CONV_WIDTH]
        gsmall["rg_w_a"][l] = _block_diag_part(g["dwa"], N_HEADS)
        gsmall["rg_w_x"][l] = _block_diag_part(g["dwx"], N_HEADS)
        gsmall["rg_b_a"][l], gsmall["rg_b_x"][l], gsmall["rg_lambda"][l] = g["dba"][0], g["dbx"][0], g["dlam"][0]
        gsmall["fox_b_f"][l] = g["dbf"][0, :N_HEADS]
        dcoef_re, dcoef_im, db_re, db_im, dc_re, dc_im = sd["mats_vjp"]((g["dwb_re"], g["dwb_im"], g["dwcr"], g["dwci"]))
        da_re, da_im, dldt = _s5_disc_bwd(
            f"l{l}_s5discb", sm["s5_a_re"], sm["s5_a_im"], sm["s5_log_dt"].reshape(S5_GROUPS, 1),
            (g["dab_re"].reshape(S5_GROUPS, S5_STATE), g["dab_im"].reshape(S5_GROUPS, S5_STATE), dcoef_re, dcoef_im))
        gsmall["s5_a_re"][l], gsmall["s5_a_im"][l], gsmall["s5_log_dt"][l] = da_re, da_im, dldt[:, 0]
        gsmall["s5_b_re"][l], gsmall["s5_b_im"][l], gsmall["s5_c_re"][l], gsmall["s5_c_im"][l] = db_re, db_im, dc_re, dc_im
        gsmall["s5_d"][l], gsmall["mix_norm_g"][l] = g["dd"][0], g["dgn"][0]
        dh, dwg, dwu, dwd, dgam, dbet = _ffn_bwd(f"l{l}_ffn1", dh1, sd["sv1"], big["ffn1_w_gate"][l],
                                                big["ffn1_w_up"][l], big["ffn1_w_down"][l], _row(sm["ln1_g"]))
        gbig["ffn1_w_gate"][l], gbig["ffn1_w_up"][l], gbig["ffn1_w_down"][l] = dwg, dwu, dwd
        gsmall["ln1_g"][l], gsmall["ln1_b"][l] = dgam[0], dbet[0]
    gbig = {n: jnp.stack(v) for n, v in gbig.items()}
    gsmall = {n: jnp.stack(v) for n, v in gsmall.items()}
    return loss_row[0, 0], dh, gbig, gsmall


def _position():
    return lax.axis_index("x"), lax.axis_index("y"), lax.axis_index("c")


_ANY = pl.BlockSpec(memory_space=pl.ANY)


def _chip_gather(name, shards):
    n = len(shards)

    def body(*refs):
        in_refs, out_refs = refs[:n], refs[n:2 * n]
        send_sems, recv_sems, local_sems = refs[2 * n:]
        x, y, c = _position()
        me = 2 * x + y
        peers = [(1 - x, y), (x, 1 - y), (1 - x, 1 - y)]
        local = [pltpu.make_async_copy(in_refs[i], out_refs[i].at[me], local_sems.at[i]) for i in range(n)]
        for cp in local:
            cp.start()
        sends = []
        for i in range(n):
            for r, (px, py) in enumerate(peers):
                cp = pltpu.make_async_remote_copy(
                    src_ref=in_refs[i], dst_ref=out_refs[i].at[me], send_sem=send_sems.at[3 * i + r],
                    recv_sem=recv_sems.at[3 * i + r], device_id=(px, py, c), device_id_type=MESH)
                cp.start()
                sends.append(cp)
        for i in range(n):
            for r, (px, py) in enumerate(peers):
                pltpu.make_async_remote_copy(
                    src_ref=in_refs[i], dst_ref=out_refs[i].at[2 * px + py], send_sem=send_sems.at[3 * i + r],
                    recv_sem=recv_sems.at[3 * i + r], device_id=(px, py, c), device_id_type=MESH).wait_recv()
        for cp in sends:
            cp.wait_send()
        for cp in local:
            cp.wait()

    return pl.pallas_call(
        body, name=name, in_specs=[_ANY] * n, out_specs=[_ANY] * n,
        out_shape=[_sds((N_CHIPS,) + a.shape, a.dtype) for a in shards],
        scratch_shapes=[pltpu.SemaphoreType.DMA((3 * n,)), pltpu.SemaphoreType.DMA((3 * n,)),
                        pltpu.SemaphoreType.DMA((n,))],
    )(*shards)


def _chip_scatter(name, stacks):
    n = len(stacks)

    def body(*refs):
        in_refs, out_refs = refs[:n], refs[n:2 * n]
        send_sems, recv_sems, local_sems = refs[2 * n:]
        x, y, c = _position()
        me = 2 * x + y
        peers = [(1 - x, y), (x, 1 - y), (1 - x, 1 - y)]
        local = [pltpu.make_async_copy(in_refs[i].at[me], out_refs[i].at[me], local_sems.at[i]) for i in range(n)]
        for cp in local:
            cp.start()
        sends = []
        for i in range(n):
            for r, (px, py) in enumerate(peers):
                cp = pltpu.make_async_remote_copy(
                    src_ref=in_refs[i].at[2 * px + py], dst_ref=out_refs[i].at[me], send_sem=send_sems.at[3 * i + r],
                    recv_sem=recv_sems.at[3 * i + r], device_id=(px, py, c), device_id_type=MESH)
                cp.start()
                sends.append(cp)
        for i in range(n):
            for r, (px, py) in enumerate(peers):
                pltpu.make_async_remote_copy(
                    src_ref=in_refs[i].at[me], dst_ref=out_refs[i].at[2 * px + py], send_sem=send_sems.at[3 * i + r],
                    recv_sem=recv_sems.at[3 * i + r], device_id=(px, py, c), device_id_type=MESH).wait_recv()
        for cp in sends:
            cp.wait_send()
        for cp in local:
            cp.wait()

    return pl.pallas_call(
        body, name=name, in_specs=[_ANY] * n, out_specs=[_ANY] * n,
        out_shape=[_sds(a.shape, a.dtype) for a in stacks],
        scratch_shapes=[pltpu.SemaphoreType.DMA((3 * n,)), pltpu.SemaphoreType.DMA((3 * n,)),
                        pltpu.SemaphoreType.DMA((n,))],
    )(*stacks)


def _sibling_swap(name, arrs):
    n = len(arrs)

    def body(*refs):
        in_refs, out_refs = refs[:n], refs[n:2 * n]
        send_sems, recv_sems = refs[2 * n:]
        x, y, c = _position()
        copies = [pltpu.make_async_remote_copy(
            src_ref=in_refs[i], dst_ref=out_refs[i], send_sem=send_sems.at[i], recv_sem=recv_sems.at[i],
            device_id=(x, y, 1 - c), device_id_type=MESH) for i in range(n)]
        for cp in copies:
            cp.start()
        for cp in copies:
            cp.wait_recv()
        for cp in copies:
            cp.wait_send()

    return pl.pallas_call(
        body, name=name, in_specs=[_ANY] * n, out_specs=[_ANY] * n,
        out_shape=[_sds(a.shape, a.dtype) for a in arrs],
        scratch_shapes=[pltpu.SemaphoreType.DMA((n,)), pltpu.SemaphoreType.DMA((n,))],
    )(*arrs)


def _dev_gather(name, arr):
    def body(in_ref, out_ref, send_sems, recv_sems, local_sem):
        x, y, c = _position()
        me = 4 * x + 2 * y + c
        local = pltpu.make_async_copy(in_ref, out_ref.at[me], local_sem)
        local.start()
        peers = []
        for k in range(1, N_DEV):
            peers.append((1 - x if k & 4 else x, 1 - y if k & 2 else y, 1 - c if k & 1 else c))
        sends = []
        for k, peer in enumerate(peers):
            cp = pltpu.make_async_remote_copy(src_ref=in_ref, dst_ref=out_ref.at[me], send_sem=send_sems.at[k],
                                              recv_sem=recv_sems.at[k], device_id=peer, device_id_type=MESH)
            cp.start()
            sends.append(cp)
        for k, (px, py, pc) in enumerate(peers):
            pltpu.make_async_remote_copy(src_ref=in_ref, dst_ref=out_ref.at[4 * px + 2 * py + pc],
                                         send_sem=send_sems.at[k], recv_sem=recv_sems.at[k], device_id=(px, py, pc),
                                         device_id_type=MESH).wait_recv()
        for cp in sends:
            cp.wait_send()
        local.wait()

    return pl.pallas_call(
        body, name=name, in_specs=[_ANY], out_specs=_ANY, out_shape=_sds((N_DEV,) + arr.shape, arr.dtype),
        scratch_shapes=[pltpu.SemaphoreType.DMA((N_DEV - 1,)), pltpu.SemaphoreType.DMA((N_DEV - 1,)),
                        pltpu.SemaphoreType.DMA],
    )(arr)


COLUMN_SHARDED = ("ffn1_w_gate", "ffn1_w_up", "ffn2_w_gate", "ffn2_w_up")
PACK_QUANTUM = 128 * 256


def _permute_in_cols(w):
    pad = jnp.zeros(w.shape[:-1] + (128 - N_HEADS,), w.dtype)
    return jnp.concatenate([w[..., :F_OFF + N_HEADS], pad, w[..., F_OFF + N_HEADS:]], axis=-1)


def _unpermute_in_cols(w):
    return jnp.concatenate([w[..., :F_OFF + N_HEADS], w[..., CU_OFF:]], axis=-1)


def _unstack(name, st):
    _, l, r, c = st.shape
    if name in COLUMN_SHARDED:
        return st.transpose(1, 2, 0, 3).reshape(l, r, N_CHIPS * c)
    return st.transpose(1, 0, 2, 3).reshape(l, N_CHIPS * r, c)


def _restack(name, g):
    l, r, c = g.shape
    if name in COLUMN_SHARDED:
        return g.reshape(l, r, N_CHIPS, c // N_CHIPS).transpose(2, 0, 1, 3)
    return g.reshape(l, N_CHIPS, r // N_CHIPS, c).transpose(1, 0, 2, 3)


def _pack(arrs):
    flat = jnp.concatenate([a.reshape(-1) for a in arrs])
    pad = -flat.shape[0] % PACK_QUANTUM
    return jnp.pad(flat, (0, pad)).reshape(-1, 128)


def _unpack(buf, shapes):
    flat = buf.reshape(-1)
    out, off = [], 0
    for shp in shapes:
        size = math.prod(shp)
        out.append(flat[off:off + size].reshape(shp))
        off += size
    return out


WEIGHT_NAMES = ["ffn1_w_gate", "ffn1_w_up", "ffn1_w_down", "ln1_g", "ln1_b", "w_in", "conv_w", "conv_b", "rg_w_a",
                "rg_b_a", "rg_w_x", "rg_b_x", "rg_lambda", "fox_b_f", "s5_a_re", "s5_a_im", "s5_log_dt", "s5_b_re",
                "s5_b_im", "s5_c_re", "s5_c_im", "s5_d", "s5_w_glu", "mix_norm_g", "w_out", "ln2_g", "ln2_b",
                "ffn2_w_gate", "ffn2_w_up", "ffn2_w_down", "ln3_g", "ln3_b"]


def _train_step(x, loss_target, w, m, v):
    ix, iy, _ = _position()
    chip = 2 * ix + iy

    shards = [(_permute_in_cols(w[n]) if n == "w_in" else w[n]).astype(bf16) for n in BIG_NAMES]
    stacks = _chip_gather("gather_weights", shards + [w["conv_w"]])
    big = {n: _unstack(n, st) for n, st in zip(BIG_NAMES, stacks)}
    small = {n: w[n] for n in SMALL_NAMES}
    small["conv_w"] = stacks[-1].transpose(1, 2, 0, 3).reshape(DEPTH, CONV_WIDTH, D_A)

    loss_local, gx, gbig, gsmall = _local_step(x[0], loss_target[0], big, small)

    sent = [_restack(n, gbig[n]).astype(bf16) for n in BIG_NAMES]
    recv = _chip_scatter("scatter_grads", sent)
    partial = {}
    for n, st in zip(BIG_NAMES, recv):
        _, l, r, c = st.shape
        p = _sum_stack("sum_" + n, st.reshape(N_CHIPS, l * r, c))
        partial[n] = _unpermute_in_cols(p) if n == "w_in" else p
    other = dict(zip(BIG_NAMES, _sibling_swap("swap_grads", [partial[n] for n in BIG_NAMES])))

    small_shapes = [gsmall[n].shape for n in SMALL_NAMES]
    total = _sum_stack("sum_small", _dev_gather("gather_small", _pack([gsmall[n] for n in SMALL_NAMES])))
    gsm = dict(zip(SMALL_NAMES, _unpack(total, small_shapes)))
    cw = D_A // N_CHIPS
    gsm["conv_w"] = lax.dynamic_slice_in_dim(gsm["conv_w"], chip * cw, cw, axis=2)

    grads, deltas, new_m, new_v = {}, {}, {}, {}
    for n in BIG_NAMES:
        shp = w[n].shape
        two_d = (shp[0] * shp[1], shp[2])
        g, d, mm, vv = _adamw("adamw_" + n, w[n].reshape(two_d), partial[n], other[n], m[n].reshape(two_d),
                              v[n].reshape(two_d))
        grads[n], deltas[n], new_m[n], new_v[n] = (t.reshape(shp) for t in (g, d, mm, vv))
    shapes = [w[n].shape for n in SMALL_NAMES]
    gp = _pack([gsm[n] for n in SMALL_NAMES])
    res = _adamw("adamw_small", _pack([w[n] for n in SMALL_NAMES]), gp, jnp.zeros_like(gp),
                 _pack([m[n] for n in SMALL_NAMES]), _pack([v[n] for n in SMALL_NAMES]))
    for dst, buf in zip((grads, deltas, new_m, new_v), res):
        dst.update(zip(SMALL_NAMES, _unpack(buf, shapes)))

    loss = lax.psum(loss_local, ("x", "y", "c"))
    return (loss, gx[None], *[grads[n] for n in WEIGHT_NAMES], *[deltas[n] for n in WEIGHT_NAMES],
            *[new_m[n] for n in WEIGHT_NAMES], *[new_v[n] for n in WEIGHT_NAMES])


def kernel(x, ffn1_w_gate, ffn1_w_up, ffn1_w_down, ln1_g, ln1_b, w_in, conv_w, conv_b, rg_w_a, rg_b_a, rg_w_x, rg_b_x, rg_lambda, fox_b_f, s5_a_re, s5_a_im, s5_log_dt, s5_b_re, s5_b_im, s5_c_re, s5_c_im, s5_d, s5_w_glu, mix_norm_g, w_out, ln2_g, ln2_b, ffn2_w_gate, ffn2_w_up, ffn2_w_down, ln3_g, ln3_b, loss_target, m_ffn1_w_gate, m_ffn1_w_up, m_ffn1_w_down, m_ln1_g, m_ln1_b, m_w_in, m_conv_w, m_conv_b, m_rg_w_a, m_rg_b_a, m_rg_w_x, m_rg_b_x, m_rg_lambda, m_fox_b_f, m_s5_a_re, m_s5_a_im, m_s5_log_dt, m_s5_b_re, m_s5_b_im, m_s5_c_re, m_s5_c_im, m_s5_d, m_s5_w_glu, m_mix_norm_g, m_w_out, m_ln2_g, m_ln2_b, m_ffn2_w_gate, m_ffn2_w_up, m_ffn2_w_down, m_ln3_g, m_ln3_b, v_ffn1_w_gate, v_ffn1_w_up, v_ffn1_w_down, v_ln1_g, v_ln1_b, v_w_in, v_conv_w, v_conv_b, v_rg_w_a, v_rg_b_a, v_rg_w_x, v_rg_b_x, v_rg_lambda, v_fox_b_f, v_s5_a_re, v_s5_a_im, v_s5_log_dt, v_s5_b_re, v_s5_b_im, v_s5_c_re, v_s5_c_im, v_s5_d, v_s5_w_glu, v_mix_norm_g, v_w_out, v_ln2_g, v_ln2_b, v_ffn2_w_gate, v_ffn2_w_up, v_ffn2_w_down, v_ln3_g, v_ln3_b):
    args = dict(locals())
    w = {n: args[n] for n in WEIGHT_NAMES}
    m = {n: args["m_" + n] for n in WEIGHT_NAMES}
    v = {n: args["v_" + n] for n in WEIGHT_NAMES}
    return _train_step(x, loss_target, w, m, v)
```

```python
import functools
import math

import jax
import jax.numpy as jnp
from jax import lax
from jax.experimental import pallas as pl
from jax.experimental.pallas import tpu as pltpu

f32 = jnp.float32
bf16 = jnp.bfloat16

D_MODEL = 1024
D_FF = 2816
D_A = 384
D_B = 384
D_C = 256
N_HEADS = 6
HEAD_DIM = 64
S5_GROUPS = 16
S5_GROUP = 16
S5_STATE = 64
S5_LANES = S5_GROUPS * S5_STATE
N_IN = 2 * D_A + 3 * D_B + N_HEADS + D_C
F_OFF = 5 * D_A
CU_OFF = F_OFF + 128
N_IN_P = CU_OFF + D_C
CONV_WIDTH = 4
DEPTH = 2
ALPHA = (2 * DEPTH) ** 0.25
LN_EPS = 1e-5
RMS_EPS = 1e-6
RG_C = 8.0
ATT_SCALE = HEAD_DIM ** -0.5
ADAM_LR, ADAM_B1, ADAM_B2, ADAM_EPS, ADAM_WD, ADAM_STEP = 0.001, 0.9, 0.999, 1e-08, 0.01, 10

SCAN_CHUNK = 64
ROW_TILE = 256
ATT_TILE = 256
N_CHIPS = 4
N_DEV = 8
MESH = pl.DeviceIdType.MESH

_DN = {
    "nn": (((1,), (0,)), ((), ())),
    "nt": (((1,), (1,)), ((), ())),
    "tn": (((0,), (0,)), ((), ())),
}


def _sds(shape, dtype=f32):
    return jax.ShapeDtypeStruct(shape, dtype)


def _tile(n, target):
    best = None
    for t in range(128, min(n, target) + 1, 128):
        if n % t == 0:
            best = t
    return best or n


def _params(*sem):
    return pltpu.CompilerParams(dimension_semantics=sem)


def _mm(name, mode, dims, tiles, a_list, b_list, pairs, n_acc, epilogue, outs, extras=(), vecs=()):
    m, n, k = dims
    tm, tn, tk = tiles
    nk = k // tk
    na, nb, ne, nv, no = len(a_list), len(b_list), len(extras), len(vecs), len(outs)

    def body(*refs):
        a_refs = refs[:na]
        b_refs = refs[na:na + nb]
        e_refs = refs[na + nb:na + nb + ne]
        v_refs = refs[na + nb + ne:na + nb + ne + nv]
        o_refs = refs[na + nb + ne + nv:na + nb + ne + nv + no]
        acc_refs = refs[na + nb + ne + nv + no:]
        kk = pl.program_id(2)

        @pl.when(kk == 0)
        def _():
            for acc in acc_refs:
                acc[...] = jnp.zeros_like(acc)

        a_vals = [r[...].astype(bf16) for r in a_refs]
        b_vals = [r[...].astype(bf16) for r in b_refs]
        for ai, bi, ci in pairs:
            acc_refs[ci][...] += lax.dot_general(a_vals[ai], b_vals[bi], _DN[mode], preferred_element_type=f32)

        @pl.when(kk == nk - 1)
        def _():
            res = epilogue([acc[...] for acc in acc_refs], [e[...] for e in e_refs], [v[...] for v in v_refs])
            for o, r in zip(o_refs, res):
                o[...] = r.astype(o.dtype)

    if mode == "tn":
        a_spec = pl.BlockSpec((tk, tm), lambda i, j, kk: (kk, i))
    else:
        a_spec = pl.BlockSpec((tm, tk), lambda i, j, kk: (i, kk))
    if mode == "nt":
        b_spec = pl.BlockSpec((tn, tk), lambda i, j, kk: (j, kk))
    else:
        b_spec = pl.BlockSpec((tk, tn), lambda i, j, kk: (kk, j))
    o_spec = pl.BlockSpec((tm, tn), lambda i, j, kk: (i, j))
    v_spec = pl.BlockSpec((1, tn), lambda i, j, kk: (0, j))
    res = pl.pallas_call(
        body,
        name=name,
        grid=(m // tm, n // tn, nk),
        in_specs=[a_spec] * na + [b_spec] * nb + [o_spec] * ne + [v_spec] * nv,
        out_specs=[o_spec] * no,
        out_shape=[_sds((m, n), dt) for dt in outs],
        scratch_shapes=[pltpu.VMEM((tm, tn), f32)] * n_acc,
        compiler_params=_params("parallel", "parallel", "arbitrary"),
    )(*a_list, *b_list, *extras, *vecs)
    return res


def _layer_norm_rows(r, gamma, beta):
    mu = jnp.mean(r, axis=-1, keepdims=True)
    xc = r - mu
    var = jnp.mean(xc * xc, axis=-1, keepdims=True)
    return xc * lax.rsqrt(var + LN_EPS) * gamma + beta


def _mm_plain(name, mode, a, b, dims, scale=1.0, out_dtype=f32, add=None, add_coef=1.0, tiles=None):
    m, n, k = dims
    tiles = tiles or (_tile(m, 512), _tile(n, 1024), _tile(k, 1024))

    def epilogue(accs, extras, vecs):
        r = accs[0] if scale == 1.0 else accs[0] * scale
        if extras:
            r = r + add_coef * extras[0]
        return [r]

    return _mm(name, mode, dims, tiles, [a], [b], [(0, 0, 0)], 1, epilogue, [out_dtype],
               extras=[] if add is None else [add])[0]


def _ffn_up(name, h, wg, wu):
    s = h.shape[0]

    def epilogue(accs, extras, vecs):
        g, u = accs
        return [g, u, g * jax.nn.sigmoid(g) * u]

    return _mm(name, "nn", (s, D_FF, D_MODEL), (_tile(s, 512), _tile(D_FF, 1408), D_MODEL), [h], [wg, wu],
               [(0, 0, 0), (0, 1, 1)], 2, epilogue, [f32, f32, bf16])


def _mm_ln(name, a, w, resid, gamma, beta, scale):
    s, k = a.shape

    def epilogue(accs, extras, vecs):
        r = ALPHA * extras[0] + scale * accs[0]
        return [r, _layer_norm_rows(r, vecs[0], vecs[1])]

    return _mm(name, "nn", (s, D_MODEL, k), (_tile(s, 256), D_MODEL, _tile(k, 1408)), [a], [w],
               [(0, 0, 0)], 1, epilogue, [f32, f32], extras=[resid], vecs=[gamma, beta])


def _ffn_dact(name, dr, wd, g, u):
    s = dr.shape[0]

    def epilogue(accs, extras, vecs):
        da = 0.5 * accs[0]
        gg, uu = extras
        sg = jax.nn.sigmoid(gg)
        return [da * uu * (sg * (1.0 + gg * (1.0 - sg))), da * (gg * sg)]

    return _mm(name, "nt", (s, D_FF, D_MODEL), (_tile(s, 512), _tile(D_FF, 1408), D_MODEL), [dr], [wd],
               [(0, 0, 0)], 1, epilogue, [bf16, bf16], extras=[g, u])


def _mm2(name, mode, dims, a0, b0, a1, b1, add=None, add_coef=1.0, separate=False, tiles=None):
    m, n, k = dims
    tiles = tiles or (_tile(m, 512), _tile(n, 1024), _tile(k, 1024))

    def epilogue(accs, extras, vecs):
        if separate:
            return list(accs)
        r = accs[0]
        if extras:
            r = r + add_coef * extras[0]
        return [r]

    a_list = [a0] if a1 is None else [a0, a1]
    b_list = [b0] if b1 is None else [b0, b1]
    pairs = [(0, 0, 0), (len(a_list) - 1, len(b_list) - 1, 1 if separate else 0)]
    return _mm(name, mode, dims, tiles, a_list, b_list, pairs, 2 if separate else 1, epilogue,
               [f32, f32] if separate else [f32], extras=[] if add is None else [add])


def _row_call(name, body, s, ins, params, outs, accs):
    tm = ROW_TILE
    in_specs = [pl.BlockSpec((tm, a.shape[1]), lambda i: (i, 0)) for a in ins]
    in_specs += [pl.BlockSpec(p.shape, lambda i, nd=p.ndim: (0,) * nd) for p in params]
    out_specs = [pl.BlockSpec((tm, o.shape[1]), lambda i: (i, 0)) for o in outs]
    out_specs += [pl.BlockSpec(a.shape, lambda i, nd=len(a.shape): (0,) * nd) for a in accs]
    return pl.pallas_call(
        body,
        name=name,
        grid=(s // tm,),
        in_specs=in_specs,
        out_specs=out_specs,
        out_shape=list(outs) + list(accs),
        compiler_params=_params("arbitrary"),
    )(*ins, *params)


def _zero_at_first(refs):
    @pl.when(pl.program_id(0) == 0)
    def _():
        for r in refs:
            r[...] = jnp.zeros_like(r)


def _ln_bwd(name, r, dh, gamma):
    s = r.shape[0]

    def body(r_ref, dh_ref, g_ref, dr_ref, dg_ref, db_ref):
        _zero_at_first([dg_ref, db_ref])
        rr = r_ref[...]
        dy = dh_ref[...]
        mu = jnp.mean(rr, axis=-1, keepdims=True)
        xc = rr - mu
        rstd = lax.rsqrt(jnp.mean(xc * xc, axis=-1, keepdims=True) + LN_EPS)
        xhat = xc * rstd
        dxh = dy * g_ref[...]
        dr_ref[...] = rstd * (dxh - jnp.mean(dxh, axis=-1, keepdims=True)
                              - xhat * jnp.mean(dxh * xhat, axis=-1, keepdims=True))
        dg_ref[...] += jnp.sum(dy * xhat, axis=0, keepdims=True)
        db_ref[...] += jnp.sum(dy, axis=0, keepdims=True)

    return _row_call(name, body, s, [r, dh], [gamma], [_sds((s, D_MODEL))], [_sds((1, D_MODEL)), _sds((1, D_MODEL))])


def _loss_head(name, y, target):
    s = y.shape[0]

    def body(y_ref, t_ref, dy_ref, l_ref):
        _zero_at_first([l_ref])
        e = y_ref[...] - t_ref[...]
        dy_ref[...] = e / D_MODEL
        l_ref[...] += 0.5 * jnp.sum(jnp.mean(e * e, axis=-1, keepdims=True), axis=0, keepdims=True)

    return _row_call(name, body, s, [y, target], [], [_sds((s, D_MODEL))], [_sds((1, 128))])


def _expm1(x):
    series = x * (1.0 + x / 2.0 * (1.0 + x / 3.0 * (1.0 + x / 4.0 * (1.0 + x / 5.0 * (1.0 + x / 6.0 * (1.0 + x / 7.0))))))
    return jnp.where(jnp.abs(x) < 0.25, series, jnp.exp(x) - 1.0)


def _gates_fn(xa, wa, wx, ba, bx, lam, tap_a, tap_x):
    xb = xa.astype(bf16)
    r = jax.nn.sigmoid(jnp.dot(xb, wa, preferred_element_type=f32) + ba + tap_a)
    i = jax.nn.sigmoid(jnp.dot(xb, wx, preferred_element_type=f32) + bx + tap_x)
    log_a = -RG_C * r * jax.nn.softplus(-lam)
    a = jnp.exp(log_a)
    gated = jnp.sqrt(-_expm1(2.0 * log_a)) * (i * xa)
    return a, gated


def _rg_gates(name, xa, wa, wx, ba, bx, lam):
    s = xa.shape[0]

    def body(xa_ref, wa_ref, wx_ref, ba_ref, bx_ref, lam_ref, a_ref, g_ref):
        a, g = _gates_fn(xa_ref[...], wa_ref[...], wx_ref[...], ba_ref[...], bx_ref[...], lam_ref[...], 0.0, 0.0)
        a_ref[...] = a
        g_ref[...] = g

    return _row_call(name, body, s, [xa], [wa, wx, ba, bx, lam], [_sds((s, D_A)), _sds((s, D_A))], [])


def _rg_gates_bwd(name, xa, ga, h_prev, wa, wx, ba, bx, lam):
    s = xa.shape[0]

    def body(xa_ref, ga_ref, hp_ref, wa_ref, wx_ref, ba_ref, bx_ref, lam_ref,
             dxa_ref, dwa_ref, dwx_ref, dba_ref, dbx_ref, dlam_ref):
        _zero_at_first([dwa_ref, dwx_ref, dba_ref, dbx_ref, dlam_ref])
        xa_v = xa_ref[...]
        zero = jnp.zeros((xa_v.shape[0], D_A), f32)
        fn = lambda x, ba_, bx_, lam_, ta, tx: _gates_fn(x, wa_ref[...], wx_ref[...], ba_, bx_, lam_, ta, tx)
        _, vjp = jax.vjp(fn, xa_v, ba_ref[...], bx_ref[...], lam_ref[...], zero, zero)
        gav = ga_ref[...]
        dxa, dba, dbx, dlam, dta, dtx = vjp((gav * hp_ref[...], gav))
        dxa_ref[...] = dxa
        xb = xa_v.astype(bf16)
        dwa_ref[...] += lax.dot_general(xb, dta.astype(bf16), _DN["tn"], preferred_element_type=f32)
        dwx_ref[...] += lax.dot_general(xb, dtx.astype(bf16), _DN["tn"], preferred_element_type=f32)
        dba_ref[...] += dba
        dbx_ref[...] += dbx
        dlam_ref[...] += dlam

    return _row_call(name, body, s, [xa, ga, h_prev], [wa, wx, ba, bx, lam], [_sds((s, D_A))],
                     [_sds((D_A, D_A)), _sds((D_A, D_A)), _sds((1, D_A)), _sds((1, D_A)), _sds((1, D_A))])


def _rms(v, g):
    return v * lax.rsqrt(jnp.mean(v * v, axis=-1, keepdims=True) + RMS_EPS) * g


def _mix_out_fn(ag, ha, ob, hre, him, cu, d, gn, tap_y, tap_gl, wcr, wci, wglu):
    out_a = jax.nn.gelu(ag) * ha
    y = (jnp.dot(hre.astype(bf16), wcr, preferred_element_type=f32)
         + jnp.dot(him.astype(bf16), wci, preferred_element_type=f32) + d * cu + tap_y)
    y2 = jax.nn.gelu(y)
    gl = jnp.dot(y2.astype(bf16), wglu, preferred_element_type=f32) + tap_gl
    out_c = y2 * jax.nn.sigmoid(gl)
    o = jnp.concatenate([_rms(out_a, gn[:, :D_A]), _rms(ob, gn[:, D_A:D_A + D_B]), _rms(out_c, gn[:, D_A + D_B:])],
                        axis=-1)
    return o, y2


def _mix_out(name, ag, ha, ob, hre, him, cu, d, gn, wcr, wci, wglu):
    s = ag.shape[0]

    def body(ag_ref, ha_ref, ob_ref, hre_ref, him_ref, cu_ref, d_ref, gn_ref, wcr_ref, wci_ref, wglu_ref, o_ref):
        o, _ = _mix_out_fn(ag_ref[...], ha_ref[...], ob_ref[...], hre_ref[...], him_ref[...], cu_ref[...], d_ref[...],
                           gn_ref[...], 0.0, 0.0, wcr_ref[...], wci_ref[...], wglu_ref[...])
        o_ref[...] = o.astype(o_ref.dtype)

    return _row_call(name, body, s, [ag, ha, ob, hre, him, cu], [d, gn, wcr, wci, wglu], [_sds((s, D_MODEL), bf16)], [])[0]


def _mix_out_bwd(name, do, ag, ha, ob, hre, him, cu, d, gn, wcr, wci, wglu):
    s = ag.shape[0]

    def body(do_ref, ag_ref, ha_ref, ob_ref, hre_ref, him_ref, cu_ref, d_ref, gn_ref, wcr_ref, wci_ref, wglu_ref,
             dag_ref, dha_ref, dob_ref, dhre_ref, dhim_ref, dcu_ref, dwcr_ref, dwci_ref, dwglu_ref, dd_ref, dgn_ref):
        _zero_at_first([dwcr_ref, dwci_ref, dwglu_ref, dd_ref, dgn_ref])
        tm = ag_ref.shape[0]
        zero = jnp.zeros((tm, D_C), f32)
        hre_v, him_v = hre_ref[...], him_ref[...]
        fn = lambda *a: _mix_out_fn(*a, wcr_ref[...], wci_ref[...], wglu_ref[...])
        _, vjp, y2 = jax.vjp(fn, ag_ref[...], ha_ref[...], ob_ref[...], hre_v, him_v, cu_ref[...], d_ref[...],
                             gn_ref[...], zero, zero, has_aux=True)
        dag, dha, dob, dhre, dhim, dcu, dd, dgn, dy, dgl = vjp(do_ref[...])
        dag_ref[...] = dag
        dha_ref[...] = dha
        dob_ref[...] = dob
        dhre_ref[...] = dhre
        dhim_ref[...] = dhim
        dcu_ref[...] = dcu
        dyb = dy.astype(bf16)
        dwcr_ref[...] += lax.dot_general(hre_v.astype(bf16), dyb, _DN["tn"], preferred_element_type=f32)
        dwci_ref[...] += lax.dot_general(him_v.astype(bf16), dyb, _DN["tn"], preferred_element_type=f32)
        dwglu_ref[...] += lax.dot_general(y2.astype(bf16), dgl.astype(bf16), _DN["tn"], preferred_element_type=f32)
        dd_ref[...] += dd
        dgn_ref[...] += dgn

    outs = [_sds((s, D_A)), _sds((s, D_A)), _sds((s, D_B)), _sds((s, S5_LANES)), _sds((s, S5_LANES)), _sds((s, D_C))]
    accs = [_sds((S5_LANES, D_C)), _sds((S5_LANES, D_C)), _sds((D_C, D_C)), _sds((1, D_C)), _sds((1, D_MODEL))]
    return _row_call(name, body, s, [do, ag, ha, ob, hre, him, cu], [d, gn, wcr, wci, wglu], outs, accs)


def _log_f(name, f, bf):
    s = f.shape[0]

    def body(f_ref, b_ref, o_ref):
        o_ref[...] = jax.nn.log_sigmoid(f_ref[...] + b_ref[...])

    return _row_call(name, body, s, [f], [bf], [_sds((s, 128))], [])[0]


def _log_f_bwd(name, dlf, f, bf):
    s = f.shape[0]

    def body(dl_ref, f_ref, b_ref, df_ref, db_ref):
        _zero_at_first([db_ref])
        df = dl_ref[...] * jax.nn.sigmoid(-(f_ref[...] + b_ref[...]))
        df_ref[...] = df
        db_ref[...] += jnp.sum(df, axis=0, keepdims=True)

    return _row_call(name, body, s, [dlf, f], [bf], [_sds((s, 128))], [_sds((1, 128))])


def _s5_decay_grad(name, hp_re, hp_im, g_re, g_im):
    s = g_re.shape[0]

    def body(hr_ref, hi_ref, gr_ref, gi_ref, dr_ref, di_ref):
        _zero_at_first([dr_ref, di_ref])
        hr, hi, gr, gi = hr_ref[...], hi_ref[...], gr_ref[...], gi_ref[...]
        dr_ref[...] += jnp.sum(hr * gr + hi * gi, axis=0, keepdims=True)
        di_ref[...] += jnp.sum(hr * gi - hi * gr, axis=0, keepdims=True)

    return _row_call(name, body, s, [hp_re, hp_im, g_re, g_im], [], [], [_sds((1, S5_LANES)), _sds((1, S5_LANES))])


def _conv_fwd(name, ax, w, b):
    s = ax.shape[0]
    tm = ROW_TILE

    def body(x_ref, halo_ref, w_ref, b_ref, o_ref):
        i = pl.program_id(0)
        x = x_ref[...]
        halo = jnp.where(i == 0, 0.0, halo_ref[...])
        ext = jnp.concatenate([halo, x], axis=0)
        acc = b_ref[...] + w_ref[3:4, :] * x
        for k in range(CONV_WIDTH - 1):
            acc = acc + w_ref[k:k + 1, :] * pltpu.roll(ext, CONV_WIDTH - 1 - k, 0)[8:, :]
        o_ref[...] = acc

    return pl.pallas_call(
        body,
        name=name,
        grid=(s // tm,),
        in_specs=[pl.BlockSpec((tm, D_A), lambda i: (i, 0)),
                  pl.BlockSpec((8, D_A), lambda i: (jnp.maximum(i * (tm // 8) - 1, 0), 0)),
                  pl.BlockSpec((CONV_WIDTH, D_A), lambda i: (0, 0)),
                  pl.BlockSpec((1, D_A), lambda i: (0, 0))],
        out_specs=pl.BlockSpec((tm, D_A), lambda i: (i, 0)),
        out_shape=_sds((s, D_A)),
        compiler_params=_params("arbitrary"),
    )(ax, ax, w, b)


def _conv_bwd(name, dxa, ax, w):
    s = ax.shape[0]
    tm = ROW_TILE
    nblk = s // tm

    def body(dx_ref, dnext_ref, x_ref, halo_ref, w_ref, dax_ref, dw_ref):
        i = pl.program_id(0)
        _zero_at_first([dw_ref])
        dx = dx_ref[...]
        dnext = jnp.where(i == nblk - 1, 0.0, dnext_ref[...])
        dext = jnp.concatenate([dx, dnext], axis=0)
        x = x_ref[...]
        halo = jnp.where(i == 0, 0.0, halo_ref[...])
        ext = jnp.concatenate([halo, x], axis=0)
        acc = w_ref[3:4, :] * dx
        dw_ref[3:4, :] += jnp.sum(dx * x, axis=0, keepdims=True)
        for k in range(CONV_WIDTH - 1):
            sh = CONV_WIDTH - 1 - k
            acc = acc + w_ref[k:k + 1, :] * pltpu.roll(dext, tm + 8 - sh, 0)[:tm, :]
            dw_ref[k:k + 1, :] += jnp.sum(dx * pltpu.roll(ext, sh, 0)[8:, :], axis=0, keepdims=True)
        dw_ref[4:5, :] += jnp.sum(dx, axis=0, keepdims=True)
        dax_ref[...] = acc

    return pl.pallas_call(
        body,
        name=name,
        grid=(nblk,),
        in_specs=[pl.BlockSpec((tm, D_A), lambda i: (i, 0)),
                  pl.BlockSpec((8, D_A), lambda i: (jnp.minimum((i + 1) * (tm // 8), s // 8 - 1), 0)),
                  pl.BlockSpec((tm, D_A), lambda i: (i, 0)),
                  pl.BlockSpec((8, D_A), lambda i: (jnp.maximum(i * (tm // 8) - 1, 0), 0)),
                  pl.BlockSpec((CONV_WIDTH, D_A), lambda i: (0, 0))],
        out_specs=[pl.BlockSpec((tm, D_A), lambda i: (i, 0)), pl.BlockSpec((8, D_A), lambda i: (0, 0))],
        out_shape=[_sds((s, D_A)), _sds((8, D_A))],
        compiler_params=_params("arbitrary"),
    )(dxa, dxa, ax, ax, w)


def _lin_scan(name, a, b, reverse):
    s, c = a.shape
    L = SCAN_CHUNK
    n = s // L
    cb = 128

    def body(a_ref, b_ref, h_ref, p_ref, hl_ref, pl_ref, car_ref):
        def pos(j):
            return (L - 1 - j) if reverse else j

        def step(jj, carry):
            h, p = carry
            j = pos(jj)
            aj = a_ref[:, j, :]
            h = aj * h + b_ref[:, j, :]
            p = aj * p
            h_ref[:, j, :] = h
            p_ref[:, j, :] = p
            return h, p

        h_last, p_last = lax.fori_loop(0, L, step, (jnp.zeros((n, cb), f32), jnp.ones((n, cb), f32)))
        hl_ref[...] = h_last
        pl_ref[...] = p_last

        def chunk_step(cc, carry):
            ch = (n - 1 - cc) if reverse else cc
            car_ref[pl.ds(ch, 1), :] = carry
            return hl_ref[pl.ds(ch, 1), :] + pl_ref[pl.ds(ch, 1), :] * carry

        lax.fori_loop(0, n, chunk_step, jnp.zeros((1, cb), f32))
        car = car_ref[...]

        def fix(j, _):
            h_ref[:, j, :] = h_ref[:, j, :] + p_ref[:, j, :] * car
            return 0

        lax.fori_loop(0, L, fix, 0)

    spec = pl.BlockSpec((n, L, cb), lambda i: (0, 0, i))
    out = pl.pallas_call(
        body,
        name=name,
        grid=(c // cb,),
        in_specs=[spec, spec],
        out_specs=spec,
        out_shape=_sds((n, L, c)),
        scratch_shapes=[pltpu.VMEM((n, L, cb), f32), pltpu.VMEM((n, cb), f32), pltpu.VMEM((n, cb), f32),
                        pltpu.VMEM((n, cb), f32)],
        compiler_params=_params("arbitrary"),
    )(a.reshape(n, L, c), b.reshape(n, L, c))
    return out.reshape(s, c)


def _s5_scan(name, b_re, b_im, a_re, a_im, reverse):
    s, c = b_re.shape
    L = SCAN_CHUNK
    n = s // L
    cb = 128

    def body(br_ref, bi_ref, ar_ref, ai_ref, hr_ref, hi_ref, pr_ref, pi_ref, hlr_ref, hli_ref, cr_ref, ci_ref):
        ar = ar_ref[...]
        ai = ai_ref[...]

        def pos(j):
            return (L - 1 - j) if reverse else j

        def step(jj, carry):
            hr, hi, pr, pi = carry
            j = pos(jj)
            hr, hi = ar * hr - ai * hi + br_ref[:, j, :], ar * hi + ai * hr + bi_ref[:, j, :]
            pr, pi = ar * pr - ai * pi, ar * pi + ai * pr
            hr_ref[:, j, :] = hr
            hi_ref[:, j, :] = hi
            pr_ref[pl.ds(j, 1), :] = pr
            pi_ref[pl.ds(j, 1), :] = pi
            return hr, hi, pr, pi

        zero = jnp.zeros((n, cb), f32)
        hr_l, hi_l, pr_l, pi_l = lax.fori_loop(0, L, step, (zero, zero, jnp.ones((1, cb), f32), jnp.zeros((1, cb), f32)))
        hlr_ref[...] = hr_l
        hli_ref[...] = hi_l

        def chunk_step(cc, carry):
            car_r, car_i = carry
            ch = (n - 1 - cc) if reverse else cc
            cr_ref[pl.ds(ch, 1), :] = car_r
            ci_ref[pl.ds(ch, 1), :] = car_i
            return (hlr_ref[pl.ds(ch, 1), :] + pr_l * car_r - pi_l * car_i,
                    hli_ref[pl.ds(ch, 1), :] + pr_l * car_i + pi_l * car_r)

        lax.fori_loop(0, n, chunk_step, (jnp.zeros((1, cb), f32), jnp.zeros((1, cb), f32)))
        car_r = cr_ref[...]
        car_i = ci_ref[...]

        def fix(j, _):
            pr = pr_ref[pl.ds(j, 1), :]
            pi = pi_ref[pl.ds(j, 1), :]
            hr_ref[:, j, :] = hr_ref[:, j, :] + pr * car_r - pi * car_i
            hi_ref[:, j, :] = hi_ref[:, j, :] + pr * car_i + pi * car_r
            return 0

        lax.fori_loop(0, L, fix, 0)

    spec = pl.BlockSpec((n, L, cb), lambda i: (0, 0, i))
    vspec = pl.BlockSpec((1, cb), lambda i: (0, i))
    hr, hi = pl.pallas_call(
        body,
        name=name,
        grid=(c // cb,),
        in_specs=[spec, spec, vspec, vspec],
        out_specs=[spec, spec],
        out_shape=[_sds((n, L, c)), _sds((n, L, c))],
        scratch_shapes=[pltpu.VMEM((L, cb), f32), pltpu.VMEM((L, cb), f32), pltpu.VMEM((n, cb), f32),
                        pltpu.VMEM((n, cb), f32), pltpu.VMEM((n, cb), f32), pltpu.VMEM((n, cb), f32)],
        compiler_params=_params("arbitrary"),
    )(b_re.reshape(n, L, c), b_im.reshape(n, L, c), a_re, a_im)
    return hr.reshape(s, c), hi.reshape(s, c)


def _causal_mask(t):
    row = lax.broadcasted_iota(jnp.int32, (t, t), 0)
    col = lax.broadcasted_iota(jnp.int32, (t, t), 1)
    return row >= col


def _attn_fwd(name, q, k, v, cq, ck):
    h, s, dh = q.shape
    t = ATT_TILE
    nq = s // t

    def body(q_ref, k_ref, v_ref, cq_ref, ck_ref, o_ref, lse_ref):
        qi = pl.program_id(1)
        qb = q_ref[...].astype(bf16)
        cqv = cq_ref[...]

        def block(kb, carry, masked):
            m, l, acc = carry
            ks = pl.multiple_of(kb * t, t)
            kk = k_ref[pl.ds(ks, t), :].astype(bf16)
            vv = v_ref[pl.ds(ks, t), :].astype(bf16)
            sc = lax.dot_general(qb, kk, _DN["nt"], preferred_element_type=f32) * ATT_SCALE + (cqv - ck_ref[kb])
            if masked:
                sc = jnp.where(_causal_mask(t), sc, -jnp.inf)
            mn = jnp.maximum(m, jnp.max(sc, axis=1, keepdims=True))
            p = jnp.exp(sc - mn)
            al = jnp.exp(m - mn)
            l = al * l + jnp.sum(p, axis=1, keepdims=True)
            acc = al * acc + jnp.dot(p.astype(bf16), vv, preferred_element_type=f32)
            return mn, l, acc

        init = (jnp.full((t, 1), -jnp.inf, f32), jnp.zeros((t, 1), f32), jnp.zeros((t, dh), f32))
        carry = lax.fori_loop(0, qi, lambda kb, c: block(kb, c, False), init)
        m, l, acc = block(qi, carry, True)
        o_ref[...] = acc / l
        lse_ref[...] = m + jnp.log(l)

    return pl.pallas_call(
        body,
        name=name,
        grid=(h, nq),
        in_specs=[pl.BlockSpec((None, t, dh), lambda hh, i: (hh, i, 0)),
                  pl.BlockSpec((None, s, dh), lambda hh, i: (hh, 0, 0)),
                  pl.BlockSpec((None, s, dh), lambda hh, i: (hh, 0, 0)),
                  pl.BlockSpec((None, t, 1), lambda hh, i: (hh, i, 0)),
                  pl.BlockSpec((None, nq, 1, t), lambda hh, i: (hh, 0, 0, 0))],
        out_specs=[pl.BlockSpec((None, t, dh), lambda hh, i: (hh, i, 0)),
                   pl.BlockSpec((None, t, 1), lambda hh, i: (hh, i, 0))],
        out_shape=[_sds((h, s, dh)), _sds((h, s, 1))],
        compiler_params=_params("parallel", "arbitrary"),
    )(q, k, v, cq, ck)


def _attn_bwd_dq(name, q, k, v, cq, ck, o, do, lse):
    h, s, dh = q.shape
    t = ATT_TILE
    nq = s // t

    def body(q_ref, k_ref, v_ref, cq_ref, ck_ref, o_ref, do_ref, lse_ref, dq_ref, dl_ref):
        qi = pl.program_id(1)
        qb = q_ref[...].astype(bf16)
        cqv = cq_ref[...]
        dov = do_ref[...]
        dob = dov.astype(bf16)
        delta = jnp.sum(dov * o_ref[...], axis=1, keepdims=True)
        lse_v = lse_ref[...]

        def block(kb, carry, masked):
            dq, psum = carry
            ks = pl.multiple_of(kb * t, t)
            kk = k_ref[pl.ds(ks, t), :].astype(bf16)
            vv = v_ref[pl.ds(ks, t), :].astype(bf16)
            sc = lax.dot_general(qb, kk, _DN["nt"], preferred_element_type=f32) * ATT_SCALE + (cqv - ck_ref[kb])
            p = jnp.exp(sc - lse_v)
            if masked:
                p = jnp.where(_causal_mask(t), p, 0.0)
            dp = lax.dot_general(dob, vv, _DN["nt"], preferred_element_type=f32)
            ds = p * (dp - delta)
            return (dq + jnp.dot(ds.astype(bf16), kk, preferred_element_type=f32),
                    psum + jnp.sum(p * dp, axis=1, keepdims=True))

        carry = lax.fori_loop(0, qi, lambda kb, c: block(kb, c, False), (jnp.zeros((t, dh), f32), jnp.zeros((t, 1), f32)))
        dq, psum = block(qi, carry, True)
        dq_ref[...] = dq * ATT_SCALE
        dl_ref[...] = psum

    qspec = pl.BlockSpec((None, t, dh), lambda hh, i: (hh, i, 0))
    fspec = pl.BlockSpec((None, s, dh), lambda hh, i: (hh, 0, 0))
    cspec = pl.BlockSpec((None, t, 1), lambda hh, i: (hh, i, 0))
    return pl.pallas_call(
        body,
        name=name,
        grid=(h, nq),
        in_specs=[qspec, fspec, fspec, cspec, pl.BlockSpec((None, nq, 1, t), lambda hh, i: (hh, 0, 0, 0)),
                  qspec, qspec, cspec],
        out_specs=[qspec, cspec],
        out_shape=[_sds((h, s, dh)), _sds((h, s, 1))],
        compiler_params=_params("parallel", "arbitrary"),
    )(q, k, v, cq, ck, o, do, lse)


def _attn_bwd_dkv(name, q, k, v, cq, ck, do, lse, delta):
    h, s, dh = q.shape
    t = ATT_TILE
    nq = s // t

    def body(q_ref, k_ref, v_ref, cq_ref, ck_ref, do_ref, lse_ref, dl_ref, dk_ref, dv_ref, dck_ref):
        kj = pl.program_id(1)
        kk = k_ref[...].astype(bf16)
        vv = v_ref[...].astype(bf16)
        ckv = ck_ref[...]

        def block(qi, carry, masked):
            dk, dv, dcs = carry
            qs = pl.multiple_of(qi * t, t)
            qq = q_ref[pl.ds(qs, t), :].astype(bf16)
            dob = do_ref[pl.ds(qs, t), :].astype(bf16)
            sc = (lax.dot_general(qq, kk, _DN["nt"], preferred_element_type=f32) * ATT_SCALE
                  + (cq_ref[pl.ds(qs, t), :] - ckv))
            p = jnp.exp(sc - lse_ref[pl.ds(qs, t), :])
            if masked:
                p = jnp.where(_causal_mask(t), p, 0.0)
            dv = dv + lax.dot_general(p.astype(bf16), dob, _DN["tn"], preferred_element_type=f32)
            dp = lax.dot_general(dob, vv, _DN["nt"], preferred_element_type=f32)
            ds = p * (dp - dl_ref[pl.ds(qs, t), :])
            dk = dk + lax.dot_general(ds.astype(bf16), qq, _DN["tn"], preferred_element_type=f32)
            return dk, dv, dcs + jnp.sum(ds, axis=0, keepdims=True)

        init = (jnp.zeros((t, dh), f32), jnp.zeros((t, dh), f32), jnp.zeros((1, t), f32))
        carry = block(kj, init, True)
        dk, dv, dcs = lax.fori_loop(kj + 1, nq, lambda qi, c: block(qi, c, False), carry)
        dk_ref[...] = dk * ATT_SCALE
        dv_ref[...] = dv
        dck_ref[...] = -dcs

    kspec = pl.BlockSpec((None, t, dh), lambda hh, j: (hh, j, 0))
    fspec = pl.BlockSpec((None, s, dh), lambda hh, j: (hh, 0, 0))
    fcol = pl.BlockSpec((None, s, 1), lambda hh, j: (hh, 0, 0))
    crow = pl.BlockSpec((None, None, 1, t), lambda hh, j: (hh, j, 0, 0))
    return pl.pallas_call(
        body,
        name=name,
        grid=(h, nq),
        in_specs=[fspec, kspec, kspec, fcol, crow, fspec, fcol, fcol],
        out_specs=[kspec, kspec, crow],
        out_shape=[_sds((h, s, dh)), _sds((h, s, dh)), _sds((h, nq, 1, t))],
        compiler_params=_params("parallel", "arbitrary"),
    )(q, k, v, cq, ck, do, lse, delta)


ATT_FEAT = 128
ATT_TQ = 1024
ATT_TK = 256


def _att_tiles(s):
    tq = min(ATT_TQ, s)
    return tq, ATT_TK, tq // ATT_TK


def _keys_le_queries(tk, tq, k0, q0):
    row = lax.broadcasted_iota(jnp.int32, (tk, tq), 0) + k0
    col = lax.broadcasted_iota(jnp.int32, (tk, tq), 1) + q0
    return row <= col


def _attn_fwd_t(name, qt_aug, k_aug, vt):
    h, _, s = qt_aug.shape
    tq, tk, ratio = _att_tiles(s)

    def body(qt_ref, k_ref, vt_ref, o_ref, lse_ref):
        qi = pl.program_id(1)
        qt = qt_ref[...]

        def block(kb, carry, masked):
            m, l, acc = carry
            ks = pl.multiple_of(kb * tk, tk)
            st = jnp.dot(k_ref[pl.ds(ks, tk), :], qt, preferred_element_type=f32)
            if masked:
                st = jnp.where(_keys_le_queries(tk, tq, ks, qi * tq), st, -jnp.inf)
            mn = jnp.maximum(m, jnp.max(st, axis=0, keepdims=True))
            p = jnp.exp(st - mn)
            al = jnp.exp(m - mn)
            l = al * l + jnp.sum(p, axis=0, keepdims=True)
            acc = al * acc + jnp.dot(vt_ref[kb], p.astype(bf16), preferred_element_type=f32)
            return mn, l, acc

        init = (jnp.full((1, tq), -jnp.inf, f32), jnp.zeros((1, tq), f32), jnp.zeros((HEAD_DIM, tq), f32))
        first = lax.fori_loop(0, qi * ratio, lambda kb, c: block(kb, c, False), init)
        m, l, acc = lax.fori_loop(qi * ratio, (qi + 1) * ratio, lambda kb, c: block(kb, c, True), first)
        o_ref[...] = acc / l
        lse_ref[...] = m + jnp.log(l)

    return pl.pallas_call(
        body,
        name=name,
        grid=(h, s // tq),
        in_specs=[pl.BlockSpec((None, ATT_FEAT, tq), lambda hh, i: (hh, 0, i)),
                  pl.BlockSpec((None, s, ATT_FEAT), lambda hh, i: (hh, 0, 0)),
                  pl.BlockSpec((None, s // tk, HEAD_DIM, tk), lambda hh, i: (hh, 0, 0, 0))],
        out_specs=[pl.BlockSpec((None, HEAD_DIM, tq), lambda hh, i: (hh, 0, i)),
                   pl.BlockSpec((None, 1, tq), lambda hh, i: (hh, 0, i))],
        out_shape=[_sds((h, HEAD_DIM, s)), _sds((h, 1, s))],
        compiler_params=_params("parallel", "arbitrary"),
    )(qt_aug, k_aug, vt)


def _attn_bwd_dq_t(name, qt_aug, k_aug, v, kt, ot, dot_, lse):
    h, _, s = qt_aug.shape
    tq, tk, ratio = _att_tiles(s)

    def body(qt_ref, k_ref, v_ref, kt_ref, o_ref, do_ref, lse_ref, dq_ref, dl_ref):
        qi = pl.program_id(1)
        qt = qt_ref[...]
        dov = do_ref[...]
        dob = dov.astype(bf16)
        delta = jnp.sum(dov * o_ref[...], axis=0, keepdims=True)
        lse_v = lse_ref[...]

        def block(kb, carry, masked):
            dq, psum = carry
            ks = pl.multiple_of(kb * tk, tk)
            st = jnp.dot(k_ref[pl.ds(ks, tk), :], qt, preferred_element_type=f32)
            p = jnp.exp(st - lse_v)
            if masked:
                p = jnp.where(_keys_le_queries(tk, tq, ks, qi * tq), p, 0.0)
            dp = jnp.dot(v_ref[pl.ds(ks, tk), :], dob, preferred_element_type=f32)
            ds = p * (dp - delta)
            return (dq + jnp.dot(kt_ref[kb], ds.astype(bf16), preferred_element_type=f32),
                    psum + jnp.sum(p * dp, axis=0, keepdims=True))

        carry = lax.fori_loop(0, qi * ratio, lambda kb, c: block(kb, c, False),
                              (jnp.zeros((HEAD_DIM, tq), f32), jnp.zeros((1, tq), f32)))
        dq, psum = lax.fori_loop(qi * ratio, (qi + 1) * ratio, lambda kb, c: block(kb, c, True), carry)
        dq_ref[...] = dq * ATT_SCALE
        dl_ref[...] = psum

    qspec = pl.BlockSpec((None, HEAD_DIM, tq), lambda hh, i: (hh, 0, i))
    rspec = pl.BlockSpec((None, 1, tq), lambda hh, i: (hh, 0, i))
    return pl.pallas_call(
        body,
        name=name,
        grid=(h, s // tq),
        in_specs=[pl.BlockSpec((None, ATT_FEAT, tq), lambda hh, i: (hh, 0, i)),
                  pl.BlockSpec((None, s, ATT_FEAT), lambda hh, i: (hh, 0, 0)),
                  pl.BlockSpec((None, s, HEAD_DIM), lambda hh, i: (hh, 0, 0)),
                  pl.BlockSpec((None, s // tk, HEAD_DIM, tk), lambda hh, i: (hh, 0, 0, 0)),
                  qspec, qspec, rspec],
        out_specs=[qspec, rspec],
        out_shape=[_sds((h, HEAD_DIM, s)), _sds((h, 1, s))],
        compiler_params=_params("parallel", "arbitrary"),
    )(qt_aug, k_aug, v, kt, ot, dot_, lse)


def _attn_bwd_dkv_t(name, qt_blocks, k_aug, v, qh, do, dot_blocks, lse, delta):
    h, s, _ = k_aug.shape
    tq, tk, ratio = _att_tiles(s)
    nq = s // tq

    def body(qt_ref, k_ref, v_ref, q_ref, do_ref, dot_ref, lse_ref, dl_ref, dk_ref, dv_ref, dck_ref, dsum_ref):
        kj = pl.program_id(1)
        kk = k_ref[...]
        vv = v_ref[...]
        dsum_ref[...] = jnp.zeros_like(dsum_ref)

        def block(qi, carry, masked):
            dk, dv = carry
            qs = pl.multiple_of(qi * tq, tq)
            st = jnp.dot(kk, qt_ref[qi], preferred_element_type=f32)
            p = jnp.exp(st - lse_ref[qi])
            if masked:
                p = jnp.where(_keys_le_queries(tk, tq, kj * tk, qs), p, 0.0)
            dv = dv + jnp.dot(p.astype(bf16), do_ref[pl.ds(qs, tq), :], preferred_element_type=f32)
            dp = jnp.dot(vv, dot_ref[qi], preferred_element_type=f32)
            ds = p * (dp - dl_ref[qi])
            dsum_ref[...] += ds
            dk = dk + jnp.dot(ds.astype(bf16), q_ref[pl.ds(qs, tq), :], preferred_element_type=f32)
            return dk, dv

        first = kj // ratio
        carry = block(first, (jnp.zeros((tk, HEAD_DIM), f32), jnp.zeros((tk, HEAD_DIM), f32)), True)
        dk, dv = lax.fori_loop(first + 1, nq, lambda qi, c: block(qi, c, False), carry)
        dk_ref[...] = dk
        dv_ref[...] = dv
        dck_ref[...] = -jnp.sum(dsum_ref[...], axis=1, keepdims=True)

    full = lambda shape: pl.BlockSpec((None,) + shape, lambda hh, j: (hh,) + (0,) * len(shape))
    kspec = pl.BlockSpec((None, tk, HEAD_DIM), lambda hh, j: (hh, j, 0))
    return pl.pallas_call(
        body,
        name=name,
        grid=(h, s // tk),
        in_specs=[full((nq, ATT_FEAT, tq)),
                  pl.BlockSpec((None, tk, ATT_FEAT), lambda hh, j: (hh, j, 0)),
                  kspec, full((s, HEAD_DIM)), full((s, HEAD_DIM)), full((nq, HEAD_DIM, tq)),
                  full((nq, 1, tq)), full((nq, 1, tq))],
        out_specs=[kspec, kspec, pl.BlockSpec((None, tk, 1), lambda hh, j: (hh, j, 0))],
        out_shape=[_sds((h, s, HEAD_DIM)), _sds((h, s, HEAD_DIM)), _sds((h, s, 1))],
        scratch_shapes=[pltpu.VMEM((tk, tq), f32)],
        compiler_params=_params("parallel", "arbitrary"),
    )(qt_blocks, k_aug, v, qh, do, dot_blocks, lse, delta)


def _split3(c):
    hi = lax.reduce_precision(c, 8, 7)
    r = c - hi
    mid = lax.reduce_precision(r, 8, 7)
    lo = lax.reduce_precision(r - mid, 8, 7)
    return hi.astype(bf16), mid.astype(bf16), lo.astype(bf16)


def _attn_operands(q, k, v, c6):
    s = q.shape[0]
    tq, tk, _ = _att_tiles(s)
    qh = _heads(q * ATT_SCALE).astype(bf16)
    kh = _heads(k).astype(bf16)
    vh = _heads(v).astype(bf16)
    parts = _split3(c6)
    ones = jnp.ones((N_HEADS, s, 3), bf16)
    cpos = jnp.stack(parts, axis=-1)
    pad = jnp.zeros((N_HEADS, s, ATT_FEAT - HEAD_DIM - 6), bf16)
    q_aug = jnp.concatenate([qh, cpos, ones, pad], axis=-1)
    k_aug = jnp.concatenate([kh, ones, -cpos, pad], axis=-1)

    def blocks_t(a, t):
        return a.reshape(N_HEADS, s // t, t, a.shape[-1]).transpose(0, 1, 3, 2)

    return dict(qt_aug=q_aug.transpose(0, 2, 1), qt_blocks=blocks_t(q_aug, tq), k_aug=k_aug, v=vh,
                vt=blocks_t(vh, tk), kt=blocks_t(kh, tk), qh=qh)


def _s5_disc_fn(are, aim, ldt):
    dt = jnp.exp(ldt)
    er = jnp.exp(are * dt)
    br = er * jnp.cos(aim * dt)
    bi = er * jnp.sin(aim * dt)
    nr = br - 1.0
    den = are * are + aim * aim
    return br, bi, (nr * are + bi * aim) / den, (bi * are - nr * aim) / den


def _s5_disc(name, are, aim, ldt):
    def body(a_ref, b_ref, c_ref, o0, o1, o2, o3):
        r = _s5_disc_fn(a_ref[...], b_ref[...], c_ref[...])
        o0[...], o1[...], o2[...], o3[...] = r

    shp = _sds((S5_GROUPS, S5_STATE))
    return pl.pallas_call(body, name=name, out_shape=[shp] * 4)(are, aim, ldt)


def _s5_disc_bwd(name, are, aim, ldt, cts):
    def body(a_ref, b_ref, c_ref, d0, d1, d2, d3, o0, o1, o2):
        _, vjp = jax.vjp(_s5_disc_fn, a_ref[...], b_ref[...], c_ref[...])
        o0[...], o1[...], o2[...] = vjp((d0[...], d1[...], d2[...], d3[...]))

    shp = _sds((S5_GROUPS, S5_STATE))
    return pl.pallas_call(body, name=name, out_shape=[shp, shp, _sds((S5_GROUPS, 1))])(are, aim, ldt, *cts)


def _adamw_rows(w, g, m, v):
    m = ADAM_B1 * m + (1.0 - ADAM_B1) * g
    v = ADAM_B2 * v + (1.0 - ADAM_B2) * (g * g)
    m_hat = m / (1.0 - ADAM_B1 ** ADAM_STEP)
    v_hat = v / (1.0 - ADAM_B2 ** ADAM_STEP)
    return -ADAM_LR * (m_hat / (jnp.sqrt(v_hat) + ADAM_EPS) + ADAM_WD * w), m, v


def _adamw(name, w, ga, gb, m, v):
    rows, cols = w.shape
    tr = _tile(rows, 256) if rows % 128 == 0 else rows

    def body(w_ref, ga_ref, gb_ref, m_ref, v_ref, g_out, d_out, m_out, v_out):
        g = ga_ref[...] + gb_ref[...]
        d, mm, vv = _adamw_rows(w_ref[...], g, m_ref[...], v_ref[...])
        g_out[...] = g
        d_out[...] = d
        m_out[...] = mm
        v_out[...] = vv

    spec = pl.BlockSpec((tr, cols), lambda i: (i, 0))
    return pl.pallas_call(
        body, name=name, grid=(rows // tr,), in_specs=[spec] * 5, out_specs=[spec] * 4,
        out_shape=[_sds((rows, cols))] * 4, compiler_params=_params("parallel"),
    )(w, ga, gb, m, v)


def _sum_stack(name, st):
    n, rows, cols = st.shape
    tr = _tile(rows, 256) if rows % 128 == 0 else rows

    def body(s_ref, o_ref):
        acc = s_ref[0].astype(f32)
        for j in range(1, n):
            acc = acc + s_ref[j].astype(f32)
        o_ref[...] = acc

    return pl.pallas_call(
        body, name=name, grid=(rows // tr,), in_specs=[pl.BlockSpec((n, tr, cols), lambda i: (0, i, 0))],
        out_specs=pl.BlockSpec((tr, cols), lambda i: (i, 0)), out_shape=_sds((rows, cols)),
        compiler_params=_params("parallel"),
    )(st)


def _block_diag(w):
    h, n, m = w.shape
    return jnp.einsum("hij,hg->higj", w, jnp.eye(h, dtype=w.dtype)).reshape(h * n, h * m)


def _block_diag_part(dense, h):
    n, m = dense.shape[0] // h, dense.shape[1] // h
    return jnp.einsum("higj,hg->hij", dense.reshape(h, n, h, m), jnp.eye(h, dtype=dense.dtype))


def _s5_matrices(coef_re, coef_im, b_re, b_im, c_re, c_im):
    bb_re = coef_re[:, :, None] * b_re - coef_im[:, :, None] * b_im
    bb_im = coef_re[:, :, None] * b_im + coef_im[:, :, None] * b_re
    wb_re = _block_diag(jnp.swapaxes(bb_re, 1, 2))
    wb_im = _block_diag(jnp.swapaxes(bb_im, 1, 2))
    wc_re = _block_diag(jnp.swapaxes(c_re, 1, 2))
    wc_im = _block_diag(jnp.swapaxes(-c_im, 1, 2))
    return wb_re, wb_im, wc_re, wc_im


def _heads(t):
    s = t.shape[0]
    return t.reshape(s, N_HEADS, HEAD_DIM).transpose(1, 0, 2)


def _unheads(t):
    s = t.shape[1]
    return t.transpose(1, 0, 2).reshape(s, N_HEADS * HEAD_DIM)


def _shift_down(t):
    return jnp.concatenate([jnp.zeros((1, t.shape[1]), t.dtype), t[:-1]], axis=0)


def _shift_up(t):
    return jnp.concatenate([t[1:], jnp.zeros((1, t.shape[1]), t.dtype)], axis=0)


def _row(v):
    return v.reshape(1, -1)


def _ffn_fwd(tag, h, wg, wu, wd, gamma, beta):
    g, u, act = _ffn_up(tag + "_up", h, wg, wu)
    r, out = _mm_ln(tag + "_down", act, wd, h, gamma, beta, 0.5)
    return out, dict(h=h, g=g, u=u, act=act, r=r)


def _ffn_bwd(tag, dout, sv, wg, wu, wd, gamma):
    s = dout.shape[0]
    dr, dgam, dbet = _ln_bwd(tag + "_lnb", sv["r"], dout, gamma)
    dwd = _mm_plain(tag + "_dwd", "tn", sv["act"], dr, (D_FF, D_MODEL, s), scale=0.5,
                    tiles=(_tile(D_FF, 1408), 1024, _tile(s, 1024)))
    dg, du = _ffn_dact(tag + "_dact", dr, wd, sv["g"], sv["u"])
    dwg, dwu = _mm2(tag + "_dwgu", "tn", (D_MODEL, D_FF, s), sv["h"], dg, None, du, separate=True,
                    tiles=(512, _tile(D_FF, 1408), _tile(s, 1024)))
    dh = _mm2(tag + "_dh", "nt", (s, D_MODEL, D_FF), dg, wg, du, wu, add=dr, add_coef=ALPHA,
              tiles=(_tile(s, 512), 1024, _tile(D_FF, 1408)))[0]
    return dh, dwg, dwu, dwd, dgam, dbet


def _mixer_fwd(tag, h1, w):
    s = h1.shape[0]
    nt = s // ATT_TILE
    z = _mm_plain(tag + "_win", "nn", h1, w["w_in"], (s, N_IN_P, D_MODEL), tiles=(_tile(s, 512), 768, D_MODEL))
    ax, ag = z[:, :D_A], z[:, D_A:2 * D_A]
    q, k, v = z[:, 2 * D_A:3 * D_A], z[:, 3 * D_A:4 * D_A], z[:, 4 * D_A:5 * D_A]
    f, cu = z[:, F_OFF:F_OFF + 128], z[:, CU_OFF:]
    xa = _conv_fwd(tag + "_conv", ax, w["conv_w"], w["conv_b"])
    a, gated = _rg_gates(tag + "_gates", xa, w["rg_wa"], w["rg_wx"], w["rg_ba"], w["rg_bx"], w["rg_lam"])
    ha = _lin_scan(tag + "_rgscan", a, gated, False)
    ones = jnp.ones((s, 128), f32)
    c = _lin_scan(tag + "_cumf", ones, _log_f(tag + "_logf", f, w["fox_bf"]), False)
    att = _attn_operands(q, k, v, c[:, :N_HEADS].T)
    ot, lse = _attn_fwd_t(tag + "_attn", att["qt_aug"], att["k_aug"], att["vt"])
    ob = ot.reshape(D_B, s).T
    bu_re, bu_im = _mm2(tag + "_s5in", "nn", (s, S5_LANES, D_C), cu, w["wb_re"], None, w["wb_im"], separate=True,
                        tiles=(_tile(s, 512), 1024, D_C))
    hre, him = _s5_scan(tag + "_s5scan", bu_re, bu_im, w["abar_re"], w["abar_im"], False)
    o = _mix_out(tag + "_mixout", ag, ha, ob, hre, him, cu, w["s5_d"], w["mix_g"], w["wc_re"], w["wc_im"], w["w_glu"])
    sv = dict(h1=h1, ax=ax, ag=ag, f=f, cu=cu, xa=xa, a=a, ha=ha, att=att, ot=ot, lse=lse, ob=ob, hre=hre, him=him, o=o)
    return o, sv


def _mixer_bwd(tag, do, dr2, sv, w):
    s = do.shape[0]
    (dag, dha, dob, dhre, dhim, dcu1, dwcr, dwci, dwglu, dd, dgn) = _mix_out_bwd(
        tag + "_mixoutb", do, sv["ag"], sv["ha"], sv["ob"], sv["hre"], sv["him"], sv["cu"], w["s5_d"], w["mix_g"],
        w["wc_re"], w["wc_im"], w["w_glu"])
    gre, gim = _s5_scan(tag + "_s5scanb", dhre, dhim, w["abar_re"], -w["abar_im"], True)
    dab_re, dab_im = _s5_decay_grad(tag + "_s5dec", _shift_down(sv["hre"]), _shift_down(sv["him"]), gre, gim)
    dwb_re, dwb_im = _mm2(tag + "_s5dwb", "tn", (D_C, S5_LANES, s), sv["cu"], gre, None, gim, separate=True,
                          tiles=(D_C, 1024, _tile(s, 1024)))
    dcu = _mm2(tag + "_s5dcu", "nt", (s, D_C, S5_LANES), gre, w["wb_re"], gim, w["wb_im"], add=dcu1,
               tiles=(_tile(s, 512), D_C, 1024))[0]
    att = sv["att"]
    tq = _att_tiles(s)[0]
    nt = s // tq
    dot_ = dob.T.reshape(N_HEADS, HEAD_DIM, s)
    dqt, delta = _attn_bwd_dq_t(tag + "_attndq", att["qt_aug"], att["k_aug"], att["v"], att["kt"], sv["ot"], dot_,
                                sv["lse"])
    dot_blocks = dot_.astype(bf16).reshape(N_HEADS, HEAD_DIM, nt, tq).transpose(0, 2, 1, 3)
    dkh, dvh, dck = _attn_bwd_dkv_t(tag + "_attndkv", att["qt_blocks"], att["k_aug"], att["v"], att["qh"],
                                    _heads(dob).astype(bf16), dot_blocks, sv["lse"].reshape(N_HEADS, nt, 1, tq),
                                    delta.reshape(N_HEADS, nt, 1, tq))
    dq, dk, dv = dqt.reshape(D_B, s).T, _unheads(dkh), _unheads(dvh)
    dc = jnp.pad(dck[:, :, 0].T, ((0, 0), (0, 128 - N_HEADS)))
    dlf = _lin_scan(tag + "_cumfb", jnp.ones((s, 128), f32), dc, True)
    df, dbf = _log_f_bwd(tag + "_logfb", dlf, sv["f"], w["fox_bf"])
    ga = _lin_scan(tag + "_rgscanb", _shift_up(sv["a"]), dha, True)
    dxa, dwa, dwx, dba, dbx, dlam = _rg_gates_bwd(tag + "_gatesb", sv["xa"], ga, _shift_down(sv["ha"]), w["rg_wa"],
                                                  w["rg_wx"], w["rg_ba"], w["rg_bx"], w["rg_lam"])
    dax, dconv = _conv_bwd(tag + "_convb", dxa, sv["ax"], w["conv_w"])
    dz = jnp.concatenate([dax, dag, dq, dk, dv, df, dcu], axis=-1).astype(bf16)
    dwin = _mm_plain(tag + "_dwin", "tn", sv["h1"], dz, (D_MODEL, N_IN_P, s), tiles=(512, 768, _tile(s, 1024)))
    dh1 = _mm_plain(tag + "_dh1", "nt", dz, w["w_in"], (s, D_MODEL, N_IN_P), add=dr2, add_coef=ALPHA,
                    tiles=(_tile(s, 512), 1024, 768))
    grads = dict(dwin=dwin, dwglu=dwglu, dconv=dconv, dwa=dwa, dwx=dwx, dba=dba, dbx=dbx, dlam=dlam, dbf=dbf,
                 dab_re=dab_re, dab_im=dab_im, dwb_re=dwb_re, dwb_im=dwb_im, dwcr=dwcr, dwci=dwci, dd=dd, dgn=dgn)
    return dh1, grads


SMALL_NAMES = ["ln1_g", "ln1_b", "conv_w", "conv_b", "rg_w_a", "rg_b_a", "rg_w_x", "rg_b_x", "rg_lambda", "fox_b_f",
               "s5_a_re", "s5_a_im", "s5_log_dt", "s5_b_re", "s5_b_im", "s5_c_re", "s5_c_im", "s5_d", "mix_norm_g",
               "ln2_g", "ln2_b", "ln3_g", "ln3_b"]
BIG_NAMES = ["ffn1_w_gate", "ffn1_w_up", "ffn1_w_down", "w_in", "s5_w_glu", "w_out", "ffn2_w_gate", "ffn2_w_up",
             "ffn2_w_down"]


def _local_step(x, target, big, small):
    h = x
    saved = []
    for l in range(DEPTH):
        sm = {n: small[n][l] for n in SMALL_NAMES}
        abar_re, abar_im, coef_re, coef_im = _s5_disc(f"l{l}_s5disc", sm["s5_a_re"], sm["s5_a_im"],
                                                      sm["s5_log_dt"].reshape(S5_GROUPS, 1))
        mats, mats_vjp = jax.vjp(_s5_matrices, coef_re, coef_im, sm["s5_b_re"], sm["s5_b_im"], sm["s5_c_re"],
                                 sm["s5_c_im"])
        w = dict(
            w_in=big["w_in"][l], w_glu=big["s5_w_glu"][l], conv_w=sm["conv_w"], conv_b=_row(sm["conv_b"]),
            rg_wa=_block_diag(sm["rg_w_a"]).astype(bf16), rg_wx=_block_diag(sm["rg_w_x"]).astype(bf16),
            rg_ba=_row(sm["rg_b_a"]), rg_bx=_row(sm["rg_b_x"]), rg_lam=_row(sm["rg_lambda"]),
            fox_bf=jnp.pad(_row(sm["fox_b_f"]), ((0, 0), (0, 128 - N_HEADS))),
            abar_re=_row(abar_re), abar_im=_row(abar_im),
            wb_re=mats[0].astype(bf16), wb_im=mats[1].astype(bf16), wc_re=mats[2].astype(bf16),
            wc_im=mats[3].astype(bf16), s5_d=_row(sm["s5_d"]), mix_g=_row(sm["mix_norm_g"]))
        h1, sv1 = _ffn_fwd(f"l{l}_ffn1", h, big["ffn1_w_gate"][l], big["ffn1_w_up"][l], big["ffn1_w_down"][l],
                           _row(sm["ln1_g"]), _row(sm["ln1_b"]))
        o, svm = _mixer_fwd(f"l{l}_mix", h1, w)
        r2, h2 = _mm_ln(f"l{l}_wout", o, big["w_out"][l], h1, _row(sm["ln2_g"]), _row(sm["ln2_b"]), 1.0)
        h3, sv2 = _ffn_fwd(f"l{l}_ffn2", h2, big["ffn2_w_gate"][l], big["ffn2_w_up"][l], big["ffn2_w_down"][l],
                           _row(sm["ln3_g"]), _row(sm["ln3_b"]))
        saved.append(dict(sm=sm, w=w, sv1=sv1, svm=svm, r2=r2, sv2=sv2, mats_vjp=mats_vjp))
        h = h3

    dh, loss_row = _loss_head("loss_head", h, target)
    s = x.shape[0]
    gbig = {n: [None] * DEPTH for n in BIG_NAMES}
    gsmall = {n: [None] * DEPTH for n in SMALL_NAMES}
    for l in reversed(range(DEPTH)):
        sd = saved[l]
        sm, w = sd["sm"], sd["w"]
        dh2, dwg, dwu, dwd, dgam, dbet = _ffn_bwd(f"l{l}_ffn2", dh, sd["sv2"], big["ffn2_w_gate"][l],
                                                 big["ffn2_w_up"][l], big["ffn2_w_down"][l], _row(sm["ln3_g"]))
        gbig["ffn2_w_gate"][l], gbig["ffn2_w_up"][l], gbig["ffn2_w_down"][l] = dwg, dwu, dwd
        gsmall["ln3_g"][l], gsmall["ln3_b"][l] = dgam[0], dbet[0]
        dr2, dgam, dbet = _ln_bwd(f"l{l}_ln2b", sd["r2"], dh2, _row(sm["ln2_g"]))
        gsmall["ln2_g"][l], gsmall["ln2_b"][l] = dgam[0], dbet[0]
        gbig["w_out"][l] = _mm_plain(f"l{l}_dwout", "tn", sd["svm"]["o"], dr2, (D_MODEL, D_MODEL, s))
        do = _mm_plain(f"l{l}_do", "nt", dr2, big["w_out"][l], (s, D_MODEL, D_MODEL))
        dh1, g = _mixer_bwd(f"l{l}_mix", do, dr2, sd["svm"], w)
        gbig["w_in"][l], gbig["s5_w_glu"][l] = g["dwin"], g["dwglu"]
        gsmall["conv_w"][l], gsmall["conv_b"][l] = g["dconv"][:CONV_WIDTH], g["dconv"][CONV_WIDTH]
        gsmall["rg_w_a"][l] = _block_diag_part(g["dwa"], N_HEADS)
        gsmall["rg_w_x"][l] = _block_diag_part(g["dwx"], N_HEADS)
        gsmall["rg_b_a"][l], gsmall["rg_b_x"][l], gsmall["rg_lambda"][l] = g["dba"][0], g["dbx"][0], g["dlam"][0]
        gsmall["fox_b_f"][l] = g["dbf"][0, :N_HEADS]
        dcoef_re, dcoef_im, db_re, db_im, dc_re, dc_im = sd["mats_vjp"]((g["dwb_re"], g["dwb_im"], g["dwcr"], g["dwci"]))
        da_re, da_im, dldt = _s5_disc_bwd(
            f"l{l}_s5discb", sm["s5_a_re"], sm["s5_a_im"], sm["s5_log_dt"].reshape(S5_GROUPS, 1),
            (g["dab_re"].reshape(S5_GROUPS, S5_STATE), g["dab_im"].reshape(S5_GROUPS, S5_STATE), dcoef_re, dcoef_im))
        gsmall["s5_a_re"][l], gsmall["s5_a_im"][l], gsmall["s5_log_dt"][l] = da_re, da_im, dldt[:, 0]
        gsmall["s5_b_re"][l], gsmall["s5_b_im"][l], gsmall["s5_c_re"][l], gsmall["s5_c_im"][l] = db_re, db_im, dc_re, dc_im
        gsmall["s5_d"][l], gsmall["mix_norm_g"][l] = g["dd"][0], g["dgn"][0]
        dh, dwg, dwu, dwd, dgam, dbet = _ffn_bwd(f"l{l}_ffn1", dh1, sd["sv1"], big["ffn1_w_gate"][l],
                                                big["ffn1_w_up"][l], big["ffn1_w_down"][l], _row(sm["ln1_g"]))
        gbig["ffn1_w_gate"][l], gbig["ffn1_w_up"][l], gbig["ffn1_w_down"][l] = dwg, dwu, dwd
        gsmall["ln1_g"][l], gsmall["ln1_b"][l] = dgam[0], dbet[0]
    gbig = {n: jnp.stack(v) for n, v in gbig.items()}
    gsmall = {n: jnp.stack(v) for n, v in gsmall.items()}
    return loss_row[0, 0], dh, gbig, gsmall


def _position():
    return lax.axis_index("x"), lax.axis_index("y"), lax.axis_index("c")


_ANY = pl.BlockSpec(memory_space=pl.ANY)


def _chip_gather(name, shards):
    n = len(shards)

    def body(*refs):
        in_refs, out_refs = refs[:n], refs[n:2 * n]
        send_sems, recv_sems, local_sems = refs[2 * n:]
        x, y, c = _position()
        me = 2 * x + y
        peers = [(1 - x, y), (x, 1 - y), (1 - x, 1 - y)]
        local = [pltpu.make_async_copy(in_refs[i], out_refs[i].at[me], local_sems.at[i]) for i in range(n)]
        for cp in local:
            cp.start()
        sends = []
        for i in range(n):
            for r, (px, py) in enumerate(peers):
                cp = pltpu.make_async_remote_copy(
                    src_ref=in_refs[i], dst_ref=out_refs[i].at[me], send_sem=send_sems.at[3 * i + r],
                    recv_sem=recv_sems.at[3 * i + r], device_id=(px, py, c), device_id_type=MESH)
                cp.start()
                sends.append(cp)
        for i in range(n):
            for r, (px, py) in enumerate(peers):
                pltpu.make_async_remote_copy(
                    src_ref=in_refs[i], dst_ref=out_refs[i].at[2 * px + py], send_sem=send_sems.at[3 * i + r],
                    recv_sem=recv_sems.at[3 * i + r], device_id=(px, py, c), device_id_type=MESH).wait_recv()
        for cp in sends:
            cp.wait_send()
        for cp in local:
            cp.wait()

    return pl.pallas_call(
        body, name=name, in_specs=[_ANY] * n, out_specs=[_ANY] * n,
        out_shape=[_sds((N_CHIPS,) + a.shape, a.dtype) for a in shards],
        scratch_shapes=[pltpu.SemaphoreType.DMA((3 * n,)), pltpu.SemaphoreType.DMA((3 * n,)),
                        pltpu.SemaphoreType.DMA((n,))],
    )(*shards)


def _chip_scatter(name, stacks):
    n = len(stacks)

    def body(*refs):
        in_refs, out_refs = refs[:n], refs[n:2 * n]
        send_sems, recv_sems, local_sems = refs[2 * n:]
        x, y, c = _position()
        me = 2 * x + y
        peers = [(1 - x, y), (x, 1 - y), (1 - x, 1 - y)]
        local = [pltpu.make_async_copy(in_refs[i].at[me], out_refs[i].at[me], local_sems.at[i]) for i in range(n)]
        for cp in local:
            cp.start()
        sends = []
        for i in range(n):
            for r, (px, py) in enumerate(peers):
                cp = pltpu.make_async_remote_copy(
                    src_ref=in_refs[i].at[2 * px + py], dst_ref=out_refs[i].at[me], send_sem=send_sems.at[3 * i + r],
                    recv_sem=recv_sems.at[3 * i + r], device_id=(px, py, c), device_id_type=MESH)
                cp.start()
                sends.append(cp)
        for i in range(n):
            for r, (px, py) in enumerate(peers):
                pltpu.make_async_remote_copy(
                    src_ref=in_refs[i].at[me], dst_ref=out_refs[i].at[2 * px + py], send_sem=send_sems.at[3 * i + r],
                    recv_sem=recv_sems.at[3 * i + r], device_id=(px, py, c), device_id_type=MESH).wait_recv()
        for cp in sends:
            cp.wait_send()
        for cp in local:
            cp.wait()

    return pl.pallas_call(
        body, name=name, in_specs=[_ANY] * n, out_specs=[_ANY] * n,
        out_shape=[_sds(a.shape, a.dtype) for a in stacks],
        scratch_shapes=[pltpu.SemaphoreType.DMA((3 * n,)), pltpu.SemaphoreType.DMA((3 * n,)),
                        pltpu.SemaphoreType.DMA((n,))],
    )(*stacks)


def _sibling_swap(name, arrs):
    n = len(arrs)

    def body(*refs):
        in_refs, out_refs = refs[:n], refs[n:2 * n]
        send_sems, recv_sems = refs[2 * n:]
        x, y, c = _position()
        copies = [pltpu.make_async_remote_copy(
            src_ref=in_refs[i], dst_ref=out_refs[i], send_sem=send_sems.at[i], recv_sem=recv_sems.at[i],
            device_id=(x, y, 1 - c), device_id_type=MESH) for i in range(n)]
        for cp in copies:
            cp.start()
        for cp in copies:
            cp.wait_recv()
        for cp in copies:
            cp.wait_send()

    return pl.pallas_call(
        body, name=name, in_specs=[_ANY] * n, out_specs=[_ANY] * n,
        out_shape=[_sds(a.shape, a.dtype) for a in arrs],
        scratch_shapes=[pltpu.SemaphoreType.DMA((n,)), pltpu.SemaphoreType.DMA((n,))],
    )(*arrs)


def _dev_gather(name, arr):
    def body(in_ref, out_ref, send_sems, recv_sems, local_sem):
        x, y, c = _position()
        me = 4 * x + 2 * y + c
        local = pltpu.make_async_copy(in_ref, out_ref.at[me], local_sem)
        local.start()
        peers = []
        for k in range(1, N_DEV):
            peers.append((1 - x if k & 4 else x, 1 - y if k & 2 else y, 1 - c if k & 1 else c))
        sends = []
        for k, peer in enumerate(peers):
            cp = pltpu.make_async_remote_copy(src_ref=in_ref, dst_ref=out_ref.at[me], send_sem=send_sems.at[k],
                                              recv_sem=recv_sems.at[k], device_id=peer, device_id_type=MESH)
            cp.start()
            sends.append(cp)
        for k, (px, py, pc) in enumerate(peers):
            pltpu.make_async_remote_copy(src_ref=in_ref, dst_ref=out_ref.at[4 * px + 2 * py + pc],
                                         send_sem=send_sems.at[k], recv_sem=recv_sems.at[k], device_id=(px, py, pc),
                                         device_id_type=MESH).wait_recv()
        for cp in sends:
            cp.wait_send()
        local.wait()

    return pl.pallas_call(
        body, name=name, in_specs=[_ANY], out_specs=_ANY, out_shape=_sds((N_DEV,) + arr.shape, arr.dtype),
        scratch_shapes=[pltpu.SemaphoreType.DMA((N_DEV - 1,)), pltpu.SemaphoreType.DMA((N_DEV - 1,)),
                        pltpu.SemaphoreType.DMA],
    )(arr)


COLUMN_SHARDED = ("ffn1_w_gate", "ffn1_w_up", "ffn2_w_gate", "ffn2_w_up")
PACK_QUANTUM = 128 * 256


def _permute_in_cols(w):
    pad = jnp.zeros(w.shape[:-1] + (128 - N_HEADS,), w.dtype)
    return jnp.concatenate([w[..., :F_OFF + N_HEADS], pad, w[..., F_OFF + N_HEADS:]], axis=-1)


def _unpermute_in_cols(w):
    return jnp.concatenate([w[..., :F_OFF + N_HEADS], w[..., CU_OFF:]], axis=-1)


def _unstack(name, st):
    _, l, r, c = st.shape
    if name in COLUMN_SHARDED:
        return st.transpose(1, 2, 0, 3).reshape(l, r, N_CHIPS * c)
    return st.transpose(1, 0, 2, 3).reshape(l, N_CHIPS * r, c)


def _restack(name, g):
    l, r, c = g.shape
    if name in COLUMN_SHARDED:
        return g.reshape(l, r, N_CHIPS, c // N_CHIPS).transpose(2, 0, 1, 3)
    return g.reshape(l, N_CHIPS, r // N_CHIPS, c).transpose(1, 0, 2, 3)


def _pack(arrs):
    flat = jnp.concatenate([a.reshape(-1) for a in arrs])
    pad = -flat.shape[0] % PACK_QUANTUM
    return jnp.pad(flat, (0, pad)).reshape(-1, 128)


def _unpack(buf, shapes):
    flat = buf.reshape(-1)
    out, off = [], 0
    for shp in shapes:
        size = math.prod(shp)
        out.append(flat[off:off + size].reshape(shp))
        off += size
    return out


WEIGHT_NAMES = ["ffn1_w_gate", "ffn1_w_up", "ffn1_w_down", "ln1_g", "ln1_b", "w_in", "conv_w", "conv_b", "rg_w_a",
                "rg_b_a", "rg_w_x", "rg_b_x", "rg_lambda", "fox_b_f", "s5_a_re", "s5_a_im", "s5_log_dt", "s5_b_re",
                "s5_b_im", "s5_c_re", "s5_c_im", "s5_d", "s5_w_glu", "mix_norm_g", "w_out", "ln2_g", "ln2_b",
                "ffn2_w_gate", "ffn2_w_up", "ffn2_w_down", "ln3_g", "ln3_b"]


def _train_step(x, loss_target, w, m, v):
    ix, iy, _ = _position()
    chip = 2 * ix + iy

    shards = [(_permute_in_cols(w[n]) if n == "w_in" else w[n]).astype(bf16) for n in BIG_NAMES]
    stacks = _chip_gather("gather_weights", shards + [w["conv_w"]])
    big = {n: _unstack(n, st) for n, st in zip(BIG_NAMES, stacks)}
    small = {n: w[n] for n in SMALL_NAMES}
    small["conv_w"] = stacks[-1].transpose(1, 2, 0, 3).reshape(DEPTH, CONV_WIDTH, D_A)

    loss_local, gx, gbig, gsmall = _local_step(x[0], loss_target[0], big, small)

    sent = [_restack(n, gbig[n]).astype(bf16) for n in BIG_NAMES]
    recv = _chip_scatter("scatter_grads", sent)
    partial = {}
    for n, st in zip(BIG_NAMES, recv):
        _, l, r, c = st.shape
        p = _sum_stack("sum_" + n, st.reshape(N_CHIPS, l * r, c))
        partial[n] = _unpermute_in_cols(p) if n == "w_in" else p
    other = dict(zip(BIG_NAMES, _sibling_swap("swap_grads", [partial[n] for n in BIG_NAMES])))

    small_shapes = [gsmall[n].shape for n in SMALL_NAMES]
    total = _sum_stack("sum_small", _dev_gather("gather_small", _pack([gsmall[n] for n in SMALL_NAMES])))
    gsm = dict(zip(SMALL_NAMES, _unpack(total, small_shapes)))
    cw = D_A // N_CHIPS
    gsm["conv_w"] = lax.dynamic_slice_in_dim(gsm["conv_w"], chip * cw, cw, axis=2)

    grads, deltas, new_m, new_v = {}, {}, {}, {}
    for n in BIG_NAMES:
        shp = w[n].shape
        two_d = (shp[0] * shp[1], shp[2])
        g, d, mm, vv = _adamw("adamw_" + n, w[n].reshape(two_d), partial[n], other[n], m[n].reshape(two_d),
                              v[n].reshape(two_d))
        grads[n], deltas[n], new_m[n], new_v[n] = (t.reshape(shp) for t in (g, d, mm, vv))
    shapes = [w[n].shape for n in SMALL_NAMES]
    gp = _pack([gsm[n] for n in SMALL_NAMES])
    res = _adamw("adamw_small", _pack([w[n] for n in SMALL_NAMES]), gp, jnp.zeros_like(gp),
                 _pack([m[n] for n in SMALL_NAMES]), _pack([v[n] for n in SMALL_NAMES]))
    for dst, buf in zip((grads, deltas, new_m, new_v), res):
        dst.update(zip(SMALL_NAMES, _unpack(buf, shapes)))

    loss = lax.psum(loss_local, ("x", "y", "c"))
    return (loss, gx[None], *[grads[n] for n in WEIGHT_NAMES], *[deltas[n] for n in WEIGHT_NAMES],
            *[new_m[n] for n in WEIGHT_NAMES], *[new_v[n] for n in WEIGHT_NAMES])


def kernel(x, ffn1_w_gate, ffn1_w_up, ffn1_w_down, ln1_g, ln1_b, w_in, conv_w, conv_b, rg_w_a, rg_b_a, rg_w_x, rg_b_x, rg_lambda, fox_b_f, s5_a_re, s5_a_im, s5_log_dt, s5_b_re, s5_b_im, s5_c_re, s5_c_im, s5_d, s5_w_glu, mix_norm_g, w_out, ln2_g, ln2_b, ffn2_w_gate, ffn2_w_up, ffn2_w_down, ln3_g, ln3_b, loss_target, m_ffn1_w_gate, m_ffn1_w_up, m_ffn1_w_down, m_ln1_g, m_ln1_b, m_w_in, m_conv_w, m_conv_b, m_rg_w_a, m_rg_b_a, m_rg_w_x, m_rg_b_x, m_rg_lambda, m_fox_b_f, m_s5_a_re, m_s5_a_im, m_s5_log_dt, m_s5_b_re, m_s5_b_im, m_s5_c_re, m_s5_c_im, m_s5_d, m_s5_w_glu, m_mix_norm_g, m_w_out, m_ln2_g, m_ln2_b, m_ffn2_w_gate, m_ffn2_w_up, m_ffn2_w_down, m_ln3_g, m_ln3_b, v_ffn1_w_gate, v_ffn1_w_up, v_ffn1_w_down, v_ln1_g, v_ln1_b, v_w_in, v_conv_w, v_conv_b, v_rg_w_a, v_rg_b_a, v_rg_w_x, v_rg_b_x, v_rg_lambda, v_fox_b_f, v_s5_a_re, v_s5_a_im, v_s5_log_dt, v_s5_b_re, v_s5_b_im, v_s5_c_re, v_s5_c_im, v_s5_d, v_s5_w_glu, v_mix_norm_g, v_w_out, v_ln2_g, v_ln2_b, v_ffn2_w_gate, v_ffn2_w_up, v_ffn2_w_down, v_ln3_g, v_ln3_b):
    args = dict(locals())
    w = {n: args[n] for n in WEIGHT_NAMES}
    m = {n: args["m_" + n] for n in WEIGHT_NAMES}
    v = {n: args["v_" + n] for n in WEIGHT_NAMES}
    return _train_step(x, loss_target, w, m, v)
```

```python
import functools
import math

import jax
import jax.numpy as jnp
from jax import lax
from jax.experimental import pallas as pl
from jax.experimental.pallas import tpu as pltpu

f32 = jnp.float32
bf16 = jnp.bfloat16

D_MODEL = 1024
D_FF = 2816
D_A = 384
D_B = 384
D_C = 256
N_HEADS = 6
HEAD_DIM = 64
S5_GROUPS = 16
S5_GROUP = 16
S5_STATE = 64
S5_LANES = S5_GROUPS * S5_STATE
N_IN = 2 * D_A + 3 * D_B + N_HEADS + D_C
F_OFF = 5 * D_A
CU_OFF = F_OFF + 128
N_IN_P = CU_OFF + D_C
CONV_WIDTH = 4
DEPTH = 2
ALPHA = (2 * DEPTH) ** 0.25
LN_EPS = 1e-5
RMS_EPS = 1e-6
RG_C = 8.0
ATT_SCALE = HEAD_DIM ** -0.5
ADAM_LR, ADAM_B1, ADAM_B2, ADAM_EPS, ADAM_WD, ADAM_STEP = 0.001, 0.9, 0.999, 1e-08, 0.01, 10

SCAN_CHUNK = 64
ROW_TILE = 256
ATT_TILE = 256
N_CHIPS = 4
N_DEV = 8
MESH = pl.DeviceIdType.MESH

_DN = {
    "nn": (((1,), (0,)), ((), ())),
    "nt": (((1,), (1,)), ((), ())),
    "tn": (((0,), (0,)), ((), ())),
}


def _sds(shape, dtype=f32):
    return jax.ShapeDtypeStruct(shape, dtype)


def _tile(n, target):
    best = None
    for t in range(128, min(n, target) + 1, 128):
        if n % t == 0:
            best = t
    return best or n


def _row_tile(rows, target=256):
    best = None
    for t in range(16, min(rows, target) + 1, 16):
        if rows % t == 0:
            best = t
    return best or rows


def _params(*sem):
    return pltpu.CompilerParams(dimension_semantics=sem)


def _mm(name, mode, dims, tiles, a_list, b_list, pairs, n_acc, epilogue, outs, extras=(), vecs=()):
    m, n, k = dims
    tm, tn, tk = tiles
    nk = k // tk
    na, nb, ne, nv, no = len(a_list), len(b_list), len(extras), len(vecs), len(outs)

    def body(*refs):
        a_refs = refs[:na]
        b_refs = refs[na:na + nb]
        e_refs = refs[na + nb:na + nb + ne]
        v_refs = refs[na + nb + ne:na + nb + ne + nv]
        o_refs = refs[na + nb + ne + nv:na + nb + ne + nv + no]
        acc_refs = refs[na + nb + ne + nv + no:]
        kk = pl.program_id(2)

        @pl.when(kk == 0)
        def _():
            for acc in acc_refs:
                acc[...] = jnp.zeros_like(acc)

        a_vals = [r[...].astype(bf16) for r in a_refs]
        b_vals = [r[...].astype(bf16) for r in b_refs]
        for ai, bi, ci in pairs:
            acc_refs[ci][...] += lax.dot_general(a_vals[ai], b_vals[bi], _DN[mode], preferred_element_type=f32)

        @pl.when(kk == nk - 1)
        def _():
            res = epilogue([acc[...] for acc in acc_refs], [e[...] for e in e_refs], [v[...] for v in v_refs])
            for o, r in zip(o_refs, res):
                o[...] = r.astype(o.dtype)

    if mode == "tn":
        a_spec = pl.BlockSpec((tk, tm), lambda i, j, kk: (kk, i))
    else:
        a_spec = pl.BlockSpec((tm, tk), lambda i, j, kk: (i, kk))
    if mode == "nt":
        b_spec = pl.BlockSpec((tn, tk), lambda i, j, kk: (j, kk))
    else:
        b_spec = pl.BlockSpec((tk, tn), lambda i, j, kk: (kk, j))
    o_spec = pl.BlockSpec((tm, tn), lambda i, j, kk: (i, j))
    v_spec = pl.BlockSpec((1, tn), lambda i, j, kk: (0, j))
    res = _call(
        body,
        name=name,
        grid=(m // tm, n // tn, nk),
        in_specs=[a_spec] * na + [b_spec] * nb + [o_spec] * ne + [v_spec] * nv,
        out_specs=[o_spec] * no,
        out_shape=[_sds((m, n), dt) for dt in outs],
        scratch_shapes=[pltpu.VMEM((tm, tn), f32)] * n_acc,
        compiler_params=_params("parallel", "parallel", "arbitrary"),
    )(*a_list, *b_list, *extras, *vecs)
    return res


def _layer_norm_rows(r, gamma, beta):
    mu = jnp.mean(r, axis=-1, keepdims=True)
    xc = r - mu
    var = jnp.mean(xc * xc, axis=-1, keepdims=True)
    return xc * lax.rsqrt(var + LN_EPS) * gamma + beta


def _mm_plain(name, mode, a, b, dims, scale=1.0, out_dtype=f32, add=None, add_coef=1.0, tiles=None):
    m, n, k = dims
    tiles = tiles or (_tile(m, 512), _tile(n, 1024), _tile(k, 1024))

    def epilogue(accs, extras, vecs):
        r = accs[0] if scale == 1.0 else accs[0] * scale
        if extras:
            r = r + add_coef * extras[0]
        return [r]

    return _mm(name, mode, dims, tiles, [a], [b], [(0, 0, 0)], 1, epilogue, [out_dtype],
               extras=[] if add is None else [add])[0]


def _ffn_up(name, h, wg, wu):
    s = h.shape[0]

    def epilogue(accs, extras, vecs):
        g, u = accs
        return [g, u, g * jax.nn.sigmoid(g) * u]

    return _mm(name, "nn", (s, D_FF, D_MODEL), (_tile(s, 512), _tile(D_FF, 1408), D_MODEL), [h], [wg, wu],
               [(0, 0, 0), (0, 1, 1)], 2, epilogue, [f32, f32, bf16])


def _mm_ln(name, a, w, resid, gamma, beta, scale):
    s, k = a.shape

    def epilogue(accs, extras, vecs):
        r = ALPHA * extras[0] + scale * accs[0]
        return [r, _layer_norm_rows(r, vecs[0], vecs[1])]

    return _mm(name, "nn", (s, D_MODEL, k), (_tile(s, 256), D_MODEL, _tile(k, 1408)), [a], [w],
               [(0, 0, 0)], 1, epilogue, [f32, f32], extras=[resid], vecs=[gamma, beta])


def _ffn_dact(name, dr, wd, g, u):
    s = dr.shape[0]

    def epilogue(accs, extras, vecs):
        da = 0.5 * accs[0]
        gg, uu = extras
        sg = jax.nn.sigmoid(gg)
        return [da * uu * (sg * (1.0 + gg * (1.0 - sg))), da * (gg * sg)]

    return _mm(name, "nt", (s, D_FF, D_MODEL), (_tile(s, 512), _tile(D_FF, 1408), D_MODEL), [dr], [wd],
               [(0, 0, 0)], 1, epilogue, [bf16, bf16], extras=[g, u])


def _mm2(name, mode, dims, a0, b0, a1, b1, add=None, add_coef=1.0, separate=False, tiles=None):
    m, n, k = dims
    tiles = tiles or (_tile(m, 512), _tile(n, 1024), _tile(k, 1024))

    def epilogue(accs, extras, vecs):
        if separate:
            return list(accs)
        r = accs[0]
        if extras:
            r = r + add_coef * extras[0]
        return [r]

    a_list = [a0] if a1 is None else [a0, a1]
    b_list = [b0] if b1 is None else [b0, b1]
    pairs = [(0, 0, 0), (len(a_list) - 1, len(b_list) - 1, 1 if separate else 0)]
    return _mm(name, mode, dims, tiles, a_list, b_list, pairs, 2 if separate else 1, epilogue,
               [f32, f32] if separate else [f32], extras=[] if add is None else [add])


def _row_call(name, body, s, ins, params, outs, accs):
    tm = ROW_TILE
    in_specs = [pl.BlockSpec((tm, a.shape[1]), lambda i: (i, 0)) for a in ins]
    in_specs += [pl.BlockSpec(p.shape, lambda i, nd=p.ndim: (0,) * nd) for p in params]
    out_specs = [pl.BlockSpec((tm, o.shape[1]), lambda i: (i, 0)) for o in outs]
    out_specs += [pl.BlockSpec(a.shape, lambda i, nd=len(a.shape): (0,) * nd) for a in accs]
    return pl.pallas_call(
        body,
        name=name,
        grid=(s // tm,),
        in_specs=in_specs,
        out_specs=out_specs,
        out_shape=list(outs) + list(accs),
        compiler_params=_params("arbitrary"),
    )(*ins, *params)


def _zero_at_first(refs):
    @pl.when(pl.program_id(0) == 0)
    def _():
        for r in refs:
            r[...] = jnp.zeros_like(r)


def _ln_bwd(name, r, dh, gamma):
    s = r.shape[0]

    def body(r_ref, dh_ref, g_ref, dr_ref, dg_ref, db_ref):
        _zero_at_first([dg_ref, db_ref])
        rr = r_ref[...]
        dy = dh_ref[...]
        mu = jnp.mean(rr, axis=-1, keepdims=True)
        xc = rr - mu
        rstd = lax.rsqrt(jnp.mean(xc * xc, axis=-1, keepdims=True) + LN_EPS)
        xhat = xc * rstd
        dxh = dy * g_ref[...]
        dr_ref[...] = rstd * (dxh - jnp.mean(dxh, axis=-1, keepdims=True)
                              - xhat * jnp.mean(dxh * xhat, axis=-1, keepdims=True))
        dg_ref[...] += jnp.sum(dy * xhat, axis=0, keepdims=True)
        db_ref[...] += jnp.sum(dy, axis=0, keepdims=True)

    return _row_call(name, body, s, [r, dh], [gamma], [_sds((s, D_MODEL))], [_sds((1, D_MODEL)), _sds((1, D_MODEL))])


def _loss_head(name, y, target):
    s = y.shape[0]

    def body(y_ref, t_ref, dy_ref, l_ref):
        _zero_at_first([l_ref])
        e = y_ref[...] - t_ref[...]
        dy_ref[...] = e / D_MODEL
        l_ref[...] += 0.5 * jnp.sum(jnp.mean(e * e, axis=-1, keepdims=True), axis=0, keepdims=True)

    return _row_call(name, body, s, [y, target], [], [_sds((s, D_MODEL))], [_sds((1, 128))])


def _expm1(x):
    series = x * (1.0 + x / 2.0 * (1.0 + x / 3.0 * (1.0 + x / 4.0 * (1.0 + x / 5.0 * (1.0 + x / 6.0 * (1.0 + x / 7.0))))))
    return jnp.where(jnp.abs(x) < 0.25, series, jnp.exp(x) - 1.0)


def _gates_fn(xa, wa, wx, ba, bx, lam, tap_a, tap_x):
    xb = xa.astype(bf16)
    r = jax.nn.sigmoid(jnp.dot(xb, wa, preferred_element_type=f32) + ba + tap_a)
    i = jax.nn.sigmoid(jnp.dot(xb, wx, preferred_element_type=f32) + bx + tap_x)
    log_a = -RG_C * r * jax.nn.softplus(-lam)
    a = jnp.exp(log_a)
    gated = jnp.sqrt(-_expm1(2.0 * log_a)) * (i * xa)
    return a, gated


def _rg_gates(name, xa, wa, wx, ba, bx, lam):
    s = xa.shape[0]

    def body(xa_ref, wa_ref, wx_ref, ba_ref, bx_ref, lam_ref, a_ref, g_ref):
        a, g = _gates_fn(xa_ref[...], wa_ref[...], wx_ref[...], ba_ref[...], bx_ref[...], lam_ref[...], 0.0, 0.0)
        a_ref[...] = a
        g_ref[...] = g

    return _row_call(name, body, s, [xa], [wa, wx, ba, bx, lam], [_sds((s, D_A)), _sds((s, D_A))], [])


def _rg_gates_bwd(name, xa, ga, h_prev, wa, wx, ba, bx, lam):
    s = xa.shape[0]

    def body(xa_ref, ga_ref, hp_ref, wa_ref, wx_ref, ba_ref, bx_ref, lam_ref,
             dxa_ref, dwa_ref, dwx_ref, dba_ref, dbx_ref, dlam_ref):
        _zero_at_first([dwa_ref, dwx_ref, dba_ref, dbx_ref, dlam_ref])
        xa_v = xa_ref[...]
        zero = jnp.zeros((xa_v.shape[0], D_A), f32)
        fn = lambda x, ba_, bx_, lam_, ta, tx: _gates_fn(x, wa_ref[...], wx_ref[...], ba_, bx_, lam_, ta, tx)
        _, vjp = jax.vjp(fn, xa_v, ba_ref[...], bx_ref[...], lam_ref[...], zero, zero)
        gav = ga_ref[...]
        dxa, dba, dbx, dlam, dta, dtx = vjp((gav * hp_ref[...], gav))
        dxa_ref[...] = dxa
        xb = xa_v.astype(bf16)
        dwa_ref[...] += lax.dot_general(xb, dta.astype(bf16), _DN["tn"], preferred_element_type=f32)
        dwx_ref[...] += lax.dot_general(xb, dtx.astype(bf16), _DN["tn"], preferred_element_type=f32)
        dba_ref[...] += dba
        dbx_ref[...] += dbx
        dlam_ref[...] += dlam

    return _row_call(name, body, s, [xa, ga, h_prev], [wa, wx, ba, bx, lam], [_sds((s, D_A))],
                     [_sds((D_A, D_A)), _sds((D_A, D_A)), _sds((1, D_A)), _sds((1, D_A)), _sds((1, D_A))])


def _rms(v, g):
    return v * lax.rsqrt(jnp.mean(v * v, axis=-1, keepdims=True) + RMS_EPS) * g


def _mix_out_fn(ag, ha, ob, hre, him, cu, d, gn, tap_y, tap_gl, wcr, wci, wglu):
    out_a = jax.nn.gelu(ag) * ha
    y = (jnp.dot(hre.astype(bf16), wcr, preferred_element_type=f32)
         + jnp.dot(him.astype(bf16), wci, preferred_element_type=f32) + d * cu + tap_y)
    y2 = jax.nn.gelu(y)
    gl = jnp.dot(y2.astype(bf16), wglu, preferred_element_type=f32) + tap_gl
    out_c = y2 * jax.nn.sigmoid(gl)
    o = jnp.concatenate([_rms(out_a, gn[:, :D_A]), _rms(ob, gn[:, D_A:D_A + D_B]), _rms(out_c, gn[:, D_A + D_B:])],
                        axis=-1)
    return o, y2


def _mix_out(name, ag, ha, ob, hre, him, cu, d, gn, wcr, wci, wglu):
    s = ag.shape[0]

    def body(ag_ref, ha_ref, ob_ref, hre_ref, him_ref, cu_ref, d_ref, gn_ref, wcr_ref, wci_ref, wglu_ref, o_ref):
        o, _ = _mix_out_fn(ag_ref[...], ha_ref[...], ob_ref[...], hre_ref[...], him_ref[...], cu_ref[...], d_ref[...],
                           gn_ref[...], 0.0, 0.0, wcr_ref[...], wci_ref[...], wglu_ref[...])
        o_ref[...] = o.astype(o_ref.dtype)

    return _row_call(name, body, s, [ag, ha, ob, hre, him, cu], [d, gn, wcr, wci, wglu], [_sds((s, D_MODEL), bf16)], [])[0]


def _mix_out_bwd(name, do, ag, ha, ob, hre, him, cu, d, gn, wcr, wci, wglu):
    s = ag.shape[0]

    def body(do_ref, ag_ref, ha_ref, ob_ref, hre_ref, him_ref, cu_ref, d_ref, gn_ref, wcr_ref, wci_ref, wglu_ref,
             dag_ref, dha_ref, dob_ref, dhre_ref, dhim_ref, dcu_ref, dwcr_ref, dwci_ref, dwglu_ref, dd_ref, dgn_ref):
        _zero_at_first([dwcr_ref, dwci_ref, dwglu_ref, dd_ref, dgn_ref])
        tm = ag_ref.shape[0]
        zero = jnp.zeros((tm, D_C), f32)
        hre_v, him_v = hre_ref[...], him_ref[...]
        fn = lambda *a: _mix_out_fn(*a, wcr_ref[...], wci_ref[...], wglu_ref[...])
        _, vjp, y2 = jax.vjp(fn, ag_ref[...], ha_ref[...], ob_ref[...], hre_v, him_v, cu_ref[...], d_ref[...],
                             gn_ref[...], zero, zero, has_aux=True)
        dag, dha, dob, dhre, dhim, dcu, dd, dgn, dy, dgl = vjp(do_ref[...])
        dag_ref[...] = dag
        dha_ref[...] = dha
        dob_ref[...] = dob
        dhre_ref[...] = dhre
        dhim_ref[...] = dhim
        dcu_ref[...] = dcu
        dyb = dy.astype(bf16)
        dwcr_ref[...] += lax.dot_general(hre_v.astype(bf16), dyb, _DN["tn"], preferred_element_type=f32)
        dwci_ref[...] += lax.dot_general(him_v.astype(bf16), dyb, _DN["tn"], preferred_element_type=f32)
        dwglu_ref[...] += lax.dot_general(y2.astype(bf16), dgl.astype(bf16), _DN["tn"], preferred_element_type=f32)
        dd_ref[...] += dd
        dgn_ref[...] += dgn

    outs = [_sds((s, D_A)), _sds((s, D_A)), _sds((s, D_B)), _sds((s, S5_LANES)), _sds((s, S5_LANES)), _sds((s, D_C))]
    accs = [_sds((S5_LANES, D_C)), _sds((S5_LANES, D_C)), _sds((D_C, D_C)), _sds((1, D_C)), _sds((1, D_MODEL))]
    return _row_call(name, body, s, [do, ag, ha, ob, hre, him, cu], [d, gn, wcr, wci, wglu], outs, accs)


def _log_f(name, f, bf):
    s = f.shape[0]

    def body(f_ref, b_ref, o_ref):
        o_ref[...] = jax.nn.log_sigmoid(f_ref[...] + b_ref[...])

    return _row_call(name, body, s, [f], [bf], [_sds((s, 128))], [])[0]


def _log_f_bwd(name, dlf, f, bf):
    s = f.shape[0]

    def body(dl_ref, f_ref, b_ref, df_ref, db_ref):
        _zero_at_first([db_ref])
        df = dl_ref[...] * jax.nn.sigmoid(-(f_ref[...] + b_ref[...]))
        df_ref[...] = df
        db_ref[...] += jnp.sum(df, axis=0, keepdims=True)

    return _row_call(name, body, s, [dlf, f], [bf], [_sds((s, 128))], [_sds((1, 128))])


def _s5_decay_grad(name, hp_re, hp_im, g_re, g_im):
    s = g_re.shape[0]

    def body(hr_ref, hi_ref, gr_ref, gi_ref, dr_ref, di_ref):
        _zero_at_first([dr_ref, di_ref])
        hr, hi, gr, gi = hr_ref[...], hi_ref[...], gr_ref[...], gi_ref[...]
        dr_ref[...] += jnp.sum(hr * gr + hi * gi, axis=0, keepdims=True)
        di_ref[...] += jnp.sum(hr * gi - hi * gr, axis=0, keepdims=True)

    return _row_call(name, body, s, [hp_re, hp_im, g_re, g_im], [], [], [_sds((1, S5_LANES)), _sds((1, S5_LANES))])


def _conv_fwd(name, ax, w, b):
    s = ax.shape[0]
    tm = ROW_TILE

    def body(x_ref, halo_ref, w_ref, b_ref, o_ref):
        i = pl.program_id(0)
        x = x_ref[...]
        halo = jnp.where(i == 0, 0.0, halo_ref[...])
        ext = jnp.concatenate([halo, x], axis=0)
        acc = b_ref[...] + w_ref[3:4, :] * x
        for k in range(CONV_WIDTH - 1):
            acc = acc + w_ref[k:k + 1, :] * pltpu.roll(ext, CONV_WIDTH - 1 - k, 0)[8:, :]
        o_ref[...] = acc

    return pl.pallas_call(
        body,
        name=name,
        grid=(s // tm,),
        in_specs=[pl.BlockSpec((tm, D_A), lambda i: (i, 0)),
                  pl.BlockSpec((8, D_A), lambda i: (jnp.maximum(i * (tm // 8) - 1, 0), 0)),
                  pl.BlockSpec((CONV_WIDTH, D_A), lambda i: (0, 0)),
                  pl.BlockSpec((1, D_A), lambda i: (0, 0))],
        out_specs=pl.BlockSpec((tm, D_A), lambda i: (i, 0)),
        out_shape=_sds((s, D_A)),
        compiler_params=_params("arbitrary"),
    )(ax, ax, w, b)


def _conv_bwd(name, dxa, ax, w):
    s = ax.shape[0]
    tm = ROW_TILE
    nblk = s // tm

    def body(dx_ref, dnext_ref, x_ref, halo_ref, w_ref, dax_ref, dw_ref):
        i = pl.program_id(0)
        _zero_at_first([dw_ref])
        dx = dx_ref[...]
        dnext = jnp.where(i == nblk - 1, 0.0, dnext_ref[...])
        dext = jnp.concatenate([dx, dnext], axis=0)
        x = x_ref[...]
        halo = jnp.where(i == 0, 0.0, halo_ref[...])
        ext = jnp.concatenate([halo, x], axis=0)
        acc = w_ref[3:4, :] * dx
        dw_ref[3:4, :] += jnp.sum(dx * x, axis=0, keepdims=True)
        for k in range(CONV_WIDTH - 1):
            sh = CONV_WIDTH - 1 - k
            acc = acc + w_ref[k:k + 1, :] * pltpu.roll(dext, tm + 8 - sh, 0)[:tm, :]
            dw_ref[k:k + 1, :] += jnp.sum(dx * pltpu.roll(ext, sh, 0)[8:, :], axis=0, keepdims=True)
        dw_ref[4:5, :] += jnp.sum(dx, axis=0, keepdims=True)
        dax_ref[...] = acc

    return pl.pallas_call(
        body,
        name=name,
        grid=(nblk,),
        in_specs=[pl.BlockSpec((tm, D_A), lambda i: (i, 0)),
                  pl.BlockSpec((8, D_A), lambda i: (jnp.minimum((i + 1) * (tm // 8), s // 8 - 1), 0)),
                  pl.BlockSpec((tm, D_A), lambda i: (i, 0)),
                  pl.BlockSpec((8, D_A), lambda i: (jnp.maximum(i * (tm // 8) - 1, 0), 0)),
                  pl.BlockSpec((CONV_WIDTH, D_A), lambda i: (0, 0))],
        out_specs=[pl.BlockSpec((tm, D_A), lambda i: (i, 0)), pl.BlockSpec((8, D_A), lambda i: (0, 0))],
        out_shape=[_sds((s, D_A)), _sds((8, D_A))],
        compiler_params=_params("arbitrary"),
    )(dxa, dxa, ax, ax, w)


def _lin_scan(name, a, b, reverse):
    s, c = a.shape
    L = SCAN_CHUNK
    n = s // L
    cb = 128

    def body(a_ref, b_ref, h_ref, p_ref, hl_ref, pl_ref, car_ref):
        def pos(j):
            return (L - 1 - j) if reverse else j

        def step(jj, carry):
            h, p = carry
            j = pos(jj)
            aj = a_ref[:, j, :]
            h = aj * h + b_ref[:, j, :]
            p = aj * p
            h_ref[:, j, :] = h
            p_ref[:, j, :] = p
            return h, p

        h_last, p_last = lax.fori_loop(0, L, step, (jnp.zeros((n, cb), f32), jnp.ones((n, cb), f32)))
        hl_ref[...] = h_last
        pl_ref[...] = p_last

        def chunk_step(cc, carry):
            ch = (n - 1 - cc) if reverse else cc
            car_ref[pl.ds(ch, 1), :] = carry
            return hl_ref[pl.ds(ch, 1), :] + pl_ref[pl.ds(ch, 1), :] * carry

        lax.fori_loop(0, n, chunk_step, jnp.zeros((1, cb), f32))
        car = car_ref[...]

        def fix(j, _):
            h_ref[:, j, :] = h_ref[:, j, :] + p_ref[:, j, :] * car
            return 0

        lax.fori_loop(0, L, fix, 0)

    spec = pl.BlockSpec((n, L, cb), lambda i: (0, 0, i))
    out = pl.pallas_call(
        body,
        name=name,
        grid=(c // cb,),
        in_specs=[spec, spec],
        out_specs=spec,
        out_shape=_sds((n, L, c)),
        scratch_shapes=[pltpu.VMEM((n, L, cb), f32), pltpu.VMEM((n, cb), f32), pltpu.VMEM((n, cb), f32),
                        pltpu.VMEM((n, cb), f32)],
        compiler_params=_params("arbitrary"),
    )(a.reshape(n, L, c), b.reshape(n, L, c))
    return out.reshape(s, c)


def _s5_scan(name, b_re, b_im, a_re, a_im, reverse):
    s, c = b_re.shape
    L = SCAN_CHUNK
    n = s // L
    cb = 128

    def body(br_ref, bi_ref, ar_ref, ai_ref, hr_ref, hi_ref, pr_ref, pi_ref, hlr_ref, hli_ref, cr_ref, ci_ref):
        ar = ar_ref[...]
        ai = ai_ref[...]

        def pos(j):
            return (L - 1 - j) if reverse else j

        def step(jj, carry):
            hr, hi, pr, pi = carry
            j = pos(jj)
            hr, hi = ar * hr - ai * hi + br_ref[:, j, :], ar * hi + ai * hr + bi_ref[:, j, :]
            pr, pi = ar * pr - ai * pi, ar * pi + ai * pr
            hr_ref[:, j, :] = hr
            hi_ref[:, j, :] = hi
            pr_ref[pl.ds(j, 1), :] = pr
            pi_ref[pl.ds(j, 1), :] = pi
            return hr, hi, pr, pi

        zero = jnp.zeros((n, cb), f32)
        hr_l, hi_l, pr_l, pi_l = lax.fori_loop(0, L, step, (zero, zero, jnp.ones((1, cb), f32), jnp.zeros((1, cb), f32)))
        hlr_ref[...] = hr_l
        hli_ref[...] = hi_l

        def chunk_step(cc, carry):
            car_r, car_i = carry
            ch = (n - 1 - cc) if reverse else cc
            cr_ref[pl.ds(ch, 1), :] = car_r
            ci_ref[pl.ds(ch, 1), :] = car_i
            return (hlr_ref[pl.ds(ch, 1), :] + pr_l * car_r - pi_l * car_i,
                    hli_ref[pl.ds(ch, 1), :] + pr_l * car_i + pi_l * car_r)

        lax.fori_loop(0, n, chunk_step, (jnp.zeros((1, cb), f32), jnp.zeros((1, cb), f32)))
        car_r = cr_ref[...]
        car_i = ci_ref[...]

        def fix(j, _):
            pr = pr_ref[pl.ds(j, 1), :]
            pi = pi_ref[pl.ds(j, 1), :]
            hr_ref[:, j, :] = hr_ref[:, j, :] + pr * car_r - pi * car_i
            hi_ref[:, j, :] = hi_ref[:, j, :] + pr * car_i + pi * car_r
            return 0

        lax.fori_loop(0, L, fix, 0)

    spec = pl.BlockSpec((n, L, cb), lambda i: (0, 0, i))
    vspec = pl.BlockSpec((1, cb), lambda i: (0, i))
    hr, hi = _call(
        body,
        name=name,
        grid=(c // cb,),
        in_specs=[spec, spec, vspec, vspec],
        out_specs=[spec, spec],
        out_shape=[_sds((n, L, c)), _sds((n, L, c))],
        scratch_shapes=[pltpu.VMEM((L, cb), f32), pltpu.VMEM((L, cb), f32), pltpu.VMEM((n, cb), f32),
                        pltpu.VMEM((n, cb), f32), pltpu.VMEM((n, cb), f32), pltpu.VMEM((n, cb), f32)],
        compiler_params=_params("arbitrary"),
    )(b_re.reshape(n, L, c), b_im.reshape(n, L, c), a_re, a_im)
    return hr.reshape(s, c), hi.reshape(s, c)


def _causal_mask(t):
    row = lax.broadcasted_iota(jnp.int32, (t, t), 0)
    col = lax.broadcasted_iota(jnp.int32, (t, t), 1)
    return row >= col


def _attn_fwd(name, q, k, v, cq, ck):
    h, s, dh = q.shape
    t = ATT_TILE
    nq = s // t

    def body(q_ref, k_ref, v_ref, cq_ref, ck_ref, o_ref, lse_ref):
        qi = pl.program_id(1)
        qb = q_ref[...].astype(bf16)
        cqv = cq_ref[...]

        def block(kb, carry, masked):
            m, l, acc = carry
            ks = pl.multiple_of(kb * t, t)
            kk = k_ref[pl.ds(ks, t), :].astype(bf16)
            vv = v_ref[pl.ds(ks, t), :].astype(bf16)
            sc = lax.dot_general(qb, kk, _DN["nt"], preferred_element_type=f32) * ATT_SCALE + (cqv - ck_ref[kb])
            if masked:
                sc = jnp.where(_causal_mask(t), sc, -jnp.inf)
            mn = jnp.maximum(m, jnp.max(sc, axis=1, keepdims=True))
            p = jnp.exp(sc - mn)
            al = jnp.exp(m - mn)
            l = al * l + jnp.sum(p, axis=1, keepdims=True)
            acc = al * acc + jnp.dot(p.astype(bf16), vv, preferred_element_type=f32)
            return mn, l, acc

        init = (jnp.full((t, 1), -jnp.inf, f32), jnp.zeros((t, 1), f32), jnp.zeros((t, dh), f32))
        carry = lax.fori_loop(0, qi, lambda kb, c: block(kb, c, False), init)
        m, l, acc = block(qi, carry, True)
        o_ref[...] = acc / l
        lse_ref[...] = m + jnp.log(l)

    return pl.pallas_call(
        body,
        name=name,
        grid=(h, nq),
        in_specs=[pl.BlockSpec((None, t, dh), lambda hh, i: (hh, i, 0)),
                  pl.BlockSpec((None, s, dh), lambda hh, i: (hh, 0, 0)),
                  pl.BlockSpec((None, s, dh), lambda hh, i: (hh, 0, 0)),
                  pl.BlockSpec((None, t, 1), lambda hh, i: (hh, i, 0)),
                  pl.BlockSpec((None, nq, 1, t), lambda hh, i: (hh, 0, 0, 0))],
        out_specs=[pl.BlockSpec((None, t, dh), lambda hh, i: (hh, i, 0)),
                   pl.BlockSpec((None, t, 1), lambda hh, i: (hh, i, 0))],
        out_shape=[_sds((h, s, dh)), _sds((h, s, 1))],
        compiler_params=_params("parallel", "arbitrary"),
    )(q, k, v, cq, ck)


def _attn_bwd_dq(name, q, k, v, cq, ck, o, do, lse):
    h, s, dh = q.shape
    t = ATT_TILE
    nq = s // t

    def body(q_ref, k_ref, v_ref, cq_ref, ck_ref, o_ref, do_ref, lse_ref, dq_ref, dl_ref):
        qi = pl.program_id(1)
        qb = q_ref[...].astype(bf16)
        cqv = cq_ref[...]
        dov = do_ref[...]
        dob = dov.astype(bf16)
        delta = jnp.sum(dov * o_ref[...], axis=1, keepdims=True)
        lse_v = lse_ref[...]

        def block(kb, carry, masked):
            dq, psum = carry
            ks = pl.multiple_of(kb * t, t)
            kk = k_ref[pl.ds(ks, t), :].astype(bf16)
            vv = v_ref[pl.ds(ks, t), :].astype(bf16)
            sc = lax.dot_general(qb, kk, _DN["nt"], preferred_element_type=f32) * ATT_SCALE + (cqv - ck_ref[kb])
            p = jnp.exp(sc - lse_v)
            if masked:
                p = jnp.where(_causal_mask(t), p, 0.0)
            dp = lax.dot_general(dob, vv, _DN["nt"], preferred_element_type=f32)
            ds = p * (dp - delta)
            return (dq + jnp.dot(ds.astype(bf16), kk, preferred_element_type=f32),
                    psum + jnp.sum(p * dp, axis=1, keepdims=True))

        carry = lax.fori_loop(0, qi, lambda kb, c: block(kb, c, False), (jnp.zeros((t, dh), f32), jnp.zeros((t, 1), f32)))
        dq, psum = block(qi, carry, True)
        dq_ref[...] = dq * ATT_SCALE
        dl_ref[...] = psum

    qspec = pl.BlockSpec((None, t, dh), lambda hh, i: (hh, i, 0))
    fspec = pl.BlockSpec((None, s, dh), lambda hh, i: (hh, 0, 0))
    cspec = pl.BlockSpec((None, t, 1), lambda hh, i: (hh, i, 0))
    return pl.pallas_call(
        body,
        name=name,
        grid=(h, nq),
        in_specs=[qspec, fspec, fspec, cspec, pl.BlockSpec((None, nq, 1, t), lambda hh, i: (hh, 0, 0, 0)),
                  qspec, qspec, cspec],
        out_specs=[qspec, cspec],
        out_shape=[_sds((h, s, dh)), _sds((h, s, 1))],
        compiler_params=_params("parallel", "arbitrary"),
    )(q, k, v, cq, ck, o, do, lse)


def _attn_bwd_dkv(name, q, k, v, cq, ck, do, lse, delta):
    h, s, dh = q.shape
    t = ATT_TILE
    nq = s // t

    def body(q_ref, k_ref, v_ref, cq_ref, ck_ref, do_ref, lse_ref, dl_ref, dk_ref, dv_ref, dck_ref):
        kj = pl.program_id(1)
        kk = k_ref[...].astype(bf16)
        vv = v_ref[...].astype(bf16)
        ckv = ck_ref[...]

        def block(qi, carry, masked):
            dk, dv, dcs = carry
            qs = pl.multiple_of(qi * t, t)
            qq = q_ref[pl.ds(qs, t), :].astype(bf16)
            dob = do_ref[pl.ds(qs, t), :].astype(bf16)
            sc = (lax.dot_general(qq, kk, _DN["nt"], preferred_element_type=f32) * ATT_SCALE
                  + (cq_ref[pl.ds(qs, t), :] - ckv))
            p = jnp.exp(sc - lse_ref[pl.ds(qs, t), :])
            if masked:
                p = jnp.where(_causal_mask(t), p, 0.0)
            dv = dv + lax.dot_general(p.astype(bf16), dob, _DN["tn"], preferred_element_type=f32)
            dp = lax.dot_general(dob, vv, _DN["nt"], preferred_element_type=f32)
            ds = p * (dp - dl_ref[pl.ds(qs, t), :])
            dk = dk + lax.dot_general(ds.astype(bf16), qq, _DN["tn"], preferred_element_type=f32)
            return dk, dv, dcs + jnp.sum(ds, axis=0, keepdims=True)

        init = (jnp.zeros((t, dh), f32), jnp.zeros((t, dh), f32), jnp.zeros((1, t), f32))
        carry = block(kj, init, True)
        dk, dv, dcs = lax.fori_loop(kj + 1, nq, lambda qi, c: block(qi, c, False), carry)
        dk_ref[...] = dk * ATT_SCALE
        dv_ref[...] = dv
        dck_ref[...] = -dcs

    kspec = pl.BlockSpec((None, t, dh), lambda hh, j: (hh, j, 0))
    fspec = pl.BlockSpec((None, s, dh), lambda hh, j: (hh, 0, 0))
    fcol = pl.BlockSpec((None, s, 1), lambda hh, j: (hh, 0, 0))
    crow = pl.BlockSpec((None, None, 1, t), lambda hh, j: (hh, j, 0, 0))
    return pl.pallas_call(
        body,
        name=name,
        grid=(h, nq),
        in_specs=[fspec, kspec, kspec, fcol, crow, fspec, fcol, fcol],
        out_specs=[kspec, kspec, crow],
        out_shape=[_sds((h, s, dh)), _sds((h, s, dh)), _sds((h, nq, 1, t))],
        compiler_params=_params("parallel", "arbitrary"),
    )(q, k, v, cq, ck, do, lse, delta)


ATT_FEAT = 128
ATT_TQ = 1024
ATT_TK = 256


def _att_tiles(s):
    tq = min(ATT_TQ, s)
    return tq, ATT_TK, tq // ATT_TK


def _keys_le_queries(tk, tq, k0, q0):
    row = lax.broadcasted_iota(jnp.int32, (tk, tq), 0) + k0
    col = lax.broadcasted_iota(jnp.int32, (tk, tq), 1) + q0
    return row <= col


def _attn_fwd_t(name, qt_aug, k_aug, vt):
    h, _, s = qt_aug.shape
    tq, tk, ratio = _att_tiles(s)

    def body(qt_ref, k_ref, vt_ref, o_ref, lse_ref):
        qi = pl.program_id(1)
        qt = qt_ref[...]

        def block(kb, carry, masked):
            m, l, acc = carry
            ks = pl.multiple_of(kb * tk, tk)
            st = jnp.dot(k_ref[pl.ds(ks, tk), :], qt, preferred_element_type=f32)
            if masked:
                st = jnp.where(_keys_le_queries(tk, tq, ks, qi * tq), st, -jnp.inf)
            mn = jnp.maximum(m, jnp.max(st, axis=0, keepdims=True))
            p = jnp.exp(st - mn)
            al = jnp.exp(m - mn)
            l = al * l + jnp.sum(p, axis=0, keepdims=True)
            acc = al * acc + jnp.dot(vt_ref[kb], p.astype(bf16), preferred_element_type=f32)
            return mn, l, acc

        init = (jnp.full((1, tq), -jnp.inf, f32), jnp.zeros((1, tq), f32), jnp.zeros((HEAD_DIM, tq), f32))
        first = lax.fori_loop(0, qi * ratio, lambda kb, c: block(kb, c, False), init)
        m, l, acc = lax.fori_loop(qi * ratio, (qi + 1) * ratio, lambda kb, c: block(kb, c, True), first)
        o_ref[...] = acc / l
        lse_ref[...] = m + jnp.log(l)

    return _call(
        body,
        name=name,
        grid=(h, s // tq),
        in_specs=[pl.BlockSpec((None, ATT_FEAT, tq), lambda hh, i: (hh, 0, i)),
                  pl.BlockSpec((None, s, ATT_FEAT), lambda hh, i: (hh, 0, 0)),
                  pl.BlockSpec((None, s // tk, HEAD_DIM, tk), lambda hh, i: (hh, 0, 0, 0))],
        out_specs=[pl.BlockSpec((None, HEAD_DIM, tq), lambda hh, i: (hh, 0, i)),
                   pl.BlockSpec((None, 1, tq), lambda hh, i: (hh, 0, i))],
        out_shape=[_sds((h, HEAD_DIM, s)), _sds((h, 1, s))],
        compiler_params=_params("parallel", "arbitrary"),
    )(qt_aug, k_aug, vt)


def _attn_bwd_dq_t(name, qt_aug, k_aug, v, kt, ot, dot_, lse):
    h, _, s = qt_aug.shape
    tq, tk, ratio = _att_tiles(s)

    def body(qt_ref, k_ref, v_ref, kt_ref, o_ref, do_ref, lse_ref, dq_ref, dl_ref):
        qi = pl.program_id(1)
        qt = qt_ref[...]
        dov = do_ref[...]
        dob = dov.astype(bf16)
        delta = jnp.sum(dov * o_ref[...], axis=0, keepdims=True)
        lse_v = lse_ref[...]

        def block(kb, carry, masked):
            dq, psum = carry
            ks = pl.multiple_of(kb * tk, tk)
            st = jnp.dot(k_ref[pl.ds(ks, tk), :], qt, preferred_element_type=f32)
            p = jnp.exp(st - lse_v)
            if masked:
                p = jnp.where(_keys_le_queries(tk, tq, ks, qi * tq), p, 0.0)
            dp = jnp.dot(v_ref[pl.ds(ks, tk), :], dob, preferred_element_type=f32)
            ds = p * (dp - delta)
            return (dq + jnp.dot(kt_ref[kb], ds.astype(bf16), preferred_element_type=f32),
                    psum + jnp.sum(p * dp, axis=0, keepdims=True))

        carry = lax.fori_loop(0, qi * ratio, lambda kb, c: block(kb, c, False),
                              (jnp.zeros((HEAD_DIM, tq), f32), jnp.zeros((1, tq), f32)))
        dq, psum = lax.fori_loop(qi * ratio, (qi + 1) * ratio, lambda kb, c: block(kb, c, True), carry)
        dq_ref[...] = dq * ATT_SCALE
        dl_ref[...] = psum

    qspec = pl.BlockSpec((None, HEAD_DIM, tq), lambda hh, i: (hh, 0, i))
    rspec = pl.BlockSpec((None, 1, tq), lambda hh, i: (hh, 0, i))
    return _call(
        body,
        name=name,
        grid=(h, s // tq),
        in_specs=[pl.BlockSpec((None, ATT_FEAT, tq), lambda hh, i: (hh, 0, i)),
                  pl.BlockSpec((None, s, ATT_FEAT), lambda hh, i: (hh, 0, 0)),
                  pl.BlockSpec((None, s, HEAD_DIM), lambda hh, i: (hh, 0, 0)),
                  pl.BlockSpec((None, s // tk, HEAD_DIM, tk), lambda hh, i: (hh, 0, 0, 0)),
                  qspec, qspec, rspec],
        out_specs=[qspec, rspec],
        out_shape=[_sds((h, HEAD_DIM, s)), _sds((h, 1, s))],
        compiler_params=_params("parallel", "arbitrary"),
    )(qt_aug, k_aug, v, kt, ot, dot_, lse)


def _attn_bwd_dkv_t(name, qt_blocks, k_aug, v, qh, do, dot_blocks, lse, delta):
    h, s, _ = k_aug.shape
    tq, tk, ratio = _att_tiles(s)
    nq = s // tq

    def body(qt_ref, k_ref, v_ref, q_ref, do_ref, dot_ref, lse_ref, dl_ref, dk_ref, dv_ref, dck_ref, dsum_ref):
        kj = pl.program_id(1)
        kk = k_ref[...]
        vv = v_ref[...]
        dsum_ref[...] = jnp.zeros_like(dsum_ref)

        def block(qi, carry, masked):
            dk, dv = carry
            qs = pl.multiple_of(qi * tq, tq)
            st = jnp.dot(kk, qt_ref[qi], preferred_element_type=f32)
            p = jnp.exp(st - lse_ref[qi])
            if masked:
                p = jnp.where(_keys_le_queries(tk, tq, kj * tk, qs), p, 0.0)
            dv = dv + jnp.dot(p.astype(bf16), do_ref[pl.ds(qs, tq), :], preferred_element_type=f32)
            dp = jnp.dot(vv, dot_ref[qi], preferred_element_type=f32)
            ds = p * (dp - dl_ref[qi])
            dsum_ref[...] += ds
            dk = dk + jnp.dot(ds.astype(bf16), q_ref[pl.ds(qs, tq), :], preferred_element_type=f32)
            return dk, dv

        first = kj // ratio
        carry = block(first, (jnp.zeros((tk, HEAD_DIM), f32), jnp.zeros((tk, HEAD_DIM), f32)), True)
        dk, dv = lax.fori_loop(first + 1, nq, lambda qi, c: block(qi, c, False), carry)
        dk_ref[...] = dk
        dv_ref[...] = dv
        dck_ref[...] = -jnp.sum(dsum_ref[...], axis=1, keepdims=True)

    full = lambda shape: pl.BlockSpec((None,) + shape, lambda hh, j: (hh,) + (0,) * len(shape))
    kspec = pl.BlockSpec((None, tk, HEAD_DIM), lambda hh, j: (hh, j, 0))
    return _call(
        body,
        name=name,
        grid=(h, s // tk),
        in_specs=[full((nq, ATT_FEAT, tq)),
                  pl.BlockSpec((None, tk, ATT_FEAT), lambda hh, j: (hh, j, 0)),
                  kspec, full((s, HEAD_DIM)), full((s, HEAD_DIM)), full((nq, HEAD_DIM, tq)),
                  full((nq, 1, tq)), full((nq, 1, tq))],
        out_specs=[kspec, kspec, pl.BlockSpec((None, tk, 1), lambda hh, j: (hh, j, 0))],
        out_shape=[_sds((h, s, HEAD_DIM)), _sds((h, s, HEAD_DIM)), _sds((h, s, 1))],
        scratch_shapes=[pltpu.VMEM((tk, tq), f32)],
        compiler_params=_params("parallel", "arbitrary"),
    )(qt_blocks, k_aug, v, qh, do, dot_blocks, lse, delta)


def _split3(c):
    hi = lax.reduce_precision(c, 8, 7)
    r = c - hi
    mid = lax.reduce_precision(r, 8, 7)
    lo = lax.reduce_precision(r - mid, 8, 7)
    return hi.astype(bf16), mid.astype(bf16), lo.astype(bf16)


def _attn_operands(q, k, v, c6):
    s = q.shape[0]
    tq, tk, _ = _att_tiles(s)
    qh = _heads(q * ATT_SCALE).astype(bf16)
    kh = _heads(k).astype(bf16)
    vh = _heads(v).astype(bf16)
    parts = _split3(c6)
    ones = jnp.ones((N_HEADS, s, 3), bf16)
    cpos = jnp.stack(parts, axis=-1)
    pad = jnp.zeros((N_HEADS, s, ATT_FEAT - HEAD_DIM - 6), bf16)
    q_aug = jnp.concatenate([qh, cpos, ones, pad], axis=-1)
    k_aug = jnp.concatenate([kh, ones, -cpos, pad], axis=-1)

    def blocks_t(a, t):
        return a.reshape(N_HEADS, s // t, t, a.shape[-1]).transpose(0, 1, 3, 2)

    return dict(qt_aug=q_aug.transpose(0, 2, 1), qt_blocks=blocks_t(q_aug, tq), k_aug=k_aug, v=vh,
                vt=blocks_t(vh, tk), kt=blocks_t(kh, tk), qh=qh)


def _s5_disc_fn(are, aim, ldt):
    dt = jnp.exp(ldt)
    er = jnp.exp(are * dt)
    br = er * jnp.cos(aim * dt)
    bi = er * jnp.sin(aim * dt)
    nr = br - 1.0
    den = are * are + aim * aim
    return br, bi, (nr * are + bi * aim) / den, (bi * are - nr * aim) / den


def _s5_disc(name, are, aim, ldt):
    def body(a_ref, b_ref, c_ref, o0, o1, o2, o3):
        r = _s5_disc_fn(a_ref[...], b_ref[...], c_ref[...])
        o0[...], o1[...], o2[...], o3[...] = r

    shp = _sds((S5_GROUPS, S5_STATE))
    return pl.pallas_call(body, name=name, out_shape=[shp] * 4)(are, aim, ldt)


def _s5_disc_bwd(name, are, aim, ldt, cts):
    def body(a_ref, b_ref, c_ref, d0, d1, d2, d3, o0, o1, o2):
        _, vjp = jax.vjp(_s5_disc_fn, a_ref[...], b_ref[...], c_ref[...])
        o0[...], o1[...], o2[...] = vjp((d0[...], d1[...], d2[...], d3[...]))

    shp = _sds((S5_GROUPS, S5_STATE))
    return pl.pallas_call(body, name=name, out_shape=[shp, shp, _sds((S5_GROUPS, 1))])(are, aim, ldt, *cts)


def _adamw_rows(w, g, m, v):
    m = ADAM_B1 * m + (1.0 - ADAM_B1) * g
    v = ADAM_B2 * v + (1.0 - ADAM_B2) * (g * g)
    m_hat = m / (1.0 - ADAM_B1 ** ADAM_STEP)
    v_hat = v / (1.0 - ADAM_B2 ** ADAM_STEP)
    return -ADAM_LR * (m_hat / (jnp.sqrt(v_hat) + ADAM_EPS) + ADAM_WD * w), m, v


def _adamw(name, w, ga, gb, m, v):
    rows, cols = w.shape
    tr = _row_tile(rows)

    def body(w_ref, ga_ref, gb_ref, m_ref, v_ref, g_out, d_out, m_out, v_out):
        g = ga_ref[...] + gb_ref[...]
        d, mm, vv = _adamw_rows(w_ref[...], g, m_ref[...], v_ref[...])
        g_out[...] = g
        d_out[...] = d
        m_out[...] = mm
        v_out[...] = vv

    spec = pl.BlockSpec((tr, cols), lambda i: (i, 0))
    return pl.pallas_call(
        body, name=name, grid=(rows // tr,), in_specs=[spec] * 5, out_specs=[spec] * 4,
        out_shape=[_sds((rows, cols))] * 4, compiler_params=_params("parallel"),
    )(w, ga, gb, m, v)


def _sum_stack(name, st):
    n, rows, cols = st.shape
    tr = _row_tile(rows)

    def body(s_ref, o_ref):
        acc = s_ref[0].astype(f32)
        for j in range(1, n):
            acc = acc + s_ref[j].astype(f32)
        o_ref[...] = acc

    return pl.pallas_call(
        body, name=name, grid=(rows // tr,), in_specs=[pl.BlockSpec((n, tr, cols), lambda i: (0, i, 0))],
        out_specs=pl.BlockSpec((tr, cols), lambda i: (i, 0)), out_shape=_sds((rows, cols)),
        compiler_params=_params("parallel"),
    )(st)


def _block_diag(w):
    h, n, m = w.shape
    return jnp.einsum("hij,hg->higj", w, jnp.eye(h, dtype=w.dtype)).reshape(h * n, h * m)


def _block_diag_part(dense, h):
    n, m = dense.shape[0] // h, dense.shape[1] // h
    return jnp.einsum("higj,hg->hij", dense.reshape(h, n, h, m), jnp.eye(h, dtype=dense.dtype))


def _s5_matrices(coef_re, coef_im, b_re, b_im, c_re, c_im):
    bb_re = coef_re[:, :, None] * b_re - coef_im[:, :, None] * b_im
    bb_im = coef_re[:, :, None] * b_im + coef_im[:, :, None] * b_re
    wb_re = _block_diag(jnp.swapaxes(bb_re, 1, 2))
    wb_im = _block_diag(jnp.swapaxes(bb_im, 1, 2))
    wc_re = _block_diag(jnp.swapaxes(c_re, 1, 2))
    wc_im = _block_diag(jnp.swapaxes(-c_im, 1, 2))
    return wb_re, wb_im, wc_re, wc_im


def _heads(t):
    s = t.shape[0]
    return t.reshape(s, N_HEADS, HEAD_DIM).transpose(1, 0, 2)


def _unheads(t):
    s = t.shape[1]
    return t.transpose(1, 0, 2).reshape(s, N_HEADS * HEAD_DIM)


def _shift_down(t):
    return jnp.concatenate([jnp.zeros((1, t.shape[1]), t.dtype), t[:-1]], axis=0)


def _shift_up(t):
    return jnp.concatenate([t[1:], jnp.zeros((1, t.shape[1]), t.dtype)], axis=0)


def _row(v):
    return v.reshape(1, -1)


def _ffn_fwd(tag, h, wg, wu, wd, gamma, beta):
    g, u, act = _ffn_up(tag + "_up", h, wg, wu)
    r, out = _mm_ln(tag + "_down", act, wd, h, gamma, beta, 0.5)
    return out, dict(h=h, g=g, u=u, act=act, r=r)


def _ffn_bwd(tag, dout, sv, wg, wu, wd, gamma):
    s = dout.shape[0]
    dr, dgam, dbet = _ln_bwd(tag + "_lnb", sv["r"], dout, gamma)
    dwd = _mm_plain(tag + "_dwd", "tn", sv["act"], dr, (D_FF, D_MODEL, s), scale=0.5,
                    tiles=(_tile(D_FF, 1408), 1024, _tile(s, 1024)))
    dg, du = _ffn_dact(tag + "_dact", dr, wd, sv["g"], sv["u"])
    dwg, dwu = _mm2(tag + "_dwgu", "tn", (D_MODEL, D_FF, s), sv["h"], dg, None, du, separate=True,
                    tiles=(512, _tile(D_FF, 1408), _tile(s, 1024)))
    dh = _mm2(tag + "_dh", "nt", (s, D_MODEL, D_FF), dg, wg, du, wu, add=dr, add_coef=ALPHA,
              tiles=(_tile(s, 512), 1024, _tile(D_FF, 1408)))[0]
    return dh, dwg, dwu, dwd, dgam, dbet


def _mixer_fwd(tag, h1, w):
    s = h1.shape[0]
    nt = s // ATT_TILE
    z = _mm_plain(tag + "_win", "nn", h1, w["w_in"], (s, N_IN_P, D_MODEL), tiles=(_tile(s, 512), 768, D_MODEL))
    ax, ag = z[:, :D_A], z[:, D_A:2 * D_A]
    q, k, v = z[:, 2 * D_A:3 * D_A], z[:, 3 * D_A:4 * D_A], z[:, 4 * D_A:5 * D_A]
    f, cu = z[:, F_OFF:F_OFF + 128], z[:, CU_OFF:]
    xa = _conv_fwd(tag + "_conv", ax, w["conv_w"], w["conv_b"])
    a, gated = _rg_gates(tag + "_gates", xa, w["rg_wa"], w["rg_wx"], w["rg_ba"], w["rg_bx"], w["rg_lam"])
    ha = _lin_scan(tag + "_rgscan", a, gated, False)
    ones = jnp.ones((s, 128), f32)
    c = _lin_scan(tag + "_cumf", ones, _log_f(tag + "_logf", f, w["fox_bf"]), False)
    att = _attn_operands(q, k, v, c[:, :N_HEADS].T)
    ot, lse = _attn_fwd_t(tag + "_attn", att["qt_aug"], att["k_aug"], att["vt"])
    ob = ot.reshape(D_B, s).T
    bu_re, bu_im = _mm2(tag + "_s5in", "nn", (s, S5_LANES, D_C), cu, w["wb_re"], None, w["wb_im"], separate=True,
                        tiles=(_tile(s, 512), 1024, D_C))
    hre, him = _s5_scan(tag + "_s5scan", bu_re, bu_im, w["abar_re"], w["abar_im"], False)
    o = _mix_out(tag + "_mixout", ag, ha, ob, hre, him, cu, w["s5_d"], w["mix_g"], w["wc_re"], w["wc_im"], w["w_glu"])
    sv = dict(h1=h1, ax=ax, ag=ag, f=f, cu=cu, xa=xa, a=a, ha=ha, att=att, ot=ot, lse=lse, ob=ob, hre=hre, him=him, o=o)
    return o, sv


def _mixer_bwd(tag, do, dr2, sv, w):
    s = do.shape[0]
    (dag, dha, dob, dhre, dhim, dcu1, dwcr, dwci, dwglu, dd, dgn) = _mix_out_bwd(
        tag + "_mixoutb", do, sv["ag"], sv["ha"], sv["ob"], sv["hre"], sv["him"], sv["cu"], w["s5_d"], w["mix_g"],
        w["wc_re"], w["wc_im"], w["w_glu"])
    gre, gim = _s5_scan(tag + "_s5scanb", dhre, dhim, w["abar_re"], -w["abar_im"], True)
    dab_re, dab_im = _s5_decay_grad(tag + "_s5dec", _shift_down(sv["hre"]), _shift_down(sv["him"]), gre, gim)
    dwb_re, dwb_im = _mm2(tag + "_s5dwb", "tn", (D_C, S5_LANES, s), sv["cu"], gre, None, gim, separate=True,
                          tiles=(D_C, 1024, _tile(s, 1024)))
    dcu = _mm2(tag + "_s5dcu", "nt", (s, D_C, S5_LANES), gre, w["wb_re"], gim, w["wb_im"], add=dcu1,
               tiles=(_tile(s, 512), D_C, 1024))[0]
    att = sv["att"]
    tq = _att_tiles(s)[0]
    nt = s // tq
    dot_ = dob.T.reshape(N_HEADS, HEAD_DIM, s)
    dqt, delta = _attn_bwd_dq_t(tag + "_attndq", att["qt_aug"], att["k_aug"], att["v"], att["kt"], sv["ot"], dot_,
                                sv["lse"])
    dot_blocks = dot_.astype(bf16).reshape(N_HEADS, HEAD_DIM, nt, tq).transpose(0, 2, 1, 3)
    dkh, dvh, dck = _attn_bwd_dkv_t(tag + "_attndkv", att["qt_blocks"], att["k_aug"], att["v"], att["qh"],
                                    _heads(dob).astype(bf16), dot_blocks, sv["lse"].reshape(N_HEADS, nt, 1, tq),
                                    delta.reshape(N_HEADS, nt, 1, tq))
    dq, dk, dv = dqt.reshape(D_B, s).T, _unheads(dkh), _unheads(dvh)
    dc = jnp.pad(dck[:, :, 0].T, ((0, 0), (0, 128 - N_HEADS)))
    dlf = _lin_scan(tag + "_cumfb", jnp.ones((s, 128), f32), dc, True)
    df, dbf = _log_f_bwd(tag + "_logfb", dlf, sv["f"], w["fox_bf"])
    ga = _lin_scan(tag + "_rgscanb", _shift_up(sv["a"]), dha, True)
    dxa, dwa, dwx, dba, dbx, dlam = _rg_gates_bwd(tag + "_gatesb", sv["xa"], ga, _shift_down(sv["ha"]), w["rg_wa"],
                                                  w["rg_wx"], w["rg_ba"], w["rg_bx"], w["rg_lam"])
    dax, dconv = _conv_bwd(tag + "_convb", dxa, sv["ax"], w["conv_w"])
    dz = jnp.concatenate([dax, dag, dq, dk, dv, df, dcu], axis=-1).astype(bf16)
    dwin = _mm_plain(tag + "_dwin", "tn", sv["h1"], dz, (D_MODEL, N_IN_P, s), tiles=(512, 768, _tile(s, 1024)))
    dh1 = _mm_plain(tag + "_dh1", "nt", dz, w["w_in"], (s, D_MODEL, N_IN_P), add=dr2, add_coef=ALPHA,
                    tiles=(_tile(s, 512), 1024, 768))
    grads = dict(dwin=dwin, dwglu=dwglu, dconv=dconv, dwa=dwa, dwx=dwx, dba=dba, dbx=dbx, dlam=dlam, dbf=dbf,
                 dab_re=dab_re, dab_im=dab_im, dwb_re=dwb_re, dwb_im=dwb_im, dwcr=dwcr, dwci=dwci, dd=dd, dgn=dgn)
    return dh1, grads


SMALL_NAMES = ["ln1_g", "ln1_b", "conv_w", "conv_b", "rg_w_a", "rg_b_a", "rg_w_x", "rg_b_x", "rg_lambda", "fox_b_f",
               "s5_a_re", "s5_a_im", "s5_log_dt", "s5_b_re", "s5_b_im", "s5_c_re", "s5_c_im", "s5_d", "mix_norm_g",
               "ln2_g", "ln2_b", "ln3_g", "ln3_b"]
BIG_NAMES = ["ffn1_w_gate", "ffn1_w_up", "ffn1_w_down", "w_in", "s5_w_glu", "w_out", "ffn2_w_gate", "ffn2_w_up",
             "ffn2_w_down"]


def _local_step(x, target, weight, small, on_grads):
    h = x
    saved = []
    for l in range(DEPTH):
        big = {}

        def fetch(group, l=l, big=big):
            for n in GROUPS[group]:
                big[n] = weight(l, n)

        sm = {n: small[n][l] for n in SMALL_NAMES}
        abar_re, abar_im, coef_re, coef_im = _s5_disc(f"l{l}_s5disc", sm["s5_a_re"], sm["s5_a_im"],
                                                      sm["s5_log_dt"].reshape(S5_GROUPS, 1))
        mats, mats_vjp = jax.vjp(_s5_matrices, coef_re, coef_im, sm["s5_b_re"], sm["s5_b_im"], sm["s5_c_re"],
                                 sm["s5_c_im"])
        w = dict(
            conv_w=sm["conv_w"], conv_b=_row(sm["conv_b"]),
            rg_wa=_block_diag(sm["rg_w_a"]).astype(bf16), rg_wx=_block_diag(sm["rg_w_x"]).astype(bf16),
            rg_ba=_row(sm["rg_b_a"]), rg_bx=_row(sm["rg_b_x"]), rg_lam=_row(sm["rg_lambda"]),
            fox_bf=jnp.pad(_row(sm["fox_b_f"]), ((0, 0), (0, 128 - N_HEADS))),
            abar_re=_row(abar_re), abar_im=_row(abar_im),
            wb_re=mats[0].astype(bf16), wb_im=mats[1].astype(bf16), wc_re=mats[2].astype(bf16),
            wc_im=mats[3].astype(bf16), s5_d=_row(sm["s5_d"]), mix_g=_row(sm["mix_norm_g"]))
        fetch("F1")
        h1, sv1 = _ffn_fwd(f"l{l}_ffn1", h, big["ffn1_w_gate"], big["ffn1_w_up"], big["ffn1_w_down"],
                           _row(sm["ln1_g"]), _row(sm["ln1_b"]))
        fetch("MX")
        w["w_in"], w["w_glu"] = big["w_in"], big["s5_w_glu"]
        o, svm = _mixer_fwd(f"l{l}_mix", h1, w)
        r2, h2 = _mm_ln(f"l{l}_wout", o, big["w_out"], h1, _row(sm["ln2_g"]), _row(sm["ln2_b"]), 1.0)
        fetch("F2")
        h3, sv2 = _ffn_fwd(f"l{l}_ffn2", h2, big["ffn2_w_gate"], big["ffn2_w_up"], big["ffn2_w_down"],
                           _row(sm["ln3_g"]), _row(sm["ln3_b"]))
        saved.append(dict(sm=sm, w=w, big=big, sv1=sv1, svm=svm, r2=r2, sv2=sv2, mats_vjp=mats_vjp))
        h = h3

    dh, loss_row = _loss_head("loss_head", h, target)
    s = x.shape[0]
    gsmall = {n: [None] * DEPTH for n in SMALL_NAMES}
    for l in reversed(range(DEPTH)):
        sd = saved[l]
        sm, w, big = sd["sm"], sd["w"], sd["big"]
        dh2, dwg, dwu, dwd, dgam, dbet = _ffn_bwd(f"l{l}_ffn2", dh, sd["sv2"], big["ffn2_w_gate"],
                                                 big["ffn2_w_up"], big["ffn2_w_down"], _row(sm["ln3_g"]))
        on_grads((l, "F2"), dict(ffn2_w_gate=dwg, ffn2_w_up=dwu, ffn2_w_down=dwd))
        gsmall["ln3_g"][l], gsmall["ln3_b"][l] = dgam[0], dbet[0]
        dr2, dgam, dbet = _ln_bwd(f"l{l}_ln2b", sd["r2"], dh2, _row(sm["ln2_g"]))
        gsmall["ln2_g"][l], gsmall["ln2_b"][l] = dgam[0], dbet[0]
        dwout = _mm_plain(f"l{l}_dwout", "tn", sd["svm"]["o"], dr2, (D_MODEL, D_MODEL, s))
        do = _mm_plain(f"l{l}_do", "nt", dr2, big["w_out"], (s, D_MODEL, D_MODEL))
        dh1, g = _mixer_bwd(f"l{l}_mix", do, dr2, sd["svm"], w)
        on_grads((l, "MX"), dict(w_in=g["dwin"], s5_w_glu=g["dwglu"], w_out=dwout))
        gsmall["conv_w"][l], gsmall["conv_b"][l] = g["dconv"][:CONV_WIDTH], g["dconv"][CONV_WIDTH]
        gsmall["rg_w_a"][l] = _block_diag_part(g["dwa"], N_HEADS)
        gsmall["rg_w_x"][l] = _block_diag_part(g["dwx"], N_HEADS)
        gsmall["rg_b_a"][l], gsmall["rg_b_x"][l], gsmall["rg_lambda"][l] = g["dba"][0], g["dbx"][0], g["dlam"][0]
        gsmall["fox_b_f"][l] = g["dbf"][0, :N_HEADS]
        dcoef_re, dcoef_im, db_re, db_im, dc_re, dc_im = sd["mats_vjp"]((g["dwb_re"], g["dwb_im"], g["dwcr"], g["dwci"]))
        da_re, da_im, dldt = _s5_disc_bwd(
            f"l{l}_s5discb", sm["s5_a_re"], sm["s5_a_im"], sm["s5_log_dt"].reshape(S5_GROUPS, 1),
            (g["dab_re"].reshape(S5_GROUPS, S5_STATE), g["dab_im"].reshape(S5_GROUPS, S5_STATE), dcoef_re, dcoef_im))
        gsmall["s5_a_re"][l], gsmall["s5_a_im"][l], gsmall["s5_log_dt"][l] = da_re, da_im, dldt[:, 0]
        gsmall["s5_b_re"][l], gsmall["s5_b_im"][l], gsmall["s5_c_re"][l], gsmall["s5_c_im"][l] = db_re, db_im, dc_re, dc_im
        gsmall["s5_d"][l], gsmall["mix_norm_g"][l] = g["dd"][0], g["dgn"][0]
        dh, dwg, dwu, dwd, dgam, dbet = _ffn_bwd(f"l{l}_ffn1", dh1, sd["sv1"], big["ffn1_w_gate"],
                                                big["ffn1_w_up"], big["ffn1_w_down"], _row(sm["ln1_g"]))
        on_grads((l, "F1"), dict(ffn1_w_gate=dwg, ffn1_w_up=dwu, ffn1_w_down=dwd))
        gsmall["ln1_g"][l], gsmall["ln1_b"][l] = dgam[0], dbet[0]
    gsmall = {n: jnp.stack(v) for n, v in gsmall.items()}
    return loss_row[0, 0], dh, gsmall


def _position():
    return lax.axis_index("x"), lax.axis_index("y"), lax.axis_index("c")


_ANY = pl.BlockSpec(memory_space=pl.ANY)


def _chip_gather(name, shards):
    n = len(shards)

    def body(*refs):
        in_refs, out_refs = refs[:n], refs[n:2 * n]
        send_sems, recv_sems, local_sems = refs[2 * n:]
        x, y, c = _position()
        me = 2 * x + y
        peers = [(1 - x, y), (x, 1 - y), (1 - x, 1 - y)]
        local = [pltpu.make_async_copy(in_refs[i], out_refs[i].at[me], local_sems.at[i]) for i in range(n)]
        for cp in local:
            cp.start()
        sends = []
        for i in range(n):
            for r, (px, py) in enumerate(peers):
                cp = pltpu.make_async_remote_copy(
                    src_ref=in_refs[i], dst_ref=out_refs[i].at[me], send_sem=send_sems.at[3 * i + r],
                    recv_sem=recv_sems.at[3 * i + r], device_id=(px, py, c), device_id_type=MESH)
                cp.start()
                sends.append(cp)
        for i in range(n):
            for r, (px, py) in enumerate(peers):
                pltpu.make_async_remote_copy(
                    src_ref=in_refs[i], dst_ref=out_refs[i].at[2 * px + py], send_sem=send_sems.at[3 * i + r],
                    recv_sem=recv_sems.at[3 * i + r], device_id=(px, py, c), device_id_type=MESH).wait_recv()
        for cp in sends:
            cp.wait_send()
        for cp in local:
            cp.wait()

    return pl.pallas_call(
        body, name=name, in_specs=[_ANY] * n, out_specs=[_ANY] * n,
        out_shape=[_sds((N_CHIPS,) + a.shape, a.dtype) for a in shards],
        scratch_shapes=[pltpu.SemaphoreType.DMA((3 * n,)), pltpu.SemaphoreType.DMA((3 * n,)),
                        pltpu.SemaphoreType.DMA((n,))],
    )(*shards)


def _chip_scatter(name, stacks):
    n = len(stacks)

    def body(*refs):
        in_refs, out_refs = refs[:n], refs[n:2 * n]
        send_sems, recv_sems, local_sems = refs[2 * n:]
        x, y, c = _position()
        me = 2 * x + y
        peers = [(1 - x, y), (x, 1 - y), (1 - x, 1 - y)]
        local = [pltpu.make_async_copy(in_refs[i].at[me], out_refs[i].at[me], local_sems.at[i]) for i in range(n)]
        for cp in local:
            cp.start()
        sends = []
        for i in range(n):
            for r, (px, py) in enumerate(peers):
                cp = pltpu.make_async_remote_copy(
                    src_ref=in_refs[i].at[2 * px + py], dst_ref=out_refs[i].at[me], send_sem=send_sems.at[3 * i + r],
                    recv_sem=recv_sems.at[3 * i + r], device_id=(px, py, c), device_id_type=MESH)
                cp.start()
                sends.append(cp)
        for i in range(n):
            for r, (px, py) in enumerate(peers):
                pltpu.make_async_remote_copy(
                    src_ref=in_refs[i].at[me], dst_ref=out_refs[i].at[2 * px + py], send_sem=send_sems.at[3 * i + r],
                    recv_sem=recv_sems.at[3 * i + r], device_id=(px, py, c), device_id_type=MESH).wait_recv()
        for cp in sends:
            cp.wait_send()
        for cp in local:
            cp.wait()

    return pl.pallas_call(
        body, name=name, in_specs=[_ANY] * n, out_specs=[_ANY] * n,
        out_shape=[_sds(a.shape, a.dtype) for a in stacks],
        scratch_shapes=[pltpu.SemaphoreType.DMA((3 * n,)), pltpu.SemaphoreType.DMA((3 * n,)),
                        pltpu.SemaphoreType.DMA((n,))],
    )(*stacks)


def _sibling_swap(name, arrs):
    n = len(arrs)

    def body(*refs):
        in_refs, out_refs = refs[:n], refs[n:2 * n]
        send_sems, recv_sems = refs[2 * n:]
        x, y, c = _position()
        copies = [pltpu.make_async_remote_copy(
            src_ref=in_refs[i], dst_ref=out_refs[i], send_sem=send_sems.at[i], recv_sem=recv_sems.at[i],
            device_id=(x, y, 1 - c), device_id_type=MESH) for i in range(n)]
        for cp in copies:
            cp.start()
        for cp in copies:
            cp.wait_recv()
        for cp in copies:
            cp.wait_send()

    return pl.pallas_call(
        body, name=name, in_specs=[_ANY] * n, out_specs=[_ANY] * n,
        out_shape=[_sds(a.shape, a.dtype) for a in arrs],
        scratch_shapes=[pltpu.SemaphoreType.DMA((n,)), pltpu.SemaphoreType.DMA((n,))],
    )(*arrs)


def _dev_gather(name, arr):
    def body(in_ref, out_ref, send_sems, recv_sems, local_sem):
        x, y, c = _position()
        me = 4 * x + 2 * y + c
        local = pltpu.make_async_copy(in_ref, out_ref.at[me], local_sem)
        local.start()
        peers = []
        for k in range(1, N_DEV):
            peers.append((1 - x if k & 4 else x, 1 - y if k & 2 else y, 1 - c if k & 1 else c))
        sends = []
        for k, peer in enumerate(peers):
            cp = pltpu.make_async_remote_copy(src_ref=in_ref, dst_ref=out_ref.at[me], send_sem=send_sems.at[k],
                                              recv_sem=recv_sems.at[k], device_id=peer, device_id_type=MESH)
            cp.start()
            sends.append(cp)
        for k, (px, py, pc) in enumerate(peers):
            pltpu.make_async_remote_copy(src_ref=in_ref, dst_ref=out_ref.at[4 * px + 2 * py + pc],
                                         send_sem=send_sems.at[k], recv_sem=recv_sems.at[k], device_id=(px, py, pc),
                                         device_id_type=MESH).wait_recv()
        for cp in sends:
            cp.wait_send()
        local.wait()

    return pl.pallas_call(
        body, name=name, in_specs=[_ANY], out_specs=_ANY, out_shape=_sds((N_DEV,) + arr.shape, arr.dtype),
        scratch_shapes=[pltpu.SemaphoreType.DMA((N_DEV - 1,)), pltpu.SemaphoreType.DMA((N_DEV - 1,)),
                        pltpu.SemaphoreType.DMA],
    )(arr)


COLUMN_SHARDED = ("ffn1_w_gate", "ffn1_w_up", "ffn2_w_gate", "ffn2_w_up")
PACK_QUANTUM = 128 * 256


def _permute_in_cols(w):
    pad = jnp.zeros(w.shape[:-1] + (128 - N_HEADS,), w.dtype)
    return jnp.concatenate([w[..., :F_OFF + N_HEADS], pad, w[..., F_OFF + N_HEADS:]], axis=-1)


def _unpermute_in_cols(w):
    return jnp.concatenate([w[..., :F_OFF + N_HEADS], w[..., CU_OFF:]], axis=-1)


def _unstack(name, st):
    _, l, r, c = st.shape
    if name in COLUMN_SHARDED:
        return st.transpose(1, 2, 0, 3).reshape(l, r, N_CHIPS * c)
    return st.transpose(1, 0, 2, 3).reshape(l, N_CHIPS * r, c)


def _restack(name, g):
    l, r, c = g.shape
    if name in COLUMN_SHARDED:
        return g.reshape(l, r, N_CHIPS, c // N_CHIPS).transpose(2, 0, 1, 3)
    return g.reshape(l, N_CHIPS, r // N_CHIPS, c).transpose(1, 0, 2, 3)


def _pack(arrs):
    flat = jnp.concatenate([a.reshape(-1) for a in arrs])
    pad = -flat.shape[0] % PACK_QUANTUM
    return jnp.pad(flat, (0, pad)).reshape(-1, 128)


def _unpack(buf, shapes):
    flat = buf.reshape(-1)
    out, off = [], 0
    for shp in shapes:
        size = math.prod(shp)
        out.append(flat[off:off + size].reshape(shp))
        off += size
    return out


WEIGHT_NAMES = ["ffn1_w_gate", "ffn1_w_up", "ffn1_w_down", "ln1_g", "ln1_b", "w_in", "conv_w", "conv_b", "rg_w_a",
                "rg_b_a", "rg_w_x", "rg_b_x", "rg_lambda", "fox_b_f", "s5_a_re", "s5_a_im", "s5_log_dt", "s5_b_re",
                "s5_b_im", "s5_c_re", "s5_c_im", "s5_d", "s5_w_glu", "mix_norm_g", "w_out", "ln2_g", "ln2_b",
                "ffn2_w_gate", "ffn2_w_up", "ffn2_w_down", "ln3_g", "ln3_b"]


def _train_step(x, loss_target, w, m, v):
    ix, iy, _ = _position()
    chip = 2 * ix + iy

    shards = [(_permute_in_cols(w[n]) if n == "w_in" else w[n]).astype(bf16) for n in BIG_NAMES]
    stacks = _chip_gather("gather_weights", shards + [w["conv_w"]])
    big = {n: _unstack(n, st) for n, st in zip(BIG_NAMES, stacks)}
    small = {n: w[n] for n in SMALL_NAMES}
    small["conv_w"] = stacks[-1].transpose(1, 2, 0, 3).reshape(DEPTH, CONV_WIDTH, D_A)

    loss_local, gx, gbig, gsmall = _local_step(x[0], loss_target[0], big, small)

    sent = [_restack(n, gbig[n]).astype(bf16) for n in BIG_NAMES]
    recv = _chip_scatter("scatter_grads", sent)
    partial = {}
    for n, st in zip(BIG_NAMES, recv):
        _, l, r, c = st.shape
        p = _sum_stack("sum_" + n, st.reshape(N_CHIPS, l * r, c))
        partial[n] = _unpermute_in_cols(p) if n == "w_in" else p
    other = dict(zip(BIG_NAMES, _sibling_swap("swap_grads", [partial[n] for n in BIG_NAMES])))

    small_shapes = [gsmall[n].shape for n in SMALL_NAMES]
    total = _sum_stack("sum_small", _dev_gather("gather_small", _pack([gsmall[n] for n in SMALL_NAMES])))
    gsm = dict(zip(SMALL_NAMES, _unpack(total, small_shapes)))
    cw = D_A // N_CHIPS
    gsm["conv_w"] = lax.dynamic_slice_in_dim(gsm["conv_w"], chip * cw, cw, axis=2)

    grads, deltas, new_m, new_v = {}, {}, {}, {}
    for n in BIG_NAMES:
        shp = w[n].shape
        two_d = (shp[0] * shp[1], shp[2])
        g, d, mm, vv = _adamw("adamw_" + n, w[n].reshape(two_d), partial[n], other[n], m[n].reshape(two_d),
                              v[n].reshape(two_d))
        grads[n], deltas[n], new_m[n], new_v[n] = (t.reshape(shp) for t in (g, d, mm, vv))
    shapes = [w[n].shape for n in SMALL_NAMES]
    gp = _pack([gsm[n] for n in SMALL_NAMES])
    res = _adamw("adamw_small", _pack([w[n] for n in SMALL_NAMES]), gp, jnp.zeros_like(gp),
                 _pack([m[n] for n in SMALL_NAMES]), _pack([v[n] for n in SMALL_NAMES]))
    for dst, buf in zip((grads, deltas, new_m, new_v), res):
        dst.update(zip(SMALL_NAMES, _unpack(buf, shapes)))

    loss = lax.psum(loss_local, ("x", "y", "c"))
    return (loss, gx[None], *[grads[n] for n in WEIGHT_NAMES], *[deltas[n] for n in WEIGHT_NAMES],
            *[new_m[n] for n in WEIGHT_NAMES], *[new_v[n] for n in WEIGHT_NAMES])


def _remote(src, dst, send_sems, recv_sems, k, peer):
    return pltpu.make_async_remote_copy(src_ref=src, dst_ref=dst, send_sem=send_sems.at[k], recv_sem=recv_sems.at[k],
                                        device_id=peer, device_id_type=MESH)


class _ChipGatherPart:
    def __init__(self, arrays):
        self.arrays, self.results = list(arrays), None

    def out_shape(self):
        return [_sds((N_CHIPS,) + a.shape, a.dtype) for a in self.arrays]

    def sems(self):
        n = len(self.arrays)
        return [pltpu.SemaphoreType.DMA((3 * n,)), pltpu.SemaphoreType.DMA((3 * n,)), pltpu.SemaphoreType.DMA((n,))]

    def copies(self, ins, outs, sems):
        send_sems, recv_sems, local_sems = sems
        x, y, c = _position()
        me = 2 * x + y
        local, sends, recvs = [], [], []
        for i, (src, dst) in enumerate(zip(ins, outs)):
            local.append(pltpu.make_async_copy(self.mine(src, me), dst.at[me], local_sems.at[i]))
            for r, (px, py) in enumerate([(1 - x, y), (x, 1 - y), (1 - x, 1 - y)]):
                peer = 2 * px + py
                sends.append(_remote(self.theirs(src, peer), dst.at[me], send_sems, recv_sems, 3 * i + r, (px, py, c)))
                recvs.append(_remote(self.mine(src, me), dst.at[peer], send_sems, recv_sems, 3 * i + r, (px, py, c)))
        return local, sends, recvs

    def mine(self, src, me):
        return src

    def theirs(self, src, peer):
        return src


class _ChipScatterPart(_ChipGatherPart):
    def out_shape(self):
        return [_sds(a.shape, a.dtype) for a in self.arrays]

    def mine(self, src, me):
        return src.at[me]

    def theirs(self, src, peer):
        return src.at[peer]


class _SiblingSwapPart:
    def __init__(self, arrays):
        self.arrays, self.results = list(arrays), None

    def out_shape(self):
        return [_sds(a.shape, a.dtype) for a in self.arrays]

    def sems(self):
        n = len(self.arrays)
        return [pltpu.SemaphoreType.DMA((n,)), pltpu.SemaphoreType.DMA((n,))]

    def copies(self, ins, outs, sems):
        x, y, c = _position()
        both = [_remote(src, dst, sems[0], sems[1], i, (x, y, 1 - c)) for i, (src, dst) in enumerate(zip(ins, outs))]
        return [], both, both


class _DevGatherPart:
    def __init__(self, array):
        self.arrays, self.results = [array], None

    def out_shape(self):
        return [_sds((N_DEV,) + self.arrays[0].shape, self.arrays[0].dtype)]

    def sems(self):
        return [pltpu.SemaphoreType.DMA((N_DEV - 1,)), pltpu.SemaphoreType.DMA((N_DEV - 1,)),
                pltpu.SemaphoreType.DMA((1,))]

    def copies(self, ins, outs, sems):
        send_sems, recv_sems, local_sems = sems
        (src,), (dst,) = ins, outs
        x, y, c = _position()
        me = 4 * x + 2 * y + c
        local = [pltpu.make_async_copy(src, dst.at[me], local_sems.at[0])]
        sends, recvs = [], []
        for k in range(1, N_DEV):
            px, py, pc = (1 - x if k & 4 else x, 1 - y if k & 2 else y, 1 - c if k & 1 else c)
            sends.append(_remote(src, dst.at[me], send_sems, recv_sems, k - 1, (px, py, pc)))
            recvs.append(_remote(src, dst.at[4 * px + 2 * py + pc], send_sems, recv_sems, k - 1, (px, py, pc)))
        return local, sends, recvs


def _split_by(parts, refs, count):
    out, off = [], 0
    for p in parts:
        out.append(refs[off:off + count(p)])
        off += count(p)
    return out


def _parts_refs(parts, in_refs, out_refs, sem_refs):
    return zip(parts, _split_by(parts, in_refs, lambda p: len(p.arrays)),
               _split_by(parts, out_refs, lambda p: len(p.arrays)), _split_by(parts, sem_refs, lambda p: len(p.sems())))


def _exchange_start(parts, in_refs, out_refs, sem_refs):
    for part, ins, outs, sems in _parts_refs(parts, in_refs, out_refs, sem_refs):
        local, sends, _ = part.copies(ins, outs, sems)
        for cp in local + sends:
            cp.start()


def _exchange_finish(parts, in_refs, out_refs, sem_refs):
    for part, ins, outs, sems in _parts_refs(parts, in_refs, out_refs, sem_refs):
        local, sends, recvs = part.copies(ins, outs, sems)
        for cp in recvs:
            cp.wait_recv()
        for cp in sends:
            cp.wait_send()
        for cp in local:
            cp.wait()


def _exchange_operands(parts):
    return ([a for p in parts for a in p.arrays], [s for p in parts for s in p.out_shape()],
            [s for p in parts for s in p.sems()])


def _set_results(parts, res):
    for part, outs in zip(parts, _split_by(parts, list(res), lambda p: len(p.arrays))):
        part.results = list(outs)


def _exchange_now(name, parts):
    x_in, x_out, x_sem = _exchange_operands(parts)
    n = len(x_in)

    def body(*refs):
        _exchange_start(parts, refs[:n], refs[n:2 * n], refs[2 * n:])
        _exchange_finish(parts, refs[:n], refs[n:2 * n], refs[2 * n:])

    res = pl.pallas_call(body, name=name, in_specs=[_ANY] * n, out_specs=[_ANY] * n, out_shape=x_out,
                         scratch_shapes=x_sem)(*x_in)
    _set_results(parts, res)


_RIDERS = {}


def _call(body, *, name, grid, in_specs, out_specs, out_shape, scratch_shapes=(), compiler_params=None):
    make_parts = _RIDERS.pop(name, None)
    if make_parts is None:
        return pl.pallas_call(body, name=name, grid=grid, in_specs=in_specs, out_specs=out_specs, out_shape=out_shape,
                              scratch_shapes=scratch_shapes, compiler_params=compiler_params)
    parts = make_parts()
    x_in, x_out, x_sem = _exchange_operands(parts)
    n_out, n_scr, n_x = len(out_shape), len(scratch_shapes), len(x_in)

    def run(*args):
        n_in = len(args)

        def hosted(*refs):
            ins, xi = refs[:n_in], refs[n_in:n_in + n_x]
            outs, xo = refs[n_in + n_x:n_in + n_x + n_out], refs[n_in + n_x + n_out:n_in + 2 * n_x + n_out]
            scr, xs = refs[n_in + 2 * n_x + n_out:n_in + 2 * n_x + n_out + n_scr], refs[n_in + 2 * n_x + n_out + n_scr:]
            first = functools.reduce(jnp.logical_and, [pl.program_id(d) == 0 for d in range(len(grid))])
            last = functools.reduce(jnp.logical_and, [pl.program_id(d) == grid[d] - 1 for d in range(len(grid))])

            @pl.when(first)
            def _():
                _exchange_start(parts, xi, xo, xs)

            body(*ins, *outs, *scr)

            @pl.when(last)
            def _():
                _exchange_finish(parts, xi, xo, xs)

        res = pl.pallas_call(
            hosted, name=name, grid=grid, in_specs=list(in_specs) + [_ANY] * n_x,
            out_specs=list(out_specs) + [_ANY] * n_x, out_shape=list(out_shape) + x_out,
            scratch_shapes=list(scratch_shapes) + x_sem, compiler_params=_params(*["arbitrary"] * len(grid)),
        )(*args, *x_in)
        _set_results(parts, res[n_out:])
        return list(res[:n_out])

    return run


GROUPS = {"F1": ["ffn1_w_gate", "ffn1_w_up", "ffn1_w_down"], "MX": ["w_in", "s5_w_glu", "w_out"],
          "F2": ["ffn2_w_gate", "ffn2_w_up", "ffn2_w_down"]}
GROUP_OF = {n: g for g, names in GROUPS.items() for n in names}
GATHER_HOSTS = {"l0_ffn1_up": (0, "MX"), "l0_mix_attn": (0, "F2"), "l0_mix_s5scan": (1, "F1"), "l0_ffn2_up": (1, "MX"),
                "l1_mix_attn": (1, "F2")}
SCATTER_HOSTS = {"l1_mix_attndkv": (1, "F2"), "l1_ffn1_dact": (1, "MX"), "l0_mix_attndq": (1, "F1"),
                 "l0_mix_attndkv": (0, "F2"), "l0_ffn1_dact": (0, "MX")}


def _unstack_layer(name, st):
    _, r, c = st.shape
    if name in COLUMN_SHARDED:
        return st.transpose(1, 0, 2).reshape(r, N_CHIPS * c)
    return st.reshape(N_CHIPS * r, c)


def _restack_layer(name, g):
    r, c = g.shape
    if name in COLUMN_SHARDED:
        return g.reshape(r, N_CHIPS, c // N_CHIPS).transpose(1, 0, 2)
    return g.reshape(N_CHIPS, r // N_CHIPS, c)


def _adamw_layer(name, layer, w, ga, gb, m, v, bufs):
    _, r, c = w.shape
    tr = _row_tile(r)

    def body(w_ref, ga_ref, gb_ref, m_ref, v_ref, *rest):
        g_out, d_out, m_out, v_out = rest[-4:]
        g = ga_ref[...] + gb_ref[...]
        d, mm, vv = _adamw_rows(w_ref[...], g, m_ref[...], v_ref[...])
        g_out[...] = g
        d_out[...] = d
        m_out[...] = mm
        v_out[...] = vv

    full = pl.BlockSpec((None, tr, c), lambda i: (layer, i, 0))
    flat = pl.BlockSpec((tr, c), lambda i: (i, 0))
    extra = {} if bufs is None else dict(input_output_aliases={5 + k: k for k in range(4)})
    return pl.pallas_call(
        body, name=name, grid=(r // tr,),
        in_specs=[full, flat, flat, full, full] + ([] if bufs is None else [_ANY] * 4),
        out_specs=[full] * 4, out_shape=[_sds(w.shape)] * 4, compiler_params=_params("parallel"), **extra,
    )(w, ga, gb, m, v, *([] if bufs is None else bufs))


def _train_step(x, loss_target, w, m, v):
    ix, iy, _ = _position()
    chip = 2 * ix + iy
    shard = {n: (_permute_in_cols(w[n]) if n == "w_in" else w[n]).astype(bf16) for n in BIG_NAMES}

    gathered = {(0, "F1"): _ChipGatherPart([shard[n][0] for n in GROUPS["F1"]] + [w["conv_w"]])}
    _exchange_now("gather_first", [gathered[(0, "F1")]])

    def gather_parts(key):
        gathered[key] = _ChipGatherPart([shard[n][key[0]] for n in GROUPS[key[1]]])
        return [gathered[key]]

    for host, key in GATHER_HOSTS.items():
        _RIDERS[host] = functools.partial(gather_parts, key)

    def weight(layer, name):
        group = GROUP_OF[name]
        return _unstack_layer(name, gathered[(layer, group)].results[GROUPS[group].index(name)])

    small = {n: w[n] for n in SMALL_NAMES}
    small["conv_w"] = gathered[(0, "F1")].results[-1].transpose(1, 2, 0, 3).reshape(DEPTH, CONV_WIDTH, D_A)

    grads_full, scattered = {}, {}

    def scatter_parts(key):
        scattered[key] = _ChipScatterPart([_restack_layer(n, grads_full[key][n]).astype(bf16) for n in GROUPS[key[1]]])
        return [scattered[key]]

    for host, key in SCATTER_HOSTS.items():
        _RIDERS[host] = functools.partial(scatter_parts, key)

    loss_local, gx, gsmall = _local_step(x[0], loss_target[0], weight, small, grads_full.__setitem__)

    partial = {}

    def reduce_group(key):
        for n, st in zip(GROUPS[key[1]], scattered[key].results):
            p = _sum_stack(f"sum_l{key[0]}_{n}", st)
            partial[(key[0], n)] = _unpermute_in_cols(p) if n == "w_in" else p

    for key in SCATTER_HOSTS.values():
        reduce_group(key)
    early = list(partial)
    small_shapes = [gsmall[n].shape for n in SMALL_NAMES]
    last_key = (0, "F1")
    last_parts = scatter_parts(last_key) + [_SiblingSwapPart([partial[k] for k in early]),
                                            _DevGatherPart(_pack([gsmall[n] for n in SMALL_NAMES]))]
    _exchange_now("exchange_last", last_parts)
    other = dict(zip(early, last_parts[1].results))
    reduce_group(last_key)
    late = [(0, n) for n in GROUPS["F1"]]
    swap_late = _SiblingSwapPart([partial[k] for k in late])
    _exchange_now("swap_last", [swap_late])
    other.update(zip(late, swap_late.results))

    total = _sum_stack("sum_small", last_parts[2].results[0])
    gsm = dict(zip(SMALL_NAMES, _unpack(total, small_shapes)))
    cw = D_A // N_CHIPS
    gsm["conv_w"] = lax.dynamic_slice_in_dim(gsm["conv_w"], chip * cw, cw, axis=2)

    grads, deltas, new_m, new_v = {}, {}, {}, {}
    for n in BIG_NAMES:
        bufs = None
        for layer in range(DEPTH):
            bufs = _adamw_layer(f"adamw_l{layer}_{n}", layer, w[n], partial[(layer, n)], other[(layer, n)], m[n], v[n],
                                bufs)
        grads[n], deltas[n], new_m[n], new_v[n] = bufs
    shapes = [w[n].shape for n in SMALL_NAMES]
    gp = _pack([gsm[n] for n in SMALL_NAMES])
    res = _adamw("adamw_small", _pack([w[n] for n in SMALL_NAMES]), gp, jnp.zeros_like(gp),
                 _pack([m[n] for n in SMALL_NAMES]), _pack([v[n] for n in SMALL_NAMES]))
    for dst, buf in zip((grads, deltas, new_m, new_v), res):
        dst.update(zip(SMALL_NAMES, _unpack(buf, shapes)))

    loss = lax.psum(loss_local, ("x", "y", "c"))
    return (loss, gx[None], *[grads[n] for n in WEIGHT_NAMES], *[deltas[n] for n in WEIGHT_NAMES],
            *[new_m[n] for n in WEIGHT_NAMES], *[new_v[n] for n in WEIGHT_NAMES])


def kernel(x, ffn1_w_gate, ffn1_w_up, ffn1_w_down, ln1_g, ln1_b, w_in, conv_w, conv_b, rg_w_a, rg_b_a, rg_w_x, rg_b_x, rg_lambda, fox_b_f, s5_a_re, s5_a_im, s5_log_dt, s5_b_re, s5_b_im, s5_c_re, s5_c_im, s5_d, s5_w_glu, mix_norm_g, w_out, ln2_g, ln2_b, ffn2_w_gate, ffn2_w_up, ffn2_w_down, ln3_g, ln3_b, loss_target, m_ffn1_w_gate, m_ffn1_w_up, m_ffn1_w_down, m_ln1_g, m_ln1_b, m_w_in, m_conv_w, m_conv_b, m_rg_w_a, m_rg_b_a, m_rg_w_x, m_rg_b_x, m_rg_lambda, m_fox_b_f, m_s5_a_re, m_s5_a_im, m_s5_log_dt, m_s5_b_re, m_s5_b_im, m_s5_c_re, m_s5_c_im, m_s5_d, m_s5_w_glu, m_mix_norm_g, m_w_out, m_ln2_g, m_ln2_b, m_ffn2_w_gate, m_ffn2_w_up, m_ffn2_w_down, m_ln3_g, m_ln3_b, v_ffn1_w_gate, v_ffn1_w_up, v_ffn1_w_down, v_ln1_g, v_ln1_b, v_w_in, v_conv_w, v_conv_b, v_rg_w_a, v_rg_b_a, v_rg_w_x, v_rg_b_x, v_rg_lambda, v_fox_b_f, v_s5_a_re, v_s5_a_im, v_s5_log_dt, v_s5_b_re, v_s5_b_im, v_s5_c_re, v_s5_c_im, v_s5_d, v_s5_w_glu, v_mix_norm_g, v_w_out, v_ln2_g, v_ln2_b, v_ffn2_w_gate, v_ffn2_w_up, v_ffn2_w_down, v_ln3_g, v_ln3_b):
    args = dict(locals())
    w = {n: args[n] for n in WEIGHT_NAMES}
    m = {n: args["m_" + n] for n in WEIGHT_NAMES}
    v = {n: args["v_" + n] for n in WEIGHT_NAMES}
    return _train_step(x, loss_target, w, m, v)
```

```python
import functools
import math

import jax
import jax.numpy as jnp
from jax import lax
from jax.experimental import pallas as pl
from jax.experimental.pallas import tpu as pltpu

f32 = jnp.float32
bf16 = jnp.bfloat16

D_MODEL = 1024
D_FF = 2816
D_A = 384
D_B = 384
D_C = 256
N_HEADS = 6
HEAD_DIM = 64
S5_GROUPS = 16
S5_GROUP = 16
S5_STATE = 64
S5_LANES = S5_GROUPS * S5_STATE
N_IN = 2 * D_A + 3 * D_B + N_HEADS + D_C
F_OFF = 5 * D_A
CU_OFF = F_OFF + 128
N_IN_P = CU_OFF + D_C
CONV_WIDTH = 4
DEPTH = 2
ALPHA = (2 * DEPTH) ** 0.25
LN_EPS = 1e-5
RMS_EPS = 1e-6
RG_C = 8.0
ATT_SCALE = HEAD_DIM ** -0.5
ADAM_LR, ADAM_B1, ADAM_B2, ADAM_EPS, ADAM_WD, ADAM_STEP = 0.001, 0.9, 0.999, 1e-08, 0.01, 10

SCAN_CHUNK = 64
ROW_TILE = 256
ATT_TILE = 256
N_CHIPS = 4
N_DEV = 8
MESH = pl.DeviceIdType.MESH

_DN = {
    "nn": (((1,), (0,)), ((), ())),
    "nt": (((1,), (1,)), ((), ())),
    "tn": (((0,), (0,)), ((), ())),
}


def _sds(shape, dtype=f32):
    return jax.ShapeDtypeStruct(shape, dtype)


def _tile(n, target):
    best = None
    for t in range(128, min(n, target) + 1, 128):
        if n % t == 0:
            best = t
    return best or n


def _row_tile(rows, target=256):
    best = None
    for t in range(16, min(rows, target) + 1, 16):
        if rows % t == 0:
            best = t
    return best or rows


def _params(*sem):
    return pltpu.CompilerParams(dimension_semantics=sem)


def _mm(name, mode, dims, tiles, a_list, b_list, pairs, n_acc, epilogue, outs, extras=(), vecs=()):
    m, n, k = dims
    tm, tn, tk = tiles
    nk = k // tk
    na, nb, ne, nv, no = len(a_list), len(b_list), len(extras), len(vecs), len(outs)

    def body(*refs):
        a_refs = refs[:na]
        b_refs = refs[na:na + nb]
        e_refs = refs[na + nb:na + nb + ne]
        v_refs = refs[na + nb + ne:na + nb + ne + nv]
        o_refs = refs[na + nb + ne + nv:na + nb + ne + nv + no]
        acc_refs = refs[na + nb + ne + nv + no:]
        kk = pl.program_id(2)

        @pl.when(kk == 0)
        def _():
            for acc in acc_refs:
                acc[...] = jnp.zeros_like(acc)

        a_vals = [r[...].astype(bf16) for r in a_refs]
        b_vals = [r[...].astype(bf16) for r in b_refs]
        for ai, bi, ci in pairs:
            acc_refs[ci][...] += lax.dot_general(a_vals[ai], b_vals[bi], _DN[mode], preferred_element_type=f32)

        @pl.when(kk == nk - 1)
        def _():
            res = epilogue([acc[...] for acc in acc_refs], [e[...] for e in e_refs], [v[...] for v in v_refs])
            for o, r in zip(o_refs, res):
                o[...] = r.astype(o.dtype)

    if mode == "tn":
        a_spec = pl.BlockSpec((tk, tm), lambda i, j, kk: (kk, i))
    else:
        a_spec = pl.BlockSpec((tm, tk), lambda i, j, kk: (i, kk))
    if mode == "nt":
        b_spec = pl.BlockSpec((tn, tk), lambda i, j, kk: (j, kk))
    else:
        b_spec = pl.BlockSpec((tk, tn), lambda i, j, kk: (kk, j))
    o_spec = pl.BlockSpec((tm, tn), lambda i, j, kk: (i, j))
    v_spec = pl.BlockSpec((1, tn), lambda i, j, kk: (0, j))
    res = _call(
        body,
        name=name,
        grid=(m // tm, n // tn, nk),
        in_specs=[a_spec] * na + [b_spec] * nb + [o_spec] * ne + [v_spec] * nv,
        out_specs=[o_spec] * no,
        out_shape=[_sds((m, n), dt) for dt in outs],
        scratch_shapes=[pltpu.VMEM((tm, tn), f32)] * n_acc,
        compiler_params=_params("parallel", "parallel", "arbitrary"),
    )(*a_list, *b_list, *extras, *vecs)
    return res


def _layer_norm_rows(r, gamma, beta):
    mu = jnp.mean(r, axis=-1, keepdims=True)
    xc = r - mu
    var = jnp.mean(xc * xc, axis=-1, keepdims=True)
    return xc * lax.rsqrt(var + LN_EPS) * gamma + beta


def _mm_plain(name, mode, a, b, dims, scale=1.0, out_dtype=f32, add=None, add_coef=1.0, tiles=None):
    m, n, k = dims
    tiles = tiles or (_tile(m, 512), _tile(n, 1024), _tile(k, 1024))

    def epilogue(accs, extras, vecs):
        r = accs[0] if scale == 1.0 else accs[0] * scale
        if extras:
            r = r + add_coef * extras[0]
        return [r]

    return _mm(name, mode, dims, tiles, [a], [b], [(0, 0, 0)], 1, epilogue, [out_dtype],
               extras=[] if add is None else [add])[0]


def _ffn_up(name, h, wg, wu):
    s = h.shape[0]

    def epilogue(accs, extras, vecs):
        g, u = accs
        return [g, u, g * jax.nn.sigmoid(g) * u]

    return _mm(name, "nn", (s, D_FF, D_MODEL), (_tile(s, 512), _tile(D_FF, 1408), D_MODEL), [h], [wg, wu],
               [(0, 0, 0), (0, 1, 1)], 2, epilogue, [f32, f32, bf16])


def _mm_ln(name, a, w, resid, gamma, beta, scale):
    s, k = a.shape

    def epilogue(accs, extras, vecs):
        r = ALPHA * extras[0] + scale * accs[0]
        return [r, _layer_norm_rows(r, vecs[0], vecs[1])]

    return _mm(name, "nn", (s, D_MODEL, k), (_tile(s, 256), D_MODEL, _tile(k, 1408)), [a], [w],
               [(0, 0, 0)], 1, epilogue, [f32, f32], extras=[resid], vecs=[gamma, beta])


def _ffn_dact(name, dr, wd, g, u):
    s = dr.shape[0]

    def epilogue(accs, extras, vecs):
        da = 0.5 * accs[0]
        gg, uu = extras
        sg = jax.nn.sigmoid(gg)
        return [da * uu * (sg * (1.0 + gg * (1.0 - sg))), da * (gg * sg)]

    return _mm(name, "nt", (s, D_FF, D_MODEL), (_tile(s, 512), _tile(D_FF, 1408), D_MODEL), [dr], [wd],
               [(0, 0, 0)], 1, epilogue, [bf16, bf16], extras=[g, u])


def _mm2(name, mode, dims, a0, b0, a1, b1, add=None, add_coef=1.0, separate=False, tiles=None):
    m, n, k = dims
    tiles = tiles or (_tile(m, 512), _tile(n, 1024), _tile(k, 1024))

    def epilogue(accs, extras, vecs):
        if separate:
            return list(accs)
        r = accs[0]
        if extras:
            r = r + add_coef * extras[0]
        return [r]

    a_list = [a0] if a1 is None else [a0, a1]
    b_list = [b0] if b1 is None else [b0, b1]
    pairs = [(0, 0, 0), (len(a_list) - 1, len(b_list) - 1, 1 if separate else 0)]
    return _mm(name, mode, dims, tiles, a_list, b_list, pairs, 2 if separate else 1, epilogue,
               [f32, f32] if separate else [f32], extras=[] if add is None else [add])


def _row_call(name, body, s, ins, params, outs, accs):
    tm = ROW_TILE
    in_specs = [pl.BlockSpec((tm, a.shape[1]), lambda i: (i, 0)) for a in ins]
    in_specs += [pl.BlockSpec(p.shape, lambda i, nd=p.ndim: (0,) * nd) for p in params]
    out_specs = [pl.BlockSpec((tm, o.shape[1]), lambda i: (i, 0)) for o in outs]
    out_specs += [pl.BlockSpec(a.shape, lambda i, nd=len(a.shape): (0,) * nd) for a in accs]
    return pl.pallas_call(
        body,
        name=name,
        grid=(s // tm,),
        in_specs=in_specs,
        out_specs=out_specs,
        out_shape=list(outs) + list(accs),
        compiler_params=_params("arbitrary"),
    )(*ins, *params)


def _zero_at_first(refs):
    @pl.when(pl.program_id(0) == 0)
    def _():
        for r in refs:
            r[...] = jnp.zeros_like(r)


def _ln_bwd(name, r, dh, gamma):
    s = r.shape[0]

    def body(r_ref, dh_ref, g_ref, dr_ref, dg_ref, db_ref):
        _zero_at_first([dg_ref, db_ref])
        rr = r_ref[...]
        dy = dh_ref[...]
        mu = jnp.mean(rr, axis=-1, keepdims=True)
        xc = rr - mu
        rstd = lax.rsqrt(jnp.mean(xc * xc, axis=-1, keepdims=True) + LN_EPS)
        xhat = xc * rstd
        dxh = dy * g_ref[...]
        dr_ref[...] = rstd * (dxh - jnp.mean(dxh, axis=-1, keepdims=True)
                              - xhat * jnp.mean(dxh * xhat, axis=-1, keepdims=True))
        dg_ref[...] += jnp.sum(dy * xhat, axis=0, keepdims=True)
        db_ref[...] += jnp.sum(dy, axis=0, keepdims=True)

    return _row_call(name, body, s, [r, dh], [gamma], [_sds((s, D_MODEL))], [_sds((1, D_MODEL)), _sds((1, D_MODEL))])


def _loss_head(name, y, target):
    s = y.shape[0]

    def body(y_ref, t_ref, dy_ref, l_ref):
        _zero_at_first([l_ref])
        e = y_ref[...] - t_ref[...]
        dy_ref[...] = e / D_MODEL
        l_ref[...] += 0.5 * jnp.sum(jnp.mean(e * e, axis=-1, keepdims=True), axis=0, keepdims=True)

    return _row_call(name, body, s, [y, target], [], [_sds((s, D_MODEL))], [_sds((1, 128))])


def _expm1(x):
    series = x * (1.0 + x / 2.0 * (1.0 + x / 3.0 * (1.0 + x / 4.0 * (1.0 + x / 5.0 * (1.0 + x / 6.0 * (1.0 + x / 7.0))))))
    return jnp.where(jnp.abs(x) < 0.25, series, jnp.exp(x) - 1.0)


def _gates_fn(xa, wa, wx, ba, bx, lam, tap_a, tap_x):
    xb = xa.astype(bf16)
    r = jax.nn.sigmoid(jnp.dot(xb, wa, preferred_element_type=f32) + ba + tap_a)
    i = jax.nn.sigmoid(jnp.dot(xb, wx, preferred_element_type=f32) + bx + tap_x)
    log_a = -RG_C * r * jax.nn.softplus(-lam)
    a = jnp.exp(log_a)
    gated = jnp.sqrt(-_expm1(2.0 * log_a)) * (i * xa)
    return a, gated


def _rg_gates(name, xa, wa, wx, ba, bx, lam):
    s = xa.shape[0]

    def body(xa_ref, wa_ref, wx_ref, ba_ref, bx_ref, lam_ref, a_ref, g_ref):
        a, g = _gates_fn(xa_ref[...], wa_ref[...], wx_ref[...], ba_ref[...], bx_ref[...], lam_ref[...], 0.0, 0.0)
        a_ref[...] = a
        g_ref[...] = g

    return _row_call(name, body, s, [xa], [wa, wx, ba, bx, lam], [_sds((s, D_A)), _sds((s, D_A))], [])


def _rg_gates_bwd(name, xa, ga, h_prev, wa, wx, ba, bx, lam):
    s = xa.shape[0]

    def body(xa_ref, ga_ref, hp_ref, wa_ref, wx_ref, ba_ref, bx_ref, lam_ref,
             dxa_ref, dwa_ref, dwx_ref, dba_ref, dbx_ref, dlam_ref):
        _zero_at_first([dwa_ref, dwx_ref, dba_ref, dbx_ref, dlam_ref])
        xa_v = xa_ref[...]
        zero = jnp.zeros((xa_v.shape[0], D_A), f32)
        fn = lambda x, ba_, bx_, lam_, ta, tx: _gates_fn(x, wa_ref[...], wx_ref[...], ba_, bx_, lam_, ta, tx)
        _, vjp = jax.vjp(fn, xa_v, ba_ref[...], bx_ref[...], lam_ref[...], zero, zero)
        gav = ga_ref[...]
        dxa, dba, dbx, dlam, dta, dtx = vjp((gav * hp_ref[...], gav))
        dxa_ref[...] = dxa
        xb = xa_v.astype(bf16)
        dwa_ref[...] += lax.dot_general(xb, dta.astype(bf16), _DN["tn"], preferred_element_type=f32)
        dwx_ref[...] += lax.dot_general(xb, dtx.astype(bf16), _DN["tn"], preferred_element_type=f32)
        dba_ref[...] += dba
        dbx_ref[...] += dbx
        dlam_ref[...] += dlam

    return _row_call(name, body, s, [xa, ga, h_prev], [wa, wx, ba, bx, lam], [_sds((s, D_A))],
                     [_sds((D_A, D_A)), _sds((D_A, D_A)), _sds((1, D_A)), _sds((1, D_A)), _sds((1, D_A))])


def _rms(v, g):
    return v * lax.rsqrt(jnp.mean(v * v, axis=-1, keepdims=True) + RMS_EPS) * g


def _mix_out_fn(ag, ha, ob, hre, him, cu, d, gn, tap_y, tap_gl, wcr, wci, wglu):
    out_a = jax.nn.gelu(ag) * ha
    y = (jnp.dot(hre.astype(bf16), wcr, preferred_element_type=f32)
         + jnp.dot(him.astype(bf16), wci, preferred_element_type=f32) + d * cu + tap_y)
    y2 = jax.nn.gelu(y)
    gl = jnp.dot(y2.astype(bf16), wglu, preferred_element_type=f32) + tap_gl
    out_c = y2 * jax.nn.sigmoid(gl)
    o = jnp.concatenate([_rms(out_a, gn[:, :D_A]), _rms(ob, gn[:, D_A:D_A + D_B]), _rms(out_c, gn[:, D_A + D_B:])],
                        axis=-1)
    return o, y2


def _mix_out(name, ag, ha, ob, hre, him, cu, d, gn, wcr, wci, wglu):
    s = ag.shape[0]

    def body(ag_ref, ha_ref, ob_ref, hre_ref, him_ref, cu_ref, d_ref, gn_ref, wcr_ref, wci_ref, wglu_ref, o_ref):
        o, _ = _mix_out_fn(ag_ref[...], ha_ref[...], ob_ref[...], hre_ref[...], him_ref[...], cu_ref[...], d_ref[...],
                           gn_ref[...], 0.0, 0.0, wcr_ref[...], wci_ref[...], wglu_ref[...])
        o_ref[...] = o.astype(o_ref.dtype)

    return _row_call(name, body, s, [ag, ha, ob, hre, him, cu], [d, gn, wcr, wci, wglu], [_sds((s, D_MODEL), bf16)], [])[0]


def _mix_out_bwd(name, do, ag, ha, ob, hre, him, cu, d, gn, wcr, wci, wglu):
    s = ag.shape[0]

    def body(do_ref, ag_ref, ha_ref, ob_ref, hre_ref, him_ref, cu_ref, d_ref, gn_ref, wcr_ref, wci_ref, wglu_ref,
             dag_ref, dha_ref, dob_ref, dhre_ref, dhim_ref, dcu_ref, dwcr_ref, dwci_ref, dwglu_ref, dd_ref, dgn_ref):
        _zero_at_first([dwcr_ref, dwci_ref, dwglu_ref, dd_ref, dgn_ref])
        tm = ag_ref.shape[0]
        zero = jnp.zeros((tm, D_C), f32)
        hre_v, him_v = hre_ref[...], him_ref[...]
        fn = lambda *a: _mix_out_fn(*a, wcr_ref[...], wci_ref[...], wglu_ref[...])
        _, vjp, y2 = jax.vjp(fn, ag_ref[...], ha_ref[...], ob_ref[...], hre_v, him_v, cu_ref[...], d_ref[...],
                             gn_ref[...], zero, zero, has_aux=True)
        dag, dha, dob, dhre, dhim, dcu, dd, dgn, dy, dgl = vjp(do_ref[...])
        dag_ref[...] = dag
        dha_ref[...] = dha
        dob_ref[...] = dob
        dhre_ref[...] = dhre
        dhim_ref[...] = dhim
        dcu_ref[...] = dcu
        dyb = dy.astype(bf16)
        dwcr_ref[...] += lax.dot_general(hre_v.astype(bf16), dyb, _DN["tn"], preferred_element_type=f32)
        dwci_ref[...] += lax.dot_general(him_v.astype(bf16), dyb, _DN["tn"], preferred_element_type=f32)
        dwglu_ref[...] += lax.dot_general(y2.astype(bf16), dgl.astype(bf16), _DN["tn"], preferred_element_type=f32)
        dd_ref[...] += dd
        dgn_ref[...] += dgn

    outs = [_sds((s, D_A)), _sds((s, D_A)), _sds((s, D_B)), _sds((s, S5_LANES)), _sds((s, S5_LANES)), _sds((s, D_C))]
    accs = [_sds((S5_LANES, D_C)), _sds((S5_LANES, D_C)), _sds((D_C, D_C)), _sds((1, D_C)), _sds((1, D_MODEL))]
    return _row_call(name, body, s, [do, ag, ha, ob, hre, him, cu], [d, gn, wcr, wci, wglu], outs, accs)


def _log_f(name, f, bf):
    s = f.shape[0]

    def body(f_ref, b_ref, o_ref):
        o_ref[...] = jax.nn.log_sigmoid(f_ref[...] + b_ref[...])

    return _row_call(name, body, s, [f], [bf], [_sds((s, 128))], [])[0]


def _log_f_bwd(name, dlf, f, bf):
    s = f.shape[0]

    def body(dl_ref, f_ref, b_ref, df_ref, db_ref):
        _zero_at_first([db_ref])
        df = dl_ref[...] * jax.nn.sigmoid(-(f_ref[...] + b_ref[...]))
        df_ref[...] = df
        db_ref[...] += jnp.sum(df, axis=0, keepdims=True)

    return _row_call(name, body, s, [dlf, f], [bf], [_sds((s, 128))], [_sds((1, 128))])


def _s5_decay_grad(name, hp_re, hp_im, g_re, g_im):
    s = g_re.shape[0]

    def body(hr_ref, hi_ref, gr_ref, gi_ref, dr_ref, di_ref):
        _zero_at_first([dr_ref, di_ref])
        hr, hi, gr, gi = hr_ref[...], hi_ref[...], gr_ref[...], gi_ref[...]
        dr_ref[...] += jnp.sum(hr * gr + hi * gi, axis=0, keepdims=True)
        di_ref[...] += jnp.sum(hr * gi - hi * gr, axis=0, keepdims=True)

    return _row_call(name, body, s, [hp_re, hp_im, g_re, g_im], [], [], [_sds((1, S5_LANES)), _sds((1, S5_LANES))])


def _conv_fwd(name, ax, w, b):
    s = ax.shape[0]
    tm = ROW_TILE

    def body(x_ref, halo_ref, w_ref, b_ref, o_ref):
        i = pl.program_id(0)
        x = x_ref[...]
        halo = jnp.where(i == 0, 0.0, halo_ref[...])
        ext = jnp.concatenate([halo, x], axis=0)
        acc = b_ref[...] + w_ref[3:4, :] * x
        for k in range(CONV_WIDTH - 1):
            acc = acc + w_ref[k:k + 1, :] * pltpu.roll(ext, CONV_WIDTH - 1 - k, 0)[8:, :]
        o_ref[...] = acc

    return pl.pallas_call(
        body,
        name=name,
        grid=(s // tm,),
        in_specs=[pl.BlockSpec((tm, D_A), lambda i: (i, 0)),
                  pl.BlockSpec((8, D_A), lambda i: (jnp.maximum(i * (tm // 8) - 1, 0), 0)),
                  pl.BlockSpec((CONV_WIDTH, D_A), lambda i: (0, 0)),
                  pl.BlockSpec((1, D_A), lambda i: (0, 0))],
        out_specs=pl.BlockSpec((tm, D_A), lambda i: (i, 0)),
        out_shape=_sds((s, D_A)),
        compiler_params=_params("arbitrary"),
    )(ax, ax, w, b)


def _conv_bwd(name, dxa, ax, w):
    s = ax.shape[0]
    tm = ROW_TILE
    nblk = s // tm

    def body(dx_ref, dnext_ref, x_ref, halo_ref, w_ref, dax_ref, dw_ref):
        i = pl.program_id(0)
        _zero_at_first([dw_ref])
        dx = dx_ref[...]
        dnext = jnp.where(i == nblk - 1, 0.0, dnext_ref[...])
        dext = jnp.concatenate([dx, dnext], axis=0)
        x = x_ref[...]
        halo = jnp.where(i == 0, 0.0, halo_ref[...])
        ext = jnp.concatenate([halo, x], axis=0)
        acc = w_ref[3:4, :] * dx
        dw_ref[3:4, :] += jnp.sum(dx * x, axis=0, keepdims=True)
        for k in range(CONV_WIDTH - 1):
            sh = CONV_WIDTH - 1 - k
            acc = acc + w_ref[k:k + 1, :] * pltpu.roll(dext, tm + 8 - sh, 0)[:tm, :]
            dw_ref[k:k + 1, :] += jnp.sum(dx * pltpu.roll(ext, sh, 0)[8:, :], axis=0, keepdims=True)
        dw_ref[4:5, :] += jnp.sum(dx, axis=0, keepdims=True)
        dax_ref[...] = acc

    return pl.pallas_call(
        body,
        name=name,
        grid=(nblk,),
        in_specs=[pl.BlockSpec((tm, D_A), lambda i: (i, 0)),
                  pl.BlockSpec((8, D_A), lambda i: (jnp.minimum((i + 1) * (tm // 8), s // 8 - 1), 0)),
                  pl.BlockSpec((tm, D_A), lambda i: (i, 0)),
                  pl.BlockSpec((8, D_A), lambda i: (jnp.maximum(i * (tm // 8) - 1, 0), 0)),
                  pl.BlockSpec((CONV_WIDTH, D_A), lambda i: (0, 0))],
        out_specs=[pl.BlockSpec((tm, D_A), lambda i: (i, 0)), pl.BlockSpec((8, D_A), lambda i: (0, 0))],
        out_shape=[_sds((s, D_A)), _sds((8, D_A))],
        compiler_params=_params("arbitrary"),
    )(dxa, dxa, ax, ax, w)


def _lin_scan(name, a, b, reverse):
    s, c = a.shape
    L = SCAN_CHUNK
    n = s // L
    cb = 128

    def body(a_ref, b_ref, h_ref, p_ref, hl_ref, pl_ref, car_ref):
        def pos(j):
            return (L - 1 - j) if reverse else j

        def step(jj, carry):
            h, p = carry
            j = pos(jj)
            aj = a_ref[:, j, :]
            h = aj * h + b_ref[:, j, :]
            p = aj * p
            h_ref[:, j, :] = h
            p_ref[:, j, :] = p
            return h, p

        h_last, p_last = lax.fori_loop(0, L, step, (jnp.zeros((n, cb), f32), jnp.ones((n, cb), f32)))
        hl_ref[...] = h_last
        pl_ref[...] = p_last

        def chunk_step(cc, carry):
            ch = (n - 1 - cc) if reverse else cc
            car_ref[pl.ds(ch, 1), :] = carry
            return hl_ref[pl.ds(ch, 1), :] + pl_ref[pl.ds(ch, 1), :] * carry

        lax.fori_loop(0, n, chunk_step, jnp.zeros((1, cb), f32))
        car = car_ref[...]

        def fix(j, _):
            h_ref[:, j, :] = h_ref[:, j, :] + p_ref[:, j, :] * car
            return 0

        lax.fori_loop(0, L, fix, 0)

    spec = pl.BlockSpec((n, L, cb), lambda i: (0, 0, i))
    (out,) = _call(
        body,
        name=name,
        grid=(c // cb,),
        in_specs=[spec, spec],
        out_specs=[spec],
        out_shape=[_sds((n, L, c))],
        scratch_shapes=[pltpu.VMEM((n, L, cb), f32), pltpu.VMEM((n, cb), f32), pltpu.VMEM((n, cb), f32),
                        pltpu.VMEM((n, cb), f32)],
        compiler_params=_params("arbitrary"),
    )(a.reshape(n, L, c), b.reshape(n, L, c))
    return out.reshape(s, c)


def _s5_scan(name, b_re, b_im, a_re, a_im, reverse):
    s, c = b_re.shape
    L = SCAN_CHUNK
    n = s // L
    cb = 128

    def body(br_ref, bi_ref, ar_ref, ai_ref, hr_ref, hi_ref, pr_ref, pi_ref, hlr_ref, hli_ref, cr_ref, ci_ref):
        ar = ar_ref[...]
        ai = ai_ref[...]

        def pos(j):
            return (L - 1 - j) if reverse else j

        def step(jj, carry):
            hr, hi, pr, pi = carry
            j = pos(jj)
            hr, hi = ar * hr - ai * hi + br_ref[:, j, :], ar * hi + ai * hr + bi_ref[:, j, :]
            pr, pi = ar * pr - ai * pi, ar * pi + ai * pr
            hr_ref[:, j, :] = hr
            hi_ref[:, j, :] = hi
            pr_ref[pl.ds(j, 1), :] = pr
            pi_ref[pl.ds(j, 1), :] = pi
            return hr, hi, pr, pi

        zero = jnp.zeros((n, cb), f32)
        hr_l, hi_l, pr_l, pi_l = lax.fori_loop(0, L, step, (zero, zero, jnp.ones((1, cb), f32), jnp.zeros((1, cb), f32)))
        hlr_ref[...] = hr_l
        hli_ref[...] = hi_l

        def chunk_step(cc, carry):
            car_r, car_i = carry
            ch = (n - 1 - cc) if reverse else cc
            cr_ref[pl.ds(ch, 1), :] = car_r
            ci_ref[pl.ds(ch, 1), :] = car_i
            return (hlr_ref[pl.ds(ch, 1), :] + pr_l * car_r - pi_l * car_i,
                    hli_ref[pl.ds(ch, 1), :] + pr_l * car_i + pi_l * car_r)

        lax.fori_loop(0, n, chunk_step, (jnp.zeros((1, cb), f32), jnp.zeros((1, cb), f32)))
        car_r = cr_ref[...]
        car_i = ci_ref[...]

        def fix(j, _):
            pr = pr_ref[pl.ds(j, 1), :]
            pi = pi_ref[pl.ds(j, 1), :]
            hr_ref[:, j, :] = hr_ref[:, j, :] + pr * car_r - pi * car_i
            hi_ref[:, j, :] = hi_ref[:, j, :] + pr * car_i + pi * car_r
            return 0

        lax.fori_loop(0, L, fix, 0)

    spec = pl.BlockSpec((n, L, cb), lambda i: (0, 0, i))
    vspec = pl.BlockSpec((1, cb), lambda i: (0, i))
    hr, hi = _call(
        body,
        name=name,
        grid=(c // cb,),
        in_specs=[spec, spec, vspec, vspec],
        out_specs=[spec, spec],
        out_shape=[_sds((n, L, c)), _sds((n, L, c))],
        scratch_shapes=[pltpu.VMEM((L, cb), f32), pltpu.VMEM((L, cb), f32), pltpu.VMEM((n, cb), f32),
                        pltpu.VMEM((n, cb), f32), pltpu.VMEM((n, cb), f32), pltpu.VMEM((n, cb), f32)],
        compiler_params=_params("arbitrary"),
    )(b_re.reshape(n, L, c), b_im.reshape(n, L, c), a_re, a_im)
    return hr.reshape(s, c), hi.reshape(s, c)


def _causal_mask(t):
    row = lax.broadcasted_iota(jnp.int32, (t, t), 0)
    col = lax.broadcasted_iota(jnp.int32, (t, t), 1)
    return row >= col


def _attn_fwd(name, q, k, v, cq, ck):
    h, s, dh = q.shape
    t = ATT_TILE
    nq = s // t

    def body(q_ref, k_ref, v_ref, cq_ref, ck_ref, o_ref, lse_ref):
        qi = pl.program_id(1)
        qb = q_ref[...].astype(bf16)
        cqv = cq_ref[...]

        def block(kb, carry, masked):
            m, l, acc = carry
            ks = pl.multiple_of(kb * t, t)
            kk = k_ref[pl.ds(ks, t), :].astype(bf16)
            vv = v_ref[pl.ds(ks, t), :].astype(bf16)
            sc = lax.dot_general(qb, kk, _DN["nt"], preferred_element_type=f32) * ATT_SCALE + (cqv - ck_ref[kb])
            if masked:
                sc = jnp.where(_causal_mask(t), sc, -jnp.inf)
            mn = jnp.maximum(m, jnp.max(sc, axis=1, keepdims=True))
            p = jnp.exp(sc - mn)
            al = jnp.exp(m - mn)
            l = al * l + jnp.sum(p, axis=1, keepdims=True)
            acc = al * acc + jnp.dot(p.astype(bf16), vv, preferred_element_type=f32)
            return mn, l, acc

        init = (jnp.full((t, 1), -jnp.inf, f32), jnp.zeros((t, 1), f32), jnp.zeros((t, dh), f32))
        carry = lax.fori_loop(0, qi, lambda kb, c: block(kb, c, False), init)
        m, l, acc = block(qi, carry, True)
        o_ref[...] = acc / l
        lse_ref[...] = m + jnp.log(l)

    return pl.pallas_call(
        body,
        name=name,
        grid=(h, nq),
        in_specs=[pl.BlockSpec((None, t, dh), lambda hh, i: (hh, i, 0)),
                  pl.BlockSpec((None, s, dh), lambda hh, i: (hh, 0, 0)),
                  pl.BlockSpec((None, s, dh), lambda hh, i: (hh, 0, 0)),
                  pl.BlockSpec((None, t, 1), lambda hh, i: (hh, i, 0)),
                  pl.BlockSpec((None, nq, 1, t), lambda hh, i: (hh, 0, 0, 0))],
        out_specs=[pl.BlockSpec((None, t, dh), lambda hh, i: (hh, i, 0)),
                   pl.BlockSpec((None, t, 1), lambda hh, i: (hh, i, 0))],
        out_shape=[_sds((h, s, dh)), _sds((h, s, 1))],
        compiler_params=_params("parallel", "arbitrary"),
    )(q, k, v, cq, ck)


def _attn_bwd_dq(name, q, k, v, cq, ck, o, do, lse):
    h, s, dh = q.shape
    t = ATT_TILE
    nq = s // t

    def body(q_ref, k_ref, v_ref, cq_ref, ck_ref, o_ref, do_ref, lse_ref, dq_ref, dl_ref):
        qi = pl.program_id(1)
        qb = q_ref[...].astype(bf16)
        cqv = cq_ref[...]
        dov = do_ref[...]
        dob = dov.astype(bf16)
        delta = jnp.sum(dov * o_ref[...], axis=1, keepdims=True)
        lse_v = lse_ref[...]

        def block(kb, carry, masked):
            dq, psum = carry
            ks = pl.multiple_of(kb * t, t)
            kk = k_ref[pl.ds(ks, t), :].astype(bf16)
            vv = v_ref[pl.ds(ks, t), :].astype(bf16)
            sc = lax.dot_general(qb, kk, _DN["nt"], preferred_element_type=f32) * ATT_SCALE + (cqv - ck_ref[kb])
            p = jnp.exp(sc - lse_v)
            if masked:
                p = jnp.where(_causal_mask(t), p, 0.0)
            dp = lax.dot_general(dob, vv, _DN["nt"], preferred_element_type=f32)
            ds = p * (dp - delta)
            return (dq + jnp.dot(ds.astype(bf16), kk, preferred_element_type=f32),
                    psum + jnp.sum(p * dp, axis=1, keepdims=True))

        carry = lax.fori_loop(0, qi, lambda kb, c: block(kb, c, False), (jnp.zeros((t, dh), f32), jnp.zeros((t, 1), f32)))
        dq, psum = block(qi, carry, True)
        dq_ref[...] = dq * ATT_SCALE
        dl_ref[...] = psum

    qspec = pl.BlockSpec((None, t, dh), lambda hh, i: (hh, i, 0))
    fspec = pl.BlockSpec((None, s, dh), lambda hh, i: (hh, 0, 0))
    cspec = pl.BlockSpec((None, t, 1), lambda hh, i: (hh, i, 0))
    return pl.pallas_call(
        body,
        name=name,
        grid=(h, nq),
        in_specs=[qspec, fspec, fspec, cspec, pl.BlockSpec((None, nq, 1, t), lambda hh, i: (hh, 0, 0, 0)),
                  qspec, qspec, cspec],
        out_specs=[qspec, cspec],
        out_shape=[_sds((h, s, dh)), _sds((h, s, 1))],
        compiler_params=_params("parallel", "arbitrary"),
    )(q, k, v, cq, ck, o, do, lse)


def _attn_bwd_dkv(name, q, k, v, cq, ck, do, lse, delta):
    h, s, dh = q.shape
    t = ATT_TILE
    nq = s // t

    def body(q_ref, k_ref, v_ref, cq_ref, ck_ref, do_ref, lse_ref, dl_ref, dk_ref, dv_ref, dck_ref):
        kj = pl.program_id(1)
        kk = k_ref[...].astype(bf16)
        vv = v_ref[...].astype(bf16)
        ckv = ck_ref[...]

        def block(qi, carry, masked):
            dk, dv, dcs = carry
            qs = pl.multiple_of(qi * t, t)
            qq = q_ref[pl.ds(qs, t), :].astype(bf16)
            dob = do_ref[pl.ds(qs, t), :].astype(bf16)
            sc = (lax.dot_general(qq, kk, _DN["nt"], preferred_element_type=f32) * ATT_SCALE
                  + (cq_ref[pl.ds(qs, t), :] - ckv))
            p = jnp.exp(sc - lse_ref[pl.ds(qs, t), :])
            if masked:
                p = jnp.where(_causal_mask(t), p, 0.0)
            dv = dv + lax.dot_general(p.astype(bf16), dob, _DN["tn"], preferred_element_type=f32)
            dp = lax.dot_general(dob, vv, _DN["nt"], preferred_element_type=f32)
            ds = p * (dp - dl_ref[pl.ds(qs, t), :])
            dk = dk + lax.dot_general(ds.astype(bf16), qq, _DN["tn"], preferred_element_type=f32)
            return dk, dv, dcs + jnp.sum(ds, axis=0, keepdims=True)

        init = (jnp.zeros((t, dh), f32), jnp.zeros((t, dh), f32), jnp.zeros((1, t), f32))
        carry = block(kj, init, True)
        dk, dv, dcs = lax.fori_loop(kj + 1, nq, lambda qi, c: block(qi, c, False), carry)
        dk_ref[...] = dk * ATT_SCALE
        dv_ref[...] = dv
        dck_ref[...] = -dcs

    kspec = pl.BlockSpec((None, t, dh), lambda hh, j: (hh, j, 0))
    fspec = pl.BlockSpec((None, s, dh), lambda hh, j: (hh, 0, 0))
    fcol = pl.BlockSpec((None, s, 1), lambda hh, j: (hh, 0, 0))
    crow = pl.BlockSpec((None, None, 1, t), lambda hh, j: (hh, j, 0, 0))
    return pl.pallas_call(
        body,
        name=name,
        grid=(h, nq),
        in_specs=[fspec, kspec, kspec, fcol, crow, fspec, fcol, fcol],
        out_specs=[kspec, kspec, crow],
        out_shape=[_sds((h, s, dh)), _sds((h, s, dh)), _sds((h, nq, 1, t))],
        compiler_params=_params("parallel", "arbitrary"),
    )(q, k, v, cq, ck, do, lse, delta)


ATT_FEAT = 128
ATT_TQ = 1024
ATT_TK = 256


def _att_tiles(s):
    tq = min(ATT_TQ, s)
    return tq, ATT_TK, tq // ATT_TK


def _keys_le_queries(tk, tq, k0, q0):
    row = lax.broadcasted_iota(jnp.int32, (tk, tq), 0) + k0
    col = lax.broadcasted_iota(jnp.int32, (tk, tq), 1) + q0
    return row <= col


def _attn_fwd_t(name, qt_aug, k_aug, vt):
    h, _, s = qt_aug.shape
    tq, tk, ratio = _att_tiles(s)

    def body(qt_ref, k_ref, vt_ref, o_ref, lse_ref):
        qi = pl.program_id(1)
        qt = qt_ref[...]

        def block(kb, carry, masked):
            m, l, acc = carry
            ks = pl.multiple_of(kb * tk, tk)
            st = jnp.dot(k_ref[pl.ds(ks, tk), :], qt, preferred_element_type=f32)
            if masked:
                st = jnp.where(_keys_le_queries(tk, tq, ks, qi * tq), st, -jnp.inf)
            mn = jnp.maximum(m, jnp.max(st, axis=0, keepdims=True))
            p = jnp.exp(st - mn)
            al = jnp.exp(m - mn)
            l = al * l + jnp.sum(p, axis=0, keepdims=True)
            acc = al * acc + jnp.dot(vt_ref[kb], p.astype(bf16), preferred_element_type=f32)
            return mn, l, acc

        init = (jnp.full((1, tq), -jnp.inf, f32), jnp.zeros((1, tq), f32), jnp.zeros((HEAD_DIM, tq), f32))
        first = lax.fori_loop(0, qi * ratio, lambda kb, c: block(kb, c, False), init)
        m, l, acc = lax.fori_loop(qi * ratio, (qi + 1) * ratio, lambda kb, c: block(kb, c, True), first)
        o_ref[...] = acc / l
        lse_ref[...] = m + jnp.log(l)

    return _call(
        body,
        name=name,
        grid=(h, s // tq),
        in_specs=[pl.BlockSpec((None, ATT_FEAT, tq), lambda hh, i: (hh, 0, i)),
                  pl.BlockSpec((None, s, ATT_FEAT), lambda hh, i: (hh, 0, 0)),
                  pl.BlockSpec((None, s // tk, HEAD_DIM, tk), lambda hh, i: (hh, 0, 0, 0))],
        out_specs=[pl.BlockSpec((None, HEAD_DIM, tq), lambda hh, i: (hh, 0, i)),
                   pl.BlockSpec((None, 1, tq), lambda hh, i: (hh, 0, i))],
        out_shape=[_sds((h, HEAD_DIM, s)), _sds((h, 1, s))],
        compiler_params=_params("parallel", "arbitrary"),
    )(qt_aug, k_aug, vt)


def _attn_bwd_dq_t(name, qt_aug, k_aug, v, kt, ot, dot_, lse):
    h, _, s = qt_aug.shape
    tq, tk, ratio = _att_tiles(s)

    def body(qt_ref, k_ref, v_ref, kt_ref, o_ref, do_ref, lse_ref, dq_ref, dl_ref):
        qi = pl.program_id(1)
        qt = qt_ref[...]
        dov = do_ref[...]
        dob = dov.astype(bf16)
        delta = jnp.sum(dov * o_ref[...], axis=0, keepdims=True)
        lse_v = lse_ref[...]

        def block(kb, carry, masked):
            dq, psum = carry
            ks = pl.multiple_of(kb * tk, tk)
            st = jnp.dot(k_ref[pl.ds(ks, tk), :], qt, preferred_element_type=f32)
            p = jnp.exp(st - lse_v)
            if masked:
                p = jnp.where(_keys_le_queries(tk, tq, ks, qi * tq), p, 0.0)
            dp = jnp.dot(v_ref[pl.ds(ks, tk), :], dob, preferred_element_type=f32)
            ds = p * (dp - delta)
            return (dq + jnp.dot(kt_ref[kb], ds.astype(bf16), preferred_element_type=f32),
                    psum + jnp.sum(p * dp, axis=0, keepdims=True))

        carry = lax.fori_loop(0, qi * ratio, lambda kb, c: block(kb, c, False),
                              (jnp.zeros((HEAD_DIM, tq), f32), jnp.zeros((1, tq), f32)))
        dq, psum = lax.fori_loop(qi * ratio, (qi + 1) * ratio, lambda kb, c: block(kb, c, True), carry)
        dq_ref[...] = dq * ATT_SCALE
        dl_ref[...] = psum

    qspec = pl.BlockSpec((None, HEAD_DIM, tq), lambda hh, i: (hh, 0, i))
    rspec = pl.BlockSpec((None, 1, tq), lambda hh, i: (hh, 0, i))
    return _call(
        body,
        name=name,
        grid=(h, s // tq),
        in_specs=[pl.BlockSpec((None, ATT_FEAT, tq), lambda hh, i: (hh, 0, i)),
                  pl.BlockSpec((None, s, ATT_FEAT), lambda hh, i: (hh, 0, 0)),
                  pl.BlockSpec((None, s, HEAD_DIM), lambda hh, i: (hh, 0, 0)),
                  pl.BlockSpec((None, s // tk, HEAD_DIM, tk), lambda hh, i: (hh, 0, 0, 0)),
                  qspec, qspec, rspec],
        out_specs=[qspec, rspec],
        out_shape=[_sds((h, HEAD_DIM, s)), _sds((h, 1, s))],
        compiler_params=_params("parallel", "arbitrary"),
    )(qt_aug, k_aug, v, kt, ot, dot_, lse)


def _attn_bwd_dkv_t(name, qt_blocks, k_aug, v, qh, do, dot_blocks, lse, delta):
    h, s, _ = k_aug.shape
    tq, tk, ratio = _att_tiles(s)
    nq = s // tq

    def body(qt_ref, k_ref, v_ref, q_ref, do_ref, dot_ref, lse_ref, dl_ref, dk_ref, dv_ref, dck_ref, dsum_ref):
        kj = pl.program_id(1)
        kk = k_ref[...]
        vv = v_ref[...]
        dsum_ref[...] = jnp.zeros_like(dsum_ref)

        def block(qi, carry, masked):
            dk, dv = carry
            qs = pl.multiple_of(qi * tq, tq)
            st = jnp.dot(kk, qt_ref[qi], preferred_element_type=f32)
            p = jnp.exp(st - lse_ref[qi])
            if masked:
                p = jnp.where(_keys_le_queries(tk, tq, kj * tk, qs), p, 0.0)
            dv = dv + jnp.dot(p.astype(bf16), do_ref[pl.ds(qs, tq), :], preferred_element_type=f32)
            dp = jnp.dot(vv, dot_ref[qi], preferred_element_type=f32)
            ds = p * (dp - dl_ref[qi])
            dsum_ref[...] += ds
            dk = dk + jnp.dot(ds.astype(bf16), q_ref[pl.ds(qs, tq), :], preferred_element_type=f32)
            return dk, dv

        first = kj // ratio
        carry = block(first, (jnp.zeros((tk, HEAD_DIM), f32), jnp.zeros((tk, HEAD_DIM), f32)), True)
        dk, dv = lax.fori_loop(first + 1, nq, lambda qi, c: block(qi, c, False), carry)
        dk_ref[...] = dk
        dv_ref[...] = dv
        dck_ref[...] = -jnp.sum(dsum_ref[...], axis=1, keepdims=True)

    full = lambda shape: pl.BlockSpec((None,) + shape, lambda hh, j: (hh,) + (0,) * len(shape))
    kspec = pl.BlockSpec((None, tk, HEAD_DIM), lambda hh, j: (hh, j, 0))
    return _call(
        body,
        name=name,
        grid=(h, s // tk),
        in_specs=[full((nq, ATT_FEAT, tq)),
                  pl.BlockSpec((None, tk, ATT_FEAT), lambda hh, j: (hh, j, 0)),
                  kspec, full((s, HEAD_DIM)), full((s, HEAD_DIM)), full((nq, HEAD_DIM, tq)),
                  full((nq, 1, tq)), full((nq, 1, tq))],
        out_specs=[kspec, kspec, pl.BlockSpec((None, tk, 1), lambda hh, j: (hh, j, 0))],
        out_shape=[_sds((h, s, HEAD_DIM)), _sds((h, s, HEAD_DIM)), _sds((h, s, 1))],
        scratch_shapes=[pltpu.VMEM((tk, tq), f32)],
        compiler_params=_params("parallel", "arbitrary"),
    )(qt_blocks, k_aug, v, qh, do, dot_blocks, lse, delta)


def _split3(c):
    hi = lax.reduce_precision(c, 8, 7)
    r = c - hi
    mid = lax.reduce_precision(r, 8, 7)
    lo = lax.reduce_precision(r - mid, 8, 7)
    return hi.astype(bf16), mid.astype(bf16), lo.astype(bf16)


def _attn_operands(q, k, v, c6):
    s = q.shape[0]
    tq, tk, _ = _att_tiles(s)
    qh = _heads(q * ATT_SCALE).astype(bf16)
    kh = _heads(k).astype(bf16)
    vh = _heads(v).astype(bf16)
    parts = _split3(c6)
    ones = jnp.ones((N_HEADS, s, 3), bf16)
    cpos = jnp.stack(parts, axis=-1)
    pad = jnp.zeros((N_HEADS, s, ATT_FEAT - HEAD_DIM - 6), bf16)
    q_aug = jnp.concatenate([qh, cpos, ones, pad], axis=-1)
    k_aug = jnp.concatenate([kh, ones, -cpos, pad], axis=-1)

    def blocks_t(a, t):
        return a.reshape(N_HEADS, s // t, t, a.shape[-1]).transpose(0, 1, 3, 2)

    return dict(qt_aug=q_aug.transpose(0, 2, 1), qt_blocks=blocks_t(q_aug, tq), k_aug=k_aug, v=vh,
                vt=blocks_t(vh, tk), kt=blocks_t(kh, tk), qh=qh)


def _s5_disc_fn(are, aim, ldt):
    dt = jnp.exp(ldt)
    er = jnp.exp(are * dt)
    br = er * jnp.cos(aim * dt)
    bi = er * jnp.sin(aim * dt)
    nr = br - 1.0
    den = are * are + aim * aim
    return br, bi, (nr * are + bi * aim) / den, (bi * are - nr * aim) / den


def _s5_disc(name, are, aim, ldt):
    def body(a_ref, b_ref, c_ref, o0, o1, o2, o3):
        r = _s5_disc_fn(a_ref[...], b_ref[...], c_ref[...])
        o0[...], o1[...], o2[...], o3[...] = r

    shp = _sds((S5_GROUPS, S5_STATE))
    return pl.pallas_call(body, name=name, out_shape=[shp] * 4)(are, aim, ldt)


def _s5_disc_bwd(name, are, aim, ldt, cts):
    def body(a_ref, b_ref, c_ref, d0, d1, d2, d3, o0, o1, o2):
        _, vjp = jax.vjp(_s5_disc_fn, a_ref[...], b_ref[...], c_ref[...])
        o0[...], o1[...], o2[...] = vjp((d0[...], d1[...], d2[...], d3[...]))

    shp = _sds((S5_GROUPS, S5_STATE))
    return pl.pallas_call(body, name=name, out_shape=[shp, shp, _sds((S5_GROUPS, 1))])(are, aim, ldt, *cts)


def _adamw_rows(w, g, m, v):
    m = ADAM_B1 * m + (1.0 - ADAM_B1) * g
    v = ADAM_B2 * v + (1.0 - ADAM_B2) * (g * g)
    m_hat = m / (1.0 - ADAM_B1 ** ADAM_STEP)
    v_hat = v / (1.0 - ADAM_B2 ** ADAM_STEP)
    return -ADAM_LR * (m_hat / (jnp.sqrt(v_hat) + ADAM_EPS) + ADAM_WD * w), m, v


def _adamw(name, w, ga, gb, m, v):
    rows, cols = w.shape
    tr = _row_tile(rows)

    def body(w_ref, ga_ref, gb_ref, m_ref, v_ref, g_out, d_out, m_out, v_out):
        g = ga_ref[...] + gb_ref[...]
        d, mm, vv = _adamw_rows(w_ref[...], g, m_ref[...], v_ref[...])
        g_out[...] = g
        d_out[...] = d
        m_out[...] = mm
        v_out[...] = vv

    spec = pl.BlockSpec((tr, cols), lambda i: (i, 0))
    return pl.pallas_call(
        body, name=name, grid=(rows // tr,), in_specs=[spec] * 5, out_specs=[spec] * 4,
        out_shape=[_sds((rows, cols))] * 4, compiler_params=_params("parallel"),
    )(w, ga, gb, m, v)


def _sum_stack(name, st):
    n, rows, cols = st.shape
    tr = _row_tile(rows)

    def body(s_ref, o_ref):
        acc = s_ref[0].astype(f32)
        for j in range(1, n):
            acc = acc + s_ref[j].astype(f32)
        o_ref[...] = acc

    return pl.pallas_call(
        body, name=name, grid=(rows // tr,), in_specs=[pl.BlockSpec((n, tr, cols), lambda i: (0, i, 0))],
        out_specs=pl.BlockSpec((tr, cols), lambda i: (i, 0)), out_shape=_sds((rows, cols)),
        compiler_params=_params("parallel"),
    )(st)


def _block_diag(w):
    h, n, m = w.shape
    return jnp.einsum("hij,hg->higj", w, jnp.eye(h, dtype=w.dtype)).reshape(h * n, h * m)


def _block_diag_part(dense, h):
    n, m = dense.shape[0] // h, dense.shape[1] // h
    return jnp.einsum("higj,hg->hij", dense.reshape(h, n, h, m), jnp.eye(h, dtype=dense.dtype))


def _s5_matrices(coef_re, coef_im, b_re, b_im, c_re, c_im):
    bb_re = coef_re[:, :, None] * b_re - coef_im[:, :, None] * b_im
    bb_im = coef_re[:, :, None] * b_im + coef_im[:, :, None] * b_re
    wb_re = _block_diag(jnp.swapaxes(bb_re, 1, 2))
    wb_im = _block_diag(jnp.swapaxes(bb_im, 1, 2))
    wc_re = _block_diag(jnp.swapaxes(c_re, 1, 2))
    wc_im = _block_diag(jnp.swapaxes(-c_im, 1, 2))
    return wb_re, wb_im, wc_re, wc_im


def _heads(t):
    s = t.shape[0]
    return t.reshape(s, N_HEADS, HEAD_DIM).transpose(1, 0, 2)


def _unheads(t):
    s = t.shape[1]
    return t.transpose(1, 0, 2).reshape(s, N_HEADS * HEAD_DIM)


def _shift_down(t):
    return jnp.concatenate([jnp.zeros((1, t.shape[1]), t.dtype), t[:-1]], axis=0)


def _shift_up(t):
    return jnp.concatenate([t[1:], jnp.zeros((1, t.shape[1]), t.dtype)], axis=0)


def _row(v):
    return v.reshape(1, -1)


def _ffn_fwd(tag, h, get, names, gamma, beta):
    wg, wu = get(names[0]), get(names[1])
    g, u, act = _ffn_up(tag + "_up", h, wg, wu)
    wd = get(names[2])
    r, out = _mm_ln(tag + "_down", act, wd, h, gamma, beta, 0.5)
    return out, dict(h=h, g=g, u=u, act=act, r=r, wg=wg, wu=wu, wd=wd)


def _ffn_bwd(tag, dout, sv, names, gamma, put):
    s = dout.shape[0]
    dr, dgam, dbet = _ln_bwd(tag + "_lnb", sv["r"], dout, gamma)
    put(names[2], _mm_plain(tag + "_dwd", "tn", sv["act"], dr, (D_FF, D_MODEL, s), scale=0.5,
                            tiles=(_tile(D_FF, 1408), 1024, _tile(s, 1024))))
    dg, du = _ffn_dact(tag + "_dact", dr, sv["wd"], sv["g"], sv["u"])
    dwg, dwu = _mm2(tag + "_dwgu", "tn", (D_MODEL, D_FF, s), sv["h"], dg, None, du, separate=True,
                    tiles=(512, _tile(D_FF, 1408), _tile(s, 1024)))
    put(names[0], dwg)
    put(names[1], dwu)
    dh = _mm2(tag + "_dh", "nt", (s, D_MODEL, D_FF), dg, sv["wg"], du, sv["wu"], add=dr, add_coef=ALPHA,
              tiles=(_tile(s, 512), 1024, _tile(D_FF, 1408)))[0]
    return dh, dgam, dbet


def _mixer_fwd(tag, h1, w):
    s = h1.shape[0]
    nt = s // ATT_TILE
    z = _mm_plain(tag + "_win", "nn", h1, w["w_in"], (s, N_IN_P, D_MODEL), tiles=(_tile(s, 512), 768, D_MODEL))
    ax, ag = z[:, :D_A], z[:, D_A:2 * D_A]
    q, k, v = z[:, 2 * D_A:3 * D_A], z[:, 3 * D_A:4 * D_A], z[:, 4 * D_A:5 * D_A]
    f, cu = z[:, F_OFF:F_OFF + 128], z[:, CU_OFF:]
    xa = _conv_fwd(tag + "_conv", ax, w["conv_w"], w["conv_b"])
    a, gated = _rg_gates(tag + "_gates", xa, w["rg_wa"], w["rg_wx"], w["rg_ba"], w["rg_bx"], w["rg_lam"])
    ha = _lin_scan(tag + "_rgscan", a, gated, False)
    ones = jnp.ones((s, 128), f32)
    c = _lin_scan(tag + "_cumf", ones, _log_f(tag + "_logf", f, w["fox_bf"]), False)
    att = _attn_operands(q, k, v, c[:, :N_HEADS].T)
    ot, lse = _attn_fwd_t(tag + "_attn", att["qt_aug"], att["k_aug"], att["vt"])
    ob = ot.reshape(D_B, s).T
    bu_re, bu_im = _mm2(tag + "_s5in", "nn", (s, S5_LANES, D_C), cu, w["wb_re"], None, w["wb_im"], separate=True,
                        tiles=(_tile(s, 512), 1024, D_C))
    hre, him = _s5_scan(tag + "_s5scan", bu_re, bu_im, w["abar_re"], w["abar_im"], False)
    o = _mix_out(tag + "_mixout", ag, ha, ob, hre, him, cu, w["s5_d"], w["mix_g"], w["wc_re"], w["wc_im"], w["w_glu"])
    sv = dict(h1=h1, ax=ax, ag=ag, f=f, cu=cu, xa=xa, a=a, ha=ha, att=att, ot=ot, lse=lse, ob=ob, hre=hre, him=him, o=o)
    return o, sv


def _mixer_bwd(tag, do, dr2, sv, w, put):
    s = do.shape[0]
    (dag, dha, dob, dhre, dhim, dcu1, dwcr, dwci, dwglu, dd, dgn) = _mix_out_bwd(
        tag + "_mixoutb", do, sv["ag"], sv["ha"], sv["ob"], sv["hre"], sv["him"], sv["cu"], w["s5_d"], w["mix_g"],
        w["wc_re"], w["wc_im"], w["w_glu"])
    put("s5_w_glu", dwglu)
    gre, gim = _s5_scan(tag + "_s5scanb", dhre, dhim, w["abar_re"], -w["abar_im"], True)
    dab_re, dab_im = _s5_decay_grad(tag + "_s5dec", _shift_down(sv["hre"]), _shift_down(sv["him"]), gre, gim)
    dwb_re, dwb_im = _mm2(tag + "_s5dwb", "tn", (D_C, S5_LANES, s), sv["cu"], gre, None, gim, separate=True,
                          tiles=(D_C, 1024, _tile(s, 1024)))
    dcu = _mm2(tag + "_s5dcu", "nt", (s, D_C, S5_LANES), gre, w["wb_re"], gim, w["wb_im"], add=dcu1,
               tiles=(_tile(s, 512), D_C, 1024))[0]
    att = sv["att"]
    tq = _att_tiles(s)[0]
    nt = s // tq
    dot_ = dob.T.reshape(N_HEADS, HEAD_DIM, s)
    dqt, delta = _attn_bwd_dq_t(tag + "_attndq", att["qt_aug"], att["k_aug"], att["v"], att["kt"], sv["ot"], dot_,
                                sv["lse"])
    dot_blocks = dot_.astype(bf16).reshape(N_HEADS, HEAD_DIM, nt, tq).transpose(0, 2, 1, 3)
    dkh, dvh, dck = _attn_bwd_dkv_t(tag + "_attndkv", att["qt_blocks"], att["k_aug"], att["v"], att["qh"],
                                    _heads(dob).astype(bf16), dot_blocks, sv["lse"].reshape(N_HEADS, nt, 1, tq),
                                    delta.reshape(N_HEADS, nt, 1, tq))
    dq, dk, dv = dqt.reshape(D_B, s).T, _unheads(dkh), _unheads(dvh)
    dc = jnp.pad(dck[:, :, 0].T, ((0, 0), (0, 128 - N_HEADS)))
    dlf = _lin_scan(tag + "_cumfb", jnp.ones((s, 128), f32), dc, True)
    df, dbf = _log_f_bwd(tag + "_logfb", dlf, sv["f"], w["fox_bf"])
    ga = _lin_scan(tag + "_rgscanb", _shift_up(sv["a"]), dha, True)
    dxa, dwa, dwx, dba, dbx, dlam = _rg_gates_bwd(tag + "_gatesb", sv["xa"], ga, _shift_down(sv["ha"]), w["rg_wa"],
                                                  w["rg_wx"], w["rg_ba"], w["rg_bx"], w["rg_lam"])
    dax, dconv = _conv_bwd(tag + "_convb", dxa, sv["ax"], w["conv_w"])
    dz = jnp.concatenate([dax, dag, dq, dk, dv, df, dcu], axis=-1).astype(bf16)
    put("w_in", _mm_plain(tag + "_dwin", "tn", sv["h1"], dz, (D_MODEL, N_IN_P, s), tiles=(512, 768, _tile(s, 1024))))
    dh1 = _mm_plain(tag + "_dh1", "nt", dz, w["w_in"], (s, D_MODEL, N_IN_P), add=dr2, add_coef=ALPHA,
                    tiles=(_tile(s, 512), 1024, 768))
    grads = dict(dconv=dconv, dwa=dwa, dwx=dwx, dba=dba, dbx=dbx, dlam=dlam, dbf=dbf,
                 dab_re=dab_re, dab_im=dab_im, dwb_re=dwb_re, dwb_im=dwb_im, dwcr=dwcr, dwci=dwci, dd=dd, dgn=dgn)
    return dh1, grads


SMALL_NAMES = ["ln1_g", "ln1_b", "conv_w", "conv_b", "rg_w_a", "rg_b_a", "rg_w_x", "rg_b_x", "rg_lambda", "fox_b_f",
               "s5_a_re", "s5_a_im", "s5_log_dt", "s5_b_re", "s5_b_im", "s5_c_re", "s5_c_im", "s5_d", "mix_norm_g",
               "ln2_g", "ln2_b", "ln3_g", "ln3_b"]
BIG_NAMES = ["ffn1_w_gate", "ffn1_w_up", "ffn1_w_down", "w_in", "s5_w_glu", "w_out", "ffn2_w_gate", "ffn2_w_up",
             "ffn2_w_down"]


def _local_step(x, target, weight, small, on_grads):
    h = x
    saved = []
    for l in range(DEPTH):
        get = functools.partial(weight, l)

        sm = {n: small[n][l] for n in SMALL_NAMES}
        abar_re, abar_im, coef_re, coef_im = _s5_disc(f"l{l}_s5disc", sm["s5_a_re"], sm["s5_a_im"],
                                                      sm["s5_log_dt"].reshape(S5_GROUPS, 1))
        mats, mats_vjp = jax.vjp(_s5_matrices, coef_re, coef_im, sm["s5_b_re"], sm["s5_b_im"], sm["s5_c_re"],
                                 sm["s5_c_im"])
        w = dict(
            conv_w=sm["conv_w"], conv_b=_row(sm["conv_b"]),
            rg_wa=_block_diag(sm["rg_w_a"]).astype(bf16), rg_wx=_block_diag(sm["rg_w_x"]).astype(bf16),
            rg_ba=_row(sm["rg_b_a"]), rg_bx=_row(sm["rg_b_x"]), rg_lam=_row(sm["rg_lambda"]),
            fox_bf=jnp.pad(_row(sm["fox_b_f"]), ((0, 0), (0, 128 - N_HEADS))),
            abar_re=_row(abar_re), abar_im=_row(abar_im),
            wb_re=mats[0].astype(bf16), wb_im=mats[1].astype(bf16), wc_re=mats[2].astype(bf16),
            wc_im=mats[3].astype(bf16), s5_d=_row(sm["s5_d"]), mix_g=_row(sm["mix_norm_g"]))
        h1, sv1 = _ffn_fwd(f"l{l}_ffn1", h, get, GROUPS["F1"], _row(sm["ln1_g"]), _row(sm["ln1_b"]))
        w["w_in"], w["w_glu"] = get("w_in"), get("s5_w_glu")
        o, svm = _mixer_fwd(f"l{l}_mix", h1, w)
        w_out = get("w_out")
        r2, h2 = _mm_ln(f"l{l}_wout", o, w_out, h1, _row(sm["ln2_g"]), _row(sm["ln2_b"]), 1.0)
        h3, sv2 = _ffn_fwd(f"l{l}_ffn2", h2, get, GROUPS["F2"], _row(sm["ln3_g"]), _row(sm["ln3_b"]))
        saved.append(dict(sm=sm, w=w, w_out=w_out, sv1=sv1, svm=svm, r2=r2, sv2=sv2, mats_vjp=mats_vjp))
        h = h3

    dh, loss_row = _loss_head("loss_head", h, target)
    s = x.shape[0]
    gsmall = {n: [None] * DEPTH for n in SMALL_NAMES}
    for l in reversed(range(DEPTH)):
        sd = saved[l]
        sm, w = sd["sm"], sd["w"]

        def put(name, grad, l=l):
            on_grads((l, name), grad)

        dh2, dgam, dbet = _ffn_bwd(f"l{l}_ffn2", dh, sd["sv2"], GROUPS["F2"], _row(sm["ln3_g"]), put)
        gsmall["ln3_g"][l], gsmall["ln3_b"][l] = dgam[0], dbet[0]
        dr2, dgam, dbet = _ln_bwd(f"l{l}_ln2b", sd["r2"], dh2, _row(sm["ln2_g"]))
        gsmall["ln2_g"][l], gsmall["ln2_b"][l] = dgam[0], dbet[0]
        put("w_out", _mm_plain(f"l{l}_dwout", "tn", sd["svm"]["o"], dr2, (D_MODEL, D_MODEL, s)))
        do = _mm_plain(f"l{l}_do", "nt", dr2, sd["w_out"], (s, D_MODEL, D_MODEL))
        dh1, g = _mixer_bwd(f"l{l}_mix", do, dr2, sd["svm"], w, put)
        gsmall["conv_w"][l], gsmall["conv_b"][l] = g["dconv"][:CONV_WIDTH], g["dconv"][CONV_WIDTH]
        gsmall["rg_w_a"][l] = _block_diag_part(g["dwa"], N_HEADS)
        gsmall["rg_w_x"][l] = _block_diag_part(g["dwx"], N_HEADS)
        gsmall["rg_b_a"][l], gsmall["rg_b_x"][l], gsmall["rg_lambda"][l] = g["dba"][0], g["dbx"][0], g["dlam"][0]
        gsmall["fox_b_f"][l] = g["dbf"][0, :N_HEADS]
        dcoef_re, dcoef_im, db_re, db_im, dc_re, dc_im = sd["mats_vjp"]((g["dwb_re"], g["dwb_im"], g["dwcr"], g["dwci"]))
        da_re, da_im, dldt = _s5_disc_bwd(
            f"l{l}_s5discb", sm["s5_a_re"], sm["s5_a_im"], sm["s5_log_dt"].reshape(S5_GROUPS, 1),
            (g["dab_re"].reshape(S5_GROUPS, S5_STATE), g["dab_im"].reshape(S5_GROUPS, S5_STATE), dcoef_re, dcoef_im))
        gsmall["s5_a_re"][l], gsmall["s5_a_im"][l], gsmall["s5_log_dt"][l] = da_re, da_im, dldt[:, 0]
        gsmall["s5_b_re"][l], gsmall["s5_b_im"][l], gsmall["s5_c_re"][l], gsmall["s5_c_im"][l] = db_re, db_im, dc_re, dc_im
        gsmall["s5_d"][l], gsmall["mix_norm_g"][l] = g["dd"][0], g["dgn"][0]
        dh, dgam, dbet = _ffn_bwd(f"l{l}_ffn1", dh1, sd["sv1"], GROUPS["F1"], _row(sm["ln1_g"]), put)
        gsmall["ln1_g"][l], gsmall["ln1_b"][l] = dgam[0], dbet[0]
    gsmall = {n: jnp.stack(v) for n, v in gsmall.items()}
    return loss_row[0, 0], dh, gsmall


def _position():
    return lax.axis_index("x"), lax.axis_index("y"), lax.axis_index("c")


_ANY = pl.BlockSpec(memory_space=pl.ANY)


def _chip_gather(name, shards):
    n = len(shards)

    def body(*refs):
        in_refs, out_refs = refs[:n], refs[n:2 * n]
        send_sems, recv_sems, local_sems = refs[2 * n:]
        x, y, c = _position()
        me = 2 * x + y
        peers = [(1 - x, y), (x, 1 - y), (1 - x, 1 - y)]
        local = [pltpu.make_async_copy(in_refs[i], out_refs[i].at[me], local_sems.at[i]) for i in range(n)]
        for cp in local:
            cp.start()
        sends = []
        for i in range(n):
            for r, (px, py) in enumerate(peers):
                cp = pltpu.make_async_remote_copy(
                    src_ref=in_refs[i], dst_ref=out_refs[i].at[me], send_sem=send_sems.at[3 * i + r],
                    recv_sem=recv_sems.at[3 * i + r], device_id=(px, py, c), device_id_type=MESH)
                cp.start()
                sends.append(cp)
        for i in range(n):
            for r, (px, py) in enumerate(peers):
                pltpu.make_async_remote_copy(
                    src_ref=in_refs[i], dst_ref=out_refs[i].at[2 * px + py], send_sem=send_sems.at[3 * i + r],
                    recv_sem=recv_sems.at[3 * i + r], device_id=(px, py, c), device_id_type=MESH).wait_recv()
        for cp in sends:
            cp.wait_send()
        for cp in local:
            cp.wait()

    return pl.pallas_call(
        body, name=name, in_specs=[_ANY] * n, out_specs=[_ANY] * n,
        out_shape=[_sds((N_CHIPS,) + a.shape, a.dtype) for a in shards],
        scratch_shapes=[pltpu.SemaphoreType.DMA((3 * n,)), pltpu.SemaphoreType.DMA((3 * n,)),
                        pltpu.SemaphoreType.DMA((n,))],
    )(*shards)


def _chip_scatter(name, stacks):
    n = len(stacks)

    def body(*refs):
        in_refs, out_refs = refs[:n], refs[n:2 * n]
        send_sems, recv_sems, local_sems = refs[2 * n:]
        x, y, c = _position()
        me = 2 * x + y
        peers = [(1 - x, y), (x, 1 - y), (1 - x, 1 - y)]
        local = [pltpu.make_async_copy(in_refs[i].at[me], out_refs[i].at[me], local_sems.at[i]) for i in range(n)]
        for cp in local:
            cp.start()
        sends = []
        for i in range(n):
            for r, (px, py) in enumerate(peers):
                cp = pltpu.make_async_remote_copy(
                    src_ref=in_refs[i].at[2 * px + py], dst_ref=out_refs[i].at[me], send_sem=send_sems.at[3 * i + r],
                    recv_sem=recv_sems.at[3 * i + r], device_id=(px, py, c), device_id_type=MESH)
                cp.start()
                sends.append(cp)
        for i in range(n):
            for r, (px, py) in enumerate(peers):
                pltpu.make_async_remote_copy(
                    src_ref=in_refs[i].at[me], dst_ref=out_refs[i].at[2 * px + py], send_sem=send_sems.at[3 * i + r],
                    recv_sem=recv_sems.at[3 * i + r], device_id=(px, py, c), device_id_type=MESH).wait_recv()
        for cp in sends:
            cp.wait_send()
        for cp in local:
            cp.wait()

    return pl.pallas_call(
        body, name=name, in_specs=[_ANY] * n, out_specs=[_ANY] * n,
        out_shape=[_sds(a.shape, a.dtype) for a in stacks],
        scratch_shapes=[pltpu.SemaphoreType.DMA((3 * n,)), pltpu.SemaphoreType.DMA((3 * n,)),
                        pltpu.SemaphoreType.DMA((n,))],
    )(*stacks)


def _sibling_swap(name, arrs):
    n = len(arrs)

    def body(*refs):
        in_refs, out_refs = refs[:n], refs[n:2 * n]
        send_sems, recv_sems = refs[2 * n:]
        x, y, c = _position()
        copies = [pltpu.make_async_remote_copy(
            src_ref=in_refs[i], dst_ref=out_refs[i], send_sem=send_sems.at[i], recv_sem=recv_sems.at[i],
            device_id=(x, y, 1 - c), device_id_type=MESH) for i in range(n)]
        for cp in copies:
            cp.start()
        for cp in copies:
            cp.wait_recv()
        for cp in copies:
            cp.wait_send()

    return pl.pallas_call(
        body, name=name, in_specs=[_ANY] * n, out_specs=[_ANY] * n,
        out_shape=[_sds(a.shape, a.dtype) for a in arrs],
        scratch_shapes=[pltpu.SemaphoreType.DMA((n,)), pltpu.SemaphoreType.DMA((n,))],
    )(*arrs)


def _dev_gather(name, arr):
    def body(in_ref, out_ref, send_sems, recv_sems, local_sem):
        x, y, c = _position()
        me = 4 * x + 2 * y + c
        local = pltpu.make_async_copy(in_ref, out_ref.at[me], local_sem)
        local.start()
        peers = []
        for k in range(1, N_DEV):
            peers.append((1 - x if k & 4 else x, 1 - y if k & 2 else y, 1 - c if k & 1 else c))
        sends = []
        for k, peer in enumerate(peers):
            cp = pltpu.make_async_remote_copy(src_ref=in_ref, dst_ref=out_ref.at[me], send_sem=send_sems.at[k],
                                              recv_sem=recv_sems.at[k], device_id=peer, device_id_type=MESH)
            cp.start()
            sends.append(cp)
        for k, (px, py, pc) in enumerate(peers):
            pltpu.make_async_remote_copy(src_ref=in_ref, dst_ref=out_ref.at[4 * px + 2 * py + pc],
                                         send_sem=send_sems.at[k], recv_sem=recv_sems.at[k], device_id=(px, py, pc),
                                         device_id_type=MESH).wait_recv()
        for cp in sends:
            cp.wait_send()
        local.wait()

    return pl.pallas_call(
        body, name=name, in_specs=[_ANY], out_specs=_ANY, out_shape=_sds((N_DEV,) + arr.shape, arr.dtype),
        scratch_shapes=[pltpu.SemaphoreType.DMA((N_DEV - 1,)), pltpu.SemaphoreType.DMA((N_DEV - 1,)),
                        pltpu.SemaphoreType.DMA],
    )(arr)


COLUMN_SHARDED = ("ffn1_w_gate", "ffn1_w_up", "ffn2_w_gate", "ffn2_w_up")
PACK_QUANTUM = 128 * 256


def _permute_in_cols(w):
    pad = jnp.zeros(w.shape[:-1] + (128 - N_HEADS,), w.dtype)
    return jnp.concatenate([w[..., :F_OFF + N_HEADS], pad, w[..., F_OFF + N_HEADS:]], axis=-1)


def _unpermute_in_cols(w):
    return jnp.concatenate([w[..., :F_OFF + N_HEADS], w[..., CU_OFF:]], axis=-1)


def _unstack(name, st):
    _, l, r, c = st.shape
    if name in COLUMN_SHARDED:
        return st.transpose(1, 2, 0, 3).reshape(l, r, N_CHIPS * c)
    return st.transpose(1, 0, 2, 3).reshape(l, N_CHIPS * r, c)


def _restack(name, g):
    l, r, c = g.shape
    if name in COLUMN_SHARDED:
        return g.reshape(l, r, N_CHIPS, c // N_CHIPS).transpose(2, 0, 1, 3)
    return g.reshape(l, N_CHIPS, r // N_CHIPS, c).transpose(1, 0, 2, 3)


def _pack(arrs):
    flat = jnp.concatenate([a.reshape(-1) for a in arrs])
    pad = -flat.shape[0] % PACK_QUANTUM
    return jnp.pad(flat, (0, pad)).reshape(-1, 128)


def _unpack(buf, shapes):
    flat = buf.reshape(-1)
    out, off = [], 0
    for shp in shapes:
        size = math.prod(shp)
        out.append(flat[off:off + size].reshape(shp))
        off += size
    return out


WEIGHT_NAMES = ["ffn1_w_gate", "ffn1_w_up", "ffn1_w_down", "ln1_g", "ln1_b", "w_in", "conv_w", "conv_b", "rg_w_a",
                "rg_b_a", "rg_w_x", "rg_b_x", "rg_lambda", "fox_b_f", "s5_a_re", "s5_a_im", "s5_log_dt", "s5_b_re",
                "s5_b_im", "s5_c_re", "s5_c_im", "s5_d", "s5_w_glu", "mix_norm_g", "w_out", "ln2_g", "ln2_b",
                "ffn2_w_gate", "ffn2_w_up", "ffn2_w_down", "ln3_g", "ln3_b"]


def _train_step(x, loss_target, w, m, v):
    ix, iy, _ = _position()
    chip = 2 * ix + iy

    shards = [(_permute_in_cols(w[n]) if n == "w_in" else w[n]).astype(bf16) for n in BIG_NAMES]
    stacks = _chip_gather("gather_weights", shards + [w["conv_w"]])
    big = {n: _unstack(n, st) for n, st in zip(BIG_NAMES, stacks)}
    small = {n: w[n] for n in SMALL_NAMES}
    small["conv_w"] = stacks[-1].transpose(1, 2, 0, 3).reshape(DEPTH, CONV_WIDTH, D_A)

    loss_local, gx, gbig, gsmall = _local_step(x[0], loss_target[0], big, small)

    sent = [_restack(n, gbig[n]).astype(bf16) for n in BIG_NAMES]
    recv = _chip_scatter("scatter_grads", sent)
    partial = {}
    for n, st in zip(BIG_NAMES, recv):
        _, l, r, c = st.shape
        p = _sum_stack("sum_" + n, st.reshape(N_CHIPS, l * r, c))
        partial[n] = _unpermute_in_cols(p) if n == "w_in" else p
    other = dict(zip(BIG_NAMES, _sibling_swap("swap_grads", [partial[n] for n in BIG_NAMES])))

    small_shapes = [gsmall[n].shape for n in SMALL_NAMES]
    total = _sum_stack("sum_small", _dev_gather("gather_small", _pack([gsmall[n] for n in SMALL_NAMES])))
    gsm = dict(zip(SMALL_NAMES, _unpack(total, small_shapes)))
    cw = D_A // N_CHIPS
    gsm["conv_w"] = lax.dynamic_slice_in_dim(gsm["conv_w"], chip * cw, cw, axis=2)

    grads, deltas, new_m, new_v = {}, {}, {}, {}
    for n in BIG_NAMES:
        shp = w[n].shape
        two_d = (shp[0] * shp[1], shp[2])
        g, d, mm, vv = _adamw("adamw_" + n, w[n].reshape(two_d), partial[n], other[n], m[n].reshape(two_d),
                              v[n].reshape(two_d))
        grads[n], deltas[n], new_m[n], new_v[n] = (t.reshape(shp) for t in (g, d, mm, vv))
    shapes = [w[n].shape for n in SMALL_NAMES]
    gp = _pack([gsm[n] for n in SMALL_NAMES])
    res = _adamw("adamw_small", _pack([w[n] for n in SMALL_NAMES]), gp, jnp.zeros_like(gp),
                 _pack([m[n] for n in SMALL_NAMES]), _pack([v[n] for n in SMALL_NAMES]))
    for dst, buf in zip((grads, deltas, new_m, new_v), res):
        dst.update(zip(SMALL_NAMES, _unpack(buf, shapes)))

    loss = lax.psum(loss_local, ("x", "y", "c"))
    return (loss, gx[None], *[grads[n] for n in WEIGHT_NAMES], *[deltas[n] for n in WEIGHT_NAMES],
            *[new_m[n] for n in WEIGHT_NAMES], *[new_v[n] for n in WEIGHT_NAMES])


def _remote(src, dst, send_sems, recv_sems, k, peer):
    return pltpu.make_async_remote_copy(src_ref=src, dst_ref=dst, send_sem=send_sems.at[k], recv_sem=recv_sems.at[k],
                                        device_id=peer, device_id_type=MESH)


class _ChipGatherPart:
    def __init__(self, arrays):
        self.arrays, self.results = list(arrays), None

    def out_shape(self):
        return [_sds((N_CHIPS,) + a.shape, a.dtype) for a in self.arrays]

    def sems(self):
        n = len(self.arrays)
        return [pltpu.SemaphoreType.DMA((3 * n,)), pltpu.SemaphoreType.DMA((3 * n,)), pltpu.SemaphoreType.DMA((n,))]

    def copies(self, ins, outs, sems):
        send_sems, recv_sems, local_sems = sems
        x, y, c = _position()
        me = 2 * x + y
        local, sends, recvs = [], [], []
        for i, (src, dst) in enumerate(zip(ins, outs)):
            local.append(pltpu.make_async_copy(self.mine(src, me), dst.at[me], local_sems.at[i]))
            for r, (px, py) in enumerate([(1 - x, y), (x, 1 - y), (1 - x, 1 - y)]):
                peer = 2 * px + py
                sends.append(_remote(self.theirs(src, peer), dst.at[me], send_sems, recv_sems, 3 * i + r, (px, py, c)))
                recvs.append(_remote(self.mine(src, me), dst.at[peer], send_sems, recv_sems, 3 * i + r, (px, py, c)))
        return local, sends, recvs

    def mine(self, src, me):
        return src

    def theirs(self, src, peer):
        return src


class _ChipScatterPart(_ChipGatherPart):
    def out_shape(self):
        return [_sds(a.shape, a.dtype) for a in self.arrays]

    def mine(self, src, me):
        return src.at[me]

    def theirs(self, src, peer):
        return src.at[peer]


class _SiblingSwapPart:
    def __init__(self, arrays):
        self.arrays, self.results = list(arrays), None

    def out_shape(self):
        return [_sds(a.shape, a.dtype) for a in self.arrays]

    def sems(self):
        n = len(self.arrays)
        return [pltpu.SemaphoreType.DMA((n,)), pltpu.SemaphoreType.DMA((n,))]

    def copies(self, ins, outs, sems):
        x, y, c = _position()
        both = [_remote(src, dst, sems[0], sems[1], i, (x, y, 1 - c)) for i, (src, dst) in enumerate(zip(ins, outs))]
        return [], both, both


class _DevGatherPart:
    def __init__(self, array):
        self.arrays, self.results = [array], None

    def out_shape(self):
        return [_sds((N_DEV,) + self.arrays[0].shape, self.arrays[0].dtype)]

    def sems(self):
        return [pltpu.SemaphoreType.DMA((N_DEV - 1,)), pltpu.SemaphoreType.DMA((N_DEV - 1,)),
                pltpu.SemaphoreType.DMA((1,))]

    def copies(self, ins, outs, sems):
        send_sems, recv_sems, local_sems = sems
        (src,), (dst,) = ins, outs
        x, y, c = _position()
        me = 4 * x + 2 * y + c
        local = [pltpu.make_async_copy(src, dst.at[me], local_sems.at[0])]
        sends, recvs = [], []
        for k in range(1, N_DEV):
            px, py, pc = (1 - x if k & 4 else x, 1 - y if k & 2 else y, 1 - c if k & 1 else c)
            sends.append(_remote(src, dst.at[me], send_sems, recv_sems, k - 1, (px, py, pc)))
            recvs.append(_remote(src, dst.at[4 * px + 2 * py + pc], send_sems, recv_sems, k - 1, (px, py, pc)))
        return local, sends, recvs


def _split_by(parts, refs, count):
    out, off = [], 0
    for p in parts:
        out.append(refs[off:off + count(p)])
        off += count(p)
    return out


def _parts_refs(parts, in_refs, out_refs, sem_refs):
    return zip(parts, _split_by(parts, in_refs, lambda p: len(p.arrays)),
               _split_by(parts, out_refs, lambda p: len(p.arrays)), _split_by(parts, sem_refs, lambda p: len(p.sems())))


def _exchange_start(parts, in_refs, out_refs, sem_refs):
    for part, ins, outs, sems in _parts_refs(parts, in_refs, out_refs, sem_refs):
        local, sends, _ = part.copies(ins, outs, sems)
        for cp in local + sends:
            cp.start()


def _exchange_finish(parts, in_refs, out_refs, sem_refs):
    for part, ins, outs, sems in _parts_refs(parts, in_refs, out_refs, sem_refs):
        local, sends, recvs = part.copies(ins, outs, sems)
        for cp in recvs:
            cp.wait_recv()
        for cp in sends:
            cp.wait_send()
        for cp in local:
            cp.wait()


def _exchange_operands(parts):
    return ([a for p in parts for a in p.arrays], [s for p in parts for s in p.out_shape()],
            [s for p in parts for s in p.sems()])


def _set_results(parts, res):
    for part, outs in zip(parts, _split_by(parts, list(res), lambda p: len(p.arrays))):
        part.results = list(outs)


def _exchange_now(name, parts):
    x_in, x_out, x_sem = _exchange_operands(parts)
    n = len(x_in)

    def body(*refs):
        _exchange_start(parts, refs[:n], refs[n:2 * n], refs[2 * n:])
        _exchange_finish(parts, refs[:n], refs[n:2 * n], refs[2 * n:])

    res = pl.pallas_call(body, name=name, in_specs=[_ANY] * n, out_specs=[_ANY] * n, out_shape=x_out,
                         scratch_shapes=x_sem)(*x_in)
    _set_results(parts, res)


_RIDERS = {}


def _call(body, *, name, grid, in_specs, out_specs, out_shape, scratch_shapes=(), compiler_params=None):
    make_parts = _RIDERS.pop(name, None)
    if make_parts is None:
        return pl.pallas_call(body, name=name, grid=grid, in_specs=in_specs, out_specs=out_specs, out_shape=out_shape,
                              scratch_shapes=scratch_shapes, compiler_params=compiler_params)
    parts = make_parts()
    x_in, x_out, x_sem = _exchange_operands(parts)
    n_out, n_scr, n_x = len(out_shape), len(scratch_shapes), len(x_in)

    def run(*args):
        n_in = len(args)

        def hosted(*refs):
            ins, xi = refs[:n_in], refs[n_in:n_in + n_x]
            outs, xo = refs[n_in + n_x:n_in + n_x + n_out], refs[n_in + n_x + n_out:n_in + 2 * n_x + n_out]
            scr, xs = refs[n_in + 2 * n_x + n_out:n_in + 2 * n_x + n_out + n_scr], refs[n_in + 2 * n_x + n_out + n_scr:]
            first = functools.reduce(jnp.logical_and, [pl.program_id(d) == 0 for d in range(len(grid))])
            last = functools.reduce(jnp.logical_and, [pl.program_id(d) == grid[d] - 1 for d in range(len(grid))])

            @pl.when(first)
            def _():
                _exchange_start(parts, xi, xo, xs)

            body(*ins, *outs, *scr)

            @pl.when(last)
            def _():
                _exchange_finish(parts, xi, xo, xs)

        res = pl.pallas_call(
            hosted, name=name, grid=grid, in_specs=list(in_specs) + [_ANY] * n_x,
            out_specs=list(out_specs) + [_ANY] * n_x, out_shape=list(out_shape) + x_out,
            scratch_shapes=list(scratch_shapes) + x_sem, compiler_params=_params(*["arbitrary"] * len(grid)),
        )(*args, *x_in)
        _set_results(parts, res[n_out:])
        return list(res[:n_out])

    return run


GROUPS = {"F1": ["ffn1_w_gate", "ffn1_w_up", "ffn1_w_down"], "MX": ["w_in", "s5_w_glu", "w_out"],
          "F2": ["ffn2_w_gate", "ffn2_w_up", "ffn2_w_down"]}
GROUP_OF = {n: g for g, names in GROUPS.items() for n in names}
FIRST_GATHER = [(0, "ffn1_w_gate"), (0, "ffn1_w_up")]
GATHER_HOSTS = {
    "l0_ffn1_up": [(0, "ffn1_w_down")],
    "l0_ffn1_down": [(0, "w_in"), (0, "s5_w_glu"), (0, "w_out")],
    "l0_mix_rgscan": [(0, "ffn2_w_gate")],
    "l0_mix_attn": [(0, "ffn2_w_up"), (0, "ffn2_w_down")],
    "l0_mix_s5scan": [(1, "ffn1_w_gate"), (1, "ffn1_w_up")],
    "l0_ffn2_up": [(1, "ffn1_w_down")],
    "l0_ffn2_down": [(1, "w_in"), (1, "s5_w_glu"), (1, "w_out")],
    "l1_mix_rgscan": [(1, "ffn2_w_gate")],
    "l1_mix_attn": [(1, "ffn2_w_up")],
    "l1_mix_s5scan": [(1, "ffn2_w_down")],
}
SCATTER_HOSTS = {
    "l1_ffn2_dact": [(1, "ffn2_w_down")],
    "l1_ffn2_dh": [(1, "ffn2_w_gate")],
    "l1_mix_attndq": [(1, "ffn2_w_up")],
    "l1_mix_attndkv": [(1, "w_out"), (1, "s5_w_glu")],
    "l1_mix_dh1": [(1, "w_in")],
    "l1_ffn1_dact": [(1, "ffn1_w_down")],
    "l1_ffn1_dh": [(1, "ffn1_w_gate")],
    "l0_ffn2_dact": [(1, "ffn1_w_up")],
    "l0_ffn2_dwgu": [(0, "ffn2_w_down")],
    "l0_ffn2_dh": [(0, "ffn2_w_gate")],
    "l0_mix_s5scanb": [(0, "ffn2_w_up")],
    "l0_mix_attndq": [(0, "w_out"), (0, "s5_w_glu")],
    "l0_mix_dh1": [(0, "w_in")],
    "l0_ffn1_dact": [(0, "ffn1_w_down")],
    "l0_ffn1_dh": [(0, "ffn1_w_gate")],
}
LAST_SCATTER = [(0, "ffn1_w_up")]


def _unstack_layer(name, st):
    _, r, c = st.shape
    if name in COLUMN_SHARDED:
        return st.transpose(1, 0, 2).reshape(r, N_CHIPS * c)
    return st.reshape(N_CHIPS * r, c)


def _restack_layer(name, g):
    r, c = g.shape
    if name in COLUMN_SHARDED:
        return g.reshape(r, N_CHIPS, c // N_CHIPS).transpose(1, 0, 2)
    return g.reshape(N_CHIPS, r // N_CHIPS, c)


def _adamw_layer(name, layer, w, ga, gb, m, v, bufs):
    _, r, c = w.shape
    tr = _row_tile(r)

    def body(w_ref, ga_ref, gb_ref, m_ref, v_ref, *rest):
        g_out, d_out, m_out, v_out = rest[-4:]
        g = ga_ref[...] + gb_ref[...]
        d, mm, vv = _adamw_rows(w_ref[...], g, m_ref[...], v_ref[...])
        g_out[...] = g
        d_out[...] = d
        m_out[...] = mm
        v_out[...] = vv

    full = pl.BlockSpec((None, tr, c), lambda i: (layer, i, 0))
    flat = pl.BlockSpec((tr, c), lambda i: (i, 0))
    extra = {} if bufs is None else dict(input_output_aliases={5 + k: k for k in range(4)})
    return pl.pallas_call(
        body, name=name, grid=(r // tr,),
        in_specs=[full, flat, flat, full, full] + ([] if bufs is None else [_ANY] * 4),
        out_specs=[full] * 4, out_shape=[_sds(w.shape)] * 4, compiler_params=_params("parallel"), **extra,
    )(w, ga, gb, m, v, *([] if bufs is None else bufs))


def _train_step(x, loss_target, w, m, v):
    ix, iy, _ = _position()
    chip = 2 * ix + iy
    shard = {n: (_permute_in_cols(w[n]) if n == "w_in" else w[n]).astype(bf16) for n in BIG_NAMES}

    gathered = {}

    def gather_parts(keys, extra=()):
        part = _ChipGatherPart([shard[n][layer] for layer, n in keys] + list(extra))
        gathered.update({key: (part, i) for i, key in enumerate(keys)})
        return [part]

    (first,) = gather_parts(FIRST_GATHER, extra=[w["conv_w"]])
    _exchange_now("gather_first", [first])
    for host, keys in GATHER_HOSTS.items():
        _RIDERS[host] = functools.partial(gather_parts, keys)

    def weight(layer, name):
        part, i = gathered[(layer, name)]
        return _unstack_layer(name, part.results[i])

    small = {n: w[n] for n in SMALL_NAMES}
    small["conv_w"] = first.results[-1].transpose(1, 2, 0, 3).reshape(DEPTH, CONV_WIDTH, D_A)

    grads_full, scattered = {}, {}

    def scatter_parts(keys):
        part = _ChipScatterPart([_restack_layer(n, grads_full[(layer, n)]).astype(bf16) for layer, n in keys])
        scattered.update({key: (part, i) for i, key in enumerate(keys)})
        return [part]

    for host, keys in SCATTER_HOSTS.items():
        _RIDERS[host] = functools.partial(scatter_parts, keys)

    loss_local, gx, gsmall = _local_step(x[0], loss_target[0], weight, small, grads_full.__setitem__)

    partial = {}

    def reduce_chips(keys):
        for layer, n in keys:
            part, i = scattered[(layer, n)]
            p = _sum_stack(f"sum_l{layer}_{n}", part.results[i])
            partial[(layer, n)] = _unpermute_in_cols(p) if n == "w_in" else p

    early = [key for keys in SCATTER_HOSTS.values() for key in keys]
    reduce_chips(early)
    small_shapes = [gsmall[n].shape for n in SMALL_NAMES]
    last_parts = scatter_parts(LAST_SCATTER) + [_SiblingSwapPart([partial[k] for k in early]),
                                                _DevGatherPart(_pack([gsmall[n] for n in SMALL_NAMES]))]
    _exchange_now("exchange_last", last_parts)
    other = dict(zip(early, last_parts[1].results))
    reduce_chips(LAST_SCATTER)
    swap_late = _SiblingSwapPart([partial[k] for k in LAST_SCATTER])
    _exchange_now("swap_last", [swap_late])
    other.update(zip(LAST_SCATTER, swap_late.results))

    total = _sum_stack("sum_small", last_parts[2].results[0])
    gsm = dict(zip(SMALL_NAMES, _unpack(total, small_shapes)))
    cw = D_A // N_CHIPS
    gsm["conv_w"] = lax.dynamic_slice_in_dim(gsm["conv_w"], chip * cw, cw, axis=2)

    grads, deltas, new_m, new_v = {}, {}, {}, {}
    for n in BIG_NAMES:
        bufs = None
        for layer in range(DEPTH):
            bufs = _adamw_layer(f"adamw_l{layer}_{n}", layer, w[n], partial[(layer, n)], other[(layer, n)], m[n], v[n],
                                bufs)
        grads[n], deltas[n], new_m[n], new_v[n] = bufs
    shapes = [w[n].shape for n in SMALL_NAMES]
    gp = _pack([gsm[n] for n in SMALL_NAMES])
    res = _adamw("adamw_small", _pack([w[n] for n in SMALL_NAMES]), gp, jnp.zeros_like(gp),
                 _pack([m[n] for n in SMALL_NAMES]), _pack([v[n] for n in SMALL_NAMES]))
    for dst, buf in zip((grads, deltas, new_m, new_v), res):
        dst.update(zip(SMALL_NAMES, _unpack(buf, shapes)))

    loss = lax.psum(loss_local, ("x", "y", "c"))
    return (loss, gx[None], *[grads[n] for n in WEIGHT_NAMES], *[deltas[n] for n in WEIGHT_NAMES],
            *[new_m[n] for n in WEIGHT_NAMES], *[new_v[n] for n in WEIGHT_NAMES])


def kernel(x, ffn1_w_gate, ffn1_w_up, ffn1_w_down, ln1_g, ln1_b, w_in, conv_w, conv_b, rg_w_a, rg_b_a, rg_w_x, rg_b_x, rg_lambda, fox_b_f, s5_a_re, s5_a_im, s5_log_dt, s5_b_re, s5_b_im, s5_c_re, s5_c_im, s5_d, s5_w_glu, mix_norm_g, w_out, ln2_g, ln2_b, ffn2_w_gate, ffn2_w_up, ffn2_w_down, ln3_g, ln3_b, loss_target, m_ffn1_w_gate, m_ffn1_w_up, m_ffn1_w_down, m_ln1_g, m_ln1_b, m_w_in, m_conv_w, m_conv_b, m_rg_w_a, m_rg_b_a, m_rg_w_x, m_rg_b_x, m_rg_lambda, m_fox_b_f, m_s5_a_re, m_s5_a_im, m_s5_log_dt, m_s5_b_re, m_s5_b_im, m_s5_c_re, m_s5_c_im, m_s5_d, m_s5_w_glu, m_mix_norm_g, m_w_out, m_ln2_g, m_ln2_b, m_ffn2_w_gate, m_ffn2_w_up, m_ffn2_w_down, m_ln3_g, m_ln3_b, v_ffn1_w_gate, v_ffn1_w_up, v_ffn1_w_down, v_ln1_g, v_ln1_b, v_w_in, v_conv_w, v_conv_b, v_rg_w_a, v_rg_b_a, v_rg_w_x, v_rg_b_x, v_rg_lambda, v_fox_b_f, v_s5_a_re, v_s5_a_im, v_s5_log_dt, v_s5_b_re, v_s5_b_im, v_s5_c_re, v_s5_c_im, v_s5_d, v_s5_w_glu, v_mix_norm_g, v_w_out, v_ln2_g, v_ln2_b, v_ffn2_w_gate, v_ffn2_w_up, v_ffn2_w_down, v_ln3_g, v_ln3_b):
    args = dict(locals())
    w = {n: args[n] for n in WEIGHT_NAMES}
    m = {n: args["m_" + n] for n in WEIGHT_NAMES}
    v = {n: args["v_" + n] for n in WEIGHT_NAMES}
    return _train_step(x, loss_target, w, m, v)
```

```python
import functools
import math

import jax
import jax.numpy as jnp
from jax import lax
from jax.experimental import pallas as pl
from jax.experimental.pallas import tpu as pltpu

f32 = jnp.float32
bf16 = jnp.bfloat16

D_MODEL = 1024
D_FF = 2816
D_A = 384
D_B = 384
D_C = 256
N_HEADS = 6
HEAD_DIM = 64
S5_GROUPS = 16
S5_GROUP = 16
S5_STATE = 64
S5_LANES = S5_GROUPS * S5_STATE
N_IN = 2 * D_A + 3 * D_B + N_HEADS + D_C
F_OFF = 5 * D_A
CU_OFF = F_OFF + 128
N_IN_P = CU_OFF + D_C
CONV_WIDTH = 4
DEPTH = 2
ALPHA = (2 * DEPTH) ** 0.25
LN_EPS = 1e-5
RMS_EPS = 1e-6
RG_C = 8.0
ATT_SCALE = HEAD_DIM ** -0.5
ADAM_LR, ADAM_B1, ADAM_B2, ADAM_EPS, ADAM_WD, ADAM_STEP = 0.001, 0.9, 0.999, 1e-08, 0.01, 10

SCAN_CHUNK = 64
ROW_TILE = 256
ATT_TILE = 256
N_CHIPS = 4
N_DEV = 8
MESH = pl.DeviceIdType.MESH

_DN = {
    "nn": (((1,), (0,)), ((), ())),
    "nt": (((1,), (1,)), ((), ())),
    "tn": (((0,), (0,)), ((), ())),
}


def _sds(shape, dtype=f32):
    return jax.ShapeDtypeStruct(shape, dtype)


def _tile(n, target):
    best = None
    for t in range(128, min(n, target) + 1, 128):
        if n % t == 0:
            best = t
    return best or n


def _row_tile(rows, target=256):
    best = None
    for t in range(16, min(rows, target) + 1, 16):
        if rows % t == 0:
            best = t
    return best or rows


def _params(*sem):
    return pltpu.CompilerParams(dimension_semantics=sem)


class _Slabs:
    def __init__(self, x):
        self.x = x


FF_SLAB = D_FF // 4

def _mm(name, mode, dims, tiles, a_list, b_list, pairs, n_acc, epilogue, outs, extras=(), vecs=(), split_cols=False):
    m, n, k = dims
    tm, tn, tk = tiles
    nk = k // tk
    na, nb, ne, nv, no = len(a_list), len(b_list), len(extras), len(vecs), len(outs)

    def body(*refs):
        a_refs = refs[:na]
        b_refs = refs[na:na + nb]
        e_refs = refs[na + nb:na + nb + ne]
        v_refs = refs[na + nb + ne:na + nb + ne + nv]
        o_refs = refs[na + nb + ne + nv:na + nb + ne + nv + no]
        acc_refs = refs[na + nb + ne + nv + no:]
        kk = pl.program_id(2)

        @pl.when(kk == 0)
        def _():
            for acc in acc_refs:
                acc[...] = jnp.zeros_like(acc)

        a_vals = [r[...].astype(bf16) for r in a_refs]
        b_vals = [r[...].astype(bf16) for r in b_refs]
        for ai, bi, ci in pairs:
            acc_refs[ci][...] += lax.dot_general(a_vals[ai], b_vals[bi], _DN[mode], preferred_element_type=f32)

        @pl.when(kk == nk - 1)
        def _():
            res = epilogue([acc[...] for acc in acc_refs], [e[...] for e in e_refs], [v[...] for v in v_refs])
            for o, r in zip(o_refs, res):
                o[...] = r.astype(o.dtype)

    def a_spec(a):
        if isinstance(a, _Slabs):
            if mode == "tn":
                return pl.BlockSpec((None, tk, tm), lambda i, j, kk: (i, kk, 0))
            return pl.BlockSpec((None, tm, tk), lambda i, j, kk: (kk, i, 0))
        if mode == "tn":
            return pl.BlockSpec((tk, tm), lambda i, j, kk: (kk, i))
        return pl.BlockSpec((tm, tk), lambda i, j, kk: (i, kk))

    def b_spec(b):
        if isinstance(b, _Slabs):
            if mode == "nt":
                return pl.BlockSpec((None, tn, tk), lambda i, j, kk: (kk, j, 0))
            return pl.BlockSpec((None, tk, tn), lambda i, j, kk: (j, kk, 0))
        if mode == "nt":
            return pl.BlockSpec((tn, tk), lambda i, j, kk: (j, kk))
        return pl.BlockSpec((tk, tn), lambda i, j, kk: (kk, j))

    o_spec = pl.BlockSpec((tm, tn), lambda i, j, kk: (i, j))
    o_slab_spec = pl.BlockSpec((None, tm, tn), lambda i, j, kk: (j, i, 0))
    v_spec = pl.BlockSpec((1, tn), lambda i, j, kk: (0, j))
    if split_cols:
        out_specs = [o_slab_spec] * no
        out_shape = [_sds((n // tn, m, tn), dt) for dt in outs]
    else:
        out_specs = [o_spec] * no
        out_shape = [_sds((m, n), dt) for dt in outs]
    raw = lambda t: t.x if isinstance(t, _Slabs) else t
    res = _call(
        body,
        name=name,
        grid=(m // tm, n // tn, nk),
        in_specs=([a_spec(a) for a in a_list] + [b_spec(b) for b in b_list]
                  + [o_slab_spec if isinstance(e, _Slabs) else o_spec for e in extras] + [v_spec] * nv),
        out_specs=out_specs,
        out_shape=out_shape,
        scratch_shapes=[pltpu.VMEM((tm, tn), f32)] * n_acc,
        compiler_params=_params("parallel", "parallel", "arbitrary"),
    )(*map(raw, a_list), *map(raw, b_list), *map(raw, extras), *vecs)
    return res


def _layer_norm_rows(r, gamma, beta):
    mu = jnp.mean(r, axis=-1, keepdims=True)
    xc = r - mu
    var = jnp.mean(xc * xc, axis=-1, keepdims=True)
    return xc * lax.rsqrt(var + LN_EPS) * gamma + beta


def _mm_plain(name, mode, a, b, dims, scale=1.0, out_dtype=f32, add=None, add_coef=1.0, tiles=None):
    m, n, k = dims
    tiles = tiles or (_tile(m, 512), _tile(n, 1024), _tile(k, 1024))

    def epilogue(accs, extras, vecs):
        r = accs[0] if scale == 1.0 else accs[0] * scale
        if extras:
            r = r + add_coef * extras[0]
        return [r]

    return _mm(name, mode, dims, tiles, [a], [b], [(0, 0, 0)], 1, epilogue, [out_dtype],
               extras=[] if add is None else [add])[0]


def _ffn_up(name, h, wg, wu):
    s = h.shape[0]

    def epilogue(accs, extras, vecs):
        g, u = accs
        return [g, u, g * jax.nn.sigmoid(g) * u]

    return _mm(name, "nn", (s, D_FF, D_MODEL), (_tile(s, 512), FF_SLAB, D_MODEL), [h], [_Slabs(wg), _Slabs(wu)],
               [(0, 0, 0), (0, 1, 1)], 2, epilogue, [f32, f32, bf16], split_cols=True)


def _mm_ln(name, a, w, resid, gamma, beta, scale, k_slabs=False):
    s, k = (a.shape[1], a.shape[0] * a.shape[2]) if k_slabs else a.shape

    def epilogue(accs, extras, vecs):
        r = ALPHA * extras[0] + scale * accs[0]
        return [r, _layer_norm_rows(r, vecs[0], vecs[1])]

    return _mm(name, "nn", (s, D_MODEL, k), (_tile(s, 256), D_MODEL, FF_SLAB if k_slabs else _tile(k, 1024)),
               [_Slabs(a) if k_slabs else a], [w], [(0, 0, 0)], 1, epilogue, [f32, f32], extras=[resid],
               vecs=[gamma, beta])


def _ffn_dact(name, dr, wd, g, u):
    s = dr.shape[0]

    def epilogue(accs, extras, vecs):
        da = 0.5 * accs[0]
        gg, uu = extras
        sg = jax.nn.sigmoid(gg)
        return [da * uu * (sg * (1.0 + gg * (1.0 - sg))), da * (gg * sg)]

    return _mm(name, "nt", (s, D_FF, D_MODEL), (_tile(s, 512), FF_SLAB, D_MODEL), [dr], [wd],
               [(0, 0, 0)], 1, epilogue, [bf16, bf16], extras=[_Slabs(g), _Slabs(u)], split_cols=True)


def _mm2(name, mode, dims, a0, b0, a1, b1, add=None, add_coef=1.0, separate=False, tiles=None, out_dtype=f32,
         split_cols=False):
    m, n, k = dims
    tiles = tiles or (_tile(m, 512), _tile(n, 1024), _tile(k, 1024))

    def epilogue(accs, extras, vecs):
        if separate:
            return list(accs)
        r = accs[0]
        if extras:
            r = r + add_coef * extras[0]
        return [r]

    a_list = [a0] if a1 is None else [a0, a1]
    b_list = [b0] if b1 is None else [b0, b1]
    pairs = [(0, 0, 0), (len(a_list) - 1, len(b_list) - 1, 1 if separate else 0)]
    return _mm(name, mode, dims, tiles, a_list, b_list, pairs, 2 if separate else 1, epilogue,
               [out_dtype, out_dtype] if separate else [out_dtype], extras=[] if add is None else [add],
               split_cols=split_cols)


def _row_call(name, body, s, ins, params, outs, accs):
    tm = ROW_TILE
    in_specs = [pl.BlockSpec((tm, a.shape[1]), lambda i: (i, 0)) for a in ins]
    in_specs += [pl.BlockSpec(p.shape, lambda i, nd=p.ndim: (0,) * nd) for p in params]
    out_specs = [pl.BlockSpec((tm, o.shape[1]), lambda i: (i, 0)) for o in outs]
    out_specs += [pl.BlockSpec(a.shape, lambda i, nd=len(a.shape): (0,) * nd) for a in accs]
    return pl.pallas_call(
        body,
        name=name,
        grid=(s // tm,),
        in_specs=in_specs,
        out_specs=out_specs,
        out_shape=list(outs) + list(accs),
        compiler_params=_params("arbitrary"),
    )(*ins, *params)


def _zero_at_first(refs):
    @pl.when(pl.program_id(0) == 0)
    def _():
        for r in refs:
            r[...] = jnp.zeros_like(r)


def _ln_bwd(name, r, dh, gamma):
    s = r.shape[0]

    def body(r_ref, dh_ref, g_ref, dr_ref, dg_ref, db_ref):
        _zero_at_first([dg_ref, db_ref])
        rr = r_ref[...]
        dy = dh_ref[...]
        mu = jnp.mean(rr, axis=-1, keepdims=True)
        xc = rr - mu
        rstd = lax.rsqrt(jnp.mean(xc * xc, axis=-1, keepdims=True) + LN_EPS)
        xhat = xc * rstd
        dxh = dy * g_ref[...]
        dr_ref[...] = rstd * (dxh - jnp.mean(dxh, axis=-1, keepdims=True)
                              - xhat * jnp.mean(dxh * xhat, axis=-1, keepdims=True))
        dg_ref[...] += jnp.sum(dy * xhat, axis=0, keepdims=True)
        db_ref[...] += jnp.sum(dy, axis=0, keepdims=True)

    return _row_call(name, body, s, [r, dh], [gamma], [_sds((s, D_MODEL))], [_sds((1, D_MODEL)), _sds((1, D_MODEL))])


def _loss_head(name, y, target):
    s = y.shape[0]

    def body(y_ref, t_ref, dy_ref, l_ref):
        _zero_at_first([l_ref])
        e = y_ref[...] - t_ref[...]
        dy_ref[...] = e / D_MODEL
        l_ref[...] += 0.5 * jnp.sum(jnp.mean(e * e, axis=-1, keepdims=True), axis=0, keepdims=True)

    return _row_call(name, body, s, [y, target], [], [_sds((s, D_MODEL))], [_sds((1, 128))])


def _expm1(x):
    series = x * (1.0 + x / 2.0 * (1.0 + x / 3.0 * (1.0 + x / 4.0 * (1.0 + x / 5.0 * (1.0 + x / 6.0 * (1.0 + x / 7.0))))))
    return jnp.where(jnp.abs(x) < 0.25, series, jnp.exp(x) - 1.0)


def _gates_fn(xa, wa, wx, ba, bx, lam, tap_a, tap_x):
    xb = xa.astype(bf16)
    r = jax.nn.sigmoid(jnp.dot(xb, wa, preferred_element_type=f32) + ba + tap_a)
    i = jax.nn.sigmoid(jnp.dot(xb, wx, preferred_element_type=f32) + bx + tap_x)
    log_a = -RG_C * r * jax.nn.softplus(-lam)
    a = jnp.exp(log_a)
    gated = jnp.sqrt(-_expm1(2.0 * log_a)) * (i * xa)
    return a, gated


def _rg_gates(name, xa, wa, wx, ba, bx, lam):
    s = xa.shape[0]

    def body(xa_ref, wa_ref, wx_ref, ba_ref, bx_ref, lam_ref, a_ref, g_ref):
        a, g = _gates_fn(xa_ref[...], wa_ref[...], wx_ref[...], ba_ref[...], bx_ref[...], lam_ref[...], 0.0, 0.0)
        a_ref[...] = a
        g_ref[...] = g

    return _row_call(name, body, s, [xa], [wa, wx, ba, bx, lam], [_sds((s, D_A)), _sds((s, D_A))], [])


def _rg_gates_bwd(name, xa, ga, h_prev, wa, wx, ba, bx, lam):
    s = xa.shape[0]

    def body(xa_ref, ga_ref, hp_ref, wa_ref, wx_ref, ba_ref, bx_ref, lam_ref,
             dxa_ref, dwa_ref, dwx_ref, dba_ref, dbx_ref, dlam_ref):
        _zero_at_first([dwa_ref, dwx_ref, dba_ref, dbx_ref, dlam_ref])
        xa_v = xa_ref[...]
        zero = jnp.zeros((xa_v.shape[0], D_A), f32)
        fn = lambda x, ba_, bx_, lam_, ta, tx: _gates_fn(x, wa_ref[...], wx_ref[...], ba_, bx_, lam_, ta, tx)
        _, vjp = jax.vjp(fn, xa_v, ba_ref[...], bx_ref[...], lam_ref[...], zero, zero)
        gav = ga_ref[...]
        dxa, dba, dbx, dlam, dta, dtx = vjp((gav * hp_ref[...], gav))
        dxa_ref[...] = dxa
        xb = xa_v.astype(bf16)
        dwa_ref[...] += lax.dot_general(xb, dta.astype(bf16), _DN["tn"], preferred_element_type=f32)
        dwx_ref[...] += lax.dot_general(xb, dtx.astype(bf16), _DN["tn"], preferred_element_type=f32)
        dba_ref[...] += dba
        dbx_ref[...] += dbx
        dlam_ref[...] += dlam

    return _row_call(name, body, s, [xa, ga, h_prev], [wa, wx, ba, bx, lam], [_sds((s, D_A))],
                     [_sds((D_A, D_A)), _sds((D_A, D_A)), _sds((1, D_A)), _sds((1, D_A)), _sds((1, D_A))])


def _rms(v, g):
    return v * lax.rsqrt(jnp.mean(v * v, axis=-1, keepdims=True) + RMS_EPS) * g


def _mix_out_fn(ag, ha, ob, hre, him, cu, d, gn, tap_y, tap_gl, wcr, wci, wglu):
    out_a = jax.nn.gelu(ag) * ha
    y = (jnp.dot(hre.astype(bf16), wcr, preferred_element_type=f32)
         + jnp.dot(him.astype(bf16), wci, preferred_element_type=f32) + d * cu + tap_y)
    y2 = jax.nn.gelu(y)
    gl = jnp.dot(y2.astype(bf16), wglu, preferred_element_type=f32) + tap_gl
    out_c = y2 * jax.nn.sigmoid(gl)
    o = jnp.concatenate([_rms(out_a, gn[:, :D_A]), _rms(ob, gn[:, D_A:D_A + D_B]), _rms(out_c, gn[:, D_A + D_B:])],
                        axis=-1)
    return o, y2


def _mix_out(name, ag, ha, ob, hre, him, cu, d, gn, wcr, wci, wglu):
    s = ag.shape[0]

    def body(ag_ref, ha_ref, ob_ref, hre_ref, him_ref, cu_ref, d_ref, gn_ref, wcr_ref, wci_ref, wglu_ref, o_ref):
        o, _ = _mix_out_fn(ag_ref[...], ha_ref[...], ob_ref[...], hre_ref[...], him_ref[...], cu_ref[...], d_ref[...],
                           gn_ref[...], 0.0, 0.0, wcr_ref[...], wci_ref[...], wglu_ref[...])
        o_ref[...] = o.astype(o_ref.dtype)

    return _row_call(name, body, s, [ag, ha, ob, hre, him, cu], [d, gn, wcr, wci, wglu], [_sds((s, D_MODEL), bf16)], [])[0]


def _mix_out_bwd(name, do, ag, ha, ob, hre, him, cu, d, gn, wcr, wci, wglu):
    s = ag.shape[0]

    def body(do_ref, ag_ref, ha_ref, ob_ref, hre_ref, him_ref, cu_ref, d_ref, gn_ref, wcr_ref, wci_ref, wglu_ref,
             dag_ref, dha_ref, dob_ref, dhre_ref, dhim_ref, dcu_ref, dwcr_ref, dwci_ref, dwglu_ref, dd_ref, dgn_ref):
        _zero_at_first([dwcr_ref, dwci_ref, dwglu_ref, dd_ref, dgn_ref])
        tm = ag_ref.shape[0]
        zero = jnp.zeros((tm, D_C), f32)
        hre_v, him_v = hre_ref[...], him_ref[...]
        fn = lambda *a: _mix_out_fn(*a, wcr_ref[...], wci_ref[...], wglu_ref[...])
        _, vjp, y2 = jax.vjp(fn, ag_ref[...], ha_ref[...], ob_ref[...], hre_v, him_v, cu_ref[...], d_ref[...],
                             gn_ref[...], zero, zero, has_aux=True)
        dag, dha, dob, dhre, dhim, dcu, dd, dgn, dy, dgl = vjp(do_ref[...])
        dag_ref[...] = dag
        dha_ref[...] = dha
        dob_ref[...] = dob
        dhre_ref[...] = dhre
        dhim_ref[...] = dhim
        dcu_ref[...] = dcu
        dyb = dy.astype(bf16)
        dwcr_ref[...] += lax.dot_general(hre_v.astype(bf16), dyb, _DN["tn"], preferred_element_type=f32)
        dwci_ref[...] += lax.dot_general(him_v.astype(bf16), dyb, _DN["tn"], preferred_element_type=f32)
        dwglu_ref[...] += lax.dot_general(y2.astype(bf16), dgl.astype(bf16), _DN["tn"], preferred_element_type=f32)
        dd_ref[...] += dd
        dgn_ref[...] += dgn

    outs = [_sds((s, D_A)), _sds((s, D_A)), _sds((s, D_B)), _sds((s, S5_LANES)), _sds((s, S5_LANES)), _sds((s, D_C))]
    accs = [_sds((S5_LANES, D_C)), _sds((S5_LANES, D_C)), _sds((D_C, D_C)), _sds((1, D_C)), _sds((1, D_MODEL))]
    return _row_call(name, body, s, [do, ag, ha, ob, hre, him, cu], [d, gn, wcr, wci, wglu], outs, accs)


def _log_f(name, f, bf):
    s = f.shape[0]

    def body(f_ref, b_ref, o_ref):
        o_ref[...] = jax.nn.log_sigmoid(f_ref[...] + b_ref[...])

    return _row_call(name, body, s, [f], [bf], [_sds((s, 128))], [])[0]


def _log_f_bwd(name, dlf, f, bf):
    s = f.shape[0]

    def body(dl_ref, f_ref, b_ref, df_ref, db_ref):
        _zero_at_first([db_ref])
        df = dl_ref[...] * jax.nn.sigmoid(-(f_ref[...] + b_ref[...]))
        df_ref[...] = df
        db_ref[...] += jnp.sum(df, axis=0, keepdims=True)

    return _row_call(name, body, s, [dlf, f], [bf], [_sds((s, 128))], [_sds((1, 128))])


def _s5_decay_grad(name, hp_re, hp_im, g_re, g_im):
    s = g_re.shape[0]

    def body(hr_ref, hi_ref, gr_ref, gi_ref, dr_ref, di_ref):
        _zero_at_first([dr_ref, di_ref])
        hr, hi, gr, gi = hr_ref[...], hi_ref[...], gr_ref[...], gi_ref[...]
        dr_ref[...] += jnp.sum(hr * gr + hi * gi, axis=0, keepdims=True)
        di_ref[...] += jnp.sum(hr * gi - hi * gr, axis=0, keepdims=True)

    return _row_call(name, body, s, [hp_re, hp_im, g_re, g_im], [], [], [_sds((1, S5_LANES)), _sds((1, S5_LANES))])


def _conv_fwd(name, ax, w, b):
    s = ax.shape[0]
    tm = ROW_TILE

    def body(x_ref, halo_ref, w_ref, b_ref, o_ref):
        i = pl.program_id(0)
        x = x_ref[...]
        halo = jnp.where(i == 0, 0.0, halo_ref[...])
        ext = jnp.concatenate([halo, x], axis=0)
        acc = b_ref[...] + w_ref[3:4, :] * x
        for k in range(CONV_WIDTH - 1):
            acc = acc + w_ref[k:k + 1, :] * pltpu.roll(ext, CONV_WIDTH - 1 - k, 0)[8:, :]
        o_ref[...] = acc

    return pl.pallas_call(
        body,
        name=name,
        grid=(s // tm,),
        in_specs=[pl.BlockSpec((tm, D_A), lambda i: (i, 0)),
                  pl.BlockSpec((8, D_A), lambda i: (jnp.maximum(i * (tm // 8) - 1, 0), 0)),
                  pl.BlockSpec((CONV_WIDTH, D_A), lambda i: (0, 0)),
                  pl.BlockSpec((1, D_A), lambda i: (0, 0))],
        out_specs=pl.BlockSpec((tm, D_A), lambda i: (i, 0)),
        out_shape=_sds((s, D_A)),
        compiler_params=_params("arbitrary"),
    )(ax, ax, w, b)


def _conv_bwd(name, dxa, ax, w):
    s = ax.shape[0]
    tm = ROW_TILE
    nblk = s // tm

    def body(dx_ref, dnext_ref, x_ref, halo_ref, w_ref, dax_ref, dw_ref):
        i = pl.program_id(0)
        _zero_at_first([dw_ref])
        dx = dx_ref[...]
        dnext = jnp.where(i == nblk - 1, 0.0, dnext_ref[...])
        dext = jnp.concatenate([dx, dnext], axis=0)
        x = x_ref[...]
        halo = jnp.where(i == 0, 0.0, halo_ref[...])
        ext = jnp.concatenate([halo, x], axis=0)
        acc = w_ref[3:4, :] * dx
        dw_ref[3:4, :] += jnp.sum(dx * x, axis=0, keepdims=True)
        for k in range(CONV_WIDTH - 1):
            sh = CONV_WIDTH - 1 - k
            acc = acc + w_ref[k:k + 1, :] * pltpu.roll(dext, tm + 8 - sh, 0)[:tm, :]
            dw_ref[k:k + 1, :] += jnp.sum(dx * pltpu.roll(ext, sh, 0)[8:, :], axis=0, keepdims=True)
        dw_ref[4:5, :] += jnp.sum(dx, axis=0, keepdims=True)
        dax_ref[...] = acc

    return pl.pallas_call(
        body,
        name=name,
        grid=(nblk,),
        in_specs=[pl.BlockSpec((tm, D_A), lambda i: (i, 0)),
                  pl.BlockSpec((8, D_A), lambda i: (jnp.minimum((i + 1) * (tm // 8), s // 8 - 1), 0)),
                  pl.BlockSpec((tm, D_A), lambda i: (i, 0)),
                  pl.BlockSpec((8, D_A), lambda i: (jnp.maximum(i * (tm // 8) - 1, 0), 0)),
                  pl.BlockSpec((CONV_WIDTH, D_A), lambda i: (0, 0))],
        out_specs=[pl.BlockSpec((tm, D_A), lambda i: (i, 0)), pl.BlockSpec((8, D_A), lambda i: (0, 0))],
        out_shape=[_sds((s, D_A)), _sds((8, D_A))],
        compiler_params=_params("arbitrary"),
    )(dxa, dxa, ax, ax, w)


SCAN_ROWS = 512


def _row_in_tile(shape):
    return lax.broadcasted_iota(jnp.int32, shape, 0) % 8


def _lin_scan(name, a, b, reverse):
    s, c = a.shape
    t = min(SCAN_ROWS, s)
    nb = s // t

    def body(a_ref, b_ref, h_ref, p_ref, carry_ref):
        @pl.when(pl.program_id(0) == 0)
        def _():
            carry_ref[...] = jnp.zeros_like(carry_ref)

        row = _row_in_tile((t, c))
        p = a_ref[...]
        h = b_ref[...]
        for d in (1, 2, 4):
            keep = (row < 8 - d) if reverse else (row >= d)
            shift = (t - d) if reverse else d
            h = h + jnp.where(keep, p * pltpu.roll(h, shift, 0), 0.0)
            p = jnp.where(keep, p * pltpu.roll(p, shift, 0), p)
        h_ref[...] = h
        p_ref[...] = p
        edge = 0 if reverse else 7

        def tile(k, carry):
            kk = (t // 8 - 1 - k) if reverse else k
            r0 = pl.multiple_of(kk * 8, 8)
            hh = h_ref[pl.ds(r0, 8), :] + p_ref[pl.ds(r0, 8), :] * carry
            h_ref[pl.ds(r0, 8), :] = hh
            return jnp.broadcast_to(hh[edge:edge + 1, :], (8, c))

        carry_ref[...] = lax.fori_loop(0, t // 8, tile, carry_ref[...])

    spec = pl.BlockSpec((t, c), (lambda i: (nb - 1 - i, 0)) if reverse else (lambda i: (i, 0)))
    (out,) = _call(
        body,
        name=name,
        grid=(nb,),
        in_specs=[spec, spec],
        out_specs=[spec],
        out_shape=[_sds((s, c))],
        scratch_shapes=[pltpu.VMEM((t, c), f32), pltpu.VMEM((8, c), f32)],
        compiler_params=_params("arbitrary"),
    )(a, b)
    return out


def _s5_scan(name, b_re, b_im, a_re, a_im, reverse):
    s, c = b_re.shape
    t = min(SCAN_ROWS, s)
    nb = s // t

    def body(br_ref, bi_ref, ar_ref, ai_ref, hr_ref, hi_ref, cr_ref, ci_ref):
        @pl.when(pl.program_id(0) == 0)
        def _():
            cr_ref[...] = jnp.zeros_like(cr_ref)
            ci_ref[...] = jnp.zeros_like(ci_ref)

        ar1, ai1 = ar_ref[...], ai_ref[...]
        pows = [(ar1, ai1)]
        for _ in range(7):
            pr, pi = pows[-1]
            pows.append((pr * ar1 - pi * ai1, pr * ai1 + pi * ar1))
        row8 = lax.broadcasted_iota(jnp.int32, (8, c), 0)
        wr = jnp.zeros((8, c), f32)
        wi = jnp.zeros((8, c), f32)
        for r in range(8):
            pr, pi = pows[(7 - r) if reverse else r]
            wr = jnp.where(row8 == r, pr, wr)
            wi = jnp.where(row8 == r, pi, wi)
        row = _row_in_tile((t, c))
        hr = br_ref[...]
        hi = bi_ref[...]
        for d in (1, 2, 4):
            keep = (row < 8 - d) if reverse else (row >= d)
            shift = (t - d) if reverse else d
            pr, pi = pows[d - 1]
            cr = jnp.where(keep, pr, 0.0)
            ci = jnp.where(keep, pi, 0.0)
            sr = pltpu.roll(hr, shift, 0)
            si = pltpu.roll(hi, shift, 0)
            hr, hi = hr + cr * sr - ci * si, hi + cr * si + ci * sr
        hr_ref[...] = hr
        hi_ref[...] = hi
        edge = 0 if reverse else 7

        def tile(k, carry):
            car_r, car_i = carry
            kk = (t // 8 - 1 - k) if reverse else k
            r0 = pl.multiple_of(kk * 8, 8)
            xr = hr_ref[pl.ds(r0, 8), :] + wr * car_r - wi * car_i
            xi = hi_ref[pl.ds(r0, 8), :] + wr * car_i + wi * car_r
            hr_ref[pl.ds(r0, 8), :] = xr
            hi_ref[pl.ds(r0, 8), :] = xi
            return (jnp.broadcast_to(xr[edge:edge + 1, :], (8, c)), jnp.broadcast_to(xi[edge:edge + 1, :], (8, c)))

        car_r, car_i = lax.fori_loop(0, t // 8, tile, (cr_ref[...], ci_ref[...]))
        cr_ref[...] = car_r
        ci_ref[...] = car_i

    spec = pl.BlockSpec((t, c), (lambda i: (nb - 1 - i, 0)) if reverse else (lambda i: (i, 0)))
    vspec = pl.BlockSpec((1, c), lambda i: (0, 0))
    hr, hi = _call(
        body,
        name=name,
        grid=(nb,),
        in_specs=[spec, spec, vspec, vspec],
        out_specs=[spec, spec],
        out_shape=[_sds((s, c)), _sds((s, c))],
        scratch_shapes=[pltpu.VMEM((8, c), f32), pltpu.VMEM((8, c), f32)],
        compiler_params=_params("arbitrary"),
    )(b_re, b_im, a_re, a_im)
    return hr, hi


def _causal_mask(t):
    row = lax.broadcasted_iota(jnp.int32, (t, t), 0)
    col = lax.broadcasted_iota(jnp.int32, (t, t), 1)
    return row >= col


def _attn_fwd(name, q, k, v, cq, ck):
    h, s, dh = q.shape
    t = ATT_TILE
    nq = s // t

    def body(q_ref, k_ref, v_ref, cq_ref, ck_ref, o_ref, lse_ref):
        qi = pl.program_id(1)
        qb = q_ref[...].astype(bf16)
        cqv = cq_ref[...]

        def block(kb, carry, masked):
            m, l, acc = carry
            ks = pl.multiple_of(kb * t, t)
            kk = k_ref[pl.ds(ks, t), :].astype(bf16)
            vv = v_ref[pl.ds(ks, t), :].astype(bf16)
            sc = lax.dot_general(qb, kk, _DN["nt"], preferred_element_type=f32) * ATT_SCALE + (cqv - ck_ref[kb])
            if masked:
                sc = jnp.where(_causal_mask(t), sc, -jnp.inf)
            mn = jnp.maximum(m, jnp.max(sc, axis=1, keepdims=True))
            p = jnp.exp(sc - mn)
            al = jnp.exp(m - mn)
            l = al * l + jnp.sum(p, axis=1, keepdims=True)
            acc = al * acc + jnp.dot(p.astype(bf16), vv, preferred_element_type=f32)
            return mn, l, acc

        init = (jnp.full((t, 1), -jnp.inf, f32), jnp.zeros((t, 1), f32), jnp.zeros((t, dh), f32))
        carry = lax.fori_loop(0, qi, lambda kb, c: block(kb, c, False), init)
        m, l, acc = block(qi, carry, True)
        o_ref[...] = acc / l
        lse_ref[...] = m + jnp.log(l)

    return pl.pallas_call(
        body,
        name=name,
        grid=(h, nq),
        in_specs=[pl.BlockSpec((None, t, dh), lambda hh, i: (hh, i, 0)),
                  pl.BlockSpec((None, s, dh), lambda hh, i: (hh, 0, 0)),
                  pl.BlockSpec((None, s, dh), lambda hh, i: (hh, 0, 0)),
                  pl.BlockSpec((None, t, 1), lambda hh, i: (hh, i, 0)),
                  pl.BlockSpec((None, nq, 1, t), lambda hh, i: (hh, 0, 0, 0))],
        out_specs=[pl.BlockSpec((None, t, dh), lambda hh, i: (hh, i, 0)),
                   pl.BlockSpec((None, t, 1), lambda hh, i: (hh, i, 0))],
        out_shape=[_sds((h, s, dh)), _sds((h, s, 1))],
        compiler_params=_params("parallel", "arbitrary"),
    )(q, k, v, cq, ck)


def _attn_bwd_dq(name, q, k, v, cq, ck, o, do, lse):
    h, s, dh = q.shape
    t = ATT_TILE
    nq = s // t

    def body(q_ref, k_ref, v_ref, cq_ref, ck_ref, o_ref, do_ref, lse_ref, dq_ref, dl_ref):
        qi = pl.program_id(1)
        qb = q_ref[...].astype(bf16)
        cqv = cq_ref[...]
        dov = do_ref[...]
        dob = dov.astype(bf16)
        delta = jnp.sum(dov * o_ref[...], axis=1, keepdims=True)
        lse_v = lse_ref[...]

        def block(kb, carry, masked):
            dq, psum = carry
            ks = pl.multiple_of(kb * t, t)
            kk = k_ref[pl.ds(ks, t), :].astype(bf16)
            vv = v_ref[pl.ds(ks, t), :].astype(bf16)
            sc = lax.dot_general(qb, kk, _DN["nt"], preferred_element_type=f32) * ATT_SCALE + (cqv - ck_ref[kb])
            p = jnp.exp(sc - lse_v)
            if masked:
                p = jnp.where(_causal_mask(t), p, 0.0)
            dp = lax.dot_general(dob, vv, _DN["nt"], preferred_element_type=f32)
            ds = p * (dp - delta)
            return (dq + jnp.dot(ds.astype(bf16), kk, preferred_element_type=f32),
                    psum + jnp.sum(p * dp, axis=1, keepdims=True))

        carry = lax.fori_loop(0, qi, lambda kb, c: block(kb, c, False), (jnp.zeros((t, dh), f32), jnp.zeros((t, 1), f32)))
        dq, psum = block(qi, carry, True)
        dq_ref[...] = dq * ATT_SCALE
        dl_ref[...] = psum

    qspec = pl.BlockSpec((None, t, dh), lambda hh, i: (hh, i, 0))
    fspec = pl.BlockSpec((None, s, dh), lambda hh, i: (hh, 0, 0))
    cspec = pl.BlockSpec((None, t, 1), lambda hh, i: (hh, i, 0))
    return pl.pallas_call(
        body,
        name=name,
        grid=(h, nq),
        in_specs=[qspec, fspec, fspec, cspec, pl.BlockSpec((None, nq, 1, t), lambda hh, i: (hh, 0, 0, 0)),
                  qspec, qspec, cspec],
        out_specs=[qspec, cspec],
        out_shape=[_sds((h, s, dh)), _sds((h, s, 1))],
        compiler_params=_params("parallel", "arbitrary"),
    )(q, k, v, cq, ck, o, do, lse)


def _attn_bwd_dkv(name, q, k, v, cq, ck, do, lse, delta):
    h, s, dh = q.shape
    t = ATT_TILE
    nq = s // t

    def body(q_ref, k_ref, v_ref, cq_ref, ck_ref, do_ref, lse_ref, dl_ref, dk_ref, dv_ref, dck_ref):
        kj = pl.program_id(1)
        kk = k_ref[...].astype(bf16)
        vv = v_ref[...].astype(bf16)
        ckv = ck_ref[...]

        def block(qi, carry, masked):
            dk, dv, dcs = carry
            qs = pl.multiple_of(qi * t, t)
            qq = q_ref[pl.ds(qs, t), :].astype(bf16)
            dob = do_ref[pl.ds(qs, t), :].astype(bf16)
            sc = (lax.dot_general(qq, kk, _DN["nt"], preferred_element_type=f32) * ATT_SCALE
                  + (cq_ref[pl.ds(qs, t), :] - ckv))
            p = jnp.exp(sc - lse_ref[pl.ds(qs, t), :])
            if masked:
                p = jnp.where(_causal_mask(t), p, 0.0)
            dv = dv + lax.dot_general(p.astype(bf16), dob, _DN["tn"], preferred_element_type=f32)
            dp = lax.dot_general(dob, vv, _DN["nt"], preferred_element_type=f32)
            ds = p * (dp - dl_ref[pl.ds(qs, t), :])
            dk = dk + lax.dot_general(ds.astype(bf16), qq, _DN["tn"], preferred_element_type=f32)
            return dk, dv, dcs + jnp.sum(ds, axis=0, keepdims=True)

        init = (jnp.zeros((t, dh), f32), jnp.zeros((t, dh), f32), jnp.zeros((1, t), f32))
        carry = block(kj, init, True)
        dk, dv, dcs = lax.fori_loop(kj + 1, nq, lambda qi, c: block(qi, c, False), carry)
        dk_ref[...] = dk * ATT_SCALE
        dv_ref[...] = dv
        dck_ref[...] = -dcs

    kspec = pl.BlockSpec((None, t, dh), lambda hh, j: (hh, j, 0))
    fspec = pl.BlockSpec((None, s, dh), lambda hh, j: (hh, 0, 0))
    fcol = pl.BlockSpec((None, s, 1), lambda hh, j: (hh, 0, 0))
    crow = pl.BlockSpec((None, None, 1, t), lambda hh, j: (hh, j, 0, 0))
    return pl.pallas_call(
        body,
        name=name,
        grid=(h, nq),
        in_specs=[fspec, kspec, kspec, fcol, crow, fspec, fcol, fcol],
        out_specs=[kspec, kspec, crow],
        out_shape=[_sds((h, s, dh)), _sds((h, s, dh)), _sds((h, nq, 1, t))],
        compiler_params=_params("parallel", "arbitrary"),
    )(q, k, v, cq, ck, do, lse, delta)


ATT_FEAT = 128
ATT_TQ = 1024
ATT_TK = 256


def _att_tiles(s):
    tq = min(ATT_TQ, s)
    return tq, ATT_TK, tq // ATT_TK


def _keys_le_queries(tk, tq, k0, q0):
    row = lax.broadcasted_iota(jnp.int32, (tk, tq), 0) + k0
    col = lax.broadcasted_iota(jnp.int32, (tk, tq), 1) + q0
    return row <= col


def _attn_fwd_t(name, qt_aug, k_aug, vt):
    h, _, s = qt_aug.shape
    tq, tk, ratio = _att_tiles(s)

    def body(qt_ref, k_ref, vt_ref, o_ref, lse_ref):
        qi = pl.program_id(1)
        qt = qt_ref[...]

        def block(kb, carry, masked):
            m, l, acc = carry
            ks = pl.multiple_of(kb * tk, tk)
            st = jnp.dot(k_ref[pl.ds(ks, tk), :], qt, preferred_element_type=f32)
            if masked:
                st = jnp.where(_keys_le_queries(tk, tq, ks, qi * tq), st, -jnp.inf)
            mn = jnp.maximum(m, jnp.max(st, axis=0, keepdims=True))
            p = jnp.exp(st - mn)
            al = jnp.exp(m - mn)
            l = al * l + jnp.sum(p, axis=0, keepdims=True)
            acc = al * acc + jnp.dot(vt_ref[kb], p.astype(bf16), preferred_element_type=f32)
            return mn, l, acc

        init = (jnp.full((1, tq), -jnp.inf, f32), jnp.zeros((1, tq), f32), jnp.zeros((HEAD_DIM, tq), f32))
        first = lax.fori_loop(0, qi * ratio, lambda kb, c: block(kb, c, False), init)
        m, l, acc = lax.fori_loop(qi * ratio, (qi + 1) * ratio, lambda kb, c: block(kb, c, True), first)
        o_ref[...] = acc / l
        lse_ref[...] = m + jnp.log(l)

    return _call(
        body,
        name=name,
        grid=(h, s // tq),
        in_specs=[pl.BlockSpec((None, ATT_FEAT, tq), lambda hh, i: (hh, 0, i)),
                  pl.BlockSpec((None, s, ATT_FEAT), lambda hh, i: (hh, 0, 0)),
                  pl.BlockSpec((None, s // tk, HEAD_DIM, tk), lambda hh, i: (hh, 0, 0, 0))],
        out_specs=[pl.BlockSpec((None, HEAD_DIM, tq), lambda hh, i: (hh, 0, i)),
                   pl.BlockSpec((None, 1, tq), lambda hh, i: (hh, 0, i))],
        out_shape=[_sds((h, HEAD_DIM, s)), _sds((h, 1, s))],
        compiler_params=_params("parallel", "arbitrary"),
    )(qt_aug, k_aug, vt)


def _attn_bwd_dq_t(name, qt_aug, k_aug, v, kt, ot, dot_, lse):
    h, _, s = qt_aug.shape
    tq, tk, ratio = _att_tiles(s)

    def body(qt_ref, k_ref, v_ref, kt_ref, o_ref, do_ref, lse_ref, dq_ref, dl_ref):
        qi = pl.program_id(1)
        qt = qt_ref[...]
        dov = do_ref[...]
        dob = dov.astype(bf16)
        delta = jnp.sum(dov * o_ref[...], axis=0, keepdims=True)
        lse_v = lse_ref[...]

        def block(kb, carry, masked):
            dq, psum = carry
            ks = pl.multiple_of(kb * tk, tk)
            st = jnp.dot(k_ref[pl.ds(ks, tk), :], qt, preferred_element_type=f32)
            p = jnp.exp(st - lse_v)
            if masked:
                p = jnp.where(_keys_le_queries(tk, tq, ks, qi * tq), p, 0.0)
            dp = jnp.dot(v_ref[pl.ds(ks, tk), :], dob, preferred_element_type=f32)
            ds = p * (dp - delta)
            return (dq + jnp.dot(kt_ref[kb], ds.astype(bf16), preferred_element_type=f32),
                    psum + jnp.sum(p * dp, axis=0, keepdims=True))

        carry = lax.fori_loop(0, qi * ratio, lambda kb, c: block(kb, c, False),
                              (jnp.zeros((HEAD_DIM, tq), f32), jnp.zeros((1, tq), f32)))
        dq, psum = lax.fori_loop(qi * ratio, (qi + 1) * ratio, lambda kb, c: block(kb, c, True), carry)
        dq_ref[...] = dq * ATT_SCALE
        dl_ref[...] = psum

    qspec = pl.BlockSpec((None, HEAD_DIM, tq), lambda hh, i: (hh, 0, i))
    rspec = pl.BlockSpec((None, 1, tq), lambda hh, i: (hh, 0, i))
    return _call(
        body,
        name=name,
        grid=(h, s // tq),
        in_specs=[pl.BlockSpec((None, ATT_FEAT, tq), lambda hh, i: (hh, 0, i)),
                  pl.BlockSpec((None, s, ATT_FEAT), lambda hh, i: (hh, 0, 0)),
                  pl.BlockSpec((None, s, HEAD_DIM), lambda hh, i: (hh, 0, 0)),
                  pl.BlockSpec((None, s // tk, HEAD_DIM, tk), lambda hh, i: (hh, 0, 0, 0)),
                  qspec, qspec, rspec],
        out_specs=[qspec, rspec],
        out_shape=[_sds((h, HEAD_DIM, s)), _sds((h, 1, s))],
        compiler_params=_params("parallel", "arbitrary"),
    )(qt_aug, k_aug, v, kt, ot, dot_, lse)


def _attn_bwd_dkv_t(name, qt_blocks, k_aug, v, qh, do, dot_blocks, lse, delta):
    h, s, _ = k_aug.shape
    tq, tk, ratio = _att_tiles(s)
    nq = s // tq

    def body(qt_ref, k_ref, v_ref, q_ref, do_ref, dot_ref, lse_ref, dl_ref, dk_ref, dv_ref, dck_ref, dsum_ref):
        kj = pl.program_id(1)
        kk = k_ref[...]
        vv = v_ref[...]
        dsum_ref[...] = jnp.zeros_like(dsum_ref)

        def block(qi, carry, masked):
            dk, dv = carry
            qs = pl.multiple_of(qi * tq, tq)
            st = jnp.dot(kk, qt_ref[qi], preferred_element_type=f32)
            p = jnp.exp(st - lse_ref[qi])
            if masked:
                p = jnp.where(_keys_le_queries(tk, tq, kj * tk, qs), p, 0.0)
            dv = dv + jnp.dot(p.astype(bf16), do_ref[pl.ds(qs, tq), :], preferred_element_type=f32)
            dp = jnp.dot(vv, dot_ref[qi], preferred_element_type=f32)
            ds = p * (dp - dl_ref[qi])
            dsum_ref[...] += ds
            dk = dk + jnp.dot(ds.astype(bf16), q_ref[pl.ds(qs, tq), :], preferred_element_type=f32)
            return dk, dv

        first = kj // ratio
        carry = block(first, (jnp.zeros((tk, HEAD_DIM), f32), jnp.zeros((tk, HEAD_DIM), f32)), True)
        dk, dv = lax.fori_loop(first + 1, nq, lambda qi, c: block(qi, c, False), carry)
        dk_ref[...] = dk
        dv_ref[...] = dv
        dck_ref[...] = -jnp.sum(dsum_ref[...], axis=1, keepdims=True)

    full = lambda shape: pl.BlockSpec((None,) + shape, lambda hh, j: (hh,) + (0,) * len(shape))
    kspec = pl.BlockSpec((None, tk, HEAD_DIM), lambda hh, j: (hh, j, 0))
    return _call(
        body,
        name=name,
        grid=(h, s // tk),
        in_specs=[full((nq, ATT_FEAT, tq)),
                  pl.BlockSpec((None, tk, ATT_FEAT), lambda hh, j: (hh, j, 0)),
                  kspec, full((s, HEAD_DIM)), full((s, HEAD_DIM)), full((nq, HEAD_DIM, tq)),
                  full((nq, 1, tq)), full((nq, 1, tq))],
        out_specs=[kspec, kspec, pl.BlockSpec((None, tk, 1), lambda hh, j: (hh, j, 0))],
        out_shape=[_sds((h, s, HEAD_DIM)), _sds((h, s, HEAD_DIM)), _sds((h, s, 1))],
        scratch_shapes=[pltpu.VMEM((tk, tq), f32)],
        compiler_params=_params("parallel", "arbitrary"),
    )(qt_blocks, k_aug, v, qh, do, dot_blocks, lse, delta)


def _split3(c):
    hi = lax.reduce_precision(c, 8, 7)
    r = c - hi
    mid = lax.reduce_precision(r, 8, 7)
    lo = lax.reduce_precision(r - mid, 8, 7)
    return hi.astype(bf16), mid.astype(bf16), lo.astype(bf16)


def _attn_operands(q, k, v, c6):
    s = q.shape[0]
    tq, tk, _ = _att_tiles(s)
    qh = _heads(q * ATT_SCALE).astype(bf16)
    kh = _heads(k).astype(bf16)
    vh = _heads(v).astype(bf16)
    parts = _split3(c6)
    ones = jnp.ones((N_HEADS, s, 3), bf16)
    cpos = jnp.stack(parts, axis=-1)
    pad = jnp.zeros((N_HEADS, s, ATT_FEAT - HEAD_DIM - 6), bf16)
    q_aug = jnp.concatenate([qh, cpos, ones, pad], axis=-1)
    k_aug = jnp.concatenate([kh, ones, -cpos, pad], axis=-1)

    def blocks_t(a, t):
        return a.reshape(N_HEADS, s // t, t, a.shape[-1]).transpose(0, 1, 3, 2)

    return dict(qt_aug=q_aug.transpose(0, 2, 1), qt_blocks=blocks_t(q_aug, tq), k_aug=k_aug, v=vh,
                vt=blocks_t(vh, tk), kt=blocks_t(kh, tk), qh=qh)


def _s5_disc_fn(are, aim, ldt):
    dt = jnp.exp(ldt)
    er = jnp.exp(are * dt)
    br = er * jnp.cos(aim * dt)
    bi = er * jnp.sin(aim * dt)
    nr = br - 1.0
    den = are * are + aim * aim
    return br, bi, (nr * are + bi * aim) / den, (bi * are - nr * aim) / den


def _s5_disc(name, are, aim, ldt):
    def body(a_ref, b_ref, c_ref, o0, o1, o2, o3):
        r = _s5_disc_fn(a_ref[...], b_ref[...], c_ref[...])
        o0[...], o1[...], o2[...], o3[...] = r

    shp = _sds((S5_GROUPS, S5_STATE))
    return pl.pallas_call(body, name=name, out_shape=[shp] * 4)(are, aim, ldt)


def _s5_disc_bwd(name, are, aim, ldt, cts):
    def body(a_ref, b_ref, c_ref, d0, d1, d2, d3, o0, o1, o2):
        _, vjp = jax.vjp(_s5_disc_fn, a_ref[...], b_ref[...], c_ref[...])
        o0[...], o1[...], o2[...] = vjp((d0[...], d1[...], d2[...], d3[...]))

    shp = _sds((S5_GROUPS, S5_STATE))
    return pl.pallas_call(body, name=name, out_shape=[shp, shp, _sds((S5_GROUPS, 1))])(are, aim, ldt, *cts)


def _adamw_rows(w, g, m, v):
    m = ADAM_B1 * m + (1.0 - ADAM_B1) * g
    v = ADAM_B2 * v + (1.0 - ADAM_B2) * (g * g)
    m_hat = m / (1.0 - ADAM_B1 ** ADAM_STEP)
    v_hat = v / (1.0 - ADAM_B2 ** ADAM_STEP)
    return -ADAM_LR * (m_hat / (jnp.sqrt(v_hat) + ADAM_EPS) + ADAM_WD * w), m, v


def _adamw(name, w, ga, gb, m, v):
    rows, cols = w.shape
    tr = _row_tile(rows)

    def body(w_ref, ga_ref, gb_ref, m_ref, v_ref, g_out, d_out, m_out, v_out):
        g = ga_ref[...] + gb_ref[...]
        d, mm, vv = _adamw_rows(w_ref[...], g, m_ref[...], v_ref[...])
        g_out[...] = g
        d_out[...] = d
        m_out[...] = mm
        v_out[...] = vv

    spec = pl.BlockSpec((tr, cols), lambda i: (i, 0))
    return pl.pallas_call(
        body, name=name, grid=(rows // tr,), in_specs=[spec] * 5, out_specs=[spec] * 4,
        out_shape=[_sds((rows, cols))] * 4, compiler_params=_params("parallel"),
    )(w, ga, gb, m, v)


def _sum_stack(name, st):
    n, rows, cols = st.shape
    tr = _row_tile(rows)

    def body(s_ref, o_ref):
        acc = s_ref[0].astype(f32)
        for j in range(1, n):
            acc = acc + s_ref[j].astype(f32)
        o_ref[...] = acc

    return pl.pallas_call(
        body, name=name, grid=(rows // tr,), in_specs=[pl.BlockSpec((n, tr, cols), lambda i: (0, i, 0))],
        out_specs=pl.BlockSpec((tr, cols), lambda i: (i, 0)), out_shape=_sds((rows, cols)),
        compiler_params=_params("parallel"),
    )(st)


def _block_diag(w):
    h, n, m = w.shape
    return jnp.einsum("hij,hg->higj", w, jnp.eye(h, dtype=w.dtype)).reshape(h * n, h * m)


def _block_diag_part(dense, h):
    n, m = dense.shape[0] // h, dense.shape[1] // h
    return jnp.einsum("higj,hg->hij", dense.reshape(h, n, h, m), jnp.eye(h, dtype=dense.dtype))


def _s5_matrices(coef_re, coef_im, b_re, b_im, c_re, c_im):
    bb_re = coef_re[:, :, None] * b_re - coef_im[:, :, None] * b_im
    bb_im = coef_re[:, :, None] * b_im + coef_im[:, :, None] * b_re
    wb_re = _block_diag(jnp.swapaxes(bb_re, 1, 2))
    wb_im = _block_diag(jnp.swapaxes(bb_im, 1, 2))
    wc_re = _block_diag(jnp.swapaxes(c_re, 1, 2))
    wc_im = _block_diag(jnp.swapaxes(-c_im, 1, 2))
    return wb_re, wb_im, wc_re, wc_im


def _heads(t):
    s = t.shape[0]
    return t.reshape(s, N_HEADS, HEAD_DIM).transpose(1, 0, 2)


def _unheads(t):
    s = t.shape[1]
    return t.transpose(1, 0, 2).reshape(s, N_HEADS * HEAD_DIM)


def _shift_down(t):
    return jnp.concatenate([jnp.zeros((1, t.shape[1]), t.dtype), t[:-1]], axis=0)


def _shift_up(t):
    return jnp.concatenate([t[1:], jnp.zeros((1, t.shape[1]), t.dtype)], axis=0)


def _row(v):
    return v.reshape(1, -1)


def _ffn_fwd(tag, h, get, names, gamma, beta):
    wg, wu = get(names[0]), get(names[1])
    g, u, act = _ffn_up(tag + "_up", h, wg, wu)
    wd = get(names[2])
    r, out = _mm_ln(tag + "_down", act, wd, h, gamma, beta, 0.5, k_slabs=True)
    return out, dict(h=h, g=g, u=u, act=act, r=r, wg=wg, wu=wu, wd=wd)


def _ffn_bwd(tag, dout, sv, names, gamma, put):
    s = dout.shape[0]
    dr, dgam, dbet = _ln_bwd(tag + "_lnb", sv["r"], dout, gamma)
    put(names[2], _mm_plain(tag + "_dwd", "tn", _Slabs(sv["act"]), dr, (D_FF, D_MODEL, s), scale=0.5, out_dtype=bf16,
                            tiles=(FF_SLAB, 1024, _tile(s, 1024))))
    dg, du = _ffn_dact(tag + "_dact", dr, sv["wd"], sv["g"], sv["u"])
    dwg, dwu = _mm2(tag + "_dwgu", "tn", (D_MODEL, D_FF, s), sv["h"], _Slabs(dg), None, _Slabs(du), separate=True,
                    out_dtype=bf16, split_cols=True, tiles=(512, FF_SLAB, _tile(s, 1024)))
    put(names[0], dwg)
    put(names[1], dwu)
    dh = _mm2(tag + "_dh", "nt", (s, D_MODEL, D_FF), _Slabs(dg), _Slabs(sv["wg"]), _Slabs(du), _Slabs(sv["wu"]),
              add=dr, add_coef=ALPHA, tiles=(_tile(s, 512), 1024, FF_SLAB))[0]
    return dh, dgam, dbet


def _mixer_fwd(tag, h1, w):
    s = h1.shape[0]
    nt = s // ATT_TILE
    z = _mm_plain(tag + "_win", "nn", h1, w["w_in"], (s, N_IN_P, D_MODEL), tiles=(_tile(s, 512), 768, D_MODEL))
    ax, ag = z[:, :D_A], z[:, D_A:2 * D_A]
    q, k, v = z[:, 2 * D_A:3 * D_A], z[:, 3 * D_A:4 * D_A], z[:, 4 * D_A:5 * D_A]
    f, cu = z[:, F_OFF:F_OFF + 128], z[:, CU_OFF:]
    xa = _conv_fwd(tag + "_conv", ax, w["conv_w"], w["conv_b"])
    a, gated = _rg_gates(tag + "_gates", xa, w["rg_wa"], w["rg_wx"], w["rg_ba"], w["rg_bx"], w["rg_lam"])
    ha = _lin_scan(tag + "_rgscan", a, gated, False)
    ones = jnp.ones((s, 128), f32)
    c = _lin_scan(tag + "_cumf", ones, _log_f(tag + "_logf", f, w["fox_bf"]), False)
    att = _attn_operands(q, k, v, c[:, :N_HEADS].T)
    ot, lse = _attn_fwd_t(tag + "_attn", att["qt_aug"], att["k_aug"], att["vt"])
    ob = ot.reshape(D_B, s).T
    bu_re, bu_im = _mm2(tag + "_s5in", "nn", (s, S5_LANES, D_C), cu, w["wb_re"], None, w["wb_im"], separate=True,
                        tiles=(_tile(s, 512), 1024, D_C))
    hre, him = _s5_scan(tag + "_s5scan", bu_re, bu_im, w["abar_re"], w["abar_im"], False)
    o = _mix_out(tag + "_mixout", ag, ha, ob, hre, him, cu, w["s5_d"], w["mix_g"], w["wc_re"], w["wc_im"], w["w_glu"])
    sv = dict(h1=h1, ax=ax, ag=ag, f=f, cu=cu, xa=xa, a=a, ha=ha, att=att, ot=ot, lse=lse, ob=ob, hre=hre, him=him, o=o)
    return o, sv


def _mixer_bwd(tag, do, dr2, sv, w, put):
    s = do.shape[0]
    (dag, dha, dob, dhre, dhim, dcu1, dwcr, dwci, dwglu, dd, dgn) = _mix_out_bwd(
        tag + "_mixoutb", do, sv["ag"], sv["ha"], sv["ob"], sv["hre"], sv["him"], sv["cu"], w["s5_d"], w["mix_g"],
        w["wc_re"], w["wc_im"], w["w_glu"])
    put("s5_w_glu", dwglu.astype(bf16))
    gre, gim = _s5_scan(tag + "_s5scanb", dhre, dhim, w["abar_re"], -w["abar_im"], True)
    dab_re, dab_im = _s5_decay_grad(tag + "_s5dec", _shift_down(sv["hre"]), _shift_down(sv["him"]), gre, gim)
    dwb_re, dwb_im = _mm2(tag + "_s5dwb", "tn", (D_C, S5_LANES, s), sv["cu"], gre, None, gim, separate=True,
                          tiles=(D_C, 1024, _tile(s, 1024)))
    dcu = _mm2(tag + "_s5dcu", "nt", (s, D_C, S5_LANES), gre, w["wb_re"], gim, w["wb_im"], add=dcu1,
               tiles=(_tile(s, 512), D_C, 1024))[0]
    att = sv["att"]
    tq = _att_tiles(s)[0]
    nt = s // tq
    dot_ = dob.T.reshape(N_HEADS, HEAD_DIM, s)
    dqt, delta = _attn_bwd_dq_t(tag + "_attndq", att["qt_aug"], att["k_aug"], att["v"], att["kt"], sv["ot"], dot_,
                                sv["lse"])
    dot_blocks = dot_.astype(bf16).reshape(N_HEADS, HEAD_DIM, nt, tq).transpose(0, 2, 1, 3)
    dkh, dvh, dck = _attn_bwd_dkv_t(tag + "_attndkv", att["qt_blocks"], att["k_aug"], att["v"], att["qh"],
                                    _heads(dob).astype(bf16), dot_blocks, sv["lse"].reshape(N_HEADS, nt, 1, tq),
                                    delta.reshape(N_HEADS, nt, 1, tq))
    dq, dk, dv = dqt.reshape(D_B, s).T, _unheads(dkh), _unheads(dvh)
    dc = jnp.pad(dck[:, :, 0].T, ((0, 0), (0, 128 - N_HEADS)))
    dlf = _lin_scan(tag + "_cumfb", jnp.ones((s, 128), f32), dc, True)
    df, dbf = _log_f_bwd(tag + "_logfb", dlf, sv["f"], w["fox_bf"])
    ga = _lin_scan(tag + "_rgscanb", _shift_up(sv["a"]), dha, True)
    dxa, dwa, dwx, dba, dbx, dlam = _rg_gates_bwd(tag + "_gatesb", sv["xa"], ga, _shift_down(sv["ha"]), w["rg_wa"],
                                                  w["rg_wx"], w["rg_ba"], w["rg_bx"], w["rg_lam"])
    dax, dconv = _conv_bwd(tag + "_convb", dxa, sv["ax"], w["conv_w"])
    dz = jnp.concatenate([dax, dag, dq, dk, dv, df, dcu], axis=-1).astype(bf16)
    put("w_in", _mm_plain(tag + "_dwin", "tn", sv["h1"], dz, (D_MODEL, N_IN_P, s), out_dtype=bf16,
                          tiles=(512, 768, _tile(s, 1024))))
    dh1 = _mm_plain(tag + "_dh1", "nt", dz, w["w_in"], (s, D_MODEL, N_IN_P), add=dr2, add_coef=ALPHA,
                    tiles=(_tile(s, 512), 1024, 768))
    grads = dict(dconv=dconv, dwa=dwa, dwx=dwx, dba=dba, dbx=dbx, dlam=dlam, dbf=dbf,
                 dab_re=dab_re, dab_im=dab_im, dwb_re=dwb_re, dwb_im=dwb_im, dwcr=dwcr, dwci=dwci, dd=dd, dgn=dgn)
    return dh1, grads


SMALL_NAMES = ["ln1_g", "ln1_b", "conv_w", "conv_b", "rg_w_a", "rg_b_a", "rg_w_x", "rg_b_x", "rg_lambda", "fox_b_f",
               "s5_a_re", "s5_a_im", "s5_log_dt", "s5_b_re", "s5_b_im", "s5_c_re", "s5_c_im", "s5_d", "mix_norm_g",
               "ln2_g", "ln2_b", "ln3_g", "ln3_b"]
BIG_NAMES = ["ffn1_w_gate", "ffn1_w_up", "ffn1_w_down", "w_in", "s5_w_glu", "w_out", "ffn2_w_gate", "ffn2_w_up",
             "ffn2_w_down"]


def _local_step(x, target, weight, small, on_grads):
    h = x
    saved = []
    for l in range(DEPTH):
        get = functools.partial(weight, l)

        sm = {n: small[n][l] for n in SMALL_NAMES}
        abar_re, abar_im, coef_re, coef_im = _s5_disc(f"l{l}_s5disc", sm["s5_a_re"], sm["s5_a_im"],
                                                      sm["s5_log_dt"].reshape(S5_GROUPS, 1))
        mats, mats_vjp = jax.vjp(_s5_matrices, coef_re, coef_im, sm["s5_b_re"], sm["s5_b_im"], sm["s5_c_re"],
                                 sm["s5_c_im"])
        w = dict(
            conv_w=sm["conv_w"], conv_b=_row(sm["conv_b"]),
            rg_wa=_block_diag(sm["rg_w_a"]).astype(bf16), rg_wx=_block_diag(sm["rg_w_x"]).astype(bf16),
            rg_ba=_row(sm["rg_b_a"]), rg_bx=_row(sm["rg_b_x"]), rg_lam=_row(sm["rg_lambda"]),
            fox_bf=jnp.pad(_row(sm["fox_b_f"]), ((0, 0), (0, 128 - N_HEADS))),
            abar_re=_row(abar_re), abar_im=_row(abar_im),
            wb_re=mats[0].astype(bf16), wb_im=mats[1].astype(bf16), wc_re=mats[2].astype(bf16),
            wc_im=mats[3].astype(bf16), s5_d=_row(sm["s5_d"]), mix_g=_row(sm["mix_norm_g"]))
        h1, sv1 = _ffn_fwd(f"l{l}_ffn1", h, get, GROUPS["F1"], _row(sm["ln1_g"]), _row(sm["ln1_b"]))
        w["w_in"], w["w_glu"] = get("w_in"), get("s5_w_glu")
        o, svm = _mixer_fwd(f"l{l}_mix", h1, w)
        w_out = get("w_out")
        r2, h2 = _mm_ln(f"l{l}_wout", o, w_out, h1, _row(sm["ln2_g"]), _row(sm["ln2_b"]), 1.0)
        h3, sv2 = _ffn_fwd(f"l{l}_ffn2", h2, get, GROUPS["F2"], _row(sm["ln3_g"]), _row(sm["ln3_b"]))
        saved.append(dict(sm=sm, w=w, w_out=w_out, sv1=sv1, svm=svm, r2=r2, sv2=sv2, mats_vjp=mats_vjp))
        h = h3

    dh, loss_row = _loss_head("loss_head", h, target)
    s = x.shape[0]
    gsmall = {n: [None] * DEPTH for n in SMALL_NAMES}
    for l in reversed(range(DEPTH)):
        sd = saved[l]
        sm, w = sd["sm"], sd["w"]

        def put(name, grad, l=l):
            on_grads((l, name), grad)

        dh2, dgam, dbet = _ffn_bwd(f"l{l}_ffn2", dh, sd["sv2"], GROUPS["F2"], _row(sm["ln3_g"]), put)
        gsmall["ln3_g"][l], gsmall["ln3_b"][l] = dgam[0], dbet[0]
        dr2, dgam, dbet = _ln_bwd(f"l{l}_ln2b", sd["r2"], dh2, _row(sm["ln2_g"]))
        gsmall["ln2_g"][l], gsmall["ln2_b"][l] = dgam[0], dbet[0]
        put("w_out", _mm_plain(f"l{l}_dwout", "tn", sd["svm"]["o"], dr2, (D_MODEL, D_MODEL, s), out_dtype=bf16))
        do = _mm_plain(f"l{l}_do", "nt", dr2, sd["w_out"], (s, D_MODEL, D_MODEL))
        dh1, g = _mixer_bwd(f"l{l}_mix", do, dr2, sd["svm"], w, put)
        gsmall["conv_w"][l], gsmall["conv_b"][l] = g["dconv"][:CONV_WIDTH], g["dconv"][CONV_WIDTH]
        gsmall["rg_w_a"][l] = _block_diag_part(g["dwa"], N_HEADS)
        gsmall["rg_w_x"][l] = _block_diag_part(g["dwx"], N_HEADS)
        gsmall["rg_b_a"][l], gsmall["rg_b_x"][l], gsmall["rg_lambda"][l] = g["dba"][0], g["dbx"][0], g["dlam"][0]
        gsmall["fox_b_f"][l] = g["dbf"][0, :N_HEADS]
        dcoef_re, dcoef_im, db_re, db_im, dc_re, dc_im = sd["mats_vjp"]((g["dwb_re"], g["dwb_im"], g["dwcr"], g["dwci"]))
        da_re, da_im, dldt = _s5_disc_bwd(
            f"l{l}_s5discb", sm["s5_a_re"], sm["s5_a_im"], sm["s5_log_dt"].reshape(S5_GROUPS, 1),
            (g["dab_re"].reshape(S5_GROUPS, S5_STATE), g["dab_im"].reshape(S5_GROUPS, S5_STATE), dcoef_re, dcoef_im))
        gsmall["s5_a_re"][l], gsmall["s5_a_im"][l], gsmall["s5_log_dt"][l] = da_re, da_im, dldt[:, 0]
        gsmall["s5_b_re"][l], gsmall["s5_b_im"][l], gsmall["s5_c_re"][l], gsmall["s5_c_im"][l] = db_re, db_im, dc_re, dc_im
        gsmall["s5_d"][l], gsmall["mix_norm_g"][l] = g["dd"][0], g["dgn"][0]
        dh, dgam, dbet = _ffn_bwd(f"l{l}_ffn1", dh1, sd["sv1"], GROUPS["F1"], _row(sm["ln1_g"]), put)
        gsmall["ln1_g"][l], gsmall["ln1_b"][l] = dgam[0], dbet[0]
    gsmall = {n: jnp.stack(v) for n, v in gsmall.items()}
    return loss_row[0, 0], dh, gsmall


def _position():
    return lax.axis_index("x"), lax.axis_index("y"), lax.axis_index("c")


_ANY = pl.BlockSpec(memory_space=pl.ANY)


def _chip_gather(name, shards):
    n = len(shards)

    def body(*refs):
        in_refs, out_refs = refs[:n], refs[n:2 * n]
        send_sems, recv_sems, local_sems = refs[2 * n:]
        x, y, c = _position()
        me = 2 * x + y
        peers = [(1 - x, y), (x, 1 - y), (1 - x, 1 - y)]
        local = [pltpu.make_async_copy(in_refs[i], out_refs[i].at[me], local_sems.at[i]) for i in range(n)]
        for cp in local:
            cp.start()
        sends = []
        for i in range(n):
            for r, (px, py) in enumerate(peers):
                cp = pltpu.make_async_remote_copy(
                    src_ref=in_refs[i], dst_ref=out_refs[i].at[me], send_sem=send_sems.at[3 * i + r],
                    recv_sem=recv_sems.at[3 * i + r], device_id=(px, py, c), device_id_type=MESH)
                cp.start()
                sends.append(cp)
        for i in range(n):
            for r, (px, py) in enumerate(peers):
                pltpu.make_async_remote_copy(
                    src_ref=in_refs[i], dst_ref=out_refs[i].at[2 * px + py], send_sem=send_sems.at[3 * i + r],
                    recv_sem=recv_sems.at[3 * i + r], device_id=(px, py, c), device_id_type=MESH).wait_recv()
        for cp in sends:
            cp.wait_send()
        for cp in local:
            cp.wait()

    return pl.pallas_call(
        body, name=name, in_specs=[_ANY] * n, out_specs=[_ANY] * n,
        out_shape=[_sds((N_CHIPS,) + a.shape, a.dtype) for a in shards],
        scratch_shapes=[pltpu.SemaphoreType.DMA((3 * n,)), pltpu.SemaphoreType.DMA((3 * n,)),
                        pltpu.SemaphoreType.DMA((n,))],
    )(*shards)


def _chip_scatter(name, stacks):
    n = len(stacks)

    def body(*refs):
        in_refs, out_refs = refs[:n], refs[n:2 * n]
        send_sems, recv_sems, local_sems = refs[2 * n:]
        x, y, c = _position()
        me = 2 * x + y
        peers = [(1 - x, y), (x, 1 - y), (1 - x, 1 - y)]
        local = [pltpu.make_async_copy(in_refs[i].at[me], out_refs[i].at[me], local_sems.at[i]) for i in range(n)]
        for cp in local:
            cp.start()
        sends = []
        for i in range(n):
            for r, (px, py) in enumerate(peers):
                cp = pltpu.make_async_remote_copy(
                    src_ref=in_refs[i].at[2 * px + py], dst_ref=out_refs[i].at[me], send_sem=send_sems.at[3 * i + r],
                    recv_sem=recv_sems.at[3 * i + r], device_id=(px, py, c), device_id_type=MESH)
                cp.start()
                sends.append(cp)
        for i in range(n):
            for r, (px, py) in enumerate(peers):
                pltpu.make_async_remote_copy(
                    src_ref=in_refs[i].at[me], dst_ref=out_refs[i].at[2 * px + py], send_sem=send_sems.at[3 * i + r],
                    recv_sem=recv_sems.at[3 * i + r], device_id=(px, py, c), device_id_type=MESH).wait_recv()
        for cp in sends:
            cp.wait_send()
        for cp in local:
            cp.wait()

    return pl.pallas_call(
        body, name=name, in_specs=[_ANY] * n, out_specs=[_ANY] * n,
        out_shape=[_sds(a.shape, a.dtype) for a in stacks],
        scratch_shapes=[pltpu.SemaphoreType.DMA((3 * n,)), pltpu.SemaphoreType.DMA((3 * n,)),
                        pltpu.SemaphoreType.DMA((n,))],
    )(*stacks)


def _sibling_swap(name, arrs):
    n = len(arrs)

    def body(*refs):
        in_refs, out_refs = refs[:n], refs[n:2 * n]
        send_sems, recv_sems = refs[2 * n:]
        x, y, c = _position()
        copies = [pltpu.make_async_remote_copy(
            src_ref=in_refs[i], dst_ref=out_refs[i], send_sem=send_sems.at[i], recv_sem=recv_sems.at[i],
            device_id=(x, y, 1 - c), device_id_type=MESH) for i in range(n)]
        for cp in copies:
            cp.start()
        for cp in copies:
            cp.wait_recv()
        for cp in copies:
            cp.wait_send()

    return pl.pallas_call(
        body, name=name, in_specs=[_ANY] * n, out_specs=[_ANY] * n,
        out_shape=[_sds(a.shape, a.dtype) for a in arrs],
        scratch_shapes=[pltpu.SemaphoreType.DMA((n,)), pltpu.SemaphoreType.DMA((n,))],
    )(*arrs)


def _dev_gather(name, arr):
    def body(in_ref, out_ref, send_sems, recv_sems, local_sem):
        x, y, c = _position()
        me = 4 * x + 2 * y + c
        local = pltpu.make_async_copy(in_ref, out_ref.at[me], local_sem)
        local.start()
        peers = []
        for k in range(1, N_DEV):
            peers.append((1 - x if k & 4 else x, 1 - y if k & 2 else y, 1 - c if k & 1 else c))
        sends = []
        for k, peer in enumerate(peers):
            cp = pltpu.make_async_remote_copy(src_ref=in_ref, dst_ref=out_ref.at[me], send_sem=send_sems.at[k],
                                              recv_sem=recv_sems.at[k], device_id=peer, device_id_type=MESH)
            cp.start()
            sends.append(cp)
        for k, (px, py, pc) in enumerate(peers):
            pltpu.make_async_remote_copy(src_ref=in_ref, dst_ref=out_ref.at[4 * px + 2 * py + pc],
                                         send_sem=send_sems.at[k], recv_sem=recv_sems.at[k], device_id=(px, py, pc),
                                         device_id_type=MESH).wait_recv()
        for cp in sends:
            cp.wait_send()
        local.wait()

    return pl.pallas_call(
        body, name=name, in_specs=[_ANY], out_specs=_ANY, out_shape=_sds((N_DEV,) + arr.shape, arr.dtype),
        scratch_shapes=[pltpu.SemaphoreType.DMA((N_DEV - 1,)), pltpu.SemaphoreType.DMA((N_DEV - 1,)),
                        pltpu.SemaphoreType.DMA],
    )(arr)


COLUMN_SHARDED = ("ffn1_w_gate", "ffn1_w_up", "ffn2_w_gate", "ffn2_w_up")
PACK_QUANTUM = 128 * 256


def _permute_in_cols(w):
    pad = jnp.zeros(w.shape[:-1] + (128 - N_HEADS,), w.dtype)
    return jnp.concatenate([w[..., :F_OFF + N_HEADS], pad, w[..., F_OFF + N_HEADS:]], axis=-1)


def _unpermute_in_cols(w):
    return jnp.concatenate([w[..., :F_OFF + N_HEADS], w[..., CU_OFF:]], axis=-1)


def _unstack(name, st):
    _, l, r, c = st.shape
    if name in COLUMN_SHARDED:
        return st.transpose(1, 2, 0, 3).reshape(l, r, N_CHIPS * c)
    return st.transpose(1, 0, 2, 3).reshape(l, N_CHIPS * r, c)


def _restack(name, g):
    l, r, c = g.shape
    if name in COLUMN_SHARDED:
        return g.reshape(l, r, N_CHIPS, c // N_CHIPS).transpose(2, 0, 1, 3)
    return g.reshape(l, N_CHIPS, r // N_CHIPS, c).transpose(1, 0, 2, 3)


def _pack(arrs):
    flat = jnp.concatenate([a.reshape(-1) for a in arrs])
    pad = -flat.shape[0] % PACK_QUANTUM
    return jnp.pad(flat, (0, pad)).reshape(-1, 128)


def _unpack(buf, shapes):
    flat = buf.reshape(-1)
    out, off = [], 0
    for shp in shapes:
        size = math.prod(shp)
        out.append(flat[off:off + size].reshape(shp))
        off += size
    return out


WEIGHT_NAMES = ["ffn1_w_gate", "ffn1_w_up", "ffn1_w_down", "ln1_g", "ln1_b", "w_in", "conv_w", "conv_b", "rg_w_a",
                "rg_b_a", "rg_w_x", "rg_b_x", "rg_lambda", "fox_b_f", "s5_a_re", "s5_a_im", "s5_log_dt", "s5_b_re",
                "s5_b_im", "s5_c_re", "s5_c_im", "s5_d", "s5_w_glu", "mix_norm_g", "w_out", "ln2_g", "ln2_b",
                "ffn2_w_gate", "ffn2_w_up", "ffn2_w_down", "ln3_g", "ln3_b"]


def _train_step(x, loss_target, w, m, v):
    ix, iy, _ = _position()
    chip = 2 * ix + iy

    shards = [(_permute_in_cols(w[n]) if n == "w_in" else w[n]).astype(bf16) for n in BIG_NAMES]
    stacks = _chip_gather("gather_weights", shards + [w["conv_w"]])
    big = {n: _unstack(n, st) for n, st in zip(BIG_NAMES, stacks)}
    small = {n: w[n] for n in SMALL_NAMES}
    small["conv_w"] = stacks[-1].transpose(1, 2, 0, 3).reshape(DEPTH, CONV_WIDTH, D_A)

    loss_local, gx, gbig, gsmall = _local_step(x[0], loss_target[0], big, small)

    sent = [_restack(n, gbig[n]).astype(bf16) for n in BIG_NAMES]
    recv = _chip_scatter("scatter_grads", sent)
    partial = {}
    for n, st in zip(BIG_NAMES, recv):
        _, l, r, c = st.shape
        p = _sum_stack("sum_" + n, st.reshape(N_CHIPS, l * r, c))
        partial[n] = _unpermute_in_cols(p) if n == "w_in" else p
    other = dict(zip(BIG_NAMES, _sibling_swap("swap_grads", [partial[n] for n in BIG_NAMES])))

    small_shapes = [gsmall[n].shape for n in SMALL_NAMES]
    total = _sum_stack("sum_small", _dev_gather("gather_small", _pack([gsmall[n] for n in SMALL_NAMES])))
    gsm = dict(zip(SMALL_NAMES, _unpack(total, small_shapes)))
    cw = D_A // N_CHIPS
    gsm["conv_w"] = lax.dynamic_slice_in_dim(gsm["conv_w"], chip * cw, cw, axis=2)

    grads, deltas, new_m, new_v = {}, {}, {}, {}
    for n in BIG_NAMES:
        shp = w[n].shape
        two_d = (shp[0] * shp[1], shp[2])
        g, d, mm, vv = _adamw("adamw_" + n, w[n].reshape(two_d), partial[n], other[n], m[n].reshape(two_d),
                              v[n].reshape(two_d))
        grads[n], deltas[n], new_m[n], new_v[n] = (t.reshape(shp) for t in (g, d, mm, vv))
    shapes = [w[n].shape for n in SMALL_NAMES]
    gp = _pack([gsm[n] for n in SMALL_NAMES])
    res = _adamw("adamw_small", _pack([w[n] for n in SMALL_NAMES]), gp, jnp.zeros_like(gp),
                 _pack([m[n] for n in SMALL_NAMES]), _pack([v[n] for n in SMALL_NAMES]))
    for dst, buf in zip((grads, deltas, new_m, new_v), res):
        dst.update(zip(SMALL_NAMES, _unpack(buf, shapes)))

    loss = lax.psum(loss_local, ("x", "y", "c"))
    return (loss, gx[None], *[grads[n] for n in WEIGHT_NAMES], *[deltas[n] for n in WEIGHT_NAMES],
            *[new_m[n] for n in WEIGHT_NAMES], *[new_v[n] for n in WEIGHT_NAMES])


def _remote(src, dst, send_sems, recv_sems, k, peer):
    return pltpu.make_async_remote_copy(src_ref=src, dst_ref=dst, send_sem=send_sems.at[k], recv_sem=recv_sems.at[k],
                                        device_id=peer, device_id_type=MESH)


class _ChipGatherPart:
    def __init__(self, arrays):
        self.arrays, self.results = list(arrays), None

    def out_shape(self):
        return [_sds((N_CHIPS,) + a.shape, a.dtype) for a in self.arrays]

    def sems(self):
        n = len(self.arrays)
        return [pltpu.SemaphoreType.DMA((3 * n,)), pltpu.SemaphoreType.DMA((3 * n,)), pltpu.SemaphoreType.DMA((n,))]

    def copies(self, ins, outs, sems):
        send_sems, recv_sems, local_sems = sems
        x, y, c = _position()
        me = 2 * x + y
        local, sends, recvs = [], [], []
        for i, (src, dst) in enumerate(zip(ins, outs)):
            local.append(pltpu.make_async_copy(self.mine(src, me), dst.at[me], local_sems.at[i]))
            for r, (px, py) in enumerate([(1 - x, y), (x, 1 - y), (1 - x, 1 - y)]):
                peer = 2 * px + py
                sends.append(_remote(self.theirs(src, peer), dst.at[me], send_sems, recv_sems, 3 * i + r, (px, py, c)))
                recvs.append(_remote(self.mine(src, me), dst.at[peer], send_sems, recv_sems, 3 * i + r, (px, py, c)))
        return local, sends, recvs

    def mine(self, src, me):
        return src

    def theirs(self, src, peer):
        return src


class _ChipScatterPart(_ChipGatherPart):
    def out_shape(self):
        return [_sds(a.shape, a.dtype) for a in self.arrays]

    def mine(self, src, me):
        return src.at[me]

    def theirs(self, src, peer):
        return src.at[peer]


class _SiblingSwapPart:
    def __init__(self, arrays):
        self.arrays, self.results = list(arrays), None

    def out_shape(self):
        return [_sds(a.shape, a.dtype) for a in self.arrays]

    def sems(self):
        n = len(self.arrays)
        return [pltpu.SemaphoreType.DMA((n,)), pltpu.SemaphoreType.DMA((n,))]

    def copies(self, ins, outs, sems):
        x, y, c = _position()
        both = [_remote(src, dst, sems[0], sems[1], i, (x, y, 1 - c)) for i, (src, dst) in enumerate(zip(ins, outs))]
        return [], both, both


class _DevGatherPart:
    def __init__(self, array):
        self.arrays, self.results = [array], None

    def out_shape(self):
        return [_sds((N_DEV,) + self.arrays[0].shape, self.arrays[0].dtype)]

    def sems(self):
        return [pltpu.SemaphoreType.DMA((N_DEV - 1,)), pltpu.SemaphoreType.DMA((N_DEV - 1,)),
                pltpu.SemaphoreType.DMA((1,))]

    def copies(self, ins, outs, sems):
        send_sems, recv_sems, local_sems = sems
        (src,), (dst,) = ins, outs
        x, y, c = _position()
        me = 4 * x + 2 * y + c
        local = [pltpu.make_async_copy(src, dst.at[me], local_sems.at[0])]
        sends, recvs = [], []
        for k in range(1, N_DEV):
            px, py, pc = (1 - x if k & 4 else x, 1 - y if k & 2 else y, 1 - c if k & 1 else c)
            sends.append(_remote(src, dst.at[me], send_sems, recv_sems, k - 1, (px, py, pc)))
            recvs.append(_remote(src, dst.at[4 * px + 2 * py + pc], send_sems, recv_sems, k - 1, (px, py, pc)))
        return local, sends, recvs


def _split_by(parts, refs, count):
    out, off = [], 0
    for p in parts:
        out.append(refs[off:off + count(p)])
        off += count(p)
    return out


def _parts_refs(parts, in_refs, out_refs, sem_refs):
    return zip(parts, _split_by(parts, in_refs, lambda p: len(p.arrays)),
               _split_by(parts, out_refs, lambda p: len(p.arrays)), _split_by(parts, sem_refs, lambda p: len(p.sems())))


def _exchange_start(parts, in_refs, out_refs, sem_refs):
    for part, ins, outs, sems in _parts_refs(parts, in_refs, out_refs, sem_refs):
        local, sends, _ = part.copies(ins, outs, sems)
        for cp in local + sends:
            cp.start()


def _exchange_finish(parts, in_refs, out_refs, sem_refs):
    for part, ins, outs, sems in _parts_refs(parts, in_refs, out_refs, sem_refs):
        local, sends, recvs = part.copies(ins, outs, sems)
        for cp in recvs:
            cp.wait_recv()
        for cp in sends:
            cp.wait_send()
        for cp in local:
            cp.wait()


def _exchange_operands(parts):
    return ([a for p in parts for a in p.arrays], [s for p in parts for s in p.out_shape()],
            [s for p in parts for s in p.sems()])


def _set_results(parts, res):
    for part, outs in zip(parts, _split_by(parts, list(res), lambda p: len(p.arrays))):
        part.results = list(outs)


def _exchange_now(name, parts):
    x_in, x_out, x_sem = _exchange_operands(parts)
    n = len(x_in)

    def body(*refs):
        _exchange_start(parts, refs[:n], refs[n:2 * n], refs[2 * n:])
        _exchange_finish(parts, refs[:n], refs[n:2 * n], refs[2 * n:])

    res = pl.pallas_call(body, name=name, in_specs=[_ANY] * n, out_specs=[_ANY] * n, out_shape=x_out,
                         scratch_shapes=x_sem)(*x_in)
    _set_results(parts, res)


_RIDERS = {}


def _call(body, *, name, grid, in_specs, out_specs, out_shape, scratch_shapes=(), compiler_params=None):
    make_parts = _RIDERS.pop(name, None)
    if make_parts is None:
        return pl.pallas_call(body, name=name, grid=grid, in_specs=in_specs, out_specs=out_specs, out_shape=out_shape,
                              scratch_shapes=scratch_shapes, compiler_params=compiler_params)
    parts = make_parts()
    x_in, x_out, x_sem = _exchange_operands(parts)
    n_out, n_scr, n_x = len(out_shape), len(scratch_shapes), len(x_in)

    def run(*args):
        n_in = len(args)

        def hosted(*refs):
            ins, xi = refs[:n_in], refs[n_in:n_in + n_x]
            outs, xo = refs[n_in + n_x:n_in + n_x + n_out], refs[n_in + n_x + n_out:n_in + 2 * n_x + n_out]
            scr, xs = refs[n_in + 2 * n_x + n_out:n_in + 2 * n_x + n_out + n_scr], refs[n_in + 2 * n_x + n_out + n_scr:]
            first = functools.reduce(jnp.logical_and, [pl.program_id(d) == 0 for d in range(len(grid))])
            last = functools.reduce(jnp.logical_and, [pl.program_id(d) == grid[d] - 1 for d in range(len(grid))])

            @pl.when(first)
            def _():
                _exchange_start(parts, xi, xo, xs)

            body(*ins, *outs, *scr)

            @pl.when(last)
            def _():
                _exchange_finish(parts, xi, xo, xs)

        res = pl.pallas_call(
            hosted, name=name, grid=grid, in_specs=list(in_specs) + [_ANY] * n_x,
            out_specs=list(out_specs) + [_ANY] * n_x, out_shape=list(out_shape) + x_out,
            scratch_shapes=list(scratch_shapes) + x_sem, compiler_params=_params(*["arbitrary"] * len(grid)),
        )(*args, *x_in)
        _set_results(parts, res[n_out:])
        return list(res[:n_out])

    return run


GROUPS = {"F1": ["ffn1_w_gate", "ffn1_w_up", "ffn1_w_down"], "MX": ["w_in", "s5_w_glu", "w_out"],
          "F2": ["ffn2_w_gate", "ffn2_w_up", "ffn2_w_down"]}
GROUP_OF = {n: g for g, names in GROUPS.items() for n in names}
FIRST_GATHER = [(0, "ffn1_w_gate"), (0, "ffn1_w_up")]
GATHER_HOSTS = {
    "l0_ffn1_up": [(0, "ffn1_w_down")],
    "l0_ffn1_down": [(0, "w_in"), (0, "s5_w_glu"), (0, "w_out")],
    "l0_mix_win": [(0, "ffn2_w_gate")],
    "l0_mix_attn": [(0, "ffn2_w_up"), (0, "ffn2_w_down")],
    "l0_ffn2_up": [(1, "ffn1_w_gate")],
    "l0_ffn2_down": [(1, "ffn1_w_up")],
    "l1_ffn1_up": [(1, "ffn1_w_down")],
    "l1_ffn1_down": [(1, "w_in"), (1, "s5_w_glu"), (1, "w_out")],
    "l1_mix_win": [(1, "ffn2_w_gate")],
    "l1_mix_attn": [(1, "ffn2_w_up"), (1, "ffn2_w_down")],
}
SCATTER_HOSTS = {
    "l1_ffn2_dact": [(1, "ffn2_w_down")],
    "l1_ffn2_dh": [(1, "ffn2_w_gate")],
    "l1_mix_attndq": [(1, "ffn2_w_up")],
    "l1_mix_attndkv": [(1, "w_out"), (1, "s5_w_glu")],
    "l1_mix_dh1": [(1, "w_in")],
    "l1_ffn1_dact": [(1, "ffn1_w_down")],
    "l1_ffn1_dh": [(1, "ffn1_w_gate")],
    "l0_ffn2_dact": [(1, "ffn1_w_up")],
    "l0_ffn2_dwgu": [(0, "ffn2_w_down")],
    "l0_ffn2_dh": [(0, "ffn2_w_gate")],
    "l0_mix_attndq": [(0, "w_out"), (0, "s5_w_glu"), (0, "ffn2_w_up")],
    "l0_mix_dh1": [(0, "w_in")],
    "l0_ffn1_dact": [(0, "ffn1_w_down")],
    "l0_ffn1_dh": [(0, "ffn1_w_gate")],
}
LAST_SCATTER = [(0, "ffn1_w_up")]


def _unstack_layer(name, st):
    _, r, c = st.shape
    if name in COLUMN_SHARDED:
        return st
    return st.reshape(N_CHIPS * r, c)


def _restack_layer(name, g):
    if name in COLUMN_SHARDED:
        return g
    r, c = g.shape
    return g.reshape(N_CHIPS, r // N_CHIPS, c)


def _adamw_layer(name, layer, w, ga, gb, m, v, bufs):
    _, r, c = w.shape
    tr = _row_tile(r)

    def body(w_ref, ga_ref, gb_ref, m_ref, v_ref, *rest):
        g_out, d_out, m_out, v_out = rest[-4:]
        g = ga_ref[...] + gb_ref[...]
        d, mm, vv = _adamw_rows(w_ref[...], g, m_ref[...], v_ref[...])
        g_out[...] = g
        d_out[...] = d
        m_out[...] = mm
        v_out[...] = vv

    full = pl.BlockSpec((None, tr, c), lambda i: (layer, i, 0))
    flat = pl.BlockSpec((tr, c), lambda i: (i, 0))
    extra = {} if bufs is None else dict(input_output_aliases={5 + k: k for k in range(4)})
    return pl.pallas_call(
        body, name=name, grid=(r // tr,),
        in_specs=[full, flat, flat, full, full] + ([] if bufs is None else [_ANY] * 4),
        out_specs=[full] * 4, out_shape=[_sds(w.shape)] * 4, compiler_params=_params("parallel"), **extra,
    )(w, ga, gb, m, v, *([] if bufs is None else bufs))


def _train_step(x, loss_target, w, m, v):
    ix, iy, _ = _position()
    chip = 2 * ix + iy
    shard = {n: (_permute_in_cols(w[n]) if n == "w_in" else w[n]).astype(bf16) for n in BIG_NAMES}

    gathered = {}

    def gather_parts(keys, extra=()):
        part = _ChipGatherPart([shard[n][layer] for layer, n in keys] + list(extra))
        gathered.update({key: (part, i) for i, key in enumerate(keys)})
        return [part]

    (first,) = gather_parts(FIRST_GATHER, extra=[w["conv_w"]])
    _exchange_now("gather_first", [first])
    for host, keys in GATHER_HOSTS.items():
        _RIDERS[host] = functools.partial(gather_parts, keys)

    def weight(layer, name):
        part, i = gathered[(layer, name)]
        return _unstack_layer(name, part.results[i])

    small = {n: w[n] for n in SMALL_NAMES}
    small["conv_w"] = first.results[-1].transpose(1, 2, 0, 3).reshape(DEPTH, CONV_WIDTH, D_A)

    grads_full, scattered = {}, {}

    def scatter_parts(keys):
        part = _ChipScatterPart([_restack_layer(n, grads_full[(layer, n)]) for layer, n in keys])
        scattered.update({key: (part, i) for i, key in enumerate(keys)})
        return [part]

    for host, keys in SCATTER_HOSTS.items():
        _RIDERS[host] = functools.partial(scatter_parts, keys)

    loss_local, gx, gsmall = _local_step(x[0], loss_target[0], weight, small, grads_full.__setitem__)

    partial = {}

    def reduce_chips(keys):
        for layer, n in keys:
            part, i = scattered[(layer, n)]
            p = _sum_stack(f"sum_l{layer}_{n}", part.results[i])
            partial[(layer, n)] = _unpermute_in_cols(p) if n == "w_in" else p

    early = [key for keys in SCATTER_HOSTS.values() for key in keys]
    reduce_chips(early)
    small_shapes = [gsmall[n].shape for n in SMALL_NAMES]
    last_parts = scatter_parts(LAST_SCATTER) + [_SiblingSwapPart([partial[k] for k in early]),
                                                _DevGatherPart(_pack([gsmall[n] for n in SMALL_NAMES]))]
    _exchange_now("exchange_last", last_parts)
    other = dict(zip(early, last_parts[1].results))
    reduce_chips(LAST_SCATTER)
    swap_late = _SiblingSwapPart([partial[k] for k in LAST_SCATTER])
    _exchange_now("swap_last", [swap_late])
    other.update(zip(LAST_SCATTER, swap_late.results))

    total = _sum_stack("sum_small", last_parts[2].results[0])
    gsm = dict(zip(SMALL_NAMES, _unpack(total, small_shapes)))
    cw = D_A // N_CHIPS
    gsm["conv_w"] = lax.dynamic_slice_in_dim(gsm["conv_w"], chip * cw, cw, axis=2)

    grads, deltas, new_m, new_v = {}, {}, {}, {}
    for n in BIG_NAMES:
        bufs = None
        for layer in range(DEPTH):
            bufs = _adamw_layer(f"adamw_l{layer}_{n}", layer, w[n], partial[(layer, n)], other[(layer, n)], m[n], v[n],
                                bufs)
        grads[n], deltas[n], new_m[n], new_v[n] = bufs
    shapes = [w[n].shape for n in SMALL_NAMES]
    gp = _pack([gsm[n] for n in SMALL_NAMES])
    res = _adamw("adamw_small", _pack([w[n] for n in SMALL_NAMES]), gp, jnp.zeros_like(gp),
                 _pack([m[n] for n in SMALL_NAMES]), _pack([v[n] for n in SMALL_NAMES]))
    for dst, buf in zip((grads, deltas, new_m, new_v), res):
        dst.update(zip(SMALL_NAMES, _unpack(buf, shapes)))

    loss = lax.psum(loss_local, ("x", "y", "c"))
    return (loss, gx[None], *[grads[n] for n in WEIGHT_NAMES], *[deltas[n] for n in WEIGHT_NAMES],
            *[new_m[n] for n in WEIGHT_NAMES], *[new_v[n] for n in WEIGHT_NAMES])


def kernel(x, ffn1_w_gate, ffn1_w_up, ffn1_w_down, ln1_g, ln1_b, w_in, conv_w, conv_b, rg_w_a, rg_b_a, rg_w_x, rg_b_x, rg_lambda, fox_b_f, s5_a_re, s5_a_im, s5_log_dt, s5_b_re, s5_b_im, s5_c_re, s5_c_im, s5_d, s5_w_glu, mix_norm_g, w_out, ln2_g, ln2_b, ffn2_w_gate, ffn2_w_up, ffn2_w_down, ln3_g, ln3_b, loss_target, m_ffn1_w_gate, m_ffn1_w_up, m_ffn1_w_down, m_ln1_g, m_ln1_b, m_w_in, m_conv_w, m_conv_b, m_rg_w_a, m_rg_b_a, m_rg_w_x, m_rg_b_x, m_rg_lambda, m_fox_b_f, m_s5_a_re, m_s5_a_im, m_s5_log_dt, m_s5_b_re, m_s5_b_im, m_s5_c_re, m_s5_c_im, m_s5_d, m_s5_w_glu, m_mix_norm_g, m_w_out, m_ln2_g, m_ln2_b, m_ffn2_w_gate, m_ffn2_w_up, m_ffn2_w_down, m_ln3_g, m_ln3_b, v_ffn1_w_gate, v_ffn1_w_up, v_ffn1_w_down, v_ln1_g, v_ln1_b, v_w_in, v_conv_w, v_conv_b, v_rg_w_a, v_rg_b_a, v_rg_w_x, v_rg_b_x, v_rg_lambda, v_fox_b_f, v_s5_a_re, v_s5_a_im, v_s5_log_dt, v_s5_b_re, v_s5_b_im, v_s5_c_re, v_s5_c_im, v_s5_d, v_s5_w_glu, v_mix_norm_g, v_w_out, v_ln2_g, v_ln2_b, v_ffn2_w_gate, v_ffn2_w_up, v_ffn2_w_down, v_ln3_g, v_ln3_b):
    args = dict(locals())
    w = {n: args[n] for n in WEIGHT_NAMES}
    m = {n: args["m_" + n] for n in WEIGHT_NAMES}
    v = {n: args["v_" + n] for n in WEIGHT_NAMES}
    return _train_step(x, loss_target, w, m, v)
```

```python
import functools
import math

import jax
import jax.numpy as jnp
from jax import lax
from jax.experimental import pallas as pl
from jax.experimental.pallas import tpu as pltpu

f32 = jnp.float32
bf16 = jnp.bfloat16

D_MODEL = 1024
D_FF = 2816
D_A = 384
D_B = 384
D_C = 256
N_HEADS = 6
HEAD_DIM = 64
S5_GROUPS = 16
S5_GROUP = 16
S5_STATE = 64
S5_LANES = S5_GROUPS * S5_STATE
N_IN = 2 * D_A + 3 * D_B + N_HEADS + D_C
F_OFF = 5 * D_A
CU_OFF = F_OFF + 128
N_IN_P = CU_OFF + D_C
CONV_WIDTH = 4
DEPTH = 2
ALPHA = (2 * DEPTH) ** 0.25
LN_EPS = 1e-5
RMS_EPS = 1e-6
RG_C = 8.0
ATT_SCALE = HEAD_DIM ** -0.5
ADAM_LR, ADAM_B1, ADAM_B2, ADAM_EPS, ADAM_WD, ADAM_STEP = 0.001, 0.9, 0.999, 1e-08, 0.01, 10

SCAN_CHUNK = 64
ROW_TILE = 256
ATT_TILE = 256
N_CHIPS = 4
N_DEV = 8
MESH = pl.DeviceIdType.MESH

_DN = {
    "nn": (((1,), (0,)), ((), ())),
    "nt": (((1,), (1,)), ((), ())),
    "tn": (((0,), (0,)), ((), ())),
}


def _sds(shape, dtype=f32):
    return jax.ShapeDtypeStruct(shape, dtype)


def _tile(n, target):
    best = None
    for t in range(128, min(n, target) + 1, 128):
        if n % t == 0:
            best = t
    return best or n


def _row_tile(rows, target=256):
    best = None
    for t in range(16, min(rows, target) + 1, 16):
        if rows % t == 0:
            best = t
    return best or rows


def _params(*sem):
    return pltpu.CompilerParams(dimension_semantics=sem)


class _Slabs:
    def __init__(self, x):
        self.x = x


FF_SLAB = D_FF // 4
FFN_ROWS = 1024

def _mm(name, mode, dims, tiles, a_list, b_list, pairs, n_acc, epilogue, outs, extras=(), vecs=(), split_cols=False):
    m, n, k = dims
    tm, tn, tk = tiles
    nk = k // tk
    na, nb, ne, nv, no = len(a_list), len(b_list), len(extras), len(vecs), len(outs)

    def body(*refs):
        a_refs = refs[:na]
        b_refs = refs[na:na + nb]
        e_refs = refs[na + nb:na + nb + ne]
        v_refs = refs[na + nb + ne:na + nb + ne + nv]
        o_refs = refs[na + nb + ne + nv:na + nb + ne + nv + no]
        acc_refs = refs[na + nb + ne + nv + no:]
        kk = pl.program_id(2)

        @pl.when(kk == 0)
        def _():
            for acc in acc_refs:
                acc[...] = jnp.zeros_like(acc)

        a_vals = [r[...].astype(bf16) for r in a_refs]
        b_vals = [r[...].astype(bf16) for r in b_refs]
        for ai, bi, ci in pairs:
            acc_refs[ci][...] += lax.dot_general(a_vals[ai], b_vals[bi], _DN[mode], preferred_element_type=f32)

        @pl.when(kk == nk - 1)
        def _():
            res = epilogue([acc[...] for acc in acc_refs], [e[...] for e in e_refs], [v[...] for v in v_refs])
            for o, r in zip(o_refs, res):
                o[...] = r.astype(o.dtype)

    def a_spec(a):
        if isinstance(a, _Slabs):
            if mode == "tn":
                return pl.BlockSpec((None, tk, tm), lambda i, j, kk: (i, kk, 0))
            return pl.BlockSpec((None, tm, tk), lambda i, j, kk: (kk, i, 0))
        if mode == "tn":
            return pl.BlockSpec((tk, tm), lambda i, j, kk: (kk, i))
        return pl.BlockSpec((tm, tk), lambda i, j, kk: (i, kk))

    def b_spec(b):
        if isinstance(b, _Slabs):
            if mode == "nt":
                return pl.BlockSpec((None, tn, tk), lambda i, j, kk: (kk, j, 0))
            return pl.BlockSpec((None, tk, tn), lambda i, j, kk: (j, kk, 0))
        if mode == "nt":
            return pl.BlockSpec((tn, tk), lambda i, j, kk: (j, kk))
        return pl.BlockSpec((tk, tn), lambda i, j, kk: (kk, j))

    o_spec = pl.BlockSpec((tm, tn), lambda i, j, kk: (i, j))
    o_slab_spec = pl.BlockSpec((None, tm, tn), lambda i, j, kk: (j, i, 0))
    v_spec = pl.BlockSpec((1, tn), lambda i, j, kk: (0, j))
    if split_cols:
        out_specs = [o_slab_spec] * no
        out_shape = [_sds((n // tn, m, tn), dt) for dt in outs]
    else:
        out_specs = [o_spec] * no
        out_shape = [_sds((m, n), dt) for dt in outs]
    raw = lambda t: t.x if isinstance(t, _Slabs) else t
    res = _call(
        body,
        name=name,
        grid=(m // tm, n // tn, nk),
        in_specs=([a_spec(a) for a in a_list] + [b_spec(b) for b in b_list]
                  + [o_slab_spec if isinstance(e, _Slabs) else o_spec for e in extras] + [v_spec] * nv),
        out_specs=out_specs,
        out_shape=out_shape,
        scratch_shapes=[pltpu.VMEM((tm, tn), f32)] * n_acc,
        compiler_params=_params("parallel", "parallel", "arbitrary"),
    )(*map(raw, a_list), *map(raw, b_list), *map(raw, extras), *vecs)
    return res


def _layer_norm_rows(r, gamma, beta):
    mu = jnp.mean(r, axis=-1, keepdims=True)
    xc = r - mu
    var = jnp.mean(xc * xc, axis=-1, keepdims=True)
    return xc * lax.rsqrt(var + LN_EPS) * gamma + beta


def _mm_plain(name, mode, a, b, dims, scale=1.0, out_dtype=f32, add=None, add_coef=1.0, tiles=None):
    m, n, k = dims
    tiles = tiles or (_tile(m, 512), _tile(n, 1024), _tile(k, 1024))

    def epilogue(accs, extras, vecs):
        r = accs[0] if scale == 1.0 else accs[0] * scale
        if extras:
            r = r + add_coef * extras[0]
        return [r]

    return _mm(name, mode, dims, tiles, [a], [b], [(0, 0, 0)], 1, epilogue, [out_dtype],
               extras=[] if add is None else [add])[0]


def _ffn_up(name, h, wg, wu):
    s = h.shape[0]

    def epilogue(accs, extras, vecs):
        g, u = accs
        return [g, u, g * jax.nn.sigmoid(g) * u]

    return _mm(name, "nn", (s, D_FF, D_MODEL), (_tile(s, FFN_ROWS), FF_SLAB, D_MODEL), [h], [_Slabs(wg), _Slabs(wu)],
               [(0, 0, 0), (0, 1, 1)], 2, epilogue, [bf16, bf16, bf16], split_cols=True)


def _mm_ln(name, a, w, resid, gamma, beta, scale, k_slabs=False):
    s, k = (a.shape[1], a.shape[0] * a.shape[2]) if k_slabs else a.shape

    def epilogue(accs, extras, vecs):
        r = ALPHA * extras[0] + scale * accs[0]
        return [r, _layer_norm_rows(r, vecs[0], vecs[1])]

    return _mm(name, "nn", (s, D_MODEL, k), (_tile(s, FFN_ROWS), D_MODEL, FF_SLAB if k_slabs else _tile(k, 1024)),
               [_Slabs(a) if k_slabs else a], [w], [(0, 0, 0)], 1, epilogue, [f32, f32], extras=[resid],
               vecs=[gamma, beta])


def _ffn_dact(name, dr, wd, g, u):
    s = dr.shape[0]

    def epilogue(accs, extras, vecs):
        da = 0.5 * accs[0]
        gg, uu = extras[0].astype(f32), extras[1].astype(f32)
        sg = jax.nn.sigmoid(gg)
        return [da * uu * (sg * (1.0 + gg * (1.0 - sg))), da * (gg * sg)]

    return _mm(name, "nt", (s, D_FF, D_MODEL), (_tile(s, FFN_ROWS), FF_SLAB, D_MODEL), [dr], [wd],
               [(0, 0, 0)], 1, epilogue, [bf16, bf16], extras=[_Slabs(g), _Slabs(u)], split_cols=True)


def _mm2(name, mode, dims, a0, b0, a1, b1, add=None, add_coef=1.0, separate=False, tiles=None, out_dtype=f32,
         split_cols=False):
    m, n, k = dims
    tiles = tiles or (_tile(m, 512), _tile(n, 1024), _tile(k, 1024))

    def epilogue(accs, extras, vecs):
        if separate:
            return list(accs)
        r = accs[0]
        if extras:
            r = r + add_coef * extras[0]
        return [r]

    a_list = [a0] if a1 is None else [a0, a1]
    b_list = [b0] if b1 is None else [b0, b1]
    pairs = [(0, 0, 0), (len(a_list) - 1, len(b_list) - 1, 1 if separate else 0)]
    return _mm(name, mode, dims, tiles, a_list, b_list, pairs, 2 if separate else 1, epilogue,
               [out_dtype, out_dtype] if separate else [out_dtype], extras=[] if add is None else [add],
               split_cols=split_cols)


def _row_call(name, body, s, ins, params, outs, accs):
    tm = ROW_TILE
    in_specs = [pl.BlockSpec((tm, a.shape[1]), lambda i: (i, 0)) for a in ins]
    in_specs += [pl.BlockSpec(p.shape, lambda i, nd=p.ndim: (0,) * nd) for p in params]
    out_specs = [pl.BlockSpec((tm, o.shape[1]), lambda i: (i, 0)) for o in outs]
    out_specs += [pl.BlockSpec(a.shape, lambda i, nd=len(a.shape): (0,) * nd) for a in accs]
    return pl.pallas_call(
        body,
        name=name,
        grid=(s // tm,),
        in_specs=in_specs,
        out_specs=out_specs,
        out_shape=list(outs) + list(accs),
        compiler_params=_params("arbitrary"),
    )(*ins, *params)


def _zero_at_first(refs):
    @pl.when(pl.program_id(0) == 0)
    def _():
        for r in refs:
            r[...] = jnp.zeros_like(r)


def _ln_bwd(name, r, dh, gamma):
    s = r.shape[0]

    def body(r_ref, dh_ref, g_ref, dr_ref, dg_ref, db_ref):
        _zero_at_first([dg_ref, db_ref])
        rr = r_ref[...]
        dy = dh_ref[...]
        mu = jnp.mean(rr, axis=-1, keepdims=True)
        xc = rr - mu
        rstd = lax.rsqrt(jnp.mean(xc * xc, axis=-1, keepdims=True) + LN_EPS)
        xhat = xc * rstd
        dxh = dy * g_ref[...]
        dr_ref[...] = rstd * (dxh - jnp.mean(dxh, axis=-1, keepdims=True)
                              - xhat * jnp.mean(dxh * xhat, axis=-1, keepdims=True))
        dg_ref[...] += jnp.sum(dy * xhat, axis=0, keepdims=True)
        db_ref[...] += jnp.sum(dy, axis=0, keepdims=True)

    return _row_call(name, body, s, [r, dh], [gamma], [_sds((s, D_MODEL))], [_sds((1, D_MODEL)), _sds((1, D_MODEL))])


def _loss_head(name, y, target):
    s = y.shape[0]

    def body(y_ref, t_ref, dy_ref, l_ref):
        _zero_at_first([l_ref])
        e = y_ref[...] - t_ref[...]
        dy_ref[...] = e / D_MODEL
        l_ref[...] += 0.5 * jnp.sum(jnp.mean(e * e, axis=-1, keepdims=True), axis=0, keepdims=True)

    return _row_call(name, body, s, [y, target], [], [_sds((s, D_MODEL))], [_sds((1, 128))])


def _expm1(x):
    series = x * (1.0 + x / 2.0 * (1.0 + x / 3.0 * (1.0 + x / 4.0 * (1.0 + x / 5.0 * (1.0 + x / 6.0 * (1.0 + x / 7.0))))))
    return jnp.where(jnp.abs(x) < 0.25, series, jnp.exp(x) - 1.0)


def _gates_fn(xa, wa, wx, ba, bx, lam, tap_a, tap_x):
    xb = xa.astype(bf16)
    r = jax.nn.sigmoid(jnp.dot(xb, wa, preferred_element_type=f32) + ba + tap_a)
    i = jax.nn.sigmoid(jnp.dot(xb, wx, preferred_element_type=f32) + bx + tap_x)
    log_a = -RG_C * r * jax.nn.softplus(-lam)
    a = jnp.exp(log_a)
    gated = jnp.sqrt(-_expm1(2.0 * log_a)) * (i * xa)
    return a, gated


def _rg_gates(name, xa, wa, wx, ba, bx, lam):
    s = xa.shape[0]

    def body(xa_ref, wa_ref, wx_ref, ba_ref, bx_ref, lam_ref, a_ref, g_ref):
        a, g = _gates_fn(xa_ref[...], wa_ref[...], wx_ref[...], ba_ref[...], bx_ref[...], lam_ref[...], 0.0, 0.0)
        a_ref[...] = a
        g_ref[...] = g

    return _row_call(name, body, s, [xa], [wa, wx, ba, bx, lam], [_sds((s, D_A)), _sds((s, D_A))], [])


def _rg_gates_bwd(name, xa, ga, h_prev, wa, wx, ba, bx, lam):
    s = xa.shape[0]

    def body(xa_ref, ga_ref, hp_ref, wa_ref, wx_ref, ba_ref, bx_ref, lam_ref,
             dxa_ref, dwa_ref, dwx_ref, dba_ref, dbx_ref, dlam_ref):
        _zero_at_first([dwa_ref, dwx_ref, dba_ref, dbx_ref, dlam_ref])
        xa_v = xa_ref[...]
        zero = jnp.zeros((xa_v.shape[0], D_A), f32)
        fn = lambda x, ba_, bx_, lam_, ta, tx: _gates_fn(x, wa_ref[...], wx_ref[...], ba_, bx_, lam_, ta, tx)
        _, vjp = jax.vjp(fn, xa_v, ba_ref[...], bx_ref[...], lam_ref[...], zero, zero)
        gav = ga_ref[...]
        dxa, dba, dbx, dlam, dta, dtx = vjp((gav * hp_ref[...], gav))
        dxa_ref[...] = dxa
        xb = xa_v.astype(bf16)
        dwa_ref[...] += lax.dot_general(xb, dta.astype(bf16), _DN["tn"], preferred_element_type=f32)
        dwx_ref[...] += lax.dot_general(xb, dtx.astype(bf16), _DN["tn"], preferred_element_type=f32)
        dba_ref[...] += dba
        dbx_ref[...] += dbx
        dlam_ref[...] += dlam

    return _row_call(name, body, s, [xa, ga, h_prev], [wa, wx, ba, bx, lam], [_sds((s, D_A))],
                     [_sds((D_A, D_A)), _sds((D_A, D_A)), _sds((1, D_A)), _sds((1, D_A)), _sds((1, D_A))])


def _rms(v, g):
    return v * lax.rsqrt(jnp.mean(v * v, axis=-1, keepdims=True) + RMS_EPS) * g


def _mix_out_fn(ag, ha, ob, hre, him, cu, d, gn, tap_y, tap_gl, wcr, wci, wglu):
    out_a = jax.nn.gelu(ag) * ha
    y = (jnp.dot(hre.astype(bf16), wcr, preferred_element_type=f32)
         + jnp.dot(him.astype(bf16), wci, preferred_element_type=f32) + d * cu + tap_y)
    y2 = jax.nn.gelu(y)
    gl = jnp.dot(y2.astype(bf16), wglu, preferred_element_type=f32) + tap_gl
    out_c = y2 * jax.nn.sigmoid(gl)
    o = jnp.concatenate([_rms(out_a, gn[:, :D_A]), _rms(ob, gn[:, D_A:D_A + D_B]), _rms(out_c, gn[:, D_A + D_B:])],
                        axis=-1)
    return o, y2


def _mix_out(name, ag, ha, ob, hre, him, cu, d, gn, wcr, wci, wglu):
    s = ag.shape[0]

    def body(ag_ref, ha_ref, ob_ref, hre_ref, him_ref, cu_ref, d_ref, gn_ref, wcr_ref, wci_ref, wglu_ref, o_ref):
        o, _ = _mix_out_fn(ag_ref[...], ha_ref[...], ob_ref[...], hre_ref[...], him_ref[...], cu_ref[...], d_ref[...],
                           gn_ref[...], 0.0, 0.0, wcr_ref[...], wci_ref[...], wglu_ref[...])
        o_ref[...] = o.astype(o_ref.dtype)

    return _row_call(name, body, s, [ag, ha, ob, hre, him, cu], [d, gn, wcr, wci, wglu], [_sds((s, D_MODEL), bf16)], [])[0]


def _mix_out_bwd(name, do, ag, ha, ob, hre, him, cu, d, gn, wcr, wci, wglu):
    s = ag.shape[0]

    def body(do_ref, ag_ref, ha_ref, ob_ref, hre_ref, him_ref, cu_ref, d_ref, gn_ref, wcr_ref, wci_ref, wglu_ref,
             dag_ref, dha_ref, dob_ref, dhre_ref, dhim_ref, dcu_ref, dwcr_ref, dwci_ref, dwglu_ref, dd_ref, dgn_ref):
        _zero_at_first([dwcr_ref, dwci_ref, dwglu_ref, dd_ref, dgn_ref])
        tm = ag_ref.shape[0]
        zero = jnp.zeros((tm, D_C), f32)
        hre_v, him_v = hre_ref[...], him_ref[...]
        fn = lambda *a: _mix_out_fn(*a, wcr_ref[...], wci_ref[...], wglu_ref[...])
        _, vjp, y2 = jax.vjp(fn, ag_ref[...], ha_ref[...], ob_ref[...], hre_v, him_v, cu_ref[...], d_ref[...],
                             gn_ref[...], zero, zero, has_aux=True)
        dag, dha, dob, dhre, dhim, dcu, dd, dgn, dy, dgl = vjp(do_ref[...])
        dag_ref[...] = dag
        dha_ref[...] = dha
        dob_ref[...] = dob
        dhre_ref[...] = dhre
        dhim_ref[...] = dhim
        dcu_ref[...] = dcu
        dyb = dy.astype(bf16)
        dwcr_ref[...] += lax.dot_general(hre_v.astype(bf16), dyb, _DN["tn"], preferred_element_type=f32)
        dwci_ref[...] += lax.dot_general(him_v.astype(bf16), dyb, _DN["tn"], preferred_element_type=f32)
        dwglu_ref[...] += lax.dot_general(y2.astype(bf16), dgl.astype(bf16), _DN["tn"], preferred_element_type=f32)
        dd_ref[...] += dd
        dgn_ref[...] += dgn

    outs = [_sds((s, D_A)), _sds((s, D_A)), _sds((s, D_B)), _sds((s, S5_LANES)), _sds((s, S5_LANES)), _sds((s, D_C))]
    accs = [_sds((S5_LANES, D_C)), _sds((S5_LANES, D_C)), _sds((D_C, D_C)), _sds((1, D_C)), _sds((1, D_MODEL))]
    return _row_call(name, body, s, [do, ag, ha, ob, hre, him, cu], [d, gn, wcr, wci, wglu], outs, accs)


def _log_f(name, f, bf):
    s = f.shape[0]

    def body(f_ref, b_ref, o_ref):
        o_ref[...] = jax.nn.log_sigmoid(f_ref[...] + b_ref[...])

    return _row_call(name, body, s, [f], [bf], [_sds((s, 128))], [])[0]


def _log_f_bwd(name, dlf, f, bf):
    s = f.shape[0]

    def body(dl_ref, f_ref, b_ref, df_ref, db_ref):
        _zero_at_first([db_ref])
        df = dl_ref[...] * jax.nn.sigmoid(-(f_ref[...] + b_ref[...]))
        df_ref[...] = df
        db_ref[...] += jnp.sum(df, axis=0, keepdims=True)

    return _row_call(name, body, s, [dlf, f], [bf], [_sds((s, 128))], [_sds((1, 128))])


def _s5_decay_grad(name, hp_re, hp_im, g_re, g_im):
    s = g_re.shape[0]

    def body(hr_ref, hi_ref, gr_ref, gi_ref, dr_ref, di_ref):
        _zero_at_first([dr_ref, di_ref])
        hr, hi, gr, gi = hr_ref[...], hi_ref[...], gr_ref[...], gi_ref[...]
        dr_ref[...] += jnp.sum(hr * gr + hi * gi, axis=0, keepdims=True)
        di_ref[...] += jnp.sum(hr * gi - hi * gr, axis=0, keepdims=True)

    return _row_call(name, body, s, [hp_re, hp_im, g_re, g_im], [], [], [_sds((1, S5_LANES)), _sds((1, S5_LANES))])


def _conv_fwd(name, ax, w, b):
    s = ax.shape[0]
    tm = ROW_TILE

    def body(x_ref, halo_ref, w_ref, b_ref, o_ref):
        i = pl.program_id(0)
        x = x_ref[...]
        halo = jnp.where(i == 0, 0.0, halo_ref[...])
        ext = jnp.concatenate([halo, x], axis=0)
        acc = b_ref[...] + w_ref[3:4, :] * x
        for k in range(CONV_WIDTH - 1):
            acc = acc + w_ref[k:k + 1, :] * pltpu.roll(ext, CONV_WIDTH - 1 - k, 0)[8:, :]
        o_ref[...] = acc

    return pl.pallas_call(
        body,
        name=name,
        grid=(s // tm,),
        in_specs=[pl.BlockSpec((tm, D_A), lambda i: (i, 0)),
                  pl.BlockSpec((8, D_A), lambda i: (jnp.maximum(i * (tm // 8) - 1, 0), 0)),
                  pl.BlockSpec((CONV_WIDTH, D_A), lambda i: (0, 0)),
                  pl.BlockSpec((1, D_A), lambda i: (0, 0))],
        out_specs=pl.BlockSpec((tm, D_A), lambda i: (i, 0)),
        out_shape=_sds((s, D_A)),
        compiler_params=_params("arbitrary"),
    )(ax, ax, w, b)


def _conv_bwd(name, dxa, ax, w):
    s = ax.shape[0]
    tm = ROW_TILE
    nblk = s // tm

    def body(dx_ref, dnext_ref, x_ref, halo_ref, w_ref, dax_ref, dw_ref):
        i = pl.program_id(0)
        _zero_at_first([dw_ref])
        dx = dx_ref[...]
        dnext = jnp.where(i == nblk - 1, 0.0, dnext_ref[...])
        dext = jnp.concatenate([dx, dnext], axis=0)
        x = x_ref[...]
        halo = jnp.where(i == 0, 0.0, halo_ref[...])
        ext = jnp.concatenate([halo, x], axis=0)
        acc = w_ref[3:4, :] * dx
        dw_ref[3:4, :] += jnp.sum(dx * x, axis=0, keepdims=True)
        for k in range(CONV_WIDTH - 1):
            sh = CONV_WIDTH - 1 - k
            acc = acc + w_ref[k:k + 1, :] * pltpu.roll(dext, tm + 8 - sh, 0)[:tm, :]
            dw_ref[k:k + 1, :] += jnp.sum(dx * pltpu.roll(ext, sh, 0)[8:, :], axis=0, keepdims=True)
        dw_ref[4:5, :] += jnp.sum(dx, axis=0, keepdims=True)
        dax_ref[...] = acc

    return pl.pallas_call(
        body,
        name=name,
        grid=(nblk,),
        in_specs=[pl.BlockSpec((tm, D_A), lambda i: (i, 0)),
                  pl.BlockSpec((8, D_A), lambda i: (jnp.minimum((i + 1) * (tm // 8), s // 8 - 1), 0)),
                  pl.BlockSpec((tm, D_A), lambda i: (i, 0)),
                  pl.BlockSpec((8, D_A), lambda i: (jnp.maximum(i * (tm // 8) - 1, 0), 0)),
                  pl.BlockSpec((CONV_WIDTH, D_A), lambda i: (0, 0))],
        out_specs=[pl.BlockSpec((tm, D_A), lambda i: (i, 0)), pl.BlockSpec((8, D_A), lambda i: (0, 0))],
        out_shape=[_sds((s, D_A)), _sds((8, D_A))],
        compiler_params=_params("arbitrary"),
    )(dxa, dxa, ax, ax, w)


SCAN_ROWS = 512


def _row_in_tile(shape):
    return lax.broadcasted_iota(jnp.int32, shape, 0) % 8


def _lin_scan(name, a, b, reverse):
    s, c = a.shape
    t = min(SCAN_ROWS, s)
    nb = s // t

    def body(a_ref, b_ref, h_ref, p_ref, carry_ref):
        @pl.when(pl.program_id(0) == 0)
        def _():
            carry_ref[...] = jnp.zeros_like(carry_ref)

        row = _row_in_tile((t, c))
        p = a_ref[...]
        h = b_ref[...]
        for d in (1, 2, 4):
            keep = (row < 8 - d) if reverse else (row >= d)
            shift = (t - d) if reverse else d
            h = h + jnp.where(keep, p * pltpu.roll(h, shift, 0), 0.0)
            p = jnp.where(keep, p * pltpu.roll(p, shift, 0), p)
        h_ref[...] = h
        p_ref[...] = p
        edge = 0 if reverse else 7

        def tile(k, carry):
            kk = (t // 8 - 1 - k) if reverse else k
            r0 = pl.multiple_of(kk * 8, 8)
            hh = h_ref[pl.ds(r0, 8), :] + p_ref[pl.ds(r0, 8), :] * carry
            h_ref[pl.ds(r0, 8), :] = hh
            return jnp.broadcast_to(hh[edge:edge + 1, :], (8, c))

        carry_ref[...] = lax.fori_loop(0, t // 8, tile, carry_ref[...])

    spec = pl.BlockSpec((t, c), (lambda i: (nb - 1 - i, 0)) if reverse else (lambda i: (i, 0)))
    (out,) = _call(
        body,
        name=name,
        grid=(nb,),
        in_specs=[spec, spec],
        out_specs=[spec],
        out_shape=[_sds((s, c))],
        scratch_shapes=[pltpu.VMEM((t, c), f32), pltpu.VMEM((8, c), f32)],
        compiler_params=_params("arbitrary"),
    )(a, b)
    return out


def _s5_scan(name, b_re, b_im, a_re, a_im, reverse):
    s, c = b_re.shape
    t = min(SCAN_ROWS, s)
    nb = s // t

    def body(br_ref, bi_ref, ar_ref, ai_ref, hr_ref, hi_ref, cr_ref, ci_ref):
        @pl.when(pl.program_id(0) == 0)
        def _():
            cr_ref[...] = jnp.zeros_like(cr_ref)
            ci_ref[...] = jnp.zeros_like(ci_ref)

        ar1, ai1 = ar_ref[...], ai_ref[...]
        pows = [(ar1, ai1)]
        for _ in range(7):
            pr, pi = pows[-1]
            pows.append((pr * ar1 - pi * ai1, pr * ai1 + pi * ar1))
        row8 = lax.broadcasted_iota(jnp.int32, (8, c), 0)
        wr = jnp.zeros((8, c), f32)
        wi = jnp.zeros((8, c), f32)
        for r in range(8):
            pr, pi = pows[(7 - r) if reverse else r]
            wr = jnp.where(row8 == r, pr, wr)
            wi = jnp.where(row8 == r, pi, wi)
        row = _row_in_tile((t, c))
        hr = br_ref[...]
        hi = bi_ref[...]
        for d in (1, 2, 4):
            keep = (row < 8 - d) if reverse else (row >= d)
            shift = (t - d) if reverse else d
            pr, pi = pows[d - 1]
            cr = jnp.where(keep, pr, 0.0)
            ci = jnp.where(keep, pi, 0.0)
            sr = pltpu.roll(hr, shift, 0)
            si = pltpu.roll(hi, shift, 0)
            hr, hi = hr + cr * sr - ci * si, hi + cr * si + ci * sr
        hr_ref[...] = hr
        hi_ref[...] = hi
        edge = 0 if reverse else 7

        def tile(k, carry):
            car_r, car_i = carry
            kk = (t // 8 - 1 - k) if reverse else k
            r0 = pl.multiple_of(kk * 8, 8)
            xr = hr_ref[pl.ds(r0, 8), :] + wr * car_r - wi * car_i
            xi = hi_ref[pl.ds(r0, 8), :] + wr * car_i + wi * car_r
            hr_ref[pl.ds(r0, 8), :] = xr
            hi_ref[pl.ds(r0, 8), :] = xi
            return (jnp.broadcast_to(xr[edge:edge + 1, :], (8, c)), jnp.broadcast_to(xi[edge:edge + 1, :], (8, c)))

        car_r, car_i = lax.fori_loop(0, t // 8, tile, (cr_ref[...], ci_ref[...]))
        cr_ref[...] = car_r
        ci_ref[...] = car_i

    spec = pl.BlockSpec((t, c), (lambda i: (nb - 1 - i, 0)) if reverse else (lambda i: (i, 0)))
    vspec = pl.BlockSpec((1, c), lambda i: (0, 0))
    hr, hi = _call(
        body,
        name=name,
        grid=(nb,),
        in_specs=[spec, spec, vspec, vspec],
        out_specs=[spec, spec],
        out_shape=[_sds((s, c)), _sds((s, c))],
        scratch_shapes=[pltpu.VMEM((8, c), f32), pltpu.VMEM((8, c), f32)],
        compiler_params=_params("arbitrary"),
    )(b_re, b_im, a_re, a_im)
    return hr, hi


def _causal_mask(t):
    row = lax.broadcasted_iota(jnp.int32, (t, t), 0)
    col = lax.broadcasted_iota(jnp.int32, (t, t), 1)
    return row >= col


def _attn_fwd(name, q, k, v, cq, ck):
    h, s, dh = q.shape
    t = ATT_TILE
    nq = s // t

    def body(q_ref, k_ref, v_ref, cq_ref, ck_ref, o_ref, lse_ref):
        qi = pl.program_id(1)
        qb = q_ref[...].astype(bf16)
        cqv = cq_ref[...]

        def block(kb, carry, masked):
            m, l, acc = carry
            ks = pl.multiple_of(kb * t, t)
            kk = k_ref[pl.ds(ks, t), :].astype(bf16)
            vv = v_ref[pl.ds(ks, t), :].astype(bf16)
            sc = lax.dot_general(qb, kk, _DN["nt"], preferred_element_type=f32) * ATT_SCALE + (cqv - ck_ref[kb])
            if masked:
                sc = jnp.where(_causal_mask(t), sc, -jnp.inf)
            mn = jnp.maximum(m, jnp.max(sc, axis=1, keepdims=True))
            p = jnp.exp(sc - mn)
            al = jnp.exp(m - mn)
            l = al * l + jnp.sum(p, axis=1, keepdims=True)
            acc = al * acc + jnp.dot(p.astype(bf16), vv, preferred_element_type=f32)
            return mn, l, acc

        init = (jnp.full((t, 1), -jnp.inf, f32), jnp.zeros((t, 1), f32), jnp.zeros((t, dh), f32))
        carry = lax.fori_loop(0, qi, lambda kb, c: block(kb, c, False), init)
        m, l, acc = block(qi, carry, True)
        o_ref[...] = acc / l
        lse_ref[...] = m + jnp.log(l)

    return pl.pallas_call(
        body,
        name=name,
        grid=(h, nq),
        in_specs=[pl.BlockSpec((None, t, dh), lambda hh, i: (hh, i, 0)),
                  pl.BlockSpec((None, s, dh), lambda hh, i: (hh, 0, 0)),
                  pl.BlockSpec((None, s, dh), lambda hh, i: (hh, 0, 0)),
                  pl.BlockSpec((None, t, 1), lambda hh, i: (hh, i, 0)),
                  pl.BlockSpec((None, nq, 1, t), lambda hh, i: (hh, 0, 0, 0))],
        out_specs=[pl.BlockSpec((None, t, dh), lambda hh, i: (hh, i, 0)),
                   pl.BlockSpec((None, t, 1), lambda hh, i: (hh, i, 0))],
        out_shape=[_sds((h, s, dh)), _sds((h, s, 1))],
        compiler_params=_params("parallel", "arbitrary"),
    )(q, k, v, cq, ck)


def _attn_bwd_dq(name, q, k, v, cq, ck, o, do, lse):
    h, s, dh = q.shape
    t = ATT_TILE
    nq = s // t

    def body(q_ref, k_ref, v_ref, cq_ref, ck_ref, o_ref, do_ref, lse_ref, dq_ref, dl_ref):
        qi = pl.program_id(1)
        qb = q_ref[...].astype(bf16)
        cqv = cq_ref[...]
        dov = do_ref[...]
        dob = dov.astype(bf16)
        delta = jnp.sum(dov * o_ref[...], axis=1, keepdims=True)
        lse_v = lse_ref[...]

        def block(kb, carry, masked):
            dq, psum = carry
            ks = pl.multiple_of(kb * t, t)
            kk = k_ref[pl.ds(ks, t), :].astype(bf16)
            vv = v_ref[pl.ds(ks, t), :].astype(bf16)
            sc = lax.dot_general(qb, kk, _DN["nt"], preferred_element_type=f32) * ATT_SCALE + (cqv - ck_ref[kb])
            p = jnp.exp(sc - lse_v)
            if masked:
                p = jnp.where(_causal_mask(t), p, 0.0)
            dp = lax.dot_general(dob, vv, _DN["nt"], preferred_element_type=f32)
            ds = p * (dp - delta)
            return (dq + jnp.dot(ds.astype(bf16), kk, preferred_element_type=f32),
                    psum + jnp.sum(p * dp, axis=1, keepdims=True))

        carry = lax.fori_loop(0, qi, lambda kb, c: block(kb, c, False), (jnp.zeros((t, dh), f32), jnp.zeros((t, 1), f32)))
        dq, psum = block(qi, carry, True)
        dq_ref[...] = dq * ATT_SCALE
        dl_ref[...] = psum

    qspec = pl.BlockSpec((None, t, dh), lambda hh, i: (hh, i, 0))
    fspec = pl.BlockSpec((None, s, dh), lambda hh, i: (hh, 0, 0))
    cspec = pl.BlockSpec((None, t, 1), lambda hh, i: (hh, i, 0))
    return pl.pallas_call(
        body,
        name=name,
        grid=(h, nq),
        in_specs=[qspec, fspec, fspec, cspec, pl.BlockSpec((None, nq, 1, t), lambda hh, i: (hh, 0, 0, 0)),
                  qspec, qspec, cspec],
        out_specs=[qspec, cspec],
        out_shape=[_sds((h, s, dh)), _sds((h, s, 1))],
        compiler_params=_params("parallel", "arbitrary"),
    )(q, k, v, cq, ck, o, do, lse)


def _attn_bwd_dkv(name, q, k, v, cq, ck, do, lse, delta):
    h, s, dh = q.shape
    t = ATT_TILE
    nq = s // t

    def body(q_ref, k_ref, v_ref, cq_ref, ck_ref, do_ref, lse_ref, dl_ref, dk_ref, dv_ref, dck_ref):
        kj = pl.program_id(1)
        kk = k_ref[...].astype(bf16)
        vv = v_ref[...].astype(bf16)
        ckv = ck_ref[...]

        def block(qi, carry, masked):
            dk, dv, dcs = carry
            qs = pl.multiple_of(qi * t, t)
            qq = q_ref[pl.ds(qs, t), :].astype(bf16)
            dob = do_ref[pl.ds(qs, t), :].astype(bf16)
            sc = (lax.dot_general(qq, kk, _DN["nt"], preferred_element_type=f32) * ATT_SCALE
                  + (cq_ref[pl.ds(qs, t), :] - ckv))
            p = jnp.exp(sc - lse_ref[pl.ds(qs, t), :])
            if masked:
                p = jnp.where(_causal_mask(t), p, 0.0)
            dv = dv + lax.dot_general(p.astype(bf16), dob, _DN["tn"], preferred_element_type=f32)
            dp = lax.dot_general(dob, vv, _DN["nt"], preferred_element_type=f32)
            ds = p * (dp - dl_ref[pl.ds(qs, t), :])
            dk = dk + lax.dot_general(ds.astype(bf16), qq, _DN["tn"], preferred_element_type=f32)
            return dk, dv, dcs + jnp.sum(ds, axis=0, keepdims=True)

        init = (jnp.zeros((t, dh), f32), jnp.zeros((t, dh), f32), jnp.zeros((1, t), f32))
        carry = block(kj, init, True)
        dk, dv, dcs = lax.fori_loop(kj + 1, nq, lambda qi, c: block(qi, c, False), carry)
        dk_ref[...] = dk * ATT_SCALE
        dv_ref[...] = dv
        dck_ref[...] = -dcs

    kspec = pl.BlockSpec((None, t, dh), lambda hh, j: (hh, j, 0))
    fspec = pl.BlockSpec((None, s, dh), lambda hh, j: (hh, 0, 0))
    fcol = pl.BlockSpec((None, s, 1), lambda hh, j: (hh, 0, 0))
    crow = pl.BlockSpec((None, None, 1, t), lambda hh, j: (hh, j, 0, 0))
    return pl.pallas_call(
        body,
        name=name,
        grid=(h, nq),
        in_specs=[fspec, kspec, kspec, fcol, crow, fspec, fcol, fcol],
        out_specs=[kspec, kspec, crow],
        out_shape=[_sds((h, s, dh)), _sds((h, s, dh)), _sds((h, nq, 1, t))],
        compiler_params=_params("parallel", "arbitrary"),
    )(q, k, v, cq, ck, do, lse, delta)


ATT_FEAT = 128
ATT_TQ = 1024
ATT_TK = 256


def _att_tiles(s):
    tq = min(ATT_TQ, s)
    return tq, ATT_TK, tq // ATT_TK


def _keys_le_queries(tk, tq, k0, q0):
    row = lax.broadcasted_iota(jnp.int32, (tk, tq), 0) + k0
    col = lax.broadcasted_iota(jnp.int32, (tk, tq), 1) + q0
    return row <= col


def _attn_fwd_t(name, qt, k_aug, vt):
    h, s, _ = k_aug.shape
    tq, tk, ratio = _att_tiles(s)

    def body(qt_ref, k_ref, vt_ref, o_ref, lse_ref):
        qi = pl.program_id(1)
        qt = qt_ref[...]

        def block(kb, carry, masked):
            m, l, acc = carry
            ks = pl.multiple_of(kb * tk, tk)
            st = jnp.dot(k_ref[pl.ds(ks, tk), :], qt, preferred_element_type=f32)
            if masked:
                st = jnp.where(_keys_le_queries(tk, tq, ks, qi * tq), st, -jnp.inf)
            mn = jnp.maximum(m, jnp.max(st, axis=0, keepdims=True))
            p = jnp.exp(st - mn)
            al = jnp.exp(m - mn)
            l = al * l + jnp.sum(p, axis=0, keepdims=True)
            acc = al * acc + jnp.dot(vt_ref[kb], p.astype(bf16), preferred_element_type=f32)
            return mn, l, acc

        init = (jnp.full((1, tq), -jnp.inf, f32), jnp.zeros((1, tq), f32), jnp.zeros((HEAD_DIM, tq), f32))
        first = lax.fori_loop(0, qi * ratio, lambda kb, c: block(kb, c, False), init)
        m, l, acc = lax.fori_loop(qi * ratio, (qi + 1) * ratio, lambda kb, c: block(kb, c, True), first)
        o_ref[...] = acc / l
        lse_ref[...] = m + jnp.log(l)

    return _call(
        body,
        name=name,
        grid=(h, s // tq),
        in_specs=[pl.BlockSpec((None, None, ATT_FEAT, tq), lambda hh, i: (hh, i, 0, 0)),
                  pl.BlockSpec((None, s, ATT_FEAT), lambda hh, i: (hh, 0, 0)),
                  pl.BlockSpec((None, s // tk, HEAD_DIM, tk), lambda hh, i: (hh, 0, 0, 0))],
        out_specs=[pl.BlockSpec((None, HEAD_DIM, tq), lambda hh, i: (hh, 0, i)),
                   pl.BlockSpec((None, 1, tq), lambda hh, i: (hh, 0, i))],
        out_shape=[_sds((h, HEAD_DIM, s)), _sds((h, 1, s))],
        compiler_params=_params("parallel", "arbitrary"),
    )(qt, k_aug, vt)


def _attn_bwd_dq_t(name, qt, k_aug, v, kt, ot, dot_, lse):
    h, s, _ = k_aug.shape
    tq, tk, ratio = _att_tiles(s)

    def body(qt_ref, k_ref, v_ref, kt_ref, o_ref, do_ref, lse_ref, dq_ref, dl_ref):
        qi = pl.program_id(1)
        qt = qt_ref[...]
        dob = do_ref[...]
        delta = jnp.sum(dob.astype(f32) * o_ref[...], axis=0, keepdims=True)
        lse_v = lse_ref[...]

        def block(kb, carry, masked):
            dq, psum = carry
            ks = pl.multiple_of(kb * tk, tk)
            st = jnp.dot(k_ref[pl.ds(ks, tk), :], qt, preferred_element_type=f32)
            p = jnp.exp(st - lse_v)
            if masked:
                p = jnp.where(_keys_le_queries(tk, tq, ks, qi * tq), p, 0.0)
            dp = jnp.dot(v_ref[pl.ds(ks, tk), :], dob, preferred_element_type=f32)
            ds = p * (dp - delta)
            return (dq + jnp.dot(kt_ref[kb], ds.astype(bf16), preferred_element_type=f32),
                    psum + jnp.sum(p * dp, axis=0, keepdims=True))

        carry = lax.fori_loop(0, qi * ratio, lambda kb, c: block(kb, c, False),
                              (jnp.zeros((HEAD_DIM, tq), f32), jnp.zeros((1, tq), f32)))
        dq, psum = lax.fori_loop(qi * ratio, (qi + 1) * ratio, lambda kb, c: block(kb, c, True), carry)
        dq_ref[...] = dq * ATT_SCALE
        dl_ref[...] = psum

    qspec = pl.BlockSpec((None, HEAD_DIM, tq), lambda hh, i: (hh, 0, i))
    rspec = pl.BlockSpec((None, 1, tq), lambda hh, i: (hh, 0, i))
    return _call(
        body,
        name=name,
        grid=(h, s // tq),
        in_specs=[pl.BlockSpec((None, None, ATT_FEAT, tq), lambda hh, i: (hh, i, 0, 0)),
                  pl.BlockSpec((None, s, ATT_FEAT), lambda hh, i: (hh, 0, 0)),
                  pl.BlockSpec((None, s, HEAD_DIM), lambda hh, i: (hh, 0, 0)),
                  pl.BlockSpec((None, s // tk, HEAD_DIM, tk), lambda hh, i: (hh, 0, 0, 0)),
                  qspec, pl.BlockSpec((None, None, HEAD_DIM, tq), lambda hh, i: (hh, i, 0, 0)), rspec],
        out_specs=[qspec, rspec],
        out_shape=[_sds((h, HEAD_DIM, s)), _sds((h, 1, s))],
        compiler_params=_params("parallel", "arbitrary"),
    )(qt, k_aug, v, kt, ot, dot_, lse)


def _attn_bwd_dkv_t(name, qt_blocks, k_aug, v, qh, do, dot_blocks, lse, delta):
    h, s, _ = k_aug.shape
    tq, tk, ratio = _att_tiles(s)
    nq = s // tq

    def body(qt_ref, k_ref, v_ref, q_ref, do_ref, dot_ref, lse_ref, dl_ref, dk_ref, dv_ref, dck_ref, dsum_ref):
        kj = pl.program_id(1)
        kk = k_ref[...]
        vv = v_ref[...]
        dsum_ref[...] = jnp.zeros_like(dsum_ref)

        def block(qi, carry, masked):
            dk, dv = carry
            qs = pl.multiple_of(qi * tq, tq)
            st = jnp.dot(kk, qt_ref[qi], preferred_element_type=f32)
            p = jnp.exp(st - lse_ref[qi])
            if masked:
                p = jnp.where(_keys_le_queries(tk, tq, kj * tk, qs), p, 0.0)
            dv = dv + jnp.dot(p.astype(bf16), do_ref[pl.ds(qs, tq), :], preferred_element_type=f32)
            dp = jnp.dot(vv, dot_ref[qi], preferred_element_type=f32)
            ds = p * (dp - dl_ref[qi])
            dsum_ref[...] += ds
            dk = dk + jnp.dot(ds.astype(bf16), q_ref[pl.ds(qs, tq), :], preferred_element_type=f32)
            return dk, dv

        first = kj // ratio
        carry = block(first, (jnp.zeros((tk, HEAD_DIM), f32), jnp.zeros((tk, HEAD_DIM), f32)), True)
        dk, dv = lax.fori_loop(first + 1, nq, lambda qi, c: block(qi, c, False), carry)
        dk_ref[...] = dk
        dv_ref[...] = dv
        dck_ref[...] = -jnp.sum(dsum_ref[...], axis=1, keepdims=True)

    full = lambda shape: pl.BlockSpec((None,) + shape, lambda hh, j: (hh,) + (0,) * len(shape))
    kspec = pl.BlockSpec((None, tk, HEAD_DIM), lambda hh, j: (hh, j, 0))
    return _call(
        body,
        name=name,
        grid=(h, s // tk),
        in_specs=[full((nq, ATT_FEAT, tq)),
                  pl.BlockSpec((None, tk, ATT_FEAT), lambda hh, j: (hh, j, 0)),
                  kspec, full((s, HEAD_DIM)), full((s, HEAD_DIM)), full((nq, HEAD_DIM, tq)),
                  full((nq, 1, tq)), full((nq, 1, tq))],
        out_specs=[kspec, kspec, pl.BlockSpec((None, tk, 1), lambda hh, j: (hh, j, 0))],
        out_shape=[_sds((h, s, HEAD_DIM)), _sds((h, s, HEAD_DIM)), _sds((h, s, 1))],
        scratch_shapes=[pltpu.VMEM((tk, tq), f32)],
        compiler_params=_params("parallel", "arbitrary"),
    )(qt_blocks, k_aug, v, qh, do, dot_blocks, lse, delta)


C_LANES = 128


def _selections():
    h = jnp.arange(N_HEADS)[:, None, None]
    row = jnp.arange(D_B + 3 * C_LANES)[None, :, None]
    col = jnp.arange(ATT_FEAT)[None, None, :]
    head_col = (row < D_B) & (row // HEAD_DIM == h) & (col == row % HEAD_DIM)

    def c_part(p, lane0):
        return (row == D_B + p * C_LANES + h) & (col == lane0 + p)

    c_q = c_part(0, HEAD_DIM) | c_part(1, HEAD_DIM) | c_part(2, HEAD_DIM)
    c_k = c_part(0, HEAD_DIM + 3) | c_part(1, HEAD_DIM + 3) | c_part(2, HEAD_DIM + 3)
    sel_q = (head_col | c_q).astype(bf16)
    sel_k = head_col.astype(bf16) - c_k.astype(bf16)
    sel_h = head_col[:, :D_B, :HEAD_DIM].astype(bf16)
    lane = jnp.arange(ATT_FEAT)
    ones_q = ((lane >= HEAD_DIM + 3) & (lane < HEAD_DIM + 6)).astype(f32)
    ones_k = ((lane >= HEAD_DIM) & (lane < HEAD_DIM + 3)).astype(f32)
    return dict(sel_qt=sel_q.transpose(0, 2, 1), sel_k=sel_k, sel_h=sel_h, sel_ht=sel_h.transpose(0, 2, 1),
                ones_q=ones_q.reshape(ATT_FEAT, 1), ones_k=ones_k.reshape(1, ATT_FEAT))


def _attn_prep(name, z, c, sel):
    s = z.shape[0]
    tq, tk, ratio = _att_tiles(s)

    def body(q_ref, k_ref, v_ref, c_ref, sqt_ref, sk_ref, sh_ref, sht_ref, oq_ref, ok_ref,
             qt_out, ka_out, kt_out, vt_out, v_out, qh_out):
        cv = c_ref[...]
        hi = cv.astype(bf16)
        r1 = cv - hi.astype(f32)
        mid = r1.astype(bf16)
        lo = (r1 - mid.astype(f32)).astype(bf16)
        qs = (q_ref[...] * ATT_SCALE).astype(bf16)
        kb = k_ref[...].astype(bf16)
        vb = v_ref[...].astype(bf16)
        xq = jnp.concatenate([qs, hi, mid, lo], axis=-1)
        xk = jnp.concatenate([kb, hi, mid, lo], axis=-1)
        for h in range(N_HEADS):
            qt = lax.dot_general(sqt_ref[h], xq, _DN["nt"], preferred_element_type=f32) + oq_ref[...]
            qt_out[h, 0] = qt.astype(bf16)
            ka_out[h] = (jnp.dot(xk, sk_ref[h], preferred_element_type=f32) + ok_ref[...]).astype(bf16)
            kt = lax.dot_general(sht_ref[h], kb, _DN["nt"], preferred_element_type=f32).astype(bf16)
            vt = lax.dot_general(sht_ref[h], vb, _DN["nt"], preferred_element_type=f32).astype(bf16)
            for j in range(ratio):
                kt_out[h, j] = kt[:, j * tk:(j + 1) * tk]
                vt_out[h, j] = vt[:, j * tk:(j + 1) * tk]
            v_out[h] = jnp.dot(vb, sh_ref[h], preferred_element_type=f32).astype(bf16)
            qh_out[h] = jnp.dot(qs, sh_ref[h], preferred_element_type=f32).astype(bf16)

    whole = lambda a: pl.BlockSpec(a.shape, lambda i, nd=a.ndim: (0,) * nd)
    consts = [sel["sel_qt"], sel["sel_k"], sel["sel_h"], sel["sel_ht"], sel["ones_q"], sel["ones_k"]]
    return pl.pallas_call(
        body,
        name=name,
        grid=(s // tq,),
        in_specs=[pl.BlockSpec((tq, D_B), lambda i: (i, 2)), pl.BlockSpec((tq, D_B), lambda i: (i, 3)),
                  pl.BlockSpec((tq, D_B), lambda i: (i, 4)), pl.BlockSpec((tq, C_LANES), lambda i: (i, 0))]
        + [whole(a) for a in consts],
        out_specs=[pl.BlockSpec((N_HEADS, 1, ATT_FEAT, tq), lambda i: (0, i, 0, 0)),
                   pl.BlockSpec((N_HEADS, tq, ATT_FEAT), lambda i: (0, i, 0)),
                   pl.BlockSpec((N_HEADS, ratio, HEAD_DIM, tk), lambda i: (0, i, 0, 0)),
                   pl.BlockSpec((N_HEADS, ratio, HEAD_DIM, tk), lambda i: (0, i, 0, 0)),
                   pl.BlockSpec((N_HEADS, tq, HEAD_DIM), lambda i: (0, i, 0)),
                   pl.BlockSpec((N_HEADS, tq, HEAD_DIM), lambda i: (0, i, 0))],
        out_shape=[_sds((N_HEADS, s // tq, ATT_FEAT, tq), bf16), _sds((N_HEADS, s, ATT_FEAT), bf16),
                   _sds((N_HEADS, s // tk, HEAD_DIM, tk), bf16), _sds((N_HEADS, s // tk, HEAD_DIM, tk), bf16),
                   _sds((N_HEADS, s, HEAD_DIM), bf16), _sds((N_HEADS, s, HEAD_DIM), bf16)],
        compiler_params=_params("parallel"),
    )(z, z, z, c, *consts)


def _attn_do_prep(name, dob, sel):
    s = dob.shape[0]
    tq = _att_tiles(s)[0]

    def body(do_ref, sh_ref, sht_ref, dot_out, do_out):
        db = do_ref[...].astype(bf16)
        for h in range(N_HEADS):
            dot_out[h, 0] = lax.dot_general(sht_ref[h], db, _DN["nt"], preferred_element_type=f32).astype(bf16)
            do_out[h] = jnp.dot(db, sh_ref[h], preferred_element_type=f32).astype(bf16)

    whole = lambda a: pl.BlockSpec(a.shape, lambda i, nd=a.ndim: (0,) * nd)
    return pl.pallas_call(
        body,
        name=name,
        grid=(s // tq,),
        in_specs=[pl.BlockSpec((tq, D_B), lambda i: (i, 0)), whole(sel["sel_h"]), whole(sel["sel_ht"])],
        out_specs=[pl.BlockSpec((N_HEADS, 1, HEAD_DIM, tq), lambda i: (0, i, 0, 0)),
                   pl.BlockSpec((N_HEADS, tq, HEAD_DIM), lambda i: (0, i, 0))],
        out_shape=[_sds((N_HEADS, s // tq, HEAD_DIM, tq), bf16), _sds((N_HEADS, s, HEAD_DIM), bf16)],
        compiler_params=_params("parallel"),
    )(dob, sel["sel_h"], sel["sel_ht"])


def _dz_assemble(name, dax, dag, dqt, dkh, dvh, df, dcu, sel):
    s = dax.shape[0]
    tm = _tile(s, 512)

    def body(dax_ref, dag_ref, dqt_ref, dk_ref, dv_ref, df_ref, dcu_ref, sht_ref, o_ref):
        dq = jnp.zeros((tm, D_B), f32)
        dk = jnp.zeros((tm, D_B), f32)
        dv = jnp.zeros((tm, D_B), f32)
        for h in range(N_HEADS):
            place = sht_ref[h]
            dq = dq + lax.dot_general(dqt_ref[h].astype(bf16), place, _DN["tn"], preferred_element_type=f32)
            dk = dk + jnp.dot(dk_ref[h].astype(bf16), place, preferred_element_type=f32)
            dv = dv + jnp.dot(dv_ref[h].astype(bf16), place, preferred_element_type=f32)
        pieces = [dax_ref[...], dag_ref[...], dq, dk, dv, df_ref[...], dcu_ref[...]]
        off = 0
        for p in pieces:
            o_ref[:, off:off + p.shape[1]] = p.astype(bf16)
            off += p.shape[1]

    rows = lambda c_: pl.BlockSpec((tm, c_), lambda i: (i, 0))
    heads = pl.BlockSpec((N_HEADS, tm, HEAD_DIM), lambda i: (0, i, 0))
    return pl.pallas_call(
        body,
        name=name,
        grid=(s // tm,),
        in_specs=[rows(D_A), rows(D_A), pl.BlockSpec((N_HEADS, HEAD_DIM, tm), lambda i: (0, 0, i)), heads, heads,
                  rows(128), rows(D_C), pl.BlockSpec(sel["sel_ht"].shape, lambda i: (0, 0, 0))],
        out_specs=rows(N_IN_P),
        out_shape=_sds((s, N_IN_P), bf16),
        compiler_params=_params("parallel"),
    )(dax, dag, dqt, dkh, dvh, df, dcu, sel["sel_ht"])


def _s5_disc_fn(are, aim, ldt):
    dt = jnp.exp(ldt)
    er = jnp.exp(are * dt)
    br = er * jnp.cos(aim * dt)
    bi = er * jnp.sin(aim * dt)
    nr = br - 1.0
    den = are * are + aim * aim
    return br, bi, (nr * are + bi * aim) / den, (bi * are - nr * aim) / den


def _s5_disc(name, are, aim, ldt):
    def body(a_ref, b_ref, c_ref, o0, o1, o2, o3):
        r = _s5_disc_fn(a_ref[...], b_ref[...], c_ref[...])
        o0[...], o1[...], o2[...], o3[...] = r

    shp = _sds((S5_GROUPS, S5_STATE))
    return pl.pallas_call(body, name=name, out_shape=[shp] * 4)(are, aim, ldt)


def _s5_disc_bwd(name, are, aim, ldt, cts):
    def body(a_ref, b_ref, c_ref, d0, d1, d2, d3, o0, o1, o2):
        _, vjp = jax.vjp(_s5_disc_fn, a_ref[...], b_ref[...], c_ref[...])
        o0[...], o1[...], o2[...] = vjp((d0[...], d1[...], d2[...], d3[...]))

    shp = _sds((S5_GROUPS, S5_STATE))
    return pl.pallas_call(body, name=name, out_shape=[shp, shp, _sds((S5_GROUPS, 1))])(are, aim, ldt, *cts)


def _adamw_rows(w, g, m, v):
    m = ADAM_B1 * m + (1.0 - ADAM_B1) * g
    v = ADAM_B2 * v + (1.0 - ADAM_B2) * (g * g)
    m_hat = m / (1.0 - ADAM_B1 ** ADAM_STEP)
    v_hat = v / (1.0 - ADAM_B2 ** ADAM_STEP)
    return -ADAM_LR * (m_hat / (jnp.sqrt(v_hat) + ADAM_EPS) + ADAM_WD * w), m, v


def _adamw(name, w, ga, gb, m, v):
    rows, cols = w.shape
    tr = _row_tile(rows)

    def body(w_ref, ga_ref, gb_ref, m_ref, v_ref, g_out, d_out, m_out, v_out):
        g = ga_ref[...] + gb_ref[...]
        d, mm, vv = _adamw_rows(w_ref[...], g, m_ref[...], v_ref[...])
        g_out[...] = g
        d_out[...] = d
        m_out[...] = mm
        v_out[...] = vv

    spec = pl.BlockSpec((tr, cols), lambda i: (i, 0))
    return pl.pallas_call(
        body, name=name, grid=(rows // tr,), in_specs=[spec] * 5, out_specs=[spec] * 4,
        out_shape=[_sds((rows, cols))] * 4, compiler_params=_params("parallel"),
    )(w, ga, gb, m, v)


def _sum_stack(name, st):
    n, rows, cols = st.shape
    tr = _row_tile(rows)

    def body(s_ref, o_ref):
        acc = s_ref[0].astype(f32)
        for j in range(1, n):
            acc = acc + s_ref[j].astype(f32)
        o_ref[...] = acc

    return pl.pallas_call(
        body, name=name, grid=(rows // tr,), in_specs=[pl.BlockSpec((n, tr, cols), lambda i: (0, i, 0))],
        out_specs=pl.BlockSpec((tr, cols), lambda i: (i, 0)), out_shape=_sds((rows, cols)),
        compiler_params=_params("parallel"),
    )(st)


def _block_diag(w):
    h, n, m = w.shape
    return jnp.einsum("hij,hg->higj", w, jnp.eye(h, dtype=w.dtype)).reshape(h * n, h * m)


def _block_diag_part(dense, h):
    n, m = dense.shape[0] // h, dense.shape[1] // h
    return jnp.einsum("higj,hg->hij", dense.reshape(h, n, h, m), jnp.eye(h, dtype=dense.dtype))


def _s5_matrices(coef_re, coef_im, b_re, b_im, c_re, c_im):
    bb_re = coef_re[:, :, None] * b_re - coef_im[:, :, None] * b_im
    bb_im = coef_re[:, :, None] * b_im + coef_im[:, :, None] * b_re
    wb_re = _block_diag(jnp.swapaxes(bb_re, 1, 2))
    wb_im = _block_diag(jnp.swapaxes(bb_im, 1, 2))
    wc_re = _block_diag(jnp.swapaxes(c_re, 1, 2))
    wc_im = _block_diag(jnp.swapaxes(-c_im, 1, 2))
    return wb_re, wb_im, wc_re, wc_im


def _heads(t):
    s = t.shape[0]
    return t.reshape(s, N_HEADS, HEAD_DIM).transpose(1, 0, 2)


def _unheads(t):
    s = t.shape[1]
    return t.transpose(1, 0, 2).reshape(s, N_HEADS * HEAD_DIM)


def _shift_down(t):
    return jnp.concatenate([jnp.zeros((1, t.shape[1]), t.dtype), t[:-1]], axis=0)


def _shift_up(t):
    return jnp.concatenate([t[1:], jnp.zeros((1, t.shape[1]), t.dtype)], axis=0)


def _row(v):
    return v.reshape(1, -1)


def _ffn_fwd(tag, h, get, names, gamma, beta):
    wg, wu = get(names[0]), get(names[1])
    g, u, act = _ffn_up(tag + "_up", h, wg, wu)
    wd = get(names[2])
    r, out = _mm_ln(tag + "_down", act, wd, h, gamma, beta, 0.5, k_slabs=True)
    return out, dict(h=h, g=g, u=u, act=act, r=r, wg=wg, wu=wu, wd=wd)


def _ffn_bwd(tag, dout, sv, names, gamma, put):
    s = dout.shape[0]
    dr, dgam, dbet = _ln_bwd(tag + "_lnb", sv["r"], dout, gamma)
    put(names[2], _mm_plain(tag + "_dwd", "tn", _Slabs(sv["act"]), dr, (D_FF, D_MODEL, s), scale=0.5, out_dtype=bf16,
                            tiles=(FF_SLAB, 1024, _tile(s, 1024))))
    dg, du = _ffn_dact(tag + "_dact", dr, sv["wd"], sv["g"], sv["u"])
    dwg, dwu = _mm2(tag + "_dwgu", "tn", (D_MODEL, D_FF, s), sv["h"], _Slabs(dg), None, _Slabs(du), separate=True,
                    out_dtype=bf16, split_cols=True, tiles=(512, FF_SLAB, _tile(s, 1024)))
    put(names[0], dwg)
    put(names[1], dwu)
    dh = _mm2(tag + "_dh", "nt", (s, D_MODEL, D_FF), _Slabs(dg), _Slabs(sv["wg"]), _Slabs(du), _Slabs(sv["wu"]),
              add=dr, add_coef=ALPHA, tiles=(_tile(s, FFN_ROWS), 1024, FF_SLAB))[0]
    return dh, dgam, dbet


def _mixer_fwd(tag, h1, w):
    s = h1.shape[0]
    z = _mm_plain(tag + "_win", "nn", h1, w["w_in"], (s, N_IN_P, D_MODEL), tiles=(_tile(s, 512), 768, D_MODEL))
    ax, ag = z[:, :D_A], z[:, D_A:2 * D_A]
    f, cu = z[:, F_OFF:F_OFF + 128], z[:, CU_OFF:]
    xa = _conv_fwd(tag + "_conv", ax, w["conv_w"], w["conv_b"])
    a, gated = _rg_gates(tag + "_gates", xa, w["rg_wa"], w["rg_wx"], w["rg_ba"], w["rg_bx"], w["rg_lam"])
    ha = _lin_scan(tag + "_rgscan", a, gated, False)
    ones = jnp.ones((s, 128), f32)
    c = _lin_scan(tag + "_cumf", ones, _log_f(tag + "_logf", f, w["fox_bf"]), False)
    att = dict(zip(("qt", "k_aug", "kt", "vt", "v", "qh"), _attn_prep(tag + "_attnprep", z, c, w["sel"])))
    ot, lse = _attn_fwd_t(tag + "_attn", att["qt"], att["k_aug"], att["vt"])
    ob = ot.reshape(D_B, s).T
    bu_re, bu_im = _mm2(tag + "_s5in", "nn", (s, S5_LANES, D_C), cu, w["wb_re"], None, w["wb_im"], separate=True,
                        tiles=(_tile(s, 512), 1024, D_C))
    hre, him = _s5_scan(tag + "_s5scan", bu_re, bu_im, w["abar_re"], w["abar_im"], False)
    o = _mix_out(tag + "_mixout", ag, ha, ob, hre, him, cu, w["s5_d"], w["mix_g"], w["wc_re"], w["wc_im"], w["w_glu"])
    sv = dict(h1=h1, ax=ax, ag=ag, f=f, cu=cu, xa=xa, a=a, ha=ha, att=att, ot=ot, lse=lse, ob=ob, hre=hre, him=him, o=o)
    return o, sv


def _mixer_bwd(tag, do, dr2, sv, w, put):
    s = do.shape[0]
    (dag, dha, dob, dhre, dhim, dcu1, dwcr, dwci, dwglu, dd, dgn) = _mix_out_bwd(
        tag + "_mixoutb", do, sv["ag"], sv["ha"], sv["ob"], sv["hre"], sv["him"], sv["cu"], w["s5_d"], w["mix_g"],
        w["wc_re"], w["wc_im"], w["w_glu"])
    put("s5_w_glu", dwglu.astype(bf16))
    gre, gim = _s5_scan(tag + "_s5scanb", dhre, dhim, w["abar_re"], -w["abar_im"], True)
    dab_re, dab_im = _s5_decay_grad(tag + "_s5dec", _shift_down(sv["hre"]), _shift_down(sv["him"]), gre, gim)
    dwb_re, dwb_im = _mm2(tag + "_s5dwb", "tn", (D_C, S5_LANES, s), sv["cu"], gre, None, gim, separate=True,
                          tiles=(D_C, 1024, _tile(s, 1024)))
    dcu = _mm2(tag + "_s5dcu", "nt", (s, D_C, S5_LANES), gre, w["wb_re"], gim, w["wb_im"], add=dcu1,
               tiles=(_tile(s, 512), D_C, 1024))[0]
    att = sv["att"]
    tq = _att_tiles(s)[0]
    nt = s // tq
    dot_blocks, doh = _attn_do_prep(tag + "_doprep", dob, w["sel"])
    dqt, delta = _attn_bwd_dq_t(tag + "_attndq", att["qt"], att["k_aug"], att["v"], att["kt"], sv["ot"], dot_blocks,
                                sv["lse"])
    dkh, dvh, dck = _attn_bwd_dkv_t(tag + "_attndkv", att["qt"], att["k_aug"], att["v"], att["qh"], doh, dot_blocks,
                                    sv["lse"].reshape(N_HEADS, nt, 1, tq), delta.reshape(N_HEADS, nt, 1, tq))
    dc = jnp.pad(dck[:, :, 0].T, ((0, 0), (0, 128 - N_HEADS)))
    dlf = _lin_scan(tag + "_cumfb", jnp.ones((s, 128), f32), dc, True)
    df, dbf = _log_f_bwd(tag + "_logfb", dlf, sv["f"], w["fox_bf"])
    ga = _lin_scan(tag + "_rgscanb", _shift_up(sv["a"]), dha, True)
    dxa, dwa, dwx, dba, dbx, dlam = _rg_gates_bwd(tag + "_gatesb", sv["xa"], ga, _shift_down(sv["ha"]), w["rg_wa"],
                                                  w["rg_wx"], w["rg_ba"], w["rg_bx"], w["rg_lam"])
    dax, dconv = _conv_bwd(tag + "_convb", dxa, sv["ax"], w["conv_w"])
    dz = _dz_assemble(tag + "_dz", dax, dag, dqt, dkh, dvh, df, dcu, w["sel"])
    put("w_in", _mm_plain(tag + "_dwin", "tn", sv["h1"], dz, (D_MODEL, N_IN_P, s), out_dtype=bf16,
                          tiles=(512, 768, _tile(s, 1024))))
    dh1 = _mm_plain(tag + "_dh1", "nt", dz, w["w_in"], (s, D_MODEL, N_IN_P), add=dr2, add_coef=ALPHA,
                    tiles=(_tile(s, 512), 1024, 768))
    grads = dict(dconv=dconv, dwa=dwa, dwx=dwx, dba=dba, dbx=dbx, dlam=dlam, dbf=dbf,
                 dab_re=dab_re, dab_im=dab_im, dwb_re=dwb_re, dwb_im=dwb_im, dwcr=dwcr, dwci=dwci, dd=dd, dgn=dgn)
    return dh1, grads


SMALL_NAMES = ["ln1_g", "ln1_b", "conv_w", "conv_b", "rg_w_a", "rg_b_a", "rg_w_x", "rg_b_x", "rg_lambda", "fox_b_f",
               "s5_a_re", "s5_a_im", "s5_log_dt", "s5_b_re", "s5_b_im", "s5_c_re", "s5_c_im", "s5_d", "mix_norm_g",
               "ln2_g", "ln2_b", "ln3_g", "ln3_b"]
BIG_NAMES = ["ffn1_w_gate", "ffn1_w_up", "ffn1_w_down", "w_in", "s5_w_glu", "w_out", "ffn2_w_gate", "ffn2_w_up",
             "ffn2_w_down"]


def _local_step(x, target, weight, small, on_grads):
    h = x
    saved = []
    sel = _selections()
    for l in range(DEPTH):
        get = functools.partial(weight, l)

        sm = {n: small[n][l] for n in SMALL_NAMES}
        abar_re, abar_im, coef_re, coef_im = _s5_disc(f"l{l}_s5disc", sm["s5_a_re"], sm["s5_a_im"],
                                                      sm["s5_log_dt"].reshape(S5_GROUPS, 1))
        mats, mats_vjp = jax.vjp(_s5_matrices, coef_re, coef_im, sm["s5_b_re"], sm["s5_b_im"], sm["s5_c_re"],
                                 sm["s5_c_im"])
        w = dict(
            sel=sel, conv_w=sm["conv_w"], conv_b=_row(sm["conv_b"]),
            rg_wa=_block_diag(sm["rg_w_a"]).astype(bf16), rg_wx=_block_diag(sm["rg_w_x"]).astype(bf16),
            rg_ba=_row(sm["rg_b_a"]), rg_bx=_row(sm["rg_b_x"]), rg_lam=_row(sm["rg_lambda"]),
            fox_bf=jnp.pad(_row(sm["fox_b_f"]), ((0, 0), (0, 128 - N_HEADS))),
            abar_re=_row(abar_re), abar_im=_row(abar_im),
            wb_re=mats[0].astype(bf16), wb_im=mats[1].astype(bf16), wc_re=mats[2].astype(bf16),
            wc_im=mats[3].astype(bf16), s5_d=_row(sm["s5_d"]), mix_g=_row(sm["mix_norm_g"]))
        h1, sv1 = _ffn_fwd(f"l{l}_ffn1", h, get, GROUPS["F1"], _row(sm["ln1_g"]), _row(sm["ln1_b"]))
        w["w_in"], w["w_glu"] = get("w_in"), get("s5_w_glu")
        o, svm = _mixer_fwd(f"l{l}_mix", h1, w)
        w_out = get("w_out")
        r2, h2 = _mm_ln(f"l{l}_wout", o, w_out, h1, _row(sm["ln2_g"]), _row(sm["ln2_b"]), 1.0)
        h3, sv2 = _ffn_fwd(f"l{l}_ffn2", h2, get, GROUPS["F2"], _row(sm["ln3_g"]), _row(sm["ln3_b"]))
        saved.append(dict(sm=sm, w=w, w_out=w_out, sv1=sv1, svm=svm, r2=r2, sv2=sv2, mats_vjp=mats_vjp))
        h = h3

    dh, loss_row = _loss_head("loss_head", h, target)
    s = x.shape[0]
    gsmall = {n: [None] * DEPTH for n in SMALL_NAMES}
    for l in reversed(range(DEPTH)):
        sd = saved[l]
        sm, w = sd["sm"], sd["w"]

        def put(name, grad, l=l):
            on_grads((l, name), grad)

        dh2, dgam, dbet = _ffn_bwd(f"l{l}_ffn2", dh, sd["sv2"], GROUPS["F2"], _row(sm["ln3_g"]), put)
        gsmall["ln3_g"][l], gsmall["ln3_b"][l] = dgam[0], dbet[0]
        dr2, dgam, dbet = _ln_bwd(f"l{l}_ln2b", sd["r2"], dh2, _row(sm["ln2_g"]))
        gsmall["ln2_g"][l], gsmall["ln2_b"][l] = dgam[0], dbet[0]
        put("w_out", _mm_plain(f"l{l}_dwout", "tn", sd["svm"]["o"], dr2, (D_MODEL, D_MODEL, s), out_dtype=bf16))
        do = _mm_plain(f"l{l}_do", "nt", dr2, sd["w_out"], (s, D_MODEL, D_MODEL))
        dh1, g = _mixer_bwd(f"l{l}_mix", do, dr2, sd["svm"], w, put)
        gsmall["conv_w"][l], gsmall["conv_b"][l] = g["dconv"][:CONV_WIDTH], g["dconv"][CONV_WIDTH]
        gsmall["rg_w_a"][l] = _block_diag_part(g["dwa"], N_HEADS)
        gsmall["rg_w_x"][l] = _block_diag_part(g["dwx"], N_HEADS)
        gsmall["rg_b_a"][l], gsmall["rg_b_x"][l], gsmall["rg_lambda"][l] = g["dba"][0], g["dbx"][0], g["dlam"][0]
        gsmall["fox_b_f"][l] = g["dbf"][0, :N_HEADS]
        dcoef_re, dcoef_im, db_re, db_im, dc_re, dc_im = sd["mats_vjp"]((g["dwb_re"], g["dwb_im"], g["dwcr"], g["dwci"]))
        da_re, da_im, dldt = _s5_disc_bwd(
            f"l{l}_s5discb", sm["s5_a_re"], sm["s5_a_im"], sm["s5_log_dt"].reshape(S5_GROUPS, 1),
            (g["dab_re"].reshape(S5_GROUPS, S5_STATE), g["dab_im"].reshape(S5_GROUPS, S5_STATE), dcoef_re, dcoef_im))
        gsmall["s5_a_re"][l], gsmall["s5_a_im"][l], gsmall["s5_log_dt"][l] = da_re, da_im, dldt[:, 0]
        gsmall["s5_b_re"][l], gsmall["s5_b_im"][l], gsmall["s5_c_re"][l], gsmall["s5_c_im"][l] = db_re, db_im, dc_re, dc_im
        gsmall["s5_d"][l], gsmall["mix_norm_g"][l] = g["dd"][0], g["dgn"][0]
        dh, dgam, dbet = _ffn_bwd(f"l{l}_ffn1", dh1, sd["sv1"], GROUPS["F1"], _row(sm["ln1_g"]), put)
        gsmall["ln1_g"][l], gsmall["ln1_b"][l] = dgam[0], dbet[0]
    gsmall = {n: jnp.stack(v) for n, v in gsmall.items()}
    return loss_row[0, 0], dh, gsmall


def _position():
    return lax.axis_index("x"), lax.axis_index("y"), lax.axis_index("c")


_ANY = pl.BlockSpec(memory_space=pl.ANY)


def _chip_gather(name, shards):
    n = len(shards)

    def body(*refs):
        in_refs, out_refs = refs[:n], refs[n:2 * n]
        send_sems, recv_sems, local_sems = refs[2 * n:]
        x, y, c = _position()
        me = 2 * x + y
        peers = [(1 - x, y), (x, 1 - y), (1 - x, 1 - y)]
        local = [pltpu.make_async_copy(in_refs[i], out_refs[i].at[me], local_sems.at[i]) for i in range(n)]
        for cp in local:
            cp.start()
        sends = []
        for i in range(n):
            for r, (px, py) in enumerate(peers):
                cp = pltpu.make_async_remote_copy(
                    src_ref=in_refs[i], dst_ref=out_refs[i].at[me], send_sem=send_sems.at[3 * i + r],
                    recv_sem=recv_sems.at[3 * i + r], device_id=(px, py, c), device_id_type=MESH)
                cp.start()
                sends.append(cp)
        for i in range(n):
            for r, (px, py) in enumerate(peers):
                pltpu.make_async_remote_copy(
                    src_ref=in_refs[i], dst_ref=out_refs[i].at[2 * px + py], send_sem=send_sems.at[3 * i + r],
                    recv_sem=recv_sems.at[3 * i + r], device_id=(px, py, c), device_id_type=MESH).wait_recv()
        for cp in sends:
            cp.wait_send()
        for cp in local:
            cp.wait()

    return pl.pallas_call(
        body, name=name, in_specs=[_ANY] * n, out_specs=[_ANY] * n,
        out_shape=[_sds((N_CHIPS,) + a.shape, a.dtype) for a in shards],
        scratch_shapes=[pltpu.SemaphoreType.DMA((3 * n,)), pltpu.SemaphoreType.DMA((3 * n,)),
                        pltpu.SemaphoreType.DMA((n,))],
    )(*shards)


def _chip_scatter(name, stacks):
    n = len(stacks)

    def body(*refs):
        in_refs, out_refs = refs[:n], refs[n:2 * n]
        send_sems, recv_sems, local_sems = refs[2 * n:]
        x, y, c = _position()
        me = 2 * x + y
        peers = [(1 - x, y), (x, 1 - y), (1 - x, 1 - y)]
        local = [pltpu.make_async_copy(in_refs[i].at[me], out_refs[i].at[me], local_sems.at[i]) for i in range(n)]
        for cp in local:
            cp.start()
        sends = []
        for i in range(n):
            for r, (px, py) in enumerate(peers):
                cp = pltpu.make_async_remote_copy(
                    src_ref=in_refs[i].at[2 * px + py], dst_ref=out_refs[i].at[me], send_sem=send_sems.at[3 * i + r],
                    recv_sem=recv_sems.at[3 * i + r], device_id=(px, py, c), device_id_type=MESH)
                cp.start()
                sends.append(cp)
        for i in range(n):
            for r, (px, py) in enumerate(peers):
                pltpu.make_async_remote_copy(
                    src_ref=in_refs[i].at[me], dst_ref=out_refs[i].at[2 * px + py], send_sem=send_sems.at[3 * i + r],
                    recv_sem=recv_sems.at[3 * i + r], device_id=(px, py, c), device_id_type=MESH).wait_recv()
        for cp in sends:
            cp.wait_send()
        for cp in local:
            cp.wait()

    return pl.pallas_call(
        body, name=name, in_specs=[_ANY] * n, out_specs=[_ANY] * n,
        out_shape=[_sds(a.shape, a.dtype) for a in stacks],
        scratch_shapes=[pltpu.SemaphoreType.DMA((3 * n,)), pltpu.SemaphoreType.DMA((3 * n,)),
                        pltpu.SemaphoreType.DMA((n,))],
    )(*stacks)


def _sibling_swap(name, arrs):
    n = len(arrs)

    def body(*refs):
        in_refs, out_refs = refs[:n], refs[n:2 * n]
        send_sems, recv_sems = refs[2 * n:]
        x, y, c = _position()
        copies = [pltpu.make_async_remote_copy(
            src_ref=in_refs[i], dst_ref=out_refs[i], send_sem=send_sems.at[i], recv_sem=recv_sems.at[i],
            device_id=(x, y, 1 - c), device_id_type=MESH) for i in range(n)]
        for cp in copies:
            cp.start()
        for cp in copies:
            cp.wait_recv()
        for cp in copies:
            cp.wait_send()

    return pl.pallas_call(
        body, name=name, in_specs=[_ANY] * n, out_specs=[_ANY] * n,
        out_shape=[_sds(a.shape, a.dtype) for a in arrs],
        scratch_shapes=[pltpu.SemaphoreType.DMA((n,)), pltpu.SemaphoreType.DMA((n,))],
    )(*arrs)


def _dev_gather(name, arr):
    def body(in_ref, out_ref, send_sems, recv_sems, local_sem):
        x, y, c = _position()
        me = 4 * x + 2 * y + c
        local = pltpu.make_async_copy(in_ref, out_ref.at[me], local_sem)
        local.start()
        peers = []
        for k in range(1, N_DEV):
            peers.append((1 - x if k & 4 else x, 1 - y if k & 2 else y, 1 - c if k & 1 else c))
        sends = []
        for k, peer in enumerate(peers):
            cp = pltpu.make_async_remote_copy(src_ref=in_ref, dst_ref=out_ref.at[me], send_sem=send_sems.at[k],
                                              recv_sem=recv_sems.at[k], device_id=peer, device_id_type=MESH)
            cp.start()
            sends.append(cp)
        for k, (px, py, pc) in enumerate(peers):
            pltpu.make_async_remote_copy(src_ref=in_ref, dst_ref=out_ref.at[4 * px + 2 * py + pc],
                                         send_sem=send_sems.at[k], recv_sem=recv_sems.at[k], device_id=(px, py, pc),
                                         device_id_type=MESH).wait_recv()
        for cp in sends:
            cp.wait_send()
        local.wait()

    return pl.pallas_call(
        body, name=name, in_specs=[_ANY], out_specs=_ANY, out_shape=_sds((N_DEV,) + arr.shape, arr.dtype),
        scratch_shapes=[pltpu.SemaphoreType.DMA((N_DEV - 1,)), pltpu.SemaphoreType.DMA((N_DEV - 1,)),
                        pltpu.SemaphoreType.DMA],
    )(arr)


COLUMN_SHARDED = ("ffn1_w_gate", "ffn1_w_up", "ffn2_w_gate", "ffn2_w_up")
PACK_QUANTUM = 128 * 256


def _permute_in_cols(w):
    pad = jnp.zeros(w.shape[:-1] + (128 - N_HEADS,), w.dtype)
    return jnp.concatenate([w[..., :F_OFF + N_HEADS], pad, w[..., F_OFF + N_HEADS:]], axis=-1)


def _unpermute_in_cols(w):
    return jnp.concatenate([w[..., :F_OFF + N_HEADS], w[..., CU_OFF:]], axis=-1)


def _unstack(name, st):
    _, l, r, c = st.shape
    if name in COLUMN_SHARDED:
        return st.transpose(1, 2, 0, 3).reshape(l, r, N_CHIPS * c)
    return st.transpose(1, 0, 2, 3).reshape(l, N_CHIPS * r, c)


def _restack(name, g):
    l, r, c = g.shape
    if name in COLUMN_SHARDED:
        return g.reshape(l, r, N_CHIPS, c // N_CHIPS).transpose(2, 0, 1, 3)
    return g.reshape(l, N_CHIPS, r // N_CHIPS, c).transpose(1, 0, 2, 3)


def _pack(arrs):
    flat = jnp.concatenate([a.reshape(-1) for a in arrs])
    pad = -flat.shape[0] % PACK_QUANTUM
    return jnp.pad(flat, (0, pad)).reshape(-1, 128)


def _unpack(buf, shapes):
    flat = buf.reshape(-1)
    out, off = [], 0
    for shp in shapes:
        size = math.prod(shp)
        out.append(flat[off:off + size].reshape(shp))
        off += size
    return out


WEIGHT_NAMES = ["ffn1_w_gate", "ffn1_w_up", "ffn1_w_down", "ln1_g", "ln1_b", "w_in", "conv_w", "conv_b", "rg_w_a",
                "rg_b_a", "rg_w_x", "rg_b_x", "rg_lambda", "fox_b_f", "s5_a_re", "s5_a_im", "s5_log_dt", "s5_b_re",
                "s5_b_im", "s5_c_re", "s5_c_im", "s5_d", "s5_w_glu", "mix_norm_g", "w_out", "ln2_g", "ln2_b",
                "ffn2_w_gate", "ffn2_w_up", "ffn2_w_down", "ln3_g", "ln3_b"]


def _train_step(x, loss_target, w, m, v):
    ix, iy, _ = _position()
    chip = 2 * ix + iy

    shards = [(_permute_in_cols(w[n]) if n == "w_in" else w[n]).astype(bf16) for n in BIG_NAMES]
    stacks = _chip_gather("gather_weights", shards + [w["conv_w"]])
    big = {n: _unstack(n, st) for n, st in zip(BIG_NAMES, stacks)}
    small = {n: w[n] for n in SMALL_NAMES}
    small["conv_w"] = stacks[-1].transpose(1, 2, 0, 3).reshape(DEPTH, CONV_WIDTH, D_A)

    loss_local, gx, gbig, gsmall = _local_step(x[0], loss_target[0], big, small)

    sent = [_restack(n, gbig[n]).astype(bf16) for n in BIG_NAMES]
    recv = _chip_scatter("scatter_grads", sent)
    partial = {}
    for n, st in zip(BIG_NAMES, recv):
        _, l, r, c = st.shape
        p = _sum_stack("sum_" + n, st.reshape(N_CHIPS, l * r, c))
        partial[n] = _unpermute_in_cols(p) if n == "w_in" else p
    other = dict(zip(BIG_NAMES, _sibling_swap("swap_grads", [partial[n] for n in BIG_NAMES])))

    small_shapes = [gsmall[n].shape for n in SMALL_NAMES]
    total = _sum_stack("sum_small", _dev_gather("gather_small", _pack([gsmall[n] for n in SMALL_NAMES])))
    gsm = dict(zip(SMALL_NAMES, _unpack(total, small_shapes)))
    cw = D_A // N_CHIPS
    gsm["conv_w"] = lax.dynamic_slice_in_dim(gsm["conv_w"], chip * cw, cw, axis=2)

    grads, deltas, new_m, new_v = {}, {}, {}, {}
    for n in BIG_NAMES:
        shp = w[n].shape
        two_d = (shp[0] * shp[1], shp[2])
        g, d, mm, vv = _adamw("adamw_" + n, w[n].reshape(two_d), partial[n], other[n], m[n].reshape(two_d),
                              v[n].reshape(two_d))
        grads[n], deltas[n], new_m[n], new_v[n] = (t.reshape(shp) for t in (g, d, mm, vv))
    shapes = [w[n].shape for n in SMALL_NAMES]
    gp = _pack([gsm[n] for n in SMALL_NAMES])
    res = _adamw("adamw_small", _pack([w[n] for n in SMALL_NAMES]), gp, jnp.zeros_like(gp),
                 _pack([m[n] for n in SMALL_NAMES]), _pack([v[n] for n in SMALL_NAMES]))
    for dst, buf in zip((grads, deltas, new_m, new_v), res):
        dst.update(zip(SMALL_NAMES, _unpack(buf, shapes)))

    loss = lax.psum(loss_local, ("x", "y", "c"))
    return (loss, gx[None], *[grads[n] for n in WEIGHT_NAMES], *[deltas[n] for n in WEIGHT_NAMES],
            *[new_m[n] for n in WEIGHT_NAMES], *[new_v[n] for n in WEIGHT_NAMES])


def _remote(src, dst, send_sems, recv_sems, k, peer):
    return pltpu.make_async_remote_copy(src_ref=src, dst_ref=dst, send_sem=send_sems.at[k], recv_sem=recv_sems.at[k],
                                        device_id=peer, device_id_type=MESH)


class _ChipGatherPart:
    def __init__(self, arrays):
        self.arrays, self.results = list(arrays), None

    def out_shape(self):
        return [_sds((N_CHIPS,) + a.shape, a.dtype) for a in self.arrays]

    def sems(self):
        n = len(self.arrays)
        return [pltpu.SemaphoreType.DMA((3 * n,)), pltpu.SemaphoreType.DMA((3 * n,)), pltpu.SemaphoreType.DMA((n,))]

    def copies(self, ins, outs, sems):
        send_sems, recv_sems, local_sems = sems
        x, y, c = _position()
        me = 2 * x + y
        local, sends, recvs = [], [], []
        for i, (src, dst) in enumerate(zip(ins, outs)):
            local.append(pltpu.make_async_copy(self.mine(src, me), dst.at[me], local_sems.at[i]))
            for r, (px, py) in enumerate([(1 - x, y), (x, 1 - y), (1 - x, 1 - y)]):
                peer = 2 * px + py
                sends.append(_remote(self.theirs(src, peer), dst.at[me], send_sems, recv_sems, 3 * i + r, (px, py, c)))
                recvs.append(_remote(self.mine(src, me), dst.at[peer], send_sems, recv_sems, 3 * i + r, (px, py, c)))
        return local, sends, recvs

    def mine(self, src, me):
        return src

    def theirs(self, src, peer):
        return src


class _ChipScatterPart(_ChipGatherPart):
    def out_shape(self):
        return [_sds(a.shape, a.dtype) for a in self.arrays]

    def mine(self, src, me):
        return src.at[me]

    def theirs(self, src, peer):
        return src.at[peer]


class _SiblingSwapPart:
    def __init__(self, arrays):
        self.arrays, self.results = list(arrays), None

    def out_shape(self):
        return [_sds(a.shape, a.dtype) for a in self.arrays]

    def sems(self):
        n = len(self.arrays)
        return [pltpu.SemaphoreType.DMA((n,)), pltpu.SemaphoreType.DMA((n,))]

    def copies(self, ins, outs, sems):
        x, y, c = _position()
        both = [_remote(src, dst, sems[0], sems[1], i, (x, y, 1 - c)) for i, (src, dst) in enumerate(zip(ins, outs))]
        return [], both, both


class _DevGatherPart:
    def __init__(self, array):
        self.arrays, self.results = [array], None

    def out_shape(self):
        return [_sds((N_DEV,) + self.arrays[0].shape, self.arrays[0].dtype)]

    def sems(self):
        return [pltpu.SemaphoreType.DMA((N_DEV - 1,)), pltpu.SemaphoreType.DMA((N_DEV - 1,)),
                pltpu.SemaphoreType.DMA((1,))]

    def copies(self, ins, outs, sems):
        send_sems, recv_sems, local_sems = sems
        (src,), (dst,) = ins, outs
        x, y, c = _position()
        me = 4 * x + 2 * y + c
        local = [pltpu.make_async_copy(src, dst.at[me], local_sems.at[0])]
        sends, recvs = [], []
        for k in range(1, N_DEV):
            px, py, pc = (1 - x if k & 4 else x, 1 - y if k & 2 else y, 1 - c if k & 1 else c)
            sends.append(_remote(src, dst.at[me], send_sems, recv_sems, k - 1, (px, py, pc)))
            recvs.append(_remote(src, dst.at[4 * px + 2 * py + pc], send_sems, recv_sems, k - 1, (px, py, pc)))
        return local, sends, recvs


def _split_by(parts, refs, count):
    out, off = [], 0
    for p in parts:
        out.append(refs[off:off + count(p)])
        off += count(p)
    return out


def _parts_refs(parts, in_refs, out_refs, sem_refs):
    return zip(parts, _split_by(parts, in_refs, lambda p: len(p.arrays)),
               _split_by(parts, out_refs, lambda p: len(p.arrays)), _split_by(parts, sem_refs, lambda p: len(p.sems())))


def _exchange_start(parts, in_refs, out_refs, sem_refs):
    for part, ins, outs, sems in _parts_refs(parts, in_refs, out_refs, sem_refs):
        local, sends, _ = part.copies(ins, outs, sems)
        for cp in local + sends:
            cp.start()


def _exchange_finish(parts, in_refs, out_refs, sem_refs):
    for part, ins, outs, sems in _parts_refs(parts, in_refs, out_refs, sem_refs):
        local, sends, recvs = part.copies(ins, outs, sems)
        for cp in recvs:
            cp.wait_recv()
        for cp in sends:
            cp.wait_send()
        for cp in local:
            cp.wait()


def _exchange_operands(parts):
    return ([a for p in parts for a in p.arrays], [s for p in parts for s in p.out_shape()],
            [s for p in parts for s in p.sems()])


def _set_results(parts, res):
    for part, outs in zip(parts, _split_by(parts, list(res), lambda p: len(p.arrays))):
        part.results = list(outs)


def _exchange_now(name, parts):
    x_in, x_out, x_sem = _exchange_operands(parts)
    n = len(x_in)

    def body(*refs):
        _exchange_start(parts, refs[:n], refs[n:2 * n], refs[2 * n:])
        _exchange_finish(parts, refs[:n], refs[n:2 * n], refs[2 * n:])

    res = pl.pallas_call(body, name=name, in_specs=[_ANY] * n, out_specs=[_ANY] * n, out_shape=x_out,
                         scratch_shapes=x_sem)(*x_in)
    _set_results(parts, res)


_RIDERS = {}


def _call(body, *, name, grid, in_specs, out_specs, out_shape, scratch_shapes=(), compiler_params=None):
    make_parts = _RIDERS.pop(name, None)
    if make_parts is None:
        return pl.pallas_call(body, name=name, grid=grid, in_specs=in_specs, out_specs=out_specs, out_shape=out_shape,
                              scratch_shapes=scratch_shapes, compiler_params=compiler_params)
    parts = make_parts()
    x_in, x_out, x_sem = _exchange_operands(parts)
    n_out, n_scr, n_x = len(out_shape), len(scratch_shapes), len(x_in)

    def run(*args):
        n_in = len(args)

        def hosted(*refs):
            ins, xi = refs[:n_in], refs[n_in:n_in + n_x]
            outs, xo = refs[n_in + n_x:n_in + n_x + n_out], refs[n_in + n_x + n_out:n_in + 2 * n_x + n_out]
            scr, xs = refs[n_in + 2 * n_x + n_out:n_in + 2 * n_x + n_out + n_scr], refs[n_in + 2 * n_x + n_out + n_scr:]
            first = functools.reduce(jnp.logical_and, [pl.program_id(d) == 0 for d in range(len(grid))])
            last = functools.reduce(jnp.logical_and, [pl.program_id(d) == grid[d] - 1 for d in range(len(grid))])

            @pl.when(first)
            def _():
                _exchange_start(parts, xi, xo, xs)

            body(*ins, *outs, *scr)

            @pl.when(last)
            def _():
                _exchange_finish(parts, xi, xo, xs)

        res = pl.pallas_call(
            hosted, name=name, grid=grid, in_specs=list(in_specs) + [_ANY] * n_x,
            out_specs=list(out_specs) + [_ANY] * n_x, out_shape=list(out_shape) + x_out,
            scratch_shapes=list(scratch_shapes) + x_sem, compiler_params=_params(*["arbitrary"] * len(grid)),
        )(*args, *x_in)
        _set_results(parts, res[n_out:])
        return list(res[:n_out])

    return run


GROUPS = {"F1": ["ffn1_w_gate", "ffn1_w_up", "ffn1_w_down"], "MX": ["w_in", "s5_w_glu", "w_out"],
          "F2": ["ffn2_w_gate", "ffn2_w_up", "ffn2_w_down"]}
GROUP_OF = {n: g for g, names in GROUPS.items() for n in names}
FIRST_GATHER = [(0, "ffn1_w_gate"), (0, "ffn1_w_up")]
GATHER_HOSTS = {
    "l0_ffn1_up": [(0, "ffn1_w_down")],
    "l0_ffn1_down": [(0, "w_in"), (0, "s5_w_glu"), (0, "w_out")],
    "l0_mix_win": [(0, "ffn2_w_gate")],
    "l0_mix_attn": [(0, "ffn2_w_up"), (0, "ffn2_w_down")],
    "l0_ffn2_up": [(1, "ffn1_w_gate")],
    "l0_ffn2_down": [(1, "ffn1_w_up")],
    "l1_ffn1_up": [(1, "ffn1_w_down")],
    "l1_ffn1_down": [(1, "w_in"), (1, "s5_w_glu"), (1, "w_out")],
    "l1_mix_win": [(1, "ffn2_w_gate")],
    "l1_mix_attn": [(1, "ffn2_w_up"), (1, "ffn2_w_down")],
}
SCATTER_HOSTS = {
    "l1_ffn2_dact": [(1, "ffn2_w_down")],
    "l1_ffn2_dh": [(1, "ffn2_w_gate")],
    "l1_mix_attndq": [(1, "ffn2_w_up")],
    "l1_mix_attndkv": [(1, "w_out"), (1, "s5_w_glu")],
    "l1_mix_dh1": [(1, "w_in")],
    "l1_ffn1_dact": [(1, "ffn1_w_down")],
    "l1_ffn1_dh": [(1, "ffn1_w_gate")],
    "l0_ffn2_dact": [(1, "ffn1_w_up")],
    "l0_ffn2_dwgu": [(0, "ffn2_w_down")],
    "l0_ffn2_dh": [(0, "ffn2_w_gate")],
    "l0_mix_attndq": [(0, "w_out"), (0, "s5_w_glu"), (0, "ffn2_w_up")],
    "l0_mix_dh1": [(0, "w_in")],
    "l0_ffn1_dact": [(0, "ffn1_w_down")],
    "l0_ffn1_dh": [(0, "ffn1_w_gate")],
}
LAST_SCATTER = [(0, "ffn1_w_up")]


def _unstack_layer(name, st):
    _, r, c = st.shape
    if name in COLUMN_SHARDED:
        return st
    return st.reshape(N_CHIPS * r, c)


def _restack_layer(name, g):
    if name in COLUMN_SHARDED:
        return g
    r, c = g.shape
    return g.reshape(N_CHIPS, r // N_CHIPS, c)


def _adamw_layer(name, layer, w, ga, gb, m, v, bufs):
    _, r, c = w.shape
    tr = _row_tile(r)

    def body(w_ref, ga_ref, gb_ref, m_ref, v_ref, *rest):
        g_out, d_out, m_out, v_out = rest[-4:]
        g = ga_ref[...] + gb_ref[...]
        d, mm, vv = _adamw_rows(w_ref[...], g, m_ref[...], v_ref[...])
        g_out[...] = g
        d_out[...] = d
        m_out[...] = mm
        v_out[...] = vv

    full = pl.BlockSpec((None, tr, c), lambda i: (layer, i, 0))
    flat = pl.BlockSpec((tr, c), lambda i: (i, 0))
    extra = {} if bufs is None else dict(input_output_aliases={5 + k: k for k in range(4)})
    return pl.pallas_call(
        body, name=name, grid=(r // tr,),
        in_specs=[full, flat, flat, full, full] + ([] if bufs is None else [_ANY] * 4),
        out_specs=[full] * 4, out_shape=[_sds(w.shape)] * 4, compiler_params=_params("parallel"), **extra,
    )(w, ga, gb, m, v, *([] if bufs is None else bufs))


def _train_step(x, loss_target, w, m, v):
    ix, iy, _ = _position()
    chip = 2 * ix + iy
    shard = {n: (_permute_in_cols(w[n]) if n == "w_in" else w[n]).astype(bf16) for n in BIG_NAMES}

    gathered = {}

    def gather_parts(keys, extra=()):
        part = _ChipGatherPart([shard[n][layer] for layer, n in keys] + list(extra))
        gathered.update({key: (part, i) for i, key in enumerate(keys)})
        return [part]

    (first,) = gather_parts(FIRST_GATHER, extra=[w["conv_w"]])
    _exchange_now("gather_first", [first])
    for host, keys in GATHER_HOSTS.items():
        _RIDERS[host] = functools.partial(gather_parts, keys)

    def weight(layer, name):
        part, i = gathered[(layer, name)]
        return _unstack_layer(name, part.results[i])

    small = {n: w[n] for n in SMALL_NAMES}
    small["conv_w"] = first.results[-1].transpose(1, 2, 0, 3).reshape(DEPTH, CONV_WIDTH, D_A)

    grads_full, scattered = {}, {}

    def scatter_parts(keys):
        part = _ChipScatterPart([_restack_layer(n, grads_full[(layer, n)]) for layer, n in keys])
        scattered.update({key: (part, i) for i, key in enumerate(keys)})
        return [part]

    for host, keys in SCATTER_HOSTS.items():
        _RIDERS[host] = functools.partial(scatter_parts, keys)

    loss_local, gx, gsmall = _local_step(x[0], loss_target[0], weight, small, grads_full.__setitem__)

    partial = {}

    def reduce_chips(keys):
        for layer, n in keys:
            part, i = scattered[(layer, n)]
            p = _sum_stack(f"sum_l{layer}_{n}", part.results[i])
            partial[(layer, n)] = _unpermute_in_cols(p) if n == "w_in" else p

    early = [key for keys in SCATTER_HOSTS.values() for key in keys]
    reduce_chips(early)
    small_shapes = [gsmall[n].shape for n in SMALL_NAMES]
    last_parts = scatter_parts(LAST_SCATTER) + [_SiblingSwapPart([partial[k] for k in early]),
                                                _DevGatherPart(_pack([gsmall[n] for n in SMALL_NAMES]))]
    _exchange_now("exchange_last", last_parts)
    other = dict(zip(early, last_parts[1].results))
    reduce_chips(LAST_SCATTER)
    swap_late = _SiblingSwapPart([partial[k] for k in LAST_SCATTER])
    _exchange_now("swap_last", [swap_late])
    other.update(zip(LAST_SCATTER, swap_late.results))

    total = _sum_stack("sum_small", last_parts[2].results[0])
    gsm = dict(zip(SMALL_NAMES, _unpack(total, small_shapes)))
    cw = D_A // N_CHIPS
    gsm["conv_w"] = lax.dynamic_slice_in_dim(gsm["conv_w"], chip * cw, cw, axis=2)

    grads, deltas, new_m, new_v = {}, {}, {}, {}
    for n in BIG_NAMES:
        bufs = None
        for layer in range(DEPTH):
            bufs = _adamw_layer(f"adamw_l{layer}_{n}", layer, w[n], partial[(layer, n)], other[(layer, n)], m[n], v[n],
                                bufs)
        grads[n], deltas[n], new_m[n], new_v[n] = bufs
    shapes = [w[n].shape for n in SMALL_NAMES]
    gp = _pack([gsm[n] for n in SMALL_NAMES])
    res = _adamw("adamw_small", _pack([w[n] for n in SMALL_NAMES]), gp, jnp.zeros_like(gp),
                 _pack([m[n] for n in SMALL_NAMES]), _pack([v[n] for n in SMALL_NAMES]))
    for dst, buf in zip((grads, deltas, new_m, new_v), res):
        dst.update(zip(SMALL_NAMES, _unpack(buf, shapes)))

    loss = lax.psum(loss_local, ("x", "y", "c"))
    return (loss, gx[None], *[grads[n] for n in WEIGHT_NAMES], *[deltas[n] for n in WEIGHT_NAMES],
            *[new_m[n] for n in WEIGHT_NAMES], *[new_v[n] for n in WEIGHT_NAMES])


def kernel(x, ffn1_w_gate, ffn1_w_up, ffn1_w_down, ln1_g, ln1_b, w_in, conv_w, conv_b, rg_w_a, rg_b_a, rg_w_x, rg_b_x, rg_lambda, fox_b_f, s5_a_re, s5_a_im, s5_log_dt, s5_b_re, s5_b_im, s5_c_re, s5_c_im, s5_d, s5_w_glu, mix_norm_g, w_out, ln2_g, ln2_b, ffn2_w_gate, ffn2_w_up, ffn2_w_down, ln3_g, ln3_b, loss_target, m_ffn1_w_gate, m_ffn1_w_up, m_ffn1_w_down, m_ln1_g, m_ln1_b, m_w_in, m_conv_w, m_conv_b, m_rg_w_a, m_rg_b_a, m_rg_w_x, m_rg_b_x, m_rg_lambda, m_fox_b_f, m_s5_a_re, m_s5_a_im, m_s5_log_dt, m_s5_b_re, m_s5_b_im, m_s5_c_re, m_s5_c_im, m_s5_d, m_s5_w_glu, m_mix_norm_g, m_w_out, m_ln2_g, m_ln2_b, m_ffn2_w_gate, m_ffn2_w_up, m_ffn2_w_down, m_ln3_g, m_ln3_b, v_ffn1_w_gate, v_ffn1_w_up, v_ffn1_w_down, v_ln1_g, v_ln1_b, v_w_in, v_conv_w, v_conv_b, v_rg_w_a, v_rg_b_a, v_rg_w_x, v_rg_b_x, v_rg_lambda, v_fox_b_f, v_s5_a_re, v_s5_a_im, v_s5_log_dt, v_s5_b_re, v_s5_b_im, v_s5_c_re, v_s5_c_im, v_s5_d, v_s5_w_glu, v_mix_norm_g, v_w_out, v_ln2_g, v_ln2_b, v_ffn2_w_gate, v_ffn2_w_up, v_ffn2_w_down, v_ln3_g, v_ln3_b):
    args = dict(locals())
    w = {n: args[n] for n in WEIGHT_NAMES}
    m = {n: args["m_" + n] for n in WEIGHT_NAMES}
    v = {n: args["v_" + n] for n in WEIGHT_NAMES}
    return _train_step(x, loss_target, w, m, v)
```

```python
import functools
import math

import jax
import jax.numpy as jnp
from jax import lax
from jax.experimental import pallas as pl
from jax.experimental.pallas import tpu as pltpu

f32 = jnp.float32
bf16 = jnp.bfloat16

D_MODEL = 1024
D_FF = 2816
D_A = 384
D_B = 384
D_C = 256
N_HEADS = 6
HEAD_DIM = 64
S5_GROUPS = 16
S5_GROUP = 16
S5_STATE = 64
S5_LANES = S5_GROUPS * S5_STATE
N_IN = 2 * D_A + 3 * D_B + N_HEADS + D_C
F_OFF = 5 * D_A
CU_OFF = F_OFF + 128
N_IN_P = CU_OFF + D_C
CONV_WIDTH = 4
DEPTH = 2
ALPHA = (2 * DEPTH) ** 0.25
LN_EPS = 1e-5
RMS_EPS = 1e-6
RG_C = 8.0
ATT_SCALE = HEAD_DIM ** -0.5
ADAM_LR, ADAM_B1, ADAM_B2, ADAM_EPS, ADAM_WD, ADAM_STEP = 0.001, 0.9, 0.999, 1e-08, 0.01, 10

SCAN_CHUNK = 64
ROW_TILE = 256
ATT_TILE = 256
N_CHIPS = 4
N_DEV = 8
MESH = pl.DeviceIdType.MESH

_DN = {
    "nn": (((1,), (0,)), ((), ())),
    "nt": (((1,), (1,)), ((), ())),
    "tn": (((0,), (0,)), ((), ())),
}


def _sds(shape, dtype=f32):
    return jax.ShapeDtypeStruct(shape, dtype)


def _tile(n, target):
    best = None
    for t in range(128, min(n, target) + 1, 128):
        if n % t == 0:
            best = t
    return best or n


def _row_tile(rows, target=256):
    best = None
    for t in range(16, min(rows, target) + 1, 16):
        if rows % t == 0:
            best = t
    return best or rows


def _params(*sem):
    return pltpu.CompilerParams(dimension_semantics=sem)


class _Slabs:
    def __init__(self, x):
        self.x = x


FF_SLAB = D_FF // 4
FFN_ROWS = 1024

def _mm(name, mode, dims, tiles, a_list, b_list, pairs, n_acc, epilogue, outs, extras=(), vecs=(), split_cols=False):
    m, n, k = dims
    tm, tn, tk = tiles
    nk = k // tk
    na, nb, ne, nv, no = len(a_list), len(b_list), len(extras), len(vecs), len(outs)

    def body(*refs):
        a_refs = refs[:na]
        b_refs = refs[na:na + nb]
        e_refs = refs[na + nb:na + nb + ne]
        v_refs = refs[na + nb + ne:na + nb + ne + nv]
        o_refs = refs[na + nb + ne + nv:na + nb + ne + nv + no]
        acc_refs = refs[na + nb + ne + nv + no:]
        a_vals = [r[...].astype(bf16) for r in a_refs]
        b_vals = [r[...].astype(bf16) for r in b_refs]
        products = [(ci, lax.dot_general(a_vals[ai], b_vals[bi], _DN[mode], preferred_element_type=f32))
                    for ai, bi, ci in pairs]

        def finish(accs):
            res = epilogue(accs, [e[...] for e in e_refs], [v[...] for v in v_refs])
            for o, r in zip(o_refs, res):
                o[...] = r.astype(o.dtype)

        if nk == 1:
            accs = [None] * n_acc
            for ci, prod in products:
                accs[ci] = prod if accs[ci] is None else accs[ci] + prod
            finish(accs)
            return
        kk = pl.program_id(2)

        @pl.when(kk == 0)
        def _():
            for acc in acc_refs:
                acc[...] = jnp.zeros_like(acc)

        for ci, prod in products:
            acc_refs[ci][...] += prod

        @pl.when(kk == nk - 1)
        def _():
            finish([acc[...] for acc in acc_refs])

    def a_spec(a):
        if isinstance(a, _Slabs):
            if mode == "tn":
                return pl.BlockSpec((None, tk, tm), lambda i, j, kk: (i, kk, 0))
            return pl.BlockSpec((None, tm, tk), lambda i, j, kk: (kk, i, 0))
        if mode == "tn":
            return pl.BlockSpec((tk, tm), lambda i, j, kk: (kk, i))
        return pl.BlockSpec((tm, tk), lambda i, j, kk: (i, kk))

    def b_spec(b):
        if isinstance(b, _Slabs):
            if mode == "nt":
                return pl.BlockSpec((None, tn, tk), lambda i, j, kk: (kk, j, 0))
            return pl.BlockSpec((None, tk, tn), lambda i, j, kk: (j, kk, 0))
        if mode == "nt":
            return pl.BlockSpec((tn, tk), lambda i, j, kk: (j, kk))
        return pl.BlockSpec((tk, tn), lambda i, j, kk: (kk, j))

    o_spec = pl.BlockSpec((tm, tn), lambda i, j, kk: (i, j))
    o_slab_spec = pl.BlockSpec((None, tm, tn), lambda i, j, kk: (j, i, 0))
    v_spec = pl.BlockSpec((1, tn), lambda i, j, kk: (0, j))
    if split_cols:
        out_specs = [o_slab_spec] * no
        out_shape = [_sds((n // tn, m, tn), dt) for dt in outs]
    else:
        out_specs = [o_spec] * no
        out_shape = [_sds((m, n), dt) for dt in outs]
    raw = lambda t: t.x if isinstance(t, _Slabs) else t
    res = _call(
        body,
        name=name,
        grid=(m // tm, n // tn, nk),
        in_specs=([a_spec(a) for a in a_list] + [b_spec(b) for b in b_list]
                  + [o_slab_spec if isinstance(e, _Slabs) else o_spec for e in extras] + [v_spec] * nv),
        out_specs=out_specs,
        out_shape=out_shape,
        scratch_shapes=[pltpu.VMEM((tm, tn), f32)] * (n_acc if nk > 1 else 0),
        compiler_params=_params("parallel", "parallel", "arbitrary"),
    )(*map(raw, a_list), *map(raw, b_list), *map(raw, extras), *vecs)
    return res


def _sigmoid(x):
    return 0.5 * (jnp.tanh(0.5 * x) + 1.0)


def _layer_norm_rows(r, gamma, beta):
    mu = jnp.mean(r, axis=-1, keepdims=True)
    xc = r - mu
    var = jnp.mean(xc * xc, axis=-1, keepdims=True)
    return xc * lax.rsqrt(var + LN_EPS) * gamma + beta


def _mm_plain(name, mode, a, b, dims, scale=1.0, out_dtype=f32, add=None, add_coef=1.0, tiles=None):
    m, n, k = dims
    tiles = tiles or (_tile(m, 512), _tile(n, 1024), _tile(k, 1024))

    def epilogue(accs, extras, vecs):
        r = accs[0] if scale == 1.0 else accs[0] * scale
        if extras:
            r = r + add_coef * extras[0]
        return [r]

    return _mm(name, mode, dims, tiles, [a], [b], [(0, 0, 0)], 1, epilogue, [out_dtype],
               extras=[] if add is None else [add])[0]


def _ffn_up(name, h, wg, wu):
    s = h.shape[0]

    def epilogue(accs, extras, vecs):
        g, u = accs
        return [g, u, g * _sigmoid(g) * u]

    return _mm(name, "nn", (s, D_FF, D_MODEL), (_tile(s, FFN_ROWS), FF_SLAB, D_MODEL), [h], [_Slabs(wg), _Slabs(wu)],
               [(0, 0, 0), (0, 1, 1)], 2, epilogue, [bf16, bf16, bf16], split_cols=True)


def _mm_ln(name, a, w, resid, gamma, beta, scale, k_slabs=False):
    s, k = (a.shape[1], a.shape[0] * a.shape[2]) if k_slabs else a.shape

    def epilogue(accs, extras, vecs):
        r = ALPHA * extras[0] + scale * accs[0]
        return [r, _layer_norm_rows(r, vecs[0], vecs[1])]

    return _mm(name, "nn", (s, D_MODEL, k), (_tile(s, FFN_ROWS), D_MODEL, FF_SLAB if k_slabs else _tile(k, 1024)),
               [_Slabs(a) if k_slabs else a], [w], [(0, 0, 0)], 1, epilogue, [f32, f32], extras=[resid],
               vecs=[gamma, beta])


def _ffn_dact(name, dr, wd, g, u):
    s = dr.shape[0]

    def epilogue(accs, extras, vecs):
        da = 0.5 * accs[0]
        gg, uu = extras[0].astype(f32), extras[1].astype(f32)
        sg = _sigmoid(gg)
        return [da * uu * (sg * (1.0 + gg * (1.0 - sg))), da * (gg * sg)]

    return _mm(name, "nt", (s, D_FF, D_MODEL), (_tile(s, FFN_ROWS), FF_SLAB, D_MODEL), [dr], [wd],
               [(0, 0, 0)], 1, epilogue, [bf16, bf16], extras=[_Slabs(g), _Slabs(u)], split_cols=True)


def _mm2(name, mode, dims, a0, b0, a1, b1, add=None, add_coef=1.0, separate=False, tiles=None, out_dtype=f32,
         split_cols=False):
    m, n, k = dims
    tiles = tiles or (_tile(m, 512), _tile(n, 1024), _tile(k, 1024))

    def epilogue(accs, extras, vecs):
        if separate:
            return list(accs)
        r = accs[0]
        if extras:
            r = r + add_coef * extras[0]
        return [r]

    a_list = [a0] if a1 is None else [a0, a1]
    b_list = [b0] if b1 is None else [b0, b1]
    pairs = [(0, 0, 0), (len(a_list) - 1, len(b_list) - 1, 1 if separate else 0)]
    return _mm(name, mode, dims, tiles, a_list, b_list, pairs, 2 if separate else 1, epilogue,
               [out_dtype, out_dtype] if separate else [out_dtype], extras=[] if add is None else [add],
               split_cols=split_cols)


def _row_call(name, body, s, ins, params, outs, accs):
    tm = ROW_TILE
    ins = [a if isinstance(a, tuple) else (a, a.shape[1], 0) for a in ins]
    in_specs = [pl.BlockSpec((tm, width), lambda i, cb=cb: (i, cb)) for _, width, cb in ins]
    ins = [a for a, _, _ in ins]
    in_specs += [pl.BlockSpec(p.shape, lambda i, nd=p.ndim: (0,) * nd) for p in params]
    out_specs = [pl.BlockSpec((tm, o.shape[1]), lambda i: (i, 0)) for o in outs]
    out_specs += [pl.BlockSpec(a.shape, lambda i, nd=len(a.shape): (0,) * nd) for a in accs]
    return pl.pallas_call(
        body,
        name=name,
        grid=(s // tm,),
        in_specs=in_specs,
        out_specs=out_specs,
        out_shape=list(outs) + list(accs),
        compiler_params=_params("arbitrary"),
    )(*ins, *params)


def _zero_at_first(refs):
    @pl.when(pl.program_id(0) == 0)
    def _():
        for r in refs:
            r[...] = jnp.zeros_like(r)


def _ln_bwd(name, r, dh, gamma):
    s = r.shape[0]

    def body(r_ref, dh_ref, g_ref, dr_ref, dg_ref, db_ref):
        _zero_at_first([dg_ref, db_ref])
        rr = r_ref[...]
        dy = dh_ref[...]
        mu = jnp.mean(rr, axis=-1, keepdims=True)
        xc = rr - mu
        rstd = lax.rsqrt(jnp.mean(xc * xc, axis=-1, keepdims=True) + LN_EPS)
        xhat = xc * rstd
        dxh = dy * g_ref[...]
        dr_ref[...] = rstd * (dxh - jnp.mean(dxh, axis=-1, keepdims=True)
                              - xhat * jnp.mean(dxh * xhat, axis=-1, keepdims=True))
        dg_ref[...] += jnp.sum(dy * xhat, axis=0, keepdims=True)
        db_ref[...] += jnp.sum(dy, axis=0, keepdims=True)

    return _row_call(name, body, s, [r, dh], [gamma], [_sds((s, D_MODEL))], [_sds((1, D_MODEL)), _sds((1, D_MODEL))])


def _loss_head(name, y, target):
    s = y.shape[0]

    def body(y_ref, t_ref, dy_ref, l_ref):
        _zero_at_first([l_ref])
        e = y_ref[...] - t_ref[...]
        dy_ref[...] = e / D_MODEL
        l_ref[...] += 0.5 * jnp.sum(jnp.mean(e * e, axis=-1, keepdims=True), axis=0, keepdims=True)

    return _row_call(name, body, s, [y, target], [], [_sds((s, D_MODEL))], [_sds((1, 128))])


def _expm1(x):
    series = x * (1.0 + x / 2.0 * (1.0 + x / 3.0 * (1.0 + x / 4.0 * (1.0 + x / 5.0 * (1.0 + x / 6.0 * (1.0 + x / 7.0))))))
    return jnp.where(jnp.abs(x) < 0.25, series, jnp.exp(x) - 1.0)


def _gates_fn(xa, wa, wx, ba, bx, lam, tap_a, tap_x):
    xb = xa.astype(bf16)
    r = jax.nn.sigmoid(jnp.dot(xb, wa, preferred_element_type=f32) + ba + tap_a)
    i = jax.nn.sigmoid(jnp.dot(xb, wx, preferred_element_type=f32) + bx + tap_x)
    log_a = -RG_C * r * jax.nn.softplus(-lam)
    a = jnp.exp(log_a)
    gated = jnp.sqrt(-_expm1(2.0 * log_a)) * (i * xa)
    return a, gated


def _rg_gates(name, xa, wa, wx, ba, bx, lam):
    s = xa.shape[0]

    def body(xa_ref, wa_ref, wx_ref, ba_ref, bx_ref, lam_ref, a_ref, g_ref):
        a, g = _gates_fn(xa_ref[...], wa_ref[...], wx_ref[...], ba_ref[...], bx_ref[...], lam_ref[...], 0.0, 0.0)
        a_ref[...] = a
        g_ref[...] = g

    return _row_call(name, body, s, [xa], [wa, wx, ba, bx, lam], [_sds((s, D_A)), _sds((s, D_A))], [])


def _rg_gates_bwd(name, xa, ga, h_prev, wa, wx, ba, bx, lam):
    s = xa.shape[0]

    def body(xa_ref, ga_ref, hp_ref, wa_ref, wx_ref, ba_ref, bx_ref, lam_ref,
             dxa_ref, dwa_ref, dwx_ref, dba_ref, dbx_ref, dlam_ref):
        _zero_at_first([dwa_ref, dwx_ref, dba_ref, dbx_ref, dlam_ref])
        xa_v = xa_ref[...]
        zero = jnp.zeros((xa_v.shape[0], D_A), f32)
        fn = lambda x, ba_, bx_, lam_, ta, tx: _gates_fn(x, wa_ref[...], wx_ref[...], ba_, bx_, lam_, ta, tx)
        _, vjp = jax.vjp(fn, xa_v, ba_ref[...], bx_ref[...], lam_ref[...], zero, zero)
        gav = ga_ref[...]
        dxa, dba, dbx, dlam, dta, dtx = vjp((gav * hp_ref[...], gav))
        dxa_ref[...] = dxa
        xb = xa_v.astype(bf16)
        dwa_ref[...] += lax.dot_general(xb, dta.astype(bf16), _DN["tn"], preferred_element_type=f32)
        dwx_ref[...] += lax.dot_general(xb, dtx.astype(bf16), _DN["tn"], preferred_element_type=f32)
        dba_ref[...] += dba
        dbx_ref[...] += dbx
        dlam_ref[...] += dlam

    return _row_call(name, body, s, [xa, ga, h_prev], [wa, wx, ba, bx, lam], [_sds((s, D_A))],
                     [_sds((D_A, D_A)), _sds((D_A, D_A)), _sds((1, D_A)), _sds((1, D_A)), _sds((1, D_A))])


def _rms(v, g):
    return v * lax.rsqrt(jnp.mean(v * v, axis=-1, keepdims=True) + RMS_EPS) * g


def _mix_out_fn(ag, ha, ob, hre, him, cu, d, gn, tap_y, tap_gl, wcr, wci, wglu):
    out_a = jax.nn.gelu(ag) * ha
    y = (jnp.dot(hre.astype(bf16), wcr, preferred_element_type=f32)
         + jnp.dot(him.astype(bf16), wci, preferred_element_type=f32) + d * cu + tap_y)
    y2 = jax.nn.gelu(y)
    gl = jnp.dot(y2.astype(bf16), wglu, preferred_element_type=f32) + tap_gl
    out_c = y2 * jax.nn.sigmoid(gl)
    o = jnp.concatenate([_rms(out_a, gn[:, :D_A]), _rms(ob, gn[:, D_A:D_A + D_B]), _rms(out_c, gn[:, D_A + D_B:])],
                        axis=-1)
    return o, y2


def _mix_out(name, ag, ha, ob, hre, him, cu, d, gn, wcr, wci, wglu):
    s = ha.shape[0]

    def body(ag_ref, ha_ref, ob_ref, hre_ref, him_ref, cu_ref, d_ref, gn_ref, wcr_ref, wci_ref, wglu_ref, o_ref):
        o, _ = _mix_out_fn(ag_ref[...], ha_ref[...], ob_ref[...], hre_ref[...], him_ref[...], cu_ref[...], d_ref[...],
                           gn_ref[...], 0.0, 0.0, wcr_ref[...], wci_ref[...], wglu_ref[...])
        o_ref[...] = o.astype(o_ref.dtype)

    return _row_call(name, body, s, [ag, ha, ob, hre, him, cu], [d, gn, wcr, wci, wglu], [_sds((s, D_MODEL), bf16)], [])[0]


def _mix_out_bwd(name, do, ag, ha, ob, hre, him, cu, d, gn, wcr, wci, wglu):
    s = ha.shape[0]

    def body(do_ref, ag_ref, ha_ref, ob_ref, hre_ref, him_ref, cu_ref, d_ref, gn_ref, wcr_ref, wci_ref, wglu_ref,
             dag_ref, dha_ref, dob_ref, dhre_ref, dhim_ref, dcu_ref, dwcr_ref, dwci_ref, dwglu_ref, dd_ref, dgn_ref):
        _zero_at_first([dwcr_ref, dwci_ref, dwglu_ref, dd_ref, dgn_ref])
        tm = ag_ref.shape[0]
        zero = jnp.zeros((tm, D_C), f32)
        hre_v, him_v = hre_ref[...], him_ref[...]
        fn = lambda *a: _mix_out_fn(*a, wcr_ref[...], wci_ref[...], wglu_ref[...])
        _, vjp, y2 = jax.vjp(fn, ag_ref[...], ha_ref[...], ob_ref[...], hre_v, him_v, cu_ref[...], d_ref[...],
                             gn_ref[...], zero, zero, has_aux=True)
        dag, dha, dob, dhre, dhim, dcu, dd, dgn, dy, dgl = vjp(do_ref[...])
        dag_ref[...] = dag
        dha_ref[...] = dha
        dob_ref[...] = dob
        dhre_ref[...] = dhre
        dhim_ref[...] = dhim
        dcu_ref[...] = dcu
        dyb = dy.astype(bf16)
        dwcr_ref[...] += lax.dot_general(hre_v.astype(bf16), dyb, _DN["tn"], preferred_element_type=f32)
        dwci_ref[...] += lax.dot_general(him_v.astype(bf16), dyb, _DN["tn"], preferred_element_type=f32)
        dwglu_ref[...] += lax.dot_general(y2.astype(bf16), dgl.astype(bf16), _DN["tn"], preferred_element_type=f32)
        dd_ref[...] += dd
        dgn_ref[...] += dgn

    outs = [_sds((s, D_A)), _sds((s, D_A)), _sds((s, D_B)), _sds((s, S5_LANES)), _sds((s, S5_LANES)), _sds((s, D_C))]
    accs = [_sds((S5_LANES, D_C)), _sds((S5_LANES, D_C)), _sds((D_C, D_C)), _sds((1, D_C)), _sds((1, D_MODEL))]
    return _row_call(name, body, s, [do, ag, ha, ob, hre, him, cu], [d, gn, wcr, wci, wglu], outs, accs)


def _log_f(name, f, bf):
    s = f[0].shape[0]

    def body(f_ref, b_ref, o_ref):
        o_ref[...] = jax.nn.log_sigmoid(f_ref[...] + b_ref[...])

    return _row_call(name, body, s, [f], [bf], [_sds((s, 128))], [])[0]


def _log_f_bwd(name, dlf, f, bf):
    s = dlf.shape[0]

    def body(dl_ref, f_ref, b_ref, df_ref, db_ref):
        _zero_at_first([db_ref])
        df = dl_ref[...] * jax.nn.sigmoid(-(f_ref[...] + b_ref[...]))
        df_ref[...] = df
        db_ref[...] += jnp.sum(df, axis=0, keepdims=True)

    return _row_call(name, body, s, [dlf, f], [bf], [_sds((s, 128))], [_sds((1, 128))])


def _s5_decay_grad(name, hp_re, hp_im, g_re, g_im):
    s = g_re.shape[0]

    def body(hr_ref, hi_ref, gr_ref, gi_ref, dr_ref, di_ref):
        _zero_at_first([dr_ref, di_ref])
        hr, hi, gr, gi = hr_ref[...], hi_ref[...], gr_ref[...], gi_ref[...]
        dr_ref[...] += jnp.sum(hr * gr + hi * gi, axis=0, keepdims=True)
        di_ref[...] += jnp.sum(hr * gi - hi * gr, axis=0, keepdims=True)

    return _row_call(name, body, s, [hp_re, hp_im, g_re, g_im], [], [], [_sds((1, S5_LANES)), _sds((1, S5_LANES))])


def _conv_fwd(name, ax, w, b):
    s = ax.shape[0]
    tm = ROW_TILE

    def body(x_ref, halo_ref, w_ref, b_ref, o_ref):
        i = pl.program_id(0)
        x = x_ref[...]
        halo = jnp.where(i == 0, 0.0, halo_ref[...])
        ext = jnp.concatenate([halo, x], axis=0)
        acc = b_ref[...] + w_ref[3:4, :] * x
        for k in range(CONV_WIDTH - 1):
            acc = acc + w_ref[k:k + 1, :] * pltpu.roll(ext, CONV_WIDTH - 1 - k, 0)[8:, :]
        o_ref[...] = acc

    return pl.pallas_call(
        body,
        name=name,
        grid=(s // tm,),
        in_specs=[pl.BlockSpec((tm, D_A), lambda i: (i, 0)),
                  pl.BlockSpec((8, D_A), lambda i: (jnp.maximum(i * (tm // 8) - 1, 0), 0)),
                  pl.BlockSpec((CONV_WIDTH, D_A), lambda i: (0, 0)),
                  pl.BlockSpec((1, D_A), lambda i: (0, 0))],
        out_specs=pl.BlockSpec((tm, D_A), lambda i: (i, 0)),
        out_shape=_sds((s, D_A)),
        compiler_params=_params("arbitrary"),
    )(ax, ax, w, b)


def _conv_bwd(name, dxa, ax, w):
    s = ax.shape[0]
    tm = ROW_TILE
    nblk = s // tm

    def body(dx_ref, dnext_ref, x_ref, halo_ref, w_ref, dax_ref, dw_ref):
        i = pl.program_id(0)
        _zero_at_first([dw_ref])
        dx = dx_ref[...]
        dnext = jnp.where(i == nblk - 1, 0.0, dnext_ref[...])
        dext = jnp.concatenate([dx, dnext], axis=0)
        x = x_ref[...]
        halo = jnp.where(i == 0, 0.0, halo_ref[...])
        ext = jnp.concatenate([halo, x], axis=0)
        acc = w_ref[3:4, :] * dx
        dw_ref[3:4, :] += jnp.sum(dx * x, axis=0, keepdims=True)
        for k in range(CONV_WIDTH - 1):
            sh = CONV_WIDTH - 1 - k
            acc = acc + w_ref[k:k + 1, :] * pltpu.roll(dext, tm + 8 - sh, 0)[:tm, :]
            dw_ref[k:k + 1, :] += jnp.sum(dx * pltpu.roll(ext, sh, 0)[8:, :], axis=0, keepdims=True)
        dw_ref[4:5, :] += jnp.sum(dx, axis=0, keepdims=True)
        dax_ref[...] = acc

    return pl.pallas_call(
        body,
        name=name,
        grid=(nblk,),
        in_specs=[pl.BlockSpec((tm, D_A), lambda i: (i, 0)),
                  pl.BlockSpec((8, D_A), lambda i: (jnp.minimum((i + 1) * (tm // 8), s // 8 - 1), 0)),
                  pl.BlockSpec((tm, D_A), lambda i: (i, 0)),
                  pl.BlockSpec((8, D_A), lambda i: (jnp.maximum(i * (tm // 8) - 1, 0), 0)),
                  pl.BlockSpec((CONV_WIDTH, D_A), lambda i: (0, 0))],
        out_specs=[pl.BlockSpec((tm, D_A), lambda i: (i, 0)), pl.BlockSpec((8, D_A), lambda i: (0, 0))],
        out_shape=[_sds((s, D_A)), _sds((8, D_A))],
        compiler_params=_params("arbitrary"),
    )(dxa, dxa, ax, ax, w)


SCAN_ROWS = 512


def _row_in_tile(shape):
    return lax.broadcasted_iota(jnp.int32, shape, 0) % 8


def _lin_scan(name, a, b, reverse):
    s, c = a.shape
    t = min(SCAN_ROWS, s)
    nb = s // t

    def body(a_ref, b_ref, h_ref, p_ref, carry_ref):
        @pl.when(pl.program_id(0) == 0)
        def _():
            carry_ref[...] = jnp.zeros_like(carry_ref)

        row = _row_in_tile((t, c))
        p = a_ref[...]
        h = b_ref[...]
        for d in (1, 2, 4):
            keep = (row < 8 - d) if reverse else (row >= d)
            shift = (t - d) if reverse else d
            h = h + jnp.where(keep, p * pltpu.roll(h, shift, 0), 0.0)
            p = jnp.where(keep, p * pltpu.roll(p, shift, 0), p)
        h_ref[...] = h
        p_ref[...] = p
        edge = 0 if reverse else 7

        def tile(k, carry):
            kk = (t // 8 - 1 - k) if reverse else k
            r0 = pl.multiple_of(kk * 8, 8)
            hh = h_ref[pl.ds(r0, 8), :] + p_ref[pl.ds(r0, 8), :] * carry
            h_ref[pl.ds(r0, 8), :] = hh
            return jnp.broadcast_to(hh[edge:edge + 1, :], (8, c))

        carry_ref[...] = lax.fori_loop(0, t // 8, tile, carry_ref[...])

    spec = pl.BlockSpec((t, c), (lambda i: (nb - 1 - i, 0)) if reverse else (lambda i: (i, 0)))
    (out,) = _call(
        body,
        name=name,
        grid=(nb,),
        in_specs=[spec, spec],
        out_specs=[spec],
        out_shape=[_sds((s, c))],
        scratch_shapes=[pltpu.VMEM((t, c), f32), pltpu.VMEM((8, c), f32)],
        compiler_params=_params("arbitrary"),
    )(a, b)
    return out


def _s5_scan(name, b_re, b_im, a_re, a_im, reverse):
    s, c = b_re.shape
    t = min(SCAN_ROWS, s)
    nb = s // t

    def body(br_ref, bi_ref, ar_ref, ai_ref, hr_ref, hi_ref, cr_ref, ci_ref):
        @pl.when(pl.program_id(0) == 0)
        def _():
            cr_ref[...] = jnp.zeros_like(cr_ref)
            ci_ref[...] = jnp.zeros_like(ci_ref)

        ar1, ai1 = ar_ref[...], ai_ref[...]
        pows = [(ar1, ai1)]
        for _ in range(7):
            pr, pi = pows[-1]
            pows.append((pr * ar1 - pi * ai1, pr * ai1 + pi * ar1))
        row8 = lax.broadcasted_iota(jnp.int32, (8, c), 0)
        wr = jnp.zeros((8, c), f32)
        wi = jnp.zeros((8, c), f32)
        for r in range(8):
            pr, pi = pows[(7 - r) if reverse else r]
            wr = jnp.where(row8 == r, pr, wr)
            wi = jnp.where(row8 == r, pi, wi)
        row = _row_in_tile((t, c))
        hr = br_ref[...]
        hi = bi_ref[...]
        for d in (1, 2, 4):
            keep = (row < 8 - d) if reverse else (row >= d)
            shift = (t - d) if reverse else d
            pr, pi = pows[d - 1]
            cr = jnp.where(keep, pr, 0.0)
            ci = jnp.where(keep, pi, 0.0)
            sr = pltpu.roll(hr, shift, 0)
            si = pltpu.roll(hi, shift, 0)
            hr, hi = hr + cr * sr - ci * si, hi + cr * si + ci * sr
        hr_ref[...] = hr
        hi_ref[...] = hi
        edge = 0 if reverse else 7

        def tile(k, carry):
            car_r, car_i = carry
            kk = (t // 8 - 1 - k) if reverse else k
            r0 = pl.multiple_of(kk * 8, 8)
            xr = hr_ref[pl.ds(r0, 8), :] + wr * car_r - wi * car_i
            xi = hi_ref[pl.ds(r0, 8), :] + wr * car_i + wi * car_r
            hr_ref[pl.ds(r0, 8), :] = xr
            hi_ref[pl.ds(r0, 8), :] = xi
            return (jnp.broadcast_to(xr[edge:edge + 1, :], (8, c)), jnp.broadcast_to(xi[edge:edge + 1, :], (8, c)))

        car_r, car_i = lax.fori_loop(0, t // 8, tile, (cr_ref[...], ci_ref[...]))
        cr_ref[...] = car_r
        ci_ref[...] = car_i

    spec = pl.BlockSpec((t, c), (lambda i: (nb - 1 - i, 0)) if reverse else (lambda i: (i, 0)))
    vspec = pl.BlockSpec((1, c), lambda i: (0, 0))
    hr, hi = _call(
        body,
        name=name,
        grid=(nb,),
        in_specs=[spec, spec, vspec, vspec],
        out_specs=[spec, spec],
        out_shape=[_sds((s, c)), _sds((s, c))],
        scratch_shapes=[pltpu.VMEM((8, c), f32), pltpu.VMEM((8, c), f32)],
        compiler_params=_params("arbitrary"),
    )(b_re, b_im, a_re, a_im)
    return hr, hi


def _causal_mask(t):
    row = lax.broadcasted_iota(jnp.int32, (t, t), 0)
    col = lax.broadcasted_iota(jnp.int32, (t, t), 1)
    return row >= col


def _attn_fwd(name, q, k, v, cq, ck):
    h, s, dh = q.shape
    t = ATT_TILE
    nq = s // t

    def body(q_ref, k_ref, v_ref, cq_ref, ck_ref, o_ref, lse_ref):
        qi = pl.program_id(1)
        qb = q_ref[...].astype(bf16)
        cqv = cq_ref[...]

        def block(kb, carry, masked):
            m, l, acc = carry
            ks = pl.multiple_of(kb * t, t)
            kk = k_ref[pl.ds(ks, t), :].astype(bf16)
            vv = v_ref[pl.ds(ks, t), :].astype(bf16)
            sc = lax.dot_general(qb, kk, _DN["nt"], preferred_element_type=f32) * ATT_SCALE + (cqv - ck_ref[kb])
            if masked:
                sc = jnp.where(_causal_mask(t), sc, -jnp.inf)
            mn = jnp.maximum(m, jnp.max(sc, axis=1, keepdims=True))
            p = jnp.exp(sc - mn)
            al = jnp.exp(m - mn)
            l = al * l + jnp.sum(p, axis=1, keepdims=True)
            acc = al * acc + jnp.dot(p.astype(bf16), vv, preferred_element_type=f32)
            return mn, l, acc

        init = (jnp.full((t, 1), -jnp.inf, f32), jnp.zeros((t, 1), f32), jnp.zeros((t, dh), f32))
        carry = lax.fori_loop(0, qi, lambda kb, c: block(kb, c, False), init)
        m, l, acc = block(qi, carry, True)
        o_ref[...] = acc / l
        lse_ref[...] = m + jnp.log(l)

    return pl.pallas_call(
        body,
        name=name,
        grid=(h, nq),
        in_specs=[pl.BlockSpec((None, t, dh), lambda hh, i: (hh, i, 0)),
                  pl.BlockSpec((None, s, dh), lambda hh, i: (hh, 0, 0)),
                  pl.BlockSpec((None, s, dh), lambda hh, i: (hh, 0, 0)),
                  pl.BlockSpec((None, t, 1), lambda hh, i: (hh, i, 0)),
                  pl.BlockSpec((None, nq, 1, t), lambda hh, i: (hh, 0, 0, 0))],
        out_specs=[pl.BlockSpec((None, t, dh), lambda hh, i: (hh, i, 0)),
                   pl.BlockSpec((None, t, 1), lambda hh, i: (hh, i, 0))],
        out_shape=[_sds((h, s, dh)), _sds((h, s, 1))],
        compiler_params=_params("parallel", "arbitrary"),
    )(q, k, v, cq, ck)


def _attn_bwd_dq(name, q, k, v, cq, ck, o, do, lse):
    h, s, dh = q.shape
    t = ATT_TILE
    nq = s // t

    def body(q_ref, k_ref, v_ref, cq_ref, ck_ref, o_ref, do_ref, lse_ref, dq_ref, dl_ref):
        qi = pl.program_id(1)
        qb = q_ref[...].astype(bf16)
        cqv = cq_ref[...]
        dov = do_ref[...]
        dob = dov.astype(bf16)
        delta = jnp.sum(dov * o_ref[...], axis=1, keepdims=True)
        lse_v = lse_ref[...]

        def block(kb, carry, masked):
            dq, psum = carry
            ks = pl.multiple_of(kb * t, t)
            kk = k_ref[pl.ds(ks, t), :].astype(bf16)
            vv = v_ref[pl.ds(ks, t), :].astype(bf16)
            sc = lax.dot_general(qb, kk, _DN["nt"], preferred_element_type=f32) * ATT_SCALE + (cqv - ck_ref[kb])
            p = jnp.exp(sc - lse_v)
            if masked:
                p = jnp.where(_causal_mask(t), p, 0.0)
            dp = lax.dot_general(dob, vv, _DN["nt"], preferred_element_type=f32)
            ds = p * (dp - delta)
            return (dq + jnp.dot(ds.astype(bf16), kk, preferred_element_type=f32),
                    psum + jnp.sum(p * dp, axis=1, keepdims=True))

        carry = lax.fori_loop(0, qi, lambda kb, c: block(kb, c, False), (jnp.zeros((t, dh), f32), jnp.zeros((t, 1), f32)))
        dq, psum = block(qi, carry, True)
        dq_ref[...] = dq * ATT_SCALE
        dl_ref[...] = psum

    qspec = pl.BlockSpec((None, t, dh), lambda hh, i: (hh, i, 0))
    fspec = pl.BlockSpec((None, s, dh), lambda hh, i: (hh, 0, 0))
    cspec = pl.BlockSpec((None, t, 1), lambda hh, i: (hh, i, 0))
    return pl.pallas_call(
        body,
        name=name,
        grid=(h, nq),
        in_specs=[qspec, fspec, fspec, cspec, pl.BlockSpec((None, nq, 1, t), lambda hh, i: (hh, 0, 0, 0)),
                  qspec, qspec, cspec],
        out_specs=[qspec, cspec],
        out_shape=[_sds((h, s, dh)), _sds((h, s, 1))],
        compiler_params=_params("parallel", "arbitrary"),
    )(q, k, v, cq, ck, o, do, lse)


def _attn_bwd_dkv(name, q, k, v, cq, ck, do, lse, delta):
    h, s, dh = q.shape
    t = ATT_TILE
    nq = s // t

    def body(q_ref, k_ref, v_ref, cq_ref, ck_ref, do_ref, lse_ref, dl_ref, dk_ref, dv_ref, dck_ref):
        kj = pl.program_id(1)
        kk = k_ref[...].astype(bf16)
        vv = v_ref[...].astype(bf16)
        ckv = ck_ref[...]

        def block(qi, carry, masked):
            dk, dv, dcs = carry
            qs = pl.multiple_of(qi * t, t)
            qq = q_ref[pl.ds(qs, t), :].astype(bf16)
            dob = do_ref[pl.ds(qs, t), :].astype(bf16)
            sc = (lax.dot_general(qq, kk, _DN["nt"], preferred_element_type=f32) * ATT_SCALE
                  + (cq_ref[pl.ds(qs, t), :] - ckv))
            p = jnp.exp(sc - lse_ref[pl.ds(qs, t), :])
            if masked:
                p = jnp.where(_causal_mask(t), p, 0.0)
            dv = dv + lax.dot_general(p.astype(bf16), dob, _DN["tn"], preferred_element_type=f32)
            dp = lax.dot_general(dob, vv, _DN["nt"], preferred_element_type=f32)
            ds = p * (dp - dl_ref[pl.ds(qs, t), :])
            dk = dk + lax.dot_general(ds.astype(bf16), qq, _DN["tn"], preferred_element_type=f32)
            return dk, dv, dcs + jnp.sum(ds, axis=0, keepdims=True)

        init = (jnp.zeros((t, dh), f32), jnp.zeros((t, dh), f32), jnp.zeros((1, t), f32))
        carry = block(kj, init, True)
        dk, dv, dcs = lax.fori_loop(kj + 1, nq, lambda qi, c: block(qi, c, False), carry)
        dk_ref[...] = dk * ATT_SCALE
        dv_ref[...] = dv
        dck_ref[...] = -dcs

    kspec = pl.BlockSpec((None, t, dh), lambda hh, j: (hh, j, 0))
    fspec = pl.BlockSpec((None, s, dh), lambda hh, j: (hh, 0, 0))
    fcol = pl.BlockSpec((None, s, 1), lambda hh, j: (hh, 0, 0))
    crow = pl.BlockSpec((None, None, 1, t), lambda hh, j: (hh, j, 0, 0))
    return pl.pallas_call(
        body,
        name=name,
        grid=(h, nq),
        in_specs=[fspec, kspec, kspec, fcol, crow, fspec, fcol, fcol],
        out_specs=[kspec, kspec, crow],
        out_shape=[_sds((h, s, dh)), _sds((h, s, dh)), _sds((h, nq, 1, t))],
        compiler_params=_params("parallel", "arbitrary"),
    )(q, k, v, cq, ck, do, lse, delta)


ATT_FEAT = 128
ATT_TQ = 1024
ATT_TK = 256


def _att_tiles(s):
    tq = min(ATT_TQ, s)
    return tq, ATT_TK, tq // ATT_TK


def _keys_le_queries(tk, tq, k0, q0):
    row = lax.broadcasted_iota(jnp.int32, (tk, tq), 0) + k0
    col = lax.broadcasted_iota(jnp.int32, (tk, tq), 1) + q0
    return row <= col


def _attn_fwd_t(name, qt, k_aug, vt):
    h, s, _ = k_aug.shape
    tq, tk, ratio = _att_tiles(s)

    def body(qt_ref, k_ref, vt_ref, o_ref, lse_ref):
        qi = pl.program_id(1)
        qt = qt_ref[...]

        def block(kb, carry, masked):
            m, l, acc = carry
            ks = pl.multiple_of(kb * tk, tk)
            st = jnp.dot(k_ref[pl.ds(ks, tk), :], qt, preferred_element_type=f32)
            if masked:
                st = jnp.where(_keys_le_queries(tk, tq, ks, qi * tq), st, -jnp.inf)
            mn = jnp.maximum(m, jnp.max(st, axis=0, keepdims=True))
            p = jnp.exp(st - mn)
            al = jnp.exp(m - mn)
            l = al * l + jnp.sum(p, axis=0, keepdims=True)
            acc = al * acc + jnp.dot(vt_ref[kb], p.astype(bf16), preferred_element_type=f32)
            return mn, l, acc

        init = (jnp.full((1, tq), -jnp.inf, f32), jnp.zeros((1, tq), f32), jnp.zeros((HEAD_DIM, tq), f32))
        first = lax.fori_loop(0, qi * ratio, lambda kb, c: block(kb, c, False), init)
        m, l, acc = lax.fori_loop(qi * ratio, (qi + 1) * ratio, lambda kb, c: block(kb, c, True), first)
        o_ref[...] = acc / l
        lse_ref[...] = m + jnp.log(l)

    return _call(
        body,
        name=name,
        grid=(h, s // tq),
        in_specs=[pl.BlockSpec((None, None, ATT_FEAT, tq), lambda hh, i: (hh, i, 0, 0)),
                  pl.BlockSpec((None, s, ATT_FEAT), lambda hh, i: (hh, 0, 0)),
                  pl.BlockSpec((None, s // tk, HEAD_DIM, tk), lambda hh, i: (hh, 0, 0, 0))],
        out_specs=[pl.BlockSpec((None, HEAD_DIM, tq), lambda hh, i: (hh, 0, i)),
                   pl.BlockSpec((None, 1, tq), lambda hh, i: (hh, 0, i))],
        out_shape=[_sds((h, HEAD_DIM, s)), _sds((h, 1, s))],
        compiler_params=_params("parallel", "arbitrary"),
    )(qt, k_aug, vt)


def _attn_bwd_dq_t(name, qt, k_aug, v, kt, ot, dot_, lse):
    h, s, _ = k_aug.shape
    tq, tk, ratio = _att_tiles(s)

    def body(qt_ref, k_ref, v_ref, kt_ref, o_ref, do_ref, lse_ref, dq_ref, dl_ref):
        qi = pl.program_id(1)
        qt = qt_ref[...]
        dob = do_ref[...]
        delta = jnp.sum(dob.astype(f32) * o_ref[...], axis=0, keepdims=True)
        lse_v = lse_ref[...]

        def block(kb, carry, masked):
            dq, psum = carry
            ks = pl.multiple_of(kb * tk, tk)
            st = jnp.dot(k_ref[pl.ds(ks, tk), :], qt, preferred_element_type=f32)
            p = jnp.exp(st - lse_v)
            if masked:
                p = jnp.where(_keys_le_queries(tk, tq, ks, qi * tq), p, 0.0)
            dp = jnp.dot(v_ref[pl.ds(ks, tk), :], dob, preferred_element_type=f32)
            ds = p * (dp - delta)
            return (dq + jnp.dot(kt_ref[kb], ds.astype(bf16), preferred_element_type=f32),
                    psum + jnp.sum(p * dp, axis=0, keepdims=True))

        carry = lax.fori_loop(0, qi * ratio, lambda kb, c: block(kb, c, False),
                              (jnp.zeros((HEAD_DIM, tq), f32), jnp.zeros((1, tq), f32)))
        dq, psum = lax.fori_loop(qi * ratio, (qi + 1) * ratio, lambda kb, c: block(kb, c, True), carry)
        dq_ref[...] = dq * ATT_SCALE
        dl_ref[...] = psum

    qspec = pl.BlockSpec((None, HEAD_DIM, tq), lambda hh, i: (hh, 0, i))
    rspec = pl.BlockSpec((None, 1, tq), lambda hh, i: (hh, 0, i))
    return _call(
        body,
        name=name,
        grid=(h, s // tq),
        in_specs=[pl.BlockSpec((None, None, ATT_FEAT, tq), lambda hh, i: (hh, i, 0, 0)),
                  pl.BlockSpec((None, s, ATT_FEAT), lambda hh, i: (hh, 0, 0)),
                  pl.BlockSpec((None, s, HEAD_DIM), lambda hh, i: (hh, 0, 0)),
                  pl.BlockSpec((None, s // tk, HEAD_DIM, tk), lambda hh, i: (hh, 0, 0, 0)),
                  qspec, pl.BlockSpec((None, None, HEAD_DIM, tq), lambda hh, i: (hh, i, 0, 0)), rspec],
        out_specs=[qspec, rspec],
        out_shape=[_sds((h, HEAD_DIM, s)), _sds((h, 1, s))],
        compiler_params=_params("parallel", "arbitrary"),
    )(qt, k_aug, v, kt, ot, dot_, lse)


def _attn_bwd_dkv_t(name, qt_blocks, k_aug, v, qh, do, dot_blocks, lse, delta):
    h, s, _ = k_aug.shape
    tq, tk, ratio = _att_tiles(s)
    nq = s // tq

    def body(qt_ref, k_ref, v_ref, q_ref, do_ref, dot_ref, lse_ref, dl_ref, dk_ref, dv_ref, dck_ref, dsum_ref):
        kj = pl.program_id(1)
        kk = k_ref[...]
        vv = v_ref[...]
        dsum_ref[...] = jnp.zeros_like(dsum_ref)

        def block(qi, carry, masked):
            dk, dv = carry
            qs = pl.multiple_of(qi * tq, tq)
            st = jnp.dot(kk, qt_ref[qi], preferred_element_type=f32)
            p = jnp.exp(st - lse_ref[qi])
            if masked:
                p = jnp.where(_keys_le_queries(tk, tq, kj * tk, qs), p, 0.0)
            dv = dv + jnp.dot(p.astype(bf16), do_ref[pl.ds(qs, tq), :], preferred_element_type=f32)
            dp = jnp.dot(vv, dot_ref[qi], preferred_element_type=f32)
            ds = p * (dp - dl_ref[qi])
            dsum_ref[...] += ds
            dk = dk + jnp.dot(ds.astype(bf16), q_ref[pl.ds(qs, tq), :], preferred_element_type=f32)
            return dk, dv

        first = kj // ratio
        carry = block(first, (jnp.zeros((tk, HEAD_DIM), f32), jnp.zeros((tk, HEAD_DIM), f32)), True)
        dk, dv = lax.fori_loop(first + 1, nq, lambda qi, c: block(qi, c, False), carry)
        dk_ref[...] = dk
        dv_ref[...] = dv
        dck_ref[...] = -jnp.sum(dsum_ref[...], axis=1, keepdims=True)

    full = lambda shape: pl.BlockSpec((None,) + shape, lambda hh, j: (hh,) + (0,) * len(shape))
    kspec = pl.BlockSpec((None, tk, HEAD_DIM), lambda hh, j: (hh, j, 0))
    return _call(
        body,
        name=name,
        grid=(h, s // tk),
        in_specs=[full((nq, ATT_FEAT, tq)),
                  pl.BlockSpec((None, tk, ATT_FEAT), lambda hh, j: (hh, j, 0)),
                  kspec, full((s, HEAD_DIM)), full((s, HEAD_DIM)), full((nq, HEAD_DIM, tq)),
                  full((nq, 1, tq)), full((nq, 1, tq))],
        out_specs=[kspec, kspec, pl.BlockSpec((None, tk, 1), lambda hh, j: (hh, j, 0))],
        out_shape=[_sds((h, s, HEAD_DIM)), _sds((h, s, HEAD_DIM)), _sds((h, s, 1))],
        scratch_shapes=[pltpu.VMEM((tk, tq), f32)],
        compiler_params=_params("parallel", "arbitrary"),
    )(qt_blocks, k_aug, v, qh, do, dot_blocks, lse, delta)


C_LANES = 128


def _selections():
    h = jnp.arange(N_HEADS)[:, None, None]
    row = jnp.arange(D_B + 3 * C_LANES)[None, :, None]
    col = jnp.arange(ATT_FEAT)[None, None, :]
    head_col = (row < D_B) & (row // HEAD_DIM == h) & (col == row % HEAD_DIM)

    def c_part(p, lane0):
        return (row == D_B + p * C_LANES + h) & (col == lane0 + p)

    c_q = c_part(0, HEAD_DIM) | c_part(1, HEAD_DIM) | c_part(2, HEAD_DIM)
    c_k = c_part(0, HEAD_DIM + 3) | c_part(1, HEAD_DIM + 3) | c_part(2, HEAD_DIM + 3)
    sel_q = (head_col | c_q).astype(bf16)
    sel_k = head_col.astype(bf16) - c_k.astype(bf16)
    sel_h = head_col[:, :D_B, :HEAD_DIM].astype(bf16)
    lane = jnp.arange(ATT_FEAT)
    ones_q = ((lane >= HEAD_DIM + 3) & (lane < HEAD_DIM + 6)).astype(f32)
    ones_k = ((lane >= HEAD_DIM) & (lane < HEAD_DIM + 3)).astype(f32)
    return dict(sel_qt=sel_q.transpose(0, 2, 1), sel_k=sel_k, sel_h=sel_h, sel_ht=sel_h.transpose(0, 2, 1),
                ones_q=ones_q.reshape(ATT_FEAT, 1), ones_k=ones_k.reshape(1, ATT_FEAT))


def _attn_prep(name, z, c, sel):
    s = z.shape[0]
    tq, tk, ratio = _att_tiles(s)

    def body(q_ref, k_ref, v_ref, c_ref, sqt_ref, sk_ref, sh_ref, sht_ref, oq_ref, ok_ref,
             qt_out, ka_out, kt_out, vt_out, v_out, qh_out):
        cv = c_ref[...]
        hi = cv.astype(bf16)
        r1 = cv - hi.astype(f32)
        mid = r1.astype(bf16)
        lo = (r1 - mid.astype(f32)).astype(bf16)
        qs = (q_ref[...] * ATT_SCALE).astype(bf16)
        kb = k_ref[...].astype(bf16)
        vb = v_ref[...].astype(bf16)
        xq = jnp.concatenate([qs, hi, mid, lo], axis=-1)
        xk = jnp.concatenate([kb, hi, mid, lo], axis=-1)
        for h in range(N_HEADS):
            qt = lax.dot_general(sqt_ref[h], xq, _DN["nt"], preferred_element_type=f32) + oq_ref[...]
            qt_out[h, 0] = qt.astype(bf16)
            ka_out[h] = (jnp.dot(xk, sk_ref[h], preferred_element_type=f32) + ok_ref[...]).astype(bf16)
            kt = lax.dot_general(sht_ref[h], kb, _DN["nt"], preferred_element_type=f32).astype(bf16)
            vt = lax.dot_general(sht_ref[h], vb, _DN["nt"], preferred_element_type=f32).astype(bf16)
            for j in range(ratio):
                kt_out[h, j] = kt[:, j * tk:(j + 1) * tk]
                vt_out[h, j] = vt[:, j * tk:(j + 1) * tk]
            v_out[h] = jnp.dot(vb, sh_ref[h], preferred_element_type=f32).astype(bf16)
            qh_out[h] = jnp.dot(qs, sh_ref[h], preferred_element_type=f32).astype(bf16)

    whole = lambda a: pl.BlockSpec(a.shape, lambda i, nd=a.ndim: (0,) * nd)
    consts = [sel["sel_qt"], sel["sel_k"], sel["sel_h"], sel["sel_ht"], sel["ones_q"], sel["ones_k"]]
    return pl.pallas_call(
        body,
        name=name,
        grid=(s // tq,),
        in_specs=[pl.BlockSpec((tq, D_B), lambda i: (i, 2)), pl.BlockSpec((tq, D_B), lambda i: (i, 3)),
                  pl.BlockSpec((tq, D_B), lambda i: (i, 4)), pl.BlockSpec((tq, C_LANES), lambda i: (i, 0))]
        + [whole(a) for a in consts],
        out_specs=[pl.BlockSpec((N_HEADS, 1, ATT_FEAT, tq), lambda i: (0, i, 0, 0)),
                   pl.BlockSpec((N_HEADS, tq, ATT_FEAT), lambda i: (0, i, 0)),
                   pl.BlockSpec((N_HEADS, ratio, HEAD_DIM, tk), lambda i: (0, i, 0, 0)),
                   pl.BlockSpec((N_HEADS, ratio, HEAD_DIM, tk), lambda i: (0, i, 0, 0)),
                   pl.BlockSpec((N_HEADS, tq, HEAD_DIM), lambda i: (0, i, 0)),
                   pl.BlockSpec((N_HEADS, tq, HEAD_DIM), lambda i: (0, i, 0))],
        out_shape=[_sds((N_HEADS, s // tq, ATT_FEAT, tq), bf16), _sds((N_HEADS, s, ATT_FEAT), bf16),
                   _sds((N_HEADS, s // tk, HEAD_DIM, tk), bf16), _sds((N_HEADS, s // tk, HEAD_DIM, tk), bf16),
                   _sds((N_HEADS, s, HEAD_DIM), bf16), _sds((N_HEADS, s, HEAD_DIM), bf16)],
        compiler_params=_params("parallel"),
    )(z, z, z, c, *consts)


def _attn_do_prep(name, dob, sel):
    s = dob.shape[0]
    tq = _att_tiles(s)[0]

    def body(do_ref, sh_ref, sht_ref, dot_out, do_out):
        db = do_ref[...].astype(bf16)
        for h in range(N_HEADS):
            dot_out[h, 0] = lax.dot_general(sht_ref[h], db, _DN["nt"], preferred_element_type=f32).astype(bf16)
            do_out[h] = jnp.dot(db, sh_ref[h], preferred_element_type=f32).astype(bf16)

    whole = lambda a: pl.BlockSpec(a.shape, lambda i, nd=a.ndim: (0,) * nd)
    return pl.pallas_call(
        body,
        name=name,
        grid=(s // tq,),
        in_specs=[pl.BlockSpec((tq, D_B), lambda i: (i, 0)), whole(sel["sel_h"]), whole(sel["sel_ht"])],
        out_specs=[pl.BlockSpec((N_HEADS, 1, HEAD_DIM, tq), lambda i: (0, i, 0, 0)),
                   pl.BlockSpec((N_HEADS, tq, HEAD_DIM), lambda i: (0, i, 0))],
        out_shape=[_sds((N_HEADS, s // tq, HEAD_DIM, tq), bf16), _sds((N_HEADS, s, HEAD_DIM), bf16)],
        compiler_params=_params("parallel"),
    )(dob, sel["sel_h"], sel["sel_ht"])


def _dz_assemble(name, dax, dag, dqt, dkh, dvh, df, dcu, sel):
    s = dax.shape[0]
    tm = _tile(s, 512)

    def body(dax_ref, dag_ref, dqt_ref, dk_ref, dv_ref, df_ref, dcu_ref, sht_ref, o_ref):
        dq = jnp.zeros((tm, D_B), f32)
        dk = jnp.zeros((tm, D_B), f32)
        dv = jnp.zeros((tm, D_B), f32)
        for h in range(N_HEADS):
            place = sht_ref[h]
            dq = dq + lax.dot_general(dqt_ref[h].astype(bf16), place, _DN["tn"], preferred_element_type=f32)
            dk = dk + jnp.dot(dk_ref[h].astype(bf16), place, preferred_element_type=f32)
            dv = dv + jnp.dot(dv_ref[h].astype(bf16), place, preferred_element_type=f32)
        pieces = [dax_ref[...], dag_ref[...], dq, dk, dv, df_ref[...], dcu_ref[...]]
        off = 0
        for p in pieces:
            o_ref[:, off:off + p.shape[1]] = p.astype(bf16)
            off += p.shape[1]

    rows = lambda c_: pl.BlockSpec((tm, c_), lambda i: (i, 0))
    heads = pl.BlockSpec((N_HEADS, tm, HEAD_DIM), lambda i: (0, i, 0))
    return pl.pallas_call(
        body,
        name=name,
        grid=(s // tm,),
        in_specs=[rows(D_A), rows(D_A), pl.BlockSpec((N_HEADS, HEAD_DIM, tm), lambda i: (0, 0, i)), heads, heads,
                  rows(128), rows(D_C), pl.BlockSpec(sel["sel_ht"].shape, lambda i: (0, 0, 0))],
        out_specs=rows(N_IN_P),
        out_shape=_sds((s, N_IN_P), bf16),
        compiler_params=_params("parallel"),
    )(dax, dag, dqt, dkh, dvh, df, dcu, sel["sel_ht"])


def _s5_disc_fn(are, aim, ldt):
    dt = jnp.exp(ldt)
    er = jnp.exp(are * dt)
    br = er * jnp.cos(aim * dt)
    bi = er * jnp.sin(aim * dt)
    nr = br - 1.0
    den = are * are + aim * aim
    return br, bi, (nr * are + bi * aim) / den, (bi * are - nr * aim) / den


def _s5_disc(name, are, aim, ldt):
    def body(a_ref, b_ref, c_ref, o0, o1, o2, o3):
        r = _s5_disc_fn(a_ref[...], b_ref[...], c_ref[...])
        o0[...], o1[...], o2[...], o3[...] = r

    shp = _sds((S5_GROUPS, S5_STATE))
    return pl.pallas_call(body, name=name, out_shape=[shp] * 4)(are, aim, ldt)


def _s5_disc_bwd(name, are, aim, ldt, cts):
    def body(a_ref, b_ref, c_ref, d0, d1, d2, d3, o0, o1, o2):
        _, vjp = jax.vjp(_s5_disc_fn, a_ref[...], b_ref[...], c_ref[...])
        o0[...], o1[...], o2[...] = vjp((d0[...], d1[...], d2[...], d3[...]))

    shp = _sds((S5_GROUPS, S5_STATE))
    return pl.pallas_call(body, name=name, out_shape=[shp, shp, _sds((S5_GROUPS, 1))])(are, aim, ldt, *cts)


def _adamw_rows(w, g, m, v):
    m = ADAM_B1 * m + (1.0 - ADAM_B1) * g
    v = ADAM_B2 * v + (1.0 - ADAM_B2) * (g * g)
    m_hat = m / (1.0 - ADAM_B1 ** ADAM_STEP)
    v_hat = v / (1.0 - ADAM_B2 ** ADAM_STEP)
    return -ADAM_LR * (m_hat / (jnp.sqrt(v_hat) + ADAM_EPS) + ADAM_WD * w), m, v


def _adamw(name, w, ga, gb, m, v):
    rows, cols = w.shape
    tr = _row_tile(rows)

    def body(w_ref, ga_ref, gb_ref, m_ref, v_ref, g_out, d_out, m_out, v_out):
        g = ga_ref[...] + gb_ref[...]
        d, mm, vv = _adamw_rows(w_ref[...], g, m_ref[...], v_ref[...])
        g_out[...] = g
        d_out[...] = d
        m_out[...] = mm
        v_out[...] = vv

    spec = pl.BlockSpec((tr, cols), lambda i: (i, 0))
    return pl.pallas_call(
        body, name=name, grid=(rows // tr,), in_specs=[spec] * 5, out_specs=[spec] * 4,
        out_shape=[_sds((rows, cols))] * 4, compiler_params=_params("parallel"),
    )(w, ga, gb, m, v)


def _sum_stack(name, st):
    n, rows, cols = st.shape
    tr = _row_tile(rows)

    def body(s_ref, o_ref):
        acc = s_ref[0].astype(f32)
        for j in range(1, n):
            acc = acc + s_ref[j].astype(f32)
        o_ref[...] = acc

    return pl.pallas_call(
        body, name=name, grid=(rows // tr,), in_specs=[pl.BlockSpec((n, tr, cols), lambda i: (0, i, 0))],
        out_specs=pl.BlockSpec((tr, cols), lambda i: (i, 0)), out_shape=_sds((rows, cols)),
        compiler_params=_params("parallel"),
    )(st)


def _block_diag(w):
    h, n, m = w.shape
    return jnp.einsum("hij,hg->higj", w, jnp.eye(h, dtype=w.dtype)).reshape(h * n, h * m)


def _block_diag_part(dense, h):
    n, m = dense.shape[0] // h, dense.shape[1] // h
    return jnp.einsum("higj,hg->hij", dense.reshape(h, n, h, m), jnp.eye(h, dtype=dense.dtype))


def _s5_matrices(coef_re, coef_im, b_re, b_im, c_re, c_im):
    bb_re = coef_re[:, :, None] * b_re - coef_im[:, :, None] * b_im
    bb_im = coef_re[:, :, None] * b_im + coef_im[:, :, None] * b_re
    wb_re = _block_diag(jnp.swapaxes(bb_re, 1, 2))
    wb_im = _block_diag(jnp.swapaxes(bb_im, 1, 2))
    wc_re = _block_diag(jnp.swapaxes(c_re, 1, 2))
    wc_im = _block_diag(jnp.swapaxes(-c_im, 1, 2))
    return wb_re, wb_im, wc_re, wc_im


def _heads(t):
    s = t.shape[0]
    return t.reshape(s, N_HEADS, HEAD_DIM).transpose(1, 0, 2)


def _unheads(t):
    s = t.shape[1]
    return t.transpose(1, 0, 2).reshape(s, N_HEADS * HEAD_DIM)


def _shift_down(t):
    return jnp.concatenate([jnp.zeros((1, t.shape[1]), t.dtype), t[:-1]], axis=0)


def _shift_up(t):
    return jnp.concatenate([t[1:], jnp.zeros((1, t.shape[1]), t.dtype)], axis=0)


def _row(v):
    return v.reshape(1, -1)


def _ffn_fwd(tag, h, get, names, gamma, beta):
    wg, wu = get(names[0]), get(names[1])
    g, u, act = _ffn_up(tag + "_up", h, wg, wu)
    wd = get(names[2])
    r, out = _mm_ln(tag + "_down", act, wd, h, gamma, beta, 0.5, k_slabs=True)
    return out, dict(h=h, g=g, u=u, act=act, r=r, wg=wg, wu=wu, wd=wd)


def _ffn_bwd(tag, dout, sv, names, gamma, put, after_ln=None):
    s = dout.shape[0]
    dr, dgam, dbet = _ln_bwd(tag + "_lnb", sv["r"], dout, gamma)
    if after_ln is not None:
        after_ln(dgam, dbet)
    put(names[2], _mm_plain(tag + "_dwd", "tn", _Slabs(sv["act"]), dr, (D_FF, D_MODEL, s), scale=0.5, out_dtype=bf16,
                            tiles=(FF_SLAB, 1024, _tile(s, 1024))))
    dg, du = _ffn_dact(tag + "_dact", dr, sv["wd"], sv["g"], sv["u"])
    dwg, dwu = _mm2(tag + "_dwgu", "tn", (D_MODEL, D_FF, s), sv["h"], _Slabs(dg), None, _Slabs(du), separate=True,
                    out_dtype=bf16, split_cols=True, tiles=(512, FF_SLAB, _tile(s, 1024)))
    put(names[0], dwg)
    put(names[1], dwu)
    dh = _mm2(tag + "_dh", "nt", (s, D_MODEL, D_FF), _Slabs(dg), _Slabs(sv["wg"]), _Slabs(du), _Slabs(sv["wu"]),
              add=dr, add_coef=ALPHA, tiles=(_tile(s, FFN_ROWS), 1024, FF_SLAB))[0]
    return dh, dgam, dbet


def _mixer_fwd(tag, h1, w):
    s = h1.shape[0]
    z = _mm_plain(tag + "_win", "nn", h1, w["w_in"], (s, N_IN_P, D_MODEL), tiles=(_tile(s, 512), 768, D_MODEL))
    ag, f, cu_cols = (z, D_A, 1), (z, 128, F_OFF // 128), (z, D_C, CU_OFF // D_C)
    cu = z[:, CU_OFF:]
    xa = _conv_fwd(tag + "_conv", z, w["conv_w"], w["conv_b"])
    a, gated = _rg_gates(tag + "_gates", xa, w["rg_wa"], w["rg_wx"], w["rg_ba"], w["rg_bx"], w["rg_lam"])
    ha = _lin_scan(tag + "_rgscan", a, gated, False)
    ones = jnp.ones((s, 128), f32)
    c = _lin_scan(tag + "_cumf", ones, _log_f(tag + "_logf", f, w["fox_bf"]), False)
    att = dict(zip(("qt", "k_aug", "kt", "vt", "v", "qh"), _attn_prep(tag + "_attnprep", z, c, w["sel"])))
    ot, lse = _attn_fwd_t(tag + "_attn", att["qt"], att["k_aug"], att["vt"])
    ob = ot.reshape(D_B, s).T
    bu_re, bu_im = _mm2(tag + "_s5in", "nn", (s, S5_LANES, D_C), cu, w["wb_re"], None, w["wb_im"], separate=True,
                        tiles=(_tile(s, 512), 1024, D_C))
    hre, him = _s5_scan(tag + "_s5scan", bu_re, bu_im, w["abar_re"], w["abar_im"], False)
    o = _mix_out(tag + "_mixout", ag, ha, ob, hre, him, cu_cols, w["s5_d"], w["mix_g"], w["wc_re"], w["wc_im"],
                 w["w_glu"])
    sv = dict(h1=h1, z=z, ag=ag, f=f, cu=cu, cu_cols=cu_cols, xa=xa, a=a, ha=ha, att=att, ot=ot, lse=lse, ob=ob,
              hre=hre, him=him, o=o)
    return o, sv


def _mixer_bwd(tag, do, dr2, sv, w, put):
    s = do.shape[0]
    (dag, dha, dob, dhre, dhim, dcu1, dwcr, dwci, dwglu, dd, dgn) = _mix_out_bwd(
        tag + "_mixoutb", do, sv["ag"], sv["ha"], sv["ob"], sv["hre"], sv["him"], sv["cu_cols"], w["s5_d"], w["mix_g"],
        w["wc_re"], w["wc_im"], w["w_glu"])
    put("s5_w_glu", dwglu.astype(bf16))
    gre, gim = _s5_scan(tag + "_s5scanb", dhre, dhim, w["abar_re"], -w["abar_im"], True)
    dab_re, dab_im = _s5_decay_grad(tag + "_s5dec", _shift_down(sv["hre"]), _shift_down(sv["him"]), gre, gim)
    dwb_re, dwb_im = _mm2(tag + "_s5dwb", "tn", (D_C, S5_LANES, s), sv["cu"], gre, None, gim, separate=True,
                          tiles=(D_C, 1024, _tile(s, 1024)))
    dcu = _mm2(tag + "_s5dcu", "nt", (s, D_C, S5_LANES), gre, w["wb_re"], gim, w["wb_im"], add=dcu1,
               tiles=(_tile(s, 512), D_C, 1024))[0]
    att = sv["att"]
    tq = _att_tiles(s)[0]
    nt = s // tq
    dot_blocks, doh = _attn_do_prep(tag + "_doprep", dob, w["sel"])
    dqt, delta = _attn_bwd_dq_t(tag + "_attndq", att["qt"], att["k_aug"], att["v"], att["kt"], sv["ot"], dot_blocks,
                                sv["lse"])
    dkh, dvh, dck = _attn_bwd_dkv_t(tag + "_attndkv", att["qt"], att["k_aug"], att["v"], att["qh"], doh, dot_blocks,
                                    sv["lse"].reshape(N_HEADS, nt, 1, tq), delta.reshape(N_HEADS, nt, 1, tq))
    dc = jnp.pad(dck[:, :, 0].T, ((0, 0), (0, 128 - N_HEADS)))
    dlf = _lin_scan(tag + "_cumfb", jnp.ones((s, 128), f32), dc, True)
    df, dbf = _log_f_bwd(tag + "_logfb", dlf, sv["f"], w["fox_bf"])
    ga = _lin_scan(tag + "_rgscanb", _shift_up(sv["a"]), dha, True)
    dxa, dwa, dwx, dba, dbx, dlam = _rg_gates_bwd(tag + "_gatesb", sv["xa"], ga, _shift_down(sv["ha"]), w["rg_wa"],
                                                  w["rg_wx"], w["rg_ba"], w["rg_bx"], w["rg_lam"])
    dax, dconv = _conv_bwd(tag + "_convb", dxa, sv["z"], w["conv_w"])
    dz = _dz_assemble(tag + "_dz", dax, dag, dqt, dkh, dvh, df, dcu, w["sel"])
    put("w_in", _mm_plain(tag + "_dwin", "tn", sv["h1"], dz, (D_MODEL, N_IN_P, s), out_dtype=bf16,
                          tiles=(512, 768, _tile(s, 1024))))
    dh1 = _mm_plain(tag + "_dh1", "nt", dz, w["w_in"], (s, D_MODEL, N_IN_P), add=dr2, add_coef=ALPHA,
                    tiles=(_tile(s, 512), 1024, 768))
    grads = dict(dconv=dconv, dwa=dwa, dwx=dwx, dba=dba, dbx=dbx, dlam=dlam, dbf=dbf,
                 dab_re=dab_re, dab_im=dab_im, dwb_re=dwb_re, dwb_im=dwb_im, dwcr=dwcr, dwci=dwci, dd=dd, dgn=dgn)
    return dh1, grads


SMALL_NAMES = ["ln1_g", "ln1_b", "conv_w", "conv_b", "rg_w_a", "rg_b_a", "rg_w_x", "rg_b_x", "rg_lambda", "fox_b_f",
               "s5_a_re", "s5_a_im", "s5_log_dt", "s5_b_re", "s5_b_im", "s5_c_re", "s5_c_im", "s5_d", "mix_norm_g",
               "ln2_g", "ln2_b", "ln3_g", "ln3_b"]
BIG_NAMES = ["ffn1_w_gate", "ffn1_w_up", "ffn1_w_down", "w_in", "s5_w_glu", "w_out", "ffn2_w_gate", "ffn2_w_up",
             "ffn2_w_down"]


def _local_step(x, target, weight, small, on_grads, on_small):
    h = x
    saved = []
    sel = _selections()
    for l in range(DEPTH):
        get = functools.partial(weight, l)

        sm = {n: small[n][l] for n in SMALL_NAMES}
        abar_re, abar_im, coef_re, coef_im = _s5_disc(f"l{l}_s5disc", sm["s5_a_re"], sm["s5_a_im"],
                                                      sm["s5_log_dt"].reshape(S5_GROUPS, 1))
        mats, mats_vjp = jax.vjp(_s5_matrices, coef_re, coef_im, sm["s5_b_re"], sm["s5_b_im"], sm["s5_c_re"],
                                 sm["s5_c_im"])
        w = dict(
            sel=sel, conv_w=sm["conv_w"], conv_b=_row(sm["conv_b"]),
            rg_wa=_block_diag(sm["rg_w_a"]).astype(bf16), rg_wx=_block_diag(sm["rg_w_x"]).astype(bf16),
            rg_ba=_row(sm["rg_b_a"]), rg_bx=_row(sm["rg_b_x"]), rg_lam=_row(sm["rg_lambda"]),
            fox_bf=jnp.pad(_row(sm["fox_b_f"]), ((0, 0), (0, 128 - N_HEADS))),
            abar_re=_row(abar_re), abar_im=_row(abar_im),
            wb_re=mats[0].astype(bf16), wb_im=mats[1].astype(bf16), wc_re=mats[2].astype(bf16),
            wc_im=mats[3].astype(bf16), s5_d=_row(sm["s5_d"]), mix_g=_row(sm["mix_norm_g"]))
        h1, sv1 = _ffn_fwd(f"l{l}_ffn1", h, get, GROUPS["F1"], _row(sm["ln1_g"]), _row(sm["ln1_b"]))
        w["w_in"], w["w_glu"] = get("w_in"), get("s5_w_glu")
        o, svm = _mixer_fwd(f"l{l}_mix", h1, w)
        w_out = get("w_out")
        r2, h2 = _mm_ln(f"l{l}_wout", o, w_out, h1, _row(sm["ln2_g"]), _row(sm["ln2_b"]), 1.0)
        h3, sv2 = _ffn_fwd(f"l{l}_ffn2", h2, get, GROUPS["F2"], _row(sm["ln3_g"]), _row(sm["ln3_b"]))
        saved.append(dict(sm=sm, w=w, w_out=w_out, sv1=sv1, svm=svm, r2=r2, sv2=sv2, mats_vjp=mats_vjp))
        h = h3

    dh, loss_row = _loss_head("loss_head", h, target)
    s = x.shape[0]
    gsmall = {n: [None] * DEPTH for n in SMALL_NAMES}
    for l in reversed(range(DEPTH)):
        sd = saved[l]
        sm, w = sd["sm"], sd["w"]

        def put(name, grad, l=l):
            on_grads((l, name), grad)

        dh2, dgam, dbet = _ffn_bwd(f"l{l}_ffn2", dh, sd["sv2"], GROUPS["F2"], _row(sm["ln3_g"]), put)
        gsmall["ln3_g"][l], gsmall["ln3_b"][l] = dgam[0], dbet[0]
        dr2, dgam, dbet = _ln_bwd(f"l{l}_ln2b", sd["r2"], dh2, _row(sm["ln2_g"]))
        gsmall["ln2_g"][l], gsmall["ln2_b"][l] = dgam[0], dbet[0]
        put("w_out", _mm_plain(f"l{l}_dwout", "tn", sd["svm"]["o"], dr2, (D_MODEL, D_MODEL, s), out_dtype=bf16))
        do = _mm_plain(f"l{l}_do", "nt", dr2, sd["w_out"], (s, D_MODEL, D_MODEL))
        dh1, g = _mixer_bwd(f"l{l}_mix", do, dr2, sd["svm"], w, put)
        gsmall["conv_w"][l], gsmall["conv_b"][l] = g["dconv"][:CONV_WIDTH], g["dconv"][CONV_WIDTH]
        gsmall["rg_w_a"][l] = _block_diag_part(g["dwa"], N_HEADS)
        gsmall["rg_w_x"][l] = _block_diag_part(g["dwx"], N_HEADS)
        gsmall["rg_b_a"][l], gsmall["rg_b_x"][l], gsmall["rg_lambda"][l] = g["dba"][0], g["dbx"][0], g["dlam"][0]
        gsmall["fox_b_f"][l] = g["dbf"][0, :N_HEADS]
        dcoef_re, dcoef_im, db_re, db_im, dc_re, dc_im = sd["mats_vjp"]((g["dwb_re"], g["dwb_im"], g["dwcr"], g["dwci"]))
        da_re, da_im, dldt = _s5_disc_bwd(
            f"l{l}_s5discb", sm["s5_a_re"], sm["s5_a_im"], sm["s5_log_dt"].reshape(S5_GROUPS, 1),
            (g["dab_re"].reshape(S5_GROUPS, S5_STATE), g["dab_im"].reshape(S5_GROUPS, S5_STATE), dcoef_re, dcoef_im))
        gsmall["s5_a_re"][l], gsmall["s5_a_im"][l], gsmall["s5_log_dt"][l] = da_re, da_im, dldt[:, 0]
        gsmall["s5_b_re"][l], gsmall["s5_b_im"][l], gsmall["s5_c_re"][l], gsmall["s5_c_im"][l] = db_re, db_im, dc_re, dc_im
        gsmall["s5_d"][l], gsmall["mix_norm_g"][l] = g["dd"][0], g["dgn"][0]

        def after_ln(dgam, dbet, l=l):
            gsmall["ln1_g"][l], gsmall["ln1_b"][l] = dgam[0], dbet[0]
            if l == 0:
                on_small({n: jnp.stack(v) for n, v in gsmall.items()})

        dh, _, _ = _ffn_bwd(f"l{l}_ffn1", dh1, sd["sv1"], GROUPS["F1"], _row(sm["ln1_g"]), put, after_ln)
    return loss_row[0, 0], dh


def _position():
    return lax.axis_index("x"), lax.axis_index("y"), lax.axis_index("c")


_ANY = pl.BlockSpec(memory_space=pl.ANY)


def _chip_gather(name, shards):
    n = len(shards)

    def body(*refs):
        in_refs, out_refs = refs[:n], refs[n:2 * n]
        send_sems, recv_sems, local_sems = refs[2 * n:]
        x, y, c = _position()
        me = 2 * x + y
        peers = [(1 - x, y), (x, 1 - y), (1 - x, 1 - y)]
        local = [pltpu.make_async_copy(in_refs[i], out_refs[i].at[me], local_sems.at[i]) for i in range(n)]
        for cp in local:
            cp.start()
        sends = []
        for i in range(n):
            for r, (px, py) in enumerate(peers):
                cp = pltpu.make_async_remote_copy(
                    src_ref=in_refs[i], dst_ref=out_refs[i].at[me], send_sem=send_sems.at[3 * i + r],
                    recv_sem=recv_sems.at[3 * i + r], device_id=(px, py, c), device_id_type=MESH)
                cp.start()
                sends.append(cp)
        for i in range(n):
            for r, (px, py) in enumerate(peers):
                pltpu.make_async_remote_copy(
                    src_ref=in_refs[i], dst_ref=out_refs[i].at[2 * px + py], send_sem=send_sems.at[3 * i + r],
                    recv_sem=recv_sems.at[3 * i + r], device_id=(px, py, c), device_id_type=MESH).wait_recv()
        for cp in sends:
            cp.wait_send()
        for cp in local:
            cp.wait()

    return pl.pallas_call(
        body, name=name, in_specs=[_ANY] * n, out_specs=[_ANY] * n,
        out_shape=[_sds((N_CHIPS,) + a.shape, a.dtype) for a in shards],
        scratch_shapes=[pltpu.SemaphoreType.DMA((3 * n,)), pltpu.SemaphoreType.DMA((3 * n,)),
                        pltpu.SemaphoreType.DMA((n,))],
    )(*shards)


def _chip_scatter(name, stacks):
    n = len(stacks)

    def body(*refs):
        in_refs, out_refs = refs[:n], refs[n:2 * n]
        send_sems, recv_sems, local_sems = refs[2 * n:]
        x, y, c = _position()
        me = 2 * x + y
        peers = [(1 - x, y), (x, 1 - y), (1 - x, 1 - y)]
        local = [pltpu.make_async_copy(in_refs[i].at[me], out_refs[i].at[me], local_sems.at[i]) for i in range(n)]
        for cp in local:
            cp.start()
        sends = []
        for i in range(n):
            for r, (px, py) in enumerate(peers):
                cp = pltpu.make_async_remote_copy(
                    src_ref=in_refs[i].at[2 * px + py], dst_ref=out_refs[i].at[me], send_sem=send_sems.at[3 * i + r],
                    recv_sem=recv_sems.at[3 * i + r], device_id=(px, py, c), device_id_type=MESH)
                cp.start()
                sends.append(cp)
        for i in range(n):
            for r, (px, py) in enumerate(peers):
                pltpu.make_async_remote_copy(
                    src_ref=in_refs[i].at[me], dst_ref=out_refs[i].at[2 * px + py], send_sem=send_sems.at[3 * i + r],
                    recv_sem=recv_sems.at[3 * i + r], device_id=(px, py, c), device_id_type=MESH).wait_recv()
        for cp in sends:
            cp.wait_send()
        for cp in local:
            cp.wait()

    return pl.pallas_call(
        body, name=name, in_specs=[_ANY] * n, out_specs=[_ANY] * n,
        out_shape=[_sds(a.shape, a.dtype) for a in stacks],
        scratch_shapes=[pltpu.SemaphoreType.DMA((3 * n,)), pltpu.SemaphoreType.DMA((3 * n,)),
                        pltpu.SemaphoreType.DMA((n,))],
    )(*stacks)


def _sibling_swap(name, arrs):
    n = len(arrs)

    def body(*refs):
        in_refs, out_refs = refs[:n], refs[n:2 * n]
        send_sems, recv_sems = refs[2 * n:]
        x, y, c = _position()
        copies = [pltpu.make_async_remote_copy(
            src_ref=in_refs[i], dst_ref=out_refs[i], send_sem=send_sems.at[i], recv_sem=recv_sems.at[i],
            device_id=(x, y, 1 - c), device_id_type=MESH) for i in range(n)]
        for cp in copies:
            cp.start()
        for cp in copies:
            cp.wait_recv()
        for cp in copies:
            cp.wait_send()

    return pl.pallas_call(
        body, name=name, in_specs=[_ANY] * n, out_specs=[_ANY] * n,
        out_shape=[_sds(a.shape, a.dtype) for a in arrs],
        scratch_shapes=[pltpu.SemaphoreType.DMA((n,)), pltpu.SemaphoreType.DMA((n,))],
    )(*arrs)


def _dev_gather(name, arr):
    def body(in_ref, out_ref, send_sems, recv_sems, local_sem):
        x, y, c = _position()
        me = 4 * x + 2 * y + c
        local = pltpu.make_async_copy(in_ref, out_ref.at[me], local_sem)
        local.start()
        peers = []
        for k in range(1, N_DEV):
            peers.append((1 - x if k & 4 else x, 1 - y if k & 2 else y, 1 - c if k & 1 else c))
        sends = []
        for k, peer in enumerate(peers):
            cp = pltpu.make_async_remote_copy(src_ref=in_ref, dst_ref=out_ref.at[me], send_sem=send_sems.at[k],
                                              recv_sem=recv_sems.at[k], device_id=peer, device_id_type=MESH)
            cp.start()
            sends.append(cp)
        for k, (px, py, pc) in enumerate(peers):
            pltpu.make_async_remote_copy(src_ref=in_ref, dst_ref=out_ref.at[4 * px + 2 * py + pc],
                                         send_sem=send_sems.at[k], recv_sem=recv_sems.at[k], device_id=(px, py, pc),
                                         device_id_type=MESH).wait_recv()
        for cp in sends:
            cp.wait_send()
        local.wait()

    return pl.pallas_call(
        body, name=name, in_specs=[_ANY], out_specs=_ANY, out_shape=_sds((N_DEV,) + arr.shape, arr.dtype),
        scratch_shapes=[pltpu.SemaphoreType.DMA((N_DEV - 1,)), pltpu.SemaphoreType.DMA((N_DEV - 1,)),
                        pltpu.SemaphoreType.DMA],
    )(arr)


COLUMN_SHARDED = ("ffn1_w_gate", "ffn1_w_up", "ffn2_w_gate", "ffn2_w_up")
PACK_QUANTUM = 128 * 256


def _permute_in_cols(w):
    pad = jnp.zeros(w.shape[:-1] + (128 - N_HEADS,), w.dtype)
    return jnp.concatenate([w[..., :F_OFF + N_HEADS], pad, w[..., F_OFF + N_HEADS:]], axis=-1)


def _unpermute_in_cols(w):
    return jnp.concatenate([w[..., :F_OFF + N_HEADS], w[..., CU_OFF:]], axis=-1)


def _unstack(name, st):
    _, l, r, c = st.shape
    if name in COLUMN_SHARDED:
        return st.transpose(1, 2, 0, 3).reshape(l, r, N_CHIPS * c)
    return st.transpose(1, 0, 2, 3).reshape(l, N_CHIPS * r, c)


def _restack(name, g):
    l, r, c = g.shape
    if name in COLUMN_SHARDED:
        return g.reshape(l, r, N_CHIPS, c // N_CHIPS).transpose(2, 0, 1, 3)
    return g.reshape(l, N_CHIPS, r // N_CHIPS, c).transpose(1, 0, 2, 3)


def _pack(arrs):
    flat = jnp.concatenate([a.reshape(-1) for a in arrs])
    pad = -flat.shape[0] % PACK_QUANTUM
    return jnp.pad(flat, (0, pad)).reshape(-1, 128)


def _unpack(buf, shapes):
    flat = buf.reshape(-1)
    out, off = [], 0
    for shp in shapes:
        size = math.prod(shp)
        out.append(flat[off:off + size].reshape(shp))
        off += size
    return out


WEIGHT_NAMES = ["ffn1_w_gate", "ffn1_w_up", "ffn1_w_down", "ln1_g", "ln1_b", "w_in", "conv_w", "conv_b", "rg_w_a",
                "rg_b_a", "rg_w_x", "rg_b_x", "rg_lambda", "fox_b_f", "s5_a_re", "s5_a_im", "s5_log_dt", "s5_b_re",
                "s5_b_im", "s5_c_re", "s5_c_im", "s5_d", "s5_w_glu", "mix_norm_g", "w_out", "ln2_g", "ln2_b",
                "ffn2_w_gate", "ffn2_w_up", "ffn2_w_down", "ln3_g", "ln3_b"]


def _train_step(x, loss_target, w, m, v):
    ix, iy, _ = _position()
    chip = 2 * ix + iy

    shards = [(_permute_in_cols(w[n]) if n == "w_in" else w[n]).astype(bf16) for n in BIG_NAMES]
    stacks = _chip_gather("gather_weights", shards + [w["conv_w"]])
    big = {n: _unstack(n, st) for n, st in zip(BIG_NAMES, stacks)}
    small = {n: w[n] for n in SMALL_NAMES}
    small["conv_w"] = stacks[-1].transpose(1, 2, 0, 3).reshape(DEPTH, CONV_WIDTH, D_A)

    loss_local, gx, gbig, gsmall = _local_step(x[0], loss_target[0], big, small)

    sent = [_restack(n, gbig[n]).astype(bf16) for n in BIG_NAMES]
    recv = _chip_scatter("scatter_grads", sent)
    partial = {}
    for n, st in zip(BIG_NAMES, recv):
        _, l, r, c = st.shape
        p = _sum_stack("sum_" + n, st.reshape(N_CHIPS, l * r, c))
        partial[n] = _unpermute_in_cols(p) if n == "w_in" else p
    other = dict(zip(BIG_NAMES, _sibling_swap("swap_grads", [partial[n] for n in BIG_NAMES])))

    small_shapes = [gsmall[n].shape for n in SMALL_NAMES]
    total = _sum_stack("sum_small", _dev_gather("gather_small", _pack([gsmall[n] for n in SMALL_NAMES])))
    gsm = dict(zip(SMALL_NAMES, _unpack(total, small_shapes)))
    cw = D_A // N_CHIPS
    gsm["conv_w"] = lax.dynamic_slice_in_dim(gsm["conv_w"], chip * cw, cw, axis=2)

    grads, deltas, new_m, new_v = {}, {}, {}, {}
    for n in BIG_NAMES:
        shp = w[n].shape
        two_d = (shp[0] * shp[1], shp[2])
        g, d, mm, vv = _adamw("adamw_" + n, w[n].reshape(two_d), partial[n], other[n], m[n].reshape(two_d),
                              v[n].reshape(two_d))
        grads[n], deltas[n], new_m[n], new_v[n] = (t.reshape(shp) for t in (g, d, mm, vv))
    shapes = [w[n].shape for n in SMALL_NAMES]
    gp = _pack([gsm[n] for n in SMALL_NAMES])
    res = _adamw("adamw_small", _pack([w[n] for n in SMALL_NAMES]), gp, jnp.zeros_like(gp),
                 _pack([m[n] for n in SMALL_NAMES]), _pack([v[n] for n in SMALL_NAMES]))
    for dst, buf in zip((grads, deltas, new_m, new_v), res):
        dst.update(zip(SMALL_NAMES, _unpack(buf, shapes)))

    loss = lax.psum(loss_local, ("x", "y", "c"))
    return (loss, gx[None], *[grads[n] for n in WEIGHT_NAMES], *[deltas[n] for n in WEIGHT_NAMES],
            *[new_m[n] for n in WEIGHT_NAMES], *[new_v[n] for n in WEIGHT_NAMES])


def _remote(src, dst, send_sems, recv_sems, k, peer):
    return pltpu.make_async_remote_copy(src_ref=src, dst_ref=dst, send_sem=send_sems.at[k], recv_sem=recv_sems.at[k],
                                        device_id=peer, device_id_type=MESH)


class _ChipGatherPart:
    def __init__(self, arrays):
        self.arrays, self.results = list(arrays), None

    def out_shape(self):
        return [_sds((N_CHIPS,) + a.shape, a.dtype) for a in self.arrays]

    def sems(self):
        n = len(self.arrays)
        return [pltpu.SemaphoreType.DMA((3 * n,)), pltpu.SemaphoreType.DMA((3 * n,)), pltpu.SemaphoreType.DMA((n,))]

    def copies(self, ins, outs, sems):
        send_sems, recv_sems, local_sems = sems
        x, y, c = _position()
        me = 2 * x + y
        local, sends, recvs = [], [], []
        for i, (src, dst) in enumerate(zip(ins, outs)):
            local.append(pltpu.make_async_copy(self.mine(src, me), dst.at[me], local_sems.at[i]))
            for r, (px, py) in enumerate([(1 - x, y), (x, 1 - y), (1 - x, 1 - y)]):
                peer = 2 * px + py
                sends.append(_remote(self.theirs(src, peer), dst.at[me], send_sems, recv_sems, 3 * i + r, (px, py, c)))
                recvs.append(_remote(self.mine(src, me), dst.at[peer], send_sems, recv_sems, 3 * i + r, (px, py, c)))
        return local, sends, recvs

    def mine(self, src, me):
        return src

    def theirs(self, src, peer):
        return src


class _ChipScatterPart(_ChipGatherPart):
    def out_shape(self):
        return [_sds(a.shape, a.dtype) for a in self.arrays]

    def mine(self, src, me):
        return src.at[me]

    def theirs(self, src, peer):
        return src.at[peer]


class _SiblingSwapPart:
    def __init__(self, arrays):
        self.arrays, self.results = list(arrays), None

    def out_shape(self):
        return [_sds(a.shape, a.dtype) for a in self.arrays]

    def sems(self):
        n = len(self.arrays)
        return [pltpu.SemaphoreType.DMA((n,)), pltpu.SemaphoreType.DMA((n,))]

    def copies(self, ins, outs, sems):
        x, y, c = _position()
        both = [_remote(src, dst, sems[0], sems[1], i, (x, y, 1 - c)) for i, (src, dst) in enumerate(zip(ins, outs))]
        return [], both, both


class _DevGatherPart:
    def __init__(self, array):
        self.arrays, self.results = [array], None

    def out_shape(self):
        return [_sds((N_DEV,) + self.arrays[0].shape, self.arrays[0].dtype)]

    def sems(self):
        return [pltpu.SemaphoreType.DMA((N_DEV - 1,)), pltpu.SemaphoreType.DMA((N_DEV - 1,)),
                pltpu.SemaphoreType.DMA((1,))]

    def copies(self, ins, outs, sems):
        send_sems, recv_sems, local_sems = sems
        (src,), (dst,) = ins, outs
        x, y, c = _position()
        me = 4 * x + 2 * y + c
        local = [pltpu.make_async_copy(src, dst.at[me], local_sems.at[0])]
        sends, recvs = [], []
        for k in range(1, N_DEV):
            px, py, pc = (1 - x if k & 4 else x, 1 - y if k & 2 else y, 1 - c if k & 1 else c)
            sends.append(_remote(src, dst.at[me], send_sems, recv_sems, k - 1, (px, py, pc)))
            recvs.append(_remote(src, dst.at[4 * px + 2 * py + pc], send_sems, recv_sems, k - 1, (px, py, pc)))
        return local, sends, recvs


def _split_by(parts, refs, count):
    out, off = [], 0
    for p in parts:
        out.append(refs[off:off + count(p)])
        off += count(p)
    return out


def _parts_refs(parts, in_refs, out_refs, sem_refs):
    return zip(parts, _split_by(parts, in_refs, lambda p: len(p.arrays)),
               _split_by(parts, out_refs, lambda p: len(p.arrays)), _split_by(parts, sem_refs, lambda p: len(p.sems())))


def _exchange_start(parts, in_refs, out_refs, sem_refs):
    for part, ins, outs, sems in _parts_refs(parts, in_refs, out_refs, sem_refs):
        local, sends, _ = part.copies(ins, outs, sems)
        for cp in local + sends:
            cp.start()


def _exchange_finish(parts, in_refs, out_refs, sem_refs):
    for part, ins, outs, sems in _parts_refs(parts, in_refs, out_refs, sem_refs):
        local, sends, recvs = part.copies(ins, outs, sems)
        for cp in recvs:
            cp.wait_recv()
        for cp in sends:
            cp.wait_send()
        for cp in local:
            cp.wait()


def _exchange_operands(parts):
    return ([a for p in parts for a in p.arrays], [s for p in parts for s in p.out_shape()],
            [s for p in parts for s in p.sems()])


def _set_results(parts, res):
    for part, outs in zip(parts, _split_by(parts, list(res), lambda p: len(p.arrays))):
        part.results = list(outs)


def _exchange_now(name, parts):
    x_in, x_out, x_sem = _exchange_operands(parts)
    n = len(x_in)

    def body(*refs):
        _exchange_start(parts, refs[:n], refs[n:2 * n], refs[2 * n:])
        _exchange_finish(parts, refs[:n], refs[n:2 * n], refs[2 * n:])

    res = pl.pallas_call(body, name=name, in_specs=[_ANY] * n, out_specs=[_ANY] * n, out_shape=x_out,
                         scratch_shapes=x_sem)(*x_in)
    _set_results(parts, res)


_RIDERS = {}


def _call(body, *, name, grid, in_specs, out_specs, out_shape, scratch_shapes=(), compiler_params=None):
    make_parts = _RIDERS.pop(name, None)
    if make_parts is None:
        return pl.pallas_call(body, name=name, grid=grid, in_specs=in_specs, out_specs=out_specs, out_shape=out_shape,
                              scratch_shapes=scratch_shapes, compiler_params=compiler_params)
    parts = make_parts()
    x_in, x_out, x_sem = _exchange_operands(parts)
    n_out, n_scr, n_x = len(out_shape), len(scratch_shapes), len(x_in)

    def run(*args):
        n_in = len(args)

        def hosted(*refs):
            ins, xi = refs[:n_in], refs[n_in:n_in + n_x]
            outs, xo = refs[n_in + n_x:n_in + n_x + n_out], refs[n_in + n_x + n_out:n_in + 2 * n_x + n_out]
            scr, xs = refs[n_in + 2 * n_x + n_out:n_in + 2 * n_x + n_out + n_scr], refs[n_in + 2 * n_x + n_out + n_scr:]
            first = functools.reduce(jnp.logical_and, [pl.program_id(d) == 0 for d in range(len(grid))])
            last = functools.reduce(jnp.logical_and, [pl.program_id(d) == grid[d] - 1 for d in range(len(grid))])

            @pl.when(first)
            def _():
                _exchange_start(parts, xi, xo, xs)

            body(*ins, *outs, *scr)

            @pl.when(last)
            def _():
                _exchange_finish(parts, xi, xo, xs)

        res = pl.pallas_call(
            hosted, name=name, grid=grid, in_specs=list(in_specs) + [_ANY] * n_x,
            out_specs=list(out_specs) + [_ANY] * n_x, out_shape=list(out_shape) + x_out,
            scratch_shapes=list(scratch_shapes) + x_sem, compiler_params=_params(*["arbitrary"] * len(grid)),
        )(*args, *x_in)
        _set_results(parts, res[n_out:])
        return list(res[:n_out])

    return run


GROUPS = {"F1": ["ffn1_w_gate", "ffn1_w_up", "ffn1_w_down"], "MX": ["w_in", "s5_w_glu", "w_out"],
          "F2": ["ffn2_w_gate", "ffn2_w_up", "ffn2_w_down"]}
GROUP_OF = {n: g for g, names in GROUPS.items() for n in names}
FIRST_GATHER = [(0, "ffn1_w_gate"), (0, "ffn1_w_up")]
GATHER_HOSTS = {
    "l0_ffn1_up": [(0, "ffn1_w_down")],
    "l0_ffn1_down": [(0, "w_in"), (0, "s5_w_glu"), (0, "w_out")],
    "l0_mix_win": [(0, "ffn2_w_gate")],
    "l0_mix_attn": [(0, "ffn2_w_up"), (0, "ffn2_w_down")],
    "l0_ffn2_up": [(1, "ffn1_w_gate")],
    "l0_ffn2_down": [(1, "ffn1_w_up")],
    "l1_ffn1_up": [(1, "ffn1_w_down")],
    "l1_ffn1_down": [(1, "w_in"), (1, "s5_w_glu"), (1, "w_out")],
    "l1_mix_win": [(1, "ffn2_w_gate")],
    "l1_mix_attn": [(1, "ffn2_w_up"), (1, "ffn2_w_down")],
}
SCATTER_HOSTS = {
    "l1_ffn2_dact": [(1, "ffn2_w_down")],
    "l1_ffn2_dh": [(1, "ffn2_w_gate")],
    "l1_mix_attndq": [(1, "ffn2_w_up")],
    "l1_mix_attndkv": [(1, "w_out"), (1, "s5_w_glu")],
    "l1_mix_dh1": [(1, "w_in")],
    "l1_ffn1_dact": [(1, "ffn1_w_down")],
    "l1_ffn1_dh": [(1, "ffn1_w_gate")],
    "l0_ffn2_dact": [(1, "ffn1_w_up")],
    "l0_ffn2_dwgu": [(0, "ffn2_w_down")],
    "l0_ffn2_dh": [(0, "ffn2_w_gate")],
    "l0_mix_attndq": [(0, "w_out"), (0, "s5_w_glu"), (0, "ffn2_w_up")],
    "l0_mix_dh1": [(0, "w_in")],
    "l0_ffn1_dact": [(0, "ffn1_w_down")],
    "l0_ffn1_dh": [(0, "ffn1_w_gate")],
}
LAST_SCATTER = [(0, "ffn1_w_up")]
TAIL_HOST = "l0_ffn1_dwgu"
LATE_SCATTER_HOST = "l0_ffn1_dh"


def _unstack_layer(name, st):
    _, r, c = st.shape
    if name in COLUMN_SHARDED:
        return st
    return st.reshape(N_CHIPS * r, c)


def _restack_layer(name, g):
    if name in COLUMN_SHARDED:
        return g
    r, c = g.shape
    return g.reshape(N_CHIPS, r // N_CHIPS, c)


def _adamw_layer(name, layer, w, ga, gb, m, v, bufs):
    _, r, c = w.shape
    tr = _row_tile(r)

    def body(w_ref, ga_ref, gb_ref, m_ref, v_ref, *rest):
        g_out, d_out, m_out, v_out = rest[-4:]
        g = ga_ref[...] + gb_ref[...]
        d, mm, vv = _adamw_rows(w_ref[...], g, m_ref[...], v_ref[...])
        g_out[...] = g
        d_out[...] = d
        m_out[...] = mm
        v_out[...] = vv

    full = pl.BlockSpec((None, tr, c), lambda i: (layer, i, 0))
    flat = pl.BlockSpec((tr, c), lambda i: (i, 0))
    extra = {} if bufs is None else dict(input_output_aliases={5 + k: k for k in range(4)})
    return pl.pallas_call(
        body, name=name, grid=(r // tr,),
        in_specs=[full, flat, flat, full, full] + ([] if bufs is None else [_ANY] * 4),
        out_specs=[full] * 4, out_shape=[_sds(w.shape)] * 4, compiler_params=_params("parallel"), **extra,
    )(w, ga, gb, m, v, *([] if bufs is None else bufs))


def _train_step(x, loss_target, w, m, v):
    ix, iy, _ = _position()
    chip = 2 * ix + iy
    shard = {n: (_permute_in_cols(w[n]) if n == "w_in" else w[n]).astype(bf16) for n in BIG_NAMES}

    gathered = {}

    def gather_parts(keys, extra=()):
        part = _ChipGatherPart([shard[n][layer] for layer, n in keys] + list(extra))
        gathered.update({key: (part, i) for i, key in enumerate(keys)})
        return [part]

    (first,) = gather_parts(FIRST_GATHER, extra=[w["conv_w"]])
    _exchange_now("gather_first", [first])
    for host, keys in GATHER_HOSTS.items():
        _RIDERS[host] = functools.partial(gather_parts, keys)

    def weight(layer, name):
        part, i = gathered[(layer, name)]
        return _unstack_layer(name, part.results[i])

    small = {n: w[n] for n in SMALL_NAMES}
    small["conv_w"] = first.results[-1].transpose(1, 2, 0, 3).reshape(DEPTH, CONV_WIDTH, D_A)

    grads_full, scattered = {}, {}

    def scatter_parts(keys):
        part = _ChipScatterPart([_restack_layer(n, grads_full[(layer, n)]) for layer, n in keys])
        scattered.update({key: (part, i) for i, key in enumerate(keys)})
        return [part]

    for host, keys in SCATTER_HOSTS.items():
        _RIDERS[host] = functools.partial(scatter_parts, keys)

    partial = {}

    def reduce_chips(keys):
        for layer, n in keys:
            part, i = scattered[(layer, n)]
            p = _sum_stack(f"sum_l{layer}_{n}", part.results[i])
            partial[(layer, n)] = _unpermute_in_cols(p) if n == "w_in" else p

    early = [key for host, keys in SCATTER_HOSTS.items() if host != LATE_SCATTER_HOST for key in keys]
    late = SCATTER_HOSTS[LATE_SCATTER_HOST]
    tail = {}

    def tail_parts():
        reduce_chips(early)
        tail["swap"] = _SiblingSwapPart([partial[k] for k in early])
        tail["small"] = _DevGatherPart(_pack([tail["gsmall"][n] for n in SMALL_NAMES]))
        return [tail["swap"], tail["small"]]

    _RIDERS[TAIL_HOST] = tail_parts
    loss_local, gx = _local_step(x[0], loss_target[0], weight, small, grads_full.__setitem__,
                                 functools.partial(tail.__setitem__, "gsmall"))
    other = dict(zip(early, tail["swap"].results))
    reduce_chips(late)
    last_parts = scatter_parts(LAST_SCATTER) + [_SiblingSwapPart([partial[k] for k in late])]
    _exchange_now("exchange_last", last_parts)
    other.update(zip(late, last_parts[1].results))
    reduce_chips(LAST_SCATTER)
    swap_last = _SiblingSwapPart([partial[k] for k in LAST_SCATTER])
    _exchange_now("swap_last", [swap_last])
    other.update(zip(LAST_SCATTER, swap_last.results))

    small_shapes = [tail["gsmall"][n].shape for n in SMALL_NAMES]
    total = _sum_stack("sum_small", tail["small"].results[0])
    gsm = dict(zip(SMALL_NAMES, _unpack(total, small_shapes)))
    cw = D_A // N_CHIPS
    gsm["conv_w"] = lax.dynamic_slice_in_dim(gsm["conv_w"], chip * cw, cw, axis=2)

    grads, deltas, new_m, new_v = {}, {}, {}, {}
    for n in BIG_NAMES:
        bufs = None
        for layer in range(DEPTH):
            bufs = _adamw_layer(f"adamw_l{layer}_{n}", layer, w[n], partial[(layer, n)], other[(layer, n)], m[n], v[n],
                                bufs)
        grads[n], deltas[n], new_m[n], new_v[n] = bufs
    shapes = [w[n].shape for n in SMALL_NAMES]
    gp = _pack([gsm[n] for n in SMALL_NAMES])
    res = _adamw("adamw_small", _pack([w[n] for n in SMALL_NAMES]), gp, jnp.zeros_like(gp),
                 _pack([m[n] for n in SMALL_NAMES]), _pack([v[n] for n in SMALL_NAMES]))
    for dst, buf in zip((grads, deltas, new_m, new_v), res):
        dst.update(zip(SMALL_NAMES, _unpack(buf, shapes)))

    loss = lax.psum(loss_local, ("x", "y", "c"))
    return (loss, gx[None], *[grads[n] for n in WEIGHT_NAMES], *[deltas[n] for n in WEIGHT_NAMES],
            *[new_m[n] for n in WEIGHT_NAMES], *[new_v[n] for n in WEIGHT_NAMES])


def kernel(x, ffn1_w_gate, ffn1_w_up, ffn1_w_down, ln1_g, ln1_b, w_in, conv_w, conv_b, rg_w_a, rg_b_a, rg_w_x, rg_b_x, rg_lambda, fox_b_f, s5_a_re, s5_a_im, s5_log_dt, s5_b_re, s5_b_im, s5_c_re, s5_c_im, s5_d, s5_w_glu, mix_norm_g, w_out, ln2_g, ln2_b, ffn2_w_gate, ffn2_w_up, ffn2_w_down, ln3_g, ln3_b, loss_target, m_ffn1_w_gate, m_ffn1_w_up, m_ffn1_w_down, m_ln1_g, m_ln1_b, m_w_in, m_conv_w, m_conv_b, m_rg_w_a, m_rg_b_a, m_rg_w_x, m_rg_b_x, m_rg_lambda, m_fox_b_f, m_s5_a_re, m_s5_a_im, m_s5_log_dt, m_s5_b_re, m_s5_b_im, m_s5_c_re, m_s5_c_im, m_s5_d, m_s5_w_glu, m_mix_norm_g, m_w_out, m_ln2_g, m_ln2_b, m_ffn2_w_gate, m_ffn2_w_up, m_ffn2_w_down, m_ln3_g, m_ln3_b, v_ffn1_w_gate, v_ffn1_w_up, v_ffn1_w_down, v_ln1_g, v_ln1_b, v_w_in, v_conv_w, v_conv_b, v_rg_w_a, v_rg_b_a, v_rg_w_x, v_rg_b_x, v_rg_lambda, v_fox_b_f, v_s5_a_re, v_s5_a_im, v_s5_log_dt, v_s5_b_re, v_s5_b_im, v_s5_c_re, v_s5_c_im, v_s5_d, v_s5_w_glu, v_mix_norm_g, v_w_out, v_ln2_g, v_ln2_b, v_ffn2_w_gate, v_ffn2_w_up, v_ffn2_w_down, v_ln3_g, v_ln3_b):
    args = dict(locals())
    w = {n: args[n] for n in WEIGHT_NAMES}
    m = {n: args["m_" + n] for n in WEIGHT_NAMES}
    v = {n: args["v_" + n] for n in WEIGHT_NAMES}
    return _train_step(x, loss_target, w, m, v)
```

```python
import functools
import math

import jax
import jax.numpy as jnp
from jax import lax
from jax.experimental import pallas as pl
from jax.experimental.pallas import tpu as pltpu

f32 = jnp.float32
bf16 = jnp.bfloat16

D_MODEL = 1024
D_FF = 2816
D_A = 384
D_B = 384
D_C = 256
N_HEADS = 6
HEAD_DIM = 64
S5_GROUPS = 16
S5_GROUP = 16
S5_STATE = 64
S5_LANES = S5_GROUPS * S5_STATE
N_IN = 2 * D_A + 3 * D_B + N_HEADS + D_C
F_OFF = 5 * D_A
CU_OFF = F_OFF + 128
N_IN_P = CU_OFF + D_C
CONV_WIDTH = 4
DEPTH = 2
ALPHA = (2 * DEPTH) ** 0.25
LN_EPS = 1e-5
RMS_EPS = 1e-6
RG_C = 8.0
ATT_SCALE = HEAD_DIM ** -0.5
ADAM_LR, ADAM_B1, ADAM_B2, ADAM_EPS, ADAM_WD, ADAM_STEP = 0.001, 0.9, 0.999, 1e-08, 0.01, 10

SCAN_CHUNK = 64
ROW_TILE = 256
ATT_TILE = 256
N_CHIPS = 4
N_DEV = 8
MESH = pl.DeviceIdType.MESH

_DN = {
    "nn": (((1,), (0,)), ((), ())),
    "nt": (((1,), (1,)), ((), ())),
    "tn": (((0,), (0,)), ((), ())),
}


def _sds(shape, dtype=f32):
    return jax.ShapeDtypeStruct(shape, dtype)


def _tile(n, target):
    best = None
    for t in range(128, min(n, target) + 1, 128):
        if n % t == 0:
            best = t
    return best or n


def _row_tile(rows, target=256):
    best = None
    for t in range(16, min(rows, target) + 1, 16):
        if rows % t == 0:
            best = t
    return best or rows


def _params(*sem):
    return pltpu.CompilerParams(dimension_semantics=sem)


class _Slabs:
    def __init__(self, x):
        self.x = x


FF_SLAB = D_FF // 4
FFN_ROWS = 1024

def _mm(name, mode, dims, tiles, a_list, b_list, pairs, n_acc, epilogue, outs, extras=(), vecs=(), split_cols=False):
    m, n, k = dims
    tm, tn, tk = tiles
    nk = k // tk
    na, nb, ne, nv, no = len(a_list), len(b_list), len(extras), len(vecs), len(outs)

    def body(*refs):
        a_refs = refs[:na]
        b_refs = refs[na:na + nb]
        e_refs = refs[na + nb:na + nb + ne]
        v_refs = refs[na + nb + ne:na + nb + ne + nv]
        o_refs = refs[na + nb + ne + nv:na + nb + ne + nv + no]
        acc_refs = refs[na + nb + ne + nv + no:]
        a_vals = [r[...].astype(bf16) for r in a_refs]
        b_vals = [r[...].astype(bf16) for r in b_refs]
        products = [(ci, lax.dot_general(a_vals[ai], b_vals[bi], _DN[mode], preferred_element_type=f32))
                    for ai, bi, ci in pairs]

        def finish(accs):
            res = epilogue(accs, [e[...] for e in e_refs], [v[...] for v in v_refs])
            for o, r in zip(o_refs, res):
                o[...] = r.astype(o.dtype)

        if nk == 1:
            accs = [None] * n_acc
            for ci, prod in products:
                accs[ci] = prod if accs[ci] is None else accs[ci] + prod
            finish(accs)
            return
        kk = pl.program_id(2)

        @pl.when(kk == 0)
        def _():
            for acc in acc_refs:
                acc[...] = jnp.zeros_like(acc)

        for ci, prod in products:
            acc_refs[ci][...] += prod

        @pl.when(kk == nk - 1)
        def _():
            finish([acc[...] for acc in acc_refs])

    def a_spec(a):
        if isinstance(a, _Slabs):
            if mode == "tn":
                return pl.BlockSpec((None, tk, tm), lambda i, j, kk: (i, kk, 0))
            return pl.BlockSpec((None, tm, tk), lambda i, j, kk: (kk, i, 0))
        if mode == "tn":
            return pl.BlockSpec((tk, tm), lambda i, j, kk: (kk, i))
        return pl.BlockSpec((tm, tk), lambda i, j, kk: (i, kk))

    def b_spec(b):
        if isinstance(b, _Slabs):
            if mode == "nt":
                return pl.BlockSpec((None, tn, tk), lambda i, j, kk: (kk, j, 0))
            return pl.BlockSpec((None, tk, tn), lambda i, j, kk: (j, kk, 0))
        if mode == "nt":
            return pl.BlockSpec((tn, tk), lambda i, j, kk: (j, kk))
        return pl.BlockSpec((tk, tn), lambda i, j, kk: (kk, j))

    o_spec = pl.BlockSpec((tm, tn), lambda i, j, kk: (i, j))
    o_slab_spec = pl.BlockSpec((None, tm, tn), lambda i, j, kk: (j, i, 0))
    v_spec = pl.BlockSpec((1, tn), lambda i, j, kk: (0, j))
    if split_cols:
        out_specs = [o_slab_spec] * no
        out_shape = [_sds((n // tn, m, tn), dt) for dt in outs]
    else:
        out_specs = [o_spec] * no
        out_shape = [_sds((m, n), dt) for dt in outs]
    raw = lambda t: t.x if isinstance(t, _Slabs) else t
    res = _call(
        body,
        name=name,
        grid=(m // tm, n // tn, nk),
        in_specs=([a_spec(a) for a in a_list] + [b_spec(b) for b in b_list]
                  + [o_slab_spec if isinstance(e, _Slabs) else o_spec for e in extras] + [v_spec] * nv),
        out_specs=out_specs,
        out_shape=out_shape,
        scratch_shapes=[pltpu.VMEM((tm, tn), f32)] * (n_acc if nk > 1 else 0),
        compiler_params=_params("parallel", "parallel", "arbitrary"),
    )(*map(raw, a_list), *map(raw, b_list), *map(raw, extras), *vecs)
    return res


def _sigmoid(x):
    return 0.5 * (jnp.tanh(0.5 * x) + 1.0)


def _layer_norm_rows(r, gamma, beta):
    mu = jnp.mean(r, axis=-1, keepdims=True)
    xc = r - mu
    var = jnp.mean(xc * xc, axis=-1, keepdims=True)
    return xc * lax.rsqrt(var + LN_EPS) * gamma + beta


def _mm_plain(name, mode, a, b, dims, scale=1.0, out_dtype=f32, add=None, add_coef=1.0, tiles=None):
    m, n, k = dims
    tiles = tiles or (_tile(m, 512), _tile(n, 1024), _tile(k, 1024))

    def epilogue(accs, extras, vecs):
        r = accs[0] if scale == 1.0 else accs[0] * scale
        if extras:
            r = r + add_coef * extras[0]
        return [r]

    return _mm(name, mode, dims, tiles, [a], [b], [(0, 0, 0)], 1, epilogue, [out_dtype],
               extras=[] if add is None else [add])[0]


def _ffn_up(name, h, wg, wu):
    s = h.shape[0]

    def epilogue(accs, extras, vecs):
        g, u = accs
        return [g, u, g * _sigmoid(g) * u]

    return _mm(name, "nn", (s, D_FF, D_MODEL), (_tile(s, FFN_ROWS), FF_SLAB, D_MODEL), [h], [_Slabs(wg), _Slabs(wu)],
               [(0, 0, 0), (0, 1, 1)], 2, epilogue, [bf16, bf16, bf16], split_cols=True)


def _mm_ln(name, a, w, resid, gamma, beta, scale, k_slabs=False):
    s, k = (a.shape[1], a.shape[0] * a.shape[2]) if k_slabs else a.shape

    def epilogue(accs, extras, vecs):
        r = ALPHA * extras[0] + scale * accs[0]
        return [r, _layer_norm_rows(r, vecs[0], vecs[1])]

    return _mm(name, "nn", (s, D_MODEL, k), (_tile(s, FFN_ROWS), D_MODEL, FF_SLAB if k_slabs else _tile(k, 1024)),
               [_Slabs(a) if k_slabs else a], [w], [(0, 0, 0)], 1, epilogue, [f32, f32], extras=[resid],
               vecs=[gamma, beta])


def _ffn_dact(name, dr, wd, g, u):
    s = dr.shape[0]

    def epilogue(accs, extras, vecs):
        da = 0.5 * accs[0]
        gg, uu = extras[0].astype(f32), extras[1].astype(f32)
        sg = _sigmoid(gg)
        return [da * uu * (sg * (1.0 + gg * (1.0 - sg))), da * (gg * sg)]

    return _mm(name, "nt", (s, D_FF, D_MODEL), (_tile(s, FFN_ROWS), FF_SLAB, D_MODEL), [dr], [wd],
               [(0, 0, 0)], 1, epilogue, [bf16, bf16], extras=[_Slabs(g), _Slabs(u)], split_cols=True)


def _mm2(name, mode, dims, a0, b0, a1, b1, add=None, add_coef=1.0, separate=False, tiles=None, out_dtype=f32,
         split_cols=False):
    m, n, k = dims
    tiles = tiles or (_tile(m, 512), _tile(n, 1024), _tile(k, 1024))

    def epilogue(accs, extras, vecs):
        if separate:
            return list(accs)
        r = accs[0]
        if extras:
            r = r + add_coef * extras[0]
        return [r]

    a_list = [a0] if a1 is None else [a0, a1]
    b_list = [b0] if b1 is None else [b0, b1]
    pairs = [(0, 0, 0), (len(a_list) - 1, len(b_list) - 1, 1 if separate else 0)]
    return _mm(name, mode, dims, tiles, a_list, b_list, pairs, 2 if separate else 1, epilogue,
               [out_dtype, out_dtype] if separate else [out_dtype], extras=[] if add is None else [add],
               split_cols=split_cols)


def _row_call(name, body, s, ins, params, outs, accs):
    tm = ROW_TILE
    ins = [a if isinstance(a, tuple) else (a, a.shape[1], 0) for a in ins]
    in_specs = [pl.BlockSpec((tm, width), lambda i, cb=cb: (i, cb)) for _, width, cb in ins]
    ins = [a for a, _, _ in ins]
    in_specs += [pl.BlockSpec(p.shape, lambda i, nd=p.ndim: (0,) * nd) for p in params]
    out_specs = [pl.BlockSpec((tm, o.shape[1]), lambda i: (i, 0)) for o in outs]
    out_specs += [pl.BlockSpec(a.shape, lambda i, nd=len(a.shape): (0,) * nd) for a in accs]
    return pl.pallas_call(
        body,
        name=name,
        grid=(s // tm,),
        in_specs=in_specs,
        out_specs=out_specs,
        out_shape=list(outs) + list(accs),
        compiler_params=_params("arbitrary"),
    )(*ins, *params)


def _zero_at_first(refs):
    @pl.when(pl.program_id(0) == 0)
    def _():
        for r in refs:
            r[...] = jnp.zeros_like(r)


def _ln_bwd(name, r, dh, gamma):
    s = r.shape[0]

    def body(r_ref, dh_ref, g_ref, dr_ref, dg_ref, db_ref):
        _zero_at_first([dg_ref, db_ref])
        rr = r_ref[...]
        dy = dh_ref[...]
        mu = jnp.mean(rr, axis=-1, keepdims=True)
        xc = rr - mu
        rstd = lax.rsqrt(jnp.mean(xc * xc, axis=-1, keepdims=True) + LN_EPS)
        xhat = xc * rstd
        dxh = dy * g_ref[...]
        dr_ref[...] = rstd * (dxh - jnp.mean(dxh, axis=-1, keepdims=True)
                              - xhat * jnp.mean(dxh * xhat, axis=-1, keepdims=True))
        dg_ref[...] += jnp.sum(dy * xhat, axis=0, keepdims=True)
        db_ref[...] += jnp.sum(dy, axis=0, keepdims=True)

    return _row_call(name, body, s, [r, dh], [gamma], [_sds((s, D_MODEL))], [_sds((1, D_MODEL)), _sds((1, D_MODEL))])


def _loss_head(name, y, target):
    s = y.shape[0]

    def body(y_ref, t_ref, dy_ref, l_ref):
        _zero_at_first([l_ref])
        e = y_ref[...] - t_ref[...]
        dy_ref[...] = e / D_MODEL
        l_ref[...] += 0.5 * jnp.sum(jnp.mean(e * e, axis=-1, keepdims=True), axis=0, keepdims=True)

    return _row_call(name, body, s, [y, target], [], [_sds((s, D_MODEL))], [_sds((1, 128))])


def _expm1(x):
    series = x * (1.0 + x / 2.0 * (1.0 + x / 3.0 * (1.0 + x / 4.0 * (1.0 + x / 5.0 * (1.0 + x / 6.0 * (1.0 + x / 7.0))))))
    return jnp.where(jnp.abs(x) < 0.25, series, jnp.exp(x) - 1.0)


def _gates_fn(xa, wa, wx, ba, bx, lam, tap_a, tap_x):
    xb = xa.astype(bf16)
    r = jax.nn.sigmoid(jnp.dot(xb, wa, preferred_element_type=f32) + ba + tap_a)
    i = jax.nn.sigmoid(jnp.dot(xb, wx, preferred_element_type=f32) + bx + tap_x)
    log_a = -RG_C * r * jax.nn.softplus(-lam)
    a = jnp.exp(log_a)
    gated = jnp.sqrt(-_expm1(2.0 * log_a)) * (i * xa)
    return a, gated


def _rg_gates(name, xa, wa, wx, ba, bx, lam):
    s = xa.shape[0]

    def body(xa_ref, wa_ref, wx_ref, ba_ref, bx_ref, lam_ref, a_ref, g_ref):
        a, g = _gates_fn(xa_ref[...], wa_ref[...], wx_ref[...], ba_ref[...], bx_ref[...], lam_ref[...], 0.0, 0.0)
        a_ref[...] = a
        g_ref[...] = g

    return _row_call(name, body, s, [xa], [wa, wx, ba, bx, lam], [_sds((s, D_A)), _sds((s, D_A))], [])


def _rg_gates_bwd(name, xa, ga, h_prev, wa, wx, ba, bx, lam):
    s = xa.shape[0]

    def body(xa_ref, ga_ref, hp_ref, wa_ref, wx_ref, ba_ref, bx_ref, lam_ref,
             dxa_ref, dwa_ref, dwx_ref, dba_ref, dbx_ref, dlam_ref):
        _zero_at_first([dwa_ref, dwx_ref, dba_ref, dbx_ref, dlam_ref])
        xa_v = xa_ref[...]
        zero = jnp.zeros((xa_v.shape[0], D_A), f32)
        fn = lambda x, ba_, bx_, lam_, ta, tx: _gates_fn(x, wa_ref[...], wx_ref[...], ba_, bx_, lam_, ta, tx)
        _, vjp = jax.vjp(fn, xa_v, ba_ref[...], bx_ref[...], lam_ref[...], zero, zero)
        gav = ga_ref[...]
        dxa, dba, dbx, dlam, dta, dtx = vjp((gav * hp_ref[...], gav))
        dxa_ref[...] = dxa
        xb = xa_v.astype(bf16)
        dwa_ref[...] += lax.dot_general(xb, dta.astype(bf16), _DN["tn"], preferred_element_type=f32)
        dwx_ref[...] += lax.dot_general(xb, dtx.astype(bf16), _DN["tn"], preferred_element_type=f32)
        dba_ref[...] += dba
        dbx_ref[...] += dbx
        dlam_ref[...] += dlam

    return _row_call(name, body, s, [xa, ga, h_prev], [wa, wx, ba, bx, lam], [_sds((s, D_A))],
                     [_sds((D_A, D_A)), _sds((D_A, D_A)), _sds((1, D_A)), _sds((1, D_A)), _sds((1, D_A))])


def _rms(v, g):
    return v * lax.rsqrt(jnp.mean(v * v, axis=-1, keepdims=True) + RMS_EPS) * g


def _mix_out_fn(ag, ha, ob, hre, him, cu, d, gn, tap_y, tap_gl, wcr, wci, wglu):
    out_a = jax.nn.gelu(ag) * ha
    y = (jnp.dot(hre.astype(bf16), wcr, preferred_element_type=f32)
         + jnp.dot(him.astype(bf16), wci, preferred_element_type=f32) + d * cu + tap_y)
    y2 = jax.nn.gelu(y)
    gl = jnp.dot(y2.astype(bf16), wglu, preferred_element_type=f32) + tap_gl
    out_c = y2 * jax.nn.sigmoid(gl)
    o = jnp.concatenate([_rms(out_a, gn[:, :D_A]), _rms(ob, gn[:, D_A:D_A + D_B]), _rms(out_c, gn[:, D_A + D_B:])],
                        axis=-1)
    return o, y2


def _mix_out(name, ag, ha, ob, hre, him, cu, d, gn, wcr, wci, wglu):
    s = ha.shape[0]

    def body(ag_ref, ha_ref, ob_ref, hre_ref, him_ref, cu_ref, d_ref, gn_ref, wcr_ref, wci_ref, wglu_ref, o_ref):
        o, _ = _mix_out_fn(ag_ref[...], ha_ref[...], ob_ref[...], hre_ref[...], him_ref[...], cu_ref[...], d_ref[...],
                           gn_ref[...], 0.0, 0.0, wcr_ref[...], wci_ref[...], wglu_ref[...])
        o_ref[...] = o.astype(o_ref.dtype)

    return _row_call(name, body, s, [ag, ha, ob, hre, him, cu], [d, gn, wcr, wci, wglu], [_sds((s, D_MODEL), bf16)], [])[0]


def _mix_out_bwd(name, do, ag, ha, ob, hre, him, cu, d, gn, wcr, wci, wglu):
    s = ha.shape[0]

    def body(do_ref, ag_ref, ha_ref, ob_ref, hre_ref, him_ref, cu_ref, d_ref, gn_ref, wcr_ref, wci_ref, wglu_ref,
             dag_ref, dha_ref, dob_ref, dhre_ref, dhim_ref, dcu_ref, dwcr_ref, dwci_ref, dwglu_ref, dd_ref, dgn_ref):
        _zero_at_first([dwcr_ref, dwci_ref, dwglu_ref, dd_ref, dgn_ref])
        tm = ag_ref.shape[0]
        zero = jnp.zeros((tm, D_C), f32)
        hre_v, him_v = hre_ref[...], him_ref[...]
        fn = lambda *a: _mix_out_fn(*a, wcr_ref[...], wci_ref[...], wglu_ref[...])
        _, vjp, y2 = jax.vjp(fn, ag_ref[...], ha_ref[...], ob_ref[...], hre_v, him_v, cu_ref[...], d_ref[...],
                             gn_ref[...], zero, zero, has_aux=True)
        dag, dha, dob, dhre, dhim, dcu, dd, dgn, dy, dgl = vjp(do_ref[...])
        dag_ref[...] = dag
        dha_ref[...] = dha
        dob_ref[...] = dob
        dhre_ref[...] = dhre
        dhim_ref[...] = dhim
        dcu_ref[...] = dcu
        dyb = dy.astype(bf16)
        dwcr_ref[...] += lax.dot_general(hre_v.astype(bf16), dyb, _DN["tn"], preferred_element_type=f32)
        dwci_ref[...] += lax.dot_general(him_v.astype(bf16), dyb, _DN["tn"], preferred_element_type=f32)
        dwglu_ref[...] += lax.dot_general(y2.astype(bf16), dgl.astype(bf16), _DN["tn"], preferred_element_type=f32)
        dd_ref[...] += dd
        dgn_ref[...] += dgn

    outs = [_sds((s, D_A)), _sds((s, D_A)), _sds((s, D_B)), _sds((s, S5_LANES)), _sds((s, S5_LANES)), _sds((s, D_C))]
    accs = [_sds((S5_LANES, D_C)), _sds((S5_LANES, D_C)), _sds((D_C, D_C)), _sds((1, D_C)), _sds((1, D_MODEL))]
    return _row_call(name, body, s, [do, ag, ha, ob, hre, him, cu], [d, gn, wcr, wci, wglu], outs, accs)


def _log_f(name, f, bf):
    s = f[0].shape[0]

    def body(f_ref, b_ref, o_ref):
        o_ref[...] = jax.nn.log_sigmoid(f_ref[...] + b_ref[...])

    return _row_call(name, body, s, [f], [bf], [_sds((s, 128))], [])[0]


def _log_f_bwd(name, dlf, f, bf):
    s = dlf.shape[0]

    def body(dl_ref, f_ref, b_ref, df_ref, db_ref):
        _zero_at_first([db_ref])
        df = dl_ref[...] * jax.nn.sigmoid(-(f_ref[...] + b_ref[...]))
        df_ref[...] = df
        db_ref[...] += jnp.sum(df, axis=0, keepdims=True)

    return _row_call(name, body, s, [dlf, f], [bf], [_sds((s, 128))], [_sds((1, 128))])


def _s5_decay_grad(name, hp_re, hp_im, g_re, g_im):
    s = g_re.shape[0]

    def body(hr_ref, hi_ref, gr_ref, gi_ref, dr_ref, di_ref):
        _zero_at_first([dr_ref, di_ref])
        hr, hi, gr, gi = hr_ref[...], hi_ref[...], gr_ref[...], gi_ref[...]
        dr_ref[...] += jnp.sum(hr * gr + hi * gi, axis=0, keepdims=True)
        di_ref[...] += jnp.sum(hr * gi - hi * gr, axis=0, keepdims=True)

    return _row_call(name, body, s, [hp_re, hp_im, g_re, g_im], [], [], [_sds((1, S5_LANES)), _sds((1, S5_LANES))])


def _conv_fwd(name, ax, w, b):
    s = ax.shape[0]
    tm = ROW_TILE

    def body(x_ref, halo_ref, w_ref, b_ref, o_ref):
        i = pl.program_id(0)
        x = x_ref[...]
        halo = jnp.where(i == 0, 0.0, halo_ref[...])
        ext = jnp.concatenate([halo, x], axis=0)
        acc = b_ref[...] + w_ref[3:4, :] * x
        for k in range(CONV_WIDTH - 1):
            acc = acc + w_ref[k:k + 1, :] * pltpu.roll(ext, CONV_WIDTH - 1 - k, 0)[8:, :]
        o_ref[...] = acc

    return pl.pallas_call(
        body,
        name=name,
        grid=(s // tm,),
        in_specs=[pl.BlockSpec((tm, D_A), lambda i: (i, 0)),
                  pl.BlockSpec((8, D_A), lambda i: (jnp.maximum(i * (tm // 8) - 1, 0), 0)),
                  pl.BlockSpec((CONV_WIDTH, D_A), lambda i: (0, 0)),
                  pl.BlockSpec((1, D_A), lambda i: (0, 0))],
        out_specs=pl.BlockSpec((tm, D_A), lambda i: (i, 0)),
        out_shape=_sds((s, D_A)),
        compiler_params=_params("arbitrary"),
    )(ax, ax, w, b)


def _conv_bwd(name, dxa, ax, w):
    s = ax.shape[0]
    tm = ROW_TILE
    nblk = s // tm

    def body(dx_ref, dnext_ref, x_ref, halo_ref, w_ref, dax_ref, dw_ref):
        i = pl.program_id(0)
        _zero_at_first([dw_ref])
        dx = dx_ref[...]
        dnext = jnp.where(i == nblk - 1, 0.0, dnext_ref[...])
        dext = jnp.concatenate([dx, dnext], axis=0)
        x = x_ref[...]
        halo = jnp.where(i == 0, 0.0, halo_ref[...])
        ext = jnp.concatenate([halo, x], axis=0)
        acc = w_ref[3:4, :] * dx
        dw_ref[3:4, :] += jnp.sum(dx * x, axis=0, keepdims=True)
        for k in range(CONV_WIDTH - 1):
            sh = CONV_WIDTH - 1 - k
            acc = acc + w_ref[k:k + 1, :] * pltpu.roll(dext, tm + 8 - sh, 0)[:tm, :]
            dw_ref[k:k + 1, :] += jnp.sum(dx * pltpu.roll(ext, sh, 0)[8:, :], axis=0, keepdims=True)
        dw_ref[4:5, :] += jnp.sum(dx, axis=0, keepdims=True)
        dax_ref[...] = acc

    return pl.pallas_call(
        body,
        name=name,
        grid=(nblk,),
        in_specs=[pl.BlockSpec((tm, D_A), lambda i: (i, 0)),
                  pl.BlockSpec((8, D_A), lambda i: (jnp.minimum((i + 1) * (tm // 8), s // 8 - 1), 0)),
                  pl.BlockSpec((tm, D_A), lambda i: (i, 0)),
                  pl.BlockSpec((8, D_A), lambda i: (jnp.maximum(i * (tm // 8) - 1, 0), 0)),
                  pl.BlockSpec((CONV_WIDTH, D_A), lambda i: (0, 0))],
        out_specs=[pl.BlockSpec((tm, D_A), lambda i: (i, 0)), pl.BlockSpec((8, D_A), lambda i: (0, 0))],
        out_shape=[_sds((s, D_A)), _sds((8, D_A))],
        compiler_params=_params("arbitrary"),
    )(dxa, dxa, ax, ax, w)


SCAN_ROWS = 512


def _row_in_tile(shape):
    return lax.broadcasted_iota(jnp.int32, shape, 0) % 8


def _lin_scan(name, a, b, reverse):
    s, c = a.shape
    t = min(SCAN_ROWS, s)
    nb = s // t

    def body(a_ref, b_ref, h_ref, p_ref, carry_ref):
        @pl.when(pl.program_id(0) == 0)
        def _():
            carry_ref[...] = jnp.zeros_like(carry_ref)

        row = _row_in_tile((t, c))
        p = a_ref[...]
        h = b_ref[...]
        for d in (1, 2, 4):
            keep = (row < 8 - d) if reverse else (row >= d)
            shift = (t - d) if reverse else d
            h = h + jnp.where(keep, p * pltpu.roll(h, shift, 0), 0.0)
            p = jnp.where(keep, p * pltpu.roll(p, shift, 0), p)
        h_ref[...] = h
        p_ref[...] = p
        edge = 0 if reverse else 7

        def tile(k, carry):
            kk = (t // 8 - 1 - k) if reverse else k
            r0 = pl.multiple_of(kk * 8, 8)
            hh = h_ref[pl.ds(r0, 8), :] + p_ref[pl.ds(r0, 8), :] * carry
            h_ref[pl.ds(r0, 8), :] = hh
            return jnp.broadcast_to(hh[edge:edge + 1, :], (8, c))

        carry_ref[...] = lax.fori_loop(0, t // 8, tile, carry_ref[...])

    spec = pl.BlockSpec((t, c), (lambda i: (nb - 1 - i, 0)) if reverse else (lambda i: (i, 0)))
    (out,) = _call(
        body,
        name=name,
        grid=(nb,),
        in_specs=[spec, spec],
        out_specs=[spec],
        out_shape=[_sds((s, c))],
        scratch_shapes=[pltpu.VMEM((t, c), f32), pltpu.VMEM((8, c), f32)],
        compiler_params=_params("arbitrary"),
    )(a, b)
    return out


def _s5_scan(name, b_re, b_im, a_re, a_im, reverse):
    s, c = b_re.shape
    t = min(SCAN_ROWS, s)
    nb = s // t

    def body(br_ref, bi_ref, ar_ref, ai_ref, hr_ref, hi_ref, cr_ref, ci_ref):
        @pl.when(pl.program_id(0) == 0)
        def _():
            cr_ref[...] = jnp.zeros_like(cr_ref)
            ci_ref[...] = jnp.zeros_like(ci_ref)

        ar1, ai1 = ar_ref[...], ai_ref[...]
        pows = [(ar1, ai1)]
        for _ in range(7):
            pr, pi = pows[-1]
            pows.append((pr * ar1 - pi * ai1, pr * ai1 + pi * ar1))
        row8 = lax.broadcasted_iota(jnp.int32, (8, c), 0)
        wr = jnp.zeros((8, c), f32)
        wi = jnp.zeros((8, c), f32)
        for r in range(8):
            pr, pi = pows[(7 - r) if reverse else r]
            wr = jnp.where(row8 == r, pr, wr)
            wi = jnp.where(row8 == r, pi, wi)
        row = _row_in_tile((t, c))
        hr = br_ref[...]
        hi = bi_ref[...]
        for d in (1, 2, 4):
            keep = (row < 8 - d) if reverse else (row >= d)
            shift = (t - d) if reverse else d
            pr, pi = pows[d - 1]
            cr = jnp.where(keep, pr, 0.0)
            ci = jnp.where(keep, pi, 0.0)
            sr = pltpu.roll(hr, shift, 0)
            si = pltpu.roll(hi, shift, 0)
            hr, hi = hr + cr * sr - ci * si, hi + cr * si + ci * sr
        hr_ref[...] = hr
        hi_ref[...] = hi
        edge = 0 if reverse else 7

        def tile(k, carry):
            car_r, car_i = carry
            kk = (t // 8 - 1 - k) if reverse else k
            r0 = pl.multiple_of(kk * 8, 8)
            xr = hr_ref[pl.ds(r0, 8), :] + wr * car_r - wi * car_i
            xi = hi_ref[pl.ds(r0, 8), :] + wr * car_i + wi * car_r
            hr_ref[pl.ds(r0, 8), :] = xr
            hi_ref[pl.ds(r0, 8), :] = xi
            return (jnp.broadcast_to(xr[edge:edge + 1, :], (8, c)), jnp.broadcast_to(xi[edge:edge + 1, :], (8, c)))

        car_r, car_i = lax.fori_loop(0, t // 8, tile, (cr_ref[...], ci_ref[...]))
        cr_ref[...] = car_r
        ci_ref[...] = car_i

    spec = pl.BlockSpec((t, c), (lambda i: (nb - 1 - i, 0)) if reverse else (lambda i: (i, 0)))
    vspec = pl.BlockSpec((1, c), lambda i: (0, 0))
    hr, hi = _call(
        body,
        name=name,
        grid=(nb,),
        in_specs=[spec, spec, vspec, vspec],
        out_specs=[spec, spec],
        out_shape=[_sds((s, c)), _sds((s, c))],
        scratch_shapes=[pltpu.VMEM((8, c), f32), pltpu.VMEM((8, c), f32)],
        compiler_params=_params("arbitrary"),
    )(b_re, b_im, a_re, a_im)
    return hr, hi


def _causal_mask(t):
    row = lax.broadcasted_iota(jnp.int32, (t, t), 0)
    col = lax.broadcasted_iota(jnp.int32, (t, t), 1)
    return row >= col


def _attn_fwd(name, q, k, v, cq, ck):
    h, s, dh = q.shape
    t = ATT_TILE
    nq = s // t

    def body(q_ref, k_ref, v_ref, cq_ref, ck_ref, o_ref, lse_ref):
        qi = pl.program_id(1)
        qb = q_ref[...].astype(bf16)
        cqv = cq_ref[...]

        def block(kb, carry, masked):
            m, l, acc = carry
            ks = pl.multiple_of(kb * t, t)
            kk = k_ref[pl.ds(ks, t), :].astype(bf16)
            vv = v_ref[pl.ds(ks, t), :].astype(bf16)
            sc = lax.dot_general(qb, kk, _DN["nt"], preferred_element_type=f32) * ATT_SCALE + (cqv - ck_ref[kb])
            if masked:
                sc = jnp.where(_causal_mask(t), sc, -jnp.inf)
            mn = jnp.maximum(m, jnp.max(sc, axis=1, keepdims=True))
            p = jnp.exp(sc - mn)
            al = jnp.exp(m - mn)
            l = al * l + jnp.sum(p, axis=1, keepdims=True)
            acc = al * acc + jnp.dot(p.astype(bf16), vv, preferred_element_type=f32)
            return mn, l, acc

        init = (jnp.full((t, 1), -jnp.inf, f32), jnp.zeros((t, 1), f32), jnp.zeros((t, dh), f32))
        carry = lax.fori_loop(0, qi, lambda kb, c: block(kb, c, False), init)
        m, l, acc = block(qi, carry, True)
        o_ref[...] = acc / l
        lse_ref[...] = m + jnp.log(l)

    return pl.pallas_call(
        body,
        name=name,
        grid=(h, nq),
        in_specs=[pl.BlockSpec((None, t, dh), lambda hh, i: (hh, i, 0)),
                  pl.BlockSpec((None, s, dh), lambda hh, i: (hh, 0, 0)),
                  pl.BlockSpec((None, s, dh), lambda hh, i: (hh, 0, 0)),
                  pl.BlockSpec((None, t, 1), lambda hh, i: (hh, i, 0)),
                  pl.BlockSpec((None, nq, 1, t), lambda hh, i: (hh, 0, 0, 0))],
        out_specs=[pl.BlockSpec((None, t, dh), lambda hh, i: (hh, i, 0)),
                   pl.BlockSpec((None, t, 1), lambda hh, i: (hh, i, 0))],
        out_shape=[_sds((h, s, dh)), _sds((h, s, 1))],
        compiler_params=_params("parallel", "arbitrary"),
    )(q, k, v, cq, ck)


def _attn_bwd_dq(name, q, k, v, cq, ck, o, do, lse):
    h, s, dh = q.shape
    t = ATT_TILE
    nq = s // t

    def body(q_ref, k_ref, v_ref, cq_ref, ck_ref, o_ref, do_ref, lse_ref, dq_ref, dl_ref):
        qi = pl.program_id(1)
        qb = q_ref[...].astype(bf16)
        cqv = cq_ref[...]
        dov = do_ref[...]
        dob = dov.astype(bf16)
        delta = jnp.sum(dov * o_ref[...], axis=1, keepdims=True)
        lse_v = lse_ref[...]

        def block(kb, carry, masked):
            dq, psum = carry
            ks = pl.multiple_of(kb * t, t)
            kk = k_ref[pl.ds(ks, t), :].astype(bf16)
            vv = v_ref[pl.ds(ks, t), :].astype(bf16)
            sc = lax.dot_general(qb, kk, _DN["nt"], preferred_element_type=f32) * ATT_SCALE + (cqv - ck_ref[kb])
            p = jnp.exp(sc - lse_v)
            if masked:
                p = jnp.where(_causal_mask(t), p, 0.0)
            dp = lax.dot_general(dob, vv, _DN["nt"], preferred_element_type=f32)
            ds = p * (dp - delta)
            return (dq + jnp.dot(ds.astype(bf16), kk, preferred_element_type=f32),
                    psum + jnp.sum(p * dp, axis=1, keepdims=True))

        carry = lax.fori_loop(0, qi, lambda kb, c: block(kb, c, False), (jnp.zeros((t, dh), f32), jnp.zeros((t, 1), f32)))
        dq, psum = block(qi, carry, True)
        dq_ref[...] = dq * ATT_SCALE
        dl_ref[...] = psum

    qspec = pl.BlockSpec((None, t, dh), lambda hh, i: (hh, i, 0))
    fspec = pl.BlockSpec((None, s, dh), lambda hh, i: (hh, 0, 0))
    cspec = pl.BlockSpec((None, t, 1), lambda hh, i: (hh, i, 0))
    return pl.pallas_call(
        body,
        name=name,
        grid=(h, nq),
        in_specs=[qspec, fspec, fspec, cspec, pl.BlockSpec((None, nq, 1, t), lambda hh, i: (hh, 0, 0, 0)),
                  qspec, qspec, cspec],
        out_specs=[qspec, cspec],
        out_shape=[_sds((h, s, dh)), _sds((h, s, 1))],
        compiler_params=_params("parallel", "arbitrary"),
    )(q, k, v, cq, ck, o, do, lse)


def _attn_bwd_dkv(name, q, k, v, cq, ck, do, lse, delta):
    h, s, dh = q.shape
    t = ATT_TILE
    nq = s // t

    def body(q_ref, k_ref, v_ref, cq_ref, ck_ref, do_ref, lse_ref, dl_ref, dk_ref, dv_ref, dck_ref):
        kj = pl.program_id(1)
        kk = k_ref[...].astype(bf16)
        vv = v_ref[...].astype(bf16)
        ckv = ck_ref[...]

        def block(qi, carry, masked):
            dk, dv, dcs = carry
            qs = pl.multiple_of(qi * t, t)
            qq = q_ref[pl.ds(qs, t), :].astype(bf16)
            dob = do_ref[pl.ds(qs, t), :].astype(bf16)
            sc = (lax.dot_general(qq, kk, _DN["nt"], preferred_element_type=f32) * ATT_SCALE
                  + (cq_ref[pl.ds(qs, t), :] - ckv))
            p = jnp.exp(sc - lse_ref[pl.ds(qs, t), :])
            if masked:
                p = jnp.where(_causal_mask(t), p, 0.0)
            dv = dv + lax.dot_general(p.astype(bf16), dob, _DN["tn"], preferred_element_type=f32)
            dp = lax.dot_general(dob, vv, _DN["nt"], preferred_element_type=f32)
            ds = p * (dp - dl_ref[pl.ds(qs, t), :])
            dk = dk + lax.dot_general(ds.astype(bf16), qq, _DN["tn"], preferred_element_type=f32)
            return dk, dv, dcs + jnp.sum(ds, axis=0, keepdims=True)

        init = (jnp.zeros((t, dh), f32), jnp.zeros((t, dh), f32), jnp.zeros((1, t), f32))
        carry = block(kj, init, True)
        dk, dv, dcs = lax.fori_loop(kj + 1, nq, lambda qi, c: block(qi, c, False), carry)
        dk_ref[...] = dk * ATT_SCALE
        dv_ref[...] = dv
        dck_ref[...] = -dcs

    kspec = pl.BlockSpec((None, t, dh), lambda hh, j: (hh, j, 0))
    fspec = pl.BlockSpec((None, s, dh), lambda hh, j: (hh, 0, 0))
    fcol = pl.BlockSpec((None, s, 1), lambda hh, j: (hh, 0, 0))
    crow = pl.BlockSpec((None, None, 1, t), lambda hh, j: (hh, j, 0, 0))
    return pl.pallas_call(
        body,
        name=name,
        grid=(h, nq),
        in_specs=[fspec, kspec, kspec, fcol, crow, fspec, fcol, fcol],
        out_specs=[kspec, kspec, crow],
        out_shape=[_sds((h, s, dh)), _sds((h, s, dh)), _sds((h, nq, 1, t))],
        compiler_params=_params("parallel", "arbitrary"),
    )(q, k, v, cq, ck, do, lse, delta)


ATT_FEAT = 128
ATT_TQ = 1024
ATT_TK = 512


def _att_tiles(s):
    tq = min(ATT_TQ, s)
    return tq, ATT_TK, tq // ATT_TK


def _keys_le_queries(tk, tq, k0, q0):
    row = lax.broadcasted_iota(jnp.int32, (tk, tq), 0) + k0
    col = lax.broadcasted_iota(jnp.int32, (tk, tq), 1) + q0
    return row <= col


def _attn_fwd_t(name, qt, k_aug, vt):
    h, s, _ = k_aug.shape
    tq, tk, ratio = _att_tiles(s)

    def body(qt_ref, k_ref, vt_ref, o_ref, lse_ref):
        qi = pl.program_id(1)
        qt = qt_ref[...]

        def block(kb, carry, masked):
            m, l, acc = carry
            ks = pl.multiple_of(kb * tk, tk)
            st = jnp.dot(k_ref[pl.ds(ks, tk), :], qt, preferred_element_type=f32)
            if masked:
                st = jnp.where(_keys_le_queries(tk, tq, ks, qi * tq), st, -jnp.inf)
            mn = jnp.maximum(m, jnp.max(st, axis=0, keepdims=True))
            p = jnp.exp(st - mn)
            al = jnp.exp(m - mn)
            l = al * l + jnp.sum(p, axis=0, keepdims=True)
            acc = al * acc + jnp.dot(vt_ref[kb], p.astype(bf16), preferred_element_type=f32)
            return mn, l, acc

        init = (jnp.full((1, tq), -jnp.inf, f32), jnp.zeros((1, tq), f32), jnp.zeros((HEAD_DIM, tq), f32))
        first = lax.fori_loop(0, qi * ratio, lambda kb, c: block(kb, c, False), init)
        m, l, acc = lax.fori_loop(qi * ratio, (qi + 1) * ratio, lambda kb, c: block(kb, c, True), first)
        o_ref[...] = acc / l
        lse_ref[...] = m + jnp.log(l)

    return _call(
        body,
        name=name,
        grid=(h, s // tq),
        in_specs=[pl.BlockSpec((None, None, ATT_FEAT, tq), lambda hh, i: (hh, i, 0, 0)),
                  pl.BlockSpec((None, s, ATT_FEAT), lambda hh, i: (hh, 0, 0)),
                  pl.BlockSpec((None, s // tk, HEAD_DIM, tk), lambda hh, i: (hh, 0, 0, 0))],
        out_specs=[pl.BlockSpec((None, HEAD_DIM, tq), lambda hh, i: (hh, 0, i)),
                   pl.BlockSpec((None, 1, tq), lambda hh, i: (hh, 0, i))],
        out_shape=[_sds((h, HEAD_DIM, s)), _sds((h, 1, s))],
        compiler_params=_params("parallel", "arbitrary"),
    )(qt, k_aug, vt)


def _attn_bwd_dq_t(name, qt, k_aug, v, kt, ot, dot_, lse):
    h, s, _ = k_aug.shape
    tq, tk, ratio = _att_tiles(s)

    def body(qt_ref, k_ref, v_ref, kt_ref, o_ref, do_ref, lse_ref, dq_ref, dl_ref):
        qi = pl.program_id(1)
        qt = qt_ref[...]
        dob = do_ref[...]
        delta = jnp.sum(dob.astype(f32) * o_ref[...], axis=0, keepdims=True)
        lse_v = lse_ref[...]

        def block(kb, carry, masked):
            dq, psum = carry
            ks = pl.multiple_of(kb * tk, tk)
            st = jnp.dot(k_ref[pl.ds(ks, tk), :], qt, preferred_element_type=f32)
            p = jnp.exp(st - lse_v)
            if masked:
                p = jnp.where(_keys_le_queries(tk, tq, ks, qi * tq), p, 0.0)
            dp = jnp.dot(v_ref[pl.ds(ks, tk), :], dob, preferred_element_type=f32)
            ds = p * (dp - delta)
            return (dq + jnp.dot(kt_ref[kb], ds.astype(bf16), preferred_element_type=f32),
                    psum + jnp.sum(p * dp, axis=0, keepdims=True))

        carry = lax.fori_loop(0, qi * ratio, lambda kb, c: block(kb, c, False),
                              (jnp.zeros((HEAD_DIM, tq), f32), jnp.zeros((1, tq), f32)))
        dq, psum = lax.fori_loop(qi * ratio, (qi + 1) * ratio, lambda kb, c: block(kb, c, True), carry)
        dq_ref[...] = dq * ATT_SCALE
        dl_ref[...] = psum

    qspec = pl.BlockSpec((None, HEAD_DIM, tq), lambda hh, i: (hh, 0, i))
    rspec = pl.BlockSpec((None, 1, tq), lambda hh, i: (hh, 0, i))
    return _call(
        body,
        name=name,
        grid=(h, s // tq),
        in_specs=[pl.BlockSpec((None, None, ATT_FEAT, tq), lambda hh, i: (hh, i, 0, 0)),
                  pl.BlockSpec((None, s, ATT_FEAT), lambda hh, i: (hh, 0, 0)),
                  pl.BlockSpec((None, s, HEAD_DIM), lambda hh, i: (hh, 0, 0)),
                  pl.BlockSpec((None, s // tk, HEAD_DIM, tk), lambda hh, i: (hh, 0, 0, 0)),
                  qspec, pl.BlockSpec((None, None, HEAD_DIM, tq), lambda hh, i: (hh, i, 0, 0)), rspec],
        out_specs=[qspec, rspec],
        out_shape=[_sds((h, HEAD_DIM, s)), _sds((h, 1, s))],
        compiler_params=_params("parallel", "arbitrary"),
    )(qt, k_aug, v, kt, ot, dot_, lse)


def _attn_bwd_dkv_t(name, qt_blocks, k_aug, v, qh, do, dot_blocks, lse, delta):
    h, s, _ = k_aug.shape
    tq, tk, ratio = _att_tiles(s)
    nq = s // tq

    def body(qt_ref, k_ref, v_ref, q_ref, do_ref, dot_ref, lse_ref, dl_ref, dk_ref, dv_ref, dck_ref, dsum_ref):
        kj = pl.program_id(1)
        kk = k_ref[...]
        vv = v_ref[...]
        dsum_ref[...] = jnp.zeros_like(dsum_ref)

        def block(qi, carry, masked):
            dk, dv = carry
            qs = pl.multiple_of(qi * tq, tq)
            st = jnp.dot(kk, qt_ref[qi], preferred_element_type=f32)
            p = jnp.exp(st - lse_ref[qi])
            if masked:
                p = jnp.where(_keys_le_queries(tk, tq, kj * tk, qs), p, 0.0)
            dv = dv + jnp.dot(p.astype(bf16), do_ref[pl.ds(qs, tq), :], preferred_element_type=f32)
            dp = jnp.dot(vv, dot_ref[qi], preferred_element_type=f32)
            ds = p * (dp - dl_ref[qi])
            dsum_ref[...] += ds
            dk = dk + jnp.dot(ds.astype(bf16), q_ref[pl.ds(qs, tq), :], preferred_element_type=f32)
            return dk, dv

        first = kj // ratio
        carry = block(first, (jnp.zeros((tk, HEAD_DIM), f32), jnp.zeros((tk, HEAD_DIM), f32)), True)
        dk, dv = lax.fori_loop(first + 1, nq, lambda qi, c: block(qi, c, False), carry)
        dk_ref[...] = dk
        dv_ref[...] = dv
        dck_ref[...] = -jnp.sum(dsum_ref[...], axis=1, keepdims=True)

    full = lambda shape: pl.BlockSpec((None,) + shape, lambda hh, j: (hh,) + (0,) * len(shape))
    kspec = pl.BlockSpec((None, tk, HEAD_DIM), lambda hh, j: (hh, j, 0))
    return _call(
        body,
        name=name,
        grid=(h, s // tk),
        in_specs=[full((nq, ATT_FEAT, tq)),
                  pl.BlockSpec((None, tk, ATT_FEAT), lambda hh, j: (hh, j, 0)),
                  kspec, full((s, HEAD_DIM)), full((s, HEAD_DIM)), full((nq, HEAD_DIM, tq)),
                  full((nq, 1, tq)), full((nq, 1, tq))],
        out_specs=[kspec, kspec, pl.BlockSpec((None, tk, 1), lambda hh, j: (hh, j, 0))],
        out_shape=[_sds((h, s, HEAD_DIM)), _sds((h, s, HEAD_DIM)), _sds((h, s, 1))],
        scratch_shapes=[pltpu.VMEM((tk, tq), f32)],
        compiler_params=_params("parallel", "arbitrary"),
    )(qt_blocks, k_aug, v, qh, do, dot_blocks, lse, delta)


C_LANES = 128


def _selections():
    h = jnp.arange(N_HEADS)[:, None, None]
    row = jnp.arange(D_B + 3 * C_LANES)[None, :, None]
    col = jnp.arange(ATT_FEAT)[None, None, :]
    head_col = (row < D_B) & (row // HEAD_DIM == h) & (col == row % HEAD_DIM)

    def c_part(p, lane0):
        return (row == D_B + p * C_LANES + h) & (col == lane0 + p)

    c_q = c_part(0, HEAD_DIM) | c_part(1, HEAD_DIM) | c_part(2, HEAD_DIM)
    c_k = c_part(0, HEAD_DIM + 3) | c_part(1, HEAD_DIM + 3) | c_part(2, HEAD_DIM + 3)
    sel_q = (head_col | c_q).astype(bf16)
    sel_k = head_col.astype(bf16) - c_k.astype(bf16)
    sel_h = head_col[:, :D_B, :HEAD_DIM].astype(bf16)
    lane = jnp.arange(ATT_FEAT)
    ones_q = ((lane >= HEAD_DIM + 3) & (lane < HEAD_DIM + 6)).astype(f32)
    ones_k = ((lane >= HEAD_DIM) & (lane < HEAD_DIM + 3)).astype(f32)
    return dict(sel_qt=sel_q.transpose(0, 2, 1), sel_k=sel_k, sel_h=sel_h, sel_ht=sel_h.transpose(0, 2, 1),
                ones_q=ones_q.reshape(ATT_FEAT, 1), ones_k=ones_k.reshape(1, ATT_FEAT))


def _attn_prep(name, z, c, sel):
    s = z.shape[0]
    tq, tk, ratio = _att_tiles(s)

    def body(q_ref, k_ref, v_ref, c_ref, sqt_ref, sk_ref, sh_ref, sht_ref, oq_ref, ok_ref,
             qt_out, ka_out, kt_out, vt_out, v_out, qh_out):
        cv = c_ref[...]
        hi = cv.astype(bf16)
        r1 = cv - hi.astype(f32)
        mid = r1.astype(bf16)
        lo = (r1 - mid.astype(f32)).astype(bf16)
        qs = (q_ref[...] * ATT_SCALE).astype(bf16)
        kb = k_ref[...].astype(bf16)
        vb = v_ref[...].astype(bf16)
        xq = jnp.concatenate([qs, hi, mid, lo], axis=-1)
        xk = jnp.concatenate([kb, hi, mid, lo], axis=-1)
        for h in range(N_HEADS):
            qt = lax.dot_general(sqt_ref[h], xq, _DN["nt"], preferred_element_type=f32) + oq_ref[...]
            qt_out[h, 0] = qt.astype(bf16)
            ka_out[h] = (jnp.dot(xk, sk_ref[h], preferred_element_type=f32) + ok_ref[...]).astype(bf16)
            kt = lax.dot_general(sht_ref[h], kb, _DN["nt"], preferred_element_type=f32).astype(bf16)
            vt = lax.dot_general(sht_ref[h], vb, _DN["nt"], preferred_element_type=f32).astype(bf16)
            for j in range(ratio):
                kt_out[h, j] = kt[:, j * tk:(j + 1) * tk]
                vt_out[h, j] = vt[:, j * tk:(j + 1) * tk]
            v_out[h] = jnp.dot(vb, sh_ref[h], preferred_element_type=f32).astype(bf16)
            qh_out[h] = jnp.dot(qs, sh_ref[h], preferred_element_type=f32).astype(bf16)

    whole = lambda a: pl.BlockSpec(a.shape, lambda i, nd=a.ndim: (0,) * nd)
    consts = [sel["sel_qt"], sel["sel_k"], sel["sel_h"], sel["sel_ht"], sel["ones_q"], sel["ones_k"]]
    return pl.pallas_call(
        body,
        name=name,
        grid=(s // tq,),
        in_specs=[pl.BlockSpec((tq, D_B), lambda i: (i, 2)), pl.BlockSpec((tq, D_B), lambda i: (i, 3)),
                  pl.BlockSpec((tq, D_B), lambda i: (i, 4)), pl.BlockSpec((tq, C_LANES), lambda i: (i, 0))]
        + [whole(a) for a in consts],
        out_specs=[pl.BlockSpec((N_HEADS, 1, ATT_FEAT, tq), lambda i: (0, i, 0, 0)),
                   pl.BlockSpec((N_HEADS, tq, ATT_FEAT), lambda i: (0, i, 0)),
                   pl.BlockSpec((N_HEADS, ratio, HEAD_DIM, tk), lambda i: (0, i, 0, 0)),
                   pl.BlockSpec((N_HEADS, ratio, HEAD_DIM, tk), lambda i: (0, i, 0, 0)),
                   pl.BlockSpec((N_HEADS, tq, HEAD_DIM), lambda i: (0, i, 0)),
                   pl.BlockSpec((N_HEADS, tq, HEAD_DIM), lambda i: (0, i, 0))],
        out_shape=[_sds((N_HEADS, s // tq, ATT_FEAT, tq), bf16), _sds((N_HEADS, s, ATT_FEAT), bf16),
                   _sds((N_HEADS, s // tk, HEAD_DIM, tk), bf16), _sds((N_HEADS, s // tk, HEAD_DIM, tk), bf16),
                   _sds((N_HEADS, s, HEAD_DIM), bf16), _sds((N_HEADS, s, HEAD_DIM), bf16)],
        compiler_params=_params("parallel"),
    )(z, z, z, c, *consts)


def _attn_do_prep(name, dob, sel):
    s = dob.shape[0]
    tq = _att_tiles(s)[0]

    def body(do_ref, sh_ref, sht_ref, dot_out, do_out):
        db = do_ref[...].astype(bf16)
        for h in range(N_HEADS):
            dot_out[h, 0] = lax.dot_general(sht_ref[h], db, _DN["nt"], preferred_element_type=f32).astype(bf16)
            do_out[h] = jnp.dot(db, sh_ref[h], preferred_element_type=f32).astype(bf16)

    whole = lambda a: pl.BlockSpec(a.shape, lambda i, nd=a.ndim: (0,) * nd)
    return pl.pallas_call(
        body,
        name=name,
        grid=(s // tq,),
        in_specs=[pl.BlockSpec((tq, D_B), lambda i: (i, 0)), whole(sel["sel_h"]), whole(sel["sel_ht"])],
        out_specs=[pl.BlockSpec((N_HEADS, 1, HEAD_DIM, tq), lambda i: (0, i, 0, 0)),
                   pl.BlockSpec((N_HEADS, tq, HEAD_DIM), lambda i: (0, i, 0))],
        out_shape=[_sds((N_HEADS, s // tq, HEAD_DIM, tq), bf16), _sds((N_HEADS, s, HEAD_DIM), bf16)],
        compiler_params=_params("parallel"),
    )(dob, sel["sel_h"], sel["sel_ht"])


def _dz_assemble(name, dax, dag, dqt, dkh, dvh, df, dcu, sel):
    s = dax.shape[0]
    tm = _tile(s, 512)

    def body(dax_ref, dag_ref, dqt_ref, dk_ref, dv_ref, df_ref, dcu_ref, sht_ref, o_ref):
        dq = jnp.zeros((tm, D_B), f32)
        dk = jnp.zeros((tm, D_B), f32)
        dv = jnp.zeros((tm, D_B), f32)
        for h in range(N_HEADS):
            place = sht_ref[h]
            dq = dq + lax.dot_general(dqt_ref[h].astype(bf16), place, _DN["tn"], preferred_element_type=f32)
            dk = dk + jnp.dot(dk_ref[h].astype(bf16), place, preferred_element_type=f32)
            dv = dv + jnp.dot(dv_ref[h].astype(bf16), place, preferred_element_type=f32)
        pieces = [dax_ref[...], dag_ref[...], dq, dk, dv, df_ref[...], dcu_ref[...]]
        off = 0
        for p in pieces:
            o_ref[:, off:off + p.shape[1]] = p.astype(bf16)
            off += p.shape[1]

    rows = lambda c_: pl.BlockSpec((tm, c_), lambda i: (i, 0))
    heads = pl.BlockSpec((N_HEADS, tm, HEAD_DIM), lambda i: (0, i, 0))
    return pl.pallas_call(
        body,
        name=name,
        grid=(s // tm,),
        in_specs=[rows(D_A), rows(D_A), pl.BlockSpec((N_HEADS, HEAD_DIM, tm), lambda i: (0, 0, i)), heads, heads,
                  rows(128), rows(D_C), pl.BlockSpec(sel["sel_ht"].shape, lambda i: (0, 0, 0))],
        out_specs=rows(N_IN_P),
        out_shape=_sds((s, N_IN_P), bf16),
        compiler_params=_params("parallel"),
    )(dax, dag, dqt, dkh, dvh, df, dcu, sel["sel_ht"])


def _s5_disc_fn(are, aim, ldt):
    dt = jnp.exp(ldt)
    er = jnp.exp(are * dt)
    br = er * jnp.cos(aim * dt)
    bi = er * jnp.sin(aim * dt)
    nr = br - 1.0
    den = are * are + aim * aim
    return br, bi, (nr * are + bi * aim) / den, (bi * are - nr * aim) / den


def _s5_disc(name, are, aim, ldt):
    def body(a_ref, b_ref, c_ref, o0, o1, o2, o3):
        r = _s5_disc_fn(a_ref[...], b_ref[...], c_ref[...])
        o0[...], o1[...], o2[...], o3[...] = r

    shp = _sds((S5_GROUPS, S5_STATE))
    return pl.pallas_call(body, name=name, out_shape=[shp] * 4)(are, aim, ldt)


def _s5_disc_bwd(name, are, aim, ldt, cts):
    def body(a_ref, b_ref, c_ref, d0, d1, d2, d3, o0, o1, o2):
        _, vjp = jax.vjp(_s5_disc_fn, a_ref[...], b_ref[...], c_ref[...])
        o0[...], o1[...], o2[...] = vjp((d0[...], d1[...], d2[...], d3[...]))

    shp = _sds((S5_GROUPS, S5_STATE))
    return pl.pallas_call(body, name=name, out_shape=[shp, shp, _sds((S5_GROUPS, 1))])(are, aim, ldt, *cts)


def _adamw_rows(w, g, m, v):
    m = ADAM_B1 * m + (1.0 - ADAM_B1) * g
    v = ADAM_B2 * v + (1.0 - ADAM_B2) * (g * g)
    m_hat = m / (1.0 - ADAM_B1 ** ADAM_STEP)
    v_hat = v / (1.0 - ADAM_B2 ** ADAM_STEP)
    return -ADAM_LR * (m_hat / (jnp.sqrt(v_hat) + ADAM_EPS) + ADAM_WD * w), m, v


def _adamw(name, w, ga, gb, m, v):
    rows, cols = w.shape
    tr = _row_tile(rows)

    def body(w_ref, ga_ref, gb_ref, m_ref, v_ref, g_out, d_out, m_out, v_out):
        g = ga_ref[...] + gb_ref[...]
        d, mm, vv = _adamw_rows(w_ref[...], g, m_ref[...], v_ref[...])
        g_out[...] = g
        d_out[...] = d
        m_out[...] = mm
        v_out[...] = vv

    spec = pl.BlockSpec((tr, cols), lambda i: (i, 0))
    return pl.pallas_call(
        body, name=name, grid=(rows // tr,), in_specs=[spec] * 5, out_specs=[spec] * 4,
        out_shape=[_sds((rows, cols))] * 4, compiler_params=_params("parallel"),
    )(w, ga, gb, m, v)


def _sum_stack(name, st):
    n, rows, cols = st.shape
    tr = _row_tile(rows)

    def body(s_ref, o_ref):
        acc = s_ref[0].astype(f32)
        for j in range(1, n):
            acc = acc + s_ref[j].astype(f32)
        o_ref[...] = acc

    return pl.pallas_call(
        body, name=name, grid=(rows // tr,), in_specs=[pl.BlockSpec((n, tr, cols), lambda i: (0, i, 0))],
        out_specs=pl.BlockSpec((tr, cols), lambda i: (i, 0)), out_shape=_sds((rows, cols)),
        compiler_params=_params("parallel"),
    )(st)


def _block_diag(w):
    h, n, m = w.shape
    return jnp.einsum("hij,hg->higj", w, jnp.eye(h, dtype=w.dtype)).reshape(h * n, h * m)


def _block_diag_part(dense, h):
    n, m = dense.shape[0] // h, dense.shape[1] // h
    return jnp.einsum("higj,hg->hij", dense.reshape(h, n, h, m), jnp.eye(h, dtype=dense.dtype))


def _s5_matrices(coef_re, coef_im, b_re, b_im, c_re, c_im):
    bb_re = coef_re[:, :, None] * b_re - coef_im[:, :, None] * b_im
    bb_im = coef_re[:, :, None] * b_im + coef_im[:, :, None] * b_re
    wb_re = _block_diag(jnp.swapaxes(bb_re, 1, 2))
    wb_im = _block_diag(jnp.swapaxes(bb_im, 1, 2))
    wc_re = _block_diag(jnp.swapaxes(c_re, 1, 2))
    wc_im = _block_diag(jnp.swapaxes(-c_im, 1, 2))
    return wb_re, wb_im, wc_re, wc_im


def _heads(t):
    s = t.shape[0]
    return t.reshape(s, N_HEADS, HEAD_DIM).transpose(1, 0, 2)


def _unheads(t):
    s = t.shape[1]
    return t.transpose(1, 0, 2).reshape(s, N_HEADS * HEAD_DIM)


def _shift_down(t):
    return jnp.concatenate([jnp.zeros((1, t.shape[1]), t.dtype), t[:-1]], axis=0)


def _shift_up(t):
    return jnp.concatenate([t[1:], jnp.zeros((1, t.shape[1]), t.dtype)], axis=0)


def _row(v):
    return v.reshape(1, -1)


def _ffn_fwd(tag, h, get, names, gamma, beta):
    wg, wu = get(names[0]), get(names[1])
    g, u, act = _ffn_up(tag + "_up", h, wg, wu)
    wd = get(names[2])
    r, out = _mm_ln(tag + "_down", act, wd, h, gamma, beta, 0.5, k_slabs=True)
    return out, dict(h=h, g=g, u=u, act=act, r=r, wg=wg, wu=wu, wd=wd)


def _ffn_bwd(tag, dout, sv, names, gamma, put, after_ln=None):
    s = dout.shape[0]
    dr, dgam, dbet = _ln_bwd(tag + "_lnb", sv["r"], dout, gamma)
    if after_ln is not None:
        after_ln(dgam, dbet)
    put(names[2], _mm_plain(tag + "_dwd", "tn", _Slabs(sv["act"]), dr, (D_FF, D_MODEL, s), scale=0.5, out_dtype=bf16,
                            tiles=(FF_SLAB, 1024, _tile(s, 1024))))
    dg, du = _ffn_dact(tag + "_dact", dr, sv["wd"], sv["g"], sv["u"])
    dwg, dwu = _mm2(tag + "_dwgu", "tn", (D_MODEL, D_FF, s), sv["h"], _Slabs(dg), None, _Slabs(du), separate=True,
                    out_dtype=bf16, split_cols=True, tiles=(512, FF_SLAB, _tile(s, 1024)))
    put(names[0], dwg)
    put(names[1], dwu)
    dh = _mm2(tag + "_dh", "nt", (s, D_MODEL, D_FF), _Slabs(dg), _Slabs(sv["wg"]), _Slabs(du), _Slabs(sv["wu"]),
              add=dr, add_coef=ALPHA, tiles=(_tile(s, FFN_ROWS), 1024, FF_SLAB))[0]
    return dh, dgam, dbet


def _mixer_fwd(tag, h1, w):
    s = h1.shape[0]
    z = _mm_plain(tag + "_win", "nn", h1, w["w_in"], (s, N_IN_P, D_MODEL), tiles=(_tile(s, 512), 768, D_MODEL))
    ag, f, cu_cols = (z, D_A, 1), (z, 128, F_OFF // 128), (z, D_C, CU_OFF // D_C)
    cu = z[:, CU_OFF:]
    xa = _conv_fwd(tag + "_conv", z, w["conv_w"], w["conv_b"])
    a, gated = _rg_gates(tag + "_gates", xa, w["rg_wa"], w["rg_wx"], w["rg_ba"], w["rg_bx"], w["rg_lam"])
    ha = _lin_scan(tag + "_rgscan", a, gated, False)
    ones = jnp.ones((s, 128), f32)
    c = _lin_scan(tag + "_cumf", ones, _log_f(tag + "_logf", f, w["fox_bf"]), False)
    att = dict(zip(("qt", "k_aug", "kt", "vt", "v", "qh"), _attn_prep(tag + "_attnprep", z, c, w["sel"])))
    ot, lse = _attn_fwd_t(tag + "_attn", att["qt"], att["k_aug"], att["vt"])
    ob = ot.reshape(D_B, s).T
    bu_re, bu_im = _mm2(tag + "_s5in", "nn", (s, S5_LANES, D_C), cu, w["wb_re"], None, w["wb_im"], separate=True,
                        tiles=(_tile(s, 512), 1024, D_C))
    hre, him = _s5_scan(tag + "_s5scan", bu_re, bu_im, w["abar_re"], w["abar_im"], False)
    o = _mix_out(tag + "_mixout", ag, ha, ob, hre, him, cu_cols, w["s5_d"], w["mix_g"], w["wc_re"], w["wc_im"],
                 w["w_glu"])
    sv = dict(h1=h1, z=z, ag=ag, f=f, cu=cu, cu_cols=cu_cols, xa=xa, a=a, ha=ha, att=att, ot=ot, lse=lse, ob=ob,
              hre=hre, him=him, o=o)
    return o, sv


def _mixer_bwd(tag, do, dr2, sv, w, put):
    s = do.shape[0]
    (dag, dha, dob, dhre, dhim, dcu1, dwcr, dwci, dwglu, dd, dgn) = _mix_out_bwd(
        tag + "_mixoutb", do, sv["ag"], sv["ha"], sv["ob"], sv["hre"], sv["him"], sv["cu_cols"], w["s5_d"], w["mix_g"],
        w["wc_re"], w["wc_im"], w["w_glu"])
    put("s5_w_glu", dwglu.astype(bf16))
    gre, gim = _s5_scan(tag + "_s5scanb", dhre, dhim, w["abar_re"], -w["abar_im"], True)
    dab_re, dab_im = _s5_decay_grad(tag + "_s5dec", _shift_down(sv["hre"]), _shift_down(sv["him"]), gre, gim)
    dwb_re, dwb_im = _mm2(tag + "_s5dwb", "tn", (D_C, S5_LANES, s), sv["cu"], gre, None, gim, separate=True,
                          tiles=(D_C, 1024, _tile(s, 1024)))
    dcu = _mm2(tag + "_s5dcu", "nt", (s, D_C, S5_LANES), gre, w["wb_re"], gim, w["wb_im"], add=dcu1,
               tiles=(_tile(s, 512), D_C, 1024))[0]
    att = sv["att"]
    tq = _att_tiles(s)[0]
    nt = s // tq
    dot_blocks, doh = _attn_do_prep(tag + "_doprep", dob, w["sel"])
    dqt, delta = _attn_bwd_dq_t(tag + "_attndq", att["qt"], att["k_aug"], att["v"], att["kt"], sv["ot"], dot_blocks,
                                sv["lse"])
    dkh, dvh, dck = _attn_bwd_dkv_t(tag + "_attndkv", att["qt"], att["k_aug"], att["v"], att["qh"], doh, dot_blocks,
                                    sv["lse"].reshape(N_HEADS, nt, 1, tq), delta.reshape(N_HEADS, nt, 1, tq))
    dc = jnp.pad(dck[:, :, 0].T, ((0, 0), (0, 128 - N_HEADS)))
    dlf = _lin_scan(tag + "_cumfb", jnp.ones((s, 128), f32), dc, True)
    df, dbf = _log_f_bwd(tag + "_logfb", dlf, sv["f"], w["fox_bf"])
    ga = _lin_scan(tag + "_rgscanb", _shift_up(sv["a"]), dha, True)
    dxa, dwa, dwx, dba, dbx, dlam = _rg_gates_bwd(tag + "_gatesb", sv["xa"], ga, _shift_down(sv["ha"]), w["rg_wa"],
                                                  w["rg_wx"], w["rg_ba"], w["rg_bx"], w["rg_lam"])
    dax, dconv = _conv_bwd(tag + "_convb", dxa, sv["z"], w["conv_w"])
    dz = _dz_assemble(tag + "_dz", dax, dag, dqt, dkh, dvh, df, dcu, w["sel"])
    put("w_in", _mm_plain(tag + "_dwin", "tn", sv["h1"], dz, (D_MODEL, N_IN_P, s), out_dtype=bf16,
                          tiles=(512, 768, _tile(s, 1024))))
    dh1 = _mm_plain(tag + "_dh1", "nt", dz, w["w_in"], (s, D_MODEL, N_IN_P), add=dr2, add_coef=ALPHA,
                    tiles=(_tile(s, 512), 1024, 768))
    grads = dict(dconv=dconv, dwa=dwa, dwx=dwx, dba=dba, dbx=dbx, dlam=dlam, dbf=dbf,
                 dab_re=dab_re, dab_im=dab_im, dwb_re=dwb_re, dwb_im=dwb_im, dwcr=dwcr, dwci=dwci, dd=dd, dgn=dgn)
    return dh1, grads


SMALL_NAMES = ["ln1_g", "ln1_b", "conv_w", "conv_b", "rg_w_a", "rg_b_a", "rg_w_x", "rg_b_x", "rg_lambda", "fox_b_f",
               "s5_a_re", "s5_a_im", "s5_log_dt", "s5_b_re", "s5_b_im", "s5_c_re", "s5_c_im", "s5_d", "mix_norm_g",
               "ln2_g", "ln2_b", "ln3_g", "ln3_b"]
BIG_NAMES = ["ffn1_w_gate", "ffn1_w_up", "ffn1_w_down", "w_in", "s5_w_glu", "w_out", "ffn2_w_gate", "ffn2_w_up",
             "ffn2_w_down"]


def _local_step(x, target, weight, small, on_grads, on_small):
    h = x
    saved = []
    sel = _selections()
    for l in range(DEPTH):
        get = functools.partial(weight, l)

        sm = {n: small[n][l] for n in SMALL_NAMES}
        abar_re, abar_im, coef_re, coef_im = _s5_disc(f"l{l}_s5disc", sm["s5_a_re"], sm["s5_a_im"],
                                                      sm["s5_log_dt"].reshape(S5_GROUPS, 1))
        mats, mats_vjp = jax.vjp(_s5_matrices, coef_re, coef_im, sm["s5_b_re"], sm["s5_b_im"], sm["s5_c_re"],
                                 sm["s5_c_im"])
        w = dict(
            sel=sel, conv_w=sm["conv_w"], conv_b=_row(sm["conv_b"]),
            rg_wa=_block_diag(sm["rg_w_a"]).astype(bf16), rg_wx=_block_diag(sm["rg_w_x"]).astype(bf16),
            rg_ba=_row(sm["rg_b_a"]), rg_bx=_row(sm["rg_b_x"]), rg_lam=_row(sm["rg_lambda"]),
            fox_bf=jnp.pad(_row(sm["fox_b_f"]), ((0, 0), (0, 128 - N_HEADS))),
            abar_re=_row(abar_re), abar_im=_row(abar_im),
            wb_re=mats[0].astype(bf16), wb_im=mats[1].astype(bf16), wc_re=mats[2].astype(bf16),
            wc_im=mats[3].astype(bf16), s5_d=_row(sm["s5_d"]), mix_g=_row(sm["mix_norm_g"]))
        h1, sv1 = _ffn_fwd(f"l{l}_ffn1", h, get, GROUPS["F1"], _row(sm["ln1_g"]), _row(sm["ln1_b"]))
        w["w_in"], w["w_glu"] = get("w_in"), get("s5_w_glu")
        o, svm = _mixer_fwd(f"l{l}_mix", h1, w)
        w_out = get("w_out")
        r2, h2 = _mm_ln(f"l{l}_wout", o, w_out, h1, _row(sm["ln2_g"]), _row(sm["ln2_b"]), 1.0)
        h3, sv2 = _ffn_fwd(f"l{l}_ffn2", h2, get, GROUPS["F2"], _row(sm["ln3_g"]), _row(sm["ln3_b"]))
        saved.append(dict(sm=sm, w=w, w_out=w_out, sv1=sv1, svm=svm, r2=r2, sv2=sv2, mats_vjp=mats_vjp))
        h = h3

    dh, loss_row = _loss_head("loss_head", h, target)
    s = x.shape[0]
    gsmall = {n: [None] * DEPTH for n in SMALL_NAMES}
    for l in reversed(range(DEPTH)):
        sd = saved[l]
        sm, w = sd["sm"], sd["w"]

        def put(name, grad, l=l):
            on_grads((l, name), grad)

        dh2, dgam, dbet = _ffn_bwd(f"l{l}_ffn2", dh, sd["sv2"], GROUPS["F2"], _row(sm["ln3_g"]), put)
        gsmall["ln3_g"][l], gsmall["ln3_b"][l] = dgam[0], dbet[0]
        dr2, dgam, dbet = _ln_bwd(f"l{l}_ln2b", sd["r2"], dh2, _row(sm["ln2_g"]))
        gsmall["ln2_g"][l], gsmall["ln2_b"][l] = dgam[0], dbet[0]
        put("w_out", _mm_plain(f"l{l}_dwout", "tn", sd["svm"]["o"], dr2, (D_MODEL, D_MODEL, s), out_dtype=bf16))
        do = _mm_plain(f"l{l}_do", "nt", dr2, sd["w_out"], (s, D_MODEL, D_MODEL))
        dh1, g = _mixer_bwd(f"l{l}_mix", do, dr2, sd["svm"], w, put)
        gsmall["conv_w"][l], gsmall["conv_b"][l] = g["dconv"][:CONV_WIDTH], g["dconv"][CONV_WIDTH]
        gsmall["rg_w_a"][l] = _block_diag_part(g["dwa"], N_HEADS)
        gsmall["rg_w_x"][l] = _block_diag_part(g["dwx"], N_HEADS)
        gsmall["rg_b_a"][l], gsmall["rg_b_x"][l], gsmall["rg_lambda"][l] = g["dba"][0], g["dbx"][0], g["dlam"][0]
        gsmall["fox_b_f"][l] = g["dbf"][0, :N_HEADS]
        dcoef_re, dcoef_im, db_re, db_im, dc_re, dc_im = sd["mats_vjp"]((g["dwb_re"], g["dwb_im"], g["dwcr"], g["dwci"]))
        da_re, da_im, dldt = _s5_disc_bwd(
            f"l{l}_s5discb", sm["s5_a_re"], sm["s5_a_im"], sm["s5_log_dt"].reshape(S5_GROUPS, 1),
            (g["dab_re"].reshape(S5_GROUPS, S5_STATE), g["dab_im"].reshape(S5_GROUPS, S5_STATE), dcoef_re, dcoef_im))
        gsmall["s5_a_re"][l], gsmall["s5_a_im"][l], gsmall["s5_log_dt"][l] = da_re, da_im, dldt[:, 0]
        gsmall["s5_b_re"][l], gsmall["s5_b_im"][l], gsmall["s5_c_re"][l], gsmall["s5_c_im"][l] = db_re, db_im, dc_re, dc_im
        gsmall["s5_d"][l], gsmall["mix_norm_g"][l] = g["dd"][0], g["dgn"][0]

        def after_ln(dgam, dbet, l=l):
            gsmall["ln1_g"][l], gsmall["ln1_b"][l] = dgam[0], dbet[0]
            if l == 0:
                on_small({n: jnp.stack(v) for n, v in gsmall.items()})

        dh, _, _ = _ffn_bwd(f"l{l}_ffn1", dh1, sd["sv1"], GROUPS["F1"], _row(sm["ln1_g"]), put, after_ln)
    return loss_row[0, 0], dh


def _position():
    return lax.axis_index("x"), lax.axis_index("y"), lax.axis_index("c")


_ANY = pl.BlockSpec(memory_space=pl.ANY)


def _chip_gather(name, shards):
    n = len(shards)

    def body(*refs):
        in_refs, out_refs = refs[:n], refs[n:2 * n]
        send_sems, recv_sems, local_sems = refs[2 * n:]
        x, y, c = _position()
        me = 2 * x + y
        peers = [(1 - x, y), (x, 1 - y), (1 - x, 1 - y)]
        local = [pltpu.make_async_copy(in_refs[i], out_refs[i].at[me], local_sems.at[i]) for i in range(n)]
        for cp in local:
            cp.start()
        sends = []
        for i in range(n):
            for r, (px, py) in enumerate(peers):
                cp = pltpu.make_async_remote_copy(
                    src_ref=in_refs[i], dst_ref=out_refs[i].at[me], send_sem=send_sems.at[3 * i + r],
                    recv_sem=recv_sems.at[3 * i + r], device_id=(px, py, c), device_id_type=MESH)
                cp.start()
                sends.append(cp)
        for i in range(n):
            for r, (px, py) in enumerate(peers):
                pltpu.make_async_remote_copy(
                    src_ref=in_refs[i], dst_ref=out_refs[i].at[2 * px + py], send_sem=send_sems.at[3 * i + r],
                    recv_sem=recv_sems.at[3 * i + r], device_id=(px, py, c), device_id_type=MESH).wait_recv()
        for cp in sends:
            cp.wait_send()
        for cp in local:
            cp.wait()

    return pl.pallas_call(
        body, name=name, in_specs=[_ANY] * n, out_specs=[_ANY] * n,
        out_shape=[_sds((N_CHIPS,) + a.shape, a.dtype) for a in shards],
        scratch_shapes=[pltpu.SemaphoreType.DMA((3 * n,)), pltpu.SemaphoreType.DMA((3 * n,)),
                        pltpu.SemaphoreType.DMA((n,))],
    )(*shards)


def _chip_scatter(name, stacks):
    n = len(stacks)

    def body(*refs):
        in_refs, out_refs = refs[:n], refs[n:2 * n]
        send_sems, recv_sems, local_sems = refs[2 * n:]
        x, y, c = _position()
        me = 2 * x + y
        peers = [(1 - x, y), (x, 1 - y), (1 - x, 1 - y)]
        local = [pltpu.make_async_copy(in_refs[i].at[me], out_refs[i].at[me], local_sems.at[i]) for i in range(n)]
        for cp in local:
            cp.start()
        sends = []
        for i in range(n):
            for r, (px, py) in enumerate(peers):
                cp = pltpu.make_async_remote_copy(
                    src_ref=in_refs[i].at[2 * px + py], dst_ref=out_refs[i].at[me], send_sem=send_sems.at[3 * i + r],
                    recv_sem=recv_sems.at[3 * i + r], device_id=(px, py, c), device_id_type=MESH)
                cp.start()
                sends.append(cp)
        for i in range(n):
            for r, (px, py) in enumerate(peers):
                pltpu.make_async_remote_copy(
                    src_ref=in_refs[i].at[me], dst_ref=out_refs[i].at[2 * px + py], send_sem=send_sems.at[3 * i + r],
                    recv_sem=recv_sems.at[3 * i + r], device_id=(px, py, c), device_id_type=MESH).wait_recv()
        for cp in sends:
            cp.wait_send()
        for cp in local:
            cp.wait()

    return pl.pallas_call(
        body, name=name, in_specs=[_ANY] * n, out_specs=[_ANY] * n,
        out_shape=[_sds(a.shape, a.dtype) for a in stacks],
        scratch_shapes=[pltpu.SemaphoreType.DMA((3 * n,)), pltpu.SemaphoreType.DMA((3 * n,)),
                        pltpu.SemaphoreType.DMA((n,))],
    )(*stacks)


def _sibling_swap(name, arrs):
    n = len(arrs)

    def body(*refs):
        in_refs, out_refs = refs[:n], refs[n:2 * n]
        send_sems, recv_sems = refs[2 * n:]
        x, y, c = _position()
        copies = [pltpu.make_async_remote_copy(
            src_ref=in_refs[i], dst_ref=out_refs[i], send_sem=send_sems.at[i], recv_sem=recv_sems.at[i],
            device_id=(x, y, 1 - c), device_id_type=MESH) for i in range(n)]
        for cp in copies:
            cp.start()
        for cp in copies:
            cp.wait_recv()
        for cp in copies:
            cp.wait_send()

    return pl.pallas_call(
        body, name=name, in_specs=[_ANY] * n, out_specs=[_ANY] * n,
        out_shape=[_sds(a.shape, a.dtype) for a in arrs],
        scratch_shapes=[pltpu.SemaphoreType.DMA((n,)), pltpu.SemaphoreType.DMA((n,))],
    )(*arrs)


def _dev_gather(name, arr):
    def body(in_ref, out_ref, send_sems, recv_sems, local_sem):
        x, y, c = _position()
        me = 4 * x + 2 * y + c
        local = pltpu.make_async_copy(in_ref, out_ref.at[me], local_sem)
        local.start()
        peers = []
        for k in range(1, N_DEV):
            peers.append((1 - x if k & 4 else x, 1 - y if k & 2 else y, 1 - c if k & 1 else c))
        sends = []
        for k, peer in enumerate(peers):
            cp = pltpu.make_async_remote_copy(src_ref=in_ref, dst_ref=out_ref.at[me], send_sem=send_sems.at[k],
                                              recv_sem=recv_sems.at[k], device_id=peer, device_id_type=MESH)
            cp.start()
            sends.append(cp)
        for k, (px, py, pc) in enumerate(peers):
            pltpu.make_async_remote_copy(src_ref=in_ref, dst_ref=out_ref.at[4 * px + 2 * py + pc],
                                         send_sem=send_sems.at[k], recv_sem=recv_sems.at[k], device_id=(px, py, pc),
                                         device_id_type=MESH).wait_recv()
        for cp in sends:
            cp.wait_send()
        local.wait()

    return pl.pallas_call(
        body, name=name, in_specs=[_ANY], out_specs=_ANY, out_shape=_sds((N_DEV,) + arr.shape, arr.dtype),
        scratch_shapes=[pltpu.SemaphoreType.DMA((N_DEV - 1,)), pltpu.SemaphoreType.DMA((N_DEV - 1,)),
                        pltpu.SemaphoreType.DMA],
    )(arr)


COLUMN_SHARDED = ("ffn1_w_gate", "ffn1_w_up", "ffn2_w_gate", "ffn2_w_up")
PACK_QUANTUM = 128 * 256


def _permute_in_cols(w):
    pad = jnp.zeros(w.shape[:-1] + (128 - N_HEADS,), w.dtype)
    return jnp.concatenate([w[..., :F_OFF + N_HEADS], pad, w[..., F_OFF + N_HEADS:]], axis=-1)


def _unpermute_in_cols(w):
    return jnp.concatenate([w[..., :F_OFF + N_HEADS], w[..., CU_OFF:]], axis=-1)


def _unstack(name, st):
    _, l, r, c = st.shape
    if name in COLUMN_SHARDED:
        return st.transpose(1, 2, 0, 3).reshape(l, r, N_CHIPS * c)
    return st.transpose(1, 0, 2, 3).reshape(l, N_CHIPS * r, c)


def _restack(name, g):
    l, r, c = g.shape
    if name in COLUMN_SHARDED:
        return g.reshape(l, r, N_CHIPS, c // N_CHIPS).transpose(2, 0, 1, 3)
    return g.reshape(l, N_CHIPS, r // N_CHIPS, c).transpose(1, 0, 2, 3)


def _pack(arrs):
    flat = jnp.concatenate([a.reshape(-1) for a in arrs])
    pad = -flat.shape[0] % PACK_QUANTUM
    return jnp.pad(flat, (0, pad)).reshape(-1, 128)


def _unpack(buf, shapes):
    flat = buf.reshape(-1)
    out, off = [], 0
    for shp in shapes:
        size = math.prod(shp)
        out.append(flat[off:off + size].reshape(shp))
        off += size
    return out


WEIGHT_NAMES = ["ffn1_w_gate", "ffn1_w_up", "ffn1_w_down", "ln1_g", "ln1_b", "w_in", "conv_w", "conv_b", "rg_w_a",
                "rg_b_a", "rg_w_x", "rg_b_x", "rg_lambda", "fox_b_f", "s5_a_re", "s5_a_im", "s5_log_dt", "s5_b_re",
                "s5_b_im", "s5_c_re", "s5_c_im", "s5_d", "s5_w_glu", "mix_norm_g", "w_out", "ln2_g", "ln2_b",
                "ffn2_w_gate", "ffn2_w_up", "ffn2_w_down", "ln3_g", "ln3_b"]


def _train_step(x, loss_target, w, m, v):
    ix, iy, _ = _position()
    chip = 2 * ix + iy

    shards = [(_permute_in_cols(w[n]) if n == "w_in" else w[n]).astype(bf16) for n in BIG_NAMES]
    stacks = _chip_gather("gather_weights", shards + [w["conv_w"]])
    big = {n: _unstack(n, st) for n, st in zip(BIG_NAMES, stacks)}
    small = {n: w[n] for n in SMALL_NAMES}
    small["conv_w"] = stacks[-1].transpose(1, 2, 0, 3).reshape(DEPTH, CONV_WIDTH, D_A)

    loss_local, gx, gbig, gsmall = _local_step(x[0], loss_target[0], big, small)

    sent = [_restack(n, gbig[n]).astype(bf16) for n in BIG_NAMES]
    recv = _chip_scatter("scatter_grads", sent)
    partial = {}
    for n, st in zip(BIG_NAMES, recv):
        _, l, r, c = st.shape
        p = _sum_stack("sum_" + n, st.reshape(N_CHIPS, l * r, c))
        partial[n] = _unpermute_in_cols(p) if n == "w_in" else p
    other = dict(zip(BIG_NAMES, _sibling_swap("swap_grads", [partial[n] for n in BIG_NAMES])))

    small_shapes = [gsmall[n].shape for n in SMALL_NAMES]
    total = _sum_stack("sum_small", _dev_gather("gather_small", _pack([gsmall[n] for n in SMALL_NAMES])))
    gsm = dict(zip(SMALL_NAMES, _unpack(total, small_shapes)))
    cw = D_A // N_CHIPS
    gsm["conv_w"] = lax.dynamic_slice_in_dim(gsm["conv_w"], chip * cw, cw, axis=2)

    grads, deltas, new_m, new_v = {}, {}, {}, {}
    for n in BIG_NAMES:
        shp = w[n].shape
        two_d = (shp[0] * shp[1], shp[2])
        g, d, mm, vv = _adamw("adamw_" + n, w[n].reshape(two_d), partial[n], other[n], m[n].reshape(two_d),
                              v[n].reshape(two_d))
        grads[n], deltas[n], new_m[n], new_v[n] = (t.reshape(shp) for t in (g, d, mm, vv))
    shapes = [w[n].shape for n in SMALL_NAMES]
    gp = _pack([gsm[n] for n in SMALL_NAMES])
    res = _adamw("adamw_small", _pack([w[n] for n in SMALL_NAMES]), gp, jnp.zeros_like(gp),
                 _pack([m[n] for n in SMALL_NAMES]), _pack([v[n] for n in SMALL_NAMES]))
    for dst, buf in zip((grads, deltas, new_m, new_v), res):
        dst.update(zip(SMALL_NAMES, _unpack(buf, shapes)))

    loss = lax.psum(loss_local, ("x", "y", "c"))
    return (loss, gx[None], *[grads[n] for n in WEIGHT_NAMES], *[deltas[n] for n in WEIGHT_NAMES],
            *[new_m[n] for n in WEIGHT_NAMES], *[new_v[n] for n in WEIGHT_NAMES])


def _remote(src, dst, send_sems, recv_sems, k, peer):
    return pltpu.make_async_remote_copy(src_ref=src, dst_ref=dst, send_sem=send_sems.at[k], recv_sem=recv_sems.at[k],
                                        device_id=peer, device_id_type=MESH)


class _ChipGatherPart:
    def __init__(self, arrays):
        self.arrays, self.results = list(arrays), None

    def out_shape(self):
        return [_sds((N_CHIPS,) + a.shape, a.dtype) for a in self.arrays]

    def sems(self):
        n = len(self.arrays)
        return [pltpu.SemaphoreType.DMA((3 * n,)), pltpu.SemaphoreType.DMA((3 * n,)), pltpu.SemaphoreType.DMA((n,))]

    def copies(self, ins, outs, sems):
        send_sems, recv_sems, local_sems = sems
        x, y, c = _position()
        me = 2 * x + y
        local, sends, recvs = [], [], []
        for i, (src, dst) in enumerate(zip(ins, outs)):
            local.append(pltpu.make_async_copy(self.mine(src, me), dst.at[me], local_sems.at[i]))
            for r, (px, py) in enumerate([(1 - x, y), (x, 1 - y), (1 - x, 1 - y)]):
                peer = 2 * px + py
                sends.append(_remote(self.theirs(src, peer), dst.at[me], send_sems, recv_sems, 3 * i + r, (px, py, c)))
                recvs.append(_remote(self.mine(src, me), dst.at[peer], send_sems, recv_sems, 3 * i + r, (px, py, c)))
        return local, sends, recvs

    def mine(self, src, me):
        return src

    def theirs(self, src, peer):
        return src


class _ChipGatherHalvesPart(_ChipGatherPart):
    def sems(self):
        n = len(self.arrays)
        return super().sems() + [pltpu.SemaphoreType.DMA((3 * n,)), pltpu.SemaphoreType.DMA((3 * n,))]

    def _half(self, ref, which):
        rows = ref.shape[0] // 2
        return ref.at[pl.ds(which * rows, rows)]

    def copies(self, ins, outs, sems):
        send_sems, recv_sems, local_sems = sems[:3]
        x, y, c = _position()
        me = 2 * x + y
        local, sends, recvs = [], [], []
        for i, (src, dst) in enumerate(zip(ins, outs)):
            local.append(pltpu.make_async_copy(src, dst.at[me], local_sems.at[i]))
            for r, (px, py) in enumerate([(1 - x, y), (x, 1 - y), (1 - x, 1 - y)]):
                sends.append(_remote(self._half(src, c), self._half(dst.at[me], c), send_sems, recv_sems, 3 * i + r,
                                     (px, py, c)))
                recvs.append(_remote(self._half(src, c), self._half(dst.at[2 * px + py], c), send_sems, recv_sems,
                                     3 * i + r, (px, py, c)))
        return local, sends, recvs

    def forwards(self, ins, outs, sems):
        send_sems, recv_sems = sems[3:]
        x, y, c = _position()
        sends, recvs = [], []
        for i, dst in enumerate(outs):
            for r, (px, py) in enumerate([(1 - x, y), (x, 1 - y), (1 - x, 1 - y)]):
                slot = dst.at[2 * px + py]
                sends.append(_remote(self._half(slot, c), self._half(slot, c), send_sems, recv_sems, 3 * i + r,
                                     (x, y, 1 - c)))
                recvs.append(_remote(self._half(slot, c), self._half(slot, 1 - c), send_sems, recv_sems, 3 * i + r,
                                     (x, y, 1 - c)))
        return sends, recvs


class _ChipScatterPart(_ChipGatherPart):
    def out_shape(self):
        return [_sds(a.shape, a.dtype) for a in self.arrays]

    def mine(self, src, me):
        return src.at[me]

    def theirs(self, src, peer):
        return src.at[peer]


class _SiblingSwapPart:
    def __init__(self, arrays):
        self.arrays, self.results = list(arrays), None

    def out_shape(self):
        return [_sds(a.shape, a.dtype) for a in self.arrays]

    def sems(self):
        n = len(self.arrays)
        return [pltpu.SemaphoreType.DMA((n,)), pltpu.SemaphoreType.DMA((n,))]

    def copies(self, ins, outs, sems):
        x, y, c = _position()
        both = [_remote(src, dst, sems[0], sems[1], i, (x, y, 1 - c)) for i, (src, dst) in enumerate(zip(ins, outs))]
        return [], both, both


class _DevGatherPart:
    def __init__(self, array):
        self.arrays, self.results = [array], None

    def out_shape(self):
        return [_sds((N_DEV,) + self.arrays[0].shape, self.arrays[0].dtype)]

    def sems(self):
        return [pltpu.SemaphoreType.DMA((N_DEV - 1,)), pltpu.SemaphoreType.DMA((N_DEV - 1,)),
                pltpu.SemaphoreType.DMA((1,))]

    def copies(self, ins, outs, sems):
        send_sems, recv_sems, local_sems = sems
        (src,), (dst,) = ins, outs
        x, y, c = _position()
        me = 4 * x + 2 * y + c
        local = [pltpu.make_async_copy(src, dst.at[me], local_sems.at[0])]
        sends, recvs = [], []
        for k in range(1, N_DEV):
            px, py, pc = (1 - x if k & 4 else x, 1 - y if k & 2 else y, 1 - c if k & 1 else c)
            sends.append(_remote(src, dst.at[me], send_sems, recv_sems, k - 1, (px, py, pc)))
            recvs.append(_remote(src, dst.at[4 * px + 2 * py + pc], send_sems, recv_sems, k - 1, (px, py, pc)))
        return local, sends, recvs


def _split_by(parts, refs, count):
    out, off = [], 0
    for p in parts:
        out.append(refs[off:off + count(p)])
        off += count(p)
    return out


def _parts_refs(parts, in_refs, out_refs, sem_refs):
    return zip(parts, _split_by(parts, in_refs, lambda p: len(p.arrays)),
               _split_by(parts, out_refs, lambda p: len(p.arrays)), _split_by(parts, sem_refs, lambda p: len(p.sems())))


def _exchange_start(parts, in_refs, out_refs, sem_refs):
    for part, ins, outs, sems in _parts_refs(parts, in_refs, out_refs, sem_refs):
        local, sends, _ = part.copies(ins, outs, sems)
        for cp in local + sends:
            cp.start()


def _exchange_finish(parts, in_refs, out_refs, sem_refs):
    split = list(_parts_refs(parts, in_refs, out_refs, sem_refs))
    copies = [part.copies(ins, outs, sems) for part, ins, outs, sems in split]
    for _, _, recvs in copies:
        for cp in recvs:
            cp.wait_recv()
    second = [part.forwards(ins, outs, sems) for part, ins, outs, sems in split if hasattr(part, "forwards")]
    for sends, _ in second:
        for cp in sends:
            cp.start()
    for sends, recvs in second:
        for cp in recvs:
            cp.wait_recv()
        for cp in sends:
            cp.wait_send()
    for local, sends, _ in copies:
        for cp in sends:
            cp.wait_send()
        for cp in local:
            cp.wait()


def _exchange_operands(parts):
    return ([a for p in parts for a in p.arrays], [s for p in parts for s in p.out_shape()],
            [s for p in parts for s in p.sems()])


def _set_results(parts, res):
    for part, outs in zip(parts, _split_by(parts, list(res), lambda p: len(p.arrays))):
        part.results = list(outs)


def _exchange_now(name, parts):
    x_in, x_out, x_sem = _exchange_operands(parts)
    n = len(x_in)

    def body(*refs):
        _exchange_start(parts, refs[:n], refs[n:2 * n], refs[2 * n:])
        _exchange_finish(parts, refs[:n], refs[n:2 * n], refs[2 * n:])

    res = pl.pallas_call(body, name=name, in_specs=[_ANY] * n, out_specs=[_ANY] * n, out_shape=x_out,
                         scratch_shapes=x_sem)(*x_in)
    _set_results(parts, res)


_RIDERS = {}


def _call(body, *, name, grid, in_specs, out_specs, out_shape, scratch_shapes=(), compiler_params=None):
    make_parts = _RIDERS.pop(name, None)
    if make_parts is None:
        return pl.pallas_call(body, name=name, grid=grid, in_specs=in_specs, out_specs=out_specs, out_shape=out_shape,
                              scratch_shapes=scratch_shapes, compiler_params=compiler_params)
    parts = make_parts()
    x_in, x_out, x_sem = _exchange_operands(parts)
    n_out, n_scr, n_x = len(out_shape), len(scratch_shapes), len(x_in)

    def run(*args):
        n_in = len(args)

        def hosted(*refs):
            ins, xi = refs[:n_in], refs[n_in:n_in + n_x]
            outs, xo = refs[n_in + n_x:n_in + n_x + n_out], refs[n_in + n_x + n_out:n_in + 2 * n_x + n_out]
            scr, xs = refs[n_in + 2 * n_x + n_out:n_in + 2 * n_x + n_out + n_scr], refs[n_in + 2 * n_x + n_out + n_scr:]
            first = functools.reduce(jnp.logical_and, [pl.program_id(d) == 0 for d in range(len(grid))])
            last = functools.reduce(jnp.logical_and, [pl.program_id(d) == grid[d] - 1 for d in range(len(grid))])

            @pl.when(first)
            def _():
                _exchange_start(parts, xi, xo, xs)

            body(*ins, *outs, *scr)

            @pl.when(last)
            def _():
                _exchange_finish(parts, xi, xo, xs)

        res = pl.pallas_call(
            hosted, name=name, grid=grid, in_specs=list(in_specs) + [_ANY] * n_x,
            out_specs=list(out_specs) + [_ANY] * n_x, out_shape=list(out_shape) + x_out,
            scratch_shapes=list(scratch_shapes) + x_sem, compiler_params=_params(*["arbitrary"] * len(grid)),
        )(*args, *x_in)
        _set_results(parts, res[n_out:])
        return list(res[:n_out])

    return run


GROUPS = {"F1": ["ffn1_w_gate", "ffn1_w_up", "ffn1_w_down"], "MX": ["w_in", "s5_w_glu", "w_out"],
          "F2": ["ffn2_w_gate", "ffn2_w_up", "ffn2_w_down"]}
GROUP_OF = {n: g for g, names in GROUPS.items() for n in names}
FIRST_GATHER = [(0, "ffn1_w_gate"), (0, "ffn1_w_up")]
GATHER_HOSTS = {
    "l0_ffn1_up": [(0, "ffn1_w_down")],
    "l0_ffn1_down": [(0, "w_in"), (0, "s5_w_glu"), (0, "w_out")],
    "l0_mix_win": [(0, "ffn2_w_gate")],
    "l0_mix_attn": [(0, "ffn2_w_up"), (0, "ffn2_w_down")],
    "l0_ffn2_up": [(1, "ffn1_w_gate")],
    "l0_ffn2_down": [(1, "ffn1_w_up")],
    "l1_ffn1_up": [(1, "ffn1_w_down")],
    "l1_ffn1_down": [(1, "w_in"), (1, "s5_w_glu"), (1, "w_out")],
    "l1_mix_win": [(1, "ffn2_w_gate")],
    "l1_mix_attn": [(1, "ffn2_w_up"), (1, "ffn2_w_down")],
}
SCATTER_HOSTS = {
    "l1_ffn2_dact": [(1, "ffn2_w_down")],
    "l1_ffn2_dh": [(1, "ffn2_w_gate")],
    "l1_mix_attndq": [(1, "ffn2_w_up")],
    "l1_mix_attndkv": [(1, "w_out"), (1, "s5_w_glu")],
    "l1_mix_dh1": [(1, "w_in")],
    "l1_ffn1_dact": [(1, "ffn1_w_down")],
    "l1_ffn1_dh": [(1, "ffn1_w_gate")],
    "l0_ffn2_dact": [(1, "ffn1_w_up")],
    "l0_ffn2_dwgu": [(0, "ffn2_w_down")],
    "l0_ffn2_dh": [(0, "ffn2_w_gate")],
    "l0_mix_attndq": [(0, "w_out"), (0, "s5_w_glu"), (0, "ffn2_w_up")],
    "l0_mix_dh1": [(0, "w_in")],
    "l0_ffn1_dact": [(0, "ffn1_w_down")],
    "l0_ffn1_dh": [(0, "ffn1_w_gate")],
}
LAST_SCATTER = [(0, "ffn1_w_up")]
TAIL_HOST = "l0_ffn1_dwgu"
LATE_SCATTER_HOST = "l0_ffn1_dh"


def _unstack_layer(name, st):
    _, r, c = st.shape
    if name in COLUMN_SHARDED:
        return st
    return st.reshape(N_CHIPS * r, c)


def _restack_layer(name, g):
    if name in COLUMN_SHARDED:
        return g
    r, c = g.shape
    return g.reshape(N_CHIPS, r // N_CHIPS, c)


def _adamw_layer(name, layer, w, ga, gb, m, v, bufs):
    _, r, c = w.shape
    tr = _row_tile(r)

    def body(w_ref, ga_ref, gb_ref, m_ref, v_ref, *rest):
        g_out, d_out, m_out, v_out = rest[-4:]
        g = ga_ref[...] + gb_ref[...]
        d, mm, vv = _adamw_rows(w_ref[...], g, m_ref[...], v_ref[...])
        g_out[...] = g
        d_out[...] = d
        m_out[...] = mm
        v_out[...] = vv

    full = pl.BlockSpec((None, tr, c), lambda i: (layer, i, 0))
    flat = pl.BlockSpec((tr, c), lambda i: (i, 0))
    extra = {} if bufs is None else dict(input_output_aliases={5 + k: k for k in range(4)})
    return pl.pallas_call(
        body, name=name, grid=(r // tr,),
        in_specs=[full, flat, flat, full, full] + ([] if bufs is None else [_ANY] * 4),
        out_specs=[full] * 4, out_shape=[_sds(w.shape)] * 4, compiler_params=_params("parallel"), **extra,
    )(w, ga, gb, m, v, *([] if bufs is None else bufs))


def _train_step(x, loss_target, w, m, v):
    ix, iy, _ = _position()
    chip = 2 * ix + iy
    shard = {n: (_permute_in_cols(w[n]) if n == "w_in" else w[n]).astype(bf16) for n in BIG_NAMES}

    gathered = {}

    def gather_parts(keys, extra=()):
        part = _ChipGatherHalvesPart([shard[n][layer] for layer, n in keys] + list(extra))
        gathered.update({key: (part, i) for i, key in enumerate(keys)})
        return [part]

    (first,) = gather_parts(FIRST_GATHER, extra=[w["conv_w"]])
    _exchange_now("gather_first", [first])
    for host, keys in GATHER_HOSTS.items():
        _RIDERS[host] = functools.partial(gather_parts, keys)

    def weight(layer, name):
        part, i = gathered[(layer, name)]
        return _unstack_layer(name, part.results[i])

    small = {n: w[n] for n in SMALL_NAMES}
    small["conv_w"] = first.results[-1].transpose(1, 2, 0, 3).reshape(DEPTH, CONV_WIDTH, D_A)

    grads_full, scattered = {}, {}

    def scatter_parts(keys):
        part = _ChipScatterPart([_restack_layer(n, grads_full[(layer, n)]) for layer, n in keys])
        scattered.update({key: (part, i) for i, key in enumerate(keys)})
        return [part]

    for host, keys in SCATTER_HOSTS.items():
        _RIDERS[host] = functools.partial(scatter_parts, keys)

    partial = {}

    def reduce_chips(keys):
        for layer, n in keys:
            part, i = scattered[(layer, n)]
            p = _sum_stack(f"sum_l{layer}_{n}", part.results[i])
            partial[(layer, n)] = _unpermute_in_cols(p) if n == "w_in" else p

    early = [key for host, keys in SCATTER_HOSTS.items() if host != LATE_SCATTER_HOST for key in keys]
    late = SCATTER_HOSTS[LATE_SCATTER_HOST]
    tail = {}

    def tail_parts():
        reduce_chips(early)
        tail["swap"] = _SiblingSwapPart([partial[k] for k in early])
        tail["small"] = _DevGatherPart(_pack([tail["gsmall"][n] for n in SMALL_NAMES]))
        return [tail["swap"], tail["small"]]

    _RIDERS[TAIL_HOST] = tail_parts
    loss_local, gx = _local_step(x[0], loss_target[0], weight, small, grads_full.__setitem__,
                                 functools.partial(tail.__setitem__, "gsmall"))
    other = dict(zip(early, tail["swap"].results))
    reduce_chips(late)
    last_parts = scatter_parts(LAST_SCATTER) + [_SiblingSwapPart([partial[k] for k in late])]
    _exchange_now("exchange_last", last_parts)
    other.update(zip(late, last_parts[1].results))
    reduce_chips(LAST_SCATTER)
    swap_last = _SiblingSwapPart([partial[k] for k in LAST_SCATTER])
    _exchange_now("swap_last", [swap_last])
    other.update(zip(LAST_SCATTER, swap_last.results))

    small_shapes = [tail["gsmall"][n].shape for n in SMALL_NAMES]
    total = _sum_stack("sum_small", tail["small"].results[0])
    gsm = dict(zip(SMALL_NAMES, _unpack(total, small_shapes)))
    cw = D_A // N_CHIPS
    gsm["conv_w"] = lax.dynamic_slice_in_dim(gsm["conv_w"], chip * cw, cw, axis=2)

    grads, deltas, new_m, new_v = {}, {}, {}, {}
    for n in BIG_NAMES:
        bufs = None
        for layer in range(DEPTH):
            bufs = _adamw_layer(f"adamw_l{layer}_{n}", layer, w[n], partial[(layer, n)], other[(layer, n)], m[n], v[n],
                                bufs)
        grads[n], deltas[n], new_m[n], new_v[n] = bufs
    shapes = [w[n].shape for n in SMALL_NAMES]
    gp = _pack([gsm[n] for n in SMALL_NAMES])
    res = _adamw("adamw_small", _pack([w[n] for n in SMALL_NAMES]), gp, jnp.zeros_like(gp),
                 _pack([m[n] for n in SMALL_NAMES]), _pack([v[n] for n in SMALL_NAMES]))
    for dst, buf in zip((grads, deltas, new_m, new_v), res):
        dst.update(zip(SMALL_NAMES, _unpack(buf, shapes)))

    loss = lax.psum(loss_local, ("x", "y", "c"))
    return (loss, gx[None], *[grads[n] for n in WEIGHT_NAMES], *[deltas[n] for n in WEIGHT_NAMES],
            *[new_m[n] for n in WEIGHT_NAMES], *[new_v[n] for n in WEIGHT_NAMES])


def kernel(x, ffn1_w_gate, ffn1_w_up, ffn1_w_down, ln1_g, ln1_b, w_in, conv_w, conv_b, rg_w_a, rg_b_a, rg_w_x, rg_b_x, rg_lambda, fox_b_f, s5_a_re, s5_a_im, s5_log_dt, s5_b_re, s5_b_im, s5_c_re, s5_c_im, s5_d, s5_w_glu, mix_norm_g, w_out, ln2_g, ln2_b, ffn2_w_gate, ffn2_w_up, ffn2_w_down, ln3_g, ln3_b, loss_target, m_ffn1_w_gate, m_ffn1_w_up, m_ffn1_w_down, m_ln1_g, m_ln1_b, m_w_in, m_conv_w, m_conv_b, m_rg_w_a, m_rg_b_a, m_rg_w_x, m_rg_b_x, m_rg_lambda, m_fox_b_f, m_s5_a_re, m_s5_a_im, m_s5_log_dt, m_s5_b_re, m_s5_b_im, m_s5_c_re, m_s5_c_im, m_s5_d, m_s5_w_glu, m_mix_norm_g, m_w_out, m_ln2_g, m_ln2_b, m_ffn2_w_gate, m_ffn2_w_up, m_ffn2_w_down, m_ln3_g, m_ln3_b, v_ffn1_w_gate, v_ffn1_w_up, v_ffn1_w_down, v_ln1_g, v_ln1_b, v_w_in, v_conv_w, v_conv_b, v_rg_w_a, v_rg_b_a, v_rg_w_x, v_rg_b_x, v_rg_lambda, v_fox_b_f, v_s5_a_re, v_s5_a_im, v_s5_log_dt, v_s5_b_re, v_s5_b_im, v_s5_c_re, v_s5_c_im, v_s5_d, v_s5_w_glu, v_mix_norm_g, v_w_out, v_ln2_g, v_ln2_b, v_ffn2_w_gate, v_ffn2_w_up, v_ffn2_w_down, v_ln3_g, v_ln3_b):
    args = dict(locals())
    w = {n: args[n] for n in WEIGHT_NAMES}
    m = {n: args["m_" + n] for n in WEIGHT_NAMES}
    v = {n: args["v_" + n] for n in WEIGHT_NAMES}
    return _train_step(x, loss_target, w, m, v)
```

```python
import functools
import math

import jax
import jax.numpy as jnp
from jax import lax
from jax.experimental import pallas as pl
from jax.experimental.pallas import tpu as pltpu

f32 = jnp.float32
bf16 = jnp.bfloat16

D_MODEL = 1024
D_FF = 2816
D_A = 384
D_B = 384
D_C = 256
N_HEADS = 6
HEAD_DIM = 64
S5_GROUPS = 16
S5_GROUP = 16
S5_STATE = 64
S5_LANES = S5_GROUPS * S5_STATE
N_IN = 2 * D_A + 3 * D_B + N_HEADS + D_C
F_OFF = 5 * D_A
CU_OFF = F_OFF + 128
N_IN_P = CU_OFF + D_C
CONV_WIDTH = 4
DEPTH = 2
ALPHA = (2 * DEPTH) ** 0.25
LN_EPS = 1e-5
RMS_EPS = 1e-6
RG_C = 8.0
ATT_SCALE = HEAD_DIM ** -0.5
ADAM_LR, ADAM_B1, ADAM_B2, ADAM_EPS, ADAM_WD, ADAM_STEP = 0.001, 0.9, 0.999, 1e-08, 0.01, 10

ROW_TILE = 256
N_CHIPS = 4
N_DEV = 8
MESH = pl.DeviceIdType.MESH

_DN = {
    "nn": (((1,), (0,)), ((), ())),
    "nt": (((1,), (1,)), ((), ())),
    "tn": (((0,), (0,)), ((), ())),
}


def _sds(shape, dtype=f32):
    return jax.ShapeDtypeStruct(shape, dtype)


def _tile(n, target):
    best = None
    for t in range(128, min(n, target) + 1, 128):
        if n % t == 0:
            best = t
    return best or n


def _row_tile(rows, target=256):
    best = None
    for t in range(16, min(rows, target) + 1, 16):
        if rows % t == 0:
            best = t
    return best or rows


def _params(*sem):
    return pltpu.CompilerParams(dimension_semantics=sem)


class _Slabs:
    def __init__(self, x):
        self.x = x


FF_SLAB = D_FF // 4
FFN_ROWS = 1024

def _mm(name, mode, dims, tiles, a_list, b_list, pairs, n_acc, epilogue, outs, extras=(), vecs=(), split_cols=False):
    m, n, k = dims
    tm, tn, tk = tiles
    nk = k // tk
    na, nb, ne, nv, no = len(a_list), len(b_list), len(extras), len(vecs), len(outs)

    def body(*refs):
        a_refs = refs[:na]
        b_refs = refs[na:na + nb]
        e_refs = refs[na + nb:na + nb + ne]
        v_refs = refs[na + nb + ne:na + nb + ne + nv]
        o_refs = refs[na + nb + ne + nv:na + nb + ne + nv + no]
        acc_refs = refs[na + nb + ne + nv + no:]
        a_vals = [r[...].astype(bf16) for r in a_refs]
        b_vals = [r[...].astype(bf16) for r in b_refs]
        products = [(ci, lax.dot_general(a_vals[ai], b_vals[bi], _DN[mode], preferred_element_type=f32))
                    for ai, bi, ci in pairs]

        def finish(accs):
            res = epilogue(accs, [e[...] for e in e_refs], [v[...] for v in v_refs])
            for o, r in zip(o_refs, res):
                o[...] = r.astype(o.dtype)

        if nk == 1:
            accs = [None] * n_acc
            for ci, prod in products:
                accs[ci] = prod if accs[ci] is None else accs[ci] + prod
            finish(accs)
            return
        kk = pl.program_id(2)

        @pl.when(kk == 0)
        def _():
            for acc in acc_refs:
                acc[...] = jnp.zeros_like(acc)

        for ci, prod in products:
            acc_refs[ci][...] += prod

        @pl.when(kk == nk - 1)
        def _():
            finish([acc[...] for acc in acc_refs])

    def a_spec(a):
        if isinstance(a, _Slabs):
            if mode == "tn":
                return pl.BlockSpec((None, tk, tm), lambda i, j, kk: (i, kk, 0))
            return pl.BlockSpec((None, tm, tk), lambda i, j, kk: (kk, i, 0))
        if mode == "tn":
            return pl.BlockSpec((tk, tm), lambda i, j, kk: (kk, i))
        return pl.BlockSpec((tm, tk), lambda i, j, kk: (i, kk))

    def b_spec(b):
        if isinstance(b, _Slabs):
            if mode == "nt":
                return pl.BlockSpec((None, tn, tk), lambda i, j, kk: (kk, j, 0))
            return pl.BlockSpec((None, tk, tn), lambda i, j, kk: (j, kk, 0))
        if mode == "nt":
            return pl.BlockSpec((tn, tk), lambda i, j, kk: (j, kk))
        return pl.BlockSpec((tk, tn), lambda i, j, kk: (kk, j))

    o_spec = pl.BlockSpec((tm, tn), lambda i, j, kk: (i, j))
    o_slab_spec = pl.BlockSpec((None, tm, tn), lambda i, j, kk: (j, i, 0))
    v_spec = pl.BlockSpec((1, tn), lambda i, j, kk: (0, j))
    if split_cols:
        out_specs = [o_slab_spec] * no
        out_shape = [_sds((n // tn, m, tn), dt) for dt in outs]
    else:
        out_specs = [o_spec] * no
        out_shape = [_sds((m, n), dt) for dt in outs]
    raw = lambda t: t.x if isinstance(t, _Slabs) else t
    res = _call(
        body,
        name=name,
        grid=(m // tm, n // tn, nk),
        in_specs=([a_spec(a) for a in a_list] + [b_spec(b) for b in b_list]
                  + [o_slab_spec if isinstance(e, _Slabs) else o_spec for e in extras] + [v_spec] * nv),
        out_specs=out_specs,
        out_shape=out_shape,
        scratch_shapes=[pltpu.VMEM((tm, tn), f32)] * (n_acc if nk > 1 else 0),
        compiler_params=_params("parallel", "parallel", "arbitrary"),
    )(*map(raw, a_list), *map(raw, b_list), *map(raw, extras), *vecs)
    return res


def _sigmoid(x):
    return 0.5 * (jnp.tanh(0.5 * x) + 1.0)


def _layer_norm_rows(r, gamma, beta):
    mu = jnp.mean(r, axis=-1, keepdims=True)
    xc = r - mu
    var = jnp.mean(xc * xc, axis=-1, keepdims=True)
    return xc * lax.rsqrt(var + LN_EPS) * gamma + beta


def _mm_plain(name, mode, a, b, dims, scale=1.0, out_dtype=f32, add=None, add_coef=1.0, tiles=None):
    m, n, k = dims
    tiles = tiles or (_tile(m, 512), _tile(n, 1024), _tile(k, 1024))

    def epilogue(accs, extras, vecs):
        r = accs[0] if scale == 1.0 else accs[0] * scale
        if extras:
            r = r + add_coef * extras[0]
        return [r]

    return _mm(name, mode, dims, tiles, [a], [b], [(0, 0, 0)], 1, epilogue, [out_dtype],
               extras=[] if add is None else [add])[0]


def _ffn_up(name, h, wg, wu):
    s = h.shape[0]

    def epilogue(accs, extras, vecs):
        g, u = accs
        return [g, u, g * _sigmoid(g) * u]

    return _mm(name, "nn", (s, D_FF, D_MODEL), (_tile(s, FFN_ROWS), FF_SLAB, D_MODEL), [h], [_Slabs(wg), _Slabs(wu)],
               [(0, 0, 0), (0, 1, 1)], 2, epilogue, [bf16, bf16, bf16], split_cols=True)


def _mm_ln(name, a, w, resid, gamma, beta, scale, k_slabs=False):
    s, k = (a.shape[1], a.shape[0] * a.shape[2]) if k_slabs else a.shape

    def epilogue(accs, extras, vecs):
        r = ALPHA * extras[0] + scale * accs[0]
        return [r, _layer_norm_rows(r, vecs[0], vecs[1])]

    return _mm(name, "nn", (s, D_MODEL, k), (_tile(s, FFN_ROWS), D_MODEL, FF_SLAB if k_slabs else _tile(k, 1024)),
               [_Slabs(a) if k_slabs else a], [w], [(0, 0, 0)], 1, epilogue, [f32, f32], extras=[resid],
               vecs=[gamma, beta])


def _ffn_dact(name, dr, wd, g, u):
    s = dr.shape[0]

    def epilogue(accs, extras, vecs):
        da = 0.5 * accs[0]
        gg, uu = extras[0].astype(f32), extras[1].astype(f32)
        sg = _sigmoid(gg)
        return [da * uu * (sg * (1.0 + gg * (1.0 - sg))), da * (gg * sg)]

    return _mm(name, "nt", (s, D_FF, D_MODEL), (_tile(s, FFN_ROWS), FF_SLAB, D_MODEL), [dr], [wd],
               [(0, 0, 0)], 1, epilogue, [bf16, bf16], extras=[_Slabs(g), _Slabs(u)], split_cols=True)


def _mm2(name, mode, dims, a0, b0, a1, b1, add=None, add_coef=1.0, separate=False, tiles=None, out_dtype=f32,
         split_cols=False):
    m, n, k = dims
    tiles = tiles or (_tile(m, 512), _tile(n, 1024), _tile(k, 1024))

    def epilogue(accs, extras, vecs):
        if separate:
            return list(accs)
        r = accs[0]
        if extras:
            r = r + add_coef * extras[0]
        return [r]

    a_list = [a0] if a1 is None else [a0, a1]
    b_list = [b0] if b1 is None else [b0, b1]
    pairs = [(0, 0, 0), (len(a_list) - 1, len(b_list) - 1, 1 if separate else 0)]
    return _mm(name, mode, dims, tiles, a_list, b_list, pairs, 2 if separate else 1, epilogue,
               [out_dtype, out_dtype] if separate else [out_dtype], extras=[] if add is None else [add],
               split_cols=split_cols)


def _row_call(name, body, s, ins, params, outs, accs):
    tm = ROW_TILE
    ins = [a if isinstance(a, tuple) else (a, a.shape[1], 0) for a in ins]
    in_specs = [pl.BlockSpec((tm, width), lambda i, cb=cb: (i, cb)) for _, width, cb in ins]
    ins = [a for a, _, _ in ins]
    in_specs += [pl.BlockSpec(p.shape, lambda i, nd=p.ndim: (0,) * nd) for p in params]
    out_specs = [pl.BlockSpec((tm, o.shape[1]), lambda i: (i, 0)) for o in outs]
    out_specs += [pl.BlockSpec(a.shape, lambda i, nd=len(a.shape): (0,) * nd) for a in accs]
    return pl.pallas_call(
        body,
        name=name,
        grid=(s // tm,),
        in_specs=in_specs,
        out_specs=out_specs,
        out_shape=list(outs) + list(accs),
        compiler_params=_params("arbitrary"),
    )(*ins, *params)


def _zero_at_first(refs):
    @pl.when(pl.program_id(0) == 0)
    def _():
        for r in refs:
            r[...] = jnp.zeros_like(r)


def _ln_bwd(name, r, dh, gamma):
    s = r.shape[0]

    def body(r_ref, dh_ref, g_ref, dr_ref, dg_ref, db_ref):
        _zero_at_first([dg_ref, db_ref])
        rr = r_ref[...]
        dy = dh_ref[...]
        mu = jnp.mean(rr, axis=-1, keepdims=True)
        xc = rr - mu
        rstd = lax.rsqrt(jnp.mean(xc * xc, axis=-1, keepdims=True) + LN_EPS)
        xhat = xc * rstd
        dxh = dy * g_ref[...]
        dr_ref[...] = rstd * (dxh - jnp.mean(dxh, axis=-1, keepdims=True)
                              - xhat * jnp.mean(dxh * xhat, axis=-1, keepdims=True))
        dg_ref[...] += jnp.sum(dy * xhat, axis=0, keepdims=True)
        db_ref[...] += jnp.sum(dy, axis=0, keepdims=True)

    return _row_call(name, body, s, [r, dh], [gamma], [_sds((s, D_MODEL))], [_sds((1, D_MODEL)), _sds((1, D_MODEL))])


def _loss_head(name, y, target):
    s = y.shape[0]

    def body(y_ref, t_ref, dy_ref, l_ref):
        _zero_at_first([l_ref])
        e = y_ref[...] - t_ref[...]
        dy_ref[...] = e / D_MODEL
        l_ref[...] += 0.5 * jnp.sum(jnp.mean(e * e, axis=-1, keepdims=True), axis=0, keepdims=True)

    return _row_call(name, body, s, [y, target], [], [_sds((s, D_MODEL))], [_sds((1, 128))])


def _expm1(x):
    series = x * (1.0 + x / 2.0 * (1.0 + x / 3.0 * (1.0 + x / 4.0 * (1.0 + x / 5.0 * (1.0 + x / 6.0 * (1.0 + x / 7.0))))))
    return jnp.where(jnp.abs(x) < 0.25, series, jnp.exp(x) - 1.0)


def _gates_fn(xa, wa, wx, ba, bx, lam, tap_a, tap_x):
    xb = xa.astype(bf16)
    r = jax.nn.sigmoid(jnp.dot(xb, wa, preferred_element_type=f32) + ba + tap_a)
    i = jax.nn.sigmoid(jnp.dot(xb, wx, preferred_element_type=f32) + bx + tap_x)
    log_a = -RG_C * r * jax.nn.softplus(-lam)
    a = jnp.exp(log_a)
    gated = jnp.sqrt(-_expm1(2.0 * log_a)) * (i * xa)
    return a, gated


def _rg_gates(name, xa, wa, wx, ba, bx, lam):
    s = xa.shape[0]

    def body(xa_ref, wa_ref, wx_ref, ba_ref, bx_ref, lam_ref, a_ref, g_ref):
        a, g = _gates_fn(xa_ref[...], wa_ref[...], wx_ref[...], ba_ref[...], bx_ref[...], lam_ref[...], 0.0, 0.0)
        a_ref[...] = a
        g_ref[...] = g

    return _row_call(name, body, s, [xa], [wa, wx, ba, bx, lam], [_sds((s, D_A)), _sds((s, D_A))], [])


def _rg_gates_bwd(name, xa, ga, h_prev, wa, wx, ba, bx, lam):
    s = xa.shape[0]

    def body(xa_ref, ga_ref, hp_ref, wa_ref, wx_ref, ba_ref, bx_ref, lam_ref,
             dxa_ref, dwa_ref, dwx_ref, dba_ref, dbx_ref, dlam_ref):
        _zero_at_first([dwa_ref, dwx_ref, dba_ref, dbx_ref, dlam_ref])
        xa_v = xa_ref[...]
        zero = jnp.zeros((xa_v.shape[0], D_A), f32)
        fn = lambda x, ba_, bx_, lam_, ta, tx: _gates_fn(x, wa_ref[...], wx_ref[...], ba_, bx_, lam_, ta, tx)
        _, vjp = jax.vjp(fn, xa_v, ba_ref[...], bx_ref[...], lam_ref[...], zero, zero)
        gav = ga_ref[...]
        dxa, dba, dbx, dlam, dta, dtx = vjp((gav * hp_ref[...], gav))
        dxa_ref[...] = dxa
        xb = xa_v.astype(bf16)
        dwa_ref[...] += lax.dot_general(xb, dta.astype(bf16), _DN["tn"], preferred_element_type=f32)
        dwx_ref[...] += lax.dot_general(xb, dtx.astype(bf16), _DN["tn"], preferred_element_type=f32)
        dba_ref[...] += dba
        dbx_ref[...] += dbx
        dlam_ref[...] += dlam

    return _row_call(name, body, s, [xa, ga, h_prev], [wa, wx, ba, bx, lam], [_sds((s, D_A))],
                     [_sds((D_A, D_A)), _sds((D_A, D_A)), _sds((1, D_A)), _sds((1, D_A)), _sds((1, D_A))])


def _rms(v, g):
    return v * lax.rsqrt(jnp.mean(v * v, axis=-1, keepdims=True) + RMS_EPS) * g


def _mix_out_fn(ag, ha, ob, hre, him, cu, d, gn, tap_y, tap_gl, wcr, wci, wglu):
    out_a = jax.nn.gelu(ag) * ha
    y = (jnp.dot(hre.astype(bf16), wcr, preferred_element_type=f32)
         + jnp.dot(him.astype(bf16), wci, preferred_element_type=f32) + d * cu + tap_y)
    y2 = jax.nn.gelu(y)
    gl = jnp.dot(y2.astype(bf16), wglu, preferred_element_type=f32) + tap_gl
    out_c = y2 * jax.nn.sigmoid(gl)
    o = jnp.concatenate([_rms(out_a, gn[:, :D_A]), _rms(ob, gn[:, D_A:D_A + D_B]), _rms(out_c, gn[:, D_A + D_B:])],
                        axis=-1)
    return o, y2


def _mix_out(name, ag, ha, ob, hre, him, cu, d, gn, wcr, wci, wglu):
    s = ha.shape[0]

    def body(ag_ref, ha_ref, ob_ref, hre_ref, him_ref, cu_ref, d_ref, gn_ref, wcr_ref, wci_ref, wglu_ref, o_ref):
        o, _ = _mix_out_fn(ag_ref[...], ha_ref[...], ob_ref[...], hre_ref[...], him_ref[...], cu_ref[...], d_ref[...],
                           gn_ref[...], 0.0, 0.0, wcr_ref[...], wci_ref[...], wglu_ref[...])
        o_ref[...] = o.astype(o_ref.dtype)

    return _row_call(name, body, s, [ag, ha, ob, hre, him, cu], [d, gn, wcr, wci, wglu], [_sds((s, D_MODEL), bf16)], [])[0]


def _mix_out_bwd(name, do, ag, ha, ob, hre, him, cu, d, gn, wcr, wci, wglu):
    s = ha.shape[0]

    def body(do_ref, ag_ref, ha_ref, ob_ref, hre_ref, him_ref, cu_ref, d_ref, gn_ref, wcr_ref, wci_ref, wglu_ref,
             dag_ref, dha_ref, dob_ref, dhre_ref, dhim_ref, dcu_ref, dwcr_ref, dwci_ref, dwglu_ref, dd_ref, dgn_ref):
        _zero_at_first([dwcr_ref, dwci_ref, dwglu_ref, dd_ref, dgn_ref])
        tm = ag_ref.shape[0]
        zero = jnp.zeros((tm, D_C), f32)
        hre_v, him_v = hre_ref[...], him_ref[...]
        fn = lambda *a: _mix_out_fn(*a, wcr_ref[...], wci_ref[...], wglu_ref[...])
        _, vjp, y2 = jax.vjp(fn, ag_ref[...], ha_ref[...], ob_ref[...], hre_v, him_v, cu_ref[...], d_ref[...],
                             gn_ref[...], zero, zero, has_aux=True)
        dag, dha, dob, dhre, dhim, dcu, dd, dgn, dy, dgl = vjp(do_ref[...])
        dag_ref[...] = dag
        dha_ref[...] = dha
        dob_ref[...] = dob
        dhre_ref[...] = dhre
        dhim_ref[...] = dhim
        dcu_ref[...] = dcu
        dyb = dy.astype(bf16)
        dwcr_ref[...] += lax.dot_general(hre_v.astype(bf16), dyb, _DN["tn"], preferred_element_type=f32)
        dwci_ref[...] += lax.dot_general(him_v.astype(bf16), dyb, _DN["tn"], preferred_element_type=f32)
        dwglu_ref[...] += lax.dot_general(y2.astype(bf16), dgl.astype(bf16), _DN["tn"], preferred_element_type=f32)
        dd_ref[...] += dd
        dgn_ref[...] += dgn

    outs = [_sds((s, D_A)), _sds((s, D_A)), _sds((s, D_B)), _sds((s, S5_LANES)), _sds((s, S5_LANES)), _sds((s, D_C))]
    accs = [_sds((S5_LANES, D_C)), _sds((S5_LANES, D_C)), _sds((D_C, D_C)), _sds((1, D_C)), _sds((1, D_MODEL))]
    return _row_call(name, body, s, [do, ag, ha, ob, hre, him, cu], [d, gn, wcr, wci, wglu], outs, accs)


def _log_f(name, f, bf):
    s = f[0].shape[0]

    def body(f_ref, b_ref, o_ref):
        o_ref[...] = jax.nn.log_sigmoid(f_ref[...] + b_ref[...])

    return _row_call(name, body, s, [f], [bf], [_sds((s, 128))], [])[0]


def _log_f_bwd(name, dlf, f, bf):
    s = dlf.shape[0]

    def body(dl_ref, f_ref, b_ref, df_ref, db_ref):
        _zero_at_first([db_ref])
        df = dl_ref[...] * jax.nn.sigmoid(-(f_ref[...] + b_ref[...]))
        df_ref[...] = df
        db_ref[...] += jnp.sum(df, axis=0, keepdims=True)

    return _row_call(name, body, s, [dlf, f], [bf], [_sds((s, 128))], [_sds((1, 128))])


def _s5_decay_grad(name, h_re, h_im, g_re, g_im):
    s = g_re.shape[0]
    tm = ROW_TILE

    def body(hr_ref, hi_ref, hhr_ref, hhi_ref, gr_ref, gi_ref, dr_ref, di_ref):
        i = pl.program_id(0)
        _zero_at_first([dr_ref, di_ref])

        def previous(h_ref, halo_ref):
            halo = jnp.where(i == 0, 0.0, halo_ref[...])
            return pltpu.roll(jnp.concatenate([halo, h_ref[...]], axis=0), 1, 0)[8:, :]

        hr, hi, gr, gi = previous(hr_ref, hhr_ref), previous(hi_ref, hhi_ref), gr_ref[...], gi_ref[...]
        dr_ref[...] += jnp.sum(hr * gr + hi * gi, axis=0, keepdims=True)
        di_ref[...] += jnp.sum(hr * gi - hi * gr, axis=0, keepdims=True)

    rows = pl.BlockSpec((tm, S5_LANES), lambda i: (i, 0))
    halo = pl.BlockSpec((8, S5_LANES), lambda i: (jnp.maximum(i * (tm // 8) - 1, 0), 0))
    acc = pl.BlockSpec((1, S5_LANES), lambda i: (0, 0))
    return pl.pallas_call(
        body,
        name=name,
        grid=(s // tm,),
        in_specs=[rows, rows, halo, halo, rows, rows],
        out_specs=[acc, acc],
        out_shape=[_sds((1, S5_LANES)), _sds((1, S5_LANES))],
        compiler_params=_params("arbitrary"),
    )(h_re, h_im, h_re, h_im, g_re, g_im)


def _conv_fwd(name, ax, w, b):
    s = ax.shape[0]
    tm = ROW_TILE

    def body(x_ref, halo_ref, w_ref, b_ref, o_ref):
        i = pl.program_id(0)
        x = x_ref[...]
        halo = jnp.where(i == 0, 0.0, halo_ref[...])
        ext = jnp.concatenate([halo, x], axis=0)
        acc = b_ref[...] + w_ref[3:4, :] * x
        for k in range(CONV_WIDTH - 1):
            acc = acc + w_ref[k:k + 1, :] * pltpu.roll(ext, CONV_WIDTH - 1 - k, 0)[8:, :]
        o_ref[...] = acc

    return pl.pallas_call(
        body,
        name=name,
        grid=(s // tm,),
        in_specs=[pl.BlockSpec((tm, D_A), lambda i: (i, 0)),
                  pl.BlockSpec((8, D_A), lambda i: (jnp.maximum(i * (tm // 8) - 1, 0), 0)),
                  pl.BlockSpec((CONV_WIDTH, D_A), lambda i: (0, 0)),
                  pl.BlockSpec((1, D_A), lambda i: (0, 0))],
        out_specs=pl.BlockSpec((tm, D_A), lambda i: (i, 0)),
        out_shape=_sds((s, D_A)),
        compiler_params=_params("arbitrary"),
    )(ax, ax, w, b)


def _conv_bwd(name, dxa, ax, w):
    s = ax.shape[0]
    tm = ROW_TILE
    nblk = s // tm

    def body(dx_ref, dnext_ref, x_ref, halo_ref, w_ref, dax_ref, dw_ref):
        i = pl.program_id(0)
        _zero_at_first([dw_ref])
        dx = dx_ref[...]
        dnext = jnp.where(i == nblk - 1, 0.0, dnext_ref[...])
        dext = jnp.concatenate([dx, dnext], axis=0)
        x = x_ref[...]
        halo = jnp.where(i == 0, 0.0, halo_ref[...])
        ext = jnp.concatenate([halo, x], axis=0)
        acc = w_ref[3:4, :] * dx
        dw_ref[3:4, :] += jnp.sum(dx * x, axis=0, keepdims=True)
        for k in range(CONV_WIDTH - 1):
            sh = CONV_WIDTH - 1 - k
            acc = acc + w_ref[k:k + 1, :] * pltpu.roll(dext, tm + 8 - sh, 0)[:tm, :]
            dw_ref[k:k + 1, :] += jnp.sum(dx * pltpu.roll(ext, sh, 0)[8:, :], axis=0, keepdims=True)
        dw_ref[4:5, :] += jnp.sum(dx, axis=0, keepdims=True)
        dax_ref[...] = acc

    return pl.pallas_call(
        body,
        name=name,
        grid=(nblk,),
        in_specs=[pl.BlockSpec((tm, D_A), lambda i: (i, 0)),
                  pl.BlockSpec((8, D_A), lambda i: (jnp.minimum((i + 1) * (tm // 8), s // 8 - 1), 0)),
                  pl.BlockSpec((tm, D_A), lambda i: (i, 0)),
                  pl.BlockSpec((8, D_A), lambda i: (jnp.maximum(i * (tm // 8) - 1, 0), 0)),
                  pl.BlockSpec((CONV_WIDTH, D_A), lambda i: (0, 0))],
        out_specs=[pl.BlockSpec((tm, D_A), lambda i: (i, 0)), pl.BlockSpec((8, D_A), lambda i: (0, 0))],
        out_shape=[_sds((s, D_A)), _sds((8, D_A))],
        compiler_params=_params("arbitrary"),
    )(dxa, dxa, ax, ax, w)


SCAN_ROWS = 512


def _row_in_tile(shape):
    return lax.broadcasted_iota(jnp.int32, shape, 0) % 8


def _lin_scan(name, a, b, reverse):
    s, c = a.shape
    t = min(SCAN_ROWS, s)
    nb = s // t

    def body(a_ref, b_ref, h_ref, p_ref, carry_ref):
        @pl.when(pl.program_id(0) == 0)
        def _():
            carry_ref[...] = jnp.zeros_like(carry_ref)

        row = _row_in_tile((t, c))
        p = a_ref[...]
        h = b_ref[...]
        for d in (1, 2, 4):
            keep = (row < 8 - d) if reverse else (row >= d)
            shift = (t - d) if reverse else d
            h = h + jnp.where(keep, p * pltpu.roll(h, shift, 0), 0.0)
            p = jnp.where(keep, p * pltpu.roll(p, shift, 0), p)
        h_ref[...] = h
        p_ref[...] = p
        edge = 0 if reverse else 7

        def tile(k, carry):
            kk = (t // 8 - 1 - k) if reverse else k
            r0 = pl.multiple_of(kk * 8, 8)
            hh = h_ref[pl.ds(r0, 8), :] + p_ref[pl.ds(r0, 8), :] * carry
            h_ref[pl.ds(r0, 8), :] = hh
            return jnp.broadcast_to(hh[edge:edge + 1, :], (8, c))

        carry_ref[...] = lax.fori_loop(0, t // 8, tile, carry_ref[...])

    spec = pl.BlockSpec((t, c), (lambda i: (nb - 1 - i, 0)) if reverse else (lambda i: (i, 0)))
    (out,) = _call(
        body,
        name=name,
        grid=(nb,),
        in_specs=[spec, spec],
        out_specs=[spec],
        out_shape=[_sds((s, c))],
        scratch_shapes=[pltpu.VMEM((t, c), f32), pltpu.VMEM((8, c), f32)],
        compiler_params=_params("arbitrary"),
    )(a, b)
    return out


def _s5_scan(name, b_re, b_im, a_re, a_im, reverse):
    s, c = b_re.shape
    t = min(SCAN_ROWS, s)
    nb = s // t

    def body(br_ref, bi_ref, ar_ref, ai_ref, hr_ref, hi_ref, cr_ref, ci_ref):
        @pl.when(pl.program_id(0) == 0)
        def _():
            cr_ref[...] = jnp.zeros_like(cr_ref)
            ci_ref[...] = jnp.zeros_like(ci_ref)

        ar1, ai1 = ar_ref[...], ai_ref[...]
        pows = [(ar1, ai1)]
        for _ in range(7):
            pr, pi = pows[-1]
            pows.append((pr * ar1 - pi * ai1, pr * ai1 + pi * ar1))
        row8 = lax.broadcasted_iota(jnp.int32, (8, c), 0)
        wr = jnp.zeros((8, c), f32)
        wi = jnp.zeros((8, c), f32)
        for r in range(8):
            pr, pi = pows[(7 - r) if reverse else r]
            wr = jnp.where(row8 == r, pr, wr)
            wi = jnp.where(row8 == r, pi, wi)
        row = _row_in_tile((t, c))
        hr = br_ref[...]
        hi = bi_ref[...]
        for d in (1, 2, 4):
            keep = (row < 8 - d) if reverse else (row >= d)
            shift = (t - d) if reverse else d
            pr, pi = pows[d - 1]
            cr = jnp.where(keep, pr, 0.0)
            ci = jnp.where(keep, pi, 0.0)
            sr = pltpu.roll(hr, shift, 0)
            si = pltpu.roll(hi, shift, 0)
            hr, hi = hr + cr * sr - ci * si, hi + cr * si + ci * sr
        hr_ref[...] = hr
        hi_ref[...] = hi
        edge = 0 if reverse else 7

        def tile(k, carry):
            car_r, car_i = carry
            kk = (t // 8 - 1 - k) if reverse else k
            r0 = pl.multiple_of(kk * 8, 8)
            xr = hr_ref[pl.ds(r0, 8), :] + wr * car_r - wi * car_i
            xi = hi_ref[pl.ds(r0, 8), :] + wr * car_i + wi * car_r
            hr_ref[pl.ds(r0, 8), :] = xr
            hi_ref[pl.ds(r0, 8), :] = xi
            return (jnp.broadcast_to(xr[edge:edge + 1, :], (8, c)), jnp.broadcast_to(xi[edge:edge + 1, :], (8, c)))

        car_r, car_i = lax.fori_loop(0, t // 8, tile, (cr_ref[...], ci_ref[...]))
        cr_ref[...] = car_r
        ci_ref[...] = car_i

    spec = pl.BlockSpec((t, c), (lambda i: (nb - 1 - i, 0)) if reverse else (lambda i: (i, 0)))
    vspec = pl.BlockSpec((1, c), lambda i: (0, 0))
    hr, hi = _call(
        body,
        name=name,
        grid=(nb,),
        in_specs=[spec, spec, vspec, vspec],
        out_specs=[spec, spec],
        out_shape=[_sds((s, c)), _sds((s, c))],
        scratch_shapes=[pltpu.VMEM((8, c), f32), pltpu.VMEM((8, c), f32)],
        compiler_params=_params("arbitrary"),
    )(b_re, b_im, a_re, a_im)
    return hr, hi


ATT_FEAT = 128
ATT_TQ = 1024
ATT_TK = 1024


def _att_tiles(s):
    tq = min(ATT_TQ, s)
    return tq, ATT_TK, tq // ATT_TK


def _keys_le_queries(tk, tq, k0, q0):
    row = lax.broadcasted_iota(jnp.int32, (tk, tq), 0) + k0
    col = lax.broadcasted_iota(jnp.int32, (tk, tq), 1) + q0
    return row <= col


def _attn_fwd_t(name, qt, k_aug, vt):
    h, s, _ = k_aug.shape
    tq, tk, ratio = _att_tiles(s)

    def body(qt_ref, k_ref, vt_ref, o_ref, lse_ref):
        qi = pl.program_id(1)
        qt = qt_ref[...]

        def block(kb, carry, masked):
            m, l, acc = carry
            ks = pl.multiple_of(kb * tk, tk)
            st = jnp.dot(k_ref[pl.ds(ks, tk), :], qt, preferred_element_type=f32)
            if masked:
                st = jnp.where(_keys_le_queries(tk, tq, ks, qi * tq), st, -jnp.inf)
            mn = jnp.maximum(m, jnp.max(st, axis=0, keepdims=True))
            p = jnp.exp(st - mn)
            al = jnp.exp(m - mn)
            l = al * l + jnp.sum(p, axis=0, keepdims=True)
            acc = al * acc + jnp.dot(vt_ref[kb], p.astype(bf16), preferred_element_type=f32)
            return mn, l, acc

        init = (jnp.full((1, tq), -jnp.inf, f32), jnp.zeros((1, tq), f32), jnp.zeros((HEAD_DIM, tq), f32))
        first = lax.fori_loop(0, qi * ratio, lambda kb, c: block(kb, c, False), init)
        m, l, acc = lax.fori_loop(qi * ratio, (qi + 1) * ratio, lambda kb, c: block(kb, c, True), first)
        o_ref[...] = acc / l
        lse_ref[...] = m + jnp.log(l)

    return _call(
        body,
        name=name,
        grid=(h, s // tq),
        in_specs=[pl.BlockSpec((None, None, ATT_FEAT, tq), lambda hh, i: (hh, i, 0, 0)),
                  pl.BlockSpec((None, s, ATT_FEAT), lambda hh, i: (hh, 0, 0)),
                  pl.BlockSpec((None, s // tk, HEAD_DIM, tk), lambda hh, i: (hh, 0, 0, 0))],
        out_specs=[pl.BlockSpec((None, HEAD_DIM, tq), lambda hh, i: (hh, 0, i)),
                   pl.BlockSpec((None, 1, tq), lambda hh, i: (hh, 0, i))],
        out_shape=[_sds((h, HEAD_DIM, s)), _sds((h, 1, s))],
        compiler_params=_params("parallel", "arbitrary"),
    )(qt, k_aug, vt)


def _attn_bwd_dq_t(name, qt, k_aug, v, kt, ot, dot_, lse):
    h, s, _ = k_aug.shape
    tq, tk, ratio = _att_tiles(s)

    def body(qt_ref, k_ref, v_ref, kt_ref, o_ref, do_ref, lse_ref, dq_ref, dl_ref):
        qi = pl.program_id(1)
        qt = qt_ref[...]
        dob = do_ref[...]
        delta = jnp.sum(dob.astype(f32) * o_ref[...], axis=0, keepdims=True)
        lse_v = lse_ref[...]

        def block(kb, carry, masked):
            dq, psum = carry
            ks = pl.multiple_of(kb * tk, tk)
            st = jnp.dot(k_ref[pl.ds(ks, tk), :], qt, preferred_element_type=f32)
            p = jnp.exp(st - lse_v)
            if masked:
                p = jnp.where(_keys_le_queries(tk, tq, ks, qi * tq), p, 0.0)
            dp = jnp.dot(v_ref[pl.ds(ks, tk), :], dob, preferred_element_type=f32)
            ds = p * (dp - delta)
            return (dq + jnp.dot(kt_ref[kb], ds.astype(bf16), preferred_element_type=f32),
                    psum + jnp.sum(p * dp, axis=0, keepdims=True))

        carry = lax.fori_loop(0, qi * ratio, lambda kb, c: block(kb, c, False),
                              (jnp.zeros((HEAD_DIM, tq), f32), jnp.zeros((1, tq), f32)))
        dq, psum = lax.fori_loop(qi * ratio, (qi + 1) * ratio, lambda kb, c: block(kb, c, True), carry)
        dq_ref[...] = dq * ATT_SCALE
        dl_ref[...] = psum

    qspec = pl.BlockSpec((None, HEAD_DIM, tq), lambda hh, i: (hh, 0, i))
    rspec = pl.BlockSpec((None, 1, tq), lambda hh, i: (hh, 0, i))
    return _call(
        body,
        name=name,
        grid=(h, s // tq),
        in_specs=[pl.BlockSpec((None, None, ATT_FEAT, tq), lambda hh, i: (hh, i, 0, 0)),
                  pl.BlockSpec((None, s, ATT_FEAT), lambda hh, i: (hh, 0, 0)),
                  pl.BlockSpec((None, s, HEAD_DIM), lambda hh, i: (hh, 0, 0)),
                  pl.BlockSpec((None, s // tk, HEAD_DIM, tk), lambda hh, i: (hh, 0, 0, 0)),
                  qspec, pl.BlockSpec((None, None, HEAD_DIM, tq), lambda hh, i: (hh, i, 0, 0)), rspec],
        out_specs=[qspec, rspec],
        out_shape=[_sds((h, HEAD_DIM, s)), _sds((h, 1, s))],
        compiler_params=_params("parallel", "arbitrary"),
    )(qt, k_aug, v, kt, ot, dot_, lse)


def _attn_bwd_dkv_t(name, qt_blocks, k_aug, v, qh, do, dot_blocks, lse, delta):
    h, s, _ = k_aug.shape
    tq, tk, ratio = _att_tiles(s)
    nq = s // tq

    def body(qt_ref, k_ref, v_ref, q_ref, do_ref, dot_ref, lse_ref, dl_ref, dk_ref, dv_ref, dck_ref, dsum_ref):
        kj = pl.program_id(1)
        kk = k_ref[...]
        vv = v_ref[...]
        dsum_ref[...] = jnp.zeros_like(dsum_ref)

        def block(qi, carry, masked):
            dk, dv = carry
            qs = pl.multiple_of(qi * tq, tq)
            st = jnp.dot(kk, qt_ref[qi], preferred_element_type=f32)
            p = jnp.exp(st - lse_ref[qi])
            if masked:
                p = jnp.where(_keys_le_queries(tk, tq, kj * tk, qs), p, 0.0)
            dv = dv + jnp.dot(p.astype(bf16), do_ref[pl.ds(qs, tq), :], preferred_element_type=f32)
            dp = jnp.dot(vv, dot_ref[qi], preferred_element_type=f32)
            ds = p * (dp - dl_ref[qi])
            dsum_ref[...] += ds
            dk = dk + jnp.dot(ds.astype(bf16), q_ref[pl.ds(qs, tq), :], preferred_element_type=f32)
            return dk, dv

        first = kj // ratio
        carry = block(first, (jnp.zeros((tk, HEAD_DIM), f32), jnp.zeros((tk, HEAD_DIM), f32)), True)
        dk, dv = lax.fori_loop(first + 1, nq, lambda qi, c: block(qi, c, False), carry)
        dk_ref[...] = dk
        dv_ref[...] = dv
        d = dsum_ref[...]
        hi = d.astype(bf16)
        r1 = d - hi.astype(f32)
        mid = r1.astype(bf16)
        lo = (r1 - mid.astype(f32)).astype(bf16)
        ones = jnp.ones((8, tq), bf16)
        sums = sum(lax.dot_general(ones, part, _DN["nt"], preferred_element_type=f32) for part in (hi, mid, lo))
        dck_ref[...] = -sums[0:1, :]

    full = lambda shape: pl.BlockSpec((None,) + shape, lambda hh, j: (hh,) + (0,) * len(shape))
    kspec = pl.BlockSpec((None, tk, HEAD_DIM), lambda hh, j: (hh, j, 0))
    return _call(
        body,
        name=name,
        grid=(h, s // tk),
        in_specs=[full((nq, ATT_FEAT, tq)),
                  pl.BlockSpec((None, tk, ATT_FEAT), lambda hh, j: (hh, j, 0)),
                  kspec, full((s, HEAD_DIM)), full((s, HEAD_DIM)), full((nq, HEAD_DIM, tq)),
                  full((nq, 1, tq)), full((nq, 1, tq))],
        out_specs=[kspec, kspec, pl.BlockSpec((None, None, 1, tk), lambda hh, j: (hh, j, 0, 0))],
        out_shape=[_sds((h, s, HEAD_DIM)), _sds((h, s, HEAD_DIM)), _sds((h, s // tk, 1, tk))],
        scratch_shapes=[pltpu.VMEM((tk, tq), f32)],
        compiler_params=_params("parallel", "arbitrary"),
    )(qt_blocks, k_aug, v, qh, do, dot_blocks, lse, delta)


C_LANES = 128


def _selections():
    h = jnp.arange(N_HEADS)[:, None, None]
    row = jnp.arange(D_B + 3 * C_LANES)[None, :, None]
    col = jnp.arange(ATT_FEAT)[None, None, :]
    head_col = (row < D_B) & (row // HEAD_DIM == h) & (col == row % HEAD_DIM)

    def c_part(p, lane0):
        return (row == D_B + p * C_LANES + h) & (col == lane0 + p)

    c_q = c_part(0, HEAD_DIM) | c_part(1, HEAD_DIM) | c_part(2, HEAD_DIM)
    c_k = c_part(0, HEAD_DIM + 3) | c_part(1, HEAD_DIM + 3) | c_part(2, HEAD_DIM + 3)
    sel_q = (head_col | c_q).astype(bf16)
    sel_k = head_col.astype(bf16) - c_k.astype(bf16)
    sel_h = head_col[:, :D_B, :HEAD_DIM].astype(bf16)
    lane = jnp.arange(ATT_FEAT)
    ones_q = ((lane >= HEAD_DIM + 3) & (lane < HEAD_DIM + 6)).astype(f32)
    ones_k = ((lane >= HEAD_DIM) & (lane < HEAD_DIM + 3)).astype(f32)
    return dict(sel_qt=sel_q.transpose(0, 2, 1), sel_k=sel_k, sel_h=sel_h, sel_ht=sel_h.transpose(0, 2, 1),
                ones_q=ones_q.reshape(ATT_FEAT, 1), ones_k=ones_k.reshape(1, ATT_FEAT))


def _attn_prep(name, z, c, sel):
    s = z.shape[0]
    tq, tk, ratio = _att_tiles(s)

    def body(q_ref, k_ref, v_ref, c_ref, sqt_ref, sk_ref, sh_ref, sht_ref, oq_ref, ok_ref,
             qt_out, ka_out, kt_out, vt_out, v_out, qh_out):
        cv = c_ref[...]
        hi = cv.astype(bf16)
        r1 = cv - hi.astype(f32)
        mid = r1.astype(bf16)
        lo = (r1 - mid.astype(f32)).astype(bf16)
        qs = (q_ref[...] * ATT_SCALE).astype(bf16)
        kb = k_ref[...].astype(bf16)
        vb = v_ref[...].astype(bf16)
        xq = jnp.concatenate([qs, hi, mid, lo], axis=-1)
        xk = jnp.concatenate([kb, hi, mid, lo], axis=-1)
        for h in range(N_HEADS):
            qt = lax.dot_general(sqt_ref[h], xq, _DN["nt"], preferred_element_type=f32) + oq_ref[...]
            qt_out[h, 0] = qt.astype(bf16)
            ka_out[h] = (jnp.dot(xk, sk_ref[h], preferred_element_type=f32) + ok_ref[...]).astype(bf16)
            kt = lax.dot_general(sht_ref[h], kb, _DN["nt"], preferred_element_type=f32).astype(bf16)
            vt = lax.dot_general(sht_ref[h], vb, _DN["nt"], preferred_element_type=f32).astype(bf16)
            for j in range(ratio):
                kt_out[h, j] = kt[:, j * tk:(j + 1) * tk]
                vt_out[h, j] = vt[:, j * tk:(j + 1) * tk]
            v_out[h] = jnp.dot(vb, sh_ref[h], preferred_element_type=f32).astype(bf16)
            qh_out[h] = jnp.dot(qs, sh_ref[h], preferred_element_type=f32).astype(bf16)

    whole = lambda a: pl.BlockSpec(a.shape, lambda i, nd=a.ndim: (0,) * nd)
    consts = [sel["sel_qt"], sel["sel_k"], sel["sel_h"], sel["sel_ht"], sel["ones_q"], sel["ones_k"]]
    return pl.pallas_call(
        body,
        name=name,
        grid=(s // tq,),
        in_specs=[pl.BlockSpec((tq, D_B), lambda i: (i, 2)), pl.BlockSpec((tq, D_B), lambda i: (i, 3)),
                  pl.BlockSpec((tq, D_B), lambda i: (i, 4)), pl.BlockSpec((tq, C_LANES), lambda i: (i, 0))]
        + [whole(a) for a in consts],
        out_specs=[pl.BlockSpec((N_HEADS, 1, ATT_FEAT, tq), lambda i: (0, i, 0, 0)),
                   pl.BlockSpec((N_HEADS, tq, ATT_FEAT), lambda i: (0, i, 0)),
                   pl.BlockSpec((N_HEADS, ratio, HEAD_DIM, tk), lambda i: (0, i, 0, 0)),
                   pl.BlockSpec((N_HEADS, ratio, HEAD_DIM, tk), lambda i: (0, i, 0, 0)),
                   pl.BlockSpec((N_HEADS, tq, HEAD_DIM), lambda i: (0, i, 0)),
                   pl.BlockSpec((N_HEADS, tq, HEAD_DIM), lambda i: (0, i, 0))],
        out_shape=[_sds((N_HEADS, s // tq, ATT_FEAT, tq), bf16), _sds((N_HEADS, s, ATT_FEAT), bf16),
                   _sds((N_HEADS, s // tk, HEAD_DIM, tk), bf16), _sds((N_HEADS, s // tk, HEAD_DIM, tk), bf16),
                   _sds((N_HEADS, s, HEAD_DIM), bf16), _sds((N_HEADS, s, HEAD_DIM), bf16)],
        compiler_params=_params("parallel"),
    )(z, z, z, c, *consts)


def _attn_do_prep(name, dob, sel):
    s = dob.shape[0]
    tq = _att_tiles(s)[0]

    def body(do_ref, sh_ref, sht_ref, dot_out, do_out):
        db = do_ref[...].astype(bf16)
        for h in range(N_HEADS):
            dot_out[h, 0] = lax.dot_general(sht_ref[h], db, _DN["nt"], preferred_element_type=f32).astype(bf16)
            do_out[h] = jnp.dot(db, sh_ref[h], preferred_element_type=f32).astype(bf16)

    whole = lambda a: pl.BlockSpec(a.shape, lambda i, nd=a.ndim: (0,) * nd)
    return pl.pallas_call(
        body,
        name=name,
        grid=(s // tq,),
        in_specs=[pl.BlockSpec((tq, D_B), lambda i: (i, 0)), whole(sel["sel_h"]), whole(sel["sel_ht"])],
        out_specs=[pl.BlockSpec((N_HEADS, 1, HEAD_DIM, tq), lambda i: (0, i, 0, 0)),
                   pl.BlockSpec((N_HEADS, tq, HEAD_DIM), lambda i: (0, i, 0))],
        out_shape=[_sds((N_HEADS, s // tq, HEAD_DIM, tq), bf16), _sds((N_HEADS, s, HEAD_DIM), bf16)],
        compiler_params=_params("parallel"),
    )(dob, sel["sel_h"], sel["sel_ht"])


def _dz_assemble(name, dax, dag, dqt, dkh, dvh, df, dcu, sel):
    s = dax.shape[0]
    tm = _tile(s, 512)

    def body(dax_ref, dag_ref, dqt_ref, dk_ref, dv_ref, df_ref, dcu_ref, sht_ref, o_ref):
        dq = jnp.zeros((tm, D_B), f32)
        dk = jnp.zeros((tm, D_B), f32)
        dv = jnp.zeros((tm, D_B), f32)
        for h in range(N_HEADS):
            place = sht_ref[h]
            dq = dq + lax.dot_general(dqt_ref[h].astype(bf16), place, _DN["tn"], preferred_element_type=f32)
            dk = dk + jnp.dot(dk_ref[h].astype(bf16), place, preferred_element_type=f32)
            dv = dv + jnp.dot(dv_ref[h].astype(bf16), place, preferred_element_type=f32)
        pieces = [dax_ref[...], dag_ref[...], dq, dk, dv, df_ref[...], dcu_ref[...]]
        off = 0
        for p in pieces:
            o_ref[:, off:off + p.shape[1]] = p.astype(bf16)
            off += p.shape[1]

    rows = lambda c_: pl.BlockSpec((tm, c_), lambda i: (i, 0))
    heads = pl.BlockSpec((N_HEADS, tm, HEAD_DIM), lambda i: (0, i, 0))
    return pl.pallas_call(
        body,
        name=name,
        grid=(s // tm,),
        in_specs=[rows(D_A), rows(D_A), pl.BlockSpec((N_HEADS, HEAD_DIM, tm), lambda i: (0, 0, i)), heads, heads,
                  rows(128), rows(D_C), pl.BlockSpec(sel["sel_ht"].shape, lambda i: (0, 0, 0))],
        out_specs=rows(N_IN_P),
        out_shape=_sds((s, N_IN_P), bf16),
        compiler_params=_params("parallel"),
    )(dax, dag, dqt, dkh, dvh, df, dcu, sel["sel_ht"])


def _s5_disc_fn(are, aim, ldt):
    dt = jnp.exp(ldt)
    er = jnp.exp(are * dt)
    br = er * jnp.cos(aim * dt)
    bi = er * jnp.sin(aim * dt)
    nr = br - 1.0
    den = are * are + aim * aim
    return br, bi, (nr * are + bi * aim) / den, (bi * are - nr * aim) / den


def _s5_disc(name, are, aim, ldt):
    def body(a_ref, b_ref, c_ref, o0, o1, o2, o3):
        r = _s5_disc_fn(a_ref[...], b_ref[...], c_ref[...])
        o0[...], o1[...], o2[...], o3[...] = r

    shp = _sds((S5_GROUPS, S5_STATE))
    return pl.pallas_call(body, name=name, out_shape=[shp] * 4)(are, aim, ldt)


def _s5_disc_bwd(name, are, aim, ldt, cts):
    def body(a_ref, b_ref, c_ref, d0, d1, d2, d3, o0, o1, o2):
        _, vjp = jax.vjp(_s5_disc_fn, a_ref[...], b_ref[...], c_ref[...])
        o0[...], o1[...], o2[...] = vjp((d0[...], d1[...], d2[...], d3[...]))

    shp = _sds((S5_GROUPS, S5_STATE))
    return pl.pallas_call(body, name=name, out_shape=[shp, shp, _sds((S5_GROUPS, 1))])(are, aim, ldt, *cts)


def _adamw_rows(w, g, m, v):
    m = ADAM_B1 * m + (1.0 - ADAM_B1) * g
    v = ADAM_B2 * v + (1.0 - ADAM_B2) * (g * g)
    m_hat = m / (1.0 - ADAM_B1 ** ADAM_STEP)
    v_hat = v / (1.0 - ADAM_B2 ** ADAM_STEP)
    return -ADAM_LR * (m_hat / (jnp.sqrt(v_hat) + ADAM_EPS) + ADAM_WD * w), m, v


def _adamw(name, w, ga, gb, m, v):
    rows, cols = w.shape
    tr = _row_tile(rows)

    def body(w_ref, ga_ref, gb_ref, m_ref, v_ref, g_out, d_out, m_out, v_out):
        g = ga_ref[...] + gb_ref[...]
        d, mm, vv = _adamw_rows(w_ref[...], g, m_ref[...], v_ref[...])
        g_out[...] = g
        d_out[...] = d
        m_out[...] = mm
        v_out[...] = vv

    spec = pl.BlockSpec((tr, cols), lambda i: (i, 0))
    return pl.pallas_call(
        body, name=name, grid=(rows // tr,), in_specs=[spec] * 5, out_specs=[spec] * 4,
        out_shape=[_sds((rows, cols))] * 4, compiler_params=_params("parallel"),
    )(w, ga, gb, m, v)


def _sum_stack(name, st):
    n, rows, cols = st.shape
    tr = _row_tile(rows)

    def body(s_ref, o_ref):
        acc = s_ref[0].astype(f32)
        for j in range(1, n):
            acc = acc + s_ref[j].astype(f32)
        o_ref[...] = acc

    return pl.pallas_call(
        body, name=name, grid=(rows // tr,), in_specs=[pl.BlockSpec((n, tr, cols), lambda i: (0, i, 0))],
        out_specs=pl.BlockSpec((tr, cols), lambda i: (i, 0)), out_shape=_sds((rows, cols)),
        compiler_params=_params("parallel"),
    )(st)


def _block_diag(w):
    h, n, m = w.shape
    return jnp.einsum("hij,hg->higj", w, jnp.eye(h, dtype=w.dtype)).reshape(h * n, h * m)


def _block_diag_part(dense, h):
    n, m = dense.shape[0] // h, dense.shape[1] // h
    return jnp.einsum("higj,hg->hij", dense.reshape(h, n, h, m), jnp.eye(h, dtype=dense.dtype))


def _s5_matrices(coef_re, coef_im, b_re, b_im, c_re, c_im):
    bb_re = coef_re[:, :, None] * b_re - coef_im[:, :, None] * b_im
    bb_im = coef_re[:, :, None] * b_im + coef_im[:, :, None] * b_re
    wb_re = _block_diag(jnp.swapaxes(bb_re, 1, 2))
    wb_im = _block_diag(jnp.swapaxes(bb_im, 1, 2))
    wc_re = _block_diag(jnp.swapaxes(c_re, 1, 2))
    wc_im = _block_diag(jnp.swapaxes(-c_im, 1, 2))
    return wb_re, wb_im, wc_re, wc_im


def _shift_down(t):
    return jnp.concatenate([jnp.zeros((1, t.shape[1]), t.dtype), t[:-1]], axis=0)


def _shift_up(t):
    return jnp.concatenate([t[1:], jnp.zeros((1, t.shape[1]), t.dtype)], axis=0)


def _row(v):
    return v.reshape(1, -1)


def _ffn_fwd(tag, h, get, names, gamma, beta):
    wg, wu = get(names[0]), get(names[1])
    g, u, act = _ffn_up(tag + "_up", h, wg, wu)
    wd = get(names[2])
    r, out = _mm_ln(tag + "_down", act, wd, h, gamma, beta, 0.5, k_slabs=True)
    return out, dict(h=h, g=g, u=u, act=act, r=r, wg=wg, wu=wu, wd=wd)


def _ffn_bwd(tag, dout, sv, names, gamma, put, after_ln=None):
    s = dout.shape[0]
    dr, dgam, dbet = _ln_bwd(tag + "_lnb", sv["r"], dout, gamma)
    if after_ln is not None:
        after_ln(dgam, dbet)
    put(names[2], _mm_plain(tag + "_dwd", "tn", _Slabs(sv["act"]), dr, (D_FF, D_MODEL, s), scale=0.5, out_dtype=bf16,
                            tiles=(FF_SLAB, 1024, _tile(s, 1024))))
    dg, du = _ffn_dact(tag + "_dact", dr, sv["wd"], sv["g"], sv["u"])
    dwg, dwu = _mm2(tag + "_dwgu", "tn", (D_MODEL, D_FF, s), sv["h"], _Slabs(dg), None, _Slabs(du), separate=True,
                    out_dtype=bf16, split_cols=True, tiles=(512, FF_SLAB, _tile(s, 1024)))
    put(names[0], dwg)
    put(names[1], dwu)
    dh = _mm2(tag + "_dh", "nt", (s, D_MODEL, D_FF), _Slabs(dg), _Slabs(sv["wg"]), _Slabs(du), _Slabs(sv["wu"]),
              add=dr, add_coef=ALPHA, tiles=(_tile(s, FFN_ROWS), 1024, FF_SLAB))[0]
    return dh, dgam, dbet


def _mixer_fwd(tag, h1, w):
    s = h1.shape[0]
    z = _mm_plain(tag + "_win", "nn", h1, w["w_in"], (s, N_IN_P, D_MODEL), tiles=(_tile(s, 512), 768, D_MODEL))
    ag, f, cu_cols = (z, D_A, 1), (z, 128, F_OFF // 128), (z, D_C, CU_OFF // D_C)
    cu = z[:, CU_OFF:]
    xa = _conv_fwd(tag + "_conv", z, w["conv_w"], w["conv_b"])
    a, gated = _rg_gates(tag + "_gates", xa, w["rg_wa"], w["rg_wx"], w["rg_ba"], w["rg_bx"], w["rg_lam"])
    ha = _lin_scan(tag + "_rgscan", a, gated, False)
    ones = jnp.ones((s, 128), f32)
    c = _lin_scan(tag + "_cumf", ones, _log_f(tag + "_logf", f, w["fox_bf"]), False)
    att = dict(zip(("qt", "k_aug", "kt", "vt", "v", "qh"), _attn_prep(tag + "_attnprep", z, c, w["sel"])))
    ot, lse = _attn_fwd_t(tag + "_attn", att["qt"], att["k_aug"], att["vt"])
    ob = ot.reshape(D_B, s).T
    bu_re, bu_im = _mm2(tag + "_s5in", "nn", (s, S5_LANES, D_C), cu, w["wb_re"], None, w["wb_im"], separate=True,
                        tiles=(_tile(s, 512), 1024, D_C))
    hre, him = _s5_scan(tag + "_s5scan", bu_re, bu_im, w["abar_re"], w["abar_im"], False)
    o = _mix_out(tag + "_mixout", ag, ha, ob, hre, him, cu_cols, w["s5_d"], w["mix_g"], w["wc_re"], w["wc_im"],
                 w["w_glu"])
    sv = dict(h1=h1, z=z, ag=ag, f=f, cu=cu, cu_cols=cu_cols, xa=xa, a=a, ha=ha, att=att, ot=ot, lse=lse, ob=ob,
              hre=hre, him=him, o=o)
    return o, sv


def _mixer_bwd(tag, do, dr2, sv, w, put):
    s = do.shape[0]
    (dag, dha, dob, dhre, dhim, dcu1, dwcr, dwci, dwglu, dd, dgn) = _mix_out_bwd(
        tag + "_mixoutb", do, sv["ag"], sv["ha"], sv["ob"], sv["hre"], sv["him"], sv["cu_cols"], w["s5_d"], w["mix_g"],
        w["wc_re"], w["wc_im"], w["w_glu"])
    put("s5_w_glu", dwglu.astype(bf16))
    gre, gim = _s5_scan(tag + "_s5scanb", dhre, dhim, w["abar_re"], -w["abar_im"], True)
    dab_re, dab_im = _s5_decay_grad(tag + "_s5dec", sv["hre"], sv["him"], gre, gim)
    dwb_re, dwb_im = _mm2(tag + "_s5dwb", "tn", (D_C, S5_LANES, s), sv["cu"], gre, None, gim, separate=True,
                          tiles=(D_C, 1024, _tile(s, 1024)))
    dcu = _mm2(tag + "_s5dcu", "nt", (s, D_C, S5_LANES), gre, w["wb_re"], gim, w["wb_im"], add=dcu1,
               tiles=(_tile(s, 512), D_C, 1024))[0]
    att = sv["att"]
    tq = _att_tiles(s)[0]
    nt = s // tq
    dot_blocks, doh = _attn_do_prep(tag + "_doprep", dob, w["sel"])
    dqt, delta = _attn_bwd_dq_t(tag + "_attndq", att["qt"], att["k_aug"], att["v"], att["kt"], sv["ot"], dot_blocks,
                                sv["lse"])
    dkh, dvh, dck = _attn_bwd_dkv_t(tag + "_attndkv", att["qt"], att["k_aug"], att["v"], att["qh"], doh, dot_blocks,
                                    sv["lse"].reshape(N_HEADS, nt, 1, tq), delta.reshape(N_HEADS, nt, 1, tq))
    dc = jnp.pad(dck.reshape(N_HEADS, s).T, ((0, 0), (0, 128 - N_HEADS)))
    dlf = _lin_scan(tag + "_cumfb", jnp.ones((s, 128), f32), dc, True)
    df, dbf = _log_f_bwd(tag + "_logfb", dlf, sv["f"], w["fox_bf"])
    ga = _lin_scan(tag + "_rgscanb", _shift_up(sv["a"]), dha, True)
    dxa, dwa, dwx, dba, dbx, dlam = _rg_gates_bwd(tag + "_gatesb", sv["xa"], ga, _shift_down(sv["ha"]), w["rg_wa"],
                                                  w["rg_wx"], w["rg_ba"], w["rg_bx"], w["rg_lam"])
    dax, dconv = _conv_bwd(tag + "_convb", dxa, sv["z"], w["conv_w"])
    dz = _dz_assemble(tag + "_dz", dax, dag, dqt, dkh, dvh, df, dcu, w["sel"])
    put("w_in", _mm_plain(tag + "_dwin", "tn", sv["h1"], dz, (D_MODEL, N_IN_P, s), out_dtype=bf16,
                          tiles=(512, 768, _tile(s, 1024))))
    dh1 = _mm_plain(tag + "_dh1", "nt", dz, w["w_in"], (s, D_MODEL, N_IN_P), add=dr2, add_coef=ALPHA,
                    tiles=(_tile(s, 512), 1024, 768))
    grads = dict(dconv=dconv, dwa=dwa, dwx=dwx, dba=dba, dbx=dbx, dlam=dlam, dbf=dbf,
                 dab_re=dab_re, dab_im=dab_im, dwb_re=dwb_re, dwb_im=dwb_im, dwcr=dwcr, dwci=dwci, dd=dd, dgn=dgn)
    return dh1, grads


SMALL_NAMES = ["ln1_g", "ln1_b", "conv_w", "conv_b", "rg_w_a", "rg_b_a", "rg_w_x", "rg_b_x", "rg_lambda", "fox_b_f",
               "s5_a_re", "s5_a_im", "s5_log_dt", "s5_b_re", "s5_b_im", "s5_c_re", "s5_c_im", "s5_d", "mix_norm_g",
               "ln2_g", "ln2_b", "ln3_g", "ln3_b"]
BIG_NAMES = ["ffn1_w_gate", "ffn1_w_up", "ffn1_w_down", "w_in", "s5_w_glu", "w_out", "ffn2_w_gate", "ffn2_w_up",
             "ffn2_w_down"]


def _local_step(x, target, weight, small, on_grads, on_small):
    h = x
    saved = []
    sel = _selections()
    for l in range(DEPTH):
        get = functools.partial(weight, l)

        sm = {n: small[n][l] for n in SMALL_NAMES}
        abar_re, abar_im, coef_re, coef_im = _s5_disc(f"l{l}_s5disc", sm["s5_a_re"], sm["s5_a_im"],
                                                      sm["s5_log_dt"].reshape(S5_GROUPS, 1))
        mats, mats_vjp = jax.vjp(_s5_matrices, coef_re, coef_im, sm["s5_b_re"], sm["s5_b_im"], sm["s5_c_re"],
                                 sm["s5_c_im"])
        w = dict(
            sel=sel, conv_w=sm["conv_w"], conv_b=_row(sm["conv_b"]),
            rg_wa=_block_diag(sm["rg_w_a"]).astype(bf16), rg_wx=_block_diag(sm["rg_w_x"]).astype(bf16),
            rg_ba=_row(sm["rg_b_a"]), rg_bx=_row(sm["rg_b_x"]), rg_lam=_row(sm["rg_lambda"]),
            fox_bf=jnp.pad(_row(sm["fox_b_f"]), ((0, 0), (0, 128 - N_HEADS))),
            abar_re=_row(abar_re), abar_im=_row(abar_im),
            wb_re=mats[0].astype(bf16), wb_im=mats[1].astype(bf16), wc_re=mats[2].astype(bf16),
            wc_im=mats[3].astype(bf16), s5_d=_row(sm["s5_d"]), mix_g=_row(sm["mix_norm_g"]))
        h1, sv1 = _ffn_fwd(f"l{l}_ffn1", h, get, GROUPS["F1"], _row(sm["ln1_g"]), _row(sm["ln1_b"]))
        w["w_in"], w["w_glu"] = get("w_in"), get("s5_w_glu")
        o, svm = _mixer_fwd(f"l{l}_mix", h1, w)
        w_out = get("w_out")
        r2, h2 = _mm_ln(f"l{l}_wout", o, w_out, h1, _row(sm["ln2_g"]), _row(sm["ln2_b"]), 1.0)
        h3, sv2 = _ffn_fwd(f"l{l}_ffn2", h2, get, GROUPS["F2"], _row(sm["ln3_g"]), _row(sm["ln3_b"]))
        saved.append(dict(sm=sm, w=w, w_out=w_out, sv1=sv1, svm=svm, r2=r2, sv2=sv2, mats_vjp=mats_vjp))
        h = h3

    dh, loss_row = _loss_head("loss_head", h, target)
    s = x.shape[0]
    gsmall = {n: [None] * DEPTH for n in SMALL_NAMES}
    for l in reversed(range(DEPTH)):
        sd = saved[l]
        sm, w = sd["sm"], sd["w"]

        def put(name, grad, l=l):
            on_grads((l, name), grad)

        dh2, dgam, dbet = _ffn_bwd(f"l{l}_ffn2", dh, sd["sv2"], GROUPS["F2"], _row(sm["ln3_g"]), put)
        gsmall["ln3_g"][l], gsmall["ln3_b"][l] = dgam[0], dbet[0]
        dr2, dgam, dbet = _ln_bwd(f"l{l}_ln2b", sd["r2"], dh2, _row(sm["ln2_g"]))
        gsmall["ln2_g"][l], gsmall["ln2_b"][l] = dgam[0], dbet[0]
        put("w_out", _mm_plain(f"l{l}_dwout", "tn", sd["svm"]["o"], dr2, (D_MODEL, D_MODEL, s), out_dtype=bf16))
        do = _mm_plain(f"l{l}_do", "nt", dr2, sd["w_out"], (s, D_MODEL, D_MODEL))
        dh1, g = _mixer_bwd(f"l{l}_mix", do, dr2, sd["svm"], w, put)
        gsmall["conv_w"][l], gsmall["conv_b"][l] = g["dconv"][:CONV_WIDTH], g["dconv"][CONV_WIDTH]
        gsmall["rg_w_a"][l] = _block_diag_part(g["dwa"], N_HEADS)
        gsmall["rg_w_x"][l] = _block_diag_part(g["dwx"], N_HEADS)
        gsmall["rg_b_a"][l], gsmall["rg_b_x"][l], gsmall["rg_lambda"][l] = g["dba"][0], g["dbx"][0], g["dlam"][0]
        gsmall["fox_b_f"][l] = g["dbf"][0, :N_HEADS]
        dcoef_re, dcoef_im, db_re, db_im, dc_re, dc_im = sd["mats_vjp"]((g["dwb_re"], g["dwb_im"], g["dwcr"], g["dwci"]))
        da_re, da_im, dldt = _s5_disc_bwd(
            f"l{l}_s5discb", sm["s5_a_re"], sm["s5_a_im"], sm["s5_log_dt"].reshape(S5_GROUPS, 1),
            (g["dab_re"].reshape(S5_GROUPS, S5_STATE), g["dab_im"].reshape(S5_GROUPS, S5_STATE), dcoef_re, dcoef_im))
        gsmall["s5_a_re"][l], gsmall["s5_a_im"][l], gsmall["s5_log_dt"][l] = da_re, da_im, dldt[:, 0]
        gsmall["s5_b_re"][l], gsmall["s5_b_im"][l], gsmall["s5_c_re"][l], gsmall["s5_c_im"][l] = db_re, db_im, dc_re, dc_im
        gsmall["s5_d"][l], gsmall["mix_norm_g"][l] = g["dd"][0], g["dgn"][0]

        def after_ln(dgam, dbet, l=l):
            gsmall["ln1_g"][l], gsmall["ln1_b"][l] = dgam[0], dbet[0]
            if l == 0:
                on_small({n: jnp.stack(v) for n, v in gsmall.items()})

        dh, _, _ = _ffn_bwd(f"l{l}_ffn1", dh1, sd["sv1"], GROUPS["F1"], _row(sm["ln1_g"]), put, after_ln)
    return loss_row[0, 0], dh


def _position():
    return lax.axis_index("x"), lax.axis_index("y"), lax.axis_index("c")


_ANY = pl.BlockSpec(memory_space=pl.ANY)


COLUMN_SHARDED = ("ffn1_w_gate", "ffn1_w_up", "ffn2_w_gate", "ffn2_w_up")
PACK_QUANTUM = 128 * 256


def _permute_in_cols(w):
    pad = jnp.zeros(w.shape[:-1] + (128 - N_HEADS,), w.dtype)
    return jnp.concatenate([w[..., :F_OFF + N_HEADS], pad, w[..., F_OFF + N_HEADS:]], axis=-1)


def _unpermute_in_cols(w):
    return jnp.concatenate([w[..., :F_OFF + N_HEADS], w[..., CU_OFF:]], axis=-1)


def _pack(arrs):
    flat = jnp.concatenate([a.reshape(-1) for a in arrs])
    pad = -flat.shape[0] % PACK_QUANTUM
    return jnp.pad(flat, (0, pad)).reshape(-1, 128)


def _unpack(buf, shapes):
    flat = buf.reshape(-1)
    out, off = [], 0
    for shp in shapes:
        size = math.prod(shp)
        out.append(flat[off:off + size].reshape(shp))
        off += size
    return out


WEIGHT_NAMES = ["ffn1_w_gate", "ffn1_w_up", "ffn1_w_down", "ln1_g", "ln1_b", "w_in", "conv_w", "conv_b", "rg_w_a",
                "rg_b_a", "rg_w_x", "rg_b_x", "rg_lambda", "fox_b_f", "s5_a_re", "s5_a_im", "s5_log_dt", "s5_b_re",
                "s5_b_im", "s5_c_re", "s5_c_im", "s5_d", "s5_w_glu", "mix_norm_g", "w_out", "ln2_g", "ln2_b",
                "ffn2_w_gate", "ffn2_w_up", "ffn2_w_down", "ln3_g", "ln3_b"]


def _remote(src, dst, send_sems, recv_sems, k, peer):
    return pltpu.make_async_remote_copy(src_ref=src, dst_ref=dst, send_sem=send_sems.at[k], recv_sem=recv_sems.at[k],
                                        device_id=peer, device_id_type=MESH)


class _ChipGatherPart:
    def __init__(self, arrays):
        self.arrays, self.results = list(arrays), None

    def out_shape(self):
        return [_sds((N_CHIPS,) + a.shape, a.dtype) for a in self.arrays]

    def sems(self):
        n = len(self.arrays)
        return [pltpu.SemaphoreType.DMA((3 * n,)), pltpu.SemaphoreType.DMA((3 * n,)), pltpu.SemaphoreType.DMA((n,))]

    def copies(self, ins, outs, sems):
        send_sems, recv_sems, local_sems = sems
        x, y, c = _position()
        me = 2 * x + y
        local, sends, recvs = [], [], []
        for i, (src, dst) in enumerate(zip(ins, outs)):
            local.append(pltpu.make_async_copy(self.mine(src, me), dst.at[me], local_sems.at[i]))
            for r, (px, py) in enumerate([(1 - x, y), (x, 1 - y), (1 - x, 1 - y)]):
                peer = 2 * px + py
                sends.append(_remote(self.theirs(src, peer), dst.at[me], send_sems, recv_sems, 3 * i + r, (px, py, c)))
                recvs.append(_remote(self.mine(src, me), dst.at[peer], send_sems, recv_sems, 3 * i + r, (px, py, c)))
        return local, sends, recvs

    def mine(self, src, me):
        return src

    def theirs(self, src, peer):
        return src


class _ChipGatherHalvesPart(_ChipGatherPart):
    def sems(self):
        n = len(self.arrays)
        return super().sems() + [pltpu.SemaphoreType.DMA((3 * n,)), pltpu.SemaphoreType.DMA((3 * n,))]

    def _half(self, ref, which):
        rows = ref.shape[0] // 2
        return ref.at[pl.ds(which * rows, rows)]

    def copies(self, ins, outs, sems):
        send_sems, recv_sems, local_sems = sems[:3]
        x, y, c = _position()
        me = 2 * x + y
        local, sends, recvs = [], [], []
        for i, (src, dst) in enumerate(zip(ins, outs)):
            local.append(pltpu.make_async_copy(src, dst.at[me], local_sems.at[i]))
            for r, (px, py) in enumerate([(1 - x, y), (x, 1 - y), (1 - x, 1 - y)]):
                sends.append(_remote(self._half(src, c), self._half(dst.at[me], c), send_sems, recv_sems, 3 * i + r,
                                     (px, py, c)))
                recvs.append(_remote(self._half(src, c), self._half(dst.at[2 * px + py], c), send_sems, recv_sems,
                                     3 * i + r, (px, py, c)))
        return local, sends, recvs

    def forwards(self, ins, outs, sems):
        send_sems, recv_sems = sems[3:]
        x, y, c = _position()
        sends, recvs = [], []
        for i, dst in enumerate(outs):
            for r, (px, py) in enumerate([(1 - x, y), (x, 1 - y), (1 - x, 1 - y)]):
                slot = dst.at[2 * px + py]
                sends.append(_remote(self._half(slot, c), self._half(slot, c), send_sems, recv_sems, 3 * i + r,
                                     (x, y, 1 - c)))
                recvs.append(_remote(self._half(slot, c), self._half(slot, 1 - c), send_sems, recv_sems, 3 * i + r,
                                     (x, y, 1 - c)))
        return sends, recvs


class _ChipScatterPart(_ChipGatherPart):
    def out_shape(self):
        return [_sds(a.shape, a.dtype) for a in self.arrays]

    def mine(self, src, me):
        return src.at[me]

    def theirs(self, src, peer):
        return src.at[peer]


class _SiblingSwapPart:
    def __init__(self, arrays):
        self.arrays, self.results = list(arrays), None

    def out_shape(self):
        return [_sds(a.shape, a.dtype) for a in self.arrays]

    def sems(self):
        n = len(self.arrays)
        return [pltpu.SemaphoreType.DMA((n,)), pltpu.SemaphoreType.DMA((n,))]

    def copies(self, ins, outs, sems):
        x, y, c = _position()
        both = [_remote(src, dst, sems[0], sems[1], i, (x, y, 1 - c)) for i, (src, dst) in enumerate(zip(ins, outs))]
        return [], both, both


class _DevGatherPart:
    def __init__(self, array):
        self.arrays, self.results = [array], None

    def out_shape(self):
        return [_sds((N_DEV,) + self.arrays[0].shape, self.arrays[0].dtype)]

    def sems(self):
        return [pltpu.SemaphoreType.DMA((N_DEV - 1,)), pltpu.SemaphoreType.DMA((N_DEV - 1,)),
                pltpu.SemaphoreType.DMA((1,))]

    def copies(self, ins, outs, sems):
        send_sems, recv_sems, local_sems = sems
        (src,), (dst,) = ins, outs
        x, y, c = _position()
        me = 4 * x + 2 * y + c
        local = [pltpu.make_async_copy(src, dst.at[me], local_sems.at[0])]
        sends, recvs = [], []
        for k in range(1, N_DEV):
            px, py, pc = (1 - x if k & 4 else x, 1 - y if k & 2 else y, 1 - c if k & 1 else c)
            sends.append(_remote(src, dst.at[me], send_sems, recv_sems, k - 1, (px, py, pc)))
            recvs.append(_remote(src, dst.at[4 * px + 2 * py + pc], send_sems, recv_sems, k - 1, (px, py, pc)))
        return local, sends, recvs


def _split_by(parts, refs, count):
    out, off = [], 0
    for p in parts:
        out.append(refs[off:off + count(p)])
        off += count(p)
    return out


def _parts_refs(parts, in_refs, out_refs, sem_refs):
    return zip(parts, _split_by(parts, in_refs, lambda p: len(p.arrays)),
               _split_by(parts, out_refs, lambda p: len(p.arrays)), _split_by(parts, sem_refs, lambda p: len(p.sems())))


def _exchange_start(parts, in_refs, out_refs, sem_refs):
    for part, ins, outs, sems in _parts_refs(parts, in_refs, out_refs, sem_refs):
        local, sends, _ = part.copies(ins, outs, sems)
        for cp in local + sends:
            cp.start()


def _exchange_finish(parts, in_refs, out_refs, sem_refs):
    split = list(_parts_refs(parts, in_refs, out_refs, sem_refs))
    copies = [part.copies(ins, outs, sems) for part, ins, outs, sems in split]
    for _, _, recvs in copies:
        for cp in recvs:
            cp.wait_recv()
    second = [part.forwards(ins, outs, sems) for part, ins, outs, sems in split if hasattr(part, "forwards")]
    for sends, _ in second:
        for cp in sends:
            cp.start()
    for sends, recvs in second:
        for cp in recvs:
            cp.wait_recv()
        for cp in sends:
            cp.wait_send()
    for local, sends, _ in copies:
        for cp in sends:
            cp.wait_send()
        for cp in local:
            cp.wait()


def _exchange_operands(parts):
    return ([a for p in parts for a in p.arrays], [s for p in parts for s in p.out_shape()],
            [s for p in parts for s in p.sems()])


def _set_results(parts, res):
    for part, outs in zip(parts, _split_by(parts, list(res), lambda p: len(p.arrays))):
        part.results = list(outs)


def _exchange_now(name, parts):
    x_in, x_out, x_sem = _exchange_operands(parts)
    n = len(x_in)

    def body(*refs):
        _exchange_start(parts, refs[:n], refs[n:2 * n], refs[2 * n:])
        _exchange_finish(parts, refs[:n], refs[n:2 * n], refs[2 * n:])

    res = pl.pallas_call(body, name=name, in_specs=[_ANY] * n, out_specs=[_ANY] * n, out_shape=x_out,
                         scratch_shapes=x_sem)(*x_in)
    _set_results(parts, res)


_RIDERS = {}


def _call(body, *, name, grid, in_specs, out_specs, out_shape, scratch_shapes=(), compiler_params=None):
    make_parts = _RIDERS.pop(name, None)
    if make_parts is None:
        return pl.pallas_call(body, name=name, grid=grid, in_specs=in_specs, out_specs=out_specs, out_shape=out_shape,
                              scratch_shapes=scratch_shapes, compiler_params=compiler_params)
    parts = make_parts()
    x_in, x_out, x_sem = _exchange_operands(parts)
    n_out, n_scr, n_x = len(out_shape), len(scratch_shapes), len(x_in)

    def run(*args):
        n_in = len(args)

        def hosted(*refs):
            ins, xi = refs[:n_in], refs[n_in:n_in + n_x]
            outs, xo = refs[n_in + n_x:n_in + n_x + n_out], refs[n_in + n_x + n_out:n_in + 2 * n_x + n_out]
            scr, xs = refs[n_in + 2 * n_x + n_out:n_in + 2 * n_x + n_out + n_scr], refs[n_in + 2 * n_x + n_out + n_scr:]
            first = functools.reduce(jnp.logical_and, [pl.program_id(d) == 0 for d in range(len(grid))])
            last = functools.reduce(jnp.logical_and, [pl.program_id(d) == grid[d] - 1 for d in range(len(grid))])

            @pl.when(first)
            def _():
                _exchange_start(parts, xi, xo, xs)

            body(*ins, *outs, *scr)

            @pl.when(last)
            def _():
                _exchange_finish(parts, xi, xo, xs)

        res = pl.pallas_call(
            hosted, name=name, grid=grid, in_specs=list(in_specs) + [_ANY] * n_x,
            out_specs=list(out_specs) + [_ANY] * n_x, out_shape=list(out_shape) + x_out,
            scratch_shapes=list(scratch_shapes) + x_sem, compiler_params=_params(*["arbitrary"] * len(grid)),
        )(*args, *x_in)
        _set_results(parts, res[n_out:])
        return list(res[:n_out])

    return run


GROUPS = {"F1": ["ffn1_w_gate", "ffn1_w_up", "ffn1_w_down"], "MX": ["w_in", "s5_w_glu", "w_out"],
          "F2": ["ffn2_w_gate", "ffn2_w_up", "ffn2_w_down"]}
FIRST_GATHER = [(0, "ffn1_w_gate"), (0, "ffn1_w_up")]
GATHER_HOSTS = {
    "l0_ffn1_up": [(0, "ffn1_w_down")],
    "l0_ffn1_down": [(0, "w_in"), (0, "s5_w_glu"), (0, "w_out")],
    "l0_mix_win": [(0, "ffn2_w_gate")],
    "l0_mix_attn": [(0, "ffn2_w_up"), (0, "ffn2_w_down")],
    "l0_ffn2_up": [(1, "ffn1_w_gate")],
    "l0_ffn2_down": [(1, "ffn1_w_up")],
    "l1_ffn1_up": [(1, "ffn1_w_down")],
    "l1_ffn1_down": [(1, "w_in"), (1, "s5_w_glu"), (1, "w_out")],
    "l1_mix_win": [(1, "ffn2_w_gate")],
    "l1_mix_attn": [(1, "ffn2_w_up"), (1, "ffn2_w_down")],
}
SCATTER_HOSTS = {
    "l1_ffn2_dact": [(1, "ffn2_w_down")],
    "l1_ffn2_dh": [(1, "ffn2_w_gate")],
    "l1_mix_attndq": [(1, "ffn2_w_up")],
    "l1_mix_attndkv": [(1, "w_out"), (1, "s5_w_glu")],
    "l1_mix_dh1": [(1, "w_in")],
    "l1_ffn1_dact": [(1, "ffn1_w_down")],
    "l1_ffn1_dh": [(1, "ffn1_w_gate")],
    "l0_ffn2_dact": [(1, "ffn1_w_up")],
    "l0_ffn2_dwgu": [(0, "ffn2_w_down")],
    "l0_ffn2_dh": [(0, "ffn2_w_gate")],
    "l0_mix_attndq": [(0, "w_out"), (0, "s5_w_glu"), (0, "ffn2_w_up")],
    "l0_mix_dh1": [(0, "w_in")],
    "l0_ffn1_dact": [(0, "ffn1_w_down")],
    "l0_ffn1_dh": [(0, "ffn1_w_gate")],
}
LAST_SCATTER = [(0, "ffn1_w_up")]
TAIL_HOST = "l0_ffn1_dwgu"
LATE_SCATTER_HOST = "l0_ffn1_dh"


def _unstack_layer(name, st):
    _, r, c = st.shape
    if name in COLUMN_SHARDED:
        return st
    return st.reshape(N_CHIPS * r, c)


def _restack_layer(name, g):
    if name in COLUMN_SHARDED:
        return g
    r, c = g.shape
    return g.reshape(N_CHIPS, r // N_CHIPS, c)


def _adamw_layer(name, layer, w, ga, gb, m, v, bufs):
    _, r, c = w.shape
    tr = _row_tile(r)

    def body(w_ref, ga_ref, gb_ref, m_ref, v_ref, *rest):
        g_out, d_out, m_out, v_out = rest[-4:]
        g = ga_ref[...] + gb_ref[...]
        d, mm, vv = _adamw_rows(w_ref[...], g, m_ref[...], v_ref[...])
        g_out[...] = g
        d_out[...] = d
        m_out[...] = mm
        v_out[...] = vv

    full = pl.BlockSpec((None, tr, c), lambda i: (layer, i, 0))
    flat = pl.BlockSpec((tr, c), lambda i: (i, 0))
    extra = {} if bufs is None else dict(input_output_aliases={5 + k: k for k in range(4)})
    return pl.pallas_call(
        body, name=name, grid=(r // tr,),
        in_specs=[full, flat, flat, full, full] + ([] if bufs is None else [_ANY] * 4),
        out_specs=[full] * 4, out_shape=[_sds(w.shape)] * 4, compiler_params=_params("parallel"), **extra,
    )(w, ga, gb, m, v, *([] if bufs is None else bufs))


def _train_step(x, loss_target, w, m, v):
    ix, iy, _ = _position()
    chip = 2 * ix + iy
    shard = {n: (_permute_in_cols(w[n]) if n == "w_in" else w[n]).astype(bf16) for n in BIG_NAMES}

    gathered = {}

    def gather_parts(keys, extra=()):
        part = _ChipGatherHalvesPart([shard[n][layer] for layer, n in keys] + list(extra))
        gathered.update({key: (part, i) for i, key in enumerate(keys)})
        return [part]

    (first,) = gather_parts(FIRST_GATHER, extra=[w["conv_w"]])
    _exchange_now("gather_first", [first])
    for host, keys in GATHER_HOSTS.items():
        _RIDERS[host] = functools.partial(gather_parts, keys)

    def weight(layer, name):
        part, i = gathered[(layer, name)]
        return _unstack_layer(name, part.results[i])

    small = {n: w[n] for n in SMALL_NAMES}
    small["conv_w"] = first.results[-1].transpose(1, 2, 0, 3).reshape(DEPTH, CONV_WIDTH, D_A)

    grads_full, scattered = {}, {}

    def scatter_parts(keys):
        part = _ChipScatterPart([_restack_layer(n, grads_full[(layer, n)]) for layer, n in keys])
        scattered.update({key: (part, i) for i, key in enumerate(keys)})
        return [part]

    for host, keys in SCATTER_HOSTS.items():
        _RIDERS[host] = functools.partial(scatter_parts, keys)

    partial = {}

    def reduce_chips(keys):
        for layer, n in keys:
            part, i = scattered[(layer, n)]
            p = _sum_stack(f"sum_l{layer}_{n}", part.results[i])
            partial[(layer, n)] = _unpermute_in_cols(p) if n == "w_in" else p

    early = [key for host, keys in SCATTER_HOSTS.items() if host != LATE_SCATTER_HOST for key in keys]
    late = SCATTER_HOSTS[LATE_SCATTER_HOST]
    tail = {}

    def tail_parts():
        reduce_chips(early)
        tail["swap"] = _SiblingSwapPart([partial[k] for k in early])
        tail["small"] = _DevGatherPart(_pack([tail["gsmall"][n] for n in SMALL_NAMES]))
        return [tail["swap"], tail["small"]]

    _RIDERS[TAIL_HOST] = tail_parts
    loss_local, gx = _local_step(x[0], loss_target[0], weight, small, grads_full.__setitem__,
                                 functools.partial(tail.__setitem__, "gsmall"))
    other = dict(zip(early, tail["swap"].results))
    reduce_chips(late)
    last_parts = scatter_parts(LAST_SCATTER) + [_SiblingSwapPart([partial[k] for k in late])]
    _exchange_now("exchange_last", last_parts)
    other.update(zip(late, last_parts[1].results))
    reduce_chips(LAST_SCATTER)
    swap_last = _SiblingSwapPart([partial[k] for k in LAST_SCATTER])
    _exchange_now("swap_last", [swap_last])
    other.update(zip(LAST_SCATTER, swap_last.results))

    small_shapes = [tail["gsmall"][n].shape for n in SMALL_NAMES]
    total = _sum_stack("sum_small", tail["small"].results[0])
    gsm = dict(zip(SMALL_NAMES, _unpack(total, small_shapes)))
    cw = D_A // N_CHIPS
    gsm["conv_w"] = lax.dynamic_slice_in_dim(gsm["conv_w"], chip * cw, cw, axis=2)

    grads, deltas, new_m, new_v = {}, {}, {}, {}
    for n in BIG_NAMES:
        bufs = None
        for layer in range(DEPTH):
            bufs = _adamw_layer(f"adamw_l{layer}_{n}", layer, w[n], partial[(layer, n)], other[(layer, n)], m[n], v[n],
                                bufs)
        grads[n], deltas[n], new_m[n], new_v[n] = bufs
    shapes = [w[n].shape for n in SMALL_NAMES]
    gp = _pack([gsm[n] for n in SMALL_NAMES])
    res = _adamw("adamw_small", _pack([w[n] for n in SMALL_NAMES]), gp, jnp.zeros_like(gp),
                 _pack([m[n] for n in SMALL_NAMES]), _pack([v[n] for n in SMALL_NAMES]))
    for dst, buf in zip((grads, deltas, new_m, new_v), res):
        dst.update(zip(SMALL_NAMES, _unpack(buf, shapes)))

    loss = lax.psum(loss_local, ("x", "y", "c"))
    return (loss, gx[None], *[grads[n] for n in WEIGHT_NAMES], *[deltas[n] for n in WEIGHT_NAMES],
            *[new_m[n] for n in WEIGHT_NAMES], *[new_v[n] for n in WEIGHT_NAMES])


def kernel(x, ffn1_w_gate, ffn1_w_up, ffn1_w_down, ln1_g, ln1_b, w_in, conv_w, conv_b, rg_w_a, rg_b_a, rg_w_x, rg_b_x, rg_lambda, fox_b_f, s5_a_re, s5_a_im, s5_log_dt, s5_b_re, s5_b_im, s5_c_re, s5_c_im, s5_d, s5_w_glu, mix_norm_g, w_out, ln2_g, ln2_b, ffn2_w_gate, ffn2_w_up, ffn2_w_down, ln3_g, ln3_b, loss_target, m_ffn1_w_gate, m_ffn1_w_up, m_ffn1_w_down, m_ln1_g, m_ln1_b, m_w_in, m_conv_w, m_conv_b, m_rg_w_a, m_rg_b_a, m_rg_w_x, m_rg_b_x, m_rg_lambda, m_fox_b_f, m_s5_a_re, m_s5_a_im, m_s5_log_dt, m_s5_b_re, m_s5_b_im, m_s5_c_re, m_s5_c_im, m_s5_d, m_s5_w_glu, m_mix_norm_g, m_w_out, m_ln2_g, m_ln2_b, m_ffn2_w_gate, m_ffn2_w_up, m_ffn2_w_down, m_ln3_g, m_ln3_b, v_ffn1_w_gate, v_ffn1_w_up, v_ffn1_w_down, v_ln1_g, v_ln1_b, v_w_in, v_conv_w, v_conv_b, v_rg_w_a, v_rg_b_a, v_rg_w_x, v_rg_b_x, v_rg_lambda, v_fox_b_f, v_s5_a_re, v_s5_a_im, v_s5_log_dt, v_s5_b_re, v_s5_b_im, v_s5_c_re, v_s5_c_im, v_s5_d, v_s5_w_glu, v_mix_norm_g, v_w_out, v_ln2_g, v_ln2_b, v_ffn2_w_gate, v_ffn2_w_up, v_ffn2_w_down, v_ln3_g, v_ln3_b):
    args = dict(locals())
    w = {n: args[n] for n in WEIGHT_NAMES}
    m = {n: args["m_" + n] for n in WEIGHT_NAMES}
    v = {n: args["v_" + n] for n in WEIGHT_NAMES}
    return _train_step(x, loss_target, w, m, v)
```

```python
import functools
import math

import jax
import jax.numpy as jnp
from jax import lax
from jax.experimental import pallas as pl
from jax.experimental.pallas import tpu as pltpu

f32 = jnp.float32
bf16 = jnp.bfloat16

D_MODEL = 1024
D_FF = 2816
D_A = 384
D_B = 384
D_C = 256
N_HEADS = 6
HEAD_DIM = 64
S5_GROUPS = 16
S5_GROUP = 16
S5_STATE = 64
S5_LANES = S5_GROUPS * S5_STATE
N_IN = 2 * D_A + 3 * D_B + N_HEADS + D_C
F_OFF = 5 * D_A
CU_OFF = F_OFF + 128
N_IN_P = CU_OFF + D_C
CONV_WIDTH = 4
DEPTH = 2
ALPHA = (2 * DEPTH) ** 0.25
LN_EPS = 1e-5
RMS_EPS = 1e-6
RG_C = 8.0
ATT_SCALE = HEAD_DIM ** -0.5
ADAM_LR, ADAM_B1, ADAM_B2, ADAM_EPS, ADAM_WD, ADAM_STEP = 0.001, 0.9, 0.999, 1e-08, 0.01, 10

ROW_TILE = 256
N_CHIPS = 4
N_DEV = 8
MESH = pl.DeviceIdType.MESH

_DN = {
    "nn": (((1,), (0,)), ((), ())),
    "nt": (((1,), (1,)), ((), ())),
    "tn": (((0,), (0,)), ((), ())),
}


def _sds(shape, dtype=f32):
    return jax.ShapeDtypeStruct(shape, dtype)


def _tile(n, target):
    best = None
    for t in range(128, min(n, target) + 1, 128):
        if n % t == 0:
            best = t
    return best or n


def _row_tile(rows, target=256):
    best = None
    for t in range(16, min(rows, target) + 1, 16):
        if rows % t == 0:
            best = t
    return best or rows


def _params(*sem):
    return pltpu.CompilerParams(dimension_semantics=sem)


class _Slabs:
    def __init__(self, x):
        self.x = x


FF_SLAB = D_FF // 4
FFN_ROWS = 1024

def _mm(name, mode, dims, tiles, a_list, b_list, pairs, n_acc, epilogue, outs, extras=(), vecs=(), split_cols=False):
    m, n, k = dims
    tm, tn, tk = tiles
    nk = k // tk
    na, nb, ne, nv, no = len(a_list), len(b_list), len(extras), len(vecs), len(outs)

    def body(*refs):
        a_refs = refs[:na]
        b_refs = refs[na:na + nb]
        e_refs = refs[na + nb:na + nb + ne]
        v_refs = refs[na + nb + ne:na + nb + ne + nv]
        o_refs = refs[na + nb + ne + nv:na + nb + ne + nv + no]
        acc_refs = refs[na + nb + ne + nv + no:]
        a_vals = [r[...].astype(bf16) for r in a_refs]
        b_vals = [r[...].astype(bf16) for r in b_refs]
        products = [(ci, lax.dot_general(a_vals[ai], b_vals[bi], _DN[mode], preferred_element_type=f32))
                    for ai, bi, ci in pairs]

        def finish(accs):
            res = epilogue(accs, [e[...] for e in e_refs], [v[...] for v in v_refs])
            for o, r in zip(o_refs, res):
                o[...] = r.astype(o.dtype)

        if nk == 1:
            accs = [None] * n_acc
            for ci, prod in products:
                accs[ci] = prod if accs[ci] is None else accs[ci] + prod
            finish(accs)
            return
        kk = pl.program_id(2)

        @pl.when(kk == 0)
        def _():
            for acc in acc_refs:
                acc[...] = jnp.zeros_like(acc)

        for ci, prod in products:
            acc_refs[ci][...] += prod

        @pl.when(kk == nk - 1)
        def _():
            finish([acc[...] for acc in acc_refs])

    def a_spec(a):
        if isinstance(a, _Slabs):
            if mode == "tn":
                return pl.BlockSpec((None, tk, tm), lambda i, j, kk: (i, kk, 0))
            return pl.BlockSpec((None, tm, tk), lambda i, j, kk: (kk, i, 0))
        if mode == "tn":
            return pl.BlockSpec((tk, tm), lambda i, j, kk: (kk, i))
        return pl.BlockSpec((tm, tk), lambda i, j, kk: (i, kk))

    def b_spec(b):
        if isinstance(b, _Slabs):
            if mode == "nt":
                return pl.BlockSpec((None, tn, tk), lambda i, j, kk: (kk, j, 0))
            return pl.BlockSpec((None, tk, tn), lambda i, j, kk: (j, kk, 0))
        if mode == "nt":
            return pl.BlockSpec((tn, tk), lambda i, j, kk: (j, kk))
        return pl.BlockSpec((tk, tn), lambda i, j, kk: (kk, j))

    o_spec = pl.BlockSpec((tm, tn), lambda i, j, kk: (i, j))
    o_slab_spec = pl.BlockSpec((None, tm, tn), lambda i, j, kk: (j, i, 0))
    v_spec = pl.BlockSpec((1, tn), lambda i, j, kk: (0, j))
    if split_cols:
        out_specs = [o_slab_spec] * no
        out_shape = [_sds((n // tn, m, tn), dt) for dt in outs]
    else:
        out_specs = [o_spec] * no
        out_shape = [_sds((m, n), dt) for dt in outs]
    raw = lambda t: t.x if isinstance(t, _Slabs) else t
    res = _call(
        body,
        name=name,
        grid=(m // tm, n // tn, nk),
        in_specs=([a_spec(a) for a in a_list] + [b_spec(b) for b in b_list]
                  + [o_slab_spec if isinstance(e, _Slabs) else o_spec for e in extras] + [v_spec] * nv),
        out_specs=out_specs,
        out_shape=out_shape,
        scratch_shapes=[pltpu.VMEM((tm, tn), f32)] * (n_acc if nk > 1 else 0),
        compiler_params=_params("parallel", "parallel", "arbitrary"),
    )(*map(raw, a_list), *map(raw, b_list), *map(raw, extras), *vecs)
    return res


def _sigmoid(x):
    return 0.5 * (jnp.tanh(0.5 * x) + 1.0)


def _layer_norm_rows(r, gamma, beta):
    mu = jnp.mean(r, axis=-1, keepdims=True)
    xc = r - mu
    var = jnp.mean(xc * xc, axis=-1, keepdims=True)
    return xc * lax.rsqrt(var + LN_EPS) * gamma + beta


def _mm_plain(name, mode, a, b, dims, scale=1.0, out_dtype=f32, add=None, add_coef=1.0, tiles=None):
    m, n, k = dims
    tiles = tiles or (_tile(m, 512), _tile(n, 1024), _tile(k, 1024))

    def epilogue(accs, extras, vecs):
        r = accs[0] if scale == 1.0 else accs[0] * scale
        if extras:
            r = r + add_coef * extras[0]
        return [r]

    return _mm(name, mode, dims, tiles, [a], [b], [(0, 0, 0)], 1, epilogue, [out_dtype],
               extras=[] if add is None else [add])[0]


def _ffn_up(name, h, wg, wu):
    s = h.shape[0]

    def epilogue(accs, extras, vecs):
        g, u = accs
        return [g, u, g * _sigmoid(g) * u]

    return _mm(name, "nt", (s, D_FF, D_MODEL), (_tile(s, FFN_ROWS), FF_SLAB, D_MODEL), [h], [wg, wu],
               [(0, 0, 0), (0, 1, 1)], 2, epilogue, [bf16, bf16, bf16], split_cols=True)


def _mm_ln(name, a, w, resid, gamma, beta, scale, k_slabs=False):
    s, k = (a.shape[1], a.shape[0] * a.shape[2]) if k_slabs else a.shape

    def epilogue(accs, extras, vecs):
        r = ALPHA * extras[0] + scale * accs[0]
        return [r, _layer_norm_rows(r, vecs[0], vecs[1])]

    return _mm(name, "nn", (s, D_MODEL, k), (_tile(s, FFN_ROWS), D_MODEL, FF_SLAB if k_slabs else _tile(k, 1024)),
               [_Slabs(a) if k_slabs else a], [w], [(0, 0, 0)], 1, epilogue, [f32, f32], extras=[resid],
               vecs=[gamma, beta])


def _ffn_dact(name, dr, wd, g, u):
    s = dr.shape[0]

    def epilogue(accs, extras, vecs):
        da = 0.5 * accs[0]
        gg, uu = extras[0].astype(f32), extras[1].astype(f32)
        sg = _sigmoid(gg)
        return [da * uu * (sg * (1.0 + gg * (1.0 - sg))), da * (gg * sg)]

    return _mm(name, "nt", (s, D_FF, D_MODEL), (_tile(s, FFN_ROWS), FF_SLAB, D_MODEL), [dr], [wd],
               [(0, 0, 0)], 1, epilogue, [bf16, bf16], extras=[_Slabs(g), _Slabs(u)], split_cols=True)


def _mm2(name, mode, dims, a0, b0, a1, b1, add=None, add_coef=1.0, separate=False, tiles=None, out_dtype=f32,
         split_cols=False):
    m, n, k = dims
    tiles = tiles or (_tile(m, 512), _tile(n, 1024), _tile(k, 1024))

    def epilogue(accs, extras, vecs):
        if separate:
            return list(accs)
        r = accs[0]
        if extras:
            r = r + add_coef * extras[0]
        return [r]

    a_list = [a0] if a1 is None else [a0, a1]
    b_list = [b0] if b1 is None else [b0, b1]
    pairs = [(0, 0, 0), (len(a_list) - 1, len(b_list) - 1, 1 if separate else 0)]
    return _mm(name, mode, dims, tiles, a_list, b_list, pairs, 2 if separate else 1, epilogue,
               [out_dtype, out_dtype] if separate else [out_dtype], extras=[] if add is None else [add],
               split_cols=split_cols)


def _row_call(name, body, s, ins, params, outs, accs):
    tm = ROW_TILE
    ins = [a if isinstance(a, tuple) else (a, a.shape[1], 0) for a in ins]
    in_specs = [pl.BlockSpec((tm, width), lambda i, cb=cb: (i, cb)) for _, width, cb in ins]
    ins = [a for a, _, _ in ins]
    in_specs += [pl.BlockSpec(p.shape, lambda i, nd=p.ndim: (0,) * nd) for p in params]
    out_specs = [pl.BlockSpec((tm, o.shape[1]), lambda i: (i, 0)) for o in outs]
    out_specs += [pl.BlockSpec(a.shape, lambda i, nd=len(a.shape): (0,) * nd) for a in accs]
    return pl.pallas_call(
        body,
        name=name,
        grid=(s // tm,),
        in_specs=in_specs,
        out_specs=out_specs,
        out_shape=list(outs) + list(accs),
        compiler_params=_params("arbitrary"),
    )(*ins, *params)


def _zero_at_first(refs):
    @pl.when(pl.program_id(0) == 0)
    def _():
        for r in refs:
            r[...] = jnp.zeros_like(r)


def _ln_bwd(name, r, dh, gamma):
    s = r.shape[0]

    def body(r_ref, dh_ref, g_ref, dr_ref, dg_ref, db_ref):
        _zero_at_first([dg_ref, db_ref])
        rr = r_ref[...]
        dy = dh_ref[...]
        mu = jnp.mean(rr, axis=-1, keepdims=True)
        xc = rr - mu
        rstd = lax.rsqrt(jnp.mean(xc * xc, axis=-1, keepdims=True) + LN_EPS)
        xhat = xc * rstd
        dxh = dy * g_ref[...]
        dr_ref[...] = rstd * (dxh - jnp.mean(dxh, axis=-1, keepdims=True)
                              - xhat * jnp.mean(dxh * xhat, axis=-1, keepdims=True))
        dg_ref[...] += jnp.sum(dy * xhat, axis=0, keepdims=True)
        db_ref[...] += jnp.sum(dy, axis=0, keepdims=True)

    return _row_call(name, body, s, [r, dh], [gamma], [_sds((s, D_MODEL))], [_sds((1, D_MODEL)), _sds((1, D_MODEL))])


def _loss_head(name, y, target):
    s = y.shape[0]

    def body(y_ref, t_ref, dy_ref, l_ref):
        _zero_at_first([l_ref])
        e = y_ref[...] - t_ref[...]
        dy_ref[...] = e / D_MODEL
        l_ref[...] += 0.5 * jnp.sum(jnp.mean(e * e, axis=-1, keepdims=True), axis=0, keepdims=True)

    return _row_call(name, body, s, [y, target], [], [_sds((s, D_MODEL))], [_sds((1, 128))])


def _expm1(x):
    series = x * (1.0 + x / 2.0 * (1.0 + x / 3.0 * (1.0 + x / 4.0 * (1.0 + x / 5.0 * (1.0 + x / 6.0 * (1.0 + x / 7.0))))))
    return jnp.where(jnp.abs(x) < 0.25, series, jnp.exp(x) - 1.0)


def _gates_fn(xa, wa, wx, ba, bx, lam, tap_a, tap_x):
    xb = xa.astype(bf16)
    r = jax.nn.sigmoid(jnp.dot(xb, wa, preferred_element_type=f32) + ba + tap_a)
    i = jax.nn.sigmoid(jnp.dot(xb, wx, preferred_element_type=f32) + bx + tap_x)
    log_a = -RG_C * r * jax.nn.softplus(-lam)
    a = jnp.exp(log_a)
    gated = jnp.sqrt(-_expm1(2.0 * log_a)) * (i * xa)
    return a, gated


def _rg_gates(name, xa, wa, wx, ba, bx, lam):
    s = xa.shape[0]

    def body(xa_ref, wa_ref, wx_ref, ba_ref, bx_ref, lam_ref, a_ref, g_ref):
        a, g = _gates_fn(xa_ref[...], wa_ref[...], wx_ref[...], ba_ref[...], bx_ref[...], lam_ref[...], 0.0, 0.0)
        a_ref[...] = a
        g_ref[...] = g

    return _row_call(name, body, s, [xa], [wa, wx, ba, bx, lam], [_sds((s, D_A)), _sds((s, D_A))], [])


def _rg_gates_bwd(name, xa, ga, h_prev, wa, wx, ba, bx, lam):
    s = xa.shape[0]

    def body(xa_ref, ga_ref, hp_ref, wa_ref, wx_ref, ba_ref, bx_ref, lam_ref,
             dxa_ref, dwa_ref, dwx_ref, dba_ref, dbx_ref, dlam_ref):
        _zero_at_first([dwa_ref, dwx_ref, dba_ref, dbx_ref, dlam_ref])
        xa_v = xa_ref[...]
        zero = jnp.zeros((xa_v.shape[0], D_A), f32)
        fn = lambda x, ba_, bx_, lam_, ta, tx: _gates_fn(x, wa_ref[...], wx_ref[...], ba_, bx_, lam_, ta, tx)
        _, vjp = jax.vjp(fn, xa_v, ba_ref[...], bx_ref[...], lam_ref[...], zero, zero)
        gav = ga_ref[...]
        dxa, dba, dbx, dlam, dta, dtx = vjp((gav * hp_ref[...], gav))
        dxa_ref[...] = dxa
        xb = xa_v.astype(bf16)
        dwa_ref[...] += lax.dot_general(xb, dta.astype(bf16), _DN["tn"], preferred_element_type=f32)
        dwx_ref[...] += lax.dot_general(xb, dtx.astype(bf16), _DN["tn"], preferred_element_type=f32)
        dba_ref[...] += dba
        dbx_ref[...] += dbx
        dlam_ref[...] += dlam

    return _row_call(name, body, s, [xa, ga, h_prev], [wa, wx, ba, bx, lam], [_sds((s, D_A))],
                     [_sds((D_A, D_A)), _sds((D_A, D_A)), _sds((1, D_A)), _sds((1, D_A)), _sds((1, D_A))])


def _rms(v, g):
    return v * lax.rsqrt(jnp.mean(v * v, axis=-1, keepdims=True) + RMS_EPS) * g


def _mix_out_fn(ag, ha, ob, hre, him, cu, d, gn, tap_y, tap_gl, wcr, wci, wglu):
    out_a = jax.nn.gelu(ag) * ha
    y = (jnp.dot(hre.astype(bf16), wcr, preferred_element_type=f32)
         + jnp.dot(him.astype(bf16), wci, preferred_element_type=f32) + d * cu + tap_y)
    y2 = jax.nn.gelu(y)
    gl = jnp.dot(y2.astype(bf16), wglu, preferred_element_type=f32) + tap_gl
    out_c = y2 * jax.nn.sigmoid(gl)
    o = jnp.concatenate([_rms(out_a, gn[:, :D_A]), _rms(ob, gn[:, D_A:D_A + D_B]), _rms(out_c, gn[:, D_A + D_B:])],
                        axis=-1)
    return o, y2


def _mix_out(name, ag, ha, ob, hre, him, cu, d, gn, wcr, wci, wglu):
    s = ha.shape[0]

    def body(ag_ref, ha_ref, ob_ref, hre_ref, him_ref, cu_ref, d_ref, gn_ref, wcr_ref, wci_ref, wglu_ref, o_ref):
        o, _ = _mix_out_fn(ag_ref[...], ha_ref[...], ob_ref[...], hre_ref[...], him_ref[...], cu_ref[...], d_ref[...],
                           gn_ref[...], 0.0, 0.0, wcr_ref[...], wci_ref[...], wglu_ref[...])
        o_ref[...] = o.astype(o_ref.dtype)

    return _row_call(name, body, s, [ag, ha, ob, hre, him, cu], [d, gn, wcr, wci, wglu], [_sds((s, D_MODEL), bf16)], [])[0]


def _mix_out_bwd(name, do, ag, ha, ob, hre, him, cu, d, gn, wcr, wci, wglu):
    s = ha.shape[0]

    def body(do_ref, ag_ref, ha_ref, ob_ref, hre_ref, him_ref, cu_ref, d_ref, gn_ref, wcr_ref, wci_ref, wglu_ref,
             dag_ref, dha_ref, dob_ref, dhre_ref, dhim_ref, dcu_ref, dwcr_ref, dwci_ref, dwglu_ref, dd_ref, dgn_ref):
        _zero_at_first([dwcr_ref, dwci_ref, dwglu_ref, dd_ref, dgn_ref])
        tm = ag_ref.shape[0]
        zero = jnp.zeros((tm, D_C), f32)
        hre_v, him_v = hre_ref[...], him_ref[...]
        fn = lambda *a: _mix_out_fn(*a, wcr_ref[...], wci_ref[...], wglu_ref[...])
        _, vjp, y2 = jax.vjp(fn, ag_ref[...], ha_ref[...], ob_ref[...], hre_v, him_v, cu_ref[...], d_ref[...],
                             gn_ref[...], zero, zero, has_aux=True)
        dag, dha, dob, dhre, dhim, dcu, dd, dgn, dy, dgl = vjp(do_ref[...])
        dag_ref[...] = dag
        dha_ref[...] = dha
        dob_ref[...] = dob
        dhre_ref[...] = dhre
        dhim_ref[...] = dhim
        dcu_ref[...] = dcu
        dyb = dy.astype(bf16)
        dwcr_ref[...] += lax.dot_general(hre_v.astype(bf16), dyb, _DN["tn"], preferred_element_type=f32)
        dwci_ref[...] += lax.dot_general(him_v.astype(bf16), dyb, _DN["tn"], preferred_element_type=f32)
        dwglu_ref[...] += lax.dot_general(y2.astype(bf16), dgl.astype(bf16), _DN["tn"], preferred_element_type=f32)
        dd_ref[...] += dd
        dgn_ref[...] += dgn

    outs = [_sds((s, D_A)), _sds((s, D_A)), _sds((s, D_B)), _sds((s, S5_LANES)), _sds((s, S5_LANES)), _sds((s, D_C))]
    accs = [_sds((S5_LANES, D_C)), _sds((S5_LANES, D_C)), _sds((D_C, D_C)), _sds((1, D_C)), _sds((1, D_MODEL))]
    return _row_call(name, body, s, [do, ag, ha, ob, hre, him, cu], [d, gn, wcr, wci, wglu], outs, accs)


def _log_f(name, f, bf):
    s = f[0].shape[0]

    def body(f_ref, b_ref, o_ref):
        o_ref[...] = jax.nn.log_sigmoid(f_ref[...] + b_ref[...])

    return _row_call(name, body, s, [f], [bf], [_sds((s, 128))], [])[0]


def _log_f_bwd(name, dlf, f, bf):
    s = dlf.shape[0]

    def body(dl_ref, f_ref, b_ref, df_ref, db_ref):
        _zero_at_first([db_ref])
        df = dl_ref[...] * jax.nn.sigmoid(-(f_ref[...] + b_ref[...]))
        df_ref[...] = df
        db_ref[...] += jnp.sum(df, axis=0, keepdims=True)

    return _row_call(name, body, s, [dlf, f], [bf], [_sds((s, 128))], [_sds((1, 128))])


def _s5_decay_grad(name, h_re, h_im, g_re, g_im):
    s = g_re.shape[0]
    tm = ROW_TILE

    def body(hr_ref, hi_ref, hhr_ref, hhi_ref, gr_ref, gi_ref, dr_ref, di_ref):
        i = pl.program_id(0)
        _zero_at_first([dr_ref, di_ref])

        def previous(h_ref, halo_ref):
            halo = jnp.where(i == 0, 0.0, halo_ref[...])
            return pltpu.roll(jnp.concatenate([halo, h_ref[...]], axis=0), 1, 0)[8:, :]

        hr, hi, gr, gi = previous(hr_ref, hhr_ref), previous(hi_ref, hhi_ref), gr_ref[...], gi_ref[...]
        dr_ref[...] += jnp.sum(hr * gr + hi * gi, axis=0, keepdims=True)
        di_ref[...] += jnp.sum(hr * gi - hi * gr, axis=0, keepdims=True)

    rows = pl.BlockSpec((tm, S5_LANES), lambda i: (i, 0))
    halo = pl.BlockSpec((8, S5_LANES), lambda i: (jnp.maximum(i * (tm // 8) - 1, 0), 0))
    acc = pl.BlockSpec((1, S5_LANES), lambda i: (0, 0))
    return pl.pallas_call(
        body,
        name=name,
        grid=(s // tm,),
        in_specs=[rows, rows, halo, halo, rows, rows],
        out_specs=[acc, acc],
        out_shape=[_sds((1, S5_LANES)), _sds((1, S5_LANES))],
        compiler_params=_params("arbitrary"),
    )(h_re, h_im, h_re, h_im, g_re, g_im)


def _conv_fwd(name, ax, w, b):
    s = ax.shape[0]
    tm = ROW_TILE

    def body(x_ref, halo_ref, w_ref, b_ref, o_ref):
        i = pl.program_id(0)
        x = x_ref[...]
        halo = jnp.where(i == 0, 0.0, halo_ref[...])
        ext = jnp.concatenate([halo, x], axis=0)
        acc = b_ref[...] + w_ref[3:4, :] * x
        for k in range(CONV_WIDTH - 1):
            acc = acc + w_ref[k:k + 1, :] * pltpu.roll(ext, CONV_WIDTH - 1 - k, 0)[8:, :]
        o_ref[...] = acc

    return pl.pallas_call(
        body,
        name=name,
        grid=(s // tm,),
        in_specs=[pl.BlockSpec((tm, D_A), lambda i: (i, 0)),
                  pl.BlockSpec((8, D_A), lambda i: (jnp.maximum(i * (tm // 8) - 1, 0), 0)),
                  pl.BlockSpec((CONV_WIDTH, D_A), lambda i: (0, 0)),
                  pl.BlockSpec((1, D_A), lambda i: (0, 0))],
        out_specs=pl.BlockSpec((tm, D_A), lambda i: (i, 0)),
        out_shape=_sds((s, D_A)),
        compiler_params=_params("arbitrary"),
    )(ax, ax, w, b)


def _conv_bwd(name, dxa, ax, w):
    s = ax.shape[0]
    tm = ROW_TILE
    nblk = s // tm

    def body(dx_ref, dnext_ref, x_ref, halo_ref, w_ref, dax_ref, dw_ref):
        i = pl.program_id(0)
        _zero_at_first([dw_ref])
        dx = dx_ref[...]
        dnext = jnp.where(i == nblk - 1, 0.0, dnext_ref[...])
        dext = jnp.concatenate([dx, dnext], axis=0)
        x = x_ref[...]
        halo = jnp.where(i == 0, 0.0, halo_ref[...])
        ext = jnp.concatenate([halo, x], axis=0)
        acc = w_ref[3:4, :] * dx
        dw_ref[3:4, :] += jnp.sum(dx * x, axis=0, keepdims=True)
        for k in range(CONV_WIDTH - 1):
            sh = CONV_WIDTH - 1 - k
            acc = acc + w_ref[k:k + 1, :] * pltpu.roll(dext, tm + 8 - sh, 0)[:tm, :]
            dw_ref[k:k + 1, :] += jnp.sum(dx * pltpu.roll(ext, sh, 0)[8:, :], axis=0, keepdims=True)
        dw_ref[4:5, :] += jnp.sum(dx, axis=0, keepdims=True)
        dax_ref[...] = acc

    return pl.pallas_call(
        body,
        name=name,
        grid=(nblk,),
        in_specs=[pl.BlockSpec((tm, D_A), lambda i: (i, 0)),
                  pl.BlockSpec((8, D_A), lambda i: (jnp.minimum((i + 1) * (tm // 8), s // 8 - 1), 0)),
                  pl.BlockSpec((tm, D_A), lambda i: (i, 0)),
                  pl.BlockSpec((8, D_A), lambda i: (jnp.maximum(i * (tm // 8) - 1, 0), 0)),
                  pl.BlockSpec((CONV_WIDTH, D_A), lambda i: (0, 0))],
        out_specs=[pl.BlockSpec((tm, D_A), lambda i: (i, 0)), pl.BlockSpec((8, D_A), lambda i: (0, 0))],
        out_shape=[_sds((s, D_A)), _sds((8, D_A))],
        compiler_params=_params("arbitrary"),
    )(dxa, dxa, ax, ax, w)


SCAN_ROWS = 512


def _row_in_tile(shape):
    return lax.broadcasted_iota(jnp.int32, shape, 0) % 8


def _lin_scan(name, a, b, reverse):
    s, c = a.shape
    t = min(SCAN_ROWS, s)
    nb = s // t

    def body(a_ref, b_ref, h_ref, p_ref, carry_ref):
        @pl.when(pl.program_id(0) == 0)
        def _():
            carry_ref[...] = jnp.zeros_like(carry_ref)

        row = _row_in_tile((t, c))
        p = a_ref[...]
        h = b_ref[...]
        for d in (1, 2, 4):
            keep = (row < 8 - d) if reverse else (row >= d)
            shift = (t - d) if reverse else d
            h = h + jnp.where(keep, p * pltpu.roll(h, shift, 0), 0.0)
            p = jnp.where(keep, p * pltpu.roll(p, shift, 0), p)
        h_ref[...] = h
        p_ref[...] = p
        edge = 0 if reverse else 7

        def tile(k, carry):
            kk = (t // 8 - 1 - k) if reverse else k
            r0 = pl.multiple_of(kk * 8, 8)
            hh = h_ref[pl.ds(r0, 8), :] + p_ref[pl.ds(r0, 8), :] * carry
            h_ref[pl.ds(r0, 8), :] = hh
            return jnp.broadcast_to(hh[edge:edge + 1, :], (8, c))

        carry_ref[...] = lax.fori_loop(0, t // 8, tile, carry_ref[...])

    spec = pl.BlockSpec((t, c), (lambda i: (nb - 1 - i, 0)) if reverse else (lambda i: (i, 0)))
    (out,) = _call(
        body,
        name=name,
        grid=(nb,),
        in_specs=[spec, spec],
        out_specs=[spec],
        out_shape=[_sds((s, c))],
        scratch_shapes=[pltpu.VMEM((t, c), f32), pltpu.VMEM((8, c), f32)],
        compiler_params=_params("arbitrary"),
    )(a, b)
    return out


def _s5_scan(name, b_re, b_im, a_re, a_im, reverse):
    s, c = b_re.shape
    t = min(SCAN_ROWS, s)
    nb = s // t

    def body(br_ref, bi_ref, ar_ref, ai_ref, hr_ref, hi_ref, cr_ref, ci_ref):
        @pl.when(pl.program_id(0) == 0)
        def _():
            cr_ref[...] = jnp.zeros_like(cr_ref)
            ci_ref[...] = jnp.zeros_like(ci_ref)

        ar1, ai1 = ar_ref[...], ai_ref[...]
        pows = [(ar1, ai1)]
        for _ in range(7):
            pr, pi = pows[-1]
            pows.append((pr * ar1 - pi * ai1, pr * ai1 + pi * ar1))
        row8 = lax.broadcasted_iota(jnp.int32, (8, c), 0)
        wr = jnp.zeros((8, c), f32)
        wi = jnp.zeros((8, c), f32)
        for r in range(8):
            pr, pi = pows[(7 - r) if reverse else r]
            wr = jnp.where(row8 == r, pr, wr)
            wi = jnp.where(row8 == r, pi, wi)
        row = _row_in_tile((t, c))
        hr = br_ref[...]
        hi = bi_ref[...]
        for d in (1, 2, 4):
            keep = (row < 8 - d) if reverse else (row >= d)
            shift = (t - d) if reverse else d
            pr, pi = pows[d - 1]
            cr = jnp.where(keep, pr, 0.0)
            ci = jnp.where(keep, pi, 0.0)
            sr = pltpu.roll(hr, shift, 0)
            si = pltpu.roll(hi, shift, 0)
            hr, hi = hr + cr * sr - ci * si, hi + cr * si + ci * sr
        hr_ref[...] = hr
        hi_ref[...] = hi
        edge = 0 if reverse else 7

        def tile(k, carry):
            car_r, car_i = carry
            kk = (t // 8 - 1 - k) if reverse else k
            r0 = pl.multiple_of(kk * 8, 8)
            xr = hr_ref[pl.ds(r0, 8), :] + wr * car_r - wi * car_i
            xi = hi_ref[pl.ds(r0, 8), :] + wr * car_i + wi * car_r
            hr_ref[pl.ds(r0, 8), :] = xr
            hi_ref[pl.ds(r0, 8), :] = xi
            return (jnp.broadcast_to(xr[edge:edge + 1, :], (8, c)), jnp.broadcast_to(xi[edge:edge + 1, :], (8, c)))

        car_r, car_i = lax.fori_loop(0, t // 8, tile, (cr_ref[...], ci_ref[...]))
        cr_ref[...] = car_r
        ci_ref[...] = car_i

    spec = pl.BlockSpec((t, c), (lambda i: (nb - 1 - i, 0)) if reverse else (lambda i: (i, 0)))
    vspec = pl.BlockSpec((1, c), lambda i: (0, 0))
    hr, hi = _call(
        body,
        name=name,
        grid=(nb,),
        in_specs=[spec, spec, vspec, vspec],
        out_specs=[spec, spec],
        out_shape=[_sds((s, c)), _sds((s, c))],
        scratch_shapes=[pltpu.VMEM((8, c), f32), pltpu.VMEM((8, c), f32)],
        compiler_params=_params("arbitrary"),
    )(b_re, b_im, a_re, a_im)
    return hr, hi


ATT_FEAT = 128
ATT_TQ = 1024
ATT_TK = 1024
ATT_TK_KEY_SIDE = 512


def _att_tiles(s, key_side=False):
    tq = min(ATT_TQ, s)
    tk = min(ATT_TK_KEY_SIDE if key_side else ATT_TK, tq)
    return tq, tk, tq // tk


def _keys_le_queries(tk, tq, k0, q0):
    row = lax.broadcasted_iota(jnp.int32, (tk, tq), 0) + k0
    col = lax.broadcasted_iota(jnp.int32, (tk, tq), 1) + q0
    return row <= col


def _attn_fwd_t(name, qt, k_aug, vt):
    h, s, _ = k_aug.shape
    tq, tk, ratio = _att_tiles(s)

    def body(qt_ref, k_ref, vt_ref, o_ref, lse_ref):
        qi = pl.program_id(1)
        qt = qt_ref[...]

        def block(kb, carry, masked):
            m, l, acc = carry
            ks = pl.multiple_of(kb * tk, tk)
            st = jnp.dot(k_ref[pl.ds(ks, tk), :], qt, preferred_element_type=f32)
            if masked:
                st = jnp.where(_keys_le_queries(tk, tq, ks, qi * tq), st, -jnp.inf)
            mn = jnp.maximum(m, jnp.max(st, axis=0, keepdims=True))
            p = jnp.exp(st - mn)
            al = jnp.exp(m - mn)
            l = al * l + jnp.sum(p, axis=0, keepdims=True)
            acc = al * acc + jnp.dot(vt_ref[kb], p.astype(bf16), preferred_element_type=f32)
            return mn, l, acc

        init = (jnp.full((1, tq), -jnp.inf, f32), jnp.zeros((1, tq), f32), jnp.zeros((HEAD_DIM, tq), f32))
        first = lax.fori_loop(0, qi * ratio, lambda kb, c: block(kb, c, False), init)
        m, l, acc = lax.fori_loop(qi * ratio, (qi + 1) * ratio, lambda kb, c: block(kb, c, True), first)
        o_ref[...] = acc / l
        lse_ref[...] = m + jnp.log(l)

    return _call(
        body,
        name=name,
        grid=(h, s // tq),
        in_specs=[pl.BlockSpec((None, None, ATT_FEAT, tq), lambda hh, i: (hh, i, 0, 0)),
                  pl.BlockSpec((None, s, ATT_FEAT), lambda hh, i: (hh, 0, 0)),
                  pl.BlockSpec((None, s // tk, HEAD_DIM, tk), lambda hh, i: (hh, 0, 0, 0))],
        out_specs=[pl.BlockSpec((None, HEAD_DIM, tq), lambda hh, i: (hh, 0, i)),
                   pl.BlockSpec((None, 1, tq), lambda hh, i: (hh, 0, i))],
        out_shape=[_sds((h, HEAD_DIM, s)), _sds((h, 1, s))],
        compiler_params=_params("parallel", "arbitrary"),
    )(qt, k_aug, vt)


def _attn_bwd_dq_t(name, qt, k_aug, v, kt, ot, dot_, lse):
    h, s, _ = k_aug.shape
    tq, tk, ratio = _att_tiles(s)

    def body(qt_ref, k_ref, v_ref, kt_ref, o_ref, do_ref, lse_ref, dq_ref, dl_ref):
        qi = pl.program_id(1)
        qt = qt_ref[...]
        dob = do_ref[...]
        delta = jnp.sum(dob.astype(f32) * o_ref[...], axis=0, keepdims=True)
        lse_v = lse_ref[...]

        def block(kb, carry, masked):
            dq, psum = carry
            ks = pl.multiple_of(kb * tk, tk)
            st = jnp.dot(k_ref[pl.ds(ks, tk), :], qt, preferred_element_type=f32)
            p = jnp.exp(st - lse_v)
            if masked:
                p = jnp.where(_keys_le_queries(tk, tq, ks, qi * tq), p, 0.0)
            dp = jnp.dot(v_ref[pl.ds(ks, tk), :], dob, preferred_element_type=f32)
            ds = p * (dp - delta)
            return (dq + jnp.dot(kt_ref[kb], ds.astype(bf16), preferred_element_type=f32),
                    psum + jnp.sum(p * dp, axis=0, keepdims=True))

        carry = lax.fori_loop(0, qi * ratio, lambda kb, c: block(kb, c, False),
                              (jnp.zeros((HEAD_DIM, tq), f32), jnp.zeros((1, tq), f32)))
        dq, psum = lax.fori_loop(qi * ratio, (qi + 1) * ratio, lambda kb, c: block(kb, c, True), carry)
        dq_ref[...] = dq * ATT_SCALE
        dl_ref[...] = psum

    qspec = pl.BlockSpec((None, HEAD_DIM, tq), lambda hh, i: (hh, 0, i))
    rspec = pl.BlockSpec((None, 1, tq), lambda hh, i: (hh, 0, i))
    return _call(
        body,
        name=name,
        grid=(h, s // tq),
        in_specs=[pl.BlockSpec((None, None, ATT_FEAT, tq), lambda hh, i: (hh, i, 0, 0)),
                  pl.BlockSpec((None, s, ATT_FEAT), lambda hh, i: (hh, 0, 0)),
                  pl.BlockSpec((None, s, HEAD_DIM), lambda hh, i: (hh, 0, 0)),
                  pl.BlockSpec((None, s // tk, HEAD_DIM, tk), lambda hh, i: (hh, 0, 0, 0)),
                  qspec, pl.BlockSpec((None, None, HEAD_DIM, tq), lambda hh, i: (hh, i, 0, 0)), rspec],
        out_specs=[qspec, rspec],
        out_shape=[_sds((h, HEAD_DIM, s)), _sds((h, 1, s))],
        compiler_params=_params("parallel", "arbitrary"),
    )(qt, k_aug, v, kt, ot, dot_, lse)


def _attn_bwd_dkv_t(name, qt_blocks, k_aug, v, qh, do, dot_blocks, lse, delta):
    h, s, _ = k_aug.shape
    tq, tk, ratio = _att_tiles(s, key_side=True)
    nq = s // tq

    def body(qt_ref, k_ref, v_ref, q_ref, do_ref, dot_ref, lse_ref, dl_ref, dk_ref, dv_ref, dck_ref, dsum_ref):
        kj = pl.program_id(1)
        kk = k_ref[...]
        vv = v_ref[...]
        dsum_ref[...] = jnp.zeros_like(dsum_ref)

        def block(qi, carry, masked):
            dk, dv = carry
            qs = pl.multiple_of(qi * tq, tq)
            st = jnp.dot(kk, qt_ref[qi], preferred_element_type=f32)
            p = jnp.exp(st - lse_ref[qi])
            if masked:
                p = jnp.where(_keys_le_queries(tk, tq, kj * tk, qs), p, 0.0)
            dv = dv + jnp.dot(p.astype(bf16), do_ref[pl.ds(qs, tq), :], preferred_element_type=f32)
            dp = jnp.dot(vv, dot_ref[qi], preferred_element_type=f32)
            ds = p * (dp - dl_ref[qi])
            dsum_ref[...] += ds
            dk = dk + jnp.dot(ds.astype(bf16), q_ref[pl.ds(qs, tq), :], preferred_element_type=f32)
            return dk, dv

        first = kj // ratio
        carry = block(first, (jnp.zeros((tk, HEAD_DIM), f32), jnp.zeros((tk, HEAD_DIM), f32)), True)
        dk, dv = lax.fori_loop(first + 1, nq, lambda qi, c: block(qi, c, False), carry)
        dk_ref[...] = dk
        dv_ref[...] = dv
        d = dsum_ref[...]
        hi = d.astype(bf16)
        r1 = d - hi.astype(f32)
        mid = r1.astype(bf16)
        lo = (r1 - mid.astype(f32)).astype(bf16)
        ones = jnp.ones((8, tq), bf16)
        sums = sum(lax.dot_general(ones, part, _DN["nt"], preferred_element_type=f32) for part in (hi, mid, lo))
        dck_ref[...] = -sums[0:1, :]

    full = lambda shape: pl.BlockSpec((None,) + shape, lambda hh, j: (hh,) + (0,) * len(shape))
    kspec = pl.BlockSpec((None, tk, HEAD_DIM), lambda hh, j: (hh, j, 0))
    return _call(
        body,
        name=name,
        grid=(h, s // tk),
        in_specs=[full((nq, ATT_FEAT, tq)),
                  pl.BlockSpec((None, tk, ATT_FEAT), lambda hh, j: (hh, j, 0)),
                  kspec, full((s, HEAD_DIM)), full((s, HEAD_DIM)), full((nq, HEAD_DIM, tq)),
                  full((nq, 1, tq)), full((nq, 1, tq))],
        out_specs=[kspec, kspec, pl.BlockSpec((None, None, 1, tk), lambda hh, j: (hh, j, 0, 0))],
        out_shape=[_sds((h, s, HEAD_DIM)), _sds((h, s, HEAD_DIM)), _sds((h, s // tk, 1, tk))],
        scratch_shapes=[pltpu.VMEM((tk, tq), f32)],
        compiler_params=_params("parallel", "arbitrary"),
    )(qt_blocks, k_aug, v, qh, do, dot_blocks, lse, delta)


C_LANES = 128


def _selections():
    h = jnp.arange(N_HEADS)[:, None, None]
    row = jnp.arange(D_B + 3 * C_LANES)[None, :, None]
    col = jnp.arange(ATT_FEAT)[None, None, :]
    head_col = (row < D_B) & (row // HEAD_DIM == h) & (col == row % HEAD_DIM)

    def c_part(p, lane0):
        return (row == D_B + p * C_LANES + h) & (col == lane0 + p)

    c_q = c_part(0, HEAD_DIM) | c_part(1, HEAD_DIM) | c_part(2, HEAD_DIM)
    c_k = c_part(0, HEAD_DIM + 3) | c_part(1, HEAD_DIM + 3) | c_part(2, HEAD_DIM + 3)
    sel_q = (head_col | c_q).astype(bf16)
    sel_k = head_col.astype(bf16) - c_k.astype(bf16)
    sel_h = head_col[:, :D_B, :HEAD_DIM].astype(bf16)
    lane = jnp.arange(ATT_FEAT)
    ones_q = ((lane >= HEAD_DIM + 3) & (lane < HEAD_DIM + 6)).astype(f32)
    ones_k = ((lane >= HEAD_DIM) & (lane < HEAD_DIM + 3)).astype(f32)
    return dict(sel_qt=sel_q.transpose(0, 2, 1), sel_k=sel_k, sel_h=sel_h, sel_ht=sel_h.transpose(0, 2, 1),
                ones_q=ones_q.reshape(ATT_FEAT, 1), ones_k=ones_k.reshape(1, ATT_FEAT))


def _attn_prep(name, z, c, sel):
    s = z.shape[0]
    tq, tk, ratio = _att_tiles(s)

    def body(q_ref, k_ref, v_ref, c_ref, sqt_ref, sk_ref, sh_ref, sht_ref, oq_ref, ok_ref,
             qt_out, ka_out, kt_out, vt_out, v_out, qh_out):
        cv = c_ref[...]
        hi = cv.astype(bf16)
        r1 = cv - hi.astype(f32)
        mid = r1.astype(bf16)
        lo = (r1 - mid.astype(f32)).astype(bf16)
        qs = (q_ref[...] * ATT_SCALE).astype(bf16)
        kb = k_ref[...].astype(bf16)
        vb = v_ref[...].astype(bf16)
        xq = jnp.concatenate([qs, hi, mid, lo], axis=-1)
        xk = jnp.concatenate([kb, hi, mid, lo], axis=-1)
        for h in range(N_HEADS):
            qt = lax.dot_general(sqt_ref[h], xq, _DN["nt"], preferred_element_type=f32) + oq_ref[...]
            qt_out[h, 0] = qt.astype(bf16)
            ka_out[h] = (jnp.dot(xk, sk_ref[h], preferred_element_type=f32) + ok_ref[...]).astype(bf16)
            kt = lax.dot_general(sht_ref[h], kb, _DN["nt"], preferred_element_type=f32).astype(bf16)
            vt = lax.dot_general(sht_ref[h], vb, _DN["nt"], preferred_element_type=f32).astype(bf16)
            for j in range(ratio):
                kt_out[h, j] = kt[:, j * tk:(j + 1) * tk]
                vt_out[h, j] = vt[:, j * tk:(j + 1) * tk]
            v_out[h] = jnp.dot(vb, sh_ref[h], preferred_element_type=f32).astype(bf16)
            qh_out[h] = jnp.dot(qs, sh_ref[h], preferred_element_type=f32).astype(bf16)

    whole = lambda a: pl.BlockSpec(a.shape, lambda i, nd=a.ndim: (0,) * nd)
    consts = [sel["sel_qt"], sel["sel_k"], sel["sel_h"], sel["sel_ht"], sel["ones_q"], sel["ones_k"]]
    return pl.pallas_call(
        body,
        name=name,
        grid=(s // tq,),
        in_specs=[pl.BlockSpec((tq, D_B), lambda i: (i, 2)), pl.BlockSpec((tq, D_B), lambda i: (i, 3)),
                  pl.BlockSpec((tq, D_B), lambda i: (i, 4)), pl.BlockSpec((tq, C_LANES), lambda i: (i, 0))]
        + [whole(a) for a in consts],
        out_specs=[pl.BlockSpec((N_HEADS, 1, ATT_FEAT, tq), lambda i: (0, i, 0, 0)),
                   pl.BlockSpec((N_HEADS, tq, ATT_FEAT), lambda i: (0, i, 0)),
                   pl.BlockSpec((N_HEADS, ratio, HEAD_DIM, tk), lambda i: (0, i, 0, 0)),
                   pl.BlockSpec((N_HEADS, ratio, HEAD_DIM, tk), lambda i: (0, i, 0, 0)),
                   pl.BlockSpec((N_HEADS, tq, HEAD_DIM), lambda i: (0, i, 0)),
                   pl.BlockSpec((N_HEADS, tq, HEAD_DIM), lambda i: (0, i, 0))],
        out_shape=[_sds((N_HEADS, s // tq, ATT_FEAT, tq), bf16), _sds((N_HEADS, s, ATT_FEAT), bf16),
                   _sds((N_HEADS, s // tk, HEAD_DIM, tk), bf16), _sds((N_HEADS, s // tk, HEAD_DIM, tk), bf16),
                   _sds((N_HEADS, s, HEAD_DIM), bf16), _sds((N_HEADS, s, HEAD_DIM), bf16)],
        compiler_params=_params("parallel"),
    )(z, z, z, c, *consts)


def _attn_do_prep(name, dob, sel):
    s = dob.shape[0]
    tq = _att_tiles(s)[0]

    def body(do_ref, sh_ref, sht_ref, dot_out, do_out):
        db = do_ref[...].astype(bf16)
        for h in range(N_HEADS):
            dot_out[h, 0] = lax.dot_general(sht_ref[h], db, _DN["nt"], preferred_element_type=f32).astype(bf16)
            do_out[h] = jnp.dot(db, sh_ref[h], preferred_element_type=f32).astype(bf16)

    whole = lambda a: pl.BlockSpec(a.shape, lambda i, nd=a.ndim: (0,) * nd)
    return pl.pallas_call(
        body,
        name=name,
        grid=(s // tq,),
        in_specs=[pl.BlockSpec((tq, D_B), lambda i: (i, 0)), whole(sel["sel_h"]), whole(sel["sel_ht"])],
        out_specs=[pl.BlockSpec((N_HEADS, 1, HEAD_DIM, tq), lambda i: (0, i, 0, 0)),
                   pl.BlockSpec((N_HEADS, tq, HEAD_DIM), lambda i: (0, i, 0))],
        out_shape=[_sds((N_HEADS, s // tq, HEAD_DIM, tq), bf16), _sds((N_HEADS, s, HEAD_DIM), bf16)],
        compiler_params=_params("parallel"),
    )(dob, sel["sel_h"], sel["sel_ht"])


def _dz_assemble(name, dax, dag, dqt, dkh, dvh, df, dcu, sel):
    s = dax.shape[0]
    tm = _tile(s, 512)

    def body(dax_ref, dag_ref, dqt_ref, dk_ref, dv_ref, df_ref, dcu_ref, sht_ref, o_ref):
        dq = jnp.zeros((tm, D_B), f32)
        dk = jnp.zeros((tm, D_B), f32)
        dv = jnp.zeros((tm, D_B), f32)
        for h in range(N_HEADS):
            place = sht_ref[h]
            dq = dq + lax.dot_general(dqt_ref[h].astype(bf16), place, _DN["tn"], preferred_element_type=f32)
            dk = dk + jnp.dot(dk_ref[h].astype(bf16), place, preferred_element_type=f32)
            dv = dv + jnp.dot(dv_ref[h].astype(bf16), place, preferred_element_type=f32)
        pieces = [dax_ref[...], dag_ref[...], dq, dk, dv, df_ref[...], dcu_ref[...]]
        off = 0
        for p in pieces:
            o_ref[:, off:off + p.shape[1]] = p.astype(bf16)
            off += p.shape[1]

    rows = lambda c_: pl.BlockSpec((tm, c_), lambda i: (i, 0))
    heads = pl.BlockSpec((N_HEADS, tm, HEAD_DIM), lambda i: (0, i, 0))
    return pl.pallas_call(
        body,
        name=name,
        grid=(s // tm,),
        in_specs=[rows(D_A), rows(D_A), pl.BlockSpec((N_HEADS, HEAD_DIM, tm), lambda i: (0, 0, i)), heads, heads,
                  rows(128), rows(D_C), pl.BlockSpec(sel["sel_ht"].shape, lambda i: (0, 0, 0))],
        out_specs=rows(N_IN_P),
        out_shape=_sds((s, N_IN_P), bf16),
        compiler_params=_params("parallel"),
    )(dax, dag, dqt, dkh, dvh, df, dcu, sel["sel_ht"])


def _s5_disc_fn(are, aim, ldt):
    dt = jnp.exp(ldt)
    er = jnp.exp(are * dt)
    br = er * jnp.cos(aim * dt)
    bi = er * jnp.sin(aim * dt)
    nr = br - 1.0
    den = are * are + aim * aim
    return br, bi, (nr * are + bi * aim) / den, (bi * are - nr * aim) / den


def _s5_disc(name, are, aim, ldt):
    def body(a_ref, b_ref, c_ref, o0, o1, o2, o3):
        r = _s5_disc_fn(a_ref[...], b_ref[...], c_ref[...])
        o0[...], o1[...], o2[...], o3[...] = r

    shp = _sds((S5_GROUPS, S5_STATE))
    return pl.pallas_call(body, name=name, out_shape=[shp] * 4)(are, aim, ldt)


def _s5_disc_bwd(name, are, aim, ldt, cts):
    def body(a_ref, b_ref, c_ref, d0, d1, d2, d3, o0, o1, o2):
        _, vjp = jax.vjp(_s5_disc_fn, a_ref[...], b_ref[...], c_ref[...])
        o0[...], o1[...], o2[...] = vjp((d0[...], d1[...], d2[...], d3[...]))

    shp = _sds((S5_GROUPS, S5_STATE))
    return pl.pallas_call(body, name=name, out_shape=[shp, shp, _sds((S5_GROUPS, 1))])(are, aim, ldt, *cts)


def _adamw_rows(w, g, m, v):
    m = ADAM_B1 * m + (1.0 - ADAM_B1) * g
    v = ADAM_B2 * v + (1.0 - ADAM_B2) * (g * g)
    m_hat = m / (1.0 - ADAM_B1 ** ADAM_STEP)
    v_hat = v / (1.0 - ADAM_B2 ** ADAM_STEP)
    return -ADAM_LR * (m_hat / (jnp.sqrt(v_hat) + ADAM_EPS) + ADAM_WD * w), m, v


def _adamw(name, w, ga, gb, m, v):
    rows, cols = w.shape
    tr = _row_tile(rows)

    def body(w_ref, ga_ref, gb_ref, m_ref, v_ref, g_out, d_out, m_out, v_out):
        g = ga_ref[...] + gb_ref[...]
        d, mm, vv = _adamw_rows(w_ref[...], g, m_ref[...], v_ref[...])
        g_out[...] = g
        d_out[...] = d
        m_out[...] = mm
        v_out[...] = vv

    spec = pl.BlockSpec((tr, cols), lambda i: (i, 0))
    return pl.pallas_call(
        body, name=name, grid=(rows // tr,), in_specs=[spec] * 5, out_specs=[spec] * 4,
        out_shape=[_sds((rows, cols))] * 4, compiler_params=_params("parallel"),
    )(w, ga, gb, m, v)


def _sum_stack(name, st):
    n, rows, cols = st.shape
    tr = _row_tile(rows)

    def body(s_ref, o_ref):
        acc = s_ref[0].astype(f32)
        for j in range(1, n):
            acc = acc + s_ref[j].astype(f32)
        o_ref[...] = acc

    return pl.pallas_call(
        body, name=name, grid=(rows // tr,), in_specs=[pl.BlockSpec((n, tr, cols), lambda i: (0, i, 0))],
        out_specs=pl.BlockSpec((tr, cols), lambda i: (i, 0)), out_shape=_sds((rows, cols)),
        compiler_params=_params("parallel"),
    )(st)


def _block_diag(w):
    h, n, m = w.shape
    return jnp.einsum("hij,hg->higj", w, jnp.eye(h, dtype=w.dtype)).reshape(h * n, h * m)


def _block_diag_part(dense, h):
    n, m = dense.shape[0] // h, dense.shape[1] // h
    return jnp.einsum("higj,hg->hij", dense.reshape(h, n, h, m), jnp.eye(h, dtype=dense.dtype))


def _s5_matrices(coef_re, coef_im, b_re, b_im, c_re, c_im):
    bb_re = coef_re[:, :, None] * b_re - coef_im[:, :, None] * b_im
    bb_im = coef_re[:, :, None] * b_im + coef_im[:, :, None] * b_re
    wb_re = _block_diag(jnp.swapaxes(bb_re, 1, 2))
    wb_im = _block_diag(jnp.swapaxes(bb_im, 1, 2))
    wc_re = _block_diag(jnp.swapaxes(c_re, 1, 2))
    wc_im = _block_diag(jnp.swapaxes(-c_im, 1, 2))
    return wb_re, wb_im, wc_re, wc_im


def _shift_down(t):
    return jnp.concatenate([jnp.zeros((1, t.shape[1]), t.dtype), t[:-1]], axis=0)


def _shift_up(t):
    return jnp.concatenate([t[1:], jnp.zeros((1, t.shape[1]), t.dtype)], axis=0)


def _row(v):
    return v.reshape(1, -1)


def _ffn_fwd(tag, h, get, names, gamma, beta):
    wg, wu = get(names[0]), get(names[1])
    g, u, act = _ffn_up(tag + "_up", h, wg, wu)
    wd = get(names[2])
    r, out = _mm_ln(tag + "_down", act, wd, h, gamma, beta, 0.5, k_slabs=True)
    return out, dict(h=h, g=g, u=u, act=act, r=r, wg=wg, wu=wu, wd=wd)


def _ffn_bwd(tag, dout, sv, names, gamma, put, after_ln=None):
    s = dout.shape[0]
    dr, dgam, dbet = _ln_bwd(tag + "_lnb", sv["r"], dout, gamma)
    if after_ln is not None:
        after_ln(dgam, dbet)
    put(names[2], _mm_plain(tag + "_dwd", "tn", _Slabs(sv["act"]), dr, (D_FF, D_MODEL, s), scale=0.5, out_dtype=bf16,
                            tiles=(FF_SLAB, 1024, _tile(s, 1024))))
    dg, du = _ffn_dact(tag + "_dact", dr, sv["wd"], sv["g"], sv["u"])
    dwg, dwu = _mm2(tag + "_dwgu", "tn", (D_FF, D_MODEL, s), _Slabs(dg), sv["h"], _Slabs(du), None, separate=True,
                    out_dtype=bf16, tiles=(FF_SLAB, 1024, _tile(s, 1024)))
    put(names[0], dwg)
    put(names[1], dwu)
    dh = _mm2(tag + "_dh", "nn", (s, D_MODEL, D_FF), _Slabs(dg), sv["wg"], _Slabs(du), sv["wu"],
              add=dr, add_coef=ALPHA, tiles=(_tile(s, FFN_ROWS), 1024, FF_SLAB))[0]
    return dh, dgam, dbet


def _mixer_fwd(tag, h1, w):
    s = h1.shape[0]
    z = _mm_plain(tag + "_win", "nn", h1, w["w_in"], (s, N_IN_P, D_MODEL), tiles=(_tile(s, 512), 768, D_MODEL))
    ag, f, cu_cols = (z, D_A, 1), (z, 128, F_OFF // 128), (z, D_C, CU_OFF // D_C)
    cu = z[:, CU_OFF:]
    xa = _conv_fwd(tag + "_conv", z, w["conv_w"], w["conv_b"])
    a, gated = _rg_gates(tag + "_gates", xa, w["rg_wa"], w["rg_wx"], w["rg_ba"], w["rg_bx"], w["rg_lam"])
    ha = _lin_scan(tag + "_rgscan", a, gated, False)
    ones = jnp.ones((s, 128), f32)
    c = _lin_scan(tag + "_cumf", ones, _log_f(tag + "_logf", f, w["fox_bf"]), False)
    att = dict(zip(("qt", "k_aug", "kt", "vt", "v", "qh"), _attn_prep(tag + "_attnprep", z, c, w["sel"])))
    ot, lse = _attn_fwd_t(tag + "_attn", att["qt"], att["k_aug"], att["vt"])
    ob = ot.reshape(D_B, s).T
    bu_re, bu_im = _mm2(tag + "_s5in", "nn", (s, S5_LANES, D_C), cu, w["wb_re"], None, w["wb_im"], separate=True,
                        tiles=(_tile(s, 512), 1024, D_C))
    hre, him = _s5_scan(tag + "_s5scan", bu_re, bu_im, w["abar_re"], w["abar_im"], False)
    o = _mix_out(tag + "_mixout", ag, ha, ob, hre, him, cu_cols, w["s5_d"], w["mix_g"], w["wc_re"], w["wc_im"],
                 w["w_glu"])
    sv = dict(h1=h1, z=z, ag=ag, f=f, cu=cu, cu_cols=cu_cols, xa=xa, a=a, ha=ha, att=att, ot=ot, lse=lse, ob=ob,
              hre=hre, him=him, o=o)
    return o, sv


def _mixer_bwd(tag, do, dr2, sv, w, put):
    s = do.shape[0]
    (dag, dha, dob, dhre, dhim, dcu1, dwcr, dwci, dwglu, dd, dgn) = _mix_out_bwd(
        tag + "_mixoutb", do, sv["ag"], sv["ha"], sv["ob"], sv["hre"], sv["him"], sv["cu_cols"], w["s5_d"], w["mix_g"],
        w["wc_re"], w["wc_im"], w["w_glu"])
    put("s5_w_glu", dwglu.astype(bf16))
    gre, gim = _s5_scan(tag + "_s5scanb", dhre, dhim, w["abar_re"], -w["abar_im"], True)
    dab_re, dab_im = _s5_decay_grad(tag + "_s5dec", sv["hre"], sv["him"], gre, gim)
    dwb_re, dwb_im = _mm2(tag + "_s5dwb", "tn", (D_C, S5_LANES, s), sv["cu"], gre, None, gim, separate=True,
                          tiles=(D_C, 1024, _tile(s, 1024)))
    dcu = _mm2(tag + "_s5dcu", "nt", (s, D_C, S5_LANES), gre, w["wb_re"], gim, w["wb_im"], add=dcu1,
               tiles=(_tile(s, 512), D_C, 1024))[0]
    att = sv["att"]
    tq = _att_tiles(s)[0]
    nt = s // tq
    dot_blocks, doh = _attn_do_prep(tag + "_doprep", dob, w["sel"])
    dqt, delta = _attn_bwd_dq_t(tag + "_attndq", att["qt"], att["k_aug"], att["v"], att["kt"], sv["ot"], dot_blocks,
                                sv["lse"])
    dkh, dvh, dck = _attn_bwd_dkv_t(tag + "_attndkv", att["qt"], att["k_aug"], att["v"], att["qh"], doh, dot_blocks,
                                    sv["lse"].reshape(N_HEADS, nt, 1, tq), delta.reshape(N_HEADS, nt, 1, tq))
    dc = jnp.pad(dck.reshape(N_HEADS, s).T, ((0, 0), (0, 128 - N_HEADS)))
    dlf = _lin_scan(tag + "_cumfb", jnp.ones((s, 128), f32), dc, True)
    df, dbf = _log_f_bwd(tag + "_logfb", dlf, sv["f"], w["fox_bf"])
    ga = _lin_scan(tag + "_rgscanb", _shift_up(sv["a"]), dha, True)
    dxa, dwa, dwx, dba, dbx, dlam = _rg_gates_bwd(tag + "_gatesb", sv["xa"], ga, _shift_down(sv["ha"]), w["rg_wa"],
                                                  w["rg_wx"], w["rg_ba"], w["rg_bx"], w["rg_lam"])
    dax, dconv = _conv_bwd(tag + "_convb", dxa, sv["z"], w["conv_w"])
    dz = _dz_assemble(tag + "_dz", dax, dag, dqt, dkh, dvh, df, dcu, w["sel"])
    put("w_in", _mm_plain(tag + "_dwin", "tn", sv["h1"], dz, (D_MODEL, N_IN_P, s), out_dtype=bf16,
                          tiles=(512, 768, _tile(s, 1024))))
    dh1 = _mm_plain(tag + "_dh1", "nt", dz, w["w_in"], (s, D_MODEL, N_IN_P), add=dr2, add_coef=ALPHA,
                    tiles=(_tile(s, 512), 1024, 768))
    grads = dict(dconv=dconv, dwa=dwa, dwx=dwx, dba=dba, dbx=dbx, dlam=dlam, dbf=dbf,
                 dab_re=dab_re, dab_im=dab_im, dwb_re=dwb_re, dwb_im=dwb_im, dwcr=dwcr, dwci=dwci, dd=dd, dgn=dgn)
    return dh1, grads


SMALL_NAMES = ["ln1_g", "ln1_b", "conv_w", "conv_b", "rg_w_a", "rg_b_a", "rg_w_x", "rg_b_x", "rg_lambda", "fox_b_f",
               "s5_a_re", "s5_a_im", "s5_log_dt", "s5_b_re", "s5_b_im", "s5_c_re", "s5_c_im", "s5_d", "mix_norm_g",
               "ln2_g", "ln2_b", "ln3_g", "ln3_b"]
BIG_NAMES = ["ffn1_w_gate", "ffn1_w_up", "ffn1_w_down", "w_in", "s5_w_glu", "w_out", "ffn2_w_gate", "ffn2_w_up",
             "ffn2_w_down"]


def _local_step(x, target, weight, small, on_grads, on_small):
    h = x
    saved = []
    sel = _selections()
    for l in range(DEPTH):
        get = functools.partial(weight, l)

        sm = {n: small[n][l] for n in SMALL_NAMES}
        abar_re, abar_im, coef_re, coef_im = _s5_disc(f"l{l}_s5disc", sm["s5_a_re"], sm["s5_a_im"],
                                                      sm["s5_log_dt"].reshape(S5_GROUPS, 1))
        mats, mats_vjp = jax.vjp(_s5_matrices, coef_re, coef_im, sm["s5_b_re"], sm["s5_b_im"], sm["s5_c_re"],
                                 sm["s5_c_im"])
        w = dict(
            sel=sel, conv_w=sm["conv_w"], conv_b=_row(sm["conv_b"]),
            rg_wa=_block_diag(sm["rg_w_a"]).astype(bf16), rg_wx=_block_diag(sm["rg_w_x"]).astype(bf16),
            rg_ba=_row(sm["rg_b_a"]), rg_bx=_row(sm["rg_b_x"]), rg_lam=_row(sm["rg_lambda"]),
            fox_bf=jnp.pad(_row(sm["fox_b_f"]), ((0, 0), (0, 128 - N_HEADS))),
            abar_re=_row(abar_re), abar_im=_row(abar_im),
            wb_re=mats[0].astype(bf16), wb_im=mats[1].astype(bf16), wc_re=mats[2].astype(bf16),
            wc_im=mats[3].astype(bf16), s5_d=_row(sm["s5_d"]), mix_g=_row(sm["mix_norm_g"]))
        h1, sv1 = _ffn_fwd(f"l{l}_ffn1", h, get, GROUPS["F1"], _row(sm["ln1_g"]), _row(sm["ln1_b"]))
        w["w_in"], w["w_glu"] = get("w_in"), get("s5_w_glu")
        o, svm = _mixer_fwd(f"l{l}_mix", h1, w)
        w_out = get("w_out")
        r2, h2 = _mm_ln(f"l{l}_wout", o, w_out, h1, _row(sm["ln2_g"]), _row(sm["ln2_b"]), 1.0)
        h3, sv2 = _ffn_fwd(f"l{l}_ffn2", h2, get, GROUPS["F2"], _row(sm["ln3_g"]), _row(sm["ln3_b"]))
        saved.append(dict(sm=sm, w=w, w_out=w_out, sv1=sv1, svm=svm, r2=r2, sv2=sv2, mats_vjp=mats_vjp))
        h = h3

    dh, loss_row = _loss_head("loss_head", h, target)
    s = x.shape[0]
    gsmall = {n: [None] * DEPTH for n in SMALL_NAMES}
    for l in reversed(range(DEPTH)):
        sd = saved[l]
        sm, w = sd["sm"], sd["w"]

        def put(name, grad, l=l):
            on_grads((l, name), grad)

        dh2, dgam, dbet = _ffn_bwd(f"l{l}_ffn2", dh, sd["sv2"], GROUPS["F2"], _row(sm["ln3_g"]), put)
        gsmall["ln3_g"][l], gsmall["ln3_b"][l] = dgam[0], dbet[0]
        dr2, dgam, dbet = _ln_bwd(f"l{l}_ln2b", sd["r2"], dh2, _row(sm["ln2_g"]))
        gsmall["ln2_g"][l], gsmall["ln2_b"][l] = dgam[0], dbet[0]
        put("w_out", _mm_plain(f"l{l}_dwout", "tn", sd["svm"]["o"], dr2, (D_MODEL, D_MODEL, s), out_dtype=bf16))
        do = _mm_plain(f"l{l}_do", "nt", dr2, sd["w_out"], (s, D_MODEL, D_MODEL))
        dh1, g = _mixer_bwd(f"l{l}_mix", do, dr2, sd["svm"], w, put)
        gsmall["conv_w"][l], gsmall["conv_b"][l] = g["dconv"][:CONV_WIDTH], g["dconv"][CONV_WIDTH]
        gsmall["rg_w_a"][l] = _block_diag_part(g["dwa"], N_HEADS)
        gsmall["rg_w_x"][l] = _block_diag_part(g["dwx"], N_HEADS)
        gsmall["rg_b_a"][l], gsmall["rg_b_x"][l], gsmall["rg_lambda"][l] = g["dba"][0], g["dbx"][0], g["dlam"][0]
        gsmall["fox_b_f"][l] = g["dbf"][0, :N_HEADS]
        dcoef_re, dcoef_im, db_re, db_im, dc_re, dc_im = sd["mats_vjp"]((g["dwb_re"], g["dwb_im"], g["dwcr"], g["dwci"]))
        da_re, da_im, dldt = _s5_disc_bwd(
            f"l{l}_s5discb", sm["s5_a_re"], sm["s5_a_im"], sm["s5_log_dt"].reshape(S5_GROUPS, 1),
            (g["dab_re"].reshape(S5_GROUPS, S5_STATE), g["dab_im"].reshape(S5_GROUPS, S5_STATE), dcoef_re, dcoef_im))
        gsmall["s5_a_re"][l], gsmall["s5_a_im"][l], gsmall["s5_log_dt"][l] = da_re, da_im, dldt[:, 0]
        gsmall["s5_b_re"][l], gsmall["s5_b_im"][l], gsmall["s5_c_re"][l], gsmall["s5_c_im"][l] = db_re, db_im, dc_re, dc_im
        gsmall["s5_d"][l], gsmall["mix_norm_g"][l] = g["dd"][0], g["dgn"][0]

        def after_ln(dgam, dbet, l=l):
            gsmall["ln1_g"][l], gsmall["ln1_b"][l] = dgam[0], dbet[0]
            if l == 0:
                on_small({n: jnp.stack(v) for n, v in gsmall.items()})

        dh, _, _ = _ffn_bwd(f"l{l}_ffn1", dh1, sd["sv1"], GROUPS["F1"], _row(sm["ln1_g"]), put, after_ln)
    return loss_row[0, 0], dh


def _position():
    return lax.axis_index("x"), lax.axis_index("y"), lax.axis_index("c")


_ANY = pl.BlockSpec(memory_space=pl.ANY)


COLUMN_SHARDED = ("ffn1_w_gate", "ffn1_w_up", "ffn2_w_gate", "ffn2_w_up")
PACK_QUANTUM = 128 * 256


def _permute_in_cols(w):
    pad = jnp.zeros(w.shape[:-1] + (128 - N_HEADS,), w.dtype)
    return jnp.concatenate([w[..., :F_OFF + N_HEADS], pad, w[..., F_OFF + N_HEADS:]], axis=-1)


def _unpermute_in_cols(w):
    return jnp.concatenate([w[..., :F_OFF + N_HEADS], w[..., CU_OFF:]], axis=-1)


def _pack(arrs):
    flat = jnp.concatenate([a.reshape(-1) for a in arrs])
    pad = -flat.shape[0] % PACK_QUANTUM
    return jnp.pad(flat, (0, pad)).reshape(-1, 128)


def _unpack(buf, shapes):
    flat = buf.reshape(-1)
    out, off = [], 0
    for shp in shapes:
        size = math.prod(shp)
        out.append(flat[off:off + size].reshape(shp))
        off += size
    return out


WEIGHT_NAMES = ["ffn1_w_gate", "ffn1_w_up", "ffn1_w_down", "ln1_g", "ln1_b", "w_in", "conv_w", "conv_b", "rg_w_a",
                "rg_b_a", "rg_w_x", "rg_b_x", "rg_lambda", "fox_b_f", "s5_a_re", "s5_a_im", "s5_log_dt", "s5_b_re",
                "s5_b_im", "s5_c_re", "s5_c_im", "s5_d", "s5_w_glu", "mix_norm_g", "w_out", "ln2_g", "ln2_b",
                "ffn2_w_gate", "ffn2_w_up", "ffn2_w_down", "ln3_g", "ln3_b"]


def _remote(src, dst, send_sems, recv_sems, k, peer):
    return pltpu.make_async_remote_copy(src_ref=src, dst_ref=dst, send_sem=send_sems.at[k], recv_sem=recv_sems.at[k],
                                        device_id=peer, device_id_type=MESH)


class _ChipGatherPart:
    def __init__(self, arrays):
        self.arrays, self.results = list(arrays), None

    def out_shape(self):
        return [_sds((N_CHIPS,) + a.shape, a.dtype) for a in self.arrays]

    def sems(self):
        n = len(self.arrays)
        return [pltpu.SemaphoreType.DMA((3 * n,)), pltpu.SemaphoreType.DMA((3 * n,)), pltpu.SemaphoreType.DMA((n,))]

    def copies(self, ins, outs, sems):
        send_sems, recv_sems, local_sems = sems
        x, y, c = _position()
        me = 2 * x + y
        local, sends, recvs = [], [], []
        for i, (src, dst) in enumerate(zip(ins, outs)):
            local.append(pltpu.make_async_copy(self.mine(src, me), dst.at[me], local_sems.at[i]))
            for r, (px, py) in enumerate([(1 - x, y), (x, 1 - y), (1 - x, 1 - y)]):
                peer = 2 * px + py
                sends.append(_remote(self.theirs(src, peer), dst.at[me], send_sems, recv_sems, 3 * i + r, (px, py, c)))
                recvs.append(_remote(self.mine(src, me), dst.at[peer], send_sems, recv_sems, 3 * i + r, (px, py, c)))
        return local, sends, recvs

    def mine(self, src, me):
        return src

    def theirs(self, src, peer):
        return src


class _ChipGatherHalvesPart(_ChipGatherPart):
    def sems(self):
        n = len(self.arrays)
        return super().sems() + [pltpu.SemaphoreType.DMA((3 * n,)), pltpu.SemaphoreType.DMA((3 * n,))]

    def _half(self, ref, which):
        rows = ref.shape[0] // 2
        return ref.at[pl.ds(which * rows, rows)]

    def copies(self, ins, outs, sems):
        send_sems, recv_sems, local_sems = sems[:3]
        x, y, c = _position()
        me = 2 * x + y
        local, sends, recvs = [], [], []
        for i, (src, dst) in enumerate(zip(ins, outs)):
            local.append(pltpu.make_async_copy(src, dst.at[me], local_sems.at[i]))
            for r, (px, py) in enumerate([(1 - x, y), (x, 1 - y), (1 - x, 1 - y)]):
                sends.append(_remote(self._half(src, c), self._half(dst.at[me], c), send_sems, recv_sems, 3 * i + r,
                                     (px, py, c)))
                recvs.append(_remote(self._half(src, c), self._half(dst.at[2 * px + py], c), send_sems, recv_sems,
                                     3 * i + r, (px, py, c)))
        return local, sends, recvs

    def forwards(self, ins, outs, sems):
        send_sems, recv_sems = sems[3:]
        x, y, c = _position()
        sends, recvs = [], []
        for i, dst in enumerate(outs):
            for r, (px, py) in enumerate([(1 - x, y), (x, 1 - y), (1 - x, 1 - y)]):
                slot = dst.at[2 * px + py]
                sends.append(_remote(self._half(slot, c), self._half(slot, c), send_sems, recv_sems, 3 * i + r,
                                     (x, y, 1 - c)))
                recvs.append(_remote(self._half(slot, c), self._half(slot, 1 - c), send_sems, recv_sems, 3 * i + r,
                                     (x, y, 1 - c)))
        return sends, recvs


class _ChipScatterPart(_ChipGatherPart):
    def out_shape(self):
        return [_sds(a.shape, a.dtype) for a in self.arrays]

    def mine(self, src, me):
        return src.at[me]

    def theirs(self, src, peer):
        return src.at[peer]


class _SiblingSwapPart:
    def __init__(self, arrays):
        self.arrays, self.results = list(arrays), None

    def out_shape(self):
        return [_sds(a.shape, a.dtype) for a in self.arrays]

    def sems(self):
        n = len(self.arrays)
        return [pltpu.SemaphoreType.DMA((n,)), pltpu.SemaphoreType.DMA((n,))]

    def copies(self, ins, outs, sems):
        x, y, c = _position()
        both = [_remote(src, dst, sems[0], sems[1], i, (x, y, 1 - c)) for i, (src, dst) in enumerate(zip(ins, outs))]
        return [], both, both


class _DevGatherPart:
    def __init__(self, array):
        self.arrays, self.results = [array], None

    def out_shape(self):
        return [_sds((N_DEV,) + self.arrays[0].shape, self.arrays[0].dtype)]

    def sems(self):
        return [pltpu.SemaphoreType.DMA((N_DEV - 1,)), pltpu.SemaphoreType.DMA((N_DEV - 1,)),
                pltpu.SemaphoreType.DMA((1,))]

    def copies(self, ins, outs, sems):
        send_sems, recv_sems, local_sems = sems
        (src,), (dst,) = ins, outs
        x, y, c = _position()
        me = 4 * x + 2 * y + c
        local = [pltpu.make_async_copy(src, dst.at[me], local_sems.at[0])]
        sends, recvs = [], []
        for k in range(1, N_DEV):
            px, py, pc = (1 - x if k & 4 else x, 1 - y if k & 2 else y, 1 - c if k & 1 else c)
            sends.append(_remote(src, dst.at[me], send_sems, recv_sems, k - 1, (px, py, pc)))
            recvs.append(_remote(src, dst.at[4 * px + 2 * py + pc], send_sems, recv_sems, k - 1, (px, py, pc)))
        return local, sends, recvs


def _split_by(parts, refs, count):
    out, off = [], 0
    for p in parts:
        out.append(refs[off:off + count(p)])
        off += count(p)
    return out


def _parts_refs(parts, in_refs, out_refs, sem_refs):
    return zip(parts, _split_by(parts, in_refs, lambda p: len(p.arrays)),
               _split_by(parts, out_refs, lambda p: len(p.arrays)), _split_by(parts, sem_refs, lambda p: len(p.sems())))


def _exchange_start(parts, in_refs, out_refs, sem_refs):
    for part, ins, outs, sems in _parts_refs(parts, in_refs, out_refs, sem_refs):
        local, sends, _ = part.copies(ins, outs, sems)
        for cp in local + sends:
            cp.start()


def _exchange_finish(parts, in_refs, out_refs, sem_refs):
    split = list(_parts_refs(parts, in_refs, out_refs, sem_refs))
    copies = [part.copies(ins, outs, sems) for part, ins, outs, sems in split]
    for _, _, recvs in copies:
        for cp in recvs:
            cp.wait_recv()
    second = [part.forwards(ins, outs, sems) for part, ins, outs, sems in split if hasattr(part, "forwards")]
    for sends, _ in second:
        for cp in sends:
            cp.start()
    for sends, recvs in second:
        for cp in recvs:
            cp.wait_recv()
        for cp in sends:
            cp.wait_send()
    for local, sends, _ in copies:
        for cp in sends:
            cp.wait_send()
        for cp in local:
            cp.wait()


def _exchange_operands(parts):
    return ([a for p in parts for a in p.arrays], [s for p in parts for s in p.out_shape()],
            [s for p in parts for s in p.sems()])


def _set_results(parts, res):
    for part, outs in zip(parts, _split_by(parts, list(res), lambda p: len(p.arrays))):
        part.results = list(outs)


def _exchange_now(name, parts):
    x_in, x_out, x_sem = _exchange_operands(parts)
    n = len(x_in)

    def body(*refs):
        _exchange_start(parts, refs[:n], refs[n:2 * n], refs[2 * n:])
        _exchange_finish(parts, refs[:n], refs[n:2 * n], refs[2 * n:])

    res = pl.pallas_call(body, name=name, in_specs=[_ANY] * n, out_specs=[_ANY] * n, out_shape=x_out,
                         scratch_shapes=x_sem)(*x_in)
    _set_results(parts, res)


_RIDERS = {}


def _call(body, *, name, grid, in_specs, out_specs, out_shape, scratch_shapes=(), compiler_params=None):
    make_parts = _RIDERS.pop(name, None)
    if make_parts is None:
        return pl.pallas_call(body, name=name, grid=grid, in_specs=in_specs, out_specs=out_specs, out_shape=out_shape,
                              scratch_shapes=scratch_shapes, compiler_params=compiler_params)
    parts = make_parts()
    x_in, x_out, x_sem = _exchange_operands(parts)
    n_out, n_scr, n_x = len(out_shape), len(scratch_shapes), len(x_in)

    def run(*args):
        n_in = len(args)

        def hosted(*refs):
            ins, xi = refs[:n_in], refs[n_in:n_in + n_x]
            outs, xo = refs[n_in + n_x:n_in + n_x + n_out], refs[n_in + n_x + n_out:n_in + 2 * n_x + n_out]
            scr, xs = refs[n_in + 2 * n_x + n_out:n_in + 2 * n_x + n_out + n_scr], refs[n_in + 2 * n_x + n_out + n_scr:]
            first = functools.reduce(jnp.logical_and, [pl.program_id(d) == 0 for d in range(len(grid))])
            last = functools.reduce(jnp.logical_and, [pl.program_id(d) == grid[d] - 1 for d in range(len(grid))])

            @pl.when(first)
            def _():
                _exchange_start(parts, xi, xo, xs)

            body(*ins, *outs, *scr)

            @pl.when(last)
            def _():
                _exchange_finish(parts, xi, xo, xs)

        res = pl.pallas_call(
            hosted, name=name, grid=grid, in_specs=list(in_specs) + [_ANY] * n_x,
            out_specs=list(out_specs) + [_ANY] * n_x, out_shape=list(out_shape) + x_out,
            scratch_shapes=list(scratch_shapes) + x_sem, compiler_params=_params(*["arbitrary"] * len(grid)),
        )(*args, *x_in)
        _set_results(parts, res[n_out:])
        return list(res[:n_out])

    return run


GROUPS = {"F1": ["ffn1_w_gate", "ffn1_w_up", "ffn1_w_down"], "MX": ["w_in", "s5_w_glu", "w_out"],
          "F2": ["ffn2_w_gate", "ffn2_w_up", "ffn2_w_down"]}
FIRST_GATHER = [(0, "ffn1_w_gate"), (0, "ffn1_w_up")]
GATHER_HOSTS = {
    "l0_ffn1_up": [(0, "ffn1_w_down")],
    "l0_ffn1_down": [(0, "w_in"), (0, "s5_w_glu"), (0, "w_out")],
    "l0_mix_win": [(0, "ffn2_w_gate")],
    "l0_mix_attn": [(0, "ffn2_w_up"), (0, "ffn2_w_down")],
    "l0_ffn2_up": [(1, "ffn1_w_gate")],
    "l0_ffn2_down": [(1, "ffn1_w_up")],
    "l1_ffn1_up": [(1, "ffn1_w_down")],
    "l1_ffn1_down": [(1, "w_in"), (1, "s5_w_glu"), (1, "w_out")],
    "l1_mix_win": [(1, "ffn2_w_gate")],
    "l1_mix_attn": [(1, "ffn2_w_up"), (1, "ffn2_w_down")],
}
SCATTER_HOSTS = {
    "l1_ffn2_dact": [(1, "ffn2_w_down")],
    "l1_ffn2_dh": [(1, "ffn2_w_gate")],
    "l1_mix_attndq": [(1, "ffn2_w_up")],
    "l1_mix_attndkv": [(1, "w_out"), (1, "s5_w_glu")],
    "l1_mix_dh1": [(1, "w_in")],
    "l1_ffn1_dact": [(1, "ffn1_w_down")],
    "l1_ffn1_dh": [(1, "ffn1_w_gate")],
    "l0_ffn2_dact": [(1, "ffn1_w_up")],
    "l0_ffn2_dwgu": [(0, "ffn2_w_down")],
    "l0_ffn2_dh": [(0, "ffn2_w_gate")],
    "l0_mix_attndq": [(0, "w_out"), (0, "s5_w_glu"), (0, "ffn2_w_up")],
    "l0_mix_dh1": [(0, "w_in")],
    "l0_ffn1_dact": [(0, "ffn1_w_down")],
    "l0_ffn1_dh": [(0, "ffn1_w_gate")],
}
LAST_SCATTER = [(0, "ffn1_w_up")]
TAIL_HOST = "l0_ffn1_dwgu"
LATE_SCATTER_HOST = "l0_ffn1_dh"


def _sharded_rows(name, a):
    return jnp.swapaxes(a, 1, 2) if name in COLUMN_SHARDED else a


def _unstack_layer(st):
    _, r, c = st.shape
    return st.reshape(N_CHIPS * r, c)


def _restack_layer(g):
    r, c = g.shape
    return g.reshape(N_CHIPS, r // N_CHIPS, c)


def _adamw_layer(name, layer, w, ga, gb, m, v, bufs):
    _, r, c = w.shape
    tr = _row_tile(r)

    def body(w_ref, ga_ref, gb_ref, m_ref, v_ref, *rest):
        g_out, d_out, m_out, v_out = rest[-4:]
        g = ga_ref[...] + gb_ref[...]
        d, mm, vv = _adamw_rows(w_ref[...], g, m_ref[...], v_ref[...])
        g_out[...] = g
        d_out[...] = d
        m_out[...] = mm
        v_out[...] = vv

    full = pl.BlockSpec((None, tr, c), lambda i: (layer, i, 0))
    flat = pl.BlockSpec((tr, c), lambda i: (i, 0))
    extra = {} if bufs is None else dict(input_output_aliases={5 + k: k for k in range(4)})
    return pl.pallas_call(
        body, name=name, grid=(r // tr,),
        in_specs=[full, flat, flat, full, full] + ([] if bufs is None else [_ANY] * 4),
        out_specs=[full] * 4, out_shape=[_sds(w.shape)] * 4, compiler_params=_params("parallel"), **extra,
    )(w, ga, gb, m, v, *([] if bufs is None else bufs))


def _train_step(x, loss_target, w, m, v):
    ix, iy, _ = _position()
    chip = 2 * ix + iy
    shard = {n: (_permute_in_cols(w[n]) if n == "w_in" else _sharded_rows(n, w[n])).astype(bf16) for n in BIG_NAMES}

    gathered = {}

    def gather_parts(keys, extra=()):
        part = _ChipGatherHalvesPart([shard[n][layer] for layer, n in keys] + list(extra))
        gathered.update({key: (part, i) for i, key in enumerate(keys)})
        return [part]

    (first,) = gather_parts(FIRST_GATHER, extra=[w["conv_w"]])
    _exchange_now("gather_first", [first])
    for host, keys in GATHER_HOSTS.items():
        _RIDERS[host] = functools.partial(gather_parts, keys)

    def weight(layer, name):
        part, i = gathered[(layer, name)]
        return _unstack_layer(part.results[i])

    small = {n: w[n] for n in SMALL_NAMES}
    small["conv_w"] = first.results[-1].transpose(1, 2, 0, 3).reshape(DEPTH, CONV_WIDTH, D_A)

    grads_full, scattered = {}, {}

    def scatter_parts(keys):
        part = _ChipScatterPart([_restack_layer(grads_full[key]) for key in keys])
        scattered.update({key: (part, i) for i, key in enumerate(keys)})
        return [part]

    for host, keys in SCATTER_HOSTS.items():
        _RIDERS[host] = functools.partial(scatter_parts, keys)

    partial = {}

    def reduce_chips(keys):
        for layer, n in keys:
            part, i = scattered[(layer, n)]
            p = _sum_stack(f"sum_l{layer}_{n}", part.results[i])
            partial[(layer, n)] = _unpermute_in_cols(p) if n == "w_in" else p

    early = [key for host, keys in SCATTER_HOSTS.items() if host != LATE_SCATTER_HOST for key in keys]
    late = SCATTER_HOSTS[LATE_SCATTER_HOST]
    tail = {}

    def tail_parts():
        reduce_chips(early)
        tail["swap"] = _SiblingSwapPart([partial[k] for k in early])
        tail["small"] = _DevGatherPart(_pack([tail["gsmall"][n] for n in SMALL_NAMES]))
        return [tail["swap"], tail["small"]]

    _RIDERS[TAIL_HOST] = tail_parts
    loss_local, gx = _local_step(x[0], loss_target[0], weight, small, grads_full.__setitem__,
                                 functools.partial(tail.__setitem__, "gsmall"))
    other = dict(zip(early, tail["swap"].results))
    reduce_chips(late)
    last_parts = scatter_parts(LAST_SCATTER) + [_SiblingSwapPart([partial[k] for k in late])]
    _exchange_now("exchange_last", last_parts)
    other.update(zip(late, last_parts[1].results))
    reduce_chips(LAST_SCATTER)
    swap_last = _SiblingSwapPart([partial[k] for k in LAST_SCATTER])
    _exchange_now("swap_last", [swap_last])
    other.update(zip(LAST_SCATTER, swap_last.results))

    small_shapes = [tail["gsmall"][n].shape for n in SMALL_NAMES]
    total = _sum_stack("sum_small", tail["small"].results[0])
    gsm = dict(zip(SMALL_NAMES, _unpack(total, small_shapes)))
    cw = D_A // N_CHIPS
    gsm["conv_w"] = lax.dynamic_slice_in_dim(gsm["conv_w"], chip * cw, cw, axis=2)

    grads, deltas, new_m, new_v = {}, {}, {}, {}
    for n in BIG_NAMES:
        bufs = None
        wr, mr, vr = (_sharded_rows(n, t) for t in (w[n], m[n], v[n]))
        for layer in range(DEPTH):
            bufs = _adamw_layer(f"adamw_l{layer}_{n}", layer, wr, partial[(layer, n)], other[(layer, n)], mr, vr, bufs)
        grads[n], deltas[n], new_m[n], new_v[n] = (_sharded_rows(n, t) for t in bufs)
    shapes = [w[n].shape for n in SMALL_NAMES]
    gp = _pack([gsm[n] for n in SMALL_NAMES])
    res = _adamw("adamw_small", _pack([w[n] for n in SMALL_NAMES]), gp, jnp.zeros_like(gp),
                 _pack([m[n] for n in SMALL_NAMES]), _pack([v[n] for n in SMALL_NAMES]))
    for dst, buf in zip((grads, deltas, new_m, new_v), res):
        dst.update(zip(SMALL_NAMES, _unpack(buf, shapes)))

    loss = lax.psum(loss_local, ("x", "y", "c"))
    return (loss, gx[None], *[grads[n] for n in WEIGHT_NAMES], *[deltas[n] for n in WEIGHT_NAMES],
            *[new_m[n] for n in WEIGHT_NAMES], *[new_v[n] for n in WEIGHT_NAMES])


def kernel(x, ffn1_w_gate, ffn1_w_up, ffn1_w_down, ln1_g, ln1_b, w_in, conv_w, conv_b, rg_w_a, rg_b_a, rg_w_x, rg_b_x, rg_lambda, fox_b_f, s5_a_re, s5_a_im, s5_log_dt, s5_b_re, s5_b_im, s5_c_re, s5_c_im, s5_d, s5_w_glu, mix_norm_g, w_out, ln2_g, ln2_b, ffn2_w_gate, ffn2_w_up, ffn2_w_down, ln3_g, ln3_b, loss_target, m_ffn1_w_gate, m_ffn1_w_up, m_ffn1_w_down, m_ln1_g, m_ln1_b, m_w_in, m_conv_w, m_conv_b, m_rg_w_a, m_rg_b_a, m_rg_w_x, m_rg_b_x, m_rg_lambda, m_fox_b_f, m_s5_a_re, m_s5_a_im, m_s5_log_dt, m_s5_b_re, m_s5_b_im, m_s5_c_re, m_s5_c_im, m_s5_d, m_s5_w_glu, m_mix_norm_g, m_w_out, m_ln2_g, m_ln2_b, m_ffn2_w_gate, m_ffn2_w_up, m_ffn2_w_down, m_ln3_g, m_ln3_b, v_ffn1_w_gate, v_ffn1_w_up, v_ffn1_w_down, v_ln1_g, v_ln1_b, v_w_in, v_conv_w, v_conv_b, v_rg_w_a, v_rg_b_a, v_rg_w_x, v_rg_b_x, v_rg_lambda, v_fox_b_f, v_s5_a_re, v_s5_a_im, v_s5_log_dt, v_s5_b_re, v_s5_b_im, v_s5_c_re, v_s5_c_im, v_s5_d, v_s5_w_glu, v_mix_norm_g, v_w_out, v_ln2_g, v_ln2_b, v_ffn2_w_gate, v_ffn2_w_up, v_ffn2_w_down, v_ln3_g, v_ln3_b):
    args = dict(locals())
    w = {n: args[n] for n in WEIGHT_NAMES}
    m = {n: args["m_" + n] for n in WEIGHT_NAMES}
    v = {n: args["v_" + n] for n in WEIGHT_NAMES}
    return _train_step(x, loss_target, w, m, v)
```

```python
import functools
import math

import jax
import jax.numpy as jnp
from jax import lax
from jax.experimental import pallas as pl
from jax.experimental.pallas import tpu as pltpu

f32 = jnp.float32
bf16 = jnp.bfloat16

D_MODEL = 1024
D_FF = 2816
D_A = 384
D_B = 384
D_C = 256
N_HEADS = 6
HEAD_DIM = 64
S5_GROUPS = 16
S5_GROUP = 16
S5_STATE = 64
S5_LANES = S5_GROUPS * S5_STATE
N_IN = 2 * D_A + 3 * D_B + N_HEADS + D_C
F_OFF = 5 * D_A
CU_OFF = F_OFF + 128
N_IN_P = CU_OFF + D_C
CONV_WIDTH = 4
DEPTH = 2
ALPHA = (2 * DEPTH) ** 0.25
LN_EPS = 1e-5
RMS_EPS = 1e-6
RG_C = 8.0
ATT_SCALE = HEAD_DIM ** -0.5
ADAM_LR, ADAM_B1, ADAM_B2, ADAM_EPS, ADAM_WD, ADAM_STEP = 0.001, 0.9, 0.999, 1e-08, 0.01, 10

ROW_TILE = 256
N_CHIPS = 4
N_DEV = 8
MESH = pl.DeviceIdType.MESH

_DN = {
    "nn": (((1,), (0,)), ((), ())),
    "nt": (((1,), (1,)), ((), ())),
    "tn": (((0,), (0,)), ((), ())),
}


def _sds(shape, dtype=f32):
    return jax.ShapeDtypeStruct(shape, dtype)


def _tile(n, target):
    best = None
    for t in range(128, min(n, target) + 1, 128):
        if n % t == 0:
            best = t
    return best or n


def _row_tile(rows, target=256):
    best = None
    for t in range(16, min(rows, target) + 1, 16):
        if rows % t == 0:
            best = t
    return best or rows


def _params(*sem):
    return pltpu.CompilerParams(dimension_semantics=sem)


class _Slabs:
    def __init__(self, x):
        self.x = x


FF_SLAB = D_FF // 4
FFN_ROWS = 1024

def _mm(name, mode, dims, tiles, a_list, b_list, pairs, n_acc, epilogue, outs, extras=(), vecs=(), split_cols=False):
    m, n, k = dims
    tm, tn, tk = tiles
    nk = k // tk
    na, nb, ne, nv, no = len(a_list), len(b_list), len(extras), len(vecs), len(outs)

    def body(*refs):
        a_refs = refs[:na]
        b_refs = refs[na:na + nb]
        e_refs = refs[na + nb:na + nb + ne]
        v_refs = refs[na + nb + ne:na + nb + ne + nv]
        o_refs = refs[na + nb + ne + nv:na + nb + ne + nv + no]
        acc_refs = refs[na + nb + ne + nv + no:]
        a_vals = [r[...].astype(bf16) for r in a_refs]
        b_vals = [r[...].astype(bf16) for r in b_refs]
        products = [(ci, lax.dot_general(a_vals[ai], b_vals[bi], _DN[mode], preferred_element_type=f32))
                    for ai, bi, ci in pairs]

        def finish(accs):
            res = epilogue(accs, [e[...] for e in e_refs], [v[...] for v in v_refs])
            for o, r in zip(o_refs, res):
                o[...] = r.astype(o.dtype)

        if nk == 1:
            accs = [None] * n_acc
            for ci, prod in products:
                accs[ci] = prod if accs[ci] is None else accs[ci] + prod
            finish(accs)
            return
        kk = pl.program_id(2)

        @pl.when(kk == 0)
        def _():
            for acc in acc_refs:
                acc[...] = jnp.zeros_like(acc)

        for ci, prod in products:
            acc_refs[ci][...] += prod

        @pl.when(kk == nk - 1)
        def _():
            finish([acc[...] for acc in acc_refs])

    def a_spec(a):
        if isinstance(a, _Slabs):
            if mode == "tn":
                return pl.BlockSpec((None, tk, tm), lambda i, j, kk: (i, kk, 0))
            return pl.BlockSpec((None, tm, tk), lambda i, j, kk: (kk, i, 0))
        if mode == "tn":
            return pl.BlockSpec((tk, tm), lambda i, j, kk: (kk, i))
        return pl.BlockSpec((tm, tk), lambda i, j, kk: (i, kk))

    def b_spec(b):
        if isinstance(b, _Slabs):
            if mode == "nt":
                return pl.BlockSpec((None, tn, tk), lambda i, j, kk: (kk, j, 0))
            return pl.BlockSpec((None, tk, tn), lambda i, j, kk: (j, kk, 0))
        if mode == "nt":
            return pl.BlockSpec((tn, tk), lambda i, j, kk: (j, kk))
        return pl.BlockSpec((tk, tn), lambda i, j, kk: (kk, j))

    o_spec = pl.BlockSpec((tm, tn), lambda i, j, kk: (i, j))
    o_slab_spec = pl.BlockSpec((None, tm, tn), lambda i, j, kk: (j, i, 0))
    v_spec = pl.BlockSpec((1, tn), lambda i, j, kk: (0, j))
    if split_cols:
        out_specs = [o_slab_spec] * no
        out_shape = [_sds((n // tn, m, tn), dt) for dt in outs]
    else:
        out_specs = [o_spec] * no
        out_shape = [_sds((m, n), dt) for dt in outs]
    raw = lambda t: t.x if isinstance(t, _Slabs) else t
    res = _call(
        body,
        name=name,
        grid=(m // tm, n // tn, nk),
        in_specs=([a_spec(a) for a in a_list] + [b_spec(b) for b in b_list]
                  + [o_slab_spec if isinstance(e, _Slabs) else o_spec for e in extras] + [v_spec] * nv),
        out_specs=out_specs,
        out_shape=out_shape,
        scratch_shapes=[pltpu.VMEM((tm, tn), f32)] * (n_acc if nk > 1 else 0),
        compiler_params=_params("parallel", "parallel", "arbitrary"),
    )(*map(raw, a_list), *map(raw, b_list), *map(raw, extras), *vecs)
    return res


def _sigmoid(x):
    return 0.5 * (jnp.tanh(0.5 * x) + 1.0)


def _layer_norm_rows(r, gamma, beta):
    mu = jnp.mean(r, axis=-1, keepdims=True)
    xc = r - mu
    var = jnp.mean(xc * xc, axis=-1, keepdims=True)
    return xc * lax.rsqrt(var + LN_EPS) * gamma + beta


def _mm_plain(name, mode, a, b, dims, scale=1.0, out_dtype=f32, add=None, add_coef=1.0, tiles=None):
    m, n, k = dims
    tiles = tiles or (_tile(m, 512), _tile(n, 1024), _tile(k, 1024))

    def epilogue(accs, extras, vecs):
        r = accs[0] if scale == 1.0 else accs[0] * scale
        if extras:
            r = r + add_coef * extras[0]
        return [r]

    return _mm(name, mode, dims, tiles, [a], [b], [(0, 0, 0)], 1, epilogue, [out_dtype],
               extras=[] if add is None else [add])[0]


def _ffn_up(name, h, wg, wu):
    s = h.shape[0]

    def epilogue(accs, extras, vecs):
        g, u = accs
        return [g, u, g * _sigmoid(g) * u]

    return _mm(name, "nt", (s, D_FF, D_MODEL), (_tile(s, FFN_ROWS), FF_SLAB, D_MODEL), [h], [wg, wu],
               [(0, 0, 0), (0, 1, 1)], 2, epilogue, [bf16, bf16, bf16], split_cols=True)


def _mm_ln(name, a, w, resid, gamma, beta, scale, k_slabs=False):
    s, k = (a.shape[1], a.shape[0] * a.shape[2]) if k_slabs else a.shape

    def epilogue(accs, extras, vecs):
        r = ALPHA * extras[0] + scale * accs[0]
        return [r, _layer_norm_rows(r, vecs[0], vecs[1])]

    return _mm(name, "nn", (s, D_MODEL, k), (_tile(s, FFN_ROWS), D_MODEL, FF_SLAB if k_slabs else _tile(k, 1024)),
               [_Slabs(a) if k_slabs else a], [w], [(0, 0, 0)], 1, epilogue, [f32, f32], extras=[resid],
               vecs=[gamma, beta])


def _ffn_dact(name, dr, wd, g, u):
    s = dr.shape[0]

    def epilogue(accs, extras, vecs):
        da = 0.5 * accs[0]
        gg, uu = extras[0].astype(f32), extras[1].astype(f32)
        sg = _sigmoid(gg)
        return [da * uu * (sg * (1.0 + gg * (1.0 - sg))), da * (gg * sg)]

    return _mm(name, "nt", (s, D_FF, D_MODEL), (_tile(s, FFN_ROWS), FF_SLAB, D_MODEL), [dr], [wd],
               [(0, 0, 0)], 1, epilogue, [bf16, bf16], extras=[_Slabs(g), _Slabs(u)], split_cols=True)


def _mm2(name, mode, dims, a0, b0, a1, b1, add=None, add_coef=1.0, separate=False, tiles=None, out_dtype=f32,
         split_cols=False):
    m, n, k = dims
    tiles = tiles or (_tile(m, 512), _tile(n, 1024), _tile(k, 1024))

    def epilogue(accs, extras, vecs):
        if separate:
            return list(accs)
        r = accs[0]
        if extras:
            r = r + add_coef * extras[0]
        return [r]

    a_list = [a0] if a1 is None else [a0, a1]
    b_list = [b0] if b1 is None else [b0, b1]
    pairs = [(0, 0, 0), (len(a_list) - 1, len(b_list) - 1, 1 if separate else 0)]
    return _mm(name, mode, dims, tiles, a_list, b_list, pairs, 2 if separate else 1, epilogue,
               [out_dtype, out_dtype] if separate else [out_dtype], extras=[] if add is None else [add],
               split_cols=split_cols)


def _row_call(name, body, s, ins, params, outs, accs):
    tm = ROW_TILE
    ins = [a if isinstance(a, tuple) else (a, a.shape[1], 0) for a in ins]
    in_specs = [pl.BlockSpec((tm, width), lambda i, cb=cb: (i, cb)) for _, width, cb in ins]
    ins = [a for a, _, _ in ins]
    in_specs += [pl.BlockSpec(p.shape, lambda i, nd=p.ndim: (0,) * nd) for p in params]
    out_specs = [pl.BlockSpec((tm, o.shape[1]), lambda i: (i, 0)) for o in outs]
    out_specs += [pl.BlockSpec(a.shape, lambda i, nd=len(a.shape): (0,) * nd) for a in accs]
    return pl.pallas_call(
        body,
        name=name,
        grid=(s // tm,),
        in_specs=in_specs,
        out_specs=out_specs,
        out_shape=list(outs) + list(accs),
        compiler_params=_params("arbitrary"),
    )(*ins, *params)


def _zero_at_first(refs):
    @pl.when(pl.program_id(0) == 0)
    def _():
        for r in refs:
            r[...] = jnp.zeros_like(r)


def _ln_bwd(name, r, dh, gamma):
    s = r.shape[0]

    def body(r_ref, dh_ref, g_ref, dr_ref, dg_ref, db_ref):
        _zero_at_first([dg_ref, db_ref])
        rr = r_ref[...]
        dy = dh_ref[...]
        mu = jnp.mean(rr, axis=-1, keepdims=True)
        xc = rr - mu
        rstd = lax.rsqrt(jnp.mean(xc * xc, axis=-1, keepdims=True) + LN_EPS)
        xhat = xc * rstd
        dxh = dy * g_ref[...]
        dr_ref[...] = rstd * (dxh - jnp.mean(dxh, axis=-1, keepdims=True)
                              - xhat * jnp.mean(dxh * xhat, axis=-1, keepdims=True))
        dg_ref[...] += jnp.sum(dy * xhat, axis=0, keepdims=True)
        db_ref[...] += jnp.sum(dy, axis=0, keepdims=True)

    return _row_call(name, body, s, [r, dh], [gamma], [_sds((s, D_MODEL))], [_sds((1, D_MODEL)), _sds((1, D_MODEL))])


def _loss_head(name, y, target):
    s = y.shape[0]

    def body(y_ref, t_ref, dy_ref, l_ref):
        _zero_at_first([l_ref])
        e = y_ref[...] - t_ref[...]
        dy_ref[...] = e / D_MODEL
        l_ref[...] += 0.5 * jnp.sum(jnp.mean(e * e, axis=-1, keepdims=True), axis=0, keepdims=True)

    return _row_call(name, body, s, [y, target], [], [_sds((s, D_MODEL))], [_sds((1, 128))])


def _expm1(x):
    series = x * (1.0 + x / 2.0 * (1.0 + x / 3.0 * (1.0 + x / 4.0 * (1.0 + x / 5.0 * (1.0 + x / 6.0 * (1.0 + x / 7.0))))))
    return jnp.where(jnp.abs(x) < 0.25, series, jnp.exp(x) - 1.0)


def _gates_fn(xa, wa, wx, ba, bx, lam, tap_a, tap_x):
    xb = xa.astype(bf16)
    r = jax.nn.sigmoid(jnp.dot(xb, wa, preferred_element_type=f32) + ba + tap_a)
    i = jax.nn.sigmoid(jnp.dot(xb, wx, preferred_element_type=f32) + bx + tap_x)
    log_a = -RG_C * r * jax.nn.softplus(-lam)
    a = jnp.exp(log_a)
    gated = jnp.sqrt(-_expm1(2.0 * log_a)) * (i * xa)
    return a, gated


def _rg_gates(name, xa, wa, wx, ba, bx, lam):
    s = xa.shape[0]

    def body(xa_ref, wa_ref, wx_ref, ba_ref, bx_ref, lam_ref, a_ref, g_ref):
        a, g = _gates_fn(xa_ref[...], wa_ref[...], wx_ref[...], ba_ref[...], bx_ref[...], lam_ref[...], 0.0, 0.0)
        a_ref[...] = a
        g_ref[...] = g

    return _row_call(name, body, s, [xa], [wa, wx, ba, bx, lam], [_sds((s, D_A)), _sds((s, D_A))], [])


def _rg_gates_bwd(name, xa, ga, h_prev, wa, wx, ba, bx, lam):
    s = xa.shape[0]

    def body(xa_ref, ga_ref, hp_ref, wa_ref, wx_ref, ba_ref, bx_ref, lam_ref,
             dxa_ref, dwa_ref, dwx_ref, dba_ref, dbx_ref, dlam_ref):
        _zero_at_first([dwa_ref, dwx_ref, dba_ref, dbx_ref, dlam_ref])
        xa_v = xa_ref[...]
        zero = jnp.zeros((xa_v.shape[0], D_A), f32)
        fn = lambda x, ba_, bx_, lam_, ta, tx: _gates_fn(x, wa_ref[...], wx_ref[...], ba_, bx_, lam_, ta, tx)
        _, vjp = jax.vjp(fn, xa_v, ba_ref[...], bx_ref[...], lam_ref[...], zero, zero)
        gav = ga_ref[...]
        dxa, dba, dbx, dlam, dta, dtx = vjp((gav * hp_ref[...], gav))
        dxa_ref[...] = dxa
        xb = xa_v.astype(bf16)
        dwa_ref[...] += lax.dot_general(xb, dta.astype(bf16), _DN["tn"], preferred_element_type=f32)
        dwx_ref[...] += lax.dot_general(xb, dtx.astype(bf16), _DN["tn"], preferred_element_type=f32)
        dba_ref[...] += dba
        dbx_ref[...] += dbx
        dlam_ref[...] += dlam

    return _row_call(name, body, s, [xa, ga, h_prev], [wa, wx, ba, bx, lam], [_sds((s, D_A))],
                     [_sds((D_A, D_A)), _sds((D_A, D_A)), _sds((1, D_A)), _sds((1, D_A)), _sds((1, D_A))])


def _rms(v, g):
    return v * lax.rsqrt(jnp.mean(v * v, axis=-1, keepdims=True) + RMS_EPS) * g


def _mix_out_fn(ag, ha, ob, hre, him, cu, d, gn, tap_y, tap_gl, wcr, wci, wglu):
    out_a = jax.nn.gelu(ag) * ha
    y = (jnp.dot(hre.astype(bf16), wcr, preferred_element_type=f32)
         + jnp.dot(him.astype(bf16), wci, preferred_element_type=f32) + d * cu + tap_y)
    y2 = jax.nn.gelu(y)
    gl = jnp.dot(y2.astype(bf16), wglu, preferred_element_type=f32) + tap_gl
    out_c = y2 * jax.nn.sigmoid(gl)
    o = jnp.concatenate([_rms(out_a, gn[:, :D_A]), _rms(ob, gn[:, D_A:D_A + D_B]), _rms(out_c, gn[:, D_A + D_B:])],
                        axis=-1)
    return o, y2


def _mix_out(name, ag, ha, ob, hre, him, cu, d, gn, wcr, wci, wglu):
    s = ha.shape[0]

    def body(ag_ref, ha_ref, ob_ref, hre_ref, him_ref, cu_ref, d_ref, gn_ref, wcr_ref, wci_ref, wglu_ref, o_ref):
        o, _ = _mix_out_fn(ag_ref[...], ha_ref[...], ob_ref[...], hre_ref[...], him_ref[...], cu_ref[...], d_ref[...],
                           gn_ref[...], 0.0, 0.0, wcr_ref[...], wci_ref[...], wglu_ref[...])
        o_ref[...] = o.astype(o_ref.dtype)

    return _row_call(name, body, s, [ag, ha, ob, hre, him, cu], [d, gn, wcr, wci, wglu], [_sds((s, D_MODEL), bf16)], [])[0]


def _mix_out_bwd(name, do, ag, ha, ob, hre, him, cu, d, gn, wcr, wci, wglu):
    s = ha.shape[0]

    def body(do_ref, ag_ref, ha_ref, ob_ref, hre_ref, him_ref, cu_ref, d_ref, gn_ref, wcr_ref, wci_ref, wglu_ref,
             dag_ref, dha_ref, dob_ref, dhre_ref, dhim_ref, dcu_ref, dwcr_ref, dwci_ref, dwglu_ref, dd_ref, dgn_ref):
        _zero_at_first([dwcr_ref, dwci_ref, dwglu_ref, dd_ref, dgn_ref])
        tm = ag_ref.shape[0]
        zero = jnp.zeros((tm, D_C), f32)
        hre_v, him_v = hre_ref[...], him_ref[...]
        fn = lambda *a: _mix_out_fn(*a, wcr_ref[...], wci_ref[...], wglu_ref[...])
        _, vjp, y2 = jax.vjp(fn, ag_ref[...], ha_ref[...], ob_ref[...], hre_v, him_v, cu_ref[...], d_ref[...],
                             gn_ref[...], zero, zero, has_aux=True)
        dag, dha, dob, dhre, dhim, dcu, dd, dgn, dy, dgl = vjp(do_ref[...])
        dag_ref[...] = dag
        dha_ref[...] = dha
        dob_ref[...] = dob
        dhre_ref[...] = dhre
        dhim_ref[...] = dhim
        dcu_ref[...] = dcu
        dyb = dy.astype(bf16)
        dwcr_ref[...] += lax.dot_general(hre_v.astype(bf16), dyb, _DN["tn"], preferred_element_type=f32)
        dwci_ref[...] += lax.dot_general(him_v.astype(bf16), dyb, _DN["tn"], preferred_element_type=f32)
        dwglu_ref[...] += lax.dot_general(y2.astype(bf16), dgl.astype(bf16), _DN["tn"], preferred_element_type=f32)
        dd_ref[...] += dd
        dgn_ref[...] += dgn

    outs = [_sds((s, D_A)), _sds((s, D_A)), _sds((s, D_B)), _sds((s, S5_LANES)), _sds((s, S5_LANES)), _sds((s, D_C))]
    accs = [_sds((S5_LANES, D_C)), _sds((S5_LANES, D_C)), _sds((D_C, D_C)), _sds((1, D_C)), _sds((1, D_MODEL))]
    return _row_call(name, body, s, [do, ag, ha, ob, hre, him, cu], [d, gn, wcr, wci, wglu], outs, accs)


def _log_f(name, f, bf):
    s = f[0].shape[0]

    def body(f_ref, b_ref, o_ref):
        o_ref[...] = jax.nn.log_sigmoid(f_ref[...] + b_ref[...])

    return _row_call(name, body, s, [f], [bf], [_sds((s, 128))], [])[0]


def _log_f_bwd(name, dlf, f, bf):
    s = dlf.shape[0]

    def body(dl_ref, f_ref, b_ref, df_ref, db_ref):
        _zero_at_first([db_ref])
        df = dl_ref[...] * jax.nn.sigmoid(-(f_ref[...] + b_ref[...]))
        df_ref[...] = df
        db_ref[...] += jnp.sum(df, axis=0, keepdims=True)

    return _row_call(name, body, s, [dlf, f], [bf], [_sds((s, 128))], [_sds((1, 128))])


def _s5_decay_grad(name, h_re, h_im, g_re, g_im):
    s = g_re.shape[0]
    tm = ROW_TILE

    def body(hr_ref, hi_ref, hhr_ref, hhi_ref, gr_ref, gi_ref, dr_ref, di_ref):
        i = pl.program_id(0)
        _zero_at_first([dr_ref, di_ref])

        def previous(h_ref, halo_ref):
            halo = jnp.where(i == 0, 0.0, halo_ref[...])
            return pltpu.roll(jnp.concatenate([halo, h_ref[...]], axis=0), 1, 0)[8:, :]

        hr, hi, gr, gi = previous(hr_ref, hhr_ref), previous(hi_ref, hhi_ref), gr_ref[...], gi_ref[...]
        dr_ref[...] += jnp.sum(hr * gr + hi * gi, axis=0, keepdims=True)
        di_ref[...] += jnp.sum(hr * gi - hi * gr, axis=0, keepdims=True)

    rows = pl.BlockSpec((tm, S5_LANES), lambda i: (i, 0))
    halo = pl.BlockSpec((8, S5_LANES), lambda i: (jnp.maximum(i * (tm // 8) - 1, 0), 0))
    acc = pl.BlockSpec((1, S5_LANES), lambda i: (0, 0))
    return pl.pallas_call(
        body,
        name=name,
        grid=(s // tm,),
        in_specs=[rows, rows, halo, halo, rows, rows],
        out_specs=[acc, acc],
        out_shape=[_sds((1, S5_LANES)), _sds((1, S5_LANES))],
        compiler_params=_params("arbitrary"),
    )(h_re, h_im, h_re, h_im, g_re, g_im)


def _conv_fwd(name, ax, w, b):
    s = ax.shape[0]
    tm = ROW_TILE

    def body(x_ref, halo_ref, w_ref, b_ref, o_ref):
        i = pl.program_id(0)
        x = x_ref[...]
        halo = jnp.where(i == 0, 0.0, halo_ref[...])
        ext = jnp.concatenate([halo, x], axis=0)
        acc = b_ref[...] + w_ref[3:4, :] * x
        for k in range(CONV_WIDTH - 1):
            acc = acc + w_ref[k:k + 1, :] * pltpu.roll(ext, CONV_WIDTH - 1 - k, 0)[8:, :]
        o_ref[...] = acc

    return pl.pallas_call(
        body,
        name=name,
        grid=(s // tm,),
        in_specs=[pl.BlockSpec((tm, D_A), lambda i: (i, 0)),
                  pl.BlockSpec((8, D_A), lambda i: (jnp.maximum(i * (tm // 8) - 1, 0), 0)),
                  pl.BlockSpec((CONV_WIDTH, D_A), lambda i: (0, 0)),
                  pl.BlockSpec((1, D_A), lambda i: (0, 0))],
        out_specs=pl.BlockSpec((tm, D_A), lambda i: (i, 0)),
        out_shape=_sds((s, D_A)),
        compiler_params=_params("arbitrary"),
    )(ax, ax, w, b)


def _conv_bwd(name, dxa, ax, w):
    s = ax.shape[0]
    tm = ROW_TILE
    nblk = s // tm

    def body(dx_ref, dnext_ref, x_ref, halo_ref, w_ref, dax_ref, dw_ref):
        i = pl.program_id(0)
        _zero_at_first([dw_ref])
        dx = dx_ref[...]
        dnext = jnp.where(i == nblk - 1, 0.0, dnext_ref[...])
        dext = jnp.concatenate([dx, dnext], axis=0)
        x = x_ref[...]
        halo = jnp.where(i == 0, 0.0, halo_ref[...])
        ext = jnp.concatenate([halo, x], axis=0)
        acc = w_ref[3:4, :] * dx
        dw_ref[3:4, :] += jnp.sum(dx * x, axis=0, keepdims=True)
        for k in range(CONV_WIDTH - 1):
            sh = CONV_WIDTH - 1 - k
            acc = acc + w_ref[k:k + 1, :] * pltpu.roll(dext, tm + 8 - sh, 0)[:tm, :]
            dw_ref[k:k + 1, :] += jnp.sum(dx * pltpu.roll(ext, sh, 0)[8:, :], axis=0, keepdims=True)
        dw_ref[4:5, :] += jnp.sum(dx, axis=0, keepdims=True)
        dax_ref[...] = acc

    return pl.pallas_call(
        body,
        name=name,
        grid=(nblk,),
        in_specs=[pl.BlockSpec((tm, D_A), lambda i: (i, 0)),
                  pl.BlockSpec((8, D_A), lambda i: (jnp.minimum((i + 1) * (tm // 8), s // 8 - 1), 0)),
                  pl.BlockSpec((tm, D_A), lambda i: (i, 0)),
                  pl.BlockSpec((8, D_A), lambda i: (jnp.maximum(i * (tm // 8) - 1, 0), 0)),
                  pl.BlockSpec((CONV_WIDTH, D_A), lambda i: (0, 0))],
        out_specs=[pl.BlockSpec((tm, D_A), lambda i: (i, 0)), pl.BlockSpec((8, D_A), lambda i: (0, 0))],
        out_shape=[_sds((s, D_A)), _sds((8, D_A))],
        compiler_params=_params("arbitrary"),
    )(dxa, dxa, ax, ax, w)


SCAN_ROWS = 512


def _row_in_tile(shape):
    return lax.broadcasted_iota(jnp.int32, shape, 0) % 8


def _lin_scan(name, a, b, reverse):
    s, c = a.shape
    t = min(SCAN_ROWS, s)
    nb = s // t

    def body(a_ref, b_ref, h_ref, p_ref, carry_ref):
        @pl.when(pl.program_id(0) == 0)
        def _():
            carry_ref[...] = jnp.zeros_like(carry_ref)

        row = _row_in_tile((t, c))
        p = a_ref[...]
        h = b_ref[...]
        for d in (1, 2, 4):
            keep = (row < 8 - d) if reverse else (row >= d)
            shift = (t - d) if reverse else d
            h = h + jnp.where(keep, p * pltpu.roll(h, shift, 0), 0.0)
            p = jnp.where(keep, p * pltpu.roll(p, shift, 0), p)
        h_ref[...] = h
        p_ref[...] = p
        edge = 0 if reverse else 7

        def tile(k, carry):
            kk = (t // 8 - 1 - k) if reverse else k
            r0 = pl.multiple_of(kk * 8, 8)
            hh = h_ref[pl.ds(r0, 8), :] + p_ref[pl.ds(r0, 8), :] * carry
            h_ref[pl.ds(r0, 8), :] = hh
            return jnp.broadcast_to(hh[edge:edge + 1, :], (8, c))

        carry_ref[...] = lax.fori_loop(0, t // 8, tile, carry_ref[...])

    spec = pl.BlockSpec((t, c), (lambda i: (nb - 1 - i, 0)) if reverse else (lambda i: (i, 0)))
    (out,) = _call(
        body,
        name=name,
        grid=(nb,),
        in_specs=[spec, spec],
        out_specs=[spec],
        out_shape=[_sds((s, c))],
        scratch_shapes=[pltpu.VMEM((t, c), f32), pltpu.VMEM((8, c), f32)],
        compiler_params=_params("arbitrary"),
    )(a, b)
    return out


def _s5_scan(name, b_re, b_im, a_re, a_im, reverse):
    s, c = b_re.shape
    t = min(SCAN_ROWS, s)
    nb = s // t

    def body(br_ref, bi_ref, ar_ref, ai_ref, hr_ref, hi_ref, cr_ref, ci_ref):
        @pl.when(pl.program_id(0) == 0)
        def _():
            cr_ref[...] = jnp.zeros_like(cr_ref)
            ci_ref[...] = jnp.zeros_like(ci_ref)

        ar1, ai1 = ar_ref[...], ai_ref[...]
        pows = [(ar1, ai1)]
        for _ in range(7):
            pr, pi = pows[-1]
            pows.append((pr * ar1 - pi * ai1, pr * ai1 + pi * ar1))
        row8 = lax.broadcasted_iota(jnp.int32, (8, c), 0)
        wr = jnp.zeros((8, c), f32)
        wi = jnp.zeros((8, c), f32)
        for r in range(8):
            pr, pi = pows[(7 - r) if reverse else r]
            wr = jnp.where(row8 == r, pr, wr)
            wi = jnp.where(row8 == r, pi, wi)
        row = _row_in_tile((t, c))
        hr = br_ref[...]
        hi = bi_ref[...]
        for d in (1, 2, 4):
            keep = (row < 8 - d) if reverse else (row >= d)
            shift = (t - d) if reverse else d
            pr, pi = pows[d - 1]
            cr = jnp.where(keep, pr, 0.0)
            ci = jnp.where(keep, pi, 0.0)
            sr = pltpu.roll(hr, shift, 0)
            si = pltpu.roll(hi, shift, 0)
            hr, hi = hr + cr * sr - ci * si, hi + cr * si + ci * sr
        hr_ref[...] = hr
        hi_ref[...] = hi
        edge = 0 if reverse else 7

        def tile(k, carry):
            car_r, car_i = carry
            kk = (t // 8 - 1 - k) if reverse else k
            r0 = pl.multiple_of(kk * 8, 8)
            xr = hr_ref[pl.ds(r0, 8), :] + wr * car_r - wi * car_i
            xi = hi_ref[pl.ds(r0, 8), :] + wr * car_i + wi * car_r
            hr_ref[pl.ds(r0, 8), :] = xr
            hi_ref[pl.ds(r0, 8), :] = xi
            return (jnp.broadcast_to(xr[edge:edge + 1, :], (8, c)), jnp.broadcast_to(xi[edge:edge + 1, :], (8, c)))

        car_r, car_i = lax.fori_loop(0, t // 8, tile, (cr_ref[...], ci_ref[...]))
        cr_ref[...] = car_r
        ci_ref[...] = car_i

    spec = pl.BlockSpec((t, c), (lambda i: (nb - 1 - i, 0)) if reverse else (lambda i: (i, 0)))
    vspec = pl.BlockSpec((1, c), lambda i: (0, 0))
    hr, hi = _call(
        body,
        name=name,
        grid=(nb,),
        in_specs=[spec, spec, vspec, vspec],
        out_specs=[spec, spec],
        out_shape=[_sds((s, c)), _sds((s, c))],
        scratch_shapes=[pltpu.VMEM((8, c), f32), pltpu.VMEM((8, c), f32)],
        compiler_params=_params("arbitrary"),
    )(b_re, b_im, a_re, a_im)
    return hr, hi


ATT_FEAT = 128
ATT_TQ = 1024
ATT_TK = 1024
ATT_TK_KEY_SIDE = 512


def _att_tiles(s, key_side=False):
    tq = min(ATT_TQ, s)
    tk = min(ATT_TK_KEY_SIDE if key_side else ATT_TK, tq)
    return tq, tk, tq // tk


def _keys_le_queries(tk, tq, k0, q0):
    row = lax.broadcasted_iota(jnp.int32, (tk, tq), 0) + k0
    col = lax.broadcasted_iota(jnp.int32, (tk, tq), 1) + q0
    return row <= col


def _attn_fwd_t(name, qt, k_aug, vt):
    h, s, _ = k_aug.shape
    tq, tk, ratio = _att_tiles(s)

    def body(qt_ref, k_ref, vt_ref, o_ref, lse_ref):
        qi = pl.program_id(1)
        qt = qt_ref[...]

        def block(kb, carry, masked):
            m, l, acc = carry
            ks = pl.multiple_of(kb * tk, tk)
            st = jnp.dot(k_ref[pl.ds(ks, tk), :], qt, preferred_element_type=f32)
            if masked:
                st = jnp.where(_keys_le_queries(tk, tq, ks, qi * tq), st, -jnp.inf)
            mn = jnp.maximum(m, jnp.max(st, axis=0, keepdims=True))
            p = jnp.exp(st - mn)
            al = jnp.exp(m - mn)
            l = al * l + jnp.sum(p, axis=0, keepdims=True)
            acc = al * acc + jnp.dot(vt_ref[kb], p.astype(bf16), preferred_element_type=f32)
            return mn, l, acc

        init = (jnp.full((1, tq), -jnp.inf, f32), jnp.zeros((1, tq), f32), jnp.zeros((HEAD_DIM, tq), f32))
        first = lax.fori_loop(0, qi * ratio, lambda kb, c: block(kb, c, False), init)
        m, l, acc = lax.fori_loop(qi * ratio, (qi + 1) * ratio, lambda kb, c: block(kb, c, True), first)
        o_ref[...] = acc / l
        lse_ref[...] = m + jnp.log(l)

    return _call(
        body,
        name=name,
        grid=(h, s // tq),
        in_specs=[pl.BlockSpec((None, None, ATT_FEAT, tq), lambda hh, i: (hh, i, 0, 0)),
                  pl.BlockSpec((None, s, ATT_FEAT), lambda hh, i: (hh, 0, 0)),
                  pl.BlockSpec((None, s // tk, HEAD_DIM, tk), lambda hh, i: (hh, 0, 0, 0))],
        out_specs=[pl.BlockSpec((None, HEAD_DIM, tq), lambda hh, i: (hh, 0, i)),
                   pl.BlockSpec((None, 1, tq), lambda hh, i: (hh, 0, i))],
        out_shape=[_sds((h, HEAD_DIM, s)), _sds((h, 1, s))],
        compiler_params=_params("parallel", "arbitrary"),
    )(qt, k_aug, vt)


def _attn_bwd_dq_t(name, qt, k_aug, v, kt, ot, dot_, lse):
    h, s, _ = k_aug.shape
    tq, tk, ratio = _att_tiles(s)

    def body(qt_ref, k_ref, v_ref, kt_ref, o_ref, do_ref, lse_ref, dq_ref, dl_ref):
        qi = pl.program_id(1)
        qt = qt_ref[...]
        dob = do_ref[...]
        delta = jnp.sum(dob.astype(f32) * o_ref[...], axis=0, keepdims=True)
        lse_v = lse_ref[...]

        def block(kb, carry, masked):
            dq, psum = carry
            ks = pl.multiple_of(kb * tk, tk)
            st = jnp.dot(k_ref[pl.ds(ks, tk), :], qt, preferred_element_type=f32)
            p = jnp.exp(st - lse_v)
            if masked:
                p = jnp.where(_keys_le_queries(tk, tq, ks, qi * tq), p, 0.0)
            dp = jnp.dot(v_ref[pl.ds(ks, tk), :], dob, preferred_element_type=f32)
            ds = p * (dp - delta)
            return (dq + jnp.dot(kt_ref[kb], ds.astype(bf16), preferred_element_type=f32),
                    psum + jnp.sum(p * dp, axis=0, keepdims=True))

        carry = lax.fori_loop(0, qi * ratio, lambda kb, c: block(kb, c, False),
                              (jnp.zeros((HEAD_DIM, tq), f32), jnp.zeros((1, tq), f32)))
        dq, psum = lax.fori_loop(qi * ratio, (qi + 1) * ratio, lambda kb, c: block(kb, c, True), carry)
        dq_ref[...] = dq * ATT_SCALE
        dl_ref[...] = psum

    qspec = pl.BlockSpec((None, HEAD_DIM, tq), lambda hh, i: (hh, 0, i))
    rspec = pl.BlockSpec((None, 1, tq), lambda hh, i: (hh, 0, i))
    return _call(
        body,
        name=name,
        grid=(h, s // tq),
        in_specs=[pl.BlockSpec((None, None, ATT_FEAT, tq), lambda hh, i: (hh, i, 0, 0)),
                  pl.BlockSpec((None, s, ATT_FEAT), lambda hh, i: (hh, 0, 0)),
                  pl.BlockSpec((None, s, HEAD_DIM), lambda hh, i: (hh, 0, 0)),
                  pl.BlockSpec((None, s // tk, HEAD_DIM, tk), lambda hh, i: (hh, 0, 0, 0)),
                  qspec, pl.BlockSpec((None, None, HEAD_DIM, tq), lambda hh, i: (hh, i, 0, 0)), rspec],
        out_specs=[qspec, rspec],
        out_shape=[_sds((h, HEAD_DIM, s)), _sds((h, 1, s))],
        compiler_params=_params("parallel", "arbitrary"),
    )(qt, k_aug, v, kt, ot, dot_, lse)


def _attn_bwd_dkv_t(name, qt_blocks, k_aug, v, qh, do, dot_blocks, lse, delta):
    h, s, _ = k_aug.shape
    tq, tk, ratio = _att_tiles(s, key_side=True)
    nq = s // tq

    def body(qt_ref, k_ref, v_ref, q_ref, do_ref, dot_ref, lse_ref, dl_ref, dk_ref, dv_ref, dck_ref, dsum_ref):
        kj = pl.program_id(1)
        kk = k_ref[...]
        vv = v_ref[...]
        dsum_ref[...] = jnp.zeros_like(dsum_ref)

        def block(qi, carry, masked):
            dk, dv = carry
            qs = pl.multiple_of(qi * tq, tq)
            st = jnp.dot(kk, qt_ref[qi], preferred_element_type=f32)
            p = jnp.exp(st - lse_ref[qi])
            if masked:
                p = jnp.where(_keys_le_queries(tk, tq, kj * tk, qs), p, 0.0)
            dv = dv + jnp.dot(p.astype(bf16), do_ref[pl.ds(qs, tq), :], preferred_element_type=f32)
            dp = jnp.dot(vv, dot_ref[qi], preferred_element_type=f32)
            ds = p * (dp - dl_ref[qi])
            dsum_ref[...] += ds
            dk = dk + jnp.dot(ds.astype(bf16), q_ref[pl.ds(qs, tq), :], preferred_element_type=f32)
            return dk, dv

        first = kj // ratio
        carry = block(first, (jnp.zeros((tk, HEAD_DIM), f32), jnp.zeros((tk, HEAD_DIM), f32)), True)
        dk, dv = lax.fori_loop(first + 1, nq, lambda qi, c: block(qi, c, False), carry)
        dk_ref[...] = dk
        dv_ref[...] = dv
        col = jnp.sum(dsum_ref[...], axis=1, keepdims=True)
        dck_ref[...] = -jnp.transpose(jnp.broadcast_to(col, (tk, 128)))[0:1, :]

    full = lambda shape: pl.BlockSpec((None,) + shape, lambda hh, j: (hh,) + (0,) * len(shape))
    kspec = pl.BlockSpec((None, tk, HEAD_DIM), lambda hh, j: (hh, j, 0))
    return _call(
        body,
        name=name,
        grid=(h, s // tk),
        in_specs=[full((nq, ATT_FEAT, tq)),
                  pl.BlockSpec((None, tk, ATT_FEAT), lambda hh, j: (hh, j, 0)),
                  kspec, full((s, HEAD_DIM)), full((s, HEAD_DIM)), full((nq, HEAD_DIM, tq)),
                  full((nq, 1, tq)), full((nq, 1, tq))],
        out_specs=[kspec, kspec, pl.BlockSpec((None, None, 1, tk), lambda hh, j: (hh, j, 0, 0))],
        out_shape=[_sds((h, s, HEAD_DIM)), _sds((h, s, HEAD_DIM)), _sds((h, s // tk, 1, tk))],
        scratch_shapes=[pltpu.VMEM((tk, tq), f32)],
        compiler_params=_params("parallel", "arbitrary"),
    )(qt_blocks, k_aug, v, qh, do, dot_blocks, lse, delta)


C_LANES = 128


def _selections():
    h = jnp.arange(N_HEADS)[:, None, None]
    row = jnp.arange(D_B + 3 * C_LANES)[None, :, None]
    col = jnp.arange(ATT_FEAT)[None, None, :]
    head_col = (row < D_B) & (row // HEAD_DIM == h) & (col == row % HEAD_DIM)

    def c_part(p, lane0):
        return (row == D_B + p * C_LANES + h) & (col == lane0 + p)

    c_q = c_part(0, HEAD_DIM) | c_part(1, HEAD_DIM) | c_part(2, HEAD_DIM)
    c_k = c_part(0, HEAD_DIM + 3) | c_part(1, HEAD_DIM + 3) | c_part(2, HEAD_DIM + 3)
    sel_q = (head_col | c_q).astype(bf16)
    sel_k = head_col.astype(bf16) - c_k.astype(bf16)
    sel_h = head_col[:, :D_B, :HEAD_DIM].astype(bf16)
    lane = jnp.arange(ATT_FEAT)
    ones_q = ((lane >= HEAD_DIM + 3) & (lane < HEAD_DIM + 6)).astype(f32)
    ones_k = ((lane >= HEAD_DIM) & (lane < HEAD_DIM + 3)).astype(f32)
    return dict(sel_qt=sel_q.transpose(0, 2, 1), sel_k=sel_k, sel_h=sel_h, sel_ht=sel_h.transpose(0, 2, 1),
                ones_q=ones_q.reshape(ATT_FEAT, 1), ones_k=ones_k.reshape(1, ATT_FEAT))


def _attn_prep(name, z, c, sel):
    s = z.shape[0]
    tq, tk, ratio = _att_tiles(s)

    def body(q_ref, k_ref, v_ref, c_ref, sqt_ref, sk_ref, sh_ref, sht_ref, oq_ref, ok_ref,
             qt_out, ka_out, kt_out, vt_out, v_out, qh_out):
        cv = c_ref[...]
        hi = cv.astype(bf16)
        r1 = cv - hi.astype(f32)
        mid = r1.astype(bf16)
        lo = (r1 - mid.astype(f32)).astype(bf16)
        qs = (q_ref[...] * ATT_SCALE).astype(bf16)
        kb = k_ref[...].astype(bf16)
        vb = v_ref[...].astype(bf16)
        xq = jnp.concatenate([qs, hi, mid, lo], axis=-1)
        xk = jnp.concatenate([kb, hi, mid, lo], axis=-1)
        for h in range(N_HEADS):
            qt = lax.dot_general(sqt_ref[h], xq, _DN["nt"], preferred_element_type=f32) + oq_ref[...]
            qt_out[h, 0] = qt.astype(bf16)
            ka_out[h] = (jnp.dot(xk, sk_ref[h], preferred_element_type=f32) + ok_ref[...]).astype(bf16)
            kt = lax.dot_general(sht_ref[h], kb, _DN["nt"], preferred_element_type=f32).astype(bf16)
            vt = lax.dot_general(sht_ref[h], vb, _DN["nt"], preferred_element_type=f32).astype(bf16)
            for j in range(ratio):
                kt_out[h, j] = kt[:, j * tk:(j + 1) * tk]
                vt_out[h, j] = vt[:, j * tk:(j + 1) * tk]
            v_out[h] = jnp.dot(vb, sh_ref[h], preferred_element_type=f32).astype(bf16)
            qh_out[h] = jnp.dot(qs, sh_ref[h], preferred_element_type=f32).astype(bf16)

    whole = lambda a: pl.BlockSpec(a.shape, lambda i, nd=a.ndim: (0,) * nd)
    consts = [sel["sel_qt"], sel["sel_k"], sel["sel_h"], sel["sel_ht"], sel["ones_q"], sel["ones_k"]]
    return pl.pallas_call(
        body,
        name=name,
        grid=(s // tq,),
        in_specs=[pl.BlockSpec((tq, D_B), lambda i: (i, 2)), pl.BlockSpec((tq, D_B), lambda i: (i, 3)),
                  pl.BlockSpec((tq, D_B), lambda i: (i, 4)), pl.BlockSpec((tq, C_LANES), lambda i: (i, 0))]
        + [whole(a) for a in consts],
        out_specs=[pl.BlockSpec((N_HEADS, 1, ATT_FEAT, tq), lambda i: (0, i, 0, 0)),
                   pl.BlockSpec((N_HEADS, tq, ATT_FEAT), lambda i: (0, i, 0)),
                   pl.BlockSpec((N_HEADS, ratio, HEAD_DIM, tk), lambda i: (0, i, 0, 0)),
                   pl.BlockSpec((N_HEADS, ratio, HEAD_DIM, tk), lambda i: (0, i, 0, 0)),
                   pl.BlockSpec((N_HEADS, tq, HEAD_DIM), lambda i: (0, i, 0)),
                   pl.BlockSpec((N_HEADS, tq, HEAD_DIM), lambda i: (0, i, 0))],
        out_shape=[_sds((N_HEADS, s // tq, ATT_FEAT, tq), bf16), _sds((N_HEADS, s, ATT_FEAT), bf16),
                   _sds((N_HEADS, s // tk, HEAD_DIM, tk), bf16), _sds((N_HEADS, s // tk, HEAD_DIM, tk), bf16),
                   _sds((N_HEADS, s, HEAD_DIM), bf16), _sds((N_HEADS, s, HEAD_DIM), bf16)],
        compiler_params=_params("parallel"),
    )(z, z, z, c, *consts)


def _attn_do_prep(name, dob, sel):
    s = dob.shape[0]
    tq = _att_tiles(s)[0]

    def body(do_ref, sh_ref, sht_ref, dot_out, do_out):
        db = do_ref[...].astype(bf16)
        for h in range(N_HEADS):
            dot_out[h, 0] = lax.dot_general(sht_ref[h], db, _DN["nt"], preferred_element_type=f32).astype(bf16)
            do_out[h] = jnp.dot(db, sh_ref[h], preferred_element_type=f32).astype(bf16)

    whole = lambda a: pl.BlockSpec(a.shape, lambda i, nd=a.ndim: (0,) * nd)
    return pl.pallas_call(
        body,
        name=name,
        grid=(s // tq,),
        in_specs=[pl.BlockSpec((tq, D_B), lambda i: (i, 0)), whole(sel["sel_h"]), whole(sel["sel_ht"])],
        out_specs=[pl.BlockSpec((N_HEADS, 1, HEAD_DIM, tq), lambda i: (0, i, 0, 0)),
                   pl.BlockSpec((N_HEADS, tq, HEAD_DIM), lambda i: (0, i, 0))],
        out_shape=[_sds((N_HEADS, s // tq, HEAD_DIM, tq), bf16), _sds((N_HEADS, s, HEAD_DIM), bf16)],
        compiler_params=_params("parallel"),
    )(dob, sel["sel_h"], sel["sel_ht"])


def _dz_assemble(name, dax, dag, dqt, dkh, dvh, df, dcu, sel):
    s = dax.shape[0]
    tm = _tile(s, 512)

    def body(dax_ref, dag_ref, dqt_ref, dk_ref, dv_ref, df_ref, dcu_ref, sht_ref, o_ref):
        dq = jnp.zeros((tm, D_B), f32)
        dk = jnp.zeros((tm, D_B), f32)
        dv = jnp.zeros((tm, D_B), f32)
        for h in range(N_HEADS):
            place = sht_ref[h]
            dq = dq + lax.dot_general(dqt_ref[h].astype(bf16), place, _DN["tn"], preferred_element_type=f32)
            dk = dk + jnp.dot(dk_ref[h].astype(bf16), place, preferred_element_type=f32)
            dv = dv + jnp.dot(dv_ref[h].astype(bf16), place, preferred_element_type=f32)
        pieces = [dax_ref[...], dag_ref[...], dq, dk, dv, df_ref[...], dcu_ref[...]]
        off = 0
        for p in pieces:
            o_ref[:, off:off + p.shape[1]] = p.astype(bf16)
            off += p.shape[1]

    rows = lambda c_: pl.BlockSpec((tm, c_), lambda i: (i, 0))
    heads = pl.BlockSpec((N_HEADS, tm, HEAD_DIM), lambda i: (0, i, 0))
    return pl.pallas_call(
        body,
        name=name,
        grid=(s // tm,),
        in_specs=[rows(D_A), rows(D_A), pl.BlockSpec((N_HEADS, HEAD_DIM, tm), lambda i: (0, 0, i)), heads, heads,
                  rows(128), rows(D_C), pl.BlockSpec(sel["sel_ht"].shape, lambda i: (0, 0, 0))],
        out_specs=rows(N_IN_P),
        out_shape=_sds((s, N_IN_P), bf16),
        compiler_params=_params("parallel"),
    )(dax, dag, dqt, dkh, dvh, df, dcu, sel["sel_ht"])


def _s5_disc_fn(are, aim, ldt):
    dt = jnp.exp(ldt)
    er = jnp.exp(are * dt)
    br = er * jnp.cos(aim * dt)
    bi = er * jnp.sin(aim * dt)
    nr = br - 1.0
    den = are * are + aim * aim
    return br, bi, (nr * are + bi * aim) / den, (bi * are - nr * aim) / den


def _s5_disc(name, are, aim, ldt):
    def body(a_ref, b_ref, c_ref, o0, o1, o2, o3):
        r = _s5_disc_fn(a_ref[...], b_ref[...], c_ref[...])
        o0[...], o1[...], o2[...], o3[...] = r

    shp = _sds((S5_GROUPS, S5_STATE))
    return pl.pallas_call(body, name=name, out_shape=[shp] * 4)(are, aim, ldt)


def _s5_disc_bwd(name, are, aim, ldt, cts):
    def body(a_ref, b_ref, c_ref, d0, d1, d2, d3, o0, o1, o2):
        _, vjp = jax.vjp(_s5_disc_fn, a_ref[...], b_ref[...], c_ref[...])
        o0[...], o1[...], o2[...] = vjp((d0[...], d1[...], d2[...], d3[...]))

    shp = _sds((S5_GROUPS, S5_STATE))
    return pl.pallas_call(body, name=name, out_shape=[shp, shp, _sds((S5_GROUPS, 1))])(are, aim, ldt, *cts)


def _adamw_rows(w, g, m, v):
    m = ADAM_B1 * m + (1.0 - ADAM_B1) * g
    v = ADAM_B2 * v + (1.0 - ADAM_B2) * (g * g)
    m_hat = m / (1.0 - ADAM_B1 ** ADAM_STEP)
    v_hat = v / (1.0 - ADAM_B2 ** ADAM_STEP)
    return -ADAM_LR * (m_hat / (jnp.sqrt(v_hat) + ADAM_EPS) + ADAM_WD * w), m, v


def _adamw(name, w, ga, gb, m, v):
    rows, cols = w.shape
    tr = _row_tile(rows)

    def body(w_ref, ga_ref, gb_ref, m_ref, v_ref, g_out, d_out, m_out, v_out):
        g = ga_ref[...] + gb_ref[...]
        d, mm, vv = _adamw_rows(w_ref[...], g, m_ref[...], v_ref[...])
        g_out[...] = g
        d_out[...] = d
        m_out[...] = mm
        v_out[...] = vv

    spec = pl.BlockSpec((tr, cols), lambda i: (i, 0))
    return pl.pallas_call(
        body, name=name, grid=(rows // tr,), in_specs=[spec] * 5, out_specs=[spec] * 4,
        out_shape=[_sds((rows, cols))] * 4, compiler_params=_params("parallel"),
    )(w, ga, gb, m, v)


def _sum_stack(name, st):
    n, rows, cols = st.shape
    tr = _row_tile(rows)

    def body(s_ref, o_ref):
        acc = s_ref[0].astype(f32)
        for j in range(1, n):
            acc = acc + s_ref[j].astype(f32)
        o_ref[...] = acc

    return pl.pallas_call(
        body, name=name, grid=(rows // tr,), in_specs=[pl.BlockSpec((n, tr, cols), lambda i: (0, i, 0))],
        out_specs=pl.BlockSpec((tr, cols), lambda i: (i, 0)), out_shape=_sds((rows, cols)),
        compiler_params=_params("parallel"),
    )(st)


def _block_diag(w):
    h, n, m = w.shape
    return jnp.einsum("hij,hg->higj", w, jnp.eye(h, dtype=w.dtype)).reshape(h * n, h * m)


def _block_diag_part(dense, h):
    n, m = dense.shape[0] // h, dense.shape[1] // h
    return jnp.einsum("higj,hg->hij", dense.reshape(h, n, h, m), jnp.eye(h, dtype=dense.dtype))


def _s5_matrices(coef_re, coef_im, b_re, b_im, c_re, c_im):
    bb_re = coef_re[:, :, None] * b_re - coef_im[:, :, None] * b_im
    bb_im = coef_re[:, :, None] * b_im + coef_im[:, :, None] * b_re
    wb_re = _block_diag(jnp.swapaxes(bb_re, 1, 2))
    wb_im = _block_diag(jnp.swapaxes(bb_im, 1, 2))
    wc_re = _block_diag(jnp.swapaxes(c_re, 1, 2))
    wc_im = _block_diag(jnp.swapaxes(-c_im, 1, 2))
    return wb_re, wb_im, wc_re, wc_im


def _shift_down(t):
    return jnp.concatenate([jnp.zeros((1, t.shape[1]), t.dtype), t[:-1]], axis=0)


def _shift_up(t):
    return jnp.concatenate([t[1:], jnp.zeros((1, t.shape[1]), t.dtype)], axis=0)


def _row(v):
    return v.reshape(1, -1)


def _ffn_fwd(tag, h, get, names, gamma, beta):
    wg, wu = get(names[0]), get(names[1])
    g, u, act = _ffn_up(tag + "_up", h, wg, wu)
    wd = get(names[2])
    r, out = _mm_ln(tag + "_down", act, wd, h, gamma, beta, 0.5, k_slabs=True)
    return out, dict(h=h, g=g, u=u, act=act, r=r, wg=wg, wu=wu, wd=wd)


def _ffn_bwd(tag, dout, sv, names, gamma, put, after_ln=None):
    s = dout.shape[0]
    dr, dgam, dbet = _ln_bwd(tag + "_lnb", sv["r"], dout, gamma)
    if after_ln is not None:
        after_ln(dgam, dbet)
    put(names[2], _mm_plain(tag + "_dwd", "tn", _Slabs(sv["act"]), dr, (D_FF, D_MODEL, s), scale=0.5, out_dtype=bf16,
                            tiles=(FF_SLAB, 1024, _tile(s, 1024))))
    dg, du = _ffn_dact(tag + "_dact", dr, sv["wd"], sv["g"], sv["u"])
    dwg, dwu = _mm2(tag + "_dwgu", "tn", (D_FF, D_MODEL, s), _Slabs(dg), sv["h"], _Slabs(du), None, separate=True,
                    out_dtype=bf16, tiles=(FF_SLAB, 1024, _tile(s, 1024)))
    put(names[0], dwg)
    put(names[1], dwu)
    dh = _mm2(tag + "_dh", "nn", (s, D_MODEL, D_FF), _Slabs(dg), sv["wg"], _Slabs(du), sv["wu"],
              add=dr, add_coef=ALPHA, tiles=(_tile(s, FFN_ROWS), 1024, FF_SLAB))[0]
    return dh, dgam, dbet


def _mixer_fwd(tag, h1, w):
    s = h1.shape[0]
    z = _mm_plain(tag + "_win", "nn", h1, w["w_in"], (s, N_IN_P, D_MODEL), tiles=(_tile(s, 512), 768, D_MODEL))
    ag, f, cu_cols = (z, D_A, 1), (z, 128, F_OFF // 128), (z, D_C, CU_OFF // D_C)
    cu = z[:, CU_OFF:]
    xa = _conv_fwd(tag + "_conv", z, w["conv_w"], w["conv_b"])
    a, gated = _rg_gates(tag + "_gates", xa, w["rg_wa"], w["rg_wx"], w["rg_ba"], w["rg_bx"], w["rg_lam"])
    ha = _lin_scan(tag + "_rgscan", a, gated, False)
    ones = jnp.ones((s, 128), f32)
    c = _lin_scan(tag + "_cumf", ones, _log_f(tag + "_logf", f, w["fox_bf"]), False)
    att = dict(zip(("qt", "k_aug", "kt", "vt", "v", "qh"), _attn_prep(tag + "_attnprep", z, c, w["sel"])))
    ot, lse = _attn_fwd_t(tag + "_attn", att["qt"], att["k_aug"], att["vt"])
    ob = ot.reshape(D_B, s).T
    bu_re, bu_im = _mm2(tag + "_s5in", "nn", (s, S5_LANES, D_C), cu, w["wb_re"], None, w["wb_im"], separate=True,
                        tiles=(_tile(s, 512), 1024, D_C))
    hre, him = _s5_scan(tag + "_s5scan", bu_re, bu_im, w["abar_re"], w["abar_im"], False)
    o = _mix_out(tag + "_mixout", ag, ha, ob, hre, him, cu_cols, w["s5_d"], w["mix_g"], w["wc_re"], w["wc_im"],
                 w["w_glu"])
    sv = dict(h1=h1, z=z, ag=ag, f=f, cu=cu, cu_cols=cu_cols, xa=xa, a=a, ha=ha, att=att, ot=ot, lse=lse, ob=ob,
              hre=hre, him=him, o=o)
    return o, sv


def _mixer_bwd(tag, do, dr2, sv, w, put):
    s = do.shape[0]
    (dag, dha, dob, dhre, dhim, dcu1, dwcr, dwci, dwglu, dd, dgn) = _mix_out_bwd(
        tag + "_mixoutb", do, sv["ag"], sv["ha"], sv["ob"], sv["hre"], sv["him"], sv["cu_cols"], w["s5_d"], w["mix_g"],
        w["wc_re"], w["wc_im"], w["w_glu"])
    put("s5_w_glu", dwglu.astype(bf16))
    gre, gim = _s5_scan(tag + "_s5scanb", dhre, dhim, w["abar_re"], -w["abar_im"], True)
    dab_re, dab_im = _s5_decay_grad(tag + "_s5dec", sv["hre"], sv["him"], gre, gim)
    dwb_re, dwb_im = _mm2(tag + "_s5dwb", "tn", (D_C, S5_LANES, s), sv["cu"], gre, None, gim, separate=True,
                          tiles=(D_C, 1024, _tile(s, 1024)))
    dcu = _mm2(tag + "_s5dcu", "nt", (s, D_C, S5_LANES), gre, w["wb_re"], gim, w["wb_im"], add=dcu1,
               tiles=(_tile(s, 512), D_C, 1024))[0]
    att = sv["att"]
    tq = _att_tiles(s)[0]
    nt = s // tq
    dot_blocks, doh = _attn_do_prep(tag + "_doprep", dob, w["sel"])
    dqt, delta = _attn_bwd_dq_t(tag + "_attndq", att["qt"], att["k_aug"], att["v"], att["kt"], sv["ot"], dot_blocks,
                                sv["lse"])
    dkh, dvh, dck = _attn_bwd_dkv_t(tag + "_attndkv", att["qt"], att["k_aug"], att["v"], att["qh"], doh, dot_blocks,
                                    sv["lse"].reshape(N_HEADS, nt, 1, tq), delta.reshape(N_HEADS, nt, 1, tq))
    dc = jnp.pad(dck.reshape(N_HEADS, s).T, ((0, 0), (0, 128 - N_HEADS)))
    dlf = _lin_scan(tag + "_cumfb", jnp.ones((s, 128), f32), dc, True)
    df, dbf = _log_f_bwd(tag + "_logfb", dlf, sv["f"], w["fox_bf"])
    ga = _lin_scan(tag + "_rgscanb", _shift_up(sv["a"]), dha, True)
    dxa, dwa, dwx, dba, dbx, dlam = _rg_gates_bwd(tag + "_gatesb", sv["xa"], ga, _shift_down(sv["ha"]), w["rg_wa"],
                                                  w["rg_wx"], w["rg_ba"], w["rg_bx"], w["rg_lam"])
    dax, dconv = _conv_bwd(tag + "_convb", dxa, sv["z"], w["conv_w"])
    dz = _dz_assemble(tag + "_dz", dax, dag, dqt, dkh, dvh, df, dcu, w["sel"])
    put("w_in", _mm_plain(tag + "_dwin", "tn", sv["h1"], dz, (D_MODEL, N_IN_P, s), out_dtype=bf16,
                          tiles=(512, 768, _tile(s, 1024))))
    dh1 = _mm_plain(tag + "_dh1", "nt", dz, w["w_in"], (s, D_MODEL, N_IN_P), add=dr2, add_coef=ALPHA,
                    tiles=(_tile(s, 512), 1024, 768))
    grads = dict(dconv=dconv, dwa=dwa, dwx=dwx, dba=dba, dbx=dbx, dlam=dlam, dbf=dbf,
                 dab_re=dab_re, dab_im=dab_im, dwb_re=dwb_re, dwb_im=dwb_im, dwcr=dwcr, dwci=dwci, dd=dd, dgn=dgn)
    return dh1, grads


SMALL_NAMES = ["ln1_g", "ln1_b", "conv_w", "conv_b", "rg_w_a", "rg_b_a", "rg_w_x", "rg_b_x", "rg_lambda", "fox_b_f",
               "s5_a_re", "s5_a_im", "s5_log_dt", "s5_b_re", "s5_b_im", "s5_c_re", "s5_c_im", "s5_d", "mix_norm_g",
               "ln2_g", "ln2_b", "ln3_g", "ln3_b"]
BIG_NAMES = ["ffn1_w_gate", "ffn1_w_up", "ffn1_w_down", "w_in", "s5_w_glu", "w_out", "ffn2_w_gate", "ffn2_w_up",
             "ffn2_w_down"]


def _local_step(x, target, weight, small, on_grads, on_small):
    h = x
    saved = []
    sel = _selections()
    for l in range(DEPTH):
        get = functools.partial(weight, l)

        sm = {n: small[n][l] for n in SMALL_NAMES}
        abar_re, abar_im, coef_re, coef_im = _s5_disc(f"l{l}_s5disc", sm["s5_a_re"], sm["s5_a_im"],
                                                      sm["s5_log_dt"].reshape(S5_GROUPS, 1))
        mats, mats_vjp = jax.vjp(_s5_matrices, coef_re, coef_im, sm["s5_b_re"], sm["s5_b_im"], sm["s5_c_re"],
                                 sm["s5_c_im"])
        w = dict(
            sel=sel, conv_w=sm["conv_w"], conv_b=_row(sm["conv_b"]),
            rg_wa=_block_diag(sm["rg_w_a"]).astype(bf16), rg_wx=_block_diag(sm["rg_w_x"]).astype(bf16),
            rg_ba=_row(sm["rg_b_a"]), rg_bx=_row(sm["rg_b_x"]), rg_lam=_row(sm["rg_lambda"]),
            fox_bf=jnp.pad(_row(sm["fox_b_f"]), ((0, 0), (0, 128 - N_HEADS))),
            abar_re=_row(abar_re), abar_im=_row(abar_im),
            wb_re=mats[0].astype(bf16), wb_im=mats[1].astype(bf16), wc_re=mats[2].astype(bf16),
            wc_im=mats[3].astype(bf16), s5_d=_row(sm["s5_d"]), mix_g=_row(sm["mix_norm_g"]))
        h1, sv1 = _ffn_fwd(f"l{l}_ffn1", h, get, GROUPS["F1"], _row(sm["ln1_g"]), _row(sm["ln1_b"]))
        w["w_in"], w["w_glu"] = get("w_in"), get("s5_w_glu")
        o, svm = _mixer_fwd(f"l{l}_mix", h1, w)
        w_out = get("w_out")
        r2, h2 = _mm_ln(f"l{l}_wout", o, w_out, h1, _row(sm["ln2_g"]), _row(sm["ln2_b"]), 1.0)
        h3, sv2 = _ffn_fwd(f"l{l}_ffn2", h2, get, GROUPS["F2"], _row(sm["ln3_g"]), _row(sm["ln3_b"]))
        saved.append(dict(sm=sm, w=w, w_out=w_out, sv1=sv1, svm=svm, r2=r2, sv2=sv2, mats_vjp=mats_vjp))
        h = h3

    dh, loss_row = _loss_head("loss_head", h, target)
    s = x.shape[0]
    gsmall = {n: [None] * DEPTH for n in SMALL_NAMES}
    for l in reversed(range(DEPTH)):
        sd = saved[l]
        sm, w = sd["sm"], sd["w"]

        def put(name, grad, l=l):
            on_grads((l, name), grad)

        dh2, dgam, dbet = _ffn_bwd(f"l{l}_ffn2", dh, sd["sv2"], GROUPS["F2"], _row(sm["ln3_g"]), put)
        gsmall["ln3_g"][l], gsmall["ln3_b"][l] = dgam[0], dbet[0]
        dr2, dgam, dbet = _ln_bwd(f"l{l}_ln2b", sd["r2"], dh2, _row(sm["ln2_g"]))
        gsmall["ln2_g"][l], gsmall["ln2_b"][l] = dgam[0], dbet[0]
        put("w_out", _mm_plain(f"l{l}_dwout", "tn", sd["svm"]["o"], dr2, (D_MODEL, D_MODEL, s), out_dtype=bf16))
        do = _mm_plain(f"l{l}_do", "nt", dr2, sd["w_out"], (s, D_MODEL, D_MODEL))
        dh1, g = _mixer_bwd(f"l{l}_mix", do, dr2, sd["svm"], w, put)
        gsmall["conv_w"][l], gsmall["conv_b"][l] = g["dconv"][:CONV_WIDTH], g["dconv"][CONV_WIDTH]
        gsmall["rg_w_a"][l] = _block_diag_part(g["dwa"], N_HEADS)
        gsmall["rg_w_x"][l] = _block_diag_part(g["dwx"], N_HEADS)
        gsmall["rg_b_a"][l], gsmall["rg_b_x"][l], gsmall["rg_lambda"][l] = g["dba"][0], g["dbx"][0], g["dlam"][0]
        gsmall["fox_b_f"][l] = g["dbf"][0, :N_HEADS]
        dcoef_re, dcoef_im, db_re, db_im, dc_re, dc_im = sd["mats_vjp"]((g["dwb_re"], g["dwb_im"], g["dwcr"], g["dwci"]))
        da_re, da_im, dldt = _s5_disc_bwd(
            f"l{l}_s5discb", sm["s5_a_re"], sm["s5_a_im"], sm["s5_log_dt"].reshape(S5_GROUPS, 1),
            (g["dab_re"].reshape(S5_GROUPS, S5_STATE), g["dab_im"].reshape(S5_GROUPS, S5_STATE), dcoef_re, dcoef_im))
        gsmall["s5_a_re"][l], gsmall["s5_a_im"][l], gsmall["s5_log_dt"][l] = da_re, da_im, dldt[:, 0]
        gsmall["s5_b_re"][l], gsmall["s5_b_im"][l], gsmall["s5_c_re"][l], gsmall["s5_c_im"][l] = db_re, db_im, dc_re, dc_im
        gsmall["s5_d"][l], gsmall["mix_norm_g"][l] = g["dd"][0], g["dgn"][0]

        def after_ln(dgam, dbet, l=l):
            gsmall["ln1_g"][l], gsmall["ln1_b"][l] = dgam[0], dbet[0]
            if l == 0:
                on_small({n: jnp.stack(v) for n, v in gsmall.items()})

        dh, _, _ = _ffn_bwd(f"l{l}_ffn1", dh1, sd["sv1"], GROUPS["F1"], _row(sm["ln1_g"]), put, after_ln)
    return loss_row[0, 0], dh


def _position():
    return lax.axis_index("x"), lax.axis_index("y"), lax.axis_index("c")


_ANY = pl.BlockSpec(memory_space=pl.ANY)


COLUMN_SHARDED = ("ffn1_w_gate", "ffn1_w_up", "ffn2_w_gate", "ffn2_w_up")
PACK_QUANTUM = 128 * 256


def _permute_in_cols(w):
    pad = jnp.zeros(w.shape[:-1] + (128 - N_HEADS,), w.dtype)
    return jnp.concatenate([w[..., :F_OFF + N_HEADS], pad, w[..., F_OFF + N_HEADS:]], axis=-1)


def _unpermute_in_cols(w):
    return jnp.concatenate([w[..., :F_OFF + N_HEADS], w[..., CU_OFF:]], axis=-1)


def _pack(arrs):
    flat = jnp.concatenate([a.reshape(-1) for a in arrs])
    pad = -flat.shape[0] % PACK_QUANTUM
    return jnp.pad(flat, (0, pad)).reshape(-1, 128)


def _unpack(buf, shapes):
    flat = buf.reshape(-1)
    out, off = [], 0
    for shp in shapes:
        size = math.prod(shp)
        out.append(flat[off:off + size].reshape(shp))
        off += size
    return out


WEIGHT_NAMES = ["ffn1_w_gate", "ffn1_w_up", "ffn1_w_down", "ln1_g", "ln1_b", "w_in", "conv_w", "conv_b", "rg_w_a",
                "rg_b_a", "rg_w_x", "rg_b_x", "rg_lambda", "fox_b_f", "s5_a_re", "s5_a_im", "s5_log_dt", "s5_b_re",
                "s5_b_im", "s5_c_re", "s5_c_im", "s5_d", "s5_w_glu", "mix_norm_g", "w_out", "ln2_g", "ln2_b",
                "ffn2_w_gate", "ffn2_w_up", "ffn2_w_down", "ln3_g", "ln3_b"]


def _remote(src, dst, send_sems, recv_sems, k, peer):
    return pltpu.make_async_remote_copy(src_ref=src, dst_ref=dst, send_sem=send_sems.at[k], recv_sem=recv_sems.at[k],
                                        device_id=peer, device_id_type=MESH)


class _ChipGatherPart:
    def __init__(self, arrays):
        self.arrays, self.results = list(arrays), None

    def out_shape(self):
        return [_sds((N_CHIPS,) + a.shape, a.dtype) for a in self.arrays]

    def sems(self):
        n = len(self.arrays)
        return [pltpu.SemaphoreType.DMA((3 * n,)), pltpu.SemaphoreType.DMA((3 * n,)), pltpu.SemaphoreType.DMA((n,))]

    def copies(self, ins, outs, sems):
        send_sems, recv_sems, local_sems = sems
        x, y, c = _position()
        me = 2 * x + y
        local, sends, recvs = [], [], []
        for i, (src, dst) in enumerate(zip(ins, outs)):
            local.append(pltpu.make_async_copy(self.mine(src, me), dst.at[me], local_sems.at[i]))
            for r, (px, py) in enumerate([(1 - x, y), (x, 1 - y), (1 - x, 1 - y)]):
                peer = 2 * px + py
                sends.append(_remote(self.theirs(src, peer), dst.at[me], send_sems, recv_sems, 3 * i + r, (px, py, c)))
                recvs.append(_remote(self.mine(src, me), dst.at[peer], send_sems, recv_sems, 3 * i + r, (px, py, c)))
        return local, sends, recvs

    def mine(self, src, me):
        return src

    def theirs(self, src, peer):
        return src


class _ChipGatherHalvesPart(_ChipGatherPart):
    def sems(self):
        n = len(self.arrays)
        return super().sems() + [pltpu.SemaphoreType.DMA((3 * n,)), pltpu.SemaphoreType.DMA((3 * n,))]

    def _half(self, ref, which):
        rows = ref.shape[0] // 2
        return ref.at[pl.ds(which * rows, rows)]

    def copies(self, ins, outs, sems):
        send_sems, recv_sems, local_sems = sems[:3]
        x, y, c = _position()
        me = 2 * x + y
        local, sends, recvs = [], [], []
        for i, (src, dst) in enumerate(zip(ins, outs)):
            local.append(pltpu.make_async_copy(src, dst.at[me], local_sems.at[i]))
            for r, (px, py) in enumerate([(1 - x, y), (x, 1 - y), (1 - x, 1 - y)]):
                sends.append(_remote(self._half(src, c), self._half(dst.at[me], c), send_sems, recv_sems, 3 * i + r,
                                     (px, py, c)))
                recvs.append(_remote(self._half(src, c), self._half(dst.at[2 * px + py], c), send_sems, recv_sems,
                                     3 * i + r, (px, py, c)))
        return local, sends, recvs

    def forwards(self, ins, outs, sems):
        send_sems, recv_sems = sems[3:]
        x, y, c = _position()
        sends, recvs = [], []
        for i, dst in enumerate(outs):
            for r, (px, py) in enumerate([(1 - x, y), (x, 1 - y), (1 - x, 1 - y)]):
                slot = dst.at[2 * px + py]
                sends.append(_remote(self._half(slot, c), self._half(slot, c), send_sems, recv_sems, 3 * i + r,
                                     (x, y, 1 - c)))
                recvs.append(_remote(self._half(slot, c), self._half(slot, 1 - c), send_sems, recv_sems, 3 * i + r,
                                     (x, y, 1 - c)))
        return sends, recvs


class _ChipScatterPart(_ChipGatherPart):
    def out_shape(self):
        return [_sds(a.shape, a.dtype) for a in self.arrays]

    def mine(self, src, me):
        return src.at[me]

    def theirs(self, src, peer):
        return src.at[peer]


class _SiblingSwapPart:
    def __init__(self, arrays):
        self.arrays, self.results = list(arrays), None

    def out_shape(self):
        return [_sds(a.shape, a.dtype) for a in self.arrays]

    def sems(self):
        n = len(self.arrays)
        return [pltpu.SemaphoreType.DMA((n,)), pltpu.SemaphoreType.DMA((n,))]

    def copies(self, ins, outs, sems):
        x, y, c = _position()
        both = [_remote(src, dst, sems[0], sems[1], i, (x, y, 1 - c)) for i, (src, dst) in enumerate(zip(ins, outs))]
        return [], both, both


def _split_by(parts, refs, count):
    out, off = [], 0
    for p in parts:
        out.append(refs[off:off + count(p)])
        off += count(p)
    return out


def _parts_refs(parts, in_refs, out_refs, sem_refs):
    return zip(parts, _split_by(parts, in_refs, lambda p: len(p.arrays)),
               _split_by(parts, out_refs, lambda p: len(p.arrays)), _split_by(parts, sem_refs, lambda p: len(p.sems())))


def _exchange_start(parts, in_refs, out_refs, sem_refs):
    for part, ins, outs, sems in _parts_refs(parts, in_refs, out_refs, sem_refs):
        local, sends, _ = part.copies(ins, outs, sems)
        for cp in local + sends:
            cp.start()


def _exchange_finish(parts, in_refs, out_refs, sem_refs):
    split = list(_parts_refs(parts, in_refs, out_refs, sem_refs))
    copies = [part.copies(ins, outs, sems) for part, ins, outs, sems in split]
    for _, _, recvs in copies:
        for cp in recvs:
            cp.wait_recv()
    second = [part.forwards(ins, outs, sems) for part, ins, outs, sems in split if hasattr(part, "forwards")]
    for sends, _ in second:
        for cp in sends:
            cp.start()
    for sends, recvs in second:
        for cp in recvs:
            cp.wait_recv()
        for cp in sends:
            cp.wait_send()
    for local, sends, _ in copies:
        for cp in sends:
            cp.wait_send()
        for cp in local:
            cp.wait()


def _exchange_operands(parts):
    return ([a for p in parts for a in p.arrays], [s for p in parts for s in p.out_shape()],
            [s for p in parts for s in p.sems()])


def _set_results(parts, res):
    for part, outs in zip(parts, _split_by(parts, list(res), lambda p: len(p.arrays))):
        part.results = list(outs)


def _exchange_now(name, parts):
    x_in, x_out, x_sem = _exchange_operands(parts)
    n = len(x_in)

    def body(*refs):
        _exchange_start(parts, refs[:n], refs[n:2 * n], refs[2 * n:])
        _exchange_finish(parts, refs[:n], refs[n:2 * n], refs[2 * n:])

    res = pl.pallas_call(body, name=name, in_specs=[_ANY] * n, out_specs=[_ANY] * n, out_shape=x_out,
                         scratch_shapes=x_sem)(*x_in)
    _set_results(parts, res)


_RIDERS = {}


def _call(body, *, name, grid, in_specs, out_specs, out_shape, scratch_shapes=(), compiler_params=None):
    make_parts = _RIDERS.pop(name, None)
    if make_parts is None:
        return pl.pallas_call(body, name=name, grid=grid, in_specs=in_specs, out_specs=out_specs, out_shape=out_shape,
                              scratch_shapes=scratch_shapes, compiler_params=compiler_params)
    parts = make_parts()
    x_in, x_out, x_sem = _exchange_operands(parts)
    n_out, n_scr, n_x = len(out_shape), len(scratch_shapes), len(x_in)

    def run(*args):
        n_in = len(args)

        def hosted(*refs):
            ins, xi = refs[:n_in], refs[n_in:n_in + n_x]
            outs, xo = refs[n_in + n_x:n_in + n_x + n_out], refs[n_in + n_x + n_out:n_in + 2 * n_x + n_out]
            scr, xs = refs[n_in + 2 * n_x + n_out:n_in + 2 * n_x + n_out + n_scr], refs[n_in + 2 * n_x + n_out + n_scr:]
            first = functools.reduce(jnp.logical_and, [pl.program_id(d) == 0 for d in range(len(grid))])
            last = functools.reduce(jnp.logical_and, [pl.program_id(d) == grid[d] - 1 for d in range(len(grid))])

            @pl.when(first)
            def _():
                _exchange_start(parts, xi, xo, xs)

            body(*ins, *outs, *scr)

            @pl.when(last)
            def _():
                _exchange_finish(parts, xi, xo, xs)

        res = pl.pallas_call(
            hosted, name=name, grid=grid, in_specs=list(in_specs) + [_ANY] * n_x,
            out_specs=list(out_specs) + [_ANY] * n_x, out_shape=list(out_shape) + x_out,
            scratch_shapes=list(scratch_shapes) + x_sem, compiler_params=_params(*["arbitrary"] * len(grid)),
        )(*args, *x_in)
        _set_results(parts, res[n_out:])
        return list(res[:n_out])

    return run


GROUPS = {"F1": ["ffn1_w_gate", "ffn1_w_up", "ffn1_w_down"], "MX": ["w_in", "s5_w_glu", "w_out"],
          "F2": ["ffn2_w_gate", "ffn2_w_up", "ffn2_w_down"]}
FIRST_GATHER = [(0, "ffn1_w_gate"), (0, "ffn1_w_up")]
GATHER_HOSTS = {
    "l0_ffn1_up": [(0, "ffn1_w_down")],
    "l0_ffn1_down": [(0, "w_in"), (0, "s5_w_glu"), (0, "w_out")],
    "l0_mix_win": [(0, "ffn2_w_gate")],
    "l0_mix_attn": [(0, "ffn2_w_up"), (0, "ffn2_w_down")],
    "l0_ffn2_up": [(1, "ffn1_w_gate")],
    "l0_ffn2_down": [(1, "ffn1_w_up")],
    "l1_ffn1_up": [(1, "ffn1_w_down")],
    "l1_ffn1_down": [(1, "w_in"), (1, "s5_w_glu"), (1, "w_out")],
    "l1_mix_win": [(1, "ffn2_w_gate")],
    "l1_mix_attn": [(1, "ffn2_w_up"), (1, "ffn2_w_down")],
}
SCATTER_HOSTS = {
    "l1_ffn2_dact": [(1, "ffn2_w_down")],
    "l1_ffn2_dh": [(1, "ffn2_w_gate")],
    "l1_mix_attndq": [(1, "ffn2_w_up")],
    "l1_mix_attndkv": [(1, "w_out"), (1, "s5_w_glu")],
    "l1_mix_dh1": [(1, "w_in")],
    "l1_ffn1_dact": [(1, "ffn1_w_down")],
    "l1_ffn1_dh": [(1, "ffn1_w_gate")],
    "l0_ffn2_dact": [(1, "ffn1_w_up")],
    "l0_ffn2_dwgu": [(0, "ffn2_w_down")],
    "l0_ffn2_dh": [(0, "ffn2_w_gate")],
    "l0_mix_attndq": [(0, "w_out"), (0, "s5_w_glu"), (0, "ffn2_w_up")],
    "l0_mix_dh1": [(0, "w_in")],
    "l0_ffn1_dact": [(0, "ffn1_w_down")],
    "l0_ffn1_dh": [(0, "ffn1_w_gate")],
}
LAST_SCATTER = [(0, "ffn1_w_up")]
SMALL_HOST = "l0_ffn1_dwd"
SMALL_PACK_ORDER = [n for n in SMALL_NAMES if n != "conv_w"] + ["conv_w"]
TAIL_HOST = "l0_ffn1_dwgu"
LATE_SCATTER_HOST = "l0_ffn1_dh"


def _sharded_rows(name, a):
    return jnp.swapaxes(a, 1, 2) if name in COLUMN_SHARDED else a


def _unstack_layer(st):
    _, r, c = st.shape
    return st.reshape(N_CHIPS * r, c)


def _restack_layer(g):
    r, c = g.shape
    return g.reshape(N_CHIPS, r // N_CHIPS, c)


def _adamw_layer(name, layer, w, ga, gb, m, v, bufs):
    _, r, c = w.shape
    tr = _row_tile(r)

    def body(w_ref, ga_ref, gb_ref, m_ref, v_ref, *rest):
        g_out, d_out, m_out, v_out = rest[-4:]
        g = ga_ref[...] + gb_ref[...]
        d, mm, vv = _adamw_rows(w_ref[...], g, m_ref[...], v_ref[...])
        g_out[...] = g
        d_out[...] = d
        m_out[...] = mm
        v_out[...] = vv

    full = pl.BlockSpec((None, tr, c), lambda i: (layer, i, 0))
    flat = pl.BlockSpec((tr, c), lambda i: (i, 0))
    extra = {} if bufs is None else dict(input_output_aliases={5 + k: k for k in range(4)})
    return pl.pallas_call(
        body, name=name, grid=(r // tr,),
        in_specs=[full, flat, flat, full, full] + ([] if bufs is None else [_ANY] * 4),
        out_specs=[full] * 4, out_shape=[_sds(w.shape)] * 4, compiler_params=_params("parallel"), **extra,
    )(w, ga, gb, m, v, *([] if bufs is None else bufs))


def _train_step(x, loss_target, w, m, v):
    ix, iy, _ = _position()
    chip = 2 * ix + iy
    shard = {n: (_permute_in_cols(w[n]) if n == "w_in" else _sharded_rows(n, w[n])).astype(bf16) for n in BIG_NAMES}

    gathered = {}

    def gather_parts(keys, extra=()):
        part = _ChipGatherHalvesPart([shard[n][layer] for layer, n in keys] + list(extra))
        gathered.update({key: (part, i) for i, key in enumerate(keys)})
        return [part]

    (first,) = gather_parts(FIRST_GATHER, extra=[w["conv_w"]])
    _exchange_now("gather_first", [first])
    for host, keys in GATHER_HOSTS.items():
        _RIDERS[host] = functools.partial(gather_parts, keys)

    def weight(layer, name):
        part, i = gathered[(layer, name)]
        return _unstack_layer(part.results[i])

    small = {n: w[n] for n in SMALL_NAMES}
    small["conv_w"] = first.results[-1].transpose(1, 2, 0, 3).reshape(DEPTH, CONV_WIDTH, D_A)

    grads_full, scattered = {}, {}

    def scatter_parts(keys):
        part = _ChipScatterPart([_restack_layer(grads_full[key]) for key in keys])
        scattered.update({key: (part, i) for i, key in enumerate(keys)})
        return [part]

    for host, keys in SCATTER_HOSTS.items():
        _RIDERS[host] = functools.partial(scatter_parts, keys)

    partial = {}

    def reduce_chips(keys):
        for layer, n in keys:
            part, i = scattered[(layer, n)]
            p = _sum_stack(f"sum_l{layer}_{n}", part.results[i])
            partial[(layer, n)] = _unpermute_in_cols(p) if n == "w_in" else p

    early = [key for host, keys in SCATTER_HOSTS.items() if host != LATE_SCATTER_HOST for key in keys]
    late = SCATTER_HOSTS[LATE_SCATTER_HOST]
    tail = {}

    def small_parts():
        tail["small"] = _ChipGatherPart([_pack([tail["gsmall"][n] for n in SMALL_PACK_ORDER])])
        return [tail["small"]]

    def tail_parts():
        reduce_chips(early)
        tail["small_sum"] = _sum_stack("sum_small", tail["small"].results[0])
        tail["swap"] = _SiblingSwapPart([partial[k] for k in early] + [tail["small_sum"]])
        return [tail["swap"]]

    _RIDERS[SMALL_HOST] = small_parts
    _RIDERS[TAIL_HOST] = tail_parts
    loss_local, gx = _local_step(x[0], loss_target[0], weight, small, grads_full.__setitem__,
                                 functools.partial(tail.__setitem__, "gsmall"))
    other = dict(zip(early, tail["swap"].results[:-1]))
    small_mine, small_other = tail["small_sum"], tail["swap"].results[-1]
    reduce_chips(late)
    last_parts = scatter_parts(LAST_SCATTER) + [_SiblingSwapPart([partial[k] for k in late])]
    _exchange_now("exchange_last", last_parts)
    other.update(zip(late, last_parts[1].results))
    reduce_chips(LAST_SCATTER)
    swap_last = _SiblingSwapPart([partial[k] for k in LAST_SCATTER])
    _exchange_now("swap_last", [swap_last])
    other.update(zip(LAST_SCATTER, swap_last.results))

    grads, deltas, new_m, new_v = {}, {}, {}, {}
    for n in BIG_NAMES:
        bufs = None
        wr, mr, vr = (_sharded_rows(n, t) for t in (w[n], m[n], v[n]))
        for layer in range(DEPTH):
            bufs = _adamw_layer(f"adamw_l{layer}_{n}", layer, wr, partial[(layer, n)], other[(layer, n)], mr, vr, bufs)
        grads[n], deltas[n], new_m[n], new_v[n] = (_sharded_rows(n, t) for t in bufs)
    packed = SMALL_PACK_ORDER[:-1]
    shapes = [w[n].shape for n in packed]
    res = _adamw("adamw_small", _pack([w[n] for n in packed]), small_mine, small_other,
                 _pack([m[n] for n in packed]), _pack([v[n] for n in packed]))
    for dst, buf in zip((grads, deltas, new_m, new_v), res):
        dst.update(zip(packed, _unpack(buf, shapes)))
    cw = D_A // N_CHIPS
    conv_shape = (DEPTH, CONV_WIDTH, D_A)
    offset = sum(math.prod(s_) for s_ in shapes)

    def conv_grad(buf):
        full = buf.reshape(-1)[offset:offset + math.prod(conv_shape)].reshape(conv_shape)
        return lax.dynamic_slice_in_dim(full, chip * cw, cw, axis=2).reshape(DEPTH * CONV_WIDTH, cw)

    rows = lambda t: t.reshape(DEPTH * CONV_WIDTH, cw)
    res = _adamw("adamw_conv_w", rows(w["conv_w"]), conv_grad(small_mine), conv_grad(small_other),
                 rows(m["conv_w"]), rows(v["conv_w"]))
    for dst, buf in zip((grads, deltas, new_m, new_v), res):
        dst["conv_w"] = buf.reshape(w["conv_w"].shape)

    loss = lax.psum(loss_local, ("x", "y", "c"))
    return (loss, gx[None], *[grads[n] for n in WEIGHT_NAMES], *[deltas[n] for n in WEIGHT_NAMES],
            *[new_m[n] for n in WEIGHT_NAMES], *[new_v[n] for n in WEIGHT_NAMES])


def kernel(x, ffn1_w_gate, ffn1_w_up, ffn1_w_down, ln1_g, ln1_b, w_in, conv_w, conv_b, rg_w_a, rg_b_a, rg_w_x, rg_b_x, rg_lambda, fox_b_f, s5_a_re, s5_a_im, s5_log_dt, s5_b_re, s5_b_im, s5_c_re, s5_c_im, s5_d, s5_w_glu, mix_norm_g, w_out, ln2_g, ln2_b, ffn2_w_gate, ffn2_w_up, ffn2_w_down, ln3_g, ln3_b, loss_target, m_ffn1_w_gate, m_ffn1_w_up, m_ffn1_w_down, m_ln1_g, m_ln1_b, m_w_in, m_conv_w, m_conv_b, m_rg_w_a, m_rg_b_a, m_rg_w_x, m_rg_b_x, m_rg_lambda, m_fox_b_f, m_s5_a_re, m_s5_a_im, m_s5_log_dt, m_s5_b_re, m_s5_b_im, m_s5_c_re, m_s5_c_im, m_s5_d, m_s5_w_glu, m_mix_norm_g, m_w_out, m_ln2_g, m_ln2_b, m_ffn2_w_gate, m_ffn2_w_up, m_ffn2_w_down, m_ln3_g, m_ln3_b, v_ffn1_w_gate, v_ffn1_w_up, v_ffn1_w_down, v_ln1_g, v_ln1_b, v_w_in, v_conv_w, v_conv_b, v_rg_w_a, v_rg_b_a, v_rg_w_x, v_rg_b_x, v_rg_lambda, v_fox_b_f, v_s5_a_re, v_s5_a_im, v_s5_log_dt, v_s5_b_re, v_s5_b_im, v_s5_c_re, v_s5_c_im, v_s5_d, v_s5_w_glu, v_mix_norm_g, v_w_out, v_ln2_g, v_ln2_b, v_ffn2_w_gate, v_ffn2_w_up, v_ffn2_w_down, v_ln3_g, v_ln3_b):
    args = dict(locals())
    w = {n: args[n] for n in WEIGHT_NAMES}
    m = {n: args["m_" + n] for n in WEIGHT_NAMES}
    v = {n: args["v_" + n] for n in WEIGHT_NAMES}
    return _train_step(x, loss_target, w, m, v)
```

```python
import functools
import math

import jax
import jax.numpy as jnp
from jax import lax
from jax.experimental import pallas as pl
from jax.experimental.pallas import tpu as pltpu

f32 = jnp.float32
bf16 = jnp.bfloat16

D_MODEL = 1024
D_FF = 2816
D_A = 384
D_B = 384
D_C = 256
N_HEADS = 6
HEAD_DIM = 64
S5_GROUPS = 16
S5_GROUP = 16
S5_STATE = 64
S5_LANES = S5_GROUPS * S5_STATE
N_IN = 2 * D_A + 3 * D_B + N_HEADS + D_C
F_OFF = 5 * D_A
CU_OFF = F_OFF + 128
N_IN_P = CU_OFF + D_C
CONV_WIDTH = 4
DEPTH = 2
ALPHA = (2 * DEPTH) ** 0.25
LN_EPS = 1e-5
RMS_EPS = 1e-6
RG_C = 8.0
ATT_SCALE = HEAD_DIM ** -0.5
ADAM_LR, ADAM_B1, ADAM_B2, ADAM_EPS, ADAM_WD, ADAM_STEP = 0.001, 0.9, 0.999, 1e-08, 0.01, 10

ROW_TILE = 256
N_CHIPS = 4
N_DEV = 8
MESH = pl.DeviceIdType.MESH

_DN = {
    "nn": (((1,), (0,)), ((), ())),
    "nt": (((1,), (1,)), ((), ())),
    "tn": (((0,), (0,)), ((), ())),
}


def _sds(shape, dtype=f32):
    return jax.ShapeDtypeStruct(shape, dtype)


def _tile(n, target):
    best = None
    for t in range(128, min(n, target) + 1, 128):
        if n % t == 0:
            best = t
    return best or n


def _row_tile(rows, target=256):
    best = None
    for t in range(16, min(rows, target) + 1, 16):
        if rows % t == 0:
            best = t
    return best or rows


def _params(*sem):
    return pltpu.CompilerParams(dimension_semantics=sem)


class _Slabs:
    def __init__(self, x):
        self.x = x


class _KPart:
    def __init__(self, x, j):
        self.x, self.j = x, j


FF_SLAB = D_FF // 4
FFN_ROWS = 1024

def _mm(name, mode, dims, tiles, a_list, b_list, pairs, n_acc, epilogue, outs, extras=(), vecs=(), split_cols=False):
    m, n, k = dims
    tm, tn, tk = tiles
    nk = k // tk
    na, nb, ne, nv, no = len(a_list), len(b_list), len(extras), len(vecs), len(outs)

    def body(*refs):
        a_refs = refs[:na]
        b_refs = refs[na:na + nb]
        e_refs = refs[na + nb:na + nb + ne]
        v_refs = refs[na + nb + ne:na + nb + ne + nv]
        o_refs = refs[na + nb + ne + nv:na + nb + ne + nv + no]
        acc_refs = refs[na + nb + ne + nv + no:]
        a_vals = [r[...].astype(bf16) for r in a_refs]
        b_vals = [r[...].astype(bf16) for r in b_refs]
        products = [(ci, lax.dot_general(a_vals[ai], b_vals[bi], _DN[mode], preferred_element_type=f32))
                    for ai, bi, ci in pairs]

        def finish(accs):
            res = epilogue(accs, [e[...] for e in e_refs], [v[...] for v in v_refs])
            for o, r in zip(o_refs, res):
                o[...] = r.astype(o.dtype)

        if nk == 1:
            accs = [None] * n_acc
            for ci, prod in products:
                accs[ci] = prod if accs[ci] is None else accs[ci] + prod
            finish(accs)
            return
        kk = pl.program_id(2)

        @pl.when(kk == 0)
        def _():
            for acc in acc_refs:
                acc[...] = jnp.zeros_like(acc)

        for ci, prod in products:
            acc_refs[ci][...] += prod

        @pl.when(kk == nk - 1)
        def _():
            finish([acc[...] for acc in acc_refs])

    def a_spec(a):
        if isinstance(a, _KPart):
            return pl.BlockSpec((None, tm, tk), lambda i, j, kk, part=a.j: (part, i, 0))
        if isinstance(a, _Slabs):
            if mode == "tn":
                return pl.BlockSpec((None, tk, tm), lambda i, j, kk: (i, kk, 0))
            return pl.BlockSpec((None, tm, tk), lambda i, j, kk: (kk, i, 0))
        if mode == "tn":
            return pl.BlockSpec((tk, tm), lambda i, j, kk: (kk, i))
        return pl.BlockSpec((tm, tk), lambda i, j, kk: (i, kk))

    def b_spec(b):
        if isinstance(b, _KPart):
            return pl.BlockSpec((tk, tn), lambda i, j, kk, part=b.j: (part, j))
        if isinstance(b, _Slabs):
            if mode == "nt":
                return pl.BlockSpec((None, tn, tk), lambda i, j, kk: (kk, j, 0))
            return pl.BlockSpec((None, tk, tn), lambda i, j, kk: (j, kk, 0))
        if mode == "nt":
            return pl.BlockSpec((tn, tk), lambda i, j, kk: (j, kk))
        return pl.BlockSpec((tk, tn), lambda i, j, kk: (kk, j))

    o_spec = pl.BlockSpec((tm, tn), lambda i, j, kk: (i, j))
    o_slab_spec = pl.BlockSpec((None, tm, tn), lambda i, j, kk: (j, i, 0))
    v_spec = pl.BlockSpec((1, tn), lambda i, j, kk: (0, j))
    if split_cols:
        out_specs = [o_slab_spec] * no
        out_shape = [_sds((n // tn, m, tn), dt) for dt in outs]
    else:
        out_specs = [o_spec] * no
        out_shape = [_sds((m, n), dt) for dt in outs]
    raw = lambda t: t.x if isinstance(t, (_Slabs, _KPart)) else t
    res = _call(
        body,
        name=name,
        grid=(m // tm, n // tn, nk),
        in_specs=([a_spec(a) for a in a_list] + [b_spec(b) for b in b_list]
                  + [o_slab_spec if isinstance(e, _Slabs) else o_spec for e in extras] + [v_spec] * nv),
        out_specs=out_specs,
        out_shape=out_shape,
        scratch_shapes=[pltpu.VMEM((tm, tn), f32)] * (n_acc if nk > 1 else 0),
        compiler_params=_params("parallel", "parallel", "arbitrary"),
    )(*map(raw, a_list), *map(raw, b_list), *map(raw, extras), *vecs)
    return res


def _sigmoid(x):
    return 0.5 * (jnp.tanh(0.5 * x) + 1.0)


def _layer_norm_rows(r, gamma, beta):
    mu = jnp.mean(r, axis=-1, keepdims=True)
    xc = r - mu
    var = jnp.mean(xc * xc, axis=-1, keepdims=True)
    return xc * lax.rsqrt(var + LN_EPS) * gamma + beta


def _mm_plain(name, mode, a, b, dims, scale=1.0, out_dtype=f32, add=None, add_coef=1.0, tiles=None):
    m, n, k = dims
    tiles = tiles or (_tile(m, 512), _tile(n, 1024), _tile(k, 1024))

    def epilogue(accs, extras, vecs):
        r = accs[0] if scale == 1.0 else accs[0] * scale
        if extras:
            r = r + add_coef * extras[0]
        return [r]

    return _mm(name, mode, dims, tiles, [a], [b], [(0, 0, 0)], 1, epilogue, [out_dtype],
               extras=[] if add is None else [add])[0]


def _ffn_up(name, h, wg, wu):
    s = h.shape[0]

    def epilogue(accs, extras, vecs):
        g, u = accs
        return [g, u, g * _sigmoid(g) * u]

    return _mm(name, "nt", (s, D_FF, D_MODEL), (_tile(s, FFN_ROWS), FF_SLAB, D_MODEL), [h], [wg, wu],
               [(0, 0, 0), (0, 1, 1)], 2, epilogue, [bf16, bf16, bf16], split_cols=True)


def _mm_ln(name, a, w, resid, gamma, beta, scale, k_slabs=False):
    def epilogue(accs, extras, vecs):
        r = ALPHA * extras[0] + scale * accs[0]
        return [r, _layer_norm_rows(r, vecs[0], vecs[1])]

    if k_slabs:
        n_slabs, s, slab = a.shape
        return _mm(name, "nn", (s, D_MODEL, slab), (_tile(s, 512), D_MODEL, slab),
                   [_KPart(a, j) for j in range(n_slabs)], [_KPart(w, j) for j in range(n_slabs)],
                   [(j, j, 0) for j in range(n_slabs)], 1, epilogue, [f32, f32], extras=[resid], vecs=[gamma, beta])
    s, k = a.shape
    return _mm(name, "nn", (s, D_MODEL, k), (_tile(s, FFN_ROWS), D_MODEL, _tile(k, 1024)),
               [a], [w], [(0, 0, 0)], 1, epilogue, [f32, f32], extras=[resid], vecs=[gamma, beta])


def _ffn_dact(name, dr, wd, g, u):
    s = dr.shape[0]

    def epilogue(accs, extras, vecs):
        da = 0.5 * accs[0]
        gg, uu = extras[0].astype(f32), extras[1].astype(f32)
        sg = _sigmoid(gg)
        return [da * uu * (sg * (1.0 + gg * (1.0 - sg))), da * (gg * sg)]

    return _mm(name, "nt", (s, D_FF, D_MODEL), (_tile(s, FFN_ROWS), FF_SLAB, D_MODEL), [dr], [wd],
               [(0, 0, 0)], 1, epilogue, [bf16, bf16], extras=[_Slabs(g), _Slabs(u)], split_cols=True)


def _mm2(name, mode, dims, a0, b0, a1, b1, add=None, add_coef=1.0, separate=False, tiles=None, out_dtype=f32,
         split_cols=False):
    m, n, k = dims
    tiles = tiles or (_tile(m, 512), _tile(n, 1024), _tile(k, 1024))

    def epilogue(accs, extras, vecs):
        if separate:
            return list(accs)
        r = accs[0]
        if extras:
            r = r + add_coef * extras[0]
        return [r]

    a_list = [a0] if a1 is None else [a0, a1]
    b_list = [b0] if b1 is None else [b0, b1]
    pairs = [(0, 0, 0), (len(a_list) - 1, len(b_list) - 1, 1 if separate else 0)]
    return _mm(name, mode, dims, tiles, a_list, b_list, pairs, 2 if separate else 1, epilogue,
               [out_dtype, out_dtype] if separate else [out_dtype], extras=[] if add is None else [add],
               split_cols=split_cols)


def _row_call(name, body, s, ins, params, outs, accs):
    tm = ROW_TILE
    ins = [a if isinstance(a, tuple) else (a, a.shape[1], 0) for a in ins]
    in_specs = [pl.BlockSpec((tm, width), lambda i, cb=cb: (i, cb)) for _, width, cb in ins]
    ins = [a for a, _, _ in ins]
    in_specs += [pl.BlockSpec(p.shape, lambda i, nd=p.ndim: (0,) * nd) for p in params]
    out_specs = [pl.BlockSpec((tm, o.shape[1]), lambda i: (i, 0)) for o in outs]
    out_specs += [pl.BlockSpec(a.shape, lambda i, nd=len(a.shape): (0,) * nd) for a in accs]
    return pl.pallas_call(
        body,
        name=name,
        grid=(s // tm,),
        in_specs=in_specs,
        out_specs=out_specs,
        out_shape=list(outs) + list(accs),
        compiler_params=_params("arbitrary"),
    )(*ins, *params)


def _zero_at_first(refs):
    @pl.when(pl.program_id(0) == 0)
    def _():
        for r in refs:
            r[...] = jnp.zeros_like(r)


def _ln_bwd(name, r, dh, gamma):
    s = r.shape[0]

    def body(r_ref, dh_ref, g_ref, dr_ref, dg_ref, db_ref):
        _zero_at_first([dg_ref, db_ref])
        rr = r_ref[...]
        dy = dh_ref[...]
        mu = jnp.mean(rr, axis=-1, keepdims=True)
        xc = rr - mu
        rstd = lax.rsqrt(jnp.mean(xc * xc, axis=-1, keepdims=True) + LN_EPS)
        xhat = xc * rstd
        dxh = dy * g_ref[...]
        dr_ref[...] = rstd * (dxh - jnp.mean(dxh, axis=-1, keepdims=True)
                              - xhat * jnp.mean(dxh * xhat, axis=-1, keepdims=True))
        dg_ref[...] += jnp.sum(dy * xhat, axis=0, keepdims=True)
        db_ref[...] += jnp.sum(dy, axis=0, keepdims=True)

    return _row_call(name, body, s, [r, dh], [gamma], [_sds((s, D_MODEL))], [_sds((1, D_MODEL)), _sds((1, D_MODEL))])


def _loss_head(name, y, target):
    s = y.shape[0]

    def body(y_ref, t_ref, dy_ref, l_ref):
        _zero_at_first([l_ref])
        e = y_ref[...] - t_ref[...]
        dy_ref[...] = e / D_MODEL
        l_ref[...] += 0.5 * jnp.sum(jnp.mean(e * e, axis=-1, keepdims=True), axis=0, keepdims=True)

    return _row_call(name, body, s, [y, target], [], [_sds((s, D_MODEL))], [_sds((1, 128))])


def _expm1(x):
    series = x * (1.0 + x / 2.0 * (1.0 + x / 3.0 * (1.0 + x / 4.0 * (1.0 + x / 5.0 * (1.0 + x / 6.0 * (1.0 + x / 7.0))))))
    return jnp.where(jnp.abs(x) < 0.25, series, jnp.exp(x) - 1.0)


def _gates_fn(xa, wa, wx, ba, bx, lam, tap_a, tap_x):
    xb = xa.astype(bf16)
    r = jax.nn.sigmoid(jnp.dot(xb, wa, preferred_element_type=f32) + ba + tap_a)
    i = jax.nn.sigmoid(jnp.dot(xb, wx, preferred_element_type=f32) + bx + tap_x)
    log_a = -RG_C * r * jax.nn.softplus(-lam)
    a = jnp.exp(log_a)
    gated = jnp.sqrt(-_expm1(2.0 * log_a)) * (i * xa)
    return a, gated


def _rg_gates(name, xa, wa, wx, ba, bx, lam):
    s = xa.shape[0]

    def body(xa_ref, wa_ref, wx_ref, ba_ref, bx_ref, lam_ref, a_ref, g_ref):
        a, g = _gates_fn(xa_ref[...], wa_ref[...], wx_ref[...], ba_ref[...], bx_ref[...], lam_ref[...], 0.0, 0.0)
        a_ref[...] = a
        g_ref[...] = g

    return _row_call(name, body, s, [xa], [wa, wx, ba, bx, lam], [_sds((s, D_A)), _sds((s, D_A))], [])


def _rg_gates_bwd(name, xa, ga, h_prev, wa, wx, ba, bx, lam):
    s = xa.shape[0]

    def body(xa_ref, ga_ref, hp_ref, wa_ref, wx_ref, ba_ref, bx_ref, lam_ref,
             dxa_ref, dwa_ref, dwx_ref, dba_ref, dbx_ref, dlam_ref):
        _zero_at_first([dwa_ref, dwx_ref, dba_ref, dbx_ref, dlam_ref])
        xa_v = xa_ref[...]
        zero = jnp.zeros((xa_v.shape[0], D_A), f32)
        fn = lambda x, ba_, bx_, lam_, ta, tx: _gates_fn(x, wa_ref[...], wx_ref[...], ba_, bx_, lam_, ta, tx)
        _, vjp = jax.vjp(fn, xa_v, ba_ref[...], bx_ref[...], lam_ref[...], zero, zero)
        gav = ga_ref[...]
        dxa, dba, dbx, dlam, dta, dtx = vjp((gav * hp_ref[...], gav))
        dxa_ref[...] = dxa
        xb = xa_v.astype(bf16)
        dwa_ref[...] += lax.dot_general(xb, dta.astype(bf16), _DN["tn"], preferred_element_type=f32)
        dwx_ref[...] += lax.dot_general(xb, dtx.astype(bf16), _DN["tn"], preferred_element_type=f32)
        dba_ref[...] += dba
        dbx_ref[...] += dbx
        dlam_ref[...] += dlam

    return _row_call(name, body, s, [xa, ga, h_prev], [wa, wx, ba, bx, lam], [_sds((s, D_A))],
                     [_sds((D_A, D_A)), _sds((D_A, D_A)), _sds((1, D_A)), _sds((1, D_A)), _sds((1, D_A))])


def _rms(v, g):
    return v * lax.rsqrt(jnp.mean(v * v, axis=-1, keepdims=True) + RMS_EPS) * g


def _mix_out_fn(ag, ha, ob, hre, him, cu, d, gn, tap_y, tap_gl, wcr, wci, wglu):
    out_a = jax.nn.gelu(ag) * ha
    y = (jnp.dot(hre.astype(bf16), wcr, preferred_element_type=f32)
         + jnp.dot(him.astype(bf16), wci, preferred_element_type=f32) + d * cu + tap_y)
    y2 = jax.nn.gelu(y)
    gl = jnp.dot(y2.astype(bf16), wglu, preferred_element_type=f32) + tap_gl
    out_c = y2 * jax.nn.sigmoid(gl)
    o = jnp.concatenate([_rms(out_a, gn[:, :D_A]), _rms(ob, gn[:, D_A:D_A + D_B]), _rms(out_c, gn[:, D_A + D_B:])],
                        axis=-1)
    return o, y2


def _mix_out(name, ag, ha, ob, hre, him, cu, d, gn, wcr, wci, wglu):
    s = ha.shape[0]

    def body(ag_ref, ha_ref, ob_ref, hre_ref, him_ref, cu_ref, d_ref, gn_ref, wcr_ref, wci_ref, wglu_ref, o_ref):
        o, _ = _mix_out_fn(ag_ref[...], ha_ref[...], ob_ref[...], hre_ref[...], him_ref[...], cu_ref[...], d_ref[...],
                           gn_ref[...], 0.0, 0.0, wcr_ref[...], wci_ref[...], wglu_ref[...])
        o_ref[...] = o.astype(o_ref.dtype)

    return _row_call(name, body, s, [ag, ha, ob, hre, him, cu], [d, gn, wcr, wci, wglu], [_sds((s, D_MODEL), bf16)], [])[0]


def _mix_out_bwd(name, do, ag, ha, ob, hre, him, cu, d, gn, wcr, wci, wglu):
    s = ha.shape[0]

    def body(do_ref, ag_ref, ha_ref, ob_ref, hre_ref, him_ref, cu_ref, d_ref, gn_ref, wcr_ref, wci_ref, wglu_ref,
             dag_ref, dha_ref, dob_ref, dhre_ref, dhim_ref, dcu_ref, dwcr_ref, dwci_ref, dwglu_ref, dd_ref, dgn_ref):
        _zero_at_first([dwcr_ref, dwci_ref, dwglu_ref, dd_ref, dgn_ref])
        tm = ag_ref.shape[0]
        zero = jnp.zeros((tm, D_C), f32)
        hre_v, him_v = hre_ref[...], him_ref[...]
        fn = lambda *a: _mix_out_fn(*a, wcr_ref[...], wci_ref[...], wglu_ref[...])
        _, vjp, y2 = jax.vjp(fn, ag_ref[...], ha_ref[...], ob_ref[...], hre_v, him_v, cu_ref[...], d_ref[...],
                             gn_ref[...], zero, zero, has_aux=True)
        dag, dha, dob, dhre, dhim, dcu, dd, dgn, dy, dgl = vjp(do_ref[...])
        dag_ref[...] = dag
        dha_ref[...] = dha
        dob_ref[...] = dob
        dhre_ref[...] = dhre
        dhim_ref[...] = dhim
        dcu_ref[...] = dcu
        dyb = dy.astype(bf16)
        dwcr_ref[...] += lax.dot_general(hre_v.astype(bf16), dyb, _DN["tn"], preferred_element_type=f32)
        dwci_ref[...] += lax.dot_general(him_v.astype(bf16), dyb, _DN["tn"], preferred_element_type=f32)
        dwglu_ref[...] += lax.dot_general(y2.astype(bf16), dgl.astype(bf16), _DN["tn"], preferred_element_type=f32)
        dd_ref[...] += dd
        dgn_ref[...] += dgn

    outs = [_sds((s, D_A)), _sds((s, D_A)), _sds((s, D_B)), _sds((s, S5_LANES)), _sds((s, S5_LANES)), _sds((s, D_C))]
    accs = [_sds((S5_LANES, D_C)), _sds((S5_LANES, D_C)), _sds((D_C, D_C)), _sds((1, D_C)), _sds((1, D_MODEL))]
    return _row_call(name, body, s, [do, ag, ha, ob, hre, him, cu], [d, gn, wcr, wci, wglu], outs, accs)


def _log_f(name, f, bf):
    s = f[0].shape[0]

    def body(f_ref, b_ref, o_ref):
        o_ref[...] = jax.nn.log_sigmoid(f_ref[...] + b_ref[...])

    return _row_call(name, body, s, [f], [bf], [_sds((s, 128))], [])[0]


def _log_f_bwd(name, dlf, f, bf):
    s = dlf.shape[0]

    def body(dl_ref, f_ref, b_ref, df_ref, db_ref):
        _zero_at_first([db_ref])
        df = dl_ref[...] * jax.nn.sigmoid(-(f_ref[...] + b_ref[...]))
        df_ref[...] = df
        db_ref[...] += jnp.sum(df, axis=0, keepdims=True)

    return _row_call(name, body, s, [dlf, f], [bf], [_sds((s, 128))], [_sds((1, 128))])


def _s5_decay_grad(name, h_re, h_im, g_re, g_im):
    s = g_re.shape[0]
    tm = ROW_TILE

    def body(hr_ref, hi_ref, hhr_ref, hhi_ref, gr_ref, gi_ref, dr_ref, di_ref):
        i = pl.program_id(0)
        _zero_at_first([dr_ref, di_ref])

        def previous(h_ref, halo_ref):
            halo = jnp.where(i == 0, 0.0, halo_ref[...])
            return pltpu.roll(jnp.concatenate([halo, h_ref[...]], axis=0), 1, 0)[8:, :]

        hr, hi, gr, gi = previous(hr_ref, hhr_ref), previous(hi_ref, hhi_ref), gr_ref[...], gi_ref[...]
        dr_ref[...] += jnp.sum(hr * gr + hi * gi, axis=0, keepdims=True)
        di_ref[...] += jnp.sum(hr * gi - hi * gr, axis=0, keepdims=True)

    rows = pl.BlockSpec((tm, S5_LANES), lambda i: (i, 0))
    halo = pl.BlockSpec((8, S5_LANES), lambda i: (jnp.maximum(i * (tm // 8) - 1, 0), 0))
    acc = pl.BlockSpec((1, S5_LANES), lambda i: (0, 0))
    return pl.pallas_call(
        body,
        name=name,
        grid=(s // tm,),
        in_specs=[rows, rows, halo, halo, rows, rows],
        out_specs=[acc, acc],
        out_shape=[_sds((1, S5_LANES)), _sds((1, S5_LANES))],
        compiler_params=_params("arbitrary"),
    )(h_re, h_im, h_re, h_im, g_re, g_im)


def _conv_fwd(name, ax, w, b):
    s = ax.shape[0]
    tm = ROW_TILE

    def body(x_ref, halo_ref, w_ref, b_ref, o_ref):
        i = pl.program_id(0)
        x = x_ref[...]
        halo = jnp.where(i == 0, 0.0, halo_ref[...])
        ext = jnp.concatenate([halo, x], axis=0)
        acc = b_ref[...] + w_ref[3:4, :] * x
        for k in range(CONV_WIDTH - 1):
            acc = acc + w_ref[k:k + 1, :] * pltpu.roll(ext, CONV_WIDTH - 1 - k, 0)[8:, :]
        o_ref[...] = acc

    return pl.pallas_call(
        body,
        name=name,
        grid=(s // tm,),
        in_specs=[pl.BlockSpec((tm, D_A), lambda i: (i, 0)),
                  pl.BlockSpec((8, D_A), lambda i: (jnp.maximum(i * (tm // 8) - 1, 0), 0)),
                  pl.BlockSpec((CONV_WIDTH, D_A), lambda i: (0, 0)),
                  pl.BlockSpec((1, D_A), lambda i: (0, 0))],
        out_specs=pl.BlockSpec((tm, D_A), lambda i: (i, 0)),
        out_shape=_sds((s, D_A)),
        compiler_params=_params("arbitrary"),
    )(ax, ax, w, b)


def _conv_bwd(name, dxa, ax, w):
    s = ax.shape[0]
    tm = ROW_TILE
    nblk = s // tm

    def body(dx_ref, dnext_ref, x_ref, halo_ref, w_ref, dax_ref, dw_ref):
        i = pl.program_id(0)
        _zero_at_first([dw_ref])
        dx = dx_ref[...]
        dnext = jnp.where(i == nblk - 1, 0.0, dnext_ref[...])
        dext = jnp.concatenate([dx, dnext], axis=0)
        x = x_ref[...]
        halo = jnp.where(i == 0, 0.0, halo_ref[...])
        ext = jnp.concatenate([halo, x], axis=0)
        acc = w_ref[3:4, :] * dx
        dw_ref[3:4, :] += jnp.sum(dx * x, axis=0, keepdims=True)
        for k in range(CONV_WIDTH - 1):
            sh = CONV_WIDTH - 1 - k
            acc = acc + w_ref[k:k + 1, :] * pltpu.roll(dext, tm + 8 - sh, 0)[:tm, :]
            dw_ref[k:k + 1, :] += jnp.sum(dx * pltpu.roll(ext, sh, 0)[8:, :], axis=0, keepdims=True)
        dw_ref[4:5, :] += jnp.sum(dx, axis=0, keepdims=True)
        dax_ref[...] = acc

    return pl.pallas_call(
        body,
        name=name,
        grid=(nblk,),
        in_specs=[pl.BlockSpec((tm, D_A), lambda i: (i, 0)),
                  pl.BlockSpec((8, D_A), lambda i: (jnp.minimum((i + 1) * (tm // 8), s // 8 - 1), 0)),
                  pl.BlockSpec((tm, D_A), lambda i: (i, 0)),
                  pl.BlockSpec((8, D_A), lambda i: (jnp.maximum(i * (tm // 8) - 1, 0), 0)),
                  pl.BlockSpec((CONV_WIDTH, D_A), lambda i: (0, 0))],
        out_specs=[pl.BlockSpec((tm, D_A), lambda i: (i, 0)), pl.BlockSpec((8, D_A), lambda i: (0, 0))],
        out_shape=[_sds((s, D_A)), _sds((8, D_A))],
        compiler_params=_params("arbitrary"),
    )(dxa, dxa, ax, ax, w)


SCAN_ROWS = 512


def _row_in_tile(shape):
    return lax.broadcasted_iota(jnp.int32, shape, 0) % 8


def _lin_scan(name, a, b, reverse):
    s, c = a.shape
    t = min(SCAN_ROWS, s)
    nb = s // t

    def body(a_ref, b_ref, h_ref, p_ref, carry_ref):
        @pl.when(pl.program_id(0) == 0)
        def _():
            carry_ref[...] = jnp.zeros_like(carry_ref)

        row = _row_in_tile((t, c))
        p = a_ref[...]
        h = b_ref[...]
        for d in (1, 2, 4):
            keep = (row < 8 - d) if reverse else (row >= d)
            shift = (t - d) if reverse else d
            h = h + jnp.where(keep, p * pltpu.roll(h, shift, 0), 0.0)
            p = jnp.where(keep, p * pltpu.roll(p, shift, 0), p)
        h_ref[...] = h
        p_ref[...] = p
        edge = 0 if reverse else 7

        def tile(k, carry):
            kk = (t // 8 - 1 - k) if reverse else k
            r0 = pl.multiple_of(kk * 8, 8)
            hh = h_ref[pl.ds(r0, 8), :] + p_ref[pl.ds(r0, 8), :] * carry
            h_ref[pl.ds(r0, 8), :] = hh
            return jnp.broadcast_to(hh[edge:edge + 1, :], (8, c))

        carry_ref[...] = lax.fori_loop(0, t // 8, tile, carry_ref[...])

    spec = pl.BlockSpec((t, c), (lambda i: (nb - 1 - i, 0)) if reverse else (lambda i: (i, 0)))
    (out,) = _call(
        body,
        name=name,
        grid=(nb,),
        in_specs=[spec, spec],
        out_specs=[spec],
        out_shape=[_sds((s, c))],
        scratch_shapes=[pltpu.VMEM((t, c), f32), pltpu.VMEM((8, c), f32)],
        compiler_params=_params("arbitrary"),
    )(a, b)
    return out


def _s5_scan(name, b_re, b_im, a_re, a_im, reverse):
    s, c = b_re.shape
    t = min(SCAN_ROWS, s)
    nb = s // t

    def body(br_ref, bi_ref, ar_ref, ai_ref, hr_ref, hi_ref, cr_ref, ci_ref):
        @pl.when(pl.program_id(0) == 0)
        def _():
            cr_ref[...] = jnp.zeros_like(cr_ref)
            ci_ref[...] = jnp.zeros_like(ci_ref)

        ar1, ai1 = ar_ref[...], ai_ref[...]
        pows = [(ar1, ai1)]
        for _ in range(7):
            pr, pi = pows[-1]
            pows.append((pr * ar1 - pi * ai1, pr * ai1 + pi * ar1))
        row8 = lax.broadcasted_iota(jnp.int32, (8, c), 0)
        wr = jnp.zeros((8, c), f32)
        wi = jnp.zeros((8, c), f32)
        for r in range(8):
            pr, pi = pows[(7 - r) if reverse else r]
            wr = jnp.where(row8 == r, pr, wr)
            wi = jnp.where(row8 == r, pi, wi)
        row = _row_in_tile((t, c))
        hr = br_ref[...]
        hi = bi_ref[...]
        for d in (1, 2, 4):
            keep = (row < 8 - d) if reverse else (row >= d)
            shift = (t - d) if reverse else d
            pr, pi = pows[d - 1]
            cr = jnp.where(keep, pr, 0.0)
            ci = jnp.where(keep, pi, 0.0)
            sr = pltpu.roll(hr, shift, 0)
            si = pltpu.roll(hi, shift, 0)
            hr, hi = hr + cr * sr - ci * si, hi + cr * si + ci * sr
        hr_ref[...] = hr
        hi_ref[...] = hi
        edge = 0 if reverse else 7

        def tile(k, carry):
            car_r, car_i = carry
            kk = (t // 8 - 1 - k) if reverse else k
            r0 = pl.multiple_of(kk * 8, 8)
            xr = hr_ref[pl.ds(r0, 8), :] + wr * car_r - wi * car_i
            xi = hi_ref[pl.ds(r0, 8), :] + wr * car_i + wi * car_r
            hr_ref[pl.ds(r0, 8), :] = xr
            hi_ref[pl.ds(r0, 8), :] = xi
            return (jnp.broadcast_to(xr[edge:edge + 1, :], (8, c)), jnp.broadcast_to(xi[edge:edge + 1, :], (8, c)))

        car_r, car_i = lax.fori_loop(0, t // 8, tile, (cr_ref[...], ci_ref[...]))
        cr_ref[...] = car_r
        ci_ref[...] = car_i

    spec = pl.BlockSpec((t, c), (lambda i: (nb - 1 - i, 0)) if reverse else (lambda i: (i, 0)))
    vspec = pl.BlockSpec((1, c), lambda i: (0, 0))
    hr, hi = _call(
        body,
        name=name,
        grid=(nb,),
        in_specs=[spec, spec, vspec, vspec],
        out_specs=[spec, spec],
        out_shape=[_sds((s, c)), _sds((s, c))],
        scratch_shapes=[pltpu.VMEM((8, c), f32), pltpu.VMEM((8, c), f32)],
        compiler_params=_params("arbitrary"),
    )(b_re, b_im, a_re, a_im)
    return hr, hi


ATT_FEAT = 128
ATT_TQ = 1024
ATT_TK = 1024
ATT_TK_KEY_SIDE = 512


def _att_tiles(s, key_side=False):
    tq = min(ATT_TQ, s)
    tk = min(ATT_TK_KEY_SIDE if key_side else ATT_TK, tq)
    return tq, tk, tq // tk


def _keys_le_queries(tk, tq, k0, q0):
    row = lax.broadcasted_iota(jnp.int32, (tk, tq), 0) + k0
    col = lax.broadcasted_iota(jnp.int32, (tk, tq), 1) + q0
    return row <= col


def _attn_fwd_t(name, qt, k_aug, vt):
    h, s, _ = k_aug.shape
    tq, tk, ratio = _att_tiles(s)

    def body(qt_ref, k_ref, vt_ref, o_ref, lse_ref):
        qi = pl.program_id(1)
        qt = qt_ref[...]

        def block(kb, carry, masked):
            m, l, acc = carry
            ks = pl.multiple_of(kb * tk, tk)
            st = jnp.dot(k_ref[pl.ds(ks, tk), :], qt, preferred_element_type=f32)
            if masked:
                st = jnp.where(_keys_le_queries(tk, tq, ks, qi * tq), st, -jnp.inf)
            mn = jnp.maximum(m, jnp.max(st, axis=0, keepdims=True))
            p = jnp.exp(st - mn)
            al = jnp.exp(m - mn)
            l = al * l + jnp.sum(p, axis=0, keepdims=True)
            acc = al * acc + jnp.dot(vt_ref[kb], p.astype(bf16), preferred_element_type=f32)
            return mn, l, acc

        init = (jnp.full((1, tq), -jnp.inf, f32), jnp.zeros((1, tq), f32), jnp.zeros((HEAD_DIM, tq), f32))
        first = lax.fori_loop(0, qi * ratio, lambda kb, c: block(kb, c, False), init)
        m, l, acc = lax.fori_loop(qi * ratio, (qi + 1) * ratio, lambda kb, c: block(kb, c, True), first)
        o_ref[...] = acc / l
        lse_ref[...] = m + jnp.log(l)

    return _call(
        body,
        name=name,
        grid=(h, s // tq),
        in_specs=[pl.BlockSpec((None, None, ATT_FEAT, tq), lambda hh, i: (hh, i, 0, 0)),
                  pl.BlockSpec((None, s, ATT_FEAT), lambda hh, i: (hh, 0, 0)),
                  pl.BlockSpec((None, s // tk, HEAD_DIM, tk), lambda hh, i: (hh, 0, 0, 0))],
        out_specs=[pl.BlockSpec((None, HEAD_DIM, tq), lambda hh, i: (hh, 0, i)),
                   pl.BlockSpec((None, 1, tq), lambda hh, i: (hh, 0, i))],
        out_shape=[_sds((h, HEAD_DIM, s)), _sds((h, 1, s))],
        compiler_params=_params("parallel", "arbitrary"),
    )(qt, k_aug, vt)


def _attn_bwd_dq_t(name, qt, k_aug, v, kt, ot, dot_, lse):
    h, s, _ = k_aug.shape
    tq, tk, ratio = _att_tiles(s)

    def body(qt_ref, k_ref, v_ref, kt_ref, o_ref, do_ref, lse_ref, dq_ref, dl_ref):
        qi = pl.program_id(1)
        qt = qt_ref[...]
        dob = do_ref[...]
        delta = jnp.sum(dob.astype(f32) * o_ref[...], axis=0, keepdims=True)
        lse_v = lse_ref[...]

        def block(kb, carry, masked):
            dq, psum = carry
            ks = pl.multiple_of(kb * tk, tk)
            st = jnp.dot(k_ref[pl.ds(ks, tk), :], qt, preferred_element_type=f32)
            p = jnp.exp(st - lse_v)
            if masked:
                p = jnp.where(_keys_le_queries(tk, tq, ks, qi * tq), p, 0.0)
            dp = jnp.dot(v_ref[pl.ds(ks, tk), :], dob, preferred_element_type=f32)
            ds = p * (dp - delta)
            return (dq + jnp.dot(kt_ref[kb], ds.astype(bf16), preferred_element_type=f32),
                    psum + jnp.sum(p * dp, axis=0, keepdims=True))

        carry = lax.fori_loop(0, qi * ratio, lambda kb, c: block(kb, c, False),
                              (jnp.zeros((HEAD_DIM, tq), f32), jnp.zeros((1, tq), f32)))
        dq, psum = lax.fori_loop(qi * ratio, (qi + 1) * ratio, lambda kb, c: block(kb, c, True), carry)
        dq_ref[...] = dq * ATT_SCALE
        dl_ref[...] = psum

    qspec = pl.BlockSpec((None, HEAD_DIM, tq), lambda hh, i: (hh, 0, i))
    rspec = pl.BlockSpec((None, 1, tq), lambda hh, i: (hh, 0, i))
    return _call(
        body,
        name=name,
        grid=(h, s // tq),
        in_specs=[pl.BlockSpec((None, None, ATT_FEAT, tq), lambda hh, i: (hh, i, 0, 0)),
                  pl.BlockSpec((None, s, ATT_FEAT), lambda hh, i: (hh, 0, 0)),
                  pl.BlockSpec((None, s, HEAD_DIM), lambda hh, i: (hh, 0, 0)),
                  pl.BlockSpec((None, s // tk, HEAD_DIM, tk), lambda hh, i: (hh, 0, 0, 0)),
                  qspec, pl.BlockSpec((None, None, HEAD_DIM, tq), lambda hh, i: (hh, i, 0, 0)), rspec],
        out_specs=[qspec, rspec],
        out_shape=[_sds((h, HEAD_DIM, s)), _sds((h, 1, s))],
        compiler_params=_params("parallel", "arbitrary"),
    )(qt, k_aug, v, kt, ot, dot_, lse)


def _attn_bwd_dkv_t(name, qt_blocks, k_aug, v, qh, do, dot_blocks, lse, delta):
    h, s, _ = k_aug.shape
    tq, tk, ratio = _att_tiles(s, key_side=True)
    nq = s // tq

    def body(qt_ref, k_ref, v_ref, q_ref, do_ref, dot_ref, lse_ref, dl_ref, dk_ref, dv_ref, dck_ref, dsum_ref):
        kj = pl.program_id(1)
        kk = k_ref[...]
        vv = v_ref[...]
        dsum_ref[...] = jnp.zeros_like(dsum_ref)

        def block(qi, carry, masked):
            dk, dv = carry
            qs = pl.multiple_of(qi * tq, tq)
            st = jnp.dot(kk, qt_ref[qi], preferred_element_type=f32)
            p = jnp.exp(st - lse_ref[qi])
            if masked:
                p = jnp.where(_keys_le_queries(tk, tq, kj * tk, qs), p, 0.0)
            dv = dv + jnp.dot(p.astype(bf16), do_ref[pl.ds(qs, tq), :], preferred_element_type=f32)
            dp = jnp.dot(vv, dot_ref[qi], preferred_element_type=f32)
            ds = p * (dp - dl_ref[qi])
            dsum_ref[...] += ds
            dk = dk + jnp.dot(ds.astype(bf16), q_ref[pl.ds(qs, tq), :], preferred_element_type=f32)
            return dk, dv

        first = kj // ratio
        carry = block(first, (jnp.zeros((tk, HEAD_DIM), f32), jnp.zeros((tk, HEAD_DIM), f32)), True)
        dk, dv = lax.fori_loop(first + 1, nq, lambda qi, c: block(qi, c, False), carry)
        dk_ref[...] = dk
        dv_ref[...] = dv
        col = jnp.sum(dsum_ref[...], axis=1, keepdims=True)
        dck_ref[...] = -jnp.transpose(jnp.broadcast_to(col, (tk, 128)))[0:1, :]

    full = lambda shape: pl.BlockSpec((None,) + shape, lambda hh, j: (hh,) + (0,) * len(shape))
    kspec = pl.BlockSpec((None, tk, HEAD_DIM), lambda hh, j: (hh, j, 0))
    return _call(
        body,
        name=name,
        grid=(h, s // tk),
        in_specs=[full((nq, ATT_FEAT, tq)),
                  pl.BlockSpec((None, tk, ATT_FEAT), lambda hh, j: (hh, j, 0)),
                  kspec, full((s, HEAD_DIM)), full((s, HEAD_DIM)), full((nq, HEAD_DIM, tq)),
                  full((nq, 1, tq)), full((nq, 1, tq))],
        out_specs=[kspec, kspec, pl.BlockSpec((None, None, 1, tk), lambda hh, j: (hh, j, 0, 0))],
        out_shape=[_sds((h, s, HEAD_DIM)), _sds((h, s, HEAD_DIM)), _sds((h, s // tk, 1, tk))],
        scratch_shapes=[pltpu.VMEM((tk, tq), f32)],
        compiler_params=_params("parallel", "arbitrary"),
    )(qt_blocks, k_aug, v, qh, do, dot_blocks, lse, delta)


C_LANES = 128


def _selections():
    h = jnp.arange(N_HEADS)[:, None, None]
    row = jnp.arange(D_B + 3 * C_LANES)[None, :, None]
    col = jnp.arange(ATT_FEAT)[None, None, :]
    head_col = (row < D_B) & (row // HEAD_DIM == h) & (col == row % HEAD_DIM)

    def c_part(p, lane0):
        return (row == D_B + p * C_LANES + h) & (col == lane0 + p)

    c_q = c_part(0, HEAD_DIM) | c_part(1, HEAD_DIM) | c_part(2, HEAD_DIM)
    c_k = c_part(0, HEAD_DIM + 3) | c_part(1, HEAD_DIM + 3) | c_part(2, HEAD_DIM + 3)
    sel_q = (head_col | c_q).astype(bf16)
    sel_k = head_col.astype(bf16) - c_k.astype(bf16)
    sel_h = head_col[:, :D_B, :HEAD_DIM].astype(bf16)
    lane = jnp.arange(ATT_FEAT)
    ones_q = ((lane >= HEAD_DIM + 3) & (lane < HEAD_DIM + 6)).astype(f32)
    ones_k = ((lane >= HEAD_DIM) & (lane < HEAD_DIM + 3)).astype(f32)
    return dict(sel_qt=sel_q.transpose(0, 2, 1), sel_k=sel_k, sel_h=sel_h, sel_ht=sel_h.transpose(0, 2, 1),
                ones_q=ones_q.reshape(ATT_FEAT, 1), ones_k=ones_k.reshape(1, ATT_FEAT))


def _attn_prep(name, z, c, sel):
    s = z.shape[0]
    tq, tk, ratio = _att_tiles(s)

    def body(q_ref, k_ref, v_ref, c_ref, sqt_ref, sk_ref, sh_ref, sht_ref, oq_ref, ok_ref,
             qt_out, ka_out, kt_out, vt_out, v_out, qh_out):
        cv = c_ref[...]
        hi = cv.astype(bf16)
        r1 = cv - hi.astype(f32)
        mid = r1.astype(bf16)
        lo = (r1 - mid.astype(f32)).astype(bf16)
        qs = (q_ref[...] * ATT_SCALE).astype(bf16)
        kb = k_ref[...].astype(bf16)
        vb = v_ref[...].astype(bf16)
        xq = jnp.concatenate([qs, hi, mid, lo], axis=-1)
        xk = jnp.concatenate([kb, hi, mid, lo], axis=-1)
        for h in range(N_HEADS):
            qt = lax.dot_general(sqt_ref[h], xq, _DN["nt"], preferred_element_type=f32) + oq_ref[...]
            qt_out[h, 0] = qt.astype(bf16)
            ka_out[h] = (jnp.dot(xk, sk_ref[h], preferred_element_type=f32) + ok_ref[...]).astype(bf16)
            kt = lax.dot_general(sht_ref[h], kb, _DN["nt"], preferred_element_type=f32).astype(bf16)
            vt = lax.dot_general(sht_ref[h], vb, _DN["nt"], preferred_element_type=f32).astype(bf16)
            for j in range(ratio):
                kt_out[h, j] = kt[:, j * tk:(j + 1) * tk]
                vt_out[h, j] = vt[:, j * tk:(j + 1) * tk]
            v_out[h] = jnp.dot(vb, sh_ref[h], preferred_element_type=f32).astype(bf16)
            qh_out[h] = jnp.dot(qs, sh_ref[h], preferred_element_type=f32).astype(bf16)

    whole = lambda a: pl.BlockSpec(a.shape, lambda i, nd=a.ndim: (0,) * nd)
    consts = [sel["sel_qt"], sel["sel_k"], sel["sel_h"], sel["sel_ht"], sel["ones_q"], sel["ones_k"]]
    return pl.pallas_call(
        body,
        name=name,
        grid=(s // tq,),
        in_specs=[pl.BlockSpec((tq, D_B), lambda i: (i, 2)), pl.BlockSpec((tq, D_B), lambda i: (i, 3)),
                  pl.BlockSpec((tq, D_B), lambda i: (i, 4)), pl.BlockSpec((tq, C_LANES), lambda i: (i, 0))]
        + [whole(a) for a in consts],
        out_specs=[pl.BlockSpec((N_HEADS, 1, ATT_FEAT, tq), lambda i: (0, i, 0, 0)),
                   pl.BlockSpec((N_HEADS, tq, ATT_FEAT), lambda i: (0, i, 0)),
                   pl.BlockSpec((N_HEADS, ratio, HEAD_DIM, tk), lambda i: (0, i, 0, 0)),
                   pl.BlockSpec((N_HEADS, ratio, HEAD_DIM, tk), lambda i: (0, i, 0, 0)),
                   pl.BlockSpec((N_HEADS, tq, HEAD_DIM), lambda i: (0, i, 0)),
                   pl.BlockSpec((N_HEADS, tq, HEAD_DIM), lambda i: (0, i, 0))],
        out_shape=[_sds((N_HEADS, s // tq, ATT_FEAT, tq), bf16), _sds((N_HEADS, s, ATT_FEAT), bf16),
                   _sds((N_HEADS, s // tk, HEAD_DIM, tk), bf16), _sds((N_HEADS, s // tk, HEAD_DIM, tk), bf16),
                   _sds((N_HEADS, s, HEAD_DIM), bf16), _sds((N_HEADS, s, HEAD_DIM), bf16)],
        compiler_params=_params("parallel"),
    )(z, z, z, c, *consts)


def _attn_do_prep(name, dob, sel):
    s = dob.shape[0]
    tq = _att_tiles(s)[0]

    def body(do_ref, sh_ref, sht_ref, dot_out, do_out):
        db = do_ref[...].astype(bf16)
        for h in range(N_HEADS):
            dot_out[h, 0] = lax.dot_general(sht_ref[h], db, _DN["nt"], preferred_element_type=f32).astype(bf16)
            do_out[h] = jnp.dot(db, sh_ref[h], preferred_element_type=f32).astype(bf16)

    whole = lambda a: pl.BlockSpec(a.shape, lambda i, nd=a.ndim: (0,) * nd)
    return pl.pallas_call(
        body,
        name=name,
        grid=(s // tq,),
        in_specs=[pl.BlockSpec((tq, D_B), lambda i: (i, 0)), whole(sel["sel_h"]), whole(sel["sel_ht"])],
        out_specs=[pl.BlockSpec((N_HEADS, 1, HEAD_DIM, tq), lambda i: (0, i, 0, 0)),
                   pl.BlockSpec((N_HEADS, tq, HEAD_DIM), lambda i: (0, i, 0))],
        out_shape=[_sds((N_HEADS, s // tq, HEAD_DIM, tq), bf16), _sds((N_HEADS, s, HEAD_DIM), bf16)],
        compiler_params=_params("parallel"),
    )(dob, sel["sel_h"], sel["sel_ht"])


def _dz_assemble(name, dax, dag, dqt, dkh, dvh, df, dcu, sel):
    s = dax.shape[0]
    tm = _tile(s, 512)

    def body(dax_ref, dag_ref, dqt_ref, dk_ref, dv_ref, df_ref, dcu_ref, sht_ref, o_ref):
        dq = jnp.zeros((tm, D_B), f32)
        dk = jnp.zeros((tm, D_B), f32)
        dv = jnp.zeros((tm, D_B), f32)
        for h in range(N_HEADS):
            place = sht_ref[h]
            dq = dq + lax.dot_general(dqt_ref[h].astype(bf16), place, _DN["tn"], preferred_element_type=f32)
            dk = dk + jnp.dot(dk_ref[h].astype(bf16), place, preferred_element_type=f32)
            dv = dv + jnp.dot(dv_ref[h].astype(bf16), place, preferred_element_type=f32)
        pieces = [dax_ref[...], dag_ref[...], dq, dk, dv, df_ref[...], dcu_ref[...]]
        off = 0
        for p in pieces:
            o_ref[:, off:off + p.shape[1]] = p.astype(bf16)
            off += p.shape[1]

    rows = lambda c_: pl.BlockSpec((tm, c_), lambda i: (i, 0))
    heads = pl.BlockSpec((N_HEADS, tm, HEAD_DIM), lambda i: (0, i, 0))
    return pl.pallas_call(
        body,
        name=name,
        grid=(s // tm,),
        in_specs=[rows(D_A), rows(D_A), pl.BlockSpec((N_HEADS, HEAD_DIM, tm), lambda i: (0, 0, i)), heads, heads,
                  rows(128), rows(D_C), pl.BlockSpec(sel["sel_ht"].shape, lambda i: (0, 0, 0))],
        out_specs=rows(N_IN_P),
        out_shape=_sds((s, N_IN_P), bf16),
        compiler_params=_params("parallel"),
    )(dax, dag, dqt, dkh, dvh, df, dcu, sel["sel_ht"])


def _s5_disc_fn(are, aim, ldt):
    dt = jnp.exp(ldt)
    er = jnp.exp(are * dt)
    br = er * jnp.cos(aim * dt)
    bi = er * jnp.sin(aim * dt)
    nr = br - 1.0
    den = are * are + aim * aim
    return br, bi, (nr * are + bi * aim) / den, (bi * are - nr * aim) / den


def _s5_disc(name, are, aim, ldt):
    def body(a_ref, b_ref, c_ref, o0, o1, o2, o3):
        r = _s5_disc_fn(a_ref[...], b_ref[...], c_ref[...])
        o0[...], o1[...], o2[...], o3[...] = r

    shp = _sds((S5_GROUPS, S5_STATE))
    return pl.pallas_call(body, name=name, out_shape=[shp] * 4)(are, aim, ldt)


def _s5_disc_bwd(name, are, aim, ldt, cts):
    def body(a_ref, b_ref, c_ref, d0, d1, d2, d3, o0, o1, o2):
        _, vjp = jax.vjp(_s5_disc_fn, a_ref[...], b_ref[...], c_ref[...])
        o0[...], o1[...], o2[...] = vjp((d0[...], d1[...], d2[...], d3[...]))

    shp = _sds((S5_GROUPS, S5_STATE))
    return pl.pallas_call(body, name=name, out_shape=[shp, shp, _sds((S5_GROUPS, 1))])(are, aim, ldt, *cts)


def _adamw_rows(w, g, m, v):
    m = ADAM_B1 * m + (1.0 - ADAM_B1) * g
    v = ADAM_B2 * v + (1.0 - ADAM_B2) * (g * g)
    m_hat = m / (1.0 - ADAM_B1 ** ADAM_STEP)
    v_hat = v / (1.0 - ADAM_B2 ** ADAM_STEP)
    return -ADAM_LR * (m_hat / (jnp.sqrt(v_hat) + ADAM_EPS) + ADAM_WD * w), m, v


def _adamw(name, w, ga, gb, m, v):
    rows, cols = w.shape
    tr = _row_tile(rows)

    def body(w_ref, ga_ref, gb_ref, m_ref, v_ref, g_out, d_out, m_out, v_out):
        g = ga_ref[...] + gb_ref[...]
        d, mm, vv = _adamw_rows(w_ref[...], g, m_ref[...], v_ref[...])
        g_out[...] = g
        d_out[...] = d
        m_out[...] = mm
        v_out[...] = vv

    spec = pl.BlockSpec((tr, cols), lambda i: (i, 0))
    return pl.pallas_call(
        body, name=name, grid=(rows // tr,), in_specs=[spec] * 5, out_specs=[spec] * 4,
        out_shape=[_sds((rows, cols))] * 4, compiler_params=_params("parallel"),
    )(w, ga, gb, m, v)


def _sum_stack(name, st):
    n, rows, cols = st.shape
    tr = _row_tile(rows)

    def body(s_ref, o_ref):
        acc = s_ref[0].astype(f32)
        for j in range(1, n):
            acc = acc + s_ref[j].astype(f32)
        o_ref[...] = acc

    return pl.pallas_call(
        body, name=name, grid=(rows // tr,), in_specs=[pl.BlockSpec((n, tr, cols), lambda i: (0, i, 0))],
        out_specs=pl.BlockSpec((tr, cols), lambda i: (i, 0)), out_shape=_sds((rows, cols)),
        compiler_params=_params("parallel"),
    )(st)


def _block_diag(w):
    h, n, m = w.shape
    return jnp.einsum("hij,hg->higj", w, jnp.eye(h, dtype=w.dtype)).reshape(h * n, h * m)


def _block_diag_part(dense, h):
    n, m = dense.shape[0] // h, dense.shape[1] // h
    return jnp.einsum("higj,hg->hij", dense.reshape(h, n, h, m), jnp.eye(h, dtype=dense.dtype))


def _s5_matrices(coef_re, coef_im, b_re, b_im, c_re, c_im):
    bb_re = coef_re[:, :, None] * b_re - coef_im[:, :, None] * b_im
    bb_im = coef_re[:, :, None] * b_im + coef_im[:, :, None] * b_re
    wb_re = _block_diag(jnp.swapaxes(bb_re, 1, 2))
    wb_im = _block_diag(jnp.swapaxes(bb_im, 1, 2))
    wc_re = _block_diag(jnp.swapaxes(c_re, 1, 2))
    wc_im = _block_diag(jnp.swapaxes(-c_im, 1, 2))
    return wb_re, wb_im, wc_re, wc_im


def _shift_down(t):
    return jnp.concatenate([jnp.zeros((1, t.shape[1]), t.dtype), t[:-1]], axis=0)


def _shift_up(t):
    return jnp.concatenate([t[1:], jnp.zeros((1, t.shape[1]), t.dtype)], axis=0)


def _row(v):
    return v.reshape(1, -1)


def _ffn_fwd(tag, h, get, names, gamma, beta):
    wg, wu = get(names[0]), get(names[1])
    g, u, act = _ffn_up(tag + "_up", h, wg, wu)
    wd = get(names[2])
    r, out = _mm_ln(tag + "_down", act, wd, h, gamma, beta, 0.5, k_slabs=True)
    return out, dict(h=h, g=g, u=u, act=act, r=r, wg=wg, wu=wu, wd=wd)


def _ffn_bwd(tag, dout, sv, names, gamma, put, after_ln=None):
    s = dout.shape[0]
    dr, dgam, dbet = _ln_bwd(tag + "_lnb", sv["r"], dout, gamma)
    if after_ln is not None:
        after_ln(dgam, dbet)
    put(names[2], _mm_plain(tag + "_dwd", "tn", _Slabs(sv["act"]), dr, (D_FF, D_MODEL, s), scale=0.5, out_dtype=bf16,
                            tiles=(FF_SLAB, 1024, _tile(s, 1024))))
    dg, du = _ffn_dact(tag + "_dact", dr, sv["wd"], sv["g"], sv["u"])
    dwg, dwu = _mm2(tag + "_dwgu", "tn", (D_FF, D_MODEL, s), _Slabs(dg), sv["h"], _Slabs(du), None, separate=True,
                    out_dtype=bf16, tiles=(FF_SLAB, 1024, _tile(s, 1024)))
    put(names[0], dwg)
    put(names[1], dwu)
    slabs = range(dg.shape[0])
    dh = _mm(tag + "_dh", "nn", (s, D_MODEL, FF_SLAB), (_tile(s, 512), D_MODEL, FF_SLAB),
             [_KPart(dg, j) for j in slabs] + [_KPart(du, j) for j in slabs],
             [_KPart(sv["wg"], j) for j in slabs] + [_KPart(sv["wu"], j) for j in slabs],
             [(j, j, 0) for j in range(2 * len(slabs))], 1,
             lambda accs, extras, vecs: [accs[0] + ALPHA * extras[0]], [f32], extras=[dr])[0]
    return dh, dgam, dbet


def _mixer_fwd(tag, h1, w):
    s = h1.shape[0]
    z = _mm_plain(tag + "_win", "nn", h1, w["w_in"], (s, N_IN_P, D_MODEL), tiles=(_tile(s, 512), 768, D_MODEL))
    ag, f, cu_cols = (z, D_A, 1), (z, 128, F_OFF // 128), (z, D_C, CU_OFF // D_C)
    cu = z[:, CU_OFF:]
    xa = _conv_fwd(tag + "_conv", z, w["conv_w"], w["conv_b"])
    a, gated = _rg_gates(tag + "_gates", xa, w["rg_wa"], w["rg_wx"], w["rg_ba"], w["rg_bx"], w["rg_lam"])
    ha = _lin_scan(tag + "_rgscan", a, gated, False)
    ones = jnp.ones((s, 128), f32)
    c = _lin_scan(tag + "_cumf", ones, _log_f(tag + "_logf", f, w["fox_bf"]), False)
    att = dict(zip(("qt", "k_aug", "kt", "vt", "v", "qh"), _attn_prep(tag + "_attnprep", z, c, w["sel"])))
    ot, lse = _attn_fwd_t(tag + "_attn", att["qt"], att["k_aug"], att["vt"])
    ob = ot.reshape(D_B, s).T
    bu_re, bu_im = _mm2(tag + "_s5in", "nn", (s, S5_LANES, D_C), cu, w["wb_re"], None, w["wb_im"], separate=True,
                        tiles=(_tile(s, 512), 1024, D_C))
    hre, him = _s5_scan(tag + "_s5scan", bu_re, bu_im, w["abar_re"], w["abar_im"], False)
    o = _mix_out(tag + "_mixout", ag, ha, ob, hre, him, cu_cols, w["s5_d"], w["mix_g"], w["wc_re"], w["wc_im"],
                 w["w_glu"])
    sv = dict(h1=h1, z=z, ag=ag, f=f, cu=cu, cu_cols=cu_cols, xa=xa, a=a, ha=ha, att=att, ot=ot, lse=lse, ob=ob,
              hre=hre, him=him, o=o)
    return o, sv


def _mixer_bwd(tag, do, dr2, sv, w, put):
    s = do.shape[0]
    (dag, dha, dob, dhre, dhim, dcu1, dwcr, dwci, dwglu, dd, dgn) = _mix_out_bwd(
        tag + "_mixoutb", do, sv["ag"], sv["ha"], sv["ob"], sv["hre"], sv["him"], sv["cu_cols"], w["s5_d"], w["mix_g"],
        w["wc_re"], w["wc_im"], w["w_glu"])
    put("s5_w_glu", dwglu.astype(bf16))
    gre, gim = _s5_scan(tag + "_s5scanb", dhre, dhim, w["abar_re"], -w["abar_im"], True)
    dab_re, dab_im = _s5_decay_grad(tag + "_s5dec", sv["hre"], sv["him"], gre, gim)
    dwb_re, dwb_im = _mm2(tag + "_s5dwb", "tn", (D_C, S5_LANES, s), sv["cu"], gre, None, gim, separate=True,
                          tiles=(D_C, 1024, _tile(s, 1024)))
    dcu = _mm2(tag + "_s5dcu", "nt", (s, D_C, S5_LANES), gre, w["wb_re"], gim, w["wb_im"], add=dcu1,
               tiles=(_tile(s, 512), D_C, 1024))[0]
    att = sv["att"]
    tq = _att_tiles(s)[0]
    nt = s // tq
    dot_blocks, doh = _attn_do_prep(tag + "_doprep", dob, w["sel"])
    dqt, delta = _attn_bwd_dq_t(tag + "_attndq", att["qt"], att["k_aug"], att["v"], att["kt"], sv["ot"], dot_blocks,
                                sv["lse"])
    dkh, dvh, dck = _attn_bwd_dkv_t(tag + "_attndkv", att["qt"], att["k_aug"], att["v"], att["qh"], doh, dot_blocks,
                                    sv["lse"].reshape(N_HEADS, nt, 1, tq), delta.reshape(N_HEADS, nt, 1, tq))
    dc = jnp.pad(dck.reshape(N_HEADS, s).T, ((0, 0), (0, 128 - N_HEADS)))
    dlf = _lin_scan(tag + "_cumfb", jnp.ones((s, 128), f32), dc, True)
    df, dbf = _log_f_bwd(tag + "_logfb", dlf, sv["f"], w["fox_bf"])
    ga = _lin_scan(tag + "_rgscanb", _shift_up(sv["a"]), dha, True)
    dxa, dwa, dwx, dba, dbx, dlam = _rg_gates_bwd(tag + "_gatesb", sv["xa"], ga, _shift_down(sv["ha"]), w["rg_wa"],
                                                  w["rg_wx"], w["rg_ba"], w["rg_bx"], w["rg_lam"])
    dax, dconv = _conv_bwd(tag + "_convb", dxa, sv["z"], w["conv_w"])
    dz = _dz_assemble(tag + "_dz", dax, dag, dqt, dkh, dvh, df, dcu, w["sel"])
    put("w_in", _mm_plain(tag + "_dwin", "tn", sv["h1"], dz, (D_MODEL, N_IN_P, s), out_dtype=bf16,
                          tiles=(512, 768, _tile(s, 1024))))
    dh1 = _mm_plain(tag + "_dh1", "nt", dz, w["w_in"], (s, D_MODEL, N_IN_P), add=dr2, add_coef=ALPHA,
                    tiles=(_tile(s, 512), 1024, 768))
    grads = dict(dconv=dconv, dwa=dwa, dwx=dwx, dba=dba, dbx=dbx, dlam=dlam, dbf=dbf,
                 dab_re=dab_re, dab_im=dab_im, dwb_re=dwb_re, dwb_im=dwb_im, dwcr=dwcr, dwci=dwci, dd=dd, dgn=dgn)
    return dh1, grads


SMALL_NAMES = ["ln1_g", "ln1_b", "conv_w", "conv_b", "rg_w_a", "rg_b_a", "rg_w_x", "rg_b_x", "rg_lambda", "fox_b_f",
               "s5_a_re", "s5_a_im", "s5_log_dt", "s5_b_re", "s5_b_im", "s5_c_re", "s5_c_im", "s5_d", "mix_norm_g",
               "ln2_g", "ln2_b", "ln3_g", "ln3_b"]
BIG_NAMES = ["ffn1_w_gate", "ffn1_w_up", "ffn1_w_down", "w_in", "s5_w_glu", "w_out", "ffn2_w_gate", "ffn2_w_up",
             "ffn2_w_down"]


def _local_step(x, target, weight, small, on_grads, on_small):
    h = x
    saved = []
    sel = _selections()
    for l in range(DEPTH):
        get = functools.partial(weight, l)

        sm = {n: small[n][l] for n in SMALL_NAMES}
        abar_re, abar_im, coef_re, coef_im = _s5_disc(f"l{l}_s5disc", sm["s5_a_re"], sm["s5_a_im"],
                                                      sm["s5_log_dt"].reshape(S5_GROUPS, 1))
        mats, mats_vjp = jax.vjp(_s5_matrices, coef_re, coef_im, sm["s5_b_re"], sm["s5_b_im"], sm["s5_c_re"],
                                 sm["s5_c_im"])
        w = dict(
            sel=sel, conv_w=sm["conv_w"], conv_b=_row(sm["conv_b"]),
            rg_wa=_block_diag(sm["rg_w_a"]).astype(bf16), rg_wx=_block_diag(sm["rg_w_x"]).astype(bf16),
            rg_ba=_row(sm["rg_b_a"]), rg_bx=_row(sm["rg_b_x"]), rg_lam=_row(sm["rg_lambda"]),
            fox_bf=jnp.pad(_row(sm["fox_b_f"]), ((0, 0), (0, 128 - N_HEADS))),
            abar_re=_row(abar_re), abar_im=_row(abar_im),
            wb_re=mats[0].astype(bf16), wb_im=mats[1].astype(bf16), wc_re=mats[2].astype(bf16),
            wc_im=mats[3].astype(bf16), s5_d=_row(sm["s5_d"]), mix_g=_row(sm["mix_norm_g"]))
        h1, sv1 = _ffn_fwd(f"l{l}_ffn1", h, get, GROUPS["F1"], _row(sm["ln1_g"]), _row(sm["ln1_b"]))
        w["w_in"], w["w_glu"] = get("w_in"), get("s5_w_glu")
        o, svm = _mixer_fwd(f"l{l}_mix", h1, w)
        w_out = get("w_out")
        r2, h2 = _mm_ln(f"l{l}_wout", o, w_out, h1, _row(sm["ln2_g"]), _row(sm["ln2_b"]), 1.0)
        h3, sv2 = _ffn_fwd(f"l{l}_ffn2", h2, get, GROUPS["F2"], _row(sm["ln3_g"]), _row(sm["ln3_b"]))
        saved.append(dict(sm=sm, w=w, w_out=w_out, sv1=sv1, svm=svm, r2=r2, sv2=sv2, mats_vjp=mats_vjp))
        h = h3

    dh, loss_row = _loss_head("loss_head", h, target)
    s = x.shape[0]
    gsmall = {n: [None] * DEPTH for n in SMALL_NAMES}
    for l in reversed(range(DEPTH)):
        sd = saved[l]
        sm, w = sd["sm"], sd["w"]

        def put(name, grad, l=l):
            on_grads((l, name), grad)

        dh2, dgam, dbet = _ffn_bwd(f"l{l}_ffn2", dh, sd["sv2"], GROUPS["F2"], _row(sm["ln3_g"]), put)
        gsmall["ln3_g"][l], gsmall["ln3_b"][l] = dgam[0], dbet[0]
        dr2, dgam, dbet = _ln_bwd(f"l{l}_ln2b", sd["r2"], dh2, _row(sm["ln2_g"]))
        gsmall["ln2_g"][l], gsmall["ln2_b"][l] = dgam[0], dbet[0]
        put("w_out", _mm_plain(f"l{l}_dwout", "tn", sd["svm"]["o"], dr2, (D_MODEL, D_MODEL, s), out_dtype=bf16))
        do = _mm_plain(f"l{l}_do", "nt", dr2, sd["w_out"], (s, D_MODEL, D_MODEL))
        dh1, g = _mixer_bwd(f"l{l}_mix", do, dr2, sd["svm"], w, put)
        gsmall["conv_w"][l], gsmall["conv_b"][l] = g["dconv"][:CONV_WIDTH], g["dconv"][CONV_WIDTH]
        gsmall["rg_w_a"][l] = _block_diag_part(g["dwa"], N_HEADS)
        gsmall["rg_w_x"][l] = _block_diag_part(g["dwx"], N_HEADS)
        gsmall["rg_b_a"][l], gsmall["rg_b_x"][l], gsmall["rg_lambda"][l] = g["dba"][0], g["dbx"][0], g["dlam"][0]
        gsmall["fox_b_f"][l] = g["dbf"][0, :N_HEADS]
        dcoef_re, dcoef_im, db_re, db_im, dc_re, dc_im = sd["mats_vjp"]((g["dwb_re"], g["dwb_im"], g["dwcr"], g["dwci"]))
        da_re, da_im, dldt = _s5_disc_bwd(
            f"l{l}_s5discb", sm["s5_a_re"], sm["s5_a_im"], sm["s5_log_dt"].reshape(S5_GROUPS, 1),
            (g["dab_re"].reshape(S5_GROUPS, S5_STATE), g["dab_im"].reshape(S5_GROUPS, S5_STATE), dcoef_re, dcoef_im))
        gsmall["s5_a_re"][l], gsmall["s5_a_im"][l], gsmall["s5_log_dt"][l] = da_re, da_im, dldt[:, 0]
        gsmall["s5_b_re"][l], gsmall["s5_b_im"][l], gsmall["s5_c_re"][l], gsmall["s5_c_im"][l] = db_re, db_im, dc_re, dc_im
        gsmall["s5_d"][l], gsmall["mix_norm_g"][l] = g["dd"][0], g["dgn"][0]

        def after_ln(dgam, dbet, l=l):
            gsmall["ln1_g"][l], gsmall["ln1_b"][l] = dgam[0], dbet[0]
            if l == 0:
                on_small({n: jnp.stack(v) for n, v in gsmall.items()})

        dh, _, _ = _ffn_bwd(f"l{l}_ffn1", dh1, sd["sv1"], GROUPS["F1"], _row(sm["ln1_g"]), put, after_ln)
    return loss_row[0, 0], dh


def _position():
    return lax.axis_index("x"), lax.axis_index("y"), lax.axis_index("c")


_ANY = pl.BlockSpec(memory_space=pl.ANY)


COLUMN_SHARDED = ("ffn1_w_gate", "ffn1_w_up", "ffn2_w_gate", "ffn2_w_up")
PACK_QUANTUM = 128 * 256


def _permute_in_cols(w):
    pad = jnp.zeros(w.shape[:-1] + (128 - N_HEADS,), w.dtype)
    return jnp.concatenate([w[..., :F_OFF + N_HEADS], pad, w[..., F_OFF + N_HEADS:]], axis=-1)


def _unpermute_in_cols(w):
    return jnp.concatenate([w[..., :F_OFF + N_HEADS], w[..., CU_OFF:]], axis=-1)


def _pack(arrs):
    flat = jnp.concatenate([a.reshape(-1) for a in arrs])
    pad = -flat.shape[0] % PACK_QUANTUM
    return jnp.pad(flat, (0, pad)).reshape(-1, 128)


def _unpack(buf, shapes):
    flat = buf.reshape(-1)
    out, off = [], 0
    for shp in shapes:
        size = math.prod(shp)
        out.append(flat[off:off + size].reshape(shp))
        off += size
    return out


WEIGHT_NAMES = ["ffn1_w_gate", "ffn1_w_up", "ffn1_w_down", "ln1_g", "ln1_b", "w_in", "conv_w", "conv_b", "rg_w_a",
                "rg_b_a", "rg_w_x", "rg_b_x", "rg_lambda", "fox_b_f", "s5_a_re", "s5_a_im", "s5_log_dt", "s5_b_re",
                "s5_b_im", "s5_c_re", "s5_c_im", "s5_d", "s5_w_glu", "mix_norm_g", "w_out", "ln2_g", "ln2_b",
                "ffn2_w_gate", "ffn2_w_up", "ffn2_w_down", "ln3_g", "ln3_b"]


def _remote(src, dst, send_sems, recv_sems, k, peer):
    return pltpu.make_async_remote_copy(src_ref=src, dst_ref=dst, send_sem=send_sems.at[k], recv_sem=recv_sems.at[k],
                                        device_id=peer, device_id_type=MESH)


class _ChipGatherPart:
    def __init__(self, arrays):
        self.arrays, self.results = list(arrays), None

    def out_shape(self):
        return [_sds((N_CHIPS,) + a.shape, a.dtype) for a in self.arrays]

    def sems(self):
        n = len(self.arrays)
        return [pltpu.SemaphoreType.DMA((3 * n,)), pltpu.SemaphoreType.DMA((3 * n,)), pltpu.SemaphoreType.DMA((n,))]

    def copies(self, ins, outs, sems):
        send_sems, recv_sems, local_sems = sems
        x, y, c = _position()
        me = 2 * x + y
        local, sends, recvs = [], [], []
        for i, (src, dst) in enumerate(zip(ins, outs)):
            local.append(pltpu.make_async_copy(self.mine(src, me), dst.at[me], local_sems.at[i]))
            for r, (px, py) in enumerate([(1 - x, y), (x, 1 - y), (1 - x, 1 - y)]):
                peer = 2 * px + py
                sends.append(_remote(self.theirs(src, peer), dst.at[me], send_sems, recv_sems, 3 * i + r, (px, py, c)))
                recvs.append(_remote(self.mine(src, me), dst.at[peer], send_sems, recv_sems, 3 * i + r, (px, py, c)))
        return local, sends, recvs

    def mine(self, src, me):
        return src

    def theirs(self, src, peer):
        return src


class _ChipGatherHalvesPart(_ChipGatherPart):
    def sems(self):
        n = len(self.arrays)
        return super().sems() + [pltpu.SemaphoreType.DMA((3 * n,)), pltpu.SemaphoreType.DMA((3 * n,))]

    def _half(self, ref, which):
        rows = ref.shape[0] // 2
        return ref.at[pl.ds(which * rows, rows)]

    def copies(self, ins, outs, sems):
        send_sems, recv_sems, local_sems = sems[:3]
        x, y, c = _position()
        me = 2 * x + y
        local, sends, recvs = [], [], []
        for i, (src, dst) in enumerate(zip(ins, outs)):
            local.append(pltpu.make_async_copy(src, dst.at[me], local_sems.at[i]))
            for r, (px, py) in enumerate([(1 - x, y), (x, 1 - y), (1 - x, 1 - y)]):
                sends.append(_remote(self._half(src, c), self._half(dst.at[me], c), send_sems, recv_sems, 3 * i + r,
                                     (px, py, c)))
                recvs.append(_remote(self._half(src, c), self._half(dst.at[2 * px + py], c), send_sems, recv_sems,
                                     3 * i + r, (px, py, c)))
        return local, sends, recvs

    def forwards(self, ins, outs, sems):
        send_sems, recv_sems = sems[3:]
        x, y, c = _position()
        sends, recvs = [], []
        for i, dst in enumerate(outs):
            for r, (px, py) in enumerate([(1 - x, y), (x, 1 - y), (1 - x, 1 - y)]):
                slot = dst.at[2 * px + py]
                sends.append(_remote(self._half(slot, c), self._half(slot, c), send_sems, recv_sems, 3 * i + r,
                                     (x, y, 1 - c)))
                recvs.append(_remote(self._half(slot, c), self._half(slot, 1 - c), send_sems, recv_sems, 3 * i + r,
                                     (x, y, 1 - c)))
        return sends, recvs


class _ChipScatterPart(_ChipGatherPart):
    def out_shape(self):
        return [_sds(a.shape, a.dtype) for a in self.arrays]

    def mine(self, src, me):
        return src.at[me]

    def theirs(self, src, peer):
        return src.at[peer]


class _SiblingSwapPart:
    def __init__(self, arrays):
        self.arrays, self.results = list(arrays), None

    def out_shape(self):
        return [_sds(a.shape, a.dtype) for a in self.arrays]

    def sems(self):
        n = len(self.arrays)
        return [pltpu.SemaphoreType.DMA((n,)), pltpu.SemaphoreType.DMA((n,))]

    def copies(self, ins, outs, sems):
        x, y, c = _position()
        both = [_remote(src, dst, sems[0], sems[1], i, (x, y, 1 - c)) for i, (src, dst) in enumerate(zip(ins, outs))]
        return [], both, both


def _split_by(parts, refs, count):
    out, off = [], 0
    for p in parts:
        out.append(refs[off:off + count(p)])
        off += count(p)
    return out


def _parts_refs(parts, in_refs, out_refs, sem_refs):
    return zip(parts, _split_by(parts, in_refs, lambda p: len(p.arrays)),
               _split_by(parts, out_refs, lambda p: len(p.arrays)), _split_by(parts, sem_refs, lambda p: len(p.sems())))


def _exchange_start(parts, in_refs, out_refs, sem_refs):
    for part, ins, outs, sems in _parts_refs(parts, in_refs, out_refs, sem_refs):
        local, sends, _ = part.copies(ins, outs, sems)
        for cp in local + sends:
            cp.start()


def _exchange_finish(parts, in_refs, out_refs, sem_refs):
    split = list(_parts_refs(parts, in_refs, out_refs, sem_refs))
    copies = [part.copies(ins, outs, sems) for part, ins, outs, sems in split]
    for _, _, recvs in copies:
        for cp in recvs:
            cp.wait_recv()
    second = [part.forwards(ins, outs, sems) for part, ins, outs, sems in split if hasattr(part, "forwards")]
    for sends, _ in second:
        for cp in sends:
            cp.start()
    for sends, recvs in second:
        for cp in recvs:
            cp.wait_recv()
        for cp in sends:
            cp.wait_send()
    for local, sends, _ in copies:
        for cp in sends:
            cp.wait_send()
        for cp in local:
            cp.wait()


def _exchange_operands(parts):
    return ([a for p in parts for a in p.arrays], [s for p in parts for s in p.out_shape()],
            [s for p in parts for s in p.sems()])


def _set_results(parts, res):
    for part, outs in zip(parts, _split_by(parts, list(res), lambda p: len(p.arrays))):
        part.results = list(outs)


def _exchange_now(name, parts):
    x_in, x_out, x_sem = _exchange_operands(parts)
    n = len(x_in)

    def body(*refs):
        _exchange_start(parts, refs[:n], refs[n:2 * n], refs[2 * n:])
        _exchange_finish(parts, refs[:n], refs[n:2 * n], refs[2 * n:])

    res = pl.pallas_call(body, name=name, in_specs=[_ANY] * n, out_specs=[_ANY] * n, out_shape=x_out,
                         scratch_shapes=x_sem)(*x_in)
    _set_results(parts, res)


_RIDERS = {}


def _call(body, *, name, grid, in_specs, out_specs, out_shape, scratch_shapes=(), compiler_params=None):
    make_parts = _RIDERS.pop(name, None)
    if make_parts is None:
        return pl.pallas_call(body, name=name, grid=grid, in_specs=in_specs, out_specs=out_specs, out_shape=out_shape,
                              scratch_shapes=scratch_shapes, compiler_params=compiler_params)
    parts = make_parts()
    x_in, x_out, x_sem = _exchange_operands(parts)
    n_out, n_scr, n_x = len(out_shape), len(scratch_shapes), len(x_in)

    def run(*args):
        n_in = len(args)

        def hosted(*refs):
            ins, xi = refs[:n_in], refs[n_in:n_in + n_x]
            outs, xo = refs[n_in + n_x:n_in + n_x + n_out], refs[n_in + n_x + n_out:n_in + 2 * n_x + n_out]
            scr, xs = refs[n_in + 2 * n_x + n_out:n_in + 2 * n_x + n_out + n_scr], refs[n_in + 2 * n_x + n_out + n_scr:]
            first = functools.reduce(jnp.logical_and, [pl.program_id(d) == 0 for d in range(len(grid))])
            last = functools.reduce(jnp.logical_and, [pl.program_id(d) == grid[d] - 1 for d in range(len(grid))])

            @pl.when(first)
            def _():
                _exchange_start(parts, xi, xo, xs)

            body(*ins, *outs, *scr)

            @pl.when(last)
            def _():
                _exchange_finish(parts, xi, xo, xs)

        res = pl.pallas_call(
            hosted, name=name, grid=grid, in_specs=list(in_specs) + [_ANY] * n_x,
            out_specs=list(out_specs) + [_ANY] * n_x, out_shape=list(out_shape) + x_out,
            scratch_shapes=list(scratch_shapes) + x_sem, compiler_params=_params(*["arbitrary"] * len(grid)),
        )(*args, *x_in)
        _set_results(parts, res[n_out:])
        return list(res[:n_out])

    return run


GROUPS = {"F1": ["ffn1_w_gate", "ffn1_w_up", "ffn1_w_down"], "MX": ["w_in", "s5_w_glu", "w_out"],
          "F2": ["ffn2_w_gate", "ffn2_w_up", "ffn2_w_down"]}
FIRST_GATHER = [(0, "ffn1_w_gate"), (0, "ffn1_w_up")]
GATHER_HOSTS = {
    "l0_ffn1_up": [(0, "ffn1_w_down")],
    "l0_ffn1_down": [(0, "w_in"), (0, "s5_w_glu"), (0, "w_out")],
    "l0_mix_win": [(0, "ffn2_w_gate")],
    "l0_mix_attn": [(0, "ffn2_w_up"), (0, "ffn2_w_down")],
    "l0_ffn2_up": [(1, "ffn1_w_gate")],
    "l0_ffn2_down": [(1, "ffn1_w_up")],
    "l1_ffn1_up": [(1, "ffn1_w_down")],
    "l1_ffn1_down": [(1, "w_in"), (1, "s5_w_glu"), (1, "w_out")],
    "l1_mix_win": [(1, "ffn2_w_gate")],
    "l1_mix_attn": [(1, "ffn2_w_up"), (1, "ffn2_w_down")],
}
SCATTER_HOSTS = {
    "l1_ffn2_dact": [(1, "ffn2_w_down")],
    "l1_ffn2_dh": [(1, "ffn2_w_gate")],
    "l1_mix_attndq": [(1, "ffn2_w_up")],
    "l1_mix_attndkv": [(1, "w_out"), (1, "s5_w_glu")],
    "l1_mix_dh1": [(1, "w_in")],
    "l1_ffn1_dact": [(1, "ffn1_w_down")],
    "l1_ffn1_dh": [(1, "ffn1_w_gate")],
    "l0_ffn2_dact": [(1, "ffn1_w_up")],
    "l0_ffn2_dwgu": [(0, "ffn2_w_down")],
    "l0_ffn2_dh": [(0, "ffn2_w_gate")],
    "l0_mix_attndq": [(0, "w_out"), (0, "s5_w_glu"), (0, "ffn2_w_up")],
    "l0_mix_dh1": [(0, "w_in")],
    "l0_ffn1_dact": [(0, "ffn1_w_down")],
    "l0_ffn1_dh": [(0, "ffn1_w_gate")],
}
LAST_SCATTER = [(0, "ffn1_w_up")]
SMALL_HOST = "l0_ffn1_dwd"
SMALL_PACK_ORDER = [n for n in SMALL_NAMES if n != "conv_w"] + ["conv_w"]
TAIL_HOST = "l0_ffn1_dwgu"
LATE_SCATTER_HOST = "l0_ffn1_dh"


def _sharded_rows(name, a):
    return jnp.swapaxes(a, 1, 2) if name in COLUMN_SHARDED else a


def _unstack_layer(st):
    _, r, c = st.shape
    return st.reshape(N_CHIPS * r, c)


def _restack_layer(g):
    r, c = g.shape
    return g.reshape(N_CHIPS, r // N_CHIPS, c)


def _adamw_layer(name, layer, w, ga, gb, m, v, bufs):
    _, r, c = w.shape
    tr = _row_tile(r)

    def body(w_ref, ga_ref, gb_ref, m_ref, v_ref, *rest):
        g_out, d_out, m_out, v_out = rest[-4:]
        g = ga_ref[...] + gb_ref[...]
        d, mm, vv = _adamw_rows(w_ref[...], g, m_ref[...], v_ref[...])
        g_out[...] = g
        d_out[...] = d
        m_out[...] = mm
        v_out[...] = vv

    full = pl.BlockSpec((None, tr, c), lambda i: (layer, i, 0))
    flat = pl.BlockSpec((tr, c), lambda i: (i, 0))
    extra = {} if bufs is None else dict(input_output_aliases={5 + k: k for k in range(4)})
    return pl.pallas_call(
        body, name=name, grid=(r // tr,),
        in_specs=[full, flat, flat, full, full] + ([] if bufs is None else [_ANY] * 4),
        out_specs=[full] * 4, out_shape=[_sds(w.shape)] * 4, compiler_params=_params("parallel"), **extra,
    )(w, ga, gb, m, v, *([] if bufs is None else bufs))


def _train_step(x, loss_target, w, m, v):
    ix, iy, _ = _position()
    chip = 2 * ix + iy
    shard = {n: (_permute_in_cols(w[n]) if n == "w_in" else _sharded_rows(n, w[n])).astype(bf16) for n in BIG_NAMES}

    gathered = {}

    def gather_parts(keys, extra=()):
        part = _ChipGatherHalvesPart([shard[n][layer] for layer, n in keys] + list(extra))
        gathered.update({key: (part, i) for i, key in enumerate(keys)})
        return [part]

    (first,) = gather_parts(FIRST_GATHER, extra=[w["conv_w"]])
    _exchange_now("gather_first", [first])
    for host, keys in GATHER_HOSTS.items():
        _RIDERS[host] = functools.partial(gather_parts, keys)

    def weight(layer, name):
        part, i = gathered[(layer, name)]
        return _unstack_layer(part.results[i])

    small = {n: w[n] for n in SMALL_NAMES}
    small["conv_w"] = first.results[-1].transpose(1, 2, 0, 3).reshape(DEPTH, CONV_WIDTH, D_A)

    grads_full, scattered = {}, {}

    def scatter_parts(keys):
        part = _ChipScatterPart([_restack_layer(grads_full[key]) for key in keys])
        scattered.update({key: (part, i) for i, key in enumerate(keys)})
        return [part]

    for host, keys in SCATTER_HOSTS.items():
        _RIDERS[host] = functools.partial(scatter_parts, keys)

    partial = {}

    def reduce_chips(keys):
        for layer, n in keys:
            part, i = scattered[(layer, n)]
            p = _sum_stack(f"sum_l{layer}_{n}", part.results[i])
            partial[(layer, n)] = _unpermute_in_cols(p) if n == "w_in" else p

    early = [key for host, keys in SCATTER_HOSTS.items() if host != LATE_SCATTER_HOST for key in keys]
    late = SCATTER_HOSTS[LATE_SCATTER_HOST]
    tail = {}

    def small_parts():
        tail["small"] = _ChipGatherPart([_pack([tail["gsmall"][n] for n in SMALL_PACK_ORDER])])
        return [tail["small"]]

    def tail_parts():
        reduce_chips(early)
        tail["small_sum"] = _sum_stack("sum_small", tail["small"].results[0])
        tail["swap"] = _SiblingSwapPart([partial[k] for k in early] + [tail["small_sum"]])
        return [tail["swap"]]

    _RIDERS[SMALL_HOST] = small_parts
    _RIDERS[TAIL_HOST] = tail_parts
    loss_local, gx = _local_step(x[0], loss_target[0], weight, small, grads_full.__setitem__,
                                 functools.partial(tail.__setitem__, "gsmall"))
    other = dict(zip(early, tail["swap"].results[:-1]))
    small_mine, small_other = tail["small_sum"], tail["swap"].results[-1]
    reduce_chips(late)
    last_parts = scatter_parts(LAST_SCATTER) + [_SiblingSwapPart([partial[k] for k in late])]
    _exchange_now("exchange_last", last_parts)
    other.update(zip(late, last_parts[1].results))
    reduce_chips(LAST_SCATTER)
    swap_last = _SiblingSwapPart([partial[k] for k in LAST_SCATTER])
    _exchange_now("swap_last", [swap_last])
    other.update(zip(LAST_SCATTER, swap_last.results))

    grads, deltas, new_m, new_v = {}, {}, {}, {}
    for n in BIG_NAMES:
        bufs = None
        wr, mr, vr = (_sharded_rows(n, t) for t in (w[n], m[n], v[n]))
        for layer in range(DEPTH):
            bufs = _adamw_layer(f"adamw_l{layer}_{n}", layer, wr, partial[(layer, n)], other[(layer, n)], mr, vr, bufs)
        grads[n], deltas[n], new_m[n], new_v[n] = (_sharded_rows(n, t) for t in bufs)
    packed = SMALL_PACK_ORDER[:-1]
    shapes = [w[n].shape for n in packed]
    res = _adamw("adamw_small", _pack([w[n] for n in packed]), small_mine, small_other,
                 _pack([m[n] for n in packed]), _pack([v[n] for n in packed]))
    for dst, buf in zip((grads, deltas, new_m, new_v), res):
        dst.update(zip(packed, _unpack(buf, shapes)))
    cw = D_A // N_CHIPS
    conv_shape = (DEPTH, CONV_WIDTH, D_A)
    offset = sum(math.prod(s_) for s_ in shapes)

    def conv_grad(buf):
        full = buf.reshape(-1)[offset:offset + math.prod(conv_shape)].reshape(conv_shape)
        return lax.dynamic_slice_in_dim(full, chip * cw, cw, axis=2).reshape(DEPTH * CONV_WIDTH, cw)

    rows = lambda t: t.reshape(DEPTH * CONV_WIDTH, cw)
    res = _adamw("adamw_conv_w", rows(w["conv_w"]), conv_grad(small_mine), conv_grad(small_other),
                 rows(m["conv_w"]), rows(v["conv_w"]))
    for dst, buf in zip((grads, deltas, new_m, new_v), res):
        dst["conv_w"] = buf.reshape(w["conv_w"].shape)

    loss = lax.psum(loss_local, ("x", "y", "c"))
    return (loss, gx[None], *[grads[n] for n in WEIGHT_NAMES], *[deltas[n] for n in WEIGHT_NAMES],
            *[new_m[n] for n in WEIGHT_NAMES], *[new_v[n] for n in WEIGHT_NAMES])


def kernel(x, ffn1_w_gate, ffn1_w_up, ffn1_w_down, ln1_g, ln1_b, w_in, conv_w, conv_b, rg_w_a, rg_b_a, rg_w_x, rg_b_x, rg_lambda, fox_b_f, s5_a_re, s5_a_im, s5_log_dt, s5_b_re, s5_b_im, s5_c_re, s5_c_im, s5_d, s5_w_glu, mix_norm_g, w_out, ln2_g, ln2_b, ffn2_w_gate, ffn2_w_up, ffn2_w_down, ln3_g, ln3_b, loss_target, m_ffn1_w_gate, m_ffn1_w_up, m_ffn1_w_down, m_ln1_g, m_ln1_b, m_w_in, m_conv_w, m_conv_b, m_rg_w_a, m_rg_b_a, m_rg_w_x, m_rg_b_x, m_rg_lambda, m_fox_b_f, m_s5_a_re, m_s5_a_im, m_s5_log_dt, m_s5_b_re, m_s5_b_im, m_s5_c_re, m_s5_c_im, m_s5_d, m_s5_w_glu, m_mix_norm_g, m_w_out, m_ln2_g, m_ln2_b, m_ffn2_w_gate, m_ffn2_w_up, m_ffn2_w_down, m_ln3_g, m_ln3_b, v_ffn1_w_gate, v_ffn1_w_up, v_ffn1_w_down, v_ln1_g, v_ln1_b, v_w_in, v_conv_w, v_conv_b, v_rg_w_a, v_rg_b_a, v_rg_w_x, v_rg_b_x, v_rg_lambda, v_fox_b_f, v_s5_a_re, v_s5_a_im, v_s5_log_dt, v_s5_b_re, v_s5_b_im, v_s5_c_re, v_s5_c_im, v_s5_d, v_s5_w_glu, v_mix_norm_g, v_w_out, v_ln2_g, v_ln2_b, v_ffn2_w_gate, v_ffn2_w_up, v_ffn2_w_down, v_ln3_g, v_ln3_b):
    args = dict(locals())
    w = {n: args[n] for n in WEIGHT_NAMES}
    m = {n: args["m_" + n] for n in WEIGHT_NAMES}
    v = {n: args["v_" + n] for n in WEIGHT_NAMES}
    return _train_step(x, loss_target, w, m, v)
```

```python
import functools
import math

import jax
import jax.numpy as jnp
from jax import lax
from jax.experimental import pallas as pl
from jax.experimental.pallas import tpu as pltpu

f32 = jnp.float32
bf16 = jnp.bfloat16

D_MODEL = 1024
D_FF = 2816
D_A = 384
D_B = 384
D_C = 256
N_HEADS = 6
HEAD_DIM = 64
S5_GROUPS = 16
S5_GROUP = 16
S5_STATE = 64
S5_LANES = S5_GROUPS * S5_STATE
N_IN = 2 * D_A + 3 * D_B + N_HEADS + D_C
F_OFF = 5 * D_A
CU_OFF = F_OFF + 128
N_IN_P = CU_OFF + D_C
CONV_WIDTH = 4
DEPTH = 2
ALPHA = (2 * DEPTH) ** 0.25
LN_EPS = 1e-5
RMS_EPS = 1e-6
RG_C = 8.0
ATT_SCALE = HEAD_DIM ** -0.5
ADAM_LR, ADAM_B1, ADAM_B2, ADAM_EPS, ADAM_WD, ADAM_STEP = 0.001, 0.9, 0.999, 1e-08, 0.01, 10

ROW_TILE = 256
N_CHIPS = 4
N_DEV = 8
MESH = pl.DeviceIdType.MESH

_DN = {
    "nn": (((1,), (0,)), ((), ())),
    "nt": (((1,), (1,)), ((), ())),
    "tn": (((0,), (0,)), ((), ())),
}


def _sds(shape, dtype=f32):
    return jax.ShapeDtypeStruct(shape, dtype)


def _tile(n, target):
    best = None
    for t in range(128, min(n, target) + 1, 128):
        if n % t == 0:
            best = t
    return best or n


def _row_tile(rows, target=256):
    best = None
    for t in range(16, min(rows, target) + 1, 16):
        if rows % t == 0:
            best = t
    return best or rows


def _params(*sem):
    return pltpu.CompilerParams(dimension_semantics=sem)


class _Slabs:
    def __init__(self, x):
        self.x = x


class _KPart:
    def __init__(self, x, j):
        self.x, self.j = x, j


FF_SLAB = D_FF // 4
FFN_ROWS = 1024

def _mm(name, mode, dims, tiles, a_list, b_list, pairs, n_acc, epilogue, outs, extras=(), vecs=(), split_cols=False):
    m, n, k = dims
    tm, tn, tk = tiles
    nk = k // tk
    na, nb, ne, nv, no = len(a_list), len(b_list), len(extras), len(vecs), len(outs)

    def body(*refs):
        a_refs = refs[:na]
        b_refs = refs[na:na + nb]
        e_refs = refs[na + nb:na + nb + ne]
        v_refs = refs[na + nb + ne:na + nb + ne + nv]
        o_refs = refs[na + nb + ne + nv:na + nb + ne + nv + no]
        acc_refs = refs[na + nb + ne + nv + no:]
        a_vals = [r[...].astype(bf16) for r in a_refs]
        b_vals = [r[...].astype(bf16) for r in b_refs]
        products = [(ci, lax.dot_general(a_vals[ai], b_vals[bi], _DN[mode], preferred_element_type=f32))
                    for ai, bi, ci in pairs]

        def finish(accs):
            res = epilogue(accs, [e[...] for e in e_refs], [v[...] for v in v_refs])
            for o, r in zip(o_refs, res):
                o[...] = r.astype(o.dtype)

        if nk == 1:
            accs = [None] * n_acc
            for ci, prod in products:
                accs[ci] = prod if accs[ci] is None else accs[ci] + prod
            finish(accs)
            return
        kk = pl.program_id(2)

        @pl.when(kk == 0)
        def _():
            for acc in acc_refs:
                acc[...] = jnp.zeros_like(acc)

        for ci, prod in products:
            acc_refs[ci][...] += prod

        @pl.when(kk == nk - 1)
        def _():
            finish([acc[...] for acc in acc_refs])

    def a_spec(a):
        if isinstance(a, _KPart):
            return pl.BlockSpec((None, tm, tk), lambda i, j, kk, part=a.j: (part, i, 0))
        if isinstance(a, _Slabs):
            if mode == "tn":
                return pl.BlockSpec((None, tk, tm), lambda i, j, kk: (i, kk, 0))
            return pl.BlockSpec((None, tm, tk), lambda i, j, kk: (kk, i, 0))
        if mode == "tn":
            return pl.BlockSpec((tk, tm), lambda i, j, kk: (kk, i))
        return pl.BlockSpec((tm, tk), lambda i, j, kk: (i, kk))

    def b_spec(b):
        if isinstance(b, _KPart):
            return pl.BlockSpec((tk, tn), lambda i, j, kk, part=b.j: (part, j))
        if isinstance(b, _Slabs):
            if mode == "nt":
                return pl.BlockSpec((None, tn, tk), lambda i, j, kk: (kk, j, 0))
            return pl.BlockSpec((None, tk, tn), lambda i, j, kk: (j, kk, 0))
        if mode == "nt":
            return pl.BlockSpec((tn, tk), lambda i, j, kk: (j, kk))
        return pl.BlockSpec((tk, tn), lambda i, j, kk: (kk, j))

    o_spec = pl.BlockSpec((tm, tn), lambda i, j, kk: (i, j))
    o_slab_spec = pl.BlockSpec((None, tm, tn), lambda i, j, kk: (j, i, 0))
    v_spec = pl.BlockSpec((1, tn), lambda i, j, kk: (0, j))
    if split_cols:
        out_specs = [o_slab_spec] * no
        out_shape = [_sds((n // tn, m, tn), dt) for dt in outs]
    else:
        out_specs = [o_spec] * no
        out_shape = [_sds((m, n), dt) for dt in outs]
    raw = lambda t: t.x if isinstance(t, (_Slabs, _KPart)) else t
    res = _call(
        body,
        name=name,
        grid=(m // tm, n // tn, nk),
        in_specs=([a_spec(a) for a in a_list] + [b_spec(b) for b in b_list]
                  + [o_slab_spec if isinstance(e, _Slabs) else o_spec for e in extras] + [v_spec] * nv),
        out_specs=out_specs,
        out_shape=out_shape,
        scratch_shapes=[pltpu.VMEM((tm, tn), f32)] * (n_acc if nk > 1 else 0),
        compiler_params=_params("parallel", "parallel", "arbitrary"),
    )(*map(raw, a_list), *map(raw, b_list), *map(raw, extras), *vecs)
    return res


def _sigmoid(x):
    return 0.5 * (jnp.tanh(0.5 * x) + 1.0)


def _layer_norm_rows(r, gamma, beta):
    mu = jnp.mean(r, axis=-1, keepdims=True)
    xc = r - mu
    var = jnp.mean(xc * xc, axis=-1, keepdims=True)
    return xc * lax.rsqrt(var + LN_EPS) * gamma + beta


def _mm_plain(name, mode, a, b, dims, scale=1.0, out_dtype=f32, add=None, add_coef=1.0, tiles=None):
    m, n, k = dims
    tiles = tiles or (_tile(m, 512), _tile(n, 1024), _tile(k, 1024))

    def epilogue(accs, extras, vecs):
        r = accs[0] if scale == 1.0 else accs[0] * scale
        if extras:
            r = r + add_coef * extras[0]
        return [r]

    return _mm(name, mode, dims, tiles, [a], [b], [(0, 0, 0)], 1, epilogue, [out_dtype],
               extras=[] if add is None else [add])[0]


def _ffn_up(name, h, wg, wu):
    s = h.shape[0]

    def epilogue(accs, extras, vecs):
        g, u = accs
        return [g, u, g * _sigmoid(g) * u]

    return _mm(name, "nt", (s, D_FF, D_MODEL), (_tile(s, FFN_ROWS), FF_SLAB, D_MODEL), [h], [wg, wu],
               [(0, 0, 0), (0, 1, 1)], 2, epilogue, [bf16, bf16, bf16], split_cols=True)


def _mm_ln(name, a, w, resid, gamma, beta, scale, k_slabs=False):
    def epilogue(accs, extras, vecs):
        r = ALPHA * extras[0] + scale * accs[0]
        return [r, _layer_norm_rows(r, vecs[0], vecs[1])]

    if k_slabs:
        n_slabs, s, slab = a.shape
        return _mm(name, "nn", (s, D_MODEL, slab), (_tile(s, 512), D_MODEL, slab),
                   [_KPart(a, j) for j in range(n_slabs)], [_KPart(w, j) for j in range(n_slabs)],
                   [(j, j, 0) for j in range(n_slabs)], 1, epilogue, [f32, f32], extras=[resid], vecs=[gamma, beta])
    s, k = a.shape
    return _mm(name, "nn", (s, D_MODEL, k), (_tile(s, FFN_ROWS), D_MODEL, _tile(k, 1024)),
               [a], [w], [(0, 0, 0)], 1, epilogue, [f32, f32], extras=[resid], vecs=[gamma, beta])


def _ffn_dact(name, dr, wd, g, u):
    s = dr.shape[0]

    def epilogue(accs, extras, vecs):
        da = 0.5 * accs[0]
        gg, uu = extras[0].astype(f32), extras[1].astype(f32)
        sg = _sigmoid(gg)
        return [da * uu * (sg * (1.0 + gg * (1.0 - sg))), da * (gg * sg)]

    return _mm(name, "nt", (s, D_FF, D_MODEL), (_tile(s, FFN_ROWS), FF_SLAB, D_MODEL), [dr], [wd],
               [(0, 0, 0)], 1, epilogue, [bf16, bf16], extras=[_Slabs(g), _Slabs(u)], split_cols=True)


def _mm2(name, mode, dims, a0, b0, a1, b1, add=None, add_coef=1.0, separate=False, tiles=None, out_dtype=f32,
         split_cols=False):
    m, n, k = dims
    tiles = tiles or (_tile(m, 512), _tile(n, 1024), _tile(k, 1024))

    def epilogue(accs, extras, vecs):
        if separate:
            return list(accs)
        r = accs[0]
        if extras:
            r = r + add_coef * extras[0]
        return [r]

    a_list = [a0] if a1 is None else [a0, a1]
    b_list = [b0] if b1 is None else [b0, b1]
    pairs = [(0, 0, 0), (len(a_list) - 1, len(b_list) - 1, 1 if separate else 0)]
    return _mm(name, mode, dims, tiles, a_list, b_list, pairs, 2 if separate else 1, epilogue,
               [out_dtype, out_dtype] if separate else [out_dtype], extras=[] if add is None else [add],
               split_cols=split_cols)


def _row_call(name, body, s, ins, params, outs, accs):
    tm = ROW_TILE
    ins = [a if isinstance(a, tuple) else (a, a.shape[1], 0) for a in ins]
    in_specs = [pl.BlockSpec((tm, width), lambda i, cb=cb: (i, cb)) for _, width, cb in ins]
    ins = [a for a, _, _ in ins]
    in_specs += [pl.BlockSpec(p.shape, lambda i, nd=p.ndim: (0,) * nd) for p in params]
    out_specs = [pl.BlockSpec((tm, o.shape[1]), lambda i: (i, 0)) for o in outs]
    out_specs += [pl.BlockSpec(a.shape, lambda i, nd=len(a.shape): (0,) * nd) for a in accs]
    return pl.pallas_call(
        body,
        name=name,
        grid=(s // tm,),
        in_specs=in_specs,
        out_specs=out_specs,
        out_shape=list(outs) + list(accs),
        compiler_params=_params("arbitrary"),
    )(*ins, *params)


def _zero_at_first(refs):
    @pl.when(pl.program_id(0) == 0)
    def _():
        for r in refs:
            r[...] = jnp.zeros_like(r)


def _ln_bwd(name, r, dh, gamma):
    s = r.shape[0]

    def body(r_ref, dh_ref, g_ref, dr_ref, dg_ref, db_ref):
        _zero_at_first([dg_ref, db_ref])
        rr = r_ref[...]
        dy = dh_ref[...]
        mu = jnp.mean(rr, axis=-1, keepdims=True)
        xc = rr - mu
        rstd = lax.rsqrt(jnp.mean(xc * xc, axis=-1, keepdims=True) + LN_EPS)
        xhat = xc * rstd
        dxh = dy * g_ref[...]
        dr_ref[...] = rstd * (dxh - jnp.mean(dxh, axis=-1, keepdims=True)
                              - xhat * jnp.mean(dxh * xhat, axis=-1, keepdims=True))
        dg_ref[...] += jnp.sum(dy * xhat, axis=0, keepdims=True)
        db_ref[...] += jnp.sum(dy, axis=0, keepdims=True)

    return _row_call(name, body, s, [r, dh], [gamma], [_sds((s, D_MODEL))], [_sds((1, D_MODEL)), _sds((1, D_MODEL))])


def _loss_head(name, y, target):
    s = y.shape[0]

    def body(y_ref, t_ref, dy_ref, l_ref):
        _zero_at_first([l_ref])
        e = y_ref[...] - t_ref[...]
        dy_ref[...] = e / D_MODEL
        l_ref[...] += 0.5 * jnp.sum(jnp.mean(e * e, axis=-1, keepdims=True), axis=0, keepdims=True)

    return _row_call(name, body, s, [y, target], [], [_sds((s, D_MODEL))], [_sds((1, 128))])


def _expm1(x):
    series = x * (1.0 + x / 2.0 * (1.0 + x / 3.0 * (1.0 + x / 4.0 * (1.0 + x / 5.0 * (1.0 + x / 6.0 * (1.0 + x / 7.0))))))
    return jnp.where(jnp.abs(x) < 0.25, series, jnp.exp(x) - 1.0)


def _gates_fn(xa, wa, wx, ba, bx, lam, tap_a, tap_x):
    xb = xa.astype(bf16)
    r = jax.nn.sigmoid(jnp.dot(xb, wa, preferred_element_type=f32) + ba + tap_a)
    i = jax.nn.sigmoid(jnp.dot(xb, wx, preferred_element_type=f32) + bx + tap_x)
    log_a = -RG_C * r * jax.nn.softplus(-lam)
    a = jnp.exp(log_a)
    gated = jnp.sqrt(-_expm1(2.0 * log_a)) * (i * xa)
    return a, gated


def _rg_gates(name, xa, wa, wx, ba, bx, lam):
    s = xa.shape[0]

    def body(xa_ref, wa_ref, wx_ref, ba_ref, bx_ref, lam_ref, a_ref, g_ref):
        a, g = _gates_fn(xa_ref[...], wa_ref[...], wx_ref[...], ba_ref[...], bx_ref[...], lam_ref[...], 0.0, 0.0)
        a_ref[...] = a
        g_ref[...] = g

    return _row_call(name, body, s, [xa], [wa, wx, ba, bx, lam], [_sds((s, D_A)), _sds((s, D_A))], [])


def _rg_gates_bwd(name, xa, ga, h_prev, wa, wx, ba, bx, lam):
    s = xa.shape[0]

    def body(xa_ref, ga_ref, hp_ref, wa_ref, wx_ref, ba_ref, bx_ref, lam_ref,
             dxa_ref, dwa_ref, dwx_ref, dba_ref, dbx_ref, dlam_ref):
        _zero_at_first([dwa_ref, dwx_ref, dba_ref, dbx_ref, dlam_ref])
        xa_v = xa_ref[...]
        zero = jnp.zeros((xa_v.shape[0], D_A), f32)
        fn = lambda x, ba_, bx_, lam_, ta, tx: _gates_fn(x, wa_ref[...], wx_ref[...], ba_, bx_, lam_, ta, tx)
        _, vjp = jax.vjp(fn, xa_v, ba_ref[...], bx_ref[...], lam_ref[...], zero, zero)
        gav = ga_ref[...]
        dxa, dba, dbx, dlam, dta, dtx = vjp((gav * hp_ref[...], gav))
        dxa_ref[...] = dxa
        xb = xa_v.astype(bf16)
        dwa_ref[...] += lax.dot_general(xb, dta.astype(bf16), _DN["tn"], preferred_element_type=f32)
        dwx_ref[...] += lax.dot_general(xb, dtx.astype(bf16), _DN["tn"], preferred_element_type=f32)
        dba_ref[...] += dba
        dbx_ref[...] += dbx
        dlam_ref[...] += dlam

    return _row_call(name, body, s, [xa, ga, h_prev], [wa, wx, ba, bx, lam], [_sds((s, D_A))],
                     [_sds((D_A, D_A)), _sds((D_A, D_A)), _sds((1, D_A)), _sds((1, D_A)), _sds((1, D_A))])


def _rms(v, g):
    return v * lax.rsqrt(jnp.mean(v * v, axis=-1, keepdims=True) + RMS_EPS) * g


def _mix_out_fn(ag, ha, ob, hre, him, cu, d, gn, tap_y, tap_gl, wcr, wci, wglu):
    out_a = jax.nn.gelu(ag) * ha
    y = (jnp.dot(hre.astype(bf16), wcr, preferred_element_type=f32)
         + jnp.dot(him.astype(bf16), wci, preferred_element_type=f32) + d * cu + tap_y)
    y2 = jax.nn.gelu(y)
    gl = jnp.dot(y2.astype(bf16), wglu, preferred_element_type=f32) + tap_gl
    out_c = y2 * jax.nn.sigmoid(gl)
    o = jnp.concatenate([_rms(out_a, gn[:, :D_A]), _rms(ob, gn[:, D_A:D_A + D_B]), _rms(out_c, gn[:, D_A + D_B:])],
                        axis=-1)
    return o, y2


def _mix_out(name, ag, ha, ob, hre, him, cu, d, gn, wcr, wci, wglu):
    s = ha.shape[0]

    def body(ag_ref, ha_ref, ob_ref, hre_ref, him_ref, cu_ref, d_ref, gn_ref, wcr_ref, wci_ref, wglu_ref, o_ref):
        o, _ = _mix_out_fn(ag_ref[...], ha_ref[...], ob_ref[...], hre_ref[...], him_ref[...], cu_ref[...], d_ref[...],
                           gn_ref[...], 0.0, 0.0, wcr_ref[...], wci_ref[...], wglu_ref[...])
        o_ref[...] = o.astype(o_ref.dtype)

    return _row_call(name, body, s, [ag, ha, ob, hre, him, cu], [d, gn, wcr, wci, wglu], [_sds((s, D_MODEL), bf16)], [])[0]


def _mix_out_bwd(name, do, ag, ha, ob, hre, him, cu, d, gn, wcr, wci, wglu):
    s = ha.shape[0]

    def body(do_ref, ag_ref, ha_ref, ob_ref, hre_ref, him_ref, cu_ref, d_ref, gn_ref, wcr_ref, wci_ref, wglu_ref,
             dag_ref, dha_ref, dob_ref, dhre_ref, dhim_ref, dcu_ref, dwcr_ref, dwci_ref, dwglu_ref, dd_ref, dgn_ref):
        _zero_at_first([dwcr_ref, dwci_ref, dwglu_ref, dd_ref, dgn_ref])
        tm = ag_ref.shape[0]
        zero = jnp.zeros((tm, D_C), f32)
        hre_v, him_v = hre_ref[...], him_ref[...]
        fn = lambda *a: _mix_out_fn(*a, wcr_ref[...], wci_ref[...], wglu_ref[...])
        _, vjp, y2 = jax.vjp(fn, ag_ref[...], ha_ref[...], ob_ref[...], hre_v, him_v, cu_ref[...], d_ref[...],
                             gn_ref[...], zero, zero, has_aux=True)
        dag, dha, dob, dhre, dhim, dcu, dd, dgn, dy, dgl = vjp(do_ref[...])
        dag_ref[...] = dag
        dha_ref[...] = dha
        dob_ref[...] = dob
        dhre_ref[...] = dhre
        dhim_ref[...] = dhim
        dcu_ref[...] = dcu
        dyb = dy.astype(bf16)
        dwcr_ref[...] += lax.dot_general(hre_v.astype(bf16), dyb, _DN["tn"], preferred_element_type=f32)
        dwci_ref[...] += lax.dot_general(him_v.astype(bf16), dyb, _DN["tn"], preferred_element_type=f32)
        dwglu_ref[...] += lax.dot_general(y2.astype(bf16), dgl.astype(bf16), _DN["tn"], preferred_element_type=f32)
        dd_ref[...] += dd
        dgn_ref[...] += dgn

    outs = [_sds((s, D_A)), _sds((s, D_A)), _sds((s, D_B)), _sds((s, S5_LANES)), _sds((s, S5_LANES)), _sds((s, D_C))]
    accs = [_sds((S5_LANES, D_C)), _sds((S5_LANES, D_C)), _sds((D_C, D_C)), _sds((1, D_C)), _sds((1, D_MODEL))]
    return _row_call(name, body, s, [do, ag, ha, ob, hre, him, cu], [d, gn, wcr, wci, wglu], outs, accs)


def _log_f(name, f, bf):
    s = f[0].shape[0]

    def body(f_ref, b_ref, o_ref):
        o_ref[...] = jax.nn.log_sigmoid(f_ref[...] + b_ref[...])

    return _row_call(name, body, s, [f], [bf], [_sds((s, 128))], [])[0]


def _log_f_bwd(name, dlf, f, bf):
    s = dlf.shape[0]

    def body(dl_ref, f_ref, b_ref, df_ref, db_ref):
        _zero_at_first([db_ref])
        df = dl_ref[...] * jax.nn.sigmoid(-(f_ref[...] + b_ref[...]))
        df_ref[...] = df
        db_ref[...] += jnp.sum(df, axis=0, keepdims=True)

    return _row_call(name, body, s, [dlf, f], [bf], [_sds((s, 128))], [_sds((1, 128))])


def _s5_decay_grad(name, h_re, h_im, g_re, g_im):
    s = g_re.shape[0]
    tm = ROW_TILE

    def body(hr_ref, hi_ref, hhr_ref, hhi_ref, gr_ref, gi_ref, dr_ref, di_ref):
        i = pl.program_id(0)
        _zero_at_first([dr_ref, di_ref])

        def previous(h_ref, halo_ref):
            halo = jnp.where(i == 0, 0.0, halo_ref[...])
            return pltpu.roll(jnp.concatenate([halo, h_ref[...]], axis=0), 1, 0)[8:, :]

        hr, hi, gr, gi = previous(hr_ref, hhr_ref), previous(hi_ref, hhi_ref), gr_ref[...], gi_ref[...]
        dr_ref[...] += jnp.sum(hr * gr + hi * gi, axis=0, keepdims=True)
        di_ref[...] += jnp.sum(hr * gi - hi * gr, axis=0, keepdims=True)

    rows = pl.BlockSpec((tm, S5_LANES), lambda i: (i, 0))
    halo = pl.BlockSpec((8, S5_LANES), lambda i: (jnp.maximum(i * (tm // 8) - 1, 0), 0))
    acc = pl.BlockSpec((1, S5_LANES), lambda i: (0, 0))
    return pl.pallas_call(
        body,
        name=name,
        grid=(s // tm,),
        in_specs=[rows, rows, halo, halo, rows, rows],
        out_specs=[acc, acc],
        out_shape=[_sds((1, S5_LANES)), _sds((1, S5_LANES))],
        compiler_params=_params("arbitrary"),
    )(h_re, h_im, h_re, h_im, g_re, g_im)


def _conv_fwd(name, ax, w, b):
    s = ax.shape[0]
    tm = ROW_TILE

    def body(x_ref, halo_ref, w_ref, b_ref, o_ref):
        i = pl.program_id(0)
        x = x_ref[...]
        halo = jnp.where(i == 0, 0.0, halo_ref[...])
        ext = jnp.concatenate([halo, x], axis=0)
        acc = b_ref[...] + w_ref[3:4, :] * x
        for k in range(CONV_WIDTH - 1):
            acc = acc + w_ref[k:k + 1, :] * pltpu.roll(ext, CONV_WIDTH - 1 - k, 0)[8:, :]
        o_ref[...] = acc

    return pl.pallas_call(
        body,
        name=name,
        grid=(s // tm,),
        in_specs=[pl.BlockSpec((tm, D_A), lambda i: (i, 0)),
                  pl.BlockSpec((8, D_A), lambda i: (jnp.maximum(i * (tm // 8) - 1, 0), 0)),
                  pl.BlockSpec((CONV_WIDTH, D_A), lambda i: (0, 0)),
                  pl.BlockSpec((1, D_A), lambda i: (0, 0))],
        out_specs=pl.BlockSpec((tm, D_A), lambda i: (i, 0)),
        out_shape=_sds((s, D_A)),
        compiler_params=_params("arbitrary"),
    )(ax, ax, w, b)


def _conv_bwd(name, dxa, ax, w):
    s = ax.shape[0]
    tm = ROW_TILE
    nblk = s // tm

    def body(dx_ref, dnext_ref, x_ref, halo_ref, w_ref, dax_ref, dw_ref):
        i = pl.program_id(0)
        _zero_at_first([dw_ref])
        dx = dx_ref[...]
        dnext = jnp.where(i == nblk - 1, 0.0, dnext_ref[...])
        dext = jnp.concatenate([dx, dnext], axis=0)
        x = x_ref[...]
        halo = jnp.where(i == 0, 0.0, halo_ref[...])
        ext = jnp.concatenate([halo, x], axis=0)
        acc = w_ref[3:4, :] * dx
        dw_ref[3:4, :] += jnp.sum(dx * x, axis=0, keepdims=True)
        for k in range(CONV_WIDTH - 1):
            sh = CONV_WIDTH - 1 - k
            acc = acc + w_ref[k:k + 1, :] * pltpu.roll(dext, tm + 8 - sh, 0)[:tm, :]
            dw_ref[k:k + 1, :] += jnp.sum(dx * pltpu.roll(ext, sh, 0)[8:, :], axis=0, keepdims=True)
        dw_ref[4:5, :] += jnp.sum(dx, axis=0, keepdims=True)
        dax_ref[...] = acc

    return pl.pallas_call(
        body,
        name=name,
        grid=(nblk,),
        in_specs=[pl.BlockSpec((tm, D_A), lambda i: (i, 0)),
                  pl.BlockSpec((8, D_A), lambda i: (jnp.minimum((i + 1) * (tm // 8), s // 8 - 1), 0)),
                  pl.BlockSpec((tm, D_A), lambda i: (i, 0)),
                  pl.BlockSpec((8, D_A), lambda i: (jnp.maximum(i * (tm // 8) - 1, 0), 0)),
                  pl.BlockSpec((CONV_WIDTH, D_A), lambda i: (0, 0))],
        out_specs=[pl.BlockSpec((tm, D_A), lambda i: (i, 0)), pl.BlockSpec((8, D_A), lambda i: (0, 0))],
        out_shape=[_sds((s, D_A)), _sds((8, D_A))],
        compiler_params=_params("arbitrary"),
    )(dxa, dxa, ax, ax, w)


SCAN_ROWS = 512


def _row_in_tile(shape):
    return lax.broadcasted_iota(jnp.int32, shape, 0) % 8


def _lin_scan(name, a, b, reverse):
    s, c = a.shape
    t = min(SCAN_ROWS, s)
    nb = s // t

    def body(a_ref, b_ref, h_ref, p_ref, carry_ref):
        @pl.when(pl.program_id(0) == 0)
        def _():
            carry_ref[...] = jnp.zeros_like(carry_ref)

        row = _row_in_tile((t, c))
        p = a_ref[...]
        h = b_ref[...]
        for d in (1, 2, 4):
            keep = (row < 8 - d) if reverse else (row >= d)
            shift = (t - d) if reverse else d
            h = h + jnp.where(keep, p * pltpu.roll(h, shift, 0), 0.0)
            p = jnp.where(keep, p * pltpu.roll(p, shift, 0), p)
        h_ref[...] = h
        p_ref[...] = p
        edge = 0 if reverse else 7

        def tile(k, carry):
            kk = (t // 8 - 1 - k) if reverse else k
            r0 = pl.multiple_of(kk * 8, 8)
            hh = h_ref[pl.ds(r0, 8), :] + p_ref[pl.ds(r0, 8), :] * carry
            h_ref[pl.ds(r0, 8), :] = hh
            return jnp.broadcast_to(hh[edge:edge + 1, :], (8, c))

        carry_ref[...] = lax.fori_loop(0, t // 8, tile, carry_ref[...])

    spec = pl.BlockSpec((t, c), (lambda i: (nb - 1 - i, 0)) if reverse else (lambda i: (i, 0)))
    (out,) = _call(
        body,
        name=name,
        grid=(nb,),
        in_specs=[spec, spec],
        out_specs=[spec],
        out_shape=[_sds((s, c))],
        scratch_shapes=[pltpu.VMEM((t, c), f32), pltpu.VMEM((8, c), f32)],
        compiler_params=_params("arbitrary"),
    )(a, b)
    return out


def _s5_scan(name, b_re, b_im, a_re, a_im, reverse):
    s, c = b_re.shape
    t = min(SCAN_ROWS, s)
    nb = s // t

    def body(br_ref, bi_ref, ar_ref, ai_ref, hr_ref, hi_ref, cr_ref, ci_ref):
        @pl.when(pl.program_id(0) == 0)
        def _():
            cr_ref[...] = jnp.zeros_like(cr_ref)
            ci_ref[...] = jnp.zeros_like(ci_ref)

        ar1, ai1 = ar_ref[...], ai_ref[...]
        pows = [(ar1, ai1)]
        for _ in range(7):
            pr, pi = pows[-1]
            pows.append((pr * ar1 - pi * ai1, pr * ai1 + pi * ar1))
        row8 = lax.broadcasted_iota(jnp.int32, (8, c), 0)
        wr = jnp.zeros((8, c), f32)
        wi = jnp.zeros((8, c), f32)
        for r in range(8):
            pr, pi = pows[(7 - r) if reverse else r]
            wr = jnp.where(row8 == r, pr, wr)
            wi = jnp.where(row8 == r, pi, wi)
        row = _row_in_tile((t, c))
        hr = br_ref[...]
        hi = bi_ref[...]
        for d in (1, 2, 4):
            keep = (row < 8 - d) if reverse else (row >= d)
            shift = (t - d) if reverse else d
            pr, pi = pows[d - 1]
            cr = jnp.where(keep, pr, 0.0)
            ci = jnp.where(keep, pi, 0.0)
            sr = pltpu.roll(hr, shift, 0)
            si = pltpu.roll(hi, shift, 0)
            hr, hi = hr + cr * sr - ci * si, hi + cr * si + ci * sr
        hr_ref[...] = hr
        hi_ref[...] = hi
        edge = 0 if reverse else 7

        def tile(k, carry):
            car_r, car_i = carry
            kk = (t // 8 - 1 - k) if reverse else k
            r0 = pl.multiple_of(kk * 8, 8)
            xr = hr_ref[pl.ds(r0, 8), :] + wr * car_r - wi * car_i
            xi = hi_ref[pl.ds(r0, 8), :] + wr * car_i + wi * car_r
            hr_ref[pl.ds(r0, 8), :] = xr
            hi_ref[pl.ds(r0, 8), :] = xi
            return (jnp.broadcast_to(xr[edge:edge + 1, :], (8, c)), jnp.broadcast_to(xi[edge:edge + 1, :], (8, c)))

        car_r, car_i = lax.fori_loop(0, t // 8, tile, (cr_ref[...], ci_ref[...]))
        cr_ref[...] = car_r
        ci_ref[...] = car_i

    spec = pl.BlockSpec((t, c), (lambda i: (nb - 1 - i, 0)) if reverse else (lambda i: (i, 0)))
    vspec = pl.BlockSpec((1, c), lambda i: (0, 0))
    hr, hi = _call(
        body,
        name=name,
        grid=(nb,),
        in_specs=[spec, spec, vspec, vspec],
        out_specs=[spec, spec],
        out_shape=[_sds((s, c)), _sds((s, c))],
        scratch_shapes=[pltpu.VMEM((8, c), f32), pltpu.VMEM((8, c), f32)],
        compiler_params=_params("arbitrary"),
    )(b_re, b_im, a_re, a_im)
    return hr, hi


ATT_FEAT = 128
ATT_TQ = 1024
ATT_TK = 1024
ATT_TK_KEY_SIDE = 512


def _att_tiles(s, key_side=False):
    tq = min(ATT_TQ, s)
    tk = min(ATT_TK_KEY_SIDE if key_side else ATT_TK, tq)
    return tq, tk, tq // tk


def _keys_le_queries(tk, tq, k0, q0):
    row = lax.broadcasted_iota(jnp.int32, (tk, tq), 0) + k0
    col = lax.broadcasted_iota(jnp.int32, (tk, tq), 1) + q0
    return row <= col


def _attn_fwd_t(name, qt, k_aug, vt):
    h, s, _ = k_aug.shape
    tq, tk, ratio = _att_tiles(s)

    def body(qt_ref, k_ref, vt_ref, o_ref, lse_ref):
        qi = pl.program_id(1)
        qt = qt_ref[...]

        def block(kb, carry, masked):
            m, l, acc = carry
            ks = pl.multiple_of(kb * tk, tk)
            st = jnp.dot(k_ref[pl.ds(ks, tk), :], qt, preferred_element_type=f32)
            if masked:
                st = jnp.where(_keys_le_queries(tk, tq, ks, qi * tq), st, -jnp.inf)
            mn = jnp.maximum(m, jnp.max(st, axis=0, keepdims=True))
            p = jnp.exp(st - mn)
            al = jnp.exp(m - mn)
            l = al * l + jnp.sum(p, axis=0, keepdims=True)
            acc = al * acc + jnp.dot(vt_ref[kb], p.astype(bf16), preferred_element_type=f32)
            return mn, l, acc

        init = (jnp.full((1, tq), -jnp.inf, f32), jnp.zeros((1, tq), f32), jnp.zeros((HEAD_DIM, tq), f32))
        first = lax.fori_loop(0, qi * ratio, lambda kb, c: block(kb, c, False), init)
        m, l, acc = lax.fori_loop(qi * ratio, (qi + 1) * ratio, lambda kb, c: block(kb, c, True), first)
        o_ref[...] = acc / l
        lse_ref[...] = m + jnp.log(l)

    return _call(
        body,
        name=name,
        grid=(h, s // tq),
        in_specs=[pl.BlockSpec((None, None, ATT_FEAT, tq), lambda hh, i: (hh, i, 0, 0)),
                  pl.BlockSpec((None, s, ATT_FEAT), lambda hh, i: (hh, 0, 0)),
                  pl.BlockSpec((None, s // tk, HEAD_DIM, tk), lambda hh, i: (hh, 0, 0, 0))],
        out_specs=[pl.BlockSpec((None, HEAD_DIM, tq), lambda hh, i: (hh, 0, i)),
                   pl.BlockSpec((None, 1, tq), lambda hh, i: (hh, 0, i))],
        out_shape=[_sds((h, HEAD_DIM, s)), _sds((h, 1, s))],
        compiler_params=_params("parallel", "arbitrary"),
    )(qt, k_aug, vt)


def _attn_bwd_dq_t(name, qt, k_aug, v, kt, ot, dot_, lse):
    h, s, _ = k_aug.shape
    tq, tk, ratio = _att_tiles(s)

    def body(qt_ref, k_ref, v_ref, kt_ref, o_ref, do_ref, lse_ref, dq_ref, dl_ref):
        qi = pl.program_id(1)
        qt = qt_ref[...]
        dob = do_ref[...]
        delta = jnp.sum(dob.astype(f32) * o_ref[...], axis=0, keepdims=True)
        lse_v = lse_ref[...]

        def block(kb, carry, masked):
            dq, psum = carry
            ks = pl.multiple_of(kb * tk, tk)
            st = jnp.dot(k_ref[pl.ds(ks, tk), :], qt, preferred_element_type=f32)
            p = jnp.exp(st - lse_v)
            if masked:
                p = jnp.where(_keys_le_queries(tk, tq, ks, qi * tq), p, 0.0)
            dp = jnp.dot(v_ref[pl.ds(ks, tk), :], dob, preferred_element_type=f32)
            ds = p * (dp - delta)
            return (dq + jnp.dot(kt_ref[kb], ds.astype(bf16), preferred_element_type=f32),
                    psum + jnp.sum(p * dp, axis=0, keepdims=True))

        carry = lax.fori_loop(0, qi * ratio, lambda kb, c: block(kb, c, False),
                              (jnp.zeros((HEAD_DIM, tq), f32), jnp.zeros((1, tq), f32)))
        dq, psum = lax.fori_loop(qi * ratio, (qi + 1) * ratio, lambda kb, c: block(kb, c, True), carry)
        dq_ref[...] = dq * ATT_SCALE
        dl_ref[...] = psum

    qspec = pl.BlockSpec((None, HEAD_DIM, tq), lambda hh, i: (hh, 0, i))
    rspec = pl.BlockSpec((None, 1, tq), lambda hh, i: (hh, 0, i))
    return _call(
        body,
        name=name,
        grid=(h, s // tq),
        in_specs=[pl.BlockSpec((None, None, ATT_FEAT, tq), lambda hh, i: (hh, i, 0, 0)),
                  pl.BlockSpec((None, s, ATT_FEAT), lambda hh, i: (hh, 0, 0)),
                  pl.BlockSpec((None, s, HEAD_DIM), lambda hh, i: (hh, 0, 0)),
                  pl.BlockSpec((None, s // tk, HEAD_DIM, tk), lambda hh, i: (hh, 0, 0, 0)),
                  qspec, pl.BlockSpec((None, None, HEAD_DIM, tq), lambda hh, i: (hh, i, 0, 0)), rspec],
        out_specs=[qspec, rspec],
        out_shape=[_sds((h, HEAD_DIM, s)), _sds((h, 1, s))],
        compiler_params=_params("parallel", "arbitrary"),
    )(qt, k_aug, v, kt, ot, dot_, lse)


def _attn_bwd_dkv_t(name, qt_blocks, k_aug, v, qh, do, dot_blocks, lse, delta):
    h, s, _ = k_aug.shape
    tq, tk, ratio = _att_tiles(s, key_side=True)
    nq = s // tq

    def body(qt_ref, k_ref, v_ref, q_ref, do_ref, dot_ref, lse_ref, dl_ref, dk_ref, dv_ref, dck_ref, dsum_ref):
        kj = pl.program_id(1)
        kk = k_ref[...]
        vv = v_ref[...]
        dsum_ref[...] = jnp.zeros_like(dsum_ref)

        def block(qi, carry, masked):
            dk, dv = carry
            qs = pl.multiple_of(qi * tq, tq)
            st = jnp.dot(kk, qt_ref[qi], preferred_element_type=f32)
            p = jnp.exp(st - lse_ref[qi])
            if masked:
                p = jnp.where(_keys_le_queries(tk, tq, kj * tk, qs), p, 0.0)
            dv = dv + jnp.dot(p.astype(bf16), do_ref[pl.ds(qs, tq), :], preferred_element_type=f32)
            dp = jnp.dot(vv, dot_ref[qi], preferred_element_type=f32)
            ds = p * (dp - dl_ref[qi])
            dsum_ref[...] += ds
            dk = dk + jnp.dot(ds.astype(bf16), q_ref[pl.ds(qs, tq), :], preferred_element_type=f32)
            return dk, dv

        first = kj // ratio
        carry = block(first, (jnp.zeros((tk, HEAD_DIM), f32), jnp.zeros((tk, HEAD_DIM), f32)), True)
        dk, dv = lax.fori_loop(first + 1, nq, lambda qi, c: block(qi, c, False), carry)
        dk_ref[...] = dk
        dv_ref[...] = dv
        col = jnp.sum(dsum_ref[...], axis=1, keepdims=True)
        dck_ref[...] = -jnp.transpose(jnp.broadcast_to(col, (tk, 128)))[0:1, :]

    full = lambda shape: pl.BlockSpec((None,) + shape, lambda hh, j: (hh,) + (0,) * len(shape))
    kspec = pl.BlockSpec((None, tk, HEAD_DIM), lambda hh, j: (hh, j, 0))
    return _call(
        body,
        name=name,
        grid=(h, s // tk),
        in_specs=[full((nq, ATT_FEAT, tq)),
                  pl.BlockSpec((None, tk, ATT_FEAT), lambda hh, j: (hh, j, 0)),
                  kspec, full((s, HEAD_DIM)), full((s, HEAD_DIM)), full((nq, HEAD_DIM, tq)),
                  full((nq, 1, tq)), full((nq, 1, tq))],
        out_specs=[kspec, kspec, pl.BlockSpec((None, None, 1, tk), lambda hh, j: (hh, j, 0, 0))],
        out_shape=[_sds((h, s, HEAD_DIM)), _sds((h, s, HEAD_DIM)), _sds((h, s // tk, 1, tk))],
        scratch_shapes=[pltpu.VMEM((tk, tq), f32)],
        compiler_params=_params("parallel", "arbitrary"),
    )(qt_blocks, k_aug, v, qh, do, dot_blocks, lse, delta)


C_LANES = 128


def _selections():
    h = jnp.arange(N_HEADS)[:, None, None]
    row = jnp.arange(D_B + 3 * C_LANES)[None, :, None]
    col = jnp.arange(ATT_FEAT)[None, None, :]
    head_col = (row < D_B) & (row // HEAD_DIM == h) & (col == row % HEAD_DIM)

    def c_part(p, lane0):
        return (row == D_B + p * C_LANES + h) & (col == lane0 + p)

    c_q = c_part(0, HEAD_DIM) | c_part(1, HEAD_DIM) | c_part(2, HEAD_DIM)
    c_k = c_part(0, HEAD_DIM + 3) | c_part(1, HEAD_DIM + 3) | c_part(2, HEAD_DIM + 3)
    sel_q = (head_col | c_q).astype(bf16)
    sel_k = head_col.astype(bf16) - c_k.astype(bf16)
    sel_h = head_col[:, :D_B, :HEAD_DIM].astype(bf16)
    lane = jnp.arange(ATT_FEAT)
    ones_q = ((lane >= HEAD_DIM + 3) & (lane < HEAD_DIM + 6)).astype(f32)
    ones_k = ((lane >= HEAD_DIM) & (lane < HEAD_DIM + 3)).astype(f32)
    return dict(sel_qt=sel_q.transpose(0, 2, 1), sel_k=sel_k, sel_h=sel_h, sel_ht=sel_h.transpose(0, 2, 1),
                ones_q=ones_q.reshape(ATT_FEAT, 1), ones_k=ones_k.reshape(1, ATT_FEAT))


def _attn_prep(name, z, c, sel):
    s = z.shape[0]
    tq, tk, ratio = _att_tiles(s)

    def body(q_ref, k_ref, v_ref, c_ref, sqt_ref, sk_ref, sh_ref, sht_ref, oq_ref, ok_ref,
             qt_out, ka_out, kt_out, vt_out, v_out, qh_out):
        cv = c_ref[...]
        hi = cv.astype(bf16)
        r1 = cv - hi.astype(f32)
        mid = r1.astype(bf16)
        lo = (r1 - mid.astype(f32)).astype(bf16)
        qs = (q_ref[...] * ATT_SCALE).astype(bf16)
        kb = k_ref[...].astype(bf16)
        vb = v_ref[...].astype(bf16)
        xq = jnp.concatenate([qs, hi, mid, lo], axis=-1)
        xk = jnp.concatenate([kb, hi, mid, lo], axis=-1)
        for h in range(N_HEADS):
            qt = lax.dot_general(sqt_ref[h], xq, _DN["nt"], preferred_element_type=f32) + oq_ref[...]
            qt_out[h, 0] = qt.astype(bf16)
            ka_out[h] = (jnp.dot(xk, sk_ref[h], preferred_element_type=f32) + ok_ref[...]).astype(bf16)
            kt = lax.dot_general(sht_ref[h], kb, _DN["nt"], preferred_element_type=f32).astype(bf16)
            vt = lax.dot_general(sht_ref[h], vb, _DN["nt"], preferred_element_type=f32).astype(bf16)
            for j in range(ratio):
                kt_out[h, j] = kt[:, j * tk:(j + 1) * tk]
                vt_out[h, j] = vt[:, j * tk:(j + 1) * tk]
            v_out[h] = jnp.dot(vb, sh_ref[h], preferred_element_type=f32).astype(bf16)
            qh_out[h] = jnp.dot(qs, sh_ref[h], preferred_element_type=f32).astype(bf16)

    whole = lambda a: pl.BlockSpec(a.shape, lambda i, nd=a.ndim: (0,) * nd)
    consts = [sel["sel_qt"], sel["sel_k"], sel["sel_h"], sel["sel_ht"], sel["ones_q"], sel["ones_k"]]
    return pl.pallas_call(
        body,
        name=name,
        grid=(s // tq,),
        in_specs=[pl.BlockSpec((tq, D_B), lambda i: (i, 2)), pl.BlockSpec((tq, D_B), lambda i: (i, 3)),
                  pl.BlockSpec((tq, D_B), lambda i: (i, 4)), pl.BlockSpec((tq, C_LANES), lambda i: (i, 0))]
        + [whole(a) for a in consts],
        out_specs=[pl.BlockSpec((N_HEADS, 1, ATT_FEAT, tq), lambda i: (0, i, 0, 0)),
                   pl.BlockSpec((N_HEADS, tq, ATT_FEAT), lambda i: (0, i, 0)),
                   pl.BlockSpec((N_HEADS, ratio, HEAD_DIM, tk), lambda i: (0, i, 0, 0)),
                   pl.BlockSpec((N_HEADS, ratio, HEAD_DIM, tk), lambda i: (0, i, 0, 0)),
                   pl.BlockSpec((N_HEADS, tq, HEAD_DIM), lambda i: (0, i, 0)),
                   pl.BlockSpec((N_HEADS, tq, HEAD_DIM), lambda i: (0, i, 0))],
        out_shape=[_sds((N_HEADS, s // tq, ATT_FEAT, tq), bf16), _sds((N_HEADS, s, ATT_FEAT), bf16),
                   _sds((N_HEADS, s // tk, HEAD_DIM, tk), bf16), _sds((N_HEADS, s // tk, HEAD_DIM, tk), bf16),
                   _sds((N_HEADS, s, HEAD_DIM), bf16), _sds((N_HEADS, s, HEAD_DIM), bf16)],
        compiler_params=_params("parallel"),
    )(z, z, z, c, *consts)


def _attn_do_prep(name, dob, sel):
    s = dob.shape[0]
    tq = _att_tiles(s)[0]

    def body(do_ref, sh_ref, sht_ref, dot_out, do_out):
        db = do_ref[...].astype(bf16)
        for h in range(N_HEADS):
            dot_out[h, 0] = lax.dot_general(sht_ref[h], db, _DN["nt"], preferred_element_type=f32).astype(bf16)
            do_out[h] = jnp.dot(db, sh_ref[h], preferred_element_type=f32).astype(bf16)

    whole = lambda a: pl.BlockSpec(a.shape, lambda i, nd=a.ndim: (0,) * nd)
    return pl.pallas_call(
        body,
        name=name,
        grid=(s // tq,),
        in_specs=[pl.BlockSpec((tq, D_B), lambda i: (i, 0)), whole(sel["sel_h"]), whole(sel["sel_ht"])],
        out_specs=[pl.BlockSpec((N_HEADS, 1, HEAD_DIM, tq), lambda i: (0, i, 0, 0)),
                   pl.BlockSpec((N_HEADS, tq, HEAD_DIM), lambda i: (0, i, 0))],
        out_shape=[_sds((N_HEADS, s // tq, HEAD_DIM, tq), bf16), _sds((N_HEADS, s, HEAD_DIM), bf16)],
        compiler_params=_params("parallel"),
    )(dob, sel["sel_h"], sel["sel_ht"])


def _dz_assemble(name, dax, dag, dqt, dkh, dvh, df, dcu, sel):
    s = dax.shape[0]
    tm = _tile(s, 512)

    def body(dax_ref, dag_ref, dqt_ref, dk_ref, dv_ref, df_ref, dcu_ref, sht_ref, o_ref):
        dq = jnp.zeros((tm, D_B), f32)
        dk = jnp.zeros((tm, D_B), f32)
        dv = jnp.zeros((tm, D_B), f32)
        for h in range(N_HEADS):
            place = sht_ref[h]
            dq = dq + lax.dot_general(dqt_ref[h].astype(bf16), place, _DN["tn"], preferred_element_type=f32)
            dk = dk + jnp.dot(dk_ref[h].astype(bf16), place, preferred_element_type=f32)
            dv = dv + jnp.dot(dv_ref[h].astype(bf16), place, preferred_element_type=f32)
        pieces = [dax_ref[...], dag_ref[...], dq, dk, dv, df_ref[...], dcu_ref[...]]
        off = 0
        for p in pieces:
            o_ref[:, off:off + p.shape[1]] = p.astype(bf16)
            off += p.shape[1]

    rows = lambda c_: pl.BlockSpec((tm, c_), lambda i: (i, 0))
    heads = pl.BlockSpec((N_HEADS, tm, HEAD_DIM), lambda i: (0, i, 0))
    return pl.pallas_call(
        body,
        name=name,
        grid=(s // tm,),
        in_specs=[rows(D_A), rows(D_A), pl.BlockSpec((N_HEADS, HEAD_DIM, tm), lambda i: (0, 0, i)), heads, heads,
                  rows(128), rows(D_C), pl.BlockSpec(sel["sel_ht"].shape, lambda i: (0, 0, 0))],
        out_specs=rows(N_IN_P),
        out_shape=_sds((s, N_IN_P), bf16),
        compiler_params=_params("parallel"),
    )(dax, dag, dqt, dkh, dvh, df, dcu, sel["sel_ht"])


def _s5_disc_fn(are, aim, ldt):
    dt = jnp.exp(ldt)
    er = jnp.exp(are * dt)
    br = er * jnp.cos(aim * dt)
    bi = er * jnp.sin(aim * dt)
    nr = br - 1.0
    den = are * are + aim * aim
    return br, bi, (nr * are + bi * aim) / den, (bi * are - nr * aim) / den


def _s5_disc(name, are, aim, ldt):
    def body(a_ref, b_ref, c_ref, o0, o1, o2, o3):
        r = _s5_disc_fn(a_ref[...], b_ref[...], c_ref[...])
        o0[...], o1[...], o2[...], o3[...] = r

    shp = _sds((S5_GROUPS, S5_STATE))
    return pl.pallas_call(body, name=name, out_shape=[shp] * 4)(are, aim, ldt)


def _s5_disc_bwd(name, are, aim, ldt, cts):
    def body(a_ref, b_ref, c_ref, d0, d1, d2, d3, o0, o1, o2):
        _, vjp = jax.vjp(_s5_disc_fn, a_ref[...], b_ref[...], c_ref[...])
        o0[...], o1[...], o2[...] = vjp((d0[...], d1[...], d2[...], d3[...]))

    shp = _sds((S5_GROUPS, S5_STATE))
    return pl.pallas_call(body, name=name, out_shape=[shp, shp, _sds((S5_GROUPS, 1))])(are, aim, ldt, *cts)


def _adamw_rows(w, g, m, v):
    m = ADAM_B1 * m + (1.0 - ADAM_B1) * g
    v = ADAM_B2 * v + (1.0 - ADAM_B2) * (g * g)
    m_hat = m / (1.0 - ADAM_B1 ** ADAM_STEP)
    v_hat = v / (1.0 - ADAM_B2 ** ADAM_STEP)
    return -ADAM_LR * (m_hat / (jnp.sqrt(v_hat) + ADAM_EPS) + ADAM_WD * w), m, v


def _adamw(name, w, ga, gb, m, v):
    rows, cols = w.shape
    tr = _row_tile(rows)

    def body(w_ref, ga_ref, gb_ref, m_ref, v_ref, g_out, d_out, m_out, v_out):
        g = ga_ref[...] + gb_ref[...]
        d, mm, vv = _adamw_rows(w_ref[...], g, m_ref[...], v_ref[...])
        g_out[...] = g
        d_out[...] = d
        m_out[...] = mm
        v_out[...] = vv

    spec = pl.BlockSpec((tr, cols), lambda i: (i, 0))
    return pl.pallas_call(
        body, name=name, grid=(rows // tr,), in_specs=[spec] * 5, out_specs=[spec] * 4,
        out_shape=[_sds((rows, cols))] * 4, compiler_params=_params("parallel"),
    )(w, ga, gb, m, v)


def _sum_stack(name, st):
    n, rows, cols = st.shape
    tr = _row_tile(rows)

    def body(s_ref, o_ref):
        acc = s_ref[0].astype(f32)
        for j in range(1, n):
            acc = acc + s_ref[j].astype(f32)
        o_ref[...] = acc

    return pl.pallas_call(
        body, name=name, grid=(rows // tr,), in_specs=[pl.BlockSpec((n, tr, cols), lambda i: (0, i, 0))],
        out_specs=pl.BlockSpec((tr, cols), lambda i: (i, 0)), out_shape=_sds((rows, cols)),
        compiler_params=_params("parallel"),
    )(st)


def _block_diag(w):
    h, n, m = w.shape
    return jnp.einsum("hij,hg->higj", w, jnp.eye(h, dtype=w.dtype)).reshape(h * n, h * m)


def _block_diag_part(dense, h):
    n, m = dense.shape[0] // h, dense.shape[1] // h
    return jnp.einsum("higj,hg->hij", dense.reshape(h, n, h, m), jnp.eye(h, dtype=dense.dtype))


def _s5_matrices(coef_re, coef_im, b_re, b_im, c_re, c_im):
    bb_re = coef_re[:, :, None] * b_re - coef_im[:, :, None] * b_im
    bb_im = coef_re[:, :, None] * b_im + coef_im[:, :, None] * b_re
    wb_re = _block_diag(jnp.swapaxes(bb_re, 1, 2))
    wb_im = _block_diag(jnp.swapaxes(bb_im, 1, 2))
    wc_re = _block_diag(jnp.swapaxes(c_re, 1, 2))
    wc_im = _block_diag(jnp.swapaxes(-c_im, 1, 2))
    return wb_re, wb_im, wc_re, wc_im


def _shift_down(t):
    return jnp.concatenate([jnp.zeros((1, t.shape[1]), t.dtype), t[:-1]], axis=0)


def _shift_up(t):
    return jnp.concatenate([t[1:], jnp.zeros((1, t.shape[1]), t.dtype)], axis=0)


def _row(v):
    return v.reshape(1, -1)


def _ffn_fwd(tag, h, get, names, gamma, beta):
    wg, wu = get(names[0]), get(names[1])
    g, u, act = _ffn_up(tag + "_up", h, wg, wu)
    wd = get(names[2])
    r, out = _mm_ln(tag + "_down", act, wd, h, gamma, beta, 0.5, k_slabs=True)
    return out, dict(h=h, g=g, u=u, act=act, r=r, wg=wg, wu=wu, wd=wd)


def _ffn_bwd(tag, dout, sv, names, gamma, put, after_ln=None):
    s = dout.shape[0]
    dr, dgam, dbet = _ln_bwd(tag + "_lnb", sv["r"], dout, gamma)
    if after_ln is not None:
        after_ln(dgam, dbet)
    put(names[2], _mm_plain(tag + "_dwd", "tn", _Slabs(sv["act"]), dr, (D_FF, D_MODEL, s), scale=0.5, out_dtype=bf16,
                            tiles=(FF_SLAB, 1024, _tile(s, 2048))))
    dg, du = _ffn_dact(tag + "_dact", dr, sv["wd"], sv["g"], sv["u"])
    dwg, dwu = _mm2(tag + "_dwgu", "tn", (D_FF, D_MODEL, s), _Slabs(dg), sv["h"], _Slabs(du), None, separate=True,
                    out_dtype=bf16, tiles=(FF_SLAB, 1024, _tile(s, 2048)))
    put(names[0], dwg)
    put(names[1], dwu)
    slabs = range(dg.shape[0])
    dh = _mm(tag + "_dh", "nn", (s, D_MODEL, FF_SLAB), (_tile(s, 512), D_MODEL, FF_SLAB),
             [_KPart(dg, j) for j in slabs] + [_KPart(du, j) for j in slabs],
             [_KPart(sv["wg"], j) for j in slabs] + [_KPart(sv["wu"], j) for j in slabs],
             [(j, j, 0) for j in range(2 * len(slabs))], 1,
             lambda accs, extras, vecs: [accs[0] + ALPHA * extras[0]], [f32], extras=[dr])[0]
    return dh, dgam, dbet


def _mixer_fwd(tag, h1, w):
    s = h1.shape[0]
    z = _mm_plain(tag + "_win", "nn", h1, w["w_in"], (s, N_IN_P, D_MODEL), tiles=(_tile(s, 512), 768, D_MODEL))
    ag, f, cu_cols = (z, D_A, 1), (z, 128, F_OFF // 128), (z, D_C, CU_OFF // D_C)
    cu = z[:, CU_OFF:]
    xa = _conv_fwd(tag + "_conv", z, w["conv_w"], w["conv_b"])
    a, gated = _rg_gates(tag + "_gates", xa, w["rg_wa"], w["rg_wx"], w["rg_ba"], w["rg_bx"], w["rg_lam"])
    ha = _lin_scan(tag + "_rgscan", a, gated, False)
    ones = jnp.ones((s, 128), f32)
    c = _lin_scan(tag + "_cumf", ones, _log_f(tag + "_logf", f, w["fox_bf"]), False)
    att = dict(zip(("qt", "k_aug", "kt", "vt", "v", "qh"), _attn_prep(tag + "_attnprep", z, c, w["sel"])))
    ot, lse = _attn_fwd_t(tag + "_attn", att["qt"], att["k_aug"], att["vt"])
    ob = ot.reshape(D_B, s).T
    bu_re, bu_im = _mm2(tag + "_s5in", "nn", (s, S5_LANES, D_C), cu, w["wb_re"], None, w["wb_im"], separate=True,
                        tiles=(_tile(s, 512), 1024, D_C))
    hre, him = _s5_scan(tag + "_s5scan", bu_re, bu_im, w["abar_re"], w["abar_im"], False)
    o = _mix_out(tag + "_mixout", ag, ha, ob, hre, him, cu_cols, w["s5_d"], w["mix_g"], w["wc_re"], w["wc_im"],
                 w["w_glu"])
    sv = dict(h1=h1, z=z, ag=ag, f=f, cu=cu, cu_cols=cu_cols, xa=xa, a=a, ha=ha, att=att, ot=ot, lse=lse, ob=ob,
              hre=hre, him=him, o=o)
    return o, sv


def _mixer_bwd(tag, do, dr2, sv, w, put):
    s = do.shape[0]
    (dag, dha, dob, dhre, dhim, dcu1, dwcr, dwci, dwglu, dd, dgn) = _mix_out_bwd(
        tag + "_mixoutb", do, sv["ag"], sv["ha"], sv["ob"], sv["hre"], sv["him"], sv["cu_cols"], w["s5_d"], w["mix_g"],
        w["wc_re"], w["wc_im"], w["w_glu"])
    put("s5_w_glu", dwglu.astype(bf16))
    gre, gim = _s5_scan(tag + "_s5scanb", dhre, dhim, w["abar_re"], -w["abar_im"], True)
    dab_re, dab_im = _s5_decay_grad(tag + "_s5dec", sv["hre"], sv["him"], gre, gim)
    dwb_re, dwb_im = _mm2(tag + "_s5dwb", "tn", (D_C, S5_LANES, s), sv["cu"], gre, None, gim, separate=True,
                          tiles=(D_C, 1024, _tile(s, 1024)))
    dcu = _mm2(tag + "_s5dcu", "nt", (s, D_C, S5_LANES), gre, w["wb_re"], gim, w["wb_im"], add=dcu1,
               tiles=(_tile(s, 512), D_C, 1024))[0]
    att = sv["att"]
    tq = _att_tiles(s)[0]
    nt = s // tq
    dot_blocks, doh = _attn_do_prep(tag + "_doprep", dob, w["sel"])
    dqt, delta = _attn_bwd_dq_t(tag + "_attndq", att["qt"], att["k_aug"], att["v"], att["kt"], sv["ot"], dot_blocks,
                                sv["lse"])
    dkh, dvh, dck = _attn_bwd_dkv_t(tag + "_attndkv", att["qt"], att["k_aug"], att["v"], att["qh"], doh, dot_blocks,
                                    sv["lse"].reshape(N_HEADS, nt, 1, tq), delta.reshape(N_HEADS, nt, 1, tq))
    dc = jnp.pad(dck.reshape(N_HEADS, s).T, ((0, 0), (0, 128 - N_HEADS)))
    dlf = _lin_scan(tag + "_cumfb", jnp.ones((s, 128), f32), dc, True)
    df, dbf = _log_f_bwd(tag + "_logfb", dlf, sv["f"], w["fox_bf"])
    ga = _lin_scan(tag + "_rgscanb", _shift_up(sv["a"]), dha, True)
    dxa, dwa, dwx, dba, dbx, dlam = _rg_gates_bwd(tag + "_gatesb", sv["xa"], ga, _shift_down(sv["ha"]), w["rg_wa"],
                                                  w["rg_wx"], w["rg_ba"], w["rg_bx"], w["rg_lam"])
    dax, dconv = _conv_bwd(tag + "_convb", dxa, sv["z"], w["conv_w"])
    dz = _dz_assemble(tag + "_dz", dax, dag, dqt, dkh, dvh, df, dcu, w["sel"])
    put("w_in", _mm_plain(tag + "_dwin", "tn", sv["h1"], dz, (D_MODEL, N_IN_P, s), out_dtype=bf16,
                          tiles=(512, 768, _tile(s, 1024))))
    dh1 = _mm_plain(tag + "_dh1", "nt", dz, w["w_in"], (s, D_MODEL, N_IN_P), add=dr2, add_coef=ALPHA,
                    tiles=(_tile(s, 512), 1024, 768))
    grads = dict(dconv=dconv, dwa=dwa, dwx=dwx, dba=dba, dbx=dbx, dlam=dlam, dbf=dbf,
                 dab_re=dab_re, dab_im=dab_im, dwb_re=dwb_re, dwb_im=dwb_im, dwcr=dwcr, dwci=dwci, dd=dd, dgn=dgn)
    return dh1, grads


SMALL_NAMES = ["ln1_g", "ln1_b", "conv_w", "conv_b", "rg_w_a", "rg_b_a", "rg_w_x", "rg_b_x", "rg_lambda", "fox_b_f",
               "s5_a_re", "s5_a_im", "s5_log_dt", "s5_b_re", "s5_b_im", "s5_c_re", "s5_c_im", "s5_d", "mix_norm_g",
               "ln2_g", "ln2_b", "ln3_g", "ln3_b"]
BIG_NAMES = ["ffn1_w_gate", "ffn1_w_up", "ffn1_w_down", "w_in", "s5_w_glu", "w_out", "ffn2_w_gate", "ffn2_w_up",
             "ffn2_w_down"]


def _local_step(x, target, weight, small, on_grads, on_small):
    h = x
    saved = []
    sel = _selections()
    for l in range(DEPTH):
        get = functools.partial(weight, l)

        sm = {n: small[n][l] for n in SMALL_NAMES}
        abar_re, abar_im, coef_re, coef_im = _s5_disc(f"l{l}_s5disc", sm["s5_a_re"], sm["s5_a_im"],
                                                      sm["s5_log_dt"].reshape(S5_GROUPS, 1))
        mats, mats_vjp = jax.vjp(_s5_matrices, coef_re, coef_im, sm["s5_b_re"], sm["s5_b_im"], sm["s5_c_re"],
                                 sm["s5_c_im"])
        w = dict(
            sel=sel, conv_w=sm["conv_w"], conv_b=_row(sm["conv_b"]),
            rg_wa=_block_diag(sm["rg_w_a"]).astype(bf16), rg_wx=_block_diag(sm["rg_w_x"]).astype(bf16),
            rg_ba=_row(sm["rg_b_a"]), rg_bx=_row(sm["rg_b_x"]), rg_lam=_row(sm["rg_lambda"]),
            fox_bf=jnp.pad(_row(sm["fox_b_f"]), ((0, 0), (0, 128 - N_HEADS))),
            abar_re=_row(abar_re), abar_im=_row(abar_im),
            wb_re=mats[0].astype(bf16), wb_im=mats[1].astype(bf16), wc_re=mats[2].astype(bf16),
            wc_im=mats[3].astype(bf16), s5_d=_row(sm["s5_d"]), mix_g=_row(sm["mix_norm_g"]))
        h1, sv1 = _ffn_fwd(f"l{l}_ffn1", h, get, GROUPS["F1"], _row(sm["ln1_g"]), _row(sm["ln1_b"]))
        w["w_in"], w["w_glu"] = get("w_in"), get("s5_w_glu")
        o, svm = _mixer_fwd(f"l{l}_mix", h1, w)
        w_out = get("w_out")
        r2, h2 = _mm_ln(f"l{l}_wout", o, w_out, h1, _row(sm["ln2_g"]), _row(sm["ln2_b"]), 1.0)
        h3, sv2 = _ffn_fwd(f"l{l}_ffn2", h2, get, GROUPS["F2"], _row(sm["ln3_g"]), _row(sm["ln3_b"]))
        saved.append(dict(sm=sm, w=w, w_out=w_out, sv1=sv1, svm=svm, r2=r2, sv2=sv2, mats_vjp=mats_vjp))
        h = h3

    dh, loss_row = _loss_head("loss_head", h, target)
    s = x.shape[0]
    gsmall = {n: [None] * DEPTH for n in SMALL_NAMES}
    for l in reversed(range(DEPTH)):
        sd = saved[l]
        sm, w = sd["sm"], sd["w"]

        def put(name, grad, l=l):
            on_grads((l, name), grad)

        dh2, dgam, dbet = _ffn_bwd(f"l{l}_ffn2", dh, sd["sv2"], GROUPS["F2"], _row(sm["ln3_g"]), put)
        gsmall["ln3_g"][l], gsmall["ln3_b"][l] = dgam[0], dbet[0]
        dr2, dgam, dbet = _ln_bwd(f"l{l}_ln2b", sd["r2"], dh2, _row(sm["ln2_g"]))
        gsmall["ln2_g"][l], gsmall["ln2_b"][l] = dgam[0], dbet[0]
        put("w_out", _mm_plain(f"l{l}_dwout", "tn", sd["svm"]["o"], dr2, (D_MODEL, D_MODEL, s), out_dtype=bf16))
        do = _mm_plain(f"l{l}_do", "nt", dr2, sd["w_out"], (s, D_MODEL, D_MODEL))
        dh1, g = _mixer_bwd(f"l{l}_mix", do, dr2, sd["svm"], w, put)
        gsmall["conv_w"][l], gsmall["conv_b"][l] = g["dconv"][:CONV_WIDTH], g["dconv"][CONV_WIDTH]
        gsmall["rg_w_a"][l] = _block_diag_part(g["dwa"], N_HEADS)
        gsmall["rg_w_x"][l] = _block_diag_part(g["dwx"], N_HEADS)
        gsmall["rg_b_a"][l], gsmall["rg_b_x"][l], gsmall["rg_lambda"][l] = g["dba"][0], g["dbx"][0], g["dlam"][0]
        gsmall["fox_b_f"][l] = g["dbf"][0, :N_HEADS]
        dcoef_re, dcoef_im, db_re, db_im, dc_re, dc_im = sd["mats_vjp"]((g["dwb_re"], g["dwb_im"], g["dwcr"], g["dwci"]))
        da_re, da_im, dldt = _s5_disc_bwd(
            f"l{l}_s5discb", sm["s5_a_re"], sm["s5_a_im"], sm["s5_log_dt"].reshape(S5_GROUPS, 1),
            (g["dab_re"].reshape(S5_GROUPS, S5_STATE), g["dab_im"].reshape(S5_GROUPS, S5_STATE), dcoef_re, dcoef_im))
        gsmall["s5_a_re"][l], gsmall["s5_a_im"][l], gsmall["s5_log_dt"][l] = da_re, da_im, dldt[:, 0]
        gsmall["s5_b_re"][l], gsmall["s5_b_im"][l], gsmall["s5_c_re"][l], gsmall["s5_c_im"][l] = db_re, db_im, dc_re, dc_im
        gsmall["s5_d"][l], gsmall["mix_norm_g"][l] = g["dd"][0], g["dgn"][0]

        def after_ln(dgam, dbet, l=l):
            gsmall["ln1_g"][l], gsmall["ln1_b"][l] = dgam[0], dbet[0]
            if l == 0:
                on_small({n: jnp.stack(v) for n, v in gsmall.items()})

        dh, _, _ = _ffn_bwd(f"l{l}_ffn1", dh1, sd["sv1"], GROUPS["F1"], _row(sm["ln1_g"]), put, after_ln)
    return loss_row[0, 0], dh


def _position():
    return lax.axis_index("x"), lax.axis_index("y"), lax.axis_index("c")


_ANY = pl.BlockSpec(memory_space=pl.ANY)


COLUMN_SHARDED = ("ffn1_w_gate", "ffn1_w_up", "ffn2_w_gate", "ffn2_w_up")
PACK_QUANTUM = 128 * 256


def _permute_in_cols(w):
    pad = jnp.zeros(w.shape[:-1] + (128 - N_HEADS,), w.dtype)
    return jnp.concatenate([w[..., :F_OFF + N_HEADS], pad, w[..., F_OFF + N_HEADS:]], axis=-1)


def _unpermute_in_cols(w):
    return jnp.concatenate([w[..., :F_OFF + N_HEADS], w[..., CU_OFF:]], axis=-1)


def _pack(arrs):
    flat = jnp.concatenate([a.reshape(-1) for a in arrs])
    pad = -flat.shape[0] % PACK_QUANTUM
    return jnp.pad(flat, (0, pad)).reshape(-1, 128)


def _unpack(buf, shapes):
    flat = buf.reshape(-1)
    out, off = [], 0
    for shp in shapes:
        size = math.prod(shp)
        out.append(flat[off:off + size].reshape(shp))
        off += size
    return out


WEIGHT_NAMES = ["ffn1_w_gate", "ffn1_w_up", "ffn1_w_down", "ln1_g", "ln1_b", "w_in", "conv_w", "conv_b", "rg_w_a",
                "rg_b_a", "rg_w_x", "rg_b_x", "rg_lambda", "fox_b_f", "s5_a_re", "s5_a_im", "s5_log_dt", "s5_b_re",
                "s5_b_im", "s5_c_re", "s5_c_im", "s5_d", "s5_w_glu", "mix_norm_g", "w_out", "ln2_g", "ln2_b",
                "ffn2_w_gate", "ffn2_w_up", "ffn2_w_down", "ln3_g", "ln3_b"]


def _remote(src, dst, send_sems, recv_sems, k, peer):
    return pltpu.make_async_remote_copy(src_ref=src, dst_ref=dst, send_sem=send_sems.at[k], recv_sem=recv_sems.at[k],
                                        device_id=peer, device_id_type=MESH)


class _ChipGatherPart:
    def __init__(self, arrays):
        self.arrays, self.results = list(arrays), None

    def out_shape(self):
        return [_sds((N_CHIPS,) + a.shape, a.dtype) for a in self.arrays]

    def sems(self):
        n = len(self.arrays)
        return [pltpu.SemaphoreType.DMA((3 * n,)), pltpu.SemaphoreType.DMA((3 * n,)), pltpu.SemaphoreType.DMA((n,))]

    def copies(self, ins, outs, sems):
        send_sems, recv_sems, local_sems = sems
        x, y, c = _position()
        me = 2 * x + y
        local, sends, recvs = [], [], []
        for i, (src, dst) in enumerate(zip(ins, outs)):
            local.append(pltpu.make_async_copy(self.mine(src, me), dst.at[me], local_sems.at[i]))
            for r, (px, py) in enumerate([(1 - x, y), (x, 1 - y), (1 - x, 1 - y)]):
                peer = 2 * px + py
                sends.append(_remote(self.theirs(src, peer), dst.at[me], send_sems, recv_sems, 3 * i + r, (px, py, c)))
                recvs.append(_remote(self.mine(src, me), dst.at[peer], send_sems, recv_sems, 3 * i + r, (px, py, c)))
        return local, sends, recvs

    def mine(self, src, me):
        return src

    def theirs(self, src, peer):
        return src


class _ChipGatherHalvesPart(_ChipGatherPart):
    def sems(self):
        n = len(self.arrays)
        return super().sems() + [pltpu.SemaphoreType.DMA((3 * n,)), pltpu.SemaphoreType.DMA((3 * n,))]

    def _half(self, ref, which):
        rows = ref.shape[0] // 2
        return ref.at[pl.ds(which * rows, rows)]

    def copies(self, ins, outs, sems):
        send_sems, recv_sems, local_sems = sems[:3]
        x, y, c = _position()
        me = 2 * x + y
        local, sends, recvs = [], [], []
        for i, (src, dst) in enumerate(zip(ins, outs)):
            local.append(pltpu.make_async_copy(src, dst.at[me], local_sems.at[i]))
            for r, (px, py) in enumerate([(1 - x, y), (x, 1 - y), (1 - x, 1 - y)]):
                sends.append(_remote(self._half(src, c), self._half(dst.at[me], c), send_sems, recv_sems, 3 * i + r,
                                     (px, py, c)))
                recvs.append(_remote(self._half(src, c), self._half(dst.at[2 * px + py], c), send_sems, recv_sems,
                                     3 * i + r, (px, py, c)))
        return local, sends, recvs

    def forwards(self, ins, outs, sems):
        send_sems, recv_sems = sems[3:]
        x, y, c = _position()
        sends, recvs = [], []
        for i, dst in enumerate(outs):
            for r, (px, py) in enumerate([(1 - x, y), (x, 1 - y), (1 - x, 1 - y)]):
                slot = dst.at[2 * px + py]
                sends.append(_remote(self._half(slot, c), self._half(slot, c), send_sems, recv_sems, 3 * i + r,
                                     (x, y, 1 - c)))
                recvs.append(_remote(self._half(slot, c), self._half(slot, 1 - c), send_sems, recv_sems, 3 * i + r,
                                     (x, y, 1 - c)))
        return sends, recvs


class _ChipScatterPart(_ChipGatherPart):
    def out_shape(self):
        return [_sds(a.shape, a.dtype) for a in self.arrays]

    def mine(self, src, me):
        return src.at[me]

    def theirs(self, src, peer):
        return src.at[peer]


class _SiblingSwapPart:
    def __init__(self, arrays):
        self.arrays, self.results = list(arrays), None

    def out_shape(self):
        return [_sds(a.shape, a.dtype) for a in self.arrays]

    def sems(self):
        n = len(self.arrays)
        return [pltpu.SemaphoreType.DMA((n,)), pltpu.SemaphoreType.DMA((n,))]

    def copies(self, ins, outs, sems):
        x, y, c = _position()
        both = [_remote(src, dst, sems[0], sems[1], i, (x, y, 1 - c)) for i, (src, dst) in enumerate(zip(ins, outs))]
        return [], both, both


def _split_by(parts, refs, count):
    out, off = [], 0
    for p in parts:
        out.append(refs[off:off + count(p)])
        off += count(p)
    return out


def _parts_refs(parts, in_refs, out_refs, sem_refs):
    return zip(parts, _split_by(parts, in_refs, lambda p: len(p.arrays)),
               _split_by(parts, out_refs, lambda p: len(p.arrays)), _split_by(parts, sem_refs, lambda p: len(p.sems())))


def _exchange_start(parts, in_refs, out_refs, sem_refs):
    for part, ins, outs, sems in _parts_refs(parts, in_refs, out_refs, sem_refs):
        local, sends, _ = part.copies(ins, outs, sems)
        for cp in local + sends:
            cp.start()


def _exchange_finish(parts, in_refs, out_refs, sem_refs):
    split = list(_parts_refs(parts, in_refs, out_refs, sem_refs))
    copies = [part.copies(ins, outs, sems) for part, ins, outs, sems in split]
    for _, _, recvs in copies:
        for cp in recvs:
            cp.wait_recv()
    second = [part.forwards(ins, outs, sems) for part, ins, outs, sems in split if hasattr(part, "forwards")]
    for sends, _ in second:
        for cp in sends:
            cp.start()
    for sends, recvs in second:
        for cp in recvs:
            cp.wait_recv()
        for cp in sends:
            cp.wait_send()
    for local, sends, _ in copies:
        for cp in sends:
            cp.wait_send()
        for cp in local:
            cp.wait()


def _exchange_operands(parts):
    return ([a for p in parts for a in p.arrays], [s for p in parts for s in p.out_shape()],
            [s for p in parts for s in p.sems()])


def _set_results(parts, res):
    for part, outs in zip(parts, _split_by(parts, list(res), lambda p: len(p.arrays))):
        part.results = list(outs)


def _exchange_now(name, parts):
    x_in, x_out, x_sem = _exchange_operands(parts)
    n = len(x_in)

    def body(*refs):
        _exchange_start(parts, refs[:n], refs[n:2 * n], refs[2 * n:])
        _exchange_finish(parts, refs[:n], refs[n:2 * n], refs[2 * n:])

    res = pl.pallas_call(body, name=name, in_specs=[_ANY] * n, out_specs=[_ANY] * n, out_shape=x_out,
                         scratch_shapes=x_sem)(*x_in)
    _set_results(parts, res)


_RIDERS = {}


def _call(body, *, name, grid, in_specs, out_specs, out_shape, scratch_shapes=(), compiler_params=None):
    make_parts = _RIDERS.pop(name, None)
    if make_parts is None:
        return pl.pallas_call(body, name=name, grid=grid, in_specs=in_specs, out_specs=out_specs, out_shape=out_shape,
                              scratch_shapes=scratch_shapes, compiler_params=compiler_params)
    parts = make_parts()
    x_in, x_out, x_sem = _exchange_operands(parts)
    n_out, n_scr, n_x = len(out_shape), len(scratch_shapes), len(x_in)

    def run(*args):
        n_in = len(args)

        def hosted(*refs):
            ins, xi = refs[:n_in], refs[n_in:n_in + n_x]
            outs, xo = refs[n_in + n_x:n_in + n_x + n_out], refs[n_in + n_x + n_out:n_in + 2 * n_x + n_out]
            scr, xs = refs[n_in + 2 * n_x + n_out:n_in + 2 * n_x + n_out + n_scr], refs[n_in + 2 * n_x + n_out + n_scr:]
            first = functools.reduce(jnp.logical_and, [pl.program_id(d) == 0 for d in range(len(grid))])
            last = functools.reduce(jnp.logical_and, [pl.program_id(d) == grid[d] - 1 for d in range(len(grid))])

            @pl.when(first)
            def _():
                _exchange_start(parts, xi, xo, xs)

            body(*ins, *outs, *scr)

            @pl.when(last)
            def _():
                _exchange_finish(parts, xi, xo, xs)

        res = pl.pallas_call(
            hosted, name=name, grid=grid, in_specs=list(in_specs) + [_ANY] * n_x,
            out_specs=list(out_specs) + [_ANY] * n_x, out_shape=list(out_shape) + x_out,
            scratch_shapes=list(scratch_shapes) + x_sem, compiler_params=_params(*["arbitrary"] * len(grid)),
        )(*args, *x_in)
        _set_results(parts, res[n_out:])
        return list(res[:n_out])

    return run


GROUPS = {"F1": ["ffn1_w_gate", "ffn1_w_up", "ffn1_w_down"], "MX": ["w_in", "s5_w_glu", "w_out"],
          "F2": ["ffn2_w_gate", "ffn2_w_up", "ffn2_w_down"]}
FIRST_GATHER = [(0, "ffn1_w_gate"), (0, "ffn1_w_up")]
GATHER_HOSTS = {
    "l0_ffn1_up": [(0, "ffn1_w_down")],
    "l0_ffn1_down": [(0, "w_in"), (0, "s5_w_glu"), (0, "w_out")],
    "l0_mix_attn": [(0, "ffn2_w_up"), (0, "ffn2_w_down"), (1, "w_in"), (1, "s5_w_glu"), (1, "w_out")],
    "l0_mix_s5scan": [(0, "ffn2_w_gate")],
    "l0_ffn2_up": [(1, "ffn1_w_gate")],
    "l0_ffn2_down": [(1, "ffn1_w_up")],
    "l1_ffn1_up": [(1, "ffn1_w_down")],
    "l1_mix_attn": [(1, "ffn2_w_up"), (1, "ffn2_w_down")],
    "l1_mix_s5scan": [(1, "ffn2_w_gate")],
}
SCATTER_HOSTS = {
    "l1_ffn2_dact": [(1, "ffn2_w_down")],
    "l1_mix_attndq": [(1, "ffn2_w_up")],
    "l1_mix_attndkv": [(1, "w_out"), (1, "s5_w_glu"), (1, "ffn2_w_gate")],
    "l1_ffn1_dwd": [(1, "w_in")],
    "l1_ffn1_dact": [(1, "ffn1_w_down")],
    "l0_ffn2_dact": [(1, "ffn1_w_up")],
    "l0_ffn2_dwgu": [(0, "ffn2_w_down")],
    "l0_mix_attndq": [(0, "w_out"), (0, "s5_w_glu"), (0, "ffn2_w_up")],
    "l0_mix_attndkv": [(0, "ffn2_w_gate"), (1, "ffn1_w_gate")],
    "l0_mix_dh1": [(0, "w_in")],
    "l0_ffn1_dact": [(0, "ffn1_w_down")],
    "l0_ffn1_dh": [(0, "ffn1_w_gate")],
}
LAST_SCATTER = [(0, "ffn1_w_up")]
SMALL_HOST = "l0_ffn1_dwd"
SMALL_PACK_ORDER = [n for n in SMALL_NAMES if n != "conv_w"] + ["conv_w"]
TAIL_HOST = "l0_ffn1_dwgu"
LATE_SCATTER_HOST = "l0_ffn1_dh"


def _sharded_rows(name, a):
    return jnp.swapaxes(a, 1, 2) if name in COLUMN_SHARDED else a


def _unstack_layer(st):
    _, r, c = st.shape
    return st.reshape(N_CHIPS * r, c)


def _restack_layer(g):
    r, c = g.shape
    return g.reshape(N_CHIPS, r // N_CHIPS, c)


def _adamw_layer(name, layer, w, ga, gb, m, v, bufs):
    _, r, c = w.shape
    tr = _row_tile(r)

    def body(w_ref, ga_ref, gb_ref, m_ref, v_ref, *rest):
        g_out, d_out, m_out, v_out = rest[-4:]
        g = ga_ref[...] + gb_ref[...]
        d, mm, vv = _adamw_rows(w_ref[...], g, m_ref[...], v_ref[...])
        g_out[...] = g
        d_out[...] = d
        m_out[...] = mm
        v_out[...] = vv

    full = pl.BlockSpec((None, tr, c), lambda i: (layer, i, 0))
    flat = pl.BlockSpec((tr, c), lambda i: (i, 0))
    extra = {} if bufs is None else dict(input_output_aliases={5 + k: k for k in range(4)})
    return pl.pallas_call(
        body, name=name, grid=(r // tr,),
        in_specs=[full, flat, flat, full, full] + ([] if bufs is None else [_ANY] * 4),
        out_specs=[full] * 4, out_shape=[_sds(w.shape)] * 4, compiler_params=_params("parallel"), **extra,
    )(w, ga, gb, m, v, *([] if bufs is None else bufs))


def _train_step(x, loss_target, w, m, v):
    ix, iy, _ = _position()
    chip = 2 * ix + iy
    shard = {n: (_permute_in_cols(w[n]) if n == "w_in" else _sharded_rows(n, w[n])).astype(bf16) for n in BIG_NAMES}

    gathered = {}

    def gather_parts(keys, extra=()):
        part = _ChipGatherHalvesPart([shard[n][layer] for layer, n in keys] + list(extra))
        gathered.update({key: (part, i) for i, key in enumerate(keys)})
        return [part]

    (first,) = gather_parts(FIRST_GATHER, extra=[w["conv_w"]])
    _exchange_now("gather_first", [first])
    for host, keys in GATHER_HOSTS.items():
        _RIDERS[host] = functools.partial(gather_parts, keys)

    def weight(layer, name):
        part, i = gathered[(layer, name)]
        return _unstack_layer(part.results[i])

    small = {n: w[n] for n in SMALL_NAMES}
    small["conv_w"] = first.results[-1].transpose(1, 2, 0, 3).reshape(DEPTH, CONV_WIDTH, D_A)

    grads_full, scattered = {}, {}

    def scatter_parts(keys):
        part = _ChipScatterPart([_restack_layer(grads_full[key]) for key in keys])
        scattered.update({key: (part, i) for i, key in enumerate(keys)})
        return [part]

    for host, keys in SCATTER_HOSTS.items():
        _RIDERS[host] = functools.partial(scatter_parts, keys)

    partial = {}

    def reduce_chips(keys):
        for layer, n in keys:
            part, i = scattered[(layer, n)]
            p = _sum_stack(f"sum_l{layer}_{n}", part.results[i])
            partial[(layer, n)] = _unpermute_in_cols(p) if n == "w_in" else p

    early = [key for host, keys in SCATTER_HOSTS.items() if host != LATE_SCATTER_HOST for key in keys]
    late = SCATTER_HOSTS[LATE_SCATTER_HOST]
    tail = {}

    def small_parts():
        tail["small"] = _ChipGatherPart([_pack([tail["gsmall"][n] for n in SMALL_PACK_ORDER])])
        return [tail["small"]]

    def tail_parts():
        reduce_chips(early)
        tail["small_sum"] = _sum_stack("sum_small", tail["small"].results[0])
        tail["swap"] = _SiblingSwapPart([partial[k] for k in early] + [tail["small_sum"]])
        return [tail["swap"]]

    _RIDERS[SMALL_HOST] = small_parts
    _RIDERS[TAIL_HOST] = tail_parts
    loss_local, gx = _local_step(x[0], loss_target[0], weight, small, grads_full.__setitem__,
                                 functools.partial(tail.__setitem__, "gsmall"))
    other = dict(zip(early, tail["swap"].results[:-1]))
    small_mine, small_other = tail["small_sum"], tail["swap"].results[-1]
    reduce_chips(late)
    last_parts = scatter_parts(LAST_SCATTER) + [_SiblingSwapPart([partial[k] for k in late])]
    _exchange_now("exchange_last", last_parts)
    other.update(zip(late, last_parts[1].results))
    reduce_chips(LAST_SCATTER)
    swap_last = _SiblingSwapPart([partial[k] for k in LAST_SCATTER])
    _exchange_now("swap_last", [swap_last])
    other.update(zip(LAST_SCATTER, swap_last.results))

    grads, deltas, new_m, new_v = {}, {}, {}, {}
    for n in BIG_NAMES:
        bufs = None
        wr, mr, vr = (_sharded_rows(n, t) for t in (w[n], m[n], v[n]))
        for layer in range(DEPTH):
            bufs = _adamw_layer(f"adamw_l{layer}_{n}", layer, wr, partial[(layer, n)], other[(layer, n)], mr, vr, bufs)
        grads[n], deltas[n], new_m[n], new_v[n] = (_sharded_rows(n, t) for t in bufs)
    packed = SMALL_PACK_ORDER[:-1]
    shapes = [w[n].shape for n in packed]
    res = _adamw("adamw_small", _pack([w[n] for n in packed]), small_mine, small_other,
                 _pack([m[n] for n in packed]), _pack([v[n] for n in packed]))
    for dst, buf in zip((grads, deltas, new_m, new_v), res):
        dst.update(zip(packed, _unpack(buf, shapes)))
    cw = D_A // N_CHIPS
    conv_shape = (DEPTH, CONV_WIDTH, D_A)
    offset = sum(math.prod(s_) for s_ in shapes)

    def conv_grad(buf):
        full = buf.reshape(-1)[offset:offset + math.prod(conv_shape)].reshape(conv_shape)
        return lax.dynamic_slice_in_dim(full, chip * cw, cw, axis=2).reshape(DEPTH * CONV_WIDTH, cw)

    rows = lambda t: t.reshape(DEPTH * CONV_WIDTH, cw)
    res = _adamw("adamw_conv_w", rows(w["conv_w"]), conv_grad(small_mine), conv_grad(small_other),
                 rows(m["conv_w"]), rows(v["conv_w"]))
    for dst, buf in zip((grads, deltas, new_m, new_v), res):
        dst["conv_w"] = buf.reshape(w["conv_w"].shape)

    loss = lax.psum(loss_local, ("x", "y", "c"))
    return (loss, gx[None], *[grads[n] for n in WEIGHT_NAMES], *[deltas[n] for n in WEIGHT_NAMES],
            *[new_m[n] for n in WEIGHT_NAMES], *[new_v[n] for n in WEIGHT_NAMES])


def kernel(x, ffn1_w_gate, ffn1_w_up, ffn1_w_down, ln1_g, ln1_b, w_in, conv_w, conv_b, rg_w_a, rg_b_a, rg_w_x, rg_b_x, rg_lambda, fox_b_f, s5_a_re, s5_a_im, s5_log_dt, s5_b_re, s5_b_im, s5_c_re, s5_c_im, s5_d, s5_w_glu, mix_norm_g, w_out, ln2_g, ln2_b, ffn2_w_gate, ffn2_w_up, ffn2_w_down, ln3_g, ln3_b, loss_target, m_ffn1_w_gate, m_ffn1_w_up, m_ffn1_w_down, m_ln1_g, m_ln1_b, m_w_in, m_conv_w, m_conv_b, m_rg_w_a, m_rg_b_a, m_rg_w_x, m_rg_b_x, m_rg_lambda, m_fox_b_f, m_s5_a_re, m_s5_a_im, m_s5_log_dt, m_s5_b_re, m_s5_b_im, m_s5_c_re, m_s5_c_im, m_s5_d, m_s5_w_glu, m_mix_norm_g, m_w_out, m_ln2_g, m_ln2_b, m_ffn2_w_gate, m_ffn2_w_up, m_ffn2_w_down, m_ln3_g, m_ln3_b, v_ffn1_w_gate, v_ffn1_w_up, v_ffn1_w_down, v_ln1_g, v_ln1_b, v_w_in, v_conv_w, v_conv_b, v_rg_w_a, v_rg_b_a, v_rg_w_x, v_rg_b_x, v_rg_lambda, v_fox_b_f, v_s5_a_re, v_s5_a_im, v_s5_log_dt, v_s5_b_re, v_s5_b_im, v_s5_c_re, v_s5_c_im, v_s5_d, v_s5_w_glu, v_mix_norm_g, v_w_out, v_ln2_g, v_ln2_b, v_ffn2_w_gate, v_ffn2_w_up, v_ffn2_w_down, v_ln3_g, v_ln3_b):
    args = dict(locals())
    w = {n: args[n] for n in WEIGHT_NAMES}
    m = {n: args["m_" + n] for n in WEIGHT_NAMES}
    v = {n: args["v_" + n] for n in WEIGHT_NAMES}
    return _train_step(x, loss_target, w, m, v)
```

```python
import functools
import math

import jax
import jax.numpy as jnp
from jax import lax
from jax.experimental import pallas as pl
from jax.experimental.pallas import tpu as pltpu

f32 = jnp.float32
bf16 = jnp.bfloat16

D_MODEL = 1024
D_FF = 2816
D_A = 384
D_B = 384
D_C = 256
N_HEADS = 6
HEAD_DIM = 64
S5_GROUPS = 16
S5_GROUP = 16
S5_STATE = 64
S5_LANES = S5_GROUPS * S5_STATE
N_IN = 2 * D_A + 3 * D_B + N_HEADS + D_C
F_OFF = 5 * D_A
CU_OFF = F_OFF + 128
N_IN_P = CU_OFF + D_C
CONV_WIDTH = 4
DEPTH = 2
ALPHA = (2 * DEPTH) ** 0.25
LN_EPS = 1e-5
RMS_EPS = 1e-6
RG_C = 8.0
ATT_SCALE = HEAD_DIM ** -0.5
ADAM_LR, ADAM_B1, ADAM_B2, ADAM_EPS, ADAM_WD, ADAM_STEP = 0.001, 0.9, 0.999, 1e-08, 0.01, 10

ROW_TILE = 512
N_CHIPS = 4
N_DEV = 8
MESH = pl.DeviceIdType.MESH

_DN = {
    "nn": (((1,), (0,)), ((), ())),
    "nt": (((1,), (1,)), ((), ())),
    "tn": (((0,), (0,)), ((), ())),
}


def _sds(shape, dtype=f32):
    return jax.ShapeDtypeStruct(shape, dtype)


def _tile(n, target):
    best = None
    for t in range(128, min(n, target) + 1, 128):
        if n % t == 0:
            best = t
    return best or n


def _row_tile(rows, target=256):
    best = None
    for t in range(16, min(rows, target) + 1, 16):
        if rows % t == 0:
            best = t
    return best or rows


def _params(*sem):
    return pltpu.CompilerParams(dimension_semantics=sem)


class _Slabs:
    def __init__(self, x):
        self.x = x


class _KPart:
    def __init__(self, x, j):
        self.x, self.j = x, j


FF_SLAB = D_FF // 4
FFN_ROWS = 1024

def _mm(name, mode, dims, tiles, a_list, b_list, pairs, n_acc, epilogue, outs, extras=(), vecs=(), split_cols=False):
    m, n, k = dims
    tm, tn, tk = tiles
    nk = k // tk
    na, nb, ne, nv, no = len(a_list), len(b_list), len(extras), len(vecs), len(outs)

    def body(*refs):
        a_refs = refs[:na]
        b_refs = refs[na:na + nb]
        e_refs = refs[na + nb:na + nb + ne]
        v_refs = refs[na + nb + ne:na + nb + ne + nv]
        o_refs = refs[na + nb + ne + nv:na + nb + ne + nv + no]
        acc_refs = refs[na + nb + ne + nv + no:]
        a_vals = [r[...].astype(bf16) for r in a_refs]
        b_vals = [r[...].astype(bf16) for r in b_refs]
        products = [(ci, lax.dot_general(a_vals[ai], b_vals[bi], _DN[mode], preferred_element_type=f32))
                    for ai, bi, ci in pairs]

        def finish(accs):
            res = epilogue(accs, [e[...] for e in e_refs], [v[...] for v in v_refs])
            for o, r in zip(o_refs, res):
                o[...] = r.astype(o.dtype)

        if nk == 1:
            accs = [None] * n_acc
            for ci, prod in products:
                accs[ci] = prod if accs[ci] is None else accs[ci] + prod
            finish(accs)
            return
        kk = pl.program_id(2)

        @pl.when(kk == 0)
        def _():
            for acc in acc_refs:
                acc[...] = jnp.zeros_like(acc)

        for ci, prod in products:
            acc_refs[ci][...] += prod

        @pl.when(kk == nk - 1)
        def _():
            finish([acc[...] for acc in acc_refs])

    def a_spec(a):
        if isinstance(a, _KPart):
            return pl.BlockSpec((None, tm, tk), lambda i, j, kk, part=a.j: (part, i, 0))
        if isinstance(a, _Slabs):
            if mode == "tn":
                return pl.BlockSpec((None, tk, tm), lambda i, j, kk: (i, kk, 0))
            return pl.BlockSpec((None, tm, tk), lambda i, j, kk: (kk, i, 0))
        if mode == "tn":
            return pl.BlockSpec((tk, tm), lambda i, j, kk: (kk, i))
        return pl.BlockSpec((tm, tk), lambda i, j, kk: (i, kk))

    def b_spec(b):
        if isinstance(b, _KPart):
            return pl.BlockSpec((tk, tn), lambda i, j, kk, part=b.j: (part, j))
        if isinstance(b, _Slabs):
            if mode == "nt":
                return pl.BlockSpec((None, tn, tk), lambda i, j, kk: (kk, j, 0))
            return pl.BlockSpec((None, tk, tn), lambda i, j, kk: (j, kk, 0))
        if mode == "nt":
            return pl.BlockSpec((tn, tk), lambda i, j, kk: (j, kk))
        return pl.BlockSpec((tk, tn), lambda i, j, kk: (kk, j))

    o_spec = pl.BlockSpec((tm, tn), lambda i, j, kk: (i, j))
    o_slab_spec = pl.BlockSpec((None, tm, tn), lambda i, j, kk: (j, i, 0))
    v_spec = pl.BlockSpec((1, tn), lambda i, j, kk: (0, j))
    if split_cols:
        out_specs = [o_slab_spec] * no
        out_shape = [_sds((n // tn, m, tn), dt) for dt in outs]
    else:
        out_specs = [o_spec] * no
        out_shape = [_sds((m, n), dt) for dt in outs]
    raw = lambda t: t.x if isinstance(t, (_Slabs, _KPart)) else t
    res = _call(
        body,
        name=name,
        grid=(m // tm, n // tn, nk),
        in_specs=([a_spec(a) for a in a_list] + [b_spec(b) for b in b_list]
                  + [o_slab_spec if isinstance(e, _Slabs) else o_spec for e in extras] + [v_spec] * nv),
        out_specs=out_specs,
        out_shape=out_shape,
        scratch_shapes=[pltpu.VMEM((tm, tn), f32)] * (n_acc if nk > 1 else 0),
        compiler_params=_params("parallel", "parallel", "arbitrary"),
    )(*map(raw, a_list), *map(raw, b_list), *map(raw, extras), *vecs)
    return res


def _sigmoid(x):
    return 0.5 * (jnp.tanh(0.5 * x) + 1.0)


def _layer_norm_rows(r, gamma, beta):
    mu = jnp.mean(r, axis=-1, keepdims=True)
    xc = r - mu
    var = jnp.mean(xc * xc, axis=-1, keepdims=True)
    return xc * lax.rsqrt(var + LN_EPS) * gamma + beta


def _mm_plain(name, mode, a, b, dims, scale=1.0, out_dtype=f32, add=None, add_coef=1.0, tiles=None):
    m, n, k = dims
    tiles = tiles or (_tile(m, 512), _tile(n, 1024), _tile(k, 1024))

    def epilogue(accs, extras, vecs):
        r = accs[0] if scale == 1.0 else accs[0] * scale
        if extras:
            r = r + add_coef * extras[0]
        return [r]

    return _mm(name, mode, dims, tiles, [a], [b], [(0, 0, 0)], 1, epilogue, [out_dtype],
               extras=[] if add is None else [add])[0]


def _ffn_up(name, h, wg, wu):
    s = h.shape[0]

    def epilogue(accs, extras, vecs):
        g, u = accs
        return [g, u, g * _sigmoid(g) * u]

    return _mm(name, "nt", (s, D_FF, D_MODEL), (_tile(s, FFN_ROWS), FF_SLAB, D_MODEL), [h], [wg, wu],
               [(0, 0, 0), (0, 1, 1)], 2, epilogue, [bf16, bf16, bf16], split_cols=True)


def _mm_ln(name, a, w, resid, gamma, beta, scale, k_slabs=False):
    def epilogue(accs, extras, vecs):
        r = ALPHA * extras[0] + scale * accs[0]
        return [r, _layer_norm_rows(r, vecs[0], vecs[1])]

    if k_slabs:
        n_slabs, s, slab = a.shape
        return _mm(name, "nn", (s, D_MODEL, slab), (_tile(s, 512), D_MODEL, slab),
                   [_KPart(a, j) for j in range(n_slabs)], [_KPart(w, j) for j in range(n_slabs)],
                   [(j, j, 0) for j in range(n_slabs)], 1, epilogue, [f32, f32], extras=[resid], vecs=[gamma, beta])
    s, k = a.shape
    return _mm(name, "nn", (s, D_MODEL, k), (_tile(s, FFN_ROWS), D_MODEL, _tile(k, 1024)),
               [a], [w], [(0, 0, 0)], 1, epilogue, [f32, f32], extras=[resid], vecs=[gamma, beta])


def _ffn_dact(name, dr, wd, g, u):
    s = dr.shape[0]

    def epilogue(accs, extras, vecs):
        da = 0.5 * accs[0]
        gg, uu = extras[0].astype(f32), extras[1].astype(f32)
        sg = _sigmoid(gg)
        return [da * uu * (sg * (1.0 + gg * (1.0 - sg))), da * (gg * sg)]

    return _mm(name, "nt", (s, D_FF, D_MODEL), (_tile(s, FFN_ROWS), FF_SLAB, D_MODEL), [dr], [wd],
               [(0, 0, 0)], 1, epilogue, [bf16, bf16], extras=[_Slabs(g), _Slabs(u)], split_cols=True)


def _mm2(name, mode, dims, a0, b0, a1, b1, add=None, add_coef=1.0, separate=False, tiles=None, out_dtype=f32,
         split_cols=False):
    m, n, k = dims
    tiles = tiles or (_tile(m, 512), _tile(n, 1024), _tile(k, 1024))

    def epilogue(accs, extras, vecs):
        if separate:
            return list(accs)
        r = accs[0]
        if extras:
            r = r + add_coef * extras[0]
        return [r]

    a_list = [a0] if a1 is None else [a0, a1]
    b_list = [b0] if b1 is None else [b0, b1]
    pairs = [(0, 0, 0), (len(a_list) - 1, len(b_list) - 1, 1 if separate else 0)]
    return _mm(name, mode, dims, tiles, a_list, b_list, pairs, 2 if separate else 1, epilogue,
               [out_dtype, out_dtype] if separate else [out_dtype], extras=[] if add is None else [add],
               split_cols=split_cols)


def _row_call(name, body, s, ins, params, outs, accs):
    tm = ROW_TILE
    ins = [a if isinstance(a, tuple) else (a, a.shape[1], 0) for a in ins]
    in_specs = [pl.BlockSpec((tm, width), lambda i, cb=cb: (i, cb)) for _, width, cb in ins]
    ins = [a for a, _, _ in ins]
    in_specs += [pl.BlockSpec(p.shape, lambda i, nd=p.ndim: (0,) * nd) for p in params]
    out_specs = [pl.BlockSpec((tm, o.shape[1]), lambda i: (i, 0)) for o in outs]
    out_specs += [pl.BlockSpec(a.shape, lambda i, nd=len(a.shape): (0,) * nd) for a in accs]
    return pl.pallas_call(
        body,
        name=name,
        grid=(s // tm,),
        in_specs=in_specs,
        out_specs=out_specs,
        out_shape=list(outs) + list(accs),
        compiler_params=_params("arbitrary"),
    )(*ins, *params)


def _zero_at_first(refs):
    @pl.when(pl.program_id(0) == 0)
    def _():
        for r in refs:
            r[...] = jnp.zeros_like(r)


def _ln_bwd(name, r, dh, gamma):
    s = r.shape[0]

    def body(r_ref, dh_ref, g_ref, dr_ref, dg_ref, db_ref):
        _zero_at_first([dg_ref, db_ref])
        rr = r_ref[...]
        dy = dh_ref[...]
        mu = jnp.mean(rr, axis=-1, keepdims=True)
        xc = rr - mu
        rstd = lax.rsqrt(jnp.mean(xc * xc, axis=-1, keepdims=True) + LN_EPS)
        xhat = xc * rstd
        dxh = dy * g_ref[...]
        dr_ref[...] = rstd * (dxh - jnp.mean(dxh, axis=-1, keepdims=True)
                              - xhat * jnp.mean(dxh * xhat, axis=-1, keepdims=True))
        dg_ref[...] += jnp.sum(dy * xhat, axis=0, keepdims=True)
        db_ref[...] += jnp.sum(dy, axis=0, keepdims=True)

    return _row_call(name, body, s, [r, dh], [gamma], [_sds((s, D_MODEL))], [_sds((1, D_MODEL)), _sds((1, D_MODEL))])


def _loss_head(name, y, target):
    s = y.shape[0]

    def body(y_ref, t_ref, dy_ref, l_ref):
        _zero_at_first([l_ref])
        e = y_ref[...] - t_ref[...]
        dy_ref[...] = e / D_MODEL
        l_ref[...] += 0.5 * jnp.sum(jnp.mean(e * e, axis=-1, keepdims=True), axis=0, keepdims=True)

    return _row_call(name, body, s, [y, target], [], [_sds((s, D_MODEL))], [_sds((1, 128))])


def _expm1(x):
    series = x * (1.0 + x / 2.0 * (1.0 + x / 3.0 * (1.0 + x / 4.0 * (1.0 + x / 5.0 * (1.0 + x / 6.0 * (1.0 + x / 7.0))))))
    return jnp.where(jnp.abs(x) < 0.25, series, jnp.exp(x) - 1.0)


def _gates_fn(xa, wa, wx, ba, bx, lam, tap_a, tap_x):
    xb = xa.astype(bf16)
    r = jax.nn.sigmoid(jnp.dot(xb, wa, preferred_element_type=f32) + ba + tap_a)
    i = jax.nn.sigmoid(jnp.dot(xb, wx, preferred_element_type=f32) + bx + tap_x)
    log_a = -RG_C * r * jax.nn.softplus(-lam)
    a = jnp.exp(log_a)
    gated = jnp.sqrt(-_expm1(2.0 * log_a)) * (i * xa)
    return a, gated


def _rg_gates(name, xa, wa, wx, ba, bx, lam):
    s = xa.shape[0]

    def body(xa_ref, wa_ref, wx_ref, ba_ref, bx_ref, lam_ref, a_ref, g_ref):
        a, g = _gates_fn(xa_ref[...], wa_ref[...], wx_ref[...], ba_ref[...], bx_ref[...], lam_ref[...], 0.0, 0.0)
        a_ref[...] = a
        g_ref[...] = g

    return _row_call(name, body, s, [xa], [wa, wx, ba, bx, lam], [_sds((s, D_A)), _sds((s, D_A))], [])


def _rg_gates_bwd(name, xa, ga, h_prev, wa, wx, ba, bx, lam):
    s = xa.shape[0]

    def body(xa_ref, ga_ref, hp_ref, wa_ref, wx_ref, ba_ref, bx_ref, lam_ref,
             dxa_ref, dwa_ref, dwx_ref, dba_ref, dbx_ref, dlam_ref):
        _zero_at_first([dwa_ref, dwx_ref, dba_ref, dbx_ref, dlam_ref])
        xa_v = xa_ref[...]
        zero = jnp.zeros((xa_v.shape[0], D_A), f32)
        fn = lambda x, ba_, bx_, lam_, ta, tx: _gates_fn(x, wa_ref[...], wx_ref[...], ba_, bx_, lam_, ta, tx)
        _, vjp = jax.vjp(fn, xa_v, ba_ref[...], bx_ref[...], lam_ref[...], zero, zero)
        gav = ga_ref[...]
        dxa, dba, dbx, dlam, dta, dtx = vjp((gav * hp_ref[...], gav))
        dxa_ref[...] = dxa
        xb = xa_v.astype(bf16)
        dwa_ref[...] += lax.dot_general(xb, dta.astype(bf16), _DN["tn"], preferred_element_type=f32)
        dwx_ref[...] += lax.dot_general(xb, dtx.astype(bf16), _DN["tn"], preferred_element_type=f32)
        dba_ref[...] += dba
        dbx_ref[...] += dbx
        dlam_ref[...] += dlam

    return _row_call(name, body, s, [xa, ga, h_prev], [wa, wx, ba, bx, lam], [_sds((s, D_A))],
                     [_sds((D_A, D_A)), _sds((D_A, D_A)), _sds((1, D_A)), _sds((1, D_A)), _sds((1, D_A))])


def _rms(v, g):
    return v * lax.rsqrt(jnp.mean(v * v, axis=-1, keepdims=True) + RMS_EPS) * g


def _mix_out_fn(ag, ha, ob, hre, him, cu, d, gn, tap_y, tap_gl, wcr, wci, wglu):
    out_a = jax.nn.gelu(ag) * ha
    y = (jnp.dot(hre.astype(bf16), wcr, preferred_element_type=f32)
         + jnp.dot(him.astype(bf16), wci, preferred_element_type=f32) + d * cu + tap_y)
    y2 = jax.nn.gelu(y)
    gl = jnp.dot(y2.astype(bf16), wglu, preferred_element_type=f32) + tap_gl
    out_c = y2 * jax.nn.sigmoid(gl)
    o = jnp.concatenate([_rms(out_a, gn[:, :D_A]), _rms(ob, gn[:, D_A:D_A + D_B]), _rms(out_c, gn[:, D_A + D_B:])],
                        axis=-1)
    return o, y2


def _mix_out(name, ag, ha, ob, hre, him, cu, d, gn, wcr, wci, wglu):
    s = ha.shape[0]

    def body(ag_ref, ha_ref, ob_ref, hre_ref, him_ref, cu_ref, d_ref, gn_ref, wcr_ref, wci_ref, wglu_ref, o_ref):
        o, _ = _mix_out_fn(ag_ref[...], ha_ref[...], ob_ref[...], hre_ref[...], him_ref[...], cu_ref[...], d_ref[...],
                           gn_ref[...], 0.0, 0.0, wcr_ref[...], wci_ref[...], wglu_ref[...])
        o_ref[...] = o.astype(o_ref.dtype)

    return _row_call(name, body, s, [ag, ha, ob, hre, him, cu], [d, gn, wcr, wci, wglu], [_sds((s, D_MODEL), bf16)], [])[0]


def _mix_out_bwd(name, do, ag, ha, ob, hre, him, cu, d, gn, wcr, wci, wglu):
    s = ha.shape[0]

    def body(do_ref, ag_ref, ha_ref, ob_ref, hre_ref, him_ref, cu_ref, d_ref, gn_ref, wcr_ref, wci_ref, wglu_ref,
             dag_ref, dha_ref, dob_ref, dhre_ref, dhim_ref, dcu_ref, dwcr_ref, dwci_ref, dwglu_ref, dd_ref, dgn_ref):
        _zero_at_first([dwcr_ref, dwci_ref, dwglu_ref, dd_ref, dgn_ref])
        tm = ag_ref.shape[0]
        zero = jnp.zeros((tm, D_C), f32)
        hre_v, him_v = hre_ref[...], him_ref[...]
        fn = lambda *a: _mix_out_fn(*a, wcr_ref[...], wci_ref[...], wglu_ref[...])
        _, vjp, y2 = jax.vjp(fn, ag_ref[...], ha_ref[...], ob_ref[...], hre_v, him_v, cu_ref[...], d_ref[...],
                             gn_ref[...], zero, zero, has_aux=True)
        dag, dha, dob, dhre, dhim, dcu, dd, dgn, dy, dgl = vjp(do_ref[...])
        dag_ref[...] = dag
        dha_ref[...] = dha
        dob_ref[...] = dob
        dhre_ref[...] = dhre
        dhim_ref[...] = dhim
        dcu_ref[...] = dcu
        dyb = dy.astype(bf16)
        dwcr_ref[...] += lax.dot_general(hre_v.astype(bf16), dyb, _DN["tn"], preferred_element_type=f32)
        dwci_ref[...] += lax.dot_general(him_v.astype(bf16), dyb, _DN["tn"], preferred_element_type=f32)
        dwglu_ref[...] += lax.dot_general(y2.astype(bf16), dgl.astype(bf16), _DN["tn"], preferred_element_type=f32)
        dd_ref[...] += dd
        dgn_ref[...] += dgn

    outs = [_sds((s, D_A)), _sds((s, D_A)), _sds((s, D_B)), _sds((s, S5_LANES)), _sds((s, S5_LANES)), _sds((s, D_C))]
    accs = [_sds((S5_LANES, D_C)), _sds((S5_LANES, D_C)), _sds((D_C, D_C)), _sds((1, D_C)), _sds((1, D_MODEL))]
    return _row_call(name, body, s, [do, ag, ha, ob, hre, him, cu], [d, gn, wcr, wci, wglu], outs, accs)


def _log_f(name, f, bf):
    s = f[0].shape[0]

    def body(f_ref, b_ref, o_ref):
        o_ref[...] = jax.nn.log_sigmoid(f_ref[...] + b_ref[...])

    return _row_call(name, body, s, [f], [bf], [_sds((s, 128))], [])[0]


def _log_f_bwd(name, dlf, f, bf):
    s = dlf.shape[0]

    def body(dl_ref, f_ref, b_ref, df_ref, db_ref):
        _zero_at_first([db_ref])
        df = dl_ref[...] * jax.nn.sigmoid(-(f_ref[...] + b_ref[...]))
        df_ref[...] = df
        db_ref[...] += jnp.sum(df, axis=0, keepdims=True)

    return _row_call(name, body, s, [dlf, f], [bf], [_sds((s, 128))], [_sds((1, 128))])


def _s5_decay_grad(name, h_re, h_im, g_re, g_im):
    s = g_re.shape[0]
    tm = ROW_TILE

    def body(hr_ref, hi_ref, hhr_ref, hhi_ref, gr_ref, gi_ref, dr_ref, di_ref):
        i = pl.program_id(0)
        _zero_at_first([dr_ref, di_ref])

        def previous(h_ref, halo_ref):
            halo = jnp.where(i == 0, 0.0, halo_ref[...])
            return pltpu.roll(jnp.concatenate([halo, h_ref[...]], axis=0), 1, 0)[8:, :]

        hr, hi, gr, gi = previous(hr_ref, hhr_ref), previous(hi_ref, hhi_ref), gr_ref[...], gi_ref[...]
        dr_ref[...] += jnp.sum(hr * gr + hi * gi, axis=0, keepdims=True)
        di_ref[...] += jnp.sum(hr * gi - hi * gr, axis=0, keepdims=True)

    rows = pl.BlockSpec((tm, S5_LANES), lambda i: (i, 0))
    halo = pl.BlockSpec((8, S5_LANES), lambda i: (jnp.maximum(i * (tm // 8) - 1, 0), 0))
    acc = pl.BlockSpec((1, S5_LANES), lambda i: (0, 0))
    return pl.pallas_call(
        body,
        name=name,
        grid=(s // tm,),
        in_specs=[rows, rows, halo, halo, rows, rows],
        out_specs=[acc, acc],
        out_shape=[_sds((1, S5_LANES)), _sds((1, S5_LANES))],
        compiler_params=_params("arbitrary"),
    )(h_re, h_im, h_re, h_im, g_re, g_im)


def _conv_fwd(name, ax, w, b):
    s = ax.shape[0]
    tm = ROW_TILE

    def body(x_ref, halo_ref, w_ref, b_ref, o_ref):
        i = pl.program_id(0)
        x = x_ref[...]
        halo = jnp.where(i == 0, 0.0, halo_ref[...])
        ext = jnp.concatenate([halo, x], axis=0)
        acc = b_ref[...] + w_ref[3:4, :] * x
        for k in range(CONV_WIDTH - 1):
            acc = acc + w_ref[k:k + 1, :] * pltpu.roll(ext, CONV_WIDTH - 1 - k, 0)[8:, :]
        o_ref[...] = acc

    return pl.pallas_call(
        body,
        name=name,
        grid=(s // tm,),
        in_specs=[pl.BlockSpec((tm, D_A), lambda i: (i, 0)),
                  pl.BlockSpec((8, D_A), lambda i: (jnp.maximum(i * (tm // 8) - 1, 0), 0)),
                  pl.BlockSpec((CONV_WIDTH, D_A), lambda i: (0, 0)),
                  pl.BlockSpec((1, D_A), lambda i: (0, 0))],
        out_specs=pl.BlockSpec((tm, D_A), lambda i: (i, 0)),
        out_shape=_sds((s, D_A)),
        compiler_params=_params("arbitrary"),
    )(ax, ax, w, b)


def _conv_bwd(name, dxa, ax, w):
    s = ax.shape[0]
    tm = ROW_TILE
    nblk = s // tm

    def body(dx_ref, dnext_ref, x_ref, halo_ref, w_ref, dax_ref, dw_ref):
        i = pl.program_id(0)
        _zero_at_first([dw_ref])
        dx = dx_ref[...]
        dnext = jnp.where(i == nblk - 1, 0.0, dnext_ref[...])
        dext = jnp.concatenate([dx, dnext], axis=0)
        x = x_ref[...]
        halo = jnp.where(i == 0, 0.0, halo_ref[...])
        ext = jnp.concatenate([halo, x], axis=0)
        acc = w_ref[3:4, :] * dx
        dw_ref[3:4, :] += jnp.sum(dx * x, axis=0, keepdims=True)
        for k in range(CONV_WIDTH - 1):
            sh = CONV_WIDTH - 1 - k
            acc = acc + w_ref[k:k + 1, :] * pltpu.roll(dext, tm + 8 - sh, 0)[:tm, :]
            dw_ref[k:k + 1, :] += jnp.sum(dx * pltpu.roll(ext, sh, 0)[8:, :], axis=0, keepdims=True)
        dw_ref[4:5, :] += jnp.sum(dx, axis=0, keepdims=True)
        dax_ref[...] = acc

    return pl.pallas_call(
        body,
        name=name,
        grid=(nblk,),
        in_specs=[pl.BlockSpec((tm, D_A), lambda i: (i, 0)),
                  pl.BlockSpec((8, D_A), lambda i: (jnp.minimum((i + 1) * (tm // 8), s // 8 - 1), 0)),
                  pl.BlockSpec((tm, D_A), lambda i: (i, 0)),
                  pl.BlockSpec((8, D_A), lambda i: (jnp.maximum(i * (tm // 8) - 1, 0), 0)),
                  pl.BlockSpec((CONV_WIDTH, D_A), lambda i: (0, 0))],
        out_specs=[pl.BlockSpec((tm, D_A), lambda i: (i, 0)), pl.BlockSpec((8, D_A), lambda i: (0, 0))],
        out_shape=[_sds((s, D_A)), _sds((8, D_A))],
        compiler_params=_params("arbitrary"),
    )(dxa, dxa, ax, ax, w)


SCAN_ROWS = 512


def _row_in_tile(shape):
    return lax.broadcasted_iota(jnp.int32, shape, 0) % 8


def _lin_scan(name, a, b, reverse):
    s, c = a.shape
    t = min(SCAN_ROWS, s)
    nb = s // t

    def body(a_ref, b_ref, h_ref, p_ref, carry_ref):
        @pl.when(pl.program_id(0) == 0)
        def _():
            carry_ref[...] = jnp.zeros_like(carry_ref)

        row = _row_in_tile((t, c))
        p = a_ref[...]
        h = b_ref[...]
        for d in (1, 2, 4):
            keep = (row < 8 - d) if reverse else (row >= d)
            shift = (t - d) if reverse else d
            h = h + jnp.where(keep, p * pltpu.roll(h, shift, 0), 0.0)
            p = jnp.where(keep, p * pltpu.roll(p, shift, 0), p)
        h_ref[...] = h
        p_ref[...] = p
        edge = 0 if reverse else 7

        def tile(k, carry):
            kk = (t // 8 - 1 - k) if reverse else k
            r0 = pl.multiple_of(kk * 8, 8)
            hh = h_ref[pl.ds(r0, 8), :] + p_ref[pl.ds(r0, 8), :] * carry
            h_ref[pl.ds(r0, 8), :] = hh
            return jnp.broadcast_to(hh[edge:edge + 1, :], (8, c))

        carry_ref[...] = lax.fori_loop(0, t // 8, tile, carry_ref[...])

    spec = pl.BlockSpec((t, c), (lambda i: (nb - 1 - i, 0)) if reverse else (lambda i: (i, 0)))
    (out,) = _call(
        body,
        name=name,
        grid=(nb,),
        in_specs=[spec, spec],
        out_specs=[spec],
        out_shape=[_sds((s, c))],
        scratch_shapes=[pltpu.VMEM((t, c), f32), pltpu.VMEM((8, c), f32)],
        compiler_params=_params("arbitrary"),
    )(a, b)
    return out


def _s5_scan(name, b_re, b_im, a_re, a_im, reverse):
    s, c = b_re.shape
    t = min(SCAN_ROWS, s)
    nb = s // t

    def body(br_ref, bi_ref, ar_ref, ai_ref, hr_ref, hi_ref, cr_ref, ci_ref):
        @pl.when(pl.program_id(0) == 0)
        def _():
            cr_ref[...] = jnp.zeros_like(cr_ref)
            ci_ref[...] = jnp.zeros_like(ci_ref)

        ar1, ai1 = ar_ref[...], ai_ref[...]
        pows = [(ar1, ai1)]
        for _ in range(7):
            pr, pi = pows[-1]
            pows.append((pr * ar1 - pi * ai1, pr * ai1 + pi * ar1))
        row8 = lax.broadcasted_iota(jnp.int32, (8, c), 0)
        wr = jnp.zeros((8, c), f32)
        wi = jnp.zeros((8, c), f32)
        for r in range(8):
            pr, pi = pows[(7 - r) if reverse else r]
            wr = jnp.where(row8 == r, pr, wr)
            wi = jnp.where(row8 == r, pi, wi)
        row = _row_in_tile((t, c))
        hr = br_ref[...]
        hi = bi_ref[...]
        for d in (1, 2, 4):
            keep = (row < 8 - d) if reverse else (row >= d)
            shift = (t - d) if reverse else d
            pr, pi = pows[d - 1]
            cr = jnp.where(keep, pr, 0.0)
            ci = jnp.where(keep, pi, 0.0)
            sr = pltpu.roll(hr, shift, 0)
            si = pltpu.roll(hi, shift, 0)
            hr, hi = hr + cr * sr - ci * si, hi + cr * si + ci * sr
        hr_ref[...] = hr
        hi_ref[...] = hi
        edge = 0 if reverse else 7

        def tile(k, carry):
            car_r, car_i = carry
            kk = (t // 8 - 1 - k) if reverse else k
            r0 = pl.multiple_of(kk * 8, 8)
            xr = hr_ref[pl.ds(r0, 8), :] + wr * car_r - wi * car_i
            xi = hi_ref[pl.ds(r0, 8), :] + wr * car_i + wi * car_r
            hr_ref[pl.ds(r0, 8), :] = xr
            hi_ref[pl.ds(r0, 8), :] = xi
            return (jnp.broadcast_to(xr[edge:edge + 1, :], (8, c)), jnp.broadcast_to(xi[edge:edge + 1, :], (8, c)))

        car_r, car_i = lax.fori_loop(0, t // 8, tile, (cr_ref[...], ci_ref[...]))
        cr_ref[...] = car_r
        ci_ref[...] = car_i

    spec = pl.BlockSpec((t, c), (lambda i: (nb - 1 - i, 0)) if reverse else (lambda i: (i, 0)))
    vspec = pl.BlockSpec((1, c), lambda i: (0, 0))
    hr, hi = _call(
        body,
        name=name,
        grid=(nb,),
        in_specs=[spec, spec, vspec, vspec],
        out_specs=[spec, spec],
        out_shape=[_sds((s, c)), _sds((s, c))],
        scratch_shapes=[pltpu.VMEM((8, c), f32), pltpu.VMEM((8, c), f32)],
        compiler_params=_params("arbitrary"),
    )(b_re, b_im, a_re, a_im)
    return hr, hi


ATT_FEAT = 128
ATT_TQ = 1024
ATT_TK = 1024
ATT_TK_KEY_SIDE = 512


def _att_tiles(s, key_side=False):
    tq = min(ATT_TQ, s)
    tk = min(ATT_TK_KEY_SIDE if key_side else ATT_TK, tq)
    return tq, tk, tq // tk


def _keys_le_queries(tk, tq, k0, q0):
    row = lax.broadcasted_iota(jnp.int32, (tk, tq), 0) + k0
    col = lax.broadcasted_iota(jnp.int32, (tk, tq), 1) + q0
    return row <= col


def _attn_fwd_t(name, qt, k_aug, vt):
    h, s, _ = k_aug.shape
    tq, tk, ratio = _att_tiles(s)

    def body(qt_ref, k_ref, vt_ref, o_ref, lse_ref):
        qi = pl.program_id(1)
        qt = qt_ref[...]

        def block(kb, carry, masked):
            m, l, acc = carry
            ks = pl.multiple_of(kb * tk, tk)
            st = jnp.dot(k_ref[pl.ds(ks, tk), :], qt, preferred_element_type=f32)
            if masked:
                st = jnp.where(_keys_le_queries(tk, tq, ks, qi * tq), st, -jnp.inf)
            mn = jnp.maximum(m, jnp.max(st, axis=0, keepdims=True))
            p = jnp.exp(st - mn)
            al = jnp.exp(m - mn)
            l = al * l + jnp.sum(p, axis=0, keepdims=True)
            acc = al * acc + jnp.dot(vt_ref[kb], p.astype(bf16), preferred_element_type=f32)
            return mn, l, acc

        init = (jnp.full((1, tq), -jnp.inf, f32), jnp.zeros((1, tq), f32), jnp.zeros((HEAD_DIM, tq), f32))
        first = lax.fori_loop(0, qi * ratio, lambda kb, c: block(kb, c, False), init)
        m, l, acc = lax.fori_loop(qi * ratio, (qi + 1) * ratio, lambda kb, c: block(kb, c, True), first)
        o_ref[...] = acc / l
        lse_ref[...] = m + jnp.log(l)

    return _call(
        body,
        name=name,
        grid=(h, s // tq),
        in_specs=[pl.BlockSpec((None, None, ATT_FEAT, tq), lambda hh, i: (hh, i, 0, 0)),
                  pl.BlockSpec((None, s, ATT_FEAT), lambda hh, i: (hh, 0, 0)),
                  pl.BlockSpec((None, s // tk, HEAD_DIM, tk), lambda hh, i: (hh, 0, 0, 0))],
        out_specs=[pl.BlockSpec((None, HEAD_DIM, tq), lambda hh, i: (hh, 0, i)),
                   pl.BlockSpec((None, 1, tq), lambda hh, i: (hh, 0, i))],
        out_shape=[_sds((h, HEAD_DIM, s)), _sds((h, 1, s))],
        compiler_params=_params("parallel", "arbitrary"),
    )(qt, k_aug, vt)


def _attn_bwd_dq_t(name, qt, k_aug, v, kt, ot, dot_, lse):
    h, s, _ = k_aug.shape
    tq, tk, ratio = _att_tiles(s)

    def body(qt_ref, k_ref, v_ref, kt_ref, o_ref, do_ref, lse_ref, dq_ref, dl_ref):
        qi = pl.program_id(1)
        qt = qt_ref[...]
        dob = do_ref[...]
        delta = jnp.sum(dob.astype(f32) * o_ref[...], axis=0, keepdims=True)
        lse_v = lse_ref[...]

        def block(kb, carry, masked):
            dq, psum = carry
            ks = pl.multiple_of(kb * tk, tk)
            st = jnp.dot(k_ref[pl.ds(ks, tk), :], qt, preferred_element_type=f32)
            p = jnp.exp(st - lse_v)
            if masked:
                p = jnp.where(_keys_le_queries(tk, tq, ks, qi * tq), p, 0.0)
            dp = jnp.dot(v_ref[pl.ds(ks, tk), :], dob, preferred_element_type=f32)
            ds = p * (dp - delta)
            return (dq + jnp.dot(kt_ref[kb], ds.astype(bf16), preferred_element_type=f32),
                    psum + jnp.sum(p * dp, axis=0, keepdims=True))

        carry = lax.fori_loop(0, qi * ratio, lambda kb, c: block(kb, c, False),
                              (jnp.zeros((HEAD_DIM, tq), f32), jnp.zeros((1, tq), f32)))
        dq, psum = lax.fori_loop(qi * ratio, (qi + 1) * ratio, lambda kb, c: block(kb, c, True), carry)
        dq_ref[...] = dq * ATT_SCALE
        dl_ref[...] = psum

    qspec = pl.BlockSpec((None, HEAD_DIM, tq), lambda hh, i: (hh, 0, i))
    rspec = pl.BlockSpec((None, 1, tq), lambda hh, i: (hh, 0, i))
    return _call(
        body,
        name=name,
        grid=(h, s // tq),
        in_specs=[pl.BlockSpec((None, None, ATT_FEAT, tq), lambda hh, i: (hh, i, 0, 0)),
                  pl.BlockSpec((None, s, ATT_FEAT), lambda hh, i: (hh, 0, 0)),
                  pl.BlockSpec((None, s, HEAD_DIM), lambda hh, i: (hh, 0, 0)),
                  pl.BlockSpec((None, s // tk, HEAD_DIM, tk), lambda hh, i: (hh, 0, 0, 0)),
                  qspec, pl.BlockSpec((None, None, HEAD_DIM, tq), lambda hh, i: (hh, i, 0, 0)), rspec],
        out_specs=[qspec, rspec],
        out_shape=[_sds((h, HEAD_DIM, s)), _sds((h, 1, s))],
        compiler_params=_params("parallel", "arbitrary"),
    )(qt, k_aug, v, kt, ot, dot_, lse)


def _attn_bwd_dkv_t(name, qt_blocks, k_aug, v, qh, do, dot_blocks, lse, delta):
    h, s, _ = k_aug.shape
    tq, tk, ratio = _att_tiles(s, key_side=True)
    nq = s // tq

    def body(qt_ref, k_ref, v_ref, q_ref, do_ref, dot_ref, lse_ref, dl_ref, dk_ref, dv_ref, dck_ref, dsum_ref):
        kj = pl.program_id(1)
        kk = k_ref[...]
        vv = v_ref[...]
        dsum_ref[...] = jnp.zeros_like(dsum_ref)

        def block(qi, carry, masked):
            dk, dv = carry
            qs = pl.multiple_of(qi * tq, tq)
            st = jnp.dot(kk, qt_ref[qi], preferred_element_type=f32)
            p = jnp.exp(st - lse_ref[qi])
            if masked:
                p = jnp.where(_keys_le_queries(tk, tq, kj * tk, qs), p, 0.0)
            dv = dv + jnp.dot(p.astype(bf16), do_ref[pl.ds(qs, tq), :], preferred_element_type=f32)
            dp = jnp.dot(vv, dot_ref[qi], preferred_element_type=f32)
            ds = p * (dp - dl_ref[qi])
            dsum_ref[...] += ds
            dk = dk + jnp.dot(ds.astype(bf16), q_ref[pl.ds(qs, tq), :], preferred_element_type=f32)
            return dk, dv

        first = kj // ratio
        carry = block(first, (jnp.zeros((tk, HEAD_DIM), f32), jnp.zeros((tk, HEAD_DIM), f32)), True)
        dk, dv = lax.fori_loop(first + 1, nq, lambda qi, c: block(qi, c, False), carry)
        dk_ref[...] = dk
        dv_ref[...] = dv
        col = jnp.sum(dsum_ref[...], axis=1, keepdims=True)
        dck_ref[...] = -jnp.transpose(jnp.broadcast_to(col, (tk, 128)))[0:1, :]

    full = lambda shape: pl.BlockSpec((None,) + shape, lambda hh, j: (hh,) + (0,) * len(shape))
    kspec = pl.BlockSpec((None, tk, HEAD_DIM), lambda hh, j: (hh, j, 0))
    return _call(
        body,
        name=name,
        grid=(h, s // tk),
        in_specs=[full((nq, ATT_FEAT, tq)),
                  pl.BlockSpec((None, tk, ATT_FEAT), lambda hh, j: (hh, j, 0)),
                  kspec, full((s, HEAD_DIM)), full((s, HEAD_DIM)), full((nq, HEAD_DIM, tq)),
                  full((nq, 1, tq)), full((nq, 1, tq))],
        out_specs=[kspec, kspec, pl.BlockSpec((None, None, 1, tk), lambda hh, j: (hh, j, 0, 0))],
        out_shape=[_sds((h, s, HEAD_DIM)), _sds((h, s, HEAD_DIM)), _sds((h, s // tk, 1, tk))],
        scratch_shapes=[pltpu.VMEM((tk, tq), f32)],
        compiler_params=_params("parallel", "arbitrary"),
    )(qt_blocks, k_aug, v, qh, do, dot_blocks, lse, delta)


C_LANES = 128


def _selections():
    h = jnp.arange(N_HEADS)[:, None, None]
    row = jnp.arange(D_B + 3 * C_LANES)[None, :, None]
    col = jnp.arange(ATT_FEAT)[None, None, :]
    head_col = (row < D_B) & (row // HEAD_DIM == h) & (col == row % HEAD_DIM)

    def c_part(p, lane0):
        return (row == D_B + p * C_LANES + h) & (col == lane0 + p)

    c_q = c_part(0, HEAD_DIM) | c_part(1, HEAD_DIM) | c_part(2, HEAD_DIM)
    c_k = c_part(0, HEAD_DIM + 3) | c_part(1, HEAD_DIM + 3) | c_part(2, HEAD_DIM + 3)
    sel_q = (head_col | c_q).astype(bf16)
    sel_k = head_col.astype(bf16) - c_k.astype(bf16)
    sel_h = head_col[:, :D_B, :HEAD_DIM].astype(bf16)
    lane = jnp.arange(ATT_FEAT)
    ones_q = ((lane >= HEAD_DIM + 3) & (lane < HEAD_DIM + 6)).astype(f32)
    ones_k = ((lane >= HEAD_DIM) & (lane < HEAD_DIM + 3)).astype(f32)
    return dict(sel_qt=sel_q.transpose(0, 2, 1), sel_k=sel_k, sel_h=sel_h, sel_ht=sel_h.transpose(0, 2, 1),
                ones_q=ones_q.reshape(ATT_FEAT, 1), ones_k=ones_k.reshape(1, ATT_FEAT))


def _attn_prep(name, z, c, sel):
    s = z.shape[0]
    tq, tk, ratio = _att_tiles(s)

    def body(q_ref, k_ref, v_ref, c_ref, sqt_ref, sk_ref, sh_ref, sht_ref, oq_ref, ok_ref,
             qt_out, ka_out, kt_out, vt_out, v_out, qh_out):
        cv = c_ref[...]
        hi = cv.astype(bf16)
        r1 = cv - hi.astype(f32)
        mid = r1.astype(bf16)
        lo = (r1 - mid.astype(f32)).astype(bf16)
        qs = (q_ref[...] * ATT_SCALE).astype(bf16)
        kb = k_ref[...].astype(bf16)
        vb = v_ref[...].astype(bf16)
        xq = jnp.concatenate([qs, hi, mid, lo], axis=-1)
        xk = jnp.concatenate([kb, hi, mid, lo], axis=-1)
        for h in range(N_HEADS):
            qt = lax.dot_general(sqt_ref[h], xq, _DN["nt"], preferred_element_type=f32) + oq_ref[...]
            qt_out[h, 0] = qt.astype(bf16)
            ka_out[h] = (jnp.dot(xk, sk_ref[h], preferred_element_type=f32) + ok_ref[...]).astype(bf16)
            kt = lax.dot_general(sht_ref[h], kb, _DN["nt"], preferred_element_type=f32).astype(bf16)
            vt = lax.dot_general(sht_ref[h], vb, _DN["nt"], preferred_element_type=f32).astype(bf16)
            for j in range(ratio):
                kt_out[h, j] = kt[:, j * tk:(j + 1) * tk]
                vt_out[h, j] = vt[:, j * tk:(j + 1) * tk]
            v_out[h] = jnp.dot(vb, sh_ref[h], preferred_element_type=f32).astype(bf16)
            qh_out[h] = jnp.dot(qs, sh_ref[h], preferred_element_type=f32).astype(bf16)

    whole = lambda a: pl.BlockSpec(a.shape, lambda i, nd=a.ndim: (0,) * nd)
    consts = [sel["sel_qt"], sel["sel_k"], sel["sel_h"], sel["sel_ht"], sel["ones_q"], sel["ones_k"]]
    return pl.pallas_call(
        body,
        name=name,
        grid=(s // tq,),
        in_specs=[pl.BlockSpec((tq, D_B), lambda i: (i, 2)), pl.BlockSpec((tq, D_B), lambda i: (i, 3)),
                  pl.BlockSpec((tq, D_B), lambda i: (i, 4)), pl.BlockSpec((tq, C_LANES), lambda i: (i, 0))]
        + [whole(a) for a in consts],
        out_specs=[pl.BlockSpec((N_HEADS, 1, ATT_FEAT, tq), lambda i: (0, i, 0, 0)),
                   pl.BlockSpec((N_HEADS, tq, ATT_FEAT), lambda i: (0, i, 0)),
                   pl.BlockSpec((N_HEADS, ratio, HEAD_DIM, tk), lambda i: (0, i, 0, 0)),
                   pl.BlockSpec((N_HEADS, ratio, HEAD_DIM, tk), lambda i: (0, i, 0, 0)),
                   pl.BlockSpec((N_HEADS, tq, HEAD_DIM), lambda i: (0, i, 0)),
                   pl.BlockSpec((N_HEADS, tq, HEAD_DIM), lambda i: (0, i, 0))],
        out_shape=[_sds((N_HEADS, s // tq, ATT_FEAT, tq), bf16), _sds((N_HEADS, s, ATT_FEAT), bf16),
                   _sds((N_HEADS, s // tk, HEAD_DIM, tk), bf16), _sds((N_HEADS, s // tk, HEAD_DIM, tk), bf16),
                   _sds((N_HEADS, s, HEAD_DIM), bf16), _sds((N_HEADS, s, HEAD_DIM), bf16)],
        compiler_params=_params("parallel"),
    )(z, z, z, c, *consts)


def _attn_do_prep(name, dob, sel):
    s = dob.shape[0]
    tq = _att_tiles(s)[0]

    def body(do_ref, sh_ref, sht_ref, dot_out, do_out):
        db = do_ref[...].astype(bf16)
        for h in range(N_HEADS):
            dot_out[h, 0] = lax.dot_general(sht_ref[h], db, _DN["nt"], preferred_element_type=f32).astype(bf16)
            do_out[h] = jnp.dot(db, sh_ref[h], preferred_element_type=f32).astype(bf16)

    whole = lambda a: pl.BlockSpec(a.shape, lambda i, nd=a.ndim: (0,) * nd)
    return pl.pallas_call(
        body,
        name=name,
        grid=(s // tq,),
        in_specs=[pl.BlockSpec((tq, D_B), lambda i: (i, 0)), whole(sel["sel_h"]), whole(sel["sel_ht"])],
        out_specs=[pl.BlockSpec((N_HEADS, 1, HEAD_DIM, tq), lambda i: (0, i, 0, 0)),
                   pl.BlockSpec((N_HEADS, tq, HEAD_DIM), lambda i: (0, i, 0))],
        out_shape=[_sds((N_HEADS, s // tq, HEAD_DIM, tq), bf16), _sds((N_HEADS, s, HEAD_DIM), bf16)],
        compiler_params=_params("parallel"),
    )(dob, sel["sel_h"], sel["sel_ht"])


def _dz_assemble(name, dax, dag, dqt, dkh, dvh, df, dcu, sel):
    s = dax.shape[0]
    tm = _tile(s, 512)

    def body(dax_ref, dag_ref, dqt_ref, dk_ref, dv_ref, df_ref, dcu_ref, sht_ref, o_ref):
        dq = jnp.zeros((tm, D_B), f32)
        dk = jnp.zeros((tm, D_B), f32)
        dv = jnp.zeros((tm, D_B), f32)
        for h in range(N_HEADS):
            place = sht_ref[h]
            dq = dq + lax.dot_general(dqt_ref[h].astype(bf16), place, _DN["tn"], preferred_element_type=f32)
            dk = dk + jnp.dot(dk_ref[h].astype(bf16), place, preferred_element_type=f32)
            dv = dv + jnp.dot(dv_ref[h].astype(bf16), place, preferred_element_type=f32)
        pieces = [dax_ref[...], dag_ref[...], dq, dk, dv, df_ref[...], dcu_ref[...]]
        off = 0
        for p in pieces:
            o_ref[:, off:off + p.shape[1]] = p.astype(bf16)
            off += p.shape[1]

    rows = lambda c_: pl.BlockSpec((tm, c_), lambda i: (i, 0))
    heads = pl.BlockSpec((N_HEADS, tm, HEAD_DIM), lambda i: (0, i, 0))
    return pl.pallas_call(
        body,
        name=name,
        grid=(s // tm,),
        in_specs=[rows(D_A), rows(D_A), pl.BlockSpec((N_HEADS, HEAD_DIM, tm), lambda i: (0, 0, i)), heads, heads,
                  rows(128), rows(D_C), pl.BlockSpec(sel["sel_ht"].shape, lambda i: (0, 0, 0))],
        out_specs=rows(N_IN_P),
        out_shape=_sds((s, N_IN_P), bf16),
        compiler_params=_params("parallel"),
    )(dax, dag, dqt, dkh, dvh, df, dcu, sel["sel_ht"])


def _s5_disc_fn(are, aim, ldt):
    dt = jnp.exp(ldt)
    er = jnp.exp(are * dt)
    br = er * jnp.cos(aim * dt)
    bi = er * jnp.sin(aim * dt)
    nr = br - 1.0
    den = are * are + aim * aim
    return br, bi, (nr * are + bi * aim) / den, (bi * are - nr * aim) / den


def _s5_disc(name, are, aim, ldt):
    def body(a_ref, b_ref, c_ref, o0, o1, o2, o3):
        r = _s5_disc_fn(a_ref[...], b_ref[...], c_ref[...])
        o0[...], o1[...], o2[...], o3[...] = r

    shp = _sds((S5_GROUPS, S5_STATE))
    return pl.pallas_call(body, name=name, out_shape=[shp] * 4)(are, aim, ldt)


def _s5_disc_bwd(name, are, aim, ldt, cts):
    def body(a_ref, b_ref, c_ref, d0, d1, d2, d3, o0, o1, o2):
        _, vjp = jax.vjp(_s5_disc_fn, a_ref[...], b_ref[...], c_ref[...])
        o0[...], o1[...], o2[...] = vjp((d0[...], d1[...], d2[...], d3[...]))

    shp = _sds((S5_GROUPS, S5_STATE))
    return pl.pallas_call(body, name=name, out_shape=[shp, shp, _sds((S5_GROUPS, 1))])(are, aim, ldt, *cts)


def _adamw_rows(w, g, m, v):
    m = ADAM_B1 * m + (1.0 - ADAM_B1) * g
    v = ADAM_B2 * v + (1.0 - ADAM_B2) * (g * g)
    m_hat = m / (1.0 - ADAM_B1 ** ADAM_STEP)
    v_hat = v / (1.0 - ADAM_B2 ** ADAM_STEP)
    return -ADAM_LR * (m_hat / (jnp.sqrt(v_hat) + ADAM_EPS) + ADAM_WD * w), m, v


def _adamw(name, w, ga, gb, m, v):
    rows, cols = w.shape
    tr = _row_tile(rows)

    def body(w_ref, ga_ref, gb_ref, m_ref, v_ref, g_out, d_out, m_out, v_out):
        g = ga_ref[...] + gb_ref[...]
        d, mm, vv = _adamw_rows(w_ref[...], g, m_ref[...], v_ref[...])
        g_out[...] = g
        d_out[...] = d
        m_out[...] = mm
        v_out[...] = vv

    spec = pl.BlockSpec((tr, cols), lambda i: (i, 0))
    return pl.pallas_call(
        body, name=name, grid=(rows // tr,), in_specs=[spec] * 5, out_specs=[spec] * 4,
        out_shape=[_sds((rows, cols))] * 4, compiler_params=_params("parallel"),
    )(w, ga, gb, m, v)


def _sum_stack(name, st):
    n, rows, cols = st.shape
    tr = _row_tile(rows)

    def body(s_ref, o_ref):
        acc = s_ref[0].astype(f32)
        for j in range(1, n):
            acc = acc + s_ref[j].astype(f32)
        o_ref[...] = acc

    return pl.pallas_call(
        body, name=name, grid=(rows // tr,), in_specs=[pl.BlockSpec((n, tr, cols), lambda i: (0, i, 0))],
        out_specs=pl.BlockSpec((tr, cols), lambda i: (i, 0)), out_shape=_sds((rows, cols)),
        compiler_params=_params("parallel"),
    )(st)


def _block_diag(w):
    h, n, m = w.shape
    return jnp.einsum("hij,hg->higj", w, jnp.eye(h, dtype=w.dtype)).reshape(h * n, h * m)


def _block_diag_part(dense, h):
    n, m = dense.shape[0] // h, dense.shape[1] // h
    return jnp.einsum("higj,hg->hij", dense.reshape(h, n, h, m), jnp.eye(h, dtype=dense.dtype))


def _s5_matrices(coef_re, coef_im, b_re, b_im, c_re, c_im):
    bb_re = coef_re[:, :, None] * b_re - coef_im[:, :, None] * b_im
    bb_im = coef_re[:, :, None] * b_im + coef_im[:, :, None] * b_re
    wb_re = _block_diag(jnp.swapaxes(bb_re, 1, 2))
    wb_im = _block_diag(jnp.swapaxes(bb_im, 1, 2))
    wc_re = _block_diag(jnp.swapaxes(c_re, 1, 2))
    wc_im = _block_diag(jnp.swapaxes(-c_im, 1, 2))
    return wb_re, wb_im, wc_re, wc_im


def _shift_down(t):
    return jnp.concatenate([jnp.zeros((1, t.shape[1]), t.dtype), t[:-1]], axis=0)


def _shift_up(t):
    return jnp.concatenate([t[1:], jnp.zeros((1, t.shape[1]), t.dtype)], axis=0)


def _row(v):
    return v.reshape(1, -1)


def _ffn_fwd(tag, h, get, names, gamma, beta):
    wg, wu = get(names[0]), get(names[1])
    g, u, act = _ffn_up(tag + "_up", h, wg, wu)
    wd = get(names[2])
    r, out = _mm_ln(tag + "_down", act, wd, h, gamma, beta, 0.5, k_slabs=True)
    return out, dict(h=h, g=g, u=u, act=act, r=r, wg=wg, wu=wu, wd=wd)


def _ffn_bwd(tag, dout, sv, names, gamma, put, after_ln=None):
    s = dout.shape[0]
    dr, dgam, dbet = _ln_bwd(tag + "_lnb", sv["r"], dout, gamma)
    if after_ln is not None:
        after_ln(dgam, dbet)
    put(names[2], _mm_plain(tag + "_dwd", "tn", _Slabs(sv["act"]), dr, (D_FF, D_MODEL, s), scale=0.5, out_dtype=bf16,
                            tiles=(FF_SLAB, 1024, _tile(s, 2048))))
    dg, du = _ffn_dact(tag + "_dact", dr, sv["wd"], sv["g"], sv["u"])
    dwg, dwu = _mm2(tag + "_dwgu", "tn", (D_FF, D_MODEL, s), _Slabs(dg), sv["h"], _Slabs(du), None, separate=True,
                    out_dtype=bf16, tiles=(FF_SLAB, 1024, _tile(s, 2048)))
    put(names[0], dwg)
    put(names[1], dwu)
    slabs = range(dg.shape[0])
    dh = _mm(tag + "_dh", "nn", (s, D_MODEL, FF_SLAB), (_tile(s, 512), D_MODEL, FF_SLAB),
             [_KPart(dg, j) for j in slabs] + [_KPart(du, j) for j in slabs],
             [_KPart(sv["wg"], j) for j in slabs] + [_KPart(sv["wu"], j) for j in slabs],
             [(j, j, 0) for j in range(2 * len(slabs))], 1,
             lambda accs, extras, vecs: [accs[0] + ALPHA * extras[0]], [f32], extras=[dr])[0]
    return dh, dgam, dbet


def _mixer_fwd(tag, h1, w):
    s = h1.shape[0]
    z = _mm_plain(tag + "_win", "nn", h1, w["w_in"], (s, N_IN_P, D_MODEL), tiles=(_tile(s, 512), 768, D_MODEL))
    ag, f, cu_cols = (z, D_A, 1), (z, 128, F_OFF // 128), (z, D_C, CU_OFF // D_C)
    cu = z[:, CU_OFF:]
    xa = _conv_fwd(tag + "_conv", z, w["conv_w"], w["conv_b"])
    a, gated = _rg_gates(tag + "_gates", xa, w["rg_wa"], w["rg_wx"], w["rg_ba"], w["rg_bx"], w["rg_lam"])
    ha = _lin_scan(tag + "_rgscan", a, gated, False)
    ones = jnp.ones((s, 128), f32)
    c = _lin_scan(tag + "_cumf", ones, _log_f(tag + "_logf", f, w["fox_bf"]), False)
    att = dict(zip(("qt", "k_aug", "kt", "vt", "v", "qh"), _attn_prep(tag + "_attnprep", z, c, w["sel"])))
    ot, lse = _attn_fwd_t(tag + "_attn", att["qt"], att["k_aug"], att["vt"])
    ob = ot.reshape(D_B, s).T
    bu_re, bu_im = _mm2(tag + "_s5in", "nn", (s, S5_LANES, D_C), cu, w["wb_re"], None, w["wb_im"], separate=True,
                        tiles=(_tile(s, 512), 1024, D_C))
    hre, him = _s5_scan(tag + "_s5scan", bu_re, bu_im, w["abar_re"], w["abar_im"], False)
    o = _mix_out(tag + "_mixout", ag, ha, ob, hre, him, cu_cols, w["s5_d"], w["mix_g"], w["wc_re"], w["wc_im"],
                 w["w_glu"])
    sv = dict(h1=h1, z=z, ag=ag, f=f, cu=cu, cu_cols=cu_cols, xa=xa, a=a, ha=ha, att=att, ot=ot, lse=lse, ob=ob,
              hre=hre, him=him, o=o)
    return o, sv


def _mixer_bwd(tag, do, dr2, sv, w, put):
    s = do.shape[0]
    (dag, dha, dob, dhre, dhim, dcu1, dwcr, dwci, dwglu, dd, dgn) = _mix_out_bwd(
        tag + "_mixoutb", do, sv["ag"], sv["ha"], sv["ob"], sv["hre"], sv["him"], sv["cu_cols"], w["s5_d"], w["mix_g"],
        w["wc_re"], w["wc_im"], w["w_glu"])
    put("s5_w_glu", dwglu.astype(bf16))
    gre, gim = _s5_scan(tag + "_s5scanb", dhre, dhim, w["abar_re"], -w["abar_im"], True)
    dab_re, dab_im = _s5_decay_grad(tag + "_s5dec", sv["hre"], sv["him"], gre, gim)
    dwb_re, dwb_im = _mm2(tag + "_s5dwb", "tn", (D_C, S5_LANES, s), sv["cu"], gre, None, gim, separate=True,
                          tiles=(D_C, 1024, _tile(s, 1024)))
    dcu = _mm2(tag + "_s5dcu", "nt", (s, D_C, S5_LANES), gre, w["wb_re"], gim, w["wb_im"], add=dcu1,
               tiles=(_tile(s, 512), D_C, 1024))[0]
    att = sv["att"]
    tq = _att_tiles(s)[0]
    nt = s // tq
    dot_blocks, doh = _attn_do_prep(tag + "_doprep", dob, w["sel"])
    dqt, delta = _attn_bwd_dq_t(tag + "_attndq", att["qt"], att["k_aug"], att["v"], att["kt"], sv["ot"], dot_blocks,
                                sv["lse"])
    dkh, dvh, dck = _attn_bwd_dkv_t(tag + "_attndkv", att["qt"], att["k_aug"], att["v"], att["qh"], doh, dot_blocks,
                                    sv["lse"].reshape(N_HEADS, nt, 1, tq), delta.reshape(N_HEADS, nt, 1, tq))
    dc = jnp.pad(dck.reshape(N_HEADS, s).T, ((0, 0), (0, 128 - N_HEADS)))
    dlf = _lin_scan(tag + "_cumfb", jnp.ones((s, 128), f32), dc, True)
    df, dbf = _log_f_bwd(tag + "_logfb", dlf, sv["f"], w["fox_bf"])
    ga = _lin_scan(tag + "_rgscanb", _shift_up(sv["a"]), dha, True)
    dxa, dwa, dwx, dba, dbx, dlam = _rg_gates_bwd(tag + "_gatesb", sv["xa"], ga, _shift_down(sv["ha"]), w["rg_wa"],
                                                  w["rg_wx"], w["rg_ba"], w["rg_bx"], w["rg_lam"])
    dax, dconv = _conv_bwd(tag + "_convb", dxa, sv["z"], w["conv_w"])
    dz = _dz_assemble(tag + "_dz", dax, dag, dqt, dkh, dvh, df, dcu, w["sel"])
    put("w_in", _mm_plain(tag + "_dwin", "tn", sv["h1"], dz, (D_MODEL, N_IN_P, s), out_dtype=bf16,
                          tiles=(512, 768, _tile(s, 1024))))
    dh1 = _mm_plain(tag + "_dh1", "nt", dz, w["w_in"], (s, D_MODEL, N_IN_P), add=dr2, add_coef=ALPHA,
                    tiles=(_tile(s, 512), 1024, 768))
    grads = dict(dconv=dconv, dwa=dwa, dwx=dwx, dba=dba, dbx=dbx, dlam=dlam, dbf=dbf,
                 dab_re=dab_re, dab_im=dab_im, dwb_re=dwb_re, dwb_im=dwb_im, dwcr=dwcr, dwci=dwci, dd=dd, dgn=dgn)
    return dh1, grads


SMALL_NAMES = ["ln1_g", "ln1_b", "conv_w", "conv_b", "rg_w_a", "rg_b_a", "rg_w_x", "rg_b_x", "rg_lambda", "fox_b_f",
               "s5_a_re", "s5_a_im", "s5_log_dt", "s5_b_re", "s5_b_im", "s5_c_re", "s5_c_im", "s5_d", "mix_norm_g",
               "ln2_g", "ln2_b", "ln3_g", "ln3_b"]
BIG_NAMES = ["ffn1_w_gate", "ffn1_w_up", "ffn1_w_down", "w_in", "s5_w_glu", "w_out", "ffn2_w_gate", "ffn2_w_up",
             "ffn2_w_down"]


def _local_step(x, target, weight, small, on_grads, on_small):
    h = x
    saved = []
    sel = _selections()
    for l in range(DEPTH):
        get = functools.partial(weight, l)

        sm = {n: small[n][l] for n in SMALL_NAMES}
        abar_re, abar_im, coef_re, coef_im = _s5_disc(f"l{l}_s5disc", sm["s5_a_re"], sm["s5_a_im"],
                                                      sm["s5_log_dt"].reshape(S5_GROUPS, 1))
        mats, mats_vjp = jax.vjp(_s5_matrices, coef_re, coef_im, sm["s5_b_re"], sm["s5_b_im"], sm["s5_c_re"],
                                 sm["s5_c_im"])
        w = dict(
            sel=sel, conv_w=sm["conv_w"], conv_b=_row(sm["conv_b"]),
            rg_wa=_block_diag(sm["rg_w_a"]).astype(bf16), rg_wx=_block_diag(sm["rg_w_x"]).astype(bf16),
            rg_ba=_row(sm["rg_b_a"]), rg_bx=_row(sm["rg_b_x"]), rg_lam=_row(sm["rg_lambda"]),
            fox_bf=jnp.pad(_row(sm["fox_b_f"]), ((0, 0), (0, 128 - N_HEADS))),
            abar_re=_row(abar_re), abar_im=_row(abar_im),
            wb_re=mats[0].astype(bf16), wb_im=mats[1].astype(bf16), wc_re=mats[2].astype(bf16),
            wc_im=mats[3].astype(bf16), s5_d=_row(sm["s5_d"]), mix_g=_row(sm["mix_norm_g"]))
        h1, sv1 = _ffn_fwd(f"l{l}_ffn1", h, get, GROUPS["F1"], _row(sm["ln1_g"]), _row(sm["ln1_b"]))
        w["w_in"], w["w_glu"] = get("w_in"), get("s5_w_glu")
        o, svm = _mixer_fwd(f"l{l}_mix", h1, w)
        w_out = get("w_out")
        r2, h2 = _mm_ln(f"l{l}_wout", o, w_out, h1, _row(sm["ln2_g"]), _row(sm["ln2_b"]), 1.0)
        h3, sv2 = _ffn_fwd(f"l{l}_ffn2", h2, get, GROUPS["F2"], _row(sm["ln3_g"]), _row(sm["ln3_b"]))
        saved.append(dict(sm=sm, w=w, w_out=w_out, sv1=sv1, svm=svm, r2=r2, sv2=sv2, mats_vjp=mats_vjp))
        h = h3

    dh, loss_row = _loss_head("loss_head", h, target)
    s = x.shape[0]
    gsmall = {n: [None] * DEPTH for n in SMALL_NAMES}
    for l in reversed(range(DEPTH)):
        sd = saved[l]
        sm, w = sd["sm"], sd["w"]

        def put(name, grad, l=l):
            on_grads((l, name), grad)

        dh2, dgam, dbet = _ffn_bwd(f"l{l}_ffn2", dh, sd["sv2"], GROUPS["F2"], _row(sm["ln3_g"]), put)
        gsmall["ln3_g"][l], gsmall["ln3_b"][l] = dgam[0], dbet[0]
        dr2, dgam, dbet = _ln_bwd(f"l{l}_ln2b", sd["r2"], dh2, _row(sm["ln2_g"]))
        gsmall["ln2_g"][l], gsmall["ln2_b"][l] = dgam[0], dbet[0]
        put("w_out", _mm_plain(f"l{l}_dwout", "tn", sd["svm"]["o"], dr2, (D_MODEL, D_MODEL, s), out_dtype=bf16))
        do = _mm_plain(f"l{l}_do", "nt", dr2, sd["w_out"], (s, D_MODEL, D_MODEL))
        dh1, g = _mixer_bwd(f"l{l}_mix", do, dr2, sd["svm"], w, put)
        gsmall["conv_w"][l], gsmall["conv_b"][l] = g["dconv"][:CONV_WIDTH], g["dconv"][CONV_WIDTH]
        gsmall["rg_w_a"][l] = _block_diag_part(g["dwa"], N_HEADS)
        gsmall["rg_w_x"][l] = _block_diag_part(g["dwx"], N_HEADS)
        gsmall["rg_b_a"][l], gsmall["rg_b_x"][l], gsmall["rg_lambda"][l] = g["dba"][0], g["dbx"][0], g["dlam"][0]
        gsmall["fox_b_f"][l] = g["dbf"][0, :N_HEADS]
        dcoef_re, dcoef_im, db_re, db_im, dc_re, dc_im = sd["mats_vjp"]((g["dwb_re"], g["dwb_im"], g["dwcr"], g["dwci"]))
        da_re, da_im, dldt = _s5_disc_bwd(
            f"l{l}_s5discb", sm["s5_a_re"], sm["s5_a_im"], sm["s5_log_dt"].reshape(S5_GROUPS, 1),
            (g["dab_re"].reshape(S5_GROUPS, S5_STATE), g["dab_im"].reshape(S5_GROUPS, S5_STATE), dcoef_re, dcoef_im))
        gsmall["s5_a_re"][l], gsmall["s5_a_im"][l], gsmall["s5_log_dt"][l] = da_re, da_im, dldt[:, 0]
        gsmall["s5_b_re"][l], gsmall["s5_b_im"][l], gsmall["s5_c_re"][l], gsmall["s5_c_im"][l] = db_re, db_im, dc_re, dc_im
        gsmall["s5_d"][l], gsmall["mix_norm_g"][l] = g["dd"][0], g["dgn"][0]

        def after_ln(dgam, dbet, l=l):
            gsmall["ln1_g"][l], gsmall["ln1_b"][l] = dgam[0], dbet[0]
            if l == 0:
                on_small({n: jnp.stack(v) for n, v in gsmall.items()})

        dh, _, _ = _ffn_bwd(f"l{l}_ffn1", dh1, sd["sv1"], GROUPS["F1"], _row(sm["ln1_g"]), put, after_ln)
    return loss_row[0, 0], dh


def _position():
    return lax.axis_index("x"), lax.axis_index("y"), lax.axis_index("c")


_ANY = pl.BlockSpec(memory_space=pl.ANY)


COLUMN_SHARDED = ("ffn1_w_gate", "ffn1_w_up", "ffn2_w_gate", "ffn2_w_up")
PACK_QUANTUM = 128 * 256


def _permute_in_cols(w):
    pad = jnp.zeros(w.shape[:-1] + (128 - N_HEADS,), w.dtype)
    return jnp.concatenate([w[..., :F_OFF + N_HEADS], pad, w[..., F_OFF + N_HEADS:]], axis=-1)


def _unpermute_in_cols(w):
    return jnp.concatenate([w[..., :F_OFF + N_HEADS], w[..., CU_OFF:]], axis=-1)


def _pack(arrs):
    flat = jnp.concatenate([a.reshape(-1) for a in arrs])
    pad = -flat.shape[0] % PACK_QUANTUM
    return jnp.pad(flat, (0, pad)).reshape(-1, 128)


def _unpack(buf, shapes):
    flat = buf.reshape(-1)
    out, off = [], 0
    for shp in shapes:
        size = math.prod(shp)
        out.append(flat[off:off + size].reshape(shp))
        off += size
    return out


WEIGHT_NAMES = ["ffn1_w_gate", "ffn1_w_up", "ffn1_w_down", "ln1_g", "ln1_b", "w_in", "conv_w", "conv_b", "rg_w_a",
                "rg_b_a", "rg_w_x", "rg_b_x", "rg_lambda", "fox_b_f", "s5_a_re", "s5_a_im", "s5_log_dt", "s5_b_re",
                "s5_b_im", "s5_c_re", "s5_c_im", "s5_d", "s5_w_glu", "mix_norm_g", "w_out", "ln2_g", "ln2_b",
                "ffn2_w_gate", "ffn2_w_up", "ffn2_w_down", "ln3_g", "ln3_b"]


def _remote(src, dst, send_sems, recv_sems, k, peer):
    return pltpu.make_async_remote_copy(src_ref=src, dst_ref=dst, send_sem=send_sems.at[k], recv_sem=recv_sems.at[k],
                                        device_id=peer, device_id_type=MESH)


class _ChipGatherPart:
    def __init__(self, arrays):
        self.arrays, self.results = list(arrays), None

    def out_shape(self):
        return [_sds((N_CHIPS,) + a.shape, a.dtype) for a in self.arrays]

    def sems(self):
        n = len(self.arrays)
        return [pltpu.SemaphoreType.DMA((3 * n,)), pltpu.SemaphoreType.DMA((3 * n,)), pltpu.SemaphoreType.DMA((n,))]

    def copies(self, ins, outs, sems):
        send_sems, recv_sems, local_sems = sems
        x, y, c = _position()
        me = 2 * x + y
        local, sends, recvs = [], [], []
        for i, (src, dst) in enumerate(zip(ins, outs)):
            local.append(pltpu.make_async_copy(self.mine(src, me), dst.at[me], local_sems.at[i]))
            for r, (px, py) in enumerate([(1 - x, y), (x, 1 - y), (1 - x, 1 - y)]):
                peer = 2 * px + py
                sends.append(_remote(self.theirs(src, peer), dst.at[me], send_sems, recv_sems, 3 * i + r, (px, py, c)))
                recvs.append(_remote(self.mine(src, me), dst.at[peer], send_sems, recv_sems, 3 * i + r, (px, py, c)))
        return local, sends, recvs

    def mine(self, src, me):
        return src

    def theirs(self, src, peer):
        return src


class _ChipGatherHalvesPart(_ChipGatherPart):
    def sems(self):
        n = len(self.arrays)
        return super().sems() + [pltpu.SemaphoreType.DMA((3 * n,)), pltpu.SemaphoreType.DMA((3 * n,))]

    def _half(self, ref, which):
        rows = ref.shape[0] // 2
        return ref.at[pl.ds(which * rows, rows)]

    def copies(self, ins, outs, sems):
        send_sems, recv_sems, local_sems = sems[:3]
        x, y, c = _position()
        me = 2 * x + y
        local, sends, recvs = [], [], []
        for i, (src, dst) in enumerate(zip(ins, outs)):
            local.append(pltpu.make_async_copy(src, dst.at[me], local_sems.at[i]))
            for r, (px, py) in enumerate([(1 - x, y), (x, 1 - y), (1 - x, 1 - y)]):
                sends.append(_remote(self._half(src, c), self._half(dst.at[me], c), send_sems, recv_sems, 3 * i + r,
                                     (px, py, c)))
                recvs.append(_remote(self._half(src, c), self._half(dst.at[2 * px + py], c), send_sems, recv_sems,
                                     3 * i + r, (px, py, c)))
        return local, sends, recvs

    def forwards(self, ins, outs, sems):
        send_sems, recv_sems = sems[3:]
        x, y, c = _position()
        sends, recvs = [], []
        for i, dst in enumerate(outs):
            for r, (px, py) in enumerate([(1 - x, y), (x, 1 - y), (1 - x, 1 - y)]):
                slot = dst.at[2 * px + py]
                sends.append(_remote(self._half(slot, c), self._half(slot, c), send_sems, recv_sems, 3 * i + r,
                                     (x, y, 1 - c)))
                recvs.append(_remote(self._half(slot, c), self._half(slot, 1 - c), send_sems, recv_sems, 3 * i + r,
                                     (x, y, 1 - c)))
        return sends, recvs


class _ChipScatterPart(_ChipGatherPart):
    def out_shape(self):
        return [_sds(a.shape, a.dtype) for a in self.arrays]

    def mine(self, src, me):
        return src.at[me]

    def theirs(self, src, peer):
        return src.at[peer]


class _SiblingSwapPart:
    def __init__(self, arrays):
        self.arrays, self.results = list(arrays), None

    def out_shape(self):
        return [_sds(a.shape, a.dtype) for a in self.arrays]

    def sems(self):
        n = len(self.arrays)
        return [pltpu.SemaphoreType.DMA((n,)), pltpu.SemaphoreType.DMA((n,))]

    def copies(self, ins, outs, sems):
        x, y, c = _position()
        both = [_remote(src, dst, sems[0], sems[1], i, (x, y, 1 - c)) for i, (src, dst) in enumerate(zip(ins, outs))]
        return [], both, both


def _split_by(parts, refs, count):
    out, off = [], 0
    for p in parts:
        out.append(refs[off:off + count(p)])
        off += count(p)
    return out


def _parts_refs(parts, in_refs, out_refs, sem_refs):
    return zip(parts, _split_by(parts, in_refs, lambda p: len(p.arrays)),
               _split_by(parts, out_refs, lambda p: len(p.arrays)), _split_by(parts, sem_refs, lambda p: len(p.sems())))


def _exchange_start(parts, in_refs, out_refs, sem_refs):
    for part, ins, outs, sems in _parts_refs(parts, in_refs, out_refs, sem_refs):
        local, sends, _ = part.copies(ins, outs, sems)
        for cp in local + sends:
            cp.start()


def _exchange_finish(parts, in_refs, out_refs, sem_refs):
    split = list(_parts_refs(parts, in_refs, out_refs, sem_refs))
    copies = [part.copies(ins, outs, sems) for part, ins, outs, sems in split]
    for _, _, recvs in copies:
        for cp in recvs:
            cp.wait_recv()
    second = [part.forwards(ins, outs, sems) for part, ins, outs, sems in split if hasattr(part, "forwards")]
    for sends, _ in second:
        for cp in sends:
            cp.start()
    for sends, recvs in second:
        for cp in recvs:
            cp.wait_recv()
        for cp in sends:
            cp.wait_send()
    for local, sends, _ in copies:
        for cp in sends:
            cp.wait_send()
        for cp in local:
            cp.wait()


def _exchange_operands(parts):
    return ([a for p in parts for a in p.arrays], [s for p in parts for s in p.out_shape()],
            [s for p in parts for s in p.sems()])


def _set_results(parts, res):
    for part, outs in zip(parts, _split_by(parts, list(res), lambda p: len(p.arrays))):
        part.results = list(outs)


def _exchange_now(name, parts):
    x_in, x_out, x_sem = _exchange_operands(parts)
    n = len(x_in)

    def body(*refs):
        _exchange_start(parts, refs[:n], refs[n:2 * n], refs[2 * n:])
        _exchange_finish(parts, refs[:n], refs[n:2 * n], refs[2 * n:])

    res = pl.pallas_call(body, name=name, in_specs=[_ANY] * n, out_specs=[_ANY] * n, out_shape=x_out,
                         scratch_shapes=x_sem)(*x_in)
    _set_results(parts, res)


_RIDERS = {}


def _call(body, *, name, grid, in_specs, out_specs, out_shape, scratch_shapes=(), compiler_params=None):
    make_parts = _RIDERS.pop(name, None)
    if make_parts is None:
        return pl.pallas_call(body, name=name, grid=grid, in_specs=in_specs, out_specs=out_specs, out_shape=out_shape,
                              scratch_shapes=scratch_shapes, compiler_params=compiler_params)
    parts = make_parts()
    x_in, x_out, x_sem = _exchange_operands(parts)
    n_out, n_scr, n_x = len(out_shape), len(scratch_shapes), len(x_in)

    def run(*args):
        n_in = len(args)

        def hosted(*refs):
            ins, xi = refs[:n_in], refs[n_in:n_in + n_x]
            outs, xo = refs[n_in + n_x:n_in + n_x + n_out], refs[n_in + n_x + n_out:n_in + 2 * n_x + n_out]
            scr, xs = refs[n_in + 2 * n_x + n_out:n_in + 2 * n_x + n_out + n_scr], refs[n_in + 2 * n_x + n_out + n_scr:]
            first = functools.reduce(jnp.logical_and, [pl.program_id(d) == 0 for d in range(len(grid))])
            last = functools.reduce(jnp.logical_and, [pl.program_id(d) == grid[d] - 1 for d in range(len(grid))])

            @pl.when(first)
            def _():
                _exchange_start(parts, xi, xo, xs)

            body(*ins, *outs, *scr)

            @pl.when(last)
            def _():
                _exchange_finish(parts, xi, xo, xs)

        res = pl.pallas_call(
            hosted, name=name, grid=grid, in_specs=list(in_specs) + [_ANY] * n_x,
            out_specs=list(out_specs) + [_ANY] * n_x, out_shape=list(out_shape) + x_out,
            scratch_shapes=list(scratch_shapes) + x_sem, compiler_params=_params(*["arbitrary"] * len(grid)),
        )(*args, *x_in)
        _set_results(parts, res[n_out:])
        return list(res[:n_out])

    return run


GROUPS = {"F1": ["ffn1_w_gate", "ffn1_w_up", "ffn1_w_down"], "MX": ["w_in", "s5_w_glu", "w_out"],
          "F2": ["ffn2_w_gate", "ffn2_w_up", "ffn2_w_down"]}
FIRST_GATHER = [(0, "ffn1_w_gate"), (0, "ffn1_w_up")]
GATHER_HOSTS = {
    "l0_ffn1_up": [(0, "ffn1_w_down")],
    "l0_ffn1_down": [(0, "w_in"), (0, "s5_w_glu"), (0, "w_out")],
    "l0_mix_attn": [(0, "ffn2_w_up"), (0, "ffn2_w_down"), (1, "w_in")],
    "l0_mix_s5scan": [(0, "ffn2_w_gate")],
    "l0_wout": [(1, "s5_w_glu"), (1, "w_out")],
    "l0_ffn2_up": [(1, "ffn1_w_gate")],
    "l0_ffn2_down": [(1, "ffn1_w_up")],
    "l1_ffn1_up": [(1, "ffn1_w_down")],
    "l1_mix_attn": [(1, "ffn2_w_up"), (1, "ffn2_w_down")],
    "l1_mix_s5scan": [(1, "ffn2_w_gate")],
}
SCATTER_HOSTS = {
    "l1_ffn2_dact": [(1, "ffn2_w_down")],
    "l1_mix_attndq": [(1, "ffn2_w_up")],
    "l1_mix_attndkv": [(1, "w_out"), (1, "s5_w_glu"), (1, "ffn2_w_gate")],
    "l1_ffn1_dact": [(1, "ffn1_w_down")],
    "l1_ffn1_dwgu": [(1, "w_in")],
    "l0_ffn2_dact": [(1, "ffn1_w_up")],
    "l0_ffn2_dwgu": [(0, "ffn2_w_down")],
    "l0_mix_attndq": [(0, "w_out"), (0, "s5_w_glu"), (0, "ffn2_w_up")],
    "l0_mix_attndkv": [(0, "ffn2_w_gate"), (1, "ffn1_w_gate")],
    "l0_mix_dh1": [(0, "w_in")],
    "l0_ffn1_dact": [(0, "ffn1_w_down")],
    "l0_ffn1_dh": [(0, "ffn1_w_gate")],
}
LAST_SCATTER = [(0, "ffn1_w_up")]
SMALL_HOST = "l0_ffn1_dwd"
SMALL_PACK_ORDER = [n for n in SMALL_NAMES if n != "conv_w"] + ["conv_w"]
TAIL_HOST = "l0_ffn1_dwgu"
LATE_SCATTER_HOST = "l0_ffn1_dh"


def _sharded_rows(name, a):
    return jnp.swapaxes(a, 1, 2) if name in COLUMN_SHARDED else a


def _unstack_layer(st):
    _, r, c = st.shape
    return st.reshape(N_CHIPS * r, c)


def _restack_layer(g):
    r, c = g.shape
    return g.reshape(N_CHIPS, r // N_CHIPS, c)


def _adamw_layer(name, layer, w, ga, gb, m, v, bufs):
    _, r, c = w.shape
    tr = _row_tile(r)

    def body(w_ref, ga_ref, gb_ref, m_ref, v_ref, *rest):
        g_out, d_out, m_out, v_out = rest[-4:]
        g = ga_ref[...] + gb_ref[...]
        d, mm, vv = _adamw_rows(w_ref[...], g, m_ref[...], v_ref[...])
        g_out[...] = g
        d_out[...] = d
        m_out[...] = mm
        v_out[...] = vv

    full = pl.BlockSpec((None, tr, c), lambda i: (layer, i, 0))
    flat = pl.BlockSpec((tr, c), lambda i: (i, 0))
    extra = {} if bufs is None else dict(input_output_aliases={5 + k: k for k in range(4)})
    return pl.pallas_call(
        body, name=name, grid=(r // tr,),
        in_specs=[full, flat, flat, full, full] + ([] if bufs is None else [_ANY] * 4),
        out_specs=[full] * 4, out_shape=[_sds(w.shape)] * 4, compiler_params=_params("parallel"), **extra,
    )(w, ga, gb, m, v, *([] if bufs is None else bufs))


def _train_step(x, loss_target, w, m, v):
    ix, iy, _ = _position()
    chip = 2 * ix + iy
    shard = {n: (_permute_in_cols(w[n]) if n == "w_in" else _sharded_rows(n, w[n])).astype(bf16) for n in BIG_NAMES}

    gathered = {}

    def gather_parts(keys, extra=()):
        part = _ChipGatherHalvesPart([shard[n][layer] for layer, n in keys] + list(extra))
        gathered.update({key: (part, i) for i, key in enumerate(keys)})
        return [part]

    (first,) = gather_parts(FIRST_GATHER, extra=[w["conv_w"]])
    _exchange_now("gather_first", [first])
    for host, keys in GATHER_HOSTS.items():
        _RIDERS[host] = functools.partial(gather_parts, keys)

    def weight(layer, name):
        part, i = gathered[(layer, name)]
        return _unstack_layer(part.results[i])

    small = {n: w[n] for n in SMALL_NAMES}
    small["conv_w"] = first.results[-1].transpose(1, 2, 0, 3).reshape(DEPTH, CONV_WIDTH, D_A)

    grads_full, scattered = {}, {}

    def scatter_parts(keys):
        part = _ChipScatterPart([_restack_layer(grads_full[key]) for key in keys])
        scattered.update({key: (part, i) for i, key in enumerate(keys)})
        return [part]

    for host, keys in SCATTER_HOSTS.items():
        _RIDERS[host] = functools.partial(scatter_parts, keys)

    partial = {}

    def reduce_chips(keys):
        for layer, n in keys:
            part, i = scattered[(layer, n)]
            p = _sum_stack(f"sum_l{layer}_{n}", part.results[i])
            partial[(layer, n)] = _unpermute_in_cols(p) if n == "w_in" else p

    early = [key for host, keys in SCATTER_HOSTS.items() if host != LATE_SCATTER_HOST for key in keys]
    late = SCATTER_HOSTS[LATE_SCATTER_HOST]
    tail = {}

    def small_parts():
        tail["small"] = _ChipGatherPart([_pack([tail["gsmall"][n] for n in SMALL_PACK_ORDER])])
        return [tail["small"]]

    def tail_parts():
        reduce_chips(early)
        tail["small_sum"] = _sum_stack("sum_small", tail["small"].results[0])
        tail["swap"] = _SiblingSwapPart([partial[k] for k in early] + [tail["small_sum"]])
        return [tail["swap"]]

    _RIDERS[SMALL_HOST] = small_parts
    _RIDERS[TAIL_HOST] = tail_parts
    loss_local, gx = _local_step(x[0], loss_target[0], weight, small, grads_full.__setitem__,
                                 functools.partial(tail.__setitem__, "gsmall"))
    other = dict(zip(early, tail["swap"].results[:-1]))
    small_mine, small_other = tail["small_sum"], tail["swap"].results[-1]
    reduce_chips(late)
    last_parts = scatter_parts(LAST_SCATTER) + [_SiblingSwapPart([partial[k] for k in late])]
    _exchange_now("exchange_last", last_parts)
    other.update(zip(late, last_parts[1].results))
    reduce_chips(LAST_SCATTER)
    swap_last = _SiblingSwapPart([partial[k] for k in LAST_SCATTER])
    _exchange_now("swap_last", [swap_last])
    other.update(zip(LAST_SCATTER, swap_last.results))

    grads, deltas, new_m, new_v = {}, {}, {}, {}
    for n in BIG_NAMES:
        bufs = None
        wr, mr, vr = (_sharded_rows(n, t) for t in (w[n], m[n], v[n]))
        for layer in range(DEPTH):
            bufs = _adamw_layer(f"adamw_l{layer}_{n}", layer, wr, partial[(layer, n)], other[(layer, n)], mr, vr, bufs)
        grads[n], deltas[n], new_m[n], new_v[n] = (_sharded_rows(n, t) for t in bufs)
    packed = SMALL_PACK_ORDER[:-1]
    shapes = [w[n].shape for n in packed]
    res = _adamw("adamw_small", _pack([w[n] for n in packed]), small_mine, small_other,
                 _pack([m[n] for n in packed]), _pack([v[n] for n in packed]))
    for dst, buf in zip((grads, deltas, new_m, new_v), res):
        dst.update(zip(packed, _unpack(buf, shapes)))
    cw = D_A // N_CHIPS
    conv_shape = (DEPTH, CONV_WIDTH, D_A)
    offset = sum(math.prod(s_) for s_ in shapes)

    def conv_grad(buf):
        full = buf.reshape(-1)[offset:offset + math.prod(conv_shape)].reshape(conv_shape)
        return lax.dynamic_slice_in_dim(full, chip * cw, cw, axis=2).reshape(DEPTH * CONV_WIDTH, cw)

    rows = lambda t: t.reshape(DEPTH * CONV_WIDTH, cw)
    res = _adamw("adamw_conv_w", rows(w["conv_w"]), conv_grad(small_mine), conv_grad(small_other),
                 rows(m["conv_w"]), rows(v["conv_w"]))
    for dst, buf in zip((grads, deltas, new_m, new_v), res):
        dst["conv_w"] = buf.reshape(w["conv_w"].shape)

    loss = lax.psum(loss_local, ("x", "y", "c"))
    return (loss, gx[None], *[grads[n] for n in WEIGHT_NAMES], *[deltas[n] for n in WEIGHT_NAMES],
            *[new_m[n] for n in WEIGHT_NAMES], *[new_v[n] for n in WEIGHT_NAMES])


def kernel(x, ffn1_w_gate, ffn1_w_up, ffn1_w_down, ln1_g, ln1_b, w_in, conv_w, conv_b, rg_w_a, rg_b_a, rg_w_x, rg_b_x, rg_lambda, fox_b_f, s5_a_re, s5_a_im, s5_log_dt, s5_b_re, s5_b_im, s5_c_re, s5_c_im, s5_d, s5_w_glu, mix_norm_g, w_out, ln2_g, ln2_b, ffn2_w_gate, ffn2_w_up, ffn2_w_down, ln3_g, ln3_b, loss_target, m_ffn1_w_gate, m_ffn1_w_up, m_ffn1_w_down, m_ln1_g, m_ln1_b, m_w_in, m_conv_w, m_conv_b, m_rg_w_a, m_rg_b_a, m_rg_w_x, m_rg_b_x, m_rg_lambda, m_fox_b_f, m_s5_a_re, m_s5_a_im, m_s5_log_dt, m_s5_b_re, m_s5_b_im, m_s5_c_re, m_s5_c_im, m_s5_d, m_s5_w_glu, m_mix_norm_g, m_w_out, m_ln2_g, m_ln2_b, m_ffn2_w_gate, m_ffn2_w_up, m_ffn2_w_down, m_ln3_g, m_ln3_b, v_ffn1_w_gate, v_ffn1_w_up, v_ffn1_w_down, v_ln1_g, v_ln1_b, v_w_in, v_conv_w, v_conv_b, v_rg_w_a, v_rg_b_a, v_rg_w_x, v_rg_b_x, v_rg_lambda, v_fox_b_f, v_s5_a_re, v_s5_a_im, v_s5_log_dt, v_s5_b_re, v_s5_b_im, v_s5_c_re, v_s5_c_im, v_s5_d, v_s5_w_glu, v_mix_norm_g, v_w_out, v_ln2_g, v_ln2_b, v_ffn2_w_gate, v_ffn2_w_up, v_ffn2_w_down, v_ln3_g, v_ln3_b):
    args = dict(locals())
    w = {n: args[n] for n in WEIGHT_NAMES}
    m = {n: args["m_" + n] for n in WEIGHT_NAMES}
    v = {n: args["v_" + n] for n in WEIGHT_NAMES}
    return _train_step(x, loss_target, w, m, v)
```

```python
import functools
import math

import jax
import jax.numpy as jnp
from jax import lax
from jax.experimental import pallas as pl
from jax.experimental.pallas import tpu as pltpu

f32 = jnp.float32
bf16 = jnp.bfloat16

D_MODEL = 1024
D_FF = 2816
D_A = 384
D_B = 384
D_C = 256
N_HEADS = 6
HEAD_DIM = 64
S5_GROUPS = 16
S5_STATE = 64
S5_LANES = S5_GROUPS * S5_STATE
F_OFF = 5 * D_A
CU_OFF = F_OFF + 128
N_IN_P = CU_OFF + D_C
CONV_WIDTH = 4
DEPTH = 2
ALPHA = (2 * DEPTH) ** 0.25
LN_EPS = 1e-5
RMS_EPS = 1e-6
RG_C = 8.0
ATT_SCALE = HEAD_DIM ** -0.5
ADAM_LR, ADAM_B1, ADAM_B2, ADAM_EPS, ADAM_WD, ADAM_STEP = 0.001, 0.9, 0.999, 1e-08, 0.01, 10

ROW_TILE = 512
N_CHIPS = 4
MESH = pl.DeviceIdType.MESH

_DN = {
    "nn": (((1,), (0,)), ((), ())),
    "nt": (((1,), (1,)), ((), ())),
    "tn": (((0,), (0,)), ((), ())),
}


def _sds(shape, dtype=f32):
    return jax.ShapeDtypeStruct(shape, dtype)


def _tile(n, target):
    best = None
    for t in range(128, min(n, target) + 1, 128):
        if n % t == 0:
            best = t
    return best or n


def _row_tile(rows, target=256):
    best = None
    for t in range(16, min(rows, target) + 1, 16):
        if rows % t == 0:
            best = t
    return best or rows


def _params(*sem):
    return pltpu.CompilerParams(dimension_semantics=sem)


class _Slabs:
    def __init__(self, x):
        self.x = x


class _KPart:
    def __init__(self, x, j):
        self.x, self.j = x, j


FF_SLAB = D_FF // N_CHIPS
FFN_ROWS = 1024

def _mm(name, mode, dims, tiles, a_list, b_list, pairs, n_acc, epilogue, outs, extras=(), vecs=(), split_cols=False):
    m, n, k = dims
    tm, tn, tk = tiles
    nk = k // tk
    na, nb, ne, nv, no = len(a_list), len(b_list), len(extras), len(vecs), len(outs)

    def body(*refs):
        a_refs = refs[:na]
        b_refs = refs[na:na + nb]
        e_refs = refs[na + nb:na + nb + ne]
        v_refs = refs[na + nb + ne:na + nb + ne + nv]
        o_refs = refs[na + nb + ne + nv:na + nb + ne + nv + no]
        acc_refs = refs[na + nb + ne + nv + no:]
        a_vals = [r[...].astype(bf16) for r in a_refs]
        b_vals = [r[...].astype(bf16) for r in b_refs]
        products = [(ci, lax.dot_general(a_vals[ai], b_vals[bi], _DN[mode], preferred_element_type=f32))
                    for ai, bi, ci in pairs]

        def finish(accs):
            res = epilogue(accs, [e[...] for e in e_refs], [v[...] for v in v_refs])
            for o, r in zip(o_refs, res):
                o[...] = r.astype(o.dtype)

        if nk == 1:
            accs = [None] * n_acc
            for ci, prod in products:
                accs[ci] = prod if accs[ci] is None else accs[ci] + prod
            finish(accs)
            return
        kk = pl.program_id(2)

        @pl.when(kk == 0)
        def _():
            for acc in acc_refs:
                acc[...] = jnp.zeros_like(acc)

        for ci, prod in products:
            acc_refs[ci][...] += prod

        @pl.when(kk == nk - 1)
        def _():
            finish([acc[...] for acc in acc_refs])

    def a_spec(a):
        if isinstance(a, _KPart):
            return pl.BlockSpec((None, tm, tk), lambda i, j, kk, part=a.j: (part, i, 0))
        if isinstance(a, _Slabs):
            if mode == "tn":
                return pl.BlockSpec((None, tk, tm), lambda i, j, kk: (i, kk, 0))
            return pl.BlockSpec((None, tm, tk), lambda i, j, kk: (kk, i, 0))
        if mode == "tn":
            return pl.BlockSpec((tk, tm), lambda i, j, kk: (kk, i))
        return pl.BlockSpec((tm, tk), lambda i, j, kk: (i, kk))

    def b_spec(b):
        if isinstance(b, _KPart):
            return pl.BlockSpec((tk, tn), lambda i, j, kk, part=b.j: (part, j))
        if isinstance(b, _Slabs):
            if mode == "nt":
                return pl.BlockSpec((None, tn, tk), lambda i, j, kk: (kk, j, 0))
            return pl.BlockSpec((None, tk, tn), lambda i, j, kk: (j, kk, 0))
        if mode == "nt":
            return pl.BlockSpec((tn, tk), lambda i, j, kk: (j, kk))
        return pl.BlockSpec((tk, tn), lambda i, j, kk: (kk, j))

    o_spec = pl.BlockSpec((tm, tn), lambda i, j, kk: (i, j))
    o_slab_spec = pl.BlockSpec((None, tm, tn), lambda i, j, kk: (j, i, 0))
    v_spec = pl.BlockSpec((1, tn), lambda i, j, kk: (0, j))
    if split_cols:
        out_specs = [o_slab_spec] * no
        out_shape = [_sds((n // tn, m, tn), dt) for dt in outs]
    else:
        out_specs = [o_spec] * no
        out_shape = [_sds((m, n), dt) for dt in outs]
    raw = lambda t: t.x if isinstance(t, (_Slabs, _KPart)) else t
    res = _call(
        body,
        name=name,
        grid=(m // tm, n // tn, nk),
        in_specs=([a_spec(a) for a in a_list] + [b_spec(b) for b in b_list]
                  + [o_slab_spec if isinstance(e, _Slabs) else o_spec for e in extras] + [v_spec] * nv),
        out_specs=out_specs,
        out_shape=out_shape,
        scratch_shapes=[pltpu.VMEM((tm, tn), f32)] * (n_acc if nk > 1 else 0),
        compiler_params=_params("parallel", "parallel", "arbitrary"),
    )(*map(raw, a_list), *map(raw, b_list), *map(raw, extras), *vecs)
    return res


def _sigmoid(x):
    return 0.5 * (jnp.tanh(0.5 * x) + 1.0)


def _layer_norm_rows(r, gamma, beta):
    mu = jnp.mean(r, axis=-1, keepdims=True)
    xc = r - mu
    var = jnp.mean(xc * xc, axis=-1, keepdims=True)
    return xc * lax.rsqrt(var + LN_EPS) * gamma + beta


def _mm_plain(name, mode, a, b, dims, scale=1.0, out_dtype=f32, add=None, add_coef=1.0, tiles=None):
    m, n, k = dims
    tiles = tiles or (_tile(m, 512), _tile(n, 1024), _tile(k, 1024))

    def epilogue(accs, extras, vecs):
        r = accs[0] if scale == 1.0 else accs[0] * scale
        if extras:
            r = r + add_coef * extras[0]
        return [r]

    return _mm(name, mode, dims, tiles, [a], [b], [(0, 0, 0)], 1, epilogue, [out_dtype],
               extras=[] if add is None else [add])[0]


def _ffn_up(name, h, wg, wu):
    s = h.shape[0]

    def epilogue(accs, extras, vecs):
        g, u = accs
        return [g, u, g * _sigmoid(g) * u]

    return _mm(name, "nt", (s, D_FF, D_MODEL), (_tile(s, FFN_ROWS), FF_SLAB, D_MODEL), [h], [wg, wu],
               [(0, 0, 0), (0, 1, 1)], 2, epilogue, [bf16, bf16, bf16], split_cols=True)


def _ffn_gate(name, h, wg):
    s = h.shape[0]
    return _mm(name, "nt", (s, D_FF, D_MODEL), (_tile(s, FFN_ROWS), FF_SLAB, D_MODEL), [h], [wg], [(0, 0, 0)], 1,
               lambda accs, extras, vecs: [accs[0]], [bf16], split_cols=True)[0]


def _ffn_up_given_gate(name, h, wu, g):
    s = h.shape[0]

    def epilogue(accs, extras, vecs):
        gg = extras[0].astype(f32)
        return [accs[0], gg * _sigmoid(gg) * accs[0]]

    return _mm(name, "nt", (s, D_FF, D_MODEL), (_tile(s, FFN_ROWS), FF_SLAB, D_MODEL), [h], [wu], [(0, 0, 0)], 1,
               epilogue, [bf16, bf16], extras=[_Slabs(g)], split_cols=True)


def _mm_ln(name, a, w, resid, gamma, beta, scale, k_slabs=False):
    def epilogue(accs, extras, vecs):
        r = ALPHA * extras[0] + scale * accs[0]
        return [r, _layer_norm_rows(r, vecs[0], vecs[1])]

    if k_slabs:
        n_slabs, s, slab = a.shape
        return _mm(name, "nn", (s, D_MODEL, slab), (_tile(s, 512), D_MODEL, slab),
                   [_KPart(a, j) for j in range(n_slabs)], [_KPart(w, j) for j in range(n_slabs)],
                   [(j, j, 0) for j in range(n_slabs)], 1, epilogue, [f32, f32], extras=[resid], vecs=[gamma, beta])
    s, k = a.shape
    return _mm(name, "nn", (s, D_MODEL, k), (_tile(s, FFN_ROWS), D_MODEL, _tile(k, 1024)),
               [a], [w], [(0, 0, 0)], 1, epilogue, [f32, f32], extras=[resid], vecs=[gamma, beta])


def _ffn_dact(name, dr, wd, g, u):
    s = dr.shape[0]

    def epilogue(accs, extras, vecs):
        da = 0.5 * accs[0]
        gg, uu = extras[0].astype(f32), extras[1].astype(f32)
        sg = _sigmoid(gg)
        return [da * uu * (sg * (1.0 + gg * (1.0 - sg))), da * (gg * sg)]

    return _mm(name, "nt", (s, D_FF, D_MODEL), (_tile(s, FFN_ROWS), FF_SLAB, D_MODEL), [dr], [wd],
               [(0, 0, 0)], 1, epilogue, [bf16, bf16], extras=[_Slabs(g), _Slabs(u)], split_cols=True)


def _mm2(name, mode, dims, a0, b0, a1, b1, add=None, add_coef=1.0, separate=False, tiles=None, out_dtype=f32,
         split_cols=False):
    m, n, k = dims
    tiles = tiles or (_tile(m, 512), _tile(n, 1024), _tile(k, 1024))

    def epilogue(accs, extras, vecs):
        if separate:
            return list(accs)
        r = accs[0]
        if extras:
            r = r + add_coef * extras[0]
        return [r]

    a_list = [a0] if a1 is None else [a0, a1]
    b_list = [b0] if b1 is None else [b0, b1]
    pairs = [(0, 0, 0), (len(a_list) - 1, len(b_list) - 1, 1 if separate else 0)]
    return _mm(name, mode, dims, tiles, a_list, b_list, pairs, 2 if separate else 1, epilogue,
               [out_dtype, out_dtype] if separate else [out_dtype], extras=[] if add is None else [add],
               split_cols=split_cols)


def _row_call(name, body, s, ins, params, outs, accs):
    tm = ROW_TILE
    ins = [a if isinstance(a, tuple) else (a, a.shape[1], 0) for a in ins]
    in_specs = [pl.BlockSpec((tm, width), lambda i, cb=cb: (i, cb)) for _, width, cb in ins]
    ins = [a for a, _, _ in ins]
    in_specs += [pl.BlockSpec(p.shape, lambda i, nd=p.ndim: (0,) * nd) for p in params]
    out_specs = [pl.BlockSpec((tm, o.shape[1]), lambda i: (i, 0)) for o in outs]
    out_specs += [pl.BlockSpec(a.shape, lambda i, nd=len(a.shape): (0,) * nd) for a in accs]
    return pl.pallas_call(
        body,
        name=name,
        grid=(s // tm,),
        in_specs=in_specs,
        out_specs=out_specs,
        out_shape=list(outs) + list(accs),
        compiler_params=_params("arbitrary"),
    )(*ins, *params)


def _zero_at_first(refs):
    @pl.when(pl.program_id(0) == 0)
    def _():
        for r in refs:
            r[...] = jnp.zeros_like(r)


def _ln_bwd(name, r, dh, gamma):
    s = r.shape[0]

    def body(r_ref, dh_ref, g_ref, dr_ref, dg_ref, db_ref):
        _zero_at_first([dg_ref, db_ref])
        rr = r_ref[...]
        dy = dh_ref[...]
        mu = jnp.mean(rr, axis=-1, keepdims=True)
        xc = rr - mu
        rstd = lax.rsqrt(jnp.mean(xc * xc, axis=-1, keepdims=True) + LN_EPS)
        xhat = xc * rstd
        dxh = dy * g_ref[...]
        dr_ref[...] = rstd * (dxh - jnp.mean(dxh, axis=-1, keepdims=True)
                              - xhat * jnp.mean(dxh * xhat, axis=-1, keepdims=True))
        dg_ref[...] += jnp.sum(dy * xhat, axis=0, keepdims=True)
        db_ref[...] += jnp.sum(dy, axis=0, keepdims=True)

    return _row_call(name, body, s, [r, dh], [gamma], [_sds((s, D_MODEL))], [_sds((1, D_MODEL)), _sds((1, D_MODEL))])


def _loss_head(name, y, target):
    s = y.shape[0]

    def body(y_ref, t_ref, dy_ref, l_ref):
        _zero_at_first([l_ref])
        e = y_ref[...] - t_ref[...]
        dy_ref[...] = e / D_MODEL
        l_ref[...] += 0.5 * jnp.sum(jnp.mean(e * e, axis=-1, keepdims=True), axis=0, keepdims=True)

    return _row_call(name, body, s, [y, target], [], [_sds((s, D_MODEL))], [_sds((1, 128))])


def _expm1(x):
    series = x * (1.0 + x / 2.0 * (1.0 + x / 3.0 * (1.0 + x / 4.0 * (1.0 + x / 5.0 * (1.0 + x / 6.0 * (1.0 + x / 7.0))))))
    return jnp.where(jnp.abs(x) < 0.25, series, jnp.exp(x) - 1.0)


def _gates_fn(xa, wa, wx, ba, bx, lam, tap_a, tap_x):
    xb = xa.astype(bf16)
    r = jax.nn.sigmoid(jnp.dot(xb, wa, preferred_element_type=f32) + ba + tap_a)
    i = jax.nn.sigmoid(jnp.dot(xb, wx, preferred_element_type=f32) + bx + tap_x)
    log_a = -RG_C * r * jax.nn.softplus(-lam)
    a = jnp.exp(log_a)
    gated = jnp.sqrt(-_expm1(2.0 * log_a)) * (i * xa)
    return a, gated


def _rg_gates(name, xa, wa, wx, ba, bx, lam):
    s = xa.shape[0]

    def body(xa_ref, wa_ref, wx_ref, ba_ref, bx_ref, lam_ref, a_ref, g_ref):
        a, g = _gates_fn(xa_ref[...], wa_ref[...], wx_ref[...], ba_ref[...], bx_ref[...], lam_ref[...], 0.0, 0.0)
        a_ref[...] = a
        g_ref[...] = g

    return _row_call(name, body, s, [xa], [wa, wx, ba, bx, lam], [_sds((s, D_A)), _sds((s, D_A))], [])


def _rg_gates_bwd(name, xa, ga, h_prev, wa, wx, ba, bx, lam):
    s = xa.shape[0]

    def body(xa_ref, ga_ref, hp_ref, wa_ref, wx_ref, ba_ref, bx_ref, lam_ref,
             dxa_ref, dwa_ref, dwx_ref, dba_ref, dbx_ref, dlam_ref):
        _zero_at_first([dwa_ref, dwx_ref, dba_ref, dbx_ref, dlam_ref])
        xa_v = xa_ref[...]
        zero = jnp.zeros((xa_v.shape[0], D_A), f32)
        fn = lambda x, ba_, bx_, lam_, ta, tx: _gates_fn(x, wa_ref[...], wx_ref[...], ba_, bx_, lam_, ta, tx)
        _, vjp = jax.vjp(fn, xa_v, ba_ref[...], bx_ref[...], lam_ref[...], zero, zero)
        gav = ga_ref[...]
        dxa, dba, dbx, dlam, dta, dtx = vjp((gav * hp_ref[...], gav))
        dxa_ref[...] = dxa
        xb = xa_v.astype(bf16)
        dwa_ref[...] += lax.dot_general(xb, dta.astype(bf16), _DN["tn"], preferred_element_type=f32)
        dwx_ref[...] += lax.dot_general(xb, dtx.astype(bf16), _DN["tn"], preferred_element_type=f32)
        dba_ref[...] += dba
        dbx_ref[...] += dbx
        dlam_ref[...] += dlam

    return _row_call(name, body, s, [xa, ga, h_prev], [wa, wx, ba, bx, lam], [_sds((s, D_A))],
                     [_sds((D_A, D_A)), _sds((D_A, D_A)), _sds((1, D_A)), _sds((1, D_A)), _sds((1, D_A))])


def _rms(v, g):
    return v * lax.rsqrt(jnp.mean(v * v, axis=-1, keepdims=True) + RMS_EPS) * g


def _mix_out_fn(ag, ha, ob, hre, him, cu, d, gn, tap_y, tap_gl, wcr, wci, wglu):
    out_a = jax.nn.gelu(ag) * ha
    y = (jnp.dot(hre.astype(bf16), wcr, preferred_element_type=f32)
         + jnp.dot(him.astype(bf16), wci, preferred_element_type=f32) + d * cu + tap_y)
    y2 = jax.nn.gelu(y)
    gl = jnp.dot(y2.astype(bf16), wglu, preferred_element_type=f32) + tap_gl
    out_c = y2 * jax.nn.sigmoid(gl)
    o = jnp.concatenate([_rms(out_a, gn[:, :D_A]), _rms(ob, gn[:, D_A:D_A + D_B]), _rms(out_c, gn[:, D_A + D_B:])],
                        axis=-1)
    return o, y2


def _mix_out(name, ag, ha, ob, hre, him, cu, d, gn, wcr, wci, wglu):
    s = ha.shape[0]

    def body(ag_ref, ha_ref, ob_ref, hre_ref, him_ref, cu_ref, d_ref, gn_ref, wcr_ref, wci_ref, wglu_ref, o_ref):
        o, _ = _mix_out_fn(ag_ref[...], ha_ref[...], ob_ref[...], hre_ref[...], him_ref[...], cu_ref[...], d_ref[...],
                           gn_ref[...], 0.0, 0.0, wcr_ref[...], wci_ref[...], wglu_ref[...])
        o_ref[...] = o.astype(o_ref.dtype)

    return _row_call(name, body, s, [ag, ha, ob, hre, him, cu], [d, gn, wcr, wci, wglu], [_sds((s, D_MODEL), bf16)], [])[0]


def _mix_out_bwd(name, do, ag, ha, ob, hre, him, cu, d, gn, wcr, wci, wglu):
    s = ha.shape[0]

    def body(do_ref, ag_ref, ha_ref, ob_ref, hre_ref, him_ref, cu_ref, d_ref, gn_ref, wcr_ref, wci_ref, wglu_ref,
             dag_ref, dha_ref, dob_ref, dhre_ref, dhim_ref, dcu_ref, dwcr_ref, dwci_ref, dwglu_ref, dd_ref, dgn_ref):
        _zero_at_first([dwcr_ref, dwci_ref, dwglu_ref, dd_ref, dgn_ref])
        tm = ag_ref.shape[0]
        zero = jnp.zeros((tm, D_C), f32)
        hre_v, him_v = hre_ref[...], him_ref[...]
        fn = lambda *a: _mix_out_fn(*a, wcr_ref[...], wci_ref[...], wglu_ref[...])
        _, vjp, y2 = jax.vjp(fn, ag_ref[...], ha_ref[...], ob_ref[...], hre_v, him_v, cu_ref[...], d_ref[...],
                             gn_ref[...], zero, zero, has_aux=True)
        dag, dha, dob, dhre, dhim, dcu, dd, dgn, dy, dgl = vjp(do_ref[...])
        dag_ref[...] = dag
        dha_ref[...] = dha
        dob_ref[...] = dob
        dhre_ref[...] = dhre
        dhim_ref[...] = dhim
        dcu_ref[...] = dcu
        dyb = dy.astype(bf16)
        dwcr_ref[...] += lax.dot_general(hre_v.astype(bf16), dyb, _DN["tn"], preferred_element_type=f32)
        dwci_ref[...] += lax.dot_general(him_v.astype(bf16), dyb, _DN["tn"], preferred_element_type=f32)
        dwglu_ref[...] += lax.dot_general(y2.astype(bf16), dgl.astype(bf16), _DN["tn"], preferred_element_type=f32)
        dd_ref[...] += dd
        dgn_ref[...] += dgn

    outs = [_sds((s, D_A)), _sds((s, D_A)), _sds((s, D_B)), _sds((s, S5_LANES)), _sds((s, S5_LANES)), _sds((s, D_C))]
    accs = [_sds((S5_LANES, D_C)), _sds((S5_LANES, D_C)), _sds((D_C, D_C)), _sds((1, D_C)), _sds((1, D_MODEL))]
    return _row_call(name, body, s, [do, ag, ha, ob, hre, him, cu], [d, gn, wcr, wci, wglu], outs, accs)


def _log_f(name, f, bf):
    s = f[0].shape[0]

    def body(f_ref, b_ref, o_ref):
        o_ref[...] = jax.nn.log_sigmoid(f_ref[...] + b_ref[...])

    return _row_call(name, body, s, [f], [bf], [_sds((s, 128))], [])[0]


def _log_f_bwd(name, dlf, f, bf):
    s = dlf.shape[0]

    def body(dl_ref, f_ref, b_ref, df_ref, db_ref):
        _zero_at_first([db_ref])
        df = dl_ref[...] * jax.nn.sigmoid(-(f_ref[...] + b_ref[...]))
        df_ref[...] = df
        db_ref[...] += jnp.sum(df, axis=0, keepdims=True)

    return _row_call(name, body, s, [dlf, f], [bf], [_sds((s, 128))], [_sds((1, 128))])


def _s5_decay_grad(name, h_re, h_im, g_re, g_im):
    s = g_re.shape[0]
    tm = ROW_TILE

    def body(hr_ref, hi_ref, hhr_ref, hhi_ref, gr_ref, gi_ref, dr_ref, di_ref):
        i = pl.program_id(0)
        _zero_at_first([dr_ref, di_ref])

        def previous(h_ref, halo_ref):
            halo = jnp.where(i == 0, 0.0, halo_ref[...])
            return pltpu.roll(jnp.concatenate([halo, h_ref[...]], axis=0), 1, 0)[8:, :]

        hr, hi, gr, gi = previous(hr_ref, hhr_ref), previous(hi_ref, hhi_ref), gr_ref[...], gi_ref[...]
        dr_ref[...] += jnp.sum(hr * gr + hi * gi, axis=0, keepdims=True)
        di_ref[...] += jnp.sum(hr * gi - hi * gr, axis=0, keepdims=True)

    rows = pl.BlockSpec((tm, S5_LANES), lambda i: (i, 0))
    halo = pl.BlockSpec((8, S5_LANES), lambda i: (jnp.maximum(i * (tm // 8) - 1, 0), 0))
    acc = pl.BlockSpec((1, S5_LANES), lambda i: (0, 0))
    return pl.pallas_call(
        body,
        name=name,
        grid=(s // tm,),
        in_specs=[rows, rows, halo, halo, rows, rows],
        out_specs=[acc, acc],
        out_shape=[_sds((1, S5_LANES)), _sds((1, S5_LANES))],
        compiler_params=_params("arbitrary"),
    )(h_re, h_im, h_re, h_im, g_re, g_im)


def _conv_fwd(name, ax, w, b):
    s = ax.shape[0]
    tm = ROW_TILE

    def body(x_ref, halo_ref, w_ref, b_ref, o_ref):
        i = pl.program_id(0)
        x = x_ref[...]
        halo = jnp.where(i == 0, 0.0, halo_ref[...])
        ext = jnp.concatenate([halo, x], axis=0)
        acc = b_ref[...] + w_ref[3:4, :] * x
        for k in range(CONV_WIDTH - 1):
            acc = acc + w_ref[k:k + 1, :] * pltpu.roll(ext, CONV_WIDTH - 1 - k, 0)[8:, :]
        o_ref[...] = acc

    return pl.pallas_call(
        body,
        name=name,
        grid=(s // tm,),
        in_specs=[pl.BlockSpec((tm, D_A), lambda i: (i, 0)),
                  pl.BlockSpec((8, D_A), lambda i: (jnp.maximum(i * (tm // 8) - 1, 0), 0)),
                  pl.BlockSpec((CONV_WIDTH, D_A), lambda i: (0, 0)),
                  pl.BlockSpec((1, D_A), lambda i: (0, 0))],
        out_specs=pl.BlockSpec((tm, D_A), lambda i: (i, 0)),
        out_shape=_sds((s, D_A)),
        compiler_params=_params("arbitrary"),
    )(ax, ax, w, b)


def _conv_bwd(name, dxa, ax, w):
    s = ax.shape[0]
    tm = ROW_TILE
    nblk = s // tm

    def body(dx_ref, dnext_ref, x_ref, halo_ref, w_ref, dax_ref, dw_ref):
        i = pl.program_id(0)
        _zero_at_first([dw_ref])
        dx = dx_ref[...]
        dnext = jnp.where(i == nblk - 1, 0.0, dnext_ref[...])
        dext = jnp.concatenate([dx, dnext], axis=0)
        x = x_ref[...]
        halo = jnp.where(i == 0, 0.0, halo_ref[...])
        ext = jnp.concatenate([halo, x], axis=0)
        acc = w_ref[3:4, :] * dx
        dw_ref[3:4, :] += jnp.sum(dx * x, axis=0, keepdims=True)
        for k in range(CONV_WIDTH - 1):
            sh = CONV_WIDTH - 1 - k
            acc = acc + w_ref[k:k + 1, :] * pltpu.roll(dext, tm + 8 - sh, 0)[:tm, :]
            dw_ref[k:k + 1, :] += jnp.sum(dx * pltpu.roll(ext, sh, 0)[8:, :], axis=0, keepdims=True)
        dw_ref[4:5, :] += jnp.sum(dx, axis=0, keepdims=True)
        dax_ref[...] = acc

    return pl.pallas_call(
        body,
        name=name,
        grid=(nblk,),
        in_specs=[pl.BlockSpec((tm, D_A), lambda i: (i, 0)),
                  pl.BlockSpec((8, D_A), lambda i: (jnp.minimum((i + 1) * (tm // 8), s // 8 - 1), 0)),
                  pl.BlockSpec((tm, D_A), lambda i: (i, 0)),
                  pl.BlockSpec((8, D_A), lambda i: (jnp.maximum(i * (tm // 8) - 1, 0), 0)),
                  pl.BlockSpec((CONV_WIDTH, D_A), lambda i: (0, 0))],
        out_specs=[pl.BlockSpec((tm, D_A), lambda i: (i, 0)), pl.BlockSpec((8, D_A), lambda i: (0, 0))],
        out_shape=[_sds((s, D_A)), _sds((8, D_A))],
        compiler_params=_params("arbitrary"),
    )(dxa, dxa, ax, ax, w)


SCAN_ROWS = 512


def _row_in_tile(shape):
    return lax.broadcasted_iota(jnp.int32, shape, 0) % 8


def _lin_scan(name, a, b, reverse):
    s, c = a.shape
    t = min(SCAN_ROWS, s)
    nb = s // t

    def body(a_ref, b_ref, h_ref, p_ref, carry_ref):
        @pl.when(pl.program_id(0) == 0)
        def _():
            carry_ref[...] = jnp.zeros_like(carry_ref)

        row = _row_in_tile((t, c))
        p = a_ref[...]
        h = b_ref[...]
        for d in (1, 2, 4):
            keep = (row < 8 - d) if reverse else (row >= d)
            shift = (t - d) if reverse else d
            h = h + jnp.where(keep, p * pltpu.roll(h, shift, 0), 0.0)
            p = jnp.where(keep, p * pltpu.roll(p, shift, 0), p)
        h_ref[...] = h
        p_ref[...] = p
        edge = 0 if reverse else 7

        def tile(k, carry):
            kk = (t // 8 - 1 - k) if reverse else k
            r0 = pl.multiple_of(kk * 8, 8)
            hh = h_ref[pl.ds(r0, 8), :] + p_ref[pl.ds(r0, 8), :] * carry
            h_ref[pl.ds(r0, 8), :] = hh
            return jnp.broadcast_to(hh[edge:edge + 1, :], (8, c))

        carry_ref[...] = lax.fori_loop(0, t // 8, tile, carry_ref[...])

    spec = pl.BlockSpec((t, c), (lambda i: (nb - 1 - i, 0)) if reverse else (lambda i: (i, 0)))
    (out,) = _call(
        body,
        name=name,
        grid=(nb,),
        in_specs=[spec, spec],
        out_specs=[spec],
        out_shape=[_sds((s, c))],
        scratch_shapes=[pltpu.VMEM((t, c), f32), pltpu.VMEM((8, c), f32)],
        compiler_params=_params("arbitrary"),
    )(a, b)
    return out


def _s5_scan(name, b_re, b_im, a_re, a_im, reverse):
    s, c = b_re.shape
    t = min(SCAN_ROWS, s)
    nb = s // t

    def body(br_ref, bi_ref, ar_ref, ai_ref, hr_ref, hi_ref, cr_ref, ci_ref):
        @pl.when(pl.program_id(0) == 0)
        def _():
            cr_ref[...] = jnp.zeros_like(cr_ref)
            ci_ref[...] = jnp.zeros_like(ci_ref)

        ar1, ai1 = ar_ref[...], ai_ref[...]
        pows = [(ar1, ai1)]
        for _ in range(7):
            pr, pi = pows[-1]
            pows.append((pr * ar1 - pi * ai1, pr * ai1 + pi * ar1))
        row8 = lax.broadcasted_iota(jnp.int32, (8, c), 0)
        wr = jnp.zeros((8, c), f32)
        wi = jnp.zeros((8, c), f32)
        for r in range(8):
            pr, pi = pows[(7 - r) if reverse else r]
            wr = jnp.where(row8 == r, pr, wr)
            wi = jnp.where(row8 == r, pi, wi)
        row = _row_in_tile((t, c))
        hr = br_ref[...]
        hi = bi_ref[...]
        for d in (1, 2, 4):
            keep = (row < 8 - d) if reverse else (row >= d)
            shift = (t - d) if reverse else d
            pr, pi = pows[d - 1]
            cr = jnp.where(keep, pr, 0.0)
            ci = jnp.where(keep, pi, 0.0)
            sr = pltpu.roll(hr, shift, 0)
            si = pltpu.roll(hi, shift, 0)
            hr, hi = hr + cr * sr - ci * si, hi + cr * si + ci * sr
        hr_ref[...] = hr
        hi_ref[...] = hi
        edge = 0 if reverse else 7

        def tile(k, carry):
            car_r, car_i = carry
            kk = (t // 8 - 1 - k) if reverse else k
            r0 = pl.multiple_of(kk * 8, 8)
            xr = hr_ref[pl.ds(r0, 8), :] + wr * car_r - wi * car_i
            xi = hi_ref[pl.ds(r0, 8), :] + wr * car_i + wi * car_r
            hr_ref[pl.ds(r0, 8), :] = xr
            hi_ref[pl.ds(r0, 8), :] = xi
            return (jnp.broadcast_to(xr[edge:edge + 1, :], (8, c)), jnp.broadcast_to(xi[edge:edge + 1, :], (8, c)))

        car_r, car_i = lax.fori_loop(0, t // 8, tile, (cr_ref[...], ci_ref[...]))
        cr_ref[...] = car_r
        ci_ref[...] = car_i

    spec = pl.BlockSpec((t, c), (lambda i: (nb - 1 - i, 0)) if reverse else (lambda i: (i, 0)))
    vspec = pl.BlockSpec((1, c), lambda i: (0, 0))
    hr, hi = _call(
        body,
        name=name,
        grid=(nb,),
        in_specs=[spec, spec, vspec, vspec],
        out_specs=[spec, spec],
        out_shape=[_sds((s, c)), _sds((s, c))],
        scratch_shapes=[pltpu.VMEM((8, c), f32), pltpu.VMEM((8, c), f32)],
        compiler_params=_params("arbitrary"),
    )(b_re, b_im, a_re, a_im)
    return hr, hi


ATT_FEAT = 128
ATT_TQ = 1024
ATT_TK = 1024
ATT_TK_KEY_SIDE = 512


def _att_tiles(s, key_side=False):
    tq = min(ATT_TQ, s)
    tk = min(ATT_TK_KEY_SIDE if key_side else ATT_TK, tq)
    return tq, tk, tq // tk


def _keys_le_queries(tk, tq, k0, q0):
    row = lax.broadcasted_iota(jnp.int32, (tk, tq), 0) + k0
    col = lax.broadcasted_iota(jnp.int32, (tk, tq), 1) + q0
    return row <= col


def _attn_fwd_t(name, qt, k_aug, vt):
    h, s, _ = k_aug.shape
    tq, tk, ratio = _att_tiles(s)

    def body(qt_ref, k_ref, vt_ref, o_ref, lse_ref):
        qi = pl.program_id(1)
        qt = qt_ref[...]

        def block(kb, carry, masked):
            m, l, acc = carry
            ks = pl.multiple_of(kb * tk, tk)
            st = jnp.dot(k_ref[pl.ds(ks, tk), :], qt, preferred_element_type=f32)
            if masked:
                st = jnp.where(_keys_le_queries(tk, tq, ks, qi * tq), st, -jnp.inf)
            mn = jnp.maximum(m, jnp.max(st, axis=0, keepdims=True))
            p = jnp.exp(st - mn)
            al = jnp.exp(m - mn)
            l = al * l + jnp.sum(p, axis=0, keepdims=True)
            acc = al * acc + jnp.dot(vt_ref[kb], p.astype(bf16), preferred_element_type=f32)
            return mn, l, acc

        init = (jnp.full((1, tq), -jnp.inf, f32), jnp.zeros((1, tq), f32), jnp.zeros((HEAD_DIM, tq), f32))
        first = lax.fori_loop(0, qi * ratio, lambda kb, c: block(kb, c, False), init)
        m, l, acc = lax.fori_loop(qi * ratio, (qi + 1) * ratio, lambda kb, c: block(kb, c, True), first)
        o_ref[...] = acc / l
        lse_ref[...] = m + jnp.log(l)

    return _call(
        body,
        name=name,
        grid=(h, s // tq),
        in_specs=[pl.BlockSpec((None, None, ATT_FEAT, tq), lambda hh, i: (hh, i, 0, 0)),
                  pl.BlockSpec((None, s, ATT_FEAT), lambda hh, i: (hh, 0, 0)),
                  pl.BlockSpec((None, s // tk, HEAD_DIM, tk), lambda hh, i: (hh, 0, 0, 0))],
        out_specs=[pl.BlockSpec((None, HEAD_DIM, tq), lambda hh, i: (hh, 0, i)),
                   pl.BlockSpec((None, 1, tq), lambda hh, i: (hh, 0, i))],
        out_shape=[_sds((h, HEAD_DIM, s)), _sds((h, 1, s))],
        compiler_params=_params("parallel", "arbitrary"),
    )(qt, k_aug, vt)


def _attn_bwd_dq_t(name, qt, k_aug, v, kt, ot, dot_, lse):
    h, s, _ = k_aug.shape
    tq, tk, ratio = _att_tiles(s)

    def body(qt_ref, k_ref, v_ref, kt_ref, o_ref, do_ref, lse_ref, dq_ref, dl_ref):
        qi = pl.program_id(1)
        qt = qt_ref[...]
        dob = do_ref[...]
        delta = jnp.sum(dob.astype(f32) * o_ref[...], axis=0, keepdims=True)
        lse_v = lse_ref[...]

        def block(kb, carry, masked):
            dq, psum = carry
            ks = pl.multiple_of(kb * tk, tk)
            st = jnp.dot(k_ref[pl.ds(ks, tk), :], qt, preferred_element_type=f32)
            p = jnp.exp(st - lse_v)
            if masked:
                p = jnp.where(_keys_le_queries(tk, tq, ks, qi * tq), p, 0.0)
            dp = jnp.dot(v_ref[pl.ds(ks, tk), :], dob, preferred_element_type=f32)
            ds = p * (dp - delta)
            return (dq + jnp.dot(kt_ref[kb], ds.astype(bf16), preferred_element_type=f32),
                    psum + jnp.sum(p * dp, axis=0, keepdims=True))

        carry = lax.fori_loop(0, qi * ratio, lambda kb, c: block(kb, c, False),
                              (jnp.zeros((HEAD_DIM, tq), f32), jnp.zeros((1, tq), f32)))
        dq, psum = lax.fori_loop(qi * ratio, (qi + 1) * ratio, lambda kb, c: block(kb, c, True), carry)
        dq_ref[...] = dq * ATT_SCALE
        dl_ref[...] = psum

    qspec = pl.BlockSpec((None, HEAD_DIM, tq), lambda hh, i: (hh, 0, i))
    rspec = pl.BlockSpec((None, 1, tq), lambda hh, i: (hh, 0, i))
    return _call(
        body,
        name=name,
        grid=(h, s // tq),
        in_specs=[pl.BlockSpec((None, None, ATT_FEAT, tq), lambda hh, i: (hh, i, 0, 0)),
                  pl.BlockSpec((None, s, ATT_FEAT), lambda hh, i: (hh, 0, 0)),
                  pl.BlockSpec((None, s, HEAD_DIM), lambda hh, i: (hh, 0, 0)),
                  pl.BlockSpec((None, s // tk, HEAD_DIM, tk), lambda hh, i: (hh, 0, 0, 0)),
                  qspec, pl.BlockSpec((None, None, HEAD_DIM, tq), lambda hh, i: (hh, i, 0, 0)), rspec],
        out_specs=[qspec, rspec],
        out_shape=[_sds((h, HEAD_DIM, s)), _sds((h, 1, s))],
        compiler_params=_params("parallel", "arbitrary"),
    )(qt, k_aug, v, kt, ot, dot_, lse)


def _attn_bwd_dkv_t(name, qt_blocks, k_aug, v, qh, do, dot_blocks, lse, delta):
    h, s, _ = k_aug.shape
    tq, tk, ratio = _att_tiles(s, key_side=True)
    nq = s // tq

    def body(qt_ref, k_ref, v_ref, q_ref, do_ref, dot_ref, lse_ref, dl_ref, dk_ref, dv_ref, dck_ref, dsum_ref):
        kj = pl.program_id(1)
        kk = k_ref[...]
        vv = v_ref[...]
        dsum_ref[...] = jnp.zeros_like(dsum_ref)

        def block(qi, carry, masked):
            dk, dv = carry
            qs = pl.multiple_of(qi * tq, tq)
            st = jnp.dot(kk, qt_ref[qi], preferred_element_type=f32)
            p = jnp.exp(st - lse_ref[qi])
            if masked:
                p = jnp.where(_keys_le_queries(tk, tq, kj * tk, qs), p, 0.0)
            dv = dv + jnp.dot(p.astype(bf16), do_ref[pl.ds(qs, tq), :], preferred_element_type=f32)
            dp = jnp.dot(vv, dot_ref[qi], preferred_element_type=f32)
            ds = p * (dp - dl_ref[qi])
            dsum_ref[...] += ds
            dk = dk + jnp.dot(ds.astype(bf16), q_ref[pl.ds(qs, tq), :], preferred_element_type=f32)
            return dk, dv

        first = kj // ratio
        carry = block(first, (jnp.zeros((tk, HEAD_DIM), f32), jnp.zeros((tk, HEAD_DIM), f32)), True)
        dk, dv = lax.fori_loop(first + 1, nq, lambda qi, c: block(qi, c, False), carry)
        dk_ref[...] = dk
        dv_ref[...] = dv
        col = jnp.sum(dsum_ref[...], axis=1, keepdims=True)
        dck_ref[...] = -jnp.transpose(jnp.broadcast_to(col, (tk, 128)))[0:1, :]

    full = lambda shape: pl.BlockSpec((None,) + shape, lambda hh, j: (hh,) + (0,) * len(shape))
    kspec = pl.BlockSpec((None, tk, HEAD_DIM), lambda hh, j: (hh, j, 0))
    return _call(
        body,
        name=name,
        grid=(h, s // tk),
        in_specs=[full((nq, ATT_FEAT, tq)),
                  pl.BlockSpec((None, tk, ATT_FEAT), lambda hh, j: (hh, j, 0)),
                  kspec, full((s, HEAD_DIM)), full((s, HEAD_DIM)), full((nq, HEAD_DIM, tq)),
                  full((nq, 1, tq)), full((nq, 1, tq))],
        out_specs=[kspec, kspec, pl.BlockSpec((None, None, 1, tk), lambda hh, j: (hh, j, 0, 0))],
        out_shape=[_sds((h, s, HEAD_DIM)), _sds((h, s, HEAD_DIM)), _sds((h, s // tk, 1, tk))],
        scratch_shapes=[pltpu.VMEM((tk, tq), f32)],
        compiler_params=_params("parallel", "arbitrary"),
    )(qt_blocks, k_aug, v, qh, do, dot_blocks, lse, delta)


C_LANES = 128


def _selections():
    h = jnp.arange(N_HEADS)[:, None, None]
    row = jnp.arange(D_B + 3 * C_LANES)[None, :, None]
    col = jnp.arange(ATT_FEAT)[None, None, :]
    head_col = (row < D_B) & (row // HEAD_DIM == h) & (col == row % HEAD_DIM)

    def c_part(p, lane0):
        return (row == D_B + p * C_LANES + h) & (col == lane0 + p)

    c_q = c_part(0, HEAD_DIM) | c_part(1, HEAD_DIM) | c_part(2, HEAD_DIM)
    c_k = c_part(0, HEAD_DIM + 3) | c_part(1, HEAD_DIM + 3) | c_part(2, HEAD_DIM + 3)
    sel_q = (head_col | c_q).astype(bf16)
    sel_k = head_col.astype(bf16) - c_k.astype(bf16)
    sel_h = head_col[:, :D_B, :HEAD_DIM].astype(bf16)
    lane = jnp.arange(ATT_FEAT)
    ones_q = ((lane >= HEAD_DIM + 3) & (lane < HEAD_DIM + 6)).astype(f32)
    ones_k = ((lane >= HEAD_DIM) & (lane < HEAD_DIM + 3)).astype(f32)
    return dict(sel_qt=sel_q.transpose(0, 2, 1), sel_k=sel_k, sel_h=sel_h, sel_ht=sel_h.transpose(0, 2, 1),
                ones_q=ones_q.reshape(ATT_FEAT, 1), ones_k=ones_k.reshape(1, ATT_FEAT))


def _attn_prep(name, z, c, sel):
    s = z.shape[0]
    tq, tk, ratio = _att_tiles(s)

    def body(q_ref, k_ref, v_ref, c_ref, sqt_ref, sk_ref, sh_ref, sht_ref, oq_ref, ok_ref,
             qt_out, ka_out, kt_out, vt_out, v_out, qh_out):
        cv = c_ref[...]
        hi = cv.astype(bf16)
        r1 = cv - hi.astype(f32)
        mid = r1.astype(bf16)
        lo = (r1 - mid.astype(f32)).astype(bf16)
        qs = (q_ref[...] * ATT_SCALE).astype(bf16)
        kb = k_ref[...].astype(bf16)
        vb = v_ref[...].astype(bf16)
        xq = jnp.concatenate([qs, hi, mid, lo], axis=-1)
        xk = jnp.concatenate([kb, hi, mid, lo], axis=-1)
        for h in range(N_HEADS):
            qt = lax.dot_general(sqt_ref[h], xq, _DN["nt"], preferred_element_type=f32) + oq_ref[...]
            qt_out[h, 0] = qt.astype(bf16)
            ka_out[h] = (jnp.dot(xk, sk_ref[h], preferred_element_type=f32) + ok_ref[...]).astype(bf16)
            kt = lax.dot_general(sht_ref[h], kb, _DN["nt"], preferred_element_type=f32).astype(bf16)
            vt = lax.dot_general(sht_ref[h], vb, _DN["nt"], preferred_element_type=f32).astype(bf16)
            for j in range(ratio):
                kt_out[h, j] = kt[:, j * tk:(j + 1) * tk]
                vt_out[h, j] = vt[:, j * tk:(j + 1) * tk]
            v_out[h] = jnp.dot(vb, sh_ref[h], preferred_element_type=f32).astype(bf16)
            qh_out[h] = jnp.dot(qs, sh_ref[h], preferred_element_type=f32).astype(bf16)

    whole = lambda a: pl.BlockSpec(a.shape, lambda i, nd=a.ndim: (0,) * nd)
    consts = [sel["sel_qt"], sel["sel_k"], sel["sel_h"], sel["sel_ht"], sel["ones_q"], sel["ones_k"]]
    return pl.pallas_call(
        body,
        name=name,
        grid=(s // tq,),
        in_specs=[pl.BlockSpec((tq, D_B), lambda i: (i, 2)), pl.BlockSpec((tq, D_B), lambda i: (i, 3)),
                  pl.BlockSpec((tq, D_B), lambda i: (i, 4)), pl.BlockSpec((tq, C_LANES), lambda i: (i, 0))]
        + [whole(a) for a in consts],
        out_specs=[pl.BlockSpec((N_HEADS, 1, ATT_FEAT, tq), lambda i: (0, i, 0, 0)),
                   pl.BlockSpec((N_HEADS, tq, ATT_FEAT), lambda i: (0, i, 0)),
                   pl.BlockSpec((N_HEADS, ratio, HEAD_DIM, tk), lambda i: (0, i, 0, 0)),
                   pl.BlockSpec((N_HEADS, ratio, HEAD_DIM, tk), lambda i: (0, i, 0, 0)),
                   pl.BlockSpec((N_HEADS, tq, HEAD_DIM), lambda i: (0, i, 0)),
                   pl.BlockSpec((N_HEADS, tq, HEAD_DIM), lambda i: (0, i, 0))],
        out_shape=[_sds((N_HEADS, s // tq, ATT_FEAT, tq), bf16), _sds((N_HEADS, s, ATT_FEAT), bf16),
                   _sds((N_HEADS, s // tk, HEAD_DIM, tk), bf16), _sds((N_HEADS, s // tk, HEAD_DIM, tk), bf16),
                   _sds((N_HEADS, s, HEAD_DIM), bf16), _sds((N_HEADS, s, HEAD_DIM), bf16)],
        compiler_params=_params("parallel"),
    )(z, z, z, c, *consts)


def _attn_do_prep(name, dob, sel):
    s = dob.shape[0]
    tq = _att_tiles(s)[0]

    def body(do_ref, sh_ref, sht_ref, dot_out, do_out):
        db = do_ref[...].astype(bf16)
        for h in range(N_HEADS):
            dot_out[h, 0] = lax.dot_general(sht_ref[h], db, _DN["nt"], preferred_element_type=f32).astype(bf16)
            do_out[h] = jnp.dot(db, sh_ref[h], preferred_element_type=f32).astype(bf16)

    whole = lambda a: pl.BlockSpec(a.shape, lambda i, nd=a.ndim: (0,) * nd)
    return pl.pallas_call(
        body,
        name=name,
        grid=(s // tq,),
        in_specs=[pl.BlockSpec((tq, D_B), lambda i: (i, 0)), whole(sel["sel_h"]), whole(sel["sel_ht"])],
        out_specs=[pl.BlockSpec((N_HEADS, 1, HEAD_DIM, tq), lambda i: (0, i, 0, 0)),
                   pl.BlockSpec((N_HEADS, tq, HEAD_DIM), lambda i: (0, i, 0))],
        out_shape=[_sds((N_HEADS, s // tq, HEAD_DIM, tq), bf16), _sds((N_HEADS, s, HEAD_DIM), bf16)],
        compiler_params=_params("parallel"),
    )(dob, sel["sel_h"], sel["sel_ht"])


def _dz_assemble(name, dax, dag, dqt, dkh, dvh, df, dcu, sel):
    s = dax.shape[0]
    tm = _tile(s, 512)

    def body(dax_ref, dag_ref, dqt_ref, dk_ref, dv_ref, df_ref, dcu_ref, sht_ref, o_ref):
        dq = jnp.zeros((tm, D_B), f32)
        dk = jnp.zeros((tm, D_B), f32)
        dv = jnp.zeros((tm, D_B), f32)
        for h in range(N_HEADS):
            place = sht_ref[h]
            dq = dq + lax.dot_general(dqt_ref[h].astype(bf16), place, _DN["tn"], preferred_element_type=f32)
            dk = dk + jnp.dot(dk_ref[h].astype(bf16), place, preferred_element_type=f32)
            dv = dv + jnp.dot(dv_ref[h].astype(bf16), place, preferred_element_type=f32)
        pieces = [dax_ref[...], dag_ref[...], dq, dk, dv, df_ref[...], dcu_ref[...]]
        off = 0
        for p in pieces:
            o_ref[:, off:off + p.shape[1]] = p.astype(bf16)
            off += p.shape[1]

    rows = lambda c_: pl.BlockSpec((tm, c_), lambda i: (i, 0))
    heads = pl.BlockSpec((N_HEADS, tm, HEAD_DIM), lambda i: (0, i, 0))
    return pl.pallas_call(
        body,
        name=name,
        grid=(s // tm,),
        in_specs=[rows(D_A), rows(D_A), pl.BlockSpec((N_HEADS, HEAD_DIM, tm), lambda i: (0, 0, i)), heads, heads,
                  rows(128), rows(D_C), pl.BlockSpec(sel["sel_ht"].shape, lambda i: (0, 0, 0))],
        out_specs=rows(N_IN_P),
        out_shape=_sds((s, N_IN_P), bf16),
        compiler_params=_params("parallel"),
    )(dax, dag, dqt, dkh, dvh, df, dcu, sel["sel_ht"])


def _s5_disc_fn(are, aim, ldt):
    dt = jnp.exp(ldt)
    er = jnp.exp(are * dt)
    br = er * jnp.cos(aim * dt)
    bi = er * jnp.sin(aim * dt)
    nr = br - 1.0
    den = are * are + aim * aim
    return br, bi, (nr * are + bi * aim) / den, (bi * are - nr * aim) / den


def _s5_disc(name, are, aim, ldt):
    def body(a_ref, b_ref, c_ref, o0, o1, o2, o3):
        r = _s5_disc_fn(a_ref[...], b_ref[...], c_ref[...])
        o0[...], o1[...], o2[...], o3[...] = r

    shp = _sds((S5_GROUPS, S5_STATE))
    return pl.pallas_call(body, name=name, out_shape=[shp] * 4)(are, aim, ldt)


def _s5_disc_bwd(name, are, aim, ldt, cts):
    def body(a_ref, b_ref, c_ref, d0, d1, d2, d3, o0, o1, o2):
        _, vjp = jax.vjp(_s5_disc_fn, a_ref[...], b_ref[...], c_ref[...])
        o0[...], o1[...], o2[...] = vjp((d0[...], d1[...], d2[...], d3[...]))

    shp = _sds((S5_GROUPS, S5_STATE))
    return pl.pallas_call(body, name=name, out_shape=[shp, shp, _sds((S5_GROUPS, 1))])(are, aim, ldt, *cts)


def _adamw_rows(w, g, m, v):
    m = ADAM_B1 * m + (1.0 - ADAM_B1) * g
    v = ADAM_B2 * v + (1.0 - ADAM_B2) * (g * g)
    m_hat = m / (1.0 - ADAM_B1 ** ADAM_STEP)
    v_hat = v / (1.0 - ADAM_B2 ** ADAM_STEP)
    return -ADAM_LR * (m_hat / (jnp.sqrt(v_hat) + ADAM_EPS) + ADAM_WD * w), m, v


def _adamw(name, w, ga, gb, m, v):
    rows, cols = w.shape
    tr = _row_tile(rows)

    def body(w_ref, ga_ref, gb_ref, m_ref, v_ref, g_out, d_out, m_out, v_out):
        g = ga_ref[...] + gb_ref[...]
        d, mm, vv = _adamw_rows(w_ref[...], g, m_ref[...], v_ref[...])
        g_out[...] = g
        d_out[...] = d
        m_out[...] = mm
        v_out[...] = vv

    spec = pl.BlockSpec((tr, cols), lambda i: (i, 0))
    return pl.pallas_call(
        body, name=name, grid=(rows // tr,), in_specs=[spec] * 5, out_specs=[spec] * 4,
        out_shape=[_sds((rows, cols))] * 4, compiler_params=_params("parallel"),
    )(w, ga, gb, m, v)


def _sum_stack(name, st):
    n, rows, cols = st.shape
    tr = _row_tile(rows)

    def body(s_ref, o_ref):
        acc = s_ref[0].astype(f32)
        for j in range(1, n):
            acc = acc + s_ref[j].astype(f32)
        o_ref[...] = acc

    return pl.pallas_call(
        body, name=name, grid=(rows // tr,), in_specs=[pl.BlockSpec((n, tr, cols), lambda i: (0, i, 0))],
        out_specs=pl.BlockSpec((tr, cols), lambda i: (i, 0)), out_shape=_sds((rows, cols)),
        compiler_params=_params("parallel"),
    )(st)


def _block_diag(w):
    h, n, m = w.shape
    return jnp.einsum("hij,hg->higj", w, jnp.eye(h, dtype=w.dtype)).reshape(h * n, h * m)


def _block_diag_part(dense, h):
    n, m = dense.shape[0] // h, dense.shape[1] // h
    return jnp.einsum("higj,hg->hij", dense.reshape(h, n, h, m), jnp.eye(h, dtype=dense.dtype))


def _s5_matrices(coef_re, coef_im, b_re, b_im, c_re, c_im):
    bb_re = coef_re[:, :, None] * b_re - coef_im[:, :, None] * b_im
    bb_im = coef_re[:, :, None] * b_im + coef_im[:, :, None] * b_re
    wb_re = _block_diag(jnp.swapaxes(bb_re, 1, 2))
    wb_im = _block_diag(jnp.swapaxes(bb_im, 1, 2))
    wc_re = _block_diag(jnp.swapaxes(c_re, 1, 2))
    wc_im = _block_diag(jnp.swapaxes(-c_im, 1, 2))
    return wb_re, wb_im, wc_re, wc_im


def _shift_down(t):
    return jnp.concatenate([jnp.zeros((1, t.shape[1]), t.dtype), t[:-1]], axis=0)


def _shift_up(t):
    return jnp.concatenate([t[1:], jnp.zeros((1, t.shape[1]), t.dtype)], axis=0)


def _row(v):
    return v.reshape(1, -1)


def _ffn_fwd(tag, h, get, names, gamma, beta, gate_first=False):
    wg = get(names[0])
    if gate_first:
        g = _ffn_gate(tag + "_gate", h, wg)
        wu = get(names[1])
        u, act = _ffn_up_given_gate(tag + "_up", h, wu, g)
    else:
        wu = get(names[1])
        g, u, act = _ffn_up(tag + "_up", h, wg, wu)
    wd = get(names[2])
    r, out = _mm_ln(tag + "_down", act, wd, h, gamma, beta, 0.5, k_slabs=True)
    return out, dict(h=h, g=g, u=u, act=act, r=r, wg=wg, wu=wu, wd=wd)


def _ffn_bwd(tag, dout, sv, names, gamma, put, after_ln=None):
    s = dout.shape[0]
    dr, dgam, dbet = _ln_bwd(tag + "_lnb", sv["r"], dout, gamma)
    if after_ln is not None:
        after_ln(dgam, dbet)
    put(names[2], _mm_plain(tag + "_dwd", "tn", _Slabs(sv["act"]), dr, (D_FF, D_MODEL, s), scale=0.5, out_dtype=bf16,
                            tiles=(FF_SLAB, 1024, _tile(s, 2048))))
    dg, du = _ffn_dact(tag + "_dact", dr, sv["wd"], sv["g"], sv["u"])
    dwg, dwu = _mm2(tag + "_dwgu", "tn", (D_FF, D_MODEL, s), _Slabs(dg), sv["h"], _Slabs(du), None, separate=True,
                    out_dtype=bf16, tiles=(FF_SLAB, 1024, _tile(s, 2048)))
    put(names[0], dwg)
    put(names[1], dwu)
    slabs = range(dg.shape[0])
    dh = _mm(tag + "_dh", "nn", (s, D_MODEL, FF_SLAB), (_tile(s, 512), D_MODEL, FF_SLAB),
             [_KPart(dg, j) for j in slabs] + [_KPart(du, j) for j in slabs],
             [_KPart(sv["wg"], j) for j in slabs] + [_KPart(sv["wu"], j) for j in slabs],
             [(j, j, 0) for j in range(2 * len(slabs))], 1,
             lambda accs, extras, vecs: [accs[0] + ALPHA * extras[0]], [f32], extras=[dr])[0]
    return dh, dgam, dbet


def _mixer_fwd(tag, h1, w):
    s = h1.shape[0]
    z = _mm_plain(tag + "_win", "nn", h1, w["w_in"], (s, N_IN_P, D_MODEL), tiles=(_tile(s, 512), 768, D_MODEL))
    ag, f, cu_cols = (z, D_A, 1), (z, 128, F_OFF // 128), (z, D_C, CU_OFF // D_C)
    cu = z[:, CU_OFF:]
    xa = _conv_fwd(tag + "_conv", z, w["conv_w"], w["conv_b"])
    a, gated = _rg_gates(tag + "_gates", xa, w["rg_wa"], w["rg_wx"], w["rg_ba"], w["rg_bx"], w["rg_lam"])
    ha = _lin_scan(tag + "_rgscan", a, gated, False)
    ones = jnp.ones((s, 128), f32)
    c = _lin_scan(tag + "_cumf", ones, _log_f(tag + "_logf", f, w["fox_bf"]), False)
    att = dict(zip(("qt", "k_aug", "kt", "vt", "v", "qh"), _attn_prep(tag + "_attnprep", z, c, w["sel"])))
    ot, lse = _attn_fwd_t(tag + "_attn", att["qt"], att["k_aug"], att["vt"])
    ob = ot.reshape(D_B, s).T
    bu_re, bu_im = _mm2(tag + "_s5in", "nn", (s, S5_LANES, D_C), cu, w["wb_re"], None, w["wb_im"], separate=True,
                        tiles=(_tile(s, 512), 1024, D_C))
    hre, him = _s5_scan(tag + "_s5scan", bu_re, bu_im, w["abar_re"], w["abar_im"], False)
    o = _mix_out(tag + "_mixout", ag, ha, ob, hre, him, cu_cols, w["s5_d"], w["mix_g"], w["wc_re"], w["wc_im"],
                 w["w_glu"])
    sv = dict(h1=h1, z=z, ag=ag, f=f, cu=cu, cu_cols=cu_cols, xa=xa, a=a, ha=ha, att=att, ot=ot, lse=lse, ob=ob,
              hre=hre, him=him, o=o)
    return o, sv


def _mixer_bwd(tag, do, dr2, sv, w, put):
    s = do.shape[0]
    (dag, dha, dob, dhre, dhim, dcu1, dwcr, dwci, dwglu, dd, dgn) = _mix_out_bwd(
        tag + "_mixoutb", do, sv["ag"], sv["ha"], sv["ob"], sv["hre"], sv["him"], sv["cu_cols"], w["s5_d"], w["mix_g"],
        w["wc_re"], w["wc_im"], w["w_glu"])
    put("s5_w_glu", dwglu.astype(bf16))
    gre, gim = _s5_scan(tag + "_s5scanb", dhre, dhim, w["abar_re"], -w["abar_im"], True)
    dab_re, dab_im = _s5_decay_grad(tag + "_s5dec", sv["hre"], sv["him"], gre, gim)
    dwb_re, dwb_im = _mm2(tag + "_s5dwb", "tn", (D_C, S5_LANES, s), sv["cu"], gre, None, gim, separate=True,
                          tiles=(D_C, 1024, _tile(s, 1024)))
    dcu = _mm2(tag + "_s5dcu", "nt", (s, D_C, S5_LANES), gre, w["wb_re"], gim, w["wb_im"], add=dcu1,
               tiles=(_tile(s, 512), D_C, 1024))[0]
    att = sv["att"]
    tq = _att_tiles(s)[0]
    nt = s // tq
    dot_blocks, doh = _attn_do_prep(tag + "_doprep", dob, w["sel"])
    dqt, delta = _attn_bwd_dq_t(tag + "_attndq", att["qt"], att["k_aug"], att["v"], att["kt"], sv["ot"], dot_blocks,
                                sv["lse"])
    dkh, dvh, dck = _attn_bwd_dkv_t(tag + "_attndkv", att["qt"], att["k_aug"], att["v"], att["qh"], doh, dot_blocks,
                                    sv["lse"].reshape(N_HEADS, nt, 1, tq), delta.reshape(N_HEADS, nt, 1, tq))
    dc = jnp.pad(dck.reshape(N_HEADS, s).T, ((0, 0), (0, 128 - N_HEADS)))
    dlf = _lin_scan(tag + "_cumfb", jnp.ones((s, 128), f32), dc, True)
    df, dbf = _log_f_bwd(tag + "_logfb", dlf, sv["f"], w["fox_bf"])
    ga = _lin_scan(tag + "_rgscanb", _shift_up(sv["a"]), dha, True)
    dxa, dwa, dwx, dba, dbx, dlam = _rg_gates_bwd(tag + "_gatesb", sv["xa"], ga, _shift_down(sv["ha"]), w["rg_wa"],
                                                  w["rg_wx"], w["rg_ba"], w["rg_bx"], w["rg_lam"])
    dax, dconv = _conv_bwd(tag + "_convb", dxa, sv["z"], w["conv_w"])
    dz = _dz_assemble(tag + "_dz", dax, dag, dqt, dkh, dvh, df, dcu, w["sel"])
    put("w_in", _mm_plain(tag + "_dwin", "tn", sv["h1"], dz, (D_MODEL, N_IN_P, s), out_dtype=bf16,
                          tiles=(512, 768, _tile(s, 1024))))
    dh1 = _mm_plain(tag + "_dh1", "nt", dz, w["w_in"], (s, D_MODEL, N_IN_P), add=dr2, add_coef=ALPHA,
                    tiles=(_tile(s, 512), 1024, 768))
    grads = dict(dconv=dconv, dwa=dwa, dwx=dwx, dba=dba, dbx=dbx, dlam=dlam, dbf=dbf,
                 dab_re=dab_re, dab_im=dab_im, dwb_re=dwb_re, dwb_im=dwb_im, dwcr=dwcr, dwci=dwci, dd=dd, dgn=dgn)
    return dh1, grads


SMALL_NAMES = ["ln1_g", "ln1_b", "conv_w", "conv_b", "rg_w_a", "rg_b_a", "rg_w_x", "rg_b_x", "rg_lambda", "fox_b_f",
               "s5_a_re", "s5_a_im", "s5_log_dt", "s5_b_re", "s5_b_im", "s5_c_re", "s5_c_im", "s5_d", "mix_norm_g",
               "ln2_g", "ln2_b", "ln3_g", "ln3_b"]
BIG_NAMES = ["ffn1_w_gate", "ffn1_w_up", "ffn1_w_down", "w_in", "s5_w_glu", "w_out", "ffn2_w_gate", "ffn2_w_up",
             "ffn2_w_down"]


def _local_step(x, target, weight, small, on_grads, on_small):
    h = x
    saved = []
    sel = _selections()
    for l in range(DEPTH):
        get = functools.partial(weight, l)

        sm = {n: small[n][l] for n in SMALL_NAMES}
        abar_re, abar_im, coef_re, coef_im = _s5_disc(f"l{l}_s5disc", sm["s5_a_re"], sm["s5_a_im"],
                                                      sm["s5_log_dt"].reshape(S5_GROUPS, 1))
        mats, mats_vjp = jax.vjp(_s5_matrices, coef_re, coef_im, sm["s5_b_re"], sm["s5_b_im"], sm["s5_c_re"],
                                 sm["s5_c_im"])
        w = dict(
            sel=sel, conv_w=sm["conv_w"], conv_b=_row(sm["conv_b"]),
            rg_wa=_block_diag(sm["rg_w_a"]).astype(bf16), rg_wx=_block_diag(sm["rg_w_x"]).astype(bf16),
            rg_ba=_row(sm["rg_b_a"]), rg_bx=_row(sm["rg_b_x"]), rg_lam=_row(sm["rg_lambda"]),
            fox_bf=jnp.pad(_row(sm["fox_b_f"]), ((0, 0), (0, 128 - N_HEADS))),
            abar_re=_row(abar_re), abar_im=_row(abar_im),
            wb_re=mats[0].astype(bf16), wb_im=mats[1].astype(bf16), wc_re=mats[2].astype(bf16),
            wc_im=mats[3].astype(bf16), s5_d=_row(sm["s5_d"]), mix_g=_row(sm["mix_norm_g"]))
        h1, sv1 = _ffn_fwd(f"l{l}_ffn1", h, get, GROUPS["F1"], _row(sm["ln1_g"]), _row(sm["ln1_b"]),
                           gate_first=(l == 0))
        w["w_in"], w["w_glu"] = get("w_in"), get("s5_w_glu")
        o, svm = _mixer_fwd(f"l{l}_mix", h1, w)
        w_out = get("w_out")
        r2, h2 = _mm_ln(f"l{l}_wout", o, w_out, h1, _row(sm["ln2_g"]), _row(sm["ln2_b"]), 1.0)
        h3, sv2 = _ffn_fwd(f"l{l}_ffn2", h2, get, GROUPS["F2"], _row(sm["ln3_g"]), _row(sm["ln3_b"]))
        saved.append(dict(sm=sm, w=w, w_out=w_out, sv1=sv1, svm=svm, r2=r2, sv2=sv2, mats_vjp=mats_vjp))
        h = h3

    dh, loss_row = _loss_head("loss_head", h, target)
    s = x.shape[0]
    gsmall = {n: [None] * DEPTH for n in SMALL_NAMES}
    for l in reversed(range(DEPTH)):
        sd = saved[l]
        sm, w = sd["sm"], sd["w"]

        def put(name, grad, l=l):
            on_grads((l, name), grad)

        dh2, dgam, dbet = _ffn_bwd(f"l{l}_ffn2", dh, sd["sv2"], GROUPS["F2"], _row(sm["ln3_g"]), put)
        gsmall["ln3_g"][l], gsmall["ln3_b"][l] = dgam[0], dbet[0]
        dr2, dgam, dbet = _ln_bwd(f"l{l}_ln2b", sd["r2"], dh2, _row(sm["ln2_g"]))
        gsmall["ln2_g"][l], gsmall["ln2_b"][l] = dgam[0], dbet[0]
        put("w_out", _mm_plain(f"l{l}_dwout", "tn", sd["svm"]["o"], dr2, (D_MODEL, D_MODEL, s), out_dtype=bf16))
        do = _mm_plain(f"l{l}_do", "nt", dr2, sd["w_out"], (s, D_MODEL, D_MODEL))
        dh1, g = _mixer_bwd(f"l{l}_mix", do, dr2, sd["svm"], w, put)
        gsmall["conv_w"][l], gsmall["conv_b"][l] = g["dconv"][:CONV_WIDTH], g["dconv"][CONV_WIDTH]
        gsmall["rg_w_a"][l] = _block_diag_part(g["dwa"], N_HEADS)
        gsmall["rg_w_x"][l] = _block_diag_part(g["dwx"], N_HEADS)
        gsmall["rg_b_a"][l], gsmall["rg_b_x"][l], gsmall["rg_lambda"][l] = g["dba"][0], g["dbx"][0], g["dlam"][0]
        gsmall["fox_b_f"][l] = g["dbf"][0, :N_HEADS]
        dcoef_re, dcoef_im, db_re, db_im, dc_re, dc_im = sd["mats_vjp"]((g["dwb_re"], g["dwb_im"], g["dwcr"], g["dwci"]))
        da_re, da_im, dldt = _s5_disc_bwd(
            f"l{l}_s5discb", sm["s5_a_re"], sm["s5_a_im"], sm["s5_log_dt"].reshape(S5_GROUPS, 1),
            (g["dab_re"].reshape(S5_GROUPS, S5_STATE), g["dab_im"].reshape(S5_GROUPS, S5_STATE), dcoef_re, dcoef_im))
        gsmall["s5_a_re"][l], gsmall["s5_a_im"][l], gsmall["s5_log_dt"][l] = da_re, da_im, dldt[:, 0]
        gsmall["s5_b_re"][l], gsmall["s5_b_im"][l], gsmall["s5_c_re"][l], gsmall["s5_c_im"][l] = db_re, db_im, dc_re, dc_im
        gsmall["s5_d"][l], gsmall["mix_norm_g"][l] = g["dd"][0], g["dgn"][0]

        def after_ln(dgam, dbet, l=l):
            gsmall["ln1_g"][l], gsmall["ln1_b"][l] = dgam[0], dbet[0]
            if l == 0:
                on_small({n: jnp.stack(v) for n, v in gsmall.items()})

        dh, _, _ = _ffn_bwd(f"l{l}_ffn1", dh1, sd["sv1"], GROUPS["F1"], _row(sm["ln1_g"]), put, after_ln)
    return loss_row[0, 0], dh


def _position():
    return lax.axis_index("x"), lax.axis_index("y"), lax.axis_index("c")


_ANY = pl.BlockSpec(memory_space=pl.ANY)


COLUMN_SHARDED = ("ffn1_w_gate", "ffn1_w_up", "ffn2_w_gate", "ffn2_w_up")
PACK_QUANTUM = 128 * 256


def _permute_in_cols(w):
    pad = jnp.zeros(w.shape[:-1] + (128 - N_HEADS,), w.dtype)
    return jnp.concatenate([w[..., :F_OFF + N_HEADS], pad, w[..., F_OFF + N_HEADS:]], axis=-1)


def _unpermute_in_cols(w):
    return jnp.concatenate([w[..., :F_OFF + N_HEADS], w[..., CU_OFF:]], axis=-1)


def _pack(arrs):
    flat = jnp.concatenate([a.reshape(-1) for a in arrs])
    pad = -flat.shape[0] % PACK_QUANTUM
    return jnp.pad(flat, (0, pad)).reshape(-1, 128)


def _unpack(buf, shapes):
    flat = buf.reshape(-1)
    out, off = [], 0
    for shp in shapes:
        size = math.prod(shp)
        out.append(flat[off:off + size].reshape(shp))
        off += size
    return out


WEIGHT_NAMES = ["ffn1_w_gate", "ffn1_w_up", "ffn1_w_down", "ln1_g", "ln1_b", "w_in", "conv_w", "conv_b", "rg_w_a",
                "rg_b_a", "rg_w_x", "rg_b_x", "rg_lambda", "fox_b_f", "s5_a_re", "s5_a_im", "s5_log_dt", "s5_b_re",
                "s5_b_im", "s5_c_re", "s5_c_im", "s5_d", "s5_w_glu", "mix_norm_g", "w_out", "ln2_g", "ln2_b",
                "ffn2_w_gate", "ffn2_w_up", "ffn2_w_down", "ln3_g", "ln3_b"]


def _remote(src, dst, send_sems, recv_sems, k, peer):
    return pltpu.make_async_remote_copy(src_ref=src, dst_ref=dst, send_sem=send_sems.at[k], recv_sem=recv_sems.at[k],
                                        device_id=peer, device_id_type=MESH)


class _ChipGatherPart:
    def __init__(self, arrays):
        self.arrays, self.results = list(arrays), None

    def out_shape(self):
        return [_sds((N_CHIPS,) + a.shape, a.dtype) for a in self.arrays]

    def sems(self):
        n = len(self.arrays)
        return [pltpu.SemaphoreType.DMA((3 * n,)), pltpu.SemaphoreType.DMA((3 * n,)), pltpu.SemaphoreType.DMA((n,))]

    def copies(self, ins, outs, sems):
        send_sems, recv_sems, local_sems = sems
        x, y, c = _position()
        me = 2 * x + y
        local, sends, recvs = [], [], []
        for i, (src, dst) in enumerate(zip(ins, outs)):
            local.append(pltpu.make_async_copy(self.mine(src, me), dst.at[me], local_sems.at[i]))
            for r, (px, py) in enumerate([(1 - x, y), (x, 1 - y), (1 - x, 1 - y)]):
                peer = 2 * px + py
                sends.append(_remote(self.theirs(src, peer), dst.at[me], send_sems, recv_sems, 3 * i + r, (px, py, c)))
                recvs.append(_remote(self.mine(src, me), dst.at[peer], send_sems, recv_sems, 3 * i + r, (px, py, c)))
        return local, sends, recvs

    def mine(self, src, me):
        return src

    def theirs(self, src, peer):
        return src


class _ChipGatherHalvesPart(_ChipGatherPart):
    def sems(self):
        n = len(self.arrays)
        return super().sems() + [pltpu.SemaphoreType.DMA((3 * n,)), pltpu.SemaphoreType.DMA((3 * n,))]

    def _half(self, ref, which):
        rows = ref.shape[0] // 2
        return ref.at[pl.ds(which * rows, rows)]

    def copies(self, ins, outs, sems):
        send_sems, recv_sems, local_sems = sems[:3]
        x, y, c = _position()
        me = 2 * x + y
        local, sends, recvs = [], [], []
        for i, (src, dst) in enumerate(zip(ins, outs)):
            local.append(pltpu.make_async_copy(src, dst.at[me], local_sems.at[i]))
            for r, (px, py) in enumerate([(1 - x, y), (x, 1 - y), (1 - x, 1 - y)]):
                sends.append(_remote(self._half(src, c), self._half(dst.at[me], c), send_sems, recv_sems, 3 * i + r,
                                     (px, py, c)))
                recvs.append(_remote(self._half(src, c), self._half(dst.at[2 * px + py], c), send_sems, recv_sems,
                                     3 * i + r, (px, py, c)))
        return local, sends, recvs

    def forwards(self, ins, outs, sems):
        send_sems, recv_sems = sems[3:]
        x, y, c = _position()
        sends, recvs = [], []
        for i, dst in enumerate(outs):
            for r, (px, py) in enumerate([(1 - x, y), (x, 1 - y), (1 - x, 1 - y)]):
                slot = dst.at[2 * px + py]
                sends.append(_remote(self._half(slot, c), self._half(slot, c), send_sems, recv_sems, 3 * i + r,
                                     (x, y, 1 - c)))
                recvs.append(_remote(self._half(slot, c), self._half(slot, 1 - c), send_sems, recv_sems, 3 * i + r,
                                     (x, y, 1 - c)))
        return sends, recvs


class _ChipScatterPart(_ChipGatherPart):
    def out_shape(self):
        return [_sds(a.shape, a.dtype) for a in self.arrays]

    def mine(self, src, me):
        return src.at[me]

    def theirs(self, src, peer):
        return src.at[peer]


class _SiblingSwapPart:
    def __init__(self, arrays):
        self.arrays, self.results = list(arrays), None

    def out_shape(self):
        return [_sds(a.shape, a.dtype) for a in self.arrays]

    def sems(self):
        n = len(self.arrays)
        return [pltpu.SemaphoreType.DMA((n,)), pltpu.SemaphoreType.DMA((n,))]

    def copies(self, ins, outs, sems):
        x, y, c = _position()
        both = [_remote(src, dst, sems[0], sems[1], i, (x, y, 1 - c)) for i, (src, dst) in enumerate(zip(ins, outs))]
        return [], both, both


def _split_by(parts, refs, count):
    out, off = [], 0
    for p in parts:
        out.append(refs[off:off + count(p)])
        off += count(p)
    return out


def _parts_refs(parts, in_refs, out_refs, sem_refs):
    return zip(parts, _split_by(parts, in_refs, lambda p: len(p.arrays)),
               _split_by(parts, out_refs, lambda p: len(p.arrays)), _split_by(parts, sem_refs, lambda p: len(p.sems())))


def _exchange_start(parts, in_refs, out_refs, sem_refs):
    for part, ins, outs, sems in _parts_refs(parts, in_refs, out_refs, sem_refs):
        local, sends, _ = part.copies(ins, outs, sems)
        for cp in local + sends:
            cp.start()


def _exchange_finish(parts, in_refs, out_refs, sem_refs):
    split = list(_parts_refs(parts, in_refs, out_refs, sem_refs))
    copies = [part.copies(ins, outs, sems) for part, ins, outs, sems in split]
    for _, _, recvs in copies:
        for cp in recvs:
            cp.wait_recv()
    second = [part.forwards(ins, outs, sems) for part, ins, outs, sems in split if hasattr(part, "forwards")]
    for sends, _ in second:
        for cp in sends:
            cp.start()
    for sends, recvs in second:
        for cp in recvs:
            cp.wait_recv()
        for cp in sends:
            cp.wait_send()
    for local, sends, _ in copies:
        for cp in sends:
            cp.wait_send()
        for cp in local:
            cp.wait()


def _exchange_operands(parts):
    return ([a for p in parts for a in p.arrays], [s for p in parts for s in p.out_shape()],
            [s for p in parts for s in p.sems()])


def _set_results(parts, res):
    for part, outs in zip(parts, _split_by(parts, list(res), lambda p: len(p.arrays))):
        part.results = list(outs)


def _exchange_now(name, parts):
    x_in, x_out, x_sem = _exchange_operands(parts)
    n = len(x_in)

    def body(*refs):
        _exchange_start(parts, refs[:n], refs[n:2 * n], refs[2 * n:])
        _exchange_finish(parts, refs[:n], refs[n:2 * n], refs[2 * n:])

    res = pl.pallas_call(body, name=name, in_specs=[_ANY] * n, out_specs=[_ANY] * n, out_shape=x_out,
                         scratch_shapes=x_sem)(*x_in)
    _set_results(parts, res)


_RIDERS = {}


def _call(body, *, name, grid, in_specs, out_specs, out_shape, scratch_shapes=(), compiler_params=None):
    make_parts = _RIDERS.pop(name, None)
    if make_parts is None:
        return pl.pallas_call(body, name=name, grid=grid, in_specs=in_specs, out_specs=out_specs, out_shape=out_shape,
                              scratch_shapes=scratch_shapes, compiler_params=compiler_params)
    parts = make_parts()
    x_in, x_out, x_sem = _exchange_operands(parts)
    n_out, n_scr, n_x = len(out_shape), len(scratch_shapes), len(x_in)

    def run(*args):
        n_in = len(args)

        def hosted(*refs):
            ins, xi = refs[:n_in], refs[n_in:n_in + n_x]
            outs, xo = refs[n_in + n_x:n_in + n_x + n_out], refs[n_in + n_x + n_out:n_in + 2 * n_x + n_out]
            scr, xs = refs[n_in + 2 * n_x + n_out:n_in + 2 * n_x + n_out + n_scr], refs[n_in + 2 * n_x + n_out + n_scr:]
            first = functools.reduce(jnp.logical_and, [pl.program_id(d) == 0 for d in range(len(grid))])
            last = functools.reduce(jnp.logical_and, [pl.program_id(d) == grid[d] - 1 for d in range(len(grid))])

            @pl.when(first)
            def _():
                _exchange_start(parts, xi, xo, xs)

            body(*ins, *outs, *scr)

            @pl.when(last)
            def _():
                _exchange_finish(parts, xi, xo, xs)

        res = pl.pallas_call(
            hosted, name=name, grid=grid, in_specs=list(in_specs) + [_ANY] * n_x,
            out_specs=list(out_specs) + [_ANY] * n_x, out_shape=list(out_shape) + x_out,
            scratch_shapes=list(scratch_shapes) + x_sem, compiler_params=_params(*["arbitrary"] * len(grid)),
        )(*args, *x_in)
        _set_results(parts, res[n_out:])
        return list(res[:n_out])

    return run


GROUPS = {"F1": ["ffn1_w_gate", "ffn1_w_up", "ffn1_w_down"], "MX": ["w_in", "s5_w_glu", "w_out"],
          "F2": ["ffn2_w_gate", "ffn2_w_up", "ffn2_w_down"]}
FIRST_GATHER = [(0, "ffn1_w_gate")]
GATHER_HOSTS = {
    "l0_ffn1_gate": [(0, "ffn1_w_up")],
    "l0_ffn1_up": [(0, "ffn1_w_down")],
    "l0_ffn1_down": [(0, "w_in"), (0, "s5_w_glu"), (0, "w_out")],
    "l0_mix_attn": [(0, "ffn2_w_up"), (0, "ffn2_w_down"), (1, "w_in")],
    "l0_mix_s5scan": [(0, "ffn2_w_gate")],
    "l0_wout": [(1, "s5_w_glu"), (1, "w_out")],
    "l0_ffn2_up": [(1, "ffn1_w_gate")],
    "l0_ffn2_down": [(1, "ffn1_w_up")],
    "l1_ffn1_up": [(1, "ffn1_w_down")],
    "l1_mix_attn": [(1, "ffn2_w_up"), (1, "ffn2_w_down")],
    "l1_mix_s5scan": [(1, "ffn2_w_gate")],
}
SCATTER_HOSTS = {
    "l1_ffn2_dact": [(1, "ffn2_w_down")],
    "l1_mix_attndq": [(1, "ffn2_w_up")],
    "l1_mix_attndkv": [(1, "w_out"), (1, "s5_w_glu"), (1, "ffn2_w_gate")],
    "l1_ffn1_dact": [(1, "ffn1_w_down")],
    "l1_ffn1_dwgu": [(1, "w_in")],
    "l0_ffn2_dact": [(1, "ffn1_w_up")],
    "l0_ffn2_dwgu": [(0, "ffn2_w_down")],
    "l0_mix_attndq": [(0, "w_out"), (0, "s5_w_glu"), (0, "ffn2_w_up")],
    "l0_mix_attndkv": [(0, "ffn2_w_gate"), (1, "ffn1_w_gate")],
    "l0_mix_dh1": [(0, "w_in")],
    "l0_ffn1_dact": [(0, "ffn1_w_down")],
    "l0_ffn1_dh": [(0, "ffn1_w_gate")],
}
LAST_SCATTER = [(0, "ffn1_w_up")]
SMALL_HOST = "l0_ffn1_dwd"
SMALL_PACK_ORDER = [n for n in SMALL_NAMES if n != "conv_w"] + ["conv_w"]
TAIL_HOST = "l0_ffn1_dwgu"
LATE_SCATTER_HOST = "l0_ffn1_dh"


def _sharded_rows(name, a):
    return jnp.swapaxes(a, 1, 2) if name in COLUMN_SHARDED else a


def _unstack_layer(st):
    _, r, c = st.shape
    return st.reshape(N_CHIPS * r, c)


def _restack_layer(g):
    r, c = g.shape
    return g.reshape(N_CHIPS, r // N_CHIPS, c)


def _adamw_layer(name, layer, w, ga, gb, m, v, bufs):
    _, r, c = w.shape
    tr = _row_tile(r)

    def body(w_ref, ga_ref, gb_ref, m_ref, v_ref, *rest):
        g_out, d_out, m_out, v_out = rest[-4:]
        g = ga_ref[...] + gb_ref[...]
        d, mm, vv = _adamw_rows(w_ref[...], g, m_ref[...], v_ref[...])
        g_out[...] = g
        d_out[...] = d
        m_out[...] = mm
        v_out[...] = vv

    full = pl.BlockSpec((None, tr, c), lambda i: (layer, i, 0))
    flat = pl.BlockSpec((tr, c), lambda i: (i, 0))
    extra = {} if bufs is None else dict(input_output_aliases={5 + k: k for k in range(4)})
    return pl.pallas_call(
        body, name=name, grid=(r // tr,),
        in_specs=[full, flat, flat, full, full] + ([] if bufs is None else [_ANY] * 4),
        out_specs=[full] * 4, out_shape=[_sds(w.shape)] * 4, compiler_params=_params("parallel"), **extra,
    )(w, ga, gb, m, v, *([] if bufs is None else bufs))


def _train_step(x, loss_target, w, m, v):
    ix, iy, _ = _position()
    chip = 2 * ix + iy
    shard = {n: (_permute_in_cols(w[n]) if n == "w_in" else _sharded_rows(n, w[n])).astype(bf16) for n in BIG_NAMES}

    gathered = {}

    def gather_parts(keys, extra=()):
        part = _ChipGatherHalvesPart([shard[n][layer] for layer, n in keys] + list(extra))
        gathered.update({key: (part, i) for i, key in enumerate(keys)})
        return [part]

    (first,) = gather_parts(FIRST_GATHER, extra=[w["conv_w"]])
    _exchange_now("gather_first", [first])
    for host, keys in GATHER_HOSTS.items():
        _RIDERS[host] = functools.partial(gather_parts, keys)

    def weight(layer, name):
        part, i = gathered[(layer, name)]
        return _unstack_layer(part.results[i])

    small = {n: w[n] for n in SMALL_NAMES}
    small["conv_w"] = first.results[-1].transpose(1, 2, 0, 3).reshape(DEPTH, CONV_WIDTH, D_A)

    grads_full, scattered = {}, {}

    def scatter_parts(keys):
        part = _ChipScatterPart([_restack_layer(grads_full[key]) for key in keys])
        scattered.update({key: (part, i) for i, key in enumerate(keys)})
        return [part]

    for host, keys in SCATTER_HOSTS.items():
        _RIDERS[host] = functools.partial(scatter_parts, keys)

    partial = {}

    def reduce_chips(keys):
        for layer, n in keys:
            part, i = scattered[(layer, n)]
            p = _sum_stack(f"sum_l{layer}_{n}", part.results[i])
            partial[(layer, n)] = _unpermute_in_cols(p) if n == "w_in" else p

    early = [key for host, keys in SCATTER_HOSTS.items() if host != LATE_SCATTER_HOST for key in keys]
    late = SCATTER_HOSTS[LATE_SCATTER_HOST]
    tail = {}

    def small_parts():
        tail["small"] = _ChipGatherPart([_pack([tail["gsmall"][n] for n in SMALL_PACK_ORDER])])
        return [tail["small"]]

    def tail_parts():
        reduce_chips(early)
        tail["small_sum"] = _sum_stack("sum_small", tail["small"].results[0])
        tail["swap"] = _SiblingSwapPart([partial[k] for k in early] + [tail["small_sum"]])
        return [tail["swap"]]

    _RIDERS[SMALL_HOST] = small_parts
    _RIDERS[TAIL_HOST] = tail_parts
    loss_local, gx = _local_step(x[0], loss_target[0], weight, small, grads_full.__setitem__,
                                 functools.partial(tail.__setitem__, "gsmall"))
    other = dict(zip(early, tail["swap"].results[:-1]))
    small_mine, small_other = tail["small_sum"], tail["swap"].results[-1]
    reduce_chips(late)
    last_parts = scatter_parts(LAST_SCATTER) + [_SiblingSwapPart([partial[k] for k in late])]
    _exchange_now("exchange_last", last_parts)
    other.update(zip(late, last_parts[1].results))
    reduce_chips(LAST_SCATTER)
    swap_last = _SiblingSwapPart([partial[k] for k in LAST_SCATTER])
    _exchange_now("swap_last", [swap_last])
    other.update(zip(LAST_SCATTER, swap_last.results))

    grads, deltas, new_m, new_v = {}, {}, {}, {}
    for n in BIG_NAMES:
        bufs = None
        wr, mr, vr = (_sharded_rows(n, t) for t in (w[n], m[n], v[n]))
        for layer in range(DEPTH):
            bufs = _adamw_layer(f"adamw_l{layer}_{n}", layer, wr, partial[(layer, n)], other[(layer, n)], mr, vr, bufs)
        grads[n], deltas[n], new_m[n], new_v[n] = (_sharded_rows(n, t) for t in bufs)
    packed = SMALL_PACK_ORDER[:-1]
    shapes = [w[n].shape for n in packed]
    res = _adamw("adamw_small", _pack([w[n] for n in packed]), small_mine, small_other,
                 _pack([m[n] for n in packed]), _pack([v[n] for n in packed]))
    for dst, buf in zip((grads, deltas, new_m, new_v), res):
        dst.update(zip(packed, _unpack(buf, shapes)))
    cw = D_A // N_CHIPS
    conv_shape = (DEPTH, CONV_WIDTH, D_A)
    offset = sum(math.prod(s_) for s_ in shapes)

    def conv_grad(buf):
        full = buf.reshape(-1)[offset:offset + math.prod(conv_shape)].reshape(conv_shape)
        return lax.dynamic_slice_in_dim(full, chip * cw, cw, axis=2).reshape(DEPTH * CONV_WIDTH, cw)

    rows = lambda t: t.reshape(DEPTH * CONV_WIDTH, cw)
    res = _adamw("adamw_conv_w", rows(w["conv_w"]), conv_grad(small_mine), conv_grad(small_other),
                 rows(m["conv_w"]), rows(v["conv_w"]))
    for dst, buf in zip((grads, deltas, new_m, new_v), res):
        dst["conv_w"] = buf.reshape(w["conv_w"].shape)

    loss = lax.psum(loss_local, ("x", "y", "c"))
    return (loss, gx[None], *[grads[n] for n in WEIGHT_NAMES], *[deltas[n] for n in WEIGHT_NAMES],
            *[new_m[n] for n in WEIGHT_NAMES], *[new_v[n] for n in WEIGHT_NAMES])


def kernel(x, ffn1_w_gate, ffn1_w_up, ffn1_w_down, ln1_g, ln1_b, w_in, conv_w, conv_b, rg_w_a, rg_b_a, rg_w_x, rg_b_x, rg_lambda, fox_b_f, s5_a_re, s5_a_im, s5_log_dt, s5_b_re, s5_b_im, s5_c_re, s5_c_im, s5_d, s5_w_glu, mix_norm_g, w_out, ln2_g, ln2_b, ffn2_w_gate, ffn2_w_up, ffn2_w_down, ln3_g, ln3_b, loss_target, m_ffn1_w_gate, m_ffn1_w_up, m_ffn1_w_down, m_ln1_g, m_ln1_b, m_w_in, m_conv_w, m_conv_b, m_rg_w_a, m_rg_b_a, m_rg_w_x, m_rg_b_x, m_rg_lambda, m_fox_b_f, m_s5_a_re, m_s5_a_im, m_s5_log_dt, m_s5_b_re, m_s5_b_im, m_s5_c_re, m_s5_c_im, m_s5_d, m_s5_w_glu, m_mix_norm_g, m_w_out, m_ln2_g, m_ln2_b, m_ffn2_w_gate, m_ffn2_w_up, m_ffn2_w_down, m_ln3_g, m_ln3_b, v_ffn1_w_gate, v_ffn1_w_up, v_ffn1_w_down, v_ln1_g, v_ln1_b, v_w_in, v_conv_w, v_conv_b, v_rg_w_a, v_rg_b_a, v_rg_w_x, v_rg_b_x, v_rg_lambda, v_fox_b_f, v_s5_a_re, v_s5_a_im, v_s5_log_dt, v_s5_b_re, v_s5_b_im, v_s5_c_re, v_s5_c_im, v_s5_d, v_s5_w_glu, v_mix_norm_g, v_w_out, v_ln2_g, v_ln2_b, v_ffn2_w_gate, v_ffn2_w_up, v_ffn2_w_down, v_ln3_g, v_ln3_b):
    args = dict(locals())
    w = {n: args[n] for n in WEIGHT_NAMES}
    m = {n: args["m_" + n] for n in WEIGHT_NAMES}
    v = {n: args["v_" + n] for n in WEIGHT_NAMES}
    return _train_step(x, loss_target, w, m, v)
```

```python
import functools
import math

import jax
import jax.numpy as jnp
from jax import lax
from jax.experimental import pallas as pl
from jax.experimental.pallas import tpu as pltpu

f32 = jnp.float32
bf16 = jnp.bfloat16

D_MODEL = 1024
D_FF = 2816
D_A = 384
D_B = 384
D_C = 256
N_HEADS = 6
HEAD_DIM = 64
S5_GROUPS = 16
S5_STATE = 64
S5_LANES = S5_GROUPS * S5_STATE
F_OFF = 5 * D_A
CU_OFF = F_OFF + 128
N_IN_P = CU_OFF + D_C
CONV_WIDTH = 4
DEPTH = 2
ALPHA = (2 * DEPTH) ** 0.25
LN_EPS = 1e-5
RMS_EPS = 1e-6
RG_C = 8.0
ATT_SCALE = HEAD_DIM ** -0.5
ADAM_LR, ADAM_B1, ADAM_B2, ADAM_EPS, ADAM_WD, ADAM_STEP = 0.001, 0.9, 0.999, 1e-08, 0.01, 10

ROW_TILE = 512
N_CHIPS = 4
MESH = pl.DeviceIdType.MESH

_DN = {
    "nn": (((1,), (0,)), ((), ())),
    "nt": (((1,), (1,)), ((), ())),
    "tn": (((0,), (0,)), ((), ())),
}


def _sds(shape, dtype=f32):
    return jax.ShapeDtypeStruct(shape, dtype)


def _tile(n, target):
    best = None
    for t in range(128, min(n, target) + 1, 128):
        if n % t == 0:
            best = t
    return best or n


def _row_tile(rows, target=256):
    best = None
    for t in range(16, min(rows, target) + 1, 16):
        if rows % t == 0:
            best = t
    return best or rows


def _params(*sem):
    return pltpu.CompilerParams(dimension_semantics=sem)


class _Slabs:
    def __init__(self, x):
        self.x = x


class _KPart:
    def __init__(self, x, j):
        self.x, self.j = x, j


FF_SLAB = D_FF // N_CHIPS
FFN_ROWS = 1024

def _mm(name, mode, dims, tiles, a_list, b_list, pairs, n_acc, epilogue, outs, extras=(), vecs=(), split_cols=False):
    m, n, k = dims
    tm, tn, tk = tiles
    nk = k // tk
    na, nb, ne, nv, no = len(a_list), len(b_list), len(extras), len(vecs), len(outs)

    def body(*refs):
        a_refs = refs[:na]
        b_refs = refs[na:na + nb]
        e_refs = refs[na + nb:na + nb + ne]
        v_refs = refs[na + nb + ne:na + nb + ne + nv]
        o_refs = refs[na + nb + ne + nv:na + nb + ne + nv + no]
        acc_refs = refs[na + nb + ne + nv + no:]
        a_vals = [r[...].astype(bf16) for r in a_refs]
        b_vals = [r[...].astype(bf16) for r in b_refs]
        products = [(ci, lax.dot_general(a_vals[ai], b_vals[bi], _DN[mode], preferred_element_type=f32))
                    for ai, bi, ci in pairs]

        def finish(accs):
            res = epilogue(accs, [e[...] for e in e_refs], [v[...] for v in v_refs])
            for o, r in zip(o_refs, res):
                o[...] = r.astype(o.dtype)

        if nk == 1:
            accs = [None] * n_acc
            for ci, prod in products:
                accs[ci] = prod if accs[ci] is None else accs[ci] + prod
            finish(accs)
            return
        kk = pl.program_id(2)

        @pl.when(kk == 0)
        def _():
            for acc in acc_refs:
                acc[...] = jnp.zeros_like(acc)

        for ci, prod in products:
            acc_refs[ci][...] += prod

        @pl.when(kk == nk - 1)
        def _():
            finish([acc[...] for acc in acc_refs])

    def a_spec(a):
        if isinstance(a, _KPart):
            return pl.BlockSpec((None, tm, tk), lambda i, j, kk, part=a.j: (part, i, 0))
        if isinstance(a, _Slabs):
            if mode == "tn":
                return pl.BlockSpec((None, tk, tm), lambda i, j, kk: (i, kk, 0))
            return pl.BlockSpec((None, tm, tk), lambda i, j, kk: (kk, i, 0))
        if mode == "tn":
            return pl.BlockSpec((tk, tm), lambda i, j, kk: (kk, i))
        return pl.BlockSpec((tm, tk), lambda i, j, kk: (i, kk))

    def b_spec(b):
        if isinstance(b, _KPart):
            return pl.BlockSpec((tk, tn), lambda i, j, kk, part=b.j: (part, j))
        if isinstance(b, _Slabs):
            if mode == "nt":
                return pl.BlockSpec((None, tn, tk), lambda i, j, kk: (kk, j, 0))
            return pl.BlockSpec((None, tk, tn), lambda i, j, kk: (j, kk, 0))
        if mode == "nt":
            return pl.BlockSpec((tn, tk), lambda i, j, kk: (j, kk))
        return pl.BlockSpec((tk, tn), lambda i, j, kk: (kk, j))

    o_spec = pl.BlockSpec((tm, tn), lambda i, j, kk: (i, j))
    o_slab_spec = pl.BlockSpec((None, tm, tn), lambda i, j, kk: (j, i, 0))
    v_spec = pl.BlockSpec((1, tn), lambda i, j, kk: (0, j))
    if split_cols:
        out_specs = [o_slab_spec] * no
        out_shape = [_sds((n // tn, m, tn), dt) for dt in outs]
    else:
        out_specs = [o_spec] * no
        out_shape = [_sds((m, n), dt) for dt in outs]
    raw = lambda t: t.x if isinstance(t, (_Slabs, _KPart)) else t
    res = _call(
        body,
        name=name,
        grid=(m // tm, n // tn, nk),
        in_specs=([a_spec(a) for a in a_list] + [b_spec(b) for b in b_list]
                  + [o_slab_spec if isinstance(e, _Slabs) else o_spec for e in extras] + [v_spec] * nv),
        out_specs=out_specs,
        out_shape=out_shape,
        scratch_shapes=[pltpu.VMEM((tm, tn), f32)] * (n_acc if nk > 1 else 0),
        compiler_params=_params("parallel", "parallel", "arbitrary"),
    )(*map(raw, a_list), *map(raw, b_list), *map(raw, extras), *vecs)
    return res


def _sigmoid(x):
    return 0.5 * (jnp.tanh(0.5 * x) + 1.0)


def _layer_norm_rows(r, gamma, beta):
    mu = jnp.mean(r, axis=-1, keepdims=True)
    xc = r - mu
    var = jnp.mean(xc * xc, axis=-1, keepdims=True)
    return xc * lax.rsqrt(var + LN_EPS) * gamma + beta


def _mm_plain(name, mode, a, b, dims, scale=1.0, out_dtype=f32, add=None, add_coef=1.0, tiles=None):
    m, n, k = dims
    tiles = tiles or (_tile(m, 512), _tile(n, 1024), _tile(k, 1024))

    def epilogue(accs, extras, vecs):
        r = accs[0] if scale == 1.0 else accs[0] * scale
        if extras:
            r = r + add_coef * extras[0]
        return [r]

    return _mm(name, mode, dims, tiles, [a], [b], [(0, 0, 0)], 1, epilogue, [out_dtype],
               extras=[] if add is None else [add])[0]


def _ffn_up(name, h, wg, wu):
    s = h.shape[0]

    def epilogue(accs, extras, vecs):
        g, u = accs
        return [g, u, g * _sigmoid(g) * u]

    return _mm(name, "nt", (s, D_FF, D_MODEL), (_tile(s, FFN_ROWS), FF_SLAB, D_MODEL), [h], [wg, wu],
               [(0, 0, 0), (0, 1, 1)], 2, epilogue, [bf16, bf16, bf16], split_cols=True)


def _ffn_gate(name, h, wg):
    s = h.shape[0]
    return _mm(name, "nt", (s, D_FF, D_MODEL), (_tile(s, FFN_ROWS), FF_SLAB, D_MODEL), [h], [wg], [(0, 0, 0)], 1,
               lambda accs, extras, vecs: [accs[0]], [bf16], split_cols=True)[0]


def _ffn_up_given_gate(name, h, wu, g):
    s = h.shape[0]

    def epilogue(accs, extras, vecs):
        gg = extras[0].astype(f32)
        return [accs[0], gg * _sigmoid(gg) * accs[0]]

    return _mm(name, "nt", (s, D_FF, D_MODEL), (_tile(s, FFN_ROWS), FF_SLAB, D_MODEL), [h], [wu], [(0, 0, 0)], 1,
               epilogue, [bf16, bf16], extras=[_Slabs(g)], split_cols=True)


def _mm_ln(name, a, w, resid, gamma, beta, scale, k_slabs=False):
    def epilogue(accs, extras, vecs):
        r = ALPHA * extras[0] + scale * accs[0]
        return [r, _layer_norm_rows(r, vecs[0], vecs[1])]

    if k_slabs:
        n_slabs, s, slab = a.shape
        return _mm(name, "nn", (s, D_MODEL, slab), (_tile(s, 512), D_MODEL, slab),
                   [_KPart(a, j) for j in range(n_slabs)], [_KPart(w, j) for j in range(n_slabs)],
                   [(j, j, 0) for j in range(n_slabs)], 1, epilogue, [f32, f32], extras=[resid], vecs=[gamma, beta])
    s, k = a.shape
    return _mm(name, "nn", (s, D_MODEL, k), (_tile(s, FFN_ROWS), D_MODEL, _tile(k, 1024)),
               [a], [w], [(0, 0, 0)], 1, epilogue, [f32, f32], extras=[resid], vecs=[gamma, beta])


def _ffn_dact(name, dr, wd, g, u):
    s = dr.shape[0]

    def epilogue(accs, extras, vecs):
        da = 0.5 * accs[0]
        gg, uu = extras[0].astype(f32), extras[1].astype(f32)
        sg = _sigmoid(gg)
        return [da * uu * (sg * (1.0 + gg * (1.0 - sg))), da * (gg * sg)]

    return _mm(name, "nt", (s, D_FF, D_MODEL), (_tile(s, FFN_ROWS), FF_SLAB, D_MODEL), [dr], [wd],
               [(0, 0, 0)], 1, epilogue, [bf16, bf16], extras=[_Slabs(g), _Slabs(u)], split_cols=True)


def _mm2(name, mode, dims, a0, b0, a1, b1, add=None, add_coef=1.0, separate=False, tiles=None, out_dtype=f32,
         split_cols=False):
    m, n, k = dims
    tiles = tiles or (_tile(m, 512), _tile(n, 1024), _tile(k, 1024))

    def epilogue(accs, extras, vecs):
        if separate:
            return list(accs)
        r = accs[0]
        if extras:
            r = r + add_coef * extras[0]
        return [r]

    a_list = [a0] if a1 is None else [a0, a1]
    b_list = [b0] if b1 is None else [b0, b1]
    pairs = [(0, 0, 0), (len(a_list) - 1, len(b_list) - 1, 1 if separate else 0)]
    return _mm(name, mode, dims, tiles, a_list, b_list, pairs, 2 if separate else 1, epilogue,
               [out_dtype, out_dtype] if separate else [out_dtype], extras=[] if add is None else [add],
               split_cols=split_cols)


def _row_call(name, body, s, ins, params, outs, accs):
    tm = ROW_TILE
    ins = [a if isinstance(a, tuple) else (a, a.shape[1], 0) for a in ins]
    in_specs = [pl.BlockSpec((tm, width), lambda i, cb=cb: (i, cb)) for _, width, cb in ins]
    ins = [a for a, _, _ in ins]
    in_specs += [pl.BlockSpec(p.shape, lambda i, nd=p.ndim: (0,) * nd) for p in params]
    out_specs = [pl.BlockSpec((tm, o.shape[1]), lambda i: (i, 0)) for o in outs]
    out_specs += [pl.BlockSpec(a.shape, lambda i, nd=len(a.shape): (0,) * nd) for a in accs]
    return pl.pallas_call(
        body,
        name=name,
        grid=(s // tm,),
        in_specs=in_specs,
        out_specs=out_specs,
        out_shape=list(outs) + list(accs),
        compiler_params=_params("arbitrary"),
    )(*ins, *params)


def _zero_at_first(refs):
    @pl.when(pl.program_id(0) == 0)
    def _():
        for r in refs:
            r[...] = jnp.zeros_like(r)


def _ln_bwd(name, r, dh, gamma):
    s = r.shape[0]

    def body(r_ref, dh_ref, g_ref, dr_ref, dg_ref, db_ref):
        _zero_at_first([dg_ref, db_ref])
        rr = r_ref[...]
        dy = dh_ref[...]
        mu = jnp.mean(rr, axis=-1, keepdims=True)
        xc = rr - mu
        rstd = lax.rsqrt(jnp.mean(xc * xc, axis=-1, keepdims=True) + LN_EPS)
        xhat = xc * rstd
        dxh = dy * g_ref[...]
        dr_ref[...] = rstd * (dxh - jnp.mean(dxh, axis=-1, keepdims=True)
                              - xhat * jnp.mean(dxh * xhat, axis=-1, keepdims=True))
        dg_ref[...] += jnp.sum(dy * xhat, axis=0, keepdims=True)
        db_ref[...] += jnp.sum(dy, axis=0, keepdims=True)

    return _row_call(name, body, s, [r, dh], [gamma], [_sds((s, D_MODEL))], [_sds((1, D_MODEL)), _sds((1, D_MODEL))])


def _loss_head(name, y, target):
    s = y.shape[0]

    def body(y_ref, t_ref, dy_ref, l_ref):
        _zero_at_first([l_ref])
        e = y_ref[...] - t_ref[...]
        dy_ref[...] = e / D_MODEL
        l_ref[...] += 0.5 * jnp.sum(jnp.mean(e * e, axis=-1, keepdims=True), axis=0, keepdims=True)

    return _row_call(name, body, s, [y, target], [], [_sds((s, D_MODEL))], [_sds((1, 128))])


def _expm1(x):
    series = x * (1.0 + x / 2.0 * (1.0 + x / 3.0 * (1.0 + x / 4.0 * (1.0 + x / 5.0 * (1.0 + x / 6.0 * (1.0 + x / 7.0))))))
    return jnp.where(jnp.abs(x) < 0.25, series, jnp.exp(x) - 1.0)


def _gates_fn(xa, wa, wx, ba, bx, lam, tap_a, tap_x):
    xb = xa.astype(bf16)
    r = jax.nn.sigmoid(jnp.dot(xb, wa, preferred_element_type=f32) + ba + tap_a)
    i = jax.nn.sigmoid(jnp.dot(xb, wx, preferred_element_type=f32) + bx + tap_x)
    log_a = -RG_C * r * jax.nn.softplus(-lam)
    a = jnp.exp(log_a)
    gated = jnp.sqrt(-_expm1(2.0 * log_a)) * (i * xa)
    return a, gated


def _rg_gates(name, xa, wa, wx, ba, bx, lam):
    s = xa.shape[0]

    def body(xa_ref, wa_ref, wx_ref, ba_ref, bx_ref, lam_ref, a_ref, g_ref):
        a, g = _gates_fn(xa_ref[...], wa_ref[...], wx_ref[...], ba_ref[...], bx_ref[...], lam_ref[...], 0.0, 0.0)
        a_ref[...] = a
        g_ref[...] = g

    return _row_call(name, body, s, [xa], [wa, wx, ba, bx, lam], [_sds((s, D_A)), _sds((s, D_A))], [])


def _rg_gates_bwd(name, xa, ga, h_prev, wa, wx, ba, bx, lam):
    s = xa.shape[0]

    def body(xa_ref, ga_ref, hp_ref, wa_ref, wx_ref, ba_ref, bx_ref, lam_ref,
             dxa_ref, dwa_ref, dwx_ref, dba_ref, dbx_ref, dlam_ref):
        _zero_at_first([dwa_ref, dwx_ref, dba_ref, dbx_ref, dlam_ref])
        xa_v = xa_ref[...]
        zero = jnp.zeros((xa_v.shape[0], D_A), f32)
        fn = lambda x, ba_, bx_, lam_, ta, tx: _gates_fn(x, wa_ref[...], wx_ref[...], ba_, bx_, lam_, ta, tx)
        _, vjp = jax.vjp(fn, xa_v, ba_ref[...], bx_ref[...], lam_ref[...], zero, zero)
        gav = ga_ref[...]
        dxa, dba, dbx, dlam, dta, dtx = vjp((gav * hp_ref[...], gav))
        dxa_ref[...] = dxa
        xb = xa_v.astype(bf16)
        dwa_ref[...] += lax.dot_general(xb, dta.astype(bf16), _DN["tn"], preferred_element_type=f32)
        dwx_ref[...] += lax.dot_general(xb, dtx.astype(bf16), _DN["tn"], preferred_element_type=f32)
        dba_ref[...] += dba
        dbx_ref[...] += dbx
        dlam_ref[...] += dlam

    return _row_call(name, body, s, [xa, ga, h_prev], [wa, wx, ba, bx, lam], [_sds((s, D_A))],
                     [_sds((D_A, D_A)), _sds((D_A, D_A)), _sds((1, D_A)), _sds((1, D_A)), _sds((1, D_A))])


def _rms(v, g):
    return v * lax.rsqrt(jnp.mean(v * v, axis=-1, keepdims=True) + RMS_EPS) * g


def _mix_out_fn(ag, ha, ob, hre, him, cu, d, gn, tap_y, tap_gl, wcr, wci, wglu):
    out_a = jax.nn.gelu(ag) * ha
    y = (jnp.dot(hre.astype(bf16), wcr, preferred_element_type=f32)
         + jnp.dot(him.astype(bf16), wci, preferred_element_type=f32) + d * cu + tap_y)
    y2 = jax.nn.gelu(y)
    gl = jnp.dot(y2.astype(bf16), wglu, preferred_element_type=f32) + tap_gl
    out_c = y2 * jax.nn.sigmoid(gl)
    o = jnp.concatenate([_rms(out_a, gn[:, :D_A]), _rms(ob, gn[:, D_A:D_A + D_B]), _rms(out_c, gn[:, D_A + D_B:])],
                        axis=-1)
    return o, y2


def _mix_out(name, ag, ha, ob, hre, him, cu, d, gn, wcr, wci, wglu):
    s = ha.shape[0]

    def body(ag_ref, ha_ref, ob_ref, hre_ref, him_ref, cu_ref, d_ref, gn_ref, wcr_ref, wci_ref, wglu_ref, o_ref):
        o, _ = _mix_out_fn(ag_ref[...], ha_ref[...], ob_ref[...], hre_ref[...], him_ref[...], cu_ref[...], d_ref[...],
                           gn_ref[...], 0.0, 0.0, wcr_ref[...], wci_ref[...], wglu_ref[...])
        o_ref[...] = o.astype(o_ref.dtype)

    return _row_call(name, body, s, [ag, ha, ob, hre, him, cu], [d, gn, wcr, wci, wglu], [_sds((s, D_MODEL), bf16)], [])[0]


def _mix_out_bwd(name, do, ag, ha, ob, hre, him, cu, d, gn, wcr, wci, wglu):
    s = ha.shape[0]

    def body(do_ref, ag_ref, ha_ref, ob_ref, hre_ref, him_ref, cu_ref, d_ref, gn_ref, wcr_ref, wci_ref, wglu_ref,
             dag_ref, dha_ref, dob_ref, dhre_ref, dhim_ref, dcu_ref, dwcr_ref, dwci_ref, dwglu_ref, dd_ref, dgn_ref):
        _zero_at_first([dwcr_ref, dwci_ref, dwglu_ref, dd_ref, dgn_ref])
        tm = ag_ref.shape[0]
        zero = jnp.zeros((tm, D_C), f32)
        hre_v, him_v = hre_ref[...], him_ref[...]
        fn = lambda *a: _mix_out_fn(*a, wcr_ref[...], wci_ref[...], wglu_ref[...])
        _, vjp, y2 = jax.vjp(fn, ag_ref[...], ha_ref[...], ob_ref[...], hre_v, him_v, cu_ref[...], d_ref[...],
                             gn_ref[...], zero, zero, has_aux=True)
        dag, dha, dob, dhre, dhim, dcu, dd, dgn, dy, dgl = vjp(do_ref[...])
        dag_ref[...] = dag
        dha_ref[...] = dha
        dob_ref[...] = dob
        dhre_ref[...] = dhre
        dhim_ref[...] = dhim
        dcu_ref[...] = dcu
        dyb = dy.astype(bf16)
        dwcr_ref[...] += lax.dot_general(hre_v.astype(bf16), dyb, _DN["tn"], preferred_element_type=f32)
        dwci_ref[...] += lax.dot_general(him_v.astype(bf16), dyb, _DN["tn"], preferred_element_type=f32)
        dwglu_ref[...] += lax.dot_general(y2.astype(bf16), dgl.astype(bf16), _DN["tn"], preferred_element_type=f32)
        dd_ref[...] += dd
        dgn_ref[...] += dgn

    outs = [_sds((s, D_A)), _sds((s, D_A)), _sds((s, D_B)), _sds((s, S5_LANES)), _sds((s, S5_LANES)), _sds((s, D_C))]
    accs = [_sds((S5_LANES, D_C)), _sds((S5_LANES, D_C)), _sds((D_C, D_C)), _sds((1, D_C)), _sds((1, D_MODEL))]
    return _row_call(name, body, s, [do, ag, ha, ob, hre, him, cu], [d, gn, wcr, wci, wglu], outs, accs)


def _log_f(name, f, bf):
    s = f[0].shape[0]

    def body(f_ref, b_ref, o_ref):
        o_ref[...] = jax.nn.log_sigmoid(f_ref[...] + b_ref[...])

    return _row_call(name, body, s, [f], [bf], [_sds((s, 128))], [])[0]


def _log_f_bwd(name, dlf, f, bf):
    s = dlf.shape[0]

    def body(dl_ref, f_ref, b_ref, df_ref, db_ref):
        _zero_at_first([db_ref])
        df = dl_ref[...] * jax.nn.sigmoid(-(f_ref[...] + b_ref[...]))
        df_ref[...] = df
        db_ref[...] += jnp.sum(df, axis=0, keepdims=True)

    return _row_call(name, body, s, [dlf, f], [bf], [_sds((s, 128))], [_sds((1, 128))])


def _s5_decay_grad(name, h_re, h_im, g_re, g_im):
    s = g_re.shape[0]
    tm = ROW_TILE

    def body(hr_ref, hi_ref, hhr_ref, hhi_ref, gr_ref, gi_ref, dr_ref, di_ref):
        i = pl.program_id(0)
        _zero_at_first([dr_ref, di_ref])

        def previous(h_ref, halo_ref):
            halo = jnp.where(i == 0, 0.0, halo_ref[...])
            return pltpu.roll(jnp.concatenate([halo, h_ref[...]], axis=0), 1, 0)[8:, :]

        hr, hi, gr, gi = previous(hr_ref, hhr_ref), previous(hi_ref, hhi_ref), gr_ref[...], gi_ref[...]
        dr_ref[...] += jnp.sum(hr * gr + hi * gi, axis=0, keepdims=True)
        di_ref[...] += jnp.sum(hr * gi - hi * gr, axis=0, keepdims=True)

    rows = pl.BlockSpec((tm, S5_LANES), lambda i: (i, 0))
    halo = pl.BlockSpec((8, S5_LANES), lambda i: (jnp.maximum(i * (tm // 8) - 1, 0), 0))
    acc = pl.BlockSpec((1, S5_LANES), lambda i: (0, 0))
    return pl.pallas_call(
        body,
        name=name,
        grid=(s // tm,),
        in_specs=[rows, rows, halo, halo, rows, rows],
        out_specs=[acc, acc],
        out_shape=[_sds((1, S5_LANES)), _sds((1, S5_LANES))],
        compiler_params=_params("arbitrary"),
    )(h_re, h_im, h_re, h_im, g_re, g_im)


def _conv_fwd(name, ax, w, b):
    s = ax.shape[0]
    tm = ROW_TILE

    def body(x_ref, halo_ref, w_ref, b_ref, o_ref):
        i = pl.program_id(0)
        x = x_ref[...]
        halo = jnp.where(i == 0, 0.0, halo_ref[...])
        ext = jnp.concatenate([halo, x], axis=0)
        acc = b_ref[...] + w_ref[3:4, :] * x
        for k in range(CONV_WIDTH - 1):
            acc = acc + w_ref[k:k + 1, :] * pltpu.roll(ext, CONV_WIDTH - 1 - k, 0)[8:, :]
        o_ref[...] = acc

    return pl.pallas_call(
        body,
        name=name,
        grid=(s // tm,),
        in_specs=[pl.BlockSpec((tm, D_A), lambda i: (i, 0)),
                  pl.BlockSpec((8, D_A), lambda i: (jnp.maximum(i * (tm // 8) - 1, 0), 0)),
                  pl.BlockSpec((CONV_WIDTH, D_A), lambda i: (0, 0)),
                  pl.BlockSpec((1, D_A), lambda i: (0, 0))],
        out_specs=pl.BlockSpec((tm, D_A), lambda i: (i, 0)),
        out_shape=_sds((s, D_A)),
        compiler_params=_params("arbitrary"),
    )(ax, ax, w, b)


def _conv_bwd(name, dxa, ax, w):
    s = ax.shape[0]
    tm = ROW_TILE
    nblk = s // tm

    def body(dx_ref, dnext_ref, x_ref, halo_ref, w_ref, dax_ref, dw_ref):
        i = pl.program_id(0)
        _zero_at_first([dw_ref])
        dx = dx_ref[...]
        dnext = jnp.where(i == nblk - 1, 0.0, dnext_ref[...])
        dext = jnp.concatenate([dx, dnext], axis=0)
        x = x_ref[...]
        halo = jnp.where(i == 0, 0.0, halo_ref[...])
        ext = jnp.concatenate([halo, x], axis=0)
        acc = w_ref[3:4, :] * dx
        dw_ref[3:4, :] += jnp.sum(dx * x, axis=0, keepdims=True)
        for k in range(CONV_WIDTH - 1):
            sh = CONV_WIDTH - 1 - k
            acc = acc + w_ref[k:k + 1, :] * pltpu.roll(dext, tm + 8 - sh, 0)[:tm, :]
            dw_ref[k:k + 1, :] += jnp.sum(dx * pltpu.roll(ext, sh, 0)[8:, :], axis=0, keepdims=True)
        dw_ref[4:5, :] += jnp.sum(dx, axis=0, keepdims=True)
        dax_ref[...] = acc

    return pl.pallas_call(
        body,
        name=name,
        grid=(nblk,),
        in_specs=[pl.BlockSpec((tm, D_A), lambda i: (i, 0)),
                  pl.BlockSpec((8, D_A), lambda i: (jnp.minimum((i + 1) * (tm // 8), s // 8 - 1), 0)),
                  pl.BlockSpec((tm, D_A), lambda i: (i, 0)),
                  pl.BlockSpec((8, D_A), lambda i: (jnp.maximum(i * (tm // 8) - 1, 0), 0)),
                  pl.BlockSpec((CONV_WIDTH, D_A), lambda i: (0, 0))],
        out_specs=[pl.BlockSpec((tm, D_A), lambda i: (i, 0)), pl.BlockSpec((8, D_A), lambda i: (0, 0))],
        out_shape=[_sds((s, D_A)), _sds((8, D_A))],
        compiler_params=_params("arbitrary"),
    )(dxa, dxa, ax, ax, w)


SCAN_ROWS = 512


def _row_in_tile(shape):
    return lax.broadcasted_iota(jnp.int32, shape, 0) % 8


def _lin_scan(name, a, b, reverse):
    s, c = a.shape
    t = min(SCAN_ROWS, s)
    nb = s // t

    def body(a_ref, b_ref, h_ref, p_ref, carry_ref):
        @pl.when(pl.program_id(0) == 0)
        def _():
            carry_ref[...] = jnp.zeros_like(carry_ref)

        row = _row_in_tile((t, c))
        p = a_ref[...]
        h = b_ref[...]
        for d in (1, 2, 4):
            keep = (row < 8 - d) if reverse else (row >= d)
            shift = (t - d) if reverse else d
            h = h + jnp.where(keep, p * pltpu.roll(h, shift, 0), 0.0)
            p = jnp.where(keep, p * pltpu.roll(p, shift, 0), p)
        h_ref[...] = h
        p_ref[...] = p
        edge = 0 if reverse else 7

        def tile(k, carry):
            kk = (t // 8 - 1 - k) if reverse else k
            r0 = pl.multiple_of(kk * 8, 8)
            hh = h_ref[pl.ds(r0, 8), :] + p_ref[pl.ds(r0, 8), :] * carry
            h_ref[pl.ds(r0, 8), :] = hh
            return jnp.broadcast_to(hh[edge:edge + 1, :], (8, c))

        carry_ref[...] = lax.fori_loop(0, t // 8, tile, carry_ref[...])

    spec = pl.BlockSpec((t, c), (lambda i: (nb - 1 - i, 0)) if reverse else (lambda i: (i, 0)))
    (out,) = _call(
        body,
        name=name,
        grid=(nb,),
        in_specs=[spec, spec],
        out_specs=[spec],
        out_shape=[_sds((s, c))],
        scratch_shapes=[pltpu.VMEM((t, c), f32), pltpu.VMEM((8, c), f32)],
        compiler_params=_params("arbitrary"),
    )(a, b)
    return out


def _s5_scan(name, b_re, b_im, a_re, a_im, reverse):
    s, c = b_re.shape
    t = min(SCAN_ROWS, s)
    nb = s // t

    def body(br_ref, bi_ref, ar_ref, ai_ref, hr_ref, hi_ref, cr_ref, ci_ref):
        @pl.when(pl.program_id(0) == 0)
        def _():
            cr_ref[...] = jnp.zeros_like(cr_ref)
            ci_ref[...] = jnp.zeros_like(ci_ref)

        ar1, ai1 = ar_ref[...], ai_ref[...]
        pows = [(ar1, ai1)]
        for _ in range(7):
            pr, pi = pows[-1]
            pows.append((pr * ar1 - pi * ai1, pr * ai1 + pi * ar1))
        row8 = lax.broadcasted_iota(jnp.int32, (8, c), 0)
        wr = jnp.zeros((8, c), f32)
        wi = jnp.zeros((8, c), f32)
        for r in range(8):
            pr, pi = pows[(7 - r) if reverse else r]
            wr = jnp.where(row8 == r, pr, wr)
            wi = jnp.where(row8 == r, pi, wi)
        row = _row_in_tile((t, c))
        hr = br_ref[...]
        hi = bi_ref[...]
        for d in (1, 2, 4):
            keep = (row < 8 - d) if reverse else (row >= d)
            shift = (t - d) if reverse else d
            pr, pi = pows[d - 1]
            cr = jnp.where(keep, pr, 0.0)
            ci = jnp.where(keep, pi, 0.0)
            sr = pltpu.roll(hr, shift, 0)
            si = pltpu.roll(hi, shift, 0)
            hr, hi = hr + cr * sr - ci * si, hi + cr * si + ci * sr
        hr_ref[...] = hr
        hi_ref[...] = hi
        edge = 0 if reverse else 7

        def tile(k, carry):
            car_r, car_i = carry
            kk = (t // 8 - 1 - k) if reverse else k
            r0 = pl.multiple_of(kk * 8, 8)
            xr = hr_ref[pl.ds(r0, 8), :] + wr * car_r - wi * car_i
            xi = hi_ref[pl.ds(r0, 8), :] + wr * car_i + wi * car_r
            hr_ref[pl.ds(r0, 8), :] = xr
            hi_ref[pl.ds(r0, 8), :] = xi
            return (jnp.broadcast_to(xr[edge:edge + 1, :], (8, c)), jnp.broadcast_to(xi[edge:edge + 1, :], (8, c)))

        car_r, car_i = lax.fori_loop(0, t // 8, tile, (cr_ref[...], ci_ref[...]))
        cr_ref[...] = car_r
        ci_ref[...] = car_i

    spec = pl.BlockSpec((t, c), (lambda i: (nb - 1 - i, 0)) if reverse else (lambda i: (i, 0)))
    vspec = pl.BlockSpec((1, c), lambda i: (0, 0))
    hr, hi = _call(
        body,
        name=name,
        grid=(nb,),
        in_specs=[spec, spec, vspec, vspec],
        out_specs=[spec, spec],
        out_shape=[_sds((s, c)), _sds((s, c))],
        scratch_shapes=[pltpu.VMEM((8, c), f32), pltpu.VMEM((8, c), f32)],
        compiler_params=_params("arbitrary"),
    )(b_re, b_im, a_re, a_im)
    return hr, hi


ATT_FEAT = 128
ATT_TQ = 1024
ATT_TK = 1024
ATT_TK_KEY_SIDE = 512


def _att_tiles(s, key_side=False):
    tq = min(ATT_TQ, s)
    tk = min(ATT_TK_KEY_SIDE if key_side else ATT_TK, tq)
    return tq, tk, tq // tk


def _keys_le_queries(tk, tq, k0, q0):
    row = lax.broadcasted_iota(jnp.int32, (tk, tq), 0) + k0
    col = lax.broadcasted_iota(jnp.int32, (tk, tq), 1) + q0
    return row <= col


def _attn_fwd_t(name, qt, k_aug, vt):
    h, s, _ = k_aug.shape
    tq, tk, ratio = _att_tiles(s)

    def body(qt_ref, k_ref, vt_ref, o_ref, lse_ref):
        qi = pl.program_id(1)
        qt = qt_ref[...]

        def block(kb, carry, masked):
            m, l, acc = carry
            ks = pl.multiple_of(kb * tk, tk)
            st = jnp.dot(k_ref[pl.ds(ks, tk), :], qt, preferred_element_type=f32)
            if masked:
                st = jnp.where(_keys_le_queries(tk, tq, ks, qi * tq), st, -jnp.inf)
            mn = jnp.maximum(m, jnp.max(st, axis=0, keepdims=True))
            p = jnp.exp(st - mn)
            al = jnp.exp(m - mn)
            l = al * l + jnp.sum(p, axis=0, keepdims=True)
            acc = al * acc + jnp.dot(vt_ref[kb], p.astype(bf16), preferred_element_type=f32)
            return mn, l, acc

        init = (jnp.full((1, tq), -jnp.inf, f32), jnp.zeros((1, tq), f32), jnp.zeros((HEAD_DIM, tq), f32))
        first = lax.fori_loop(0, qi * ratio, lambda kb, c: block(kb, c, False), init)
        m, l, acc = lax.fori_loop(qi * ratio, (qi + 1) * ratio, lambda kb, c: block(kb, c, True), first)
        o_ref[...] = acc / l
        lse_ref[...] = m + jnp.log(l)

    return _call(
        body,
        name=name,
        grid=(h, s // tq),
        in_specs=[pl.BlockSpec((None, None, ATT_FEAT, tq), lambda hh, i: (hh, i, 0, 0)),
                  pl.BlockSpec((None, s, ATT_FEAT), lambda hh, i: (hh, 0, 0)),
                  pl.BlockSpec((None, s // tk, HEAD_DIM, tk), lambda hh, i: (hh, 0, 0, 0))],
        out_specs=[pl.BlockSpec((None, HEAD_DIM, tq), lambda hh, i: (hh, 0, i)),
                   pl.BlockSpec((None, 1, tq), lambda hh, i: (hh, 0, i))],
        out_shape=[_sds((h, HEAD_DIM, s)), _sds((h, 1, s))],
        compiler_params=_params("parallel", "arbitrary"),
    )(qt, k_aug, vt)


def _attn_bwd_dq_t(name, qt, k_aug, v, kt, ot, dot_, lse):
    h, s, _ = k_aug.shape
    tq, tk, ratio = _att_tiles(s)

    def body(qt_ref, k_ref, v_ref, kt_ref, o_ref, do_ref, lse_ref, dq_ref, dl_ref):
        qi = pl.program_id(1)
        qt = qt_ref[...]
        dob = do_ref[...]
        delta = jnp.sum(dob.astype(f32) * o_ref[...], axis=0, keepdims=True)
        lse_v = lse_ref[...]

        def block(kb, carry, masked):
            dq, psum = carry
            ks = pl.multiple_of(kb * tk, tk)
            st = jnp.dot(k_ref[pl.ds(ks, tk), :], qt, preferred_element_type=f32)
            p = jnp.exp(st - lse_v)
            if masked:
                p = jnp.where(_keys_le_queries(tk, tq, ks, qi * tq), p, 0.0)
            dp = jnp.dot(v_ref[pl.ds(ks, tk), :], dob, preferred_element_type=f32)
            ds = p * (dp - delta)
            return (dq + jnp.dot(kt_ref[kb], ds.astype(bf16), preferred_element_type=f32),
                    psum + jnp.sum(p * dp, axis=0, keepdims=True))

        carry = lax.fori_loop(0, qi * ratio, lambda kb, c: block(kb, c, False),
                              (jnp.zeros((HEAD_DIM, tq), f32), jnp.zeros((1, tq), f32)))
        dq, psum = lax.fori_loop(qi * ratio, (qi + 1) * ratio, lambda kb, c: block(kb, c, True), carry)
        dq_ref[...] = dq * ATT_SCALE
        dl_ref[...] = psum

    qspec = pl.BlockSpec((None, HEAD_DIM, tq), lambda hh, i: (hh, 0, i))
    rspec = pl.BlockSpec((None, 1, tq), lambda hh, i: (hh, 0, i))
    return _call(
        body,
        name=name,
        grid=(h, s // tq),
        in_specs=[pl.BlockSpec((None, None, ATT_FEAT, tq), lambda hh, i: (hh, i, 0, 0)),
                  pl.BlockSpec((None, s, ATT_FEAT), lambda hh, i: (hh, 0, 0)),
                  pl.BlockSpec((None, s, HEAD_DIM), lambda hh, i: (hh, 0, 0)),
                  pl.BlockSpec((None, s // tk, HEAD_DIM, tk), lambda hh, i: (hh, 0, 0, 0)),
                  qspec, pl.BlockSpec((None, None, HEAD_DIM, tq), lambda hh, i: (hh, i, 0, 0)), rspec],
        out_specs=[qspec, rspec],
        out_shape=[_sds((h, HEAD_DIM, s)), _sds((h, 1, s))],
        compiler_params=_params("parallel", "arbitrary"),
    )(qt, k_aug, v, kt, ot, dot_, lse)


def _attn_bwd_dkv_t(name, qt_blocks, k_aug, v, qh, do, dot_blocks, lse, delta):
    h, s, _ = k_aug.shape
    tq, tk, ratio = _att_tiles(s, key_side=True)
    nq = s // tq

    def body(qt_ref, k_ref, v_ref, q_ref, do_ref, dot_ref, lse_ref, dl_ref, dk_ref, dv_ref, dck_ref, dsum_ref):
        kj = pl.program_id(1)
        kk = k_ref[...]
        vv = v_ref[...]
        dsum_ref[...] = jnp.zeros_like(dsum_ref)

        def block(qi, carry, masked):
            dk, dv = carry
            qs = pl.multiple_of(qi * tq, tq)
            st = jnp.dot(kk, qt_ref[qi], preferred_element_type=f32)
            p = jnp.exp(st - lse_ref[qi])
            if masked:
                p = jnp.where(_keys_le_queries(tk, tq, kj * tk, qs), p, 0.0)
            dv = dv + jnp.dot(p.astype(bf16), do_ref[pl.ds(qs, tq), :], preferred_element_type=f32)
            dp = jnp.dot(vv, dot_ref[qi], preferred_element_type=f32)
            ds = p * (dp - dl_ref[qi])
            dsum_ref[...] += ds
            dk = dk + jnp.dot(ds.astype(bf16), q_ref[pl.ds(qs, tq), :], preferred_element_type=f32)
            return dk, dv

        first = kj // ratio
        carry = block(first, (jnp.zeros((tk, HEAD_DIM), f32), jnp.zeros((tk, HEAD_DIM), f32)), True)
        dk, dv = lax.fori_loop(first + 1, nq, lambda qi, c: block(qi, c, False), carry)
        dk_ref[...] = dk
        dv_ref[...] = dv
        col = jnp.sum(dsum_ref[...], axis=1, keepdims=True)
        dck_ref[...] = -jnp.transpose(jnp.broadcast_to(col, (tk, 128)))[0:1, :]

    full = lambda shape: pl.BlockSpec((None,) + shape, lambda hh, j: (hh,) + (0,) * len(shape))
    kspec = pl.BlockSpec((None, tk, HEAD_DIM), lambda hh, j: (hh, j, 0))
    return _call(
        body,
        name=name,
        grid=(h, s // tk),
        in_specs=[full((nq, ATT_FEAT, tq)),
                  pl.BlockSpec((None, tk, ATT_FEAT), lambda hh, j: (hh, j, 0)),
                  kspec, full((s, HEAD_DIM)), full((s, HEAD_DIM)), full((nq, HEAD_DIM, tq)),
                  full((nq, 1, tq)), full((nq, 1, tq))],
        out_specs=[kspec, kspec, pl.BlockSpec((None, None, 1, tk), lambda hh, j: (hh, j, 0, 0))],
        out_shape=[_sds((h, s, HEAD_DIM)), _sds((h, s, HEAD_DIM)), _sds((h, s // tk, 1, tk))],
        scratch_shapes=[pltpu.VMEM((tk, tq), f32)],
        compiler_params=_params("parallel", "arbitrary"),
    )(qt_blocks, k_aug, v, qh, do, dot_blocks, lse, delta)


C_LANES = 128


def _selections():
    h = jnp.arange(N_HEADS)[:, None, None]
    row = jnp.arange(D_B + 3 * C_LANES)[None, :, None]
    col = jnp.arange(ATT_FEAT)[None, None, :]
    head_col = (row < D_B) & (row // HEAD_DIM == h) & (col == row % HEAD_DIM)

    def c_part(p, lane0):
        return (row == D_B + p * C_LANES + h) & (col == lane0 + p)

    c_q = c_part(0, HEAD_DIM) | c_part(1, HEAD_DIM) | c_part(2, HEAD_DIM)
    c_k = c_part(0, HEAD_DIM + 3) | c_part(1, HEAD_DIM + 3) | c_part(2, HEAD_DIM + 3)
    sel_q = (head_col | c_q).astype(bf16)
    sel_k = head_col.astype(bf16) - c_k.astype(bf16)
    sel_h = head_col[:, :D_B, :HEAD_DIM].astype(bf16)
    lane = jnp.arange(ATT_FEAT)
    ones_q = ((lane >= HEAD_DIM + 3) & (lane < HEAD_DIM + 6)).astype(f32)
    ones_k = ((lane >= HEAD_DIM) & (lane < HEAD_DIM + 3)).astype(f32)
    return dict(sel_qt=sel_q.transpose(0, 2, 1), sel_k=sel_k, sel_h=sel_h, sel_ht=sel_h.transpose(0, 2, 1),
                ones_q=ones_q.reshape(ATT_FEAT, 1), ones_k=ones_k.reshape(1, ATT_FEAT))


def _attn_prep(name, z, c, sel):
    s = z.shape[0]
    tq, tk, ratio = _att_tiles(s)

    def body(q_ref, k_ref, v_ref, c_ref, sqt_ref, sk_ref, sh_ref, sht_ref, oq_ref, ok_ref,
             qt_out, ka_out, kt_out, vt_out, v_out, qh_out):
        cv = c_ref[...]
        hi = cv.astype(bf16)
        r1 = cv - hi.astype(f32)
        mid = r1.astype(bf16)
        lo = (r1 - mid.astype(f32)).astype(bf16)
        qs = (q_ref[...] * ATT_SCALE).astype(bf16)
        kb = k_ref[...].astype(bf16)
        vb = v_ref[...].astype(bf16)
        xq = jnp.concatenate([qs, hi, mid, lo], axis=-1)
        xk = jnp.concatenate([kb, hi, mid, lo], axis=-1)
        for h in range(N_HEADS):
            qt = lax.dot_general(sqt_ref[h], xq, _DN["nt"], preferred_element_type=f32) + oq_ref[...]
            qt_out[h, 0] = qt.astype(bf16)
            ka_out[h] = (jnp.dot(xk, sk_ref[h], preferred_element_type=f32) + ok_ref[...]).astype(bf16)
            kt = lax.dot_general(sht_ref[h], kb, _DN["nt"], preferred_element_type=f32).astype(bf16)
            vt = lax.dot_general(sht_ref[h], vb, _DN["nt"], preferred_element_type=f32).astype(bf16)
            for j in range(ratio):
                kt_out[h, j] = kt[:, j * tk:(j + 1) * tk]
                vt_out[h, j] = vt[:, j * tk:(j + 1) * tk]
            v_out[h] = jnp.dot(vb, sh_ref[h], preferred_element_type=f32).astype(bf16)
            qh_out[h] = jnp.dot(qs, sh_ref[h], preferred_element_type=f32).astype(bf16)

    whole = lambda a: pl.BlockSpec(a.shape, lambda i, nd=a.ndim: (0,) * nd)
    consts = [sel["sel_qt"], sel["sel_k"], sel["sel_h"], sel["sel_ht"], sel["ones_q"], sel["ones_k"]]
    return pl.pallas_call(
        body,
        name=name,
        grid=(s // tq,),
        in_specs=[pl.BlockSpec((tq, D_B), lambda i: (i, 2)), pl.BlockSpec((tq, D_B), lambda i: (i, 3)),
                  pl.BlockSpec((tq, D_B), lambda i: (i, 4)), pl.BlockSpec((tq, C_LANES), lambda i: (i, 0))]
        + [whole(a) for a in consts],
        out_specs=[pl.BlockSpec((N_HEADS, 1, ATT_FEAT, tq), lambda i: (0, i, 0, 0)),
                   pl.BlockSpec((N_HEADS, tq, ATT_FEAT), lambda i: (0, i, 0)),
                   pl.BlockSpec((N_HEADS, ratio, HEAD_DIM, tk), lambda i: (0, i, 0, 0)),
                   pl.BlockSpec((N_HEADS, ratio, HEAD_DIM, tk), lambda i: (0, i, 0, 0)),
                   pl.BlockSpec((N_HEADS, tq, HEAD_DIM), lambda i: (0, i, 0)),
                   pl.BlockSpec((N_HEADS, tq, HEAD_DIM), lambda i: (0, i, 0))],
        out_shape=[_sds((N_HEADS, s // tq, ATT_FEAT, tq), bf16), _sds((N_HEADS, s, ATT_FEAT), bf16),
                   _sds((N_HEADS, s // tk, HEAD_DIM, tk), bf16), _sds((N_HEADS, s // tk, HEAD_DIM, tk), bf16),
                   _sds((N_HEADS, s, HEAD_DIM), bf16), _sds((N_HEADS, s, HEAD_DIM), bf16)],
        compiler_params=_params("parallel"),
    )(z, z, z, c, *consts)


def _attn_do_prep(name, dob, sel):
    s = dob.shape[0]
    tq = _att_tiles(s)[0]

    def body(do_ref, sh_ref, sht_ref, dot_out, do_out):
        db = do_ref[...].astype(bf16)
        for h in range(N_HEADS):
            dot_out[h, 0] = lax.dot_general(sht_ref[h], db, _DN["nt"], preferred_element_type=f32).astype(bf16)
            do_out[h] = jnp.dot(db, sh_ref[h], preferred_element_type=f32).astype(bf16)

    whole = lambda a: pl.BlockSpec(a.shape, lambda i, nd=a.ndim: (0,) * nd)
    return pl.pallas_call(
        body,
        name=name,
        grid=(s // tq,),
        in_specs=[pl.BlockSpec((tq, D_B), lambda i: (i, 0)), whole(sel["sel_h"]), whole(sel["sel_ht"])],
        out_specs=[pl.BlockSpec((N_HEADS, 1, HEAD_DIM, tq), lambda i: (0, i, 0, 0)),
                   pl.BlockSpec((N_HEADS, tq, HEAD_DIM), lambda i: (0, i, 0))],
        out_shape=[_sds((N_HEADS, s // tq, HEAD_DIM, tq), bf16), _sds((N_HEADS, s, HEAD_DIM), bf16)],
        compiler_params=_params("parallel"),
    )(dob, sel["sel_h"], sel["sel_ht"])


def _dz_assemble(name, dax, dag, dqt, dkh, dvh, df, dcu, sel):
    s = dax.shape[0]
    tm = _tile(s, 512)

    def body(dax_ref, dag_ref, dqt_ref, dk_ref, dv_ref, df_ref, dcu_ref, sht_ref, o_ref):
        dq = jnp.zeros((tm, D_B), f32)
        dk = jnp.zeros((tm, D_B), f32)
        dv = jnp.zeros((tm, D_B), f32)
        for h in range(N_HEADS):
            place = sht_ref[h]
            dq = dq + lax.dot_general(dqt_ref[h].astype(bf16), place, _DN["tn"], preferred_element_type=f32)
            dk = dk + jnp.dot(dk_ref[h].astype(bf16), place, preferred_element_type=f32)
            dv = dv + jnp.dot(dv_ref[h].astype(bf16), place, preferred_element_type=f32)
        pieces = [dax_ref[...], dag_ref[...], dq, dk, dv, df_ref[...], dcu_ref[...]]
        off = 0
        for p in pieces:
            o_ref[:, off:off + p.shape[1]] = p.astype(bf16)
            off += p.shape[1]

    rows = lambda c_: pl.BlockSpec((tm, c_), lambda i: (i, 0))
    heads = pl.BlockSpec((N_HEADS, tm, HEAD_DIM), lambda i: (0, i, 0))
    return pl.pallas_call(
        body,
        name=name,
        grid=(s // tm,),
        in_specs=[rows(D_A), rows(D_A), pl.BlockSpec((N_HEADS, HEAD_DIM, tm), lambda i: (0, 0, i)), heads, heads,
                  rows(128), rows(D_C), pl.BlockSpec(sel["sel_ht"].shape, lambda i: (0, 0, 0))],
        out_specs=rows(N_IN_P),
        out_shape=_sds((s, N_IN_P), bf16),
        compiler_params=_params("parallel"),
    )(dax, dag, dqt, dkh, dvh, df, dcu, sel["sel_ht"])


def _s5_disc_fn(are, aim, ldt):
    dt = jnp.exp(ldt)
    er = jnp.exp(are * dt)
    br = er * jnp.cos(aim * dt)
    bi = er * jnp.sin(aim * dt)
    nr = br - 1.0
    den = are * are + aim * aim
    return br, bi, (nr * are + bi * aim) / den, (bi * are - nr * aim) / den


def _s5_disc(name, are, aim, ldt):
    def body(a_ref, b_ref, c_ref, o0, o1, o2, o3):
        r = _s5_disc_fn(a_ref[...], b_ref[...], c_ref[...])
        o0[...], o1[...], o2[...], o3[...] = r

    shp = _sds((S5_GROUPS, S5_STATE))
    return pl.pallas_call(body, name=name, out_shape=[shp] * 4)(are, aim, ldt)


def _s5_disc_bwd(name, are, aim, ldt, cts):
    def body(a_ref, b_ref, c_ref, d0, d1, d2, d3, o0, o1, o2):
        _, vjp = jax.vjp(_s5_disc_fn, a_ref[...], b_ref[...], c_ref[...])
        o0[...], o1[...], o2[...] = vjp((d0[...], d1[...], d2[...], d3[...]))

    shp = _sds((S5_GROUPS, S5_STATE))
    return pl.pallas_call(body, name=name, out_shape=[shp, shp, _sds((S5_GROUPS, 1))])(are, aim, ldt, *cts)


def _adamw_rows(w, g, m, v):
    m = ADAM_B1 * m + (1.0 - ADAM_B1) * g
    v = ADAM_B2 * v + (1.0 - ADAM_B2) * (g * g)
    m_hat = m / (1.0 - ADAM_B1 ** ADAM_STEP)
    v_hat = v / (1.0 - ADAM_B2 ** ADAM_STEP)
    return -ADAM_LR * (m_hat / (jnp.sqrt(v_hat) + ADAM_EPS) + ADAM_WD * w), m, v


def _adamw(name, w, ga, gb, m, v):
    rows, cols = w.shape
    tr = _row_tile(rows)

    def body(w_ref, ga_ref, gb_ref, m_ref, v_ref, g_out, d_out, m_out, v_out):
        g = ga_ref[...] + gb_ref[...]
        d, mm, vv = _adamw_rows(w_ref[...], g, m_ref[...], v_ref[...])
        g_out[...] = g
        d_out[...] = d
        m_out[...] = mm
        v_out[...] = vv

    spec = pl.BlockSpec((tr, cols), lambda i: (i, 0))
    return pl.pallas_call(
        body, name=name, grid=(rows // tr,), in_specs=[spec] * 5, out_specs=[spec] * 4,
        out_shape=[_sds((rows, cols))] * 4, compiler_params=_params("parallel"),
    )(w, ga, gb, m, v)


def _sum_stack(name, st):
    n, rows, cols = st.shape
    tr = _row_tile(rows)

    def body(s_ref, o_ref):
        acc = s_ref[0].astype(f32)
        for j in range(1, n):
            acc = acc + s_ref[j].astype(f32)
        o_ref[...] = acc

    return pl.pallas_call(
        body, name=name, grid=(rows // tr,), in_specs=[pl.BlockSpec((n, tr, cols), lambda i: (0, i, 0))],
        out_specs=pl.BlockSpec((tr, cols), lambda i: (i, 0)), out_shape=_sds((rows, cols)),
        compiler_params=_params("parallel"),
    )(st)


def _block_diag(w):
    h, n, m = w.shape
    return jnp.einsum("hij,hg->higj", w, jnp.eye(h, dtype=w.dtype)).reshape(h * n, h * m)


def _block_diag_part(dense, h):
    n, m = dense.shape[0] // h, dense.shape[1] // h
    return jnp.einsum("higj,hg->hij", dense.reshape(h, n, h, m), jnp.eye(h, dtype=dense.dtype))


def _s5_matrices(coef_re, coef_im, b_re, b_im, c_re, c_im):
    bb_re = coef_re[:, :, None] * b_re - coef_im[:, :, None] * b_im
    bb_im = coef_re[:, :, None] * b_im + coef_im[:, :, None] * b_re
    wb_re = _block_diag(jnp.swapaxes(bb_re, 1, 2))
    wb_im = _block_diag(jnp.swapaxes(bb_im, 1, 2))
    wc_re = _block_diag(jnp.swapaxes(c_re, 1, 2))
    wc_im = _block_diag(jnp.swapaxes(-c_im, 1, 2))
    return wb_re, wb_im, wc_re, wc_im


def _shift_down(t):
    return jnp.concatenate([jnp.zeros((1, t.shape[1]), t.dtype), t[:-1]], axis=0)


def _shift_up(t):
    return jnp.concatenate([t[1:], jnp.zeros((1, t.shape[1]), t.dtype)], axis=0)


def _row(v):
    return v.reshape(1, -1)


def _ffn_fwd(tag, h, get, names, gamma, beta, gate_first=False):
    wg = get(names[0])
    if gate_first:
        g = _ffn_gate(tag + "_gate", h, wg)
        wu = get(names[1])
        u, act = _ffn_up_given_gate(tag + "_up", h, wu, g)
    else:
        wu = get(names[1])
        g, u, act = _ffn_up(tag + "_up", h, wg, wu)
    wd = get(names[2])
    r, out = _mm_ln(tag + "_down", act, wd, h, gamma, beta, 0.5, k_slabs=True)
    return out, dict(h=h, g=g, u=u, act=act, r=r, wg=wg, wu=wu, wd=wd)


def _ffn_bwd(tag, dout, sv, names, gamma, put, after_ln=None):
    s = dout.shape[0]
    dr, dgam, dbet = _ln_bwd(tag + "_lnb", sv["r"], dout, gamma)
    if after_ln is not None:
        after_ln(dgam, dbet)
    put(names[2], _mm_plain(tag + "_dwd", "tn", _Slabs(sv["act"]), dr, (D_FF, D_MODEL, s), scale=0.5, out_dtype=bf16,
                            tiles=(FF_SLAB, 1024, _tile(s, 2048))))
    dg, du = _ffn_dact(tag + "_dact", dr, sv["wd"], sv["g"], sv["u"])
    dwg, dwu = _mm2(tag + "_dwgu", "tn", (D_FF, D_MODEL, s), _Slabs(dg), sv["h"], _Slabs(du), None, separate=True,
                    out_dtype=bf16, tiles=(FF_SLAB, 1024, _tile(s, 2048)))
    put(names[0], dwg)
    put(names[1], dwu)
    slabs = range(dg.shape[0])
    dh = _mm(tag + "_dh", "nn", (s, D_MODEL, FF_SLAB), (_tile(s, 512), D_MODEL, FF_SLAB),
             [_KPart(dg, j) for j in slabs] + [_KPart(du, j) for j in slabs],
             [_KPart(sv["wg"], j) for j in slabs] + [_KPart(sv["wu"], j) for j in slabs],
             [(j, j, 0) for j in range(2 * len(slabs))], 1,
             lambda accs, extras, vecs: [accs[0] + ALPHA * extras[0]], [f32], extras=[dr])[0]
    return dh, dgam, dbet


def _mixer_fwd(tag, h1, w):
    s = h1.shape[0]
    z = _mm_plain(tag + "_win", "nn", h1, w["w_in"], (s, N_IN_P, D_MODEL), tiles=(_tile(s, 512), 768, D_MODEL))
    ag, f, cu_cols = (z, D_A, 1), (z, 128, F_OFF // 128), (z, D_C, CU_OFF // D_C)
    cu = z[:, CU_OFF:]
    xa = _conv_fwd(tag + "_conv", z, w["conv_w"], w["conv_b"])
    a, gated = _rg_gates(tag + "_gates", xa, w["rg_wa"], w["rg_wx"], w["rg_ba"], w["rg_bx"], w["rg_lam"])
    ha = _lin_scan(tag + "_rgscan", a, gated, False)
    ones = jnp.ones((s, 128), f32)
    c = _lin_scan(tag + "_cumf", ones, _log_f(tag + "_logf", f, w["fox_bf"]), False)
    att = dict(zip(("qt", "k_aug", "kt", "vt", "v", "qh"), _attn_prep(tag + "_attnprep", z, c, w["sel"])))
    ot, lse = _attn_fwd_t(tag + "_attn", att["qt"], att["k_aug"], att["vt"])
    ob = ot.reshape(D_B, s).T
    bu_re, bu_im = _mm2(tag + "_s5in", "nn", (s, S5_LANES, D_C), cu, w["wb_re"], None, w["wb_im"], separate=True,
                        tiles=(_tile(s, 512), 1024, D_C))
    hre, him = _s5_scan(tag + "_s5scan", bu_re, bu_im, w["abar_re"], w["abar_im"], False)
    o = _mix_out(tag + "_mixout", ag, ha, ob, hre, him, cu_cols, w["s5_d"], w["mix_g"], w["wc_re"], w["wc_im"],
                 w["w_glu"])
    sv = dict(h1=h1, z=z, ag=ag, f=f, cu=cu, cu_cols=cu_cols, xa=xa, a=a, ha=ha, att=att, ot=ot, lse=lse, ob=ob,
              hre=hre, him=him, o=o)
    return o, sv


def _mixer_bwd(tag, do, dr2, sv, w, put):
    s = do.shape[0]
    (dag, dha, dob, dhre, dhim, dcu1, dwcr, dwci, dwglu, dd, dgn) = _mix_out_bwd(
        tag + "_mixoutb", do, sv["ag"], sv["ha"], sv["ob"], sv["hre"], sv["him"], sv["cu_cols"], w["s5_d"], w["mix_g"],
        w["wc_re"], w["wc_im"], w["w_glu"])
    put("s5_w_glu", dwglu.astype(bf16))
    gre, gim = _s5_scan(tag + "_s5scanb", dhre, dhim, w["abar_re"], -w["abar_im"], True)
    dab_re, dab_im = _s5_decay_grad(tag + "_s5dec", sv["hre"], sv["him"], gre, gim)
    dwb_re, dwb_im = _mm2(tag + "_s5dwb", "tn", (D_C, S5_LANES, s), sv["cu"], gre, None, gim, separate=True,
                          tiles=(D_C, 1024, _tile(s, 1024)))
    dcu = _mm2(tag + "_s5dcu", "nt", (s, D_C, S5_LANES), gre, w["wb_re"], gim, w["wb_im"], add=dcu1,
               tiles=(_tile(s, 512), D_C, 1024))[0]
    att = sv["att"]
    tq = _att_tiles(s)[0]
    nt = s // tq
    dot_blocks, doh = _attn_do_prep(tag + "_doprep", dob, w["sel"])
    dqt, delta = _attn_bwd_dq_t(tag + "_attndq", att["qt"], att["k_aug"], att["v"], att["kt"], sv["ot"], dot_blocks,
                                sv["lse"])
    dkh, dvh, dck = _attn_bwd_dkv_t(tag + "_attndkv", att["qt"], att["k_aug"], att["v"], att["qh"], doh, dot_blocks,
                                    sv["lse"].reshape(N_HEADS, nt, 1, tq), delta.reshape(N_HEADS, nt, 1, tq))
    dc = jnp.pad(dck.reshape(N_HEADS, s).T, ((0, 0), (0, 128 - N_HEADS)))
    dlf = _lin_scan(tag + "_cumfb", jnp.ones((s, 128), f32), dc, True)
    df, dbf = _log_f_bwd(tag + "_logfb", dlf, sv["f"], w["fox_bf"])
    ga = _lin_scan(tag + "_rgscanb", _shift_up(sv["a"]), dha, True)
    dxa, dwa, dwx, dba, dbx, dlam = _rg_gates_bwd(tag + "_gatesb", sv["xa"], ga, _shift_down(sv["ha"]), w["rg_wa"],
                                                  w["rg_wx"], w["rg_ba"], w["rg_bx"], w["rg_lam"])
    dax, dconv = _conv_bwd(tag + "_convb", dxa, sv["z"], w["conv_w"])
    dz = _dz_assemble(tag + "_dz", dax, dag, dqt, dkh, dvh, df, dcu, w["sel"])
    put("w_in", _mm_plain(tag + "_dwin", "tn", sv["h1"], dz, (D_MODEL, N_IN_P, s), out_dtype=bf16,
                          tiles=(512, 768, _tile(s, 1024))))
    dh1 = _mm_plain(tag + "_dh1", "nt", dz, w["w_in"], (s, D_MODEL, N_IN_P), add=dr2, add_coef=ALPHA,
                    tiles=(_tile(s, 512), 1024, 768))
    grads = dict(dconv=dconv, dwa=dwa, dwx=dwx, dba=dba, dbx=dbx, dlam=dlam, dbf=dbf,
                 dab_re=dab_re, dab_im=dab_im, dwb_re=dwb_re, dwb_im=dwb_im, dwcr=dwcr, dwci=dwci, dd=dd, dgn=dgn)
    return dh1, grads


SMALL_NAMES = ["ln1_g", "ln1_b", "conv_w", "conv_b", "rg_w_a", "rg_b_a", "rg_w_x", "rg_b_x", "rg_lambda", "fox_b_f",
               "s5_a_re", "s5_a_im", "s5_log_dt", "s5_b_re", "s5_b_im", "s5_c_re", "s5_c_im", "s5_d", "mix_norm_g",
               "ln2_g", "ln2_b", "ln3_g", "ln3_b"]
BIG_NAMES = ["ffn1_w_gate", "ffn1_w_up", "ffn1_w_down", "w_in", "s5_w_glu", "w_out", "ffn2_w_gate", "ffn2_w_up",
             "ffn2_w_down"]


def _local_step(x, target, weight, small, on_grads, on_small):
    h = x
    saved = []
    sel = _selections()
    for l in range(DEPTH):
        get = functools.partial(weight, l)

        sm = {n: small[n][l] for n in SMALL_NAMES}
        abar_re, abar_im, coef_re, coef_im = _s5_disc(f"l{l}_s5disc", sm["s5_a_re"], sm["s5_a_im"],
                                                      sm["s5_log_dt"].reshape(S5_GROUPS, 1))
        mats, mats_vjp = jax.vjp(_s5_matrices, coef_re, coef_im, sm["s5_b_re"], sm["s5_b_im"], sm["s5_c_re"],
                                 sm["s5_c_im"])
        w = dict(
            sel=sel, conv_w=sm["conv_w"], conv_b=_row(sm["conv_b"]),
            rg_wa=_block_diag(sm["rg_w_a"]).astype(bf16), rg_wx=_block_diag(sm["rg_w_x"]).astype(bf16),
            rg_ba=_row(sm["rg_b_a"]), rg_bx=_row(sm["rg_b_x"]), rg_lam=_row(sm["rg_lambda"]),
            fox_bf=jnp.pad(_row(sm["fox_b_f"]), ((0, 0), (0, 128 - N_HEADS))),
            abar_re=_row(abar_re), abar_im=_row(abar_im),
            wb_re=mats[0].astype(bf16), wb_im=mats[1].astype(bf16), wc_re=mats[2].astype(bf16),
            wc_im=mats[3].astype(bf16), s5_d=_row(sm["s5_d"]), mix_g=_row(sm["mix_norm_g"]))
        h1, sv1 = _ffn_fwd(f"l{l}_ffn1", h, get, GROUPS["F1"], _row(sm["ln1_g"]), _row(sm["ln1_b"]),
                           gate_first=(l == 0))
        w["w_in"], w["w_glu"] = get("w_in"), get("s5_w_glu")
        o, svm = _mixer_fwd(f"l{l}_mix", h1, w)
        w_out = get("w_out")
        r2, h2 = _mm_ln(f"l{l}_wout", o, w_out, h1, _row(sm["ln2_g"]), _row(sm["ln2_b"]), 1.0)
        h3, sv2 = _ffn_fwd(f"l{l}_ffn2", h2, get, GROUPS["F2"], _row(sm["ln3_g"]), _row(sm["ln3_b"]))
        saved.append(dict(sm=sm, w=w, w_out=w_out, sv1=sv1, svm=svm, r2=r2, sv2=sv2, mats_vjp=mats_vjp))
        h = h3

    dh, loss_row = _loss_head("loss_head", h, target)
    s = x.shape[0]
    gsmall = {n: [None] * DEPTH for n in SMALL_NAMES}
    for l in reversed(range(DEPTH)):
        sd = saved[l]
        sm, w = sd["sm"], sd["w"]

        def put(name, grad, l=l):
            on_grads((l, name), grad)

        dh2, dgam, dbet = _ffn_bwd(f"l{l}_ffn2", dh, sd["sv2"], GROUPS["F2"], _row(sm["ln3_g"]), put)
        gsmall["ln3_g"][l], gsmall["ln3_b"][l] = dgam[0], dbet[0]
        dr2, dgam, dbet = _ln_bwd(f"l{l}_ln2b", sd["r2"], dh2, _row(sm["ln2_g"]))
        gsmall["ln2_g"][l], gsmall["ln2_b"][l] = dgam[0], dbet[0]
        put("w_out", _mm_plain(f"l{l}_dwout", "tn", sd["svm"]["o"], dr2, (D_MODEL, D_MODEL, s), out_dtype=bf16))
        do = _mm_plain(f"l{l}_do", "nt", dr2, sd["w_out"], (s, D_MODEL, D_MODEL))
        dh1, g = _mixer_bwd(f"l{l}_mix", do, dr2, sd["svm"], w, put)
        gsmall["conv_w"][l], gsmall["conv_b"][l] = g["dconv"][:CONV_WIDTH], g["dconv"][CONV_WIDTH]
        gsmall["rg_w_a"][l] = _block_diag_part(g["dwa"], N_HEADS)
        gsmall["rg_w_x"][l] = _block_diag_part(g["dwx"], N_HEADS)
        gsmall["rg_b_a"][l], gsmall["rg_b_x"][l], gsmall["rg_lambda"][l] = g["dba"][0], g["dbx"][0], g["dlam"][0]
        gsmall["fox_b_f"][l] = g["dbf"][0, :N_HEADS]
        dcoef_re, dcoef_im, db_re, db_im, dc_re, dc_im = sd["mats_vjp"]((g["dwb_re"], g["dwb_im"], g["dwcr"], g["dwci"]))
        da_re, da_im, dldt = _s5_disc_bwd(
            f"l{l}_s5discb", sm["s5_a_re"], sm["s5_a_im"], sm["s5_log_dt"].reshape(S5_GROUPS, 1),
            (g["dab_re"].reshape(S5_GROUPS, S5_STATE), g["dab_im"].reshape(S5_GROUPS, S5_STATE), dcoef_re, dcoef_im))
        gsmall["s5_a_re"][l], gsmall["s5_a_im"][l], gsmall["s5_log_dt"][l] = da_re, da_im, dldt[:, 0]
        gsmall["s5_b_re"][l], gsmall["s5_b_im"][l], gsmall["s5_c_re"][l], gsmall["s5_c_im"][l] = db_re, db_im, dc_re, dc_im
        gsmall["s5_d"][l], gsmall["mix_norm_g"][l] = g["dd"][0], g["dgn"][0]

        def after_ln(dgam, dbet, l=l):
            gsmall["ln1_g"][l], gsmall["ln1_b"][l] = dgam[0], dbet[0]
            if l == 0:
                on_small({n: jnp.stack(v) for n, v in gsmall.items()})

        dh, _, _ = _ffn_bwd(f"l{l}_ffn1", dh1, sd["sv1"], GROUPS["F1"], _row(sm["ln1_g"]), put, after_ln)
    return loss_row[0, 0], dh


def _position():
    return lax.axis_index("x"), lax.axis_index("y"), lax.axis_index("c")


_ANY = pl.BlockSpec(memory_space=pl.ANY)


COLUMN_SHARDED = ("ffn1_w_gate", "ffn1_w_up", "ffn2_w_gate", "ffn2_w_up")
PACK_QUANTUM = 128 * 256


def _permute_in_cols(w):
    pad = jnp.zeros(w.shape[:-1] + (128 - N_HEADS,), w.dtype)
    return jnp.concatenate([w[..., :F_OFF + N_HEADS], pad, w[..., F_OFF + N_HEADS:]], axis=-1)


def _unpermute_in_cols(w):
    return jnp.concatenate([w[..., :F_OFF + N_HEADS], w[..., CU_OFF:]], axis=-1)


def _pack(arrs):
    flat = jnp.concatenate([a.reshape(-1) for a in arrs])
    pad = -flat.shape[0] % PACK_QUANTUM
    return jnp.pad(flat, (0, pad)).reshape(-1, 128)


def _unpack(buf, shapes):
    flat = buf.reshape(-1)
    out, off = [], 0
    for shp in shapes:
        size = math.prod(shp)
        out.append(flat[off:off + size].reshape(shp))
        off += size
    return out


WEIGHT_NAMES = ["ffn1_w_gate", "ffn1_w_up", "ffn1_w_down", "ln1_g", "ln1_b", "w_in", "conv_w", "conv_b", "rg_w_a",
                "rg_b_a", "rg_w_x", "rg_b_x", "rg_lambda", "fox_b_f", "s5_a_re", "s5_a_im", "s5_log_dt", "s5_b_re",
                "s5_b_im", "s5_c_re", "s5_c_im", "s5_d", "s5_w_glu", "mix_norm_g", "w_out", "ln2_g", "ln2_b",
                "ffn2_w_gate", "ffn2_w_up", "ffn2_w_down", "ln3_g", "ln3_b"]


def _remote(src, dst, send_sems, recv_sems, k, peer):
    return pltpu.make_async_remote_copy(src_ref=src, dst_ref=dst, send_sem=send_sems.at[k], recv_sem=recv_sems.at[k],
                                        device_id=peer, device_id_type=MESH)


class _ChipGatherPart:
    def __init__(self, arrays):
        self.arrays, self.results = list(arrays), None

    def out_shape(self):
        return [_sds((N_CHIPS,) + a.shape, a.dtype) for a in self.arrays]

    def sems(self):
        n = len(self.arrays)
        return [pltpu.SemaphoreType.DMA((3 * n,)), pltpu.SemaphoreType.DMA((3 * n,)), pltpu.SemaphoreType.DMA((n,))]

    def copies(self, ins, outs, sems):
        send_sems, recv_sems, local_sems = sems
        x, y, c = _position()
        me = 2 * x + y
        local, sends, recvs = [], [], []
        for i, (src, dst) in enumerate(zip(ins, outs)):
            local.append(pltpu.make_async_copy(self.mine(src, me), dst.at[me], local_sems.at[i]))
            for r, (px, py) in enumerate([(1 - x, y), (x, 1 - y), (1 - x, 1 - y)]):
                peer = 2 * px + py
                sends.append(_remote(self.theirs(src, peer), dst.at[me], send_sems, recv_sems, 3 * i + r, (px, py, c)))
                recvs.append(_remote(self.mine(src, me), dst.at[peer], send_sems, recv_sems, 3 * i + r, (px, py, c)))
        return local, sends, recvs

    def mine(self, src, me):
        return src

    def theirs(self, src, peer):
        return src


class _ChipGatherHalvesPart(_ChipGatherPart):
    def sems(self):
        n = len(self.arrays)
        return super().sems() + [pltpu.SemaphoreType.DMA((3 * n,)), pltpu.SemaphoreType.DMA((3 * n,))]

    def _half(self, ref, which):
        rows = ref.shape[0] // 2
        return ref.at[pl.ds(which * rows, rows)]

    def copies(self, ins, outs, sems):
        send_sems, recv_sems, local_sems = sems[:3]
        x, y, c = _position()
        me = 2 * x + y
        local, sends, recvs = [], [], []
        for i, (src, dst) in enumerate(zip(ins, outs)):
            local.append(pltpu.make_async_copy(src, dst.at[me], local_sems.at[i]))
            for r, (px, py) in enumerate([(1 - x, y), (x, 1 - y), (1 - x, 1 - y)]):
                sends.append(_remote(self._half(src, c), self._half(dst.at[me], c), send_sems, recv_sems, 3 * i + r,
                                     (px, py, c)))
                recvs.append(_remote(self._half(src, c), self._half(dst.at[2 * px + py], c), send_sems, recv_sems,
                                     3 * i + r, (px, py, c)))
        return local, sends, recvs

    def forwards(self, ins, outs, sems):
        send_sems, recv_sems = sems[3:]
        x, y, c = _position()
        sends, recvs = [], []
        for i, dst in enumerate(outs):
            for r, (px, py) in enumerate([(1 - x, y), (x, 1 - y), (1 - x, 1 - y)]):
                slot = dst.at[2 * px + py]
                sends.append(_remote(self._half(slot, c), self._half(slot, c), send_sems, recv_sems, 3 * i + r,
                                     (x, y, 1 - c)))
                recvs.append(_remote(self._half(slot, c), self._half(slot, 1 - c), send_sems, recv_sems, 3 * i + r,
                                     (x, y, 1 - c)))
        return sends, recvs


class _ChipScatterPart(_ChipGatherPart):
    def out_shape(self):
        return [_sds(a.shape, a.dtype) for a in self.arrays]

    def mine(self, src, me):
        return src.at[me]

    def theirs(self, src, peer):
        return src.at[peer]


class _SiblingSwapPart:
    def __init__(self, arrays):
        self.arrays, self.results = list(arrays), None

    def out_shape(self):
        return [_sds(a.shape, a.dtype) for a in self.arrays]

    def sems(self):
        n = len(self.arrays)
        return [pltpu.SemaphoreType.DMA((n,)), pltpu.SemaphoreType.DMA((n,))]

    def copies(self, ins, outs, sems):
        x, y, c = _position()
        both = [_remote(src, dst, sems[0], sems[1], i, (x, y, 1 - c)) for i, (src, dst) in enumerate(zip(ins, outs))]
        return [], both, both


def _split_by(parts, refs, count):
    out, off = [], 0
    for p in parts:
        out.append(refs[off:off + count(p)])
        off += count(p)
    return out


def _parts_refs(parts, in_refs, out_refs, sem_refs):
    return zip(parts, _split_by(parts, in_refs, lambda p: len(p.arrays)),
               _split_by(parts, out_refs, lambda p: len(p.arrays)), _split_by(parts, sem_refs, lambda p: len(p.sems())))


def _exchange_start(parts, in_refs, out_refs, sem_refs):
    for part, ins, outs, sems in _parts_refs(parts, in_refs, out_refs, sem_refs):
        local, sends, _ = part.copies(ins, outs, sems)
        for cp in local + sends:
            cp.start()


def _exchange_finish(parts, in_refs, out_refs, sem_refs):
    split = list(_parts_refs(parts, in_refs, out_refs, sem_refs))
    copies = [part.copies(ins, outs, sems) for part, ins, outs, sems in split]
    for _, _, recvs in copies:
        for cp in recvs:
            cp.wait_recv()
    second = [part.forwards(ins, outs, sems) for part, ins, outs, sems in split if hasattr(part, "forwards")]
    for sends, _ in second:
        for cp in sends:
            cp.start()
    for sends, recvs in second:
        for cp in recvs:
            cp.wait_recv()
        for cp in sends:
            cp.wait_send()
    for local, sends, _ in copies:
        for cp in sends:
            cp.wait_send()
        for cp in local:
            cp.wait()


def _exchange_operands(parts):
    return ([a for p in parts for a in p.arrays], [s for p in parts for s in p.out_shape()],
            [s for p in parts for s in p.sems()])


def _set_results(parts, res):
    for part, outs in zip(parts, _split_by(parts, list(res), lambda p: len(p.arrays))):
        part.results = list(outs)


def _exchange_now(name, parts):
    x_in, x_out, x_sem = _exchange_operands(parts)
    n = len(x_in)

    def body(*refs):
        _exchange_start(parts, refs[:n], refs[n:2 * n], refs[2 * n:])
        _exchange_finish(parts, refs[:n], refs[n:2 * n], refs[2 * n:])

    res = pl.pallas_call(body, name=name, in_specs=[_ANY] * n, out_specs=[_ANY] * n, out_shape=x_out,
                         scratch_shapes=x_sem)(*x_in)
    _set_results(parts, res)


_RIDERS = {}


def _call(body, *, name, grid, in_specs, out_specs, out_shape, scratch_shapes=(), compiler_params=None):
    make_parts = _RIDERS.pop(name, None)
    if make_parts is None:
        return pl.pallas_call(body, name=name, grid=grid, in_specs=in_specs, out_specs=out_specs, out_shape=out_shape,
                              scratch_shapes=scratch_shapes, compiler_params=compiler_params)
    parts = make_parts()
    x_in, x_out, x_sem = _exchange_operands(parts)
    n_out, n_scr, n_x = len(out_shape), len(scratch_shapes), len(x_in)

    def run(*args):
        n_in = len(args)

        def hosted(*refs):
            ins, xi = refs[:n_in], refs[n_in:n_in + n_x]
            outs, xo = refs[n_in + n_x:n_in + n_x + n_out], refs[n_in + n_x + n_out:n_in + 2 * n_x + n_out]
            scr, xs = refs[n_in + 2 * n_x + n_out:n_in + 2 * n_x + n_out + n_scr], refs[n_in + 2 * n_x + n_out + n_scr:]
            first = functools.reduce(jnp.logical_and, [pl.program_id(d) == 0 for d in range(len(grid))])
            last = functools.reduce(jnp.logical_and, [pl.program_id(d) == grid[d] - 1 for d in range(len(grid))])

            @pl.when(first)
            def _():
                _exchange_start(parts, xi, xo, xs)

            body(*ins, *outs, *scr)

            @pl.when(last)
            def _():
                _exchange_finish(parts, xi, xo, xs)

        res = pl.pallas_call(
            hosted, name=name, grid=grid, in_specs=list(in_specs) + [_ANY] * n_x,
            out_specs=list(out_specs) + [_ANY] * n_x, out_shape=list(out_shape) + x_out,
            scratch_shapes=list(scratch_shapes) + x_sem, compiler_params=_params(*["arbitrary"] * len(grid)),
        )(*args, *x_in)
        _set_results(parts, res[n_out:])
        return list(res[:n_out])

    return run


GROUPS = {"F1": ["ffn1_w_gate", "ffn1_w_up", "ffn1_w_down"], "MX": ["w_in", "s5_w_glu", "w_out"],
          "F2": ["ffn2_w_gate", "ffn2_w_up", "ffn2_w_down"]}
FIRST_GATHER = [(0, "ffn1_w_gate")]
GATHER_HOSTS = {
    "l0_ffn1_gate": [(0, "ffn1_w_up")],
    "l0_ffn1_up": [(0, "ffn1_w_down")],
    "l0_ffn1_down": [(0, "w_in"), (0, "s5_w_glu"), (0, "w_out")],
    "l0_mix_attn": [(0, "ffn2_w_up"), (0, "ffn2_w_down"), (1, "w_in")],
    "l0_mix_s5scan": [(0, "ffn2_w_gate")],
    "l0_wout": [(1, "s5_w_glu"), (1, "w_out")],
    "l0_ffn2_up": [(1, "ffn1_w_gate")],
    "l0_ffn2_down": [(1, "ffn1_w_up")],
    "l1_ffn1_up": [(1, "ffn1_w_down")],
    "l1_mix_attn": [(1, "ffn2_w_up"), (1, "ffn2_w_down")],
    "l1_mix_s5scan": [(1, "ffn2_w_gate")],
}
SCATTER_HOSTS = {
    "l1_ffn2_dact": [(1, "ffn2_w_down")],
    "l1_mix_attndq": [(1, "ffn2_w_up")],
    "l1_mix_attndkv": [(1, "w_out"), (1, "s5_w_glu"), (1, "ffn2_w_gate")],
    "l1_ffn1_dact": [(1, "ffn1_w_down")],
    "l1_ffn1_dwgu": [(1, "w_in")],
    "l0_ffn2_dact": [(1, "ffn1_w_up")],
    "l0_ffn2_dwgu": [(0, "ffn2_w_down")],
    "l0_mix_attndq": [(0, "w_out"), (0, "s5_w_glu"), (0, "ffn2_w_up")],
    "l0_mix_attndkv": [(0, "ffn2_w_gate"), (1, "ffn1_w_gate")],
    "l0_mix_dh1": [(0, "w_in")],
    "l0_ffn1_dact": [(0, "ffn1_w_down")],
    "l0_ffn1_dh": [(0, "ffn1_w_gate")],
}
LAST_SCATTER = [(0, "ffn1_w_up")]
SMALL_HOST = "l0_ffn1_dwd"
SMALL_PACK_ORDER = [n for n in SMALL_NAMES if n != "conv_w"] + ["conv_w"]
TAIL_HOST = "l0_ffn1_dwgu"
LATE_SCATTER_HOST = "l0_ffn1_dh"
LAST_HOST = "adamw_ffn2"


def _sharded_rows(name, a):
    return jnp.swapaxes(a, 1, 2) if name in COLUMN_SHARDED else a


def _unstack_layer(st):
    _, r, c = st.shape
    return st.reshape(N_CHIPS * r, c)


def _restack_layer(g):
    r, c = g.shape
    return g.reshape(N_CHIPS, r // N_CHIPS, c)


def _adamw_layer(name, layer, w, ga, gb, m, v, bufs):
    _, r, c = w.shape
    tr = _row_tile(r)

    def body(w_ref, ga_ref, gb_ref, m_ref, v_ref, *rest):
        g_out, d_out, m_out, v_out = rest[-4:]
        g = ga_ref[...] + gb_ref[...]
        d, mm, vv = _adamw_rows(w_ref[...], g, m_ref[...], v_ref[...])
        g_out[...] = g
        d_out[...] = d
        m_out[...] = mm
        v_out[...] = vv

    full = pl.BlockSpec((None, tr, c), lambda i: (layer, i, 0))
    flat = pl.BlockSpec((tr, c), lambda i: (i, 0))
    extra = {} if bufs is None else dict(input_output_aliases={5 + k: k for k in range(4)})
    return pl.pallas_call(
        body, name=name, grid=(r // tr,),
        in_specs=[full, flat, flat, full, full] + ([] if bufs is None else [_ANY] * 4),
        out_specs=[full] * 4, out_shape=[_sds(w.shape)] * 4, compiler_params=_params("parallel"), **extra,
    )(w, ga, gb, m, v, *([] if bufs is None else bufs))


def _adamw_both_layers(name, ws, ms, vs, gas, gbs):
    nw = len(ws)
    _, r, c = ws[0].shape
    tr = _row_tile(r, 64)

    def body(*refs):
        ins, outs = refs[:7 * nw], refs[7 * nw:]
        for k in range(nw):
            w_ref, m_ref, v_ref = ins[3 * k:3 * k + 3]
            g_refs = ins[3 * nw + 4 * k:3 * nw + 4 * k + 4]
            g_out, d_out, m_out, v_out = outs[4 * k:4 * k + 4]
            for layer in range(DEPTH):
                g = g_refs[layer][...] + g_refs[DEPTH + layer][...]
                d, mm, vv = _adamw_rows(w_ref[layer], g, m_ref[layer], v_ref[layer])
                g_out[layer] = g
                d_out[layer] = d
                m_out[layer] = mm
                v_out[layer] = vv

    both = pl.BlockSpec((DEPTH, tr, c), lambda i: (0, i, 0))
    flat = pl.BlockSpec((tr, c), lambda i: (i, 0))
    wmv = [t for k in range(nw) for t in (ws[k], ms[k], vs[k])]
    gs = [t for k in range(nw) for t in (*gas[k], *gbs[k])]
    res = _call(
        body, name=name, grid=(r // tr,), in_specs=[both] * (3 * nw) + [flat] * (4 * nw),
        out_specs=[both] * (4 * nw), out_shape=[_sds(ws[0].shape)] * (4 * nw), compiler_params=_params("parallel"),
    )(*wmv, *gs)
    return [res[4 * k:4 * k + 4] for k in range(nw)]


def _train_step(x, loss_target, w, m, v):
    ix, iy, _ = _position()
    chip = 2 * ix + iy
    shard = {n: (_permute_in_cols(w[n]) if n == "w_in" else _sharded_rows(n, w[n])).astype(bf16) for n in BIG_NAMES}

    gathered = {}

    def gather_parts(keys, extra=()):
        part = _ChipGatherHalvesPart([shard[n][layer] for layer, n in keys] + list(extra))
        gathered.update({key: (part, i) for i, key in enumerate(keys)})
        return [part]

    (first,) = gather_parts(FIRST_GATHER, extra=[w["conv_w"]])
    _exchange_now("gather_first", [first])
    for host, keys in GATHER_HOSTS.items():
        _RIDERS[host] = functools.partial(gather_parts, keys)

    def weight(layer, name):
        part, i = gathered[(layer, name)]
        return _unstack_layer(part.results[i])

    small = {n: w[n] for n in SMALL_NAMES}
    small["conv_w"] = first.results[-1].transpose(1, 2, 0, 3).reshape(DEPTH, CONV_WIDTH, D_A)

    grads_full, scattered = {}, {}

    def scatter_parts(keys):
        part = _ChipScatterPart([_restack_layer(grads_full[key]) for key in keys])
        scattered.update({key: (part, i) for i, key in enumerate(keys)})
        return [part]

    for host, keys in SCATTER_HOSTS.items():
        _RIDERS[host] = functools.partial(scatter_parts, keys)

    partial = {}

    def reduce_chips(keys):
        for layer, n in keys:
            part, i = scattered[(layer, n)]
            p = _sum_stack(f"sum_l{layer}_{n}", part.results[i])
            partial[(layer, n)] = _unpermute_in_cols(p) if n == "w_in" else p

    early = [key for host, keys in SCATTER_HOSTS.items() if host != LATE_SCATTER_HOST for key in keys]
    late = SCATTER_HOSTS[LATE_SCATTER_HOST]
    tail = {}

    def small_parts():
        tail["small"] = _ChipGatherPart([_pack([tail["gsmall"][n] for n in SMALL_PACK_ORDER])])
        return [tail["small"]]

    def tail_parts():
        reduce_chips(early)
        tail["small_sum"] = _sum_stack("sum_small", tail["small"].results[0])
        tail["swap"] = _SiblingSwapPart([partial[k] for k in early] + [tail["small_sum"]])
        return [tail["swap"]]

    _RIDERS[SMALL_HOST] = small_parts
    _RIDERS[TAIL_HOST] = tail_parts
    loss_local, gx = _local_step(x[0], loss_target[0], weight, small, grads_full.__setitem__,
                                 functools.partial(tail.__setitem__, "gsmall"))
    other = dict(zip(early, tail["swap"].results[:-1]))
    small_mine, small_other = tail["small_sum"], tail["swap"].results[-1]
    grads, deltas, new_m, new_v = {}, {}, {}, {}
    reduce_chips(late)
    last_parts = scatter_parts(LAST_SCATTER) + [_SiblingSwapPart([partial[k] for k in late])]
    _RIDERS[LAST_HOST] = lambda: last_parts
    ffn2 = GROUPS["F2"]
    res = _adamw_both_layers(
        LAST_HOST, *[[_sharded_rows(n, t[n]) for n in ffn2] for t in (w, m, v)],
        [[partial[(layer, n)] for layer in range(DEPTH)] for n in ffn2],
        [[other[(layer, n)] for layer in range(DEPTH)] for n in ffn2])
    for n, bufs in zip(ffn2, res):
        grads[n], deltas[n], new_m[n], new_v[n] = (_sharded_rows(n, t) for t in bufs)
    other.update(zip(late, last_parts[1].results))
    reduce_chips(LAST_SCATTER)
    swap_last = _SiblingSwapPart([partial[k] for k in LAST_SCATTER])
    _exchange_now("swap_last", [swap_last])
    other.update(zip(LAST_SCATTER, swap_last.results))

    for n in BIG_NAMES:
        if n in ffn2:
            continue
        bufs = None
        wr, mr, vr = (_sharded_rows(n, t) for t in (w[n], m[n], v[n]))
        for layer in range(DEPTH):
            bufs = _adamw_layer(f"adamw_l{layer}_{n}", layer, wr, partial[(layer, n)], other[(layer, n)], mr, vr, bufs)
        grads[n], deltas[n], new_m[n], new_v[n] = (_sharded_rows(n, t) for t in bufs)
    packed = SMALL_PACK_ORDER[:-1]
    shapes = [w[n].shape for n in packed]
    res = _adamw("adamw_small", _pack([w[n] for n in packed]), small_mine, small_other,
                 _pack([m[n] for n in packed]), _pack([v[n] for n in packed]))
    for dst, buf in zip((grads, deltas, new_m, new_v), res):
        dst.update(zip(packed, _unpack(buf, shapes)))
    cw = D_A // N_CHIPS
    conv_shape = (DEPTH, CONV_WIDTH, D_A)
    offset = sum(math.prod(s_) for s_ in shapes)

    def conv_grad(buf):
        full = buf.reshape(-1)[offset:offset + math.prod(conv_shape)].reshape(conv_shape)
        return lax.dynamic_slice_in_dim(full, chip * cw, cw, axis=2).reshape(DEPTH * CONV_WIDTH, cw)

    rows = lambda t: t.reshape(DEPTH * CONV_WIDTH, cw)
    res = _adamw("adamw_conv_w", rows(w["conv_w"]), conv_grad(small_mine), conv_grad(small_other),
                 rows(m["conv_w"]), rows(v["conv_w"]))
    for dst, buf in zip((grads, deltas, new_m, new_v), res):
        dst["conv_w"] = buf.reshape(w["conv_w"].shape)

    loss = lax.psum(loss_local, ("x", "y", "c"))
    return (loss, gx[None], *[grads[n] for n in WEIGHT_NAMES], *[deltas[n] for n in WEIGHT_NAMES],
            *[new_m[n] for n in WEIGHT_NAMES], *[new_v[n] for n in WEIGHT_NAMES])


def kernel(x, ffn1_w_gate, ffn1_w_up, ffn1_w_down, ln1_g, ln1_b, w_in, conv_w, conv_b, rg_w_a, rg_b_a, rg_w_x, rg_b_x, rg_lambda, fox_b_f, s5_a_re, s5_a_im, s5_log_dt, s5_b_re, s5_b_im, s5_c_re, s5_c_im, s5_d, s5_w_glu, mix_norm_g, w_out, ln2_g, ln2_b, ffn2_w_gate, ffn2_w_up, ffn2_w_down, ln3_g, ln3_b, loss_target, m_ffn1_w_gate, m_ffn1_w_up, m_ffn1_w_down, m_ln1_g, m_ln1_b, m_w_in, m_conv_w, m_conv_b, m_rg_w_a, m_rg_b_a, m_rg_w_x, m_rg_b_x, m_rg_lambda, m_fox_b_f, m_s5_a_re, m_s5_a_im, m_s5_log_dt, m_s5_b_re, m_s5_b_im, m_s5_c_re, m_s5_c_im, m_s5_d, m_s5_w_glu, m_mix_norm_g, m_w_out, m_ln2_g, m_ln2_b, m_ffn2_w_gate, m_ffn2_w_up, m_ffn2_w_down, m_ln3_g, m_ln3_b, v_ffn1_w_gate, v_ffn1_w_up, v_ffn1_w_down, v_ln1_g, v_ln1_b, v_w_in, v_conv_w, v_conv_b, v_rg_w_a, v_rg_b_a, v_rg_w_x, v_rg_b_x, v_rg_lambda, v_fox_b_f, v_s5_a_re, v_s5_a_im, v_s5_log_dt, v_s5_b_re, v_s5_b_im, v_s5_c_re, v_s5_c_im, v_s5_d, v_s5_w_glu, v_mix_norm_g, v_w_out, v_ln2_g, v_ln2_b, v_ffn2_w_gate, v_ffn2_w_up, v_ffn2_w_down, v_ln3_g, v_ln3_b):
    args = dict(locals())
    w = {n: args[n] for n in WEIGHT_NAMES}
    m = {n: args["m_" + n] for n in WEIGHT_NAMES}
    v = {n: args["v_" + n] for n in WEIGHT_NAMES}
    return _train_step(x, loss_target, w, m, v)
```

```python
import functools
import math

import jax
import jax.numpy as jnp
from jax import lax
from jax.experimental import pallas as pl
from jax.experimental.pallas import tpu as pltpu

f32 = jnp.float32
bf16 = jnp.bfloat16

D_MODEL = 1024
D_FF = 2816
D_A = 384
D_B = 384
D_C = 256
N_HEADS = 6
HEAD_DIM = 64
S5_GROUPS = 16
S5_STATE = 64
S5_LANES = S5_GROUPS * S5_STATE
F_OFF = 5 * D_A
CU_OFF = F_OFF + 128
N_IN_P = CU_OFF + D_C
CONV_WIDTH = 4
DEPTH = 2
ALPHA = (2 * DEPTH) ** 0.25
LN_EPS = 1e-5
RMS_EPS = 1e-6
RG_C = 8.0
ATT_SCALE = HEAD_DIM ** -0.5
ADAM_LR, ADAM_B1, ADAM_B2, ADAM_EPS, ADAM_WD, ADAM_STEP = 0.001, 0.9, 0.999, 1e-08, 0.01, 10

ROW_TILE = 512
N_CHIPS = 4
MESH = pl.DeviceIdType.MESH

_DN = {
    "nn": (((1,), (0,)), ((), ())),
    "nt": (((1,), (1,)), ((), ())),
    "tn": (((0,), (0,)), ((), ())),
}


def _sds(shape, dtype=f32):
    return jax.ShapeDtypeStruct(shape, dtype)


def _tile(n, target):
    best = None
    for t in range(128, min(n, target) + 1, 128):
        if n % t == 0:
            best = t
    return best or n


def _row_tile(rows, target=256):
    best = None
    for t in range(16, min(rows, target) + 1, 16):
        if rows % t == 0:
            best = t
    return best or rows


def _params(*sem):
    return pltpu.CompilerParams(dimension_semantics=sem)


class _Slabs:
    def __init__(self, x):
        self.x = x


class _KPart:
    def __init__(self, x, j):
        self.x, self.j = x, j


FF_SLAB = D_FF // N_CHIPS
FFN_ROWS = 1024

def _mm(name, mode, dims, tiles, a_list, b_list, pairs, n_acc, epilogue, outs, extras=(), vecs=(), split_cols=False):
    m, n, k = dims
    tm, tn, tk = tiles
    nk = k // tk
    na, nb, ne, nv, no = len(a_list), len(b_list), len(extras), len(vecs), len(outs)

    def body(*refs):
        a_refs = refs[:na]
        b_refs = refs[na:na + nb]
        e_refs = refs[na + nb:na + nb + ne]
        v_refs = refs[na + nb + ne:na + nb + ne + nv]
        o_refs = refs[na + nb + ne + nv:na + nb + ne + nv + no]
        acc_refs = refs[na + nb + ne + nv + no:]
        a_vals = [r[...].astype(bf16) for r in a_refs]
        b_vals = [r[...].astype(bf16) for r in b_refs]
        products = [(ci, lax.dot_general(a_vals[ai], b_vals[bi], _DN[mode], preferred_element_type=f32))
                    for ai, bi, ci in pairs]

        def finish(accs):
            res = epilogue(accs, [e[...] for e in e_refs], [v[...] for v in v_refs])
            for o, r in zip(o_refs, res):
                o[...] = r.astype(o.dtype)

        if nk == 1:
            accs = [None] * n_acc
            for ci, prod in products:
                accs[ci] = prod if accs[ci] is None else accs[ci] + prod
            finish(accs)
            return
        kk = pl.program_id(2)

        @pl.when(kk == 0)
        def _():
            for acc in acc_refs:
                acc[...] = jnp.zeros_like(acc)

        for ci, prod in products:
            acc_refs[ci][...] += prod

        @pl.when(kk == nk - 1)
        def _():
            finish([acc[...] for acc in acc_refs])

    def a_spec(a):
        if isinstance(a, _KPart):
            return pl.BlockSpec((None, tm, tk), lambda i, j, kk, part=a.j: (part, i, 0))
        if isinstance(a, _Slabs):
            if mode == "tn":
                return pl.BlockSpec((None, tk, tm), lambda i, j, kk: (i, kk, 0))
            return pl.BlockSpec((None, tm, tk), lambda i, j, kk: (kk, i, 0))
        if mode == "tn":
            return pl.BlockSpec((tk, tm), lambda i, j, kk: (kk, i))
        return pl.BlockSpec((tm, tk), lambda i, j, kk: (i, kk))

    def b_spec(b):
        if isinstance(b, _KPart):
            return pl.BlockSpec((tk, tn), lambda i, j, kk, part=b.j: (part, j))
        if isinstance(b, _Slabs):
            if mode == "nt":
                return pl.BlockSpec((None, tn, tk), lambda i, j, kk: (kk, j, 0))
            return pl.BlockSpec((None, tk, tn), lambda i, j, kk: (j, kk, 0))
        if mode == "nt":
            return pl.BlockSpec((tn, tk), lambda i, j, kk: (j, kk))
        return pl.BlockSpec((tk, tn), lambda i, j, kk: (kk, j))

    o_spec = pl.BlockSpec((tm, tn), lambda i, j, kk: (i, j))
    o_slab_spec = pl.BlockSpec((None, tm, tn), lambda i, j, kk: (j, i, 0))
    v_spec = pl.BlockSpec((1, tn), lambda i, j, kk: (0, j))
    if split_cols:
        out_specs = [o_slab_spec] * no
        out_shape = [_sds((n // tn, m, tn), dt) for dt in outs]
    else:
        out_specs = [o_spec] * no
        out_shape = [_sds((m, n), dt) for dt in outs]
    raw = lambda t: t.x if isinstance(t, (_Slabs, _KPart)) else t
    res = _call(
        body,
        name=name,
        grid=(m // tm, n // tn, nk),
        in_specs=([a_spec(a) for a in a_list] + [b_spec(b) for b in b_list]
                  + [o_slab_spec if isinstance(e, _Slabs) else o_spec for e in extras] + [v_spec] * nv),
        out_specs=out_specs,
        out_shape=out_shape,
        scratch_shapes=[pltpu.VMEM((tm, tn), f32)] * (n_acc if nk > 1 else 0),
        compiler_params=_params("parallel", "parallel", "arbitrary"),
    )(*map(raw, a_list), *map(raw, b_list), *map(raw, extras), *vecs)
    return res


def _sigmoid(x):
    return 0.5 * (jnp.tanh(0.5 * x) + 1.0)


def _layer_norm_rows(r, gamma, beta):
    mu = jnp.mean(r, axis=-1, keepdims=True)
    xc = r - mu
    var = jnp.mean(xc * xc, axis=-1, keepdims=True)
    return xc * lax.rsqrt(var + LN_EPS) * gamma + beta


def _mm_plain(name, mode, a, b, dims, scale=1.0, out_dtype=f32, add=None, add_coef=1.0, tiles=None):
    m, n, k = dims
    tiles = tiles or (_tile(m, 512), _tile(n, 1024), _tile(k, 1024))

    def epilogue(accs, extras, vecs):
        r = accs[0] if scale == 1.0 else accs[0] * scale
        if extras:
            r = r + add_coef * extras[0]
        return [r]

    return _mm(name, mode, dims, tiles, [a], [b], [(0, 0, 0)], 1, epilogue, [out_dtype],
               extras=[] if add is None else [add])[0]


def _swiglu_parts(g, u):
    sg = _sigmoid(g)
    silu = g * sg
    return [silu, sg + silu * (1.0 - sg), u, silu * u]


def _ffn_up(name, h, wg, wu):
    s = h.shape[0]
    return _mm(name, "nt", (s, D_FF, D_MODEL), (_tile(s, FFN_ROWS), FF_SLAB, D_MODEL), [h], [wg, wu],
               [(0, 0, 0), (0, 1, 1)], 2, lambda accs, extras, vecs: _swiglu_parts(accs[0], accs[1]),
               [bf16, bf16, bf16, bf16], split_cols=True)


def _ffn_gate(name, h, wg):
    s = h.shape[0]
    return _mm(name, "nt", (s, D_FF, D_MODEL), (_tile(s, FFN_ROWS), FF_SLAB, D_MODEL), [h], [wg], [(0, 0, 0)], 1,
               lambda accs, extras, vecs: [accs[0]], [bf16], split_cols=True)[0]


def _ffn_up_given_gate(name, h, wu, g):
    s = h.shape[0]
    return _mm(name, "nt", (s, D_FF, D_MODEL), (_tile(s, FFN_ROWS), FF_SLAB, D_MODEL), [h], [wu], [(0, 0, 0)], 1,
               lambda accs, extras, vecs: _swiglu_parts(extras[0].astype(f32), accs[0]),
               [bf16, bf16, bf16, bf16], extras=[_Slabs(g)], split_cols=True)


def _mm_ln(name, a, w, resid, gamma, beta, scale, k_slabs=False):
    def epilogue(accs, extras, vecs):
        r = ALPHA * extras[0] + scale * accs[0]
        return [r, _layer_norm_rows(r, vecs[0], vecs[1])]

    if k_slabs:
        n_slabs, s, slab = a.shape
        return _mm(name, "nn", (s, D_MODEL, slab), (_tile(s, 512), D_MODEL, slab),
                   [_KPart(a, j) for j in range(n_slabs)], [_KPart(w, j) for j in range(n_slabs)],
                   [(j, j, 0) for j in range(n_slabs)], 1, epilogue, [f32, f32], extras=[resid], vecs=[gamma, beta])
    s, k = a.shape
    return _mm(name, "nn", (s, D_MODEL, k), (_tile(s, FFN_ROWS), D_MODEL, _tile(k, 1024)),
               [a], [w], [(0, 0, 0)], 1, epilogue, [f32, f32], extras=[resid], vecs=[gamma, beta])


def _ffn_dact(name, dr, wd, silu, dsilu, u):
    s = dr.shape[0]

    def epilogue(accs, extras, vecs):
        da = 0.5 * accs[0]
        si, ds, uu = (e.astype(f32) for e in extras)
        return [da * uu * ds, da * si]

    return _mm(name, "nt", (s, D_FF, D_MODEL), (_tile(s, FFN_ROWS), FF_SLAB, D_MODEL), [dr], [wd],
               [(0, 0, 0)], 1, epilogue, [bf16, bf16], extras=[_Slabs(silu), _Slabs(dsilu), _Slabs(u)],
               split_cols=True)


def _mm2(name, mode, dims, a0, b0, a1, b1, add=None, add_coef=1.0, separate=False, tiles=None, out_dtype=f32,
         split_cols=False):
    m, n, k = dims
    tiles = tiles or (_tile(m, 512), _tile(n, 1024), _tile(k, 1024))

    def epilogue(accs, extras, vecs):
        if separate:
            return list(accs)
        r = accs[0]
        if extras:
            r = r + add_coef * extras[0]
        return [r]

    a_list = [a0] if a1 is None else [a0, a1]
    b_list = [b0] if b1 is None else [b0, b1]
    pairs = [(0, 0, 0), (len(a_list) - 1, len(b_list) - 1, 1 if separate else 0)]
    return _mm(name, mode, dims, tiles, a_list, b_list, pairs, 2 if separate else 1, epilogue,
               [out_dtype, out_dtype] if separate else [out_dtype], extras=[] if add is None else [add],
               split_cols=split_cols)


def _row_call(name, body, s, ins, params, outs, accs):
    tm = ROW_TILE
    ins = [a if isinstance(a, tuple) else (a, a.shape[1], 0) for a in ins]
    in_specs = [pl.BlockSpec((tm, width), lambda i, cb=cb: (i, cb)) for _, width, cb in ins]
    ins = [a for a, _, _ in ins]
    in_specs += [pl.BlockSpec(p.shape, lambda i, nd=p.ndim: (0,) * nd) for p in params]
    out_specs = [pl.BlockSpec((tm, o.shape[1]), lambda i: (i, 0)) for o in outs]
    out_specs += [pl.BlockSpec(a.shape, lambda i, nd=len(a.shape): (0,) * nd) for a in accs]
    return pl.pallas_call(
        body,
        name=name,
        grid=(s // tm,),
        in_specs=in_specs,
        out_specs=out_specs,
        out_shape=list(outs) + list(accs),
        compiler_params=_params("arbitrary"),
    )(*ins, *params)


def _zero_at_first(refs):
    @pl.when(pl.program_id(0) == 0)
    def _():
        for r in refs:
            r[...] = jnp.zeros_like(r)


def _ln_bwd(name, r, dh, gamma):
    s = r.shape[0]

    def body(r_ref, dh_ref, g_ref, dr_ref, dg_ref, db_ref):
        _zero_at_first([dg_ref, db_ref])
        rr = r_ref[...]
        dy = dh_ref[...]
        mu = jnp.mean(rr, axis=-1, keepdims=True)
        xc = rr - mu
        rstd = lax.rsqrt(jnp.mean(xc * xc, axis=-1, keepdims=True) + LN_EPS)
        xhat = xc * rstd
        dxh = dy * g_ref[...]
        dr_ref[...] = rstd * (dxh - jnp.mean(dxh, axis=-1, keepdims=True)
                              - xhat * jnp.mean(dxh * xhat, axis=-1, keepdims=True))
        dg_ref[...] += jnp.sum(dy * xhat, axis=0, keepdims=True)
        db_ref[...] += jnp.sum(dy, axis=0, keepdims=True)

    return _row_call(name, body, s, [r, dh], [gamma], [_sds((s, D_MODEL))], [_sds((1, D_MODEL)), _sds((1, D_MODEL))])


def _loss_head(name, y, target):
    s = y.shape[0]

    def body(y_ref, t_ref, dy_ref, l_ref):
        _zero_at_first([l_ref])
        e = y_ref[...] - t_ref[...]
        dy_ref[...] = e / D_MODEL
        l_ref[...] += 0.5 * jnp.sum(jnp.mean(e * e, axis=-1, keepdims=True), axis=0, keepdims=True)

    return _row_call(name, body, s, [y, target], [], [_sds((s, D_MODEL))], [_sds((1, 128))])


def _expm1(x):
    series = x * (1.0 + x / 2.0 * (1.0 + x / 3.0 * (1.0 + x / 4.0 * (1.0 + x / 5.0 * (1.0 + x / 6.0 * (1.0 + x / 7.0))))))
    return jnp.where(jnp.abs(x) < 0.25, series, jnp.exp(x) - 1.0)


def _gates_fn(xa, wa, wx, ba, bx, lam, tap_a, tap_x):
    xb = xa.astype(bf16)
    r = jax.nn.sigmoid(jnp.dot(xb, wa, preferred_element_type=f32) + ba + tap_a)
    i = jax.nn.sigmoid(jnp.dot(xb, wx, preferred_element_type=f32) + bx + tap_x)
    log_a = -RG_C * r * jax.nn.softplus(-lam)
    a = jnp.exp(log_a)
    gated = jnp.sqrt(-_expm1(2.0 * log_a)) * (i * xa)
    return a, gated


def _rg_gates(name, xa, wa, wx, ba, bx, lam):
    s = xa.shape[0]

    def body(xa_ref, wa_ref, wx_ref, ba_ref, bx_ref, lam_ref, a_ref, g_ref):
        a, g = _gates_fn(xa_ref[...], wa_ref[...], wx_ref[...], ba_ref[...], bx_ref[...], lam_ref[...], 0.0, 0.0)
        a_ref[...] = a
        g_ref[...] = g

    return _row_call(name, body, s, [xa], [wa, wx, ba, bx, lam], [_sds((s, D_A)), _sds((s, D_A))], [])


def _rg_gates_bwd(name, xa, ga, h_prev, wa, wx, ba, bx, lam):
    s = xa.shape[0]

    def body(xa_ref, ga_ref, hp_ref, wa_ref, wx_ref, ba_ref, bx_ref, lam_ref,
             dxa_ref, dwa_ref, dwx_ref, dba_ref, dbx_ref, dlam_ref):
        _zero_at_first([dwa_ref, dwx_ref, dba_ref, dbx_ref, dlam_ref])
        xa_v = xa_ref[...]
        zero = jnp.zeros((xa_v.shape[0], D_A), f32)
        fn = lambda x, ba_, bx_, lam_, ta, tx: _gates_fn(x, wa_ref[...], wx_ref[...], ba_, bx_, lam_, ta, tx)
        _, vjp = jax.vjp(fn, xa_v, ba_ref[...], bx_ref[...], lam_ref[...], zero, zero)
        gav = ga_ref[...]
        dxa, dba, dbx, dlam, dta, dtx = vjp((gav * hp_ref[...], gav))
        dxa_ref[...] = dxa
        xb = xa_v.astype(bf16)
        dwa_ref[...] += lax.dot_general(xb, dta.astype(bf16), _DN["tn"], preferred_element_type=f32)
        dwx_ref[...] += lax.dot_general(xb, dtx.astype(bf16), _DN["tn"], preferred_element_type=f32)
        dba_ref[...] += dba
        dbx_ref[...] += dbx
        dlam_ref[...] += dlam

    return _row_call(name, body, s, [xa, ga, h_prev], [wa, wx, ba, bx, lam], [_sds((s, D_A))],
                     [_sds((D_A, D_A)), _sds((D_A, D_A)), _sds((1, D_A)), _sds((1, D_A)), _sds((1, D_A))])


def _rms(v, g):
    return v * lax.rsqrt(jnp.mean(v * v, axis=-1, keepdims=True) + RMS_EPS) * g


def _mix_out_fn(ag, ha, ob, hre, him, cu, d, gn, tap_y, tap_gl, wcr, wci, wglu):
    out_a = jax.nn.gelu(ag) * ha
    y = (jnp.dot(hre.astype(bf16), wcr, preferred_element_type=f32)
         + jnp.dot(him.astype(bf16), wci, preferred_element_type=f32) + d * cu + tap_y)
    y2 = jax.nn.gelu(y)
    gl = jnp.dot(y2.astype(bf16), wglu, preferred_element_type=f32) + tap_gl
    out_c = y2 * jax.nn.sigmoid(gl)
    o = jnp.concatenate([_rms(out_a, gn[:, :D_A]), _rms(ob, gn[:, D_A:D_A + D_B]), _rms(out_c, gn[:, D_A + D_B:])],
                        axis=-1)
    return o, y2


def _mix_out(name, ag, ha, ob, hre, him, cu, d, gn, wcr, wci, wglu):
    s = ha.shape[0]

    def body(ag_ref, ha_ref, ob_ref, hre_ref, him_ref, cu_ref, d_ref, gn_ref, wcr_ref, wci_ref, wglu_ref, o_ref):
        o, _ = _mix_out_fn(ag_ref[...], ha_ref[...], ob_ref[...], hre_ref[...], him_ref[...], cu_ref[...], d_ref[...],
                           gn_ref[...], 0.0, 0.0, wcr_ref[...], wci_ref[...], wglu_ref[...])
        o_ref[...] = o.astype(o_ref.dtype)

    return _row_call(name, body, s, [ag, ha, ob, hre, him, cu], [d, gn, wcr, wci, wglu], [_sds((s, D_MODEL), bf16)], [])[0]


def _mix_out_bwd(name, do, ag, ha, ob, hre, him, cu, d, gn, wcr, wci, wglu):
    s = ha.shape[0]

    def body(do_ref, ag_ref, ha_ref, ob_ref, hre_ref, him_ref, cu_ref, d_ref, gn_ref, wcr_ref, wci_ref, wglu_ref,
             dag_ref, dha_ref, dob_ref, dhre_ref, dhim_ref, dcu_ref, dwcr_ref, dwci_ref, dwglu_ref, dd_ref, dgn_ref):
        _zero_at_first([dwcr_ref, dwci_ref, dwglu_ref, dd_ref, dgn_ref])
        tm = ag_ref.shape[0]
        zero = jnp.zeros((tm, D_C), f32)
        hre_v, him_v = hre_ref[...], him_ref[...]
        fn = lambda *a: _mix_out_fn(*a, wcr_ref[...], wci_ref[...], wglu_ref[...])
        _, vjp, y2 = jax.vjp(fn, ag_ref[...], ha_ref[...], ob_ref[...], hre_v, him_v, cu_ref[...], d_ref[...],
                             gn_ref[...], zero, zero, has_aux=True)
        dag, dha, dob, dhre, dhim, dcu, dd, dgn, dy, dgl = vjp(do_ref[...])
        dag_ref[...] = dag
        dha_ref[...] = dha
        dob_ref[...] = dob
        dhre_ref[...] = dhre
        dhim_ref[...] = dhim
        dcu_ref[...] = dcu
        dyb = dy.astype(bf16)
        dwcr_ref[...] += lax.dot_general(hre_v.astype(bf16), dyb, _DN["tn"], preferred_element_type=f32)
        dwci_ref[...] += lax.dot_general(him_v.astype(bf16), dyb, _DN["tn"], preferred_element_type=f32)
        dwglu_ref[...] += lax.dot_general(y2.astype(bf16), dgl.astype(bf16), _DN["tn"], preferred_element_type=f32)
        dd_ref[...] += dd
        dgn_ref[...] += dgn

    outs = [_sds((s, D_A)), _sds((s, D_A)), _sds((s, D_B)), _sds((s, S5_LANES)), _sds((s, S5_LANES)), _sds((s, D_C))]
    accs = [_sds((S5_LANES, D_C)), _sds((S5_LANES, D_C)), _sds((D_C, D_C)), _sds((1, D_C)), _sds((1, D_MODEL))]
    return _row_call(name, body, s, [do, ag, ha, ob, hre, him, cu], [d, gn, wcr, wci, wglu], outs, accs)


def _log_f(name, f, bf):
    s = f[0].shape[0]

    def body(f_ref, b_ref, o_ref):
        o_ref[...] = jax.nn.log_sigmoid(f_ref[...] + b_ref[...])

    return _row_call(name, body, s, [f], [bf], [_sds((s, 128))], [])[0]


def _log_f_bwd(name, dlf, f, bf):
    s = dlf.shape[0]

    def body(dl_ref, f_ref, b_ref, df_ref, db_ref):
        _zero_at_first([db_ref])
        df = dl_ref[...] * jax.nn.sigmoid(-(f_ref[...] + b_ref[...]))
        df_ref[...] = df
        db_ref[...] += jnp.sum(df, axis=0, keepdims=True)

    return _row_call(name, body, s, [dlf, f], [bf], [_sds((s, 128))], [_sds((1, 128))])


def _s5_decay_grad(name, h_re, h_im, g_re, g_im):
    s = g_re.shape[0]
    tm = ROW_TILE

    def body(hr_ref, hi_ref, hhr_ref, hhi_ref, gr_ref, gi_ref, dr_ref, di_ref):
        i = pl.program_id(0)
        _zero_at_first([dr_ref, di_ref])

        def previous(h_ref, halo_ref):
            halo = jnp.where(i == 0, 0.0, halo_ref[...])
            return pltpu.roll(jnp.concatenate([halo, h_ref[...]], axis=0), 1, 0)[8:, :]

        hr, hi, gr, gi = previous(hr_ref, hhr_ref), previous(hi_ref, hhi_ref), gr_ref[...], gi_ref[...]
        dr_ref[...] += jnp.sum(hr * gr + hi * gi, axis=0, keepdims=True)
        di_ref[...] += jnp.sum(hr * gi - hi * gr, axis=0, keepdims=True)

    rows = pl.BlockSpec((tm, S5_LANES), lambda i: (i, 0))
    halo = pl.BlockSpec((8, S5_LANES), lambda i: (jnp.maximum(i * (tm // 8) - 1, 0), 0))
    acc = pl.BlockSpec((1, S5_LANES), lambda i: (0, 0))
    return pl.pallas_call(
        body,
        name=name,
        grid=(s // tm,),
        in_specs=[rows, rows, halo, halo, rows, rows],
        out_specs=[acc, acc],
        out_shape=[_sds((1, S5_LANES)), _sds((1, S5_LANES))],
        compiler_params=_params("arbitrary"),
    )(h_re, h_im, h_re, h_im, g_re, g_im)


def _conv_fwd(name, ax, w, b):
    s = ax.shape[0]
    tm = ROW_TILE

    def body(x_ref, halo_ref, w_ref, b_ref, o_ref):
        i = pl.program_id(0)
        x = x_ref[...]
        halo = jnp.where(i == 0, 0.0, halo_ref[...])
        ext = jnp.concatenate([halo, x], axis=0)
        acc = b_ref[...] + w_ref[3:4, :] * x
        for k in range(CONV_WIDTH - 1):
            acc = acc + w_ref[k:k + 1, :] * pltpu.roll(ext, CONV_WIDTH - 1 - k, 0)[8:, :]
        o_ref[...] = acc

    return pl.pallas_call(
        body,
        name=name,
        grid=(s // tm,),
        in_specs=[pl.BlockSpec((tm, D_A), lambda i: (i, 0)),
                  pl.BlockSpec((8, D_A), lambda i: (jnp.maximum(i * (tm // 8) - 1, 0), 0)),
                  pl.BlockSpec((CONV_WIDTH, D_A), lambda i: (0, 0)),
                  pl.BlockSpec((1, D_A), lambda i: (0, 0))],
        out_specs=pl.BlockSpec((tm, D_A), lambda i: (i, 0)),
        out_shape=_sds((s, D_A)),
        compiler_params=_params("arbitrary"),
    )(ax, ax, w, b)


def _conv_bwd(name, dxa, ax, w):
    s = ax.shape[0]
    tm = ROW_TILE
    nblk = s // tm

    def body(dx_ref, dnext_ref, x_ref, halo_ref, w_ref, dax_ref, dw_ref):
        i = pl.program_id(0)
        _zero_at_first([dw_ref])
        dx = dx_ref[...]
        dnext = jnp.where(i == nblk - 1, 0.0, dnext_ref[...])
        dext = jnp.concatenate([dx, dnext], axis=0)
        x = x_ref[...]
        halo = jnp.where(i == 0, 0.0, halo_ref[...])
        ext = jnp.concatenate([halo, x], axis=0)
        acc = w_ref[3:4, :] * dx
        dw_ref[3:4, :] += jnp.sum(dx * x, axis=0, keepdims=True)
        for k in range(CONV_WIDTH - 1):
            sh = CONV_WIDTH - 1 - k
            acc = acc + w_ref[k:k + 1, :] * pltpu.roll(dext, tm + 8 - sh, 0)[:tm, :]
            dw_ref[k:k + 1, :] += jnp.sum(dx * pltpu.roll(ext, sh, 0)[8:, :], axis=0, keepdims=True)
        dw_ref[4:5, :] += jnp.sum(dx, axis=0, keepdims=True)
        dax_ref[...] = acc

    return pl.pallas_call(
        body,
        name=name,
        grid=(nblk,),
        in_specs=[pl.BlockSpec((tm, D_A), lambda i: (i, 0)),
                  pl.BlockSpec((8, D_A), lambda i: (jnp.minimum((i + 1) * (tm // 8), s // 8 - 1), 0)),
                  pl.BlockSpec((tm, D_A), lambda i: (i, 0)),
                  pl.BlockSpec((8, D_A), lambda i: (jnp.maximum(i * (tm // 8) - 1, 0), 0)),
                  pl.BlockSpec((CONV_WIDTH, D_A), lambda i: (0, 0))],
        out_specs=[pl.BlockSpec((tm, D_A), lambda i: (i, 0)), pl.BlockSpec((8, D_A), lambda i: (0, 0))],
        out_shape=[_sds((s, D_A)), _sds((8, D_A))],
        compiler_params=_params("arbitrary"),
    )(dxa, dxa, ax, ax, w)


SCAN_ROWS = 512


def _row_in_tile(shape):
    return lax.broadcasted_iota(jnp.int32, shape, 0) % 8


def _lin_scan(name, a, b, reverse):
    s, c = a.shape
    t = min(SCAN_ROWS, s)
    nb = s // t

    def body(a_ref, b_ref, h_ref, p_ref, carry_ref):
        @pl.when(pl.program_id(0) == 0)
        def _():
            carry_ref[...] = jnp.zeros_like(carry_ref)

        row = _row_in_tile((t, c))
        p = a_ref[...]
        h = b_ref[...]
        for d in (1, 2, 4):
            keep = (row < 8 - d) if reverse else (row >= d)
            shift = (t - d) if reverse else d
            h = h + jnp.where(keep, p * pltpu.roll(h, shift, 0), 0.0)
            p = jnp.where(keep, p * pltpu.roll(p, shift, 0), p)
        h_ref[...] = h
        p_ref[...] = p
        edge = 0 if reverse else 7

        def tile(k, carry):
            kk = (t // 8 - 1 - k) if reverse else k
            r0 = pl.multiple_of(kk * 8, 8)
            hh = h_ref[pl.ds(r0, 8), :] + p_ref[pl.ds(r0, 8), :] * carry
            h_ref[pl.ds(r0, 8), :] = hh
            return jnp.broadcast_to(hh[edge:edge + 1, :], (8, c))

        carry_ref[...] = lax.fori_loop(0, t // 8, tile, carry_ref[...])

    spec = pl.BlockSpec((t, c), (lambda i: (nb - 1 - i, 0)) if reverse else (lambda i: (i, 0)))
    (out,) = _call(
        body,
        name=name,
        grid=(nb,),
        in_specs=[spec, spec],
        out_specs=[spec],
        out_shape=[_sds((s, c))],
        scratch_shapes=[pltpu.VMEM((t, c), f32), pltpu.VMEM((8, c), f32)],
        compiler_params=_params("arbitrary"),
    )(a, b)
    return out


def _s5_scan(name, b_re, b_im, a_re, a_im, reverse):
    s, c = b_re.shape
    t = min(SCAN_ROWS, s)
    nb = s // t

    def body(br_ref, bi_ref, ar_ref, ai_ref, hr_ref, hi_ref, cr_ref, ci_ref):
        @pl.when(pl.program_id(0) == 0)
        def _():
            cr_ref[...] = jnp.zeros_like(cr_ref)
            ci_ref[...] = jnp.zeros_like(ci_ref)

        ar1, ai1 = ar_ref[...], ai_ref[...]
        pows = [(ar1, ai1)]
        for _ in range(7):
            pr, pi = pows[-1]
            pows.append((pr * ar1 - pi * ai1, pr * ai1 + pi * ar1))
        row8 = lax.broadcasted_iota(jnp.int32, (8, c), 0)
        wr = jnp.zeros((8, c), f32)
        wi = jnp.zeros((8, c), f32)
        for r in range(8):
            pr, pi = pows[(7 - r) if reverse else r]
            wr = jnp.where(row8 == r, pr, wr)
            wi = jnp.where(row8 == r, pi, wi)
        row = _row_in_tile((t, c))
        hr = br_ref[...]
        hi = bi_ref[...]
        for d in (1, 2, 4):
            keep = (row < 8 - d) if reverse else (row >= d)
            shift = (t - d) if reverse else d
            pr, pi = pows[d - 1]
            cr = jnp.where(keep, pr, 0.0)
            ci = jnp.where(keep, pi, 0.0)
            sr = pltpu.roll(hr, shift, 0)
            si = pltpu.roll(hi, shift, 0)
            hr, hi = hr + cr * sr - ci * si, hi + cr * si + ci * sr
        hr_ref[...] = hr
        hi_ref[...] = hi
        edge = 0 if reverse else 7

        def tile(k, carry):
            car_r, car_i = carry
            kk = (t // 8 - 1 - k) if reverse else k
            r0 = pl.multiple_of(kk * 8, 8)
            xr = hr_ref[pl.ds(r0, 8), :] + wr * car_r - wi * car_i
            xi = hi_ref[pl.ds(r0, 8), :] + wr * car_i + wi * car_r
            hr_ref[pl.ds(r0, 8), :] = xr
            hi_ref[pl.ds(r0, 8), :] = xi
            return (jnp.broadcast_to(xr[edge:edge + 1, :], (8, c)), jnp.broadcast_to(xi[edge:edge + 1, :], (8, c)))

        car_r, car_i = lax.fori_loop(0, t // 8, tile, (cr_ref[...], ci_ref[...]))
        cr_ref[...] = car_r
        ci_ref[...] = car_i

    spec = pl.BlockSpec((t, c), (lambda i: (nb - 1 - i, 0)) if reverse else (lambda i: (i, 0)))
    vspec = pl.BlockSpec((1, c), lambda i: (0, 0))
    hr, hi = _call(
        body,
        name=name,
        grid=(nb,),
        in_specs=[spec, spec, vspec, vspec],
        out_specs=[spec, spec],
        out_shape=[_sds((s, c)), _sds((s, c))],
        scratch_shapes=[pltpu.VMEM((8, c), f32), pltpu.VMEM((8, c), f32)],
        compiler_params=_params("arbitrary"),
    )(b_re, b_im, a_re, a_im)
    return hr, hi


ATT_FEAT = 128
ATT_TQ = 1024
ATT_TK = 1024
ATT_TK_KEY_SIDE = 512


def _att_tiles(s, key_side=False):
    tq = min(ATT_TQ, s)
    tk = min(ATT_TK_KEY_SIDE if key_side else ATT_TK, tq)
    return tq, tk, tq // tk


def _keys_le_queries(tk, tq, k0, q0):
    row = lax.broadcasted_iota(jnp.int32, (tk, tq), 0) + k0
    col = lax.broadcasted_iota(jnp.int32, (tk, tq), 1) + q0
    return row <= col


def _attn_fwd_t(name, qt, k_aug, vt):
    h, s, _ = k_aug.shape
    tq, tk, ratio = _att_tiles(s)

    def body(qt_ref, k_ref, vt_ref, o_ref, lse_ref):
        qi = pl.program_id(1)
        qt = qt_ref[...]

        def block(kb, carry, masked):
            m, l, acc = carry
            ks = pl.multiple_of(kb * tk, tk)
            st = jnp.dot(k_ref[pl.ds(ks, tk), :], qt, preferred_element_type=f32)
            if masked:
                st = jnp.where(_keys_le_queries(tk, tq, ks, qi * tq), st, -jnp.inf)
            mn = jnp.maximum(m, jnp.max(st, axis=0, keepdims=True))
            p = jnp.exp(st - mn)
            al = jnp.exp(m - mn)
            l = al * l + jnp.sum(p, axis=0, keepdims=True)
            acc = al * acc + jnp.dot(vt_ref[kb], p.astype(bf16), preferred_element_type=f32)
            return mn, l, acc

        init = (jnp.full((1, tq), -jnp.inf, f32), jnp.zeros((1, tq), f32), jnp.zeros((HEAD_DIM, tq), f32))
        first = lax.fori_loop(0, qi * ratio, lambda kb, c: block(kb, c, False), init)
        m, l, acc = lax.fori_loop(qi * ratio, (qi + 1) * ratio, lambda kb, c: block(kb, c, True), first)
        o_ref[...] = acc / l
        lse_ref[...] = m + jnp.log(l)

    return _call(
        body,
        name=name,
        grid=(h, s // tq),
        in_specs=[pl.BlockSpec((None, None, ATT_FEAT, tq), lambda hh, i: (hh, i, 0, 0)),
                  pl.BlockSpec((None, s, ATT_FEAT), lambda hh, i: (hh, 0, 0)),
                  pl.BlockSpec((None, s // tk, HEAD_DIM, tk), lambda hh, i: (hh, 0, 0, 0))],
        out_specs=[pl.BlockSpec((None, HEAD_DIM, tq), lambda hh, i: (hh, 0, i)),
                   pl.BlockSpec((None, 1, tq), lambda hh, i: (hh, 0, i))],
        out_shape=[_sds((h, HEAD_DIM, s)), _sds((h, 1, s))],
        compiler_params=_params("parallel", "arbitrary"),
    )(qt, k_aug, vt)


def _attn_bwd_dq_t(name, qt, k_aug, v, kt, ot, dot_, lse):
    h, s, _ = k_aug.shape
    tq, tk, ratio = _att_tiles(s)

    def body(qt_ref, k_ref, v_ref, kt_ref, o_ref, do_ref, lse_ref, dq_ref, dl_ref):
        qi = pl.program_id(1)
        qt = qt_ref[...]
        dob = do_ref[...]
        delta = jnp.sum(dob.astype(f32) * o_ref[...], axis=0, keepdims=True)
        lse_v = lse_ref[...]

        def block(kb, carry, masked):
            dq, psum = carry
            ks = pl.multiple_of(kb * tk, tk)
            st = jnp.dot(k_ref[pl.ds(ks, tk), :], qt, preferred_element_type=f32)
            p = jnp.exp(st - lse_v)
            if masked:
                p = jnp.where(_keys_le_queries(tk, tq, ks, qi * tq), p, 0.0)
            dp = jnp.dot(v_ref[pl.ds(ks, tk), :], dob, preferred_element_type=f32)
            ds = p * (dp - delta)
            return (dq + jnp.dot(kt_ref[kb], ds.astype(bf16), preferred_element_type=f32),
                    psum + jnp.sum(p * dp, axis=0, keepdims=True))

        carry = lax.fori_loop(0, qi * ratio, lambda kb, c: block(kb, c, False),
                              (jnp.zeros((HEAD_DIM, tq), f32), jnp.zeros((1, tq), f32)))
        dq, psum = lax.fori_loop(qi * ratio, (qi + 1) * ratio, lambda kb, c: block(kb, c, True), carry)
        dq_ref[...] = dq * ATT_SCALE
        dl_ref[...] = psum

    qspec = pl.BlockSpec((None, HEAD_DIM, tq), lambda hh, i: (hh, 0, i))
    rspec = pl.BlockSpec((None, 1, tq), lambda hh, i: (hh, 0, i))
    return _call(
        body,
        name=name,
        grid=(h, s // tq),
        in_specs=[pl.BlockSpec((None, None, ATT_FEAT, tq), lambda hh, i: (hh, i, 0, 0)),
                  pl.BlockSpec((None, s, ATT_FEAT), lambda hh, i: (hh, 0, 0)),
                  pl.BlockSpec((None, s, HEAD_DIM), lambda hh, i: (hh, 0, 0)),
                  pl.BlockSpec((None, s // tk, HEAD_DIM, tk), lambda hh, i: (hh, 0, 0, 0)),
                  qspec, pl.BlockSpec((None, None, HEAD_DIM, tq), lambda hh, i: (hh, i, 0, 0)), rspec],
        out_specs=[qspec, rspec],
        out_shape=[_sds((h, HEAD_DIM, s)), _sds((h, 1, s))],
        compiler_params=_params("parallel", "arbitrary"),
    )(qt, k_aug, v, kt, ot, dot_, lse)


def _attn_bwd_dkv_t(name, qt_blocks, k_aug, v, qh, do, dot_blocks, lse, delta):
    h, s, _ = k_aug.shape
    tq, tk, ratio = _att_tiles(s, key_side=True)
    nq = s // tq

    def body(qt_ref, k_ref, v_ref, q_ref, do_ref, dot_ref, lse_ref, dl_ref, dk_ref, dv_ref, dck_ref, dsum_ref):
        kj = pl.program_id(1)
        kk = k_ref[...]
        vv = v_ref[...]
        dsum_ref[...] = jnp.zeros_like(dsum_ref)

        def block(qi, carry, masked):
            dk, dv = carry
            qs = pl.multiple_of(qi * tq, tq)
            st = jnp.dot(kk, qt_ref[qi], preferred_element_type=f32)
            p = jnp.exp(st - lse_ref[qi])
            if masked:
                p = jnp.where(_keys_le_queries(tk, tq, kj * tk, qs), p, 0.0)
            dv = dv + jnp.dot(p.astype(bf16), do_ref[pl.ds(qs, tq), :], preferred_element_type=f32)
            dp = jnp.dot(vv, dot_ref[qi], preferred_element_type=f32)
            ds = p * (dp - dl_ref[qi])
            dsum_ref[...] += ds
            dk = dk + jnp.dot(ds.astype(bf16), q_ref[pl.ds(qs, tq), :], preferred_element_type=f32)
            return dk, dv

        first = kj // ratio
        carry = block(first, (jnp.zeros((tk, HEAD_DIM), f32), jnp.zeros((tk, HEAD_DIM), f32)), True)
        dk, dv = lax.fori_loop(first + 1, nq, lambda qi, c: block(qi, c, False), carry)
        dk_ref[...] = dk
        dv_ref[...] = dv
        col = jnp.sum(dsum_ref[...], axis=1, keepdims=True)
        dck_ref[...] = -jnp.transpose(jnp.broadcast_to(col, (tk, 128)))[0:1, :]

    full = lambda shape: pl.BlockSpec((None,) + shape, lambda hh, j: (hh,) + (0,) * len(shape))
    kspec = pl.BlockSpec((None, tk, HEAD_DIM), lambda hh, j: (hh, j, 0))
    return _call(
        body,
        name=name,
        grid=(h, s // tk),
        in_specs=[full((nq, ATT_FEAT, tq)),
                  pl.BlockSpec((None, tk, ATT_FEAT), lambda hh, j: (hh, j, 0)),
                  kspec, full((s, HEAD_DIM)), full((s, HEAD_DIM)), full((nq, HEAD_DIM, tq)),
                  full((nq, 1, tq)), full((nq, 1, tq))],
        out_specs=[kspec, kspec, pl.BlockSpec((None, None, 1, tk), lambda hh, j: (hh, j, 0, 0))],
        out_shape=[_sds((h, s, HEAD_DIM)), _sds((h, s, HEAD_DIM)), _sds((h, s // tk, 1, tk))],
        scratch_shapes=[pltpu.VMEM((tk, tq), f32)],
        compiler_params=_params("parallel", "arbitrary"),
    )(qt_blocks, k_aug, v, qh, do, dot_blocks, lse, delta)


C_LANES = 128


def _selections():
    h = jnp.arange(N_HEADS)[:, None, None]
    row = jnp.arange(D_B + 3 * C_LANES)[None, :, None]
    col = jnp.arange(ATT_FEAT)[None, None, :]
    head_col = (row < D_B) & (row // HEAD_DIM == h) & (col == row % HEAD_DIM)

    def c_part(p, lane0):
        return (row == D_B + p * C_LANES + h) & (col == lane0 + p)

    c_q = c_part(0, HEAD_DIM) | c_part(1, HEAD_DIM) | c_part(2, HEAD_DIM)
    c_k = c_part(0, HEAD_DIM + 3) | c_part(1, HEAD_DIM + 3) | c_part(2, HEAD_DIM + 3)
    sel_q = (head_col | c_q).astype(bf16)
    sel_k = head_col.astype(bf16) - c_k.astype(bf16)
    sel_h = head_col[:, :D_B, :HEAD_DIM].astype(bf16)
    lane = jnp.arange(ATT_FEAT)
    ones_q = ((lane >= HEAD_DIM + 3) & (lane < HEAD_DIM + 6)).astype(f32)
    ones_k = ((lane >= HEAD_DIM) & (lane < HEAD_DIM + 3)).astype(f32)
    return dict(sel_qt=sel_q.transpose(0, 2, 1), sel_k=sel_k, sel_h=sel_h, sel_ht=sel_h.transpose(0, 2, 1),
                ones_q=ones_q.reshape(ATT_FEAT, 1), ones_k=ones_k.reshape(1, ATT_FEAT))


def _attn_prep(name, z, c, sel):
    s = z.shape[0]
    tq, tk, ratio = _att_tiles(s)

    def body(q_ref, k_ref, v_ref, c_ref, sqt_ref, sk_ref, sh_ref, sht_ref, oq_ref, ok_ref,
             qt_out, ka_out, kt_out, vt_out, v_out, qh_out):
        cv = c_ref[...]
        hi = cv.astype(bf16)
        r1 = cv - hi.astype(f32)
        mid = r1.astype(bf16)
        lo = (r1 - mid.astype(f32)).astype(bf16)
        qs = (q_ref[...] * ATT_SCALE).astype(bf16)
        kb = k_ref[...].astype(bf16)
        vb = v_ref[...].astype(bf16)
        xq = jnp.concatenate([qs, hi, mid, lo], axis=-1)
        xk = jnp.concatenate([kb, hi, mid, lo], axis=-1)
        for h in range(N_HEADS):
            qt = lax.dot_general(sqt_ref[h], xq, _DN["nt"], preferred_element_type=f32) + oq_ref[...]
            qt_out[h, 0] = qt.astype(bf16)
            ka_out[h] = (jnp.dot(xk, sk_ref[h], preferred_element_type=f32) + ok_ref[...]).astype(bf16)
            kt = lax.dot_general(sht_ref[h], kb, _DN["nt"], preferred_element_type=f32).astype(bf16)
            vt = lax.dot_general(sht_ref[h], vb, _DN["nt"], preferred_element_type=f32).astype(bf16)
            for j in range(ratio):
                kt_out[h, j] = kt[:, j * tk:(j + 1) * tk]
                vt_out[h, j] = vt[:, j * tk:(j + 1) * tk]
            v_out[h] = jnp.dot(vb, sh_ref[h], preferred_element_type=f32).astype(bf16)
            qh_out[h] = jnp.dot(qs, sh_ref[h], preferred_element_type=f32).astype(bf16)

    whole = lambda a: pl.BlockSpec(a.shape, lambda i, nd=a.ndim: (0,) * nd)
    consts = [sel["sel_qt"], sel["sel_k"], sel["sel_h"], sel["sel_ht"], sel["ones_q"], sel["ones_k"]]
    return pl.pallas_call(
        body,
        name=name,
        grid=(s // tq,),
        in_specs=[pl.BlockSpec((tq, D_B), lambda i: (i, 2)), pl.BlockSpec((tq, D_B), lambda i: (i, 3)),
                  pl.BlockSpec((tq, D_B), lambda i: (i, 4)), pl.BlockSpec((tq, C_LANES), lambda i: (i, 0))]
        + [whole(a) for a in consts],
        out_specs=[pl.BlockSpec((N_HEADS, 1, ATT_FEAT, tq), lambda i: (0, i, 0, 0)),
                   pl.BlockSpec((N_HEADS, tq, ATT_FEAT), lambda i: (0, i, 0)),
                   pl.BlockSpec((N_HEADS, ratio, HEAD_DIM, tk), lambda i: (0, i, 0, 0)),
                   pl.BlockSpec((N_HEADS, ratio, HEAD_DIM, tk), lambda i: (0, i, 0, 0)),
                   pl.BlockSpec((N_HEADS, tq, HEAD_DIM), lambda i: (0, i, 0)),
                   pl.BlockSpec((N_HEADS, tq, HEAD_DIM), lambda i: (0, i, 0))],
        out_shape=[_sds((N_HEADS, s // tq, ATT_FEAT, tq), bf16), _sds((N_HEADS, s, ATT_FEAT), bf16),
                   _sds((N_HEADS, s // tk, HEAD_DIM, tk), bf16), _sds((N_HEADS, s // tk, HEAD_DIM, tk), bf16),
                   _sds((N_HEADS, s, HEAD_DIM), bf16), _sds((N_HEADS, s, HEAD_DIM), bf16)],
        compiler_params=_params("parallel"),
    )(z, z, z, c, *consts)


def _attn_do_prep(name, dob, sel):
    s = dob.shape[0]
    tq = _att_tiles(s)[0]

    def body(do_ref, sh_ref, sht_ref, dot_out, do_out):
        db = do_ref[...].astype(bf16)
        for h in range(N_HEADS):
            dot_out[h, 0] = lax.dot_general(sht_ref[h], db, _DN["nt"], preferred_element_type=f32).astype(bf16)
            do_out[h] = jnp.dot(db, sh_ref[h], preferred_element_type=f32).astype(bf16)

    whole = lambda a: pl.BlockSpec(a.shape, lambda i, nd=a.ndim: (0,) * nd)
    return pl.pallas_call(
        body,
        name=name,
        grid=(s // tq,),
        in_specs=[pl.BlockSpec((tq, D_B), lambda i: (i, 0)), whole(sel["sel_h"]), whole(sel["sel_ht"])],
        out_specs=[pl.BlockSpec((N_HEADS, 1, HEAD_DIM, tq), lambda i: (0, i, 0, 0)),
                   pl.BlockSpec((N_HEADS, tq, HEAD_DIM), lambda i: (0, i, 0))],
        out_shape=[_sds((N_HEADS, s // tq, HEAD_DIM, tq), bf16), _sds((N_HEADS, s, HEAD_DIM), bf16)],
        compiler_params=_params("parallel"),
    )(dob, sel["sel_h"], sel["sel_ht"])


def _dz_assemble(name, dax, dag, dqt, dkh, dvh, df, dcu, sel):
    s = dax.shape[0]
    tm = _tile(s, 512)

    def body(dax_ref, dag_ref, dqt_ref, dk_ref, dv_ref, df_ref, dcu_ref, sht_ref, o_ref):
        dq = jnp.zeros((tm, D_B), f32)
        dk = jnp.zeros((tm, D_B), f32)
        dv = jnp.zeros((tm, D_B), f32)
        for h in range(N_HEADS):
            place = sht_ref[h]
            dq = dq + lax.dot_general(dqt_ref[h].astype(bf16), place, _DN["tn"], preferred_element_type=f32)
            dk = dk + jnp.dot(dk_ref[h].astype(bf16), place, preferred_element_type=f32)
            dv = dv + jnp.dot(dv_ref[h].astype(bf16), place, preferred_element_type=f32)
        pieces = [dax_ref[...], dag_ref[...], dq, dk, dv, df_ref[...], dcu_ref[...]]
        off = 0
        for p in pieces:
            o_ref[:, off:off + p.shape[1]] = p.astype(bf16)
            off += p.shape[1]

    rows = lambda c_: pl.BlockSpec((tm, c_), lambda i: (i, 0))
    heads = pl.BlockSpec((N_HEADS, tm, HEAD_DIM), lambda i: (0, i, 0))
    return pl.pallas_call(
        body,
        name=name,
        grid=(s // tm,),
        in_specs=[rows(D_A), rows(D_A), pl.BlockSpec((N_HEADS, HEAD_DIM, tm), lambda i: (0, 0, i)), heads, heads,
                  rows(128), rows(D_C), pl.BlockSpec(sel["sel_ht"].shape, lambda i: (0, 0, 0))],
        out_specs=rows(N_IN_P),
        out_shape=_sds((s, N_IN_P), bf16),
        compiler_params=_params("parallel"),
    )(dax, dag, dqt, dkh, dvh, df, dcu, sel["sel_ht"])


def _s5_disc_fn(are, aim, ldt):
    dt = jnp.exp(ldt)
    er = jnp.exp(are * dt)
    br = er * jnp.cos(aim * dt)
    bi = er * jnp.sin(aim * dt)
    nr = br - 1.0
    den = are * are + aim * aim
    return br, bi, (nr * are + bi * aim) / den, (bi * are - nr * aim) / den


def _s5_disc(name, are, aim, ldt):
    def body(a_ref, b_ref, c_ref, o0, o1, o2, o3):
        r = _s5_disc_fn(a_ref[...], b_ref[...], c_ref[...])
        o0[...], o1[...], o2[...], o3[...] = r

    shp = _sds((S5_GROUPS, S5_STATE))
    return pl.pallas_call(body, name=name, out_shape=[shp] * 4)(are, aim, ldt)


def _s5_disc_bwd(name, are, aim, ldt, cts):
    def body(a_ref, b_ref, c_ref, d0, d1, d2, d3, o0, o1, o2):
        _, vjp = jax.vjp(_s5_disc_fn, a_ref[...], b_ref[...], c_ref[...])
        o0[...], o1[...], o2[...] = vjp((d0[...], d1[...], d2[...], d3[...]))

    shp = _sds((S5_GROUPS, S5_STATE))
    return pl.pallas_call(body, name=name, out_shape=[shp, shp, _sds((S5_GROUPS, 1))])(are, aim, ldt, *cts)


def _adamw_rows(w, g, m, v):
    m = ADAM_B1 * m + (1.0 - ADAM_B1) * g
    v = ADAM_B2 * v + (1.0 - ADAM_B2) * (g * g)
    m_hat = m / (1.0 - ADAM_B1 ** ADAM_STEP)
    v_hat = v / (1.0 - ADAM_B2 ** ADAM_STEP)
    return -ADAM_LR * (m_hat / (jnp.sqrt(v_hat) + ADAM_EPS) + ADAM_WD * w), m, v


def _adamw(name, w, ga, gb, m, v):
    rows, cols = w.shape
    tr = _row_tile(rows)

    def body(w_ref, ga_ref, gb_ref, m_ref, v_ref, g_out, d_out, m_out, v_out):
        g = ga_ref[...] + gb_ref[...]
        d, mm, vv = _adamw_rows(w_ref[...], g, m_ref[...], v_ref[...])
        g_out[...] = g
        d_out[...] = d
        m_out[...] = mm
        v_out[...] = vv

    spec = pl.BlockSpec((tr, cols), lambda i: (i, 0))
    return pl.pallas_call(
        body, name=name, grid=(rows // tr,), in_specs=[spec] * 5, out_specs=[spec] * 4,
        out_shape=[_sds((rows, cols))] * 4, compiler_params=_params("parallel"),
    )(w, ga, gb, m, v)


def _sum_stack(name, st):
    n, rows, cols = st.shape
    tr = _row_tile(rows)

    def body(s_ref, o_ref):
        acc = s_ref[0].astype(f32)
        for j in range(1, n):
            acc = acc + s_ref[j].astype(f32)
        o_ref[...] = acc

    return pl.pallas_call(
        body, name=name, grid=(rows // tr,), in_specs=[pl.BlockSpec((n, tr, cols), lambda i: (0, i, 0))],
        out_specs=pl.BlockSpec((tr, cols), lambda i: (i, 0)), out_shape=_sds((rows, cols)),
        compiler_params=_params("parallel"),
    )(st)


def _block_diag(w):
    h, n, m = w.shape
    return jnp.einsum("hij,hg->higj", w, jnp.eye(h, dtype=w.dtype)).reshape(h * n, h * m)


def _block_diag_part(dense, h):
    n, m = dense.shape[0] // h, dense.shape[1] // h
    return jnp.einsum("higj,hg->hij", dense.reshape(h, n, h, m), jnp.eye(h, dtype=dense.dtype))


def _s5_matrices(coef_re, coef_im, b_re, b_im, c_re, c_im):
    bb_re = coef_re[:, :, None] * b_re - coef_im[:, :, None] * b_im
    bb_im = coef_re[:, :, None] * b_im + coef_im[:, :, None] * b_re
    wb_re = _block_diag(jnp.swapaxes(bb_re, 1, 2))
    wb_im = _block_diag(jnp.swapaxes(bb_im, 1, 2))
    wc_re = _block_diag(jnp.swapaxes(c_re, 1, 2))
    wc_im = _block_diag(jnp.swapaxes(-c_im, 1, 2))
    return wb_re, wb_im, wc_re, wc_im


def _shift_down(t):
    return jnp.concatenate([jnp.zeros((1, t.shape[1]), t.dtype), t[:-1]], axis=0)


def _shift_up(t):
    return jnp.concatenate([t[1:], jnp.zeros((1, t.shape[1]), t.dtype)], axis=0)


def _row(v):
    return v.reshape(1, -1)


def _ffn_fwd(tag, h, get, names, gamma, beta, gate_first=False):
    wg = get(names[0])
    if gate_first:
        g = _ffn_gate(tag + "_gate", h, wg)
        wu = get(names[1])
        silu, dsilu, u, act = _ffn_up_given_gate(tag + "_up", h, wu, g)
    else:
        wu = get(names[1])
        silu, dsilu, u, act = _ffn_up(tag + "_up", h, wg, wu)
    wd = get(names[2])
    r, out = _mm_ln(tag + "_down", act, wd, h, gamma, beta, 0.5, k_slabs=True)
    return out, dict(h=h, silu=silu, dsilu=dsilu, u=u, act=act, r=r, wg=wg, wu=wu, wd=wd)


def _ffn_bwd(tag, dout, sv, names, gamma, put, after_ln=None):
    s = dout.shape[0]
    dr, dgam, dbet = _ln_bwd(tag + "_lnb", sv["r"], dout, gamma)
    if after_ln is not None:
        after_ln(dgam, dbet)
    put(names[2], _mm_plain(tag + "_dwd", "tn", _Slabs(sv["act"]), dr, (D_FF, D_MODEL, s), scale=0.5, out_dtype=bf16,
                            tiles=(FF_SLAB, 1024, _tile(s, 2048))))
    dg, du = _ffn_dact(tag + "_dact", dr, sv["wd"], sv["silu"], sv["dsilu"], sv["u"])
    dwg, dwu = _mm2(tag + "_dwgu", "tn", (D_FF, D_MODEL, s), _Slabs(dg), sv["h"], _Slabs(du), None, separate=True,
                    out_dtype=bf16, tiles=(FF_SLAB, 1024, _tile(s, 2048)))
    put(names[0], dwg)
    put(names[1], dwu)
    slabs = range(dg.shape[0])
    dh = _mm(tag + "_dh", "nn", (s, D_MODEL, FF_SLAB), (_tile(s, 512), D_MODEL, FF_SLAB),
             [_KPart(dg, j) for j in slabs] + [_KPart(du, j) for j in slabs],
             [_KPart(sv["wg"], j) for j in slabs] + [_KPart(sv["wu"], j) for j in slabs],
             [(j, j, 0) for j in range(2 * len(slabs))], 1,
             lambda accs, extras, vecs: [accs[0] + ALPHA * extras[0]], [f32], extras=[dr])[0]
    return dh, dgam, dbet


def _mixer_fwd(tag, h1, w):
    s = h1.shape[0]
    z = _mm_plain(tag + "_win", "nn", h1, w["w_in"], (s, N_IN_P, D_MODEL), tiles=(_tile(s, 512), 768, D_MODEL))
    ag, f, cu_cols = (z, D_A, 1), (z, 128, F_OFF // 128), (z, D_C, CU_OFF // D_C)
    cu = z[:, CU_OFF:]
    xa = _conv_fwd(tag + "_conv", z, w["conv_w"], w["conv_b"])
    a, gated = _rg_gates(tag + "_gates", xa, w["rg_wa"], w["rg_wx"], w["rg_ba"], w["rg_bx"], w["rg_lam"])
    ha = _lin_scan(tag + "_rgscan", a, gated, False)
    ones = jnp.ones((s, 128), f32)
    c = _lin_scan(tag + "_cumf", ones, _log_f(tag + "_logf", f, w["fox_bf"]), False)
    att = dict(zip(("qt", "k_aug", "kt", "vt", "v", "qh"), _attn_prep(tag + "_attnprep", z, c, w["sel"])))
    ot, lse = _attn_fwd_t(tag + "_attn", att["qt"], att["k_aug"], att["vt"])
    ob = ot.reshape(D_B, s).T
    bu_re, bu_im = _mm2(tag + "_s5in", "nn", (s, S5_LANES, D_C), cu, w["wb_re"], None, w["wb_im"], separate=True,
                        tiles=(_tile(s, 512), 1024, D_C))
    hre, him = _s5_scan(tag + "_s5scan", bu_re, bu_im, w["abar_re"], w["abar_im"], False)
    o = _mix_out(tag + "_mixout", ag, ha, ob, hre, him, cu_cols, w["s5_d"], w["mix_g"], w["wc_re"], w["wc_im"],
                 w["w_glu"])
    sv = dict(h1=h1, z=z, ag=ag, f=f, cu=cu, cu_cols=cu_cols, xa=xa, a=a, ha=ha, att=att, ot=ot, lse=lse, ob=ob,
              hre=hre, him=him, o=o)
    return o, sv


def _mixer_bwd(tag, do, dr2, sv, w, put):
    s = do.shape[0]
    (dag, dha, dob, dhre, dhim, dcu1, dwcr, dwci, dwglu, dd, dgn) = _mix_out_bwd(
        tag + "_mixoutb", do, sv["ag"], sv["ha"], sv["ob"], sv["hre"], sv["him"], sv["cu_cols"], w["s5_d"], w["mix_g"],
        w["wc_re"], w["wc_im"], w["w_glu"])
    put("s5_w_glu", dwglu.astype(bf16))
    gre, gim = _s5_scan(tag + "_s5scanb", dhre, dhim, w["abar_re"], -w["abar_im"], True)
    dab_re, dab_im = _s5_decay_grad(tag + "_s5dec", sv["hre"], sv["him"], gre, gim)
    dwb_re, dwb_im = _mm2(tag + "_s5dwb", "tn", (D_C, S5_LANES, s), sv["cu"], gre, None, gim, separate=True,
                          tiles=(D_C, 1024, _tile(s, 1024)))
    dcu = _mm2(tag + "_s5dcu", "nt", (s, D_C, S5_LANES), gre, w["wb_re"], gim, w["wb_im"], add=dcu1,
               tiles=(_tile(s, 512), D_C, 1024))[0]
    att = sv["att"]
    tq = _att_tiles(s)[0]
    nt = s // tq
    dot_blocks, doh = _attn_do_prep(tag + "_doprep", dob, w["sel"])
    dqt, delta = _attn_bwd_dq_t(tag + "_attndq", att["qt"], att["k_aug"], att["v"], att["kt"], sv["ot"], dot_blocks,
                                sv["lse"])
    dkh, dvh, dck = _attn_bwd_dkv_t(tag + "_attndkv", att["qt"], att["k_aug"], att["v"], att["qh"], doh, dot_blocks,
                                    sv["lse"].reshape(N_HEADS, nt, 1, tq), delta.reshape(N_HEADS, nt, 1, tq))
    dc = jnp.pad(dck.reshape(N_HEADS, s).T, ((0, 0), (0, 128 - N_HEADS)))
    dlf = _lin_scan(tag + "_cumfb", jnp.ones((s, 128), f32), dc, True)
    df, dbf = _log_f_bwd(tag + "_logfb", dlf, sv["f"], w["fox_bf"])
    ga = _lin_scan(tag + "_rgscanb", _shift_up(sv["a"]), dha, True)
    dxa, dwa, dwx, dba, dbx, dlam = _rg_gates_bwd(tag + "_gatesb", sv["xa"], ga, _shift_down(sv["ha"]), w["rg_wa"],
                                                  w["rg_wx"], w["rg_ba"], w["rg_bx"], w["rg_lam"])
    dax, dconv = _conv_bwd(tag + "_convb", dxa, sv["z"], w["conv_w"])
    dz = _dz_assemble(tag + "_dz", dax, dag, dqt, dkh, dvh, df, dcu, w["sel"])
    put("w_in", _mm_plain(tag + "_dwin", "tn", sv["h1"], dz, (D_MODEL, N_IN_P, s), out_dtype=bf16,
                          tiles=(512, 768, _tile(s, 1024))))
    dh1 = _mm_plain(tag + "_dh1", "nt", dz, w["w_in"], (s, D_MODEL, N_IN_P), add=dr2, add_coef=ALPHA,
                    tiles=(_tile(s, 512), 1024, 768))
    grads = dict(dconv=dconv, dwa=dwa, dwx=dwx, dba=dba, dbx=dbx, dlam=dlam, dbf=dbf,
                 dab_re=dab_re, dab_im=dab_im, dwb_re=dwb_re, dwb_im=dwb_im, dwcr=dwcr, dwci=dwci, dd=dd, dgn=dgn)
    return dh1, grads


SMALL_NAMES = ["ln1_g", "ln1_b", "conv_w", "conv_b", "rg_w_a", "rg_b_a", "rg_w_x", "rg_b_x", "rg_lambda", "fox_b_f",
               "s5_a_re", "s5_a_im", "s5_log_dt", "s5_b_re", "s5_b_im", "s5_c_re", "s5_c_im", "s5_d", "mix_norm_g",
               "ln2_g", "ln2_b", "ln3_g", "ln3_b"]
BIG_NAMES = ["ffn1_w_gate", "ffn1_w_up", "ffn1_w_down", "w_in", "s5_w_glu", "w_out", "ffn2_w_gate", "ffn2_w_up",
             "ffn2_w_down"]


def _local_step(x, target, weight, small, on_grads, on_small):
    h = x
    saved = []
    sel = _selections()
    for l in range(DEPTH):
        get = functools.partial(weight, l)

        sm = {n: small[n][l] for n in SMALL_NAMES}
        abar_re, abar_im, coef_re, coef_im = _s5_disc(f"l{l}_s5disc", sm["s5_a_re"], sm["s5_a_im"],
                                                      sm["s5_log_dt"].reshape(S5_GROUPS, 1))
        mats, mats_vjp = jax.vjp(_s5_matrices, coef_re, coef_im, sm["s5_b_re"], sm["s5_b_im"], sm["s5_c_re"],
                                 sm["s5_c_im"])
        w = dict(
            sel=sel, conv_w=sm["conv_w"], conv_b=_row(sm["conv_b"]),
            rg_wa=_block_diag(sm["rg_w_a"]).astype(bf16), rg_wx=_block_diag(sm["rg_w_x"]).astype(bf16),
            rg_ba=_row(sm["rg_b_a"]), rg_bx=_row(sm["rg_b_x"]), rg_lam=_row(sm["rg_lambda"]),
            fox_bf=jnp.pad(_row(sm["fox_b_f"]), ((0, 0), (0, 128 - N_HEADS))),
            abar_re=_row(abar_re), abar_im=_row(abar_im),
            wb_re=mats[0].astype(bf16), wb_im=mats[1].astype(bf16), wc_re=mats[2].astype(bf16),
            wc_im=mats[3].astype(bf16), s5_d=_row(sm["s5_d"]), mix_g=_row(sm["mix_norm_g"]))
        h1, sv1 = _ffn_fwd(f"l{l}_ffn1", h, get, GROUPS["F1"], _row(sm["ln1_g"]), _row(sm["ln1_b"]),
                           gate_first=(l == 0))
        w["w_in"], w["w_glu"] = get("w_in"), get("s5_w_glu")
        o, svm = _mixer_fwd(f"l{l}_mix", h1, w)
        w_out = get("w_out")
        r2, h2 = _mm_ln(f"l{l}_wout", o, w_out, h1, _row(sm["ln2_g"]), _row(sm["ln2_b"]), 1.0)
        h3, sv2 = _ffn_fwd(f"l{l}_ffn2", h2, get, GROUPS["F2"], _row(sm["ln3_g"]), _row(sm["ln3_b"]))
        saved.append(dict(sm=sm, w=w, w_out=w_out, sv1=sv1, svm=svm, r2=r2, sv2=sv2, mats_vjp=mats_vjp))
        h = h3

    dh, loss_row = _loss_head("loss_head", h, target)
    s = x.shape[0]
    gsmall = {n: [None] * DEPTH for n in SMALL_NAMES}
    for l in reversed(range(DEPTH)):
        sd = saved[l]
        sm, w = sd["sm"], sd["w"]

        def put(name, grad, l=l):
            on_grads((l, name), grad)

        dh2, dgam, dbet = _ffn_bwd(f"l{l}_ffn2", dh, sd["sv2"], GROUPS["F2"], _row(sm["ln3_g"]), put)
        gsmall["ln3_g"][l], gsmall["ln3_b"][l] = dgam[0], dbet[0]
        dr2, dgam, dbet = _ln_bwd(f"l{l}_ln2b", sd["r2"], dh2, _row(sm["ln2_g"]))
        gsmall["ln2_g"][l], gsmall["ln2_b"][l] = dgam[0], dbet[0]
        put("w_out", _mm_plain(f"l{l}_dwout", "tn", sd["svm"]["o"], dr2, (D_MODEL, D_MODEL, s), out_dtype=bf16))
        do = _mm_plain(f"l{l}_do", "nt", dr2, sd["w_out"], (s, D_MODEL, D_MODEL))
        dh1, g = _mixer_bwd(f"l{l}_mix", do, dr2, sd["svm"], w, put)
        gsmall["conv_w"][l], gsmall["conv_b"][l] = g["dconv"][:CONV_WIDTH], g["dconv"][CONV_WIDTH]
        gsmall["rg_w_a"][l] = _block_diag_part(g["dwa"], N_HEADS)
        gsmall["rg_w_x"][l] = _block_diag_part(g["dwx"], N_HEADS)
        gsmall["rg_b_a"][l], gsmall["rg_b_x"][l], gsmall["rg_lambda"][l] = g["dba"][0], g["dbx"][0], g["dlam"][0]
        gsmall["fox_b_f"][l] = g["dbf"][0, :N_HEADS]
        dcoef_re, dcoef_im, db_re, db_im, dc_re, dc_im = sd["mats_vjp"]((g["dwb_re"], g["dwb_im"], g["dwcr"], g["dwci"]))
        da_re, da_im, dldt = _s5_disc_bwd(
            f"l{l}_s5discb", sm["s5_a_re"], sm["s5_a_im"], sm["s5_log_dt"].reshape(S5_GROUPS, 1),
            (g["dab_re"].reshape(S5_GROUPS, S5_STATE), g["dab_im"].reshape(S5_GROUPS, S5_STATE), dcoef_re, dcoef_im))
        gsmall["s5_a_re"][l], gsmall["s5_a_im"][l], gsmall["s5_log_dt"][l] = da_re, da_im, dldt[:, 0]
        gsmall["s5_b_re"][l], gsmall["s5_b_im"][l], gsmall["s5_c_re"][l], gsmall["s5_c_im"][l] = db_re, db_im, dc_re, dc_im
        gsmall["s5_d"][l], gsmall["mix_norm_g"][l] = g["dd"][0], g["dgn"][0]

        def after_ln(dgam, dbet, l=l):
            gsmall["ln1_g"][l], gsmall["ln1_b"][l] = dgam[0], dbet[0]
            if l == 0:
                on_small({n: jnp.stack(v) for n, v in gsmall.items()})

        dh, _, _ = _ffn_bwd(f"l{l}_ffn1", dh1, sd["sv1"], GROUPS["F1"], _row(sm["ln1_g"]), put, after_ln)
    return loss_row[0, 0], dh


def _position():
    return lax.axis_index("x"), lax.axis_index("y"), lax.axis_index("c")


_ANY = pl.BlockSpec(memory_space=pl.ANY)


COLUMN_SHARDED = ("ffn1_w_gate", "ffn1_w_up", "ffn2_w_gate", "ffn2_w_up")
PACK_QUANTUM = 128 * 256


def _permute_in_cols(w):
    pad = jnp.zeros(w.shape[:-1] + (128 - N_HEADS,), w.dtype)
    return jnp.concatenate([w[..., :F_OFF + N_HEADS], pad, w[..., F_OFF + N_HEADS:]], axis=-1)


def _unpermute_in_cols(w):
    return jnp.concatenate([w[..., :F_OFF + N_HEADS], w[..., CU_OFF:]], axis=-1)


def _pack(arrs):
    flat = jnp.concatenate([a.reshape(-1) for a in arrs])
    pad = -flat.shape[0] % PACK_QUANTUM
    return jnp.pad(flat, (0, pad)).reshape(-1, 128)


def _unpack(buf, shapes):
    flat = buf.reshape(-1)
    out, off = [], 0
    for shp in shapes:
        size = math.prod(shp)
        out.append(flat[off:off + size].reshape(shp))
        off += size
    return out


WEIGHT_NAMES = ["ffn1_w_gate", "ffn1_w_up", "ffn1_w_down", "ln1_g", "ln1_b", "w_in", "conv_w", "conv_b", "rg_w_a",
                "rg_b_a", "rg_w_x", "rg_b_x", "rg_lambda", "fox_b_f", "s5_a_re", "s5_a_im", "s5_log_dt", "s5_b_re",
                "s5_b_im", "s5_c_re", "s5_c_im", "s5_d", "s5_w_glu", "mix_norm_g", "w_out", "ln2_g", "ln2_b",
                "ffn2_w_gate", "ffn2_w_up", "ffn2_w_down", "ln3_g", "ln3_b"]


def _remote(src, dst, send_sems, recv_sems, k, peer):
    return pltpu.make_async_remote_copy(src_ref=src, dst_ref=dst, send_sem=send_sems.at[k], recv_sem=recv_sems.at[k],
                                        device_id=peer, device_id_type=MESH)


class _ChipGatherPart:
    def __init__(self, arrays):
        self.arrays, self.results = list(arrays), None

    def out_shape(self):
        return [_sds((N_CHIPS,) + a.shape, a.dtype) for a in self.arrays]

    def sems(self):
        n = len(self.arrays)
        return [pltpu.SemaphoreType.DMA((3 * n,)), pltpu.SemaphoreType.DMA((3 * n,)), pltpu.SemaphoreType.DMA((n,))]

    def copies(self, ins, outs, sems):
        send_sems, recv_sems, local_sems = sems
        x, y, c = _position()
        me = 2 * x + y
        local, sends, recvs = [], [], []
        for i, (src, dst) in enumerate(zip(ins, outs)):
            local.append(pltpu.make_async_copy(self.mine(src, me), dst.at[me], local_sems.at[i]))
            for r, (px, py) in enumerate([(1 - x, y), (x, 1 - y), (1 - x, 1 - y)]):
                peer = 2 * px + py
                sends.append(_remote(self.theirs(src, peer), dst.at[me], send_sems, recv_sems, 3 * i + r, (px, py, c)))
                recvs.append(_remote(self.mine(src, me), dst.at[peer], send_sems, recv_sems, 3 * i + r, (px, py, c)))
        return local, sends, recvs

    def mine(self, src, me):
        return src

    def theirs(self, src, peer):
        return src


class _ChipGatherHalvesPart(_ChipGatherPart):
    def sems(self):
        n = len(self.arrays)
        return super().sems() + [pltpu.SemaphoreType.DMA((3 * n,)), pltpu.SemaphoreType.DMA((3 * n,))]

    def _half(self, ref, which):
        rows = ref.shape[0] // 2
        return ref.at[pl.ds(which * rows, rows)]

    def copies(self, ins, outs, sems):
        send_sems, recv_sems, local_sems = sems[:3]
        x, y, c = _position()
        me = 2 * x + y
        local, sends, recvs = [], [], []
        for i, (src, dst) in enumerate(zip(ins, outs)):
            local.append(pltpu.make_async_copy(src, dst.at[me], local_sems.at[i]))
            for r, (px, py) in enumerate([(1 - x, y), (x, 1 - y), (1 - x, 1 - y)]):
                sends.append(_remote(self._half(src, c), self._half(dst.at[me], c), send_sems, recv_sems, 3 * i + r,
                                     (px, py, c)))
                recvs.append(_remote(self._half(src, c), self._half(dst.at[2 * px + py], c), send_sems, recv_sems,
                                     3 * i + r, (px, py, c)))
        return local, sends, recvs

    def forwards(self, ins, outs, sems):
        send_sems, recv_sems = sems[3:]
        x, y, c = _position()
        sends, recvs = [], []
        for i, dst in enumerate(outs):
            for r, (px, py) in enumerate([(1 - x, y), (x, 1 - y), (1 - x, 1 - y)]):
                slot = dst.at[2 * px + py]
                sends.append(_remote(self._half(slot, c), self._half(slot, c), send_sems, recv_sems, 3 * i + r,
                                     (x, y, 1 - c)))
                recvs.append(_remote(self._half(slot, c), self._half(slot, 1 - c), send_sems, recv_sems, 3 * i + r,
                                     (x, y, 1 - c)))
        return sends, recvs


class _ChipScatterPart(_ChipGatherPart):
    def out_shape(self):
        return [_sds(a.shape, a.dtype) for a in self.arrays]

    def mine(self, src, me):
        return src.at[me]

    def theirs(self, src, peer):
        return src.at[peer]


class _SiblingSwapPart:
    def __init__(self, arrays):
        self.arrays, self.results = list(arrays), None

    def out_shape(self):
        return [_sds(a.shape, a.dtype) for a in self.arrays]

    def sems(self):
        n = len(self.arrays)
        return [pltpu.SemaphoreType.DMA((n,)), pltpu.SemaphoreType.DMA((n,))]

    def copies(self, ins, outs, sems):
        x, y, c = _position()
        both = [_remote(src, dst, sems[0], sems[1], i, (x, y, 1 - c)) for i, (src, dst) in enumerate(zip(ins, outs))]
        return [], both, both


def _split_by(parts, refs, count):
    out, off = [], 0
    for p in parts:
        out.append(refs[off:off + count(p)])
        off += count(p)
    return out


def _parts_refs(parts, in_refs, out_refs, sem_refs):
    return zip(parts, _split_by(parts, in_refs, lambda p: len(p.arrays)),
               _split_by(parts, out_refs, lambda p: len(p.arrays)), _split_by(parts, sem_refs, lambda p: len(p.sems())))


def _exchange_start(parts, in_refs, out_refs, sem_refs):
    for part, ins, outs, sems in _parts_refs(parts, in_refs, out_refs, sem_refs):
        local, sends, _ = part.copies(ins, outs, sems)
        for cp in local + sends:
            cp.start()


def _exchange_finish(parts, in_refs, out_refs, sem_refs):
    split = list(_parts_refs(parts, in_refs, out_refs, sem_refs))
    copies = [part.copies(ins, outs, sems) for part, ins, outs, sems in split]
    for _, _, recvs in copies:
        for cp in recvs:
            cp.wait_recv()
    second = [part.forwards(ins, outs, sems) for part, ins, outs, sems in split if hasattr(part, "forwards")]
    for sends, _ in second:
        for cp in sends:
            cp.start()
    for sends, recvs in second:
        for cp in recvs:
            cp.wait_recv()
        for cp in sends:
            cp.wait_send()
    for local, sends, _ in copies:
        for cp in sends:
            cp.wait_send()
        for cp in local:
            cp.wait()


def _exchange_operands(parts):
    return ([a for p in parts for a in p.arrays], [s for p in parts for s in p.out_shape()],
            [s for p in parts for s in p.sems()])


def _set_results(parts, res):
    for part, outs in zip(parts, _split_by(parts, list(res), lambda p: len(p.arrays))):
        part.results = list(outs)


def _exchange_now(name, parts):
    x_in, x_out, x_sem = _exchange_operands(parts)
    n = len(x_in)

    def body(*refs):
        _exchange_start(parts, refs[:n], refs[n:2 * n], refs[2 * n:])
        _exchange_finish(parts, refs[:n], refs[n:2 * n], refs[2 * n:])

    res = pl.pallas_call(body, name=name, in_specs=[_ANY] * n, out_specs=[_ANY] * n, out_shape=x_out,
                         scratch_shapes=x_sem)(*x_in)
    _set_results(parts, res)


_RIDERS = {}


def _call(body, *, name, grid, in_specs, out_specs, out_shape, scratch_shapes=(), compiler_params=None):
    make_parts = _RIDERS.pop(name, None)
    if make_parts is None:
        return pl.pallas_call(body, name=name, grid=grid, in_specs=in_specs, out_specs=out_specs, out_shape=out_shape,
                              scratch_shapes=scratch_shapes, compiler_params=compiler_params)
    parts = make_parts()
    x_in, x_out, x_sem = _exchange_operands(parts)
    n_out, n_scr, n_x = len(out_shape), len(scratch_shapes), len(x_in)

    def run(*args):
        n_in = len(args)

        def hosted(*refs):
            ins, xi = refs[:n_in], refs[n_in:n_in + n_x]
            outs, xo = refs[n_in + n_x:n_in + n_x + n_out], refs[n_in + n_x + n_out:n_in + 2 * n_x + n_out]
            scr, xs = refs[n_in + 2 * n_x + n_out:n_in + 2 * n_x + n_out + n_scr], refs[n_in + 2 * n_x + n_out + n_scr:]
            first = functools.reduce(jnp.logical_and, [pl.program_id(d) == 0 for d in range(len(grid))])
            last = functools.reduce(jnp.logical_and, [pl.program_id(d) == grid[d] - 1 for d in range(len(grid))])

            @pl.when(first)
            def _():
                _exchange_start(parts, xi, xo, xs)

            body(*ins, *outs, *scr)

            @pl.when(last)
            def _():
                _exchange_finish(parts, xi, xo, xs)

        res = pl.pallas_call(
            hosted, name=name, grid=grid, in_specs=list(in_specs) + [_ANY] * n_x,
            out_specs=list(out_specs) + [_ANY] * n_x, out_shape=list(out_shape) + x_out,
            scratch_shapes=list(scratch_shapes) + x_sem, compiler_params=_params(*["arbitrary"] * len(grid)),
        )(*args, *x_in)
        _set_results(parts, res[n_out:])
        return list(res[:n_out])

    return run


GROUPS = {"F1": ["ffn1_w_gate", "ffn1_w_up", "ffn1_w_down"], "MX": ["w_in", "s5_w_glu", "w_out"],
          "F2": ["ffn2_w_gate", "ffn2_w_up", "ffn2_w_down"]}
FIRST_GATHER = [(0, "ffn1_w_gate")]
GATHER_HOSTS = {
    "l0_ffn1_gate": [(0, "ffn1_w_up")],
    "l0_ffn1_up": [(0, "ffn1_w_down")],
    "l0_ffn1_down": [(0, "w_in"), (0, "s5_w_glu"), (0, "w_out")],
    "l0_mix_attn": [(0, "ffn2_w_up"), (0, "ffn2_w_down"), (1, "w_in")],
    "l0_mix_s5scan": [(0, "ffn2_w_gate")],
    "l0_wout": [(1, "s5_w_glu"), (1, "w_out")],
    "l0_ffn2_up": [(1, "ffn1_w_gate")],
    "l0_ffn2_down": [(1, "ffn1_w_up")],
    "l1_ffn1_up": [(1, "ffn1_w_down")],
    "l1_mix_attn": [(1, "ffn2_w_up"), (1, "ffn2_w_down")],
    "l1_mix_s5scan": [(1, "ffn2_w_gate")],
}
SCATTER_HOSTS = {
    "l1_ffn2_dact": [(1, "ffn2_w_down")],
    "l1_mix_attndq": [(1, "ffn2_w_up")],
    "l1_mix_attndkv": [(1, "w_out"), (1, "s5_w_glu"), (1, "ffn2_w_gate")],
    "l1_ffn1_dact": [(1, "ffn1_w_down")],
    "l1_ffn1_dwgu": [(1, "w_in")],
    "l0_ffn2_dact": [(1, "ffn1_w_up")],
    "l0_ffn2_dwgu": [(0, "ffn2_w_down")],
    "l0_mix_attndq": [(0, "w_out"), (0, "s5_w_glu"), (0, "ffn2_w_up")],
    "l0_mix_attndkv": [(0, "ffn2_w_gate"), (1, "ffn1_w_gate")],
    "l0_mix_dh1": [(0, "w_in")],
    "l0_ffn1_dact": [(0, "ffn1_w_down")],
    "l0_ffn1_dh": [(0, "ffn1_w_gate")],
}
LAST_SCATTER = [(0, "ffn1_w_up")]
SMALL_HOST = "l0_ffn1_dwd"
SMALL_PACK_ORDER = [n for n in SMALL_NAMES if n != "conv_w"] + ["conv_w"]
TAIL_HOST = "l0_ffn1_dwgu"
LATE_SCATTER_HOST = "l0_ffn1_dh"
LAST_HOST = "adamw_ffn2"


def _sharded_rows(name, a):
    return jnp.swapaxes(a, 1, 2) if name in COLUMN_SHARDED else a


def _unstack_layer(st):
    _, r, c = st.shape
    return st.reshape(N_CHIPS * r, c)


def _restack_layer(g):
    r, c = g.shape
    return g.reshape(N_CHIPS, r // N_CHIPS, c)


def _adamw_layer(name, layer, w, ga, gb, m, v, bufs):
    _, r, c = w.shape
    tr = _row_tile(r)

    def body(w_ref, ga_ref, gb_ref, m_ref, v_ref, *rest):
        g_out, d_out, m_out, v_out = rest[-4:]
        g = ga_ref[...] + gb_ref[...]
        d, mm, vv = _adamw_rows(w_ref[...], g, m_ref[...], v_ref[...])
        g_out[...] = g
        d_out[...] = d
        m_out[...] = mm
        v_out[...] = vv

    full = pl.BlockSpec((None, tr, c), lambda i: (layer, i, 0))
    flat = pl.BlockSpec((tr, c), lambda i: (i, 0))
    extra = {} if bufs is None else dict(input_output_aliases={5 + k: k for k in range(4)})
    return pl.pallas_call(
        body, name=name, grid=(r // tr,),
        in_specs=[full, flat, flat, full, full] + ([] if bufs is None else [_ANY] * 4),
        out_specs=[full] * 4, out_shape=[_sds(w.shape)] * 4, compiler_params=_params("parallel"), **extra,
    )(w, ga, gb, m, v, *([] if bufs is None else bufs))


def _adamw_both_layers(name, ws, ms, vs, gas, gbs):
    nw = len(ws)
    _, r, c = ws[0].shape
    tr = _row_tile(r, 64)

    def body(*refs):
        ins, outs = refs[:7 * nw], refs[7 * nw:]
        for k in range(nw):
            w_ref, m_ref, v_ref = ins[3 * k:3 * k + 3]
            g_refs = ins[3 * nw + 4 * k:3 * nw + 4 * k + 4]
            g_out, d_out, m_out, v_out = outs[4 * k:4 * k + 4]
            for layer in range(DEPTH):
                g = g_refs[layer][...] + g_refs[DEPTH + layer][...]
                d, mm, vv = _adamw_rows(w_ref[layer], g, m_ref[layer], v_ref[layer])
                g_out[layer] = g
                d_out[layer] = d
                m_out[layer] = mm
                v_out[layer] = vv

    both = pl.BlockSpec((DEPTH, tr, c), lambda i: (0, i, 0))
    flat = pl.BlockSpec((tr, c), lambda i: (i, 0))
    wmv = [t for k in range(nw) for t in (ws[k], ms[k], vs[k])]
    gs = [t for k in range(nw) for t in (*gas[k], *gbs[k])]
    res = _call(
        body, name=name, grid=(r // tr,), in_specs=[both] * (3 * nw) + [flat] * (4 * nw),
        out_specs=[both] * (4 * nw), out_shape=[_sds(ws[0].shape)] * (4 * nw), compiler_params=_params("parallel"),
    )(*wmv, *gs)
    return [res[4 * k:4 * k + 4] for k in range(nw)]


def _train_step(x, loss_target, w, m, v):
    ix, iy, _ = _position()
    chip = 2 * ix + iy
    shard = {n: (_permute_in_cols(w[n]) if n == "w_in" else _sharded_rows(n, w[n])).astype(bf16) for n in BIG_NAMES}

    gathered = {}

    def gather_parts(keys, extra=()):
        part = _ChipGatherHalvesPart([shard[n][layer] for layer, n in keys] + list(extra))
        gathered.update({key: (part, i) for i, key in enumerate(keys)})
        return [part]

    (first,) = gather_parts(FIRST_GATHER, extra=[w["conv_w"]])
    _exchange_now("gather_first", [first])
    for host, keys in GATHER_HOSTS.items():
        _RIDERS[host] = functools.partial(gather_parts, keys)

    def weight(layer, name):
        part, i = gathered[(layer, name)]
        return _unstack_layer(part.results[i])

    small = {n: w[n] for n in SMALL_NAMES}
    small["conv_w"] = first.results[-1].transpose(1, 2, 0, 3).reshape(DEPTH, CONV_WIDTH, D_A)

    grads_full, scattered = {}, {}

    def scatter_parts(keys):
        part = _ChipScatterPart([_restack_layer(grads_full[key]) for key in keys])
        scattered.update({key: (part, i) for i, key in enumerate(keys)})
        return [part]

    for host, keys in SCATTER_HOSTS.items():
        _RIDERS[host] = functools.partial(scatter_parts, keys)

    partial = {}

    def reduce_chips(keys):
        for layer, n in keys:
            part, i = scattered[(layer, n)]
            p = _sum_stack(f"sum_l{layer}_{n}", part.results[i])
            partial[(layer, n)] = _unpermute_in_cols(p) if n == "w_in" else p

    early = [key for host, keys in SCATTER_HOSTS.items() if host != LATE_SCATTER_HOST for key in keys]
    late = SCATTER_HOSTS[LATE_SCATTER_HOST]
    tail = {}

    def small_parts():
        tail["small"] = _ChipGatherPart([_pack([tail["gsmall"][n] for n in SMALL_PACK_ORDER])])
        return [tail["small"]]

    def tail_parts():
        reduce_chips(early)
        tail["small_sum"] = _sum_stack("sum_small", tail["small"].results[0])
        tail["swap"] = _SiblingSwapPart([partial[k] for k in early] + [tail["small_sum"]])
        return [tail["swap"]]

    _RIDERS[SMALL_HOST] = small_parts
    _RIDERS[TAIL_HOST] = tail_parts
    loss_local, gx = _local_step(x[0], loss_target[0], weight, small, grads_full.__setitem__,
                                 functools.partial(tail.__setitem__, "gsmall"))
    other = dict(zip(early, tail["swap"].results[:-1]))
    small_mine, small_other = tail["small_sum"], tail["swap"].results[-1]
    grads, deltas, new_m, new_v = {}, {}, {}, {}
    reduce_chips(late)
    last_parts = scatter_parts(LAST_SCATTER) + [_SiblingSwapPart([partial[k] for k in late])]
    _RIDERS[LAST_HOST] = lambda: last_parts
    ffn2 = GROUPS["F2"]
    res = _adamw_both_layers(
        LAST_HOST, *[[_sharded_rows(n, t[n]) for n in ffn2] for t in (w, m, v)],
        [[partial[(layer, n)] for layer in range(DEPTH)] for n in ffn2],
        [[other[(layer, n)] for layer in range(DEPTH)] for n in ffn2])
    for n, bufs in zip(ffn2, res):
        grads[n], deltas[n], new_m[n], new_v[n] = (_sharded_rows(n, t) for t in bufs)
    other.update(zip(late, last_parts[1].results))
    reduce_chips(LAST_SCATTER)
    swap_last = _SiblingSwapPart([partial[k] for k in LAST_SCATTER])
    _exchange_now("swap_last", [swap_last])
    other.update(zip(LAST_SCATTER, swap_last.results))

    for n in BIG_NAMES:
        if n in ffn2:
            continue
        bufs = None
        wr, mr, vr = (_sharded_rows(n, t) for t in (w[n], m[n], v[n]))
        for layer in range(DEPTH):
            bufs = _adamw_layer(f"adamw_l{layer}_{n}", layer, wr, partial[(layer, n)], other[(layer, n)], mr, vr, bufs)
        grads[n], deltas[n], new_m[n], new_v[n] = (_sharded_rows(n, t) for t in bufs)
    packed = SMALL_PACK_ORDER[:-1]
    shapes = [w[n].shape for n in packed]
    res = _adamw("adamw_small", _pack([w[n] for n in packed]), small_mine, small_other,
                 _pack([m[n] for n in packed]), _pack([v[n] for n in packed]))
    for dst, buf in zip((grads, deltas, new_m, new_v), res):
        dst.update(zip(packed, _unpack(buf, shapes)))
    cw = D_A // N_CHIPS
    conv_shape = (DEPTH, CONV_WIDTH, D_A)
    offset = sum(math.prod(s_) for s_ in shapes)

    def conv_grad(buf):
        full = buf.reshape(-1)[offset:offset + math.prod(conv_shape)].reshape(conv_shape)
        return lax.dynamic_slice_in_dim(full, chip * cw, cw, axis=2).reshape(DEPTH * CONV_WIDTH, cw)

    rows = lambda t: t.reshape(DEPTH * CONV_WIDTH, cw)
    res = _adamw("adamw_conv_w", rows(w["conv_w"]), conv_grad(small_mine), conv_grad(small_other),
                 rows(m["conv_w"]), rows(v["conv_w"]))
    for dst, buf in zip((grads, deltas, new_m, new_v), res):
        dst["conv_w"] = buf.reshape(w["conv_w"].shape)

    loss = lax.psum(loss_local, ("x", "y", "c"))
    return (loss, gx[None], *[grads[n] for n in WEIGHT_NAMES], *[deltas[n] for n in WEIGHT_NAMES],
            *[new_m[n] for n in WEIGHT_NAMES], *[new_v[n] for n in WEIGHT_NAMES])


def kernel(x, ffn1_w_gate, ffn1_w_up, ffn1_w_down, ln1_g, ln1_b, w_in, conv_w, conv_b, rg_w_a, rg_b_a, rg_w_x, rg_b_x, rg_lambda, fox_b_f, s5_a_re, s5_a_im, s5_log_dt, s5_b_re, s5_b_im, s5_c_re, s5_c_im, s5_d, s5_w_glu, mix_norm_g, w_out, ln2_g, ln2_b, ffn2_w_gate, ffn2_w_up, ffn2_w_down, ln3_g, ln3_b, loss_target, m_ffn1_w_gate, m_ffn1_w_up, m_ffn1_w_down, m_ln1_g, m_ln1_b, m_w_in, m_conv_w, m_conv_b, m_rg_w_a, m_rg_b_a, m_rg_w_x, m_rg_b_x, m_rg_lambda, m_fox_b_f, m_s5_a_re, m_s5_a_im, m_s5_log_dt, m_s5_b_re, m_s5_b_im, m_s5_c_re, m_s5_c_im, m_s5_d, m_s5_w_glu, m_mix_norm_g, m_w_out, m_ln2_g, m_ln2_b, m_ffn2_w_gate, m_ffn2_w_up, m_ffn2_w_down, m_ln3_g, m_ln3_b, v_ffn1_w_gate, v_ffn1_w_up, v_ffn1_w_down, v_ln1_g, v_ln1_b, v_w_in, v_conv_w, v_conv_b, v_rg_w_a, v_rg_b_a, v_rg_w_x, v_rg_b_x, v_rg_lambda, v_fox_b_f, v_s5_a_re, v_s5_a_im, v_s5_log_dt, v_s5_b_re, v_s5_b_im, v_s5_c_re, v_s5_c_im, v_s5_d, v_s5_w_glu, v_mix_norm_g, v_w_out, v_ln2_g, v_ln2_b, v_ffn2_w_gate, v_ffn2_w_up, v_ffn2_w_down, v_ln3_g, v_ln3_b):
    args = dict(locals())
    w = {n: args[n] for n in WEIGHT_NAMES}
    m = {n: args["m_" + n] for n in WEIGHT_NAMES}
    v = {n: args["v_" + n] for n in WEIGHT_NAMES}
    return _train_step(x, loss_target, w, m, v)
```

```python
import functools
import math

import jax
import jax.numpy as jnp
from jax import lax
from jax.experimental import pallas as pl
from jax.experimental.pallas import tpu as pltpu

f32 = jnp.float32
bf16 = jnp.bfloat16

D_MODEL = 1024
D_FF = 2816
D_A = 384
D_B = 384
D_C = 256
N_HEADS = 6
HEAD_DIM = 64
S5_GROUPS = 16
S5_STATE = 64
S5_LANES = S5_GROUPS * S5_STATE
F_OFF = 5 * D_A
CU_OFF = F_OFF + 128
N_IN_P = CU_OFF + D_C
CONV_WIDTH = 4
DEPTH = 2
ALPHA = (2 * DEPTH) ** 0.25
LN_EPS = 1e-5
RMS_EPS = 1e-6
RG_C = 8.0
ATT_SCALE = HEAD_DIM ** -0.5
ADAM_LR, ADAM_B1, ADAM_B2, ADAM_EPS, ADAM_WD, ADAM_STEP = 0.001, 0.9, 0.999, 1e-08, 0.01, 10

ROW_TILE = 512
N_CHIPS = 4
MESH = pl.DeviceIdType.MESH

_DN = {
    "nn": (((1,), (0,)), ((), ())),
    "nt": (((1,), (1,)), ((), ())),
    "tn": (((0,), (0,)), ((), ())),
}


def _sds(shape, dtype=f32):
    return jax.ShapeDtypeStruct(shape, dtype)


def _tile(n, target):
    best = None
    for t in range(128, min(n, target) + 1, 128):
        if n % t == 0:
            best = t
    return best or n


def _row_tile(rows, target=256):
    best = None
    for t in range(16, min(rows, target) + 1, 16):
        if rows % t == 0:
            best = t
    return best or rows


def _params(*sem):
    return pltpu.CompilerParams(dimension_semantics=sem)


class _Slabs:
    def __init__(self, x):
        self.x = x


class _KPart:
    def __init__(self, x, j):
        self.x, self.j = x, j


FF_SLAB = D_FF // N_CHIPS
FFN_ROWS = 1024
UP_ROWS = 2048

def _mm(name, mode, dims, tiles, a_list, b_list, pairs, n_acc, epilogue, outs, extras=(), vecs=(), split_cols=False):
    m, n, k = dims
    tm, tn, tk = tiles
    nk = k // tk
    na, nb, ne, nv, no = len(a_list), len(b_list), len(extras), len(vecs), len(outs)

    def body(*refs):
        a_refs = refs[:na]
        b_refs = refs[na:na + nb]
        e_refs = refs[na + nb:na + nb + ne]
        v_refs = refs[na + nb + ne:na + nb + ne + nv]
        o_refs = refs[na + nb + ne + nv:na + nb + ne + nv + no]
        acc_refs = refs[na + nb + ne + nv + no:]
        a_vals = [r[...].astype(bf16) for r in a_refs]
        b_vals = [r[...].astype(bf16) for r in b_refs]
        products = [(ci, lax.dot_general(a_vals[ai], b_vals[bi], _DN[mode], preferred_element_type=f32))
                    for ai, bi, ci in pairs]

        def finish(accs):
            res = epilogue(accs, [e[...] for e in e_refs], [v[...] for v in v_refs])
            for o, r in zip(o_refs, res):
                o[...] = r.astype(o.dtype)

        if nk == 1:
            accs = [None] * n_acc
            for ci, prod in products:
                accs[ci] = prod if accs[ci] is None else accs[ci] + prod
            finish(accs)
            return
        kk = pl.program_id(2)

        @pl.when(kk == 0)
        def _():
            for acc in acc_refs:
                acc[...] = jnp.zeros_like(acc)

        for ci, prod in products:
            acc_refs[ci][...] += prod

        @pl.when(kk == nk - 1)
        def _():
            finish([acc[...] for acc in acc_refs])

    def a_spec(a):
        if isinstance(a, _KPart):
            return pl.BlockSpec((None, tm, tk), lambda i, j, kk, part=a.j: (part, i, 0))
        if isinstance(a, _Slabs):
            if mode == "tn":
                return pl.BlockSpec((None, tk, tm), lambda i, j, kk: (i, kk, 0))
            return pl.BlockSpec((None, tm, tk), lambda i, j, kk: (kk, i, 0))
        if mode == "tn":
            return pl.BlockSpec((tk, tm), lambda i, j, kk: (kk, i))
        return pl.BlockSpec((tm, tk), lambda i, j, kk: (i, kk))

    def b_spec(b):
        if isinstance(b, _KPart):
            return pl.BlockSpec((tk, tn), lambda i, j, kk, part=b.j: (part, j))
        if isinstance(b, _Slabs):
            if mode == "nt":
                return pl.BlockSpec((None, tn, tk), lambda i, j, kk: (kk, j, 0))
            return pl.BlockSpec((None, tk, tn), lambda i, j, kk: (j, kk, 0))
        if mode == "nt":
            return pl.BlockSpec((tn, tk), lambda i, j, kk: (j, kk))
        return pl.BlockSpec((tk, tn), lambda i, j, kk: (kk, j))

    o_spec = pl.BlockSpec((tm, tn), lambda i, j, kk: (i, j))
    o_slab_spec = pl.BlockSpec((None, tm, tn), lambda i, j, kk: (j, i, 0))
    v_spec = pl.BlockSpec((1, tn), lambda i, j, kk: (0, j))
    if split_cols:
        out_specs = [o_slab_spec] * no
        out_shape = [_sds((n // tn, m, tn), dt) for dt in outs]
    else:
        out_specs = [o_spec] * no
        out_shape = [_sds((m, n), dt) for dt in outs]
    raw = lambda t: t.x if isinstance(t, (_Slabs, _KPart)) else t
    res = _call(
        body,
        name=name,
        grid=(m // tm, n // tn, nk),
        in_specs=([a_spec(a) for a in a_list] + [b_spec(b) for b in b_list]
                  + [o_slab_spec if isinstance(e, _Slabs) else o_spec for e in extras] + [v_spec] * nv),
        out_specs=out_specs,
        out_shape=out_shape,
        scratch_shapes=[pltpu.VMEM((tm, tn), f32)] * (n_acc if nk > 1 else 0),
        compiler_params=_params("parallel", "parallel", "arbitrary"),
    )(*map(raw, a_list), *map(raw, b_list), *map(raw, extras), *vecs)
    return res


def _sigmoid(x):
    return 0.5 * (jnp.tanh(0.5 * x) + 1.0)


def _layer_norm_rows(r, gamma, beta):
    mu = jnp.mean(r, axis=-1, keepdims=True)
    xc = r - mu
    var = jnp.mean(xc * xc, axis=-1, keepdims=True)
    return xc * lax.rsqrt(var + LN_EPS) * gamma + beta


def _mm_plain(name, mode, a, b, dims, scale=1.0, out_dtype=f32, add=None, add_coef=1.0, tiles=None):
    m, n, k = dims
    tiles = tiles or (_tile(m, 512), _tile(n, 1024), _tile(k, 1024))

    def epilogue(accs, extras, vecs):
        r = accs[0] if scale == 1.0 else accs[0] * scale
        if extras:
            r = r + add_coef * extras[0]
        return [r]

    return _mm(name, mode, dims, tiles, [a], [b], [(0, 0, 0)], 1, epilogue, [out_dtype],
               extras=[] if add is None else [add])[0]


def _ffn_up(name, h, wg, wu):
    s = h.shape[0]

    def epilogue(accs, extras, vecs):
        g, u = accs
        return [g, u, g * _sigmoid(g) * u]

    return _mm(name, "nt", (s, D_FF, D_MODEL), (_tile(s, UP_ROWS), FF_SLAB, D_MODEL), [h], [wg, wu],
               [(0, 0, 0), (0, 1, 1)], 2, epilogue, [bf16, bf16, bf16], split_cols=True)


def _ffn_gate(name, h, wg):
    s = h.shape[0]
    return _mm(name, "nt", (s, D_FF, D_MODEL), (_tile(s, UP_ROWS), FF_SLAB, D_MODEL), [h], [wg], [(0, 0, 0)], 1,
               lambda accs, extras, vecs: [accs[0]], [bf16], split_cols=True)[0]


def _ffn_up_given_gate(name, h, wu, g):
    s = h.shape[0]

    def epilogue(accs, extras, vecs):
        gg = extras[0].astype(f32)
        return [accs[0], gg * _sigmoid(gg) * accs[0]]

    return _mm(name, "nt", (s, D_FF, D_MODEL), (_tile(s, UP_ROWS), FF_SLAB, D_MODEL), [h], [wu], [(0, 0, 0)], 1,
               epilogue, [bf16, bf16], extras=[_Slabs(g)], split_cols=True)


def _mm_ln(name, a, w, resid, gamma, beta, scale, k_slabs=False):
    def epilogue(accs, extras, vecs):
        r = ALPHA * extras[0] + scale * accs[0]
        return [r, _layer_norm_rows(r, vecs[0], vecs[1])]

    if k_slabs:
        n_slabs, s, slab = a.shape
        return _mm(name, "nn", (s, D_MODEL, slab), (_tile(s, 512), D_MODEL, slab),
                   [_KPart(a, j) for j in range(n_slabs)], [_KPart(w, j) for j in range(n_slabs)],
                   [(j, j, 0) for j in range(n_slabs)], 1, epilogue, [f32, f32], extras=[resid], vecs=[gamma, beta])
    s, k = a.shape
    return _mm(name, "nn", (s, D_MODEL, k), (_tile(s, FFN_ROWS), D_MODEL, _tile(k, 1024)),
               [a], [w], [(0, 0, 0)], 1, epilogue, [f32, f32], extras=[resid], vecs=[gamma, beta])


def _ffn_dact(name, dr, wd, g, u):
    s = dr.shape[0]

    def epilogue(accs, extras, vecs):
        da = 0.5 * accs[0]
        gg, uu = extras[0].astype(f32), extras[1].astype(f32)
        sg = _sigmoid(gg)
        return [da * uu * (sg * (1.0 + gg * (1.0 - sg))), da * (gg * sg)]

    return _mm(name, "nt", (s, D_FF, D_MODEL), (_tile(s, UP_ROWS), FF_SLAB, D_MODEL), [dr], [wd],
               [(0, 0, 0)], 1, epilogue, [bf16, bf16], extras=[_Slabs(g), _Slabs(u)], split_cols=True)


def _mm2(name, mode, dims, a0, b0, a1, b1, add=None, add_coef=1.0, separate=False, tiles=None, out_dtype=f32,
         split_cols=False):
    m, n, k = dims
    tiles = tiles or (_tile(m, 512), _tile(n, 1024), _tile(k, 1024))

    def epilogue(accs, extras, vecs):
        if separate:
            return list(accs)
        r = accs[0]
        if extras:
            r = r + add_coef * extras[0]
        return [r]

    a_list = [a0] if a1 is None else [a0, a1]
    b_list = [b0] if b1 is None else [b0, b1]
    pairs = [(0, 0, 0), (len(a_list) - 1, len(b_list) - 1, 1 if separate else 0)]
    return _mm(name, mode, dims, tiles, a_list, b_list, pairs, 2 if separate else 1, epilogue,
               [out_dtype, out_dtype] if separate else [out_dtype], extras=[] if add is None else [add],
               split_cols=split_cols)


def _row_call(name, body, s, ins, params, outs, accs):
    tm = ROW_TILE
    ins = [a if isinstance(a, tuple) else (a, a.shape[1], 0) for a in ins]
    in_specs = [pl.BlockSpec((tm, width), lambda i, cb=cb: (i, cb)) for _, width, cb in ins]
    ins = [a for a, _, _ in ins]
    in_specs += [pl.BlockSpec(p.shape, lambda i, nd=p.ndim: (0,) * nd) for p in params]
    out_specs = [pl.BlockSpec((tm, o.shape[1]), lambda i: (i, 0)) for o in outs]
    out_specs += [pl.BlockSpec(a.shape, lambda i, nd=len(a.shape): (0,) * nd) for a in accs]
    return pl.pallas_call(
        body,
        name=name,
        grid=(s // tm,),
        in_specs=in_specs,
        out_specs=out_specs,
        out_shape=list(outs) + list(accs),
        compiler_params=_params("arbitrary"),
    )(*ins, *params)


def _zero_at_first(refs):
    @pl.when(pl.program_id(0) == 0)
    def _():
        for r in refs:
            r[...] = jnp.zeros_like(r)


def _ln_bwd(name, r, dh, gamma):
    s = r.shape[0]

    def body(r_ref, dh_ref, g_ref, dr_ref, dg_ref, db_ref):
        _zero_at_first([dg_ref, db_ref])
        rr = r_ref[...]
        dy = dh_ref[...]
        mu = jnp.mean(rr, axis=-1, keepdims=True)
        xc = rr - mu
        rstd = lax.rsqrt(jnp.mean(xc * xc, axis=-1, keepdims=True) + LN_EPS)
        xhat = xc * rstd
        dxh = dy * g_ref[...]
        dr_ref[...] = rstd * (dxh - jnp.mean(dxh, axis=-1, keepdims=True)
                              - xhat * jnp.mean(dxh * xhat, axis=-1, keepdims=True))
        dg_ref[...] += jnp.sum(dy * xhat, axis=0, keepdims=True)
        db_ref[...] += jnp.sum(dy, axis=0, keepdims=True)

    return _row_call(name, body, s, [r, dh], [gamma], [_sds((s, D_MODEL))], [_sds((1, D_MODEL)), _sds((1, D_MODEL))])


def _loss_head(name, y, target):
    s = y.shape[0]

    def body(y_ref, t_ref, dy_ref, l_ref):
        _zero_at_first([l_ref])
        e = y_ref[...] - t_ref[...]
        dy_ref[...] = e / D_MODEL
        l_ref[...] += 0.5 * jnp.sum(jnp.mean(e * e, axis=-1, keepdims=True), axis=0, keepdims=True)

    return _row_call(name, body, s, [y, target], [], [_sds((s, D_MODEL))], [_sds((1, 128))])


def _expm1(x):
    series = x * (1.0 + x / 2.0 * (1.0 + x / 3.0 * (1.0 + x / 4.0 * (1.0 + x / 5.0 * (1.0 + x / 6.0 * (1.0 + x / 7.0))))))
    return jnp.where(jnp.abs(x) < 0.25, series, jnp.exp(x) - 1.0)


def _gates_fn(xa, wa, wx, ba, bx, lam, tap_a, tap_x):
    xb = xa.astype(bf16)
    r = jax.nn.sigmoid(jnp.dot(xb, wa, preferred_element_type=f32) + ba + tap_a)
    i = jax.nn.sigmoid(jnp.dot(xb, wx, preferred_element_type=f32) + bx + tap_x)
    log_a = -RG_C * r * jax.nn.softplus(-lam)
    a = jnp.exp(log_a)
    gated = jnp.sqrt(-_expm1(2.0 * log_a)) * (i * xa)
    return a, gated


def _rg_gates(name, xa, wa, wx, ba, bx, lam):
    s = xa.shape[0]

    def body(xa_ref, wa_ref, wx_ref, ba_ref, bx_ref, lam_ref, a_ref, g_ref):
        a, g = _gates_fn(xa_ref[...], wa_ref[...], wx_ref[...], ba_ref[...], bx_ref[...], lam_ref[...], 0.0, 0.0)
        a_ref[...] = a
        g_ref[...] = g

    return _row_call(name, body, s, [xa], [wa, wx, ba, bx, lam], [_sds((s, D_A)), _sds((s, D_A))], [])


def _rg_gates_bwd(name, xa, ga, h_prev, wa, wx, ba, bx, lam):
    s = xa.shape[0]

    def body(xa_ref, ga_ref, hp_ref, wa_ref, wx_ref, ba_ref, bx_ref, lam_ref,
             dxa_ref, dwa_ref, dwx_ref, dba_ref, dbx_ref, dlam_ref):
        _zero_at_first([dwa_ref, dwx_ref, dba_ref, dbx_ref, dlam_ref])
        xa_v = xa_ref[...]
        zero = jnp.zeros((xa_v.shape[0], D_A), f32)
        fn = lambda x, ba_, bx_, lam_, ta, tx: _gates_fn(x, wa_ref[...], wx_ref[...], ba_, bx_, lam_, ta, tx)
        _, vjp = jax.vjp(fn, xa_v, ba_ref[...], bx_ref[...], lam_ref[...], zero, zero)
        gav = ga_ref[...]
        dxa, dba, dbx, dlam, dta, dtx = vjp((gav * hp_ref[...], gav))
        dxa_ref[...] = dxa
        xb = xa_v.astype(bf16)
        dwa_ref[...] += lax.dot_general(xb, dta.astype(bf16), _DN["tn"], preferred_element_type=f32)
        dwx_ref[...] += lax.dot_general(xb, dtx.astype(bf16), _DN["tn"], preferred_element_type=f32)
        dba_ref[...] += dba
        dbx_ref[...] += dbx
        dlam_ref[...] += dlam

    return _row_call(name, body, s, [xa, ga, h_prev], [wa, wx, ba, bx, lam], [_sds((s, D_A))],
                     [_sds((D_A, D_A)), _sds((D_A, D_A)), _sds((1, D_A)), _sds((1, D_A)), _sds((1, D_A))])


def _rms(v, g):
    return v * lax.rsqrt(jnp.mean(v * v, axis=-1, keepdims=True) + RMS_EPS) * g


def _mix_out_fn(ag, ha, ob, hre, him, cu, d, gn, tap_y, tap_gl, wcr, wci, wglu):
    out_a = jax.nn.gelu(ag) * ha
    y = (jnp.dot(hre.astype(bf16), wcr, preferred_element_type=f32)
         + jnp.dot(him.astype(bf16), wci, preferred_element_type=f32) + d * cu + tap_y)
    y2 = jax.nn.gelu(y)
    gl = jnp.dot(y2.astype(bf16), wglu, preferred_element_type=f32) + tap_gl
    out_c = y2 * jax.nn.sigmoid(gl)
    o = jnp.concatenate([_rms(out_a, gn[:, :D_A]), _rms(ob, gn[:, D_A:D_A + D_B]), _rms(out_c, gn[:, D_A + D_B:])],
                        axis=-1)
    return o, y2


def _mix_out(name, ag, ha, ob, hre, him, cu, d, gn, wcr, wci, wglu):
    s = ha.shape[0]

    def body(ag_ref, ha_ref, ob_ref, hre_ref, him_ref, cu_ref, d_ref, gn_ref, wcr_ref, wci_ref, wglu_ref, o_ref):
        o, _ = _mix_out_fn(ag_ref[...], ha_ref[...], ob_ref[...], hre_ref[...], him_ref[...], cu_ref[...], d_ref[...],
                           gn_ref[...], 0.0, 0.0, wcr_ref[...], wci_ref[...], wglu_ref[...])
        o_ref[...] = o.astype(o_ref.dtype)

    return _row_call(name, body, s, [ag, ha, ob, hre, him, cu], [d, gn, wcr, wci, wglu], [_sds((s, D_MODEL), bf16)], [])[0]


def _mix_out_bwd(name, do, ag, ha, ob, hre, him, cu, d, gn, wcr, wci, wglu):
    s = ha.shape[0]

    def body(do_ref, ag_ref, ha_ref, ob_ref, hre_ref, him_ref, cu_ref, d_ref, gn_ref, wcr_ref, wci_ref, wglu_ref,
             dag_ref, dha_ref, dob_ref, dhre_ref, dhim_ref, dcu_ref, dwcr_ref, dwci_ref, dwglu_ref, dd_ref, dgn_ref):
        _zero_at_first([dwcr_ref, dwci_ref, dwglu_ref, dd_ref, dgn_ref])
        tm = ag_ref.shape[0]
        zero = jnp.zeros((tm, D_C), f32)
        hre_v, him_v = hre_ref[...], him_ref[...]
        fn = lambda *a: _mix_out_fn(*a, wcr_ref[...], wci_ref[...], wglu_ref[...])
        _, vjp, y2 = jax.vjp(fn, ag_ref[...], ha_ref[...], ob_ref[...], hre_v, him_v, cu_ref[...], d_ref[...],
                             gn_ref[...], zero, zero, has_aux=True)
        dag, dha, dob, dhre, dhim, dcu, dd, dgn, dy, dgl = vjp(do_ref[...])
        dag_ref[...] = dag
        dha_ref[...] = dha
        dob_ref[...] = dob
        dhre_ref[...] = dhre
        dhim_ref[...] = dhim
        dcu_ref[...] = dcu
        dyb = dy.astype(bf16)
        dwcr_ref[...] += lax.dot_general(hre_v.astype(bf16), dyb, _DN["tn"], preferred_element_type=f32)
        dwci_ref[...] += lax.dot_general(him_v.astype(bf16), dyb, _DN["tn"], preferred_element_type=f32)
        dwglu_ref[...] += lax.dot_general(y2.astype(bf16), dgl.astype(bf16), _DN["tn"], preferred_element_type=f32)
        dd_ref[...] += dd
        dgn_ref[...] += dgn

    outs = [_sds((s, D_A)), _sds((s, D_A)), _sds((s, D_B)), _sds((s, S5_LANES)), _sds((s, S5_LANES)), _sds((s, D_C))]
    accs = [_sds((S5_LANES, D_C)), _sds((S5_LANES, D_C)), _sds((D_C, D_C)), _sds((1, D_C)), _sds((1, D_MODEL))]
    return _row_call(name, body, s, [do, ag, ha, ob, hre, him, cu], [d, gn, wcr, wci, wglu], outs, accs)


def _log_f(name, f, bf):
    s = f[0].shape[0]

    def body(f_ref, b_ref, o_ref):
        o_ref[...] = jax.nn.log_sigmoid(f_ref[...] + b_ref[...])

    return _row_call(name, body, s, [f], [bf], [_sds((s, 128))], [])[0]


def _log_f_bwd(name, dlf, f, bf):
    s = dlf.shape[0]

    def body(dl_ref, f_ref, b_ref, df_ref, db_ref):
        _zero_at_first([db_ref])
        df = dl_ref[...] * jax.nn.sigmoid(-(f_ref[...] + b_ref[...]))
        df_ref[...] = df
        db_ref[...] += jnp.sum(df, axis=0, keepdims=True)

    return _row_call(name, body, s, [dlf, f], [bf], [_sds((s, 128))], [_sds((1, 128))])


def _s5_decay_grad(name, h_re, h_im, g_re, g_im):
    s = g_re.shape[0]
    tm = ROW_TILE

    def body(hr_ref, hi_ref, hhr_ref, hhi_ref, gr_ref, gi_ref, dr_ref, di_ref):
        i = pl.program_id(0)
        _zero_at_first([dr_ref, di_ref])

        def previous(h_ref, halo_ref):
            halo = jnp.where(i == 0, 0.0, halo_ref[...])
            return pltpu.roll(jnp.concatenate([halo, h_ref[...]], axis=0), 1, 0)[8:, :]

        hr, hi, gr, gi = previous(hr_ref, hhr_ref), previous(hi_ref, hhi_ref), gr_ref[...], gi_ref[...]
        dr_ref[...] += jnp.sum(hr * gr + hi * gi, axis=0, keepdims=True)
        di_ref[...] += jnp.sum(hr * gi - hi * gr, axis=0, keepdims=True)

    rows = pl.BlockSpec((tm, S5_LANES), lambda i: (i, 0))
    halo = pl.BlockSpec((8, S5_LANES), lambda i: (jnp.maximum(i * (tm // 8) - 1, 0), 0))
    acc = pl.BlockSpec((1, S5_LANES), lambda i: (0, 0))
    return pl.pallas_call(
        body,
        name=name,
        grid=(s // tm,),
        in_specs=[rows, rows, halo, halo, rows, rows],
        out_specs=[acc, acc],
        out_shape=[_sds((1, S5_LANES)), _sds((1, S5_LANES))],
        compiler_params=_params("arbitrary"),
    )(h_re, h_im, h_re, h_im, g_re, g_im)


def _conv_fwd(name, ax, w, b):
    s = ax.shape[0]
    tm = ROW_TILE

    def body(x_ref, halo_ref, w_ref, b_ref, o_ref):
        i = pl.program_id(0)
        x = x_ref[...]
        halo = jnp.where(i == 0, 0.0, halo_ref[...])
        ext = jnp.concatenate([halo, x], axis=0)
        acc = b_ref[...] + w_ref[3:4, :] * x
        for k in range(CONV_WIDTH - 1):
            acc = acc + w_ref[k:k + 1, :] * pltpu.roll(ext, CONV_WIDTH - 1 - k, 0)[8:, :]
        o_ref[...] = acc

    return pl.pallas_call(
        body,
        name=name,
        grid=(s // tm,),
        in_specs=[pl.BlockSpec((tm, D_A), lambda i: (i, 0)),
                  pl.BlockSpec((8, D_A), lambda i: (jnp.maximum(i * (tm // 8) - 1, 0), 0)),
                  pl.BlockSpec((CONV_WIDTH, D_A), lambda i: (0, 0)),
                  pl.BlockSpec((1, D_A), lambda i: (0, 0))],
        out_specs=pl.BlockSpec((tm, D_A), lambda i: (i, 0)),
        out_shape=_sds((s, D_A)),
        compiler_params=_params("arbitrary"),
    )(ax, ax, w, b)


def _conv_bwd(name, dxa, ax, w):
    s = ax.shape[0]
    tm = ROW_TILE
    nblk = s // tm

    def body(dx_ref, dnext_ref, x_ref, halo_ref, w_ref, dax_ref, dw_ref):
        i = pl.program_id(0)
        _zero_at_first([dw_ref])
        dx = dx_ref[...]
        dnext = jnp.where(i == nblk - 1, 0.0, dnext_ref[...])
        dext = jnp.concatenate([dx, dnext], axis=0)
        x = x_ref[...]
        halo = jnp.where(i == 0, 0.0, halo_ref[...])
        ext = jnp.concatenate([halo, x], axis=0)
        acc = w_ref[3:4, :] * dx
        dw_ref[3:4, :] += jnp.sum(dx * x, axis=0, keepdims=True)
        for k in range(CONV_WIDTH - 1):
            sh = CONV_WIDTH - 1 - k
            acc = acc + w_ref[k:k + 1, :] * pltpu.roll(dext, tm + 8 - sh, 0)[:tm, :]
            dw_ref[k:k + 1, :] += jnp.sum(dx * pltpu.roll(ext, sh, 0)[8:, :], axis=0, keepdims=True)
        dw_ref[4:5, :] += jnp.sum(dx, axis=0, keepdims=True)
        dax_ref[...] = acc

    return pl.pallas_call(
        body,
        name=name,
        grid=(nblk,),
        in_specs=[pl.BlockSpec((tm, D_A), lambda i: (i, 0)),
                  pl.BlockSpec((8, D_A), lambda i: (jnp.minimum((i + 1) * (tm // 8), s // 8 - 1), 0)),
                  pl.BlockSpec((tm, D_A), lambda i: (i, 0)),
                  pl.BlockSpec((8, D_A), lambda i: (jnp.maximum(i * (tm // 8) - 1, 0), 0)),
                  pl.BlockSpec((CONV_WIDTH, D_A), lambda i: (0, 0))],
        out_specs=[pl.BlockSpec((tm, D_A), lambda i: (i, 0)), pl.BlockSpec((8, D_A), lambda i: (0, 0))],
        out_shape=[_sds((s, D_A)), _sds((8, D_A))],
        compiler_params=_params("arbitrary"),
    )(dxa, dxa, ax, ax, w)


SCAN_ROWS = 512


def _row_in_tile(shape):
    return lax.broadcasted_iota(jnp.int32, shape, 0) % 8


def _lin_scan(name, a, b, reverse):
    s, c = a.shape
    t = min(SCAN_ROWS, s)
    nb = s // t

    def body(a_ref, b_ref, h_ref, p_ref, carry_ref):
        @pl.when(pl.program_id(0) == 0)
        def _():
            carry_ref[...] = jnp.zeros_like(carry_ref)

        row = _row_in_tile((t, c))
        p = a_ref[...]
        h = b_ref[...]
        for d in (1, 2, 4):
            keep = (row < 8 - d) if reverse else (row >= d)
            shift = (t - d) if reverse else d
            h = h + jnp.where(keep, p * pltpu.roll(h, shift, 0), 0.0)
            p = jnp.where(keep, p * pltpu.roll(p, shift, 0), p)
        h_ref[...] = h
        p_ref[...] = p
        edge = 0 if reverse else 7

        def tile(k, carry):
            kk = (t // 8 - 1 - k) if reverse else k
            r0 = pl.multiple_of(kk * 8, 8)
            hh = h_ref[pl.ds(r0, 8), :] + p_ref[pl.ds(r0, 8), :] * carry
            h_ref[pl.ds(r0, 8), :] = hh
            return jnp.broadcast_to(hh[edge:edge + 1, :], (8, c))

        carry_ref[...] = lax.fori_loop(0, t // 8, tile, carry_ref[...])

    spec = pl.BlockSpec((t, c), (lambda i: (nb - 1 - i, 0)) if reverse else (lambda i: (i, 0)))
    (out,) = _call(
        body,
        name=name,
        grid=(nb,),
        in_specs=[spec, spec],
        out_specs=[spec],
        out_shape=[_sds((s, c))],
        scratch_shapes=[pltpu.VMEM((t, c), f32), pltpu.VMEM((8, c), f32)],
        compiler_params=_params("arbitrary"),
    )(a, b)
    return out


def _s5_scan(name, b_re, b_im, a_re, a_im, reverse):
    s, c = b_re.shape
    t = min(SCAN_ROWS, s)
    nb = s // t

    def body(br_ref, bi_ref, ar_ref, ai_ref, hr_ref, hi_ref, cr_ref, ci_ref):
        @pl.when(pl.program_id(0) == 0)
        def _():
            cr_ref[...] = jnp.zeros_like(cr_ref)
            ci_ref[...] = jnp.zeros_like(ci_ref)

        ar1, ai1 = ar_ref[...], ai_ref[...]
        pows = [(ar1, ai1)]
        for _ in range(7):
            pr, pi = pows[-1]
            pows.append((pr * ar1 - pi * ai1, pr * ai1 + pi * ar1))
        row8 = lax.broadcasted_iota(jnp.int32, (8, c), 0)
        wr = jnp.zeros((8, c), f32)
        wi = jnp.zeros((8, c), f32)
        for r in range(8):
            pr, pi = pows[(7 - r) if reverse else r]
            wr = jnp.where(row8 == r, pr, wr)
            wi = jnp.where(row8 == r, pi, wi)
        row = _row_in_tile((t, c))
        hr = br_ref[...]
        hi = bi_ref[...]
        for d in (1, 2, 4):
            keep = (row < 8 - d) if reverse else (row >= d)
            shift = (t - d) if reverse else d
            pr, pi = pows[d - 1]
            cr = jnp.where(keep, pr, 0.0)
            ci = jnp.where(keep, pi, 0.0)
            sr = pltpu.roll(hr, shift, 0)
            si = pltpu.roll(hi, shift, 0)
            hr, hi = hr + cr * sr - ci * si, hi + cr * si + ci * sr
        hr_ref[...] = hr
        hi_ref[...] = hi
        edge = 0 if reverse else 7

        def tile(k, carry):
            car_r, car_i = carry
            kk = (t // 8 - 1 - k) if reverse else k
            r0 = pl.multiple_of(kk * 8, 8)
            xr = hr_ref[pl.ds(r0, 8), :] + wr * car_r - wi * car_i
            xi = hi_ref[pl.ds(r0, 8), :] + wr * car_i + wi * car_r
            hr_ref[pl.ds(r0, 8), :] = xr
            hi_ref[pl.ds(r0, 8), :] = xi
            return (jnp.broadcast_to(xr[edge:edge + 1, :], (8, c)), jnp.broadcast_to(xi[edge:edge + 1, :], (8, c)))

        car_r, car_i = lax.fori_loop(0, t // 8, tile, (cr_ref[...], ci_ref[...]))
        cr_ref[...] = car_r
        ci_ref[...] = car_i

    spec = pl.BlockSpec((t, c), (lambda i: (nb - 1 - i, 0)) if reverse else (lambda i: (i, 0)))
    vspec = pl.BlockSpec((1, c), lambda i: (0, 0))
    hr, hi = _call(
        body,
        name=name,
        grid=(nb,),
        in_specs=[spec, spec, vspec, vspec],
        out_specs=[spec, spec],
        out_shape=[_sds((s, c)), _sds((s, c))],
        scratch_shapes=[pltpu.VMEM((8, c), f32), pltpu.VMEM((8, c), f32)],
        compiler_params=_params("arbitrary"),
    )(b_re, b_im, a_re, a_im)
    return hr, hi


ATT_FEAT = 128
ATT_TQ = 1024
ATT_TK = 1024
ATT_TK_KEY_SIDE = 512


def _att_tiles(s, key_side=False):
    tq = min(ATT_TQ, s)
    tk = min(ATT_TK_KEY_SIDE if key_side else ATT_TK, tq)
    return tq, tk, tq // tk


def _keys_le_queries(tk, tq, k0, q0):
    row = lax.broadcasted_iota(jnp.int32, (tk, tq), 0) + k0
    col = lax.broadcasted_iota(jnp.int32, (tk, tq), 1) + q0
    return row <= col


def _attn_fwd_t(name, qt, k_aug, vt):
    h, s, _ = k_aug.shape
    tq, tk, ratio = _att_tiles(s)

    def body(qt_ref, k_ref, vt_ref, o_ref, lse_ref):
        qi = pl.program_id(1)
        qt = qt_ref[...]

        def block(kb, carry, masked):
            m, l, acc = carry
            ks = pl.multiple_of(kb * tk, tk)
            st = jnp.dot(k_ref[pl.ds(ks, tk), :], qt, preferred_element_type=f32)
            if masked:
                st = jnp.where(_keys_le_queries(tk, tq, ks, qi * tq), st, -jnp.inf)
            mn = jnp.maximum(m, jnp.max(st, axis=0, keepdims=True))
            p = jnp.exp(st - mn)
            al = jnp.exp(m - mn)
            l = al * l + jnp.sum(p, axis=0, keepdims=True)
            acc = al * acc + jnp.dot(vt_ref[kb], p.astype(bf16), preferred_element_type=f32)
            return mn, l, acc

        init = (jnp.full((1, tq), -jnp.inf, f32), jnp.zeros((1, tq), f32), jnp.zeros((HEAD_DIM, tq), f32))
        first = lax.fori_loop(0, qi * ratio, lambda kb, c: block(kb, c, False), init)
        m, l, acc = lax.fori_loop(qi * ratio, (qi + 1) * ratio, lambda kb, c: block(kb, c, True), first)
        o_ref[...] = acc / l
        lse_ref[...] = m + jnp.log(l)

    return _call(
        body,
        name=name,
        grid=(h, s // tq),
        in_specs=[pl.BlockSpec((None, None, ATT_FEAT, tq), lambda hh, i: (hh, i, 0, 0)),
                  pl.BlockSpec((None, s, ATT_FEAT), lambda hh, i: (hh, 0, 0)),
                  pl.BlockSpec((None, s // tk, HEAD_DIM, tk), lambda hh, i: (hh, 0, 0, 0))],
        out_specs=[pl.BlockSpec((None, HEAD_DIM, tq), lambda hh, i: (hh, 0, i)),
                   pl.BlockSpec((None, 1, tq), lambda hh, i: (hh, 0, i))],
        out_shape=[_sds((h, HEAD_DIM, s)), _sds((h, 1, s))],
        compiler_params=_params("parallel", "arbitrary"),
    )(qt, k_aug, vt)


def _attn_bwd_dq_t(name, qt, k_aug, v, kt, ot, dot_, lse):
    h, s, _ = k_aug.shape
    tq, tk, ratio = _att_tiles(s)

    def body(qt_ref, k_ref, v_ref, kt_ref, o_ref, do_ref, lse_ref, dq_ref, dl_ref):
        qi = pl.program_id(1)
        qt = qt_ref[...]
        dob = do_ref[...]
        delta = jnp.sum(dob.astype(f32) * o_ref[...], axis=0, keepdims=True)
        lse_v = lse_ref[...]

        def block(kb, carry, masked):
            dq, psum = carry
            ks = pl.multiple_of(kb * tk, tk)
            st = jnp.dot(k_ref[pl.ds(ks, tk), :], qt, preferred_element_type=f32)
            p = jnp.exp(st - lse_v)
            if masked:
                p = jnp.where(_keys_le_queries(tk, tq, ks, qi * tq), p, 0.0)
            dp = jnp.dot(v_ref[pl.ds(ks, tk), :], dob, preferred_element_type=f32)
            ds = p * (dp - delta)
            return (dq + jnp.dot(kt_ref[kb], ds.astype(bf16), preferred_element_type=f32),
                    psum + jnp.sum(p * dp, axis=0, keepdims=True))

        carry = lax.fori_loop(0, qi * ratio, lambda kb, c: block(kb, c, False),
                              (jnp.zeros((HEAD_DIM, tq), f32), jnp.zeros((1, tq), f32)))
        dq, psum = lax.fori_loop(qi * ratio, (qi + 1) * ratio, lambda kb, c: block(kb, c, True), carry)
        dq_ref[...] = dq * ATT_SCALE
        dl_ref[...] = psum

    qspec = pl.BlockSpec((None, HEAD_DIM, tq), lambda hh, i: (hh, 0, i))
    rspec = pl.BlockSpec((None, 1, tq), lambda hh, i: (hh, 0, i))
    return _call(
        body,
        name=name,
        grid=(h, s // tq),
        in_specs=[pl.BlockSpec((None, None, ATT_FEAT, tq), lambda hh, i: (hh, i, 0, 0)),
                  pl.BlockSpec((None, s, ATT_FEAT), lambda hh, i: (hh, 0, 0)),
                  pl.BlockSpec((None, s, HEAD_DIM), lambda hh, i: (hh, 0, 0)),
                  pl.BlockSpec((None, s // tk, HEAD_DIM, tk), lambda hh, i: (hh, 0, 0, 0)),
                  qspec, pl.BlockSpec((None, None, HEAD_DIM, tq), lambda hh, i: (hh, i, 0, 0)), rspec],
        out_specs=[qspec, rspec],
        out_shape=[_sds((h, HEAD_DIM, s)), _sds((h, 1, s))],
        compiler_params=_params("parallel", "arbitrary"),
    )(qt, k_aug, v, kt, ot, dot_, lse)


def _attn_bwd_dkv_t(name, qt_blocks, k_aug, v, qh, do, dot_blocks, lse, delta):
    h, s, _ = k_aug.shape
    tq, tk, ratio = _att_tiles(s, key_side=True)
    nq = s // tq

    def body(qt_ref, k_ref, v_ref, q_ref, do_ref, dot_ref, lse_ref, dl_ref, dk_ref, dv_ref, dck_ref, dsum_ref):
        kj = pl.program_id(1)
        kk = k_ref[...]
        vv = v_ref[...]
        dsum_ref[...] = jnp.zeros_like(dsum_ref)

        def block(qi, carry, masked):
            dk, dv = carry
            qs = pl.multiple_of(qi * tq, tq)
            st = jnp.dot(kk, qt_ref[qi], preferred_element_type=f32)
            p = jnp.exp(st - lse_ref[qi])
            if masked:
                p = jnp.where(_keys_le_queries(tk, tq, kj * tk, qs), p, 0.0)
            dv = dv + jnp.dot(p.astype(bf16), do_ref[pl.ds(qs, tq), :], preferred_element_type=f32)
            dp = jnp.dot(vv, dot_ref[qi], preferred_element_type=f32)
            ds = p * (dp - dl_ref[qi])
            dsum_ref[...] += ds
            dk = dk + jnp.dot(ds.astype(bf16), q_ref[pl.ds(qs, tq), :], preferred_element_type=f32)
            return dk, dv

        first = kj // ratio
        carry = block(first, (jnp.zeros((tk, HEAD_DIM), f32), jnp.zeros((tk, HEAD_DIM), f32)), True)
        dk, dv = lax.fori_loop(first + 1, nq, lambda qi, c: block(qi, c, False), carry)
        dk_ref[...] = dk
        dv_ref[...] = dv
        col = jnp.sum(dsum_ref[...], axis=1, keepdims=True)
        dck_ref[...] = -jnp.transpose(jnp.broadcast_to(col, (tk, 128)))[0:1, :]

    full = lambda shape: pl.BlockSpec((None,) + shape, lambda hh, j: (hh,) + (0,) * len(shape))
    kspec = pl.BlockSpec((None, tk, HEAD_DIM), lambda hh, j: (hh, j, 0))
    return _call(
        body,
        name=name,
        grid=(h, s // tk),
        in_specs=[full((nq, ATT_FEAT, tq)),
                  pl.BlockSpec((None, tk, ATT_FEAT), lambda hh, j: (hh, j, 0)),
                  kspec, full((s, HEAD_DIM)), full((s, HEAD_DIM)), full((nq, HEAD_DIM, tq)),
                  full((nq, 1, tq)), full((nq, 1, tq))],
        out_specs=[kspec, kspec, pl.BlockSpec((None, None, 1, tk), lambda hh, j: (hh, j, 0, 0))],
        out_shape=[_sds((h, s, HEAD_DIM)), _sds((h, s, HEAD_DIM)), _sds((h, s // tk, 1, tk))],
        scratch_shapes=[pltpu.VMEM((tk, tq), f32)],
        compiler_params=_params("parallel", "arbitrary"),
    )(qt_blocks, k_aug, v, qh, do, dot_blocks, lse, delta)


C_LANES = 128


def _selections():
    h = jnp.arange(N_HEADS)[:, None, None]
    row = jnp.arange(D_B + 3 * C_LANES)[None, :, None]
    col = jnp.arange(ATT_FEAT)[None, None, :]
    head_col = (row < D_B) & (row // HEAD_DIM == h) & (col == row % HEAD_DIM)

    def c_part(p, lane0):
        return (row == D_B + p * C_LANES + h) & (col == lane0 + p)

    c_q = c_part(0, HEAD_DIM) | c_part(1, HEAD_DIM) | c_part(2, HEAD_DIM)
    c_k = c_part(0, HEAD_DIM + 3) | c_part(1, HEAD_DIM + 3) | c_part(2, HEAD_DIM + 3)
    sel_q = (head_col | c_q).astype(bf16)
    sel_k = head_col.astype(bf16) - c_k.astype(bf16)
    sel_h = head_col[:, :D_B, :HEAD_DIM].astype(bf16)
    lane = jnp.arange(ATT_FEAT)
    ones_q = ((lane >= HEAD_DIM + 3) & (lane < HEAD_DIM + 6)).astype(f32)
    ones_k = ((lane >= HEAD_DIM) & (lane < HEAD_DIM + 3)).astype(f32)
    return dict(sel_qt=sel_q.transpose(0, 2, 1), sel_k=sel_k, sel_h=sel_h, sel_ht=sel_h.transpose(0, 2, 1),
                ones_q=ones_q.reshape(ATT_FEAT, 1), ones_k=ones_k.reshape(1, ATT_FEAT))


def _attn_prep(name, z, c, sel):
    s = z.shape[0]
    tq, tk, ratio = _att_tiles(s)

    def body(q_ref, k_ref, v_ref, c_ref, sqt_ref, sk_ref, sh_ref, sht_ref, oq_ref, ok_ref,
             qt_out, ka_out, kt_out, vt_out, v_out, qh_out):
        cv = c_ref[...]
        hi = cv.astype(bf16)
        r1 = cv - hi.astype(f32)
        mid = r1.astype(bf16)
        lo = (r1 - mid.astype(f32)).astype(bf16)
        qs = (q_ref[...] * ATT_SCALE).astype(bf16)
        kb = k_ref[...].astype(bf16)
        vb = v_ref[...].astype(bf16)
        xq = jnp.concatenate([qs, hi, mid, lo], axis=-1)
        xk = jnp.concatenate([kb, hi, mid, lo], axis=-1)
        for h in range(N_HEADS):
            qt = lax.dot_general(sqt_ref[h], xq, _DN["nt"], preferred_element_type=f32) + oq_ref[...]
            qt_out[h, 0] = qt.astype(bf16)
            ka_out[h] = (jnp.dot(xk, sk_ref[h], preferred_element_type=f32) + ok_ref[...]).astype(bf16)
            kt = lax.dot_general(sht_ref[h], kb, _DN["nt"], preferred_element_type=f32).astype(bf16)
            vt = lax.dot_general(sht_ref[h], vb, _DN["nt"], preferred_element_type=f32).astype(bf16)
            for j in range(ratio):
                kt_out[h, j] = kt[:, j * tk:(j + 1) * tk]
                vt_out[h, j] = vt[:, j * tk:(j + 1) * tk]
            v_out[h] = jnp.dot(vb, sh_ref[h], preferred_element_type=f32).astype(bf16)
            qh_out[h] = jnp.dot(qs, sh_ref[h], preferred_element_type=f32).astype(bf16)

    whole = lambda a: pl.BlockSpec(a.shape, lambda i, nd=a.ndim: (0,) * nd)
    consts = [sel["sel_qt"], sel["sel_k"], sel["sel_h"], sel["sel_ht"], sel["ones_q"], sel["ones_k"]]
    return pl.pallas_call(
        body,
        name=name,
        grid=(s // tq,),
        in_specs=[pl.BlockSpec((tq, D_B), lambda i: (i, 2)), pl.BlockSpec((tq, D_B), lambda i: (i, 3)),
                  pl.BlockSpec((tq, D_B), lambda i: (i, 4)), pl.BlockSpec((tq, C_LANES), lambda i: (i, 0))]
        + [whole(a) for a in consts],
        out_specs=[pl.BlockSpec((N_HEADS, 1, ATT_FEAT, tq), lambda i: (0, i, 0, 0)),
                   pl.BlockSpec((N_HEADS, tq, ATT_FEAT), lambda i: (0, i, 0)),
                   pl.BlockSpec((N_HEADS, ratio, HEAD_DIM, tk), lambda i: (0, i, 0, 0)),
                   pl.BlockSpec((N_HEADS, ratio, HEAD_DIM, tk), lambda i: (0, i, 0, 0)),
                   pl.BlockSpec((N_HEADS, tq, HEAD_DIM), lambda i: (0, i, 0)),
                   pl.BlockSpec((N_HEADS, tq, HEAD_DIM), lambda i: (0, i, 0))],
        out_shape=[_sds((N_HEADS, s // tq, ATT_FEAT, tq), bf16), _sds((N_HEADS, s, ATT_FEAT), bf16),
                   _sds((N_HEADS, s // tk, HEAD_DIM, tk), bf16), _sds((N_HEADS, s // tk, HEAD_DIM, tk), bf16),
                   _sds((N_HEADS, s, HEAD_DIM), bf16), _sds((N_HEADS, s, HEAD_DIM), bf16)],
        compiler_params=_params("parallel"),
    )(z, z, z, c, *consts)


def _attn_do_prep(name, dob, sel):
    s = dob.shape[0]
    tq = _att_tiles(s)[0]

    def body(do_ref, sh_ref, sht_ref, dot_out, do_out):
        db = do_ref[...].astype(bf16)
        for h in range(N_HEADS):
            dot_out[h, 0] = lax.dot_general(sht_ref[h], db, _DN["nt"], preferred_element_type=f32).astype(bf16)
            do_out[h] = jnp.dot(db, sh_ref[h], preferred_element_type=f32).astype(bf16)

    whole = lambda a: pl.BlockSpec(a.shape, lambda i, nd=a.ndim: (0,) * nd)
    return pl.pallas_call(
        body,
        name=name,
        grid=(s // tq,),
        in_specs=[pl.BlockSpec((tq, D_B), lambda i: (i, 0)), whole(sel["sel_h"]), whole(sel["sel_ht"])],
        out_specs=[pl.BlockSpec((N_HEADS, 1, HEAD_DIM, tq), lambda i: (0, i, 0, 0)),
                   pl.BlockSpec((N_HEADS, tq, HEAD_DIM), lambda i: (0, i, 0))],
        out_shape=[_sds((N_HEADS, s // tq, HEAD_DIM, tq), bf16), _sds((N_HEADS, s, HEAD_DIM), bf16)],
        compiler_params=_params("parallel"),
    )(dob, sel["sel_h"], sel["sel_ht"])


def _dz_assemble(name, dax, dag, dqt, dkh, dvh, df, dcu, sel):
    s = dax.shape[0]
    tm = _tile(s, 512)

    def body(dax_ref, dag_ref, dqt_ref, dk_ref, dv_ref, df_ref, dcu_ref, sht_ref, o_ref):
        dq = jnp.zeros((tm, D_B), f32)
        dk = jnp.zeros((tm, D_B), f32)
        dv = jnp.zeros((tm, D_B), f32)
        for h in range(N_HEADS):
            place = sht_ref[h]
            dq = dq + lax.dot_general(dqt_ref[h].astype(bf16), place, _DN["tn"], preferred_element_type=f32)
            dk = dk + jnp.dot(dk_ref[h].astype(bf16), place, preferred_element_type=f32)
            dv = dv + jnp.dot(dv_ref[h].astype(bf16), place, preferred_element_type=f32)
        pieces = [dax_ref[...], dag_ref[...], dq, dk, dv, df_ref[...], dcu_ref[...]]
        off = 0
        for p in pieces:
            o_ref[:, off:off + p.shape[1]] = p.astype(bf16)
            off += p.shape[1]

    rows = lambda c_: pl.BlockSpec((tm, c_), lambda i: (i, 0))
    heads = pl.BlockSpec((N_HEADS, tm, HEAD_DIM), lambda i: (0, i, 0))
    return pl.pallas_call(
        body,
        name=name,
        grid=(s // tm,),
        in_specs=[rows(D_A), rows(D_A), pl.BlockSpec((N_HEADS, HEAD_DIM, tm), lambda i: (0, 0, i)), heads, heads,
                  rows(128), rows(D_C), pl.BlockSpec(sel["sel_ht"].shape, lambda i: (0, 0, 0))],
        out_specs=rows(N_IN_P),
        out_shape=_sds((s, N_IN_P), bf16),
        compiler_params=_params("parallel"),
    )(dax, dag, dqt, dkh, dvh, df, dcu, sel["sel_ht"])


def _s5_disc_fn(are, aim, ldt):
    dt = jnp.exp(ldt)
    er = jnp.exp(are * dt)
    br = er * jnp.cos(aim * dt)
    bi = er * jnp.sin(aim * dt)
    nr = br - 1.0
    den = are * are + aim * aim
    return br, bi, (nr * are + bi * aim) / den, (bi * are - nr * aim) / den


def _s5_disc(name, are, aim, ldt):
    def body(a_ref, b_ref, c_ref, o0, o1, o2, o3):
        r = _s5_disc_fn(a_ref[...], b_ref[...], c_ref[...])
        o0[...], o1[...], o2[...], o3[...] = r

    shp = _sds((S5_GROUPS, S5_STATE))
    return pl.pallas_call(body, name=name, out_shape=[shp] * 4)(are, aim, ldt)


def _s5_disc_bwd(name, are, aim, ldt, cts):
    def body(a_ref, b_ref, c_ref, d0, d1, d2, d3, o0, o1, o2):
        _, vjp = jax.vjp(_s5_disc_fn, a_ref[...], b_ref[...], c_ref[...])
        o0[...], o1[...], o2[...] = vjp((d0[...], d1[...], d2[...], d3[...]))

    shp = _sds((S5_GROUPS, S5_STATE))
    return pl.pallas_call(body, name=name, out_shape=[shp, shp, _sds((S5_GROUPS, 1))])(are, aim, ldt, *cts)


def _adamw_rows(w, g, m, v):
    m = ADAM_B1 * m + (1.0 - ADAM_B1) * g
    v = ADAM_B2 * v + (1.0 - ADAM_B2) * (g * g)
    m_hat = m / (1.0 - ADAM_B1 ** ADAM_STEP)
    v_hat = v / (1.0 - ADAM_B2 ** ADAM_STEP)
    return -ADAM_LR * (m_hat / (jnp.sqrt(v_hat) + ADAM_EPS) + ADAM_WD * w), m, v


def _adamw(name, w, ga, gb, m, v):
    rows, cols = w.shape
    tr = _row_tile(rows)

    def body(w_ref, ga_ref, gb_ref, m_ref, v_ref, g_out, d_out, m_out, v_out):
        g = ga_ref[...] + gb_ref[...]
        d, mm, vv = _adamw_rows(w_ref[...], g, m_ref[...], v_ref[...])
        g_out[...] = g
        d_out[...] = d
        m_out[...] = mm
        v_out[...] = vv

    spec = pl.BlockSpec((tr, cols), lambda i: (i, 0))
    return pl.pallas_call(
        body, name=name, grid=(rows // tr,), in_specs=[spec] * 5, out_specs=[spec] * 4,
        out_shape=[_sds((rows, cols))] * 4, compiler_params=_params("parallel"),
    )(w, ga, gb, m, v)


def _sum_stack(name, st):
    n, rows, cols = st.shape
    tr = _row_tile(rows)

    def body(s_ref, o_ref):
        acc = s_ref[0].astype(f32)
        for j in range(1, n):
            acc = acc + s_ref[j].astype(f32)
        o_ref[...] = acc

    return pl.pallas_call(
        body, name=name, grid=(rows // tr,), in_specs=[pl.BlockSpec((n, tr, cols), lambda i: (0, i, 0))],
        out_specs=pl.BlockSpec((tr, cols), lambda i: (i, 0)), out_shape=_sds((rows, cols)),
        compiler_params=_params("parallel"),
    )(st)


def _block_diag(w):
    h, n, m = w.shape
    return jnp.einsum("hij,hg->higj", w, jnp.eye(h, dtype=w.dtype)).reshape(h * n, h * m)


def _block_diag_part(dense, h):
    n, m = dense.shape[0] // h, dense.shape[1] // h
    return jnp.einsum("higj,hg->hij", dense.reshape(h, n, h, m), jnp.eye(h, dtype=dense.dtype))


def _s5_matrices(coef_re, coef_im, b_re, b_im, c_re, c_im):
    bb_re = coef_re[:, :, None] * b_re - coef_im[:, :, None] * b_im
    bb_im = coef_re[:, :, None] * b_im + coef_im[:, :, None] * b_re
    wb_re = _block_diag(jnp.swapaxes(bb_re, 1, 2))
    wb_im = _block_diag(jnp.swapaxes(bb_im, 1, 2))
    wc_re = _block_diag(jnp.swapaxes(c_re, 1, 2))
    wc_im = _block_diag(jnp.swapaxes(-c_im, 1, 2))
    return wb_re, wb_im, wc_re, wc_im


def _shift_down(t):
    return jnp.concatenate([jnp.zeros((1, t.shape[1]), t.dtype), t[:-1]], axis=0)


def _shift_up(t):
    return jnp.concatenate([t[1:], jnp.zeros((1, t.shape[1]), t.dtype)], axis=0)


def _row(v):
    return v.reshape(1, -1)


def _ffn_fwd(tag, h, get, names, gamma, beta, gate_first=False):
    wg = get(names[0])
    if gate_first:
        g = _ffn_gate(tag + "_gate", h, wg)
        wu = get(names[1])
        u, act = _ffn_up_given_gate(tag + "_up", h, wu, g)
    else:
        wu = get(names[1])
        g, u, act = _ffn_up(tag + "_up", h, wg, wu)
    wd = get(names[2])
    r, out = _mm_ln(tag + "_down", act, wd, h, gamma, beta, 0.5, k_slabs=True)
    return out, dict(h=h, g=g, u=u, act=act, r=r, wg=wg, wu=wu, wd=wd)


def _ffn_bwd(tag, dout, sv, names, gamma, put, after_ln=None):
    s = dout.shape[0]
    dr, dgam, dbet = _ln_bwd(tag + "_lnb", sv["r"], dout, gamma)
    if after_ln is not None:
        after_ln(dgam, dbet)
    put(names[2], _mm_plain(tag + "_dwd", "tn", _Slabs(sv["act"]), dr, (D_FF, D_MODEL, s), scale=0.5, out_dtype=bf16,
                            tiles=(FF_SLAB, 1024, _tile(s, 2048))))
    dg, du = _ffn_dact(tag + "_dact", dr, sv["wd"], sv["g"], sv["u"])
    dwg, dwu = _mm2(tag + "_dwgu", "tn", (D_FF, D_MODEL, s), _Slabs(dg), sv["h"], _Slabs(du), None, separate=True,
                    out_dtype=bf16, tiles=(FF_SLAB, 1024, _tile(s, 2048)))
    put(names[0], dwg)
    put(names[1], dwu)
    slabs = range(dg.shape[0])
    dh = _mm(tag + "_dh", "nn", (s, D_MODEL, FF_SLAB), (_tile(s, 512), D_MODEL, FF_SLAB),
             [_KPart(dg, j) for j in slabs] + [_KPart(du, j) for j in slabs],
             [_KPart(sv["wg"], j) for j in slabs] + [_KPart(sv["wu"], j) for j in slabs],
             [(j, j, 0) for j in range(2 * len(slabs))], 1,
             lambda accs, extras, vecs: [accs[0] + ALPHA * extras[0]], [f32], extras=[dr])[0]
    return dh, dgam, dbet


def _mixer_fwd(tag, h1, w):
    s = h1.shape[0]
    z = _mm_plain(tag + "_win", "nn", h1, w["w_in"], (s, N_IN_P, D_MODEL), tiles=(_tile(s, 512), 768, D_MODEL))
    ag, f, cu_cols = (z, D_A, 1), (z, 128, F_OFF // 128), (z, D_C, CU_OFF // D_C)
    cu = z[:, CU_OFF:]
    xa = _conv_fwd(tag + "_conv", z, w["conv_w"], w["conv_b"])
    a, gated = _rg_gates(tag + "_gates", xa, w["rg_wa"], w["rg_wx"], w["rg_ba"], w["rg_bx"], w["rg_lam"])
    ha = _lin_scan(tag + "_rgscan", a, gated, False)
    ones = jnp.ones((s, 128), f32)
    c = _lin_scan(tag + "_cumf", ones, _log_f(tag + "_logf", f, w["fox_bf"]), False)
    att = dict(zip(("qt", "k_aug", "kt", "vt", "v", "qh"), _attn_prep(tag + "_attnprep", z, c, w["sel"])))
    ot, lse = _attn_fwd_t(tag + "_attn", att["qt"], att["k_aug"], att["vt"])
    ob = ot.reshape(D_B, s).T
    bu_re, bu_im = _mm2(tag + "_s5in", "nn", (s, S5_LANES, D_C), cu, w["wb_re"], None, w["wb_im"], separate=True,
                        tiles=(_tile(s, 512), 1024, D_C))
    hre, him = _s5_scan(tag + "_s5scan", bu_re, bu_im, w["abar_re"], w["abar_im"], False)
    o = _mix_out(tag + "_mixout", ag, ha, ob, hre, him, cu_cols, w["s5_d"], w["mix_g"], w["wc_re"], w["wc_im"],
                 w["w_glu"])
    sv = dict(h1=h1, z=z, ag=ag, f=f, cu=cu, cu_cols=cu_cols, xa=xa, a=a, ha=ha, att=att, ot=ot, lse=lse, ob=ob,
              hre=hre, him=him, o=o)
    return o, sv


def _mixer_bwd(tag, do, dr2, sv, w, put):
    s = do.shape[0]
    (dag, dha, dob, dhre, dhim, dcu1, dwcr, dwci, dwglu, dd, dgn) = _mix_out_bwd(
        tag + "_mixoutb", do, sv["ag"], sv["ha"], sv["ob"], sv["hre"], sv["him"], sv["cu_cols"], w["s5_d"], w["mix_g"],
        w["wc_re"], w["wc_im"], w["w_glu"])
    put("s5_w_glu", dwglu.astype(bf16))
    gre, gim = _s5_scan(tag + "_s5scanb", dhre, dhim, w["abar_re"], -w["abar_im"], True)
    dab_re, dab_im = _s5_decay_grad(tag + "_s5dec", sv["hre"], sv["him"], gre, gim)
    dwb_re, dwb_im = _mm2(tag + "_s5dwb", "tn", (D_C, S5_LANES, s), sv["cu"], gre, None, gim, separate=True,
                          tiles=(D_C, 1024, _tile(s, 1024)))
    dcu = _mm2(tag + "_s5dcu", "nt", (s, D_C, S5_LANES), gre, w["wb_re"], gim, w["wb_im"], add=dcu1,
               tiles=(_tile(s, 512), D_C, 1024))[0]
    att = sv["att"]
    tq = _att_tiles(s)[0]
    nt = s // tq
    dot_blocks, doh = _attn_do_prep(tag + "_doprep", dob, w["sel"])
    dqt, delta = _attn_bwd_dq_t(tag + "_attndq", att["qt"], att["k_aug"], att["v"], att["kt"], sv["ot"], dot_blocks,
                                sv["lse"])
    dkh, dvh, dck = _attn_bwd_dkv_t(tag + "_attndkv", att["qt"], att["k_aug"], att["v"], att["qh"], doh, dot_blocks,
                                    sv["lse"].reshape(N_HEADS, nt, 1, tq), delta.reshape(N_HEADS, nt, 1, tq))
    dc = jnp.pad(dck.reshape(N_HEADS, s).T, ((0, 0), (0, 128 - N_HEADS)))
    dlf = _lin_scan(tag + "_cumfb", jnp.ones((s, 128), f32), dc, True)
    df, dbf = _log_f_bwd(tag + "_logfb", dlf, sv["f"], w["fox_bf"])
    ga = _lin_scan(tag + "_rgscanb", _shift_up(sv["a"]), dha, True)
    dxa, dwa, dwx, dba, dbx, dlam = _rg_gates_bwd(tag + "_gatesb", sv["xa"], ga, _shift_down(sv["ha"]), w["rg_wa"],
                                                  w["rg_wx"], w["rg_ba"], w["rg_bx"], w["rg_lam"])
    dax, dconv = _conv_bwd(tag + "_convb", dxa, sv["z"], w["conv_w"])
    dz = _dz_assemble(tag + "_dz", dax, dag, dqt, dkh, dvh, df, dcu, w["sel"])
    put("w_in", _mm_plain(tag + "_dwin", "tn", sv["h1"], dz, (D_MODEL, N_IN_P, s), out_dtype=bf16,
                          tiles=(512, 768, _tile(s, 1024))))
    dh1 = _mm_plain(tag + "_dh1", "nt", dz, w["w_in"], (s, D_MODEL, N_IN_P), add=dr2, add_coef=ALPHA,
                    tiles=(_tile(s, 512), 1024, 768))
    grads = dict(dconv=dconv, dwa=dwa, dwx=dwx, dba=dba, dbx=dbx, dlam=dlam, dbf=dbf,
                 dab_re=dab_re, dab_im=dab_im, dwb_re=dwb_re, dwb_im=dwb_im, dwcr=dwcr, dwci=dwci, dd=dd, dgn=dgn)
    return dh1, grads


SMALL_NAMES = ["ln1_g", "ln1_b", "conv_w", "conv_b", "rg_w_a", "rg_b_a", "rg_w_x", "rg_b_x", "rg_lambda", "fox_b_f",
               "s5_a_re", "s5_a_im", "s5_log_dt", "s5_b_re", "s5_b_im", "s5_c_re", "s5_c_im", "s5_d", "mix_norm_g",
               "ln2_g", "ln2_b", "ln3_g", "ln3_b"]
BIG_NAMES = ["ffn1_w_gate", "ffn1_w_up", "ffn1_w_down", "w_in", "s5_w_glu", "w_out", "ffn2_w_gate", "ffn2_w_up",
             "ffn2_w_down"]


def _local_step(x, target, weight, small, on_grads, on_small):
    h = x
    saved = []
    sel = _selections()
    for l in range(DEPTH):
        get = functools.partial(weight, l)

        sm = {n: small[n][l] for n in SMALL_NAMES}
        abar_re, abar_im, coef_re, coef_im = _s5_disc(f"l{l}_s5disc", sm["s5_a_re"], sm["s5_a_im"],
                                                      sm["s5_log_dt"].reshape(S5_GROUPS, 1))
        mats, mats_vjp = jax.vjp(_s5_matrices, coef_re, coef_im, sm["s5_b_re"], sm["s5_b_im"], sm["s5_c_re"],
                                 sm["s5_c_im"])
        w = dict(
            sel=sel, conv_w=sm["conv_w"], conv_b=_row(sm["conv_b"]),
            rg_wa=_block_diag(sm["rg_w_a"]).astype(bf16), rg_wx=_block_diag(sm["rg_w_x"]).astype(bf16),
            rg_ba=_row(sm["rg_b_a"]), rg_bx=_row(sm["rg_b_x"]), rg_lam=_row(sm["rg_lambda"]),
            fox_bf=jnp.pad(_row(sm["fox_b_f"]), ((0, 0), (0, 128 - N_HEADS))),
            abar_re=_row(abar_re), abar_im=_row(abar_im),
            wb_re=mats[0].astype(bf16), wb_im=mats[1].astype(bf16), wc_re=mats[2].astype(bf16),
            wc_im=mats[3].astype(bf16), s5_d=_row(sm["s5_d"]), mix_g=_row(sm["mix_norm_g"]))
        h1, sv1 = _ffn_fwd(f"l{l}_ffn1", h, get, GROUPS["F1"], _row(sm["ln1_g"]), _row(sm["ln1_b"]),
                           gate_first=(l == 0))
        w["w_in"], w["w_glu"] = get("w_in"), get("s5_w_glu")
        o, svm = _mixer_fwd(f"l{l}_mix", h1, w)
        w_out = get("w_out")
        r2, h2 = _mm_ln(f"l{l}_wout", o, w_out, h1, _row(sm["ln2_g"]), _row(sm["ln2_b"]), 1.0)
        h3, sv2 = _ffn_fwd(f"l{l}_ffn2", h2, get, GROUPS["F2"], _row(sm["ln3_g"]), _row(sm["ln3_b"]))
        saved.append(dict(sm=sm, w=w, w_out=w_out, sv1=sv1, svm=svm, r2=r2, sv2=sv2, mats_vjp=mats_vjp))
        h = h3

    dh, loss_row = _loss_head("loss_head", h, target)
    s = x.shape[0]
    gsmall = {n: [None] * DEPTH for n in SMALL_NAMES}
    for l in reversed(range(DEPTH)):
        sd = saved[l]
        sm, w = sd["sm"], sd["w"]

        def put(name, grad, l=l):
            on_grads((l, name), grad)

        dh2, dgam, dbet = _ffn_bwd(f"l{l}_ffn2", dh, sd["sv2"], GROUPS["F2"], _row(sm["ln3_g"]), put)
        gsmall["ln3_g"][l], gsmall["ln3_b"][l] = dgam[0], dbet[0]
        dr2, dgam, dbet = _ln_bwd(f"l{l}_ln2b", sd["r2"], dh2, _row(sm["ln2_g"]))
        gsmall["ln2_g"][l], gsmall["ln2_b"][l] = dgam[0], dbet[0]
        put("w_out", _mm_plain(f"l{l}_dwout", "tn", sd["svm"]["o"], dr2, (D_MODEL, D_MODEL, s), out_dtype=bf16))
        do = _mm_plain(f"l{l}_do", "nt", dr2, sd["w_out"], (s, D_MODEL, D_MODEL))
        dh1, g = _mixer_bwd(f"l{l}_mix", do, dr2, sd["svm"], w, put)
        gsmall["conv_w"][l], gsmall["conv_b"][l] = g["dconv"][:CONV_WIDTH], g["dconv"][CONV_WIDTH]
        gsmall["rg_w_a"][l] = _block_diag_part(g["dwa"], N_HEADS)
        gsmall["rg_w_x"][l] = _block_diag_part(g["dwx"], N_HEADS)
        gsmall["rg_b_a"][l], gsmall["rg_b_x"][l], gsmall["rg_lambda"][l] = g["dba"][0], g["dbx"][0], g["dlam"][0]
        gsmall["fox_b_f"][l] = g["dbf"][0, :N_HEADS]
        dcoef_re, dcoef_im, db_re, db_im, dc_re, dc_im = sd["mats_vjp"]((g["dwb_re"], g["dwb_im"], g["dwcr"], g["dwci"]))
        da_re, da_im, dldt = _s5_disc_bwd(
            f"l{l}_s5discb", sm["s5_a_re"], sm["s5_a_im"], sm["s5_log_dt"].reshape(S5_GROUPS, 1),
            (g["dab_re"].reshape(S5_GROUPS, S5_STATE), g["dab_im"].reshape(S5_GROUPS, S5_STATE), dcoef_re, dcoef_im))
        gsmall["s5_a_re"][l], gsmall["s5_a_im"][l], gsmall["s5_log_dt"][l] = da_re, da_im, dldt[:, 0]
        gsmall["s5_b_re"][l], gsmall["s5_b_im"][l], gsmall["s5_c_re"][l], gsmall["s5_c_im"][l] = db_re, db_im, dc_re, dc_im
        gsmall["s5_d"][l], gsmall["mix_norm_g"][l] = g["dd"][0], g["dgn"][0]

        def after_ln(dgam, dbet, l=l):
            gsmall["ln1_g"][l], gsmall["ln1_b"][l] = dgam[0], dbet[0]
            if l == 0:
                on_small({n: jnp.stack(v) for n, v in gsmall.items()})

        dh, _, _ = _ffn_bwd(f"l{l}_ffn1", dh1, sd["sv1"], GROUPS["F1"], _row(sm["ln1_g"]), put, after_ln)
    return loss_row[0, 0], dh


def _position():
    return lax.axis_index("x"), lax.axis_index("y"), lax.axis_index("c")


_ANY = pl.BlockSpec(memory_space=pl.ANY)


COLUMN_SHARDED = ("ffn1_w_gate", "ffn1_w_up", "ffn2_w_gate", "ffn2_w_up")
PACK_QUANTUM = 128 * 256


def _permute_in_cols(w):
    pad = jnp.zeros(w.shape[:-1] + (128 - N_HEADS,), w.dtype)
    return jnp.concatenate([w[..., :F_OFF + N_HEADS], pad, w[..., F_OFF + N_HEADS:]], axis=-1)


def _unpermute_in_cols(w):
    return jnp.concatenate([w[..., :F_OFF + N_HEADS], w[..., CU_OFF:]], axis=-1)


def _pack(arrs):
    flat = jnp.concatenate([a.reshape(-1) for a in arrs])
    pad = -flat.shape[0] % PACK_QUANTUM
    return jnp.pad(flat, (0, pad)).reshape(-1, 128)


def _unpack(buf, shapes):
    flat = buf.reshape(-1)
    out, off = [], 0
    for shp in shapes:
        size = math.prod(shp)
        out.append(flat[off:off + size].reshape(shp))
        off += size
    return out


WEIGHT_NAMES = ["ffn1_w_gate", "ffn1_w_up", "ffn1_w_down", "ln1_g", "ln1_b", "w_in", "conv_w", "conv_b", "rg_w_a",
                "rg_b_a", "rg_w_x", "rg_b_x", "rg_lambda", "fox_b_f", "s5_a_re", "s5_a_im", "s5_log_dt", "s5_b_re",
                "s5_b_im", "s5_c_re", "s5_c_im", "s5_d", "s5_w_glu", "mix_norm_g", "w_out", "ln2_g", "ln2_b",
                "ffn2_w_gate", "ffn2_w_up", "ffn2_w_down", "ln3_g", "ln3_b"]


def _remote(src, dst, send_sems, recv_sems, k, peer):
    return pltpu.make_async_remote_copy(src_ref=src, dst_ref=dst, send_sem=send_sems.at[k], recv_sem=recv_sems.at[k],
                                        device_id=peer, device_id_type=MESH)


class _ChipGatherPart:
    def __init__(self, arrays):
        self.arrays, self.results = list(arrays), None

    def out_shape(self):
        return [_sds((N_CHIPS,) + a.shape, a.dtype) for a in self.arrays]

    def sems(self):
        n = len(self.arrays)
        return [pltpu.SemaphoreType.DMA((3 * n,)), pltpu.SemaphoreType.DMA((3 * n,)), pltpu.SemaphoreType.DMA((n,))]

    def copies(self, ins, outs, sems):
        send_sems, recv_sems, local_sems = sems
        x, y, c = _position()
        me = 2 * x + y
        local, sends, recvs = [], [], []
        for i, (src, dst) in enumerate(zip(ins, outs)):
            local.append(pltpu.make_async_copy(self.mine(src, me), dst.at[me], local_sems.at[i]))
            for r, (px, py) in enumerate([(1 - x, y), (x, 1 - y), (1 - x, 1 - y)]):
                peer = 2 * px + py
                sends.append(_remote(self.theirs(src, peer), dst.at[me], send_sems, recv_sems, 3 * i + r, (px, py, c)))
                recvs.append(_remote(self.mine(src, me), dst.at[peer], send_sems, recv_sems, 3 * i + r, (px, py, c)))
        return local, sends, recvs

    def mine(self, src, me):
        return src

    def theirs(self, src, peer):
        return src


class _ChipGatherHalvesPart(_ChipGatherPart):
    def sems(self):
        n = len(self.arrays)
        return super().sems() + [pltpu.SemaphoreType.DMA((3 * n,)), pltpu.SemaphoreType.DMA((3 * n,))]

    def _half(self, ref, which):
        rows = ref.shape[0] // 2
        return ref.at[pl.ds(which * rows, rows)]

    def copies(self, ins, outs, sems):
        send_sems, recv_sems, local_sems = sems[:3]
        x, y, c = _position()
        me = 2 * x + y
        local, sends, recvs = [], [], []
        for i, (src, dst) in enumerate(zip(ins, outs)):
            local.append(pltpu.make_async_copy(src, dst.at[me], local_sems.at[i]))
            for r, (px, py) in enumerate([(1 - x, y), (x, 1 - y), (1 - x, 1 - y)]):
                sends.append(_remote(self._half(src, c), self._half(dst.at[me], c), send_sems, recv_sems, 3 * i + r,
                                     (px, py, c)))
                recvs.append(_remote(self._half(src, c), self._half(dst.at[2 * px + py], c), send_sems, recv_sems,
                                     3 * i + r, (px, py, c)))
        return local, sends, recvs

    def forwards(self, ins, outs, sems):
        send_sems, recv_sems = sems[3:]
        x, y, c = _position()
        sends, recvs = [], []
        for i, dst in enumerate(outs):
            for r, (px, py) in enumerate([(1 - x, y), (x, 1 - y), (1 - x, 1 - y)]):
                slot = dst.at[2 * px + py]
                sends.append(_remote(self._half(slot, c), self._half(slot, c), send_sems, recv_sems, 3 * i + r,
                                     (x, y, 1 - c)))
                recvs.append(_remote(self._half(slot, c), self._half(slot, 1 - c), send_sems, recv_sems, 3 * i + r,
                                     (x, y, 1 - c)))
        return sends, recvs


class _ChipScatterPart(_ChipGatherPart):
    def out_shape(self):
        return [_sds(a.shape, a.dtype) for a in self.arrays]

    def mine(self, src, me):
        return src.at[me]

    def theirs(self, src, peer):
        return src.at[peer]


class _SiblingSwapPart:
    def __init__(self, arrays):
        self.arrays, self.results = list(arrays), None

    def out_shape(self):
        return [_sds(a.shape, a.dtype) for a in self.arrays]

    def sems(self):
        n = len(self.arrays)
        return [pltpu.SemaphoreType.DMA((n,)), pltpu.SemaphoreType.DMA((n,))]

    def copies(self, ins, outs, sems):
        x, y, c = _position()
        both = [_remote(src, dst, sems[0], sems[1], i, (x, y, 1 - c)) for i, (src, dst) in enumerate(zip(ins, outs))]
        return [], both, both


def _split_by(parts, refs, count):
    out, off = [], 0
    for p in parts:
        out.append(refs[off:off + count(p)])
        off += count(p)
    return out


def _parts_refs(parts, in_refs, out_refs, sem_refs):
    return zip(parts, _split_by(parts, in_refs, lambda p: len(p.arrays)),
               _split_by(parts, out_refs, lambda p: len(p.arrays)), _split_by(parts, sem_refs, lambda p: len(p.sems())))


def _exchange_start(parts, in_refs, out_refs, sem_refs):
    for part, ins, outs, sems in _parts_refs(parts, in_refs, out_refs, sem_refs):
        local, sends, _ = part.copies(ins, outs, sems)
        for cp in local + sends:
            cp.start()


def _exchange_finish(parts, in_refs, out_refs, sem_refs):
    split = list(_parts_refs(parts, in_refs, out_refs, sem_refs))
    copies = [part.copies(ins, outs, sems) for part, ins, outs, sems in split]
    for _, _, recvs in copies:
        for cp in recvs:
            cp.wait_recv()
    second = [part.forwards(ins, outs, sems) for part, ins, outs, sems in split if hasattr(part, "forwards")]
    for sends, _ in second:
        for cp in sends:
            cp.start()
    for sends, recvs in second:
        for cp in recvs:
            cp.wait_recv()
        for cp in sends:
            cp.wait_send()
    for local, sends, _ in copies:
        for cp in sends:
            cp.wait_send()
        for cp in local:
            cp.wait()


def _exchange_operands(parts):
    return ([a for p in parts for a in p.arrays], [s for p in parts for s in p.out_shape()],
            [s for p in parts for s in p.sems()])


def _set_results(parts, res):
    for part, outs in zip(parts, _split_by(parts, list(res), lambda p: len(p.arrays))):
        part.results = list(outs)


def _exchange_now(name, parts):
    x_in, x_out, x_sem = _exchange_operands(parts)
    n = len(x_in)

    def body(*refs):
        _exchange_start(parts, refs[:n], refs[n:2 * n], refs[2 * n:])
        _exchange_finish(parts, refs[:n], refs[n:2 * n], refs[2 * n:])

    res = pl.pallas_call(body, name=name, in_specs=[_ANY] * n, out_specs=[_ANY] * n, out_shape=x_out,
                         scratch_shapes=x_sem)(*x_in)
    _set_results(parts, res)


_RIDERS = {}


def _call(body, *, name, grid, in_specs, out_specs, out_shape, scratch_shapes=(), compiler_params=None):
    make_parts = _RIDERS.pop(name, None)
    if make_parts is None:
        return pl.pallas_call(body, name=name, grid=grid, in_specs=in_specs, out_specs=out_specs, out_shape=out_shape,
                              scratch_shapes=scratch_shapes, compiler_params=compiler_params)
    parts = make_parts()
    x_in, x_out, x_sem = _exchange_operands(parts)
    n_out, n_scr, n_x = len(out_shape), len(scratch_shapes), len(x_in)

    def run(*args):
        n_in = len(args)

        def hosted(*refs):
            ins, xi = refs[:n_in], refs[n_in:n_in + n_x]
            outs, xo = refs[n_in + n_x:n_in + n_x + n_out], refs[n_in + n_x + n_out:n_in + 2 * n_x + n_out]
            scr, xs = refs[n_in + 2 * n_x + n_out:n_in + 2 * n_x + n_out + n_scr], refs[n_in + 2 * n_x + n_out + n_scr:]
            first = functools.reduce(jnp.logical_and, [pl.program_id(d) == 0 for d in range(len(grid))])
            last = functools.reduce(jnp.logical_and, [pl.program_id(d) == grid[d] - 1 for d in range(len(grid))])

            @pl.when(first)
            def _():
                _exchange_start(parts, xi, xo, xs)

            body(*ins, *outs, *scr)

            @pl.when(last)
            def _():
                _exchange_finish(parts, xi, xo, xs)

        res = pl.pallas_call(
            hosted, name=name, grid=grid, in_specs=list(in_specs) + [_ANY] * n_x,
            out_specs=list(out_specs) + [_ANY] * n_x, out_shape=list(out_shape) + x_out,
            scratch_shapes=list(scratch_shapes) + x_sem, compiler_params=_params(*["arbitrary"] * len(grid)),
        )(*args, *x_in)
        _set_results(parts, res[n_out:])
        return list(res[:n_out])

    return run


GROUPS = {"F1": ["ffn1_w_gate", "ffn1_w_up", "ffn1_w_down"], "MX": ["w_in", "s5_w_glu", "w_out"],
          "F2": ["ffn2_w_gate", "ffn2_w_up", "ffn2_w_down"]}
FIRST_GATHER = [(0, "ffn1_w_gate")]
GATHER_HOSTS = {
    "l0_ffn1_gate": [(0, "ffn1_w_up")],
    "l0_ffn1_up": [(0, "ffn1_w_down")],
    "l0_ffn1_down": [(0, "w_in"), (0, "s5_w_glu"), (0, "w_out")],
    "l0_mix_attn": [(0, "ffn2_w_up"), (0, "ffn2_w_down"), (1, "w_in")],
    "l0_mix_s5scan": [(0, "ffn2_w_gate")],
    "l0_wout": [(1, "s5_w_glu"), (1, "w_out")],
    "l0_ffn2_up": [(1, "ffn1_w_gate")],
    "l0_ffn2_down": [(1, "ffn1_w_up")],
    "l1_ffn1_up": [(1, "ffn1_w_down")],
    "l1_mix_attn": [(1, "ffn2_w_up"), (1, "ffn2_w_down")],
    "l1_mix_s5scan": [(1, "ffn2_w_gate")],
}
SCATTER_HOSTS = {
    "l1_ffn2_dact": [(1, "ffn2_w_down")],
    "l1_mix_attndq": [(1, "ffn2_w_up")],
    "l1_mix_attndkv": [(1, "w_out"), (1, "s5_w_glu"), (1, "ffn2_w_gate")],
    "l1_ffn1_dact": [(1, "ffn1_w_down")],
    "l1_ffn1_dwgu": [(1, "w_in")],
    "l0_ffn2_dact": [(1, "ffn1_w_up")],
    "l0_ffn2_dwgu": [(0, "ffn2_w_down")],
    "l0_mix_attndq": [(0, "w_out"), (0, "s5_w_glu"), (0, "ffn2_w_up")],
    "l0_mix_attndkv": [(0, "ffn2_w_gate"), (1, "ffn1_w_gate")],
    "l0_mix_dh1": [(0, "w_in")],
    "l0_ffn1_dact": [(0, "ffn1_w_down")],
    "l0_ffn1_dh": [(0, "ffn1_w_gate")],
}
LAST_SCATTER = [(0, "ffn1_w_up")]
SMALL_HOST = "l0_ffn1_dwd"
SMALL_PACK_ORDER = [n for n in SMALL_NAMES if n != "conv_w"] + ["conv_w"]
TAIL_HOST = "l0_ffn1_dwgu"
LATE_SCATTER_HOST = "l0_ffn1_dh"
LAST_HOST = "adamw_ffn2"


def _sharded_rows(name, a):
    return jnp.swapaxes(a, 1, 2) if name in COLUMN_SHARDED else a


def _unstack_layer(st):
    _, r, c = st.shape
    return st.reshape(N_CHIPS * r, c)


def _restack_layer(g):
    r, c = g.shape
    return g.reshape(N_CHIPS, r // N_CHIPS, c)


def _adamw_layer(name, layer, w, ga, gb, m, v, bufs):
    _, r, c = w.shape
    tr = _row_tile(r)

    def body(w_ref, ga_ref, gb_ref, m_ref, v_ref, *rest):
        g_out, d_out, m_out, v_out = rest[-4:]
        g = ga_ref[...] + gb_ref[...]
        d, mm, vv = _adamw_rows(w_ref[...], g, m_ref[...], v_ref[...])
        g_out[...] = g
        d_out[...] = d
        m_out[...] = mm
        v_out[...] = vv

    full = pl.BlockSpec((None, tr, c), lambda i: (layer, i, 0))
    flat = pl.BlockSpec((tr, c), lambda i: (i, 0))
    extra = {} if bufs is None else dict(input_output_aliases={5 + k: k for k in range(4)})
    return pl.pallas_call(
        body, name=name, grid=(r // tr,),
        in_specs=[full, flat, flat, full, full] + ([] if bufs is None else [_ANY] * 4),
        out_specs=[full] * 4, out_shape=[_sds(w.shape)] * 4, compiler_params=_params("parallel"), **extra,
    )(w, ga, gb, m, v, *([] if bufs is None else bufs))


def _adamw_both_layers(name, ws, ms, vs, gas, gbs):
    nw = len(ws)
    _, r, c = ws[0].shape
    tr = _row_tile(r, 64)

    def body(*refs):
        ins, outs = refs[:7 * nw], refs[7 * nw:]
        for k in range(nw):
            w_ref, m_ref, v_ref = ins[3 * k:3 * k + 3]
            g_refs = ins[3 * nw + 4 * k:3 * nw + 4 * k + 4]
            g_out, d_out, m_out, v_out = outs[4 * k:4 * k + 4]
            for layer in range(DEPTH):
                g = g_refs[layer][...] + g_refs[DEPTH + layer][...]
                d, mm, vv = _adamw_rows(w_ref[layer], g, m_ref[layer], v_ref[layer])
                g_out[layer] = g
                d_out[layer] = d
                m_out[layer] = mm
                v_out[layer] = vv

    both = pl.BlockSpec((DEPTH, tr, c), lambda i: (0, i, 0))
    flat = pl.BlockSpec((tr, c), lambda i: (i, 0))
    wmv = [t for k in range(nw) for t in (ws[k], ms[k], vs[k])]
    gs = [t for k in range(nw) for t in (*gas[k], *gbs[k])]
    res = _call(
        body, name=name, grid=(r // tr,), in_specs=[both] * (3 * nw) + [flat] * (4 * nw),
        out_specs=[both] * (4 * nw), out_shape=[_sds(ws[0].shape)] * (4 * nw), compiler_params=_params("parallel"),
    )(*wmv, *gs)
    return [res[4 * k:4 * k + 4] for k in range(nw)]


def _train_step(x, loss_target, w, m, v):
    ix, iy, _ = _position()
    chip = 2 * ix + iy
    shard = {n: (_permute_in_cols(w[n]) if n == "w_in" else _sharded_rows(n, w[n])).astype(bf16) for n in BIG_NAMES}

    gathered = {}

    def gather_parts(keys, extra=()):
        part = _ChipGatherHalvesPart([shard[n][layer] for layer, n in keys] + list(extra))
        gathered.update({key: (part, i) for i, key in enumerate(keys)})
        return [part]

    (first,) = gather_parts(FIRST_GATHER, extra=[w["conv_w"]])
    _exchange_now("gather_first", [first])
    for host, keys in GATHER_HOSTS.items():
        _RIDERS[host] = functools.partial(gather_parts, keys)

    def weight(layer, name):
        part, i = gathered[(layer, name)]
        return _unstack_layer(part.results[i])

    small = {n: w[n] for n in SMALL_NAMES}
    small["conv_w"] = first.results[-1].transpose(1, 2, 0, 3).reshape(DEPTH, CONV_WIDTH, D_A)

    grads_full, scattered = {}, {}

    def scatter_parts(keys):
        part = _ChipScatterPart([_restack_layer(grads_full[key]) for key in keys])
        scattered.update({key: (part, i) for i, key in enumerate(keys)})
        return [part]

    for host, keys in SCATTER_HOSTS.items():
        _RIDERS[host] = functools.partial(scatter_parts, keys)

    partial = {}

    def reduce_chips(keys):
        for layer, n in keys:
            part, i = scattered[(layer, n)]
            p = _sum_stack(f"sum_l{layer}_{n}", part.results[i])
            partial[(layer, n)] = _unpermute_in_cols(p) if n == "w_in" else p

    early = [key for host, keys in SCATTER_HOSTS.items() if host != LATE_SCATTER_HOST for key in keys]
    late = SCATTER_HOSTS[LATE_SCATTER_HOST]
    tail = {}

    def small_parts():
        tail["small"] = _ChipGatherPart([_pack([tail["gsmall"][n] for n in SMALL_PACK_ORDER])])
        return [tail["small"]]

    def tail_parts():
        reduce_chips(early)
        tail["small_sum"] = _sum_stack("sum_small", tail["small"].results[0])
        tail["swap"] = _SiblingSwapPart([partial[k] for k in early] + [tail["small_sum"]])
        return [tail["swap"]]

    _RIDERS[SMALL_HOST] = small_parts
    _RIDERS[TAIL_HOST] = tail_parts
    loss_local, gx = _local_step(x[0], loss_target[0], weight, small, grads_full.__setitem__,
                                 functools.partial(tail.__setitem__, "gsmall"))
    other = dict(zip(early, tail["swap"].results[:-1]))
    small_mine, small_other = tail["small_sum"], tail["swap"].results[-1]
    grads, deltas, new_m, new_v = {}, {}, {}, {}
    reduce_chips(late)
    last_parts = scatter_parts(LAST_SCATTER) + [_SiblingSwapPart([partial[k] for k in late])]
    _RIDERS[LAST_HOST] = lambda: last_parts
    ffn2 = GROUPS["F2"]
    res = _adamw_both_layers(
        LAST_HOST, *[[_sharded_rows(n, t[n]) for n in ffn2] for t in (w, m, v)],
        [[partial[(layer, n)] for layer in range(DEPTH)] for n in ffn2],
        [[other[(layer, n)] for layer in range(DEPTH)] for n in ffn2])
    for n, bufs in zip(ffn2, res):
        grads[n], deltas[n], new_m[n], new_v[n] = (_sharded_rows(n, t) for t in bufs)
    other.update(zip(late, last_parts[1].results))
    reduce_chips(LAST_SCATTER)
    swap_last = _SiblingSwapPart([partial[k] for k in LAST_SCATTER])
    _exchange_now("swap_last", [swap_last])
    other.update(zip(LAST_SCATTER, swap_last.results))

    for n in BIG_NAMES:
        if n in ffn2:
            continue
        bufs = None
        wr, mr, vr = (_sharded_rows(n, t) for t in (w[n], m[n], v[n]))
        for layer in range(DEPTH):
            bufs = _adamw_layer(f"adamw_l{layer}_{n}", layer, wr, partial[(layer, n)], other[(layer, n)], mr, vr, bufs)
        grads[n], deltas[n], new_m[n], new_v[n] = (_sharded_rows(n, t) for t in bufs)
    packed = SMALL_PACK_ORDER[:-1]
    shapes = [w[n].shape for n in packed]
    res = _adamw("adamw_small", _pack([w[n] for n in packed]), small_mine, small_other,
                 _pack([m[n] for n in packed]), _pack([v[n] for n in packed]))
    for dst, buf in zip((grads, deltas, new_m, new_v), res):
        dst.update(zip(packed, _unpack(buf, shapes)))
    cw = D_A // N_CHIPS
    conv_shape = (DEPTH, CONV_WIDTH, D_A)
    offset = sum(math.prod(s_) for s_ in shapes)

    def conv_grad(buf):
        full = buf.reshape(-1)[offset:offset + math.prod(conv_shape)].reshape(conv_shape)
        return lax.dynamic_slice_in_dim(full, chip * cw, cw, axis=2).reshape(DEPTH * CONV_WIDTH, cw)

    rows = lambda t: t.reshape(DEPTH * CONV_WIDTH, cw)
    res = _adamw("adamw_conv_w", rows(w["conv_w"]), conv_grad(small_mine), conv_grad(small_other),
                 rows(m["conv_w"]), rows(v["conv_w"]))
    for dst, buf in zip((grads, deltas, new_m, new_v), res):
        dst["conv_w"] = buf.reshape(w["conv_w"].shape)

    loss = lax.psum(loss_local, ("x", "y", "c"))
    return (loss, gx[None], *[grads[n] for n in WEIGHT_NAMES], *[deltas[n] for n in WEIGHT_NAMES],
            *[new_m[n] for n in WEIGHT_NAMES], *[new_v[n] for n in WEIGHT_NAMES])


def kernel(x, ffn1_w_gate, ffn1_w_up, ffn1_w_down, ln1_g, ln1_b, w_in, conv_w, conv_b, rg_w_a, rg_b_a, rg_w_x, rg_b_x, rg_lambda, fox_b_f, s5_a_re, s5_a_im, s5_log_dt, s5_b_re, s5_b_im, s5_c_re, s5_c_im, s5_d, s5_w_glu, mix_norm_g, w_out, ln2_g, ln2_b, ffn2_w_gate, ffn2_w_up, ffn2_w_down, ln3_g, ln3_b, loss_target, m_ffn1_w_gate, m_ffn1_w_up, m_ffn1_w_down, m_ln1_g, m_ln1_b, m_w_in, m_conv_w, m_conv_b, m_rg_w_a, m_rg_b_a, m_rg_w_x, m_rg_b_x, m_rg_lambda, m_fox_b_f, m_s5_a_re, m_s5_a_im, m_s5_log_dt, m_s5_b_re, m_s5_b_im, m_s5_c_re, m_s5_c_im, m_s5_d, m_s5_w_glu, m_mix_norm_g, m_w_out, m_ln2_g, m_ln2_b, m_ffn2_w_gate, m_ffn2_w_up, m_ffn2_w_down, m_ln3_g, m_ln3_b, v_ffn1_w_gate, v_ffn1_w_up, v_ffn1_w_down, v_ln1_g, v_ln1_b, v_w_in, v_conv_w, v_conv_b, v_rg_w_a, v_rg_b_a, v_rg_w_x, v_rg_b_x, v_rg_lambda, v_fox_b_f, v_s5_a_re, v_s5_a_im, v_s5_log_dt, v_s5_b_re, v_s5_b_im, v_s5_c_re, v_s5_c_im, v_s5_d, v_s5_w_glu, v_mix_norm_g, v_w_out, v_ln2_g, v_ln2_b, v_ffn2_w_gate, v_ffn2_w_up, v_ffn2_w_down, v_ln3_g, v_ln3_b):
    args = dict(locals())
    w = {n: args[n] for n in WEIGHT_NAMES}
    m = {n: args["m_" + n] for n in WEIGHT_NAMES}
    v = {n: args["v_" + n] for n in WEIGHT_NAMES}
    return _train_step(x, loss_target, w, m, v)
```

```python
import functools
import math

import jax
import jax.numpy as jnp
from jax import lax
from jax.experimental import pallas as pl
from jax.experimental.pallas import tpu as pltpu

f32 = jnp.float32
bf16 = jnp.bfloat16

D_MODEL = 1024
D_FF = 2816
D_A = 384
D_B = 384
D_C = 256
N_HEADS = 6
HEAD_DIM = 64
S5_GROUPS = 16
S5_STATE = 64
S5_LANES = S5_GROUPS * S5_STATE
F_OFF = 5 * D_A
CU_OFF = F_OFF + 128
N_IN_P = CU_OFF + D_C
CONV_WIDTH = 4
DEPTH = 2
ALPHA = (2 * DEPTH) ** 0.25
LN_EPS = 1e-5
RMS_EPS = 1e-6
RG_C = 8.0
ATT_SCALE = HEAD_DIM ** -0.5
ADAM_LR, ADAM_B1, ADAM_B2, ADAM_EPS, ADAM_WD, ADAM_STEP = 0.001, 0.9, 0.999, 1e-08, 0.01, 10

ROW_TILE = 512
N_CHIPS = 4
MESH = pl.DeviceIdType.MESH

_DN = {
    "nn": (((1,), (0,)), ((), ())),
    "nt": (((1,), (1,)), ((), ())),
    "tn": (((0,), (0,)), ((), ())),
}


def _sds(shape, dtype=f32):
    return jax.ShapeDtypeStruct(shape, dtype)


def _tile(n, target):
    best = None
    for t in range(128, min(n, target) + 1, 128):
        if n % t == 0:
            best = t
    return best or n


def _row_tile(rows, target=256):
    best = None
    for t in range(16, min(rows, target) + 1, 16):
        if rows % t == 0:
            best = t
    return best or rows


def _params(*sem):
    return pltpu.CompilerParams(dimension_semantics=sem)


class _Slabs:
    def __init__(self, x):
        self.x = x


class _KPart:
    def __init__(self, x, j):
        self.x, self.j = x, j


FF_SLAB = D_FF // N_CHIPS
FFN_ROWS = 1024

def _mm(name, mode, dims, tiles, a_list, b_list, pairs, n_acc, epilogue, outs, extras=(), vecs=(), split_cols=False):
    m, n, k = dims
    tm, tn, tk = tiles
    nk = k // tk
    na, nb, ne, nv, no = len(a_list), len(b_list), len(extras), len(vecs), len(outs)

    def body(*refs):
        a_refs = refs[:na]
        b_refs = refs[na:na + nb]
        e_refs = refs[na + nb:na + nb + ne]
        v_refs = refs[na + nb + ne:na + nb + ne + nv]
        o_refs = refs[na + nb + ne + nv:na + nb + ne + nv + no]
        acc_refs = refs[na + nb + ne + nv + no:]
        a_vals = [r[...].astype(bf16) for r in a_refs]
        b_vals = [r[...].astype(bf16) for r in b_refs]
        products = [(ci, lax.dot_general(a_vals[ai], b_vals[bi], _DN[mode], preferred_element_type=f32))
                    for ai, bi, ci in pairs]

        def finish(accs):
            res = epilogue(accs, [e[...] for e in e_refs], [v[...] for v in v_refs])
            for o, r in zip(o_refs, res):
                o[...] = r.astype(o.dtype)

        if nk == 1:
            accs = [None] * n_acc
            for ci, prod in products:
                accs[ci] = prod if accs[ci] is None else accs[ci] + prod
            finish(accs)
            return
        kk = pl.program_id(2)

        @pl.when(kk == 0)
        def _():
            for acc in acc_refs:
                acc[...] = jnp.zeros_like(acc)

        for ci, prod in products:
            acc_refs[ci][...] += prod

        @pl.when(kk == nk - 1)
        def _():
            finish([acc[...] for acc in acc_refs])

    def a_spec(a):
        if isinstance(a, _KPart):
            return pl.BlockSpec((None, tm, tk), lambda i, j, kk, part=a.j: (part, i, 0))
        if isinstance(a, _Slabs):
            if mode == "tn":
                return pl.BlockSpec((None, tk, tm), lambda i, j, kk: (i, kk, 0))
            return pl.BlockSpec((None, tm, tk), lambda i, j, kk: (kk, i, 0))
        if mode == "tn":
            return pl.BlockSpec((tk, tm), lambda i, j, kk: (kk, i))
        return pl.BlockSpec((tm, tk), lambda i, j, kk: (i, kk))

    def b_spec(b):
        if isinstance(b, _KPart):
            return pl.BlockSpec((tk, tn), lambda i, j, kk, part=b.j: (part, j))
        if isinstance(b, _Slabs):
            if mode == "nt":
                return pl.BlockSpec((None, tn, tk), lambda i, j, kk: (kk, j, 0))
            return pl.BlockSpec((None, tk, tn), lambda i, j, kk: (j, kk, 0))
        if mode == "nt":
            return pl.BlockSpec((tn, tk), lambda i, j, kk: (j, kk))
        return pl.BlockSpec((tk, tn), lambda i, j, kk: (kk, j))

    o_spec = pl.BlockSpec((tm, tn), lambda i, j, kk: (i, j))
    o_slab_spec = pl.BlockSpec((None, tm, tn), lambda i, j, kk: (j, i, 0))
    v_spec = pl.BlockSpec((1, tn), lambda i, j, kk: (0, j))
    if split_cols:
        out_specs = [o_slab_spec] * no
        out_shape = [_sds((n // tn, m, tn), dt) for dt in outs]
    else:
        out_specs = [o_spec] * no
        out_shape = [_sds((m, n), dt) for dt in outs]
    raw = lambda t: t.x if isinstance(t, (_Slabs, _KPart)) else t
    res = _call(
        body,
        name=name,
        grid=(m // tm, n // tn, nk),
        in_specs=([a_spec(a) for a in a_list] + [b_spec(b) for b in b_list]
                  + [o_slab_spec if isinstance(e, _Slabs) else o_spec for e in extras] + [v_spec] * nv),
        out_specs=out_specs,
        out_shape=out_shape,
        scratch_shapes=[pltpu.VMEM((tm, tn), f32)] * (n_acc if nk > 1 else 0),
        compiler_params=_params("parallel", "parallel", "arbitrary"),
    )(*map(raw, a_list), *map(raw, b_list), *map(raw, extras), *vecs)
    return res


def _sigmoid(x):
    return 0.5 * (jnp.tanh(0.5 * x) + 1.0)


def _layer_norm_rows(r, gamma, beta):
    mu = jnp.mean(r, axis=-1, keepdims=True)
    xc = r - mu
    var = jnp.mean(xc * xc, axis=-1, keepdims=True)
    return xc * lax.rsqrt(var + LN_EPS) * gamma + beta


def _mm_plain(name, mode, a, b, dims, scale=1.0, out_dtype=f32, add=None, add_coef=1.0, tiles=None):
    m, n, k = dims
    tiles = tiles or (_tile(m, 512), _tile(n, 1024), _tile(k, 1024))

    def epilogue(accs, extras, vecs):
        r = accs[0] if scale == 1.0 else accs[0] * scale
        if extras:
            r = r + add_coef * extras[0]
        return [r]

    return _mm(name, mode, dims, tiles, [a], [b], [(0, 0, 0)], 1, epilogue, [out_dtype],
               extras=[] if add is None else [add])[0]


def _ffn_up(name, h, wg, wu):
    s = h.shape[0]

    def epilogue(accs, extras, vecs):
        g, u = accs
        return [g, u, g * _sigmoid(g) * u]

    return _mm(name, "nt", (s, D_FF, D_MODEL), (_tile(s, FFN_ROWS), FF_SLAB, D_MODEL), [h], [wg, wu],
               [(0, 0, 0), (0, 1, 1)], 2, epilogue, [bf16, bf16, bf16], split_cols=True)


def _ffn_gate(name, h, wg):
    s = h.shape[0]
    return _mm(name, "nt", (s, D_FF, D_MODEL), (_tile(s, FFN_ROWS), FF_SLAB, D_MODEL), [h], [wg], [(0, 0, 0)], 1,
               lambda accs, extras, vecs: [accs[0]], [bf16], split_cols=True)[0]


def _ffn_up_given_gate(name, h, wu, g):
    s = h.shape[0]

    def epilogue(accs, extras, vecs):
        gg = extras[0].astype(f32)
        return [accs[0], gg * _sigmoid(gg) * accs[0]]

    return _mm(name, "nt", (s, D_FF, D_MODEL), (_tile(s, FFN_ROWS), FF_SLAB, D_MODEL), [h], [wu], [(0, 0, 0)], 1,
               epilogue, [bf16, bf16], extras=[_Slabs(g)], split_cols=True)


def _mm_ln(name, a, w, resid, gamma, beta, scale, k_slabs=False):
    def epilogue(accs, extras, vecs):
        r = ALPHA * extras[0] + scale * accs[0]
        return [r, _layer_norm_rows(r, vecs[0], vecs[1])]

    if k_slabs:
        n_slabs, s, slab = a.shape
        return _mm(name, "nn", (s, D_MODEL, slab), (_tile(s, 512), D_MODEL, slab),
                   [_KPart(a, j) for j in range(n_slabs)], [_KPart(w, j) for j in range(n_slabs)],
                   [(j, j, 0) for j in range(n_slabs)], 1, epilogue, [f32, f32], extras=[resid], vecs=[gamma, beta])
    s, k = a.shape
    return _mm(name, "nn", (s, D_MODEL, k), (_tile(s, FFN_ROWS), D_MODEL, _tile(k, 1024)),
               [a], [w], [(0, 0, 0)], 1, epilogue, [f32, f32], extras=[resid], vecs=[gamma, beta])


def _ffn_dact(name, dr, wd, g, u):
    s = dr.shape[0]

    def epilogue(accs, extras, vecs):
        da = 0.5 * accs[0]
        gg, uu = extras[0].astype(f32), extras[1].astype(f32)
        sg = _sigmoid(gg)
        return [da * uu * (sg * (1.0 + gg * (1.0 - sg))), da * (gg * sg)]

    return _mm(name, "nt", (s, D_FF, D_MODEL), (_tile(s, FFN_ROWS), FF_SLAB, D_MODEL), [dr], [wd],
               [(0, 0, 0)], 1, epilogue, [bf16, bf16], extras=[_Slabs(g), _Slabs(u)], split_cols=True)


def _mm2(name, mode, dims, a0, b0, a1, b1, add=None, add_coef=1.0, separate=False, tiles=None, out_dtype=f32,
         split_cols=False):
    m, n, k = dims
    tiles = tiles or (_tile(m, 512), _tile(n, 1024), _tile(k, 1024))

    def epilogue(accs, extras, vecs):
        if separate:
            return list(accs)
        r = accs[0]
        if extras:
            r = r + add_coef * extras[0]
        return [r]

    a_list = [a0] if a1 is None else [a0, a1]
    b_list = [b0] if b1 is None else [b0, b1]
    pairs = [(0, 0, 0), (len(a_list) - 1, len(b_list) - 1, 1 if separate else 0)]
    return _mm(name, mode, dims, tiles, a_list, b_list, pairs, 2 if separate else 1, epilogue,
               [out_dtype, out_dtype] if separate else [out_dtype], extras=[] if add is None else [add],
               split_cols=split_cols)


def _row_call(name, body, s, ins, params, outs, accs):
    tm = ROW_TILE
    ins = [a if isinstance(a, tuple) else (a, a.shape[1], 0) for a in ins]
    in_specs = [pl.BlockSpec((tm, width), lambda i, cb=cb: (i, cb)) for _, width, cb in ins]
    ins = [a for a, _, _ in ins]
    in_specs += [pl.BlockSpec(p.shape, lambda i, nd=p.ndim: (0,) * nd) for p in params]
    out_specs = [pl.BlockSpec((tm, o.shape[1]), lambda i: (i, 0)) for o in outs]
    out_specs += [pl.BlockSpec(a.shape, lambda i, nd=len(a.shape): (0,) * nd) for a in accs]
    return pl.pallas_call(
        body,
        name=name,
        grid=(s // tm,),
        in_specs=in_specs,
        out_specs=out_specs,
        out_shape=list(outs) + list(accs),
        compiler_params=_params("arbitrary"),
    )(*ins, *params)


def _zero_at_first(refs):
    @pl.when(pl.program_id(0) == 0)
    def _():
        for r in refs:
            r[...] = jnp.zeros_like(r)


def _ln_bwd(name, r, dh, gamma):
    s = r.shape[0]

    def body(r_ref, dh_ref, g_ref, dr_ref, dg_ref, db_ref):
        _zero_at_first([dg_ref, db_ref])
        rr = r_ref[...]
        dy = dh_ref[...]
        mu = jnp.mean(rr, axis=-1, keepdims=True)
        xc = rr - mu
        rstd = lax.rsqrt(jnp.mean(xc * xc, axis=-1, keepdims=True) + LN_EPS)
        xhat = xc * rstd
        dxh = dy * g_ref[...]
        dr_ref[...] = rstd * (dxh - jnp.mean(dxh, axis=-1, keepdims=True)
                              - xhat * jnp.mean(dxh * xhat, axis=-1, keepdims=True))
        dg_ref[...] += jnp.sum(dy * xhat, axis=0, keepdims=True)
        db_ref[...] += jnp.sum(dy, axis=0, keepdims=True)

    return _row_call(name, body, s, [r, dh], [gamma], [_sds((s, D_MODEL))], [_sds((1, D_MODEL)), _sds((1, D_MODEL))])


def _loss_head(name, y, target):
    s = y.shape[0]

    def body(y_ref, t_ref, dy_ref, l_ref):
        _zero_at_first([l_ref])
        e = y_ref[...] - t_ref[...]
        dy_ref[...] = e / D_MODEL
        l_ref[...] += 0.5 * jnp.sum(jnp.mean(e * e, axis=-1, keepdims=True), axis=0, keepdims=True)

    return _row_call(name, body, s, [y, target], [], [_sds((s, D_MODEL))], [_sds((1, 128))])


def _expm1(x):
    series = x * (1.0 + x / 2.0 * (1.0 + x / 3.0 * (1.0 + x / 4.0 * (1.0 + x / 5.0 * (1.0 + x / 6.0 * (1.0 + x / 7.0))))))
    return jnp.where(jnp.abs(x) < 0.25, series, jnp.exp(x) - 1.0)


def _gates_fn(xa, wa, wx, ba, bx, lam, tap_a, tap_x):
    xb = xa.astype(bf16)
    r = jax.nn.sigmoid(jnp.dot(xb, wa, preferred_element_type=f32) + ba + tap_a)
    i = jax.nn.sigmoid(jnp.dot(xb, wx, preferred_element_type=f32) + bx + tap_x)
    log_a = -RG_C * r * jax.nn.softplus(-lam)
    a = jnp.exp(log_a)
    gated = jnp.sqrt(-_expm1(2.0 * log_a)) * (i * xa)
    return a, gated


def _rg_gates(name, xa, wa, wx, ba, bx, lam):
    s = xa.shape[0]

    def body(xa_ref, wa_ref, wx_ref, ba_ref, bx_ref, lam_ref, a_ref, g_ref):
        a, g = _gates_fn(xa_ref[...], wa_ref[...], wx_ref[...], ba_ref[...], bx_ref[...], lam_ref[...], 0.0, 0.0)
        a_ref[...] = a
        g_ref[...] = g

    return _row_call(name, body, s, [xa], [wa, wx, ba, bx, lam], [_sds((s, D_A)), _sds((s, D_A))], [])


def _rg_gates_bwd(name, xa, ga, h_prev, wa, wx, ba, bx, lam):
    s = xa.shape[0]

    def body(xa_ref, ga_ref, hp_ref, wa_ref, wx_ref, ba_ref, bx_ref, lam_ref,
             dxa_ref, dwa_ref, dwx_ref, dba_ref, dbx_ref, dlam_ref):
        _zero_at_first([dwa_ref, dwx_ref, dba_ref, dbx_ref, dlam_ref])
        xa_v = xa_ref[...]
        zero = jnp.zeros((xa_v.shape[0], D_A), f32)
        fn = lambda x, ba_, bx_, lam_, ta, tx: _gates_fn(x, wa_ref[...], wx_ref[...], ba_, bx_, lam_, ta, tx)
        _, vjp = jax.vjp(fn, xa_v, ba_ref[...], bx_ref[...], lam_ref[...], zero, zero)
        gav = ga_ref[...]
        dxa, dba, dbx, dlam, dta, dtx = vjp((gav * hp_ref[...], gav))
        dxa_ref[...] = dxa
        xb = xa_v.astype(bf16)
        dwa_ref[...] += lax.dot_general(xb, dta.astype(bf16), _DN["tn"], preferred_element_type=f32)
        dwx_ref[...] += lax.dot_general(xb, dtx.astype(bf16), _DN["tn"], preferred_element_type=f32)
        dba_ref[...] += dba
        dbx_ref[...] += dbx
        dlam_ref[...] += dlam

    return _row_call(name, body, s, [xa, ga, h_prev], [wa, wx, ba, bx, lam], [_sds((s, D_A))],
                     [_sds((D_A, D_A)), _sds((D_A, D_A)), _sds((1, D_A)), _sds((1, D_A)), _sds((1, D_A))])


def _rms(v, g):
    return v * lax.rsqrt(jnp.mean(v * v, axis=-1, keepdims=True) + RMS_EPS) * g


def _mix_out_fn(ag, ha, ob, hre, him, cu, d, gn, tap_y, tap_gl, wcr, wci, wglu):
    out_a = jax.nn.gelu(ag) * ha
    y = (jnp.dot(hre.astype(bf16), wcr, preferred_element_type=f32)
         + jnp.dot(him.astype(bf16), wci, preferred_element_type=f32) + d * cu + tap_y)
    y2 = jax.nn.gelu(y)
    gl = jnp.dot(y2.astype(bf16), wglu, preferred_element_type=f32) + tap_gl
    out_c = y2 * jax.nn.sigmoid(gl)
    o = jnp.concatenate([_rms(out_a, gn[:, :D_A]), _rms(ob, gn[:, D_A:D_A + D_B]), _rms(out_c, gn[:, D_A + D_B:])],
                        axis=-1)
    return o, y2


def _mix_out(name, ag, ha, ob, hre, him, cu, d, gn, wcr, wci, wglu):
    s = ha.shape[0]

    def body(ag_ref, ha_ref, ob_ref, hre_ref, him_ref, cu_ref, d_ref, gn_ref, wcr_ref, wci_ref, wglu_ref, o_ref):
        o, _ = _mix_out_fn(ag_ref[...], ha_ref[...], ob_ref[...], hre_ref[...], him_ref[...], cu_ref[...], d_ref[...],
                           gn_ref[...], 0.0, 0.0, wcr_ref[...], wci_ref[...], wglu_ref[...])
        o_ref[...] = o.astype(o_ref.dtype)

    return _row_call(name, body, s, [ag, ha, ob, hre, him, cu], [d, gn, wcr, wci, wglu], [_sds((s, D_MODEL), bf16)], [])[0]


def _mix_out_bwd(name, do, ag, ha, ob, hre, him, cu, d, gn, wcr, wci, wglu):
    s = ha.shape[0]

    def body(do_ref, ag_ref, ha_ref, ob_ref, hre_ref, him_ref, cu_ref, d_ref, gn_ref, wcr_ref, wci_ref, wglu_ref,
             dag_ref, dha_ref, dob_ref, dhre_ref, dhim_ref, dcu_ref, dwcr_ref, dwci_ref, dwglu_ref, dd_ref, dgn_ref):
        _zero_at_first([dwcr_ref, dwci_ref, dwglu_ref, dd_ref, dgn_ref])
        tm = ag_ref.shape[0]
        zero = jnp.zeros((tm, D_C), f32)
        hre_v, him_v = hre_ref[...], him_ref[...]
        fn = lambda *a: _mix_out_fn(*a, wcr_ref[...], wci_ref[...], wglu_ref[...])
        _, vjp, y2 = jax.vjp(fn, ag_ref[...], ha_ref[...], ob_ref[...], hre_v, him_v, cu_ref[...], d_ref[...],
                             gn_ref[...], zero, zero, has_aux=True)
        dag, dha, dob, dhre, dhim, dcu, dd, dgn, dy, dgl = vjp(do_ref[...])
        dag_ref[...] = dag
        dha_ref[...] = dha
        dob_ref[...] = dob
        dhre_ref[...] = dhre
        dhim_ref[...] = dhim
        dcu_ref[...] = dcu
        dyb = dy.astype(bf16)
        dwcr_ref[...] += lax.dot_general(hre_v.astype(bf16), dyb, _DN["tn"], preferred_element_type=f32)
        dwci_ref[...] += lax.dot_general(him_v.astype(bf16), dyb, _DN["tn"], preferred_element_type=f32)
        dwglu_ref[...] += lax.dot_general(y2.astype(bf16), dgl.astype(bf16), _DN["tn"], preferred_element_type=f32)
        dd_ref[...] += dd
        dgn_ref[...] += dgn

    outs = [_sds((s, D_A)), _sds((s, D_A)), _sds((s, D_B)), _sds((s, S5_LANES)), _sds((s, S5_LANES)), _sds((s, D_C))]
    accs = [_sds((S5_LANES, D_C)), _sds((S5_LANES, D_C)), _sds((D_C, D_C)), _sds((1, D_C)), _sds((1, D_MODEL))]
    return _row_call(name, body, s, [do, ag, ha, ob, hre, him, cu], [d, gn, wcr, wci, wglu], outs, accs)


def _log_f(name, f, bf):
    s = f[0].shape[0]

    def body(f_ref, b_ref, o_ref):
        o_ref[...] = jax.nn.log_sigmoid(f_ref[...] + b_ref[...])

    return _row_call(name, body, s, [f], [bf], [_sds((s, 128))], [])[0]


def _log_f_bwd(name, dlf, f, bf):
    s = dlf.shape[0]

    def body(dl_ref, f_ref, b_ref, df_ref, db_ref):
        _zero_at_first([db_ref])
        df = dl_ref[...] * jax.nn.sigmoid(-(f_ref[...] + b_ref[...]))
        df_ref[...] = df
        db_ref[...] += jnp.sum(df, axis=0, keepdims=True)

    return _row_call(name, body, s, [dlf, f], [bf], [_sds((s, 128))], [_sds((1, 128))])


def _s5_decay_grad(name, h_re, h_im, g_re, g_im):
    s = g_re.shape[0]
    tm = ROW_TILE

    def body(hr_ref, hi_ref, hhr_ref, hhi_ref, gr_ref, gi_ref, dr_ref, di_ref):
        i = pl.program_id(0)
        _zero_at_first([dr_ref, di_ref])

        def previous(h_ref, halo_ref):
            halo = jnp.where(i == 0, 0.0, halo_ref[...])
            return pltpu.roll(jnp.concatenate([halo, h_ref[...]], axis=0), 1, 0)[8:, :]

        hr, hi, gr, gi = previous(hr_ref, hhr_ref), previous(hi_ref, hhi_ref), gr_ref[...], gi_ref[...]
        dr_ref[...] += jnp.sum(hr * gr + hi * gi, axis=0, keepdims=True)
        di_ref[...] += jnp.sum(hr * gi - hi * gr, axis=0, keepdims=True)

    rows = pl.BlockSpec((tm, S5_LANES), lambda i: (i, 0))
    halo = pl.BlockSpec((8, S5_LANES), lambda i: (jnp.maximum(i * (tm // 8) - 1, 0), 0))
    acc = pl.BlockSpec((1, S5_LANES), lambda i: (0, 0))
    return pl.pallas_call(
        body,
        name=name,
        grid=(s // tm,),
        in_specs=[rows, rows, halo, halo, rows, rows],
        out_specs=[acc, acc],
        out_shape=[_sds((1, S5_LANES)), _sds((1, S5_LANES))],
        compiler_params=_params("arbitrary"),
    )(h_re, h_im, h_re, h_im, g_re, g_im)


def _conv_fwd(name, ax, w, b):
    s = ax.shape[0]
    tm = ROW_TILE

    def body(x_ref, halo_ref, w_ref, b_ref, o_ref):
        i = pl.program_id(0)
        x = x_ref[...]
        halo = jnp.where(i == 0, 0.0, halo_ref[...])
        ext = jnp.concatenate([halo, x], axis=0)
        acc = b_ref[...] + w_ref[3:4, :] * x
        for k in range(CONV_WIDTH - 1):
            acc = acc + w_ref[k:k + 1, :] * pltpu.roll(ext, CONV_WIDTH - 1 - k, 0)[8:, :]
        o_ref[...] = acc

    return pl.pallas_call(
        body,
        name=name,
        grid=(s // tm,),
        in_specs=[pl.BlockSpec((tm, D_A), lambda i: (i, 0)),
                  pl.BlockSpec((8, D_A), lambda i: (jnp.maximum(i * (tm // 8) - 1, 0), 0)),
                  pl.BlockSpec((CONV_WIDTH, D_A), lambda i: (0, 0)),
                  pl.BlockSpec((1, D_A), lambda i: (0, 0))],
        out_specs=pl.BlockSpec((tm, D_A), lambda i: (i, 0)),
        out_shape=_sds((s, D_A)),
        compiler_params=_params("arbitrary"),
    )(ax, ax, w, b)


def _conv_bwd(name, dxa, ax, w):
    s = ax.shape[0]
    tm = ROW_TILE
    nblk = s // tm

    def body(dx_ref, dnext_ref, x_ref, halo_ref, w_ref, dax_ref, dw_ref):
        i = pl.program_id(0)
        _zero_at_first([dw_ref])
        dx = dx_ref[...]
        dnext = jnp.where(i == nblk - 1, 0.0, dnext_ref[...])
        dext = jnp.concatenate([dx, dnext], axis=0)
        x = x_ref[...]
        halo = jnp.where(i == 0, 0.0, halo_ref[...])
        ext = jnp.concatenate([halo, x], axis=0)
        acc = w_ref[3:4, :] * dx
        dw_ref[3:4, :] += jnp.sum(dx * x, axis=0, keepdims=True)
        for k in range(CONV_WIDTH - 1):
            sh = CONV_WIDTH - 1 - k
            acc = acc + w_ref[k:k + 1, :] * pltpu.roll(dext, tm + 8 - sh, 0)[:tm, :]
            dw_ref[k:k + 1, :] += jnp.sum(dx * pltpu.roll(ext, sh, 0)[8:, :], axis=0, keepdims=True)
        dw_ref[4:5, :] += jnp.sum(dx, axis=0, keepdims=True)
        dax_ref[...] = acc

    return pl.pallas_call(
        body,
        name=name,
        grid=(nblk,),
        in_specs=[pl.BlockSpec((tm, D_A), lambda i: (i, 0)),
                  pl.BlockSpec((8, D_A), lambda i: (jnp.minimum((i + 1) * (tm // 8), s // 8 - 1), 0)),
                  pl.BlockSpec((tm, D_A), lambda i: (i, 0)),
                  pl.BlockSpec((8, D_A), lambda i: (jnp.maximum(i * (tm // 8) - 1, 0), 0)),
                  pl.BlockSpec((CONV_WIDTH, D_A), lambda i: (0, 0))],
        out_specs=[pl.BlockSpec((tm, D_A), lambda i: (i, 0)), pl.BlockSpec((8, D_A), lambda i: (0, 0))],
        out_shape=[_sds((s, D_A)), _sds((8, D_A))],
        compiler_params=_params("arbitrary"),
    )(dxa, dxa, ax, ax, w)


SCAN_ROWS = 512


def _row_in_tile(shape):
    return lax.broadcasted_iota(jnp.int32, shape, 0) % 8


def _lin_scan(name, a, b, reverse):
    s, c = a.shape
    t = min(SCAN_ROWS, s)
    nb = s // t

    def body(a_ref, b_ref, h_ref, p_ref, carry_ref):
        @pl.when(pl.program_id(0) == 0)
        def _():
            carry_ref[...] = jnp.zeros_like(carry_ref)

        row = _row_in_tile((t, c))
        p = a_ref[...]
        h = b_ref[...]
        for d in (1, 2, 4):
            keep = (row < 8 - d) if reverse else (row >= d)
            shift = (t - d) if reverse else d
            h = h + jnp.where(keep, p * pltpu.roll(h, shift, 0), 0.0)
            p = jnp.where(keep, p * pltpu.roll(p, shift, 0), p)
        h_ref[...] = h
        p_ref[...] = p
        edge = 0 if reverse else 7

        def tile(k, carry):
            kk = (t // 8 - 1 - k) if reverse else k
            r0 = pl.multiple_of(kk * 8, 8)
            hh = h_ref[pl.ds(r0, 8), :] + p_ref[pl.ds(r0, 8), :] * carry
            h_ref[pl.ds(r0, 8), :] = hh
            return jnp.broadcast_to(hh[edge:edge + 1, :], (8, c))

        carry_ref[...] = lax.fori_loop(0, t // 8, tile, carry_ref[...])

    spec = pl.BlockSpec((t, c), (lambda i: (nb - 1 - i, 0)) if reverse else (lambda i: (i, 0)))
    (out,) = _call(
        body,
        name=name,
        grid=(nb,),
        in_specs=[spec, spec],
        out_specs=[spec],
        out_shape=[_sds((s, c))],
        scratch_shapes=[pltpu.VMEM((t, c), f32), pltpu.VMEM((8, c), f32)],
        compiler_params=_params("arbitrary"),
    )(a, b)
    return out


def _s5_scan(name, b_re, b_im, a_re, a_im, reverse):
    s, c = b_re.shape
    t = min(SCAN_ROWS, s)
    nb = s // t

    def body(br_ref, bi_ref, ar_ref, ai_ref, hr_ref, hi_ref, cr_ref, ci_ref):
        @pl.when(pl.program_id(0) == 0)
        def _():
            cr_ref[...] = jnp.zeros_like(cr_ref)
            ci_ref[...] = jnp.zeros_like(ci_ref)

        ar1, ai1 = ar_ref[...], ai_ref[...]
        pows = [(ar1, ai1)]
        for _ in range(7):
            pr, pi = pows[-1]
            pows.append((pr * ar1 - pi * ai1, pr * ai1 + pi * ar1))
        row8 = lax.broadcasted_iota(jnp.int32, (8, c), 0)
        wr = jnp.zeros((8, c), f32)
        wi = jnp.zeros((8, c), f32)
        for r in range(8):
            pr, pi = pows[(7 - r) if reverse else r]
            wr = jnp.where(row8 == r, pr, wr)
            wi = jnp.where(row8 == r, pi, wi)
        row = _row_in_tile((t, c))
        hr = br_ref[...]
        hi = bi_ref[...]
        for d in (1, 2, 4):
            keep = (row < 8 - d) if reverse else (row >= d)
            shift = (t - d) if reverse else d
            pr, pi = pows[d - 1]
            cr = jnp.where(keep, pr, 0.0)
            ci = jnp.where(keep, pi, 0.0)
            sr = pltpu.roll(hr, shift, 0)
            si = pltpu.roll(hi, shift, 0)
            hr, hi = hr + cr * sr - ci * si, hi + cr * si + ci * sr
        hr_ref[...] = hr
        hi_ref[...] = hi
        edge = 0 if reverse else 7

        def tile(k, carry):
            car_r, car_i = carry
            kk = (t // 8 - 1 - k) if reverse else k
            r0 = pl.multiple_of(kk * 8, 8)
            xr = hr_ref[pl.ds(r0, 8), :] + wr * car_r - wi * car_i
            xi = hi_ref[pl.ds(r0, 8), :] + wr * car_i + wi * car_r
            hr_ref[pl.ds(r0, 8), :] = xr
            hi_ref[pl.ds(r0, 8), :] = xi
            return (jnp.broadcast_to(xr[edge:edge + 1, :], (8, c)), jnp.broadcast_to(xi[edge:edge + 1, :], (8, c)))

        car_r, car_i = lax.fori_loop(0, t // 8, tile, (cr_ref[...], ci_ref[...]))
        cr_ref[...] = car_r
        ci_ref[...] = car_i

    spec = pl.BlockSpec((t, c), (lambda i: (nb - 1 - i, 0)) if reverse else (lambda i: (i, 0)))
    vspec = pl.BlockSpec((1, c), lambda i: (0, 0))
    hr, hi = _call(
        body,
        name=name,
        grid=(nb,),
        in_specs=[spec, spec, vspec, vspec],
        out_specs=[spec, spec],
        out_shape=[_sds((s, c)), _sds((s, c))],
        scratch_shapes=[pltpu.VMEM((8, c), f32), pltpu.VMEM((8, c), f32)],
        compiler_params=_params("arbitrary"),
    )(b_re, b_im, a_re, a_im)
    return hr, hi


ATT_FEAT = 128
ATT_TQ = 1024
ATT_TK = 1024
ATT_TK_KEY_SIDE = 1024


def _att_tiles(s, key_side=False):
    tq = min(ATT_TQ, s)
    tk = min(ATT_TK_KEY_SIDE if key_side else ATT_TK, tq)
    return tq, tk, tq // tk


def _keys_le_queries(tk, tq, k0, q0):
    row = lax.broadcasted_iota(jnp.int32, (tk, tq), 0) + k0
    col = lax.broadcasted_iota(jnp.int32, (tk, tq), 1) + q0
    return row <= col


def _attn_fwd_t(name, qt, k_aug, vt):
    h, s, _ = k_aug.shape
    tq, tk, ratio = _att_tiles(s)

    def body(qt_ref, k_ref, vt_ref, o_ref, lse_ref):
        qi = pl.program_id(1)
        qt = qt_ref[...]

        def block(kb, carry, masked):
            m, l, acc = carry
            ks = pl.multiple_of(kb * tk, tk)
            st = jnp.dot(k_ref[pl.ds(ks, tk), :], qt, preferred_element_type=f32)
            if masked:
                st = jnp.where(_keys_le_queries(tk, tq, ks, qi * tq), st, -jnp.inf)
            mn = jnp.maximum(m, jnp.max(st, axis=0, keepdims=True))
            p = jnp.exp(st - mn)
            al = jnp.exp(m - mn)
            l = al * l + jnp.sum(p, axis=0, keepdims=True)
            acc = al * acc + jnp.dot(vt_ref[kb], p.astype(bf16), preferred_element_type=f32)
            return mn, l, acc

        init = (jnp.full((1, tq), -jnp.inf, f32), jnp.zeros((1, tq), f32), jnp.zeros((HEAD_DIM, tq), f32))
        first = lax.fori_loop(0, qi * ratio, lambda kb, c: block(kb, c, False), init)
        m, l, acc = lax.fori_loop(qi * ratio, (qi + 1) * ratio, lambda kb, c: block(kb, c, True), first)
        o_ref[...] = acc / l
        lse_ref[...] = m + jnp.log(l)

    return _call(
        body,
        name=name,
        grid=(h, s // tq),
        in_specs=[pl.BlockSpec((None, None, ATT_FEAT, tq), lambda hh, i: (hh, i, 0, 0)),
                  pl.BlockSpec((None, s, ATT_FEAT), lambda hh, i: (hh, 0, 0)),
                  pl.BlockSpec((None, s // tk, HEAD_DIM, tk), lambda hh, i: (hh, 0, 0, 0))],
        out_specs=[pl.BlockSpec((None, HEAD_DIM, tq), lambda hh, i: (hh, 0, i)),
                   pl.BlockSpec((None, 1, tq), lambda hh, i: (hh, 0, i))],
        out_shape=[_sds((h, HEAD_DIM, s)), _sds((h, 1, s))],
        compiler_params=_params("parallel", "arbitrary"),
    )(qt, k_aug, vt)


def _attn_bwd_dq_t(name, qt, k_aug, v, kt, ot, dot_, lse):
    h, s, _ = k_aug.shape
    tq, tk, ratio = _att_tiles(s)

    def body(qt_ref, k_ref, v_ref, kt_ref, o_ref, do_ref, lse_ref, dq_ref, dl_ref):
        qi = pl.program_id(1)
        qt = qt_ref[...]
        dob = do_ref[...]
        delta = jnp.sum(dob.astype(f32) * o_ref[...], axis=0, keepdims=True)
        lse_v = lse_ref[...]

        def block(kb, carry, masked):
            dq, psum = carry
            ks = pl.multiple_of(kb * tk, tk)
            st = jnp.dot(k_ref[pl.ds(ks, tk), :], qt, preferred_element_type=f32)
            p = jnp.exp(st - lse_v)
            if masked:
                p = jnp.where(_keys_le_queries(tk, tq, ks, qi * tq), p, 0.0)
            dp = jnp.dot(v_ref[pl.ds(ks, tk), :], dob, preferred_element_type=f32)
            ds = p * (dp - delta)
            return (dq + jnp.dot(kt_ref[kb], ds.astype(bf16), preferred_element_type=f32),
                    psum + jnp.sum(p * dp, axis=0, keepdims=True))

        carry = lax.fori_loop(0, qi * ratio, lambda kb, c: block(kb, c, False),
                              (jnp.zeros((HEAD_DIM, tq), f32), jnp.zeros((1, tq), f32)))
        dq, psum = lax.fori_loop(qi * ratio, (qi + 1) * ratio, lambda kb, c: block(kb, c, True), carry)
        dq_ref[...] = dq * ATT_SCALE
        dl_ref[...] = psum

    qspec = pl.BlockSpec((None, HEAD_DIM, tq), lambda hh, i: (hh, 0, i))
    rspec = pl.BlockSpec((None, 1, tq), lambda hh, i: (hh, 0, i))
    return _call(
        body,
        name=name,
        grid=(h, s // tq),
        in_specs=[pl.BlockSpec((None, None, ATT_FEAT, tq), lambda hh, i: (hh, i, 0, 0)),
                  pl.BlockSpec((None, s, ATT_FEAT), lambda hh, i: (hh, 0, 0)),
                  pl.BlockSpec((None, s, HEAD_DIM), lambda hh, i: (hh, 0, 0)),
                  pl.BlockSpec((None, s // tk, HEAD_DIM, tk), lambda hh, i: (hh, 0, 0, 0)),
                  qspec, pl.BlockSpec((None, None, HEAD_DIM, tq), lambda hh, i: (hh, i, 0, 0)), rspec],
        out_specs=[qspec, rspec],
        out_shape=[_sds((h, HEAD_DIM, s)), _sds((h, 1, s))],
        compiler_params=_params("parallel", "arbitrary"),
    )(qt, k_aug, v, kt, ot, dot_, lse)


def _attn_bwd_dkv_t(name, qt_blocks, k_aug, v, qh, do, dot_blocks, lse, delta):
    h, s, _ = k_aug.shape
    tq, tk, ratio = _att_tiles(s, key_side=True)
    nq = s // tq

    def body(qt_ref, k_ref, v_ref, q_ref, do_ref, dot_ref, lse_ref, dl_ref, dk_ref, dv_ref, dck_ref, dsum_ref):
        kj = pl.program_id(1)
        kk = k_ref[...]
        vv = v_ref[...]
        dsum_ref[...] = jnp.zeros_like(dsum_ref)

        def block(qi, carry, masked):
            dk, dv = carry
            qs = pl.multiple_of(qi * tq, tq)
            st = jnp.dot(kk, qt_ref[qi], preferred_element_type=f32)
            p = jnp.exp(st - lse_ref[qi])
            if masked:
                p = jnp.where(_keys_le_queries(tk, tq, kj * tk, qs), p, 0.0)
            dv = dv + jnp.dot(p.astype(bf16), do_ref[pl.ds(qs, tq), :], preferred_element_type=f32)
            dp = jnp.dot(vv, dot_ref[qi], preferred_element_type=f32)
            ds = p * (dp - dl_ref[qi])
            dsum_ref[...] += ds
            dk = dk + jnp.dot(ds.astype(bf16), q_ref[pl.ds(qs, tq), :], preferred_element_type=f32)
            return dk, dv

        first = kj // ratio
        carry = block(first, (jnp.zeros((tk, HEAD_DIM), f32), jnp.zeros((tk, HEAD_DIM), f32)), True)
        dk, dv = lax.fori_loop(first + 1, nq, lambda qi, c: block(qi, c, False), carry)
        dk_ref[...] = dk
        dv_ref[...] = dv
        col = jnp.sum(dsum_ref[...], axis=1, keepdims=True)
        dck_ref[...] = -jnp.transpose(jnp.broadcast_to(col, (tk, 128)))[0:1, :]

    full = lambda shape: pl.BlockSpec((None,) + shape, lambda hh, j: (hh,) + (0,) * len(shape))
    kspec = pl.BlockSpec((None, tk, HEAD_DIM), lambda hh, j: (hh, j, 0))
    return _call(
        body,
        name=name,
        grid=(h, s // tk),
        in_specs=[full((nq, ATT_FEAT, tq)),
                  pl.BlockSpec((None, tk, ATT_FEAT), lambda hh, j: (hh, j, 0)),
                  kspec, full((s, HEAD_DIM)), full((s, HEAD_DIM)), full((nq, HEAD_DIM, tq)),
                  full((nq, 1, tq)), full((nq, 1, tq))],
        out_specs=[kspec, kspec, pl.BlockSpec((None, None, 1, tk), lambda hh, j: (hh, j, 0, 0))],
        out_shape=[_sds((h, s, HEAD_DIM)), _sds((h, s, HEAD_DIM)), _sds((h, s // tk, 1, tk))],
        scratch_shapes=[pltpu.VMEM((tk, tq), f32)],
        compiler_params=_params("parallel", "arbitrary"),
    )(qt_blocks, k_aug, v, qh, do, dot_blocks, lse, delta)


C_LANES = 128


def _selections():
    h = jnp.arange(N_HEADS)[:, None, None]
    row = jnp.arange(D_B + 3 * C_LANES)[None, :, None]
    col = jnp.arange(ATT_FEAT)[None, None, :]
    head_col = (row < D_B) & (row // HEAD_DIM == h) & (col == row % HEAD_DIM)

    def c_part(p, lane0):
        return (row == D_B + p * C_LANES + h) & (col == lane0 + p)

    c_q = c_part(0, HEAD_DIM) | c_part(1, HEAD_DIM) | c_part(2, HEAD_DIM)
    c_k = c_part(0, HEAD_DIM + 3) | c_part(1, HEAD_DIM + 3) | c_part(2, HEAD_DIM + 3)
    sel_q = (head_col | c_q).astype(bf16)
    sel_k = head_col.astype(bf16) - c_k.astype(bf16)
    sel_h = head_col[:, :D_B, :HEAD_DIM].astype(bf16)
    lane = jnp.arange(ATT_FEAT)
    ones_q = ((lane >= HEAD_DIM + 3) & (lane < HEAD_DIM + 6)).astype(f32)
    ones_k = ((lane >= HEAD_DIM) & (lane < HEAD_DIM + 3)).astype(f32)
    return dict(sel_qt=sel_q.transpose(0, 2, 1), sel_k=sel_k, sel_h=sel_h, sel_ht=sel_h.transpose(0, 2, 1),
                ones_q=ones_q.reshape(ATT_FEAT, 1), ones_k=ones_k.reshape(1, ATT_FEAT))


def _attn_prep(name, z, c, sel):
    s = z.shape[0]
    tq, tk, ratio = _att_tiles(s)

    def body(q_ref, k_ref, v_ref, c_ref, sqt_ref, sk_ref, sh_ref, sht_ref, oq_ref, ok_ref,
             qt_out, ka_out, kt_out, vt_out, v_out, qh_out):
        cv = c_ref[...]
        hi = cv.astype(bf16)
        r1 = cv - hi.astype(f32)
        mid = r1.astype(bf16)
        lo = (r1 - mid.astype(f32)).astype(bf16)
        qs = (q_ref[...] * ATT_SCALE).astype(bf16)
        kb = k_ref[...].astype(bf16)
        vb = v_ref[...].astype(bf16)
        xq = jnp.concatenate([qs, hi, mid, lo], axis=-1)
        xk = jnp.concatenate([kb, hi, mid, lo], axis=-1)
        for h in range(N_HEADS):
            qt = lax.dot_general(sqt_ref[h], xq, _DN["nt"], preferred_element_type=f32) + oq_ref[...]
            qt_out[h, 0] = qt.astype(bf16)
            ka_out[h] = (jnp.dot(xk, sk_ref[h], preferred_element_type=f32) + ok_ref[...]).astype(bf16)
            kt = lax.dot_general(sht_ref[h], kb, _DN["nt"], preferred_element_type=f32).astype(bf16)
            vt = lax.dot_general(sht_ref[h], vb, _DN["nt"], preferred_element_type=f32).astype(bf16)
            for j in range(ratio):
                kt_out[h, j] = kt[:, j * tk:(j + 1) * tk]
                vt_out[h, j] = vt[:, j * tk:(j + 1) * tk]
            v_out[h] = jnp.dot(vb, sh_ref[h], preferred_element_type=f32).astype(bf16)
            qh_out[h] = jnp.dot(qs, sh_ref[h], preferred_element_type=f32).astype(bf16)

    whole = lambda a: pl.BlockSpec(a.shape, lambda i, nd=a.ndim: (0,) * nd)
    consts = [sel["sel_qt"], sel["sel_k"], sel["sel_h"], sel["sel_ht"], sel["ones_q"], sel["ones_k"]]
    return pl.pallas_call(
        body,
        name=name,
        grid=(s // tq,),
        in_specs=[pl.BlockSpec((tq, D_B), lambda i: (i, 2)), pl.BlockSpec((tq, D_B), lambda i: (i, 3)),
                  pl.BlockSpec((tq, D_B), lambda i: (i, 4)), pl.BlockSpec((tq, C_LANES), lambda i: (i, 0))]
        + [whole(a) for a in consts],
        out_specs=[pl.BlockSpec((N_HEADS, 1, ATT_FEAT, tq), lambda i: (0, i, 0, 0)),
                   pl.BlockSpec((N_HEADS, tq, ATT_FEAT), lambda i: (0, i, 0)),
                   pl.BlockSpec((N_HEADS, ratio, HEAD_DIM, tk), lambda i: (0, i, 0, 0)),
                   pl.BlockSpec((N_HEADS, ratio, HEAD_DIM, tk), lambda i: (0, i, 0, 0)),
                   pl.BlockSpec((N_HEADS, tq, HEAD_DIM), lambda i: (0, i, 0)),
                   pl.BlockSpec((N_HEADS, tq, HEAD_DIM), lambda i: (0, i, 0))],
        out_shape=[_sds((N_HEADS, s // tq, ATT_FEAT, tq), bf16), _sds((N_HEADS, s, ATT_FEAT), bf16),
                   _sds((N_HEADS, s // tk, HEAD_DIM, tk), bf16), _sds((N_HEADS, s // tk, HEAD_DIM, tk), bf16),
                   _sds((N_HEADS, s, HEAD_DIM), bf16), _sds((N_HEADS, s, HEAD_DIM), bf16)],
        compiler_params=_params("parallel"),
    )(z, z, z, c, *consts)


def _attn_do_prep(name, dob, sel):
    s = dob.shape[0]
    tq = _att_tiles(s)[0]

    def body(do_ref, sh_ref, sht_ref, dot_out, do_out):
        db = do_ref[...].astype(bf16)
        for h in range(N_HEADS):
            dot_out[h, 0] = lax.dot_general(sht_ref[h], db, _DN["nt"], preferred_element_type=f32).astype(bf16)
            do_out[h] = jnp.dot(db, sh_ref[h], preferred_element_type=f32).astype(bf16)

    whole = lambda a: pl.BlockSpec(a.shape, lambda i, nd=a.ndim: (0,) * nd)
    return pl.pallas_call(
        body,
        name=name,
        grid=(s // tq,),
        in_specs=[pl.BlockSpec((tq, D_B), lambda i: (i, 0)), whole(sel["sel_h"]), whole(sel["sel_ht"])],
        out_specs=[pl.BlockSpec((N_HEADS, 1, HEAD_DIM, tq), lambda i: (0, i, 0, 0)),
                   pl.BlockSpec((N_HEADS, tq, HEAD_DIM), lambda i: (0, i, 0))],
        out_shape=[_sds((N_HEADS, s // tq, HEAD_DIM, tq), bf16), _sds((N_HEADS, s, HEAD_DIM), bf16)],
        compiler_params=_params("parallel"),
    )(dob, sel["sel_h"], sel["sel_ht"])


def _dz_assemble(name, dax, dag, dqt, dkh, dvh, df, dcu, sel):
    s = dax.shape[0]
    tm = _tile(s, 512)

    def body(dax_ref, dag_ref, dqt_ref, dk_ref, dv_ref, df_ref, dcu_ref, sht_ref, o_ref):
        dq = jnp.zeros((tm, D_B), f32)
        dk = jnp.zeros((tm, D_B), f32)
        dv = jnp.zeros((tm, D_B), f32)
        for h in range(N_HEADS):
            place = sht_ref[h]
            dq = dq + lax.dot_general(dqt_ref[h].astype(bf16), place, _DN["tn"], preferred_element_type=f32)
            dk = dk + jnp.dot(dk_ref[h].astype(bf16), place, preferred_element_type=f32)
            dv = dv + jnp.dot(dv_ref[h].astype(bf16), place, preferred_element_type=f32)
        pieces = [dax_ref[...], dag_ref[...], dq, dk, dv, df_ref[...], dcu_ref[...]]
        off = 0
        for p in pieces:
            o_ref[:, off:off + p.shape[1]] = p.astype(bf16)
            off += p.shape[1]

    rows = lambda c_: pl.BlockSpec((tm, c_), lambda i: (i, 0))
    heads = pl.BlockSpec((N_HEADS, tm, HEAD_DIM), lambda i: (0, i, 0))
    return pl.pallas_call(
        body,
        name=name,
        grid=(s // tm,),
        in_specs=[rows(D_A), rows(D_A), pl.BlockSpec((N_HEADS, HEAD_DIM, tm), lambda i: (0, 0, i)), heads, heads,
                  rows(128), rows(D_C), pl.BlockSpec(sel["sel_ht"].shape, lambda i: (0, 0, 0))],
        out_specs=rows(N_IN_P),
        out_shape=_sds((s, N_IN_P), bf16),
        compiler_params=_params("parallel"),
    )(dax, dag, dqt, dkh, dvh, df, dcu, sel["sel_ht"])


def _s5_disc_fn(are, aim, ldt):
    dt = jnp.exp(ldt)
    er = jnp.exp(are * dt)
    br = er * jnp.cos(aim * dt)
    bi = er * jnp.sin(aim * dt)
    nr = br - 1.0
    den = are * are + aim * aim
    return br, bi, (nr * are + bi * aim) / den, (bi * are - nr * aim) / den


def _s5_disc(name, are, aim, ldt):
    def body(a_ref, b_ref, c_ref, o0, o1, o2, o3):
        r = _s5_disc_fn(a_ref[...], b_ref[...], c_ref[...])
        o0[...], o1[...], o2[...], o3[...] = r

    shp = _sds((S5_GROUPS, S5_STATE))
    return pl.pallas_call(body, name=name, out_shape=[shp] * 4)(are, aim, ldt)


def _s5_disc_bwd(name, are, aim, ldt, cts):
    def body(a_ref, b_ref, c_ref, d0, d1, d2, d3, o0, o1, o2):
        _, vjp = jax.vjp(_s5_disc_fn, a_ref[...], b_ref[...], c_ref[...])
        o0[...], o1[...], o2[...] = vjp((d0[...], d1[...], d2[...], d3[...]))

    shp = _sds((S5_GROUPS, S5_STATE))
    return pl.pallas_call(body, name=name, out_shape=[shp, shp, _sds((S5_GROUPS, 1))])(are, aim, ldt, *cts)


def _adamw_rows(w, g, m, v):
    m = ADAM_B1 * m + (1.0 - ADAM_B1) * g
    v = ADAM_B2 * v + (1.0 - ADAM_B2) * (g * g)
    m_hat = m / (1.0 - ADAM_B1 ** ADAM_STEP)
    v_hat = v / (1.0 - ADAM_B2 ** ADAM_STEP)
    return -ADAM_LR * (m_hat / (jnp.sqrt(v_hat) + ADAM_EPS) + ADAM_WD * w), m, v


def _adamw(name, w, ga, gb, m, v):
    rows, cols = w.shape
    tr = _row_tile(rows)

    def body(w_ref, ga_ref, gb_ref, m_ref, v_ref, g_out, d_out, m_out, v_out):
        g = ga_ref[...] + gb_ref[...]
        d, mm, vv = _adamw_rows(w_ref[...], g, m_ref[...], v_ref[...])
        g_out[...] = g
        d_out[...] = d
        m_out[...] = mm
        v_out[...] = vv

    spec = pl.BlockSpec((tr, cols), lambda i: (i, 0))
    return pl.pallas_call(
        body, name=name, grid=(rows // tr,), in_specs=[spec] * 5, out_specs=[spec] * 4,
        out_shape=[_sds((rows, cols))] * 4, compiler_params=_params("parallel"),
    )(w, ga, gb, m, v)


def _sum_stack(name, st):
    n, rows, cols = st.shape
    tr = _row_tile(rows)

    def body(s_ref, o_ref):
        acc = s_ref[0].astype(f32)
        for j in range(1, n):
            acc = acc + s_ref[j].astype(f32)
        o_ref[...] = acc

    return pl.pallas_call(
        body, name=name, grid=(rows // tr,), in_specs=[pl.BlockSpec((n, tr, cols), lambda i: (0, i, 0))],
        out_specs=pl.BlockSpec((tr, cols), lambda i: (i, 0)), out_shape=_sds((rows, cols)),
        compiler_params=_params("parallel"),
    )(st)


def _block_diag(w):
    h, n, m = w.shape
    return jnp.einsum("hij,hg->higj", w, jnp.eye(h, dtype=w.dtype)).reshape(h * n, h * m)


def _block_diag_part(dense, h):
    n, m = dense.shape[0] // h, dense.shape[1] // h
    return jnp.einsum("higj,hg->hij", dense.reshape(h, n, h, m), jnp.eye(h, dtype=dense.dtype))


def _s5_matrices(coef_re, coef_im, b_re, b_im, c_re, c_im):
    bb_re = coef_re[:, :, None] * b_re - coef_im[:, :, None] * b_im
    bb_im = coef_re[:, :, None] * b_im + coef_im[:, :, None] * b_re
    wb_re = _block_diag(jnp.swapaxes(bb_re, 1, 2))
    wb_im = _block_diag(jnp.swapaxes(bb_im, 1, 2))
    wc_re = _block_diag(jnp.swapaxes(c_re, 1, 2))
    wc_im = _block_diag(jnp.swapaxes(-c_im, 1, 2))
    return wb_re, wb_im, wc_re, wc_im


def _shift_down(t):
    return jnp.concatenate([jnp.zeros((1, t.shape[1]), t.dtype), t[:-1]], axis=0)


def _shift_up(t):
    return jnp.concatenate([t[1:], jnp.zeros((1, t.shape[1]), t.dtype)], axis=0)


def _row(v):
    return v.reshape(1, -1)


def _ffn_fwd(tag, h, get, names, gamma, beta, gate_first=False):
    wg = get(names[0])
    if gate_first:
        g = _ffn_gate(tag + "_gate", h, wg)
        wu = get(names[1])
        u, act = _ffn_up_given_gate(tag + "_up", h, wu, g)
    else:
        wu = get(names[1])
        g, u, act = _ffn_up(tag + "_up", h, wg, wu)
    wd = get(names[2])
    r, out = _mm_ln(tag + "_down", act, wd, h, gamma, beta, 0.5, k_slabs=True)
    return out, dict(h=h, g=g, u=u, act=act, r=r, wg=wg, wu=wu, wd=wd)


def _ffn_bwd(tag, dout, sv, names, gamma, put, after_ln=None):
    s = dout.shape[0]
    dr, dgam, dbet = _ln_bwd(tag + "_lnb", sv["r"], dout, gamma)
    if after_ln is not None:
        after_ln(dgam, dbet)
    put(names[2], _mm_plain(tag + "_dwd", "tn", _Slabs(sv["act"]), dr, (D_FF, D_MODEL, s), scale=0.5, out_dtype=bf16,
                            tiles=(FF_SLAB, 1024, _tile(s, 2048))))
    dg, du = _ffn_dact(tag + "_dact", dr, sv["wd"], sv["g"], sv["u"])
    dwg, dwu = _mm2(tag + "_dwgu", "tn", (D_FF, D_MODEL, s), _Slabs(dg), sv["h"], _Slabs(du), None, separate=True,
                    out_dtype=bf16, tiles=(FF_SLAB, 1024, _tile(s, 2048)))
    put(names[0], dwg)
    put(names[1], dwu)
    slabs = range(dg.shape[0])
    dh = _mm(tag + "_dh", "nn", (s, D_MODEL, FF_SLAB), (_tile(s, 512), D_MODEL, FF_SLAB),
             [_KPart(dg, j) for j in slabs] + [_KPart(du, j) for j in slabs],
             [_KPart(sv["wg"], j) for j in slabs] + [_KPart(sv["wu"], j) for j in slabs],
             [(j, j, 0) for j in range(2 * len(slabs))], 1,
             lambda accs, extras, vecs: [accs[0] + ALPHA * extras[0]], [f32], extras=[dr])[0]
    return dh, dgam, dbet


def _mixer_fwd(tag, h1, w):
    s = h1.shape[0]
    z = _mm_plain(tag + "_win", "nn", h1, w["w_in"], (s, N_IN_P, D_MODEL), tiles=(_tile(s, 512), 768, D_MODEL))
    ag, f, cu_cols = (z, D_A, 1), (z, 128, F_OFF // 128), (z, D_C, CU_OFF // D_C)
    cu = z[:, CU_OFF:]
    xa = _conv_fwd(tag + "_conv", z, w["conv_w"], w["conv_b"])
    a, gated = _rg_gates(tag + "_gates", xa, w["rg_wa"], w["rg_wx"], w["rg_ba"], w["rg_bx"], w["rg_lam"])
    ha = _lin_scan(tag + "_rgscan", a, gated, False)
    ones = jnp.ones((s, 128), f32)
    c = _lin_scan(tag + "_cumf", ones, _log_f(tag + "_logf", f, w["fox_bf"]), False)
    att = dict(zip(("qt", "k_aug", "kt", "vt", "v", "qh"), _attn_prep(tag + "_attnprep", z, c, w["sel"])))
    ot, lse = _attn_fwd_t(tag + "_attn", att["qt"], att["k_aug"], att["vt"])
    ob = ot.reshape(D_B, s).T
    bu_re, bu_im = _mm2(tag + "_s5in", "nn", (s, S5_LANES, D_C), cu, w["wb_re"], None, w["wb_im"], separate=True,
                        tiles=(_tile(s, 512), 1024, D_C))
    hre, him = _s5_scan(tag + "_s5scan", bu_re, bu_im, w["abar_re"], w["abar_im"], False)
    o = _mix_out(tag + "_mixout", ag, ha, ob, hre, him, cu_cols, w["s5_d"], w["mix_g"], w["wc_re"], w["wc_im"],
                 w["w_glu"])
    sv = dict(h1=h1, z=z, ag=ag, f=f, cu=cu, cu_cols=cu_cols, xa=xa, a=a, ha=ha, att=att, ot=ot, lse=lse, ob=ob,
              hre=hre, him=him, o=o)
    return o, sv


def _mixer_bwd(tag, do, dr2, sv, w, put):
    s = do.shape[0]
    (dag, dha, dob, dhre, dhim, dcu1, dwcr, dwci, dwglu, dd, dgn) = _mix_out_bwd(
        tag + "_mixoutb", do, sv["ag"], sv["ha"], sv["ob"], sv["hre"], sv["him"], sv["cu_cols"], w["s5_d"], w["mix_g"],
        w["wc_re"], w["wc_im"], w["w_glu"])
    put("s5_w_glu", dwglu.astype(bf16))
    gre, gim = _s5_scan(tag + "_s5scanb", dhre, dhim, w["abar_re"], -w["abar_im"], True)
    dab_re, dab_im = _s5_decay_grad(tag + "_s5dec", sv["hre"], sv["him"], gre, gim)
    dwb_re, dwb_im = _mm2(tag + "_s5dwb", "tn", (D_C, S5_LANES, s), sv["cu"], gre, None, gim, separate=True,
                          tiles=(D_C, 1024, _tile(s, 1024)))
    dcu = _mm2(tag + "_s5dcu", "nt", (s, D_C, S5_LANES), gre, w["wb_re"], gim, w["wb_im"], add=dcu1,
               tiles=(_tile(s, 512), D_C, 1024))[0]
    att = sv["att"]
    tq = _att_tiles(s)[0]
    nt = s // tq
    dot_blocks, doh = _attn_do_prep(tag + "_doprep", dob, w["sel"])
    dqt, delta = _attn_bwd_dq_t(tag + "_attndq", att["qt"], att["k_aug"], att["v"], att["kt"], sv["ot"], dot_blocks,
                                sv["lse"])
    dkh, dvh, dck = _attn_bwd_dkv_t(tag + "_attndkv", att["qt"], att["k_aug"], att["v"], att["qh"], doh, dot_blocks,
                                    sv["lse"].reshape(N_HEADS, nt, 1, tq), delta.reshape(N_HEADS, nt, 1, tq))
    dc = jnp.pad(dck.reshape(N_HEADS, s).T, ((0, 0), (0, 128 - N_HEADS)))
    dlf = _lin_scan(tag + "_cumfb", jnp.ones((s, 128), f32), dc, True)
    df, dbf = _log_f_bwd(tag + "_logfb", dlf, sv["f"], w["fox_bf"])
    ga = _lin_scan(tag + "_rgscanb", _shift_up(sv["a"]), dha, True)
    dxa, dwa, dwx, dba, dbx, dlam = _rg_gates_bwd(tag + "_gatesb", sv["xa"], ga, _shift_down(sv["ha"]), w["rg_wa"],
                                                  w["rg_wx"], w["rg_ba"], w["rg_bx"], w["rg_lam"])
    dax, dconv = _conv_bwd(tag + "_convb", dxa, sv["z"], w["conv_w"])
    dz = _dz_assemble(tag + "_dz", dax, dag, dqt, dkh, dvh, df, dcu, w["sel"])
    put("w_in", _mm_plain(tag + "_dwin", "tn", sv["h1"], dz, (D_MODEL, N_IN_P, s), out_dtype=bf16,
                          tiles=(512, 768, _tile(s, 1024))))
    dh1 = _mm_plain(tag + "_dh1", "nt", dz, w["w_in"], (s, D_MODEL, N_IN_P), add=dr2, add_coef=ALPHA,
                    tiles=(_tile(s, 512), 1024, 768))
    grads = dict(dconv=dconv, dwa=dwa, dwx=dwx, dba=dba, dbx=dbx, dlam=dlam, dbf=dbf,
                 dab_re=dab_re, dab_im=dab_im, dwb_re=dwb_re, dwb_im=dwb_im, dwcr=dwcr, dwci=dwci, dd=dd, dgn=dgn)
    return dh1, grads


SMALL_NAMES = ["ln1_g", "ln1_b", "conv_w", "conv_b", "rg_w_a", "rg_b_a", "rg_w_x", "rg_b_x", "rg_lambda", "fox_b_f",
               "s5_a_re", "s5_a_im", "s5_log_dt", "s5_b_re", "s5_b_im", "s5_c_re", "s5_c_im", "s5_d", "mix_norm_g",
               "ln2_g", "ln2_b", "ln3_g", "ln3_b"]
BIG_NAMES = ["ffn1_w_gate", "ffn1_w_up", "ffn1_w_down", "w_in", "s5_w_glu", "w_out", "ffn2_w_gate", "ffn2_w_up",
             "ffn2_w_down"]


def _local_step(x, target, weight, small, on_grads, on_small):
    h = x
    saved = []
    sel = _selections()
    for l in range(DEPTH):
        get = functools.partial(weight, l)

        sm = {n: small[n][l] for n in SMALL_NAMES}
        abar_re, abar_im, coef_re, coef_im = _s5_disc(f"l{l}_s5disc", sm["s5_a_re"], sm["s5_a_im"],
                                                      sm["s5_log_dt"].reshape(S5_GROUPS, 1))
        mats, mats_vjp = jax.vjp(_s5_matrices, coef_re, coef_im, sm["s5_b_re"], sm["s5_b_im"], sm["s5_c_re"],
                                 sm["s5_c_im"])
        w = dict(
            sel=sel, conv_w=sm["conv_w"], conv_b=_row(sm["conv_b"]),
            rg_wa=_block_diag(sm["rg_w_a"]).astype(bf16), rg_wx=_block_diag(sm["rg_w_x"]).astype(bf16),
            rg_ba=_row(sm["rg_b_a"]), rg_bx=_row(sm["rg_b_x"]), rg_lam=_row(sm["rg_lambda"]),
            fox_bf=jnp.pad(_row(sm["fox_b_f"]), ((0, 0), (0, 128 - N_HEADS))),
            abar_re=_row(abar_re), abar_im=_row(abar_im),
            wb_re=mats[0].astype(bf16), wb_im=mats[1].astype(bf16), wc_re=mats[2].astype(bf16),
            wc_im=mats[3].astype(bf16), s5_d=_row(sm["s5_d"]), mix_g=_row(sm["mix_norm_g"]))
        h1, sv1 = _ffn_fwd(f"l{l}_ffn1", h, get, GROUPS["F1"], _row(sm["ln1_g"]), _row(sm["ln1_b"]),
                           gate_first=(l == 0))
        w["w_in"], w["w_glu"] = get("w_in"), get("s5_w_glu")
        o, svm = _mixer_fwd(f"l{l}_mix", h1, w)
        w_out = get("w_out")
        r2, h2 = _mm_ln(f"l{l}_wout", o, w_out, h1, _row(sm["ln2_g"]), _row(sm["ln2_b"]), 1.0)
        h3, sv2 = _ffn_fwd(f"l{l}_ffn2", h2, get, GROUPS["F2"], _row(sm["ln3_g"]), _row(sm["ln3_b"]))
        saved.append(dict(sm=sm, w=w, w_out=w_out, sv1=sv1, svm=svm, r2=r2, sv2=sv2, mats_vjp=mats_vjp))
        h = h3

    dh, loss_row = _loss_head("loss_head", h, target)
    s = x.shape[0]
    gsmall = {n: [None] * DEPTH for n in SMALL_NAMES}
    for l in reversed(range(DEPTH)):
        sd = saved[l]
        sm, w = sd["sm"], sd["w"]

        def put(name, grad, l=l):
            on_grads((l, name), grad)

        dh2, dgam, dbet = _ffn_bwd(f"l{l}_ffn2", dh, sd["sv2"], GROUPS["F2"], _row(sm["ln3_g"]), put)
        gsmall["ln3_g"][l], gsmall["ln3_b"][l] = dgam[0], dbet[0]
        dr2, dgam, dbet = _ln_bwd(f"l{l}_ln2b", sd["r2"], dh2, _row(sm["ln2_g"]))
        gsmall["ln2_g"][l], gsmall["ln2_b"][l] = dgam[0], dbet[0]
        put("w_out", _mm_plain(f"l{l}_dwout", "tn", sd["svm"]["o"], dr2, (D_MODEL, D_MODEL, s), out_dtype=bf16))
        do = _mm_plain(f"l{l}_do", "nt", dr2, sd["w_out"], (s, D_MODEL, D_MODEL))
        dh1, g = _mixer_bwd(f"l{l}_mix", do, dr2, sd["svm"], w, put)
        gsmall["conv_w"][l], gsmall["conv_b"][l] = g["dconv"][:CONV_WIDTH], g["dconv"][CONV_WIDTH]
        gsmall["rg_w_a"][l] = _block_diag_part(g["dwa"], N_HEADS)
        gsmall["rg_w_x"][l] = _block_diag_part(g["dwx"], N_HEADS)
        gsmall["rg_b_a"][l], gsmall["rg_b_x"][l], gsmall["rg_lambda"][l] = g["dba"][0], g["dbx"][0], g["dlam"][0]
        gsmall["fox_b_f"][l] = g["dbf"][0, :N_HEADS]
        dcoef_re, dcoef_im, db_re, db_im, dc_re, dc_im = sd["mats_vjp"]((g["dwb_re"], g["dwb_im"], g["dwcr"], g["dwci"]))
        da_re, da_im, dldt = _s5_disc_bwd(
            f"l{l}_s5discb", sm["s5_a_re"], sm["s5_a_im"], sm["s5_log_dt"].reshape(S5_GROUPS, 1),
            (g["dab_re"].reshape(S5_GROUPS, S5_STATE), g["dab_im"].reshape(S5_GROUPS, S5_STATE), dcoef_re, dcoef_im))
        gsmall["s5_a_re"][l], gsmall["s5_a_im"][l], gsmall["s5_log_dt"][l] = da_re, da_im, dldt[:, 0]
        gsmall["s5_b_re"][l], gsmall["s5_b_im"][l], gsmall["s5_c_re"][l], gsmall["s5_c_im"][l] = db_re, db_im, dc_re, dc_im
        gsmall["s5_d"][l], gsmall["mix_norm_g"][l] = g["dd"][0], g["dgn"][0]

        def after_ln(dgam, dbet, l=l):
            gsmall["ln1_g"][l], gsmall["ln1_b"][l] = dgam[0], dbet[0]
            if l == 0:
                on_small({n: jnp.stack(v) for n, v in gsmall.items()})

        dh, _, _ = _ffn_bwd(f"l{l}_ffn1", dh1, sd["sv1"], GROUPS["F1"], _row(sm["ln1_g"]), put, after_ln)
    return loss_row[0, 0], dh


def _position():
    return lax.axis_index("x"), lax.axis_index("y"), lax.axis_index("c")


_ANY = pl.BlockSpec(memory_space=pl.ANY)


COLUMN_SHARDED = ("ffn1_w_gate", "ffn1_w_up", "ffn2_w_gate", "ffn2_w_up")
PACK_QUANTUM = 128 * 256


def _permute_in_cols(w):
    pad = jnp.zeros(w.shape[:-1] + (128 - N_HEADS,), w.dtype)
    return jnp.concatenate([w[..., :F_OFF + N_HEADS], pad, w[..., F_OFF + N_HEADS:]], axis=-1)


def _unpermute_in_cols(w):
    return jnp.concatenate([w[..., :F_OFF + N_HEADS], w[..., CU_OFF:]], axis=-1)


def _pack(arrs):
    flat = jnp.concatenate([a.reshape(-1) for a in arrs])
    pad = -flat.shape[0] % PACK_QUANTUM
    return jnp.pad(flat, (0, pad)).reshape(-1, 128)


def _unpack(buf, shapes):
    flat = buf.reshape(-1)
    out, off = [], 0
    for shp in shapes:
        size = math.prod(shp)
        out.append(flat[off:off + size].reshape(shp))
        off += size
    return out


WEIGHT_NAMES = ["ffn1_w_gate", "ffn1_w_up", "ffn1_w_down", "ln1_g", "ln1_b", "w_in", "conv_w", "conv_b", "rg_w_a",
                "rg_b_a", "rg_w_x", "rg_b_x", "rg_lambda", "fox_b_f", "s5_a_re", "s5_a_im", "s5_log_dt", "s5_b_re",
                "s5_b_im", "s5_c_re", "s5_c_im", "s5_d", "s5_w_glu", "mix_norm_g", "w_out", "ln2_g", "ln2_b",
                "ffn2_w_gate", "ffn2_w_up", "ffn2_w_down", "ln3_g", "ln3_b"]


def _remote(src, dst, send_sems, recv_sems, k, peer):
    return pltpu.make_async_remote_copy(src_ref=src, dst_ref=dst, send_sem=send_sems.at[k], recv_sem=recv_sems.at[k],
                                        device_id=peer, device_id_type=MESH)


class _ChipGatherPart:
    def __init__(self, arrays):
        self.arrays, self.results = list(arrays), None

    def out_shape(self):
        return [_sds((N_CHIPS,) + a.shape, a.dtype) for a in self.arrays]

    def sems(self):
        n = len(self.arrays)
        return [pltpu.SemaphoreType.DMA((3 * n,)), pltpu.SemaphoreType.DMA((3 * n,)), pltpu.SemaphoreType.DMA((n,))]

    def copies(self, ins, outs, sems):
        send_sems, recv_sems, local_sems = sems
        x, y, c = _position()
        me = 2 * x + y
        local, sends, recvs = [], [], []
        for i, (src, dst) in enumerate(zip(ins, outs)):
            local.append(pltpu.make_async_copy(self.mine(src, me), dst.at[me], local_sems.at[i]))
            for r, (px, py) in enumerate([(1 - x, y), (x, 1 - y), (1 - x, 1 - y)]):
                peer = 2 * px + py
                sends.append(_remote(self.theirs(src, peer), dst.at[me], send_sems, recv_sems, 3 * i + r, (px, py, c)))
                recvs.append(_remote(self.mine(src, me), dst.at[peer], send_sems, recv_sems, 3 * i + r, (px, py, c)))
        return local, sends, recvs

    def mine(self, src, me):
        return src

    def theirs(self, src, peer):
        return src


class _ChipGatherHalvesPart(_ChipGatherPart):
    def sems(self):
        n = len(self.arrays)
        return super().sems() + [pltpu.SemaphoreType.DMA((3 * n,)), pltpu.SemaphoreType.DMA((3 * n,))]

    def _half(self, ref, which):
        rows = ref.shape[0] // 2
        return ref.at[pl.ds(which * rows, rows)]

    def copies(self, ins, outs, sems):
        send_sems, recv_sems, local_sems = sems[:3]
        x, y, c = _position()
        me = 2 * x + y
        local, sends, recvs = [], [], []
        for i, (src, dst) in enumerate(zip(ins, outs)):
            local.append(pltpu.make_async_copy(src, dst.at[me], local_sems.at[i]))
            for r, (px, py) in enumerate([(1 - x, y), (x, 1 - y), (1 - x, 1 - y)]):
                sends.append(_remote(self._half(src, c), self._half(dst.at[me], c), send_sems, recv_sems, 3 * i + r,
                                     (px, py, c)))
                recvs.append(_remote(self._half(src, c), self._half(dst.at[2 * px + py], c), send_sems, recv_sems,
                                     3 * i + r, (px, py, c)))
        return local, sends, recvs

    def forwards(self, ins, outs, sems):
        send_sems, recv_sems = sems[3:]
        x, y, c = _position()
        sends, recvs = [], []
        for i, dst in enumerate(outs):
            for r, (px, py) in enumerate([(1 - x, y), (x, 1 - y), (1 - x, 1 - y)]):
                slot = dst.at[2 * px + py]
                sends.append(_remote(self._half(slot, c), self._half(slot, c), send_sems, recv_sems, 3 * i + r,
                                     (x, y, 1 - c)))
                recvs.append(_remote(self._half(slot, c), self._half(slot, 1 - c), send_sems, recv_sems, 3 * i + r,
                                     (x, y, 1 - c)))
        return sends, recvs


class _ChipScatterPart(_ChipGatherPart):
    def out_shape(self):
        return [_sds(a.shape, a.dtype) for a in self.arrays]

    def mine(self, src, me):
        return src.at[me]

    def theirs(self, src, peer):
        return src.at[peer]


class _SiblingSwapPart:
    def __init__(self, arrays):
        self.arrays, self.results = list(arrays), None

    def out_shape(self):
        return [_sds(a.shape, a.dtype) for a in self.arrays]

    def sems(self):
        n = len(self.arrays)
        return [pltpu.SemaphoreType.DMA((n,)), pltpu.SemaphoreType.DMA((n,))]

    def copies(self, ins, outs, sems):
        x, y, c = _position()
        both = [_remote(src, dst, sems[0], sems[1], i, (x, y, 1 - c)) for i, (src, dst) in enumerate(zip(ins, outs))]
        return [], both, both


def _split_by(parts, refs, count):
    out, off = [], 0
    for p in parts:
        out.append(refs[off:off + count(p)])
        off += count(p)
    return out


def _parts_refs(parts, in_refs, out_refs, sem_refs):
    return zip(parts, _split_by(parts, in_refs, lambda p: len(p.arrays)),
               _split_by(parts, out_refs, lambda p: len(p.arrays)), _split_by(parts, sem_refs, lambda p: len(p.sems())))


def _exchange_start(parts, in_refs, out_refs, sem_refs):
    for part, ins, outs, sems in _parts_refs(parts, in_refs, out_refs, sem_refs):
        local, sends, _ = part.copies(ins, outs, sems)
        for cp in local + sends:
            cp.start()


def _exchange_finish(parts, in_refs, out_refs, sem_refs):
    split = list(_parts_refs(parts, in_refs, out_refs, sem_refs))
    copies = [part.copies(ins, outs, sems) for part, ins, outs, sems in split]
    for _, _, recvs in copies:
        for cp in recvs:
            cp.wait_recv()
    second = [part.forwards(ins, outs, sems) for part, ins, outs, sems in split if hasattr(part, "forwards")]
    for sends, _ in second:
        for cp in sends:
            cp.start()
    for sends, recvs in second:
        for cp in recvs:
            cp.wait_recv()
        for cp in sends:
            cp.wait_send()
    for local, sends, _ in copies:
        for cp in sends:
            cp.wait_send()
        for cp in local:
            cp.wait()


def _exchange_operands(parts):
    return ([a for p in parts for a in p.arrays], [s for p in parts for s in p.out_shape()],
            [s for p in parts for s in p.sems()])


def _set_results(parts, res):
    for part, outs in zip(parts, _split_by(parts, list(res), lambda p: len(p.arrays))):
        part.results = list(outs)


def _exchange_now(name, parts):
    x_in, x_out, x_sem = _exchange_operands(parts)
    n = len(x_in)

    def body(*refs):
        _exchange_start(parts, refs[:n], refs[n:2 * n], refs[2 * n:])
        _exchange_finish(parts, refs[:n], refs[n:2 * n], refs[2 * n:])

    res = pl.pallas_call(body, name=name, in_specs=[_ANY] * n, out_specs=[_ANY] * n, out_shape=x_out,
                         scratch_shapes=x_sem)(*x_in)
    _set_results(parts, res)


_RIDERS = {}


def _call(body, *, name, grid, in_specs, out_specs, out_shape, scratch_shapes=(), compiler_params=None):
    make_parts = _RIDERS.pop(name, None)
    if make_parts is None:
        return pl.pallas_call(body, name=name, grid=grid, in_specs=in_specs, out_specs=out_specs, out_shape=out_shape,
                              scratch_shapes=scratch_shapes, compiler_params=compiler_params)
    parts = make_parts()
    x_in, x_out, x_sem = _exchange_operands(parts)
    n_out, n_scr, n_x = len(out_shape), len(scratch_shapes), len(x_in)

    def run(*args):
        n_in = len(args)

        def hosted(*refs):
            ins, xi = refs[:n_in], refs[n_in:n_in + n_x]
            outs, xo = refs[n_in + n_x:n_in + n_x + n_out], refs[n_in + n_x + n_out:n_in + 2 * n_x + n_out]
            scr, xs = refs[n_in + 2 * n_x + n_out:n_in + 2 * n_x + n_out + n_scr], refs[n_in + 2 * n_x + n_out + n_scr:]
            first = functools.reduce(jnp.logical_and, [pl.program_id(d) == 0 for d in range(len(grid))])
            last = functools.reduce(jnp.logical_and, [pl.program_id(d) == grid[d] - 1 for d in range(len(grid))])

            @pl.when(first)
            def _():
                _exchange_start(parts, xi, xo, xs)

            body(*ins, *outs, *scr)

            @pl.when(last)
            def _():
                _exchange_finish(parts, xi, xo, xs)

        res = pl.pallas_call(
            hosted, name=name, grid=grid, in_specs=list(in_specs) + [_ANY] * n_x,
            out_specs=list(out_specs) + [_ANY] * n_x, out_shape=list(out_shape) + x_out,
            scratch_shapes=list(scratch_shapes) + x_sem, compiler_params=_params(*["arbitrary"] * len(grid)),
        )(*args, *x_in)
        _set_results(parts, res[n_out:])
        return list(res[:n_out])

    return run


GROUPS = {"F1": ["ffn1_w_gate", "ffn1_w_up", "ffn1_w_down"], "MX": ["w_in", "s5_w_glu", "w_out"],
          "F2": ["ffn2_w_gate", "ffn2_w_up", "ffn2_w_down"]}
FIRST_GATHER = [(0, "ffn1_w_gate")]
GATHER_HOSTS = {
    "l0_ffn1_gate": [(0, "ffn1_w_up")],
    "l0_ffn1_up": [(0, "ffn1_w_down")],
    "l0_ffn1_down": [(0, "w_in"), (0, "s5_w_glu"), (0, "w_out")],
    "l0_mix_attn": [(0, "ffn2_w_up"), (0, "ffn2_w_down"), (1, "w_in")],
    "l0_mix_s5scan": [(0, "ffn2_w_gate")],
    "l0_wout": [(1, "s5_w_glu"), (1, "w_out")],
    "l0_ffn2_up": [(1, "ffn1_w_gate")],
    "l0_ffn2_down": [(1, "ffn1_w_up")],
    "l1_ffn1_up": [(1, "ffn1_w_down")],
    "l1_mix_attn": [(1, "ffn2_w_up"), (1, "ffn2_w_down")],
    "l1_mix_s5scan": [(1, "ffn2_w_gate")],
}
SCATTER_HOSTS = {
    "l1_ffn2_dact": [(1, "ffn2_w_down")],
    "l1_mix_attndq": [(1, "ffn2_w_up")],
    "l1_mix_attndkv": [(1, "w_out"), (1, "s5_w_glu"), (1, "ffn2_w_gate")],
    "l1_ffn1_dact": [(1, "ffn1_w_down")],
    "l1_ffn1_dwgu": [(1, "w_in")],
    "l0_ffn2_dact": [(1, "ffn1_w_up")],
    "l0_ffn2_dwgu": [(0, "ffn2_w_down")],
    "l0_mix_attndq": [(0, "w_out"), (0, "s5_w_glu"), (0, "ffn2_w_up")],
    "l0_mix_attndkv": [(0, "ffn2_w_gate"), (1, "ffn1_w_gate")],
    "l0_mix_dh1": [(0, "w_in")],
    "l0_ffn1_dact": [(0, "ffn1_w_down")],
    "l0_ffn1_dh": [(0, "ffn1_w_gate")],
}
LAST_SCATTER = [(0, "ffn1_w_up")]
SMALL_HOST = "l0_ffn1_dwd"
SMALL_PACK_ORDER = [n for n in SMALL_NAMES if n != "conv_w"] + ["conv_w"]
TAIL_HOST = "l0_ffn1_dwgu"
LATE_SCATTER_HOST = "l0_ffn1_dh"
LAST_HOST = "adamw_ffn2"


def _sharded_rows(name, a):
    return jnp.swapaxes(a, 1, 2) if name in COLUMN_SHARDED else a


def _unstack_layer(st):
    _, r, c = st.shape
    return st.reshape(N_CHIPS * r, c)


def _restack_layer(g):
    r, c = g.shape
    return g.reshape(N_CHIPS, r // N_CHIPS, c)


def _adamw_layer(name, layer, w, ga, gb, m, v, bufs):
    _, r, c = w.shape
    tr = _row_tile(r)

    def body(w_ref, ga_ref, gb_ref, m_ref, v_ref, *rest):
        g_out, d_out, m_out, v_out = rest[-4:]
        g = ga_ref[...] + gb_ref[...]
        d, mm, vv = _adamw_rows(w_ref[...], g, m_ref[...], v_ref[...])
        g_out[...] = g
        d_out[...] = d
        m_out[...] = mm
        v_out[...] = vv

    full = pl.BlockSpec((None, tr, c), lambda i: (layer, i, 0))
    flat = pl.BlockSpec((tr, c), lambda i: (i, 0))
    extra = {} if bufs is None else dict(input_output_aliases={5 + k: k for k in range(4)})
    return pl.pallas_call(
        body, name=name, grid=(r // tr,),
        in_specs=[full, flat, flat, full, full] + ([] if bufs is None else [_ANY] * 4),
        out_specs=[full] * 4, out_shape=[_sds(w.shape)] * 4, compiler_params=_params("parallel"), **extra,
    )(w, ga, gb, m, v, *([] if bufs is None else bufs))


def _adamw_both_layers(name, ws, ms, vs, gas, gbs):
    nw = len(ws)
    _, r, c = ws[0].shape
    tr = _row_tile(r, 64)

    def body(*refs):
        ins, outs = refs[:7 * nw], refs[7 * nw:]
        for k in range(nw):
            w_ref, m_ref, v_ref = ins[3 * k:3 * k + 3]
            g_refs = ins[3 * nw + 4 * k:3 * nw + 4 * k + 4]
            g_out, d_out, m_out, v_out = outs[4 * k:4 * k + 4]
            for layer in range(DEPTH):
                g = g_refs[layer][...] + g_refs[DEPTH + layer][...]
                d, mm, vv = _adamw_rows(w_ref[layer], g, m_ref[layer], v_ref[layer])
                g_out[layer] = g
                d_out[layer] = d
                m_out[layer] = mm
                v_out[layer] = vv

    both = pl.BlockSpec((DEPTH, tr, c), lambda i: (0, i, 0))
    flat = pl.BlockSpec((tr, c), lambda i: (i, 0))
    wmv = [t for k in range(nw) for t in (ws[k], ms[k], vs[k])]
    gs = [t for k in range(nw) for t in (*gas[k], *gbs[k])]
    res = _call(
        body, name=name, grid=(r // tr,), in_specs=[both] * (3 * nw) + [flat] * (4 * nw),
        out_specs=[both] * (4 * nw), out_shape=[_sds(ws[0].shape)] * (4 * nw), compiler_params=_params("parallel"),
    )(*wmv, *gs)
    return [res[4 * k:4 * k + 4] for k in range(nw)]


def _train_step(x, loss_target, w, m, v):
    ix, iy, _ = _position()
    chip = 2 * ix + iy
    shard = {n: (_permute_in_cols(w[n]) if n == "w_in" else _sharded_rows(n, w[n])).astype(bf16) for n in BIG_NAMES}

    gathered = {}

    def gather_parts(keys, extra=()):
        part = _ChipGatherHalvesPart([shard[n][layer] for layer, n in keys] + list(extra))
        gathered.update({key: (part, i) for i, key in enumerate(keys)})
        return [part]

    (first,) = gather_parts(FIRST_GATHER, extra=[w["conv_w"]])
    _exchange_now("gather_first", [first])
    for host, keys in GATHER_HOSTS.items():
        _RIDERS[host] = functools.partial(gather_parts, keys)

    def weight(layer, name):
        part, i = gathered[(layer, name)]
        return _unstack_layer(part.results[i])

    small = {n: w[n] for n in SMALL_NAMES}
    small["conv_w"] = first.results[-1].transpose(1, 2, 0, 3).reshape(DEPTH, CONV_WIDTH, D_A)

    grads_full, scattered = {}, {}

    def scatter_parts(keys):
        part = _ChipScatterPart([_restack_layer(grads_full[key]) for key in keys])
        scattered.update({key: (part, i) for i, key in enumerate(keys)})
        return [part]

    for host, keys in SCATTER_HOSTS.items():
        _RIDERS[host] = functools.partial(scatter_parts, keys)

    partial = {}

    def reduce_chips(keys):
        for layer, n in keys:
            part, i = scattered[(layer, n)]
            p = _sum_stack(f"sum_l{layer}_{n}", part.results[i])
            partial[(layer, n)] = _unpermute_in_cols(p) if n == "w_in" else p

    early = [key for host, keys in SCATTER_HOSTS.items() if host != LATE_SCATTER_HOST for key in keys]
    late = SCATTER_HOSTS[LATE_SCATTER_HOST]
    tail = {}

    def small_parts():
        tail["small"] = _ChipGatherPart([_pack([tail["gsmall"][n] for n in SMALL_PACK_ORDER])])
        return [tail["small"]]

    def tail_parts():
        reduce_chips(early)
        tail["small_sum"] = _sum_stack("sum_small", tail["small"].results[0])
        tail["swap"] = _SiblingSwapPart([partial[k] for k in early] + [tail["small_sum"]])
        return [tail["swap"]]

    _RIDERS[SMALL_HOST] = small_parts
    _RIDERS[TAIL_HOST] = tail_parts
    loss_local, gx = _local_step(x[0], loss_target[0], weight, small, grads_full.__setitem__,
                                 functools.partial(tail.__setitem__, "gsmall"))
    other = dict(zip(early, tail["swap"].results[:-1]))
    small_mine, small_other = tail["small_sum"], tail["swap"].results[-1]
    grads, deltas, new_m, new_v = {}, {}, {}, {}
    reduce_chips(late)
    last_parts = scatter_parts(LAST_SCATTER) + [_SiblingSwapPart([partial[k] for k in late])]
    _RIDERS[LAST_HOST] = lambda: last_parts
    ffn2 = GROUPS["F2"]
    res = _adamw_both_layers(
        LAST_HOST, *[[_sharded_rows(n, t[n]) for n in ffn2] for t in (w, m, v)],
        [[partial[(layer, n)] for layer in range(DEPTH)] for n in ffn2],
        [[other[(layer, n)] for layer in range(DEPTH)] for n in ffn2])
    for n, bufs in zip(ffn2, res):
        grads[n], deltas[n], new_m[n], new_v[n] = (_sharded_rows(n, t) for t in bufs)
    other.update(zip(late, last_parts[1].results))
    reduce_chips(LAST_SCATTER)
    swap_last = _SiblingSwapPart([partial[k] for k in LAST_SCATTER])
    _exchange_now("swap_last", [swap_last])
    other.update(zip(LAST_SCATTER, swap_last.results))

    for n in BIG_NAMES:
        if n in ffn2:
            continue
        bufs = None
        wr, mr, vr = (_sharded_rows(n, t) for t in (w[n], m[n], v[n]))
        for layer in range(DEPTH):
            bufs = _adamw_layer(f"adamw_l{layer}_{n}", layer, wr, partial[(layer, n)], other[(layer, n)], mr, vr, bufs)
        grads[n], deltas[n], new_m[n], new_v[n] = (_sharded_rows(n, t) for t in bufs)
    packed = SMALL_PACK_ORDER[:-1]
    shapes = [w[n].shape for n in packed]
    res = _adamw("adamw_small", _pack([w[n] for n in packed]), small_mine, small_other,
                 _pack([m[n] for n in packed]), _pack([v[n] for n in packed]))
    for dst, buf in zip((grads, deltas, new_m, new_v), res):
        dst.update(zip(packed, _unpack(buf, shapes)))
    cw = D_A // N_CHIPS
    conv_shape = (DEPTH, CONV_WIDTH, D_A)
    offset = sum(math.prod(s_) for s_ in shapes)

    def conv_grad(buf):
        full = buf.reshape(-1)[offset:offset + math.prod(conv_shape)].reshape(conv_shape)
        return lax.dynamic_slice_in_dim(full, chip * cw, cw, axis=2).reshape(DEPTH * CONV_WIDTH, cw)

    rows = lambda t: t.reshape(DEPTH * CONV_WIDTH, cw)
    res = _adamw("adamw_conv_w", rows(w["conv_w"]), conv_grad(small_mine), conv_grad(small_other),
                 rows(m["conv_w"]), rows(v["conv_w"]))
    for dst, buf in zip((grads, deltas, new_m, new_v), res):
        dst["conv_w"] = buf.reshape(w["conv_w"].shape)

    loss = lax.psum(loss_local, ("x", "y", "c"))
    return (loss, gx[None], *[grads[n] for n in WEIGHT_NAMES], *[deltas[n] for n in WEIGHT_NAMES],
            *[new_m[n] for n in WEIGHT_NAMES], *[new_v[n] for n in WEIGHT_NAMES])


def kernel(x, ffn1_w_gate, ffn1_w_up, ffn1_w_down, ln1_g, ln1_b, w_in, conv_w, conv_b, rg_w_a, rg_b_a, rg_w_x, rg_b_x, rg_lambda, fox_b_f, s5_a_re, s5_a_im, s5_log_dt, s5_b_re, s5_b_im, s5_c_re, s5_c_im, s5_d, s5_w_glu, mix_norm_g, w_out, ln2_g, ln2_b, ffn2_w_gate, ffn2_w_up, ffn2_w_down, ln3_g, ln3_b, loss_target, m_ffn1_w_gate, m_ffn1_w_up, m_ffn1_w_down, m_ln1_g, m_ln1_b, m_w_in, m_conv_w, m_conv_b, m_rg_w_a, m_rg_b_a, m_rg_w_x, m_rg_b_x, m_rg_lambda, m_fox_b_f, m_s5_a_re, m_s5_a_im, m_s5_log_dt, m_s5_b_re, m_s5_b_im, m_s5_c_re, m_s5_c_im, m_s5_d, m_s5_w_glu, m_mix_norm_g, m_w_out, m_ln2_g, m_ln2_b, m_ffn2_w_gate, m_ffn2_w_up, m_ffn2_w_down, m_ln3_g, m_ln3_b, v_ffn1_w_gate, v_ffn1_w_up, v_ffn1_w_down, v_ln1_g, v_ln1_b, v_w_in, v_conv_w, v_conv_b, v_rg_w_a, v_rg_b_a, v_rg_w_x, v_rg_b_x, v_rg_lambda, v_fox_b_f, v_s5_a_re, v_s5_a_im, v_s5_log_dt, v_s5_b_re, v_s5_b_im, v_s5_c_re, v_s5_c_im, v_s5_d, v_s5_w_glu, v_mix_norm_g, v_w_out, v_ln2_g, v_ln2_b, v_ffn2_w_gate, v_ffn2_w_up, v_ffn2_w_down, v_ln3_g, v_ln3_b):
    args = dict(locals())
    w = {n: args[n] for n in WEIGHT_NAMES}
    m = {n: args["m_" + n] for n in WEIGHT_NAMES}
    v = {n: args["v_" + n] for n in WEIGHT_NAMES}
    return _train_step(x, loss_target, w, m, v)
```

```python
import functools
import math

import jax
import jax.numpy as jnp
from jax import lax
from jax.experimental import pallas as pl
from jax.experimental.pallas import tpu as pltpu

f32 = jnp.float32
bf16 = jnp.bfloat16

D_MODEL = 1024
D_FF = 2816
D_A = 384
D_B = 384
D_C = 256
N_HEADS = 6
HEAD_DIM = 64
S5_GROUPS = 16
S5_STATE = 64
S5_LANES = S5_GROUPS * S5_STATE
F_OFF = 5 * D_A
CU_OFF = F_OFF + 128
N_IN_P = CU_OFF + D_C
CONV_WIDTH = 4
DEPTH = 2
ALPHA = (2 * DEPTH) ** 0.25
LN_EPS = 1e-5
RMS_EPS = 1e-6
RG_C = 8.0
ATT_SCALE = HEAD_DIM ** -0.5
ADAM_LR, ADAM_B1, ADAM_B2, ADAM_EPS, ADAM_WD, ADAM_STEP = 0.001, 0.9, 0.999, 1e-08, 0.01, 10

ROW_TILE = 512
N_CHIPS = 4
MESH = pl.DeviceIdType.MESH

_DN = {
    "nn": (((1,), (0,)), ((), ())),
    "nt": (((1,), (1,)), ((), ())),
    "tn": (((0,), (0,)), ((), ())),
}


def _sds(shape, dtype=f32):
    return jax.ShapeDtypeStruct(shape, dtype)


def _tile(n, target):
    best = None
    for t in range(128, min(n, target) + 1, 128):
        if n % t == 0:
            best = t
    return best or n


def _row_tile(rows, target=256):
    best = None
    for t in range(16, min(rows, target) + 1, 16):
        if rows % t == 0:
            best = t
    return best or rows


def _params(*sem):
    return pltpu.CompilerParams(dimension_semantics=sem)


class _Slabs:
    def __init__(self, x):
        self.x = x


class _KPart:
    def __init__(self, x, j):
        self.x, self.j = x, j


FF_SLAB = D_FF // N_CHIPS
FFN_ROWS = 1024

def _mm(name, mode, dims, tiles, a_list, b_list, pairs, n_acc, epilogue, outs, extras=(), vecs=(), split_cols=False):
    m, n, k = dims
    tm, tn, tk = tiles
    nk = k // tk
    na, nb, ne, nv, no = len(a_list), len(b_list), len(extras), len(vecs), len(outs)

    def body(*refs):
        a_refs = refs[:na]
        b_refs = refs[na:na + nb]
        e_refs = refs[na + nb:na + nb + ne]
        v_refs = refs[na + nb + ne:na + nb + ne + nv]
        o_refs = refs[na + nb + ne + nv:na + nb + ne + nv + no]
        acc_refs = refs[na + nb + ne + nv + no:]
        a_vals = [r[...].astype(bf16) for r in a_refs]
        b_vals = [r[...].astype(bf16) for r in b_refs]
        products = [(ci, lax.dot_general(a_vals[ai], b_vals[bi], _DN[mode], preferred_element_type=f32))
                    for ai, bi, ci in pairs]

        def finish(accs):
            res = epilogue(accs, [e[...] for e in e_refs], [v[...] for v in v_refs])
            for o, r in zip(o_refs, res):
                o[...] = r.astype(o.dtype)

        if nk == 1:
            accs = [None] * n_acc
            for ci, prod in products:
                accs[ci] = prod if accs[ci] is None else accs[ci] + prod
            finish(accs)
            return
        kk = pl.program_id(2)

        @pl.when(kk == 0)
        def _():
            for acc in acc_refs:
                acc[...] = jnp.zeros_like(acc)

        for ci, prod in products:
            acc_refs[ci][...] += prod

        @pl.when(kk == nk - 1)
        def _():
            finish([acc[...] for acc in acc_refs])

    def a_spec(a):
        if isinstance(a, _KPart):
            return pl.BlockSpec((None, tm, tk), lambda i, j, kk, part=a.j: (part, i, 0))
        if isinstance(a, _Slabs):
            if mode == "tn":
                return pl.BlockSpec((None, tk, tm), lambda i, j, kk: (i, kk, 0))
            return pl.BlockSpec((None, tm, tk), lambda i, j, kk: (kk, i, 0))
        if mode == "tn":
            return pl.BlockSpec((tk, tm), lambda i, j, kk: (kk, i))
        return pl.BlockSpec((tm, tk), lambda i, j, kk: (i, kk))

    def b_spec(b):
        if isinstance(b, _KPart):
            return pl.BlockSpec((tk, tn), lambda i, j, kk, part=b.j: (part, j))
        if isinstance(b, _Slabs):
            if mode == "nt":
                return pl.BlockSpec((None, tn, tk), lambda i, j, kk: (kk, j, 0))
            return pl.BlockSpec((None, tk, tn), lambda i, j, kk: (j, kk, 0))
        if mode == "nt":
            return pl.BlockSpec((tn, tk), lambda i, j, kk: (j, kk))
        return pl.BlockSpec((tk, tn), lambda i, j, kk: (kk, j))

    o_spec = pl.BlockSpec((tm, tn), lambda i, j, kk: (i, j))
    o_slab_spec = pl.BlockSpec((None, tm, tn), lambda i, j, kk: (j, i, 0))
    v_spec = pl.BlockSpec((1, tn), lambda i, j, kk: (0, j))
    if split_cols:
        out_specs = [o_slab_spec] * no
        out_shape = [_sds((n // tn, m, tn), dt) for dt in outs]
    else:
        out_specs = [o_spec] * no
        out_shape = [_sds((m, n), dt) for dt in outs]
    raw = lambda t: t.x if isinstance(t, (_Slabs, _KPart)) else t
    res = _call(
        body,
        name=name,
        grid=(m // tm, n // tn, nk),
        in_specs=([a_spec(a) for a in a_list] + [b_spec(b) for b in b_list]
                  + [o_slab_spec if isinstance(e, _Slabs) else o_spec for e in extras] + [v_spec] * nv),
        out_specs=out_specs,
        out_shape=out_shape,
        scratch_shapes=[pltpu.VMEM((tm, tn), f32)] * (n_acc if nk > 1 else 0),
        compiler_params=_params("parallel", "parallel", "arbitrary"),
    )(*map(raw, a_list), *map(raw, b_list), *map(raw, extras), *vecs)
    return res


def _sigmoid(x):
    return 0.5 * (jnp.tanh(0.5 * x) + 1.0)


def _layer_norm_rows(r, gamma, beta):
    mu = jnp.mean(r, axis=-1, keepdims=True)
    xc = r - mu
    var = jnp.mean(xc * xc, axis=-1, keepdims=True)
    return xc * lax.rsqrt(var + LN_EPS) * gamma + beta


def _mm_plain(name, mode, a, b, dims, scale=1.0, out_dtype=f32, add=None, add_coef=1.0, tiles=None):
    m, n, k = dims
    tiles = tiles or (_tile(m, 512), _tile(n, 1024), _tile(k, 1024))

    def epilogue(accs, extras, vecs):
        r = accs[0] if scale == 1.0 else accs[0] * scale
        if extras:
            r = r + add_coef * extras[0]
        return [r]

    return _mm(name, mode, dims, tiles, [a], [b], [(0, 0, 0)], 1, epilogue, [out_dtype],
               extras=[] if add is None else [add])[0]


def _ffn_up(name, h, wg, wu):
    s = h.shape[0]

    def epilogue(accs, extras, vecs):
        g, u = accs
        return [g, u, g * _sigmoid(g) * u]

    return _mm(name, "nt", (s, D_FF, D_MODEL), (_tile(s, FFN_ROWS), FF_SLAB, D_MODEL), [h], [wg, wu],
               [(0, 0, 0), (0, 1, 1)], 2, epilogue, [bf16, bf16, bf16], split_cols=True)


def _ffn_gate(name, h, wg):
    s = h.shape[0]
    return _mm(name, "nt", (s, D_FF, D_MODEL), (_tile(s, FFN_ROWS), FF_SLAB, D_MODEL), [h], [wg], [(0, 0, 0)], 1,
               lambda accs, extras, vecs: [accs[0]], [bf16], split_cols=True)[0]


def _ffn_up_given_gate(name, h, wu, g):
    s = h.shape[0]

    def epilogue(accs, extras, vecs):
        gg = extras[0].astype(f32)
        return [accs[0], gg * _sigmoid(gg) * accs[0]]

    return _mm(name, "nt", (s, D_FF, D_MODEL), (_tile(s, FFN_ROWS), FF_SLAB, D_MODEL), [h], [wu], [(0, 0, 0)], 1,
               epilogue, [bf16, bf16], extras=[_Slabs(g)], split_cols=True)


def _mm_ln(name, a, w, resid, gamma, beta, scale, k_slabs=False):
    def epilogue(accs, extras, vecs):
        r = ALPHA * extras[0] + scale * accs[0]
        ln = _layer_norm_rows(r, vecs[0], vecs[1])
        return [r, ln, ln]

    if k_slabs:
        n_slabs, s, slab = a.shape
        return _mm(name, "nn", (s, D_MODEL, slab), (_tile(s, 512), D_MODEL, slab),
                   [_KPart(a, j) for j in range(n_slabs)], [_KPart(w, j) for j in range(n_slabs)],
                   [(j, j, 0) for j in range(n_slabs)], 1, epilogue, [f32, f32, bf16], extras=[resid],
                   vecs=[gamma, beta])
    s, k = a.shape
    return _mm(name, "nn", (s, D_MODEL, k), (_tile(s, FFN_ROWS), D_MODEL, _tile(k, 1024)),
               [a], [w], [(0, 0, 0)], 1, epilogue, [f32, f32, bf16], extras=[resid], vecs=[gamma, beta])


def _ffn_dact(name, dr, wd, g, u):
    s = dr.shape[0]

    def epilogue(accs, extras, vecs):
        da = 0.5 * accs[0]
        gg, uu = extras[0].astype(f32), extras[1].astype(f32)
        sg = _sigmoid(gg)
        return [da * uu * (sg * (1.0 + gg * (1.0 - sg))), da * (gg * sg)]

    return _mm(name, "nt", (s, D_FF, D_MODEL), (_tile(s, FFN_ROWS), FF_SLAB, D_MODEL), [dr], [wd],
               [(0, 0, 0)], 1, epilogue, [bf16, bf16], extras=[_Slabs(g), _Slabs(u)], split_cols=True)


def _mm2(name, mode, dims, a0, b0, a1, b1, add=None, add_coef=1.0, separate=False, tiles=None, out_dtype=f32,
         split_cols=False):
    m, n, k = dims
    tiles = tiles or (_tile(m, 512), _tile(n, 1024), _tile(k, 1024))

    def epilogue(accs, extras, vecs):
        if separate:
            return list(accs)
        r = accs[0]
        if extras:
            r = r + add_coef * extras[0]
        return [r]

    a_list = [a0] if a1 is None else [a0, a1]
    b_list = [b0] if b1 is None else [b0, b1]
    pairs = [(0, 0, 0), (len(a_list) - 1, len(b_list) - 1, 1 if separate else 0)]
    return _mm(name, mode, dims, tiles, a_list, b_list, pairs, 2 if separate else 1, epilogue,
               [out_dtype, out_dtype] if separate else [out_dtype], extras=[] if add is None else [add],
               split_cols=split_cols)


def _row_call(name, body, s, ins, params, outs, accs):
    tm = ROW_TILE
    ins = [a if isinstance(a, tuple) else (a, a.shape[1], 0) for a in ins]
    in_specs = [pl.BlockSpec((tm, width), lambda i, cb=cb: (i, cb)) for _, width, cb in ins]
    ins = [a for a, _, _ in ins]
    in_specs += [pl.BlockSpec(p.shape, lambda i, nd=p.ndim: (0,) * nd) for p in params]
    out_specs = [pl.BlockSpec((tm, o.shape[1]), lambda i: (i, 0)) for o in outs]
    out_specs += [pl.BlockSpec(a.shape, lambda i, nd=len(a.shape): (0,) * nd) for a in accs]
    return pl.pallas_call(
        body,
        name=name,
        grid=(s // tm,),
        in_specs=in_specs,
        out_specs=out_specs,
        out_shape=list(outs) + list(accs),
        compiler_params=_params("arbitrary"),
    )(*ins, *params)


def _zero_at_first(refs):
    @pl.when(pl.program_id(0) == 0)
    def _():
        for r in refs:
            r[...] = jnp.zeros_like(r)


def _ln_bwd(name, r, dh, gamma):
    s = r.shape[0]

    def body(r_ref, dh_ref, g_ref, dr_ref, drb_ref, dg_ref, db_ref):
        _zero_at_first([dg_ref, db_ref])
        rr = r_ref[...]
        dy = dh_ref[...]
        mu = jnp.mean(rr, axis=-1, keepdims=True)
        xc = rr - mu
        rstd = lax.rsqrt(jnp.mean(xc * xc, axis=-1, keepdims=True) + LN_EPS)
        xhat = xc * rstd
        dxh = dy * g_ref[...]
        dr = rstd * (dxh - jnp.mean(dxh, axis=-1, keepdims=True) - xhat * jnp.mean(dxh * xhat, axis=-1, keepdims=True))
        dr_ref[...] = dr
        drb_ref[...] = dr.astype(bf16)
        dg_ref[...] += jnp.sum(dy * xhat, axis=0, keepdims=True)
        db_ref[...] += jnp.sum(dy, axis=0, keepdims=True)

    return _row_call(name, body, s, [r, dh], [gamma], [_sds((s, D_MODEL)), _sds((s, D_MODEL), bf16)],
                     [_sds((1, D_MODEL)), _sds((1, D_MODEL))])


def _loss_head(name, y, target):
    s = y.shape[0]

    def body(y_ref, t_ref, dy_ref, l_ref):
        _zero_at_first([l_ref])
        e = y_ref[...] - t_ref[...]
        dy_ref[...] = e / D_MODEL
        l_ref[...] += 0.5 * jnp.sum(jnp.mean(e * e, axis=-1, keepdims=True), axis=0, keepdims=True)

    return _row_call(name, body, s, [y, target], [], [_sds((s, D_MODEL))], [_sds((1, 128))])


def _expm1(x):
    series = x * (1.0 + x / 2.0 * (1.0 + x / 3.0 * (1.0 + x / 4.0 * (1.0 + x / 5.0 * (1.0 + x / 6.0 * (1.0 + x / 7.0))))))
    return jnp.where(jnp.abs(x) < 0.25, series, jnp.exp(x) - 1.0)


def _gates_fn(xa, wa, wx, ba, bx, lam, tap_a, tap_x):
    xb = xa.astype(bf16)
    r = jax.nn.sigmoid(jnp.dot(xb, wa, preferred_element_type=f32) + ba + tap_a)
    i = jax.nn.sigmoid(jnp.dot(xb, wx, preferred_element_type=f32) + bx + tap_x)
    log_a = -RG_C * r * jax.nn.softplus(-lam)
    a = jnp.exp(log_a)
    gated = jnp.sqrt(-_expm1(2.0 * log_a)) * (i * xa)
    return a, gated


def _rg_gates(name, xa, wa, wx, ba, bx, lam):
    s = xa.shape[0]

    def body(xa_ref, wa_ref, wx_ref, ba_ref, bx_ref, lam_ref, a_ref, g_ref):
        a, g = _gates_fn(xa_ref[...], wa_ref[...], wx_ref[...], ba_ref[...], bx_ref[...], lam_ref[...], 0.0, 0.0)
        a_ref[...] = a
        g_ref[...] = g

    return _row_call(name, body, s, [xa], [wa, wx, ba, bx, lam], [_sds((s, D_A)), _sds((s, D_A))], [])


def _rg_gates_bwd(name, xa, ga, h_prev, wa, wx, ba, bx, lam):
    s = xa.shape[0]

    def body(xa_ref, ga_ref, hp_ref, wa_ref, wx_ref, ba_ref, bx_ref, lam_ref,
             dxa_ref, dwa_ref, dwx_ref, dba_ref, dbx_ref, dlam_ref):
        _zero_at_first([dwa_ref, dwx_ref, dba_ref, dbx_ref, dlam_ref])
        xa_v = xa_ref[...]
        zero = jnp.zeros((xa_v.shape[0], D_A), f32)
        fn = lambda x, ba_, bx_, lam_, ta, tx: _gates_fn(x, wa_ref[...], wx_ref[...], ba_, bx_, lam_, ta, tx)
        _, vjp = jax.vjp(fn, xa_v, ba_ref[...], bx_ref[...], lam_ref[...], zero, zero)
        gav = ga_ref[...]
        dxa, dba, dbx, dlam, dta, dtx = vjp((gav * hp_ref[...], gav))
        dxa_ref[...] = dxa
        xb = xa_v.astype(bf16)
        dwa_ref[...] += lax.dot_general(xb, dta.astype(bf16), _DN["tn"], preferred_element_type=f32)
        dwx_ref[...] += lax.dot_general(xb, dtx.astype(bf16), _DN["tn"], preferred_element_type=f32)
        dba_ref[...] += dba
        dbx_ref[...] += dbx
        dlam_ref[...] += dlam

    return _row_call(name, body, s, [xa, ga, h_prev], [wa, wx, ba, bx, lam], [_sds((s, D_A))],
                     [_sds((D_A, D_A)), _sds((D_A, D_A)), _sds((1, D_A)), _sds((1, D_A)), _sds((1, D_A))])


def _rms(v, g):
    return v * lax.rsqrt(jnp.mean(v * v, axis=-1, keepdims=True) + RMS_EPS) * g


def _mix_out_fn(ag, ha, ob, hre, him, cu, d, gn, tap_y, tap_gl, wcr, wci, wglu):
    out_a = jax.nn.gelu(ag) * ha
    y = (jnp.dot(hre.astype(bf16), wcr, preferred_element_type=f32)
         + jnp.dot(him.astype(bf16), wci, preferred_element_type=f32) + d * cu + tap_y)
    y2 = jax.nn.gelu(y)
    gl = jnp.dot(y2.astype(bf16), wglu, preferred_element_type=f32) + tap_gl
    out_c = y2 * jax.nn.sigmoid(gl)
    o = jnp.concatenate([_rms(out_a, gn[:, :D_A]), _rms(ob, gn[:, D_A:D_A + D_B]), _rms(out_c, gn[:, D_A + D_B:])],
                        axis=-1)
    return o, y2


def _mix_out(name, ag, ha, ob, hre, him, cu, d, gn, wcr, wci, wglu):
    s = ha.shape[0]

    def body(ag_ref, ha_ref, ob_ref, hre_ref, him_ref, cu_ref, d_ref, gn_ref, wcr_ref, wci_ref, wglu_ref, o_ref):
        o, _ = _mix_out_fn(ag_ref[...], ha_ref[...], ob_ref[...], hre_ref[...], him_ref[...], cu_ref[...], d_ref[...],
                           gn_ref[...], 0.0, 0.0, wcr_ref[...], wci_ref[...], wglu_ref[...])
        o_ref[...] = o.astype(o_ref.dtype)

    return _row_call(name, body, s, [ag, ha, ob, hre, him, cu], [d, gn, wcr, wci, wglu], [_sds((s, D_MODEL), bf16)], [])[0]


def _mix_out_bwd(name, do, ag, ha, ob, hre, him, cu, d, gn, wcr, wci, wglu):
    s = ha.shape[0]

    def body(do_ref, ag_ref, ha_ref, ob_ref, hre_ref, him_ref, cu_ref, d_ref, gn_ref, wcr_ref, wci_ref, wglu_ref,
             dag_ref, dha_ref, dob_ref, dhre_ref, dhim_ref, dcu_ref, dwcr_ref, dwci_ref, dwglu_ref, dd_ref, dgn_ref):
        _zero_at_first([dwcr_ref, dwci_ref, dwglu_ref, dd_ref, dgn_ref])
        tm = ag_ref.shape[0]
        zero = jnp.zeros((tm, D_C), f32)
        hre_v, him_v = hre_ref[...], him_ref[...]
        fn = lambda *a: _mix_out_fn(*a, wcr_ref[...], wci_ref[...], wglu_ref[...])
        _, vjp, y2 = jax.vjp(fn, ag_ref[...], ha_ref[...], ob_ref[...], hre_v, him_v, cu_ref[...], d_ref[...],
                             gn_ref[...], zero, zero, has_aux=True)
        dag, dha, dob, dhre, dhim, dcu, dd, dgn, dy, dgl = vjp(do_ref[...])
        dag_ref[...] = dag
        dha_ref[...] = dha
        dob_ref[...] = dob
        dhre_ref[...] = dhre
        dhim_ref[...] = dhim
        dcu_ref[...] = dcu
        dyb = dy.astype(bf16)
        dwcr_ref[...] += lax.dot_general(hre_v.astype(bf16), dyb, _DN["tn"], preferred_element_type=f32)
        dwci_ref[...] += lax.dot_general(him_v.astype(bf16), dyb, _DN["tn"], preferred_element_type=f32)
        dwglu_ref[...] += lax.dot_general(y2.astype(bf16), dgl.astype(bf16), _DN["tn"], preferred_element_type=f32)
        dd_ref[...] += dd
        dgn_ref[...] += dgn

    outs = [_sds((s, D_A)), _sds((s, D_A)), _sds((s, D_B)), _sds((s, S5_LANES)), _sds((s, S5_LANES)), _sds((s, D_C))]
    accs = [_sds((S5_LANES, D_C)), _sds((S5_LANES, D_C)), _sds((D_C, D_C)), _sds((1, D_C)), _sds((1, D_MODEL))]
    return _row_call(name, body, s, [do, ag, ha, ob, hre, him, cu], [d, gn, wcr, wci, wglu], outs, accs)


def _log_f(name, f, bf):
    s = f[0].shape[0]

    def body(f_ref, b_ref, o_ref):
        o_ref[...] = jax.nn.log_sigmoid(f_ref[...] + b_ref[...])

    return _row_call(name, body, s, [f], [bf], [_sds((s, 128))], [])[0]


def _log_f_bwd(name, dlf, f, bf):
    s = dlf.shape[0]

    def body(dl_ref, f_ref, b_ref, df_ref, db_ref):
        _zero_at_first([db_ref])
        df = dl_ref[...] * jax.nn.sigmoid(-(f_ref[...] + b_ref[...]))
        df_ref[...] = df
        db_ref[...] += jnp.sum(df, axis=0, keepdims=True)

    return _row_call(name, body, s, [dlf, f], [bf], [_sds((s, 128))], [_sds((1, 128))])


def _s5_decay_grad(name, h_re, h_im, g_re, g_im):
    s = g_re.shape[0]
    tm = ROW_TILE

    def body(hr_ref, hi_ref, hhr_ref, hhi_ref, gr_ref, gi_ref, dr_ref, di_ref):
        i = pl.program_id(0)
        _zero_at_first([dr_ref, di_ref])

        def previous(h_ref, halo_ref):
            halo = jnp.where(i == 0, 0.0, halo_ref[...])
            return pltpu.roll(jnp.concatenate([halo, h_ref[...]], axis=0), 1, 0)[8:, :]

        hr, hi, gr, gi = previous(hr_ref, hhr_ref), previous(hi_ref, hhi_ref), gr_ref[...], gi_ref[...]
        dr_ref[...] += jnp.sum(hr * gr + hi * gi, axis=0, keepdims=True)
        di_ref[...] += jnp.sum(hr * gi - hi * gr, axis=0, keepdims=True)

    rows = pl.BlockSpec((tm, S5_LANES), lambda i: (i, 0))
    halo = pl.BlockSpec((8, S5_LANES), lambda i: (jnp.maximum(i * (tm // 8) - 1, 0), 0))
    acc = pl.BlockSpec((1, S5_LANES), lambda i: (0, 0))
    return pl.pallas_call(
        body,
        name=name,
        grid=(s // tm,),
        in_specs=[rows, rows, halo, halo, rows, rows],
        out_specs=[acc, acc],
        out_shape=[_sds((1, S5_LANES)), _sds((1, S5_LANES))],
        compiler_params=_params("arbitrary"),
    )(h_re, h_im, h_re, h_im, g_re, g_im)


def _conv_fwd(name, ax, w, b):
    s = ax.shape[0]
    tm = ROW_TILE

    def body(x_ref, halo_ref, w_ref, b_ref, o_ref):
        i = pl.program_id(0)
        x = x_ref[...]
        halo = jnp.where(i == 0, 0.0, halo_ref[...])
        ext = jnp.concatenate([halo, x], axis=0)
        acc = b_ref[...] + w_ref[3:4, :] * x
        for k in range(CONV_WIDTH - 1):
            acc = acc + w_ref[k:k + 1, :] * pltpu.roll(ext, CONV_WIDTH - 1 - k, 0)[8:, :]
        o_ref[...] = acc

    return pl.pallas_call(
        body,
        name=name,
        grid=(s // tm,),
        in_specs=[pl.BlockSpec((tm, D_A), lambda i: (i, 0)),
                  pl.BlockSpec((8, D_A), lambda i: (jnp.maximum(i * (tm // 8) - 1, 0), 0)),
                  pl.BlockSpec((CONV_WIDTH, D_A), lambda i: (0, 0)),
                  pl.BlockSpec((1, D_A), lambda i: (0, 0))],
        out_specs=pl.BlockSpec((tm, D_A), lambda i: (i, 0)),
        out_shape=_sds((s, D_A)),
        compiler_params=_params("arbitrary"),
    )(ax, ax, w, b)


def _conv_bwd(name, dxa, ax, w):
    s = ax.shape[0]
    tm = ROW_TILE
    nblk = s // tm

    def body(dx_ref, dnext_ref, x_ref, halo_ref, w_ref, dax_ref, dw_ref):
        i = pl.program_id(0)
        _zero_at_first([dw_ref])
        dx = dx_ref[...]
        dnext = jnp.where(i == nblk - 1, 0.0, dnext_ref[...])
        dext = jnp.concatenate([dx, dnext], axis=0)
        x = x_ref[...]
        halo = jnp.where(i == 0, 0.0, halo_ref[...])
        ext = jnp.concatenate([halo, x], axis=0)
        acc = w_ref[3:4, :] * dx
        dw_ref[3:4, :] += jnp.sum(dx * x, axis=0, keepdims=True)
        for k in range(CONV_WIDTH - 1):
            sh = CONV_WIDTH - 1 - k
            acc = acc + w_ref[k:k + 1, :] * pltpu.roll(dext, tm + 8 - sh, 0)[:tm, :]
            dw_ref[k:k + 1, :] += jnp.sum(dx * pltpu.roll(ext, sh, 0)[8:, :], axis=0, keepdims=True)
        dw_ref[4:5, :] += jnp.sum(dx, axis=0, keepdims=True)
        dax_ref[...] = acc

    return pl.pallas_call(
        body,
        name=name,
        grid=(nblk,),
        in_specs=[pl.BlockSpec((tm, D_A), lambda i: (i, 0)),
                  pl.BlockSpec((8, D_A), lambda i: (jnp.minimum((i + 1) * (tm // 8), s // 8 - 1), 0)),
                  pl.BlockSpec((tm, D_A), lambda i: (i, 0)),
                  pl.BlockSpec((8, D_A), lambda i: (jnp.maximum(i * (tm // 8) - 1, 0), 0)),
                  pl.BlockSpec((CONV_WIDTH, D_A), lambda i: (0, 0))],
        out_specs=[pl.BlockSpec((tm, D_A), lambda i: (i, 0)), pl.BlockSpec((8, D_A), lambda i: (0, 0))],
        out_shape=[_sds((s, D_A)), _sds((8, D_A))],
        compiler_params=_params("arbitrary"),
    )(dxa, dxa, ax, ax, w)


SCAN_ROWS = 512


def _row_in_tile(shape):
    return lax.broadcasted_iota(jnp.int32, shape, 0) % 8


def _lin_scan(name, a, b, reverse):
    s, c = a.shape
    t = min(SCAN_ROWS, s)
    nb = s // t

    def body(a_ref, b_ref, h_ref, p_ref, carry_ref):
        @pl.when(pl.program_id(0) == 0)
        def _():
            carry_ref[...] = jnp.zeros_like(carry_ref)

        row = _row_in_tile((t, c))
        p = a_ref[...]
        h = b_ref[...]
        for d in (1, 2, 4):
            keep = (row < 8 - d) if reverse else (row >= d)
            shift = (t - d) if reverse else d
            h = h + jnp.where(keep, p * pltpu.roll(h, shift, 0), 0.0)
            p = jnp.where(keep, p * pltpu.roll(p, shift, 0), p)
        h_ref[...] = h
        p_ref[...] = p
        edge = 0 if reverse else 7

        def tile(k, carry):
            kk = (t // 8 - 1 - k) if reverse else k
            r0 = pl.multiple_of(kk * 8, 8)
            hh = h_ref[pl.ds(r0, 8), :] + p_ref[pl.ds(r0, 8), :] * carry
            h_ref[pl.ds(r0, 8), :] = hh
            return jnp.broadcast_to(hh[edge:edge + 1, :], (8, c))

        carry_ref[...] = lax.fori_loop(0, t // 8, tile, carry_ref[...])

    spec = pl.BlockSpec((t, c), (lambda i: (nb - 1 - i, 0)) if reverse else (lambda i: (i, 0)))
    (out,) = _call(
        body,
        name=name,
        grid=(nb,),
        in_specs=[spec, spec],
        out_specs=[spec],
        out_shape=[_sds((s, c))],
        scratch_shapes=[pltpu.VMEM((t, c), f32), pltpu.VMEM((8, c), f32)],
        compiler_params=_params("arbitrary"),
    )(a, b)
    return out


def _s5_scan(name, b_re, b_im, a_re, a_im, reverse):
    s, c = b_re.shape
    t = min(SCAN_ROWS, s)
    nb = s // t

    def body(br_ref, bi_ref, ar_ref, ai_ref, hr_ref, hi_ref, cr_ref, ci_ref):
        @pl.when(pl.program_id(0) == 0)
        def _():
            cr_ref[...] = jnp.zeros_like(cr_ref)
            ci_ref[...] = jnp.zeros_like(ci_ref)

        ar1, ai1 = ar_ref[...], ai_ref[...]
        pows = [(ar1, ai1)]
        for _ in range(7):
            pr, pi = pows[-1]
            pows.append((pr * ar1 - pi * ai1, pr * ai1 + pi * ar1))
        row8 = lax.broadcasted_iota(jnp.int32, (8, c), 0)
        wr = jnp.zeros((8, c), f32)
        wi = jnp.zeros((8, c), f32)
        for r in range(8):
            pr, pi = pows[(7 - r) if reverse else r]
            wr = jnp.where(row8 == r, pr, wr)
            wi = jnp.where(row8 == r, pi, wi)
        row = _row_in_tile((t, c))
        hr = br_ref[...]
        hi = bi_ref[...]
        for d in (1, 2, 4):
            keep = (row < 8 - d) if reverse else (row >= d)
            shift = (t - d) if reverse else d
            pr, pi = pows[d - 1]
            cr = jnp.where(keep, pr, 0.0)
            ci = jnp.where(keep, pi, 0.0)
            sr = pltpu.roll(hr, shift, 0)
            si = pltpu.roll(hi, shift, 0)
            hr, hi = hr + cr * sr - ci * si, hi + cr * si + ci * sr
        hr_ref[...] = hr
        hi_ref[...] = hi
        edge = 0 if reverse else 7

        def tile(k, carry):
            car_r, car_i = carry
            kk = (t // 8 - 1 - k) if reverse else k
            r0 = pl.multiple_of(kk * 8, 8)
            xr = hr_ref[pl.ds(r0, 8), :] + wr * car_r - wi * car_i
            xi = hi_ref[pl.ds(r0, 8), :] + wr * car_i + wi * car_r
            hr_ref[pl.ds(r0, 8), :] = xr
            hi_ref[pl.ds(r0, 8), :] = xi
            return (jnp.broadcast_to(xr[edge:edge + 1, :], (8, c)), jnp.broadcast_to(xi[edge:edge + 1, :], (8, c)))

        car_r, car_i = lax.fori_loop(0, t // 8, tile, (cr_ref[...], ci_ref[...]))
        cr_ref[...] = car_r
        ci_ref[...] = car_i

    spec = pl.BlockSpec((t, c), (lambda i: (nb - 1 - i, 0)) if reverse else (lambda i: (i, 0)))
    vspec = pl.BlockSpec((1, c), lambda i: (0, 0))
    hr, hi = _call(
        body,
        name=name,
        grid=(nb,),
        in_specs=[spec, spec, vspec, vspec],
        out_specs=[spec, spec],
        out_shape=[_sds((s, c)), _sds((s, c))],
        scratch_shapes=[pltpu.VMEM((8, c), f32), pltpu.VMEM((8, c), f32)],
        compiler_params=_params("arbitrary"),
    )(b_re, b_im, a_re, a_im)
    return hr, hi


ATT_FEAT = 128
ATT_TQ = 1024
ATT_TK = 1024
ATT_TK_KEY_SIDE = 1024


def _att_tiles(s, key_side=False):
    tq = min(ATT_TQ, s)
    tk = min(ATT_TK_KEY_SIDE if key_side else ATT_TK, tq)
    return tq, tk, tq // tk


def _keys_le_queries(tk, tq, k0, q0):
    row = lax.broadcasted_iota(jnp.int32, (tk, tq), 0) + k0
    col = lax.broadcasted_iota(jnp.int32, (tk, tq), 1) + q0
    return row <= col


def _attn_fwd_t(name, qt, k_aug, vt):
    h, s, _ = k_aug.shape
    tq, tk, ratio = _att_tiles(s)

    def body(qt_ref, k_ref, vt_ref, o_ref, lse_ref):
        qi = pl.program_id(1)
        qt = qt_ref[...]

        def block(kb, carry, masked):
            m, l, acc = carry
            ks = pl.multiple_of(kb * tk, tk)
            st = jnp.dot(k_ref[pl.ds(ks, tk), :], qt, preferred_element_type=f32)
            if masked:
                st = jnp.where(_keys_le_queries(tk, tq, ks, qi * tq), st, -jnp.inf)
            mn = jnp.maximum(m, jnp.max(st, axis=0, keepdims=True))
            p = jnp.exp(st - mn)
            al = jnp.exp(m - mn)
            l = al * l + jnp.sum(p, axis=0, keepdims=True)
            acc = al * acc + jnp.dot(vt_ref[kb], p.astype(bf16), preferred_element_type=f32)
            return mn, l, acc

        init = (jnp.full((1, tq), -jnp.inf, f32), jnp.zeros((1, tq), f32), jnp.zeros((HEAD_DIM, tq), f32))
        first = lax.fori_loop(0, qi * ratio, lambda kb, c: block(kb, c, False), init)
        m, l, acc = lax.fori_loop(qi * ratio, (qi + 1) * ratio, lambda kb, c: block(kb, c, True), first)
        o_ref[...] = acc / l
        lse_ref[...] = m + jnp.log(l)

    return _call(
        body,
        name=name,
        grid=(h, s // tq),
        in_specs=[pl.BlockSpec((None, None, ATT_FEAT, tq), lambda hh, i: (hh, i, 0, 0)),
                  pl.BlockSpec((None, s, ATT_FEAT), lambda hh, i: (hh, 0, 0)),
                  pl.BlockSpec((None, s // tk, HEAD_DIM, tk), lambda hh, i: (hh, 0, 0, 0))],
        out_specs=[pl.BlockSpec((None, HEAD_DIM, tq), lambda hh, i: (hh, 0, i)),
                   pl.BlockSpec((None, 1, tq), lambda hh, i: (hh, 0, i))],
        out_shape=[_sds((h, HEAD_DIM, s)), _sds((h, 1, s))],
        compiler_params=_params("parallel", "arbitrary"),
    )(qt, k_aug, vt)


def _attn_bwd_dq_t(name, qt, k_aug, v, kt, ot, dot_, lse):
    h, s, _ = k_aug.shape
    tq, tk, ratio = _att_tiles(s)

    def body(qt_ref, k_ref, v_ref, kt_ref, o_ref, do_ref, lse_ref, dq_ref, dl_ref):
        qi = pl.program_id(1)
        qt = qt_ref[...]
        dob = do_ref[...]
        delta = jnp.sum(dob.astype(f32) * o_ref[...], axis=0, keepdims=True)
        lse_v = lse_ref[...]

        def block(kb, carry, masked):
            dq, psum = carry
            ks = pl.multiple_of(kb * tk, tk)
            st = jnp.dot(k_ref[pl.ds(ks, tk), :], qt, preferred_element_type=f32)
            p = jnp.exp(st - lse_v)
            if masked:
                p = jnp.where(_keys_le_queries(tk, tq, ks, qi * tq), p, 0.0)
            dp = jnp.dot(v_ref[pl.ds(ks, tk), :], dob, preferred_element_type=f32)
            ds = p * (dp - delta)
            return (dq + jnp.dot(kt_ref[kb], ds.astype(bf16), preferred_element_type=f32),
                    psum + jnp.sum(p * dp, axis=0, keepdims=True))

        carry = lax.fori_loop(0, qi * ratio, lambda kb, c: block(kb, c, False),
                              (jnp.zeros((HEAD_DIM, tq), f32), jnp.zeros((1, tq), f32)))
        dq, psum = lax.fori_loop(qi * ratio, (qi + 1) * ratio, lambda kb, c: block(kb, c, True), carry)
        dq_ref[...] = dq * ATT_SCALE
        dl_ref[...] = psum

    qspec = pl.BlockSpec((None, HEAD_DIM, tq), lambda hh, i: (hh, 0, i))
    rspec = pl.BlockSpec((None, 1, tq), lambda hh, i: (hh, 0, i))
    return _call(
        body,
        name=name,
        grid=(h, s // tq),
        in_specs=[pl.BlockSpec((None, None, ATT_FEAT, tq), lambda hh, i: (hh, i, 0, 0)),
                  pl.BlockSpec((None, s, ATT_FEAT), lambda hh, i: (hh, 0, 0)),
                  pl.BlockSpec((None, s, HEAD_DIM), lambda hh, i: (hh, 0, 0)),
                  pl.BlockSpec((None, s // tk, HEAD_DIM, tk), lambda hh, i: (hh, 0, 0, 0)),
                  qspec, pl.BlockSpec((None, None, HEAD_DIM, tq), lambda hh, i: (hh, i, 0, 0)), rspec],
        out_specs=[qspec, rspec],
        out_shape=[_sds((h, HEAD_DIM, s)), _sds((h, 1, s))],
        compiler_params=_params("parallel", "arbitrary"),
    )(qt, k_aug, v, kt, ot, dot_, lse)


def _attn_bwd_dkv_t(name, qt_blocks, k_aug, v, qh, do, dot_blocks, lse, delta):
    h, s, _ = k_aug.shape
    tq, tk, ratio = _att_tiles(s, key_side=True)
    nq = s // tq

    def body(qt_ref, k_ref, v_ref, q_ref, do_ref, dot_ref, lse_ref, dl_ref, dk_ref, dv_ref, dck_ref, dsum_ref):
        kj = pl.program_id(1)
        kk = k_ref[...]
        vv = v_ref[...]
        dsum_ref[...] = jnp.zeros_like(dsum_ref)

        def block(qi, carry, masked):
            dk, dv = carry
            qs = pl.multiple_of(qi * tq, tq)
            st = jnp.dot(kk, qt_ref[qi], preferred_element_type=f32)
            p = jnp.exp(st - lse_ref[qi])
            if masked:
                p = jnp.where(_keys_le_queries(tk, tq, kj * tk, qs), p, 0.0)
            dv = dv + jnp.dot(p.astype(bf16), do_ref[pl.ds(qs, tq), :], preferred_element_type=f32)
            dp = jnp.dot(vv, dot_ref[qi], preferred_element_type=f32)
            ds = p * (dp - dl_ref[qi])
            dsum_ref[...] += ds
            dk = dk + jnp.dot(ds.astype(bf16), q_ref[pl.ds(qs, tq), :], preferred_element_type=f32)
            return dk, dv

        first = kj // ratio
        carry = block(first, (jnp.zeros((tk, HEAD_DIM), f32), jnp.zeros((tk, HEAD_DIM), f32)), True)
        dk, dv = lax.fori_loop(first + 1, nq, lambda qi, c: block(qi, c, False), carry)
        dk_ref[...] = dk
        dv_ref[...] = dv
        col = jnp.sum(dsum_ref[...], axis=1, keepdims=True)
        dck_ref[...] = -jnp.transpose(jnp.broadcast_to(col, (tk, 128)))[0:1, :]

    full = lambda shape: pl.BlockSpec((None,) + shape, lambda hh, j: (hh,) + (0,) * len(shape))
    kspec = pl.BlockSpec((None, tk, HEAD_DIM), lambda hh, j: (hh, j, 0))
    return _call(
        body,
        name=name,
        grid=(h, s // tk),
        in_specs=[full((nq, ATT_FEAT, tq)),
                  pl.BlockSpec((None, tk, ATT_FEAT), lambda hh, j: (hh, j, 0)),
                  kspec, full((s, HEAD_DIM)), full((s, HEAD_DIM)), full((nq, HEAD_DIM, tq)),
                  full((nq, 1, tq)), full((nq, 1, tq))],
        out_specs=[kspec, kspec, pl.BlockSpec((None, None, 1, tk), lambda hh, j: (hh, j, 0, 0))],
        out_shape=[_sds((h, s, HEAD_DIM)), _sds((h, s, HEAD_DIM)), _sds((h, s // tk, 1, tk))],
        scratch_shapes=[pltpu.VMEM((tk, tq), f32)],
        compiler_params=_params("parallel", "arbitrary"),
    )(qt_blocks, k_aug, v, qh, do, dot_blocks, lse, delta)


C_LANES = 128


def _selections():
    h = jnp.arange(N_HEADS)[:, None, None]
    row = jnp.arange(D_B + 3 * C_LANES)[None, :, None]
    col = jnp.arange(ATT_FEAT)[None, None, :]
    head_col = (row < D_B) & (row // HEAD_DIM == h) & (col == row % HEAD_DIM)

    def c_part(p, lane0):
        return (row == D_B + p * C_LANES + h) & (col == lane0 + p)

    c_q = c_part(0, HEAD_DIM) | c_part(1, HEAD_DIM) | c_part(2, HEAD_DIM)
    c_k = c_part(0, HEAD_DIM + 3) | c_part(1, HEAD_DIM + 3) | c_part(2, HEAD_DIM + 3)
    sel_q = (head_col | c_q).astype(bf16)
    sel_k = head_col.astype(bf16) - c_k.astype(bf16)
    sel_h = head_col[:, :D_B, :HEAD_DIM].astype(bf16)
    lane = jnp.arange(ATT_FEAT)
    ones_q = ((lane >= HEAD_DIM + 3) & (lane < HEAD_DIM + 6)).astype(f32)
    ones_k = ((lane >= HEAD_DIM) & (lane < HEAD_DIM + 3)).astype(f32)
    return dict(sel_qt=sel_q.transpose(0, 2, 1), sel_k=sel_k, sel_h=sel_h, sel_ht=sel_h.transpose(0, 2, 1),
                ones_q=ones_q.reshape(ATT_FEAT, 1), ones_k=ones_k.reshape(1, ATT_FEAT))


def _attn_prep(name, z, c, sel):
    s = z.shape[0]
    tq, tk, ratio = _att_tiles(s)

    def body(q_ref, k_ref, v_ref, c_ref, sqt_ref, sk_ref, sh_ref, sht_ref, oq_ref, ok_ref,
             qt_out, ka_out, kt_out, vt_out, v_out, qh_out):
        cv = c_ref[...]
        hi = cv.astype(bf16)
        r1 = cv - hi.astype(f32)
        mid = r1.astype(bf16)
        lo = (r1 - mid.astype(f32)).astype(bf16)
        qs = (q_ref[...] * ATT_SCALE).astype(bf16)
        kb = k_ref[...].astype(bf16)
        vb = v_ref[...].astype(bf16)
        xq = jnp.concatenate([qs, hi, mid, lo], axis=-1)
        xk = jnp.concatenate([kb, hi, mid, lo], axis=-1)
        for h in range(N_HEADS):
            qt = lax.dot_general(sqt_ref[h], xq, _DN["nt"], preferred_element_type=f32) + oq_ref[...]
            qt_out[h, 0] = qt.astype(bf16)
            ka_out[h] = (jnp.dot(xk, sk_ref[h], preferred_element_type=f32) + ok_ref[...]).astype(bf16)
            kt = lax.dot_general(sht_ref[h], kb, _DN["nt"], preferred_element_type=f32).astype(bf16)
            vt = lax.dot_general(sht_ref[h], vb, _DN["nt"], preferred_element_type=f32).astype(bf16)
            for j in range(ratio):
                kt_out[h, j] = kt[:, j * tk:(j + 1) * tk]
                vt_out[h, j] = vt[:, j * tk:(j + 1) * tk]
            v_out[h] = jnp.dot(vb, sh_ref[h], preferred_element_type=f32).astype(bf16)
            qh_out[h] = jnp.dot(qs, sh_ref[h], preferred_element_type=f32).astype(bf16)

    whole = lambda a: pl.BlockSpec(a.shape, lambda i, nd=a.ndim: (0,) * nd)
    consts = [sel["sel_qt"], sel["sel_k"], sel["sel_h"], sel["sel_ht"], sel["ones_q"], sel["ones_k"]]
    return pl.pallas_call(
        body,
        name=name,
        grid=(s // tq,),
        in_specs=[pl.BlockSpec((tq, D_B), lambda i: (i, 2)), pl.BlockSpec((tq, D_B), lambda i: (i, 3)),
                  pl.BlockSpec((tq, D_B), lambda i: (i, 4)), pl.BlockSpec((tq, C_LANES), lambda i: (i, 0))]
        + [whole(a) for a in consts],
        out_specs=[pl.BlockSpec((N_HEADS, 1, ATT_FEAT, tq), lambda i: (0, i, 0, 0)),
                   pl.BlockSpec((N_HEADS, tq, ATT_FEAT), lambda i: (0, i, 0)),
                   pl.BlockSpec((N_HEADS, ratio, HEAD_DIM, tk), lambda i: (0, i, 0, 0)),
                   pl.BlockSpec((N_HEADS, ratio, HEAD_DIM, tk), lambda i: (0, i, 0, 0)),
                   pl.BlockSpec((N_HEADS, tq, HEAD_DIM), lambda i: (0, i, 0)),
                   pl.BlockSpec((N_HEADS, tq, HEAD_DIM), lambda i: (0, i, 0))],
        out_shape=[_sds((N_HEADS, s // tq, ATT_FEAT, tq), bf16), _sds((N_HEADS, s, ATT_FEAT), bf16),
                   _sds((N_HEADS, s // tk, HEAD_DIM, tk), bf16), _sds((N_HEADS, s // tk, HEAD_DIM, tk), bf16),
                   _sds((N_HEADS, s, HEAD_DIM), bf16), _sds((N_HEADS, s, HEAD_DIM), bf16)],
        compiler_params=_params("parallel"),
    )(z, z, z, c, *consts)


def _attn_do_prep(name, dob, sel):
    s = dob.shape[0]
    tq = _att_tiles(s)[0]

    def body(do_ref, sh_ref, sht_ref, dot_out, do_out):
        db = do_ref[...].astype(bf16)
        for h in range(N_HEADS):
            dot_out[h, 0] = lax.dot_general(sht_ref[h], db, _DN["nt"], preferred_element_type=f32).astype(bf16)
            do_out[h] = jnp.dot(db, sh_ref[h], preferred_element_type=f32).astype(bf16)

    whole = lambda a: pl.BlockSpec(a.shape, lambda i, nd=a.ndim: (0,) * nd)
    return pl.pallas_call(
        body,
        name=name,
        grid=(s // tq,),
        in_specs=[pl.BlockSpec((tq, D_B), lambda i: (i, 0)), whole(sel["sel_h"]), whole(sel["sel_ht"])],
        out_specs=[pl.BlockSpec((N_HEADS, 1, HEAD_DIM, tq), lambda i: (0, i, 0, 0)),
                   pl.BlockSpec((N_HEADS, tq, HEAD_DIM), lambda i: (0, i, 0))],
        out_shape=[_sds((N_HEADS, s // tq, HEAD_DIM, tq), bf16), _sds((N_HEADS, s, HEAD_DIM), bf16)],
        compiler_params=_params("parallel"),
    )(dob, sel["sel_h"], sel["sel_ht"])


def _dz_assemble(name, dax, dag, dqt, dkh, dvh, df, dcu, sel):
    s = dax.shape[0]
    tm = _tile(s, 512)

    def body(dax_ref, dag_ref, dqt_ref, dk_ref, dv_ref, df_ref, dcu_ref, sht_ref, o_ref):
        dq = jnp.zeros((tm, D_B), f32)
        dk = jnp.zeros((tm, D_B), f32)
        dv = jnp.zeros((tm, D_B), f32)
        for h in range(N_HEADS):
            place = sht_ref[h]
            dq = dq + lax.dot_general(dqt_ref[h].astype(bf16), place, _DN["tn"], preferred_element_type=f32)
            dk = dk + jnp.dot(dk_ref[h].astype(bf16), place, preferred_element_type=f32)
            dv = dv + jnp.dot(dv_ref[h].astype(bf16), place, preferred_element_type=f32)
        pieces = [dax_ref[...], dag_ref[...], dq, dk, dv, df_ref[...], dcu_ref[...]]
        off = 0
        for p in pieces:
            o_ref[:, off:off + p.shape[1]] = p.astype(bf16)
            off += p.shape[1]

    rows = lambda c_: pl.BlockSpec((tm, c_), lambda i: (i, 0))
    heads = pl.BlockSpec((N_HEADS, tm, HEAD_DIM), lambda i: (0, i, 0))
    return pl.pallas_call(
        body,
        name=name,
        grid=(s // tm,),
        in_specs=[rows(D_A), rows(D_A), pl.BlockSpec((N_HEADS, HEAD_DIM, tm), lambda i: (0, 0, i)), heads, heads,
                  rows(128), rows(D_C), pl.BlockSpec(sel["sel_ht"].shape, lambda i: (0, 0, 0))],
        out_specs=rows(N_IN_P),
        out_shape=_sds((s, N_IN_P), bf16),
        compiler_params=_params("parallel"),
    )(dax, dag, dqt, dkh, dvh, df, dcu, sel["sel_ht"])


def _s5_disc_fn(are, aim, ldt):
    dt = jnp.exp(ldt)
    er = jnp.exp(are * dt)
    br = er * jnp.cos(aim * dt)
    bi = er * jnp.sin(aim * dt)
    nr = br - 1.0
    den = are * are + aim * aim
    return br, bi, (nr * are + bi * aim) / den, (bi * are - nr * aim) / den


def _s5_disc(name, are, aim, ldt):
    def body(a_ref, b_ref, c_ref, o0, o1, o2, o3):
        r = _s5_disc_fn(a_ref[...], b_ref[...], c_ref[...])
        o0[...], o1[...], o2[...], o3[...] = r

    shp = _sds((S5_GROUPS, S5_STATE))
    return pl.pallas_call(body, name=name, out_shape=[shp] * 4)(are, aim, ldt)


def _s5_disc_bwd(name, are, aim, ldt, cts):
    def body(a_ref, b_ref, c_ref, d0, d1, d2, d3, o0, o1, o2):
        _, vjp = jax.vjp(_s5_disc_fn, a_ref[...], b_ref[...], c_ref[...])
        o0[...], o1[...], o2[...] = vjp((d0[...], d1[...], d2[...], d3[...]))

    shp = _sds((S5_GROUPS, S5_STATE))
    return pl.pallas_call(body, name=name, out_shape=[shp, shp, _sds((S5_GROUPS, 1))])(are, aim, ldt, *cts)


def _adamw_rows(w, g, m, v):
    m = ADAM_B1 * m + (1.0 - ADAM_B1) * g
    v = ADAM_B2 * v + (1.0 - ADAM_B2) * (g * g)
    m_hat = m / (1.0 - ADAM_B1 ** ADAM_STEP)
    v_hat = v / (1.0 - ADAM_B2 ** ADAM_STEP)
    return -ADAM_LR * (m_hat / (jnp.sqrt(v_hat) + ADAM_EPS) + ADAM_WD * w), m, v


def _adamw(name, w, ga, gb, m, v):
    rows, cols = w.shape
    tr = _row_tile(rows)

    def body(w_ref, ga_ref, gb_ref, m_ref, v_ref, g_out, d_out, m_out, v_out):
        g = ga_ref[...] + gb_ref[...]
        d, mm, vv = _adamw_rows(w_ref[...], g, m_ref[...], v_ref[...])
        g_out[...] = g
        d_out[...] = d
        m_out[...] = mm
        v_out[...] = vv

    spec = pl.BlockSpec((tr, cols), lambda i: (i, 0))
    return pl.pallas_call(
        body, name=name, grid=(rows // tr,), in_specs=[spec] * 5, out_specs=[spec] * 4,
        out_shape=[_sds((rows, cols))] * 4, compiler_params=_params("parallel"),
    )(w, ga, gb, m, v)


def _sum_stack(name, st):
    n, rows, cols = st.shape
    tr = _row_tile(rows)

    def body(s_ref, o_ref):
        acc = s_ref[0].astype(f32)
        for j in range(1, n):
            acc = acc + s_ref[j].astype(f32)
        o_ref[...] = acc

    return pl.pallas_call(
        body, name=name, grid=(rows // tr,), in_specs=[pl.BlockSpec((n, tr, cols), lambda i: (0, i, 0))],
        out_specs=pl.BlockSpec((tr, cols), lambda i: (i, 0)), out_shape=_sds((rows, cols)),
        compiler_params=_params("parallel"),
    )(st)


def _block_diag(w):
    h, n, m = w.shape
    return jnp.einsum("hij,hg->higj", w, jnp.eye(h, dtype=w.dtype)).reshape(h * n, h * m)


def _block_diag_part(dense, h):
    n, m = dense.shape[0] // h, dense.shape[1] // h
    return jnp.einsum("higj,hg->hij", dense.reshape(h, n, h, m), jnp.eye(h, dtype=dense.dtype))


def _s5_matrices(coef_re, coef_im, b_re, b_im, c_re, c_im):
    bb_re = coef_re[:, :, None] * b_re - coef_im[:, :, None] * b_im
    bb_im = coef_re[:, :, None] * b_im + coef_im[:, :, None] * b_re
    wb_re = _block_diag(jnp.swapaxes(bb_re, 1, 2))
    wb_im = _block_diag(jnp.swapaxes(bb_im, 1, 2))
    wc_re = _block_diag(jnp.swapaxes(c_re, 1, 2))
    wc_im = _block_diag(jnp.swapaxes(-c_im, 1, 2))
    return wb_re, wb_im, wc_re, wc_im


def _shift_down(t):
    return jnp.concatenate([jnp.zeros((1, t.shape[1]), t.dtype), t[:-1]], axis=0)


def _shift_up(t):
    return jnp.concatenate([t[1:], jnp.zeros((1, t.shape[1]), t.dtype)], axis=0)


def _row(v):
    return v.reshape(1, -1)


def _ffn_fwd(tag, h, get, names, gamma, beta, gate_first=False, hb=None):
    hm = h if hb is None else hb
    wg = get(names[0])
    if gate_first:
        g = _ffn_gate(tag + "_gate", hm, wg)
        wu = get(names[1])
        u, act = _ffn_up_given_gate(tag + "_up", hm, wu, g)
    else:
        wu = get(names[1])
        g, u, act = _ffn_up(tag + "_up", hm, wg, wu)
    wd = get(names[2])
    r, out, outb = _mm_ln(tag + "_down", act, wd, h, gamma, beta, 0.5, k_slabs=True)
    return out, outb, dict(h=hm, g=g, u=u, act=act, r=r, wg=wg, wu=wu, wd=wd)


def _ffn_bwd(tag, dout, sv, names, gamma, put, after_ln=None):
    s = dout.shape[0]
    dr, drb, dgam, dbet = _ln_bwd(tag + "_lnb", sv["r"], dout, gamma)
    if after_ln is not None:
        after_ln(dgam, dbet)
    put(names[2], _mm_plain(tag + "_dwd", "tn", _Slabs(sv["act"]), drb, (D_FF, D_MODEL, s), scale=0.5, out_dtype=bf16,
                            tiles=(FF_SLAB, 1024, _tile(s, 2048))))
    dg, du = _ffn_dact(tag + "_dact", drb, sv["wd"], sv["g"], sv["u"])
    dwg, dwu = _mm2(tag + "_dwgu", "tn", (D_FF, D_MODEL, s), _Slabs(dg), sv["h"], _Slabs(du), None, separate=True,
                    out_dtype=bf16, tiles=(FF_SLAB, 1024, _tile(s, 2048)))
    put(names[0], dwg)
    put(names[1], dwu)
    slabs = range(dg.shape[0])
    dh = _mm(tag + "_dh", "nn", (s, D_MODEL, FF_SLAB), (_tile(s, 512), D_MODEL, FF_SLAB),
             [_KPart(dg, j) for j in slabs] + [_KPart(du, j) for j in slabs],
             [_KPart(sv["wg"], j) for j in slabs] + [_KPart(sv["wu"], j) for j in slabs],
             [(j, j, 0) for j in range(2 * len(slabs))], 1,
             lambda accs, extras, vecs: [accs[0] + ALPHA * extras[0]], [f32], extras=[dr])[0]
    return dh, dgam, dbet


def _mixer_fwd(tag, h1, w):
    s = h1.shape[0]
    z = _mm_plain(tag + "_win", "nn", h1, w["w_in"], (s, N_IN_P, D_MODEL), tiles=(_tile(s, 512), 768, D_MODEL))
    ag, f, cu_cols = (z, D_A, 1), (z, 128, F_OFF // 128), (z, D_C, CU_OFF // D_C)
    cu = z[:, CU_OFF:]
    xa = _conv_fwd(tag + "_conv", z, w["conv_w"], w["conv_b"])
    a, gated = _rg_gates(tag + "_gates", xa, w["rg_wa"], w["rg_wx"], w["rg_ba"], w["rg_bx"], w["rg_lam"])
    ha = _lin_scan(tag + "_rgscan", a, gated, False)
    ones = jnp.ones((s, 128), f32)
    c = _lin_scan(tag + "_cumf", ones, _log_f(tag + "_logf", f, w["fox_bf"]), False)
    att = dict(zip(("qt", "k_aug", "kt", "vt", "v", "qh"), _attn_prep(tag + "_attnprep", z, c, w["sel"])))
    ot, lse = _attn_fwd_t(tag + "_attn", att["qt"], att["k_aug"], att["vt"])
    ob = ot.reshape(D_B, s).T
    bu_re, bu_im = _mm2(tag + "_s5in", "nn", (s, S5_LANES, D_C), cu, w["wb_re"], None, w["wb_im"], separate=True,
                        tiles=(_tile(s, 512), 1024, D_C))
    hre, him = _s5_scan(tag + "_s5scan", bu_re, bu_im, w["abar_re"], w["abar_im"], False)
    o = _mix_out(tag + "_mixout", ag, ha, ob, hre, him, cu_cols, w["s5_d"], w["mix_g"], w["wc_re"], w["wc_im"],
                 w["w_glu"])
    sv = dict(h1=h1, z=z, ag=ag, f=f, cu=cu, cu_cols=cu_cols, xa=xa, a=a, ha=ha, att=att, ot=ot, lse=lse, ob=ob,
              hre=hre, him=him, o=o)
    return o, sv


def _mixer_bwd(tag, do, dr2, sv, w, put):
    s = do.shape[0]
    (dag, dha, dob, dhre, dhim, dcu1, dwcr, dwci, dwglu, dd, dgn) = _mix_out_bwd(
        tag + "_mixoutb", do, sv["ag"], sv["ha"], sv["ob"], sv["hre"], sv["him"], sv["cu_cols"], w["s5_d"], w["mix_g"],
        w["wc_re"], w["wc_im"], w["w_glu"])
    put("s5_w_glu", dwglu.astype(bf16))
    gre, gim = _s5_scan(tag + "_s5scanb", dhre, dhim, w["abar_re"], -w["abar_im"], True)
    dab_re, dab_im = _s5_decay_grad(tag + "_s5dec", sv["hre"], sv["him"], gre, gim)
    dwb_re, dwb_im = _mm2(tag + "_s5dwb", "tn", (D_C, S5_LANES, s), sv["cu"], gre, None, gim, separate=True,
                          tiles=(D_C, 1024, _tile(s, 1024)))
    dcu = _mm2(tag + "_s5dcu", "nt", (s, D_C, S5_LANES), gre, w["wb_re"], gim, w["wb_im"], add=dcu1,
               tiles=(_tile(s, 512), D_C, 1024))[0]
    att = sv["att"]
    tq = _att_tiles(s)[0]
    nt = s // tq
    dot_blocks, doh = _attn_do_prep(tag + "_doprep", dob, w["sel"])
    dqt, delta = _attn_bwd_dq_t(tag + "_attndq", att["qt"], att["k_aug"], att["v"], att["kt"], sv["ot"], dot_blocks,
                                sv["lse"])
    dkh, dvh, dck = _attn_bwd_dkv_t(tag + "_attndkv", att["qt"], att["k_aug"], att["v"], att["qh"], doh, dot_blocks,
                                    sv["lse"].reshape(N_HEADS, nt, 1, tq), delta.reshape(N_HEADS, nt, 1, tq))
    dc = jnp.pad(dck.reshape(N_HEADS, s).T, ((0, 0), (0, 128 - N_HEADS)))
    dlf = _lin_scan(tag + "_cumfb", jnp.ones((s, 128), f32), dc, True)
    df, dbf = _log_f_bwd(tag + "_logfb", dlf, sv["f"], w["fox_bf"])
    ga = _lin_scan(tag + "_rgscanb", _shift_up(sv["a"]), dha, True)
    dxa, dwa, dwx, dba, dbx, dlam = _rg_gates_bwd(tag + "_gatesb", sv["xa"], ga, _shift_down(sv["ha"]), w["rg_wa"],
                                                  w["rg_wx"], w["rg_ba"], w["rg_bx"], w["rg_lam"])
    dax, dconv = _conv_bwd(tag + "_convb", dxa, sv["z"], w["conv_w"])
    dz = _dz_assemble(tag + "_dz", dax, dag, dqt, dkh, dvh, df, dcu, w["sel"])
    put("w_in", _mm_plain(tag + "_dwin", "tn", sv["h1"], dz, (D_MODEL, N_IN_P, s), out_dtype=bf16,
                          tiles=(512, 768, _tile(s, 1024))))
    dh1 = _mm_plain(tag + "_dh1", "nt", dz, w["w_in"], (s, D_MODEL, N_IN_P), add=dr2, add_coef=ALPHA,
                    tiles=(_tile(s, 512), 1024, 768))
    grads = dict(dconv=dconv, dwa=dwa, dwx=dwx, dba=dba, dbx=dbx, dlam=dlam, dbf=dbf,
                 dab_re=dab_re, dab_im=dab_im, dwb_re=dwb_re, dwb_im=dwb_im, dwcr=dwcr, dwci=dwci, dd=dd, dgn=dgn)
    return dh1, grads


SMALL_NAMES = ["ln1_g", "ln1_b", "conv_w", "conv_b", "rg_w_a", "rg_b_a", "rg_w_x", "rg_b_x", "rg_lambda", "fox_b_f",
               "s5_a_re", "s5_a_im", "s5_log_dt", "s5_b_re", "s5_b_im", "s5_c_re", "s5_c_im", "s5_d", "mix_norm_g",
               "ln2_g", "ln2_b", "ln3_g", "ln3_b"]
BIG_NAMES = ["ffn1_w_gate", "ffn1_w_up", "ffn1_w_down", "w_in", "s5_w_glu", "w_out", "ffn2_w_gate", "ffn2_w_up",
             "ffn2_w_down"]


def _local_step(x, target, weight, small, on_grads, on_small):
    h, hb = x, None
    saved = []
    sel = _selections()
    for l in range(DEPTH):
        get = functools.partial(weight, l)

        sm = {n: small[n][l] for n in SMALL_NAMES}
        abar_re, abar_im, coef_re, coef_im = _s5_disc(f"l{l}_s5disc", sm["s5_a_re"], sm["s5_a_im"],
                                                      sm["s5_log_dt"].reshape(S5_GROUPS, 1))
        mats, mats_vjp = jax.vjp(_s5_matrices, coef_re, coef_im, sm["s5_b_re"], sm["s5_b_im"], sm["s5_c_re"],
                                 sm["s5_c_im"])
        w = dict(
            sel=sel, conv_w=sm["conv_w"], conv_b=_row(sm["conv_b"]),
            rg_wa=_block_diag(sm["rg_w_a"]).astype(bf16), rg_wx=_block_diag(sm["rg_w_x"]).astype(bf16),
            rg_ba=_row(sm["rg_b_a"]), rg_bx=_row(sm["rg_b_x"]), rg_lam=_row(sm["rg_lambda"]),
            fox_bf=jnp.pad(_row(sm["fox_b_f"]), ((0, 0), (0, 128 - N_HEADS))),
            abar_re=_row(abar_re), abar_im=_row(abar_im),
            wb_re=mats[0].astype(bf16), wb_im=mats[1].astype(bf16), wc_re=mats[2].astype(bf16),
            wc_im=mats[3].astype(bf16), s5_d=_row(sm["s5_d"]), mix_g=_row(sm["mix_norm_g"]))
        h1, h1b, sv1 = _ffn_fwd(f"l{l}_ffn1", h, get, GROUPS["F1"], _row(sm["ln1_g"]), _row(sm["ln1_b"]),
                                gate_first=(l == 0), hb=hb)
        w["w_in"], w["w_glu"] = get("w_in"), get("s5_w_glu")
        o, svm = _mixer_fwd(f"l{l}_mix", h1b, w)
        w_out = get("w_out")
        r2, h2, h2b = _mm_ln(f"l{l}_wout", o, w_out, h1, _row(sm["ln2_g"]), _row(sm["ln2_b"]), 1.0)
        h3, h3b, sv2 = _ffn_fwd(f"l{l}_ffn2", h2, get, GROUPS["F2"], _row(sm["ln3_g"]), _row(sm["ln3_b"]), hb=h2b)
        saved.append(dict(sm=sm, w=w, w_out=w_out, sv1=sv1, svm=svm, r2=r2, sv2=sv2, mats_vjp=mats_vjp))
        h, hb = h3, h3b

    dh, loss_row = _loss_head("loss_head", h, target)
    s = x.shape[0]
    gsmall = {n: [None] * DEPTH for n in SMALL_NAMES}
    for l in reversed(range(DEPTH)):
        sd = saved[l]
        sm, w = sd["sm"], sd["w"]

        def put(name, grad, l=l):
            on_grads((l, name), grad)

        dh2, dgam, dbet = _ffn_bwd(f"l{l}_ffn2", dh, sd["sv2"], GROUPS["F2"], _row(sm["ln3_g"]), put)
        gsmall["ln3_g"][l], gsmall["ln3_b"][l] = dgam[0], dbet[0]
        dr2, dr2b, dgam, dbet = _ln_bwd(f"l{l}_ln2b", sd["r2"], dh2, _row(sm["ln2_g"]))
        gsmall["ln2_g"][l], gsmall["ln2_b"][l] = dgam[0], dbet[0]
        put("w_out", _mm_plain(f"l{l}_dwout", "tn", sd["svm"]["o"], dr2b, (D_MODEL, D_MODEL, s), out_dtype=bf16))
        do = _mm_plain(f"l{l}_do", "nt", dr2b, sd["w_out"], (s, D_MODEL, D_MODEL))
        dh1, g = _mixer_bwd(f"l{l}_mix", do, dr2, sd["svm"], w, put)
        gsmall["conv_w"][l], gsmall["conv_b"][l] = g["dconv"][:CONV_WIDTH], g["dconv"][CONV_WIDTH]
        gsmall["rg_w_a"][l] = _block_diag_part(g["dwa"], N_HEADS)
        gsmall["rg_w_x"][l] = _block_diag_part(g["dwx"], N_HEADS)
        gsmall["rg_b_a"][l], gsmall["rg_b_x"][l], gsmall["rg_lambda"][l] = g["dba"][0], g["dbx"][0], g["dlam"][0]
        gsmall["fox_b_f"][l] = g["dbf"][0, :N_HEADS]
        dcoef_re, dcoef_im, db_re, db_im, dc_re, dc_im = sd["mats_vjp"]((g["dwb_re"], g["dwb_im"], g["dwcr"], g["dwci"]))
        da_re, da_im, dldt = _s5_disc_bwd(
            f"l{l}_s5discb", sm["s5_a_re"], sm["s5_a_im"], sm["s5_log_dt"].reshape(S5_GROUPS, 1),
            (g["dab_re"].reshape(S5_GROUPS, S5_STATE), g["dab_im"].reshape(S5_GROUPS, S5_STATE), dcoef_re, dcoef_im))
        gsmall["s5_a_re"][l], gsmall["s5_a_im"][l], gsmall["s5_log_dt"][l] = da_re, da_im, dldt[:, 0]
        gsmall["s5_b_re"][l], gsmall["s5_b_im"][l], gsmall["s5_c_re"][l], gsmall["s5_c_im"][l] = db_re, db_im, dc_re, dc_im
        gsmall["s5_d"][l], gsmall["mix_norm_g"][l] = g["dd"][0], g["dgn"][0]

        def after_ln(dgam, dbet, l=l):
            gsmall["ln1_g"][l], gsmall["ln1_b"][l] = dgam[0], dbet[0]
            if l == 0:
                on_small({n: jnp.stack(v) for n, v in gsmall.items()})

        dh, _, _ = _ffn_bwd(f"l{l}_ffn1", dh1, sd["sv1"], GROUPS["F1"], _row(sm["ln1_g"]), put, after_ln)
    return loss_row[0, 0], dh


def _position():
    return lax.axis_index("x"), lax.axis_index("y"), lax.axis_index("c")


_ANY = pl.BlockSpec(memory_space=pl.ANY)


COLUMN_SHARDED = ("ffn1_w_gate", "ffn1_w_up", "ffn2_w_gate", "ffn2_w_up")
PACK_QUANTUM = 128 * 256


def _permute_in_cols(w):
    pad = jnp.zeros(w.shape[:-1] + (128 - N_HEADS,), w.dtype)
    return jnp.concatenate([w[..., :F_OFF + N_HEADS], pad, w[..., F_OFF + N_HEADS:]], axis=-1)


def _unpermute_in_cols(w):
    return jnp.concatenate([w[..., :F_OFF + N_HEADS], w[..., CU_OFF:]], axis=-1)


def _pack(arrs):
    flat = jnp.concatenate([a.reshape(-1) for a in arrs])
    pad = -flat.shape[0] % PACK_QUANTUM
    return jnp.pad(flat, (0, pad)).reshape(-1, 128)


def _unpack(buf, shapes):
    flat = buf.reshape(-1)
    out, off = [], 0
    for shp in shapes:
        size = math.prod(shp)
        out.append(flat[off:off + size].reshape(shp))
        off += size
    return out


WEIGHT_NAMES = ["ffn1_w_gate", "ffn1_w_up", "ffn1_w_down", "ln1_g", "ln1_b", "w_in", "conv_w", "conv_b", "rg_w_a",
                "rg_b_a", "rg_w_x", "rg_b_x", "rg_lambda", "fox_b_f", "s5_a_re", "s5_a_im", "s5_log_dt", "s5_b_re",
                "s5_b_im", "s5_c_re", "s5_c_im", "s5_d", "s5_w_glu", "mix_norm_g", "w_out", "ln2_g", "ln2_b",
                "ffn2_w_gate", "ffn2_w_up", "ffn2_w_down", "ln3_g", "ln3_b"]


def _remote(src, dst, send_sems, recv_sems, k, peer):
    return pltpu.make_async_remote_copy(src_ref=src, dst_ref=dst, send_sem=send_sems.at[k], recv_sem=recv_sems.at[k],
                                        device_id=peer, device_id_type=MESH)


class _ChipGatherPart:
    def __init__(self, arrays):
        self.arrays, self.results = list(arrays), None

    def out_shape(self):
        return [_sds((N_CHIPS,) + a.shape, a.dtype) for a in self.arrays]

    def sems(self):
        n = len(self.arrays)
        return [pltpu.SemaphoreType.DMA((3 * n,)), pltpu.SemaphoreType.DMA((3 * n,)), pltpu.SemaphoreType.DMA((n,))]

    def copies(self, ins, outs, sems):
        send_sems, recv_sems, local_sems = sems
        x, y, c = _position()
        me = 2 * x + y
        local, sends, recvs = [], [], []
        for i, (src, dst) in enumerate(zip(ins, outs)):
            local.append(pltpu.make_async_copy(self.mine(src, me), dst.at[me], local_sems.at[i]))
            for r, (px, py) in enumerate([(1 - x, y), (x, 1 - y), (1 - x, 1 - y)]):
                peer = 2 * px + py
                sends.append(_remote(self.theirs(src, peer), dst.at[me], send_sems, recv_sems, 3 * i + r, (px, py, c)))
                recvs.append(_remote(self.mine(src, me), dst.at[peer], send_sems, recv_sems, 3 * i + r, (px, py, c)))
        return local, sends, recvs

    def mine(self, src, me):
        return src

    def theirs(self, src, peer):
        return src


class _ChipGatherHalvesPart(_ChipGatherPart):
    def sems(self):
        n = len(self.arrays)
        return super().sems() + [pltpu.SemaphoreType.DMA((3 * n,)), pltpu.SemaphoreType.DMA((3 * n,))]

    def _half(self, ref, which):
        rows = ref.shape[0] // 2
        return ref.at[pl.ds(which * rows, rows)]

    def copies(self, ins, outs, sems):
        send_sems, recv_sems, local_sems = sems[:3]
        x, y, c = _position()
        me = 2 * x + y
        local, sends, recvs = [], [], []
        for i, (src, dst) in enumerate(zip(ins, outs)):
            local.append(pltpu.make_async_copy(src, dst.at[me], local_sems.at[i]))
            for r, (px, py) in enumerate([(1 - x, y), (x, 1 - y), (1 - x, 1 - y)]):
                sends.append(_remote(self._half(src, c), self._half(dst.at[me], c), send_sems, recv_sems, 3 * i + r,
                                     (px, py, c)))
                recvs.append(_remote(self._half(src, c), self._half(dst.at[2 * px + py], c), send_sems, recv_sems,
                                     3 * i + r, (px, py, c)))
        return local, sends, recvs

    def forwards(self, ins, outs, sems):
        send_sems, recv_sems = sems[3:]
        x, y, c = _position()
        sends, recvs = [], []
        for i, dst in enumerate(outs):
            for r, (px, py) in enumerate([(1 - x, y), (x, 1 - y), (1 - x, 1 - y)]):
                slot = dst.at[2 * px + py]
                sends.append(_remote(self._half(slot, c), self._half(slot, c), send_sems, recv_sems, 3 * i + r,
                                     (x, y, 1 - c)))
                recvs.append(_remote(self._half(slot, c), self._half(slot, 1 - c), send_sems, recv_sems, 3 * i + r,
                                     (x, y, 1 - c)))
        return sends, recvs


class _ChipScatterPart(_ChipGatherPart):
    def out_shape(self):
        return [_sds(a.shape, a.dtype) for a in self.arrays]

    def mine(self, src, me):
        return src.at[me]

    def theirs(self, src, peer):
        return src.at[peer]


class _SiblingSwapPart:
    def __init__(self, arrays):
        self.arrays, self.results = list(arrays), None

    def out_shape(self):
        return [_sds(a.shape, a.dtype) for a in self.arrays]

    def sems(self):
        n = len(self.arrays)
        return [pltpu.SemaphoreType.DMA((n,)), pltpu.SemaphoreType.DMA((n,))]

    def copies(self, ins, outs, sems):
        x, y, c = _position()
        both = [_remote(src, dst, sems[0], sems[1], i, (x, y, 1 - c)) for i, (src, dst) in enumerate(zip(ins, outs))]
        return [], both, both


def _split_by(parts, refs, count):
    out, off = [], 0
    for p in parts:
        out.append(refs[off:off + count(p)])
        off += count(p)
    return out


def _parts_refs(parts, in_refs, out_refs, sem_refs):
    return zip(parts, _split_by(parts, in_refs, lambda p: len(p.arrays)),
               _split_by(parts, out_refs, lambda p: len(p.arrays)), _split_by(parts, sem_refs, lambda p: len(p.sems())))


def _exchange_start(parts, in_refs, out_refs, sem_refs):
    for part, ins, outs, sems in _parts_refs(parts, in_refs, out_refs, sem_refs):
        local, sends, _ = part.copies(ins, outs, sems)
        for cp in local + sends:
            cp.start()


def _exchange_finish(parts, in_refs, out_refs, sem_refs):
    split = list(_parts_refs(parts, in_refs, out_refs, sem_refs))
    copies = [part.copies(ins, outs, sems) for part, ins, outs, sems in split]
    for _, _, recvs in copies:
        for cp in recvs:
            cp.wait_recv()
    second = [part.forwards(ins, outs, sems) for part, ins, outs, sems in split if hasattr(part, "forwards")]
    for sends, _ in second:
        for cp in sends:
            cp.start()
    for sends, recvs in second:
        for cp in recvs:
            cp.wait_recv()
        for cp in sends:
            cp.wait_send()
    for local, sends, _ in copies:
        for cp in sends:
            cp.wait_send()
        for cp in local:
            cp.wait()


def _exchange_operands(parts):
    return ([a for p in parts for a in p.arrays], [s for p in parts for s in p.out_shape()],
            [s for p in parts for s in p.sems()])


def _set_results(parts, res):
    for part, outs in zip(parts, _split_by(parts, list(res), lambda p: len(p.arrays))):
        part.results = list(outs)


def _exchange_now(name, parts):
    x_in, x_out, x_sem = _exchange_operands(parts)
    n = len(x_in)

    def body(*refs):
        _exchange_start(parts, refs[:n], refs[n:2 * n], refs[2 * n:])
        _exchange_finish(parts, refs[:n], refs[n:2 * n], refs[2 * n:])

    res = pl.pallas_call(body, name=name, in_specs=[_ANY] * n, out_specs=[_ANY] * n, out_shape=x_out,
                         scratch_shapes=x_sem)(*x_in)
    _set_results(parts, res)


_RIDERS = {}


def _call(body, *, name, grid, in_specs, out_specs, out_shape, scratch_shapes=(), compiler_params=None):
    make_parts = _RIDERS.pop(name, None)
    if make_parts is None:
        return pl.pallas_call(body, name=name, grid=grid, in_specs=in_specs, out_specs=out_specs, out_shape=out_shape,
                              scratch_shapes=scratch_shapes, compiler_params=compiler_params)
    parts = make_parts()
    x_in, x_out, x_sem = _exchange_operands(parts)
    n_out, n_scr, n_x = len(out_shape), len(scratch_shapes), len(x_in)

    def run(*args):
        n_in = len(args)

        def hosted(*refs):
            ins, xi = refs[:n_in], refs[n_in:n_in + n_x]
            outs, xo = refs[n_in + n_x:n_in + n_x + n_out], refs[n_in + n_x + n_out:n_in + 2 * n_x + n_out]
            scr, xs = refs[n_in + 2 * n_x + n_out:n_in + 2 * n_x + n_out + n_scr], refs[n_in + 2 * n_x + n_out + n_scr:]
            first = functools.reduce(jnp.logical_and, [pl.program_id(d) == 0 for d in range(len(grid))])
            last = functools.reduce(jnp.logical_and, [pl.program_id(d) == grid[d] - 1 for d in range(len(grid))])

            @pl.when(first)
            def _():
                _exchange_start(parts, xi, xo, xs)

            body(*ins, *outs, *scr)

            @pl.when(last)
            def _():
                _exchange_finish(parts, xi, xo, xs)

        res = pl.pallas_call(
            hosted, name=name, grid=grid, in_specs=list(in_specs) + [_ANY] * n_x,
            out_specs=list(out_specs) + [_ANY] * n_x, out_shape=list(out_shape) + x_out,
            scratch_shapes=list(scratch_shapes) + x_sem, compiler_params=_params(*["arbitrary"] * len(grid)),
        )(*args, *x_in)
        _set_results(parts, res[n_out:])
        return list(res[:n_out])

    return run


GROUPS = {"F1": ["ffn1_w_gate", "ffn1_w_up", "ffn1_w_down"], "MX": ["w_in", "s5_w_glu", "w_out"],
          "F2": ["ffn2_w_gate", "ffn2_w_up", "ffn2_w_down"]}
FIRST_GATHER = [(0, "ffn1_w_gate")]
GATHER_HOSTS = {
    "l0_ffn1_gate": [(0, "ffn1_w_up")],
    "l0_ffn1_up": [(0, "ffn1_w_down")],
    "l0_ffn1_down": [(0, "w_in"), (0, "s5_w_glu"), (0, "w_out")],
    "l0_mix_attn": [(0, "ffn2_w_up"), (0, "ffn2_w_down"), (1, "w_in")],
    "l0_mix_s5scan": [(0, "ffn2_w_gate")],
    "l0_wout": [(1, "s5_w_glu"), (1, "w_out")],
    "l0_ffn2_up": [(1, "ffn1_w_gate")],
    "l0_ffn2_down": [(1, "ffn1_w_up")],
    "l1_ffn1_up": [(1, "ffn1_w_down")],
    "l1_mix_attn": [(1, "ffn2_w_up"), (1, "ffn2_w_down")],
    "l1_mix_s5scan": [(1, "ffn2_w_gate")],
}
SCATTER_HOSTS = {
    "l1_ffn2_dact": [(1, "ffn2_w_down")],
    "l1_mix_attndq": [(1, "ffn2_w_up")],
    "l1_mix_attndkv": [(1, "w_out"), (1, "s5_w_glu"), (1, "ffn2_w_gate")],
    "l1_ffn1_dact": [(1, "ffn1_w_down")],
    "l1_ffn1_dwgu": [(1, "w_in")],
    "l0_ffn2_dact": [(1, "ffn1_w_up")],
    "l0_ffn2_dwgu": [(0, "ffn2_w_down")],
    "l0_mix_attndq": [(0, "w_out"), (0, "s5_w_glu"), (0, "ffn2_w_up")],
    "l0_mix_attndkv": [(0, "ffn2_w_gate"), (1, "ffn1_w_gate")],
    "l0_mix_dh1": [(0, "w_in")],
    "l0_ffn1_dact": [(0, "ffn1_w_down")],
    "l0_ffn1_dh": [(0, "ffn1_w_gate")],
}
LAST_SCATTER = [(0, "ffn1_w_up")]
SMALL_HOST = "l0_ffn1_dwd"
SMALL_PACK_ORDER = [n for n in SMALL_NAMES if n != "conv_w"] + ["conv_w"]
TAIL_HOST = "l0_ffn1_dwgu"
LATE_SCATTER_HOST = "l0_ffn1_dh"
LAST_HOST = "adamw_ffn2"


def _sharded_rows(name, a):
    return jnp.swapaxes(a, 1, 2) if name in COLUMN_SHARDED else a


def _unstack_layer(st):
    _, r, c = st.shape
    return st.reshape(N_CHIPS * r, c)


def _restack_layer(g):
    r, c = g.shape
    return g.reshape(N_CHIPS, r // N_CHIPS, c)


def _adamw_layer(name, layer, w, ga, gb, m, v, bufs):
    _, r, c = w.shape
    tr = _row_tile(r)

    def body(w_ref, ga_ref, gb_ref, m_ref, v_ref, *rest):
        g_out, d_out, m_out, v_out = rest[-4:]
        g = ga_ref[...] + gb_ref[...]
        d, mm, vv = _adamw_rows(w_ref[...], g, m_ref[...], v_ref[...])
        g_out[...] = g
        d_out[...] = d
        m_out[...] = mm
        v_out[...] = vv

    full = pl.BlockSpec((None, tr, c), lambda i: (layer, i, 0))
    flat = pl.BlockSpec((tr, c), lambda i: (i, 0))
    extra = {} if bufs is None else dict(input_output_aliases={5 + k: k for k in range(4)})
    return pl.pallas_call(
        body, name=name, grid=(r // tr,),
        in_specs=[full, flat, flat, full, full] + ([] if bufs is None else [_ANY] * 4),
        out_specs=[full] * 4, out_shape=[_sds(w.shape)] * 4, compiler_params=_params("parallel"), **extra,
    )(w, ga, gb, m, v, *([] if bufs is None else bufs))


def _adamw_both_layers(name, ws, ms, vs, gas, gbs):
    nw = len(ws)
    _, r, c = ws[0].shape
    tr = _row_tile(r, 64)

    def body(*refs):
        ins, outs = refs[:7 * nw], refs[7 * nw:]
        for k in range(nw):
            w_ref, m_ref, v_ref = ins[3 * k:3 * k + 3]
            g_refs = ins[3 * nw + 4 * k:3 * nw + 4 * k + 4]
            g_out, d_out, m_out, v_out = outs[4 * k:4 * k + 4]
            for layer in range(DEPTH):
                g = g_refs[layer][...] + g_refs[DEPTH + layer][...]
                d, mm, vv = _adamw_rows(w_ref[layer], g, m_ref[layer], v_ref[layer])
                g_out[layer] = g
                d_out[layer] = d
                m_out[layer] = mm
                v_out[layer] = vv

    both = pl.BlockSpec((DEPTH, tr, c), lambda i: (0, i, 0))
    flat = pl.BlockSpec((tr, c), lambda i: (i, 0))
    wmv = [t for k in range(nw) for t in (ws[k], ms[k], vs[k])]
    gs = [t for k in range(nw) for t in (*gas[k], *gbs[k])]
    res = _call(
        body, name=name, grid=(r // tr,), in_specs=[both] * (3 * nw) + [flat] * (4 * nw),
        out_specs=[both] * (4 * nw), out_shape=[_sds(ws[0].shape)] * (4 * nw), compiler_params=_params("parallel"),
    )(*wmv, *gs)
    return [res[4 * k:4 * k + 4] for k in range(nw)]


def _train_step(x, loss_target, w, m, v):
    ix, iy, _ = _position()
    chip = 2 * ix + iy
    shard = {n: (_permute_in_cols(w[n]) if n == "w_in" else _sharded_rows(n, w[n])).astype(bf16) for n in BIG_NAMES}

    gathered = {}

    def gather_parts(keys, extra=()):
        part = _ChipGatherHalvesPart([shard[n][layer] for layer, n in keys] + list(extra))
        gathered.update({key: (part, i) for i, key in enumerate(keys)})
        return [part]

    (first,) = gather_parts(FIRST_GATHER, extra=[w["conv_w"]])
    _exchange_now("gather_first", [first])
    for host, keys in GATHER_HOSTS.items():
        _RIDERS[host] = functools.partial(gather_parts, keys)

    def weight(layer, name):
        part, i = gathered[(layer, name)]
        return _unstack_layer(part.results[i])

    small = {n: w[n] for n in SMALL_NAMES}
    small["conv_w"] = first.results[-1].transpose(1, 2, 0, 3).reshape(DEPTH, CONV_WIDTH, D_A)

    grads_full, scattered = {}, {}

    def scatter_parts(keys):
        part = _ChipScatterPart([_restack_layer(grads_full[key]) for key in keys])
        scattered.update({key: (part, i) for i, key in enumerate(keys)})
        return [part]

    for host, keys in SCATTER_HOSTS.items():
        _RIDERS[host] = functools.partial(scatter_parts, keys)

    partial = {}

    def reduce_chips(keys):
        for layer, n in keys:
            part, i = scattered[(layer, n)]
            p = _sum_stack(f"sum_l{layer}_{n}", part.results[i])
            partial[(layer, n)] = _unpermute_in_cols(p) if n == "w_in" else p

    early = [key for host, keys in SCATTER_HOSTS.items() if host != LATE_SCATTER_HOST for key in keys]
    late = SCATTER_HOSTS[LATE_SCATTER_HOST]
    tail = {}

    def small_parts():
        tail["small"] = _ChipGatherPart([_pack([tail["gsmall"][n] for n in SMALL_PACK_ORDER])])
        return [tail["small"]]

    def tail_parts():
        reduce_chips(early)
        tail["small_sum"] = _sum_stack("sum_small", tail["small"].results[0])
        tail["swap"] = _SiblingSwapPart([partial[k] for k in early] + [tail["small_sum"]])
        return [tail["swap"]]

    _RIDERS[SMALL_HOST] = small_parts
    _RIDERS[TAIL_HOST] = tail_parts
    loss_local, gx = _local_step(x[0], loss_target[0], weight, small, grads_full.__setitem__,
                                 functools.partial(tail.__setitem__, "gsmall"))
    other = dict(zip(early, tail["swap"].results[:-1]))
    small_mine, small_other = tail["small_sum"], tail["swap"].results[-1]
    grads, deltas, new_m, new_v = {}, {}, {}, {}
    reduce_chips(late)
    last_parts = scatter_parts(LAST_SCATTER) + [_SiblingSwapPart([partial[k] for k in late])]
    _RIDERS[LAST_HOST] = lambda: last_parts
    ffn2 = GROUPS["F2"]
    res = _adamw_both_layers(
        LAST_HOST, *[[_sharded_rows(n, t[n]) for n in ffn2] for t in (w, m, v)],
        [[partial[(layer, n)] for layer in range(DEPTH)] for n in ffn2],
        [[other[(layer, n)] for layer in range(DEPTH)] for n in ffn2])
    for n, bufs in zip(ffn2, res):
        grads[n], deltas[n], new_m[n], new_v[n] = (_sharded_rows(n, t) for t in bufs)
    other.update(zip(late, last_parts[1].results))
    reduce_chips(LAST_SCATTER)
    swap_last = _SiblingSwapPart([partial[k] for k in LAST_SCATTER])
    _exchange_now("swap_last", [swap_last])
    other.update(zip(LAST_SCATTER, swap_last.results))

    for n in BIG_NAMES:
        if n in ffn2:
            continue
        bufs = None
        wr, mr, vr = (_sharded_rows(n, t) for t in (w[n], m[n], v[n]))
        for layer in range(DEPTH):
            bufs = _adamw_layer(f"adamw_l{layer}_{n}", layer, wr, partial[(layer, n)], other[(layer, n)], mr, vr, bufs)
        grads[n], deltas[n], new_m[n], new_v[n] = (_sharded_rows(n, t) for t in bufs)
    packed = SMALL_PACK_ORDER[:-1]
    shapes = [w[n].shape for n in packed]
    res = _adamw("adamw_small", _pack([w[n] for n in packed]), small_mine, small_other,
                 _pack([m[n] for n in packed]), _pack([v[n] for n in packed]))
    for dst, buf in zip((grads, deltas, new_m, new_v), res):
        dst.update(zip(packed, _unpack(buf, shapes)))
    cw = D_A // N_CHIPS
    conv_shape = (DEPTH, CONV_WIDTH, D_A)
    offset = sum(math.prod(s_) for s_ in shapes)

    def conv_grad(buf):
        full = buf.reshape(-1)[offset:offset + math.prod(conv_shape)].reshape(conv_shape)
        return lax.dynamic_slice_in_dim(full, chip * cw, cw, axis=2).reshape(DEPTH * CONV_WIDTH, cw)

    rows = lambda t: t.reshape(DEPTH * CONV_WIDTH, cw)
    res = _adamw("adamw_conv_w", rows(w["conv_w"]), conv_grad(small_mine), conv_grad(small_other),
                 rows(m["conv_w"]), rows(v["conv_w"]))
    for dst, buf in zip((grads, deltas, new_m, new_v), res):
        dst["conv_w"] = buf.reshape(w["conv_w"].shape)

    loss = lax.psum(loss_local, ("x", "y", "c"))
    return (loss, gx[None], *[grads[n] for n in WEIGHT_NAMES], *[deltas[n] for n in WEIGHT_NAMES],
            *[new_m[n] for n in WEIGHT_NAMES], *[new_v[n] for n in WEIGHT_NAMES])


def kernel(x, ffn1_w_gate, ffn1_w_up, ffn1_w_down, ln1_g, ln1_b, w_in, conv_w, conv_b, rg_w_a, rg_b_a, rg_w_x, rg_b_x, rg_lambda, fox_b_f, s5_a_re, s5_a_im, s5_log_dt, s5_b_re, s5_b_im, s5_c_re, s5_c_im, s5_d, s5_w_glu, mix_norm_g, w_out, ln2_g, ln2_b, ffn2_w_gate, ffn2_w_up, ffn2_w_down, ln3_g, ln3_b, loss_target, m_ffn1_w_gate, m_ffn1_w_up, m_ffn1_w_down, m_ln1_g, m_ln1_b, m_w_in, m_conv_w, m_conv_b, m_rg_w_a, m_rg_b_a, m_rg_w_x, m_rg_b_x, m_rg_lambda, m_fox_b_f, m_s5_a_re, m_s5_a_im, m_s5_log_dt, m_s5_b_re, m_s5_b_im, m_s5_c_re, m_s5_c_im, m_s5_d, m_s5_w_glu, m_mix_norm_g, m_w_out, m_ln2_g, m_ln2_b, m_ffn2_w_gate, m_ffn2_w_up, m_ffn2_w_down, m_ln3_g, m_ln3_b, v_ffn1_w_gate, v_ffn1_w_up, v_ffn1_w_down, v_ln1_g, v_ln1_b, v_w_in, v_conv_w, v_conv_b, v_rg_w_a, v_rg_b_a, v_rg_w_x, v_rg_b_x, v_rg_lambda, v_fox_b_f, v_s5_a_re, v_s5_a_im, v_s5_log_dt, v_s5_b_re, v_s5_b_im, v_s5_c_re, v_s5_c_im, v_s5_d, v_s5_w_glu, v_mix_norm_g, v_w_out, v_ln2_g, v_ln2_b, v_ffn2_w_gate, v_ffn2_w_up, v_ffn2_w_down, v_ln3_g, v_ln3_b):
    args = dict(locals())
    w = {n: args[n] for n in WEIGHT_NAMES}
    m = {n: args["m_" + n] for n in WEIGHT_NAMES}
    v = {n: args["v_" + n] for n in WEIGHT_NAMES}
    return _train_step(x, loss_target, w, m, v)
```

```python
import functools
import math

import jax
import jax.numpy as jnp
from jax import lax
from jax.experimental import pallas as pl
from jax.experimental.pallas import tpu as pltpu

f32 = jnp.float32
bf16 = jnp.bfloat16

D_MODEL = 1024
D_FF = 2816
D_A = 384
D_B = 384
D_C = 256
N_HEADS = 6
HEAD_DIM = 64
S5_GROUPS = 16
S5_STATE = 64
S5_LANES = S5_GROUPS * S5_STATE
F_OFF = 5 * D_A
CU_OFF = F_OFF + 128
N_IN_P = CU_OFF + D_C
CONV_WIDTH = 4
DEPTH = 2
ALPHA = (2 * DEPTH) ** 0.25
LN_EPS = 1e-5
RMS_EPS = 1e-6
RG_C = 8.0
ATT_SCALE = HEAD_DIM ** -0.5
ADAM_LR, ADAM_B1, ADAM_B2, ADAM_EPS, ADAM_WD, ADAM_STEP = 0.001, 0.9, 0.999, 1e-08, 0.01, 10

ROW_TILE = 512
N_CHIPS = 4
MESH = pl.DeviceIdType.MESH

_DN = {
    "nn": (((1,), (0,)), ((), ())),
    "nt": (((1,), (1,)), ((), ())),
    "tn": (((0,), (0,)), ((), ())),
}


def _sds(shape, dtype=f32):
    return jax.ShapeDtypeStruct(shape, dtype)


def _tile(n, target):
    best = None
    for t in range(128, min(n, target) + 1, 128):
        if n % t == 0:
            best = t
    return best or n


def _row_tile(rows, target=256):
    best = None
    for t in range(16, min(rows, target) + 1, 16):
        if rows % t == 0:
            best = t
    return best or rows


def _params(*sem):
    return pltpu.CompilerParams(dimension_semantics=sem)


class _Slabs:
    def __init__(self, x):
        self.x = x


class _KPart:
    def __init__(self, x, j):
        self.x, self.j = x, j


FF_SLAB = D_FF // N_CHIPS
FFN_ROWS = 1024

def _mm(name, mode, dims, tiles, a_list, b_list, pairs, n_acc, epilogue, outs, extras=(), vecs=(), split_cols=False):
    m, n, k = dims
    tm, tn, tk = tiles
    nk = k // tk
    na, nb, ne, nv, no = len(a_list), len(b_list), len(extras), len(vecs), len(outs)

    def body(*refs):
        a_refs = refs[:na]
        b_refs = refs[na:na + nb]
        e_refs = refs[na + nb:na + nb + ne]
        v_refs = refs[na + nb + ne:na + nb + ne + nv]
        o_refs = refs[na + nb + ne + nv:na + nb + ne + nv + no]
        acc_refs = refs[na + nb + ne + nv + no:]
        a_vals = [r[...].astype(bf16) for r in a_refs]
        b_vals = [r[...].astype(bf16) for r in b_refs]
        products = [(ci, lax.dot_general(a_vals[ai], b_vals[bi], _DN[mode], preferred_element_type=f32))
                    for ai, bi, ci in pairs]

        def finish(accs):
            res = epilogue(accs, [e[...] for e in e_refs], [v[...] for v in v_refs])
            for o, r in zip(o_refs, res):
                o[...] = r.astype(o.dtype)

        if nk == 1:
            accs = [None] * n_acc
            for ci, prod in products:
                accs[ci] = prod if accs[ci] is None else accs[ci] + prod
            finish(accs)
            return
        kk = pl.program_id(2)

        @pl.when(kk == 0)
        def _():
            for acc in acc_refs:
                acc[...] = jnp.zeros_like(acc)

        for ci, prod in products:
            acc_refs[ci][...] += prod

        @pl.when(kk == nk - 1)
        def _():
            finish([acc[...] for acc in acc_refs])

    def a_spec(a):
        if isinstance(a, _KPart):
            return pl.BlockSpec((None, tm, tk), lambda i, j, kk, part=a.j: (part, i, 0))
        if isinstance(a, _Slabs):
            if mode == "tn":
                return pl.BlockSpec((None, tk, tm), lambda i, j, kk: (i, kk, 0))
            return pl.BlockSpec((None, tm, tk), lambda i, j, kk: (kk, i, 0))
        if mode == "tn":
            return pl.BlockSpec((tk, tm), lambda i, j, kk: (kk, i))
        return pl.BlockSpec((tm, tk), lambda i, j, kk: (i, kk))

    def b_spec(b):
        if isinstance(b, _KPart):
            return pl.BlockSpec((tk, tn), lambda i, j, kk, part=b.j: (part, j))
        if isinstance(b, _Slabs):
            if mode == "nt":
                return pl.BlockSpec((None, tn, tk), lambda i, j, kk: (kk, j, 0))
            return pl.BlockSpec((None, tk, tn), lambda i, j, kk: (j, kk, 0))
        if mode == "nt":
            return pl.BlockSpec((tn, tk), lambda i, j, kk: (j, kk))
        return pl.BlockSpec((tk, tn), lambda i, j, kk: (kk, j))

    o_spec = pl.BlockSpec((tm, tn), lambda i, j, kk: (i, j))
    o_slab_spec = pl.BlockSpec((None, tm, tn), lambda i, j, kk: (j, i, 0))
    v_spec = pl.BlockSpec((1, tn), lambda i, j, kk: (0, j))
    if split_cols:
        out_specs = [o_slab_spec] * no
        out_shape = [_sds((n // tn, m, tn), dt) for dt in outs]
    else:
        out_specs = [o_spec] * no
        out_shape = [_sds((m, n), dt) for dt in outs]
    raw = lambda t: t.x if isinstance(t, (_Slabs, _KPart)) else t
    res = _call(
        body,
        name=name,
        grid=(m // tm, n // tn, nk),
        in_specs=([a_spec(a) for a in a_list] + [b_spec(b) for b in b_list]
                  + [o_slab_spec if isinstance(e, _Slabs) else o_spec for e in extras] + [v_spec] * nv),
        out_specs=out_specs,
        out_shape=out_shape,
        scratch_shapes=[pltpu.VMEM((tm, tn), f32)] * (n_acc if nk > 1 else 0),
        compiler_params=_params("parallel", "parallel", "arbitrary"),
    )(*map(raw, a_list), *map(raw, b_list), *map(raw, extras), *vecs)
    return res


def _sigmoid(x):
    return 0.5 * (jnp.tanh(0.5 * x) + 1.0)


def _layer_norm_rows(r, gamma, beta):
    mu = jnp.mean(r, axis=-1, keepdims=True)
    xc = r - mu
    var = jnp.mean(xc * xc, axis=-1, keepdims=True)
    return xc * lax.rsqrt(var + LN_EPS) * gamma + beta


def _mm_plain(name, mode, a, b, dims, scale=1.0, out_dtype=f32, add=None, add_coef=1.0, tiles=None):
    m, n, k = dims
    tiles = tiles or (_tile(m, 512), _tile(n, 1024), _tile(k, 1024))

    def epilogue(accs, extras, vecs):
        r = accs[0] if scale == 1.0 else accs[0] * scale
        if extras:
            r = r + add_coef * extras[0]
        return [r]

    return _mm(name, mode, dims, tiles, [a], [b], [(0, 0, 0)], 1, epilogue, [out_dtype],
               extras=[] if add is None else [add])[0]


def _ffn_up(name, h, wg, wu):
    s = h.shape[0]

    def epilogue(accs, extras, vecs):
        g, u = accs
        return [g, u, g * _sigmoid(g) * u]

    return _mm(name, "nt", (s, D_FF, D_MODEL), (_tile(s, FFN_ROWS), FF_SLAB, D_MODEL), [h], [wg, wu],
               [(0, 0, 0), (0, 1, 1)], 2, epilogue, [bf16, bf16, bf16], split_cols=True)


def _ffn_gate(name, h, wg):
    s = h.shape[0]
    return _mm(name, "nt", (s, D_FF, D_MODEL), (_tile(s, FFN_ROWS), FF_SLAB, D_MODEL), [h], [wg], [(0, 0, 0)], 1,
               lambda accs, extras, vecs: [accs[0]], [bf16], split_cols=True)[0]


def _ffn_up_given_gate(name, h, wu, g):
    s = h.shape[0]

    def epilogue(accs, extras, vecs):
        gg = extras[0].astype(f32)
        return [accs[0], gg * _sigmoid(gg) * accs[0]]

    return _mm(name, "nt", (s, D_FF, D_MODEL), (_tile(s, FFN_ROWS), FF_SLAB, D_MODEL), [h], [wu], [(0, 0, 0)], 1,
               epilogue, [bf16, bf16], extras=[_Slabs(g)], split_cols=True)


def _mm_ln(name, a, w, resid, gamma, beta, scale, k_slabs=False):
    def epilogue(accs, extras, vecs):
        r = ALPHA * extras[0] + scale * accs[0]
        ln = _layer_norm_rows(r, vecs[0], vecs[1])
        return [r, ln, ln]

    if k_slabs:
        n_slabs, s, slab = a.shape
        return _mm(name, "nn", (s, D_MODEL, slab), (_tile(s, 512), D_MODEL, slab),
                   [_KPart(a, j) for j in range(n_slabs)], [_KPart(w, j) for j in range(n_slabs)],
                   [(j, j, 0) for j in range(n_slabs)], 1, epilogue, [f32, f32, bf16], extras=[resid],
                   vecs=[gamma, beta])
    s, k = a.shape
    return _mm(name, "nn", (s, D_MODEL, k), (_tile(s, FFN_ROWS), D_MODEL, _tile(k, 1024)),
               [a], [w], [(0, 0, 0)], 1, epilogue, [f32, f32, bf16], extras=[resid], vecs=[gamma, beta])


def _ffn_dact(name, dr, wd, g, u):
    s = dr.shape[0]

    def epilogue(accs, extras, vecs):
        da = 0.5 * accs[0]
        gg, uu = extras[0].astype(f32), extras[1].astype(f32)
        sg = _sigmoid(gg)
        return [da * uu * (sg * (1.0 + gg * (1.0 - sg))), da * (gg * sg)]

    return _mm(name, "nt", (s, D_FF, D_MODEL), (_tile(s, FFN_ROWS), FF_SLAB, D_MODEL), [dr], [wd],
               [(0, 0, 0)], 1, epilogue, [bf16, bf16], extras=[_Slabs(g), _Slabs(u)], split_cols=True)


def _mm2(name, mode, dims, a0, b0, a1, b1, add=None, add_coef=1.0, separate=False, tiles=None, out_dtype=f32,
         split_cols=False):
    m, n, k = dims
    tiles = tiles or (_tile(m, 512), _tile(n, 1024), _tile(k, 1024))

    def epilogue(accs, extras, vecs):
        if separate:
            return list(accs)
        r = accs[0]
        if extras:
            r = r + add_coef * extras[0]
        return [r]

    a_list = [a0] if a1 is None else [a0, a1]
    b_list = [b0] if b1 is None else [b0, b1]
    pairs = [(0, 0, 0), (len(a_list) - 1, len(b_list) - 1, 1 if separate else 0)]
    return _mm(name, mode, dims, tiles, a_list, b_list, pairs, 2 if separate else 1, epilogue,
               [out_dtype, out_dtype] if separate else [out_dtype], extras=[] if add is None else [add],
               split_cols=split_cols)


def _row_call(name, body, s, ins, params, outs, accs):
    tm = ROW_TILE
    ins = [a if isinstance(a, tuple) else (a, a.shape[1], 0) for a in ins]
    in_specs = [pl.BlockSpec((tm, width), lambda i, cb=cb: (i, cb)) for _, width, cb in ins]
    ins = [a for a, _, _ in ins]
    in_specs += [pl.BlockSpec(p.shape, lambda i, nd=p.ndim: (0,) * nd) for p in params]
    out_specs = [pl.BlockSpec((tm, o.shape[1]), lambda i: (i, 0)) for o in outs]
    out_specs += [pl.BlockSpec(a.shape, lambda i, nd=len(a.shape): (0,) * nd) for a in accs]
    return pl.pallas_call(
        body,
        name=name,
        grid=(s // tm,),
        in_specs=in_specs,
        out_specs=out_specs,
        out_shape=list(outs) + list(accs),
        compiler_params=_params("arbitrary"),
    )(*ins, *params)


def _zero_at_first(refs):
    @pl.when(pl.program_id(0) == 0)
    def _():
        for r in refs:
            r[...] = jnp.zeros_like(r)


def _ln_bwd(name, r, dh, gamma):
    s = r.shape[0]

    def body(r_ref, dh_ref, g_ref, dr_ref, drb_ref, dg_ref, db_ref):
        _zero_at_first([dg_ref, db_ref])
        rr = r_ref[...]
        dy = dh_ref[...]
        mu = jnp.mean(rr, axis=-1, keepdims=True)
        xc = rr - mu
        rstd = lax.rsqrt(jnp.mean(xc * xc, axis=-1, keepdims=True) + LN_EPS)
        xhat = xc * rstd
        dxh = dy * g_ref[...]
        dr = rstd * (dxh - jnp.mean(dxh, axis=-1, keepdims=True) - xhat * jnp.mean(dxh * xhat, axis=-1, keepdims=True))
        dr_ref[...] = dr
        drb_ref[...] = dr.astype(bf16)
        dg_ref[...] += jnp.sum(dy * xhat, axis=0, keepdims=True)
        db_ref[...] += jnp.sum(dy, axis=0, keepdims=True)

    return _row_call(name, body, s, [r, dh], [gamma], [_sds((s, D_MODEL)), _sds((s, D_MODEL), bf16)],
                     [_sds((1, D_MODEL)), _sds((1, D_MODEL))])


def _loss_head(name, y, target):
    s = y.shape[0]

    def body(y_ref, t_ref, dy_ref, l_ref):
        _zero_at_first([l_ref])
        e = y_ref[...] - t_ref[...]
        dy_ref[...] = e / D_MODEL
        l_ref[...] += 0.5 * jnp.sum(jnp.mean(e * e, axis=-1, keepdims=True), axis=0, keepdims=True)

    return _row_call(name, body, s, [y, target], [], [_sds((s, D_MODEL))], [_sds((1, 128))])


def _expm1(x):
    series = x * (1.0 + x / 2.0 * (1.0 + x / 3.0 * (1.0 + x / 4.0 * (1.0 + x / 5.0 * (1.0 + x / 6.0 * (1.0 + x / 7.0))))))
    return jnp.where(jnp.abs(x) < 0.25, series, jnp.exp(x) - 1.0)


def _gates_fn(xa, wa, wx, ba, bx, lam, tap_a, tap_x):
    xb = xa.astype(bf16)
    r = jax.nn.sigmoid(jnp.dot(xb, wa, preferred_element_type=f32) + ba + tap_a)
    i = jax.nn.sigmoid(jnp.dot(xb, wx, preferred_element_type=f32) + bx + tap_x)
    log_a = -RG_C * r * jax.nn.softplus(-lam)
    a = jnp.exp(log_a)
    gated = jnp.sqrt(-_expm1(2.0 * log_a)) * (i * xa)
    return a, gated


def _rg_gates(name, xa, wa, wx, ba, bx, lam):
    s = xa.shape[0]

    def body(xa_ref, wa_ref, wx_ref, ba_ref, bx_ref, lam_ref, a_ref, g_ref):
        a, g = _gates_fn(xa_ref[...], wa_ref[...], wx_ref[...], ba_ref[...], bx_ref[...], lam_ref[...], 0.0, 0.0)
        a_ref[...] = a
        g_ref[...] = g

    return _row_call(name, body, s, [xa], [wa, wx, ba, bx, lam], [_sds((s, D_A)), _sds((s, D_A))], [])


def _rg_gates_bwd(name, xa, ga, h_prev, wa, wx, ba, bx, lam):
    s = xa.shape[0]

    def body(xa_ref, ga_ref, hp_ref, wa_ref, wx_ref, ba_ref, bx_ref, lam_ref,
             dxa_ref, dwa_ref, dwx_ref, dba_ref, dbx_ref, dlam_ref):
        _zero_at_first([dwa_ref, dwx_ref, dba_ref, dbx_ref, dlam_ref])
        xa_v = xa_ref[...]
        zero = jnp.zeros((xa_v.shape[0], D_A), f32)
        fn = lambda x, ba_, bx_, lam_, ta, tx: _gates_fn(x, wa_ref[...], wx_ref[...], ba_, bx_, lam_, ta, tx)
        _, vjp = jax.vjp(fn, xa_v, ba_ref[...], bx_ref[...], lam_ref[...], zero, zero)
        gav = ga_ref[...]
        dxa, dba, dbx, dlam, dta, dtx = vjp((gav * hp_ref[...], gav))
        dxa_ref[...] = dxa
        xb = xa_v.astype(bf16)
        dwa_ref[...] += lax.dot_general(xb, dta.astype(bf16), _DN["tn"], preferred_element_type=f32)
        dwx_ref[...] += lax.dot_general(xb, dtx.astype(bf16), _DN["tn"], preferred_element_type=f32)
        dba_ref[...] += dba
        dbx_ref[...] += dbx
        dlam_ref[...] += dlam

    return _row_call(name, body, s, [xa, ga, h_prev], [wa, wx, ba, bx, lam], [_sds((s, D_A))],
                     [_sds((D_A, D_A)), _sds((D_A, D_A)), _sds((1, D_A)), _sds((1, D_A)), _sds((1, D_A))])


def _rms(v, g):
    return v * lax.rsqrt(jnp.mean(v * v, axis=-1, keepdims=True) + RMS_EPS) * g


def _mix_out_fn(ag, ha, ob, hre, him, cu, d, gn, tap_y, tap_gl, wcr, wci, wglu):
    out_a = jax.nn.gelu(ag) * ha
    y = (jnp.dot(hre.astype(bf16), wcr, preferred_element_type=f32)
         + jnp.dot(him.astype(bf16), wci, preferred_element_type=f32) + d * cu + tap_y)
    y2 = jax.nn.gelu(y)
    gl = jnp.dot(y2.astype(bf16), wglu, preferred_element_type=f32) + tap_gl
    out_c = y2 * jax.nn.sigmoid(gl)
    o = jnp.concatenate([_rms(out_a, gn[:, :D_A]), _rms(ob, gn[:, D_A:D_A + D_B]), _rms(out_c, gn[:, D_A + D_B:])],
                        axis=-1)
    return o, y2


def _mix_out(name, ag, ha, ob, hre, him, cu, d, gn, wcr, wci, wglu):
    s = ha.shape[0]

    def body(ag_ref, ha_ref, ob_ref, hre_ref, him_ref, cu_ref, d_ref, gn_ref, wcr_ref, wci_ref, wglu_ref, o_ref):
        o, _ = _mix_out_fn(ag_ref[...], ha_ref[...], ob_ref[...], hre_ref[...], him_ref[...], cu_ref[...], d_ref[...],
                           gn_ref[...], 0.0, 0.0, wcr_ref[...], wci_ref[...], wglu_ref[...])
        o_ref[...] = o.astype(o_ref.dtype)

    return _row_call(name, body, s, [ag, ha, ob, hre, him, cu], [d, gn, wcr, wci, wglu], [_sds((s, D_MODEL), bf16)], [])[0]


def _mix_out_bwd(name, do, ag, ha, ob, hre, him, cu, d, gn, wcr, wci, wglu):
    s = ha.shape[0]

    def body(do_ref, ag_ref, ha_ref, ob_ref, hre_ref, him_ref, cu_ref, d_ref, gn_ref, wcr_ref, wci_ref, wglu_ref,
             dag_ref, dha_ref, dob_ref, dhre_ref, dhim_ref, dcu_ref, dwcr_ref, dwci_ref, dwglu_ref, dd_ref, dgn_ref):
        _zero_at_first([dwcr_ref, dwci_ref, dwglu_ref, dd_ref, dgn_ref])
        tm = ag_ref.shape[0]
        zero = jnp.zeros((tm, D_C), f32)
        hre_v, him_v = hre_ref[...], him_ref[...]
        fn = lambda *a: _mix_out_fn(*a, wcr_ref[...], wci_ref[...], wglu_ref[...])
        _, vjp, y2 = jax.vjp(fn, ag_ref[...], ha_ref[...], ob_ref[...], hre_v, him_v, cu_ref[...], d_ref[...],
                             gn_ref[...], zero, zero, has_aux=True)
        dag, dha, dob, dhre, dhim, dcu, dd, dgn, dy, dgl = vjp(do_ref[...])
        dag_ref[...] = dag
        dha_ref[...] = dha
        dob_ref[...] = dob
        dhre_ref[...] = dhre
        dhim_ref[...] = dhim
        dcu_ref[...] = dcu
        dyb = dy.astype(bf16)
        dwcr_ref[...] += lax.dot_general(hre_v.astype(bf16), dyb, _DN["tn"], preferred_element_type=f32)
        dwci_ref[...] += lax.dot_general(him_v.astype(bf16), dyb, _DN["tn"], preferred_element_type=f32)
        dwglu_ref[...] += lax.dot_general(y2.astype(bf16), dgl.astype(bf16), _DN["tn"], preferred_element_type=f32)
        dd_ref[...] += dd
        dgn_ref[...] += dgn

    outs = [_sds((s, D_A)), _sds((s, D_A)), _sds((s, D_B)), _sds((s, S5_LANES)), _sds((s, S5_LANES)), _sds((s, D_C))]
    accs = [_sds((S5_LANES, D_C)), _sds((S5_LANES, D_C)), _sds((D_C, D_C)), _sds((1, D_C)), _sds((1, D_MODEL))]
    return _row_call(name, body, s, [do, ag, ha, ob, hre, him, cu], [d, gn, wcr, wci, wglu], outs, accs)


def _log_f(name, f, bf):
    s = f[0].shape[0]

    def body(f_ref, b_ref, o_ref):
        o_ref[...] = jax.nn.log_sigmoid(f_ref[...] + b_ref[...])

    return _row_call(name, body, s, [f], [bf], [_sds((s, 128))], [])[0]


def _log_f_bwd(name, dlf, f, bf):
    s = dlf.shape[0]

    def body(dl_ref, f_ref, b_ref, df_ref, db_ref):
        _zero_at_first([db_ref])
        df = dl_ref[...] * jax.nn.sigmoid(-(f_ref[...] + b_ref[...]))
        df_ref[...] = df
        db_ref[...] += jnp.sum(df, axis=0, keepdims=True)

    return _row_call(name, body, s, [dlf, f], [bf], [_sds((s, 128))], [_sds((1, 128))])


def _s5_decay_grad(name, h_re, h_im, g_re, g_im):
    s = g_re.shape[0]
    tm = ROW_TILE

    def body(hr_ref, hi_ref, hhr_ref, hhi_ref, gr_ref, gi_ref, dr_ref, di_ref):
        i = pl.program_id(0)
        _zero_at_first([dr_ref, di_ref])

        def previous(h_ref, halo_ref):
            halo = jnp.where(i == 0, 0.0, halo_ref[...])
            return pltpu.roll(jnp.concatenate([halo, h_ref[...]], axis=0), 1, 0)[8:, :]

        hr, hi, gr, gi = previous(hr_ref, hhr_ref), previous(hi_ref, hhi_ref), gr_ref[...], gi_ref[...]
        dr_ref[...] += jnp.sum(hr * gr + hi * gi, axis=0, keepdims=True)
        di_ref[...] += jnp.sum(hr * gi - hi * gr, axis=0, keepdims=True)

    rows = pl.BlockSpec((tm, S5_LANES), lambda i: (i, 0))
    halo = pl.BlockSpec((8, S5_LANES), lambda i: (jnp.maximum(i * (tm // 8) - 1, 0), 0))
    acc = pl.BlockSpec((1, S5_LANES), lambda i: (0, 0))
    return pl.pallas_call(
        body,
        name=name,
        grid=(s // tm,),
        in_specs=[rows, rows, halo, halo, rows, rows],
        out_specs=[acc, acc],
        out_shape=[_sds((1, S5_LANES)), _sds((1, S5_LANES))],
        compiler_params=_params("arbitrary"),
    )(h_re, h_im, h_re, h_im, g_re, g_im)


def _conv_fwd(name, ax, w, b):
    s = ax.shape[0]
    tm = ROW_TILE

    def body(x_ref, halo_ref, w_ref, b_ref, o_ref):
        i = pl.program_id(0)
        x = x_ref[...]
        halo = jnp.where(i == 0, 0.0, halo_ref[...])
        ext = jnp.concatenate([halo, x], axis=0)
        acc = b_ref[...] + w_ref[3:4, :] * x
        for k in range(CONV_WIDTH - 1):
            acc = acc + w_ref[k:k + 1, :] * pltpu.roll(ext, CONV_WIDTH - 1 - k, 0)[8:, :]
        o_ref[...] = acc

    return pl.pallas_call(
        body,
        name=name,
        grid=(s // tm,),
        in_specs=[pl.BlockSpec((tm, D_A), lambda i: (i, 0)),
                  pl.BlockSpec((8, D_A), lambda i: (jnp.maximum(i * (tm // 8) - 1, 0), 0)),
                  pl.BlockSpec((CONV_WIDTH, D_A), lambda i: (0, 0)),
                  pl.BlockSpec((1, D_A), lambda i: (0, 0))],
        out_specs=pl.BlockSpec((tm, D_A), lambda i: (i, 0)),
        out_shape=_sds((s, D_A)),
        compiler_params=_params("arbitrary"),
    )(ax, ax, w, b)


def _conv_bwd(name, dxa, ax, w):
    s = ax.shape[0]
    tm = ROW_TILE
    nblk = s // tm

    def body(dx_ref, dnext_ref, x_ref, halo_ref, w_ref, dax_ref, dw_ref):
        i = pl.program_id(0)
        _zero_at_first([dw_ref])
        dx = dx_ref[...]
        dnext = jnp.where(i == nblk - 1, 0.0, dnext_ref[...])
        dext = jnp.concatenate([dx, dnext], axis=0)
        x = x_ref[...]
        halo = jnp.where(i == 0, 0.0, halo_ref[...])
        ext = jnp.concatenate([halo, x], axis=0)
        acc = w_ref[3:4, :] * dx
        dw_ref[3:4, :] += jnp.sum(dx * x, axis=0, keepdims=True)
        for k in range(CONV_WIDTH - 1):
            sh = CONV_WIDTH - 1 - k
            acc = acc + w_ref[k:k + 1, :] * pltpu.roll(dext, tm + 8 - sh, 0)[:tm, :]
            dw_ref[k:k + 1, :] += jnp.sum(dx * pltpu.roll(ext, sh, 0)[8:, :], axis=0, keepdims=True)
        dw_ref[4:5, :] += jnp.sum(dx, axis=0, keepdims=True)
        dax_ref[...] = acc

    return pl.pallas_call(
        body,
        name=name,
        grid=(nblk,),
        in_specs=[pl.BlockSpec((tm, D_A), lambda i: (i, 0)),
                  pl.BlockSpec((8, D_A), lambda i: (jnp.minimum((i + 1) * (tm // 8), s // 8 - 1), 0)),
                  pl.BlockSpec((tm, D_A), lambda i: (i, 0)),
                  pl.BlockSpec((8, D_A), lambda i: (jnp.maximum(i * (tm // 8) - 1, 0), 0)),
                  pl.BlockSpec((CONV_WIDTH, D_A), lambda i: (0, 0))],
        out_specs=[pl.BlockSpec((tm, D_A), lambda i: (i, 0)), pl.BlockSpec((8, D_A), lambda i: (0, 0))],
        out_shape=[_sds((s, D_A)), _sds((8, D_A))],
        compiler_params=_params("arbitrary"),
    )(dxa, dxa, ax, ax, w)


SCAN_ROWS = 512


def _row_in_tile(shape):
    return lax.broadcasted_iota(jnp.int32, shape, 0) % 8


def _lin_scan(name, a, b, reverse):
    s, c = a.shape
    t = min(SCAN_ROWS, s)
    nb = s // t

    def body(a_ref, b_ref, h_ref, p_ref, carry_ref):
        @pl.when(pl.program_id(0) == 0)
        def _():
            carry_ref[...] = jnp.zeros_like(carry_ref)

        row = _row_in_tile((t, c))
        p = a_ref[...]
        h = b_ref[...]
        for d in (1, 2, 4):
            keep = (row < 8 - d) if reverse else (row >= d)
            shift = (t - d) if reverse else d
            h = h + jnp.where(keep, p * pltpu.roll(h, shift, 0), 0.0)
            p = jnp.where(keep, p * pltpu.roll(p, shift, 0), p)
        h_ref[...] = h
        p_ref[...] = p
        edge = 0 if reverse else 7

        def tile(k, carry):
            kk = (t // 8 - 1 - k) if reverse else k
            r0 = pl.multiple_of(kk * 8, 8)
            hh = h_ref[pl.ds(r0, 8), :] + p_ref[pl.ds(r0, 8), :] * carry
            h_ref[pl.ds(r0, 8), :] = hh
            return jnp.broadcast_to(hh[edge:edge + 1, :], (8, c))

        carry_ref[...] = lax.fori_loop(0, t // 8, tile, carry_ref[...])

    spec = pl.BlockSpec((t, c), (lambda i: (nb - 1 - i, 0)) if reverse else (lambda i: (i, 0)))
    (out,) = _call(
        body,
        name=name,
        grid=(nb,),
        in_specs=[spec, spec],
        out_specs=[spec],
        out_shape=[_sds((s, c))],
        scratch_shapes=[pltpu.VMEM((t, c), f32), pltpu.VMEM((8, c), f32)],
        compiler_params=_params("arbitrary"),
    )(a, b)
    return out


def _s5_scan(name, b_re, b_im, a_re, a_im, reverse):
    s, c = b_re.shape
    t = min(SCAN_ROWS, s)
    nb = s // t

    def body(br_ref, bi_ref, ar_ref, ai_ref, hr_ref, hi_ref, cr_ref, ci_ref):
        @pl.when(pl.program_id(0) == 0)
        def _():
            cr_ref[...] = jnp.zeros_like(cr_ref)
            ci_ref[...] = jnp.zeros_like(ci_ref)

        ar1, ai1 = ar_ref[...], ai_ref[...]
        pows = [(ar1, ai1)]
        for _ in range(7):
            pr, pi = pows[-1]
            pows.append((pr * ar1 - pi * ai1, pr * ai1 + pi * ar1))
        row8 = lax.broadcasted_iota(jnp.int32, (8, c), 0)
        wr = jnp.zeros((8, c), f32)
        wi = jnp.zeros((8, c), f32)
        for r in range(8):
            pr, pi = pows[(7 - r) if reverse else r]
            wr = jnp.where(row8 == r, pr, wr)
            wi = jnp.where(row8 == r, pi, wi)
        row = _row_in_tile((t, c))
        hr = br_ref[...]
        hi = bi_ref[...]
        for d in (1, 2, 4):
            keep = (row < 8 - d) if reverse else (row >= d)
            shift = (t - d) if reverse else d
            pr, pi = pows[d - 1]
            cr = jnp.where(keep, pr, 0.0)
            ci = jnp.where(keep, pi, 0.0)
            sr = pltpu.roll(hr, shift, 0)
            si = pltpu.roll(hi, shift, 0)
            hr, hi = hr + cr * sr - ci * si, hi + cr * si + ci * sr
        hr_ref[...] = hr
        hi_ref[...] = hi
        edge = 0 if reverse else 7

        def tile(k, carry):
            car_r, car_i = carry
            kk = (t // 8 - 1 - k) if reverse else k
            r0 = pl.multiple_of(kk * 8, 8)
            xr = hr_ref[pl.ds(r0, 8), :] + wr * car_r - wi * car_i
            xi = hi_ref[pl.ds(r0, 8), :] + wr * car_i + wi * car_r
            hr_ref[pl.ds(r0, 8), :] = xr
            hi_ref[pl.ds(r0, 8), :] = xi
            return (jnp.broadcast_to(xr[edge:edge + 1, :], (8, c)), jnp.broadcast_to(xi[edge:edge + 1, :], (8, c)))

        car_r, car_i = lax.fori_loop(0, t // 8, tile, (cr_ref[...], ci_ref[...]))
        cr_ref[...] = car_r
        ci_ref[...] = car_i

    spec = pl.BlockSpec((t, c), (lambda i: (nb - 1 - i, 0)) if reverse else (lambda i: (i, 0)))
    vspec = pl.BlockSpec((1, c), lambda i: (0, 0))
    hr, hi = _call(
        body,
        name=name,
        grid=(nb,),
        in_specs=[spec, spec, vspec, vspec],
        out_specs=[spec, spec],
        out_shape=[_sds((s, c)), _sds((s, c))],
        scratch_shapes=[pltpu.VMEM((8, c), f32), pltpu.VMEM((8, c), f32)],
        compiler_params=_params("arbitrary"),
    )(b_re, b_im, a_re, a_im)
    return hr, hi


ATT_FEAT = 128
ATT_TQ = 1024
ATT_TK = 1024
ATT_TK_KEY_SIDE = 1024


def _att_tiles(s, key_side=False):
    tq = min(ATT_TQ, s)
    tk = min(ATT_TK_KEY_SIDE if key_side else ATT_TK, tq)
    return tq, tk, tq // tk


def _keys_le_queries(tk, tq, k0, q0):
    row = lax.broadcasted_iota(jnp.int32, (tk, tq), 0) + k0
    col = lax.broadcasted_iota(jnp.int32, (tk, tq), 1) + q0
    return row <= col


def _attn_fwd_t(name, qt, k_aug, vt):
    h, s, _ = k_aug.shape
    tq, tk, ratio = _att_tiles(s)

    def body(qt_ref, k_ref, vt_ref, o_ref, lse_ref):
        qi = pl.program_id(1)
        qt = qt_ref[...]

        def block(kb, carry, masked):
            m, l, acc = carry
            ks = pl.multiple_of(kb * tk, tk)
            st = jnp.dot(k_ref[pl.ds(ks, tk), :], qt, preferred_element_type=f32)
            if masked:
                st = jnp.where(_keys_le_queries(tk, tq, ks, qi * tq), st, -jnp.inf)
            mn = jnp.maximum(m, jnp.max(st, axis=0, keepdims=True))
            p = jnp.exp(st - mn)
            al = jnp.exp(m - mn)
            l = al * l + jnp.sum(p, axis=0, keepdims=True)
            acc = al * acc + jnp.dot(vt_ref[kb], p.astype(bf16), preferred_element_type=f32)
            return mn, l, acc

        init = (jnp.full((1, tq), -jnp.inf, f32), jnp.zeros((1, tq), f32), jnp.zeros((HEAD_DIM, tq), f32))
        first = lax.fori_loop(0, qi * ratio, lambda kb, c: block(kb, c, False), init)
        m, l, acc = lax.fori_loop(qi * ratio, (qi + 1) * ratio, lambda kb, c: block(kb, c, True), first)
        o_ref[...] = acc / l
        lse_ref[...] = m + jnp.log(l)

    return _call(
        body,
        name=name,
        grid=(h, s // tq),
        in_specs=[pl.BlockSpec((None, None, ATT_FEAT, tq), lambda hh, i: (hh, i, 0, 0)),
                  pl.BlockSpec((None, s, ATT_FEAT), lambda hh, i: (hh, 0, 0)),
                  pl.BlockSpec((None, s // tk, HEAD_DIM, tk), lambda hh, i: (hh, 0, 0, 0))],
        out_specs=[pl.BlockSpec((None, HEAD_DIM, tq), lambda hh, i: (hh, 0, i)),
                   pl.BlockSpec((None, 1, tq), lambda hh, i: (hh, 0, i))],
        out_shape=[_sds((h, HEAD_DIM, s)), _sds((h, 1, s))],
        compiler_params=_params("parallel", "arbitrary"),
    )(qt, k_aug, vt)


def _attn_bwd_dq_t(name, qt, k_aug, v, kt, ot, dot_, lse):
    h, s, _ = k_aug.shape
    tq, tk, ratio = _att_tiles(s)

    def body(qt_ref, k_ref, v_ref, kt_ref, o_ref, do_ref, lse_ref, dq_ref, dl_ref):
        qi = pl.program_id(1)
        qt = qt_ref[...]
        dob = do_ref[...]
        delta = jnp.sum(dob.astype(f32) * o_ref[...], axis=0, keepdims=True)
        lse_v = lse_ref[...]

        def block(kb, carry, masked):
            dq, psum = carry
            ks = pl.multiple_of(kb * tk, tk)
            st = jnp.dot(k_ref[pl.ds(ks, tk), :], qt, preferred_element_type=f32)
            p = jnp.exp(st - lse_v)
            if masked:
                p = jnp.where(_keys_le_queries(tk, tq, ks, qi * tq), p, 0.0)
            dp = jnp.dot(v_ref[pl.ds(ks, tk), :], dob, preferred_element_type=f32)
            ds = p * (dp - delta)
            return (dq + jnp.dot(kt_ref[kb], ds.astype(bf16), preferred_element_type=f32),
                    psum + jnp.sum(p * dp, axis=0, keepdims=True))

        carry = lax.fori_loop(0, qi * ratio, lambda kb, c: block(kb, c, False),
                              (jnp.zeros((HEAD_DIM, tq), f32), jnp.zeros((1, tq), f32)))
        dq, psum = lax.fori_loop(qi * ratio, (qi + 1) * ratio, lambda kb, c: block(kb, c, True), carry)
        dq_ref[...] = (dq * ATT_SCALE).astype(dq_ref.dtype)
        dl_ref[...] = psum

    qspec = pl.BlockSpec((None, HEAD_DIM, tq), lambda hh, i: (hh, 0, i))
    rspec = pl.BlockSpec((None, 1, tq), lambda hh, i: (hh, 0, i))
    return _call(
        body,
        name=name,
        grid=(h, s // tq),
        in_specs=[pl.BlockSpec((None, None, ATT_FEAT, tq), lambda hh, i: (hh, i, 0, 0)),
                  pl.BlockSpec((None, s, ATT_FEAT), lambda hh, i: (hh, 0, 0)),
                  pl.BlockSpec((None, s, HEAD_DIM), lambda hh, i: (hh, 0, 0)),
                  pl.BlockSpec((None, s // tk, HEAD_DIM, tk), lambda hh, i: (hh, 0, 0, 0)),
                  qspec, pl.BlockSpec((None, None, HEAD_DIM, tq), lambda hh, i: (hh, i, 0, 0)), rspec],
        out_specs=[qspec, rspec],
        out_shape=[_sds((h, HEAD_DIM, s), bf16), _sds((h, 1, s))],
        compiler_params=_params("parallel", "arbitrary"),
    )(qt, k_aug, v, kt, ot, dot_, lse)


def _attn_bwd_dkv_t(name, qt_blocks, k_aug, v, qh, do, dot_blocks, lse, delta):
    h, s, _ = k_aug.shape
    tq, tk, ratio = _att_tiles(s, key_side=True)
    nq = s // tq

    def body(qt_ref, k_ref, v_ref, q_ref, do_ref, dot_ref, lse_ref, dl_ref, dk_ref, dv_ref, dck_ref, dsum_ref):
        kj = pl.program_id(1)
        kk = k_ref[...]
        vv = v_ref[...]
        dsum_ref[...] = jnp.zeros_like(dsum_ref)

        def block(qi, carry, masked):
            dk, dv = carry
            qs = pl.multiple_of(qi * tq, tq)
            st = jnp.dot(kk, qt_ref[qi], preferred_element_type=f32)
            p = jnp.exp(st - lse_ref[qi])
            if masked:
                p = jnp.where(_keys_le_queries(tk, tq, kj * tk, qs), p, 0.0)
            dv = dv + jnp.dot(p.astype(bf16), do_ref[pl.ds(qs, tq), :], preferred_element_type=f32)
            dp = jnp.dot(vv, dot_ref[qi], preferred_element_type=f32)
            ds = p * (dp - dl_ref[qi])
            dsum_ref[...] += ds
            dk = dk + jnp.dot(ds.astype(bf16), q_ref[pl.ds(qs, tq), :], preferred_element_type=f32)
            return dk, dv

        first = kj // ratio
        carry = block(first, (jnp.zeros((tk, HEAD_DIM), f32), jnp.zeros((tk, HEAD_DIM), f32)), True)
        dk, dv = lax.fori_loop(first + 1, nq, lambda qi, c: block(qi, c, False), carry)
        dk_ref[...] = dk.astype(dk_ref.dtype)
        dv_ref[...] = dv.astype(dv_ref.dtype)
        col = jnp.sum(dsum_ref[...], axis=1, keepdims=True)
        dck_ref[...] = -jnp.transpose(jnp.broadcast_to(col, (tk, 128)))[0:1, :]

    full = lambda shape: pl.BlockSpec((None,) + shape, lambda hh, j: (hh,) + (0,) * len(shape))
    kspec = pl.BlockSpec((None, tk, HEAD_DIM), lambda hh, j: (hh, j, 0))
    return _call(
        body,
        name=name,
        grid=(h, s // tk),
        in_specs=[full((nq, ATT_FEAT, tq)),
                  pl.BlockSpec((None, tk, ATT_FEAT), lambda hh, j: (hh, j, 0)),
                  kspec, full((s, HEAD_DIM)), full((s, HEAD_DIM)), full((nq, HEAD_DIM, tq)),
                  full((nq, 1, tq)), full((nq, 1, tq))],
        out_specs=[kspec, kspec, pl.BlockSpec((None, None, 1, tk), lambda hh, j: (hh, j, 0, 0))],
        out_shape=[_sds((h, s, HEAD_DIM), bf16), _sds((h, s, HEAD_DIM), bf16), _sds((h, s // tk, 1, tk))],
        scratch_shapes=[pltpu.VMEM((tk, tq), f32)],
        compiler_params=_params("parallel", "arbitrary"),
    )(qt_blocks, k_aug, v, qh, do, dot_blocks, lse, delta)


C_LANES = 128


def _selections():
    h = jnp.arange(N_HEADS)[:, None, None]
    row = jnp.arange(D_B + 3 * C_LANES)[None, :, None]
    col = jnp.arange(ATT_FEAT)[None, None, :]
    head_col = (row < D_B) & (row // HEAD_DIM == h) & (col == row % HEAD_DIM)

    def c_part(p, lane0):
        return (row == D_B + p * C_LANES + h) & (col == lane0 + p)

    c_q = c_part(0, HEAD_DIM) | c_part(1, HEAD_DIM) | c_part(2, HEAD_DIM)
    c_k = c_part(0, HEAD_DIM + 3) | c_part(1, HEAD_DIM + 3) | c_part(2, HEAD_DIM + 3)
    sel_q = (head_col | c_q).astype(bf16)
    sel_k = head_col.astype(bf16) - c_k.astype(bf16)
    sel_h = head_col[:, :D_B, :HEAD_DIM].astype(bf16)
    lane = jnp.arange(ATT_FEAT)
    ones_q = ((lane >= HEAD_DIM + 3) & (lane < HEAD_DIM + 6)).astype(f32)
    ones_k = ((lane >= HEAD_DIM) & (lane < HEAD_DIM + 3)).astype(f32)
    return dict(sel_qt=sel_q.transpose(0, 2, 1), sel_k=sel_k, sel_h=sel_h, sel_ht=sel_h.transpose(0, 2, 1),
                ones_q=ones_q.reshape(ATT_FEAT, 1), ones_k=ones_k.reshape(1, ATT_FEAT))


def _attn_prep(name, z, c, sel):
    s = z.shape[0]
    tq, tk, ratio = _att_tiles(s)

    def body(q_ref, k_ref, v_ref, c_ref, sqt_ref, sk_ref, sh_ref, sht_ref, oq_ref, ok_ref,
             qt_out, ka_out, kt_out, vt_out, v_out, qh_out):
        cv = c_ref[...]
        hi = cv.astype(bf16)
        r1 = cv - hi.astype(f32)
        mid = r1.astype(bf16)
        lo = (r1 - mid.astype(f32)).astype(bf16)
        qs = (q_ref[...] * ATT_SCALE).astype(bf16)
        kb = k_ref[...].astype(bf16)
        vb = v_ref[...].astype(bf16)
        xq = jnp.concatenate([qs, hi, mid, lo], axis=-1)
        xk = jnp.concatenate([kb, hi, mid, lo], axis=-1)
        for h in range(N_HEADS):
            qt = lax.dot_general(sqt_ref[h], xq, _DN["nt"], preferred_element_type=f32) + oq_ref[...]
            qt_out[h, 0] = qt.astype(bf16)
            ka_out[h] = (jnp.dot(xk, sk_ref[h], preferred_element_type=f32) + ok_ref[...]).astype(bf16)
            kt = lax.dot_general(sht_ref[h], kb, _DN["nt"], preferred_element_type=f32).astype(bf16)
            vt = lax.dot_general(sht_ref[h], vb, _DN["nt"], preferred_element_type=f32).astype(bf16)
            for j in range(ratio):
                kt_out[h, j] = kt[:, j * tk:(j + 1) * tk]
                vt_out[h, j] = vt[:, j * tk:(j + 1) * tk]
            v_out[h] = jnp.dot(vb, sh_ref[h], preferred_element_type=f32).astype(bf16)
            qh_out[h] = jnp.dot(qs, sh_ref[h], preferred_element_type=f32).astype(bf16)

    whole = lambda a: pl.BlockSpec(a.shape, lambda i, nd=a.ndim: (0,) * nd)
    consts = [sel["sel_qt"], sel["sel_k"], sel["sel_h"], sel["sel_ht"], sel["ones_q"], sel["ones_k"]]
    return pl.pallas_call(
        body,
        name=name,
        grid=(s // tq,),
        in_specs=[pl.BlockSpec((tq, D_B), lambda i: (i, 2)), pl.BlockSpec((tq, D_B), lambda i: (i, 3)),
                  pl.BlockSpec((tq, D_B), lambda i: (i, 4)), pl.BlockSpec((tq, C_LANES), lambda i: (i, 0))]
        + [whole(a) for a in consts],
        out_specs=[pl.BlockSpec((N_HEADS, 1, ATT_FEAT, tq), lambda i: (0, i, 0, 0)),
                   pl.BlockSpec((N_HEADS, tq, ATT_FEAT), lambda i: (0, i, 0)),
                   pl.BlockSpec((N_HEADS, ratio, HEAD_DIM, tk), lambda i: (0, i, 0, 0)),
                   pl.BlockSpec((N_HEADS, ratio, HEAD_DIM, tk), lambda i: (0, i, 0, 0)),
                   pl.BlockSpec((N_HEADS, tq, HEAD_DIM), lambda i: (0, i, 0)),
                   pl.BlockSpec((N_HEADS, tq, HEAD_DIM), lambda i: (0, i, 0))],
        out_shape=[_sds((N_HEADS, s // tq, ATT_FEAT, tq), bf16), _sds((N_HEADS, s, ATT_FEAT), bf16),
                   _sds((N_HEADS, s // tk, HEAD_DIM, tk), bf16), _sds((N_HEADS, s // tk, HEAD_DIM, tk), bf16),
                   _sds((N_HEADS, s, HEAD_DIM), bf16), _sds((N_HEADS, s, HEAD_DIM), bf16)],
        compiler_params=_params("parallel"),
    )(z, z, z, c, *consts)


def _attn_do_prep(name, dob, sel):
    s = dob.shape[0]
    tq = _att_tiles(s)[0]

    def body(do_ref, sh_ref, sht_ref, dot_out, do_out):
        db = do_ref[...].astype(bf16)
        for h in range(N_HEADS):
            dot_out[h, 0] = lax.dot_general(sht_ref[h], db, _DN["nt"], preferred_element_type=f32).astype(bf16)
            do_out[h] = jnp.dot(db, sh_ref[h], preferred_element_type=f32).astype(bf16)

    whole = lambda a: pl.BlockSpec(a.shape, lambda i, nd=a.ndim: (0,) * nd)
    return pl.pallas_call(
        body,
        name=name,
        grid=(s // tq,),
        in_specs=[pl.BlockSpec((tq, D_B), lambda i: (i, 0)), whole(sel["sel_h"]), whole(sel["sel_ht"])],
        out_specs=[pl.BlockSpec((N_HEADS, 1, HEAD_DIM, tq), lambda i: (0, i, 0, 0)),
                   pl.BlockSpec((N_HEADS, tq, HEAD_DIM), lambda i: (0, i, 0))],
        out_shape=[_sds((N_HEADS, s // tq, HEAD_DIM, tq), bf16), _sds((N_HEADS, s, HEAD_DIM), bf16)],
        compiler_params=_params("parallel"),
    )(dob, sel["sel_h"], sel["sel_ht"])


def _dz_assemble(name, dax, dag, dqt, dkh, dvh, df, dcu, sel):
    s = dax.shape[0]
    tm = _tile(s, 512)

    def body(dax_ref, dag_ref, dqt_ref, dk_ref, dv_ref, df_ref, dcu_ref, sht_ref, o_ref):
        dq = jnp.zeros((tm, D_B), f32)
        dk = jnp.zeros((tm, D_B), f32)
        dv = jnp.zeros((tm, D_B), f32)
        for h in range(N_HEADS):
            place = sht_ref[h]
            dq = dq + lax.dot_general(dqt_ref[h].astype(bf16), place, _DN["tn"], preferred_element_type=f32)
            dk = dk + jnp.dot(dk_ref[h].astype(bf16), place, preferred_element_type=f32)
            dv = dv + jnp.dot(dv_ref[h].astype(bf16), place, preferred_element_type=f32)
        pieces = [dax_ref[...], dag_ref[...], dq, dk, dv, df_ref[...], dcu_ref[...]]
        off = 0
        for p in pieces:
            o_ref[:, off:off + p.shape[1]] = p.astype(bf16)
            off += p.shape[1]

    rows = lambda c_: pl.BlockSpec((tm, c_), lambda i: (i, 0))
    heads = pl.BlockSpec((N_HEADS, tm, HEAD_DIM), lambda i: (0, i, 0))
    return pl.pallas_call(
        body,
        name=name,
        grid=(s // tm,),
        in_specs=[rows(D_A), rows(D_A), pl.BlockSpec((N_HEADS, HEAD_DIM, tm), lambda i: (0, 0, i)), heads, heads,
                  rows(128), rows(D_C), pl.BlockSpec(sel["sel_ht"].shape, lambda i: (0, 0, 0))],
        out_specs=rows(N_IN_P),
        out_shape=_sds((s, N_IN_P), bf16),
        compiler_params=_params("parallel"),
    )(dax, dag, dqt, dkh, dvh, df, dcu, sel["sel_ht"])


def _s5_disc_fn(are, aim, ldt):
    dt = jnp.exp(ldt)
    er = jnp.exp(are * dt)
    br = er * jnp.cos(aim * dt)
    bi = er * jnp.sin(aim * dt)
    nr = br - 1.0
    den = are * are + aim * aim
    return br, bi, (nr * are + bi * aim) / den, (bi * are - nr * aim) / den


def _s5_disc(name, are, aim, ldt):
    def body(a_ref, b_ref, c_ref, o0, o1, o2, o3):
        r = _s5_disc_fn(a_ref[...], b_ref[...], c_ref[...])
        o0[...], o1[...], o2[...], o3[...] = r

    shp = _sds((S5_GROUPS, S5_STATE))
    return pl.pallas_call(body, name=name, out_shape=[shp] * 4)(are, aim, ldt)


def _s5_disc_bwd(name, are, aim, ldt, cts):
    def body(a_ref, b_ref, c_ref, d0, d1, d2, d3, o0, o1, o2):
        _, vjp = jax.vjp(_s5_disc_fn, a_ref[...], b_ref[...], c_ref[...])
        o0[...], o1[...], o2[...] = vjp((d0[...], d1[...], d2[...], d3[...]))

    shp = _sds((S5_GROUPS, S5_STATE))
    return pl.pallas_call(body, name=name, out_shape=[shp, shp, _sds((S5_GROUPS, 1))])(are, aim, ldt, *cts)


def _adamw_rows(w, g, m, v):
    m = ADAM_B1 * m + (1.0 - ADAM_B1) * g
    v = ADAM_B2 * v + (1.0 - ADAM_B2) * (g * g)
    m_hat = m / (1.0 - ADAM_B1 ** ADAM_STEP)
    v_hat = v / (1.0 - ADAM_B2 ** ADAM_STEP)
    return -ADAM_LR * (m_hat / (jnp.sqrt(v_hat) + ADAM_EPS) + ADAM_WD * w), m, v


def _adamw(name, w, ga, gb, m, v):
    rows, cols = w.shape
    tr = _row_tile(rows)

    def body(w_ref, ga_ref, gb_ref, m_ref, v_ref, g_out, d_out, m_out, v_out):
        g = ga_ref[...] + gb_ref[...]
        d, mm, vv = _adamw_rows(w_ref[...], g, m_ref[...], v_ref[...])
        g_out[...] = g
        d_out[...] = d
        m_out[...] = mm
        v_out[...] = vv

    spec = pl.BlockSpec((tr, cols), lambda i: (i, 0))
    return pl.pallas_call(
        body, name=name, grid=(rows // tr,), in_specs=[spec] * 5, out_specs=[spec] * 4,
        out_shape=[_sds((rows, cols))] * 4, compiler_params=_params("parallel"),
    )(w, ga, gb, m, v)


def _sum_stack(name, st):
    n, rows, cols = st.shape
    tr = _row_tile(rows)

    def body(s_ref, o_ref):
        acc = s_ref[0].astype(f32)
        for j in range(1, n):
            acc = acc + s_ref[j].astype(f32)
        o_ref[...] = acc

    return pl.pallas_call(
        body, name=name, grid=(rows // tr,), in_specs=[pl.BlockSpec((n, tr, cols), lambda i: (0, i, 0))],
        out_specs=pl.BlockSpec((tr, cols), lambda i: (i, 0)), out_shape=_sds((rows, cols)),
        compiler_params=_params("parallel"),
    )(st)


def _block_diag(w):
    h, n, m = w.shape
    return jnp.einsum("hij,hg->higj", w, jnp.eye(h, dtype=w.dtype)).reshape(h * n, h * m)


def _block_diag_part(dense, h):
    n, m = dense.shape[0] // h, dense.shape[1] // h
    return jnp.einsum("higj,hg->hij", dense.reshape(h, n, h, m), jnp.eye(h, dtype=dense.dtype))


def _s5_matrices(coef_re, coef_im, b_re, b_im, c_re, c_im):
    bb_re = coef_re[:, :, None] * b_re - coef_im[:, :, None] * b_im
    bb_im = coef_re[:, :, None] * b_im + coef_im[:, :, None] * b_re
    wb_re = _block_diag(jnp.swapaxes(bb_re, 1, 2))
    wb_im = _block_diag(jnp.swapaxes(bb_im, 1, 2))
    wc_re = _block_diag(jnp.swapaxes(c_re, 1, 2))
    wc_im = _block_diag(jnp.swapaxes(-c_im, 1, 2))
    return wb_re, wb_im, wc_re, wc_im


def _shift_down(t):
    return jnp.concatenate([jnp.zeros((1, t.shape[1]), t.dtype), t[:-1]], axis=0)


def _shift_up(t):
    return jnp.concatenate([t[1:], jnp.zeros((1, t.shape[1]), t.dtype)], axis=0)


def _row(v):
    return v.reshape(1, -1)


def _ffn_fwd(tag, h, get, names, gamma, beta, gate_first=False, hb=None):
    hm = h if hb is None else hb
    wg = get(names[0])
    if gate_first:
        g = _ffn_gate(tag + "_gate", hm, wg)
        wu = get(names[1])
        u, act = _ffn_up_given_gate(tag + "_up", hm, wu, g)
    else:
        wu = get(names[1])
        g, u, act = _ffn_up(tag + "_up", hm, wg, wu)
    wd = get(names[2])
    r, out, outb = _mm_ln(tag + "_down", act, wd, h, gamma, beta, 0.5, k_slabs=True)
    return out, outb, dict(h=hm, g=g, u=u, act=act, r=r, wg=wg, wu=wu, wd=wd)


def _ffn_bwd(tag, dout, sv, names, gamma, put, after_ln=None):
    s = dout.shape[0]
    dr, drb, dgam, dbet = _ln_bwd(tag + "_lnb", sv["r"], dout, gamma)
    if after_ln is not None:
        after_ln(dgam, dbet)
    put(names[2], _mm_plain(tag + "_dwd", "tn", _Slabs(sv["act"]), drb, (D_FF, D_MODEL, s), scale=0.5, out_dtype=bf16,
                            tiles=(FF_SLAB, 1024, _tile(s, 2048))))
    dg, du = _ffn_dact(tag + "_dact", drb, sv["wd"], sv["g"], sv["u"])
    dwg, dwu = _mm2(tag + "_dwgu", "tn", (D_FF, D_MODEL, s), _Slabs(dg), sv["h"], _Slabs(du), None, separate=True,
                    out_dtype=bf16, tiles=(FF_SLAB, 1024, _tile(s, 2048)))
    put(names[0], dwg)
    put(names[1], dwu)
    slabs = range(dg.shape[0])
    dh = _mm(tag + "_dh", "nn", (s, D_MODEL, FF_SLAB), (_tile(s, 512), D_MODEL, FF_SLAB),
             [_KPart(dg, j) for j in slabs] + [_KPart(du, j) for j in slabs],
             [_KPart(sv["wg"], j) for j in slabs] + [_KPart(sv["wu"], j) for j in slabs],
             [(j, j, 0) for j in range(2 * len(slabs))], 1,
             lambda accs, extras, vecs: [accs[0] + ALPHA * extras[0]], [f32], extras=[dr])[0]
    return dh, dgam, dbet


def _mixer_fwd(tag, h1, w):
    s = h1.shape[0]
    z = _mm_plain(tag + "_win", "nn", h1, w["w_in"], (s, N_IN_P, D_MODEL), tiles=(_tile(s, 512), 768, D_MODEL))
    ag, f, cu_cols = (z, D_A, 1), (z, 128, F_OFF // 128), (z, D_C, CU_OFF // D_C)
    cu = z[:, CU_OFF:]
    xa = _conv_fwd(tag + "_conv", z, w["conv_w"], w["conv_b"])
    a, gated = _rg_gates(tag + "_gates", xa, w["rg_wa"], w["rg_wx"], w["rg_ba"], w["rg_bx"], w["rg_lam"])
    ha = _lin_scan(tag + "_rgscan", a, gated, False)
    ones = jnp.ones((s, 128), f32)
    c = _lin_scan(tag + "_cumf", ones, _log_f(tag + "_logf", f, w["fox_bf"]), False)
    att = dict(zip(("qt", "k_aug", "kt", "vt", "v", "qh"), _attn_prep(tag + "_attnprep", z, c, w["sel"])))
    ot, lse = _attn_fwd_t(tag + "_attn", att["qt"], att["k_aug"], att["vt"])
    ob = ot.reshape(D_B, s).T
    bu_re, bu_im = _mm2(tag + "_s5in", "nn", (s, S5_LANES, D_C), cu, w["wb_re"], None, w["wb_im"], separate=True,
                        tiles=(_tile(s, 512), 1024, D_C))
    hre, him = _s5_scan(tag + "_s5scan", bu_re, bu_im, w["abar_re"], w["abar_im"], False)
    o = _mix_out(tag + "_mixout", ag, ha, ob, hre, him, cu_cols, w["s5_d"], w["mix_g"], w["wc_re"], w["wc_im"],
                 w["w_glu"])
    sv = dict(h1=h1, z=z, ag=ag, f=f, cu=cu, cu_cols=cu_cols, xa=xa, a=a, ha=ha, att=att, ot=ot, lse=lse, ob=ob,
              hre=hre, him=him, o=o)
    return o, sv


def _mixer_bwd(tag, do, dr2, sv, w, put):
    s = do.shape[0]
    (dag, dha, dob, dhre, dhim, dcu1, dwcr, dwci, dwglu, dd, dgn) = _mix_out_bwd(
        tag + "_mixoutb", do, sv["ag"], sv["ha"], sv["ob"], sv["hre"], sv["him"], sv["cu_cols"], w["s5_d"], w["mix_g"],
        w["wc_re"], w["wc_im"], w["w_glu"])
    put("s5_w_glu", dwglu.astype(bf16))
    gre, gim = _s5_scan(tag + "_s5scanb", dhre, dhim, w["abar_re"], -w["abar_im"], True)
    dab_re, dab_im = _s5_decay_grad(tag + "_s5dec", sv["hre"], sv["him"], gre, gim)
    dwb_re, dwb_im = _mm2(tag + "_s5dwb", "tn", (D_C, S5_LANES, s), sv["cu"], gre, None, gim, separate=True,
                          tiles=(D_C, 1024, _tile(s, 1024)))
    dcu = _mm2(tag + "_s5dcu", "nt", (s, D_C, S5_LANES), gre, w["wb_re"], gim, w["wb_im"], add=dcu1,
               tiles=(_tile(s, 512), D_C, 1024))[0]
    att = sv["att"]
    tq = _att_tiles(s)[0]
    nt = s // tq
    dot_blocks, doh = _attn_do_prep(tag + "_doprep", dob, w["sel"])
    dqt, delta = _attn_bwd_dq_t(tag + "_attndq", att["qt"], att["k_aug"], att["v"], att["kt"], sv["ot"], dot_blocks,
                                sv["lse"])
    dkh, dvh, dck = _attn_bwd_dkv_t(tag + "_attndkv", att["qt"], att["k_aug"], att["v"], att["qh"], doh, dot_blocks,
                                    sv["lse"].reshape(N_HEADS, nt, 1, tq), delta.reshape(N_HEADS, nt, 1, tq))
    dc = jnp.pad(dck.reshape(N_HEADS, s).T, ((0, 0), (0, 128 - N_HEADS)))
    dlf = _lin_scan(tag + "_cumfb", jnp.ones((s, 128), f32), dc, True)
    df, dbf = _log_f_bwd(tag + "_logfb", dlf, sv["f"], w["fox_bf"])
    ga = _lin_scan(tag + "_rgscanb", _shift_up(sv["a"]), dha, True)
    dxa, dwa, dwx, dba, dbx, dlam = _rg_gates_bwd(tag + "_gatesb", sv["xa"], ga, _shift_down(sv["ha"]), w["rg_wa"],
                                                  w["rg_wx"], w["rg_ba"], w["rg_bx"], w["rg_lam"])
    dax, dconv = _conv_bwd(tag + "_convb", dxa, sv["z"], w["conv_w"])
    dz = _dz_assemble(tag + "_dz", dax, dag, dqt, dkh, dvh, df, dcu, w["sel"])
    put("w_in", _mm_plain(tag + "_dwin", "tn", sv["h1"], dz, (D_MODEL, N_IN_P, s), out_dtype=bf16,
                          tiles=(512, 768, _tile(s, 1024))))
    dh1 = _mm_plain(tag + "_dh1", "nt", dz, w["w_in"], (s, D_MODEL, N_IN_P), add=dr2, add_coef=ALPHA,
                    tiles=(_tile(s, 512), 1024, 768))
    grads = dict(dconv=dconv, dwa=dwa, dwx=dwx, dba=dba, dbx=dbx, dlam=dlam, dbf=dbf,
                 dab_re=dab_re, dab_im=dab_im, dwb_re=dwb_re, dwb_im=dwb_im, dwcr=dwcr, dwci=dwci, dd=dd, dgn=dgn)
    return dh1, grads


SMALL_NAMES = ["ln1_g", "ln1_b", "conv_w", "conv_b", "rg_w_a", "rg_b_a", "rg_w_x", "rg_b_x", "rg_lambda", "fox_b_f",
               "s5_a_re", "s5_a_im", "s5_log_dt", "s5_b_re", "s5_b_im", "s5_c_re", "s5_c_im", "s5_d", "mix_norm_g",
               "ln2_g", "ln2_b", "ln3_g", "ln3_b"]
BIG_NAMES = ["ffn1_w_gate", "ffn1_w_up", "ffn1_w_down", "w_in", "s5_w_glu", "w_out", "ffn2_w_gate", "ffn2_w_up",
             "ffn2_w_down"]


def _local_step(x, target, weight, small, on_grads, on_small):
    h, hb = x, None
    saved = []
    sel = _selections()
    for l in range(DEPTH):
        get = functools.partial(weight, l)

        sm = {n: small[n][l] for n in SMALL_NAMES}
        abar_re, abar_im, coef_re, coef_im = _s5_disc(f"l{l}_s5disc", sm["s5_a_re"], sm["s5_a_im"],
                                                      sm["s5_log_dt"].reshape(S5_GROUPS, 1))
        mats, mats_vjp = jax.vjp(_s5_matrices, coef_re, coef_im, sm["s5_b_re"], sm["s5_b_im"], sm["s5_c_re"],
                                 sm["s5_c_im"])
        w = dict(
            sel=sel, conv_w=sm["conv_w"], conv_b=_row(sm["conv_b"]),
            rg_wa=_block_diag(sm["rg_w_a"]).astype(bf16), rg_wx=_block_diag(sm["rg_w_x"]).astype(bf16),
            rg_ba=_row(sm["rg_b_a"]), rg_bx=_row(sm["rg_b_x"]), rg_lam=_row(sm["rg_lambda"]),
            fox_bf=jnp.pad(_row(sm["fox_b_f"]), ((0, 0), (0, 128 - N_HEADS))),
            abar_re=_row(abar_re), abar_im=_row(abar_im),
            wb_re=mats[0].astype(bf16), wb_im=mats[1].astype(bf16), wc_re=mats[2].astype(bf16),
            wc_im=mats[3].astype(bf16), s5_d=_row(sm["s5_d"]), mix_g=_row(sm["mix_norm_g"]))
        h1, h1b, sv1 = _ffn_fwd(f"l{l}_ffn1", h, get, GROUPS["F1"], _row(sm["ln1_g"]), _row(sm["ln1_b"]),
                                gate_first=(l == 0), hb=hb)
        w["w_in"], w["w_glu"] = get("w_in"), get("s5_w_glu")
        o, svm = _mixer_fwd(f"l{l}_mix", h1b, w)
        w_out = get("w_out")
        r2, h2, h2b = _mm_ln(f"l{l}_wout", o, w_out, h1, _row(sm["ln2_g"]), _row(sm["ln2_b"]), 1.0)
        h3, h3b, sv2 = _ffn_fwd(f"l{l}_ffn2", h2, get, GROUPS["F2"], _row(sm["ln3_g"]), _row(sm["ln3_b"]), hb=h2b)
        saved.append(dict(sm=sm, w=w, w_out=w_out, sv1=sv1, svm=svm, r2=r2, sv2=sv2, mats_vjp=mats_vjp))
        h, hb = h3, h3b

    dh, loss_row = _loss_head("loss_head", h, target)
    s = x.shape[0]
    gsmall = {n: [None] * DEPTH for n in SMALL_NAMES}
    for l in reversed(range(DEPTH)):
        sd = saved[l]
        sm, w = sd["sm"], sd["w"]

        def put(name, grad, l=l):
            on_grads((l, name), grad)

        dh2, dgam, dbet = _ffn_bwd(f"l{l}_ffn2", dh, sd["sv2"], GROUPS["F2"], _row(sm["ln3_g"]), put)
        gsmall["ln3_g"][l], gsmall["ln3_b"][l] = dgam[0], dbet[0]
        dr2, dr2b, dgam, dbet = _ln_bwd(f"l{l}_ln2b", sd["r2"], dh2, _row(sm["ln2_g"]))
        gsmall["ln2_g"][l], gsmall["ln2_b"][l] = dgam[0], dbet[0]
        put("w_out", _mm_plain(f"l{l}_dwout", "tn", sd["svm"]["o"], dr2b, (D_MODEL, D_MODEL, s), out_dtype=bf16))
        do = _mm_plain(f"l{l}_do", "nt", dr2b, sd["w_out"], (s, D_MODEL, D_MODEL))
        dh1, g = _mixer_bwd(f"l{l}_mix", do, dr2, sd["svm"], w, put)
        gsmall["conv_w"][l], gsmall["conv_b"][l] = g["dconv"][:CONV_WIDTH], g["dconv"][CONV_WIDTH]
        gsmall["rg_w_a"][l] = _block_diag_part(g["dwa"], N_HEADS)
        gsmall["rg_w_x"][l] = _block_diag_part(g["dwx"], N_HEADS)
        gsmall["rg_b_a"][l], gsmall["rg_b_x"][l], gsmall["rg_lambda"][l] = g["dba"][0], g["dbx"][0], g["dlam"][0]
        gsmall["fox_b_f"][l] = g["dbf"][0, :N_HEADS]
        dcoef_re, dcoef_im, db_re, db_im, dc_re, dc_im = sd["mats_vjp"]((g["dwb_re"], g["dwb_im"], g["dwcr"], g["dwci"]))
        da_re, da_im, dldt = _s5_disc_bwd(
            f"l{l}_s5discb", sm["s5_a_re"], sm["s5_a_im"], sm["s5_log_dt"].reshape(S5_GROUPS, 1),
            (g["dab_re"].reshape(S5_GROUPS, S5_STATE), g["dab_im"].reshape(S5_GROUPS, S5_STATE), dcoef_re, dcoef_im))
        gsmall["s5_a_re"][l], gsmall["s5_a_im"][l], gsmall["s5_log_dt"][l] = da_re, da_im, dldt[:, 0]
        gsmall["s5_b_re"][l], gsmall["s5_b_im"][l], gsmall["s5_c_re"][l], gsmall["s5_c_im"][l] = db_re, db_im, dc_re, dc_im
        gsmall["s5_d"][l], gsmall["mix_norm_g"][l] = g["dd"][0], g["dgn"][0]

        def after_ln(dgam, dbet, l=l):
            gsmall["ln1_g"][l], gsmall["ln1_b"][l] = dgam[0], dbet[0]
            if l == 0:
                on_small({n: jnp.stack(v) for n, v in gsmall.items()})

        dh, _, _ = _ffn_bwd(f"l{l}_ffn1", dh1, sd["sv1"], GROUPS["F1"], _row(sm["ln1_g"]), put, after_ln)
    return loss_row[0, 0], dh


def _position():
    return lax.axis_index("x"), lax.axis_index("y"), lax.axis_index("c")


_ANY = pl.BlockSpec(memory_space=pl.ANY)


COLUMN_SHARDED = ("ffn1_w_gate", "ffn1_w_up", "ffn2_w_gate", "ffn2_w_up")
PACK_QUANTUM = 128 * 256


def _permute_in_cols(w):
    pad = jnp.zeros(w.shape[:-1] + (128 - N_HEADS,), w.dtype)
    return jnp.concatenate([w[..., :F_OFF + N_HEADS], pad, w[..., F_OFF + N_HEADS:]], axis=-1)


def _unpermute_in_cols(w):
    return jnp.concatenate([w[..., :F_OFF + N_HEADS], w[..., CU_OFF:]], axis=-1)


def _pack(arrs):
    flat = jnp.concatenate([a.reshape(-1) for a in arrs])
    pad = -flat.shape[0] % PACK_QUANTUM
    return jnp.pad(flat, (0, pad)).reshape(-1, 128)


def _unpack(buf, shapes):
    flat = buf.reshape(-1)
    out, off = [], 0
    for shp in shapes:
        size = math.prod(shp)
        out.append(flat[off:off + size].reshape(shp))
        off += size
    return out


WEIGHT_NAMES = ["ffn1_w_gate", "ffn1_w_up", "ffn1_w_down", "ln1_g", "ln1_b", "w_in", "conv_w", "conv_b", "rg_w_a",
                "rg_b_a", "rg_w_x", "rg_b_x", "rg_lambda", "fox_b_f", "s5_a_re", "s5_a_im", "s5_log_dt", "s5_b_re",
                "s5_b_im", "s5_c_re", "s5_c_im", "s5_d", "s5_w_glu", "mix_norm_g", "w_out", "ln2_g", "ln2_b",
                "ffn2_w_gate", "ffn2_w_up", "ffn2_w_down", "ln3_g", "ln3_b"]


def _remote(src, dst, send_sems, recv_sems, k, peer):
    return pltpu.make_async_remote_copy(src_ref=src, dst_ref=dst, send_sem=send_sems.at[k], recv_sem=recv_sems.at[k],
                                        device_id=peer, device_id_type=MESH)


class _ChipGatherPart:
    def __init__(self, arrays):
        self.arrays, self.results = list(arrays), None

    def out_shape(self):
        return [_sds((N_CHIPS,) + a.shape, a.dtype) for a in self.arrays]

    def sems(self):
        n = len(self.arrays)
        return [pltpu.SemaphoreType.DMA((3 * n,)), pltpu.SemaphoreType.DMA((3 * n,)), pltpu.SemaphoreType.DMA((n,))]

    def copies(self, ins, outs, sems):
        send_sems, recv_sems, local_sems = sems
        x, y, c = _position()
        me = 2 * x + y
        local, sends, recvs = [], [], []
        for i, (src, dst) in enumerate(zip(ins, outs)):
            local.append(pltpu.make_async_copy(self.mine(src, me), dst.at[me], local_sems.at[i]))
            for r, (px, py) in enumerate([(1 - x, y), (x, 1 - y), (1 - x, 1 - y)]):
                peer = 2 * px + py
                sends.append(_remote(self.theirs(src, peer), dst.at[me], send_sems, recv_sems, 3 * i + r, (px, py, c)))
                recvs.append(_remote(self.mine(src, me), dst.at[peer], send_sems, recv_sems, 3 * i + r, (px, py, c)))
        return local, sends, recvs

    def mine(self, src, me):
        return src

    def theirs(self, src, peer):
        return src


class _ChipGatherHalvesPart(_ChipGatherPart):
    def sems(self):
        n = len(self.arrays)
        return super().sems() + [pltpu.SemaphoreType.DMA((3 * n,)), pltpu.SemaphoreType.DMA((3 * n,))]

    def _half(self, ref, which):
        rows = ref.shape[0] // 2
        return ref.at[pl.ds(which * rows, rows)]

    def copies(self, ins, outs, sems):
        send_sems, recv_sems, local_sems = sems[:3]
        x, y, c = _position()
        me = 2 * x + y
        local, sends, recvs = [], [], []
        for i, (src, dst) in enumerate(zip(ins, outs)):
            local.append(pltpu.make_async_copy(src, dst.at[me], local_sems.at[i]))
            for r, (px, py) in enumerate([(1 - x, y), (x, 1 - y), (1 - x, 1 - y)]):
                sends.append(_remote(self._half(src, c), self._half(dst.at[me], c), send_sems, recv_sems, 3 * i + r,
                                     (px, py, c)))
                recvs.append(_remote(self._half(src, c), self._half(dst.at[2 * px + py], c), send_sems, recv_sems,
                                     3 * i + r, (px, py, c)))
        return local, sends, recvs

    def forwards(self, ins, outs, sems):
        send_sems, recv_sems = sems[3:]
        x, y, c = _position()
        sends, recvs = [], []
        for i, dst in enumerate(outs):
            for r, (px, py) in enumerate([(1 - x, y), (x, 1 - y), (1 - x, 1 - y)]):
                slot = dst.at[2 * px + py]
                sends.append(_remote(self._half(slot, c), self._half(slot, c), send_sems, recv_sems, 3 * i + r,
                                     (x, y, 1 - c)))
                recvs.append(_remote(self._half(slot, c), self._half(slot, 1 - c), send_sems, recv_sems, 3 * i + r,
                                     (x, y, 1 - c)))
        return sends, recvs


class _ChipScatterPart(_ChipGatherPart):
    def out_shape(self):
        return [_sds(a.shape, a.dtype) for a in self.arrays]

    def mine(self, src, me):
        return src.at[me]

    def theirs(self, src, peer):
        return src.at[peer]


class _SiblingSwapPart:
    def __init__(self, arrays):
        self.arrays, self.results = list(arrays), None

    def out_shape(self):
        return [_sds(a.shape, a.dtype) for a in self.arrays]

    def sems(self):
        n = len(self.arrays)
        return [pltpu.SemaphoreType.DMA((n,)), pltpu.SemaphoreType.DMA((n,))]

    def copies(self, ins, outs, sems):
        x, y, c = _position()
        both = [_remote(src, dst, sems[0], sems[1], i, (x, y, 1 - c)) for i, (src, dst) in enumerate(zip(ins, outs))]
        return [], both, both


def _split_by(parts, refs, count):
    out, off = [], 0
    for p in parts:
        out.append(refs[off:off + count(p)])
        off += count(p)
    return out


def _parts_refs(parts, in_refs, out_refs, sem_refs):
    return zip(parts, _split_by(parts, in_refs, lambda p: len(p.arrays)),
               _split_by(parts, out_refs, lambda p: len(p.arrays)), _split_by(parts, sem_refs, lambda p: len(p.sems())))


def _exchange_start(parts, in_refs, out_refs, sem_refs):
    for part, ins, outs, sems in _parts_refs(parts, in_refs, out_refs, sem_refs):
        local, sends, _ = part.copies(ins, outs, sems)
        for cp in local + sends:
            cp.start()


def _exchange_finish(parts, in_refs, out_refs, sem_refs):
    split = list(_parts_refs(parts, in_refs, out_refs, sem_refs))
    copies = [part.copies(ins, outs, sems) for part, ins, outs, sems in split]
    for _, _, recvs in copies:
        for cp in recvs:
            cp.wait_recv()
    second = [part.forwards(ins, outs, sems) for part, ins, outs, sems in split if hasattr(part, "forwards")]
    for sends, _ in second:
        for cp in sends:
            cp.start()
    for sends, recvs in second:
        for cp in recvs:
            cp.wait_recv()
        for cp in sends:
            cp.wait_send()
    for local, sends, _ in copies:
        for cp in sends:
            cp.wait_send()
        for cp in local:
            cp.wait()


def _exchange_operands(parts):
    return ([a for p in parts for a in p.arrays], [s for p in parts for s in p.out_shape()],
            [s for p in parts for s in p.sems()])


def _set_results(parts, res):
    for part, outs in zip(parts, _split_by(parts, list(res), lambda p: len(p.arrays))):
        part.results = list(outs)


def _exchange_now(name, parts):
    x_in, x_out, x_sem = _exchange_operands(parts)
    n = len(x_in)

    def body(*refs):
        _exchange_start(parts, refs[:n], refs[n:2 * n], refs[2 * n:])
        _exchange_finish(parts, refs[:n], refs[n:2 * n], refs[2 * n:])

    res = pl.pallas_call(body, name=name, in_specs=[_ANY] * n, out_specs=[_ANY] * n, out_shape=x_out,
                         scratch_shapes=x_sem)(*x_in)
    _set_results(parts, res)


_RIDERS = {}


def _call(body, *, name, grid, in_specs, out_specs, out_shape, scratch_shapes=(), compiler_params=None):
    make_parts = _RIDERS.pop(name, None)
    if make_parts is None:
        return pl.pallas_call(body, name=name, grid=grid, in_specs=in_specs, out_specs=out_specs, out_shape=out_shape,
                              scratch_shapes=scratch_shapes, compiler_params=compiler_params)
    parts = make_parts()
    x_in, x_out, x_sem = _exchange_operands(parts)
    n_out, n_scr, n_x = len(out_shape), len(scratch_shapes), len(x_in)

    def run(*args):
        n_in = len(args)

        def hosted(*refs):
            ins, xi = refs[:n_in], refs[n_in:n_in + n_x]
            outs, xo = refs[n_in + n_x:n_in + n_x + n_out], refs[n_in + n_x + n_out:n_in + 2 * n_x + n_out]
            scr, xs = refs[n_in + 2 * n_x + n_out:n_in + 2 * n_x + n_out + n_scr], refs[n_in + 2 * n_x + n_out + n_scr:]
            first = functools.reduce(jnp.logical_and, [pl.program_id(d) == 0 for d in range(len(grid))])
            last = functools.reduce(jnp.logical_and, [pl.program_id(d) == grid[d] - 1 for d in range(len(grid))])

            @pl.when(first)
            def _():
                _exchange_start(parts, xi, xo, xs)

            body(*ins, *outs, *scr)

            @pl.when(last)
            def _():
                _exchange_finish(parts, xi, xo, xs)

        res = pl.pallas_call(
            hosted, name=name, grid=grid, in_specs=list(in_specs) + [_ANY] * n_x,
            out_specs=list(out_specs) + [_ANY] * n_x, out_shape=list(out_shape) + x_out,
            scratch_shapes=list(scratch_shapes) + x_sem, compiler_params=_params(*["arbitrary"] * len(grid)),
        )(*args, *x_in)
        _set_results(parts, res[n_out:])
        return list(res[:n_out])

    return run


GROUPS = {"F1": ["ffn1_w_gate", "ffn1_w_up", "ffn1_w_down"], "MX": ["w_in", "s5_w_glu", "w_out"],
          "F2": ["ffn2_w_gate", "ffn2_w_up", "ffn2_w_down"]}
FIRST_GATHER = [(0, "ffn1_w_gate")]
GATHER_HOSTS = {
    "l0_ffn1_gate": [(0, "ffn1_w_up")],
    "l0_ffn1_up": [(0, "ffn1_w_down")],
    "l0_ffn1_down": [(0, "w_in"), (0, "s5_w_glu"), (0, "w_out")],
    "l0_mix_attn": [(0, "ffn2_w_up"), (0, "ffn2_w_down"), (1, "w_in")],
    "l0_mix_s5scan": [(0, "ffn2_w_gate")],
    "l0_wout": [(1, "s5_w_glu"), (1, "w_out")],
    "l0_ffn2_up": [(1, "ffn1_w_gate")],
    "l0_ffn2_down": [(1, "ffn1_w_up")],
    "l1_ffn1_up": [(1, "ffn1_w_down")],
    "l1_mix_attn": [(1, "ffn2_w_up"), (1, "ffn2_w_down")],
    "l1_mix_s5scan": [(1, "ffn2_w_gate")],
}
SCATTER_HOSTS = {
    "l1_ffn2_dact": [(1, "ffn2_w_down")],
    "l1_mix_attndq": [(1, "ffn2_w_up")],
    "l1_mix_attndkv": [(1, "w_out"), (1, "s5_w_glu"), (1, "ffn2_w_gate")],
    "l1_ffn1_dact": [(1, "ffn1_w_down")],
    "l1_ffn1_dwgu": [(1, "w_in")],
    "l0_ffn2_dact": [(1, "ffn1_w_up")],
    "l0_ffn2_dwgu": [(0, "ffn2_w_down")],
    "l0_mix_attndq": [(0, "w_out"), (0, "s5_w_glu"), (0, "ffn2_w_up")],
    "l0_mix_attndkv": [(0, "ffn2_w_gate"), (1, "ffn1_w_gate")],
    "l0_mix_dh1": [(0, "w_in")],
    "l0_ffn1_dact": [(0, "ffn1_w_down")],
    "l0_ffn1_dh": [(0, "ffn1_w_gate")],
}
LAST_SCATTER = [(0, "ffn1_w_up")]
SMALL_HOST = "l0_ffn1_dwd"
SMALL_PACK_ORDER = [n for n in SMALL_NAMES if n != "conv_w"] + ["conv_w"]
TAIL_HOST = "l0_ffn1_dwgu"
LATE_SCATTER_HOST = "l0_ffn1_dh"
LAST_HOST = "adamw_ffn2"


def _sharded_rows(name, a):
    return jnp.swapaxes(a, 1, 2) if name in COLUMN_SHARDED else a


def _unstack_layer(st):
    _, r, c = st.shape
    return st.reshape(N_CHIPS * r, c)


def _restack_layer(g):
    r, c = g.shape
    return g.reshape(N_CHIPS, r // N_CHIPS, c)


def _adamw_layer(name, layer, w, ga, gb, m, v, bufs):
    _, r, c = w.shape
    tr = _row_tile(r)

    def body(w_ref, ga_ref, gb_ref, m_ref, v_ref, *rest):
        g_out, d_out, m_out, v_out = rest[-4:]
        g = ga_ref[...] + gb_ref[...]
        d, mm, vv = _adamw_rows(w_ref[...], g, m_ref[...], v_ref[...])
        g_out[...] = g
        d_out[...] = d
        m_out[...] = mm
        v_out[...] = vv

    full = pl.BlockSpec((None, tr, c), lambda i: (layer, i, 0))
    flat = pl.BlockSpec((tr, c), lambda i: (i, 0))
    extra = {} if bufs is None else dict(input_output_aliases={5 + k: k for k in range(4)})
    return pl.pallas_call(
        body, name=name, grid=(r // tr,),
        in_specs=[full, flat, flat, full, full] + ([] if bufs is None else [_ANY] * 4),
        out_specs=[full] * 4, out_shape=[_sds(w.shape)] * 4, compiler_params=_params("parallel"), **extra,
    )(w, ga, gb, m, v, *([] if bufs is None else bufs))


def _adamw_both_layers(name, ws, ms, vs, gas, gbs):
    nw = len(ws)
    _, r, c = ws[0].shape
    tr = _row_tile(r, 64)

    def body(*refs):
        ins, outs = refs[:7 * nw], refs[7 * nw:]
        for k in range(nw):
            w_ref, m_ref, v_ref = ins[3 * k:3 * k + 3]
            g_refs = ins[3 * nw + 4 * k:3 * nw + 4 * k + 4]
            g_out, d_out, m_out, v_out = outs[4 * k:4 * k + 4]
            for layer in range(DEPTH):
                g = g_refs[layer][...] + g_refs[DEPTH + layer][...]
                d, mm, vv = _adamw_rows(w_ref[layer], g, m_ref[layer], v_ref[layer])
                g_out[layer] = g
                d_out[layer] = d
                m_out[layer] = mm
                v_out[layer] = vv

    both = pl.BlockSpec((DEPTH, tr, c), lambda i: (0, i, 0))
    flat = pl.BlockSpec((tr, c), lambda i: (i, 0))
    wmv = [t for k in range(nw) for t in (ws[k], ms[k], vs[k])]
    gs = [t for k in range(nw) for t in (*gas[k], *gbs[k])]
    res = _call(
        body, name=name, grid=(r // tr,), in_specs=[both] * (3 * nw) + [flat] * (4 * nw),
        out_specs=[both] * (4 * nw), out_shape=[_sds(ws[0].shape)] * (4 * nw), compiler_params=_params("parallel"),
    )(*wmv, *gs)
    return [res[4 * k:4 * k + 4] for k in range(nw)]


def _train_step(x, loss_target, w, m, v):
    ix, iy, _ = _position()
    chip = 2 * ix + iy
    shard = {n: (_permute_in_cols(w[n]) if n == "w_in" else _sharded_rows(n, w[n])).astype(bf16) for n in BIG_NAMES}

    gathered = {}

    def gather_parts(keys, extra=()):
        part = _ChipGatherHalvesPart([shard[n][layer] for layer, n in keys] + list(extra))
        gathered.update({key: (part, i) for i, key in enumerate(keys)})
        return [part]

    (first,) = gather_parts(FIRST_GATHER, extra=[w["conv_w"]])
    _exchange_now("gather_first", [first])
    for host, keys in GATHER_HOSTS.items():
        _RIDERS[host] = functools.partial(gather_parts, keys)

    def weight(layer, name):
        part, i = gathered[(layer, name)]
        return _unstack_layer(part.results[i])

    small = {n: w[n] for n in SMALL_NAMES}
    small["conv_w"] = first.results[-1].transpose(1, 2, 0, 3).reshape(DEPTH, CONV_WIDTH, D_A)

    grads_full, scattered = {}, {}

    def scatter_parts(keys):
        part = _ChipScatterPart([_restack_layer(grads_full[key]) for key in keys])
        scattered.update({key: (part, i) for i, key in enumerate(keys)})
        return [part]

    for host, keys in SCATTER_HOSTS.items():
        _RIDERS[host] = functools.partial(scatter_parts, keys)

    partial = {}

    def reduce_chips(keys):
        for layer, n in keys:
            part, i = scattered[(layer, n)]
            p = _sum_stack(f"sum_l{layer}_{n}", part.results[i])
            partial[(layer, n)] = _unpermute_in_cols(p) if n == "w_in" else p

    early = [key for host, keys in SCATTER_HOSTS.items() if host != LATE_SCATTER_HOST for key in keys]
    late = SCATTER_HOSTS[LATE_SCATTER_HOST]
    tail = {}

    def small_parts():
        tail["small"] = _ChipGatherPart([_pack([tail["gsmall"][n] for n in SMALL_PACK_ORDER])])
        return [tail["small"]]

    def tail_parts():
        reduce_chips(early)
        tail["small_sum"] = _sum_stack("sum_small", tail["small"].results[0])
        tail["swap"] = _SiblingSwapPart([partial[k] for k in early] + [tail["small_sum"]])
        return [tail["swap"]]

    _RIDERS[SMALL_HOST] = small_parts
    _RIDERS[TAIL_HOST] = tail_parts
    loss_local, gx = _local_step(x[0], loss_target[0], weight, small, grads_full.__setitem__,
                                 functools.partial(tail.__setitem__, "gsmall"))
    other = dict(zip(early, tail["swap"].results[:-1]))
    small_mine, small_other = tail["small_sum"], tail["swap"].results[-1]
    grads, deltas, new_m, new_v = {}, {}, {}, {}
    reduce_chips(late)
    last_parts = scatter_parts(LAST_SCATTER) + [_SiblingSwapPart([partial[k] for k in late])]
    _RIDERS[LAST_HOST] = lambda: last_parts
    ffn2 = GROUPS["F2"]
    res = _adamw_both_layers(
        LAST_HOST, *[[_sharded_rows(n, t[n]) for n in ffn2] for t in (w, m, v)],
        [[partial[(layer, n)] for layer in range(DEPTH)] for n in ffn2],
        [[other[(layer, n)] for layer in range(DEPTH)] for n in ffn2])
    for n, bufs in zip(ffn2, res):
        grads[n], deltas[n], new_m[n], new_v[n] = (_sharded_rows(n, t) for t in bufs)
    other.update(zip(late, last_parts[1].results))
    reduce_chips(LAST_SCATTER)
    swap_last = _SiblingSwapPart([partial[k] for k in LAST_SCATTER])
    _exchange_now("swap_last", [swap_last])
    other.update(zip(LAST_SCATTER, swap_last.results))

    for n in BIG_NAMES:
        if n in ffn2:
            continue
        bufs = None
        wr, mr, vr = (_sharded_rows(n, t) for t in (w[n], m[n], v[n]))
        for layer in range(DEPTH):
            bufs = _adamw_layer(f"adamw_l{layer}_{n}", layer, wr, partial[(layer, n)], other[(layer, n)], mr, vr, bufs)
        grads[n], deltas[n], new_m[n], new_v[n] = (_sharded_rows(n, t) for t in bufs)
    packed = SMALL_PACK_ORDER[:-1]
    shapes = [w[n].shape for n in packed]
    res = _adamw("adamw_small", _pack([w[n] for n in packed]), small_mine, small_other,
                 _pack([m[n] for n in packed]), _pack([v[n] for n in packed]))
    for dst, buf in zip((grads, deltas, new_m, new_v), res):
        dst.update(zip(packed, _unpack(buf, shapes)))
    cw = D_A // N_CHIPS
    conv_shape = (DEPTH, CONV_WIDTH, D_A)
    offset = sum(math.prod(s_) for s_ in shapes)

    def conv_grad(buf):
        full = buf.reshape(-1)[offset:offset + math.prod(conv_shape)].reshape(conv_shape)
        return lax.dynamic_slice_in_dim(full, chip * cw, cw, axis=2).reshape(DEPTH * CONV_WIDTH, cw)

    rows = lambda t: t.reshape(DEPTH * CONV_WIDTH, cw)
    res = _adamw("adamw_conv_w", rows(w["conv_w"]), conv_grad(small_mine), conv_grad(small_other),
                 rows(m["conv_w"]), rows(v["conv_w"]))
    for dst, buf in zip((grads, deltas, new_m, new_v), res):
        dst["conv_w"] = buf.reshape(w["conv_w"].shape)

    loss = lax.psum(loss_local, ("x", "y", "c"))
    return (loss, gx[None], *[grads[n] for n in WEIGHT_NAMES], *[deltas[n] for n in WEIGHT_NAMES],
            *[new_m[n] for n in WEIGHT_NAMES], *[new_v[n] for n in WEIGHT_NAMES])


def kernel(x, ffn1_w_gate, ffn1_w_up, ffn1_w_down, ln1_g, ln1_b, w_in, conv_w, conv_b, rg_w_a, rg_b_a, rg_w_x, rg_b_x, rg_lambda, fox_b_f, s5_a_re, s5_a_im, s5_log_dt, s5_b_re, s5_b_im, s5_c_re, s5_c_im, s5_d, s5_w_glu, mix_norm_g, w_out, ln2_g, ln2_b, ffn2_w_gate, ffn2_w_up, ffn2_w_down, ln3_g, ln3_b, loss_target, m_ffn1_w_gate, m_ffn1_w_up, m_ffn1_w_down, m_ln1_g, m_ln1_b, m_w_in, m_conv_w, m_conv_b, m_rg_w_a, m_rg_b_a, m_rg_w_x, m_rg_b_x, m_rg_lambda, m_fox_b_f, m_s5_a_re, m_s5_a_im, m_s5_log_dt, m_s5_b_re, m_s5_b_im, m_s5_c_re, m_s5_c_im, m_s5_d, m_s5_w_glu, m_mix_norm_g, m_w_out, m_ln2_g, m_ln2_b, m_ffn2_w_gate, m_ffn2_w_up, m_ffn2_w_down, m_ln3_g, m_ln3_b, v_ffn1_w_gate, v_ffn1_w_up, v_ffn1_w_down, v_ln1_g, v_ln1_b, v_w_in, v_conv_w, v_conv_b, v_rg_w_a, v_rg_b_a, v_rg_w_x, v_rg_b_x, v_rg_lambda, v_fox_b_f, v_s5_a_re, v_s5_a_im, v_s5_log_dt, v_s5_b_re, v_s5_b_im, v_s5_c_re, v_s5_c_im, v_s5_d, v_s5_w_glu, v_mix_norm_g, v_w_out, v_ln2_g, v_ln2_b, v_ffn2_w_gate, v_ffn2_w_up, v_ffn2_w_down, v_ln3_g, v_ln3_b):
    args = dict(locals())
    w = {n: args[n] for n in WEIGHT_NAMES}
    m = {n: args["m_" + n] for n in WEIGHT_NAMES}
    v = {n: args["v_" + n] for n in WEIGHT_NAMES}
    return _train_step(x, loss_target, w, m, v)
```
